```python
import jax, jax.numpy as jnp
from jax import lax
import numpy as np

D_MODEL = 1024
BATCH = 16
SEQ = 2048
DEPTH = 1

D_MIX = D_MODEL
A_HEADS = 8
A_HEAD_DIM = 64
D_A = A_HEADS * A_HEAD_DIM
DILATED_BRANCHES = ((128, 1), (512, 4), (2048, 16))
BLK = 128
B_HEADS = 8
B_NOPE_DIM = 64
B_ROPE_DIM = 32
B_V_DIM = 64
D_B = B_HEADS * B_V_DIM
Q_LORA = 384
KV_LORA = 256
ROPE_THETA = 10000.0
P_IN = 3 * D_A + Q_LORA + KV_LORA + B_ROPE_DIM
N_BUCKETS = 32
MAX_DISTANCE = 2048
D_FF = ((-(-8 * D_MODEL // 3) + 255) // 256) * 256
N_MOD = 6
EPS = 1e-6
NEG = -1e30

kernel_name = 'hybrid_dilated_mla_adaln_layer'


def _rmsnorm(x, g):
    xf = x.astype(jnp.float32)
    y = xf * lax.rsqrt(jnp.mean(xf * xf, axis=-1, keepdims=True) + EPS)
    return (y * g.astype(jnp.float32)).astype(x.dtype)


def _t5_bucket(dist):
    max_exact = N_BUCKETS // 2
    d = np.maximum(dist, 1).astype(np.float64)
    large = max_exact + (np.log(d / max_exact) / np.log(MAX_DISTANCE / max_exact)
                         * (N_BUCKETS - max_exact)).astype(np.int64)
    large = np.minimum(large, N_BUCKETS - 1)
    return np.where(dist < max_exact, dist, large).astype(np.int32)


def _dilated_branch(q, k, v, rel_bias, window, dilation):
    B, S, H, E = q.shape
    span = window // dilation
    n = S // dilation
    nb = -(-n // BLK)
    n_pad = nb * BLK

    def to_residue(t):
        t = t.reshape(B, n, dilation, H, E).transpose(0, 2, 3, 1, 4)
        return jnp.pad(t, ((0, 0), (0, 0), (0, 0), (0, n_pad - n), (0, 0)))

    def band(t):
        t = jnp.pad(to_residue(t), ((0, 0), (0, 0), (0, 0), (BLK, 0), (0, 0)))
        t = t.reshape(B, dilation, H, nb + 1, BLK, E)
        return jnp.concatenate([t[:, :, :, :-1], t[:, :, :, 1:]], axis=4)

    qb = to_residue(q).reshape(B, dilation, H, nb, BLK, E)
    kb, vb = band(k), band(v)
    a = np.arange(BLK)[:, None]
    bk = np.arange(2 * BLK)[None, :]
    steps = BLK + a - bk
    valid = (steps >= 0) & (steps <= span)
    first = valid & (bk >= BLK)
    mask = np.concatenate([first[None], np.broadcast_to(valid, (nb - 1,) + valid.shape)], axis=0)
    bucket = _t5_bucket(np.clip(steps, 0, span) * dilation)
    bias = jnp.transpose(rel_bias[bucket], (2, 0, 1)).astype(jnp.float32)

    logits = jnp.einsum('bdhnqe,bdhnke->bdhnqk', qb, kb,
                        preferred_element_type=jnp.float32) * (E ** -0.5)
    logits = logits + bias[None, None, :, None]
    logits = jnp.where(jnp.asarray(mask)[None, None, None], logits, NEG)
    m = jnp.max(logits, axis=-1, keepdims=True)
    p = jnp.exp(logits - m)
    s = jnp.sum(p, axis=-1, keepdims=True)
    o = jnp.einsum('bdhnqk,bdhnke->bdhnqe', p, vb.astype(jnp.float32)) / s
    lse = (m + jnp.log(s))[..., 0]

    def from_residue(t):
        t = t.reshape((B, dilation, H, n_pad) + t.shape[5:])[:, :, :, :n]
        t = jnp.moveaxis(t, 3, 1)
        return t.reshape((B, S, H) + t.shape[4:])

    return from_residue(o), from_residue(lse)


def _rope(t):
    S, R = t.shape[1], t.shape[-1]
    half = R // 2
    inv = ROPE_THETA ** (-jnp.arange(half, dtype=jnp.float32) / half)
    ang = jnp.arange(S, dtype=jnp.float32)[:, None] * inv[None, :]
    cos, sin = jnp.cos(ang)[None, :, None], jnp.sin(ang)[None, :, None]
    t1, t2 = t[..., :half].astype(jnp.float32), t[..., half:].astype(jnp.float32)
    return jnp.concatenate([t1 * cos - t2 * sin, t1 * sin + t2 * cos], axis=-1).astype(t.dtype)


def _mla_attention(q_nope, q_rope, k_nope, k_rope, v):
    B, S, H, _ = q_nope.shape
    nq = S // BLK
    scale = (B_NOPE_DIM + B_ROPE_DIM) ** -0.5
    qn = jnp.moveaxis(q_nope.reshape(B, nq, BLK, H, -1), 1, 0)
    qr = jnp.moveaxis(q_rope.reshape(B, nq, BLK, H, -1), 1, 0)
    kpos = jnp.arange(S)

    def block(args):
        qn_b, qr_b, i = args
        logits = (jnp.einsum('bqhe,bkhe->bhqk', qn_b, k_nope, preferred_element_type=jnp.float32)
                  + jnp.einsum('bqhe,bke->bhqk', qr_b, k_rope, preferred_element_type=jnp.float32)) * scale
        qpos = i * BLK + jnp.arange(BLK)
        logits = jnp.where(qpos[:, None] >= kpos[None, :], logits, NEG)
        p = jax.nn.softmax(logits, axis=-1)
        return jnp.einsum('bhqk,bkhe->bqhe', p.astype(v.dtype), v)

    o = lax.map(block, (qn, qr, jnp.arange(nq)))
    return jnp.moveaxis(o, 0, 1).reshape(B, S, H, -1)


def _mixer(h, w_in, g_cq, w_uq, g_ckv, w_ukv, rel_bias, g_out_a, g_out_b, w_out):
    B, S, _ = h.shape
    proj = h @ w_in
    i1, i2, i3 = D_A, 2 * D_A, 3 * D_A
    i4, i5 = i3 + Q_LORA, i3 + Q_LORA + KV_LORA
    qa, ka, va, cq, ckv, kr = jnp.split(proj, [i1, i2, i3, i4, i5], axis=-1)
    qa = qa.reshape(B, S, A_HEADS, A_HEAD_DIM)
    ka = ka.reshape(B, S, A_HEADS, A_HEAD_DIM)
    va = va.reshape(B, S, A_HEADS, A_HEAD_DIM)
    branches = [_dilated_branch(qa, ka, va, rel_bias, w, d) for (w, d) in DILATED_BRANCHES]
    o_stack = jnp.stack([br[0] for br in branches])
    lse = jnp.stack([br[1] for br in branches])
    wts = jax.nn.softmax(lse, axis=0)
    out_a = jnp.einsum('rbsh,rbshe->bshe', wts, o_stack).reshape(B, S, D_A).astype(h.dtype)
    q = (_rmsnorm(cq, g_cq) @ w_uq).reshape(B, S, B_HEADS, B_NOPE_DIM + B_ROPE_DIM)
    q_nope, q_rope = q[..., :B_NOPE_DIM], _rope(q[..., B_NOPE_DIM:])
    kv = (_rmsnorm(ckv, g_ckv) @ w_ukv).reshape(B, S, B_HEADS, B_NOPE_DIM + B_V_DIM)
    k_nope, v = kv[..., :B_NOPE_DIM], kv[..., B_NOPE_DIM:]
    k_rope = _rope(kr[:, :, None, :])[:, :, 0]
    out_b = _mla_attention(q_nope, q_rope, k_nope, k_rope, v).reshape(B, S, D_B)
    y = jnp.concatenate([_rmsnorm(out_a, g_out_a), _rmsnorm(out_b, g_out_b)], axis=-1)
    return y @ w_out


def _swiglu(h, w_ffn_in, w_ffn_out):
    g, u = jnp.split(h @ w_ffn_in, 2, axis=-1)
    return (jax.nn.silu(g) * u) @ w_ffn_out


def _fwd_setup_inputs(seed: int = 0) -> dict:
    key = jax.random.key(seed)
    ks = jax.random.split(key, 20)
    f32 = jnp.float32

    def nrm(k, shape, scale):
        return jax.random.normal(k, shape, f32) * scale

    def gain(k, shape):
        return 1.0 + 0.05 * jax.random.normal(k, shape, f32)

    L = DEPTH
    return {
        'x': nrm(ks[0], (BATCH, SEQ, D_MODEL), 1.0),
        'c': nrm(ks[1], (BATCH, D_MODEL), 1.0),
        'w_ada': nrm(ks[2], (L, D_MODEL, N_MOD * D_MODEL), 0.5 * D_MODEL ** -0.5),
        'b_ada': nrm(ks[3], (L, N_MOD * D_MODEL), 0.01),
        'g_norm1': gain(ks[4], (L, D_MODEL)),
        'w_in': nrm(ks[5], (L, D_MODEL, P_IN), D_MODEL ** -0.5),
        'g_cq': gain(ks[6], (L, Q_LORA)),
        'w_uq': nrm(ks[7], (L, Q_LORA, B_HEADS * (B_NOPE_DIM + B_ROPE_DIM)), Q_LORA ** -0.5),
        'g_ckv': gain(ks[8], (L, KV_LORA)),
        'w_ukv': nrm(ks[9], (L, KV_LORA, B_HEADS * (B_NOPE_DIM + B_V_DIM)), KV_LORA ** -0.5),
        'rel_bias': nrm(ks[10], (N_BUCKETS, A_HEADS), 0.5),
        'g_out_a': gain(ks[11], (L, D_A)),
        'g_out_b': gain(ks[12], (L, D_B)),
        'w_out': nrm(ks[13], (L, D_MIX, D_MODEL), D_MIX ** -0.5),
        'g_norm2': gain(ks[14], (L, D_MODEL)),
        'w_ffn_in': nrm(ks[15], (L, D_MODEL, 2 * D_FF), D_MODEL ** -0.5),
        'w_ffn_out': nrm(ks[16], (L, D_FF, D_MODEL), D_FF ** -0.5),
        'g_final': gain(ks[17], (D_MODEL,)),
    }


def _fwd_reference(x, c, w_ada, b_ada, g_norm1, w_in, g_cq, w_uq, g_ckv, w_ukv, rel_bias,
              g_out_a, g_out_b, w_out, g_norm2, w_ffn_in, w_ffn_out, g_final):
    cond = jax.nn.silu(c)
    for l in range(DEPTH):
        mod = (cond @ w_ada[l] + b_ada[l])[:, None, :]
        sh1, sc1, g1, sh2, sc2, g2 = jnp.split(mod, N_MOD, axis=-1)
        h = _rmsnorm(x, g_norm1[l]) * (1.0 + sc1) + sh1
        x = x + g1 * _mixer(h, w_in[l], g_cq[l], w_uq[l], g_ckv[l], w_ukv[l], rel_bias,
                            g_out_a[l], g_out_b[l], w_out[l])
        h = _rmsnorm(x, g_norm2[l]) * (1.0 + sc2) + sh2
        x = x + g2 * _swiglu(h, w_ffn_in[l], w_ffn_out[l])
    return _rmsnorm(x, g_final)


import jax as _jax
import jax.numpy as _jnp

TWIN_FORMAT = 'train_step'
FWD_PARAMS = ['x', 'c', 'w_ada', 'b_ada', 'g_norm1', 'w_in', 'g_cq', 'w_uq', 'g_ckv', 'w_ukv', 'rel_bias', 'g_out_a', 'g_out_b', 'w_out', 'g_norm2', 'w_ffn_in', 'w_ffn_out', 'g_final']
TWIN_WEIGHTS = ['w_ada', 'b_ada', 'g_norm1', 'w_in', 'g_cq', 'w_uq', 'g_ckv', 'w_ukv', 'rel_bias', 'g_out_a', 'g_out_b', 'w_out', 'g_norm2', 'w_ffn_in', 'w_ffn_out', 'g_final']
TWIN_DIFF_INPUT = 'x'
TWIN_INPUTS = ['x', 'c', 'w_ada', 'b_ada', 'g_norm1', 'w_in', 'g_cq', 'w_uq', 'g_ckv', 'w_ukv', 'rel_bias', 'g_out_a', 'g_out_b', 'w_out', 'g_norm2', 'w_ffn_in', 'w_ffn_out', 'g_final', 'loss_target', 'm_w_ada', 'm_b_ada', 'm_g_norm1', 'm_w_in', 'm_g_cq', 'm_w_uq', 'm_g_ckv', 'm_w_ukv', 'm_rel_bias', 'm_g_out_a', 'm_g_out_b', 'm_w_out', 'm_g_norm2', 'm_w_ffn_in', 'm_w_ffn_out', 'm_g_final', 'v_w_ada', 'v_b_ada', 'v_g_norm1', 'v_w_in', 'v_g_cq', 'v_w_uq', 'v_g_ckv', 'v_w_ukv', 'v_rel_bias', 'v_g_out_a', 'v_g_out_b', 'v_w_out', 'v_g_norm2', 'v_w_ffn_in', 'v_w_ffn_out', 'v_g_final']
TWIN_OUTPUTS = ['loss', 'grad_x', 'grad_w_ada', 'grad_b_ada', 'grad_g_norm1', 'grad_w_in', 'grad_g_cq', 'grad_w_uq', 'grad_g_ckv', 'grad_w_ukv', 'grad_rel_bias', 'grad_g_out_a', 'grad_g_out_b', 'grad_w_out', 'grad_g_norm2', 'grad_w_ffn_in', 'grad_w_ffn_out', 'grad_g_final', 'delta_w_ada', 'delta_b_ada', 'delta_g_norm1', 'delta_w_in', 'delta_g_cq', 'delta_w_uq', 'delta_g_ckv', 'delta_w_ukv', 'delta_rel_bias', 'delta_g_out_a', 'delta_g_out_b', 'delta_w_out', 'delta_g_norm2', 'delta_w_ffn_in', 'delta_w_ffn_out', 'delta_g_final', 'new_m_w_ada', 'new_m_b_ada', 'new_m_g_norm1', 'new_m_w_in', 'new_m_g_cq', 'new_m_w_uq', 'new_m_g_ckv', 'new_m_w_ukv', 'new_m_rel_bias', 'new_m_g_out_a', 'new_m_g_out_b', 'new_m_w_out', 'new_m_g_norm2', 'new_m_w_ffn_in', 'new_m_w_ffn_out', 'new_m_g_final', 'new_v_w_ada', 'new_v_b_ada', 'new_v_g_norm1', 'new_v_w_in', 'new_v_g_cq', 'new_v_w_uq', 'new_v_g_ckv', 'new_v_w_ukv', 'new_v_rel_bias', 'new_v_g_out_a', 'new_v_g_out_b', 'new_v_w_out', 'new_v_g_norm2', 'new_v_w_ffn_in', 'new_v_w_ffn_out', 'new_v_g_final']
TWIN_LEAF_KINDS = {'loss': 'loss', 'grad_x': 'grad_x', 'grad_w_ada': 'grad_w', 'grad_b_ada': 'grad_w', 'grad_g_norm1': 'grad_w', 'grad_w_in': 'grad_w', 'grad_g_cq': 'grad_w', 'grad_w_uq': 'grad_w', 'grad_g_ckv': 'grad_w', 'grad_w_ukv': 'grad_w', 'grad_rel_bias': 'grad_w', 'grad_g_out_a': 'grad_w', 'grad_g_out_b': 'grad_w', 'grad_w_out': 'grad_w', 'grad_g_norm2': 'grad_w', 'grad_w_ffn_in': 'grad_w', 'grad_w_ffn_out': 'grad_w', 'grad_g_final': 'grad_w', 'delta_w_ada': 'delta_w', 'delta_b_ada': 'delta_w', 'delta_g_norm1': 'delta_w', 'delta_w_in': 'delta_w', 'delta_g_cq': 'delta_w', 'delta_w_uq': 'delta_w', 'delta_g_ckv': 'delta_w', 'delta_w_ukv': 'delta_w', 'delta_rel_bias': 'delta_w', 'delta_g_out_a': 'delta_w', 'delta_g_out_b': 'delta_w', 'delta_w_out': 'delta_w', 'delta_g_norm2': 'delta_w', 'delta_w_ffn_in': 'delta_w', 'delta_w_ffn_out': 'delta_w', 'delta_g_final': 'delta_w', 'new_m_w_ada': 'new_m', 'new_m_b_ada': 'new_m', 'new_m_g_norm1': 'new_m', 'new_m_w_in': 'new_m', 'new_m_g_cq': 'new_m', 'new_m_w_uq': 'new_m', 'new_m_g_ckv': 'new_m', 'new_m_w_ukv': 'new_m', 'new_m_rel_bias': 'new_m', 'new_m_g_out_a': 'new_m', 'new_m_g_out_b': 'new_m', 'new_m_w_out': 'new_m', 'new_m_g_norm2': 'new_m', 'new_m_w_ffn_in': 'new_m', 'new_m_w_ffn_out': 'new_m', 'new_m_g_final': 'new_m', 'new_v_w_ada': 'new_v', 'new_v_b_ada': 'new_v', 'new_v_g_norm1': 'new_v', 'new_v_w_in': 'new_v', 'new_v_g_cq': 'new_v', 'new_v_w_uq': 'new_v', 'new_v_g_ckv': 'new_v', 'new_v_w_ukv': 'new_v', 'new_v_rel_bias': 'new_v', 'new_v_g_out_a': 'new_v', 'new_v_g_out_b': 'new_v', 'new_v_w_out': 'new_v', 'new_v_g_norm2': 'new_v', 'new_v_w_ffn_in': 'new_v', 'new_v_w_ffn_out': 'new_v', 'new_v_g_final': 'new_v'}


def _forward(args):
    return _fwd_reference(*[args[k] for k in FWD_PARAMS])


def _output_shape():
    out = _jax.eval_shape(lambda: _forward(_fwd_setup_inputs(0)))
    return out.shape, out.dtype

N_MICROBATCH = 1
ADAM_LR = 0.001
ADAM_B1 = 0.9
ADAM_B2 = 0.999
ADAM_EPS = 1e-08
ADAM_WD = 0.01
ADAM_STEP = 10
PER_EXAMPLE_BATCH_AXIS = {'x': 0, 'c': 0, 'loss_target': 0}
SHARED_INPUTS = []
_WEIGHT_DTYPES = {'w_ada': _jnp.float32, 'b_ada': _jnp.float32, 'g_norm1': _jnp.float32, 'w_in': _jnp.float32, 'g_cq': _jnp.float32, 'w_uq': _jnp.float32, 'g_ckv': _jnp.float32, 'w_ukv': _jnp.float32, 'rel_bias': _jnp.float32, 'g_out_a': _jnp.float32, 'g_out_b': _jnp.float32, 'w_out': _jnp.float32, 'g_norm2': _jnp.float32, 'w_ffn_in': _jnp.float32, 'w_ffn_out': _jnp.float32, 'g_final': _jnp.float32}
MOMENT_SCALE = {'w_ada': 1.754174e-01, 'b_ada': 2.960241e-01, 'g_norm1': 5.022492e-02, 'w_in': 8.354572e-02, 'g_cq': 2.912596e-02, 'w_uq': 2.077624e-02, 'g_ckv': 1.733903e-01, 'w_ukv': 9.905460e-02, 'rel_bias': 3.694429e-02, 'g_out_a': 1.304238e-01, 'g_out_b': 1.568939e-01, 'w_out': 1.357775e-01, 'g_norm2': 5.275337e-02, 'w_ffn_in': 2.311433e-02, 'w_ffn_out': 3.789022e-02, 'g_final': 3.235063e+01}


def _to_microbatches(a, axis):
    t = _jnp.moveaxis(a, axis, 0)
    t = t.reshape((N_MICROBATCH, t.shape[0] // N_MICROBATCH) + t.shape[1:])
    return _jnp.moveaxis(t, 1, axis + 1)


def setup_inputs(seed: int = 0) -> dict:
    inp = _fwd_setup_inputs(seed)
    key = _jax.random.fold_in(_jax.random.key(seed), 7919)
    shape, _ = _output_shape()
    out = dict(inp)
    out["loss_target"] = _jax.random.normal(_jax.random.fold_in(key, 0), shape, _jnp.float32)
    for i, name in enumerate(TWIN_WEIGHTS):
        w = inp[name].astype(_jnp.float32)
        if MOMENT_SCALE is None:
            s = _jnp.sqrt(_jnp.mean(_jnp.square(w)) + 1e-30)
        else:
            s = MOMENT_SCALE[name]
        km, kv = _jax.random.split(_jax.random.fold_in(key, i + 1))
        out[name] = w
        out["m_" + name] = s * _jax.random.normal(km, w.shape, _jnp.float32)
        out["v_" + name] = (s * s) * _jax.random.uniform(kv, w.shape, _jnp.float32, 0.5, 1.5)
    if N_MICROBATCH > 1:
        for name, axis in PER_EXAMPLE_BATCH_AXIS.items():
            out[name] = _to_microbatches(out[name], axis)
    return {'x': out['x'], 'c': out['c'], 'w_ada': out['w_ada'], 'b_ada': out['b_ada'], 'g_norm1': out['g_norm1'], 'w_in': out['w_in'], 'g_cq': out['g_cq'], 'w_uq': out['w_uq'], 'g_ckv': out['g_ckv'], 'w_ukv': out['w_ukv'], 'rel_bias': out['rel_bias'], 'g_out_a': out['g_out_a'], 'g_out_b': out['g_out_b'], 'w_out': out['w_out'], 'g_norm2': out['g_norm2'], 'w_ffn_in': out['w_ffn_in'], 'w_ffn_out': out['w_ffn_out'], 'g_final': out['g_final'], 'loss_target': out['loss_target'], 'm_w_ada': out['m_w_ada'], 'm_b_ada': out['m_b_ada'], 'm_g_norm1': out['m_g_norm1'], 'm_w_in': out['m_w_in'], 'm_g_cq': out['m_g_cq'], 'm_w_uq': out['m_w_uq'], 'm_g_ckv': out['m_g_ckv'], 'm_w_ukv': out['m_w_ukv'], 'm_rel_bias': out['m_rel_bias'], 'm_g_out_a': out['m_g_out_a'], 'm_g_out_b': out['m_g_out_b'], 'm_w_out': out['m_w_out'], 'm_g_norm2': out['m_g_norm2'], 'm_w_ffn_in': out['m_w_ffn_in'], 'm_w_ffn_out': out['m_w_ffn_out'], 'm_g_final': out['m_g_final'], 'v_w_ada': out['v_w_ada'], 'v_b_ada': out['v_b_ada'], 'v_g_norm1': out['v_g_norm1'], 'v_w_in': out['v_w_in'], 'v_g_cq': out['v_g_cq'], 'v_w_uq': out['v_w_uq'], 'v_g_ckv': out['v_g_ckv'], 'v_w_ukv': out['v_w_ukv'], 'v_rel_bias': out['v_rel_bias'], 'v_g_out_a': out['v_g_out_a'], 'v_g_out_b': out['v_g_out_b'], 'v_w_out': out['v_w_out'], 'v_g_norm2': out['v_g_norm2'], 'v_w_ffn_in': out['v_w_ffn_in'], 'v_w_ffn_out': out['v_w_ffn_out'], 'v_g_final': out['v_g_final']}


def _loss(weights, diff, rest, loss_target):
    with _jax.named_scope("forward"):
        args = {**rest, TWIN_DIFF_INPUT: diff, **{k: w.astype(_WEIGHT_DTYPES[k]) for k, w in weights.items()}}
        y = _forward(args)
    with _jax.named_scope("loss_head"):
        err = _jnp.square(y.astype(_jnp.float32) - loss_target)
        return 0.5 * _jnp.sum(_jnp.mean(err, axis=-1)) if err.ndim else 0.5 * err


def _adamw(w, g, m, v):
    m = ADAM_B1 * m + (1.0 - ADAM_B1) * g
    v = ADAM_B2 * v + (1.0 - ADAM_B2) * _jnp.square(g)
    m_hat = m / (1.0 - ADAM_B1 ** ADAM_STEP)
    v_hat = v / (1.0 - ADAM_B2 ** ADAM_STEP)
    delta = -ADAM_LR * (m_hat / (_jnp.sqrt(v_hat) + ADAM_EPS) + ADAM_WD * w)
    return delta, m, v


def reference(x, c, w_ada, b_ada, g_norm1, w_in, g_cq, w_uq, g_ckv, w_ukv, rel_bias, g_out_a, g_out_b, w_out, g_norm2, w_ffn_in, w_ffn_out, g_final, loss_target, m_w_ada, m_b_ada, m_g_norm1, m_w_in, m_g_cq, m_w_uq, m_g_ckv, m_w_ukv, m_rel_bias, m_g_out_a, m_g_out_b, m_w_out, m_g_norm2, m_w_ffn_in, m_w_ffn_out, m_g_final, v_w_ada, v_b_ada, v_g_norm1, v_w_in, v_g_cq, v_w_uq, v_g_ckv, v_w_ukv, v_rel_bias, v_g_out_a, v_g_out_b, v_w_out, v_g_norm2, v_w_ffn_in, v_w_ffn_out, v_g_final):
    given = dict(x=x, c=c, w_ada=w_ada, b_ada=b_ada, g_norm1=g_norm1, w_in=w_in, g_cq=g_cq, w_uq=w_uq, g_ckv=g_ckv, w_ukv=w_ukv, rel_bias=rel_bias, g_out_a=g_out_a, g_out_b=g_out_b, w_out=w_out, g_norm2=g_norm2, w_ffn_in=w_ffn_in, w_ffn_out=w_ffn_out, g_final=g_final, loss_target=loss_target, m_w_ada=m_w_ada, m_b_ada=m_b_ada, m_g_norm1=m_g_norm1, m_w_in=m_w_in, m_g_cq=m_g_cq, m_w_uq=m_w_uq, m_g_ckv=m_g_ckv, m_w_ukv=m_w_ukv, m_rel_bias=m_rel_bias, m_g_out_a=m_g_out_a, m_g_out_b=m_g_out_b, m_w_out=m_w_out, m_g_norm2=m_g_norm2, m_w_ffn_in=m_w_ffn_in, m_w_ffn_out=m_w_ffn_out, m_g_final=m_g_final, v_w_ada=v_w_ada, v_b_ada=v_b_ada, v_g_norm1=v_g_norm1, v_w_in=v_w_in, v_g_cq=v_g_cq, v_w_uq=v_w_uq, v_g_ckv=v_g_ckv, v_w_ukv=v_w_ukv, v_rel_bias=v_rel_bias, v_g_out_a=v_g_out_a, v_g_out_b=v_g_out_b, v_w_out=v_w_out, v_g_norm2=v_g_norm2, v_w_ffn_in=v_w_ffn_in, v_w_ffn_out=v_w_ffn_out, v_g_final=v_g_final)
    weights = {n: given[n] for n in TWIN_WEIGHTS}
    shared = {n: given[n] for n in SHARED_INPUTS}
    per_example = {n: given[n] for n in ['x', 'c']}
    grad_fn = _jax.value_and_grad(_loss, argnums=(0, 1))

    def one_microbatch(ex, loss_target):
        ex = dict(ex)
        diff = ex.pop(TWIN_DIFF_INPUT)
        return grad_fn(weights, diff, {**shared, **ex}, loss_target)

    if N_MICROBATCH == 1:
        loss, (grad_w, grad_x) = one_microbatch(per_example, given["loss_target"])
    else:
        def body(carry, xs):
            loss_sum, grad_sum = carry
            l_k, (gw_k, gx_k) = one_microbatch(xs[0], xs[1])
            with _jax.named_scope("update"):
                return (loss_sum + l_k, _jax.tree.map(_jnp.add, grad_sum, gw_k)), gx_k

        init = (_jnp.zeros((), _jnp.float32), _jax.tree.map(_jnp.zeros_like, weights))
        (loss, grad_w), grad_x = _jax.lax.scan(body, init, (per_example, given["loss_target"]))
    with _jax.named_scope("update"):
        delta_w, new_m, new_v = {}, {}, {}
        for n in TWIN_WEIGHTS:
            delta_w[n], new_m[n], new_v[n] = _adamw(weights[n], grad_w[n], given["m_" + n], given["v_" + n])
    return (loss, grad_x, *[grad_w[n] for n in TWIN_WEIGHTS], *[delta_w[n] for n in TWIN_WEIGHTS],
            *[new_m[n] for n in TWIN_WEIGHTS], *[new_v[n] for n in TWIN_WEIGHTS])
```

```python
import functools

import numpy as np
import jax
import jax.numpy as jnp
from jax import lax
from jax.experimental import pallas as pl
from jax.experimental.pallas import tpu as pltpu

F32, BF16 = jnp.float32, jnp.bfloat16

N_DEV = 8
D = 1024
S = 2048
H = 8
E_A = 64
D_A = H * E_A
Q_LORA, KV_LORA = 384, 256
NOPE, ROPE, VDIM = 64, 32, 64
HP = 128
P_IN = 3 * D_A + Q_LORA + KV_LORA + ROPE
P_PAD = 3 * D_A + Q_LORA + KV_LORA + HP
TAIL0 = 3 * D_A
TAIL = P_PAD - TAIL0
D_FF = 2816
N_MOD = 6
EPS = 1e-6
NEG = -1e30
BLK = 128
DILATIONS = (1, 4, 16)
N_BUCKETS, MAX_DISTANCE = 32, 2048
ROPE_THETA = 10000.0
SCALE_A = E_A ** -0.5
SCALE_B = (NOPE + ROPE) ** -0.5
B1, B2, LR, ADAM_EPS, WD, STEP = 0.9, 0.999, 0.001, 1e-8, 0.01, 10
VMEM_LIMIT = 56 * 1024 * 1024


def _cp(*sem):
    return pltpu.CompilerParams(dimension_semantics=sem, vmem_limit_bytes=VMEM_LIMIT)


def _pick(n, prefs):
    for p in prefs:
        if n % p == 0:
            return p
    raise ValueError(f"no tile of {prefs} divides {n}")


OPERAND_BYTES = 6 * 1024 * 1024


def _pick_rows(m, k):
    return _pick(m, [p for p in (1024, 512, 256, 128, 16) if p * k * 2 <= OPERAND_BYTES])


def _dot(a, b, dims):
    return lax.dot_general(a, b, (dims, ((), ())), preferred_element_type=F32)


def _mm_nn(a, b, out_dtype, name):
    m, k = a.shape
    n = b.shape[1]
    tm, tn = _pick_rows(m, k), _pick(n, (512, 384, 256, 128))

    def body(a_ref, b_ref, o_ref):
        o_ref[...] = _dot(a_ref[...], b_ref[...], ((1,), (0,))).astype(o_ref.dtype)

    return pl.pallas_call(
        body, name=name, grid=(m // tm, n // tn),
        in_specs=[pl.BlockSpec((tm, k), lambda i, j: (i, 0)), pl.BlockSpec((k, tn), lambda i, j: (0, j))],
        out_specs=pl.BlockSpec((tm, tn), lambda i, j: (i, j)),
        out_shape=jax.ShapeDtypeStruct((m, n), out_dtype),
        compiler_params=_cp("parallel", "parallel"),
    )(a, b)


def _mm_nt(a, b, out_dtype, name):
    m, k = a.shape
    n = b.shape[0]
    tm, tn = _pick_rows(m, k), _pick(n, (512, 384, 256, 128))

    def body(a_ref, b_ref, o_ref):
        o_ref[...] = _dot(a_ref[...], b_ref[...], ((1,), (1,))).astype(o_ref.dtype)

    return pl.pallas_call(
        body, name=name, grid=(m // tm, n // tn),
        in_specs=[pl.BlockSpec((tm, k), lambda i, j: (i, 0)), pl.BlockSpec((tn, k), lambda i, j: (j, 0))],
        out_specs=pl.BlockSpec((tm, tn), lambda i, j: (i, j)),
        out_shape=jax.ShapeDtypeStruct((m, n), out_dtype),
        compiler_params=_cp("parallel", "parallel"),
    )(a, b)


def _mm_tn(a, b, name):
    t, m = a.shape
    n = b.shape[1]
    tm, tn, tk = _pick(m, (1024, 512, 384, 256, 128)), _pick(n, (512, 384, 256, 128)), _pick(t, (512, 16))
    nk = t // tk

    def body(a_ref, b_ref, o_ref, acc_ref):
        kk = pl.program_id(2)

        @pl.when(kk == 0)
        def _():
            acc_ref[...] = jnp.zeros_like(acc_ref)

        acc_ref[...] += _dot(a_ref[...], b_ref[...], ((0,), (0,)))

        @pl.when(kk == nk - 1)
        def _():
            o_ref[...] = acc_ref[...]

    return pl.pallas_call(
        body, name=name, grid=(m // tm, n // tn, nk),
        in_specs=[pl.BlockSpec((tk, tm), lambda i, j, kk: (kk, i)), pl.BlockSpec((tk, tn), lambda i, j, kk: (kk, j))],
        out_specs=pl.BlockSpec((tm, tn), lambda i, j, kk: (i, j)),
        out_shape=jax.ShapeDtypeStruct((m, n), F32),
        scratch_shapes=[pltpu.VMEM((tm, tn), F32)],
        compiler_params=_cp("parallel", "parallel", "arbitrary"),
    )(a, b)


TM = 256


def _row(w):
    return pl.BlockSpec((TM, w), lambda i: (i, 0))


def _row_at(w, col):
    return pl.BlockSpec((TM, w), lambda i: (i, col))


def _vec(w):
    return pl.BlockSpec((1, w), lambda i: (0, 0))


def _per_ex(w):
    return pl.BlockSpec((1, 1, w), lambda i: (i // (S // TM), 0, 0))


def _pos(w):
    return pl.BlockSpec((TM, w), lambda i: (i % (S // TM), 0))


def _full(shape):
    return pl.BlockSpec(shape, lambda i: (0,) * len(shape))


def _rms(x):
    return lax.rsqrt(jnp.mean(x * x, axis=-1, keepdims=True) + EPS)


def _rms_bwd(n, r, dn):
    return r * (dn - n * jnp.mean(dn * n, axis=-1, keepdims=True))


def _colsum(v):
    return jnp.sum(v, axis=0, keepdims=True)


def _acc_first(i, ref, val, every=None):
    first = (i == 0) if every is None else (i % every == 0)

    @pl.when(first)
    def _():
        ref[...] = jnp.zeros_like(ref)

    ref[...] += val.reshape(ref.shape)


def _pre1(x, g, sc, sh):
    t = x.shape[0]

    def body(x_ref, g_ref, sc_ref, sh_ref, h_ref):
        xv = x_ref[...]
        n = xv * _rms(xv)
        h_ref[...] = ((n * g_ref[...]) * (1.0 + sc_ref[0]) + sh_ref[0]).astype(BF16)

    return pl.pallas_call(
        body, name="pre1", grid=(t // TM,),
        in_specs=[_row(D), _vec(D), _per_ex(D), _per_ex(D)],
        out_specs=_row(D), out_shape=jax.ShapeDtypeStruct((t, D), BF16),
        compiler_params=_cp("parallel"),
    )(x, g, sc, sh)


def _rope_fwd(v, c, sm, sp):
    return v * c + pltpu.roll(v, HP - ROPE // 2, 1) * sm + pltpu.roll(v, ROPE // 2, 1) * sp


def _rope_bwd(dv, c, sm, sp):
    return dv * c + pltpu.roll(dv * sm, ROPE // 2, 1) + pltpu.roll(dv * sp, HP - ROPE // 2, 1)


def _mla_pre(proj, g_cq, g_ckv, w_uq, w_k, w_v, rc, rsm, rsp):
    t = proj.shape[0]

    def body(tail_ref, gq_ref, gkv_ref, wuq_ref, wk_ref, wv_ref, c_ref, sm_ref, sp_ref,
             q_ref, k_ref, v_ref, cqn_ref, ckvn_ref):
        tail = tail_ref[...]
        cq, ckv, kr = tail[:, :Q_LORA], tail[:, Q_LORA:Q_LORA + KV_LORA], tail[:, Q_LORA + KV_LORA:]
        cqn = (cq * _rms(cq) * gq_ref[...]).astype(BF16)
        ckvn = (ckv * _rms(ckv) * gkv_ref[...]).astype(BF16)
        cqn_ref[...] = cqn
        ckvn_ref[...] = ckvn
        c, sm, sp = c_ref[...], sm_ref[...], sp_ref[...]
        q = _dot(cqn, wuq_ref[...], ((1,), (0,)))
        kn = _dot(ckvn, wk_ref[...], ((1,), (0,)))
        v_ref[...] = _dot(ckvn, wv_ref[...], ((1,), (0,))).astype(BF16)
        krr = _rope_fwd(kr, c, sm, sp)
        for h in range(H):
            sl = slice(h * HP, (h + 1) * HP)
            q_ref[:, sl] = _rope_fwd(q[:, sl], c, sm, sp).astype(BF16)
            k_ref[:, sl] = (kn[:, sl] + krr).astype(BF16)

    wide = H * HP
    return pl.pallas_call(
        body, name="mla_pre", grid=(t // TM,),
        in_specs=[_row_at(TAIL, TAIL0 // TAIL), _vec(Q_LORA), _vec(KV_LORA), _full((Q_LORA, wide)),
                  _full((KV_LORA, wide)), _full((KV_LORA, wide)), _pos(HP), _pos(HP), _pos(HP)],
        out_specs=[_row(wide), _row(wide), _row(wide), _row(Q_LORA), _row(KV_LORA)],
        out_shape=[jax.ShapeDtypeStruct((t, wide), BF16)] * 3
        + [jax.ShapeDtypeStruct((t, Q_LORA), BF16), jax.ShapeDtypeStruct((t, KV_LORA), BF16)],
        compiler_params=_cp("parallel"),
    )(proj, g_cq, g_ckv, w_uq, w_k, w_v, rc, rsm, rsp)


def _mla_pre_bwd(proj, dq_, dk_, dv_, g_cq, g_ckv, w_uq, w_k, w_v, rc, rsm, rsp):
    t = proj.shape[0]
    wide = H * HP

    def body(tail_ref, dq_ref, dk_ref, dv_ref, gq_ref, gkv_ref, wuq_ref, wk_ref, wv_ref, c_ref, sm_ref, sp_ref,
             dqo_ref, dko_ref, dvo_ref, dtail_ref, dgq_ref, dgkv_ref):
        i = pl.program_id(0)
        tail = tail_ref[...]
        cq, ckv = tail[:, :Q_LORA], tail[:, Q_LORA:Q_LORA + KV_LORA]
        c, sm, sp = c_ref[...], sm_ref[...], sp_ref[...]
        dkr = jnp.zeros((TM, HP), F32)
        for h in range(H):
            sl = slice(h * HP, (h + 1) * HP)
            dqo_ref[:, sl] = _rope_bwd(dq_ref[:, sl], c, sm, sp).astype(BF16)
            dkr = dkr + dk_ref[:, sl]
        lane = lax.broadcasted_iota(jnp.int32, (TM, HP), 1)
        dkr = jnp.where((lane >= NOPE) & (lane < NOPE + ROPE), _rope_bwd(dkr, c, sm, sp), 0.0)
        dkb = dk_ref[...].astype(BF16)
        dvb = dv_ref[...].astype(BF16)
        dko_ref[...] = dkb
        dvo_ref[...] = dvb
        dcqn = _dot(dqo_ref[...], wuq_ref[...], ((1,), (1,)))
        dckvn = _dot(dkb, wk_ref[...], ((1,), (1,))) + _dot(dvb, wv_ref[...], ((1,), (1,)))
        rq, rkv = _rms(cq), _rms(ckv)
        nq, nkv = cq * rq, ckv * rkv
        _acc_first(i, dgq_ref, _colsum(dcqn * nq))
        _acc_first(i, dgkv_ref, _colsum(dckvn * nkv))
        dtail_ref[:, :Q_LORA] = _rms_bwd(nq, rq, dcqn * gq_ref[...]).astype(BF16)
        dtail_ref[:, Q_LORA:Q_LORA + KV_LORA] = _rms_bwd(nkv, rkv, dckvn * gkv_ref[...]).astype(BF16)
        dtail_ref[:, Q_LORA + KV_LORA:] = dkr.astype(BF16)

    return pl.pallas_call(
        body, name="mla_pre_bwd", grid=(t // TM,),
        in_specs=[_row_at(TAIL, TAIL0 // TAIL), _row(wide), _row(wide), _row(wide), _vec(Q_LORA), _vec(KV_LORA),
                  _full((Q_LORA, wide)), _full((KV_LORA, wide)), _full((KV_LORA, wide)), _pos(HP), _pos(HP), _pos(HP)],
        out_specs=[_row(wide), _row(wide), _row(wide), _row(TAIL), _vec(Q_LORA), _vec(KV_LORA)],
        out_shape=[jax.ShapeDtypeStruct((t, wide), BF16)] * 3 + [jax.ShapeDtypeStruct((t, TAIL), BF16),
                   jax.ShapeDtypeStruct((1, Q_LORA), F32), jax.ShapeDtypeStruct((1, KV_LORA), F32)],
        compiler_params=_cp("arbitrary"),
    )(proj, dq_, dk_, dv_, g_cq, g_ckv, w_uq, w_k, w_v, rc, rsm, rsp)


def _post_attn(out_a, out_b, g_a, g_b):
    t = out_a.shape[0]

    def body(a_ref, b_ref, ga_ref, gb_ref, y_ref):
        a, b = a_ref[...], b_ref[...]
        y_ref[:, :D_A] = (a * _rms(a) * ga_ref[...]).astype(BF16)
        y_ref[:, D_A:] = (b * _rms(b) * gb_ref[...]).astype(BF16)

    return pl.pallas_call(
        body, name="post_attn", grid=(t // TM,),
        in_specs=[_row(D_A), _row(D_A), _vec(D_A), _vec(D_A)],
        out_specs=_row(D), out_shape=jax.ShapeDtypeStruct((t, D), BF16),
        compiler_params=_cp("parallel"),
    )(out_a, out_b, g_a, g_b)


def _post_attn_bwd(dy, out_a, out_b, g_a, g_b):
    t = dy.shape[0]

    def body(dy_ref, a_ref, b_ref, ga_ref, gb_ref, da_ref, db_ref, dga_ref, dgb_ref):
        i = pl.program_id(0)
        dy_ = dy_ref[...]
        for src, g_ref, dst, dg_ref, sl in ((a_ref, ga_ref, da_ref, dga_ref, slice(0, D_A)),
                                            (b_ref, gb_ref, db_ref, dgb_ref, slice(D_A, D))):
            v = src[...]
            r = _rms(v)
            n = v * r
            dyv = dy_[:, sl]
            _acc_first(i, dg_ref, _colsum(dyv * n))
            dst[...] = _rms_bwd(n, r, dyv * g_ref[...])

    return pl.pallas_call(
        body, name="post_attn_bwd", grid=(t // TM,),
        in_specs=[_row(D), _row(D_A), _row(D_A), _vec(D_A), _vec(D_A)],
        out_specs=[_row(D_A), _row(D_A), _vec(D_A), _vec(D_A)],
        out_shape=[jax.ShapeDtypeStruct((t, D_A), F32)] * 2 + [jax.ShapeDtypeStruct((1, D_A), F32)] * 2,
        compiler_params=_cp("arbitrary"),
    )(dy, out_a, out_b, g_a, g_b)


def _resid_norm2(x, mix, g1, g, sc, sh):
    t = x.shape[0]

    def body(x_ref, mix_ref, g1_ref, g_ref, sc_ref, sh_ref, x2_ref, h_ref):
        x2 = x_ref[...] + g1_ref[0] * mix_ref[...]
        x2_ref[...] = x2
        n = x2 * _rms(x2)
        h_ref[...] = ((n * g_ref[...]) * (1.0 + sc_ref[0]) + sh_ref[0]).astype(BF16)

    return pl.pallas_call(
        body, name="resid_norm2", grid=(t // TM,),
        in_specs=[_row(D), _row(D), _per_ex(D), _vec(D), _per_ex(D), _per_ex(D)],
        out_specs=[_row(D), _row(D)],
        out_shape=[jax.ShapeDtypeStruct((t, D), F32), jax.ShapeDtypeStruct((t, D), BF16)],
        compiler_params=_cp("parallel"),
    )(x, mix, g1, g, sc, sh)


def _sigmoid(v):
    return 1.0 / (1.0 + jnp.exp(-v))


def _swiglu(gu):
    t = gu.shape[0]

    def body(g_ref, u_ref, a_ref):
        g = g_ref[...].astype(F32)
        a_ref[...] = (g * _sigmoid(g) * u_ref[...].astype(F32)).astype(BF16)

    return pl.pallas_call(
        body, name="swiglu", grid=(t // TM,),
        in_specs=[_row_at(D_FF, 0), _row_at(D_FF, 1)],
        out_specs=_row(D_FF), out_shape=jax.ShapeDtypeStruct((t, D_FF), BF16),
        compiler_params=_cp("parallel"),
    )(gu, gu)


def _swiglu_bwd(gu, da):
    t = gu.shape[0]

    def body(g_ref, u_ref, da_ref, dgu_ref):
        j = pl.program_id(1)
        g = g_ref[...].astype(F32)
        sg = _sigmoid(g)
        dav = da_ref[...].astype(F32)

        @pl.when(j == 0)
        def _():
            dgu_ref[...] = (dav * u_ref[...].astype(F32) * (sg * (1.0 + g * (1.0 - sg)))).astype(BF16)

        @pl.when(j == 1)
        def _():
            dgu_ref[...] = (dav * (g * sg)).astype(BF16)

    return pl.pallas_call(
        body, name="swiglu_bwd", grid=(t // TM, 2),
        in_specs=[pl.BlockSpec((TM, D_FF), lambda i, j: (i, 0)), pl.BlockSpec((TM, D_FF), lambda i, j: (i, 1)),
                  pl.BlockSpec((TM, D_FF), lambda i, j: (i, 0))],
        out_specs=pl.BlockSpec((TM, D_FF), lambda i, j: (i, j)),
        out_shape=jax.ShapeDtypeStruct((t, 2 * D_FF), BF16),
        compiler_params=_cp("parallel", "arbitrary"),
    )(gu, gu, da)


def _final(x2, f, g2, g_fin, target):
    t = x2.shape[0]
    nb = t // S
    tpb = S // TM

    def body(x2_ref, f_ref, g2_ref, g_ref, t_ref, dx3_ref, df_ref, loss_ref, dgf_ref, dg2_ref):
        i = pl.program_id(0)
        fv = f_ref[...]
        x3 = x2_ref[...] + g2_ref[0] * fv
        r = _rms(x3)
        n = x3 * r
        err = n * g_ref[...] - t_ref[...]
        _acc_first(i, loss_ref, _colsum(err * err))
        dy = err * (1.0 / D)
        _acc_first(i, dgf_ref, _colsum(dy * n))
        dx3 = _rms_bwd(n, r, dy * g_ref[...])
        dx3_ref[...] = dx3
        _acc_first(i, dg2_ref, _colsum(dx3 * fv), every=tpb)
        df_ref[...] = (dx3 * g2_ref[0]).astype(BF16)

    return pl.pallas_call(
        body, name="final", grid=(t // TM,),
        in_specs=[_row(D), _row(D), _per_ex(D), _vec(D), _row(D)],
        out_specs=[_row(D), _row(D), _vec(D), _vec(D), _per_ex(D)],
        out_shape=[jax.ShapeDtypeStruct((t, D), F32), jax.ShapeDtypeStruct((t, D), BF16),
                   jax.ShapeDtypeStruct((1, D), F32), jax.ShapeDtypeStruct((1, D), F32),
                   jax.ShapeDtypeStruct((nb, 1, D), F32)],
        compiler_params=_cp("arbitrary"),
    )(x2, f, g2, g_fin, target)


def _norm_bwd(xin, dh, dres, g, sc, gate=None):
    t = xin.shape[0]
    nb = t // S
    tpb = S // TM
    gated = gate is not None

    def body(*refs):
        if gated:
            x_ref, dh_ref, dres_ref, g_ref, sc_ref, mix_ref, g1_ref, dx_ref, dsh_ref, dsc_ref, dg_ref, dg1_ref, dmix_ref = refs
        else:
            x_ref, dh_ref, dres_ref, g_ref, sc_ref, dx_ref, dsh_ref, dsc_ref, dg_ref = refs
        i = pl.program_id(0)
        xv, dhv = x_ref[...], dh_ref[...]
        r = _rms(xv)
        n = xv * r
        gv = g_ref[...]
        _acc_first(i, dsh_ref, _colsum(dhv), every=tpb)
        _acc_first(i, dsc_ref, _colsum(dhv * (n * gv)), every=tpb)
        dng = dhv * (1.0 + sc_ref[0])
        _acc_first(i, dg_ref, _colsum(dng * n))
        dx = dres_ref[...] + _rms_bwd(n, r, dng * gv)
        dx_ref[...] = dx
        if gated:
            _acc_first(i, dg1_ref, _colsum(dx * mix_ref[...]), every=tpb)
            dmix_ref[...] = (dx * g1_ref[0]).astype(BF16)

    in_specs = [_row(D), _row(D), _row(D), _vec(D), _per_ex(D)]
    out_specs = [_row(D), _per_ex(D), _per_ex(D), _vec(D)]
    out_shape = [jax.ShapeDtypeStruct((t, D), F32), jax.ShapeDtypeStruct((nb, 1, D), F32),
                 jax.ShapeDtypeStruct((nb, 1, D), F32), jax.ShapeDtypeStruct((1, D), F32)]
    args = [xin, dh, dres, g, sc]
    if gated:
        in_specs += [_row(D), _per_ex(D)]
        out_specs += [_per_ex(D), _row(D)]
        out_shape += [jax.ShapeDtypeStruct((nb, 1, D), F32), jax.ShapeDtypeStruct((t, D), BF16)]
        args += list(gate)
    return pl.pallas_call(
        body, name="norm2_bwd" if gated else "norm1_bwd", grid=(t // TM,),
        in_specs=in_specs, out_specs=out_specs, out_shape=out_shape,
        compiler_params=_cp("arbitrary"),
    )(*args)


TQ = 256


def _causal_mask(i, j):
    row = lax.broadcasted_iota(jnp.int32, (TQ, TQ), 0) + i * TQ
    col = lax.broadcasted_iota(jnp.int32, (TQ, TQ), 1) + j * TQ
    return row >= col


def _mla_fwd(q, k, v):
    t = q.shape[0]
    nb = t // S
    nq = S // TQ

    def body(q_ref, k_ref, v_ref, o_ref, lse_ref):
        i = pl.program_id(2)
        out = jnp.zeros((TQ, HP), F32)
        for h in range(2):
            sl = slice(h * HP, (h + 1) * HP)
            qh = q_ref[:, sl]

            def step(j, carry, sl=sl, qh=qh):
                m, l, acc = carry
                rows = pl.ds(pl.multiple_of(j * TQ, TQ), TQ)
                s = _dot(qh, k_ref[rows, sl], ((1,), (1,))) * SCALE_B
                s = jnp.where(_causal_mask(i, j), s, NEG)
                m2 = jnp.maximum(m, jnp.max(s, axis=-1, keepdims=True))
                p = jnp.exp(s - m2)
                a = jnp.exp(m - m2)
                l2 = a * l + jnp.sum(p, axis=-1, keepdims=True)
                acc2 = a * acc + _dot(p.astype(BF16), v_ref[rows, sl], ((1,), (0,)))
                return m2, l2, acc2

            m, l, acc = lax.fori_loop(0, i + 1, step, (jnp.full((TQ, 1), NEG, F32), jnp.zeros((TQ, 1), F32),
                                                       jnp.zeros((TQ, HP), F32)))
            out = out + acc / l
            lse_ref[:, sl] = jnp.broadcast_to(m + jnp.log(l), (TQ, HP))
        o_ref[...] = out

    return pl.pallas_call(
        body, name="mla_fwd", grid=(nb, H // 2, nq),
        in_specs=[pl.BlockSpec((TQ, 2 * HP), lambda b, p, i: (b * nq + i, p)),
                  pl.BlockSpec((S, 2 * HP), lambda b, p, i: (b, p)),
                  pl.BlockSpec((S, 2 * HP), lambda b, p, i: (b, p))],
        out_specs=[pl.BlockSpec((TQ, HP), lambda b, p, i: (b * nq + i, p)),
                   pl.BlockSpec((TQ, 2 * HP), lambda b, p, i: (b * nq + i, p))],
        out_shape=[jax.ShapeDtypeStruct((t, H * VDIM), F32), jax.ShapeDtypeStruct((t, H * HP), F32)],
        compiler_params=_cp("parallel", "parallel", "arbitrary"),
    )(q, k, v)


def _mla_bwd(q, k, v, o, do, lse):
    t = q.shape[0]
    nb = t // S
    nq = S // TQ

    def body(q_ref, k_ref, v_ref, o_ref, do_ref, lse_ref, dq_ref, dk_ref, dv_ref):
        dq_ref[...] = jnp.zeros_like(dq_ref)
        dk_ref[...] = jnp.zeros_like(dk_ref)
        dv_ref[...] = jnp.zeros_like(dv_ref)
        lane = lax.broadcasted_iota(jnp.int32, (TQ, HP), 1)

        def q_block(i, _):
            ri = pl.ds(pl.multiple_of(i * TQ, TQ), TQ)
            dov = do_ref[ri, :]
            prod = dov * o_ref[ri, :]
            dob = dov.astype(BF16)
            for h in range(2):
                sl = slice(h * HP, (h + 1) * HP)
                mine = (lane < VDIM) if h == 0 else (lane >= VDIM)
                delta = jnp.sum(jnp.where(mine, prod, 0.0), axis=-1, keepdims=True)
                qh = q_ref[ri, sl]
                lse = lse_ref[ri, sl][:, :1]

                def k_block(j, _, sl=sl, qh=qh, lse=lse, delta=delta):
                    rj = pl.ds(pl.multiple_of(j * TQ, TQ), TQ)
                    kh, vh = k_ref[rj, sl], v_ref[rj, sl]
                    s = _dot(qh, kh, ((1,), (1,))) * SCALE_B
                    p = jnp.where(_causal_mask(i, j), jnp.exp(s - lse), 0.0)
                    dp = _dot(dob, vh, ((1,), (1,)))
                    ds = (p * (dp - delta) * SCALE_B).astype(BF16)
                    pb = p.astype(BF16)
                    dq_ref[ri, sl] += _dot(ds, kh, ((1,), (0,)))
                    dk_ref[rj, sl] += _dot(ds, qh, ((0,), (0,)))
                    dv_ref[rj, sl] += _dot(pb, dob, ((0,), (0,)))
                    return 0

                lax.fori_loop(0, i + 1, k_block, 0)
            return 0

        lax.fori_loop(0, nq, q_block, 0)

    wide2 = pl.BlockSpec((S, 2 * HP), lambda b, p: (b, p))
    pair = pl.BlockSpec((S, HP), lambda b, p: (b, p))
    return pl.pallas_call(
        body, name="mla_bwd", grid=(nb, H // 2),
        in_specs=[wide2, wide2, wide2, pair, pair, wide2],
        out_specs=[wide2, wide2, wide2],
        out_shape=[jax.ShapeDtypeStruct((t, H * HP), F32)] * 3,
        compiler_params=_cp("parallel", "parallel"),
    )(q, k, v, o, do, lse)


def _t5_bucket(dist):
    max_exact = N_BUCKETS // 2
    d = np.maximum(dist, 1).astype(np.float64)
    large = max_exact + (np.log(d / max_exact) / np.log(MAX_DISTANCE / max_exact) * (N_BUCKETS - max_exact)).astype(np.int64)
    large = np.minimum(large, N_BUCKETS - 1)
    return np.where(dist < max_exact, dist, large).astype(np.int32)


def _band_geometry():
    a = np.arange(BLK)[:, None]
    bk = np.arange(2 * BLK)[None, :]
    steps = BLK + a - bk
    valid = (steps >= 0) & (steps <= BLK)
    buckets = np.stack([_t5_bucket(np.clip(steps, 0, BLK) * d) for d in DILATIONS])
    return buckets, valid


def _band_bias(rel_bias):
    buckets, valid = _band_geometry()
    onehot = (jnp.asarray(buckets)[..., None] == jnp.arange(N_BUCKETS)).astype(F32)
    bias = jnp.einsum("rqkn,nh->rhqk", onehot, rel_bias, precision=lax.Precision.HIGHEST)
    return jnp.where(jnp.asarray(valid)[None, None], bias, NEG)


def _dil_blocks(r):
    d = DILATIONS[r]
    nblk = S // d // BLK

    def geom(t):
        res, blk = t // nblk, t % nblk
        start = blk * (BLK * d) + res
        return start, jnp.maximum(start - BLK * d, res), blk > 0

    return d, geom


def _dil_fwd(proj, biasm):
    t = proj.shape[0]
    nb = t // S

    def body(q_ref, k_ref, v_ref, b_ref, o_ref, lse_ref, ob_ref, lb_ref):
        lane = lax.broadcasted_iota(jnp.int32, (BLK, HP), 1)
        col = lax.broadcasted_iota(jnp.int32, (BLK, 2 * BLK), 1)
        for r in range(3):
            d, geom = _dil_blocks(r)

            def block(tt, _, r=r, d=d, geom=geom):
                start, pstart, has_prev = geom(tt)
                cur = pl.ds(start, BLK, stride=d)
                prev = pl.ds(pstart, BLK, stride=d)
                qt = q_ref[cur, :]
                kt = jnp.concatenate([k_ref[prev, :], k_ref[cur, :]], axis=0).astype(BF16)
                vt = jnp.concatenate([v_ref[prev, :], v_ref[cur, :]], axis=0).astype(BF16)
                dead = jnp.logical_and(col < BLK, jnp.logical_not(has_prev))
                outs, lses = [], []
                for h in range(2):
                    mine = (lane < E_A) if h == 0 else (lane >= E_A)
                    qh = jnp.where(mine, qt, 0.0).astype(BF16)
                    s = _dot(qh, kt, ((1,), (1,))) * SCALE_A + b_ref[r, h]
                    s = jnp.where(dead, NEG, s)
                    m = jnp.max(s, axis=-1, keepdims=True)
                    p = jnp.exp(s - m)
                    l = jnp.sum(p, axis=-1, keepdims=True)
                    outs.append(_dot(p.astype(BF16), vt, ((1,), (0,))) / l)
                    lses.append(m + jnp.log(l))
                ob_ref[r, cur, :] = jnp.where(lane < E_A, outs[0], outs[1])
                lb_ref[r, cur, :] = jnp.where(lane < E_A, lses[0], lses[1])
                return 0

            lax.fori_loop(0, 16, block, 0)

        def merge(c, _):
            rows = pl.ds(pl.multiple_of(c * TQ, TQ), TQ)
            l0, l1, l2 = lb_ref[0, rows, :], lb_ref[1, rows, :], lb_ref[2, rows, :]
            m = jnp.maximum(jnp.maximum(l0, l1), l2)
            e0, e1, e2 = jnp.exp(l0 - m), jnp.exp(l1 - m), jnp.exp(l2 - m)
            tot = e0 + e1 + e2
            o_ref[rows, :] = (e0 * ob_ref[0, rows, :] + e1 * ob_ref[1, rows, :] + e2 * ob_ref[2, rows, :]) / tot
            lse_ref[rows, :] = m + jnp.log(tot)
            return 0

        lax.fori_loop(0, S // TQ, merge, 0)

    npair = H // 2
    return pl.pallas_call(
        body, name="dil_fwd", grid=(nb, npair),
        in_specs=[pl.BlockSpec((S, HP), lambda b, p: (b, p)), pl.BlockSpec((S, HP), lambda b, p: (b, npair + p)),
                  pl.BlockSpec((S, HP), lambda b, p: (b, 2 * npair + p)),
                  pl.BlockSpec((3, 2, BLK, 2 * BLK), lambda b, p: (0, p, 0, 0))],
        out_specs=[pl.BlockSpec((S, HP), lambda b, p: (b, p))] * 2,
        out_shape=[jax.ShapeDtypeStruct((t, D_A), F32)] * 2,
        scratch_shapes=[pltpu.VMEM((3, S, HP), F32), pltpu.VMEM((3, S, HP), F32)],
        compiler_params=_cp("parallel", "parallel"),
    )(proj, proj, proj, biasm)


def _dil_bwd(proj, biasm, o, do, lse):
    t = proj.shape[0]
    nb = t // S

    def body(q_ref, k_ref, v_ref, b_ref, o_ref, do_ref, lse_ref, dq_ref, dk_ref, dv_ref, ds_ref):
        dq_ref[...] = jnp.zeros_like(dq_ref)
        dk_ref[...] = jnp.zeros_like(dk_ref)
        dv_ref[...] = jnp.zeros_like(dv_ref)
        ds_ref[...] = jnp.zeros_like(ds_ref)
        lane = lax.broadcasted_iota(jnp.int32, (BLK, HP), 1)
        col = lax.broadcasted_iota(jnp.int32, (BLK, 2 * BLK), 1)
        for r in range(3):
            d, geom = _dil_blocks(r)

            def block(tt, _, r=r, d=d, geom=geom):
                start, pstart, has_prev = geom(tt)
                cur = pl.ds(start, BLK, stride=d)
                prev = pl.ds(pstart, BLK, stride=d)
                qt = q_ref[cur, :]
                kt = jnp.concatenate([k_ref[prev, :], k_ref[cur, :]], axis=0).astype(BF16)
                vt = jnp.concatenate([v_ref[prev, :], v_ref[cur, :]], axis=0).astype(BF16)
                dot_ = do_ref[cur, :]
                prod = dot_ * o_ref[cur, :]
                lset = lse_ref[cur, :]
                dead = jnp.logical_and(col < BLK, jnp.logical_not(has_prev))
                dqs = []
                dkt = jnp.zeros((2 * BLK, HP), F32)
                dvt = jnp.zeros((2 * BLK, HP), F32)
                for h in range(2):
                    mine = (lane < E_A) if h == 0 else (lane >= E_A)
                    qh = jnp.where(mine, qt, 0.0).astype(BF16)
                    doh = jnp.where(mine, dot_, 0.0).astype(BF16)
                    delta = jnp.sum(jnp.where(mine, prod, 0.0), axis=-1, keepdims=True)
                    lse_h = lset[:, h * E_A:h * E_A + 1]
                    s = _dot(qh, kt, ((1,), (1,))) * SCALE_A + b_ref[r, h]
                    p = jnp.where(dead, 0.0, jnp.exp(s - lse_h))
                    dp = _dot(doh, vt, ((1,), (1,)))
                    dl = p * (dp - delta)
                    ds_ref[0, r, h] += dl
                    dsb = (dl * SCALE_A).astype(BF16)
                    dqs.append(_dot(dsb, kt, ((1,), (0,))))
                    dkt = dkt + _dot(dsb, qh, ((0,), (0,)))
                    dvt = dvt + _dot(p.astype(BF16), doh, ((0,), (0,)))
                dq_ref[cur, :] += jnp.where(lane < E_A, dqs[0], dqs[1])
                dk_ref[prev, :] += dkt[:BLK]
                dv_ref[prev, :] += dvt[:BLK]
                dk_ref[cur, :] += dkt[BLK:]
                dv_ref[cur, :] += dvt[BLK:]
                return 0

            lax.fori_loop(0, 16, block, 0)

    npair = H // 2
    pair = pl.BlockSpec((S, HP), lambda b, p: (b, p))
    return pl.pallas_call(
        body, name="dil_bwd", grid=(nb, npair),
        in_specs=[pair, pl.BlockSpec((S, HP), lambda b, p: (b, npair + p)),
                  pl.BlockSpec((S, HP), lambda b, p: (b, 2 * npair + p)),
                  pl.BlockSpec((3, 2, BLK, 2 * BLK), lambda b, p: (0, p, 0, 0)), pair, pair, pair],
        out_specs=[pair, pair, pair, pl.BlockSpec((1, 3, 2, BLK, 2 * BLK), lambda b, p: (b, 0, p, 0, 0))],
        out_shape=[jax.ShapeDtypeStruct((t, D_A), F32)] * 3 + [jax.ShapeDtypeStruct((nb, 3, H, BLK, 2 * BLK), F32)],
        compiler_params=_cp("parallel", "parallel"),
    )(proj, proj, proj, biasm, o, do, lse)


def _rel_bias_grad(dlogits):
    nb = dlogits.shape[0]
    buckets, _ = _band_geometry()
    kk = 3 * BLK * 2 * BLK
    dl = jnp.transpose(dlogits, (0, 2, 1, 3, 4)).reshape(nb, H, kk)
    bk = jnp.asarray(buckets.reshape(1, kk))
    tk = kk // 12

    def body(dl_ref, bk_ref, o_ref):
        j = pl.program_id(0)
        onehot = (bk_ref[...] == lax.broadcasted_iota(jnp.int32, (N_BUCKETS, tk), 0)).astype(F32)
        tot = dl_ref[0]
        for b in range(1, nb):
            tot = tot + dl_ref[b]
        part = lax.dot_general(onehot, tot, ((((1,), (1,))), ((), ())), preferred_element_type=F32,
                               precision=lax.Precision.HIGHEST)
        _acc_first(j, o_ref, part)

    return pl.pallas_call(
        body, name="rel_bias_grad", grid=(kk // tk,),
        in_specs=[pl.BlockSpec((nb, H, tk), lambda j: (0, 0, j)), pl.BlockSpec((1, tk), lambda j: (0, j))],
        out_specs=pl.BlockSpec((N_BUCKETS, H), lambda j: (0, 0)),
        out_shape=jax.ShapeDtypeStruct((N_BUCKETS, H), F32),
        compiler_params=_cp("arbitrary"),
    )(dl, bk)


def _mesh_place():
    x, y, c = lax.axis_index("x"), lax.axis_index("y"), lax.axis_index("c")
    return x, y, c


def _peer(k):
    x, y, c = _mesh_place()
    px = 1 - x if k & 4 else x
    py = 1 - y if k & 2 else y
    pc = 1 - c if k & 1 else c
    return (px, py, pc), 4 * px + 2 * py + pc


ANY = pl.BlockSpec(memory_space=pl.ANY)


def _all_gather(arrays, name):
    n_arr = len(arrays)

    def body(*refs):
        ins, outs = refs[:n_arr], refs[n_arr:2 * n_arr]
        send, recv, loc = refs[2 * n_arr:]
        x, y, c = _mesh_place()
        me = 4 * x + 2 * y + c
        local, remote = [], []
        for a in range(n_arr):
            cp = pltpu.make_async_copy(ins[a], outs[a].at[me], loc.at[a])
            cp.start()
            local.append(cp)
            for k in range(1, N_DEV):
                dev, idx = _peer(k)
                put = pltpu.make_async_remote_copy(ins[a], outs[a].at[me], send.at[a * (N_DEV - 1) + k - 1], recv.at[a * (N_DEV - 1) + k - 1],
                                                   device_id=dev, device_id_type=pl.DeviceIdType.MESH)
                put.start()
                got = pltpu.make_async_remote_copy(ins[a], outs[a].at[idx], send.at[a * (N_DEV - 1) + k - 1], recv.at[a * (N_DEV - 1) + k - 1],
                                                   device_id=dev, device_id_type=pl.DeviceIdType.MESH)
                remote.append((put, got))
        for cp in local:
            cp.wait()
        for put, got in remote:
            put.wait_send()
            got.wait_recv()

    return pl.pallas_call(
        body, name=name,
        in_specs=[ANY] * n_arr, out_specs=[ANY] * n_arr,
        out_shape=[jax.ShapeDtypeStruct((N_DEV,) + a.shape, a.dtype) for a in arrays],
        scratch_shapes=[pltpu.SemaphoreType.DMA((n_arr * (N_DEV - 1),)), pltpu.SemaphoreType.DMA((n_arr * (N_DEV - 1),)),
                        pltpu.SemaphoreType.DMA((n_arr,))],
        compiler_params=pltpu.CompilerParams(has_side_effects=True),
    )(*arrays)


def _all_to_all(arrays, name):
    n_arr = len(arrays)

    def body(*refs):
        ins, outs = refs[:n_arr], refs[n_arr:2 * n_arr]
        send, recv, loc = refs[2 * n_arr:]
        x, y, c = _mesh_place()
        me = 4 * x + 2 * y + c
        local, remote = [], []
        for a in range(n_arr):
            cp = pltpu.make_async_copy(ins[a].at[me], outs[a].at[me], loc.at[a])
            cp.start()
            local.append(cp)
            for k in range(1, N_DEV):
                dev, idx = _peer(k)
                put = pltpu.make_async_remote_copy(ins[a].at[idx], outs[a].at[me], send.at[a * (N_DEV - 1) + k - 1], recv.at[a * (N_DEV - 1) + k - 1],
                                                   device_id=dev, device_id_type=pl.DeviceIdType.MESH)
                put.start()
                got = pltpu.make_async_remote_copy(ins[a].at[idx], outs[a].at[idx], send.at[a * (N_DEV - 1) + k - 1], recv.at[a * (N_DEV - 1) + k - 1],
                                                   device_id=dev, device_id_type=pl.DeviceIdType.MESH)
                remote.append((put, got))
        for cp in local:
            cp.wait()
        for put, got in remote:
            put.wait_send()
            got.wait_recv()

    return pl.pallas_call(
        body, name=name,
        in_specs=[ANY] * n_arr, out_specs=[ANY] * n_arr,
        out_shape=[jax.ShapeDtypeStruct(a.shape, a.dtype) for a in arrays],
        scratch_shapes=[pltpu.SemaphoreType.DMA((n_arr * (N_DEV - 1),)), pltpu.SemaphoreType.DMA((n_arr * (N_DEV - 1),)),
                        pltpu.SemaphoreType.DMA((n_arr,))],
        compiler_params=pltpu.CompilerParams(has_side_effects=True),
    )(*arrays)


def _silu_rows(c):
    def body(c_ref, o_ref):
        v = c_ref[...]
        o_ref[...] = v * _sigmoid(v)

    return pl.pallas_call(body, name="cond", out_shape=jax.ShapeDtypeStruct(c.shape, F32))(c)


def _mod_slab(cond_all, w_ada, b_slab):
    def body(c_ref, w_ref, b_ref, o_ref):
        o_ref[...] = _dot(c_ref[...].astype(BF16), w_ref[...].astype(BF16), ((1,), (0,))) + b_ref[...]

    return pl.pallas_call(body, name="mod_slab",
                          out_shape=jax.ShapeDtypeStruct((cond_all.shape[0], w_ada.shape[1]), F32),
                          compiler_params=pltpu.CompilerParams(vmem_limit_bytes=VMEM_LIMIT))(cond_all, w_ada, b_slab)


def _ada_grad(cond_all, dmod_cols):
    def body(c_ref, d_ref, o_ref):
        o_ref[...] = _dot(c_ref[...].astype(BF16), d_ref[...].astype(BF16), ((0,), (0,)))

    return pl.pallas_call(body, name="ada_grad",
                          out_shape=jax.ShapeDtypeStruct((cond_all.shape[1], dmod_cols.shape[1]), F32),
                          compiler_params=pltpu.CompilerParams(vmem_limit_bytes=VMEM_LIMIT))(cond_all, dmod_cols)


def _adam_math(g, w, m, v):
    m2 = B1 * m + (1.0 - B1) * g
    v2 = B2 * v + (1.0 - B2) * (g * g)
    m_hat = m2 / (1.0 - B1 ** STEP)
    v_hat = v2 / (1.0 - B2 ** STEP)
    return -LR * (m_hat / (jnp.sqrt(v_hat) + ADAM_EPS) + WD * w), m2, v2


def _adamw(parts, w, m, v, name):
    n, rows, cols = parts.shape
    tr = _pick(rows, (128, 96, 64, 32, 16, 8))

    def body(p_ref, w_ref, m_ref, v_ref, g_ref, d_ref, m2_ref, v2_ref):
        g = p_ref[0].astype(F32)
        for s in range(1, n):
            g = g + p_ref[s].astype(F32)
        g_ref[...] = g
        d_ref[...], m2_ref[...], v2_ref[...] = _adam_math(g, w_ref[...], m_ref[...], v_ref[...])

    blk = pl.BlockSpec((tr, cols), lambda i: (i, 0))
    return pl.pallas_call(
        body, name=name, grid=(rows // tr,),
        in_specs=[pl.BlockSpec((n, tr, cols), lambda i: (0, i, 0)), blk, blk, blk],
        out_specs=[blk] * 4, out_shape=[jax.ShapeDtypeStruct((rows, cols), F32)] * 4,
        compiler_params=_cp("parallel"),
    )(parts, w, m, v)


SMALL = (("b_ada", N_MOD * D), ("g_norm1", D), ("g_cq", Q_LORA), ("g_ckv", KV_LORA), ("rel_bias", N_BUCKETS * H),
         ("g_out_a", D_A), ("g_out_b", D_A), ("g_norm2", D), ("g_final", D))
TILE = 8 * 128


def _tiles(n):
    return -(-n // TILE) * 8


SMALL_ROWS = sum(_tiles(n) for _, n in SMALL)
LOSS_ROWS = 8
PACK_ROWS = SMALL_ROWS + LOSS_ROWS


def _pack(vals):
    parts = []
    for v in vals:
        v = v.reshape(-1)
        rows = _tiles(v.shape[0])
        parts.append(jnp.pad(v, (0, rows * 128 - v.shape[0])).reshape(rows, 128))
    return jnp.concatenate(parts, axis=0)


def _unpack(packed, shapes):
    out, r = [], 0
    for (_, n), shp in zip(SMALL, shapes):
        rows = _tiles(n)
        out.append(packed[r:r + rows].reshape(-1)[:n].reshape(shp))
        r += rows
    return out


def _small_update(parts, w, m, v):
    def body(p_ref, w_ref, m_ref, v_ref, g_ref, d_ref, m2_ref, v2_ref, loss_ref):
        tot = p_ref[0]
        for s in range(1, N_DEV):
            tot = tot + p_ref[s]
        g = tot[:SMALL_ROWS]
        g_ref[...] = g
        d_ref[...], m2_ref[...], v2_ref[...] = _adam_math(g, w_ref[...], m_ref[...], v_ref[...])
        loss_ref[...] = jnp.broadcast_to((0.5 / D) * jnp.sum(tot[SMALL_ROWS:]), loss_ref.shape)

    return pl.pallas_call(
        body, name="small_update",
        out_shape=[jax.ShapeDtypeStruct((SMALL_ROWS, 128), F32)] * 4 + [jax.ShapeDtypeStruct((8, 128), F32)],
    )(parts, w, m, v)


def _cols_from_blocks(g):
    return jnp.transpose(g, (1, 0, 2)).reshape(g.shape[1], N_DEV * g.shape[2])


def _cols_to_blocks(w):
    r, c = w.shape
    return jnp.transpose(w.reshape(r, N_DEV, c // N_DEV), (1, 0, 2))


def _pad_w_in(w):
    z = jnp.zeros((w.shape[0], NOPE), w.dtype)
    return jnp.concatenate([w[:, :P_IN - ROPE], z, w[:, P_IN - ROPE:], z[:, :HP - NOPE - ROPE]], axis=1)


def _unpad_w_in(g):
    k0 = P_IN - ROPE + NOPE
    return jnp.concatenate([g[:, :P_IN - ROPE], g[:, k0:k0 + ROPE]], axis=1)


def _pad_w_uq(w):
    w3 = w.reshape(Q_LORA, H, NOPE + ROPE)
    return jnp.pad(w3, ((0, 0), (0, 0), (0, HP - NOPE - ROPE))).reshape(Q_LORA, H * HP)


def _unpad_w_uq(g):
    return g.reshape(Q_LORA, H, HP)[:, :, :NOPE + ROPE].reshape(Q_LORA, H * (NOPE + ROPE))


def _split_w_ukv(w):
    w4 = w.reshape(KV_LORA, H // 2, 2, HP)
    z = jnp.zeros((KV_LORA, H // 2, NOPE), w.dtype)
    kn, vv = w4[..., :NOPE], w4[..., NOPE:]
    w_k = jnp.stack([jnp.concatenate([kn[:, :, 0], z], -1), jnp.concatenate([kn[:, :, 1], z], -1)], axis=2)
    w_v = jnp.stack([jnp.concatenate([vv[:, :, 0], z], -1), jnp.concatenate([z, vv[:, :, 1]], -1)], axis=2)
    return w_k.reshape(KV_LORA, H * HP), w_v.reshape(KV_LORA, H * HP)


def _join_w_ukv(g_k, g_v):
    gk = g_k.reshape(KV_LORA, H // 2, 2, HP)
    gv = g_v.reshape(KV_LORA, H // 2, 2, HP)
    even = jnp.concatenate([gk[:, :, 0, :NOPE], gv[:, :, 0, :VDIM]], -1)
    odd = jnp.concatenate([gk[:, :, 1, :NOPE], gv[:, :, 1, VDIM:]], -1)
    return jnp.stack([even, odd], axis=2).reshape(KV_LORA, H * HP)


def _rope_tables():
    half = ROPE // 2
    inv = ROPE_THETA ** (-jnp.arange(half, dtype=F32) / half)
    ang = jnp.arange(S, dtype=F32)[:, None] * inv[None, :]
    cos, sin = jnp.cos(ang), jnp.sin(ang)
    ones, zeros = jnp.ones((S, NOPE), F32), jnp.zeros((S, NOPE), F32)
    tail1, tail0 = jnp.ones((S, HP - NOPE - ROPE), F32), jnp.zeros((S, HP - NOPE - ROPE), F32)
    zh = jnp.zeros((S, half), F32)
    c = jnp.concatenate([ones, cos, cos, tail1], axis=1)
    sm = jnp.concatenate([zeros, -sin, zh, tail0], axis=1)
    sp = jnp.concatenate([zeros, zh, sin, tail0], axis=1)
    return c, sm, sp


def _local_step(x, mod, target, g_norm1, w_in_p, g_cq, w_uq_p, g_ckv, w_k, w_v, rel_bias, g_out_a, g_out_b, w_out,
                g_norm2, w_ffn_in, w_ffn_out, g_final):
    nb = x.shape[0] // S
    sh1, sc1, g1, sh2, sc2, g2 = (mod[:, n].reshape(nb, 1, D) for n in range(N_MOD))
    rc, rsm, rsp = _rope_tables()
    biasm = _band_bias(rel_bias)

    h1 = _pre1(x, g_norm1, sc1, sh1)
    proj = _mm_nn(h1, w_in_p, F32, "proj")
    q, k, v, cqn, ckvn = _mla_pre(proj, g_cq, g_ckv, w_uq_p, w_k, w_v, rc, rsm, rsp)
    out_b, lse_b = _mla_fwd(q, k, v)
    out_a, lse_a = _dil_fwd(proj, biasm)
    y = _post_attn(out_a, out_b, g_out_a, g_out_b)
    mix = _mm_nn(y, w_out, F32, "mix")
    x2, h2 = _resid_norm2(x, mix, g1, g_norm2, sc2, sh2)
    gu = _mm_nn(h2, w_ffn_in, BF16, "ffn_in")
    act = _swiglu(gu)
    f = _mm_nn(act, w_ffn_out, F32, "ffn_out")
    dx3, df, loss_cols, dg_final, dg2 = _final(x2, f, g2, g_final, target)

    dact = _mm_nt(df, w_ffn_out, BF16, "d_act")
    gw_ffn_out = _mm_tn(act, df, "gw_ffn_out")
    dgu = _swiglu_bwd(gu, dact)
    dh2 = _mm_nt(dgu, w_ffn_in, F32, "d_h2")
    gw_ffn_in = _mm_tn(h2, dgu, "gw_ffn_in")
    dx2, dsh2, dsc2, dg_norm2, dg1, dmix = _norm_bwd(x2, dh2, dx3, g_norm2, sc2, gate=(mix, g1))
    dy = _mm_nt(dmix, w_out, F32, "d_y")
    gw_out = _mm_tn(y, dmix, "gw_out")
    dout_a, dout_b, dg_out_a, dg_out_b = _post_attn_bwd(dy, out_a, out_b, g_out_a, g_out_b)
    dq_b, dk_b, dv_b = _mla_bwd(q, k, v, out_b, dout_b, lse_b)
    dq_a, dk_a, dv_a, dlogits = _dil_bwd(proj, biasm, out_a, dout_a, lse_a)
    g_rel = _rel_bias_grad(dlogits)
    dqr, dkr, dvr, dtail, dg_cq, dg_ckv = _mla_pre_bwd(proj, dq_b, dk_b, dv_b, g_cq, g_ckv, w_uq_p, w_k, w_v, rc, rsm, rsp)
    gw_uq = _mm_tn(cqn, dqr, "gw_uq")
    gw_k = _mm_tn(ckvn, dkr, "gw_k")
    gw_v = _mm_tn(ckvn, dvr, "gw_v")
    dproj = jnp.concatenate([dq_a.astype(BF16), dk_a.astype(BF16), dv_a.astype(BF16), dtail], axis=1)
    dh1 = _mm_nt(dproj, w_in_p, F32, "d_h1")
    gw_in = _mm_tn(h1, dproj, "gw_in")
    grad_x, dsh1, dsc1, dg_norm1 = _norm_bwd(x, dh1, dx2, g_norm1, sc1)

    dmod = jnp.concatenate([dsh1, dsc1, dg1, dsh2, dsc2, dg2], axis=1)
    small = dict(g_norm1=dg_norm1, g_cq=dg_cq, g_ckv=dg_ckv, rel_bias=g_rel, g_out_a=dg_out_a, g_out_b=dg_out_b,
                 g_norm2=dg_norm2, g_final=dg_final)
    big = dict(w_in=gw_in, w_uq=gw_uq, w_k=gw_k, w_v=gw_v, w_out=gw_out, w_ffn_in=gw_ffn_in, w_ffn_out=gw_ffn_out)
    return grad_x, dmod, loss_cols, small, big


def kernel(x, c, w_ada, b_ada, g_norm1, w_in, g_cq, w_uq, g_ckv, w_ukv, rel_bias, g_out_a, g_out_b, w_out, g_norm2, w_ffn_in, w_ffn_out, g_final, loss_target, m_w_ada, m_b_ada, m_g_norm1, m_w_in, m_g_cq, m_w_uq, m_g_ckv, m_w_ukv, m_rel_bias, m_g_out_a, m_g_out_b, m_w_out, m_g_norm2, m_w_ffn_in, m_w_ffn_out, m_g_final, v_w_ada, v_b_ada, v_g_norm1, v_w_in, v_g_cq, v_w_uq, v_g_ckv, v_w_ukv, v_rel_bias, v_g_out_a, v_g_out_b, v_w_out, v_g_norm2, v_w_ffn_in, v_w_ffn_out, v_g_final):
    nb = x.shape[0]
    t = nb * S
    xt, tt = x.reshape(t, D), loss_target.reshape(t, D)
    me = 4 * lax.axis_index("x") + 2 * lax.axis_index("y") + lax.axis_index("c")

    shards = [w_in[0], w_uq[0], w_ukv[0], w_out[0], w_ffn_in[0], w_ffn_out[0]]
    gathered = _all_gather([_silu_rows(c)] + [s.astype(BF16) for s in shards], "gather_weights")
    cond_all = gathered[0].reshape(N_DEV * nb, D)
    w_in_f, w_uq_f, w_ukv_f = (_cols_from_blocks(g) for g in gathered[1:4])
    w_out_f = gathered[4].reshape(D, D)
    w_ffn_in_f = _cols_from_blocks(gathered[5])
    w_ffn_out_f = gathered[6].reshape(D_FF, D)
    w_k, w_v = _split_w_ukv(w_ukv_f)

    ncol = N_MOD * D // N_DEV
    b_slab = lax.dynamic_slice(b_ada, (0, me * ncol), (1, ncol))
    slab = _mod_slab(cond_all, w_ada[0], b_slab)
    (mod_rows,) = _all_to_all([slab.reshape(N_DEV, nb, ncol)], "scatter_mod")
    mod = jnp.transpose(mod_rows, (1, 0, 2)).reshape(nb, N_MOD, D)

    grad_x, dmod, loss_cols, small, big = _local_step(
        xt, mod, tt, g_norm1, _pad_w_in(w_in_f), g_cq, _pad_w_uq(w_uq_f), g_ckv, w_k, w_v, rel_bias, g_out_a, g_out_b,
        w_out_f, g_norm2, w_ffn_in_f, w_ffn_out_f, g_final.reshape(1, D))

    slabs = [
        _cols_to_blocks(_unpad_w_in(big["w_in"])), _cols_to_blocks(_unpad_w_uq(big["w_uq"])),
        _cols_to_blocks(_join_w_ukv(big["w_k"], big["w_v"])), big["w_out"].reshape(N_DEV, D // N_DEV, D),
        _cols_to_blocks(big["w_ffn_in"]), big["w_ffn_out"].reshape(N_DEV, D_FF // N_DEV, D),
    ]
    dmod_blocks = jnp.transpose(dmod.reshape(nb, N_DEV, ncol), (1, 0, 2))
    got = _all_to_all([dmod_blocks] + [s.astype(BF16) for s in slabs], "exchange_grads")
    g_ada = _ada_grad(cond_all, got[0].reshape(N_DEV * nb, ncol))

    mine = _pack([jnp.sum(dmod, axis=0)] + [small[n] for n, _ in SMALL[1:]] + [loss_cols])
    (parts,) = _all_gather([mine], "gather_small")
    small_w = [b_ada, g_norm1, g_cq, g_ckv, rel_bias, g_out_a, g_out_b, g_norm2, g_final]
    small_m = [m_b_ada, m_g_norm1, m_g_cq, m_g_ckv, m_rel_bias, m_g_out_a, m_g_out_b, m_g_norm2, m_g_final]
    small_v = [v_b_ada, v_g_norm1, v_g_cq, v_g_ckv, v_rel_bias, v_g_out_a, v_g_out_b, v_g_norm2, v_g_final]
    sg, sd, sm, sv, loss8 = _small_update(parts, _pack(small_w), _pack(small_m), _pack(small_v))
    shapes = [w.shape for w in small_w]
    sg, sd, sm, sv = (_unpack(p, shapes) for p in (sg, sd, sm, sv))

    big_w = [w_in, w_uq, w_ukv, w_out, w_ffn_in, w_ffn_out]
    big_m = [m_w_in, m_w_uq, m_w_ukv, m_w_out, m_w_ffn_in, m_w_ffn_out]
    big_v = [v_w_in, v_w_uq, v_w_ukv, v_w_out, v_w_ffn_in, v_w_ffn_out]
    names = ["w_in", "w_uq", "w_ukv", "w_out", "w_ffn_in", "w_ffn_out"]
    upd = {"w_ada": _adamw(g_ada[None], w_ada[0], m_w_ada[0], v_w_ada[0], "adamw_w_ada")}
    for n, p, w, m, v in zip(names, got[1:], big_w, big_m, big_v):
        upd[n] = _adamw(p, w[0], m[0], v[0], "adamw_" + n)
    for i, (n, _) in enumerate(SMALL):
        upd[n] = (sg[i], sd[i], sm[i], sv[i])

    order = ["w_ada", "b_ada", "g_norm1", "w_in", "g_cq", "w_uq", "g_ckv", "w_ukv", "rel_bias", "g_out_a", "g_out_b",
             "w_out", "g_norm2", "w_ffn_in", "w_ffn_out", "g_final"]
    like = dict(w_ada=w_ada, w_in=w_in, w_uq=w_uq, w_ukv=w_ukv, w_out=w_out, w_ffn_in=w_ffn_in, w_ffn_out=w_ffn_out)
    outs = [loss8[0, 0], grad_x.reshape(x.shape)]
    for part in range(4):
        for n in order:
            val = upd[n][part]
            outs.append(val.reshape(like[n].shape) if n in like else val)
    return tuple(outs)
```

```python
import functools

import numpy as np
import jax
import jax.numpy as jnp
from jax import lax
from jax.experimental import pallas as pl
from jax.experimental.pallas import tpu as pltpu

F32, BF16 = jnp.float32, jnp.bfloat16

N_DEV = 8
D = 1024
S = 2048
H = 8
E_A = 64
D_A = H * E_A
Q_LORA, KV_LORA = 384, 256
NOPE, ROPE, VDIM = 64, 32, 64
HP = 128
P_IN = 3 * D_A + Q_LORA + KV_LORA + ROPE
P_PAD = 3 * D_A + Q_LORA + KV_LORA + HP
TAIL0 = 3 * D_A
TAIL = P_PAD - TAIL0
D_FF = 2816
N_MOD = 6
EPS = 1e-6
NEG = -1e30
BLK = 128
DILATIONS = (1, 4, 16)
N_BUCKETS, MAX_DISTANCE = 32, 2048
ROPE_THETA = 10000.0
SCALE_A = E_A ** -0.5
SCALE_B = (NOPE + ROPE) ** -0.5
B1, B2, LR, ADAM_EPS, WD, STEP = 0.9, 0.999, 0.001, 1e-8, 0.01, 10
VMEM_LIMIT = 56 * 1024 * 1024


def _cp(*sem):
    return pltpu.CompilerParams(dimension_semantics=sem, vmem_limit_bytes=VMEM_LIMIT)


def _pick(n, prefs):
    for p in prefs:
        if n % p == 0:
            return p
    raise ValueError(f"no tile of {prefs} divides {n}")


OPERAND_BYTES = 6 * 1024 * 1024


def _pick_rows(m, k):
    return _pick(m, [p for p in (1024, 512, 256, 128, 16) if p * k * 2 <= OPERAND_BYTES])


def _dot(a, b, dims):
    return lax.dot_general(a, b, (dims, ((), ())), preferred_element_type=F32)


def _mm_nn(a, b, out_dtype, name):
    m, k = a.shape
    n = b.shape[1]
    tm, tn = _pick_rows(m, k), _pick(n, (512, 384, 256, 128))

    def body(a_ref, b_ref, o_ref):
        o_ref[...] = _dot(a_ref[...], b_ref[...], ((1,), (0,))).astype(o_ref.dtype)

    return pl.pallas_call(
        body, name=name, grid=(m // tm, n // tn),
        in_specs=[pl.BlockSpec((tm, k), lambda i, j: (i, 0)), pl.BlockSpec((k, tn), lambda i, j: (0, j))],
        out_specs=pl.BlockSpec((tm, tn), lambda i, j: (i, j)),
        out_shape=jax.ShapeDtypeStruct((m, n), out_dtype),
        compiler_params=_cp("parallel", "parallel"),
    )(a, b)


def _mm_nt(a, b, out_dtype, name):
    m, k = a.shape
    n = b.shape[0]
    tm, tn = _pick_rows(m, k), _pick(n, (512, 384, 256, 128))

    def body(a_ref, b_ref, o_ref):
        o_ref[...] = _dot(a_ref[...], b_ref[...], ((1,), (1,))).astype(o_ref.dtype)

    return pl.pallas_call(
        body, name=name, grid=(m // tm, n // tn),
        in_specs=[pl.BlockSpec((tm, k), lambda i, j: (i, 0)), pl.BlockSpec((tn, k), lambda i, j: (j, 0))],
        out_specs=pl.BlockSpec((tm, tn), lambda i, j: (i, j)),
        out_shape=jax.ShapeDtypeStruct((m, n), out_dtype),
        compiler_params=_cp("parallel", "parallel"),
    )(a, b)


def _mm_tn(a, b, name):
    t, m = a.shape
    n = b.shape[1]
    tm, tn, tc = _pick(m, (512, 384, 256, 128)), _pick(n, (512, 384, 256, 128)), _pick(t, (512, 16))

    def body(a_ref, b_ref, o_ref, at_ref):
        @pl.when(pl.program_id(1) == 0)
        def _():
            def chunk(c, _):
                rows = pl.ds(pl.multiple_of(c * tc, tc), tc)
                at_ref[:, rows] = a_ref[rows, :].T
                return 0

            lax.fori_loop(0, t // tc, chunk, 0)

        o_ref[...] = _dot(at_ref[...], b_ref[...], ((1,), (0,)))

    return pl.pallas_call(
        body, name=name, grid=(m // tm, n // tn),
        in_specs=[pl.BlockSpec((t, tm), lambda i, j: (0, i)), pl.BlockSpec((t, tn), lambda i, j: (0, j))],
        out_specs=pl.BlockSpec((tm, tn), lambda i, j: (i, j)),
        out_shape=jax.ShapeDtypeStruct((m, n), F32),
        scratch_shapes=[pltpu.VMEM((tm, t), BF16)],
        compiler_params=_cp("parallel", "arbitrary"),
    )(a, b)


TM = 256


def _row(w):
    return pl.BlockSpec((TM, w), lambda i: (i, 0))


def _row_at(w, col):
    return pl.BlockSpec((TM, w), lambda i: (i, col))


def _vec(w):
    return pl.BlockSpec((1, w), lambda i: (0, 0))


def _per_ex(w):
    return pl.BlockSpec((1, 1, w), lambda i: (i // (S // TM), 0, 0))


def _pos(w):
    return pl.BlockSpec((TM, w), lambda i: (i % (S // TM), 0))


def _full(shape):
    return pl.BlockSpec(shape, lambda i: (0,) * len(shape))


def _rms(x):
    return lax.rsqrt(jnp.mean(x * x, axis=-1, keepdims=True) + EPS)


def _rms_bwd(n, r, dn):
    return r * (dn - n * jnp.mean(dn * n, axis=-1, keepdims=True))


def _colsum(v):
    return jnp.sum(v, axis=0, keepdims=True)


def _acc_first(i, ref, val, every=None):
    first = (i == 0) if every is None else (i % every == 0)

    @pl.when(first)
    def _():
        ref[...] = jnp.zeros_like(ref)

    ref[...] += val.reshape(ref.shape)


def _pre1(x, g, sc, sh):
    t = x.shape[0]

    def body(x_ref, g_ref, sc_ref, sh_ref, h_ref):
        xv = x_ref[...]
        n = xv * _rms(xv)
        h_ref[...] = ((n * g_ref[...]) * (1.0 + sc_ref[0]) + sh_ref[0]).astype(BF16)

    return pl.pallas_call(
        body, name="pre1", grid=(t // TM,),
        in_specs=[_row(D), _vec(D), _per_ex(D), _per_ex(D)],
        out_specs=_row(D), out_shape=jax.ShapeDtypeStruct((t, D), BF16),
        compiler_params=_cp("parallel"),
    )(x, g, sc, sh)


def _rope_fwd(v, c, sm, sp):
    return v * c + pltpu.roll(v, HP - ROPE // 2, 1) * sm + pltpu.roll(v, ROPE // 2, 1) * sp


def _rope_bwd(dv, c, sm, sp):
    return dv * c + pltpu.roll(dv * sm, ROPE // 2, 1) + pltpu.roll(dv * sp, HP - ROPE // 2, 1)


def _mla_pre(proj, g_cq, g_ckv, w_uq, w_k, w_v, rc, rsm, rsp):
    t = proj.shape[0]

    def body(tail_ref, gq_ref, gkv_ref, wuq_ref, wk_ref, wv_ref, c_ref, sm_ref, sp_ref,
             q_ref, k_ref, v_ref, cqn_ref, ckvn_ref):
        tail = tail_ref[...]
        cq, ckv, kr = tail[:, :Q_LORA], tail[:, Q_LORA:Q_LORA + KV_LORA], tail[:, Q_LORA + KV_LORA:]
        cqn = (cq * _rms(cq) * gq_ref[...]).astype(BF16)
        ckvn = (ckv * _rms(ckv) * gkv_ref[...]).astype(BF16)
        cqn_ref[...] = cqn
        ckvn_ref[...] = ckvn
        c, sm, sp = c_ref[...], sm_ref[...], sp_ref[...]
        q = _dot(cqn, wuq_ref[...], ((1,), (0,)))
        kn = _dot(ckvn, wk_ref[...], ((1,), (0,)))
        v_ref[...] = _dot(ckvn, wv_ref[...], ((1,), (0,))).astype(BF16)
        krr = _rope_fwd(kr, c, sm, sp)
        for h in range(H):
            sl = slice(h * HP, (h + 1) * HP)
            q_ref[:, sl] = _rope_fwd(q[:, sl], c, sm, sp).astype(BF16)
            k_ref[:, sl] = (kn[:, sl] + krr).astype(BF16)

    wide = H * HP
    return pl.pallas_call(
        body, name="mla_pre", grid=(t // TM,),
        in_specs=[_row_at(TAIL, TAIL0 // TAIL), _vec(Q_LORA), _vec(KV_LORA), _full((Q_LORA, wide)),
                  _full((KV_LORA, wide)), _full((KV_LORA, wide)), _pos(HP), _pos(HP), _pos(HP)],
        out_specs=[_row(wide), _row(wide), _row(wide), _row(Q_LORA), _row(KV_LORA)],
        out_shape=[jax.ShapeDtypeStruct((t, wide), BF16)] * 3
        + [jax.ShapeDtypeStruct((t, Q_LORA), BF16), jax.ShapeDtypeStruct((t, KV_LORA), BF16)],
        compiler_params=_cp("parallel"),
    )(proj, g_cq, g_ckv, w_uq, w_k, w_v, rc, rsm, rsp)


def _mla_pre_bwd(proj, dq_, dk_, dv_, g_cq, g_ckv, w_uq, w_k, w_v, rc, rsm, rsp):
    t = proj.shape[0]
    wide = H * HP

    def body(tail_ref, dq_ref, dk_ref, dv_ref, gq_ref, gkv_ref, wuq_ref, wk_ref, wv_ref, c_ref, sm_ref, sp_ref,
             dqo_ref, dko_ref, dvo_ref, dtail_ref, dgq_ref, dgkv_ref):
        i = pl.program_id(0)
        tail = tail_ref[...]
        cq, ckv = tail[:, :Q_LORA], tail[:, Q_LORA:Q_LORA + KV_LORA]
        c, sm, sp = c_ref[...], sm_ref[...], sp_ref[...]
        dkr = jnp.zeros((TM, HP), F32)
        for h in range(H):
            sl = slice(h * HP, (h + 1) * HP)
            dqo_ref[:, sl] = _rope_bwd(dq_ref[:, sl], c, sm, sp).astype(BF16)
            dkr = dkr + dk_ref[:, sl]
        lane = lax.broadcasted_iota(jnp.int32, (TM, HP), 1)
        dkr = jnp.where((lane >= NOPE) & (lane < NOPE + ROPE), _rope_bwd(dkr, c, sm, sp), 0.0)
        dkb = dk_ref[...].astype(BF16)
        dvb = dv_ref[...].astype(BF16)
        dko_ref[...] = dkb
        dvo_ref[...] = dvb
        dcqn = _dot(dqo_ref[...], wuq_ref[...], ((1,), (1,)))
        dckvn = _dot(dkb, wk_ref[...], ((1,), (1,))) + _dot(dvb, wv_ref[...], ((1,), (1,)))
        rq, rkv = _rms(cq), _rms(ckv)
        nq, nkv = cq * rq, ckv * rkv
        _acc_first(i, dgq_ref, _colsum(dcqn * nq))
        _acc_first(i, dgkv_ref, _colsum(dckvn * nkv))
        dtail_ref[:, :Q_LORA] = _rms_bwd(nq, rq, dcqn * gq_ref[...]).astype(BF16)
        dtail_ref[:, Q_LORA:Q_LORA + KV_LORA] = _rms_bwd(nkv, rkv, dckvn * gkv_ref[...]).astype(BF16)
        dtail_ref[:, Q_LORA + KV_LORA:] = dkr.astype(BF16)

    return pl.pallas_call(
        body, name="mla_pre_bwd", grid=(t // TM,),
        in_specs=[_row_at(TAIL, TAIL0 // TAIL), _row(wide), _row(wide), _row(wide), _vec(Q_LORA), _vec(KV_LORA),
                  _full((Q_LORA, wide)), _full((KV_LORA, wide)), _full((KV_LORA, wide)), _pos(HP), _pos(HP), _pos(HP)],
        out_specs=[_row(wide), _row(wide), _row(wide), _row(TAIL), _vec(Q_LORA), _vec(KV_LORA)],
        out_shape=[jax.ShapeDtypeStruct((t, wide), BF16)] * 3 + [jax.ShapeDtypeStruct((t, TAIL), BF16),
                   jax.ShapeDtypeStruct((1, Q_LORA), F32), jax.ShapeDtypeStruct((1, KV_LORA), F32)],
        compiler_params=_cp("arbitrary"),
    )(proj, dq_, dk_, dv_, g_cq, g_ckv, w_uq, w_k, w_v, rc, rsm, rsp)


def _post_attn(out_a, out_b, g_a, g_b):
    t = out_a.shape[0]

    def body(a_ref, b_ref, ga_ref, gb_ref, y_ref):
        a, b = a_ref[...], b_ref[...]
        y_ref[:, :D_A] = (a * _rms(a) * ga_ref[...]).astype(BF16)
        y_ref[:, D_A:] = (b * _rms(b) * gb_ref[...]).astype(BF16)

    return pl.pallas_call(
        body, name="post_attn", grid=(t // TM,),
        in_specs=[_row(D_A), _row(D_A), _vec(D_A), _vec(D_A)],
        out_specs=_row(D), out_shape=jax.ShapeDtypeStruct((t, D), BF16),
        compiler_params=_cp("parallel"),
    )(out_a, out_b, g_a, g_b)


def _post_attn_bwd(dy, out_a, out_b, g_a, g_b):
    t = dy.shape[0]

    def body(dy_ref, a_ref, b_ref, ga_ref, gb_ref, da_ref, db_ref, dga_ref, dgb_ref):
        i = pl.program_id(0)
        dy_ = dy_ref[...]
        for src, g_ref, dst, dg_ref, sl in ((a_ref, ga_ref, da_ref, dga_ref, slice(0, D_A)),
                                            (b_ref, gb_ref, db_ref, dgb_ref, slice(D_A, D))):
            v = src[...]
            r = _rms(v)
            n = v * r
            dyv = dy_[:, sl]
            _acc_first(i, dg_ref, _colsum(dyv * n))
            dst[...] = _rms_bwd(n, r, dyv * g_ref[...])

    return pl.pallas_call(
        body, name="post_attn_bwd", grid=(t // TM,),
        in_specs=[_row(D), _row(D_A), _row(D_A), _vec(D_A), _vec(D_A)],
        out_specs=[_row(D_A), _row(D_A), _vec(D_A), _vec(D_A)],
        out_shape=[jax.ShapeDtypeStruct((t, D_A), F32)] * 2 + [jax.ShapeDtypeStruct((1, D_A), F32)] * 2,
        compiler_params=_cp("arbitrary"),
    )(dy, out_a, out_b, g_a, g_b)


def _resid_norm2(x, mix, g1, g, sc, sh):
    t = x.shape[0]

    def body(x_ref, mix_ref, g1_ref, g_ref, sc_ref, sh_ref, x2_ref, h_ref):
        x2 = x_ref[...] + g1_ref[0] * mix_ref[...]
        x2_ref[...] = x2
        n = x2 * _rms(x2)
        h_ref[...] = ((n * g_ref[...]) * (1.0 + sc_ref[0]) + sh_ref[0]).astype(BF16)

    return pl.pallas_call(
        body, name="resid_norm2", grid=(t // TM,),
        in_specs=[_row(D), _row(D), _per_ex(D), _vec(D), _per_ex(D), _per_ex(D)],
        out_specs=[_row(D), _row(D)],
        out_shape=[jax.ShapeDtypeStruct((t, D), F32), jax.ShapeDtypeStruct((t, D), BF16)],
        compiler_params=_cp("parallel"),
    )(x, mix, g1, g, sc, sh)


def _sigmoid(v):
    return 1.0 / (1.0 + jnp.exp(-v))


def _swiglu(gu):
    t = gu.shape[0]

    def body(g_ref, u_ref, a_ref):
        g = g_ref[...].astype(F32)
        a_ref[...] = (g * _sigmoid(g) * u_ref[...].astype(F32)).astype(BF16)

    return pl.pallas_call(
        body, name="swiglu", grid=(t // TM,),
        in_specs=[_row_at(D_FF, 0), _row_at(D_FF, 1)],
        out_specs=_row(D_FF), out_shape=jax.ShapeDtypeStruct((t, D_FF), BF16),
        compiler_params=_cp("parallel"),
    )(gu, gu)


def _swiglu_bwd(gu, da):
    t = gu.shape[0]

    def body(g_ref, u_ref, da_ref, dgu_ref):
        j = pl.program_id(1)
        g = g_ref[...].astype(F32)
        sg = _sigmoid(g)
        dav = da_ref[...].astype(F32)

        @pl.when(j == 0)
        def _():
            dgu_ref[...] = (dav * u_ref[...].astype(F32) * (sg * (1.0 + g * (1.0 - sg)))).astype(BF16)

        @pl.when(j == 1)
        def _():
            dgu_ref[...] = (dav * (g * sg)).astype(BF16)

    return pl.pallas_call(
        body, name="swiglu_bwd", grid=(t // TM, 2),
        in_specs=[pl.BlockSpec((TM, D_FF), lambda i, j: (i, 0)), pl.BlockSpec((TM, D_FF), lambda i, j: (i, 1)),
                  pl.BlockSpec((TM, D_FF), lambda i, j: (i, 0))],
        out_specs=pl.BlockSpec((TM, D_FF), lambda i, j: (i, j)),
        out_shape=jax.ShapeDtypeStruct((t, 2 * D_FF), BF16),
        compiler_params=_cp("parallel", "arbitrary"),
    )(gu, gu, da)


def _final(x2, f, g2, g_fin, target):
    t = x2.shape[0]
    nb = t // S
    tpb = S // TM

    def body(x2_ref, f_ref, g2_ref, g_ref, t_ref, dx3_ref, df_ref, loss_ref, dgf_ref, dg2_ref):
        i = pl.program_id(0)
        fv = f_ref[...]
        x3 = x2_ref[...] + g2_ref[0] * fv
        r = _rms(x3)
        n = x3 * r
        err = n * g_ref[...] - t_ref[...]
        _acc_first(i, loss_ref, _colsum(err * err))
        dy = err * (1.0 / D)
        _acc_first(i, dgf_ref, _colsum(dy * n))
        dx3 = _rms_bwd(n, r, dy * g_ref[...])
        dx3_ref[...] = dx3
        _acc_first(i, dg2_ref, _colsum(dx3 * fv), every=tpb)
        df_ref[...] = (dx3 * g2_ref[0]).astype(BF16)

    return pl.pallas_call(
        body, name="final", grid=(t // TM,),
        in_specs=[_row(D), _row(D), _per_ex(D), _vec(D), _row(D)],
        out_specs=[_row(D), _row(D), _vec(D), _vec(D), _per_ex(D)],
        out_shape=[jax.ShapeDtypeStruct((t, D), F32), jax.ShapeDtypeStruct((t, D), BF16),
                   jax.ShapeDtypeStruct((1, D), F32), jax.ShapeDtypeStruct((1, D), F32),
                   jax.ShapeDtypeStruct((nb, 1, D), F32)],
        compiler_params=_cp("arbitrary"),
    )(x2, f, g2, g_fin, target)


def _norm_bwd(xin, dh, dres, g, sc, gate=None):
    t = xin.shape[0]
    nb = t // S
    tpb = S // TM
    gated = gate is not None

    def body(*refs):
        if gated:
            x_ref, dh_ref, dres_ref, g_ref, sc_ref, mix_ref, g1_ref, dx_ref, dsh_ref, dsc_ref, dg_ref, dg1_ref, dmix_ref = refs
        else:
            x_ref, dh_ref, dres_ref, g_ref, sc_ref, dx_ref, dsh_ref, dsc_ref, dg_ref = refs
        i = pl.program_id(0)
        xv, dhv = x_ref[...], dh_ref[...]
        r = _rms(xv)
        n = xv * r
        gv = g_ref[...]
        _acc_first(i, dsh_ref, _colsum(dhv), every=tpb)
        _acc_first(i, dsc_ref, _colsum(dhv * (n * gv)), every=tpb)
        dng = dhv * (1.0 + sc_ref[0])
        _acc_first(i, dg_ref, _colsum(dng * n))
        dx = dres_ref[...] + _rms_bwd(n, r, dng * gv)
        dx_ref[...] = dx
        if gated:
            _acc_first(i, dg1_ref, _colsum(dx * mix_ref[...]), every=tpb)
            dmix_ref[...] = (dx * g1_ref[0]).astype(BF16)

    in_specs = [_row(D), _row(D), _row(D), _vec(D), _per_ex(D)]
    out_specs = [_row(D), _per_ex(D), _per_ex(D), _vec(D)]
    out_shape = [jax.ShapeDtypeStruct((t, D), F32), jax.ShapeDtypeStruct((nb, 1, D), F32),
                 jax.ShapeDtypeStruct((nb, 1, D), F32), jax.ShapeDtypeStruct((1, D), F32)]
    args = [xin, dh, dres, g, sc]
    if gated:
        in_specs += [_row(D), _per_ex(D)]
        out_specs += [_per_ex(D), _row(D)]
        out_shape += [jax.ShapeDtypeStruct((nb, 1, D), F32), jax.ShapeDtypeStruct((t, D), BF16)]
        args += list(gate)
    return pl.pallas_call(
        body, name="norm2_bwd" if gated else "norm1_bwd", grid=(t // TM,),
        in_specs=in_specs, out_specs=out_specs, out_shape=out_shape,
        compiler_params=_cp("arbitrary"),
    )(*args)


TQ = 256
TB = 512


def _mla_fwd(q, k, v):
    t = q.shape[0]
    nb = t // S

    def body(q_ref, k_ref, v_ref, o_ref, lse_ref):
        causal = lax.broadcasted_iota(jnp.int32, (TB, TB), 0) >= lax.broadcasted_iota(jnp.int32, (TB, TB), 1)
        heads = [slice(h * HP, (h + 1) * HP) for h in range(2)]
        for i in range(S // TB):
            ri, past = slice(i * TB, (i + 1) * TB), slice(0, i * TB)
            qhs = [q_ref[ri, sl] for sl in heads]
            sd = [jnp.where(causal, _dot(qh, k_ref[ri, sl], ((1,), (1,))) * SCALE_B, NEG) for qh, sl in zip(qhs, heads)]
            ms = [jnp.max(s, axis=-1, keepdims=True) for s in sd]
            if i:
                so = [_dot(qh, k_ref[past, sl], ((1,), (1,))) * SCALE_B for qh, sl in zip(qhs, heads)]
                ms = [jnp.maximum(m, jnp.max(s, axis=-1, keepdims=True)) for m, s in zip(ms, so)]
            pd = [jnp.exp(s - m) for s, m in zip(sd, ms)]
            ls = [jnp.sum(p, axis=-1, keepdims=True) for p in pd]
            acc = [_dot(p.astype(BF16), v_ref[ri, sl], ((1,), (0,))) for p, sl in zip(pd, heads)]
            if i:
                po = [jnp.exp(s - m) for s, m in zip(so, ms)]
                ls = [l + jnp.sum(p, axis=-1, keepdims=True) for l, p in zip(ls, po)]
                acc = [a + _dot(p.astype(BF16), v_ref[past, sl], ((1,), (0,))) for a, p, sl in zip(acc, po, heads)]
            o_ref[ri, :] = acc[0] / ls[0] + acc[1] / ls[1]
            for sl, m, l in zip(heads, ms, ls):
                lse_ref[ri, sl] = jnp.broadcast_to(m + jnp.log(l), (TB, HP))

    wide2 = pl.BlockSpec((S, 2 * HP), lambda b, p: (b, p))
    return pl.pallas_call(
        body, name="mla_fwd", grid=(nb, H // 2),
        in_specs=[wide2, wide2, wide2],
        out_specs=[pl.BlockSpec((S, HP), lambda b, p: (b, p)), wide2],
        out_shape=[jax.ShapeDtypeStruct((t, H * VDIM), F32), jax.ShapeDtypeStruct((t, H * HP), F32)],
        compiler_params=_cp("parallel", "parallel"),
    )(q, k, v)


def _mla_bwd(q, k, v, o, do, lse):
    t = q.shape[0]
    nb = t // S
    nq = S // TQ

    def body(q_ref, k_ref, v_ref, o_ref, do_ref, lse_ref, dq_ref, dk_ref, dv_ref):
        lane = lax.broadcasted_iota(jnp.int32, (TB, HP), 1)
        causal = lax.broadcasted_iota(jnp.int32, (TB, TB), 0) >= lax.broadcasted_iota(jnp.int32, (TB, TB), 1)
        heads = [slice(h * HP, (h + 1) * HP) for h in range(2)]
        nblk = S // TB
        for i in reversed(range(nblk)):
            ri, past = slice(i * TB, (i + 1) * TB), slice(0, i * TB)
            dov = do_ref[ri, :]
            prod = dov * o_ref[ri, :]
            dob = dov.astype(BF16)
            deltas = [jnp.sum(jnp.where((lane < VDIM) if h == 0 else (lane >= VDIM), prod, 0.0), axis=-1, keepdims=True)
                      for h in range(2)]
            qhs = [q_ref[ri, sl] for sl in heads]
            lses = [lse_ref[ri, sl][:, :1] for sl in heads]
            for rows, diagonal in ((ri, True), (past, False)):
                if rows.stop == rows.start:
                    continue
                ps = [jnp.exp(_dot(qh, k_ref[rows, sl], ((1,), (1,))) * SCALE_B - lse) for qh, sl, lse in zip(qhs, heads, lses)]
                if diagonal:
                    ps = [jnp.where(causal, p, 0.0) for p in ps]
                dps = [_dot(dob, v_ref[rows, sl], ((1,), (1,))) for sl in heads]
                dss = [(p * (dp - delta) * SCALE_B).astype(BF16) for p, dp, delta in zip(ps, dps, deltas)]
                for sl, qh, p, ds in zip(heads, qhs, ps, dss):
                    dq = _dot(ds, k_ref[rows, sl], ((1,), (0,)))
                    dk = _dot(ds, qh, ((0,), (0,)))
                    dv = _dot(p.astype(BF16), dob, ((0,), (0,)))
                    if diagonal:
                        dq_ref[ri, sl] = dq
                    else:
                        dq_ref[ri, sl] += dq
                    if i == nblk - 1:
                        dk_ref[rows, sl] = dk
                        dv_ref[rows, sl] = dv
                    else:
                        dk_ref[rows, sl] += dk
                        dv_ref[rows, sl] += dv

    wide2 = pl.BlockSpec((S, 2 * HP), lambda b, p: (b, p))
    pair = pl.BlockSpec((S, HP), lambda b, p: (b, p))
    return pl.pallas_call(
        body, name="mla_bwd", grid=(nb, H // 2),
        in_specs=[wide2, wide2, wide2, pair, pair, wide2],
        out_specs=[wide2, wide2, wide2],
        out_shape=[jax.ShapeDtypeStruct((t, H * HP), F32)] * 3,
        compiler_params=_cp("parallel", "parallel"),
    )(q, k, v, o, do, lse)


def _t5_bucket(dist):
    max_exact = N_BUCKETS // 2
    d = np.maximum(dist, 1).astype(np.float64)
    large = max_exact + (np.log(d / max_exact) / np.log(MAX_DISTANCE / max_exact) * (N_BUCKETS - max_exact)).astype(np.int64)
    large = np.minimum(large, N_BUCKETS - 1)
    return np.where(dist < max_exact, dist, large).astype(np.int32)


def _band_geometry():
    a = np.arange(BLK)[:, None]
    bk = np.arange(2 * BLK)[None, :]
    steps = BLK + a - bk
    valid = (steps >= 0) & (steps <= BLK)
    buckets = np.stack([_t5_bucket(np.clip(steps, 0, BLK) * d) for d in DILATIONS])
    return buckets, valid


def _band_bias(rel_bias):
    buckets, valid = _band_geometry()
    onehot = (jnp.asarray(buckets)[..., None] == jnp.arange(N_BUCKETS)).astype(F32)
    bias = jnp.einsum("rqkn,nh->rhqk", onehot, rel_bias, precision=lax.Precision.HIGHEST)
    bias = jnp.where(jnp.asarray(valid)[None, None], bias, NEG)
    return bias.reshape(3, H // 2, 2 * BLK, 2 * BLK)


def _dil_items():
    items = []
    for r, d in enumerate(DILATIONS):
        for res in range(d):
            for blk in range(S // d // BLK):
                items.append((r, d, blk * BLK * d + res, blk > 0))
    return items


GROUP = 4


def _strided(start, d):
    return pl.ds(start, BLK) if d == 1 else pl.ds(start, BLK, stride=d)


def _stack_heads(tile, own):
    return jnp.where(own, jnp.concatenate([tile, tile], axis=0), 0.0).astype(BF16)


def _own_lanes():
    row = lax.broadcasted_iota(jnp.int32, (2 * BLK, HP), 0)
    lane = lax.broadcasted_iota(jnp.int32, (2 * BLK, HP), 1)
    return (lane < E_A) == (row < BLK)


def _dil_fwd(proj, biasm):
    t = proj.shape[0]
    nb = t // S

    def body(q_ref, k_ref, v_ref, b_ref, o_ref, lse_ref, ob_ref, lb_ref):
        lane = lax.broadcasted_iota(jnp.int32, (BLK, HP), 1)
        own = _own_lanes()
        items = _dil_items()
        for g in range(0, len(items), GROUP):
            grp = items[g:g + GROUP]
            ss, vts = [], []
            for r, d, start, has_prev in grp:
                cur = _strided(start, d)
                rows = [_strided(start - BLK * d, d), cur] if has_prev else [cur]
                q2 = _stack_heads(q_ref[cur, :], own)
                kt = jnp.concatenate([k_ref[x, :] for x in rows], axis=0).astype(BF16)
                vts.append(jnp.concatenate([v_ref[x, :] for x in rows], axis=0).astype(BF16))
                bias = b_ref[r, 0] if has_prev else b_ref[r, 0, :, BLK:]
                ss.append(_dot(q2, kt, ((1,), (1,))) * SCALE_A + bias)
            ms = [jnp.max(s, axis=-1, keepdims=True) for s in ss]
            ps = [jnp.exp(s - m) for s, m in zip(ss, ms)]
            ls = [jnp.sum(p, axis=-1, keepdims=True) for p in ps]
            for (r, d, start, _), p, vt, m, l in zip(grp, ps, vts, ms, ls):
                cur = _strided(start, d)
                o2 = _dot(p.astype(BF16), vt, ((1,), (0,))) / l
                lse2 = m + jnp.log(l)
                ob_ref[r, cur, :] = jnp.where(lane < E_A, o2[:BLK], o2[BLK:])
                lb_ref[r, cur, :] = jnp.where(lane < E_A, lse2[:BLK], lse2[BLK:])

        def merge(c, _):
            rows = pl.ds(pl.multiple_of(c * TQ, TQ), TQ)
            l0, l1, l2 = lb_ref[0, rows, :], lb_ref[1, rows, :], lb_ref[2, rows, :]
            m = jnp.maximum(jnp.maximum(l0, l1), l2)
            e0, e1, e2 = jnp.exp(l0 - m), jnp.exp(l1 - m), jnp.exp(l2 - m)
            tot = e0 + e1 + e2
            o_ref[rows, :] = (e0 * ob_ref[0, rows, :] + e1 * ob_ref[1, rows, :] + e2 * ob_ref[2, rows, :]) / tot
            lse_ref[rows, :] = m + jnp.log(tot)
            return 0

        lax.fori_loop(0, S // TQ, merge, 0)

    npair = H // 2
    return pl.pallas_call(
        body, name="dil_fwd", grid=(nb, npair),
        in_specs=[pl.BlockSpec((S, HP), lambda b, p: (b, p)), pl.BlockSpec((S, HP), lambda b, p: (b, npair + p)),
                  pl.BlockSpec((S, HP), lambda b, p: (b, 2 * npair + p)),
                  pl.BlockSpec((3, 1, 2 * BLK, 2 * BLK), lambda b, p: (0, p, 0, 0))],
        out_specs=[pl.BlockSpec((S, HP), lambda b, p: (b, p))] * 2,
        out_shape=[jax.ShapeDtypeStruct((t, D_A), F32)] * 2,
        scratch_shapes=[pltpu.VMEM((3, S, HP), F32), pltpu.VMEM((3, S, HP), F32)],
        compiler_params=_cp("parallel", "parallel"),
    )(proj, proj, proj, biasm)


def _dil_bwd(proj, biasm, o, do, lse):
    t = proj.shape[0]
    nb = t // S

    def body(q_ref, k_ref, v_ref, b_ref, o_ref, do_ref, lse_ref, dq_ref, dk_ref, dv_ref, ds_ref):
        dq_ref[...] = jnp.zeros_like(dq_ref)
        dk_ref[...] = jnp.zeros_like(dk_ref)
        dv_ref[...] = jnp.zeros_like(dv_ref)
        ds_ref[...] = jnp.zeros_like(ds_ref)
        lane = lax.broadcasted_iota(jnp.int32, (BLK, HP), 1)
        own = _own_lanes()
        items = _dil_items()
        for g in range(0, len(items), GROUP):
            grp = items[g:g + GROUP]
            q2s, kts, do2s, ss, dps, lse2s, delta2s = [], [], [], [], [], [], []
            for r, d, start, has_prev in grp:
                cur = _strided(start, d)
                rows = [_strided(start - BLK * d, d), cur] if has_prev else [cur]
                q2 = _stack_heads(q_ref[cur, :], own)
                kt = jnp.concatenate([k_ref[x, :] for x in rows], axis=0).astype(BF16)
                vt = jnp.concatenate([v_ref[x, :] for x in rows], axis=0).astype(BF16)
                dot_ = do_ref[cur, :]
                prod = dot_ * o_ref[cur, :]
                lset = lse_ref[cur, :]
                do2 = _stack_heads(dot_, own)
                bias = b_ref[r, 0] if has_prev else b_ref[r, 0, :, BLK:]
                ss.append(_dot(q2, kt, ((1,), (1,))) * SCALE_A + bias)
                dps.append(_dot(do2, vt, ((1,), (1,))))
                lse2s.append(jnp.concatenate([lset[:, :1], lset[:, E_A:E_A + 1]], axis=0))
                delta2s.append(jnp.concatenate([jnp.sum(jnp.where(lane < E_A, prod, 0.0), axis=-1, keepdims=True),
                                                jnp.sum(jnp.where(lane >= E_A, prod, 0.0), axis=-1, keepdims=True)], axis=0))
                q2s.append(q2)
                kts.append(kt)
                do2s.append(do2)
            ps = [jnp.exp(s - lse2) for s, lse2 in zip(ss, lse2s)]
            dls = [p * (dp - delta2) for p, dp, delta2 in zip(ps, dps, delta2s)]
            for (r, d, start, has_prev), q2, kt, do2, p, dl in zip(grp, q2s, kts, do2s, ps, dls):
                cur = _strided(start, d)
                dsb = (dl * SCALE_A).astype(BF16)
                dq2 = _dot(dsb, kt, ((1,), (0,)))
                dkt = _dot(dsb, q2, ((0,), (0,)))
                dvt = _dot(p.astype(BF16), do2, ((0,), (0,)))
                dq_ref[cur, :] += jnp.where(lane < E_A, dq2[:BLK], dq2[BLK:])
                if has_prev:
                    prev = _strided(start - BLK * d, d)
                    ds_ref[0, r, 0] += dl
                    dk_ref[prev, :] += dkt[:BLK]
                    dv_ref[prev, :] += dvt[:BLK]
                    dk_ref[cur, :] += dkt[BLK:]
                    dv_ref[cur, :] += dvt[BLK:]
                else:
                    ds_ref[0, r, 0, :, BLK:] += dl
                    dk_ref[cur, :] += dkt
                    dv_ref[cur, :] += dvt

    npair = H // 2
    pair = pl.BlockSpec((S, HP), lambda b, p: (b, p))
    return pl.pallas_call(
        body, name="dil_bwd", grid=(nb, npair),
        in_specs=[pair, pl.BlockSpec((S, HP), lambda b, p: (b, npair + p)),
                  pl.BlockSpec((S, HP), lambda b, p: (b, 2 * npair + p)),
                  pl.BlockSpec((3, 1, 2 * BLK, 2 * BLK), lambda b, p: (0, p, 0, 0)), pair, pair, pair],
        out_specs=[pair, pair, pair, pl.BlockSpec((1, 3, 1, 2 * BLK, 2 * BLK), lambda b, p: (b, 0, p, 0, 0))],
        out_shape=[jax.ShapeDtypeStruct((t, D_A), F32)] * 3 + [jax.ShapeDtypeStruct((nb, 3, npair, 2 * BLK, 2 * BLK), F32)],
        compiler_params=_cp("parallel", "parallel"),
    )(proj, proj, proj, biasm, o, do, lse)


def _rel_bias_grad(dlogits):
    nb = dlogits.shape[0]
    buckets, _ = _band_geometry()
    kk = 3 * BLK * 2 * BLK
    dl = jnp.transpose(dlogits.reshape(nb, 3, H, BLK, 2 * BLK), (0, 2, 1, 3, 4)).reshape(nb, H, kk)
    bk = jnp.asarray(buckets.reshape(1, kk))
    tk = kk // 12

    def body(dl_ref, bk_ref, o_ref):
        j = pl.program_id(0)
        onehot = (bk_ref[...] == lax.broadcasted_iota(jnp.int32, (N_BUCKETS, tk), 0)).astype(F32)
        tot = dl_ref[0]
        for b in range(1, nb):
            tot = tot + dl_ref[b]
        part = lax.dot_general(onehot, tot, ((((1,), (1,))), ((), ())), preferred_element_type=F32,
                               precision=lax.Precision.HIGHEST)
        _acc_first(j, o_ref, part)

    return pl.pallas_call(
        body, name="rel_bias_grad", grid=(kk // tk,),
        in_specs=[pl.BlockSpec((nb, H, tk), lambda j: (0, 0, j)), pl.BlockSpec((1, tk), lambda j: (0, j))],
        out_specs=pl.BlockSpec((N_BUCKETS, H), lambda j: (0, 0)),
        out_shape=jax.ShapeDtypeStruct((N_BUCKETS, H), F32),
        compiler_params=_cp("arbitrary"),
    )(dl, bk)


def _mesh_place():
    x, y, c = lax.axis_index("x"), lax.axis_index("y"), lax.axis_index("c")
    return x, y, c


def _peer(k):
    x, y, c = _mesh_place()
    px = 1 - x if k & 4 else x
    py = 1 - y if k & 2 else y
    pc = 1 - c if k & 1 else c
    return (px, py, pc), 4 * px + 2 * py + pc


ANY = pl.BlockSpec(memory_space=pl.ANY)


def _all_gather(arrays, name):
    n_arr = len(arrays)

    def body(*refs):
        ins, outs = refs[:n_arr], refs[n_arr:2 * n_arr]
        send, recv, loc = refs[2 * n_arr:]
        x, y, c = _mesh_place()
        me = 4 * x + 2 * y + c
        local, remote = [], []
        for a in range(n_arr):
            cp = pltpu.make_async_copy(ins[a], outs[a].at[me], loc.at[a])
            cp.start()
            local.append(cp)
            for k in range(1, N_DEV):
                dev, idx = _peer(k)
                put = pltpu.make_async_remote_copy(ins[a], outs[a].at[me], send.at[a * (N_DEV - 1) + k - 1], recv.at[a * (N_DEV - 1) + k - 1],
                                                   device_id=dev, device_id_type=pl.DeviceIdType.MESH)
                put.start()
                got = pltpu.make_async_remote_copy(ins[a], outs[a].at[idx], send.at[a * (N_DEV - 1) + k - 1], recv.at[a * (N_DEV - 1) + k - 1],
                                                   device_id=dev, device_id_type=pl.DeviceIdType.MESH)
                remote.append((put, got))
        for cp in local:
            cp.wait()
        for put, got in remote:
            put.wait_send()
            got.wait_recv()

    return pl.pallas_call(
        body, name=name,
        in_specs=[ANY] * n_arr, out_specs=[ANY] * n_arr,
        out_shape=[jax.ShapeDtypeStruct((N_DEV,) + a.shape, a.dtype) for a in arrays],
        scratch_shapes=[pltpu.SemaphoreType.DMA((n_arr * (N_DEV - 1),)), pltpu.SemaphoreType.DMA((n_arr * (N_DEV - 1),)),
                        pltpu.SemaphoreType.DMA((n_arr,))],
        compiler_params=pltpu.CompilerParams(has_side_effects=True),
    )(*arrays)


def _all_to_all(arrays, name):
    n_arr = len(arrays)

    def body(*refs):
        ins, outs = refs[:n_arr], refs[n_arr:2 * n_arr]
        send, recv, loc = refs[2 * n_arr:]
        x, y, c = _mesh_place()
        me = 4 * x + 2 * y + c
        local, remote = [], []
        for a in range(n_arr):
            cp = pltpu.make_async_copy(ins[a].at[me], outs[a].at[me], loc.at[a])
            cp.start()
            local.append(cp)
            for k in range(1, N_DEV):
                dev, idx = _peer(k)
                put = pltpu.make_async_remote_copy(ins[a].at[idx], outs[a].at[me], send.at[a * (N_DEV - 1) + k - 1], recv.at[a * (N_DEV - 1) + k - 1],
                                                   device_id=dev, device_id_type=pl.DeviceIdType.MESH)
                put.start()
                got = pltpu.make_async_remote_copy(ins[a].at[idx], outs[a].at[idx], send.at[a * (N_DEV - 1) + k - 1], recv.at[a * (N_DEV - 1) + k - 1],
                                                   device_id=dev, device_id_type=pl.DeviceIdType.MESH)
                remote.append((put, got))
        for cp in local:
            cp.wait()
        for put, got in remote:
            put.wait_send()
            got.wait_recv()

    return pl.pallas_call(
        body, name=name,
        in_specs=[ANY] * n_arr, out_specs=[ANY] * n_arr,
        out_shape=[jax.ShapeDtypeStruct(a.shape, a.dtype) for a in arrays],
        scratch_shapes=[pltpu.SemaphoreType.DMA((n_arr * (N_DEV - 1),)), pltpu.SemaphoreType.DMA((n_arr * (N_DEV - 1),)),
                        pltpu.SemaphoreType.DMA((n_arr,))],
        compiler_params=pltpu.CompilerParams(has_side_effects=True),
    )(*arrays)


def _silu_rows(c):
    def body(c_ref, o_ref):
        v = c_ref[...]
        o_ref[...] = v * _sigmoid(v)

    return pl.pallas_call(body, name="cond", out_shape=jax.ShapeDtypeStruct(c.shape, F32))(c)


def _mod_slab(cond_all, w_ada, b_slab):
    def body(c_ref, w_ref, b_ref, o_ref):
        o_ref[...] = _dot(c_ref[...].astype(BF16), w_ref[...].astype(BF16), ((1,), (0,))) + b_ref[...]

    return pl.pallas_call(body, name="mod_slab",
                          out_shape=jax.ShapeDtypeStruct((cond_all.shape[0], w_ada.shape[1]), F32),
                          compiler_params=pltpu.CompilerParams(vmem_limit_bytes=VMEM_LIMIT))(cond_all, w_ada, b_slab)


def _ada_grad(cond_all, dmod_cols):
    def body(c_ref, d_ref, o_ref):
        o_ref[...] = _dot(c_ref[...].astype(BF16), d_ref[...].astype(BF16), ((0,), (0,)))

    return pl.pallas_call(body, name="ada_grad",
                          out_shape=jax.ShapeDtypeStruct((cond_all.shape[1], dmod_cols.shape[1]), F32),
                          compiler_params=pltpu.CompilerParams(vmem_limit_bytes=VMEM_LIMIT))(cond_all, dmod_cols)


def _adam_math(g, w, m, v):
    m2 = B1 * m + (1.0 - B1) * g
    v2 = B2 * v + (1.0 - B2) * (g * g)
    m_hat = m2 / (1.0 - B1 ** STEP)
    v_hat = v2 / (1.0 - B2 ** STEP)
    return -LR * (m_hat / (jnp.sqrt(v_hat) + ADAM_EPS) + WD * w), m2, v2


def _adamw(parts, w, m, v, name):
    n, rows, cols = parts.shape
    tr = _pick(rows, (128, 96, 64, 32, 16, 8))

    def body(p_ref, w_ref, m_ref, v_ref, g_ref, d_ref, m2_ref, v2_ref):
        g = p_ref[0].astype(F32)
        for s in range(1, n):
            g = g + p_ref[s].astype(F32)
        g_ref[...] = g
        d_ref[...], m2_ref[...], v2_ref[...] = _adam_math(g, w_ref[...], m_ref[...], v_ref[...])

    blk = pl.BlockSpec((tr, cols), lambda i: (i, 0))
    return pl.pallas_call(
        body, name=name, grid=(rows // tr,),
        in_specs=[pl.BlockSpec((n, tr, cols), lambda i: (0, i, 0)), blk, blk, blk],
        out_specs=[blk] * 4, out_shape=[jax.ShapeDtypeStruct((rows, cols), F32)] * 4,
        compiler_params=_cp("parallel"),
    )(parts, w, m, v)


SMALL = (("b_ada", N_MOD * D), ("g_norm1", D), ("g_cq", Q_LORA), ("g_ckv", KV_LORA), ("rel_bias", N_BUCKETS * H),
         ("g_out_a", D_A), ("g_out_b", D_A), ("g_norm2", D), ("g_final", D))
TILE = 8 * 128


def _tiles(n):
    return -(-n // TILE) * 8


SMALL_ROWS = sum(_tiles(n) for _, n in SMALL)
LOSS_ROWS = 8
PACK_ROWS = SMALL_ROWS + LOSS_ROWS


def _pack(vals):
    parts = []
    for v in vals:
        v = v.reshape(-1)
        rows = _tiles(v.shape[0])
        parts.append(jnp.pad(v, (0, rows * 128 - v.shape[0])).reshape(rows, 128))
    return jnp.concatenate(parts, axis=0)


def _unpack(packed, shapes):
    out, r = [], 0
    for (_, n), shp in zip(SMALL, shapes):
        rows = _tiles(n)
        out.append(packed[r:r + rows].reshape(-1)[:n].reshape(shp))
        r += rows
    return out


def _small_update(parts, w, m, v):
    def body(p_ref, w_ref, m_ref, v_ref, g_ref, d_ref, m2_ref, v2_ref, loss_ref):
        tot = p_ref[0]
        for s in range(1, N_DEV):
            tot = tot + p_ref[s]
        g = tot[:SMALL_ROWS]
        g_ref[...] = g
        d_ref[...], m2_ref[...], v2_ref[...] = _adam_math(g, w_ref[...], m_ref[...], v_ref[...])
        loss_ref[...] = jnp.broadcast_to((0.5 / D) * jnp.sum(tot[SMALL_ROWS:]), loss_ref.shape)

    return pl.pallas_call(
        body, name="small_update",
        out_shape=[jax.ShapeDtypeStruct((SMALL_ROWS, 128), F32)] * 4 + [jax.ShapeDtypeStruct((8, 128), F32)],
    )(parts, w, m, v)


def _cols_from_blocks(g):
    return jnp.transpose(g, (1, 0, 2)).reshape(g.shape[1], N_DEV * g.shape[2])


def _cols_to_blocks(w):
    r, c = w.shape
    return jnp.transpose(w.reshape(r, N_DEV, c // N_DEV), (1, 0, 2))


def _pad_w_in(w):
    z = jnp.zeros((w.shape[0], NOPE), w.dtype)
    return jnp.concatenate([w[:, :P_IN - ROPE], z, w[:, P_IN - ROPE:], z[:, :HP - NOPE - ROPE]], axis=1)


def _unpad_w_in(g):
    k0 = P_IN - ROPE + NOPE
    return jnp.concatenate([g[:, :P_IN - ROPE], g[:, k0:k0 + ROPE]], axis=1)


def _pad_w_uq(w):
    w3 = w.reshape(Q_LORA, H, NOPE + ROPE)
    return jnp.pad(w3, ((0, 0), (0, 0), (0, HP - NOPE - ROPE))).reshape(Q_LORA, H * HP)


def _unpad_w_uq(g):
    return g.reshape(Q_LORA, H, HP)[:, :, :NOPE + ROPE].reshape(Q_LORA, H * (NOPE + ROPE))


def _split_w_ukv(w):
    w4 = w.reshape(KV_LORA, H // 2, 2, HP)
    z = jnp.zeros((KV_LORA, H // 2, NOPE), w.dtype)
    kn, vv = w4[..., :NOPE], w4[..., NOPE:]
    w_k = jnp.stack([jnp.concatenate([kn[:, :, 0], z], -1), jnp.concatenate([kn[:, :, 1], z], -1)], axis=2)
    w_v = jnp.stack([jnp.concatenate([vv[:, :, 0], z], -1), jnp.concatenate([z, vv[:, :, 1]], -1)], axis=2)
    return w_k.reshape(KV_LORA, H * HP), w_v.reshape(KV_LORA, H * HP)


def _join_w_ukv(g_k, g_v):
    gk = g_k.reshape(KV_LORA, H // 2, 2, HP)
    gv = g_v.reshape(KV_LORA, H // 2, 2, HP)
    even = jnp.concatenate([gk[:, :, 0, :NOPE], gv[:, :, 0, :VDIM]], -1)
    odd = jnp.concatenate([gk[:, :, 1, :NOPE], gv[:, :, 1, VDIM:]], -1)
    return jnp.stack([even, odd], axis=2).reshape(KV_LORA, H * HP)


def _rope_tables():
    half = ROPE // 2
    inv = ROPE_THETA ** (-jnp.arange(half, dtype=F32) / half)
    ang = jnp.arange(S, dtype=F32)[:, None] * inv[None, :]
    cos, sin = jnp.cos(ang), jnp.sin(ang)
    ones, zeros = jnp.ones((S, NOPE), F32), jnp.zeros((S, NOPE), F32)
    tail1, tail0 = jnp.ones((S, HP - NOPE - ROPE), F32), jnp.zeros((S, HP - NOPE - ROPE), F32)
    zh = jnp.zeros((S, half), F32)
    c = jnp.concatenate([ones, cos, cos, tail1], axis=1)
    sm = jnp.concatenate([zeros, -sin, zh, tail0], axis=1)
    sp = jnp.concatenate([zeros, zh, sin, tail0], axis=1)
    return c, sm, sp


def _local_step(x, mod, target, g_norm1, w_in_p, g_cq, w_uq_p, g_ckv, w_k, w_v, rel_bias, g_out_a, g_out_b, w_out,
                g_norm2, w_ffn_in, w_ffn_out, g_final):
    nb = x.shape[0] // S
    sh1, sc1, g1, sh2, sc2, g2 = (mod[:, n].reshape(nb, 1, D) for n in range(N_MOD))
    rc, rsm, rsp = _rope_tables()
    biasm = _band_bias(rel_bias)

    h1 = _pre1(x, g_norm1, sc1, sh1)
    proj = _mm_nn(h1, w_in_p, F32, "proj")
    q, k, v, cqn, ckvn = _mla_pre(proj, g_cq, g_ckv, w_uq_p, w_k, w_v, rc, rsm, rsp)
    out_b, lse_b = _mla_fwd(q, k, v)
    out_a, lse_a = _dil_fwd(proj, biasm)
    y = _post_attn(out_a, out_b, g_out_a, g_out_b)
    mix = _mm_nn(y, w_out, F32, "mix")
    x2, h2 = _resid_norm2(x, mix, g1, g_norm2, sc2, sh2)
    gu = _mm_nn(h2, w_ffn_in, BF16, "ffn_in")
    act = _swiglu(gu)
    f = _mm_nn(act, w_ffn_out, F32, "ffn_out")
    dx3, df, loss_cols, dg_final, dg2 = _final(x2, f, g2, g_final, target)

    dact = _mm_nt(df, w_ffn_out, BF16, "d_act")
    gw_ffn_out = _mm_tn(act, df, "gw_ffn_out")
    dgu = _swiglu_bwd(gu, dact)
    dh2 = _mm_nt(dgu, w_ffn_in, F32, "d_h2")
    gw_ffn_in = _mm_tn(h2, dgu, "gw_ffn_in")
    dx2, dsh2, dsc2, dg_norm2, dg1, dmix = _norm_bwd(x2, dh2, dx3, g_norm2, sc2, gate=(mix, g1))
    dy = _mm_nt(dmix, w_out, F32, "d_y")
    gw_out = _mm_tn(y, dmix, "gw_out")
    dout_a, dout_b, dg_out_a, dg_out_b = _post_attn_bwd(dy, out_a, out_b, g_out_a, g_out_b)
    dq_b, dk_b, dv_b = _mla_bwd(q, k, v, out_b, dout_b, lse_b)
    dq_a, dk_a, dv_a, dlogits = _dil_bwd(proj, biasm, out_a, dout_a, lse_a)
    g_rel = _rel_bias_grad(dlogits)
    dqr, dkr, dvr, dtail, dg_cq, dg_ckv = _mla_pre_bwd(proj, dq_b, dk_b, dv_b, g_cq, g_ckv, w_uq_p, w_k, w_v, rc, rsm, rsp)
    gw_uq = _mm_tn(cqn, dqr, "gw_uq")
    gw_k = _mm_tn(ckvn, dkr, "gw_k")
    gw_v = _mm_tn(ckvn, dvr, "gw_v")
    dproj = jnp.concatenate([dq_a.astype(BF16), dk_a.astype(BF16), dv_a.astype(BF16), dtail], axis=1)
    dh1 = _mm_nt(dproj, w_in_p, F32, "d_h1")
    gw_in = _mm_tn(h1, dproj, "gw_in")
    grad_x, dsh1, dsc1, dg_norm1 = _norm_bwd(x, dh1, dx2, g_norm1, sc1)

    dmod = jnp.concatenate([dsh1, dsc1, dg1, dsh2, dsc2, dg2], axis=1)
    small = dict(g_norm1=dg_norm1, g_cq=dg_cq, g_ckv=dg_ckv, rel_bias=g_rel, g_out_a=dg_out_a, g_out_b=dg_out_b,
                 g_norm2=dg_norm2, g_final=dg_final)
    big = dict(w_in=gw_in, w_uq=gw_uq, w_k=gw_k, w_v=gw_v, w_out=gw_out, w_ffn_in=gw_ffn_in, w_ffn_out=gw_ffn_out)
    return grad_x, dmod, loss_cols, small, big


def kernel(x, c, w_ada, b_ada, g_norm1, w_in, g_cq, w_uq, g_ckv, w_ukv, rel_bias, g_out_a, g_out_b, w_out, g_norm2, w_ffn_in, w_ffn_out, g_final, loss_target, m_w_ada, m_b_ada, m_g_norm1, m_w_in, m_g_cq, m_w_uq, m_g_ckv, m_w_ukv, m_rel_bias, m_g_out_a, m_g_out_b, m_w_out, m_g_norm2, m_w_ffn_in, m_w_ffn_out, m_g_final, v_w_ada, v_b_ada, v_g_norm1, v_w_in, v_g_cq, v_w_uq, v_g_ckv, v_w_ukv, v_rel_bias, v_g_out_a, v_g_out_b, v_w_out, v_g_norm2, v_w_ffn_in, v_w_ffn_out, v_g_final):
    nb = x.shape[0]
    t = nb * S
    xt, tt = x.reshape(t, D), loss_target.reshape(t, D)
    me = 4 * lax.axis_index("x") + 2 * lax.axis_index("y") + lax.axis_index("c")

    shards = [w_in[0], w_uq[0], w_ukv[0], w_out[0], w_ffn_in[0], w_ffn_out[0]]
    gathered = _all_gather([_silu_rows(c)] + [s.astype(BF16) for s in shards], "gather_weights")
    cond_all = gathered[0].reshape(N_DEV * nb, D)
    w_in_f, w_uq_f, w_ukv_f = (_cols_from_blocks(g) for g in gathered[1:4])
    w_out_f = gathered[4].reshape(D, D)
    w_ffn_in_f = _cols_from_blocks(gathered[5])
    w_ffn_out_f = gathered[6].reshape(D_FF, D)
    w_k, w_v = _split_w_ukv(w_ukv_f)

    ncol = N_MOD * D // N_DEV
    b_slab = lax.dynamic_slice(b_ada, (0, me * ncol), (1, ncol))
    slab = _mod_slab(cond_all, w_ada[0], b_slab)
    (mod_rows,) = _all_to_all([slab.reshape(N_DEV, nb, ncol)], "scatter_mod")
    mod = jnp.transpose(mod_rows, (1, 0, 2)).reshape(nb, N_MOD, D)

    grad_x, dmod, loss_cols, small, big = _local_step(
        xt, mod, tt, g_norm1, _pad_w_in(w_in_f), g_cq, _pad_w_uq(w_uq_f), g_ckv, w_k, w_v, rel_bias, g_out_a, g_out_b,
        w_out_f, g_norm2, w_ffn_in_f, w_ffn_out_f, g_final.reshape(1, D))

    slabs = [
        _cols_to_blocks(_unpad_w_in(big["w_in"])), _cols_to_blocks(_unpad_w_uq(big["w_uq"])),
        _cols_to_blocks(_join_w_ukv(big["w_k"], big["w_v"])), big["w_out"].reshape(N_DEV, D // N_DEV, D),
        _cols_to_blocks(big["w_ffn_in"]), big["w_ffn_out"].reshape(N_DEV, D_FF // N_DEV, D),
    ]
    dmod_blocks = jnp.transpose(dmod.reshape(nb, N_DEV, ncol), (1, 0, 2))
    got = _all_to_all([dmod_blocks] + [s.astype(BF16) for s in slabs], "exchange_grads")
    g_ada = _ada_grad(cond_all, got[0].reshape(N_DEV * nb, ncol))

    mine = _pack([jnp.sum(dmod, axis=0)] + [small[n] for n, _ in SMALL[1:]] + [loss_cols])
    (parts,) = _all_gather([mine], "gather_small")
    small_w = [b_ada, g_norm1, g_cq, g_ckv, rel_bias, g_out_a, g_out_b, g_norm2, g_final]
    small_m = [m_b_ada, m_g_norm1, m_g_cq, m_g_ckv, m_rel_bias, m_g_out_a, m_g_out_b, m_g_norm2, m_g_final]
    small_v = [v_b_ada, v_g_norm1, v_g_cq, v_g_ckv, v_rel_bias, v_g_out_a, v_g_out_b, v_g_norm2, v_g_final]
    sg, sd, sm, sv, loss8 = _small_update(parts, _pack(small_w), _pack(small_m), _pack(small_v))
    shapes = [w.shape for w in small_w]
    sg, sd, sm, sv = (_unpack(p, shapes) for p in (sg, sd, sm, sv))

    big_w = [w_in, w_uq, w_ukv, w_out, w_ffn_in, w_ffn_out]
    big_m = [m_w_in, m_w_uq, m_w_ukv, m_w_out, m_w_ffn_in, m_w_ffn_out]
    big_v = [v_w_in, v_w_uq, v_w_ukv, v_w_out, v_w_ffn_in, v_w_ffn_out]
    names = ["w_in", "w_uq", "w_ukv", "w_out", "w_ffn_in", "w_ffn_out"]
    upd = {"w_ada": _adamw(g_ada[None], w_ada[0], m_w_ada[0], v_w_ada[0], "adamw_w_ada")}
    for n, p, w, m, v in zip(names, got[1:], big_w, big_m, big_v):
        upd[n] = _adamw(p, w[0], m[0], v[0], "adamw_" + n)
    for i, (n, _) in enumerate(SMALL):
        upd[n] = (sg[i], sd[i], sm[i], sv[i])

    order = ["w_ada", "b_ada", "g_norm1", "w_in", "g_cq", "w_uq", "g_ckv", "w_ukv", "rel_bias", "g_out_a", "g_out_b",
             "w_out", "g_norm2", "w_ffn_in", "w_ffn_out", "g_final"]
    like = dict(w_ada=w_ada, w_in=w_in, w_uq=w_uq, w_ukv=w_ukv, w_out=w_out, w_ffn_in=w_ffn_in, w_ffn_out=w_ffn_out)
    outs = [loss8[0, 0], grad_x.reshape(x.shape)]
    for part in range(4):
        for n in order:
            val = upd[n][part]
            outs.append(val.reshape(like[n].shape) if n in like else val)
    return tuple(outs)
```

```python
import functools

import numpy as np
import jax
import jax.numpy as jnp
from jax import lax
from jax.experimental import pallas as pl
from jax.experimental.pallas import tpu as pltpu

F32, BF16 = jnp.float32, jnp.bfloat16

N_DEV = 8
D = 1024
S = 2048
H = 8
E_A = 64
D_A = H * E_A
Q_LORA, KV_LORA = 384, 256
NOPE, ROPE, VDIM = 64, 32, 64
HP = 128
P_IN = 3 * D_A + Q_LORA + KV_LORA + ROPE
P_PAD = 3 * D_A + Q_LORA + KV_LORA + HP
TAIL0 = 3 * D_A
TAIL = P_PAD - TAIL0
D_FF = 2816
N_MOD = 6
EPS = 1e-6
NEG = -1e30
BLK = 128
DILATIONS = (1, 4, 16)
N_BUCKETS, MAX_DISTANCE = 32, 2048
ROPE_THETA = 10000.0
SCALE_A = E_A ** -0.5
SCALE_B = (NOPE + ROPE) ** -0.5
B1, B2, LR, ADAM_EPS, WD, STEP = 0.9, 0.999, 0.001, 1e-8, 0.01, 10
VMEM_LIMIT = 56 * 1024 * 1024


def _cp(*sem):
    return pltpu.CompilerParams(dimension_semantics=sem, vmem_limit_bytes=VMEM_LIMIT)


def _pick(n, prefs):
    for p in prefs:
        if n % p == 0:
            return p
    raise ValueError(f"no tile of {prefs} divides {n}")


OPERAND_BYTES = 6 * 1024 * 1024


def _pick_rows(m, k):
    return _pick(m, [p for p in (1024, 512, 256, 128, 16) if p * k * 2 <= OPERAND_BYTES])


def _dot(a, b, dims):
    return lax.dot_general(a, b, (dims, ((), ())), preferred_element_type=F32)


def _mm_nn(a, b, out_dtype, name):
    m, k = a.shape
    n = b.shape[1]
    tm, tn = _pick_rows(m, k), _pick(n, (512, 384, 256, 128))

    def body(a_ref, b_ref, o_ref):
        o_ref[...] = _dot(a_ref[...], b_ref[...], ((1,), (0,))).astype(o_ref.dtype)

    return pl.pallas_call(
        body, name=name, grid=(m // tm, n // tn),
        in_specs=[pl.BlockSpec((tm, k), lambda i, j: (i, 0)), pl.BlockSpec((k, tn), lambda i, j: (0, j))],
        out_specs=pl.BlockSpec((tm, tn), lambda i, j: (i, j)),
        out_shape=jax.ShapeDtypeStruct((m, n), out_dtype),
        compiler_params=_cp("parallel", "parallel"),
    )(a, b)


def _mm_nt(a, b, out_dtype, name):
    m, k = a.shape
    n = b.shape[0]
    tm, tn = _pick_rows(m, k), _pick(n, (512, 384, 256, 128))

    def body(a_ref, b_ref, o_ref):
        o_ref[...] = _dot(a_ref[...], b_ref[...], ((1,), (1,))).astype(o_ref.dtype)

    return pl.pallas_call(
        body, name=name, grid=(m // tm, n // tn),
        in_specs=[pl.BlockSpec((tm, k), lambda i, j: (i, 0)), pl.BlockSpec((tn, k), lambda i, j: (j, 0))],
        out_specs=pl.BlockSpec((tm, tn), lambda i, j: (i, j)),
        out_shape=jax.ShapeDtypeStruct((m, n), out_dtype),
        compiler_params=_cp("parallel", "parallel"),
    )(a, b)


def _mm_tn(a, b, name):
    t, m = a.shape
    n = b.shape[1]
    tm, tn, tc = _pick(m, (512, 384, 256, 128)), _pick(n, (512, 384, 256, 128)), _pick(t, (512, 16))

    def body(a_ref, b_ref, o_ref, at_ref):
        @pl.when(pl.program_id(1) == 0)
        def _():
            def chunk(c, _):
                rows = pl.ds(pl.multiple_of(c * tc, tc), tc)
                at_ref[:, rows] = a_ref[rows, :].T
                return 0

            lax.fori_loop(0, t // tc, chunk, 0)

        o_ref[...] = _dot(at_ref[...], b_ref[...], ((1,), (0,)))

    return pl.pallas_call(
        body, name=name, grid=(m // tm, n // tn),
        in_specs=[pl.BlockSpec((t, tm), lambda i, j: (0, i)), pl.BlockSpec((t, tn), lambda i, j: (0, j))],
        out_specs=pl.BlockSpec((tm, tn), lambda i, j: (i, j)),
        out_shape=jax.ShapeDtypeStruct((m, n), F32),
        scratch_shapes=[pltpu.VMEM((tm, t), BF16)],
        compiler_params=_cp("parallel", "arbitrary"),
    )(a, b)


TM = 256


def _row(w):
    return pl.BlockSpec((TM, w), lambda i: (i, 0))


def _row_at(w, col):
    return pl.BlockSpec((TM, w), lambda i: (i, col))


def _vec(w):
    return pl.BlockSpec((1, w), lambda i: (0, 0))


def _per_ex(w):
    return pl.BlockSpec((1, 1, w), lambda i: (i // (S // TM), 0, 0))


def _pos(w):
    return pl.BlockSpec((TM, w), lambda i: (i % (S // TM), 0))


def _full(shape):
    return pl.BlockSpec(shape, lambda i: (0,) * len(shape))


def _rms(x):
    return lax.rsqrt(jnp.mean(x * x, axis=-1, keepdims=True) + EPS)


def _rms_bwd(n, r, dn):
    return r * (dn - n * jnp.mean(dn * n, axis=-1, keepdims=True))


def _colsum(v):
    return jnp.sum(v, axis=0, keepdims=True)


def _acc_first(i, ref, val, every=None):
    first = (i == 0) if every is None else (i % every == 0)

    @pl.when(first)
    def _():
        ref[...] = jnp.zeros_like(ref)

    ref[...] += val.reshape(ref.shape)


def _pre1(x, g, sc, sh):
    t = x.shape[0]

    def body(x_ref, g_ref, sc_ref, sh_ref, h_ref):
        xv = x_ref[...]
        n = xv * _rms(xv)
        h_ref[...] = ((n * g_ref[...]) * (1.0 + sc_ref[0]) + sh_ref[0]).astype(BF16)

    return pl.pallas_call(
        body, name="pre1", grid=(t // TM,),
        in_specs=[_row(D), _vec(D), _per_ex(D), _per_ex(D)],
        out_specs=_row(D), out_shape=jax.ShapeDtypeStruct((t, D), BF16),
        compiler_params=_cp("parallel"),
    )(x, g, sc, sh)


def _rope_fwd(v, c, sm, sp):
    return v * c + pltpu.roll(v, HP - ROPE // 2, 1) * sm + pltpu.roll(v, ROPE // 2, 1) * sp


def _rope_bwd(dv, c, sm, sp):
    return dv * c + pltpu.roll(dv * sm, ROPE // 2, 1) + pltpu.roll(dv * sp, HP - ROPE // 2, 1)


def _mla_pre(proj, g_cq, g_ckv, w_uq, w_k, w_v, rc, rsm, rsp):
    t = proj.shape[0]

    def body(tail_ref, gq_ref, gkv_ref, wuq_ref, wk_ref, wv_ref, c_ref, sm_ref, sp_ref,
             q_ref, k_ref, v_ref, cqn_ref, ckvn_ref):
        tail = tail_ref[...]
        cq, ckv, kr = tail[:, :Q_LORA], tail[:, Q_LORA:Q_LORA + KV_LORA], tail[:, Q_LORA + KV_LORA:]
        cqn = (cq * _rms(cq) * gq_ref[...]).astype(BF16)
        ckvn = (ckv * _rms(ckv) * gkv_ref[...]).astype(BF16)
        cqn_ref[...] = cqn
        ckvn_ref[...] = ckvn
        c, sm, sp = c_ref[...], sm_ref[...], sp_ref[...]
        q = _dot(cqn, wuq_ref[...], ((1,), (0,)))
        kn = _dot(ckvn, wk_ref[...], ((1,), (0,)))
        v_ref[...] = _dot(ckvn, wv_ref[...], ((1,), (0,))).astype(BF16)
        krr = _rope_fwd(kr, c, sm, sp)
        for h in range(H):
            sl = slice(h * HP, (h + 1) * HP)
            q_ref[:, sl] = _rope_fwd(q[:, sl], c, sm, sp).astype(BF16)
            k_ref[:, sl] = (kn[:, sl] + krr).astype(BF16)

    wide = H * HP
    return pl.pallas_call(
        body, name="mla_pre", grid=(t // TM,),
        in_specs=[_row_at(TAIL, TAIL0 // TAIL), _vec(Q_LORA), _vec(KV_LORA), _full((Q_LORA, wide)),
                  _full((KV_LORA, wide)), _full((KV_LORA, wide)), _pos(HP), _pos(HP), _pos(HP)],
        out_specs=[_row(wide), _row(wide), _row(wide), _row(Q_LORA), _row(KV_LORA)],
        out_shape=[jax.ShapeDtypeStruct((t, wide), BF16)] * 3
        + [jax.ShapeDtypeStruct((t, Q_LORA), BF16), jax.ShapeDtypeStruct((t, KV_LORA), BF16)],
        compiler_params=_cp("parallel"),
    )(proj, g_cq, g_ckv, w_uq, w_k, w_v, rc, rsm, rsp)


def _mla_pre_bwd(proj, dq_, dk_, dv_, g_cq, g_ckv, w_uq, w_k, w_v, rc, rsm, rsp):
    t = proj.shape[0]
    wide = H * HP

    def body(tail_ref, dq_ref, dk_ref, dv_ref, gq_ref, gkv_ref, wuq_ref, wk_ref, wv_ref, c_ref, sm_ref, sp_ref,
             dqo_ref, dko_ref, dvo_ref, dtail_ref, dgq_ref, dgkv_ref):
        i = pl.program_id(0)
        tail = tail_ref[...]
        cq, ckv = tail[:, :Q_LORA], tail[:, Q_LORA:Q_LORA + KV_LORA]
        c, sm, sp = c_ref[...], sm_ref[...], sp_ref[...]
        dkr = jnp.zeros((TM, HP), F32)
        for h in range(H):
            sl = slice(h * HP, (h + 1) * HP)
            dqo_ref[:, sl] = _rope_bwd(dq_ref[:, sl], c, sm, sp).astype(BF16)
            dkr = dkr + dk_ref[:, sl]
        lane = lax.broadcasted_iota(jnp.int32, (TM, HP), 1)
        dkr = jnp.where((lane >= NOPE) & (lane < NOPE + ROPE), _rope_bwd(dkr, c, sm, sp), 0.0)
        dkb = dk_ref[...].astype(BF16)
        dvb = dv_ref[...].astype(BF16)
        dko_ref[...] = dkb
        dvo_ref[...] = dvb
        dcqn = _dot(dqo_ref[...], wuq_ref[...], ((1,), (1,)))
        dckvn = _dot(dkb, wk_ref[...], ((1,), (1,))) + _dot(dvb, wv_ref[...], ((1,), (1,)))
        rq, rkv = _rms(cq), _rms(ckv)
        nq, nkv = cq * rq, ckv * rkv
        _acc_first(i, dgq_ref, _colsum(dcqn * nq))
        _acc_first(i, dgkv_ref, _colsum(dckvn * nkv))
        dtail_ref[:, :Q_LORA] = _rms_bwd(nq, rq, dcqn * gq_ref[...]).astype(BF16)
        dtail_ref[:, Q_LORA:Q_LORA + KV_LORA] = _rms_bwd(nkv, rkv, dckvn * gkv_ref[...]).astype(BF16)
        dtail_ref[:, Q_LORA + KV_LORA:] = dkr.astype(BF16)

    return pl.pallas_call(
        body, name="mla_pre_bwd", grid=(t // TM,),
        in_specs=[_row_at(TAIL, TAIL0 // TAIL), _row(wide), _row(wide), _row(wide), _vec(Q_LORA), _vec(KV_LORA),
                  _full((Q_LORA, wide)), _full((KV_LORA, wide)), _full((KV_LORA, wide)), _pos(HP), _pos(HP), _pos(HP)],
        out_specs=[_row(wide), _row(wide), _row(wide), _row(TAIL), _vec(Q_LORA), _vec(KV_LORA)],
        out_shape=[jax.ShapeDtypeStruct((t, wide), BF16)] * 3 + [jax.ShapeDtypeStruct((t, TAIL), BF16),
                   jax.ShapeDtypeStruct((1, Q_LORA), F32), jax.ShapeDtypeStruct((1, KV_LORA), F32)],
        compiler_params=_cp("arbitrary"),
    )(proj, dq_, dk_, dv_, g_cq, g_ckv, w_uq, w_k, w_v, rc, rsm, rsp)


def _post_attn(out_a, out_b, g_a, g_b):
    t = out_a.shape[0]

    def body(a_ref, b_ref, ga_ref, gb_ref, y_ref):
        a, b = a_ref[...], b_ref[...]
        y_ref[:, :D_A] = (a * _rms(a) * ga_ref[...]).astype(BF16)
        y_ref[:, D_A:] = (b * _rms(b) * gb_ref[...]).astype(BF16)

    return pl.pallas_call(
        body, name="post_attn", grid=(t // TM,),
        in_specs=[_row(D_A), _row(D_A), _vec(D_A), _vec(D_A)],
        out_specs=_row(D), out_shape=jax.ShapeDtypeStruct((t, D), BF16),
        compiler_params=_cp("parallel"),
    )(out_a, out_b, g_a, g_b)


def _post_attn_bwd(dy, out_a, out_b, g_a, g_b):
    t = dy.shape[0]

    def body(dy_ref, a_ref, b_ref, ga_ref, gb_ref, da_ref, db_ref, dga_ref, dgb_ref):
        i = pl.program_id(0)
        dy_ = dy_ref[...]
        for src, g_ref, dst, dg_ref, sl in ((a_ref, ga_ref, da_ref, dga_ref, slice(0, D_A)),
                                            (b_ref, gb_ref, db_ref, dgb_ref, slice(D_A, D))):
            v = src[...]
            r = _rms(v)
            n = v * r
            dyv = dy_[:, sl]
            _acc_first(i, dg_ref, _colsum(dyv * n))
            dst[...] = _rms_bwd(n, r, dyv * g_ref[...])

    return pl.pallas_call(
        body, name="post_attn_bwd", grid=(t // TM,),
        in_specs=[_row(D), _row(D_A), _row(D_A), _vec(D_A), _vec(D_A)],
        out_specs=[_row(D_A), _row(D_A), _vec(D_A), _vec(D_A)],
        out_shape=[jax.ShapeDtypeStruct((t, D_A), F32)] * 2 + [jax.ShapeDtypeStruct((1, D_A), F32)] * 2,
        compiler_params=_cp("arbitrary"),
    )(dy, out_a, out_b, g_a, g_b)


def _resid_norm2(x, mix, g1, g, sc, sh):
    t = x.shape[0]

    def body(x_ref, mix_ref, g1_ref, g_ref, sc_ref, sh_ref, x2_ref, h_ref):
        x2 = x_ref[...] + g1_ref[0] * mix_ref[...]
        x2_ref[...] = x2
        n = x2 * _rms(x2)
        h_ref[...] = ((n * g_ref[...]) * (1.0 + sc_ref[0]) + sh_ref[0]).astype(BF16)

    return pl.pallas_call(
        body, name="resid_norm2", grid=(t // TM,),
        in_specs=[_row(D), _row(D), _per_ex(D), _vec(D), _per_ex(D), _per_ex(D)],
        out_specs=[_row(D), _row(D)],
        out_shape=[jax.ShapeDtypeStruct((t, D), F32), jax.ShapeDtypeStruct((t, D), BF16)],
        compiler_params=_cp("parallel"),
    )(x, mix, g1, g, sc, sh)


def _sigmoid(v):
    return 1.0 / (1.0 + jnp.exp(-v))


def _swiglu(gu):
    t = gu.shape[0]

    def body(g_ref, u_ref, a_ref):
        g = g_ref[...].astype(F32)
        a_ref[...] = (g * _sigmoid(g) * u_ref[...].astype(F32)).astype(BF16)

    return pl.pallas_call(
        body, name="swiglu", grid=(t // TM,),
        in_specs=[_row_at(D_FF, 0), _row_at(D_FF, 1)],
        out_specs=_row(D_FF), out_shape=jax.ShapeDtypeStruct((t, D_FF), BF16),
        compiler_params=_cp("parallel"),
    )(gu, gu)


def _swiglu_bwd(gu, da):
    t = gu.shape[0]

    def body(g_ref, u_ref, da_ref, dgu_ref):
        j = pl.program_id(1)
        g = g_ref[...].astype(F32)
        sg = _sigmoid(g)
        dav = da_ref[...].astype(F32)

        @pl.when(j == 0)
        def _():
            dgu_ref[...] = (dav * u_ref[...].astype(F32) * (sg * (1.0 + g * (1.0 - sg)))).astype(BF16)

        @pl.when(j == 1)
        def _():
            dgu_ref[...] = (dav * (g * sg)).astype(BF16)

    return pl.pallas_call(
        body, name="swiglu_bwd", grid=(t // TM, 2),
        in_specs=[pl.BlockSpec((TM, D_FF), lambda i, j: (i, 0)), pl.BlockSpec((TM, D_FF), lambda i, j: (i, 1)),
                  pl.BlockSpec((TM, D_FF), lambda i, j: (i, 0))],
        out_specs=pl.BlockSpec((TM, D_FF), lambda i, j: (i, j)),
        out_shape=jax.ShapeDtypeStruct((t, 2 * D_FF), BF16),
        compiler_params=_cp("parallel", "arbitrary"),
    )(gu, gu, da)


def _final(x2, f, g2, g_fin, target):
    t = x2.shape[0]
    nb = t // S
    tpb = S // TM

    def body(x2_ref, f_ref, g2_ref, g_ref, t_ref, dx3_ref, df_ref, loss_ref, dgf_ref, dg2_ref):
        i = pl.program_id(0)
        fv = f_ref[...]
        x3 = x2_ref[...] + g2_ref[0] * fv
        r = _rms(x3)
        n = x3 * r
        err = n * g_ref[...] - t_ref[...]
        _acc_first(i, loss_ref, _colsum(err * err))
        dy = err * (1.0 / D)
        _acc_first(i, dgf_ref, _colsum(dy * n))
        dx3 = _rms_bwd(n, r, dy * g_ref[...])
        dx3_ref[...] = dx3
        _acc_first(i, dg2_ref, _colsum(dx3 * fv), every=tpb)
        df_ref[...] = (dx3 * g2_ref[0]).astype(BF16)

    return pl.pallas_call(
        body, name="final", grid=(t // TM,),
        in_specs=[_row(D), _row(D), _per_ex(D), _vec(D), _row(D)],
        out_specs=[_row(D), _row(D), _vec(D), _vec(D), _per_ex(D)],
        out_shape=[jax.ShapeDtypeStruct((t, D), F32), jax.ShapeDtypeStruct((t, D), BF16),
                   jax.ShapeDtypeStruct((1, D), F32), jax.ShapeDtypeStruct((1, D), F32),
                   jax.ShapeDtypeStruct((nb, 1, D), F32)],
        compiler_params=_cp("arbitrary"),
    )(x2, f, g2, g_fin, target)


def _norm_bwd(xin, dh, dres, g, sc, gate=None):
    t = xin.shape[0]
    nb = t // S
    tpb = S // TM
    gated = gate is not None

    def body(*refs):
        if gated:
            x_ref, dh_ref, dres_ref, g_ref, sc_ref, mix_ref, g1_ref, dx_ref, dsh_ref, dsc_ref, dg_ref, dg1_ref, dmix_ref = refs
        else:
            x_ref, dh_ref, dres_ref, g_ref, sc_ref, dx_ref, dsh_ref, dsc_ref, dg_ref = refs
        i = pl.program_id(0)
        xv, dhv = x_ref[...], dh_ref[...]
        r = _rms(xv)
        n = xv * r
        gv = g_ref[...]
        _acc_first(i, dsh_ref, _colsum(dhv), every=tpb)
        _acc_first(i, dsc_ref, _colsum(dhv * (n * gv)), every=tpb)
        dng = dhv * (1.0 + sc_ref[0])
        _acc_first(i, dg_ref, _colsum(dng * n))
        dx = dres_ref[...] + _rms_bwd(n, r, dng * gv)
        dx_ref[...] = dx
        if gated:
            _acc_first(i, dg1_ref, _colsum(dx * mix_ref[...]), every=tpb)
            dmix_ref[...] = (dx * g1_ref[0]).astype(BF16)

    in_specs = [_row(D), _row(D), _row(D), _vec(D), _per_ex(D)]
    out_specs = [_row(D), _per_ex(D), _per_ex(D), _vec(D)]
    out_shape = [jax.ShapeDtypeStruct((t, D), F32), jax.ShapeDtypeStruct((nb, 1, D), F32),
                 jax.ShapeDtypeStruct((nb, 1, D), F32), jax.ShapeDtypeStruct((1, D), F32)]
    args = [xin, dh, dres, g, sc]
    if gated:
        in_specs += [_row(D), _per_ex(D)]
        out_specs += [_per_ex(D), _row(D)]
        out_shape += [jax.ShapeDtypeStruct((nb, 1, D), F32), jax.ShapeDtypeStruct((t, D), BF16)]
        args += list(gate)
    return pl.pallas_call(
        body, name="norm2_bwd" if gated else "norm1_bwd", grid=(t // TM,),
        in_specs=in_specs, out_specs=out_specs, out_shape=out_shape,
        compiler_params=_cp("arbitrary"),
    )(*args)


TQ = 256
TB = 512


def _mla_fwd(q, k, v):
    t = q.shape[0]
    nb = t // S

    def body(q_ref, k_ref, v_ref, o_ref, lse_ref):
        causal = lax.broadcasted_iota(jnp.int32, (TB, TB), 0) >= lax.broadcasted_iota(jnp.int32, (TB, TB), 1)
        heads = [slice(h * HP, (h + 1) * HP) for h in range(2)]
        for i in range(S // TB):
            ri, past = slice(i * TB, (i + 1) * TB), slice(0, i * TB)
            qhs = [q_ref[ri, sl] for sl in heads]
            sd = [jnp.where(causal, _dot(qh, k_ref[ri, sl], ((1,), (1,))) * SCALE_B, NEG) for qh, sl in zip(qhs, heads)]
            ms = [jnp.max(s, axis=-1, keepdims=True) for s in sd]
            if i:
                so = [_dot(qh, k_ref[past, sl], ((1,), (1,))) * SCALE_B for qh, sl in zip(qhs, heads)]
                ms = [jnp.maximum(m, jnp.max(s, axis=-1, keepdims=True)) for m, s in zip(ms, so)]
            pd = [jnp.exp(s - m) for s, m in zip(sd, ms)]
            ls = [jnp.sum(p, axis=-1, keepdims=True) for p in pd]
            acc = [_dot(p.astype(BF16), v_ref[ri, sl], ((1,), (0,))) for p, sl in zip(pd, heads)]
            if i:
                po = [jnp.exp(s - m) for s, m in zip(so, ms)]
                ls = [l + jnp.sum(p, axis=-1, keepdims=True) for l, p in zip(ls, po)]
                acc = [a + _dot(p.astype(BF16), v_ref[past, sl], ((1,), (0,))) for a, p, sl in zip(acc, po, heads)]
            o_ref[ri, :] = acc[0] / ls[0] + acc[1] / ls[1]
            for sl, m, l in zip(heads, ms, ls):
                lse_ref[ri, sl] = jnp.broadcast_to(m + jnp.log(l), (TB, HP))

    wide2 = pl.BlockSpec((S, 2 * HP), lambda b, p: (b, p))
    return pl.pallas_call(
        body, name="mla_fwd", grid=(nb, H // 2),
        in_specs=[wide2, wide2, wide2],
        out_specs=[pl.BlockSpec((S, HP), lambda b, p: (b, p)), wide2],
        out_shape=[jax.ShapeDtypeStruct((t, H * VDIM), F32), jax.ShapeDtypeStruct((t, H * HP), F32)],
        compiler_params=_cp("parallel", "parallel"),
    )(q, k, v)


def _mla_bwd(q, k, v, o, do, lse):
    t = q.shape[0]
    nb = t // S
    nq = S // TQ

    def body(q_ref, k_ref, v_ref, o_ref, do_ref, lse_ref, dq_ref, dk_ref, dv_ref):
        lane = lax.broadcasted_iota(jnp.int32, (TB, HP), 1)
        causal = lax.broadcasted_iota(jnp.int32, (TB, TB), 0) >= lax.broadcasted_iota(jnp.int32, (TB, TB), 1)
        heads = [slice(h * HP, (h + 1) * HP) for h in range(2)]
        nblk = S // TB
        for i in reversed(range(nblk)):
            ri, past = slice(i * TB, (i + 1) * TB), slice(0, i * TB)
            dov = do_ref[ri, :]
            prod = dov * o_ref[ri, :]
            dob = dov.astype(BF16)
            deltas = [jnp.sum(jnp.where((lane < VDIM) if h == 0 else (lane >= VDIM), prod, 0.0), axis=-1, keepdims=True)
                      for h in range(2)]
            qhs = [q_ref[ri, sl] for sl in heads]
            lses = [lse_ref[ri, sl][:, :1] for sl in heads]
            for rows, diagonal in ((ri, True), (past, False)):
                if rows.stop == rows.start:
                    continue
                ps = [jnp.exp(_dot(qh, k_ref[rows, sl], ((1,), (1,))) * SCALE_B - lse) for qh, sl, lse in zip(qhs, heads, lses)]
                if diagonal:
                    ps = [jnp.where(causal, p, 0.0) for p in ps]
                dps = [_dot(dob, v_ref[rows, sl], ((1,), (1,))) for sl in heads]
                dss = [(p * (dp - delta) * SCALE_B).astype(BF16) for p, dp, delta in zip(ps, dps, deltas)]
                for sl, qh, p, ds in zip(heads, qhs, ps, dss):
                    dq = _dot(ds, k_ref[rows, sl], ((1,), (0,)))
                    dk = _dot(ds, qh, ((0,), (0,)))
                    dv = _dot(p.astype(BF16), dob, ((0,), (0,)))
                    if diagonal:
                        dq_ref[ri, sl] = dq
                    else:
                        dq_ref[ri, sl] += dq
                    if i == nblk - 1:
                        dk_ref[rows, sl] = dk
                        dv_ref[rows, sl] = dv
                    else:
                        dk_ref[rows, sl] += dk
                        dv_ref[rows, sl] += dv

    wide2 = pl.BlockSpec((S, 2 * HP), lambda b, p: (b, p))
    pair = pl.BlockSpec((S, HP), lambda b, p: (b, p))
    return pl.pallas_call(
        body, name="mla_bwd", grid=(nb, H // 2),
        in_specs=[wide2, wide2, wide2, pair, pair, wide2],
        out_specs=[wide2, wide2, wide2],
        out_shape=[jax.ShapeDtypeStruct((t, H * HP), F32)] * 3,
        compiler_params=_cp("parallel", "parallel"),
    )(q, k, v, o, do, lse)


def _t5_bucket(dist):
    max_exact = N_BUCKETS // 2
    d = np.maximum(dist, 1).astype(np.float64)
    large = max_exact + (np.log(d / max_exact) / np.log(MAX_DISTANCE / max_exact) * (N_BUCKETS - max_exact)).astype(np.int64)
    large = np.minimum(large, N_BUCKETS - 1)
    return np.where(dist < max_exact, dist, large).astype(np.int32)


def _band_geometry():
    a = np.arange(BLK)[:, None]
    bk = np.arange(2 * BLK)[None, :]
    steps = BLK + a - bk
    valid = (steps >= 0) & (steps <= BLK)
    buckets = np.stack([_t5_bucket(np.clip(steps, 0, BLK) * d) for d in DILATIONS])
    return buckets, valid


def _band_bias(rel_bias):
    buckets, valid = _band_geometry()
    onehot = (jnp.asarray(buckets)[..., None] == jnp.arange(N_BUCKETS)).astype(F32)
    bias = jnp.einsum("rqkn,nh->rhqk", onehot, rel_bias, precision=lax.Precision.HIGHEST)
    bias = jnp.where(jnp.asarray(valid)[None, None], bias, NEG)
    return bias.reshape(3, H // 2, 2 * BLK, 2 * BLK)


def _dil_items():
    items = []
    for r, d in enumerate(DILATIONS):
        for res in range(d):
            for blk in range(S // d // BLK):
                items.append((r, d, blk * BLK * d + res, blk > 0))
    return items


GROUP = 4


def _strided(start, d):
    return pl.ds(start, BLK) if d == 1 else pl.ds(start, BLK, stride=d)


def _stack_heads(tile, own):
    return jnp.where(own, jnp.concatenate([tile, tile], axis=0), 0.0).astype(BF16)


def _own_lanes():
    row = lax.broadcasted_iota(jnp.int32, (2 * BLK, HP), 0)
    lane = lax.broadcasted_iota(jnp.int32, (2 * BLK, HP), 1)
    return (lane < E_A) == (row < BLK)


def _dil_fwd(proj, biasm):
    t = proj.shape[0]
    nb = t // S

    def body(q_ref, k_ref, v_ref, b_ref, o_ref, lse_ref, ob_ref, lb_ref):
        lane = lax.broadcasted_iota(jnp.int32, (BLK, HP), 1)
        own = _own_lanes()
        items = _dil_items()
        for g in range(0, len(items), GROUP):
            grp = items[g:g + GROUP]
            ss, vts = [], []
            for r, d, start, has_prev in grp:
                cur = _strided(start, d)
                rows = [_strided(start - BLK * d, d), cur] if has_prev else [cur]
                q2 = _stack_heads(q_ref[cur, :], own)
                kt = jnp.concatenate([k_ref[x, :] for x in rows], axis=0).astype(BF16)
                vts.append(jnp.concatenate([v_ref[x, :] for x in rows], axis=0).astype(BF16))
                bias = b_ref[r, 0] if has_prev else b_ref[r, 0, :, BLK:]
                ss.append(_dot(q2, kt, ((1,), (1,))) * SCALE_A + bias)
            ms = [jnp.max(s, axis=-1, keepdims=True) for s in ss]
            ps = [jnp.exp(s - m) for s, m in zip(ss, ms)]
            ls = [jnp.sum(p, axis=-1, keepdims=True) for p in ps]
            for (r, d, start, _), p, vt, m, l in zip(grp, ps, vts, ms, ls):
                cur = _strided(start, d)
                o2 = _dot(p.astype(BF16), vt, ((1,), (0,))) / l
                lse2 = m + jnp.log(l)
                ob_ref[r, cur, :] = jnp.where(lane < E_A, o2[:BLK], o2[BLK:])
                lb_ref[r, cur, :] = jnp.where(lane < E_A, lse2[:BLK], lse2[BLK:])

        def merge(c, _):
            rows = pl.ds(pl.multiple_of(c * TQ, TQ), TQ)
            l0, l1, l2 = lb_ref[0, rows, :], lb_ref[1, rows, :], lb_ref[2, rows, :]
            m = jnp.maximum(jnp.maximum(l0, l1), l2)
            e0, e1, e2 = jnp.exp(l0 - m), jnp.exp(l1 - m), jnp.exp(l2 - m)
            tot = e0 + e1 + e2
            o_ref[rows, :] = (e0 * ob_ref[0, rows, :] + e1 * ob_ref[1, rows, :] + e2 * ob_ref[2, rows, :]) / tot
            lse_ref[rows, :] = m + jnp.log(tot)
            return 0

        lax.fori_loop(0, S // TQ, merge, 0)

    npair = H // 2
    return pl.pallas_call(
        body, name="dil_fwd", grid=(nb, npair),
        in_specs=[pl.BlockSpec((S, HP), lambda b, p: (b, p)), pl.BlockSpec((S, HP), lambda b, p: (b, npair + p)),
                  pl.BlockSpec((S, HP), lambda b, p: (b, 2 * npair + p)),
                  pl.BlockSpec((3, 1, 2 * BLK, 2 * BLK), lambda b, p: (0, p, 0, 0))],
        out_specs=[pl.BlockSpec((S, HP), lambda b, p: (b, p))] * 2,
        out_shape=[jax.ShapeDtypeStruct((t, D_A), F32)] * 2,
        scratch_shapes=[pltpu.VMEM((3, S, HP), F32), pltpu.VMEM((3, S, HP), F32)],
        compiler_params=_cp("parallel", "parallel"),
    )(proj, proj, proj, biasm)


def _dil_bwd(proj, biasm, o, do, lse):
    t = proj.shape[0]
    nb = t // S

    def body(q_ref, k_ref, v_ref, b_ref, o_ref, do_ref, lse_ref, dq_ref, dk_ref, dv_ref, ds_ref):
        dq_ref[...] = jnp.zeros_like(dq_ref)
        dk_ref[...] = jnp.zeros_like(dk_ref)
        dv_ref[...] = jnp.zeros_like(dv_ref)
        ds_ref[...] = jnp.zeros_like(ds_ref)
        lane = lax.broadcasted_iota(jnp.int32, (BLK, HP), 1)
        own = _own_lanes()
        items = _dil_items()
        for g in range(0, len(items), GROUP):
            grp = items[g:g + GROUP]
            q2s, kts, do2s, ss, dps, lse2s, delta2s = [], [], [], [], [], [], []
            for r, d, start, has_prev in grp:
                cur = _strided(start, d)
                rows = [_strided(start - BLK * d, d), cur] if has_prev else [cur]
                q2 = _stack_heads(q_ref[cur, :], own)
                kt = jnp.concatenate([k_ref[x, :] for x in rows], axis=0).astype(BF16)
                vt = jnp.concatenate([v_ref[x, :] for x in rows], axis=0).astype(BF16)
                dot_ = do_ref[cur, :]
                prod = dot_ * o_ref[cur, :]
                lset = lse_ref[cur, :]
                do2 = _stack_heads(dot_, own)
                bias = b_ref[r, 0] if has_prev else b_ref[r, 0, :, BLK:]
                ss.append(_dot(q2, kt, ((1,), (1,))) * SCALE_A + bias)
                dps.append(_dot(do2, vt, ((1,), (1,))))
                lse2s.append(jnp.concatenate([lset[:, :1], lset[:, E_A:E_A + 1]], axis=0))
                delta2s.append(jnp.concatenate([jnp.sum(jnp.where(lane < E_A, prod, 0.0), axis=-1, keepdims=True),
                                                jnp.sum(jnp.where(lane >= E_A, prod, 0.0), axis=-1, keepdims=True)], axis=0))
                q2s.append(q2)
                kts.append(kt)
                do2s.append(do2)
            ps = [jnp.exp(s - lse2) for s, lse2 in zip(ss, lse2s)]
            dls = [p * (dp - delta2) for p, dp, delta2 in zip(ps, dps, delta2s)]
            for (r, d, start, has_prev), q2, kt, do2, p, dl in zip(grp, q2s, kts, do2s, ps, dls):
                cur = _strided(start, d)
                dsb = (dl * SCALE_A).astype(BF16)
                dq2 = _dot(dsb, kt, ((1,), (0,)))
                dkt = _dot(dsb, q2, ((0,), (0,)))
                dvt = _dot(p.astype(BF16), do2, ((0,), (0,)))
                dq_ref[cur, :] += jnp.where(lane < E_A, dq2[:BLK], dq2[BLK:])
                if has_prev:
                    prev = _strided(start - BLK * d, d)
                    ds_ref[0, r, 0] += dl
                    dk_ref[prev, :] += dkt[:BLK]
                    dv_ref[prev, :] += dvt[:BLK]
                    dk_ref[cur, :] += dkt[BLK:]
                    dv_ref[cur, :] += dvt[BLK:]
                else:
                    ds_ref[0, r, 0, :, BLK:] += dl
                    dk_ref[cur, :] += dkt
                    dv_ref[cur, :] += dvt

    npair = H // 2
    pair = pl.BlockSpec((S, HP), lambda b, p: (b, p))
    return pl.pallas_call(
        body, name="dil_bwd", grid=(nb, npair),
        in_specs=[pair, pl.BlockSpec((S, HP), lambda b, p: (b, npair + p)),
                  pl.BlockSpec((S, HP), lambda b, p: (b, 2 * npair + p)),
                  pl.BlockSpec((3, 1, 2 * BLK, 2 * BLK), lambda b, p: (0, p, 0, 0)), pair, pair, pair],
        out_specs=[pair, pair, pair, pl.BlockSpec((1, 3, 1, 2 * BLK, 2 * BLK), lambda b, p: (b, 0, p, 0, 0))],
        out_shape=[jax.ShapeDtypeStruct((t, D_A), F32)] * 3 + [jax.ShapeDtypeStruct((nb, 3, npair, 2 * BLK, 2 * BLK), F32)],
        compiler_params=_cp("parallel", "parallel"),
    )(proj, proj, proj, biasm, o, do, lse)


def _rel_bias_grad(dlogits):
    nb = dlogits.shape[0]
    buckets, _ = _band_geometry()
    kk = 3 * BLK * 2 * BLK
    dl = jnp.transpose(dlogits.reshape(nb, 3, H, BLK, 2 * BLK), (0, 2, 1, 3, 4)).reshape(nb, H, kk)
    bk = jnp.asarray(buckets.reshape(1, kk))
    tk = kk // 12

    def body(dl_ref, bk_ref, o_ref):
        j = pl.program_id(0)
        onehot = (bk_ref[...] == lax.broadcasted_iota(jnp.int32, (N_BUCKETS, tk), 0)).astype(F32)
        tot = dl_ref[0]
        for b in range(1, nb):
            tot = tot + dl_ref[b]
        part = lax.dot_general(onehot, tot, ((((1,), (1,))), ((), ())), preferred_element_type=F32,
                               precision=lax.Precision.HIGHEST)
        _acc_first(j, o_ref, part)

    return pl.pallas_call(
        body, name="rel_bias_grad", grid=(kk // tk,),
        in_specs=[pl.BlockSpec((nb, H, tk), lambda j: (0, 0, j)), pl.BlockSpec((1, tk), lambda j: (0, j))],
        out_specs=pl.BlockSpec((N_BUCKETS, H), lambda j: (0, 0)),
        out_shape=jax.ShapeDtypeStruct((N_BUCKETS, H), F32),
        compiler_params=_cp("arbitrary"),
    )(dl, bk)


def _mesh_place():
    x, y, c = lax.axis_index("x"), lax.axis_index("y"), lax.axis_index("c")
    return x, y, c


def _peer(k):
    x, y, c = _mesh_place()
    px = 1 - x if k & 4 else x
    py = 1 - y if k & 2 else y
    pc = 1 - c if k & 1 else c
    return (px, py, pc), 4 * px + 2 * py + pc


ANY = pl.BlockSpec(memory_space=pl.ANY)


def _all_gather(arrays, name):
    n_arr = len(arrays)

    def body(*refs):
        ins, outs = refs[:n_arr], refs[n_arr:2 * n_arr]
        send, recv, loc = refs[2 * n_arr:]
        x, y, c = _mesh_place()
        me = 4 * x + 2 * y + c
        local, remote = [], []
        for a in range(n_arr):
            cp = pltpu.make_async_copy(ins[a], outs[a].at[me], loc.at[a])
            cp.start()
            local.append(cp)
            for k in range(1, N_DEV):
                dev, idx = _peer(k)
                put = pltpu.make_async_remote_copy(ins[a], outs[a].at[me], send.at[a * (N_DEV - 1) + k - 1], recv.at[a * (N_DEV - 1) + k - 1],
                                                   device_id=dev, device_id_type=pl.DeviceIdType.MESH)
                put.start()
                got = pltpu.make_async_remote_copy(ins[a], outs[a].at[idx], send.at[a * (N_DEV - 1) + k - 1], recv.at[a * (N_DEV - 1) + k - 1],
                                                   device_id=dev, device_id_type=pl.DeviceIdType.MESH)
                remote.append((put, got))
        for cp in local:
            cp.wait()
        for put, got in remote:
            put.wait_send()
            got.wait_recv()

    return pl.pallas_call(
        body, name=name,
        in_specs=[ANY] * n_arr, out_specs=[ANY] * n_arr,
        out_shape=[jax.ShapeDtypeStruct((N_DEV,) + a.shape, a.dtype) for a in arrays],
        scratch_shapes=[pltpu.SemaphoreType.DMA((n_arr * (N_DEV - 1),)), pltpu.SemaphoreType.DMA((n_arr * (N_DEV - 1),)),
                        pltpu.SemaphoreType.DMA((n_arr,))],
        compiler_params=pltpu.CompilerParams(has_side_effects=True),
    )(*arrays)


def _all_to_all(arrays, name):
    n_arr = len(arrays)

    def body(*refs):
        ins, outs = refs[:n_arr], refs[n_arr:2 * n_arr]
        send, recv, loc = refs[2 * n_arr:]
        x, y, c = _mesh_place()
        me = 4 * x + 2 * y + c
        local, remote = [], []
        for a in range(n_arr):
            cp = pltpu.make_async_copy(ins[a].at[me], outs[a].at[me], loc.at[a])
            cp.start()
            local.append(cp)
            for k in range(1, N_DEV):
                dev, idx = _peer(k)
                put = pltpu.make_async_remote_copy(ins[a].at[idx], outs[a].at[me], send.at[a * (N_DEV - 1) + k - 1], recv.at[a * (N_DEV - 1) + k - 1],
                                                   device_id=dev, device_id_type=pl.DeviceIdType.MESH)
                put.start()
                got = pltpu.make_async_remote_copy(ins[a].at[idx], outs[a].at[idx], send.at[a * (N_DEV - 1) + k - 1], recv.at[a * (N_DEV - 1) + k - 1],
                                                   device_id=dev, device_id_type=pl.DeviceIdType.MESH)
                remote.append((put, got))
        for cp in local:
            cp.wait()
        for put, got in remote:
            put.wait_send()
            got.wait_recv()

    return pl.pallas_call(
        body, name=name,
        in_specs=[ANY] * n_arr, out_specs=[ANY] * n_arr,
        out_shape=[jax.ShapeDtypeStruct(a.shape, a.dtype) for a in arrays],
        scratch_shapes=[pltpu.SemaphoreType.DMA((n_arr * (N_DEV - 1),)), pltpu.SemaphoreType.DMA((n_arr * (N_DEV - 1),)),
                        pltpu.SemaphoreType.DMA((n_arr,))],
        compiler_params=pltpu.CompilerParams(has_side_effects=True),
    )(*arrays)


HBM = pl.BlockSpec(memory_space=pltpu.HBM)
SEM = pl.BlockSpec(memory_space=pltpu.SEMAPHORE)
DATAFLOW = pltpu.SideEffectType.DATAFLOW_SIDE_EFFECTING


def _own_block_in_place(block, me):
    land = jnp.zeros((N_DEV,) + block.shape, block.dtype)
    return lax.dynamic_update_slice(land, block[None], (me,) + (0,) * block.ndim)


def _peer_copies(srcs, lands, send, recv, gather):
    x, y, c = _mesh_place()
    me = 4 * x + 2 * y + c
    out = []
    for a, (src, land) in enumerate(zip(srcs, lands)):
        for k in range(1, N_DEV):
            dev, idx = _peer(k)
            sem = a * (N_DEV - 1) + k - 1
            mine = src if gather else src.at[idx]
            put = pltpu.make_async_remote_copy(mine, land.at[me], send.at[sem], recv.at[sem],
                                               device_id=dev, device_id_type=pl.DeviceIdType.MESH)
            got = pltpu.make_async_remote_copy(mine, land.at[idx], send.at[sem], recv.at[sem],
                                               device_id=dev, device_id_type=pl.DeviceIdType.MESH)
            out.append((put, got))
    return out


def _exchange_start(srcs, lands, gather, after, name):
    n = len(srcs)

    def body(*refs):
        srcs_, lands_, send, recv = refs[:n], refs[n:2 * n], refs[2 * n + 1], refs[2 * n + 2]
        for put, _ in _peer_copies(srcs_, lands_, send, recv, gather):
            put.start()
        refs[-1][...] = jnp.zeros_like(refs[-1])

    nsem = n * (N_DEV - 1)
    thru = [pltpu.HBM(a.shape, a.dtype) for a in list(srcs) + list(lands)]
    res = pl.pallas_call(
        body, name=name,
        out_shape=(pltpu.SemaphoreType.DMA((nsem,)), pltpu.SemaphoreType.DMA((nsem,)), *thru, jax.ShapeDtypeStruct((8, 128), F32)),
        in_specs=[HBM] * (2 * n) + [ANY],
        out_specs=(SEM, SEM, *([HBM] * (2 * n)), pl.BlockSpec(memory_space=pltpu.VMEM)),
        input_output_aliases={i: 2 + i for i in range(2 * n)},
        compiler_params=pltpu.CompilerParams(has_side_effects=DATAFLOW),
    )(*[pltpu.with_memory_space_constraint(a, pltpu.HBM) for a in list(srcs) + list(lands)], after)
    return res[0], res[1], list(res[2:2 + n]), list(res[2 + n:2 + 2 * n]), res[-1]


def _exchange_wait(send, recv, srcs, lands, gather, after, name):
    n = len(srcs)

    def body(*refs):
        srcs_, lands_, send_, recv_ = refs[:n], refs[n:2 * n], refs[2 * n], refs[2 * n + 1]
        for put, got in _peer_copies(srcs_, lands_, send_, recv_, gather):
            put.wait_send()
            got.wait_recv()

    thru = [pltpu.HBM(a.shape, a.dtype) for a in list(srcs) + list(lands)]
    res = pl.pallas_call(
        body, name=name, out_shape=tuple(thru),
        in_specs=[HBM] * (2 * n) + [SEM, SEM, ANY], out_specs=tuple([HBM] * (2 * n)),
        input_output_aliases={i: i for i in range(2 * n)},
        compiler_params=pltpu.CompilerParams(has_side_effects=DATAFLOW),
    )(*srcs, *lands, send, recv, after)
    return list(res[n:])


def _silu_rows(c):
    def body(c_ref, o_ref):
        v = c_ref[...]
        o_ref[...] = v * _sigmoid(v)

    return pl.pallas_call(body, name="cond", out_shape=jax.ShapeDtypeStruct(c.shape, F32))(c)


def _mod_slab(cond_all, w_ada, b_slab):
    def body(c_ref, w_ref, b_ref, o_ref):
        o_ref[...] = _dot(c_ref[...].astype(BF16), w_ref[...].astype(BF16), ((1,), (0,))) + b_ref[...]

    return pl.pallas_call(body, name="mod_slab",
                          out_shape=jax.ShapeDtypeStruct((cond_all.shape[0], w_ada.shape[1]), F32),
                          compiler_params=pltpu.CompilerParams(vmem_limit_bytes=VMEM_LIMIT))(cond_all, w_ada, b_slab)


def _ada_grad(cond_all, dmod_cols):
    def body(c_ref, d_ref, o_ref):
        o_ref[...] = _dot(c_ref[...].astype(BF16), d_ref[...].astype(BF16), ((0,), (0,)))

    return pl.pallas_call(body, name="ada_grad",
                          out_shape=jax.ShapeDtypeStruct((cond_all.shape[1], dmod_cols.shape[1]), F32),
                          compiler_params=pltpu.CompilerParams(vmem_limit_bytes=VMEM_LIMIT))(cond_all, dmod_cols)


def _adam_math(g, w, m, v):
    m2 = B1 * m + (1.0 - B1) * g
    v2 = B2 * v + (1.0 - B2) * (g * g)
    m_hat = m2 / (1.0 - B1 ** STEP)
    v_hat = v2 / (1.0 - B2 ** STEP)
    return -LR * (m_hat / (jnp.sqrt(v_hat) + ADAM_EPS) + WD * w), m2, v2


def _adamw(parts, w, m, v, name):
    n, rows, cols = parts.shape
    tr = _pick(rows, (128, 96, 64, 32, 16, 8))

    def body(p_ref, w_ref, m_ref, v_ref, g_ref, d_ref, m2_ref, v2_ref):
        g = p_ref[0].astype(F32)
        for s in range(1, n):
            g = g + p_ref[s].astype(F32)
        g_ref[...] = g
        d_ref[...], m2_ref[...], v2_ref[...] = _adam_math(g, w_ref[...], m_ref[...], v_ref[...])

    blk = pl.BlockSpec((tr, cols), lambda i: (i, 0))
    return pl.pallas_call(
        body, name=name, grid=(rows // tr,),
        in_specs=[pl.BlockSpec((n, tr, cols), lambda i: (0, i, 0)), blk, blk, blk],
        out_specs=[blk] * 4, out_shape=[jax.ShapeDtypeStruct((rows, cols), F32)] * 4,
        compiler_params=_cp("parallel"),
    )(parts, w, m, v)


SMALL = (("b_ada", N_MOD * D), ("g_norm1", D), ("g_cq", Q_LORA), ("g_ckv", KV_LORA), ("rel_bias", N_BUCKETS * H),
         ("g_out_a", D_A), ("g_out_b", D_A), ("g_norm2", D), ("g_final", D))
TILE = 8 * 128


def _tiles(n):
    return -(-n // TILE) * 8


SMALL_ROWS = sum(_tiles(n) for _, n in SMALL)
LOSS_ROWS = 8
PACK_ROWS = SMALL_ROWS + LOSS_ROWS


def _pack(vals):
    parts = []
    for v in vals:
        v = v.reshape(-1)
        rows = _tiles(v.shape[0])
        parts.append(jnp.pad(v, (0, rows * 128 - v.shape[0])).reshape(rows, 128))
    return jnp.concatenate(parts, axis=0)


def _unpack(packed, shapes):
    out, r = [], 0
    for (_, n), shp in zip(SMALL, shapes):
        rows = _tiles(n)
        out.append(packed[r:r + rows].reshape(-1)[:n].reshape(shp))
        r += rows
    return out


def _small_update(parts, w, m, v):
    def body(p_ref, w_ref, m_ref, v_ref, g_ref, d_ref, m2_ref, v2_ref, loss_ref):
        tot = p_ref[0]
        for s in range(1, N_DEV):
            tot = tot + p_ref[s]
        g = tot[:SMALL_ROWS]
        g_ref[...] = g
        d_ref[...], m2_ref[...], v2_ref[...] = _adam_math(g, w_ref[...], m_ref[...], v_ref[...])
        loss_ref[...] = jnp.broadcast_to((0.5 / D) * jnp.sum(tot[SMALL_ROWS:]), loss_ref.shape)

    return pl.pallas_call(
        body, name="small_update",
        out_shape=[jax.ShapeDtypeStruct((SMALL_ROWS, 128), F32)] * 4 + [jax.ShapeDtypeStruct((8, 128), F32)],
    )(parts, w, m, v)


def _cols_from_blocks(g):
    return jnp.transpose(g, (1, 0, 2)).reshape(g.shape[1], N_DEV * g.shape[2])


def _cols_to_blocks(w):
    r, c = w.shape
    return jnp.transpose(w.reshape(r, N_DEV, c // N_DEV), (1, 0, 2))


def _pad_w_in(w):
    z = jnp.zeros((w.shape[0], NOPE), w.dtype)
    return jnp.concatenate([w[:, :P_IN - ROPE], z, w[:, P_IN - ROPE:], z[:, :HP - NOPE - ROPE]], axis=1)


def _unpad_w_in(g):
    k0 = P_IN - ROPE + NOPE
    return jnp.concatenate([g[:, :P_IN - ROPE], g[:, k0:k0 + ROPE]], axis=1)


def _pad_w_uq(w):
    w3 = w.reshape(Q_LORA, H, NOPE + ROPE)
    return jnp.pad(w3, ((0, 0), (0, 0), (0, HP - NOPE - ROPE))).reshape(Q_LORA, H * HP)


def _unpad_w_uq(g):
    return g.reshape(Q_LORA, H, HP)[:, :, :NOPE + ROPE].reshape(Q_LORA, H * (NOPE + ROPE))


def _split_w_ukv(w):
    w4 = w.reshape(KV_LORA, H // 2, 2, HP)
    z = jnp.zeros((KV_LORA, H // 2, NOPE), w.dtype)
    kn, vv = w4[..., :NOPE], w4[..., NOPE:]
    w_k = jnp.stack([jnp.concatenate([kn[:, :, 0], z], -1), jnp.concatenate([kn[:, :, 1], z], -1)], axis=2)
    w_v = jnp.stack([jnp.concatenate([vv[:, :, 0], z], -1), jnp.concatenate([z, vv[:, :, 1]], -1)], axis=2)
    return w_k.reshape(KV_LORA, H * HP), w_v.reshape(KV_LORA, H * HP)


def _join_w_ukv(g_k, g_v):
    gk = g_k.reshape(KV_LORA, H // 2, 2, HP)
    gv = g_v.reshape(KV_LORA, H // 2, 2, HP)
    even = jnp.concatenate([gk[:, :, 0, :NOPE], gv[:, :, 0, :VDIM]], -1)
    odd = jnp.concatenate([gk[:, :, 1, :NOPE], gv[:, :, 1, VDIM:]], -1)
    return jnp.stack([even, odd], axis=2).reshape(KV_LORA, H * HP)


def _rope_tables():
    half = ROPE // 2
    inv = ROPE_THETA ** (-jnp.arange(half, dtype=F32) / half)
    ang = jnp.arange(S, dtype=F32)[:, None] * inv[None, :]
    cos, sin = jnp.cos(ang), jnp.sin(ang)
    ones, zeros = jnp.ones((S, NOPE), F32), jnp.zeros((S, NOPE), F32)
    tail1, tail0 = jnp.ones((S, HP - NOPE - ROPE), F32), jnp.zeros((S, HP - NOPE - ROPE), F32)
    zh = jnp.zeros((S, half), F32)
    c = jnp.concatenate([ones, cos, cos, tail1], axis=1)
    sm = jnp.concatenate([zeros, -sin, zh, tail0], axis=1)
    sp = jnp.concatenate([zeros, zh, sin, tail0], axis=1)
    return c, sm, sp


def _local_step(x, mod, target, g_norm1, w_in_p, g_cq, w_uq_p, g_ckv, w_k, w_v, rel_bias, g_out_a, g_out_b, w_out,
                g_norm2, w_ffn_in, w_ffn_out, g_final, late_weights=None, on_ffn_grads=None):
    nb = x.shape[0] // S
    sh1, sc1, g1, sh2, sc2, g2 = (mod[:, n].reshape(nb, 1, D) for n in range(N_MOD))
    rc, rsm, rsp = _rope_tables()
    biasm = _band_bias(rel_bias)

    h1 = _pre1(x, g_norm1, sc1, sh1)
    proj = _mm_nn(h1, w_in_p, F32, "proj")
    q, k, v, cqn, ckvn = _mla_pre(proj, g_cq, g_ckv, w_uq_p, w_k, w_v, rc, rsm, rsp)
    out_b, lse_b = _mla_fwd(q, k, v)
    out_a, lse_a = _dil_fwd(proj, biasm)
    y = _post_attn(out_a, out_b, g_out_a, g_out_b)
    if late_weights is not None:
        w_out, w_ffn_in, w_ffn_out = late_weights(y)
    mix = _mm_nn(y, w_out, F32, "mix")
    x2, h2 = _resid_norm2(x, mix, g1, g_norm2, sc2, sh2)
    gu = _mm_nn(h2, w_ffn_in, BF16, "ffn_in")
    act = _swiglu(gu)
    f = _mm_nn(act, w_ffn_out, F32, "ffn_out")
    dx3, df, loss_cols, dg_final, dg2 = _final(x2, f, g2, g_final, target)

    dact = _mm_nt(df, w_ffn_out, BF16, "d_act")
    gw_ffn_out = _mm_tn(act, df, "gw_ffn_out")
    dgu = _swiglu_bwd(gu, dact)
    dh2 = _mm_nt(dgu, w_ffn_in, F32, "d_h2")
    gw_ffn_in = _mm_tn(h2, dgu, "gw_ffn_in")
    if on_ffn_grads is not None:
        g_norm2 = g_norm2 + on_ffn_grads(gw_ffn_in, gw_ffn_out)
    dx2, dsh2, dsc2, dg_norm2, dg1, dmix = _norm_bwd(x2, dh2, dx3, g_norm2, sc2, gate=(mix, g1))
    dy = _mm_nt(dmix, w_out, F32, "d_y")
    gw_out = _mm_tn(y, dmix, "gw_out")
    dout_a, dout_b, dg_out_a, dg_out_b = _post_attn_bwd(dy, out_a, out_b, g_out_a, g_out_b)
    dq_b, dk_b, dv_b = _mla_bwd(q, k, v, out_b, dout_b, lse_b)
    dq_a, dk_a, dv_a, dlogits = _dil_bwd(proj, biasm, out_a, dout_a, lse_a)
    g_rel = _rel_bias_grad(dlogits)
    dqr, dkr, dvr, dtail, dg_cq, dg_ckv = _mla_pre_bwd(proj, dq_b, dk_b, dv_b, g_cq, g_ckv, w_uq_p, w_k, w_v, rc, rsm, rsp)
    gw_uq = _mm_tn(cqn, dqr, "gw_uq")
    gw_k = _mm_tn(ckvn, dkr, "gw_k")
    gw_v = _mm_tn(ckvn, dvr, "gw_v")
    dproj = jnp.concatenate([dq_a.astype(BF16), dk_a.astype(BF16), dv_a.astype(BF16), dtail], axis=1)
    dh1 = _mm_nt(dproj, w_in_p, F32, "d_h1")
    gw_in = _mm_tn(h1, dproj, "gw_in")
    grad_x, dsh1, dsc1, dg_norm1 = _norm_bwd(x, dh1, dx2, g_norm1, sc1)

    dmod = jnp.concatenate([dsh1, dsc1, dg1, dsh2, dsc2, dg2], axis=1)
    small = dict(g_norm1=dg_norm1, g_cq=dg_cq, g_ckv=dg_ckv, rel_bias=g_rel, g_out_a=dg_out_a, g_out_b=dg_out_b,
                 g_norm2=dg_norm2, g_final=dg_final)
    big = dict(w_in=gw_in, w_uq=gw_uq, w_k=gw_k, w_v=gw_v, w_out=gw_out, w_ffn_in=gw_ffn_in, w_ffn_out=gw_ffn_out)
    return grad_x, dmod, loss_cols, small, big


def kernel(x, c, w_ada, b_ada, g_norm1, w_in, g_cq, w_uq, g_ckv, w_ukv, rel_bias, g_out_a, g_out_b, w_out, g_norm2, w_ffn_in, w_ffn_out, g_final, loss_target, m_w_ada, m_b_ada, m_g_norm1, m_w_in, m_g_cq, m_w_uq, m_g_ckv, m_w_ukv, m_rel_bias, m_g_out_a, m_g_out_b, m_w_out, m_g_norm2, m_w_ffn_in, m_w_ffn_out, m_g_final, v_w_ada, v_b_ada, v_g_norm1, v_w_in, v_g_cq, v_w_uq, v_g_ckv, v_w_ukv, v_rel_bias, v_g_out_a, v_g_out_b, v_w_out, v_g_norm2, v_w_ffn_in, v_w_ffn_out, v_g_final):
    nb = x.shape[0]
    t = nb * S
    xt, tt = x.reshape(t, D), loss_target.reshape(t, D)
    me = 4 * lax.axis_index("x") + 2 * lax.axis_index("y") + lax.axis_index("c")

    early = [w_in[0], w_uq[0], w_ukv[0]]
    gathered = _all_gather([_silu_rows(c)] + [s.astype(BF16) for s in early], "gather_weights")
    cond_all = gathered[0].reshape(N_DEV * nb, D)
    w_in_f, w_uq_f, w_ukv_f = (_cols_from_blocks(g) for g in gathered[1:4])
    w_k, w_v = _split_w_ukv(w_ukv_f)

    late = [s.astype(BF16) for s in (w_out[0], w_ffn_in[0], w_ffn_out[0])]
    late_send, late_recv, late_src, late_land, late_token = _exchange_start(
        late, [_own_block_in_place(s, me) for s in late], True, gathered[0], "gather_late_start")
    g_norm1_t = g_norm1 + late_token[:1, :1]

    def late_weights(after):
        w_out_g, w_ffn_in_g, w_ffn_out_g = _exchange_wait(late_send, late_recv, late_src, late_land, True, after,
                                                          "gather_late_wait")
        return w_out_g.reshape(D, D), _cols_from_blocks(w_ffn_in_g), w_ffn_out_g.reshape(D_FF, D)

    ffn = {}

    def on_ffn_grads(gw_ffn_in, gw_ffn_out):
        src = [_cols_to_blocks(gw_ffn_in).astype(BF16), gw_ffn_out.reshape(N_DEV, D_FF // N_DEV, D).astype(BF16)]
        land = [_own_block_in_place(lax.dynamic_index_in_dim(s, me, 0, keepdims=False), me) for s in src]
        ffn["send"], ffn["recv"], ffn["src"], ffn["land"], token = _exchange_start(src, land, False, src[0],
                                                                                  "exchange_ffn_start")
        return token[:1, :1]

    ncol = N_MOD * D // N_DEV
    b_slab = lax.dynamic_slice(b_ada, (0, me * ncol), (1, ncol))
    slab = _mod_slab(cond_all, w_ada[0], b_slab)
    (mod_rows,) = _all_to_all([slab.reshape(N_DEV, nb, ncol)], "scatter_mod")
    mod = jnp.transpose(mod_rows, (1, 0, 2)).reshape(nb, N_MOD, D)

    grad_x, dmod, loss_cols, small, big = _local_step(
        xt, mod, tt, g_norm1_t, _pad_w_in(w_in_f), g_cq, _pad_w_uq(w_uq_f), g_ckv, w_k, w_v, rel_bias, g_out_a, g_out_b,
        None, g_norm2, None, None, g_final.reshape(1, D), late_weights=late_weights, on_ffn_grads=on_ffn_grads)

    slabs = [
        _cols_to_blocks(_unpad_w_in(big["w_in"])), _cols_to_blocks(_unpad_w_uq(big["w_uq"])),
        _cols_to_blocks(_join_w_ukv(big["w_k"], big["w_v"])), big["w_out"].reshape(N_DEV, D // N_DEV, D),
    ]
    dmod_blocks = jnp.transpose(dmod.reshape(nb, N_DEV, ncol), (1, 0, 2))
    got = _all_to_all([dmod_blocks] + [s.astype(BF16) for s in slabs], "exchange_grads")
    got = list(got) + _exchange_wait(ffn["send"], ffn["recv"], ffn["src"], ffn["land"], False, got[0], "exchange_ffn_wait")
    g_ada = _ada_grad(cond_all, got[0].reshape(N_DEV * nb, ncol))

    mine = _pack([jnp.sum(dmod, axis=0)] + [small[n] for n, _ in SMALL[1:]] + [loss_cols])
    (parts,) = _all_gather([mine], "gather_small")
    small_w = [b_ada, g_norm1, g_cq, g_ckv, rel_bias, g_out_a, g_out_b, g_norm2, g_final]
    small_m = [m_b_ada, m_g_norm1, m_g_cq, m_g_ckv, m_rel_bias, m_g_out_a, m_g_out_b, m_g_norm2, m_g_final]
    small_v = [v_b_ada, v_g_norm1, v_g_cq, v_g_ckv, v_rel_bias, v_g_out_a, v_g_out_b, v_g_norm2, v_g_final]
    sg, sd, sm, sv, loss8 = _small_update(parts, _pack(small_w), _pack(small_m), _pack(small_v))
    shapes = [w.shape for w in small_w]
    sg, sd, sm, sv = (_unpack(p, shapes) for p in (sg, sd, sm, sv))

    big_w = [w_in, w_uq, w_ukv, w_out, w_ffn_in, w_ffn_out]
    big_m = [m_w_in, m_w_uq, m_w_ukv, m_w_out, m_w_ffn_in, m_w_ffn_out]
    big_v = [v_w_in, v_w_uq, v_w_ukv, v_w_out, v_w_ffn_in, v_w_ffn_out]
    names = ["w_in", "w_uq", "w_ukv", "w_out", "w_ffn_in", "w_ffn_out"]
    upd = {"w_ada": _adamw(g_ada[None], w_ada[0], m_w_ada[0], v_w_ada[0], "adamw_w_ada")}
    for n, p, w, m, v in zip(names, got[1:], big_w, big_m, big_v):
        upd[n] = _adamw(p, w[0], m[0], v[0], "adamw_" + n)
    for i, (n, _) in enumerate(SMALL):
        upd[n] = (sg[i], sd[i], sm[i], sv[i])

    order = ["w_ada", "b_ada", "g_norm1", "w_in", "g_cq", "w_uq", "g_ckv", "w_ukv", "rel_bias", "g_out_a", "g_out_b",
             "w_out", "g_norm2", "w_ffn_in", "w_ffn_out", "g_final"]
    like = dict(w_ada=w_ada, w_in=w_in, w_uq=w_uq, w_ukv=w_ukv, w_out=w_out, w_ffn_in=w_ffn_in, w_ffn_out=w_ffn_out)
    outs = [loss8[0, 0], grad_x.reshape(x.shape)]
    for part in range(4):
        for n in order:
            val = upd[n][part]
            outs.append(val.reshape(like[n].shape) if n in like else val)
    return tuple(outs)
```

```python
import functools

import numpy as np
import jax
import jax.numpy as jnp
from jax import lax
from jax.experimental import pallas as pl
from jax.experimental.pallas import tpu as pltpu

F32, BF16 = jnp.float32, jnp.bfloat16

N_DEV = 8
D = 1024
S = 2048
H = 8
E_A = 64
D_A = H * E_A
Q_LORA, KV_LORA = 384, 256
NOPE, ROPE, VDIM = 64, 32, 64
HP = 128
P_IN = 3 * D_A + Q_LORA + KV_LORA + ROPE
P_PAD = 3 * D_A + Q_LORA + KV_LORA + HP
TAIL0 = 3 * D_A
TAIL = P_PAD - TAIL0
D_FF = 2816
N_MOD = 6
EPS = 1e-6
NEG = -1e30
BLK = 128
DILATIONS = (1, 4, 16)
N_BUCKETS, MAX_DISTANCE = 32, 2048
ROPE_THETA = 10000.0
SCALE_A = E_A ** -0.5
SCALE_B = (NOPE + ROPE) ** -0.5
B1, B2, LR, ADAM_EPS, WD, STEP = 0.9, 0.999, 0.001, 1e-8, 0.01, 10
VMEM_LIMIT = 56 * 1024 * 1024


def _cp(*sem):
    return pltpu.CompilerParams(dimension_semantics=sem, vmem_limit_bytes=VMEM_LIMIT)


def _pick(n, prefs):
    for p in prefs:
        if n % p == 0:
            return p
    raise ValueError(f"no tile of {prefs} divides {n}")


OPERAND_BYTES = 6 * 1024 * 1024


def _pick_rows(m, k):
    return _pick(m, [p for p in (1024, 512, 256, 128, 16) if p * k * 2 <= OPERAND_BYTES])


def _dot(a, b, dims):
    return lax.dot_general(a, b, (dims, ((), ())), preferred_element_type=F32)


def _mm_nn(a, b, out_dtype, name):
    m, k = a.shape
    n = b.shape[1]
    tm, tn = _pick_rows(m, k), _pick(n, (512, 384, 256, 128))

    def body(a_ref, b_ref, o_ref):
        o_ref[...] = _dot(a_ref[...], b_ref[...], ((1,), (0,))).astype(o_ref.dtype)

    return pl.pallas_call(
        body, name=name, grid=(m // tm, n // tn),
        in_specs=[pl.BlockSpec((tm, k), lambda i, j: (i, 0)), pl.BlockSpec((k, tn), lambda i, j: (0, j))],
        out_specs=pl.BlockSpec((tm, tn), lambda i, j: (i, j)),
        out_shape=jax.ShapeDtypeStruct((m, n), out_dtype),
        compiler_params=_cp("parallel", "parallel"),
    )(a, b)


def _mm_nt(a, b, out_dtype, name):
    m, k = a.shape
    n = b.shape[0]
    tm, tn = _pick_rows(m, k), _pick(n, (512, 384, 256, 128))

    def body(a_ref, b_ref, o_ref):
        o_ref[...] = _dot(a_ref[...], b_ref[...], ((1,), (1,))).astype(o_ref.dtype)

    return pl.pallas_call(
        body, name=name, grid=(m // tm, n // tn),
        in_specs=[pl.BlockSpec((tm, k), lambda i, j: (i, 0)), pl.BlockSpec((tn, k), lambda i, j: (j, 0))],
        out_specs=pl.BlockSpec((tm, tn), lambda i, j: (i, j)),
        out_shape=jax.ShapeDtypeStruct((m, n), out_dtype),
        compiler_params=_cp("parallel", "parallel"),
    )(a, b)


def _mm_tn(a, b, name):
    t, m = a.shape
    n = b.shape[1]
    tm, tn, tc = _pick(m, (512, 384, 256, 128)), _pick(n, (512, 384, 256, 128)), _pick(t, (512, 16))

    def body(a_ref, b_ref, o_ref, at_ref):
        @pl.when(pl.program_id(1) == 0)
        def _():
            def chunk(c, _):
                rows = pl.ds(pl.multiple_of(c * tc, tc), tc)
                at_ref[:, rows] = a_ref[rows, :].T
                return 0

            lax.fori_loop(0, t // tc, chunk, 0)

        o_ref[...] = _dot(at_ref[...], b_ref[...], ((1,), (0,)))

    return pl.pallas_call(
        body, name=name, grid=(m // tm, n // tn),
        in_specs=[pl.BlockSpec((t, tm), lambda i, j: (0, i)), pl.BlockSpec((t, tn), lambda i, j: (0, j))],
        out_specs=pl.BlockSpec((tm, tn), lambda i, j: (i, j)),
        out_shape=jax.ShapeDtypeStruct((m, n), F32),
        scratch_shapes=[pltpu.VMEM((tm, t), BF16)],
        compiler_params=_cp("parallel", "arbitrary"),
    )(a, b)


TM = 256


def _row(w):
    return pl.BlockSpec((TM, w), lambda i: (i, 0))


def _row_at(w, col):
    return pl.BlockSpec((TM, w), lambda i: (i, col))


def _vec(w):
    return pl.BlockSpec((1, w), lambda i: (0, 0))


def _per_ex(w):
    return pl.BlockSpec((1, 1, w), lambda i: (i // (S // TM), 0, 0))


def _pos(w):
    return pl.BlockSpec((TM, w), lambda i: (i % (S // TM), 0))


def _full(shape):
    return pl.BlockSpec(shape, lambda i: (0,) * len(shape))


def _rms(x):
    return lax.rsqrt(jnp.mean(x * x, axis=-1, keepdims=True) + EPS)


def _rms_bwd(n, r, dn):
    return r * (dn - n * jnp.mean(dn * n, axis=-1, keepdims=True))


def _colsum(v):
    return jnp.sum(v, axis=0, keepdims=True)


def _acc_first(i, ref, val, every=None):
    first = (i == 0) if every is None else (i % every == 0)

    @pl.when(first)
    def _():
        ref[...] = jnp.zeros_like(ref)

    ref[...] += val.reshape(ref.shape)


def _pre1(x, g, sc, sh):
    t = x.shape[0]

    def body(x_ref, g_ref, sc_ref, sh_ref, h_ref):
        xv = x_ref[...]
        n = xv * _rms(xv)
        h_ref[...] = ((n * g_ref[...]) * (1.0 + sc_ref[0]) + sh_ref[0]).astype(BF16)

    return pl.pallas_call(
        body, name="pre1", grid=(t // TM,),
        in_specs=[_row(D), _vec(D), _per_ex(D), _per_ex(D)],
        out_specs=_row(D), out_shape=jax.ShapeDtypeStruct((t, D), BF16),
        compiler_params=_cp("parallel"),
    )(x, g, sc, sh)


def _rope_fwd(v, c, sm, sp):
    return v * c + pltpu.roll(v, HP - ROPE // 2, 1) * sm + pltpu.roll(v, ROPE // 2, 1) * sp


def _rope_bwd(dv, c, sm, sp):
    return dv * c + pltpu.roll(dv * sm, ROPE // 2, 1) + pltpu.roll(dv * sp, HP - ROPE // 2, 1)


def _mla_pre(proj, g_cq, g_ckv, w_uq, w_k, w_v, rc, rsm, rsp):
    t = proj.shape[0]

    def body(tail_ref, gq_ref, gkv_ref, wuq_ref, wk_ref, wv_ref, c_ref, sm_ref, sp_ref,
             q_ref, k_ref, v_ref, cqn_ref, ckvn_ref):
        tail = tail_ref[...]
        cq, ckv, kr = tail[:, :Q_LORA], tail[:, Q_LORA:Q_LORA + KV_LORA], tail[:, Q_LORA + KV_LORA:]
        cqn = (cq * _rms(cq) * gq_ref[...]).astype(BF16)
        ckvn = (ckv * _rms(ckv) * gkv_ref[...]).astype(BF16)
        cqn_ref[...] = cqn
        ckvn_ref[...] = ckvn
        c, sm, sp = c_ref[...], sm_ref[...], sp_ref[...]
        q = _dot(cqn, wuq_ref[...], ((1,), (0,)))
        kn = _dot(ckvn, wk_ref[...], ((1,), (0,)))
        v_ref[...] = _dot(ckvn, wv_ref[...], ((1,), (0,))).astype(BF16)
        krr = _rope_fwd(kr, c, sm, sp)
        for h in range(H):
            sl = slice(h * HP, (h + 1) * HP)
            q_ref[:, sl] = _rope_fwd(q[:, sl], c, sm, sp).astype(BF16)
            k_ref[:, sl] = (kn[:, sl] + krr).astype(BF16)

    wide = H * HP
    return pl.pallas_call(
        body, name="mla_pre", grid=(t // TM,),
        in_specs=[_row_at(TAIL, TAIL0 // TAIL), _vec(Q_LORA), _vec(KV_LORA), _full((Q_LORA, wide)),
                  _full((KV_LORA, wide)), _full((KV_LORA, wide)), _pos(HP), _pos(HP), _pos(HP)],
        out_specs=[_row(wide), _row(wide), _row(wide), _row(Q_LORA), _row(KV_LORA)],
        out_shape=[jax.ShapeDtypeStruct((t, wide), BF16)] * 3
        + [jax.ShapeDtypeStruct((t, Q_LORA), BF16), jax.ShapeDtypeStruct((t, KV_LORA), BF16)],
        compiler_params=_cp("parallel"),
    )(proj, g_cq, g_ckv, w_uq, w_k, w_v, rc, rsm, rsp)


def _mla_pre_bwd(proj, dq_, dk_, dv_, g_cq, g_ckv, w_uq, w_k, w_v, rc, rsm, rsp):
    t = proj.shape[0]
    wide = H * HP

    def body(tail_ref, dq_ref, dk_ref, dv_ref, gq_ref, gkv_ref, wuq_ref, wk_ref, wv_ref, c_ref, sm_ref, sp_ref,
             dqo_ref, dko_ref, dvo_ref, dtail_ref, dgq_ref, dgkv_ref):
        i = pl.program_id(0)
        tail = tail_ref[...]
        cq, ckv = tail[:, :Q_LORA], tail[:, Q_LORA:Q_LORA + KV_LORA]
        c, sm, sp = c_ref[...], sm_ref[...], sp_ref[...]
        dkr = jnp.zeros((TM, HP), F32)
        for h in range(H):
            sl = slice(h * HP, (h + 1) * HP)
            dqo_ref[:, sl] = _rope_bwd(dq_ref[:, sl], c, sm, sp).astype(BF16)
            dkr = dkr + dk_ref[:, sl]
        lane = lax.broadcasted_iota(jnp.int32, (TM, HP), 1)
        dkr = jnp.where((lane >= NOPE) & (lane < NOPE + ROPE), _rope_bwd(dkr, c, sm, sp), 0.0)
        dkb = dk_ref[...].astype(BF16)
        dvb = dv_ref[...].astype(BF16)
        dko_ref[...] = dkb
        dvo_ref[...] = dvb
        dcqn = _dot(dqo_ref[...], wuq_ref[...], ((1,), (1,)))
        dckvn = _dot(dkb, wk_ref[...], ((1,), (1,))) + _dot(dvb, wv_ref[...], ((1,), (1,)))
        rq, rkv = _rms(cq), _rms(ckv)
        nq, nkv = cq * rq, ckv * rkv
        _acc_first(i, dgq_ref, _colsum(dcqn * nq))
        _acc_first(i, dgkv_ref, _colsum(dckvn * nkv))
        dtail_ref[:, :Q_LORA] = _rms_bwd(nq, rq, dcqn * gq_ref[...]).astype(BF16)
        dtail_ref[:, Q_LORA:Q_LORA + KV_LORA] = _rms_bwd(nkv, rkv, dckvn * gkv_ref[...]).astype(BF16)
        dtail_ref[:, Q_LORA + KV_LORA:] = dkr.astype(BF16)

    return pl.pallas_call(
        body, name="mla_pre_bwd", grid=(t // TM,),
        in_specs=[_row_at(TAIL, TAIL0 // TAIL), _row(wide), _row(wide), _row(wide), _vec(Q_LORA), _vec(KV_LORA),
                  _full((Q_LORA, wide)), _full((KV_LORA, wide)), _full((KV_LORA, wide)), _pos(HP), _pos(HP), _pos(HP)],
        out_specs=[_row(wide), _row(wide), _row(wide), _row(TAIL), _vec(Q_LORA), _vec(KV_LORA)],
        out_shape=[jax.ShapeDtypeStruct((t, wide), BF16)] * 3 + [jax.ShapeDtypeStruct((t, TAIL), BF16),
                   jax.ShapeDtypeStruct((1, Q_LORA), F32), jax.ShapeDtypeStruct((1, KV_LORA), F32)],
        compiler_params=_cp("arbitrary"),
    )(proj, dq_, dk_, dv_, g_cq, g_ckv, w_uq, w_k, w_v, rc, rsm, rsp)


def _post_attn(out_a, out_b, g_a, g_b):
    t = out_a.shape[0]

    def body(a_ref, b_ref, ga_ref, gb_ref, y_ref):
        a, b = a_ref[...], b_ref[...]
        y_ref[:, :D_A] = (a * _rms(a) * ga_ref[...]).astype(BF16)
        y_ref[:, D_A:] = (b * _rms(b) * gb_ref[...]).astype(BF16)

    return pl.pallas_call(
        body, name="post_attn", grid=(t // TM,),
        in_specs=[_row(D_A), _row(D_A), _vec(D_A), _vec(D_A)],
        out_specs=_row(D), out_shape=jax.ShapeDtypeStruct((t, D), BF16),
        compiler_params=_cp("parallel"),
    )(out_a, out_b, g_a, g_b)


def _post_attn_bwd(dy, out_a, out_b, g_a, g_b):
    t = dy.shape[0]

    def body(dy_ref, a_ref, b_ref, ga_ref, gb_ref, da_ref, db_ref, dga_ref, dgb_ref):
        i = pl.program_id(0)
        dy_ = dy_ref[...]
        for src, g_ref, dst, dg_ref, sl in ((a_ref, ga_ref, da_ref, dga_ref, slice(0, D_A)),
                                            (b_ref, gb_ref, db_ref, dgb_ref, slice(D_A, D))):
            v = src[...]
            r = _rms(v)
            n = v * r
            dyv = dy_[:, sl]
            _acc_first(i, dg_ref, _colsum(dyv * n))
            dst[...] = _rms_bwd(n, r, dyv * g_ref[...])

    return pl.pallas_call(
        body, name="post_attn_bwd", grid=(t // TM,),
        in_specs=[_row(D), _row(D_A), _row(D_A), _vec(D_A), _vec(D_A)],
        out_specs=[_row(D_A), _row(D_A), _vec(D_A), _vec(D_A)],
        out_shape=[jax.ShapeDtypeStruct((t, D_A), F32)] * 2 + [jax.ShapeDtypeStruct((1, D_A), F32)] * 2,
        compiler_params=_cp("arbitrary"),
    )(dy, out_a, out_b, g_a, g_b)


def _resid_norm2(x, mix, g1, g, sc, sh):
    t = x.shape[0]

    def body(x_ref, mix_ref, g1_ref, g_ref, sc_ref, sh_ref, x2_ref, h_ref):
        x2 = x_ref[...] + g1_ref[0] * mix_ref[...]
        x2_ref[...] = x2
        n = x2 * _rms(x2)
        h_ref[...] = ((n * g_ref[...]) * (1.0 + sc_ref[0]) + sh_ref[0]).astype(BF16)

    return pl.pallas_call(
        body, name="resid_norm2", grid=(t // TM,),
        in_specs=[_row(D), _row(D), _per_ex(D), _vec(D), _per_ex(D), _per_ex(D)],
        out_specs=[_row(D), _row(D)],
        out_shape=[jax.ShapeDtypeStruct((t, D), F32), jax.ShapeDtypeStruct((t, D), BF16)],
        compiler_params=_cp("parallel"),
    )(x, mix, g1, g, sc, sh)


def _sigmoid(v):
    return 1.0 / (1.0 + jnp.exp(-v))


def _swiglu(gu):
    t = gu.shape[0]

    def body(g_ref, u_ref, a_ref):
        g = g_ref[...].astype(F32)
        a_ref[...] = (g * _sigmoid(g) * u_ref[...].astype(F32)).astype(BF16)

    return pl.pallas_call(
        body, name="swiglu", grid=(t // TM,),
        in_specs=[_row_at(D_FF, 0), _row_at(D_FF, 1)],
        out_specs=_row(D_FF), out_shape=jax.ShapeDtypeStruct((t, D_FF), BF16),
        compiler_params=_cp("parallel"),
    )(gu, gu)


def _swiglu_bwd(gu, da):
    t = gu.shape[0]

    def body(g_ref, u_ref, da_ref, dgu_ref):
        j = pl.program_id(1)
        g = g_ref[...].astype(F32)
        sg = _sigmoid(g)
        dav = da_ref[...].astype(F32)

        @pl.when(j == 0)
        def _():
            dgu_ref[...] = (dav * u_ref[...].astype(F32) * (sg * (1.0 + g * (1.0 - sg)))).astype(BF16)

        @pl.when(j == 1)
        def _():
            dgu_ref[...] = (dav * (g * sg)).astype(BF16)

    return pl.pallas_call(
        body, name="swiglu_bwd", grid=(t // TM, 2),
        in_specs=[pl.BlockSpec((TM, D_FF), lambda i, j: (i, 0)), pl.BlockSpec((TM, D_FF), lambda i, j: (i, 1)),
                  pl.BlockSpec((TM, D_FF), lambda i, j: (i, 0))],
        out_specs=pl.BlockSpec((TM, D_FF), lambda i, j: (i, j)),
        out_shape=jax.ShapeDtypeStruct((t, 2 * D_FF), BF16),
        compiler_params=_cp("parallel", "arbitrary"),
    )(gu, gu, da)


def _final(x2, f, g2, g_fin, target):
    t = x2.shape[0]
    nb = t // S
    tpb = S // TM

    def body(x2_ref, f_ref, g2_ref, g_ref, t_ref, dx3_ref, df_ref, loss_ref, dgf_ref, dg2_ref):
        i = pl.program_id(0)
        fv = f_ref[...]
        x3 = x2_ref[...] + g2_ref[0] * fv
        r = _rms(x3)
        n = x3 * r
        err = n * g_ref[...] - t_ref[...]
        _acc_first(i, loss_ref, _colsum(err * err))
        dy = err * (1.0 / D)
        _acc_first(i, dgf_ref, _colsum(dy * n))
        dx3 = _rms_bwd(n, r, dy * g_ref[...])
        dx3_ref[...] = dx3
        _acc_first(i, dg2_ref, _colsum(dx3 * fv), every=tpb)
        df_ref[...] = (dx3 * g2_ref[0]).astype(BF16)

    return pl.pallas_call(
        body, name="final", grid=(t // TM,),
        in_specs=[_row(D), _row(D), _per_ex(D), _vec(D), _row(D)],
        out_specs=[_row(D), _row(D), _vec(D), _vec(D), _per_ex(D)],
        out_shape=[jax.ShapeDtypeStruct((t, D), F32), jax.ShapeDtypeStruct((t, D), BF16),
                   jax.ShapeDtypeStruct((1, D), F32), jax.ShapeDtypeStruct((1, D), F32),
                   jax.ShapeDtypeStruct((nb, 1, D), F32)],
        compiler_params=_cp("arbitrary"),
    )(x2, f, g2, g_fin, target)


def _norm_bwd(xin, dh, dres, g, sc, gate=None):
    t = xin.shape[0]
    nb = t // S
    tpb = S // TM
    gated = gate is not None

    def body(*refs):
        if gated:
            x_ref, dh_ref, dres_ref, g_ref, sc_ref, mix_ref, g1_ref, dx_ref, dsh_ref, dsc_ref, dg_ref, dg1_ref, dmix_ref = refs
        else:
            x_ref, dh_ref, dres_ref, g_ref, sc_ref, dx_ref, dsh_ref, dsc_ref, dg_ref = refs
        i = pl.program_id(0)
        xv, dhv = x_ref[...], dh_ref[...]
        r = _rms(xv)
        n = xv * r
        gv = g_ref[...]
        _acc_first(i, dsh_ref, _colsum(dhv), every=tpb)
        _acc_first(i, dsc_ref, _colsum(dhv * (n * gv)), every=tpb)
        dng = dhv * (1.0 + sc_ref[0])
        _acc_first(i, dg_ref, _colsum(dng * n))
        dx = dres_ref[...] + _rms_bwd(n, r, dng * gv)
        dx_ref[...] = dx
        if gated:
            _acc_first(i, dg1_ref, _colsum(dx * mix_ref[...]), every=tpb)
            dmix_ref[...] = (dx * g1_ref[0]).astype(BF16)

    in_specs = [_row(D), _row(D), _row(D), _vec(D), _per_ex(D)]
    out_specs = [_row(D), _per_ex(D), _per_ex(D), _vec(D)]
    out_shape = [jax.ShapeDtypeStruct((t, D), F32), jax.ShapeDtypeStruct((nb, 1, D), F32),
                 jax.ShapeDtypeStruct((nb, 1, D), F32), jax.ShapeDtypeStruct((1, D), F32)]
    args = [xin, dh, dres, g, sc]
    if gated:
        in_specs += [_row(D), _per_ex(D)]
        out_specs += [_per_ex(D), _row(D)]
        out_shape += [jax.ShapeDtypeStruct((nb, 1, D), F32), jax.ShapeDtypeStruct((t, D), BF16)]
        args += list(gate)
    return pl.pallas_call(
        body, name="norm2_bwd" if gated else "norm1_bwd", grid=(t // TM,),
        in_specs=in_specs, out_specs=out_specs, out_shape=out_shape,
        compiler_params=_cp("arbitrary"),
    )(*args)


TQ = 256
TB = 512


def _mla_fwd(q, k, v):
    t = q.shape[0]
    nb = t // S

    def body(q_ref, k_ref, v_ref, o_ref, lse_ref):
        causal = lax.broadcasted_iota(jnp.int32, (TB, TB), 0) >= lax.broadcasted_iota(jnp.int32, (TB, TB), 1)
        heads = [slice(h * HP, (h + 1) * HP) for h in range(2)]
        for i in range(S // TB):
            ri, past = slice(i * TB, (i + 1) * TB), slice(0, i * TB)
            qhs = [q_ref[ri, sl] for sl in heads]
            sd = [jnp.where(causal, _dot(qh, k_ref[ri, sl], ((1,), (1,))) * SCALE_B, NEG) for qh, sl in zip(qhs, heads)]
            ms = [jnp.max(s, axis=-1, keepdims=True) for s in sd]
            if i:
                so = [_dot(qh, k_ref[past, sl], ((1,), (1,))) * SCALE_B for qh, sl in zip(qhs, heads)]
                ms = [jnp.maximum(m, jnp.max(s, axis=-1, keepdims=True)) for m, s in zip(ms, so)]
            pd = [jnp.exp(s - m) for s, m in zip(sd, ms)]
            ls = [jnp.sum(p, axis=-1, keepdims=True) for p in pd]
            acc = [_dot(p.astype(BF16), v_ref[ri, sl], ((1,), (0,))) for p, sl in zip(pd, heads)]
            if i:
                po = [jnp.exp(s - m) for s, m in zip(so, ms)]
                ls = [l + jnp.sum(p, axis=-1, keepdims=True) for l, p in zip(ls, po)]
                acc = [a + _dot(p.astype(BF16), v_ref[past, sl], ((1,), (0,))) for a, p, sl in zip(acc, po, heads)]
            o_ref[ri, :] = acc[0] / ls[0] + acc[1] / ls[1]
            for sl, m, l in zip(heads, ms, ls):
                lse_ref[ri, sl] = jnp.broadcast_to(m + jnp.log(l), (TB, HP))

    wide2 = pl.BlockSpec((S, 2 * HP), lambda b, p: (b, p))
    return pl.pallas_call(
        body, name="mla_fwd", grid=(nb, H // 2),
        in_specs=[wide2, wide2, wide2],
        out_specs=[pl.BlockSpec((S, HP), lambda b, p: (b, p)), wide2],
        out_shape=[jax.ShapeDtypeStruct((t, H * VDIM), F32), jax.ShapeDtypeStruct((t, H * HP), F32)],
        compiler_params=_cp("parallel", "parallel"),
    )(q, k, v)


def _mla_bwd(q, k, v, o, do, lse):
    t = q.shape[0]
    nb = t // S
    nq = S // TQ

    def body(q_ref, k_ref, v_ref, o_ref, do_ref, lse_ref, dq_ref, dk_ref, dv_ref):
        lane = lax.broadcasted_iota(jnp.int32, (TB, HP), 1)
        causal = lax.broadcasted_iota(jnp.int32, (TB, TB), 0) >= lax.broadcasted_iota(jnp.int32, (TB, TB), 1)
        heads = [slice(h * HP, (h + 1) * HP) for h in range(2)]
        nblk = S // TB
        for i in reversed(range(nblk)):
            ri, past = slice(i * TB, (i + 1) * TB), slice(0, i * TB)
            dov = do_ref[ri, :]
            prod = dov * o_ref[ri, :]
            dob = dov.astype(BF16)
            deltas = [jnp.sum(jnp.where((lane < VDIM) if h == 0 else (lane >= VDIM), prod, 0.0), axis=-1, keepdims=True)
                      for h in range(2)]
            qhs = [q_ref[ri, sl] for sl in heads]
            lses = [lse_ref[ri, sl][:, :1] for sl in heads]
            for rows, diagonal in ((ri, True), (past, False)):
                if rows.stop == rows.start:
                    continue
                ps = [jnp.exp(_dot(qh, k_ref[rows, sl], ((1,), (1,))) * SCALE_B - lse) for qh, sl, lse in zip(qhs, heads, lses)]
                if diagonal:
                    ps = [jnp.where(causal, p, 0.0) for p in ps]
                dps = [_dot(dob, v_ref[rows, sl], ((1,), (1,))) for sl in heads]
                dss = [(p * (dp - delta) * SCALE_B).astype(BF16) for p, dp, delta in zip(ps, dps, deltas)]
                for sl, qh, p, ds in zip(heads, qhs, ps, dss):
                    dq = _dot(ds, k_ref[rows, sl], ((1,), (0,)))
                    dk = _dot(ds, qh, ((0,), (0,)))
                    dv = _dot(p.astype(BF16), dob, ((0,), (0,)))
                    if diagonal:
                        dq_ref[ri, sl] = dq
                    else:
                        dq_ref[ri, sl] += dq
                    if i == nblk - 1:
                        dk_ref[rows, sl] = dk
                        dv_ref[rows, sl] = dv
                    else:
                        dk_ref[rows, sl] += dk
                        dv_ref[rows, sl] += dv

    wide2 = pl.BlockSpec((S, 2 * HP), lambda b, p: (b, p))
    pair = pl.BlockSpec((S, HP), lambda b, p: (b, p))
    return pl.pallas_call(
        body, name="mla_bwd", grid=(nb, H // 2),
        in_specs=[wide2, wide2, wide2, pair, pair, wide2],
        out_specs=[wide2, wide2, wide2],
        out_shape=[jax.ShapeDtypeStruct((t, H * HP), F32)] * 3,
        compiler_params=_cp("parallel", "parallel"),
    )(q, k, v, o, do, lse)


def _t5_bucket(dist):
    max_exact = N_BUCKETS // 2
    d = np.maximum(dist, 1).astype(np.float64)
    large = max_exact + (np.log(d / max_exact) / np.log(MAX_DISTANCE / max_exact) * (N_BUCKETS - max_exact)).astype(np.int64)
    large = np.minimum(large, N_BUCKETS - 1)
    return np.where(dist < max_exact, dist, large).astype(np.int32)


def _band_geometry():
    a = np.arange(BLK)[:, None]
    bk = np.arange(2 * BLK)[None, :]
    steps = BLK + a - bk
    valid = (steps >= 0) & (steps <= BLK)
    buckets = np.stack([_t5_bucket(np.clip(steps, 0, BLK) * d) for d in DILATIONS])
    return buckets, valid


def _band_bias(rel_bias):
    buckets, valid = _band_geometry()
    onehot = (jnp.asarray(buckets)[..., None] == jnp.arange(N_BUCKETS)).astype(F32)
    bias = jnp.einsum("rqkn,nh->rhqk", onehot, rel_bias, precision=lax.Precision.HIGHEST)
    bias = jnp.where(jnp.asarray(valid)[None, None], bias, NEG)
    return bias.reshape(3, H // 2, 2 * BLK, 2 * BLK)


def _dil_items():
    items = []
    for r, d in enumerate(DILATIONS):
        for res in range(d):
            for blk in range(S // d // BLK):
                items.append((r, d, blk * BLK * d + res, blk > 0))
    return items


GROUP = 4


def _strided(start, d):
    return pl.ds(start, BLK) if d == 1 else pl.ds(start, BLK, stride=d)


def _stack_heads(tile, own):
    return jnp.where(own, jnp.concatenate([tile, tile], axis=0), 0.0).astype(BF16)


def _own_lanes():
    row = lax.broadcasted_iota(jnp.int32, (2 * BLK, HP), 0)
    lane = lax.broadcasted_iota(jnp.int32, (2 * BLK, HP), 1)
    return (lane < E_A) == (row < BLK)


def _dil_fwd(proj, biasm):
    t = proj.shape[0]
    nb = t // S

    def body(q_ref, k_ref, v_ref, b_ref, o_ref, lse_ref, ob_ref, lb_ref):
        lane = lax.broadcasted_iota(jnp.int32, (BLK, HP), 1)
        own = _own_lanes()
        items = _dil_items()
        for g in range(0, len(items), GROUP):
            grp = items[g:g + GROUP]
            ss, vts = [], []
            for r, d, start, has_prev in grp:
                cur = _strided(start, d)
                rows = [_strided(start - BLK * d, d), cur] if has_prev else [cur]
                q2 = _stack_heads(q_ref[cur, :], own)
                kt = jnp.concatenate([k_ref[x, :] for x in rows], axis=0).astype(BF16)
                vts.append(jnp.concatenate([v_ref[x, :] for x in rows], axis=0).astype(BF16))
                bias = b_ref[r, 0] if has_prev else b_ref[r, 0, :, BLK:]
                ss.append(_dot(q2, kt, ((1,), (1,))) * SCALE_A + bias)
            ms = [jnp.max(s, axis=-1, keepdims=True) for s in ss]
            ps = [jnp.exp(s - m) for s, m in zip(ss, ms)]
            ls = [jnp.sum(p, axis=-1, keepdims=True) for p in ps]
            for (r, d, start, _), p, vt, m, l in zip(grp, ps, vts, ms, ls):
                cur = _strided(start, d)
                o2 = _dot(p.astype(BF16), vt, ((1,), (0,))) / l
                lse2 = m + jnp.log(l)
                ob_ref[r, cur, :] = jnp.where(lane < E_A, o2[:BLK], o2[BLK:])
                lb_ref[r, cur, :] = jnp.where(lane < E_A, lse2[:BLK], lse2[BLK:])

        def merge(c, _):
            rows = pl.ds(pl.multiple_of(c * TQ, TQ), TQ)
            l0, l1, l2 = lb_ref[0, rows, :], lb_ref[1, rows, :], lb_ref[2, rows, :]
            m = jnp.maximum(jnp.maximum(l0, l1), l2)
            e0, e1, e2 = jnp.exp(l0 - m), jnp.exp(l1 - m), jnp.exp(l2 - m)
            tot = e0 + e1 + e2
            o_ref[rows, :] = (e0 * ob_ref[0, rows, :] + e1 * ob_ref[1, rows, :] + e2 * ob_ref[2, rows, :]) / tot
            lse_ref[rows, :] = m + jnp.log(tot)
            return 0

        lax.fori_loop(0, S // TQ, merge, 0)

    npair = H // 2
    return pl.pallas_call(
        body, name="dil_fwd", grid=(nb, npair),
        in_specs=[pl.BlockSpec((S, HP), lambda b, p: (b, p)), pl.BlockSpec((S, HP), lambda b, p: (b, npair + p)),
                  pl.BlockSpec((S, HP), lambda b, p: (b, 2 * npair + p)),
                  pl.BlockSpec((3, 1, 2 * BLK, 2 * BLK), lambda b, p: (0, p, 0, 0))],
        out_specs=[pl.BlockSpec((S, HP), lambda b, p: (b, p))] * 2,
        out_shape=[jax.ShapeDtypeStruct((t, D_A), F32)] * 2,
        scratch_shapes=[pltpu.VMEM((3, S, HP), F32), pltpu.VMEM((3, S, HP), F32)],
        compiler_params=_cp("parallel", "parallel"),
    )(proj, proj, proj, biasm)


def _dil_bwd(proj, biasm, o, do, lse):
    t = proj.shape[0]
    nb = t // S

    def body(q_ref, k_ref, v_ref, b_ref, o_ref, do_ref, lse_ref, dq_ref, dk_ref, dv_ref, ds_ref):
        dq_ref[...] = jnp.zeros_like(dq_ref)
        dk_ref[...] = jnp.zeros_like(dk_ref)
        dv_ref[...] = jnp.zeros_like(dv_ref)
        ds_ref[...] = jnp.zeros_like(ds_ref)
        lane = lax.broadcasted_iota(jnp.int32, (BLK, HP), 1)
        own = _own_lanes()
        items = _dil_items()
        for g in range(0, len(items), GROUP):
            grp = items[g:g + GROUP]
            q2s, kts, do2s, ss, dps, lse2s, delta2s = [], [], [], [], [], [], []
            for r, d, start, has_prev in grp:
                cur = _strided(start, d)
                rows = [_strided(start - BLK * d, d), cur] if has_prev else [cur]
                q2 = _stack_heads(q_ref[cur, :], own)
                kt = jnp.concatenate([k_ref[x, :] for x in rows], axis=0).astype(BF16)
                vt = jnp.concatenate([v_ref[x, :] for x in rows], axis=0).astype(BF16)
                dot_ = do_ref[cur, :]
                prod = dot_ * o_ref[cur, :]
                lset = lse_ref[cur, :]
                do2 = _stack_heads(dot_, own)
                bias = b_ref[r, 0] if has_prev else b_ref[r, 0, :, BLK:]
                ss.append(_dot(q2, kt, ((1,), (1,))) * SCALE_A + bias)
                dps.append(_dot(do2, vt, ((1,), (1,))))
                lse2s.append(jnp.concatenate([lset[:, :1], lset[:, E_A:E_A + 1]], axis=0))
                delta2s.append(jnp.concatenate([jnp.sum(jnp.where(lane < E_A, prod, 0.0), axis=-1, keepdims=True),
                                                jnp.sum(jnp.where(lane >= E_A, prod, 0.0), axis=-1, keepdims=True)], axis=0))
                q2s.append(q2)
                kts.append(kt)
                do2s.append(do2)
            ps = [jnp.exp(s - lse2) for s, lse2 in zip(ss, lse2s)]
            dls = [p * (dp - delta2) for p, dp, delta2 in zip(ps, dps, delta2s)]
            for (r, d, start, has_prev), q2, kt, do2, p, dl in zip(grp, q2s, kts, do2s, ps, dls):
                cur = _strided(start, d)
                dsb = (dl * SCALE_A).astype(BF16)
                dq2 = _dot(dsb, kt, ((1,), (0,)))
                dkt = _dot(dsb, q2, ((0,), (0,)))
                dvt = _dot(p.astype(BF16), do2, ((0,), (0,)))
                dq_ref[cur, :] += jnp.where(lane < E_A, dq2[:BLK], dq2[BLK:])
                if has_prev:
                    prev = _strided(start - BLK * d, d)
                    ds_ref[0, r, 0] += dl
                    dk_ref[prev, :] += dkt[:BLK]
                    dv_ref[prev, :] += dvt[:BLK]
                    dk_ref[cur, :] += dkt[BLK:]
                    dv_ref[cur, :] += dvt[BLK:]
                else:
                    ds_ref[0, r, 0, :, BLK:] += dl
                    dk_ref[cur, :] += dkt
                    dv_ref[cur, :] += dvt

    npair = H // 2
    pair = pl.BlockSpec((S, HP), lambda b, p: (b, p))
    return pl.pallas_call(
        body, name="dil_bwd", grid=(nb, npair),
        in_specs=[pair, pl.BlockSpec((S, HP), lambda b, p: (b, npair + p)),
                  pl.BlockSpec((S, HP), lambda b, p: (b, 2 * npair + p)),
                  pl.BlockSpec((3, 1, 2 * BLK, 2 * BLK), lambda b, p: (0, p, 0, 0)), pair, pair, pair],
        out_specs=[pair, pair, pair, pl.BlockSpec((1, 3, 1, 2 * BLK, 2 * BLK), lambda b, p: (b, 0, p, 0, 0))],
        out_shape=[jax.ShapeDtypeStruct((t, D_A), F32)] * 3 + [jax.ShapeDtypeStruct((nb, 3, npair, 2 * BLK, 2 * BLK), F32)],
        compiler_params=_cp("parallel", "parallel"),
    )(proj, proj, proj, biasm, o, do, lse)


def _rel_bias_grad(dlogits):
    nb = dlogits.shape[0]
    buckets, _ = _band_geometry()
    kk = 3 * BLK * 2 * BLK
    dl = jnp.transpose(dlogits.reshape(nb, 3, H, BLK, 2 * BLK), (0, 2, 1, 3, 4)).reshape(nb, H, kk)
    bk = jnp.asarray(buckets.reshape(1, kk))
    tk = kk // 12

    def body(dl_ref, bk_ref, o_ref):
        j = pl.program_id(0)
        onehot = (bk_ref[...] == lax.broadcasted_iota(jnp.int32, (N_BUCKETS, tk), 0)).astype(F32)
        tot = dl_ref[0]
        for b in range(1, nb):
            tot = tot + dl_ref[b]
        part = lax.dot_general(onehot, tot, ((((1,), (1,))), ((), ())), preferred_element_type=F32,
                               precision=lax.Precision.HIGHEST)
        _acc_first(j, o_ref, part)

    return pl.pallas_call(
        body, name="rel_bias_grad", grid=(kk // tk,),
        in_specs=[pl.BlockSpec((nb, H, tk), lambda j: (0, 0, j)), pl.BlockSpec((1, tk), lambda j: (0, j))],
        out_specs=pl.BlockSpec((N_BUCKETS, H), lambda j: (0, 0)),
        out_shape=jax.ShapeDtypeStruct((N_BUCKETS, H), F32),
        compiler_params=_cp("arbitrary"),
    )(dl, bk)


def _mesh_place():
    x, y, c = lax.axis_index("x"), lax.axis_index("y"), lax.axis_index("c")
    return x, y, c


def _peer(k):
    x, y, c = _mesh_place()
    px = 1 - x if k & 4 else x
    py = 1 - y if k & 2 else y
    pc = 1 - c if k & 1 else c
    return (px, py, pc), 4 * px + 2 * py + pc


ANY = pl.BlockSpec(memory_space=pl.ANY)


def _exchange(arrays, gathers, name, after=None):
    n_arr = len(arrays)

    def body(*refs):
        ins, outs = refs[:n_arr], refs[n_arr + 1:2 * n_arr + 1]
        send, recv, loc = refs[2 * n_arr + 1:]
        x, y, c = _mesh_place()
        me = 4 * x + 2 * y + c
        local = [pltpu.make_async_copy(ins[a] if gathers[a] else ins[a].at[me], outs[a].at[me], loc.at[a])
                 for a in range(n_arr)]
        remote = _peer_copies(ins, outs, send, recv, gathers)
        for cp in local:
            cp.start()
        for put, _ in remote:
            put.start()
        for cp in local:
            cp.wait()
        for put, got in remote:
            put.wait_send()
            got.wait_recv()

    return pl.pallas_call(
        body, name=name,
        in_specs=[ANY] * (n_arr + 1), out_specs=[ANY] * n_arr,
        out_shape=[jax.ShapeDtypeStruct(((N_DEV,) if g else ()) + a.shape, a.dtype) for a, g in zip(arrays, gathers)],
        scratch_shapes=[pltpu.SemaphoreType.DMA((n_arr * (N_DEV - 1),)), pltpu.SemaphoreType.DMA((n_arr * (N_DEV - 1),)),
                        pltpu.SemaphoreType.DMA((n_arr,))],
        compiler_params=pltpu.CompilerParams(has_side_effects=True),
    )(*arrays, arrays[0] if after is None else after)


HBM = pl.BlockSpec(memory_space=pltpu.HBM)
SEM = pl.BlockSpec(memory_space=pltpu.SEMAPHORE)
DATAFLOW = pltpu.SideEffectType.DATAFLOW_SIDE_EFFECTING


def _own_block_in_place(block, me):
    land = jnp.zeros((N_DEV,) + block.shape, block.dtype)
    return lax.dynamic_update_slice(land, block[None], (me,) + (0,) * block.ndim)


def _peer_copies(srcs, lands, send, recv, gathers):
    x, y, c = _mesh_place()
    me = 4 * x + 2 * y + c
    out = []
    for a, (src, land) in enumerate(zip(srcs, lands)):
        for k in range(1, N_DEV):
            dev, idx = _peer(k)
            sem = a * (N_DEV - 1) + k - 1
            mine = src if gathers[a] else src.at[idx]
            put = pltpu.make_async_remote_copy(mine, land.at[me], send.at[sem], recv.at[sem],
                                               device_id=dev, device_id_type=pl.DeviceIdType.MESH)
            got = pltpu.make_async_remote_copy(mine, land.at[idx], send.at[sem], recv.at[sem],
                                               device_id=dev, device_id_type=pl.DeviceIdType.MESH)
            out.append((put, got))
    return out


def _exchange_start(srcs, lands, gather, after, name):
    n = len(srcs)

    def body(*refs):
        srcs_, lands_, send, recv = refs[:n], refs[n:2 * n], refs[2 * n + 1], refs[2 * n + 2]
        for put, _ in _peer_copies(srcs_, lands_, send, recv, gather):
            put.start()
        refs[-1][...] = jnp.zeros_like(refs[-1])

    nsem = n * (N_DEV - 1)
    thru = [pltpu.HBM(a.shape, a.dtype) for a in list(srcs) + list(lands)]
    res = pl.pallas_call(
        body, name=name,
        out_shape=(pltpu.SemaphoreType.DMA((nsem,)), pltpu.SemaphoreType.DMA((nsem,)), *thru, jax.ShapeDtypeStruct((8, 128), F32)),
        in_specs=[HBM] * (2 * n) + [ANY],
        out_specs=(SEM, SEM, *([HBM] * (2 * n)), pl.BlockSpec(memory_space=pltpu.VMEM)),
        input_output_aliases={i: 2 + i for i in range(2 * n)},
        compiler_params=pltpu.CompilerParams(has_side_effects=DATAFLOW),
    )(*[pltpu.with_memory_space_constraint(a, pltpu.HBM) for a in list(srcs) + list(lands)], after)
    return res[0], res[1], list(res[2:2 + n]), list(res[2 + n:2 + 2 * n]), res[-1]


def _exchange_wait(send, recv, srcs, lands, gather, after, name):
    n = len(srcs)

    def body(*refs):
        srcs_, lands_, send_, recv_ = refs[:n], refs[n:2 * n], refs[2 * n], refs[2 * n + 1]
        for put, got in _peer_copies(srcs_, lands_, send_, recv_, gather):
            put.wait_send()
            got.wait_recv()

    thru = [pltpu.HBM(a.shape, a.dtype) for a in list(srcs) + list(lands)]
    res = pl.pallas_call(
        body, name=name, out_shape=tuple(thru),
        in_specs=[HBM] * (2 * n) + [SEM, SEM, ANY], out_specs=tuple([HBM] * (2 * n)),
        input_output_aliases={i: i for i in range(2 * n)},
        compiler_params=pltpu.CompilerParams(has_side_effects=DATAFLOW),
    )(*srcs, *lands, send, recv, after)
    return list(res[n:])


def _silu_rows(c):
    def body(c_ref, o_ref):
        v = c_ref[...]
        o_ref[...] = v * _sigmoid(v)

    return pl.pallas_call(body, name="cond", out_shape=jax.ShapeDtypeStruct(c.shape, F32))(c)


def _mod_slab(cond_all, w_ada, b_slab):
    def body(c_ref, w_ref, b_ref, o_ref):
        o_ref[...] = _dot(c_ref[...].astype(BF16), w_ref[...].astype(BF16), ((1,), (0,))) + b_ref[...]

    return pl.pallas_call(body, name="mod_slab",
                          out_shape=jax.ShapeDtypeStruct((cond_all.shape[0], w_ada.shape[1]), F32),
                          compiler_params=pltpu.CompilerParams(vmem_limit_bytes=VMEM_LIMIT))(cond_all, w_ada, b_slab)


def _ada_grad(cond_all, dmod_cols):
    def body(c_ref, d_ref, o_ref):
        o_ref[...] = _dot(c_ref[...].astype(BF16), d_ref[...].astype(BF16), ((0,), (0,)))

    return pl.pallas_call(body, name="ada_grad",
                          out_shape=jax.ShapeDtypeStruct((cond_all.shape[1], dmod_cols.shape[1]), F32),
                          compiler_params=pltpu.CompilerParams(vmem_limit_bytes=VMEM_LIMIT))(cond_all, dmod_cols)


def _adam_math(g, w, m, v):
    m2 = B1 * m + (1.0 - B1) * g
    v2 = B2 * v + (1.0 - B2) * (g * g)
    m_hat = m2 / (1.0 - B1 ** STEP)
    v_hat = v2 / (1.0 - B2 ** STEP)
    return -LR * (m_hat / (jnp.sqrt(v_hat) + ADAM_EPS) + WD * w), m2, v2


def _adamw(parts, w, m, v, name):
    n, rows, cols = parts.shape
    tr = _pick(rows, (128, 96, 64, 32, 16, 8))

    def body(p_ref, w_ref, m_ref, v_ref, g_ref, d_ref, m2_ref, v2_ref):
        g = p_ref[0].astype(F32)
        for s in range(1, n):
            g = g + p_ref[s].astype(F32)
        g_ref[...] = g
        d_ref[...], m2_ref[...], v2_ref[...] = _adam_math(g, w_ref[...], m_ref[...], v_ref[...])

    blk = pl.BlockSpec((tr, cols), lambda i: (i, 0))
    return pl.pallas_call(
        body, name=name, grid=(rows // tr,),
        in_specs=[pl.BlockSpec((n, tr, cols), lambda i: (0, i, 0)), blk, blk, blk],
        out_specs=[blk] * 4, out_shape=[jax.ShapeDtypeStruct((rows, cols), F32)] * 4,
        compiler_params=_cp("parallel"),
    )(parts, w, m, v)


SMALL = (("b_ada", N_MOD * D), ("g_norm1", D), ("g_cq", Q_LORA), ("g_ckv", KV_LORA), ("rel_bias", N_BUCKETS * H),
         ("g_out_a", D_A), ("g_out_b", D_A), ("g_norm2", D), ("g_final", D))
TILE = 8 * 128


def _tiles(n):
    return -(-n // TILE) * 8


SMALL_ROWS = sum(_tiles(n) for _, n in SMALL)
LOSS_ROWS = 8
PACK_ROWS = SMALL_ROWS + LOSS_ROWS


def _pack(vals):
    parts = []
    for v in vals:
        v = v.reshape(-1)
        rows = _tiles(v.shape[0])
        parts.append(jnp.pad(v, (0, rows * 128 - v.shape[0])).reshape(rows, 128))
    return jnp.concatenate(parts, axis=0)


def _unpack(packed, shapes):
    out, r = [], 0
    for (_, n), shp in zip(SMALL, shapes):
        rows = _tiles(n)
        out.append(packed[r:r + rows].reshape(-1)[:n].reshape(shp))
        r += rows
    return out


def _small_update(parts, w, m, v):
    def body(p_ref, w_ref, m_ref, v_ref, g_ref, d_ref, m2_ref, v2_ref, loss_ref):
        tot = p_ref[0]
        for s in range(1, N_DEV):
            tot = tot + p_ref[s]
        g = tot[:SMALL_ROWS]
        g_ref[...] = g
        d_ref[...], m2_ref[...], v2_ref[...] = _adam_math(g, w_ref[...], m_ref[...], v_ref[...])
        loss_ref[...] = jnp.broadcast_to((0.5 / D) * jnp.sum(tot[SMALL_ROWS:]), loss_ref.shape)

    return pl.pallas_call(
        body, name="small_update",
        out_shape=[jax.ShapeDtypeStruct((SMALL_ROWS, 128), F32)] * 4 + [jax.ShapeDtypeStruct((8, 128), F32)],
    )(parts, w, m, v)


def _cols_from_blocks(g):
    return jnp.transpose(g, (1, 0, 2)).reshape(g.shape[1], N_DEV * g.shape[2])


def _cols_to_blocks(w):
    r, c = w.shape
    return jnp.transpose(w.reshape(r, N_DEV, c // N_DEV), (1, 0, 2))


def _pad_w_in(w):
    z = jnp.zeros((w.shape[0], NOPE), w.dtype)
    return jnp.concatenate([w[:, :P_IN - ROPE], z, w[:, P_IN - ROPE:], z[:, :HP - NOPE - ROPE]], axis=1)


def _unpad_w_in(g):
    k0 = P_IN - ROPE + NOPE
    return jnp.concatenate([g[:, :P_IN - ROPE], g[:, k0:k0 + ROPE]], axis=1)


def _pad_w_uq(w):
    w3 = w.reshape(Q_LORA, H, NOPE + ROPE)
    return jnp.pad(w3, ((0, 0), (0, 0), (0, HP - NOPE - ROPE))).reshape(Q_LORA, H * HP)


def _unpad_w_uq(g):
    return g.reshape(Q_LORA, H, HP)[:, :, :NOPE + ROPE].reshape(Q_LORA, H * (NOPE + ROPE))


def _split_w_ukv(w):
    w4 = w.reshape(KV_LORA, H // 2, 2, HP)
    z = jnp.zeros((KV_LORA, H // 2, NOPE), w.dtype)
    kn, vv = w4[..., :NOPE], w4[..., NOPE:]
    w_k = jnp.stack([jnp.concatenate([kn[:, :, 0], z], -1), jnp.concatenate([kn[:, :, 1], z], -1)], axis=2)
    w_v = jnp.stack([jnp.concatenate([vv[:, :, 0], z], -1), jnp.concatenate([z, vv[:, :, 1]], -1)], axis=2)
    return w_k.reshape(KV_LORA, H * HP), w_v.reshape(KV_LORA, H * HP)


def _join_w_ukv(g_k, g_v):
    gk = g_k.reshape(KV_LORA, H // 2, 2, HP)
    gv = g_v.reshape(KV_LORA, H // 2, 2, HP)
    even = jnp.concatenate([gk[:, :, 0, :NOPE], gv[:, :, 0, :VDIM]], -1)
    odd = jnp.concatenate([gk[:, :, 1, :NOPE], gv[:, :, 1, VDIM:]], -1)
    return jnp.stack([even, odd], axis=2).reshape(KV_LORA, H * HP)


def _rope_tables():
    half = ROPE // 2
    inv = ROPE_THETA ** (-jnp.arange(half, dtype=F32) / half)
    ang = jnp.arange(S, dtype=F32)[:, None] * inv[None, :]
    cos, sin = jnp.cos(ang), jnp.sin(ang)
    ones, zeros = jnp.ones((S, NOPE), F32), jnp.zeros((S, NOPE), F32)
    tail1, tail0 = jnp.ones((S, HP - NOPE - ROPE), F32), jnp.zeros((S, HP - NOPE - ROPE), F32)
    zh = jnp.zeros((S, half), F32)
    c = jnp.concatenate([ones, cos, cos, tail1], axis=1)
    sm = jnp.concatenate([zeros, -sin, zh, tail0], axis=1)
    sp = jnp.concatenate([zeros, zh, sin, tail0], axis=1)
    return c, sm, sp


def _local_step(x, mod, target, g_norm1, w_in_p, g_cq, w_uq_p, g_ckv, w_k, w_v, rel_bias, g_out_a, g_out_b, w_out,
                g_norm2, w_ffn_in, w_ffn_out, g_final, late_weights=None, on_ffn_grads=None, on_last_grads=None):
    nb = x.shape[0] // S
    sh1, sc1, g1, sh2, sc2, g2 = (mod[:, n].reshape(nb, 1, D) for n in range(N_MOD))
    rc, rsm, rsp = _rope_tables()
    biasm = _band_bias(rel_bias)

    h1 = _pre1(x, g_norm1, sc1, sh1)
    proj = _mm_nn(h1, w_in_p, F32, "proj")
    q, k, v, cqn, ckvn = _mla_pre(proj, g_cq, g_ckv, w_uq_p, w_k, w_v, rc, rsm, rsp)
    out_b, lse_b = _mla_fwd(q, k, v)
    out_a, lse_a = _dil_fwd(proj, biasm)
    y = _post_attn(out_a, out_b, g_out_a, g_out_b)
    if late_weights is not None:
        w_out, w_ffn_in, w_ffn_out = late_weights(y)
    mix = _mm_nn(y, w_out, F32, "mix")
    x2, h2 = _resid_norm2(x, mix, g1, g_norm2, sc2, sh2)
    gu = _mm_nn(h2, w_ffn_in, BF16, "ffn_in")
    act = _swiglu(gu)
    f = _mm_nn(act, w_ffn_out, F32, "ffn_out")
    dx3, df, loss_cols, dg_final, dg2 = _final(x2, f, g2, g_final, target)

    dact = _mm_nt(df, w_ffn_out, BF16, "d_act")
    gw_ffn_out = _mm_tn(act, df, "gw_ffn_out")
    dgu = _swiglu_bwd(gu, dact)
    dh2 = _mm_nt(dgu, w_ffn_in, F32, "d_h2")
    gw_ffn_in = _mm_tn(h2, dgu, "gw_ffn_in")
    if on_ffn_grads is not None:
        g_norm2 = g_norm2 + on_ffn_grads(gw_ffn_in, gw_ffn_out)
    dx2, dsh2, dsc2, dg_norm2, dg1, dmix = _norm_bwd(x2, dh2, dx3, g_norm2, sc2, gate=(mix, g1))
    dy = _mm_nt(dmix, w_out, F32, "d_y")
    gw_out = _mm_tn(y, dmix, "gw_out")
    dout_a, dout_b, dg_out_a, dg_out_b = _post_attn_bwd(dy, out_a, out_b, g_out_a, g_out_b)
    dq_b, dk_b, dv_b = _mla_bwd(q, k, v, out_b, dout_b, lse_b)
    dq_a, dk_a, dv_a, dlogits = _dil_bwd(proj, biasm, out_a, dout_a, lse_a)
    g_rel = _rel_bias_grad(dlogits)
    dqr, dkr, dvr, dtail, dg_cq, dg_ckv = _mla_pre_bwd(proj, dq_b, dk_b, dv_b, g_cq, g_ckv, w_uq_p, w_k, w_v, rc, rsm, rsp)
    gw_uq = _mm_tn(cqn, dqr, "gw_uq")
    gw_k = _mm_tn(ckvn, dkr, "gw_k")
    gw_v = _mm_tn(ckvn, dvr, "gw_v")
    dproj = jnp.concatenate([dq_a.astype(BF16), dk_a.astype(BF16), dv_a.astype(BF16), dtail], axis=1)
    dh1 = _mm_nt(dproj, w_in_p, F32, "d_h1")
    gw_in = _mm_tn(h1, dproj, "gw_in")
    if on_last_grads is not None:
        g_norm1 = g_norm1 + on_last_grads(dict(w_in=gw_in, w_uq=gw_uq, w_k=gw_k, w_v=gw_v, w_out=gw_out))
    grad_x, dsh1, dsc1, dg_norm1 = _norm_bwd(x, dh1, dx2, g_norm1, sc1)

    dmod = jnp.concatenate([dsh1, dsc1, dg1, dsh2, dsc2, dg2], axis=1)
    small = dict(g_norm1=dg_norm1, g_cq=dg_cq, g_ckv=dg_ckv, rel_bias=g_rel, g_out_a=dg_out_a, g_out_b=dg_out_b,
                 g_norm2=dg_norm2, g_final=dg_final)
    big = dict(w_in=gw_in, w_uq=gw_uq, w_k=gw_k, w_v=gw_v, w_out=gw_out, w_ffn_in=gw_ffn_in, w_ffn_out=gw_ffn_out)
    return grad_x, dmod, loss_cols, small, big


def kernel(x, c, w_ada, b_ada, g_norm1, w_in, g_cq, w_uq, g_ckv, w_ukv, rel_bias, g_out_a, g_out_b, w_out, g_norm2, w_ffn_in, w_ffn_out, g_final, loss_target, m_w_ada, m_b_ada, m_g_norm1, m_w_in, m_g_cq, m_w_uq, m_g_ckv, m_w_ukv, m_rel_bias, m_g_out_a, m_g_out_b, m_w_out, m_g_norm2, m_w_ffn_in, m_w_ffn_out, m_g_final, v_w_ada, v_b_ada, v_g_norm1, v_w_in, v_g_cq, v_w_uq, v_g_ckv, v_w_ukv, v_rel_bias, v_g_out_a, v_g_out_b, v_w_out, v_g_norm2, v_w_ffn_in, v_w_ffn_out, v_g_final):
    nb = x.shape[0]
    t = nb * S
    xt, tt = x.reshape(t, D), loss_target.reshape(t, D)
    me = 4 * lax.axis_index("x") + 2 * lax.axis_index("y") + lax.axis_index("c")

    early = [w_in[0], w_uq[0], w_ukv[0]]
    gathered = _exchange([_silu_rows(c)] + [s.astype(BF16) for s in early], [True] * 4, "gather_weights")
    cond_all = gathered[0].reshape(N_DEV * nb, D)
    w_in_f, w_uq_f, w_ukv_f = (_cols_from_blocks(g) for g in gathered[1:4])
    w_k, w_v = _split_w_ukv(w_ukv_f)

    ncol = N_MOD * D // N_DEV
    b_slab = lax.dynamic_slice(b_ada, (0, me * ncol), (1, ncol))
    slab = _mod_slab(cond_all, w_ada[0], b_slab)
    (mod_rows,) = _exchange([slab.reshape(N_DEV, nb, ncol)], [False], "scatter_mod")
    mod = jnp.transpose(mod_rows, (1, 0, 2)).reshape(nb, N_MOD, D)

    late = [s.astype(BF16) for s in (w_out[0], w_ffn_in[0], w_ffn_out[0])]
    late_send, late_recv, late_src, late_land, late_token = _exchange_start(
        late, [_own_block_in_place(s, me) for s in late], [True] * 3, mod_rows, "gather_late_start")
    g_norm1_t = g_norm1 + late_token[:1, :1]

    def late_weights(after):
        w_out_g, w_ffn_in_g, w_ffn_out_g = _exchange_wait(late_send, late_recv, late_src, late_land, [True] * 3, after,
                                                          "gather_late_wait")
        return w_out_g.reshape(D, D), _cols_from_blocks(w_ffn_in_g), w_ffn_out_g.reshape(D_FF, D)

    flight = {}

    def start_grads(key, slabs, name):
        src = [s.astype(BF16) for s in slabs]
        land = [_own_block_in_place(lax.dynamic_index_in_dim(s, me, 0, keepdims=False), me) for s in src]
        send, recv, src, land, token = _exchange_start(src, land, [False] * len(src), src[0], name)
        flight[key] = (send, recv, src, land)
        return token[:1, :1]

    def on_ffn_grads(gw_ffn_in, gw_ffn_out):
        return start_grads("ffn", [_cols_to_blocks(gw_ffn_in), gw_ffn_out.reshape(N_DEV, D_FF // N_DEV, D)],
                           "exchange_ffn_start")

    def on_last_grads(gw):
        return start_grads("rest", [_cols_to_blocks(_unpad_w_in(gw["w_in"])), _cols_to_blocks(_unpad_w_uq(gw["w_uq"])),
                                    _cols_to_blocks(_join_w_ukv(gw["w_k"], gw["w_v"])),
                                    gw["w_out"].reshape(N_DEV, D // N_DEV, D)], "exchange_rest_start")

    grad_x, dmod, loss_cols, small, _ = _local_step(
        xt, mod, tt, g_norm1_t, _pad_w_in(w_in_f), g_cq, _pad_w_uq(w_uq_f), g_ckv, w_k, w_v, rel_bias, g_out_a, g_out_b,
        None, g_norm2, None, None, g_final.reshape(1, D), late_weights=late_weights, on_ffn_grads=on_ffn_grads,
        on_last_grads=on_last_grads)

    upd = {}

    def land_and_update(key, names, after, name):
        got = _exchange_wait(*flight[key], [False] * len(names), after, name)
        for n, p in zip(names, got):
            w, m, v = big[n]
            upd[n] = _adamw(p, w[0], m[0], v[0], "adamw_" + n)

    big = dict(w_in=(w_in, m_w_in, v_w_in), w_uq=(w_uq, m_w_uq, v_w_uq), w_ukv=(w_ukv, m_w_ukv, v_w_ukv),
               w_out=(w_out, m_w_out, v_w_out), w_ffn_in=(w_ffn_in, m_w_ffn_in, v_w_ffn_in),
               w_ffn_out=(w_ffn_out, m_w_ffn_out, v_w_ffn_out))
    land_and_update("ffn", ["w_ffn_in", "w_ffn_out"], grad_x, "exchange_ffn_wait")
    land_and_update("rest", ["w_in", "w_uq", "w_ukv", "w_out"], upd["w_ffn_out"][0], "exchange_rest_wait")

    dmod_blocks = jnp.transpose(dmod.reshape(nb, N_DEV, ncol), (1, 0, 2))
    mine = _pack([jnp.sum(dmod, axis=0)] + [small[n] for n, _ in SMALL[1:]] + [loss_cols])
    dmod_cols, parts = _exchange([dmod_blocks, mine], [False, True], "exchange_small", after=upd["w_out"][0])
    g_ada = _ada_grad(cond_all, dmod_cols.reshape(N_DEV * nb, ncol))
    upd["w_ada"] = _adamw(g_ada[None], w_ada[0], m_w_ada[0], v_w_ada[0], "adamw_w_ada")
    small_w = [b_ada, g_norm1, g_cq, g_ckv, rel_bias, g_out_a, g_out_b, g_norm2, g_final]
    small_m = [m_b_ada, m_g_norm1, m_g_cq, m_g_ckv, m_rel_bias, m_g_out_a, m_g_out_b, m_g_norm2, m_g_final]
    small_v = [v_b_ada, v_g_norm1, v_g_cq, v_g_ckv, v_rel_bias, v_g_out_a, v_g_out_b, v_g_norm2, v_g_final]
    sg, sd, sm, sv, loss8 = _small_update(parts, _pack(small_w), _pack(small_m), _pack(small_v))
    shapes = [w.shape for w in small_w]
    sg, sd, sm, sv = (_unpack(p, shapes) for p in (sg, sd, sm, sv))
    for i, (n, _) in enumerate(SMALL):
        upd[n] = (sg[i], sd[i], sm[i], sv[i])

    order = ["w_ada", "b_ada", "g_norm1", "w_in", "g_cq", "w_uq", "g_ckv", "w_ukv", "rel_bias", "g_out_a", "g_out_b",
             "w_out", "g_norm2", "w_ffn_in", "w_ffn_out", "g_final"]
    like = dict(w_ada=w_ada, w_in=w_in, w_uq=w_uq, w_ukv=w_ukv, w_out=w_out, w_ffn_in=w_ffn_in, w_ffn_out=w_ffn_out)
    outs = [loss8[0, 0], grad_x.reshape(x.shape)]
    for part in range(4):
        for n in order:
            val = upd[n][part]
            outs.append(val.reshape(like[n].shape) if n in like else val)
    return tuple(outs)
```

```python
import functools

import numpy as np
import jax
import jax.numpy as jnp
from jax import lax
from jax.experimental import pallas as pl
from jax.experimental.pallas import tpu as pltpu

F32, BF16 = jnp.float32, jnp.bfloat16

N_DEV = 8
D = 1024
S = 2048
H = 8
E_A = 64
D_A = H * E_A
Q_LORA, KV_LORA = 384, 256
NOPE, ROPE, VDIM = 64, 32, 64
HP = 128
P_IN = 3 * D_A + Q_LORA + KV_LORA + ROPE
P_PAD = 3 * D_A + Q_LORA + KV_LORA + HP
TAIL0 = 3 * D_A
TAIL = P_PAD - TAIL0
D_FF = 2816
N_MOD = 6
EPS = 1e-6
NEG = -1e30
BLK = 128
DILATIONS = (1, 4, 16)
N_BUCKETS, MAX_DISTANCE = 32, 2048
ROPE_THETA = 10000.0
SCALE_A = E_A ** -0.5
SCALE_B = (NOPE + ROPE) ** -0.5
B1, B2, LR, ADAM_EPS, WD, STEP = 0.9, 0.999, 0.001, 1e-8, 0.01, 10
VMEM_LIMIT = 56 * 1024 * 1024


def _cp(*sem):
    return pltpu.CompilerParams(dimension_semantics=sem, vmem_limit_bytes=VMEM_LIMIT)


def _pick(n, prefs):
    for p in prefs:
        if n % p == 0:
            return p
    raise ValueError(f"no tile of {prefs} divides {n}")


OPERAND_BYTES = 6 * 1024 * 1024


def _pick_rows(m, k):
    return _pick(m, [p for p in (1024, 512, 256, 128, 16) if p * k * 2 <= OPERAND_BYTES])


def _dot(a, b, dims):
    return lax.dot_general(a, b, (dims, ((), ())), preferred_element_type=F32)


def _mm_nn(a, b, out_dtype, name):
    m, k = a.shape
    n = b.shape[1]
    tm, tn = _pick_rows(m, k), _pick(n, (512, 384, 256, 128))

    def body(a_ref, b_ref, o_ref):
        o_ref[...] = _dot(a_ref[...], b_ref[...], ((1,), (0,))).astype(o_ref.dtype)

    return pl.pallas_call(
        body, name=name, grid=(m // tm, n // tn),
        in_specs=[pl.BlockSpec((tm, k), lambda i, j: (i, 0)), pl.BlockSpec((k, tn), lambda i, j: (0, j))],
        out_specs=pl.BlockSpec((tm, tn), lambda i, j: (i, j)),
        out_shape=jax.ShapeDtypeStruct((m, n), out_dtype),
        compiler_params=_cp("parallel", "parallel"),
    )(a, b)


def _mm_nt(a, b, out_dtype, name):
    m, k = a.shape
    n = b.shape[0]
    tm, tn = _pick_rows(m, k), _pick(n, (512, 384, 256, 128))

    def body(a_ref, b_ref, o_ref):
        o_ref[...] = _dot(a_ref[...], b_ref[...], ((1,), (1,))).astype(o_ref.dtype)

    return pl.pallas_call(
        body, name=name, grid=(m // tm, n // tn),
        in_specs=[pl.BlockSpec((tm, k), lambda i, j: (i, 0)), pl.BlockSpec((tn, k), lambda i, j: (j, 0))],
        out_specs=pl.BlockSpec((tm, tn), lambda i, j: (i, j)),
        out_shape=jax.ShapeDtypeStruct((m, n), out_dtype),
        compiler_params=_cp("parallel", "parallel"),
    )(a, b)


def _mm_tn(a, bs, name):
    t, m = a.shape
    n = bs[0].shape[1]
    nb_ = len(bs)
    tm, tn, tc = _pick(m, (512, 384, 256, 128)), _pick(n, (512, 384, 256, 128)), _pick(t, (512, 16))

    def body(*refs):
        a_ref, b_refs, o_refs, at_ref = refs[0], refs[1:1 + nb_], refs[1 + nb_:1 + 2 * nb_], refs[-1]

        @pl.when(pl.program_id(1) == 0)
        def _():
            def chunk(c, _):
                rows = pl.ds(pl.multiple_of(c * tc, tc), tc)
                at_ref[:, rows] = a_ref[rows, :].T
                return 0

            lax.fori_loop(0, t // tc, chunk, 0)

        for b_ref, o_ref in zip(b_refs, o_refs):
            o_ref[...] = _dot(at_ref[...], b_ref[...], ((1,), (0,))).astype(BF16)

    res = pl.pallas_call(
        body, name=name, grid=(m // tm, n // tn),
        in_specs=[pl.BlockSpec((t, tm), lambda i, j: (0, i))] + [pl.BlockSpec((t, tn), lambda i, j: (0, j))] * nb_,
        out_specs=[pl.BlockSpec((tm, tn), lambda i, j: (i, j))] * nb_,
        out_shape=[jax.ShapeDtypeStruct((m, n), BF16)] * nb_,
        scratch_shapes=[pltpu.VMEM((tm, t), BF16)],
        compiler_params=_cp("parallel", "arbitrary"),
    )(a, *bs)
    return res[0] if nb_ == 1 else res


def _silu_parts(g):
    sg = 1.0 / (1.0 + jnp.exp(-g))
    return sg, g * sg


def _ffn_in(h2, w):
    t, k = h2.shape
    tm, tn = _pick_rows(t, k), _pick(D_FF, (256, 128))
    nj = D_FF // tn

    def body(h_ref, wg_ref, wu_ref, g_ref, u_ref, a_ref):
        hv = h_ref[...]
        g = _dot(hv, wg_ref[...], ((1,), (0,)))
        u = _dot(hv, wu_ref[...], ((1,), (0,)))
        gb, ub = g.astype(BF16), u.astype(BF16)
        g_ref[...] = gb
        u_ref[...] = ub
        a_ref[...] = (_silu_parts(gb.astype(F32))[1] * ub.astype(F32)).astype(BF16)

    blk = pl.BlockSpec((tm, tn), lambda i, j: (i, j))
    return pl.pallas_call(
        body, name="ffn_in", grid=(t // tm, nj),
        in_specs=[pl.BlockSpec((tm, k), lambda i, j: (i, 0)), pl.BlockSpec((k, tn), lambda i, j: (0, j)),
                  pl.BlockSpec((k, tn), lambda i, j: (0, j + nj))],
        out_specs=[blk] * 3, out_shape=[jax.ShapeDtypeStruct((t, D_FF), BF16)] * 3,
        compiler_params=_cp("parallel", "parallel"),
    )(h2, w, w)


def _d_act(df, w, g, u):
    t, k = df.shape
    tm, tn = _pick_rows(t, k), _pick(D_FF, (256, 128))

    def body(df_ref, w_ref, g_ref, u_ref, dg_ref, du_ref):
        da = _dot(df_ref[...], w_ref[...], ((1,), (1,)))
        gv = g_ref[...].astype(F32)
        sg, silu = _silu_parts(gv)
        dg_ref[...] = (da * u_ref[...].astype(F32) * (sg * (1.0 + gv * (1.0 - sg)))).astype(BF16)
        du_ref[...] = (da * silu).astype(BF16)

    blk = pl.BlockSpec((tm, tn), lambda i, j: (i, j))
    return pl.pallas_call(
        body, name="d_act", grid=(t // tm, D_FF // tn),
        in_specs=[pl.BlockSpec((tm, k), lambda i, j: (i, 0)), pl.BlockSpec((tn, k), lambda i, j: (j, 0)), blk, blk],
        out_specs=[blk] * 2, out_shape=[jax.ShapeDtypeStruct((t, D_FF), BF16)] * 2,
        compiler_params=_cp("parallel", "parallel"),
    )(df, w, g, u)


def _d_h2(dg, du, w):
    t = dg.shape[0]
    n = w.shape[0]
    tm, tn = _pick_rows(t, 2 * D_FF), _pick(n, (512, 256, 128))

    def body(dg_ref, du_ref, wg_ref, wu_ref, o_ref):
        o_ref[...] = (_dot(dg_ref[...], wg_ref[...], ((1,), (1,))) + _dot(du_ref[...], wu_ref[...], ((1,), (1,))))

    return pl.pallas_call(
        body, name="d_h2", grid=(t // tm, n // tn),
        in_specs=[pl.BlockSpec((tm, D_FF), lambda i, j: (i, 0)), pl.BlockSpec((tm, D_FF), lambda i, j: (i, 0)),
                  pl.BlockSpec((tn, D_FF), lambda i, j: (j, 0)), pl.BlockSpec((tn, D_FF), lambda i, j: (j, 1))],
        out_specs=pl.BlockSpec((tm, tn), lambda i, j: (i, j)),
        out_shape=jax.ShapeDtypeStruct((t, n), F32),
        compiler_params=_cp("parallel", "parallel"),
    )(dg, du, w, w)


TM = 256


def _row(w):
    return pl.BlockSpec((TM, w), lambda i: (i, 0))


def _row_at(w, col):
    return pl.BlockSpec((TM, w), lambda i: (i, col))


def _vec(w):
    return pl.BlockSpec((1, w), lambda i: (0, 0))


def _per_ex(w):
    return pl.BlockSpec((1, 1, w), lambda i: (i // (S // TM), 0, 0))


def _pos(w):
    return pl.BlockSpec((TM, w), lambda i: (i % (S // TM), 0))


def _full(shape):
    return pl.BlockSpec(shape, lambda i: (0,) * len(shape))


def _rms(x):
    return lax.rsqrt(jnp.mean(x * x, axis=-1, keepdims=True) + EPS)


def _rms_bwd(n, r, dn):
    return r * (dn - n * jnp.mean(dn * n, axis=-1, keepdims=True))


def _colsum(v):
    return jnp.sum(v, axis=0, keepdims=True)


def _acc_first(i, ref, val, every=None):
    first = (i == 0) if every is None else (i % every == 0)

    @pl.when(first)
    def _():
        ref[...] = jnp.zeros_like(ref)

    ref[...] += val.reshape(ref.shape)


def _pre1(x, g, sc, sh):
    t = x.shape[0]

    def body(x_ref, g_ref, sc_ref, sh_ref, h_ref):
        xv = x_ref[...]
        n = xv * _rms(xv)
        h_ref[...] = ((n * g_ref[...]) * (1.0 + sc_ref[0]) + sh_ref[0]).astype(BF16)

    return pl.pallas_call(
        body, name="pre1", grid=(t // TM,),
        in_specs=[_row(D), _vec(D), _per_ex(D), _per_ex(D)],
        out_specs=_row(D), out_shape=jax.ShapeDtypeStruct((t, D), BF16),
        compiler_params=_cp("parallel"),
    )(x, g, sc, sh)


def _rope_fwd(v, c, sm, sp):
    return v * c + pltpu.roll(v, HP - ROPE // 2, 1) * sm + pltpu.roll(v, ROPE // 2, 1) * sp


def _rope_bwd(dv, c, sm, sp):
    return dv * c + pltpu.roll(dv * sm, ROPE // 2, 1) + pltpu.roll(dv * sp, HP - ROPE // 2, 1)


def _mla_pre(proj, g_cq, g_ckv, w_uq, w_k, w_v, rc, rsm, rsp):
    t = proj.shape[0]

    def body(tail_ref, gq_ref, gkv_ref, wuq_ref, wk_ref, wv_ref, c_ref, sm_ref, sp_ref,
             q_ref, k_ref, v_ref, cqn_ref, ckvn_ref):
        tail = tail_ref[...]
        cq, ckv, kr = tail[:, :Q_LORA], tail[:, Q_LORA:Q_LORA + KV_LORA], tail[:, Q_LORA + KV_LORA:]
        cqn = (cq * _rms(cq) * gq_ref[...]).astype(BF16)
        ckvn = (ckv * _rms(ckv) * gkv_ref[...]).astype(BF16)
        cqn_ref[...] = cqn
        ckvn_ref[...] = ckvn
        c, sm, sp = c_ref[...], sm_ref[...], sp_ref[...]
        q = _dot(cqn, wuq_ref[...], ((1,), (0,)))
        kn = _dot(ckvn, wk_ref[...], ((1,), (0,)))
        v_ref[...] = _dot(ckvn, wv_ref[...], ((1,), (0,))).astype(BF16)
        krr = _rope_fwd(kr, c, sm, sp)
        for h in range(H):
            sl = slice(h * HP, (h + 1) * HP)
            q_ref[:, sl] = _rope_fwd(q[:, sl], c, sm, sp).astype(BF16)
            k_ref[:, sl] = (kn[:, sl] + krr).astype(BF16)

    wide = H * HP
    return pl.pallas_call(
        body, name="mla_pre", grid=(t // TM,),
        in_specs=[_row_at(TAIL, TAIL0 // TAIL), _vec(Q_LORA), _vec(KV_LORA), _full((Q_LORA, wide)),
                  _full((KV_LORA, wide)), _full((KV_LORA, wide)), _pos(HP), _pos(HP), _pos(HP)],
        out_specs=[_row(wide), _row(wide), _row(wide), _row(Q_LORA), _row(KV_LORA)],
        out_shape=[jax.ShapeDtypeStruct((t, wide), BF16)] * 3
        + [jax.ShapeDtypeStruct((t, Q_LORA), BF16), jax.ShapeDtypeStruct((t, KV_LORA), BF16)],
        compiler_params=_cp("parallel"),
    )(proj, g_cq, g_ckv, w_uq, w_k, w_v, rc, rsm, rsp)


def _mla_pre_bwd(proj, dq_, dk_, dv_, g_cq, g_ckv, w_uq, w_k, w_v, rc, rsm, rsp):
    t = proj.shape[0]
    wide = H * HP

    def body(tail_ref, dq_ref, dk_ref, dv_ref, gq_ref, gkv_ref, wuq_ref, wk_ref, wv_ref, c_ref, sm_ref, sp_ref,
             dqo_ref, dko_ref, dvo_ref, dtail_ref, dgq_ref, dgkv_ref):
        i = pl.program_id(0)
        tail = tail_ref[...]
        cq, ckv = tail[:, :Q_LORA], tail[:, Q_LORA:Q_LORA + KV_LORA]
        c, sm, sp = c_ref[...], sm_ref[...], sp_ref[...]
        dkr = jnp.zeros((TM, HP), F32)
        for h in range(H):
            sl = slice(h * HP, (h + 1) * HP)
            dqo_ref[:, sl] = _rope_bwd(dq_ref[:, sl], c, sm, sp).astype(BF16)
            dkr = dkr + dk_ref[:, sl]
        lane = lax.broadcasted_iota(jnp.int32, (TM, HP), 1)
        dkr = jnp.where((lane >= NOPE) & (lane < NOPE + ROPE), _rope_bwd(dkr, c, sm, sp), 0.0)
        dkb = dk_ref[...].astype(BF16)
        dvb = dv_ref[...].astype(BF16)
        dko_ref[...] = dkb
        dvo_ref[...] = dvb
        dcqn = _dot(dqo_ref[...], wuq_ref[...], ((1,), (1,)))
        dckvn = _dot(dkb, wk_ref[...], ((1,), (1,))) + _dot(dvb, wv_ref[...], ((1,), (1,)))
        rq, rkv = _rms(cq), _rms(ckv)
        nq, nkv = cq * rq, ckv * rkv
        _acc_first(i, dgq_ref, _colsum(dcqn * nq))
        _acc_first(i, dgkv_ref, _colsum(dckvn * nkv))
        dtail_ref[:, :Q_LORA] = _rms_bwd(nq, rq, dcqn * gq_ref[...]).astype(BF16)
        dtail_ref[:, Q_LORA:Q_LORA + KV_LORA] = _rms_bwd(nkv, rkv, dckvn * gkv_ref[...]).astype(BF16)
        dtail_ref[:, Q_LORA + KV_LORA:] = dkr.astype(BF16)

    return pl.pallas_call(
        body, name="mla_pre_bwd", grid=(t // TM,),
        in_specs=[_row_at(TAIL, TAIL0 // TAIL), _row(wide), _row(wide), _row(wide), _vec(Q_LORA), _vec(KV_LORA),
                  _full((Q_LORA, wide)), _full((KV_LORA, wide)), _full((KV_LORA, wide)), _pos(HP), _pos(HP), _pos(HP)],
        out_specs=[_row(wide), _row(wide), _row(wide), _row(TAIL), _vec(Q_LORA), _vec(KV_LORA)],
        out_shape=[jax.ShapeDtypeStruct((t, wide), BF16)] * 3 + [jax.ShapeDtypeStruct((t, TAIL), BF16),
                   jax.ShapeDtypeStruct((1, Q_LORA), F32), jax.ShapeDtypeStruct((1, KV_LORA), F32)],
        compiler_params=_cp("arbitrary"),
    )(proj, dq_, dk_, dv_, g_cq, g_ckv, w_uq, w_k, w_v, rc, rsm, rsp)


def _post_attn(out_a, out_b, g_a, g_b):
    t = out_a.shape[0]

    def body(a_ref, b_ref, ga_ref, gb_ref, y_ref):
        a, b = a_ref[...], b_ref[...]
        y_ref[:, :D_A] = (a * _rms(a) * ga_ref[...]).astype(BF16)
        y_ref[:, D_A:] = (b * _rms(b) * gb_ref[...]).astype(BF16)

    return pl.pallas_call(
        body, name="post_attn", grid=(t // TM,),
        in_specs=[_row(D_A), _row(D_A), _vec(D_A), _vec(D_A)],
        out_specs=_row(D), out_shape=jax.ShapeDtypeStruct((t, D), BF16),
        compiler_params=_cp("parallel"),
    )(out_a, out_b, g_a, g_b)


def _post_attn_bwd(dy, out_a, out_b, g_a, g_b):
    t = dy.shape[0]

    def body(dy_ref, a_ref, b_ref, ga_ref, gb_ref, da_ref, db_ref, dga_ref, dgb_ref):
        i = pl.program_id(0)
        dy_ = dy_ref[...]
        for src, g_ref, dst, dg_ref, sl in ((a_ref, ga_ref, da_ref, dga_ref, slice(0, D_A)),
                                            (b_ref, gb_ref, db_ref, dgb_ref, slice(D_A, D))):
            v = src[...]
            r = _rms(v)
            n = v * r
            dyv = dy_[:, sl]
            _acc_first(i, dg_ref, _colsum(dyv * n))
            dst[...] = _rms_bwd(n, r, dyv * g_ref[...])

    return pl.pallas_call(
        body, name="post_attn_bwd", grid=(t // TM,),
        in_specs=[_row(D), _row(D_A), _row(D_A), _vec(D_A), _vec(D_A)],
        out_specs=[_row(D_A), _row(D_A), _vec(D_A), _vec(D_A)],
        out_shape=[jax.ShapeDtypeStruct((t, D_A), F32)] * 2 + [jax.ShapeDtypeStruct((1, D_A), F32)] * 2,
        compiler_params=_cp("arbitrary"),
    )(dy, out_a, out_b, g_a, g_b)


def _resid_norm2(x, mix, g1, g, sc, sh):
    t = x.shape[0]

    def body(x_ref, mix_ref, g1_ref, g_ref, sc_ref, sh_ref, x2_ref, h_ref):
        x2 = x_ref[...] + g1_ref[0] * mix_ref[...]
        x2_ref[...] = x2
        n = x2 * _rms(x2)
        h_ref[...] = ((n * g_ref[...]) * (1.0 + sc_ref[0]) + sh_ref[0]).astype(BF16)

    return pl.pallas_call(
        body, name="resid_norm2", grid=(t // TM,),
        in_specs=[_row(D), _row(D), _per_ex(D), _vec(D), _per_ex(D), _per_ex(D)],
        out_specs=[_row(D), _row(D)],
        out_shape=[jax.ShapeDtypeStruct((t, D), F32), jax.ShapeDtypeStruct((t, D), BF16)],
        compiler_params=_cp("parallel"),
    )(x, mix, g1, g, sc, sh)


def _sigmoid(v):
    return 1.0 / (1.0 + jnp.exp(-v))


def _final(x2, f, g2, g_fin, target):
    t = x2.shape[0]
    nb = t // S
    tpb = S // TM

    def body(x2_ref, f_ref, g2_ref, g_ref, t_ref, dx3_ref, df_ref, loss_ref, dgf_ref, dg2_ref):
        i = pl.program_id(0)
        fv = f_ref[...]
        x3 = x2_ref[...] + g2_ref[0] * fv
        r = _rms(x3)
        n = x3 * r
        err = n * g_ref[...] - t_ref[...]
        _acc_first(i, loss_ref, _colsum(err * err))
        dy = err * (1.0 / D)
        _acc_first(i, dgf_ref, _colsum(dy * n))
        dx3 = _rms_bwd(n, r, dy * g_ref[...])
        dx3_ref[...] = dx3
        _acc_first(i, dg2_ref, _colsum(dx3 * fv), every=tpb)
        df_ref[...] = (dx3 * g2_ref[0]).astype(BF16)

    return pl.pallas_call(
        body, name="final", grid=(t // TM,),
        in_specs=[_row(D), _row(D), _per_ex(D), _vec(D), _row(D)],
        out_specs=[_row(D), _row(D), _vec(D), _vec(D), _per_ex(D)],
        out_shape=[jax.ShapeDtypeStruct((t, D), F32), jax.ShapeDtypeStruct((t, D), BF16),
                   jax.ShapeDtypeStruct((1, D), F32), jax.ShapeDtypeStruct((1, D), F32),
                   jax.ShapeDtypeStruct((nb, 1, D), F32)],
        compiler_params=_cp("arbitrary"),
    )(x2, f, g2, g_fin, target)


def _norm_bwd(xin, dh, dres, g, sc, gate=None):
    t = xin.shape[0]
    nb = t // S
    tpb = S // TM
    gated = gate is not None

    def body(*refs):
        if gated:
            x_ref, dh_ref, dres_ref, g_ref, sc_ref, mix_ref, g1_ref, dx_ref, dsh_ref, dsc_ref, dg_ref, dg1_ref, dmix_ref = refs
        else:
            x_ref, dh_ref, dres_ref, g_ref, sc_ref, dx_ref, dsh_ref, dsc_ref, dg_ref = refs
        i = pl.program_id(0)
        xv, dhv = x_ref[...], dh_ref[...]
        r = _rms(xv)
        n = xv * r
        gv = g_ref[...]
        _acc_first(i, dsh_ref, _colsum(dhv), every=tpb)
        _acc_first(i, dsc_ref, _colsum(dhv * (n * gv)), every=tpb)
        dng = dhv * (1.0 + sc_ref[0])
        _acc_first(i, dg_ref, _colsum(dng * n))
        dx = dres_ref[...] + _rms_bwd(n, r, dng * gv)
        dx_ref[...] = dx
        if gated:
            _acc_first(i, dg1_ref, _colsum(dx * mix_ref[...]), every=tpb)
            dmix_ref[...] = (dx * g1_ref[0]).astype(BF16)

    in_specs = [_row(D), _row(D), _row(D), _vec(D), _per_ex(D)]
    out_specs = [_row(D), _per_ex(D), _per_ex(D), _vec(D)]
    out_shape = [jax.ShapeDtypeStruct((t, D), F32), jax.ShapeDtypeStruct((nb, 1, D), F32),
                 jax.ShapeDtypeStruct((nb, 1, D), F32), jax.ShapeDtypeStruct((1, D), F32)]
    args = [xin, dh, dres, g, sc]
    if gated:
        in_specs += [_row(D), _per_ex(D)]
        out_specs += [_per_ex(D), _row(D)]
        out_shape += [jax.ShapeDtypeStruct((nb, 1, D), F32), jax.ShapeDtypeStruct((t, D), BF16)]
        args += list(gate)
    return pl.pallas_call(
        body, name="norm2_bwd" if gated else "norm1_bwd", grid=(t // TM,),
        in_specs=in_specs, out_specs=out_specs, out_shape=out_shape,
        compiler_params=_cp("arbitrary"),
    )(*args)


TQ = 256
TB = 512


def _mla_fwd(q, k, v):
    t = q.shape[0]
    nb = t // S

    def body(q_ref, k_ref, v_ref, o_ref, lse_ref):
        causal = lax.broadcasted_iota(jnp.int32, (TB, TB), 0) >= lax.broadcasted_iota(jnp.int32, (TB, TB), 1)
        heads = [slice(h * HP, (h + 1) * HP) for h in range(2)]
        for i in range(S // TB):
            ri, past = slice(i * TB, (i + 1) * TB), slice(0, i * TB)
            qhs = [q_ref[ri, sl] for sl in heads]
            sd = [jnp.where(causal, _dot(qh, k_ref[ri, sl], ((1,), (1,))) * SCALE_B, NEG) for qh, sl in zip(qhs, heads)]
            ms = [jnp.max(s, axis=-1, keepdims=True) for s in sd]
            if i:
                so = [_dot(qh, k_ref[past, sl], ((1,), (1,))) * SCALE_B for qh, sl in zip(qhs, heads)]
                ms = [jnp.maximum(m, jnp.max(s, axis=-1, keepdims=True)) for m, s in zip(ms, so)]
            pd = [jnp.exp(s - m) for s, m in zip(sd, ms)]
            ls = [jnp.sum(p, axis=-1, keepdims=True) for p in pd]
            acc = [_dot(p.astype(BF16), v_ref[ri, sl], ((1,), (0,))) for p, sl in zip(pd, heads)]
            if i:
                po = [jnp.exp(s - m) for s, m in zip(so, ms)]
                ls = [l + jnp.sum(p, axis=-1, keepdims=True) for l, p in zip(ls, po)]
                acc = [a + _dot(p.astype(BF16), v_ref[past, sl], ((1,), (0,))) for a, p, sl in zip(acc, po, heads)]
            o_ref[ri, :] = acc[0] / ls[0] + acc[1] / ls[1]
            for sl, m, l in zip(heads, ms, ls):
                lse_ref[ri, sl] = jnp.broadcast_to(m + jnp.log(l), (TB, HP))

    wide2 = pl.BlockSpec((S, 2 * HP), lambda b, p: (b, p))
    return pl.pallas_call(
        body, name="mla_fwd", grid=(nb, H // 2),
        in_specs=[wide2, wide2, wide2],
        out_specs=[pl.BlockSpec((S, HP), lambda b, p: (b, p)), wide2],
        out_shape=[jax.ShapeDtypeStruct((t, H * VDIM), F32), jax.ShapeDtypeStruct((t, H * HP), F32)],
        compiler_params=_cp("parallel", "parallel"),
    )(q, k, v)


def _mla_bwd(q, k, v, o, do, lse):
    t = q.shape[0]
    nb = t // S
    nq = S // TQ

    def body(q_ref, k_ref, v_ref, o_ref, do_ref, lse_ref, dq_ref, dk_ref, dv_ref):
        lane = lax.broadcasted_iota(jnp.int32, (TB, HP), 1)
        causal = lax.broadcasted_iota(jnp.int32, (TB, TB), 0) >= lax.broadcasted_iota(jnp.int32, (TB, TB), 1)
        heads = [slice(h * HP, (h + 1) * HP) for h in range(2)]
        nblk = S // TB
        for i in reversed(range(nblk)):
            ri, past = slice(i * TB, (i + 1) * TB), slice(0, i * TB)
            dov = do_ref[ri, :]
            prod = dov * o_ref[ri, :]
            dob = dov.astype(BF16)
            deltas = [jnp.sum(jnp.where((lane < VDIM) if h == 0 else (lane >= VDIM), prod, 0.0), axis=-1, keepdims=True)
                      for h in range(2)]
            qhs = [q_ref[ri, sl] for sl in heads]
            lses = [lse_ref[ri, sl][:, :1] for sl in heads]
            for rows, diagonal in ((ri, True), (past, False)):
                if rows.stop == rows.start:
                    continue
                ps = [jnp.exp(_dot(qh, k_ref[rows, sl], ((1,), (1,))) * SCALE_B - lse) for qh, sl, lse in zip(qhs, heads, lses)]
                if diagonal:
                    ps = [jnp.where(causal, p, 0.0) for p in ps]
                dps = [_dot(dob, v_ref[rows, sl], ((1,), (1,))) for sl in heads]
                dss = [(p * (dp - delta) * SCALE_B).astype(BF16) for p, dp, delta in zip(ps, dps, deltas)]
                for sl, qh, p, ds in zip(heads, qhs, ps, dss):
                    dq = _dot(ds, k_ref[rows, sl], ((1,), (0,)))
                    dk = _dot(ds, qh, ((0,), (0,)))
                    dv = _dot(p.astype(BF16), dob, ((0,), (0,)))
                    if diagonal:
                        dq_ref[ri, sl] = dq
                    else:
                        dq_ref[ri, sl] += dq
                    if i == nblk - 1:
                        dk_ref[rows, sl] = dk
                        dv_ref[rows, sl] = dv
                    else:
                        dk_ref[rows, sl] += dk
                        dv_ref[rows, sl] += dv

    wide2 = pl.BlockSpec((S, 2 * HP), lambda b, p: (b, p))
    pair = pl.BlockSpec((S, HP), lambda b, p: (b, p))
    return pl.pallas_call(
        body, name="mla_bwd", grid=(nb, H // 2),
        in_specs=[wide2, wide2, wide2, pair, pair, wide2],
        out_specs=[wide2, wide2, wide2],
        out_shape=[jax.ShapeDtypeStruct((t, H * HP), F32)] * 3,
        compiler_params=_cp("parallel", "parallel"),
    )(q, k, v, o, do, lse)


def _t5_bucket(dist):
    max_exact = N_BUCKETS // 2
    d = np.maximum(dist, 1).astype(np.float64)
    large = max_exact + (np.log(d / max_exact) / np.log(MAX_DISTANCE / max_exact) * (N_BUCKETS - max_exact)).astype(np.int64)
    large = np.minimum(large, N_BUCKETS - 1)
    return np.where(dist < max_exact, dist, large).astype(np.int32)


def _band_geometry():
    a = np.arange(BLK)[:, None]
    bk = np.arange(2 * BLK)[None, :]
    steps = BLK + a - bk
    valid = (steps >= 0) & (steps <= BLK)
    buckets = np.stack([_t5_bucket(np.clip(steps, 0, BLK) * d) for d in DILATIONS])
    return buckets, valid


def _band_bias(rel_bias):
    buckets, valid = _band_geometry()
    onehot = (jnp.asarray(buckets)[..., None] == jnp.arange(N_BUCKETS)).astype(F32)
    bias = jnp.einsum("rqkn,nh->rhqk", onehot, rel_bias, precision=lax.Precision.HIGHEST)
    bias = jnp.where(jnp.asarray(valid)[None, None], bias, NEG)
    return bias.reshape(3, H // 2, 2 * BLK, 2 * BLK)


def _dil_items():
    items = []
    for r, d in enumerate(DILATIONS):
        for res in range(d):
            for blk in range(S // d // BLK):
                items.append((r, d, blk * BLK * d + res, blk > 0))
    return items


GROUP = 4


def _strided(start, d):
    return pl.ds(start, BLK) if d == 1 else pl.ds(start, BLK, stride=d)


def _stack_heads(tile, own):
    return jnp.where(own, jnp.concatenate([tile, tile], axis=0), 0.0).astype(BF16)


def _own_lanes():
    row = lax.broadcasted_iota(jnp.int32, (2 * BLK, HP), 0)
    lane = lax.broadcasted_iota(jnp.int32, (2 * BLK, HP), 1)
    return (lane < E_A) == (row < BLK)


def _dil_fwd(proj, biasm):
    t = proj.shape[0]
    nb = t // S

    def body(q_ref, k_ref, v_ref, b_ref, o_ref, lse_ref, ob_ref, lb_ref):
        lane = lax.broadcasted_iota(jnp.int32, (BLK, HP), 1)
        own = _own_lanes()
        items = _dil_items()
        for g in range(0, len(items), GROUP):
            grp = items[g:g + GROUP]
            ss, vts = [], []
            for r, d, start, has_prev in grp:
                cur = _strided(start, d)
                rows = [_strided(start - BLK * d, d), cur] if has_prev else [cur]
                q2 = _stack_heads(q_ref[cur, :], own)
                kt = jnp.concatenate([k_ref[x, :] for x in rows], axis=0).astype(BF16)
                vts.append(jnp.concatenate([v_ref[x, :] for x in rows], axis=0).astype(BF16))
                bias = b_ref[r, 0] if has_prev else b_ref[r, 0, :, BLK:]
                ss.append(_dot(q2, kt, ((1,), (1,))) * SCALE_A + bias)
            ms = [jnp.max(s, axis=-1, keepdims=True) for s in ss]
            ps = [jnp.exp(s - m) for s, m in zip(ss, ms)]
            ls = [jnp.sum(p, axis=-1, keepdims=True) for p in ps]
            for (r, d, start, _), p, vt, m, l in zip(grp, ps, vts, ms, ls):
                cur = _strided(start, d)
                o2 = _dot(p.astype(BF16), vt, ((1,), (0,))) / l
                lse2 = m + jnp.log(l)
                ob_ref[r, cur, :] = jnp.where(lane < E_A, o2[:BLK], o2[BLK:])
                lb_ref[r, cur, :] = jnp.where(lane < E_A, lse2[:BLK], lse2[BLK:])

        def merge(c, _):
            rows = pl.ds(pl.multiple_of(c * TQ, TQ), TQ)
            l0, l1, l2 = lb_ref[0, rows, :], lb_ref[1, rows, :], lb_ref[2, rows, :]
            m = jnp.maximum(jnp.maximum(l0, l1), l2)
            e0, e1, e2 = jnp.exp(l0 - m), jnp.exp(l1 - m), jnp.exp(l2 - m)
            tot = e0 + e1 + e2
            o_ref[rows, :] = (e0 * ob_ref[0, rows, :] + e1 * ob_ref[1, rows, :] + e2 * ob_ref[2, rows, :]) / tot
            lse_ref[rows, :] = m + jnp.log(tot)
            return 0

        lax.fori_loop(0, S // TQ, merge, 0)

    npair = H // 2
    return pl.pallas_call(
        body, name="dil_fwd", grid=(nb, npair),
        in_specs=[pl.BlockSpec((S, HP), lambda b, p: (b, p)), pl.BlockSpec((S, HP), lambda b, p: (b, npair + p)),
                  pl.BlockSpec((S, HP), lambda b, p: (b, 2 * npair + p)),
                  pl.BlockSpec((3, 1, 2 * BLK, 2 * BLK), lambda b, p: (0, p, 0, 0))],
        out_specs=[pl.BlockSpec((S, HP), lambda b, p: (b, p))] * 2,
        out_shape=[jax.ShapeDtypeStruct((t, D_A), F32)] * 2,
        scratch_shapes=[pltpu.VMEM((3, S, HP), F32), pltpu.VMEM((3, S, HP), F32)],
        compiler_params=_cp("parallel", "parallel"),
    )(proj, proj, proj, biasm)


def _dil_bwd(proj, biasm, o, do, lse):
    t = proj.shape[0]
    nb = t // S

    def body(q_ref, k_ref, v_ref, b_ref, o_ref, do_ref, lse_ref, dq_ref, dk_ref, dv_ref, ds_ref):
        dq_ref[...] = jnp.zeros_like(dq_ref)
        dk_ref[...] = jnp.zeros_like(dk_ref)
        dv_ref[...] = jnp.zeros_like(dv_ref)
        ds_ref[...] = jnp.zeros_like(ds_ref)
        lane = lax.broadcasted_iota(jnp.int32, (BLK, HP), 1)
        own = _own_lanes()
        items = _dil_items()
        for g in range(0, len(items), GROUP):
            grp = items[g:g + GROUP]
            q2s, kts, do2s, ss, dps, lse2s, delta2s = [], [], [], [], [], [], []
            for r, d, start, has_prev in grp:
                cur = _strided(start, d)
                rows = [_strided(start - BLK * d, d), cur] if has_prev else [cur]
                q2 = _stack_heads(q_ref[cur, :], own)
                kt = jnp.concatenate([k_ref[x, :] for x in rows], axis=0).astype(BF16)
                vt = jnp.concatenate([v_ref[x, :] for x in rows], axis=0).astype(BF16)
                dot_ = do_ref[cur, :]
                prod = dot_ * o_ref[cur, :]
                lset = lse_ref[cur, :]
                do2 = _stack_heads(dot_, own)
                bias = b_ref[r, 0] if has_prev else b_ref[r, 0, :, BLK:]
                ss.append(_dot(q2, kt, ((1,), (1,))) * SCALE_A + bias)
                dps.append(_dot(do2, vt, ((1,), (1,))))
                lse2s.append(jnp.concatenate([lset[:, :1], lset[:, E_A:E_A + 1]], axis=0))
                delta2s.append(jnp.concatenate([jnp.sum(jnp.where(lane < E_A, prod, 0.0), axis=-1, keepdims=True),
                                                jnp.sum(jnp.where(lane >= E_A, prod, 0.0), axis=-1, keepdims=True)], axis=0))
                q2s.append(q2)
                kts.append(kt)
                do2s.append(do2)
            ps = [jnp.exp(s - lse2) for s, lse2 in zip(ss, lse2s)]
            dls = [p * (dp - delta2) for p, dp, delta2 in zip(ps, dps, delta2s)]
            for (r, d, start, has_prev), q2, kt, do2, p, dl in zip(grp, q2s, kts, do2s, ps, dls):
                cur = _strided(start, d)
                dsb = (dl * SCALE_A).astype(BF16)
                dq2 = _dot(dsb, kt, ((1,), (0,)))
                dkt = _dot(dsb, q2, ((0,), (0,)))
                dvt = _dot(p.astype(BF16), do2, ((0,), (0,)))
                dq_ref[cur, :] += jnp.where(lane < E_A, dq2[:BLK], dq2[BLK:])
                if has_prev:
                    prev = _strided(start - BLK * d, d)
                    ds_ref[0, r, 0] += dl
                    dk_ref[prev, :] += dkt[:BLK]
                    dv_ref[prev, :] += dvt[:BLK]
                    dk_ref[cur, :] += dkt[BLK:]
                    dv_ref[cur, :] += dvt[BLK:]
                else:
                    ds_ref[0, r, 0, :, BLK:] += dl
                    dk_ref[cur, :] += dkt
                    dv_ref[cur, :] += dvt

    npair = H // 2
    pair = pl.BlockSpec((S, HP), lambda b, p: (b, p))
    return pl.pallas_call(
        body, name="dil_bwd", grid=(nb, npair),
        in_specs=[pair, pl.BlockSpec((S, HP), lambda b, p: (b, npair + p)),
                  pl.BlockSpec((S, HP), lambda b, p: (b, 2 * npair + p)),
                  pl.BlockSpec((3, 1, 2 * BLK, 2 * BLK), lambda b, p: (0, p, 0, 0)), pair, pair, pair],
        out_specs=[pair, pair, pair, pl.BlockSpec((1, 3, 1, 2 * BLK, 2 * BLK), lambda b, p: (b, 0, p, 0, 0))],
        out_shape=[jax.ShapeDtypeStruct((t, D_A), F32)] * 3 + [jax.ShapeDtypeStruct((nb, 3, npair, 2 * BLK, 2 * BLK), F32)],
        compiler_params=_cp("parallel", "parallel"),
    )(proj, proj, proj, biasm, o, do, lse)


def _rel_bias_grad(dlogits):
    nb = dlogits.shape[0]
    buckets, _ = _band_geometry()
    kk = 3 * BLK * 2 * BLK
    dl = jnp.transpose(dlogits.reshape(nb, 3, H, BLK, 2 * BLK), (0, 2, 1, 3, 4)).reshape(nb, H, kk)
    bk = jnp.asarray(buckets.reshape(1, kk))
    tk = kk // 12

    def body(dl_ref, bk_ref, o_ref):
        j = pl.program_id(0)
        onehot = (bk_ref[...] == lax.broadcasted_iota(jnp.int32, (N_BUCKETS, tk), 0)).astype(F32)
        tot = dl_ref[0]
        for b in range(1, nb):
            tot = tot + dl_ref[b]
        part = lax.dot_general(onehot, tot, ((((1,), (1,))), ((), ())), preferred_element_type=F32,
                               precision=lax.Precision.HIGHEST)
        _acc_first(j, o_ref, part)

    return pl.pallas_call(
        body, name="rel_bias_grad", grid=(kk // tk,),
        in_specs=[pl.BlockSpec((nb, H, tk), lambda j: (0, 0, j)), pl.BlockSpec((1, tk), lambda j: (0, j))],
        out_specs=pl.BlockSpec((N_BUCKETS, H), lambda j: (0, 0)),
        out_shape=jax.ShapeDtypeStruct((N_BUCKETS, H), F32),
        compiler_params=_cp("arbitrary"),
    )(dl, bk)


def _mesh_place():
    x, y, c = lax.axis_index("x"), lax.axis_index("y"), lax.axis_index("c")
    return x, y, c


def _peer(k):
    x, y, c = _mesh_place()
    px = 1 - x if k & 4 else x
    py = 1 - y if k & 2 else y
    pc = 1 - c if k & 1 else c
    return (px, py, pc), 4 * px + 2 * py + pc


ANY = pl.BlockSpec(memory_space=pl.ANY)


def _exchange(arrays, gathers, name, after=None):
    n_arr = len(arrays)

    def body(*refs):
        ins, outs = refs[:n_arr], refs[n_arr + 1:2 * n_arr + 1]
        send, recv, loc = refs[2 * n_arr + 1:]
        x, y, c = _mesh_place()
        me = 4 * x + 2 * y + c
        local = [pltpu.make_async_copy(ins[a] if gathers[a] else ins[a].at[me], outs[a].at[me], loc.at[a])
                 for a in range(n_arr)]
        remote = _peer_copies(ins, outs, send, recv, gathers)
        for cp in local:
            cp.start()
        for put, _ in remote:
            put.start()
        for cp in local:
            cp.wait()
        for put, got in remote:
            put.wait_send()
            got.wait_recv()

    return pl.pallas_call(
        body, name=name,
        in_specs=[ANY] * (n_arr + 1), out_specs=[ANY] * n_arr,
        out_shape=[jax.ShapeDtypeStruct(((N_DEV,) if g else ()) + a.shape, a.dtype) for a, g in zip(arrays, gathers)],
        scratch_shapes=[pltpu.SemaphoreType.DMA((n_arr * (N_DEV - 1),)), pltpu.SemaphoreType.DMA((n_arr * (N_DEV - 1),)),
                        pltpu.SemaphoreType.DMA((n_arr,))],
        compiler_params=pltpu.CompilerParams(has_side_effects=True),
    )(*arrays, arrays[0] if after is None else after)


HBM = pl.BlockSpec(memory_space=pltpu.HBM)
SEM = pl.BlockSpec(memory_space=pltpu.SEMAPHORE)
DATAFLOW = pltpu.SideEffectType.DATAFLOW_SIDE_EFFECTING


def _own_block_in_place(block, me):
    land = jnp.zeros((N_DEV,) + block.shape, block.dtype)
    return lax.dynamic_update_slice(land, block[None], (me,) + (0,) * block.ndim)


def _peer_copies(srcs, lands, send, recv, gathers):
    x, y, c = _mesh_place()
    me = 4 * x + 2 * y + c
    out = []
    for a, (src, land) in enumerate(zip(srcs, lands)):
        for k in range(1, N_DEV):
            dev, idx = _peer(k)
            sem = a * (N_DEV - 1) + k - 1
            mine = src if gathers[a] else src.at[idx]
            put = pltpu.make_async_remote_copy(mine, land.at[me], send.at[sem], recv.at[sem],
                                               device_id=dev, device_id_type=pl.DeviceIdType.MESH)
            got = pltpu.make_async_remote_copy(mine, land.at[idx], send.at[sem], recv.at[sem],
                                               device_id=dev, device_id_type=pl.DeviceIdType.MESH)
            out.append((put, got))
    return out


def _exchange_start(srcs, lands, gather, after, name):
    n = len(srcs)

    def body(*refs):
        srcs_, lands_, send, recv = refs[:n], refs[n:2 * n], refs[2 * n + 1], refs[2 * n + 2]
        for put, _ in _peer_copies(srcs_, lands_, send, recv, gather):
            put.start()
        refs[-1][...] = jnp.zeros_like(refs[-1])

    nsem = n * (N_DEV - 1)
    thru = [pltpu.HBM(a.shape, a.dtype) for a in list(srcs) + list(lands)]
    res = pl.pallas_call(
        body, name=name,
        out_shape=(pltpu.SemaphoreType.DMA((nsem,)), pltpu.SemaphoreType.DMA((nsem,)), *thru, jax.ShapeDtypeStruct((8, 128), F32)),
        in_specs=[HBM] * (2 * n) + [ANY],
        out_specs=(SEM, SEM, *([HBM] * (2 * n)), pl.BlockSpec(memory_space=pltpu.VMEM)),
        input_output_aliases={i: 2 + i for i in range(2 * n)},
        compiler_params=pltpu.CompilerParams(has_side_effects=DATAFLOW),
    )(*[pltpu.with_memory_space_constraint(a, pltpu.HBM) for a in list(srcs) + list(lands)], after)
    return res[0], res[1], list(res[2:2 + n]), list(res[2 + n:2 + 2 * n]), res[-1]


def _exchange_wait(send, recv, srcs, lands, gather, after, name):
    n = len(srcs)

    def body(*refs):
        srcs_, lands_, send_, recv_ = refs[:n], refs[n:2 * n], refs[2 * n], refs[2 * n + 1]
        for put, got in _peer_copies(srcs_, lands_, send_, recv_, gather):
            put.wait_send()
            got.wait_recv()

    thru = [pltpu.HBM(a.shape, a.dtype) for a in list(srcs) + list(lands)]
    res = pl.pallas_call(
        body, name=name, out_shape=tuple(thru),
        in_specs=[HBM] * (2 * n) + [SEM, SEM, ANY], out_specs=tuple([HBM] * (2 * n)),
        input_output_aliases={i: i for i in range(2 * n)},
        compiler_params=pltpu.CompilerParams(has_side_effects=DATAFLOW),
    )(*srcs, *lands, send, recv, after)
    return list(res[n:])


def _silu_rows(c):
    def body(c_ref, o_ref):
        v = c_ref[...]
        o_ref[...] = v * _sigmoid(v)

    return pl.pallas_call(body, name="cond", out_shape=jax.ShapeDtypeStruct(c.shape, F32))(c)


def _mod_slab(cond_all, w_ada, b_slab):
    def body(c_ref, w_ref, b_ref, o_ref):
        o_ref[...] = _dot(c_ref[...].astype(BF16), w_ref[...].astype(BF16), ((1,), (0,))) + b_ref[...]

    return pl.pallas_call(body, name="mod_slab",
                          out_shape=jax.ShapeDtypeStruct((cond_all.shape[0], w_ada.shape[1]), F32),
                          compiler_params=pltpu.CompilerParams(vmem_limit_bytes=VMEM_LIMIT))(cond_all, w_ada, b_slab)


def _ada_grad(cond_all, dmod_cols):
    def body(c_ref, d_ref, o_ref):
        o_ref[...] = _dot(c_ref[...].astype(BF16), d_ref[...].astype(BF16), ((0,), (0,)))

    return pl.pallas_call(body, name="ada_grad",
                          out_shape=jax.ShapeDtypeStruct((cond_all.shape[1], dmod_cols.shape[1]), F32),
                          compiler_params=pltpu.CompilerParams(vmem_limit_bytes=VMEM_LIMIT))(cond_all, dmod_cols)


def _adam_math(g, w, m, v):
    m2 = B1 * m + (1.0 - B1) * g
    v2 = B2 * v + (1.0 - B2) * (g * g)
    m_hat = m2 / (1.0 - B1 ** STEP)
    v_hat = v2 / (1.0 - B2 ** STEP)
    return -LR * (m_hat / (jnp.sqrt(v_hat) + ADAM_EPS) + WD * w), m2, v2


def _adamw(parts, w, m, v, name):
    n, rows, cols = parts.shape
    tr = _pick(rows, (128, 96, 64, 32, 16, 8))

    def body(p_ref, w_ref, m_ref, v_ref, g_ref, d_ref, m2_ref, v2_ref):
        g = p_ref[0].astype(F32)
        for s in range(1, n):
            g = g + p_ref[s].astype(F32)
        g_ref[...] = g
        d_ref[...], m2_ref[...], v2_ref[...] = _adam_math(g, w_ref[...], m_ref[...], v_ref[...])

    blk = pl.BlockSpec((tr, cols), lambda i: (i, 0))
    return pl.pallas_call(
        body, name=name, grid=(rows // tr,),
        in_specs=[pl.BlockSpec((n, tr, cols), lambda i: (0, i, 0)), blk, blk, blk],
        out_specs=[blk] * 4, out_shape=[jax.ShapeDtypeStruct((rows, cols), F32)] * 4,
        compiler_params=_cp("parallel"),
    )(parts, w, m, v)


SMALL = (("b_ada", N_MOD * D), ("g_norm1", D), ("g_cq", Q_LORA), ("g_ckv", KV_LORA), ("rel_bias", N_BUCKETS * H),
         ("g_out_a", D_A), ("g_out_b", D_A), ("g_norm2", D), ("g_final", D))
TILE = 8 * 128


def _tiles(n):
    return -(-n // TILE) * 8


SMALL_ROWS = sum(_tiles(n) for _, n in SMALL)
LOSS_ROWS = 8
PACK_ROWS = SMALL_ROWS + LOSS_ROWS


def _pack(vals):
    parts = []
    for v in vals:
        v = v.reshape(-1)
        rows = _tiles(v.shape[0])
        parts.append(jnp.pad(v, (0, rows * 128 - v.shape[0])).reshape(rows, 128))
    return jnp.concatenate(parts, axis=0)


def _unpack(packed, shapes):
    out, r = [], 0
    for (_, n), shp in zip(SMALL, shapes):
        rows = _tiles(n)
        out.append(packed[r:r + rows].reshape(-1)[:n].reshape(shp))
        r += rows
    return out


def _small_update(parts, w, m, v):
    def body(p_ref, w_ref, m_ref, v_ref, g_ref, d_ref, m2_ref, v2_ref, loss_ref):
        tot = p_ref[0]
        for s in range(1, N_DEV):
            tot = tot + p_ref[s]
        g = tot[:SMALL_ROWS]
        g_ref[...] = g
        d_ref[...], m2_ref[...], v2_ref[...] = _adam_math(g, w_ref[...], m_ref[...], v_ref[...])
        loss_ref[...] = jnp.broadcast_to((0.5 / D) * jnp.sum(tot[SMALL_ROWS:]), loss_ref.shape)

    return pl.pallas_call(
        body, name="small_update",
        out_shape=[jax.ShapeDtypeStruct((SMALL_ROWS, 128), F32)] * 4 + [jax.ShapeDtypeStruct((8, 128), F32)],
    )(parts, w, m, v)


def _cols_from_blocks(g):
    return jnp.transpose(g, (1, 0, 2)).reshape(g.shape[1], N_DEV * g.shape[2])


def _cols_to_blocks(w):
    r, c = w.shape
    return jnp.transpose(w.reshape(r, N_DEV, c // N_DEV), (1, 0, 2))


def _pad_w_in(w):
    z = jnp.zeros((w.shape[0], NOPE), w.dtype)
    return jnp.concatenate([w[:, :P_IN - ROPE], z, w[:, P_IN - ROPE:], z[:, :HP - NOPE - ROPE]], axis=1)


def _unpad_w_in(g):
    k0 = P_IN - ROPE + NOPE
    return jnp.concatenate([g[:, :P_IN - ROPE], g[:, k0:k0 + ROPE]], axis=1)


def _pad_w_uq(w):
    w3 = w.reshape(Q_LORA, H, NOPE + ROPE)
    return jnp.pad(w3, ((0, 0), (0, 0), (0, HP - NOPE - ROPE))).reshape(Q_LORA, H * HP)


def _unpad_w_uq(g):
    return g.reshape(Q_LORA, H, HP)[:, :, :NOPE + ROPE].reshape(Q_LORA, H * (NOPE + ROPE))


def _split_w_ukv(w):
    w4 = w.reshape(KV_LORA, H // 2, 2, HP)
    z = jnp.zeros((KV_LORA, H // 2, NOPE), w.dtype)
    kn, vv = w4[..., :NOPE], w4[..., NOPE:]
    w_k = jnp.stack([jnp.concatenate([kn[:, :, 0], z], -1), jnp.concatenate([kn[:, :, 1], z], -1)], axis=2)
    w_v = jnp.stack([jnp.concatenate([vv[:, :, 0], z], -1), jnp.concatenate([z, vv[:, :, 1]], -1)], axis=2)
    return w_k.reshape(KV_LORA, H * HP), w_v.reshape(KV_LORA, H * HP)


def _join_w_ukv(g_k, g_v):
    gk = g_k.reshape(KV_LORA, H // 2, 2, HP)
    gv = g_v.reshape(KV_LORA, H // 2, 2, HP)
    even = jnp.concatenate([gk[:, :, 0, :NOPE], gv[:, :, 0, :VDIM]], -1)
    odd = jnp.concatenate([gk[:, :, 1, :NOPE], gv[:, :, 1, VDIM:]], -1)
    return jnp.stack([even, odd], axis=2).reshape(KV_LORA, H * HP)


def _rope_tables():
    half = ROPE // 2
    inv = ROPE_THETA ** (-jnp.arange(half, dtype=F32) / half)
    ang = jnp.arange(S, dtype=F32)[:, None] * inv[None, :]
    cos, sin = jnp.cos(ang), jnp.sin(ang)
    ones, zeros = jnp.ones((S, NOPE), F32), jnp.zeros((S, NOPE), F32)
    tail1, tail0 = jnp.ones((S, HP - NOPE - ROPE), F32), jnp.zeros((S, HP - NOPE - ROPE), F32)
    zh = jnp.zeros((S, half), F32)
    c = jnp.concatenate([ones, cos, cos, tail1], axis=1)
    sm = jnp.concatenate([zeros, -sin, zh, tail0], axis=1)
    sp = jnp.concatenate([zeros, zh, sin, tail0], axis=1)
    return c, sm, sp


def _local_step(x, mod, target, g_norm1, w_in_p, g_cq, w_uq_p, g_ckv, w_k, w_v, rel_bias, g_out_a, g_out_b, w_out,
                g_norm2, w_ffn_in, w_ffn_out, g_final, late_weights=None, on_ffn_grads=None, on_last_grads=None):
    nb = x.shape[0] // S
    sh1, sc1, g1, sh2, sc2, g2 = (mod[:, n].reshape(nb, 1, D) for n in range(N_MOD))
    rc, rsm, rsp = _rope_tables()
    biasm = _band_bias(rel_bias)

    h1 = _pre1(x, g_norm1, sc1, sh1)
    proj = _mm_nn(h1, w_in_p, F32, "proj")
    q, k, v, cqn, ckvn = _mla_pre(proj, g_cq, g_ckv, w_uq_p, w_k, w_v, rc, rsm, rsp)
    out_b, lse_b = _mla_fwd(q, k, v)
    out_a, lse_a = _dil_fwd(proj, biasm)
    y = _post_attn(out_a, out_b, g_out_a, g_out_b)
    if late_weights is not None:
        w_out, w_ffn_in, w_ffn_out = late_weights(y)
    mix = _mm_nn(y, w_out, F32, "mix")
    x2, h2 = _resid_norm2(x, mix, g1, g_norm2, sc2, sh2)
    ffn_g, ffn_u, act = _ffn_in(h2, w_ffn_in)
    f = _mm_nn(act, w_ffn_out, F32, "ffn_out")
    dx3, df, loss_cols, dg_final, dg2 = _final(x2, f, g2, g_final, target)

    dg_, du_ = _d_act(df, w_ffn_out, ffn_g, ffn_u)
    gw_ffn_out = _mm_tn(act, [df], "gw_ffn_out")
    dh2 = _d_h2(dg_, du_, w_ffn_in)
    gw_ffn_in = _mm_tn(h2, [dg_, du_], "gw_ffn_in")
    dx2, dsh2, dsc2, dg_norm2, dg1, dmix = _norm_bwd(x2, dh2, dx3, g_norm2, sc2, gate=(mix, g1))
    dy = _mm_nt(dmix, w_out, F32, "d_y")
    gw_out = _mm_tn(y, [dmix], "gw_out")
    if on_ffn_grads is not None:
        g_out_a = g_out_a + on_ffn_grads(gw_ffn_in, gw_ffn_out, gw_out)
    dout_a, dout_b, dg_out_a, dg_out_b = _post_attn_bwd(dy, out_a, out_b, g_out_a, g_out_b)
    dq_b, dk_b, dv_b = _mla_bwd(q, k, v, out_b, dout_b, lse_b)
    dq_a, dk_a, dv_a, dlogits = _dil_bwd(proj, biasm, out_a, dout_a, lse_a)
    g_rel = _rel_bias_grad(dlogits)
    dqr, dkr, dvr, dtail, dg_cq, dg_ckv = _mla_pre_bwd(proj, dq_b, dk_b, dv_b, g_cq, g_ckv, w_uq_p, w_k, w_v, rc, rsm, rsp)
    gw_uq = _mm_tn(cqn, [dqr], "gw_uq")
    gw_k, gw_v = _mm_tn(ckvn, [dkr, dvr], "gw_kv")
    dproj = jnp.concatenate([dq_a.astype(BF16), dk_a.astype(BF16), dv_a.astype(BF16), dtail], axis=1)
    dh1 = _mm_nt(dproj, w_in_p, F32, "d_h1")
    gw_in = _mm_tn(h1, [dproj], "gw_in")
    if on_last_grads is not None:
        g_norm1 = g_norm1 + on_last_grads(dict(w_in=gw_in, w_uq=gw_uq, w_k=gw_k, w_v=gw_v))
    grad_x, dsh1, dsc1, dg_norm1 = _norm_bwd(x, dh1, dx2, g_norm1, sc1)

    dmod = jnp.concatenate([dsh1, dsc1, dg1, dsh2, dsc2, dg2], axis=1)
    small = dict(g_norm1=dg_norm1, g_cq=dg_cq, g_ckv=dg_ckv, rel_bias=g_rel, g_out_a=dg_out_a, g_out_b=dg_out_b,
                 g_norm2=dg_norm2, g_final=dg_final)
    big = dict(w_in=gw_in, w_uq=gw_uq, w_k=gw_k, w_v=gw_v, w_out=gw_out, w_ffn_in=gw_ffn_in, w_ffn_out=gw_ffn_out)
    return grad_x, dmod, loss_cols, small, big


def kernel(x, c, w_ada, b_ada, g_norm1, w_in, g_cq, w_uq, g_ckv, w_ukv, rel_bias, g_out_a, g_out_b, w_out, g_norm2, w_ffn_in, w_ffn_out, g_final, loss_target, m_w_ada, m_b_ada, m_g_norm1, m_w_in, m_g_cq, m_w_uq, m_g_ckv, m_w_ukv, m_rel_bias, m_g_out_a, m_g_out_b, m_w_out, m_g_norm2, m_w_ffn_in, m_w_ffn_out, m_g_final, v_w_ada, v_b_ada, v_g_norm1, v_w_in, v_g_cq, v_w_uq, v_g_ckv, v_w_ukv, v_rel_bias, v_g_out_a, v_g_out_b, v_w_out, v_g_norm2, v_w_ffn_in, v_w_ffn_out, v_g_final):
    nb = x.shape[0]
    t = nb * S
    xt, tt = x.reshape(t, D), loss_target.reshape(t, D)
    me = 4 * lax.axis_index("x") + 2 * lax.axis_index("y") + lax.axis_index("c")

    early = [w_in[0], w_uq[0], w_ukv[0]]
    gathered = _exchange([_silu_rows(c)] + [s.astype(BF16) for s in early], [True] * 4, "gather_weights")
    cond_all = gathered[0].reshape(N_DEV * nb, D)
    w_in_f, w_uq_f, w_ukv_f = (_cols_from_blocks(g) for g in gathered[1:4])
    w_k, w_v = _split_w_ukv(w_ukv_f)

    ncol = N_MOD * D // N_DEV
    b_slab = lax.dynamic_slice(b_ada, (0, me * ncol), (1, ncol))
    slab = _mod_slab(cond_all, w_ada[0], b_slab)
    (mod_rows,) = _exchange([slab.reshape(N_DEV, nb, ncol)], [False], "scatter_mod")
    mod = jnp.transpose(mod_rows, (1, 0, 2)).reshape(nb, N_MOD, D)

    late = [s.astype(BF16) for s in (w_out[0], w_ffn_in[0], w_ffn_out[0])]
    late_send, late_recv, late_src, late_land, late_token = _exchange_start(
        late, [_own_block_in_place(s, me) for s in late], [True] * 3, mod_rows, "gather_late_start")
    g_norm1_t = g_norm1 + late_token[:1, :1]

    def late_weights(after):
        w_out_g, w_ffn_in_g, w_ffn_out_g = _exchange_wait(late_send, late_recv, late_src, late_land, [True] * 3, after,
                                                          "gather_late_wait")
        return w_out_g.reshape(D, D), _cols_from_blocks(w_ffn_in_g), w_ffn_out_g.reshape(D_FF, D)

    flight = {}

    def start_grads(key, src, name):
        land = [_own_block_in_place(lax.dynamic_index_in_dim(s, me, 0, keepdims=False), me) for s in src]
        send, recv, src, land, token = _exchange_start(src, land, [False] * len(src), src[0], name)
        flight[key] = (send, recv, src, land)
        return token[:1, :1]

    def half_blocks(g):
        return jnp.transpose(g.reshape(D, N_DEV // 2, 2 * D_FF // N_DEV), (1, 0, 2))

    def on_ffn_grads(gw_ffn_in, gw_ffn_out, gw_out):
        return start_grads("ffn", [jnp.concatenate([half_blocks(g) for g in gw_ffn_in], axis=0),
                                   gw_ffn_out.reshape(N_DEV, D_FF // N_DEV, D), gw_out.reshape(N_DEV, D // N_DEV, D)],
                           "exchange_ffn_start")

    def on_last_grads(gw):
        return start_grads("rest", [_cols_to_blocks(_unpad_w_in(gw["w_in"])), _cols_to_blocks(_unpad_w_uq(gw["w_uq"])),
                                    _cols_to_blocks(_join_w_ukv(gw["w_k"], gw["w_v"]))], "exchange_rest_start")

    grad_x, dmod, loss_cols, small, _ = _local_step(
        xt, mod, tt, g_norm1_t, _pad_w_in(w_in_f), g_cq, _pad_w_uq(w_uq_f), g_ckv, w_k, w_v, rel_bias, g_out_a, g_out_b,
        None, g_norm2, None, None, g_final.reshape(1, D), late_weights=late_weights, on_ffn_grads=on_ffn_grads,
        on_last_grads=on_last_grads)

    upd = {}

    def land_and_update(key, names, after, name):
        got = _exchange_wait(*flight[key], [False] * len(names), after, name)
        for n, p in zip(names, got):
            w, m, v = big[n]
            upd[n] = _adamw(p, w[0], m[0], v[0], "adamw_" + n)

    big = dict(w_in=(w_in, m_w_in, v_w_in), w_uq=(w_uq, m_w_uq, v_w_uq), w_ukv=(w_ukv, m_w_ukv, v_w_ukv),
               w_out=(w_out, m_w_out, v_w_out), w_ffn_in=(w_ffn_in, m_w_ffn_in, v_w_ffn_in),
               w_ffn_out=(w_ffn_out, m_w_ffn_out, v_w_ffn_out))
    land_and_update("ffn", ["w_ffn_in", "w_ffn_out", "w_out"], grad_x, "exchange_ffn_wait")
    land_and_update("rest", ["w_in", "w_uq", "w_ukv"], upd["w_out"][0], "exchange_rest_wait")

    dmod_blocks = jnp.transpose(dmod.reshape(nb, N_DEV, ncol), (1, 0, 2))
    mine = _pack([jnp.sum(dmod, axis=0)] + [small[n] for n, _ in SMALL[1:]] + [loss_cols])
    dmod_cols, parts = _exchange([dmod_blocks, mine], [False, True], "exchange_small", after=upd["w_ukv"][0])
    g_ada = _ada_grad(cond_all, dmod_cols.reshape(N_DEV * nb, ncol))
    upd["w_ada"] = _adamw(g_ada[None], w_ada[0], m_w_ada[0], v_w_ada[0], "adamw_w_ada")
    small_w = [b_ada, g_norm1, g_cq, g_ckv, rel_bias, g_out_a, g_out_b, g_norm2, g_final]
    small_m = [m_b_ada, m_g_norm1, m_g_cq, m_g_ckv, m_rel_bias, m_g_out_a, m_g_out_b, m_g_norm2, m_g_final]
    small_v = [v_b_ada, v_g_norm1, v_g_cq, v_g_ckv, v_rel_bias, v_g_out_a, v_g_out_b, v_g_norm2, v_g_final]
    sg, sd, sm, sv, loss8 = _small_update(parts, _pack(small_w), _pack(small_m), _pack(small_v))
    shapes = [w.shape for w in small_w]
    sg, sd, sm, sv = (_unpack(p, shapes) for p in (sg, sd, sm, sv))
    for i, (n, _) in enumerate(SMALL):
        upd[n] = (sg[i], sd[i], sm[i], sv[i])

    order = ["w_ada", "b_ada", "g_norm1", "w_in", "g_cq", "w_uq", "g_ckv", "w_ukv", "rel_bias", "g_out_a", "g_out_b",
             "w_out", "g_norm2", "w_ffn_in", "w_ffn_out", "g_final"]
    like = dict(w_ada=w_ada, w_in=w_in, w_uq=w_uq, w_ukv=w_ukv, w_out=w_out, w_ffn_in=w_ffn_in, w_ffn_out=w_ffn_out)
    outs = [loss8[0, 0], grad_x.reshape(x.shape)]
    for part in range(4):
        for n in order:
            val = upd[n][part]
            outs.append(val.reshape(like[n].shape) if n in like else val)
    return tuple(outs)
```

```python
import functools

import numpy as np
import jax
import jax.numpy as jnp
from jax import lax
from jax.experimental import pallas as pl
from jax.experimental.pallas import tpu as pltpu

F32, BF16 = jnp.float32, jnp.bfloat16

N_DEV = 8
D = 1024
S = 2048
H = 8
E_A = 64
D_A = H * E_A
Q_LORA, KV_LORA = 384, 256
NOPE, ROPE, VDIM = 64, 32, 64
HP = 128
P_IN = 3 * D_A + Q_LORA + KV_LORA + ROPE
P_PAD = 3 * D_A + Q_LORA + KV_LORA + HP
TAIL0 = 3 * D_A
TAIL = P_PAD - TAIL0
D_FF = 2816
N_MOD = 6
EPS = 1e-6
NEG = -1e30
BLK = 128
DILATIONS = (1, 4, 16)
N_BUCKETS, MAX_DISTANCE = 32, 2048
ROPE_THETA = 10000.0
SCALE_A = E_A ** -0.5
SCALE_B = (NOPE + ROPE) ** -0.5
B1, B2, LR, ADAM_EPS, WD, STEP = 0.9, 0.999, 0.001, 1e-8, 0.01, 10
VMEM_LIMIT = 56 * 1024 * 1024


def _cp(*sem):
    return pltpu.CompilerParams(dimension_semantics=sem, vmem_limit_bytes=VMEM_LIMIT)


def _pick(n, prefs):
    for p in prefs:
        if n % p == 0:
            return p
    raise ValueError(f"no tile of {prefs} divides {n}")


OPERAND_BYTES = 6 * 1024 * 1024


def _pick_rows(m, k):
    return _pick(m, [p for p in (1024, 512, 256, 128, 16) if p * k * 2 <= OPERAND_BYTES])


def _dot(a, b, dims):
    return lax.dot_general(a, b, (dims, ((), ())), preferred_element_type=F32)


def _mm_nn(a, b, out_dtype, name):
    m, k = a.shape
    n = b.shape[1]
    tm, tn = _pick_rows(m, k), _pick(n, (512, 384, 256, 128))

    def body(a_ref, b_ref, o_ref):
        o_ref[...] = _dot(a_ref[...], b_ref[...], ((1,), (0,))).astype(o_ref.dtype)

    return pl.pallas_call(
        body, name=name, grid=(m // tm, n // tn),
        in_specs=[pl.BlockSpec((tm, k), lambda i, j: (i, 0)), pl.BlockSpec((k, tn), lambda i, j: (0, j))],
        out_specs=pl.BlockSpec((tm, tn), lambda i, j: (i, j)),
        out_shape=jax.ShapeDtypeStruct((m, n), out_dtype),
        compiler_params=_cp("parallel", "parallel"),
    )(a, b)


def _mm_nt(a, b, out_dtype, name):
    m, k = a.shape
    n = b.shape[0]
    tm, tn = _pick_rows(m, k), _pick(n, (512, 384, 256, 128))

    def body(a_ref, b_ref, o_ref):
        o_ref[...] = _dot(a_ref[...], b_ref[...], ((1,), (1,))).astype(o_ref.dtype)

    return pl.pallas_call(
        body, name=name, grid=(m // tm, n // tn),
        in_specs=[pl.BlockSpec((tm, k), lambda i, j: (i, 0)), pl.BlockSpec((tn, k), lambda i, j: (j, 0))],
        out_specs=pl.BlockSpec((tm, tn), lambda i, j: (i, j)),
        out_shape=jax.ShapeDtypeStruct((m, n), out_dtype),
        compiler_params=_cp("parallel", "parallel"),
    )(a, b)


def _mm_tn(a, bs, name):
    t, m = a.shape
    n = bs[0].shape[1]
    nb_ = len(bs)
    tm, tc = _pick(m, (512, 384, 256, 128)), _pick(t, (512, 16))
    tn = n if tm <= 256 and nb_ * n * t * 2 <= 2 * OPERAND_BYTES else _pick(n, (512, 384, 256, 128))

    def body(*refs):
        a_ref, b_refs, o_refs, at_ref = refs[0], refs[1:1 + nb_], refs[1 + nb_:1 + 2 * nb_], refs[-1]

        @pl.when(pl.program_id(1) == 0)
        def _():
            def chunk(c, _):
                rows = pl.ds(pl.multiple_of(c * tc, tc), tc)
                at_ref[:, rows] = a_ref[rows, :].T
                return 0

            lax.fori_loop(0, t // tc, chunk, 0)

        for b_ref, o_ref in zip(b_refs, o_refs):
            o_ref[...] = _dot(at_ref[...], b_ref[...], ((1,), (0,))).astype(BF16)

    res = pl.pallas_call(
        body, name=name, grid=(m // tm, n // tn),
        in_specs=[pl.BlockSpec((t, tm), lambda i, j: (0, i))] + [pl.BlockSpec((t, tn), lambda i, j: (0, j))] * nb_,
        out_specs=[pl.BlockSpec((tm, tn), lambda i, j: (i, j))] * nb_,
        out_shape=[jax.ShapeDtypeStruct((m, n), BF16)] * nb_,
        scratch_shapes=[pltpu.VMEM((tm, t), BF16)],
        compiler_params=_cp("parallel", "arbitrary"),
    )(a, *bs)
    return res[0] if nb_ == 1 else res


EPI = 256


def _silu_parts(g):
    sg = 1.0 / (1.0 + jnp.exp(-g))
    return sg, g * sg


def _ffn_in(h2, w):
    t, k = h2.shape
    tm, tn = _pick_rows(t, k), _pick(D_FF, (256, 128))
    nj = D_FF // tn

    def body(h_ref, wg_ref, wu_ref, g_ref, u_ref, a_ref):
        for r in range(tm // EPI):
            rows = slice(r * EPI, (r + 1) * EPI)
            hv = h_ref[rows, :]
            gb = _dot(hv, wg_ref[...], ((1,), (0,))).astype(BF16)
            ub = _dot(hv, wu_ref[...], ((1,), (0,))).astype(BF16)
            g_ref[rows, :] = gb
            u_ref[rows, :] = ub
            a_ref[rows, :] = (_silu_parts(gb.astype(F32))[1] * ub.astype(F32)).astype(BF16)

    blk = pl.BlockSpec((tm, tn), lambda i, j: (i, j))
    return pl.pallas_call(
        body, name="ffn_in", grid=(t // tm, nj),
        in_specs=[pl.BlockSpec((tm, k), lambda i, j: (i, 0)), pl.BlockSpec((k, tn), lambda i, j: (0, j)),
                  pl.BlockSpec((k, tn), lambda i, j: (0, j + nj))],
        out_specs=[blk] * 3, out_shape=[jax.ShapeDtypeStruct((t, D_FF), BF16)] * 3,
        compiler_params=_cp("parallel", "parallel"),
    )(h2, w, w)


def _d_act(df, w, g, u):
    t, k = df.shape
    tm, tn = _pick_rows(t, k), _pick(D_FF, (256, 128))

    def body(df_ref, w_ref, g_ref, u_ref, dg_ref, du_ref):
        for r in range(tm // EPI):
            rows = slice(r * EPI, (r + 1) * EPI)
            da = _dot(df_ref[rows, :], w_ref[...], ((1,), (1,)))
            gv = g_ref[rows, :].astype(F32)
            sg, silu = _silu_parts(gv)
            dg_ref[rows, :] = (da * u_ref[rows, :].astype(F32) * (sg * (1.0 + gv * (1.0 - sg)))).astype(BF16)
            du_ref[rows, :] = (da * silu).astype(BF16)

    blk = pl.BlockSpec((tm, tn), lambda i, j: (i, j))
    return pl.pallas_call(
        body, name="d_act", grid=(t // tm, D_FF // tn),
        in_specs=[pl.BlockSpec((tm, k), lambda i, j: (i, 0)), pl.BlockSpec((tn, k), lambda i, j: (j, 0)), blk, blk],
        out_specs=[blk] * 2, out_shape=[jax.ShapeDtypeStruct((t, D_FF), BF16)] * 2,
        compiler_params=_cp("parallel", "parallel"),
    )(df, w, g, u)


def _d_h2(dg, du, w):
    t = dg.shape[0]
    n = w.shape[0]
    tm, tn = _pick_rows(t, 2 * D_FF), _pick(n, (512, 256, 128))

    def body(dg_ref, du_ref, wg_ref, wu_ref, o_ref):
        o_ref[...] = (_dot(dg_ref[...], wg_ref[...], ((1,), (1,))) + _dot(du_ref[...], wu_ref[...], ((1,), (1,))))

    return pl.pallas_call(
        body, name="d_h2", grid=(t // tm, n // tn),
        in_specs=[pl.BlockSpec((tm, D_FF), lambda i, j: (i, 0)), pl.BlockSpec((tm, D_FF), lambda i, j: (i, 0)),
                  pl.BlockSpec((tn, D_FF), lambda i, j: (j, 0)), pl.BlockSpec((tn, D_FF), lambda i, j: (j, 1))],
        out_specs=pl.BlockSpec((tm, tn), lambda i, j: (i, j)),
        out_shape=jax.ShapeDtypeStruct((t, n), F32),
        compiler_params=_cp("parallel", "parallel"),
    )(dg, du, w, w)


TM = 256


def _row(w):
    return pl.BlockSpec((TM, w), lambda i: (i, 0))


def _row_at(w, col):
    return pl.BlockSpec((TM, w), lambda i: (i, col))


def _vec(w):
    return pl.BlockSpec((1, w), lambda i: (0, 0))


def _per_ex(w):
    return pl.BlockSpec((1, 1, w), lambda i: (i // (S // TM), 0, 0))


def _pos(w):
    return pl.BlockSpec((TM, w), lambda i: (i % (S // TM), 0))


def _full(shape):
    return pl.BlockSpec(shape, lambda i: (0,) * len(shape))


def _rms(x):
    return lax.rsqrt(jnp.mean(x * x, axis=-1, keepdims=True) + EPS)


def _rms_bwd(n, r, dn):
    return r * (dn - n * jnp.mean(dn * n, axis=-1, keepdims=True))


def _colsum(v):
    return jnp.sum(v, axis=0, keepdims=True)


def _acc_first(i, ref, val, every=None):
    first = (i == 0) if every is None else (i % every == 0)

    @pl.when(first)
    def _():
        ref[...] = jnp.zeros_like(ref)

    ref[...] += val.reshape(ref.shape)


def _pre1(x, g, sc, sh):
    t = x.shape[0]

    def body(x_ref, g_ref, sc_ref, sh_ref, h_ref):
        xv = x_ref[...]
        n = xv * _rms(xv)
        h_ref[...] = ((n * g_ref[...]) * (1.0 + sc_ref[0]) + sh_ref[0]).astype(BF16)

    return pl.pallas_call(
        body, name="pre1", grid=(t // TM,),
        in_specs=[_row(D), _vec(D), _per_ex(D), _per_ex(D)],
        out_specs=_row(D), out_shape=jax.ShapeDtypeStruct((t, D), BF16),
        compiler_params=_cp("parallel"),
    )(x, g, sc, sh)


def _rope_fwd(v, c, sm, sp):
    return v * c + pltpu.roll(v, HP - ROPE // 2, 1) * sm + pltpu.roll(v, ROPE // 2, 1) * sp


def _rope_bwd(dv, c, sm, sp):
    return dv * c + pltpu.roll(dv * sm, ROPE // 2, 1) + pltpu.roll(dv * sp, HP - ROPE // 2, 1)


def _mla_pre(proj, g_cq, g_ckv, w_uq, w_k, w_v, rc, rsm, rsp):
    t = proj.shape[0]

    def body(tail_ref, gq_ref, gkv_ref, wuq_ref, wk_ref, wv_ref, c_ref, sm_ref, sp_ref,
             q_ref, k_ref, v_ref, cqn_ref, ckvn_ref):
        tail = tail_ref[...]
        cq, ckv, kr = tail[:, :Q_LORA], tail[:, Q_LORA:Q_LORA + KV_LORA], tail[:, Q_LORA + KV_LORA:]
        cqn = (cq * _rms(cq) * gq_ref[...]).astype(BF16)
        ckvn = (ckv * _rms(ckv) * gkv_ref[...]).astype(BF16)
        cqn_ref[...] = cqn
        ckvn_ref[...] = ckvn
        c, sm, sp = c_ref[...], sm_ref[...], sp_ref[...]
        q = _dot(cqn, wuq_ref[...], ((1,), (0,)))
        kn = _dot(ckvn, wk_ref[...], ((1,), (0,)))
        v_ref[...] = _dot(ckvn, wv_ref[...], ((1,), (0,))).astype(BF16)
        krr = _rope_fwd(kr, c, sm, sp)
        for h in range(H):
            sl = slice(h * HP, (h + 1) * HP)
            q_ref[:, sl] = _rope_fwd(q[:, sl], c, sm, sp).astype(BF16)
            k_ref[:, sl] = (kn[:, sl] + krr).astype(BF16)

    wide = H * HP
    return pl.pallas_call(
        body, name="mla_pre", grid=(t // TM,),
        in_specs=[_row_at(TAIL, TAIL0 // TAIL), _vec(Q_LORA), _vec(KV_LORA), _full((Q_LORA, wide)),
                  _full((KV_LORA, wide)), _full((KV_LORA, wide)), _pos(HP), _pos(HP), _pos(HP)],
        out_specs=[_row(wide), _row(wide), _row(wide), _row(Q_LORA), _row(KV_LORA)],
        out_shape=[jax.ShapeDtypeStruct((t, wide), BF16)] * 3
        + [jax.ShapeDtypeStruct((t, Q_LORA), BF16), jax.ShapeDtypeStruct((t, KV_LORA), BF16)],
        compiler_params=_cp("parallel"),
    )(proj, g_cq, g_ckv, w_uq, w_k, w_v, rc, rsm, rsp)


def _mla_pre_bwd(proj, dq_, dk_, dv_, g_cq, g_ckv, w_uq, w_k, w_v, rc, rsm, rsp):
    t = proj.shape[0]
    wide = H * HP

    def body(tail_ref, dq_ref, dk_ref, dv_ref, gq_ref, gkv_ref, wuq_ref, wk_ref, wv_ref, c_ref, sm_ref, sp_ref,
             dqo_ref, dko_ref, dvo_ref, dtail_ref, dgq_ref, dgkv_ref):
        i = pl.program_id(0)
        tail = tail_ref[...]
        cq, ckv = tail[:, :Q_LORA], tail[:, Q_LORA:Q_LORA + KV_LORA]
        c, sm, sp = c_ref[...], sm_ref[...], sp_ref[...]
        dkr = jnp.zeros((TM, HP), F32)
        for h in range(H):
            sl = slice(h * HP, (h + 1) * HP)
            dqo_ref[:, sl] = _rope_bwd(dq_ref[:, sl], c, sm, sp).astype(BF16)
            dkr = dkr + dk_ref[:, sl]
        lane = lax.broadcasted_iota(jnp.int32, (TM, HP), 1)
        dkr = jnp.where((lane >= NOPE) & (lane < NOPE + ROPE), _rope_bwd(dkr, c, sm, sp), 0.0)
        dkb = dk_ref[...].astype(BF16)
        dvb = dv_ref[...].astype(BF16)
        dko_ref[...] = dkb
        dvo_ref[...] = dvb
        dcqn = _dot(dqo_ref[...], wuq_ref[...], ((1,), (1,)))
        dckvn = _dot(dkb, wk_ref[...], ((1,), (1,))) + _dot(dvb, wv_ref[...], ((1,), (1,)))
        rq, rkv = _rms(cq), _rms(ckv)
        nq, nkv = cq * rq, ckv * rkv
        _acc_first(i, dgq_ref, _colsum(dcqn * nq))
        _acc_first(i, dgkv_ref, _colsum(dckvn * nkv))
        dtail_ref[:, :Q_LORA] = _rms_bwd(nq, rq, dcqn * gq_ref[...]).astype(BF16)
        dtail_ref[:, Q_LORA:Q_LORA + KV_LORA] = _rms_bwd(nkv, rkv, dckvn * gkv_ref[...]).astype(BF16)
        dtail_ref[:, Q_LORA + KV_LORA:] = dkr.astype(BF16)

    return pl.pallas_call(
        body, name="mla_pre_bwd", grid=(t // TM,),
        in_specs=[_row_at(TAIL, TAIL0 // TAIL), _row(wide), _row(wide), _row(wide), _vec(Q_LORA), _vec(KV_LORA),
                  _full((Q_LORA, wide)), _full((KV_LORA, wide)), _full((KV_LORA, wide)), _pos(HP), _pos(HP), _pos(HP)],
        out_specs=[_row(wide), _row(wide), _row(wide), _row(TAIL), _vec(Q_LORA), _vec(KV_LORA)],
        out_shape=[jax.ShapeDtypeStruct((t, wide), BF16)] * 3 + [jax.ShapeDtypeStruct((t, TAIL), BF16),
                   jax.ShapeDtypeStruct((1, Q_LORA), F32), jax.ShapeDtypeStruct((1, KV_LORA), F32)],
        compiler_params=_cp("arbitrary"),
    )(proj, dq_, dk_, dv_, g_cq, g_ckv, w_uq, w_k, w_v, rc, rsm, rsp)


def _post_attn(out_a, out_b, g_a, g_b):
    t = out_a.shape[0]

    def body(a_ref, b_ref, ga_ref, gb_ref, y_ref):
        a, b = a_ref[...], b_ref[...]
        y_ref[:, :D_A] = (a * _rms(a) * ga_ref[...]).astype(BF16)
        y_ref[:, D_A:] = (b * _rms(b) * gb_ref[...]).astype(BF16)

    return pl.pallas_call(
        body, name="post_attn", grid=(t // TM,),
        in_specs=[_row(D_A), _row(D_A), _vec(D_A), _vec(D_A)],
        out_specs=_row(D), out_shape=jax.ShapeDtypeStruct((t, D), BF16),
        compiler_params=_cp("parallel"),
    )(out_a, out_b, g_a, g_b)


def _post_attn_bwd(dy, out_a, out_b, g_a, g_b):
    t = dy.shape[0]

    def body(dy_ref, a_ref, b_ref, ga_ref, gb_ref, da_ref, db_ref, dga_ref, dgb_ref):
        i = pl.program_id(0)
        dy_ = dy_ref[...]
        for src, g_ref, dst, dg_ref, sl in ((a_ref, ga_ref, da_ref, dga_ref, slice(0, D_A)),
                                            (b_ref, gb_ref, db_ref, dgb_ref, slice(D_A, D))):
            v = src[...]
            r = _rms(v)
            n = v * r
            dyv = dy_[:, sl]
            _acc_first(i, dg_ref, _colsum(dyv * n))
            dst[...] = _rms_bwd(n, r, dyv * g_ref[...])

    return pl.pallas_call(
        body, name="post_attn_bwd", grid=(t // TM,),
        in_specs=[_row(D), _row(D_A), _row(D_A), _vec(D_A), _vec(D_A)],
        out_specs=[_row(D_A), _row(D_A), _vec(D_A), _vec(D_A)],
        out_shape=[jax.ShapeDtypeStruct((t, D_A), F32)] * 2 + [jax.ShapeDtypeStruct((1, D_A), F32)] * 2,
        compiler_params=_cp("arbitrary"),
    )(dy, out_a, out_b, g_a, g_b)


def _resid_norm2(x, mix, g1, g, sc, sh):
    t = x.shape[0]

    def body(x_ref, mix_ref, g1_ref, g_ref, sc_ref, sh_ref, x2_ref, h_ref):
        x2 = x_ref[...] + g1_ref[0] * mix_ref[...]
        x2_ref[...] = x2
        n = x2 * _rms(x2)
        h_ref[...] = ((n * g_ref[...]) * (1.0 + sc_ref[0]) + sh_ref[0]).astype(BF16)

    return pl.pallas_call(
        body, name="resid_norm2", grid=(t // TM,),
        in_specs=[_row(D), _row(D), _per_ex(D), _vec(D), _per_ex(D), _per_ex(D)],
        out_specs=[_row(D), _row(D)],
        out_shape=[jax.ShapeDtypeStruct((t, D), F32), jax.ShapeDtypeStruct((t, D), BF16)],
        compiler_params=_cp("parallel"),
    )(x, mix, g1, g, sc, sh)


def _sigmoid(v):
    return 1.0 / (1.0 + jnp.exp(-v))


def _final(x2, f, g2, g_fin, target):
    t = x2.shape[0]
    nb = t // S
    tpb = S // TM

    def body(x2_ref, f_ref, g2_ref, g_ref, t_ref, dx3_ref, df_ref, loss_ref, dgf_ref, dg2_ref):
        i = pl.program_id(0)
        fv = f_ref[...]
        x3 = x2_ref[...] + g2_ref[0] * fv
        r = _rms(x3)
        n = x3 * r
        err = n * g_ref[...] - t_ref[...]
        _acc_first(i, loss_ref, _colsum(err * err))
        dy = err * (1.0 / D)
        _acc_first(i, dgf_ref, _colsum(dy * n))
        dx3 = _rms_bwd(n, r, dy * g_ref[...])
        dx3_ref[...] = dx3
        _acc_first(i, dg2_ref, _colsum(dx3 * fv), every=tpb)
        df_ref[...] = (dx3 * g2_ref[0]).astype(BF16)

    return pl.pallas_call(
        body, name="final", grid=(t // TM,),
        in_specs=[_row(D), _row(D), _per_ex(D), _vec(D), _row(D)],
        out_specs=[_row(D), _row(D), _vec(D), _vec(D), _per_ex(D)],
        out_shape=[jax.ShapeDtypeStruct((t, D), F32), jax.ShapeDtypeStruct((t, D), BF16),
                   jax.ShapeDtypeStruct((1, D), F32), jax.ShapeDtypeStruct((1, D), F32),
                   jax.ShapeDtypeStruct((nb, 1, D), F32)],
        compiler_params=_cp("arbitrary"),
    )(x2, f, g2, g_fin, target)


def _norm_bwd(xin, dh, dres, g, sc, gate=None):
    t = xin.shape[0]
    nb = t // S
    tpb = S // TM
    gated = gate is not None

    def body(*refs):
        if gated:
            x_ref, dh_ref, dres_ref, g_ref, sc_ref, mix_ref, g1_ref, dx_ref, dsh_ref, dsc_ref, dg_ref, dg1_ref, dmix_ref = refs
        else:
            x_ref, dh_ref, dres_ref, g_ref, sc_ref, dx_ref, dsh_ref, dsc_ref, dg_ref = refs
        i = pl.program_id(0)
        xv, dhv = x_ref[...], dh_ref[...]
        r = _rms(xv)
        n = xv * r
        gv = g_ref[...]
        _acc_first(i, dsh_ref, _colsum(dhv), every=tpb)
        _acc_first(i, dsc_ref, _colsum(dhv * (n * gv)), every=tpb)
        dng = dhv * (1.0 + sc_ref[0])
        _acc_first(i, dg_ref, _colsum(dng * n))
        dx = dres_ref[...] + _rms_bwd(n, r, dng * gv)
        dx_ref[...] = dx
        if gated:
            _acc_first(i, dg1_ref, _colsum(dx * mix_ref[...]), every=tpb)
            dmix_ref[...] = (dx * g1_ref[0]).astype(BF16)

    in_specs = [_row(D), _row(D), _row(D), _vec(D), _per_ex(D)]
    out_specs = [_row(D), _per_ex(D), _per_ex(D), _vec(D)]
    out_shape = [jax.ShapeDtypeStruct((t, D), F32), jax.ShapeDtypeStruct((nb, 1, D), F32),
                 jax.ShapeDtypeStruct((nb, 1, D), F32), jax.ShapeDtypeStruct((1, D), F32)]
    args = [xin, dh, dres, g, sc]
    if gated:
        in_specs += [_row(D), _per_ex(D)]
        out_specs += [_per_ex(D), _row(D)]
        out_shape += [jax.ShapeDtypeStruct((nb, 1, D), F32), jax.ShapeDtypeStruct((t, D), BF16)]
        args += list(gate)
    return pl.pallas_call(
        body, name="norm2_bwd" if gated else "norm1_bwd", grid=(t // TM,),
        in_specs=in_specs, out_specs=out_specs, out_shape=out_shape,
        compiler_params=_cp("arbitrary"),
    )(*args)


TQ = 256
TB = 512


def _mla_fwd(q, k, v):
    t = q.shape[0]
    nb = t // S

    def body(q_ref, k_ref, v_ref, o_ref, lse_ref):
        causal = lax.broadcasted_iota(jnp.int32, (TB, TB), 0) >= lax.broadcasted_iota(jnp.int32, (TB, TB), 1)
        heads = [slice(h * HP, (h + 1) * HP) for h in range(2)]
        for i in range(S // TB):
            ri, past = slice(i * TB, (i + 1) * TB), slice(0, i * TB)
            qhs = [q_ref[ri, sl] for sl in heads]
            sd = [jnp.where(causal, _dot(qh, k_ref[ri, sl], ((1,), (1,))) * SCALE_B, NEG) for qh, sl in zip(qhs, heads)]
            ms = [jnp.max(s, axis=-1, keepdims=True) for s in sd]
            if i:
                so = [_dot(qh, k_ref[past, sl], ((1,), (1,))) * SCALE_B for qh, sl in zip(qhs, heads)]
                ms = [jnp.maximum(m, jnp.max(s, axis=-1, keepdims=True)) for m, s in zip(ms, so)]
            pd = [jnp.exp(s - m) for s, m in zip(sd, ms)]
            ls = [jnp.sum(p, axis=-1, keepdims=True) for p in pd]
            acc = [_dot(p.astype(BF16), v_ref[ri, sl], ((1,), (0,))) for p, sl in zip(pd, heads)]
            if i:
                po = [jnp.exp(s - m) for s, m in zip(so, ms)]
                ls = [l + jnp.sum(p, axis=-1, keepdims=True) for l, p in zip(ls, po)]
                acc = [a + _dot(p.astype(BF16), v_ref[past, sl], ((1,), (0,))) for a, p, sl in zip(acc, po, heads)]
            o_ref[ri, :] = acc[0] / ls[0] + acc[1] / ls[1]
            for sl, m, l in zip(heads, ms, ls):
                lse_ref[ri, sl] = jnp.broadcast_to(m + jnp.log(l), (TB, HP))

    wide2 = pl.BlockSpec((S, 2 * HP), lambda b, p: (b, p))
    return pl.pallas_call(
        body, name="mla_fwd", grid=(nb, H // 2),
        in_specs=[wide2, wide2, wide2],
        out_specs=[pl.BlockSpec((S, HP), lambda b, p: (b, p)), wide2],
        out_shape=[jax.ShapeDtypeStruct((t, H * VDIM), F32), jax.ShapeDtypeStruct((t, H * HP), F32)],
        compiler_params=_cp("parallel", "parallel"),
    )(q, k, v)


def _mla_bwd(q, k, v, o, do, lse):
    t = q.shape[0]
    nb = t // S
    nq = S // TQ

    def body(q_ref, k_ref, v_ref, o_ref, do_ref, lse_ref, dq_ref, dk_ref, dv_ref):
        lane = lax.broadcasted_iota(jnp.int32, (TB, HP), 1)
        causal = lax.broadcasted_iota(jnp.int32, (TB, TB), 0) >= lax.broadcasted_iota(jnp.int32, (TB, TB), 1)
        heads = [slice(h * HP, (h + 1) * HP) for h in range(2)]
        nblk = S // TB
        for i in reversed(range(nblk)):
            ri, past = slice(i * TB, (i + 1) * TB), slice(0, i * TB)
            dov = do_ref[ri, :]
            prod = dov * o_ref[ri, :]
            dob = dov.astype(BF16)
            deltas = [jnp.sum(jnp.where((lane < VDIM) if h == 0 else (lane >= VDIM), prod, 0.0), axis=-1, keepdims=True)
                      for h in range(2)]
            qhs = [q_ref[ri, sl] for sl in heads]
            lses = [lse_ref[ri, sl][:, :1] for sl in heads]
            for rows, diagonal in ((ri, True), (past, False)):
                if rows.stop == rows.start:
                    continue
                ps = [jnp.exp(_dot(qh, k_ref[rows, sl], ((1,), (1,))) * SCALE_B - lse) for qh, sl, lse in zip(qhs, heads, lses)]
                if diagonal:
                    ps = [jnp.where(causal, p, 0.0) for p in ps]
                dps = [_dot(dob, v_ref[rows, sl], ((1,), (1,))) for sl in heads]
                dss = [(p * (dp - delta) * SCALE_B).astype(BF16) for p, dp, delta in zip(ps, dps, deltas)]
                for sl, qh, p, ds in zip(heads, qhs, ps, dss):
                    dq = _dot(ds, k_ref[rows, sl], ((1,), (0,)))
                    dk = _dot(ds, qh, ((0,), (0,)))
                    dv = _dot(p.astype(BF16), dob, ((0,), (0,)))
                    if diagonal:
                        dq_ref[ri, sl] = dq
                    else:
                        dq_ref[ri, sl] += dq
                    if i == nblk - 1:
                        dk_ref[rows, sl] = dk
                        dv_ref[rows, sl] = dv
                    else:
                        dk_ref[rows, sl] += dk
                        dv_ref[rows, sl] += dv

    wide2 = pl.BlockSpec((S, 2 * HP), lambda b, p: (b, p))
    pair = pl.BlockSpec((S, HP), lambda b, p: (b, p))
    return pl.pallas_call(
        body, name="mla_bwd", grid=(nb, H // 2),
        in_specs=[wide2, wide2, wide2, pair, pair, wide2],
        out_specs=[wide2, wide2, wide2],
        out_shape=[jax.ShapeDtypeStruct((t, H * HP), F32)] * 3,
        compiler_params=_cp("parallel", "parallel"),
    )(q, k, v, o, do, lse)


def _t5_bucket(dist):
    max_exact = N_BUCKETS // 2
    d = np.maximum(dist, 1).astype(np.float64)
    large = max_exact + (np.log(d / max_exact) / np.log(MAX_DISTANCE / max_exact) * (N_BUCKETS - max_exact)).astype(np.int64)
    large = np.minimum(large, N_BUCKETS - 1)
    return np.where(dist < max_exact, dist, large).astype(np.int32)


def _band_geometry():
    a = np.arange(BLK)[:, None]
    bk = np.arange(2 * BLK)[None, :]
    steps = BLK + a - bk
    valid = (steps >= 0) & (steps <= BLK)
    buckets = np.stack([_t5_bucket(np.clip(steps, 0, BLK) * d) for d in DILATIONS])
    return buckets, valid


def _band_bias(rel_bias):
    buckets, valid = _band_geometry()
    onehot = (jnp.asarray(buckets)[..., None] == jnp.arange(N_BUCKETS)).astype(F32)
    bias = jnp.einsum("rqkn,nh->rhqk", onehot, rel_bias, precision=lax.Precision.HIGHEST)
    bias = jnp.where(jnp.asarray(valid)[None, None], bias, NEG)
    return bias.reshape(3, H // 2, 2 * BLK, 2 * BLK)


def _dil_items():
    items = []
    for r, d in enumerate(DILATIONS):
        for res in range(d):
            for blk in range(S // d // BLK):
                items.append((r, d, blk * BLK * d + res, blk > 0))
    return items


GROUP = 4


def _strided(start, d):
    return pl.ds(start, BLK) if d == 1 else pl.ds(start, BLK, stride=d)


def _stack_heads(tile, own):
    return jnp.where(own, jnp.concatenate([tile, tile], axis=0), 0.0).astype(BF16)


def _own_lanes():
    row = lax.broadcasted_iota(jnp.int32, (2 * BLK, HP), 0)
    lane = lax.broadcasted_iota(jnp.int32, (2 * BLK, HP), 1)
    return (lane < E_A) == (row < BLK)


def _dil_fwd(proj, biasm):
    t = proj.shape[0]
    nb = t // S

    def body(q_ref, k_ref, v_ref, b_ref, o_ref, lse_ref, ob_ref, lb_ref):
        lane = lax.broadcasted_iota(jnp.int32, (BLK, HP), 1)
        own = _own_lanes()
        items = _dil_items()
        for g in range(0, len(items), GROUP):
            grp = items[g:g + GROUP]
            ss, vts = [], []
            for r, d, start, has_prev in grp:
                cur = _strided(start, d)
                rows = [_strided(start - BLK * d, d), cur] if has_prev else [cur]
                q2 = _stack_heads(q_ref[cur, :], own)
                kt = jnp.concatenate([k_ref[x, :] for x in rows], axis=0).astype(BF16)
                vts.append(jnp.concatenate([v_ref[x, :] for x in rows], axis=0).astype(BF16))
                bias = b_ref[r, 0] if has_prev else b_ref[r, 0, :, BLK:]
                ss.append(_dot(q2, kt, ((1,), (1,))) * SCALE_A + bias)
            ms = [jnp.max(s, axis=-1, keepdims=True) for s in ss]
            ps = [jnp.exp(s - m) for s, m in zip(ss, ms)]
            ls = [jnp.sum(p, axis=-1, keepdims=True) for p in ps]
            for (r, d, start, _), p, vt, m, l in zip(grp, ps, vts, ms, ls):
                cur = _strided(start, d)
                o2 = _dot(p.astype(BF16), vt, ((1,), (0,))) / l
                lse2 = m + jnp.log(l)
                ob_ref[r, cur, :] = jnp.where(lane < E_A, o2[:BLK], o2[BLK:])
                lb_ref[r, cur, :] = jnp.where(lane < E_A, lse2[:BLK], lse2[BLK:])

        def merge(c, _):
            rows = pl.ds(pl.multiple_of(c * TQ, TQ), TQ)
            l0, l1, l2 = lb_ref[0, rows, :], lb_ref[1, rows, :], lb_ref[2, rows, :]
            m = jnp.maximum(jnp.maximum(l0, l1), l2)
            e0, e1, e2 = jnp.exp(l0 - m), jnp.exp(l1 - m), jnp.exp(l2 - m)
            tot = e0 + e1 + e2
            o_ref[rows, :] = (e0 * ob_ref[0, rows, :] + e1 * ob_ref[1, rows, :] + e2 * ob_ref[2, rows, :]) / tot
            lse_ref[rows, :] = m + jnp.log(tot)
            return 0

        lax.fori_loop(0, S // TQ, merge, 0)

    npair = H // 2
    return pl.pallas_call(
        body, name="dil_fwd", grid=(nb, npair),
        in_specs=[pl.BlockSpec((S, HP), lambda b, p: (b, p)), pl.BlockSpec((S, HP), lambda b, p: (b, npair + p)),
                  pl.BlockSpec((S, HP), lambda b, p: (b, 2 * npair + p)),
                  pl.BlockSpec((3, 1, 2 * BLK, 2 * BLK), lambda b, p: (0, p, 0, 0))],
        out_specs=[pl.BlockSpec((S, HP), lambda b, p: (b, p))] * 2,
        out_shape=[jax.ShapeDtypeStruct((t, D_A), F32)] * 2,
        scratch_shapes=[pltpu.VMEM((3, S, HP), F32), pltpu.VMEM((3, S, HP), F32)],
        compiler_params=_cp("parallel", "parallel"),
    )(proj, proj, proj, biasm)


def _dil_bwd(proj, biasm, o, do, lse):
    t = proj.shape[0]
    nb = t // S

    def body(q_ref, k_ref, v_ref, b_ref, o_ref, do_ref, lse_ref, dq_ref, dk_ref, dv_ref, ds_ref):
        dq_ref[...] = jnp.zeros_like(dq_ref)
        dk_ref[...] = jnp.zeros_like(dk_ref)
        dv_ref[...] = jnp.zeros_like(dv_ref)
        ds_ref[...] = jnp.zeros_like(ds_ref)
        lane = lax.broadcasted_iota(jnp.int32, (BLK, HP), 1)
        own = _own_lanes()
        items = _dil_items()
        for g in range(0, len(items), GROUP):
            grp = items[g:g + GROUP]
            q2s, kts, do2s, ss, dps, lse2s, delta2s = [], [], [], [], [], [], []
            for r, d, start, has_prev in grp:
                cur = _strided(start, d)
                rows = [_strided(start - BLK * d, d), cur] if has_prev else [cur]
                q2 = _stack_heads(q_ref[cur, :], own)
                kt = jnp.concatenate([k_ref[x, :] for x in rows], axis=0).astype(BF16)
                vt = jnp.concatenate([v_ref[x, :] for x in rows], axis=0).astype(BF16)
                dot_ = do_ref[cur, :]
                prod = dot_ * o_ref[cur, :]
                lset = lse_ref[cur, :]
                do2 = _stack_heads(dot_, own)
                bias = b_ref[r, 0] if has_prev else b_ref[r, 0, :, BLK:]
                ss.append(_dot(q2, kt, ((1,), (1,))) * SCALE_A + bias)
                dps.append(_dot(do2, vt, ((1,), (1,))))
                lse2s.append(jnp.concatenate([lset[:, :1], lset[:, E_A:E_A + 1]], axis=0))
                delta2s.append(jnp.concatenate([jnp.sum(jnp.where(lane < E_A, prod, 0.0), axis=-1, keepdims=True),
                                                jnp.sum(jnp.where(lane >= E_A, prod, 0.0), axis=-1, keepdims=True)], axis=0))
                q2s.append(q2)
                kts.append(kt)
                do2s.append(do2)
            ps = [jnp.exp(s - lse2) for s, lse2 in zip(ss, lse2s)]
            dls = [p * (dp - delta2) for p, dp, delta2 in zip(ps, dps, delta2s)]
            for (r, d, start, has_prev), q2, kt, do2, p, dl in zip(grp, q2s, kts, do2s, ps, dls):
                cur = _strided(start, d)
                dsb = (dl * SCALE_A).astype(BF16)
                dq2 = _dot(dsb, kt, ((1,), (0,)))
                dkt = _dot(dsb, q2, ((0,), (0,)))
                dvt = _dot(p.astype(BF16), do2, ((0,), (0,)))
                dq_ref[cur, :] += jnp.where(lane < E_A, dq2[:BLK], dq2[BLK:])
                if has_prev:
                    prev = _strided(start - BLK * d, d)
                    ds_ref[0, r, 0] += dl
                    dk_ref[prev, :] += dkt[:BLK]
                    dv_ref[prev, :] += dvt[:BLK]
                    dk_ref[cur, :] += dkt[BLK:]
                    dv_ref[cur, :] += dvt[BLK:]
                else:
                    ds_ref[0, r, 0, :, BLK:] += dl
                    dk_ref[cur, :] += dkt
                    dv_ref[cur, :] += dvt

    npair = H // 2
    pair = pl.BlockSpec((S, HP), lambda b, p: (b, p))
    return pl.pallas_call(
        body, name="dil_bwd", grid=(nb, npair),
        in_specs=[pair, pl.BlockSpec((S, HP), lambda b, p: (b, npair + p)),
                  pl.BlockSpec((S, HP), lambda b, p: (b, 2 * npair + p)),
                  pl.BlockSpec((3, 1, 2 * BLK, 2 * BLK), lambda b, p: (0, p, 0, 0)), pair, pair, pair],
        out_specs=[pair, pair, pair, pl.BlockSpec((1, 3, 1, 2 * BLK, 2 * BLK), lambda b, p: (b, 0, p, 0, 0))],
        out_shape=[jax.ShapeDtypeStruct((t, D_A), F32)] * 3 + [jax.ShapeDtypeStruct((nb, 3, npair, 2 * BLK, 2 * BLK), F32)],
        compiler_params=_cp("parallel", "parallel"),
    )(proj, proj, proj, biasm, o, do, lse)


def _rel_bias_grad(dlogits):
    nb = dlogits.shape[0]
    buckets, _ = _band_geometry()
    kk = 3 * BLK * 2 * BLK
    dl = jnp.transpose(dlogits.reshape(nb, 3, H, BLK, 2 * BLK), (0, 2, 1, 3, 4)).reshape(nb, H, kk)
    bk = jnp.asarray(buckets.reshape(1, kk))
    tk = kk // 12

    def body(dl_ref, bk_ref, o_ref):
        j = pl.program_id(0)
        onehot = (bk_ref[...] == lax.broadcasted_iota(jnp.int32, (N_BUCKETS, tk), 0)).astype(F32)
        tot = dl_ref[0]
        for b in range(1, nb):
            tot = tot + dl_ref[b]
        part = lax.dot_general(onehot, tot, ((((1,), (1,))), ((), ())), preferred_element_type=F32,
                               precision=lax.Precision.HIGHEST)
        _acc_first(j, o_ref, part)

    return pl.pallas_call(
        body, name="rel_bias_grad", grid=(kk // tk,),
        in_specs=[pl.BlockSpec((nb, H, tk), lambda j: (0, 0, j)), pl.BlockSpec((1, tk), lambda j: (0, j))],
        out_specs=pl.BlockSpec((N_BUCKETS, H), lambda j: (0, 0)),
        out_shape=jax.ShapeDtypeStruct((N_BUCKETS, H), F32),
        compiler_params=_cp("arbitrary"),
    )(dl, bk)


def _mesh_place():
    x, y, c = lax.axis_index("x"), lax.axis_index("y"), lax.axis_index("c")
    return x, y, c


def _peer(k):
    x, y, c = _mesh_place()
    px = 1 - x if k & 4 else x
    py = 1 - y if k & 2 else y
    pc = 1 - c if k & 1 else c
    return (px, py, pc), 4 * px + 2 * py + pc


ANY = pl.BlockSpec(memory_space=pl.ANY)


def _exchange(arrays, gathers, name, after=None):
    n_arr = len(arrays)

    def body(*refs):
        ins, outs = refs[:n_arr], refs[n_arr + 1:2 * n_arr + 1]
        send, recv, loc = refs[2 * n_arr + 1:]
        x, y, c = _mesh_place()
        me = 4 * x + 2 * y + c
        local = [pltpu.make_async_copy(ins[a] if gathers[a] else ins[a].at[me], outs[a].at[me], loc.at[a])
                 for a in range(n_arr)]
        remote = _peer_copies(ins, outs, send, recv, gathers)
        for cp in local:
            cp.start()
        for put, _ in remote:
            put.start()
        for cp in local:
            cp.wait()
        for put, got in remote:
            put.wait_send()
            got.wait_recv()

    return pl.pallas_call(
        body, name=name,
        in_specs=[ANY] * (n_arr + 1), out_specs=[ANY] * n_arr,
        out_shape=[jax.ShapeDtypeStruct(((N_DEV,) if g else ()) + a.shape, a.dtype) for a, g in zip(arrays, gathers)],
        scratch_shapes=[pltpu.SemaphoreType.DMA((n_arr * (N_DEV - 1),)), pltpu.SemaphoreType.DMA((n_arr * (N_DEV - 1),)),
                        pltpu.SemaphoreType.DMA((n_arr,))],
        compiler_params=pltpu.CompilerParams(has_side_effects=True),
    )(*arrays, arrays[0] if after is None else after)


HBM = pl.BlockSpec(memory_space=pltpu.HBM)
SEM = pl.BlockSpec(memory_space=pltpu.SEMAPHORE)
DATAFLOW = pltpu.SideEffectType.DATAFLOW_SIDE_EFFECTING


def _own_block_in_place(block, me):
    land = jnp.zeros((N_DEV,) + block.shape, block.dtype)
    return lax.dynamic_update_slice(land, block[None], (me,) + (0,) * block.ndim)


def _peer_copies(srcs, lands, send, recv, gathers):
    x, y, c = _mesh_place()
    me = 4 * x + 2 * y + c
    out = []
    for a, (src, land) in enumerate(zip(srcs, lands)):
        for k in range(1, N_DEV):
            dev, idx = _peer(k)
            sem = a * (N_DEV - 1) + k - 1
            mine = src if gathers[a] else src.at[idx]
            put = pltpu.make_async_remote_copy(mine, land.at[me], send.at[sem], recv.at[sem],
                                               device_id=dev, device_id_type=pl.DeviceIdType.MESH)
            got = pltpu.make_async_remote_copy(mine, land.at[idx], send.at[sem], recv.at[sem],
                                               device_id=dev, device_id_type=pl.DeviceIdType.MESH)
            out.append((put, got))
    return out


def _exchange_start(srcs, lands, gather, after, name):
    n = len(srcs)

    def body(*refs):
        srcs_, lands_, send, recv = refs[:n], refs[n:2 * n], refs[2 * n + 1], refs[2 * n + 2]
        for put, _ in _peer_copies(srcs_, lands_, send, recv, gather):
            put.start()
        refs[-1][...] = jnp.zeros_like(refs[-1])

    nsem = n * (N_DEV - 1)
    thru = [pltpu.HBM(a.shape, a.dtype) for a in list(srcs) + list(lands)]
    res = pl.pallas_call(
        body, name=name,
        out_shape=(pltpu.SemaphoreType.DMA((nsem,)), pltpu.SemaphoreType.DMA((nsem,)), *thru, jax.ShapeDtypeStruct((8, 128), F32)),
        in_specs=[HBM] * (2 * n) + [ANY],
        out_specs=(SEM, SEM, *([HBM] * (2 * n)), pl.BlockSpec(memory_space=pltpu.VMEM)),
        input_output_aliases={i: 2 + i for i in range(2 * n)},
        compiler_params=pltpu.CompilerParams(has_side_effects=DATAFLOW),
    )(*[pltpu.with_memory_space_constraint(a, pltpu.HBM) for a in list(srcs) + list(lands)], after)
    return res[0], res[1], list(res[2:2 + n]), list(res[2 + n:2 + 2 * n]), res[-1]


def _exchange_wait(send, recv, srcs, lands, gather, after, name):
    n = len(srcs)

    def body(*refs):
        srcs_, lands_, send_, recv_ = refs[:n], refs[n:2 * n], refs[2 * n], refs[2 * n + 1]
        for put, got in _peer_copies(srcs_, lands_, send_, recv_, gather):
            put.wait_send()
            got.wait_recv()

    thru = [pltpu.HBM(a.shape, a.dtype) for a in list(srcs) + list(lands)]
    res = pl.pallas_call(
        body, name=name, out_shape=tuple(thru),
        in_specs=[HBM] * (2 * n) + [SEM, SEM, ANY], out_specs=tuple([HBM] * (2 * n)),
        input_output_aliases={i: i for i in range(2 * n)},
        compiler_params=pltpu.CompilerParams(has_side_effects=DATAFLOW),
    )(*srcs, *lands, send, recv, after)
    return list(res[n:])


def _silu_rows(c):
    def body(c_ref, o_ref):
        v = c_ref[...]
        o_ref[...] = v * _sigmoid(v)

    return pl.pallas_call(body, name="cond", out_shape=jax.ShapeDtypeStruct(c.shape, F32))(c)


def _mod_slab(cond_all, w_ada, b_slab):
    def body(c_ref, w_ref, b_ref, o_ref):
        o_ref[...] = _dot(c_ref[...].astype(BF16), w_ref[...].astype(BF16), ((1,), (0,))) + b_ref[...]

    return pl.pallas_call(body, name="mod_slab",
                          out_shape=jax.ShapeDtypeStruct((cond_all.shape[0], w_ada.shape[1]), F32),
                          compiler_params=pltpu.CompilerParams(vmem_limit_bytes=VMEM_LIMIT))(cond_all, w_ada, b_slab)


def _ada_grad(cond_all, dmod_cols):
    def body(c_ref, d_ref, o_ref):
        o_ref[...] = _dot(c_ref[...].astype(BF16), d_ref[...].astype(BF16), ((0,), (0,)))

    return pl.pallas_call(body, name="ada_grad",
                          out_shape=jax.ShapeDtypeStruct((cond_all.shape[1], dmod_cols.shape[1]), F32),
                          compiler_params=pltpu.CompilerParams(vmem_limit_bytes=VMEM_LIMIT))(cond_all, dmod_cols)


def _adam_math(g, w, m, v):
    m2 = B1 * m + (1.0 - B1) * g
    v2 = B2 * v + (1.0 - B2) * (g * g)
    m_hat = m2 / (1.0 - B1 ** STEP)
    v_hat = v2 / (1.0 - B2 ** STEP)
    return -LR * (m_hat / (jnp.sqrt(v_hat) + ADAM_EPS) + WD * w), m2, v2


def _adamw(parts, w, m, v, name):
    n, rows, cols = parts.shape
    tr = _pick(rows, (128, 96, 64, 32, 16, 8))

    def body(p_ref, w_ref, m_ref, v_ref, g_ref, d_ref, m2_ref, v2_ref):
        g = p_ref[0].astype(F32)
        for s in range(1, n):
            g = g + p_ref[s].astype(F32)
        g_ref[...] = g
        d_ref[...], m2_ref[...], v2_ref[...] = _adam_math(g, w_ref[...], m_ref[...], v_ref[...])

    blk = pl.BlockSpec((tr, cols), lambda i: (i, 0))
    return pl.pallas_call(
        body, name=name, grid=(rows // tr,),
        in_specs=[pl.BlockSpec((n, tr, cols), lambda i: (0, i, 0)), blk, blk, blk],
        out_specs=[blk] * 4, out_shape=[jax.ShapeDtypeStruct((rows, cols), F32)] * 4,
        compiler_params=_cp("parallel"),
    )(parts, w, m, v)


ROW_PARAMS = (("g_norm1", D), ("g_cq", Q_LORA), ("g_ckv", KV_LORA), ("g_out_a", D_A), ("g_out_b", D_A), ("g_norm2", D),
              ("g_final", D))
LOSS_ROW = N_MOD + len(ROW_PARAMS)
PAY_ROWS = 16
NCOL = N_MOD * D // N_DEV


def _pack_small(dmods, rows, loss_cols):
    nb = dmods[0].shape[0]
    nrow = len(ROW_PARAMS)

    def body(*refs):
        dm, rw, loss_ref, pay_ref, blk_ref = refs[:N_MOD], refs[N_MOD:N_MOD + nrow], refs[N_MOD + nrow], refs[-2], refs[-1]
        pay_ref[...] = jnp.zeros_like(pay_ref)
        for k in range(N_MOD):
            tot = dm[k][0]
            for b in range(1, nb):
                tot = tot + dm[k][b]
            pay_ref[k:k + 1, :] = tot
        for i, (_, n) in enumerate(ROW_PARAMS):
            pay_ref[N_MOD + i:N_MOD + i + 1, :n] = rw[i][...]
        pay_ref[LOSS_ROW:LOSS_ROW + 1, :] = loss_ref[...]
        for j in range(N_DEV):
            done = 0
            while done < NCOL:
                seg, off = divmod(j * NCOL + done, D)
                ln = min(NCOL - done, D - off)
                for b in range(nb):
                    blk_ref[j, b:b + 1, done:done + ln] = dm[seg][b][:, off:off + ln]
                done += ln

    return pl.pallas_call(
        body, name="pack_small",
        out_shape=[jax.ShapeDtypeStruct((PAY_ROWS, D), F32), jax.ShapeDtypeStruct((N_DEV, nb, NCOL), F32)],
    )(*dmods, *rows, loss_cols)


def _small_update(pay, rel, ws, ms, vs):
    n_par = len(ws)

    def body(*refs):
        pay_ref, rel_ref = refs[:2]
        w_refs, m_refs, v_refs = (refs[2 + s * n_par:2 + (s + 1) * n_par] for s in range(3))
        outs, loss_ref = refs[2 + 3 * n_par:-1], refs[-1]
        tot, rtot = pay_ref[0], rel_ref[0]
        for s in range(1, N_DEV):
            tot, rtot = tot + pay_ref[s], rtot + rel_ref[s]

        def update(p, g, sl):
            outs[4 * p][:, sl] = g
            outs[4 * p + 1][:, sl], outs[4 * p + 2][:, sl], outs[4 * p + 3][:, sl] = _adam_math(
                g, w_refs[p][:, sl], m_refs[p][:, sl], v_refs[p][:, sl])

        for k in range(N_MOD):
            update(0, tot[k:k + 1, :], slice(k * D, (k + 1) * D))
        for i, (_, n) in enumerate(ROW_PARAMS):
            update(1 + i, tot[N_MOD + i:N_MOD + i + 1, :n], slice(0, n))
        update(n_par - 1, rtot, slice(0, H))
        loss_ref[...] = jnp.broadcast_to((0.5 / D) * jnp.sum(tot[LOSS_ROW:LOSS_ROW + 1, :]), loss_ref.shape)

    shapes = [jax.ShapeDtypeStruct(w.shape, F32) for w in ws for _ in range(4)]
    res = pl.pallas_call(
        body, name="small_update", out_shape=shapes + [jax.ShapeDtypeStruct((8, 128), F32)],
    )(pay, rel, *ws, *ms, *vs)
    return [tuple(res[4 * p:4 * p + 4]) for p in range(n_par)], res[-1]


def _cols_from_blocks(g):
    return jnp.transpose(g, (1, 0, 2)).reshape(g.shape[1], N_DEV * g.shape[2])


def _cols_to_blocks(w):
    r, c = w.shape
    return jnp.transpose(w.reshape(r, N_DEV, c // N_DEV), (1, 0, 2))


def _pad_w_in(w):
    z = jnp.zeros((w.shape[0], NOPE), w.dtype)
    return jnp.concatenate([w[:, :P_IN - ROPE], z, w[:, P_IN - ROPE:], z[:, :HP - NOPE - ROPE]], axis=1)


def _unpad_w_in(g):
    k0 = P_IN - ROPE + NOPE
    return jnp.concatenate([g[:, :P_IN - ROPE], g[:, k0:k0 + ROPE]], axis=1)


def _pad_w_uq(w):
    w3 = w.reshape(Q_LORA, H, NOPE + ROPE)
    return jnp.pad(w3, ((0, 0), (0, 0), (0, HP - NOPE - ROPE))).reshape(Q_LORA, H * HP)


def _unpad_w_uq(g):
    return g.reshape(Q_LORA, H, HP)[:, :, :NOPE + ROPE].reshape(Q_LORA, H * (NOPE + ROPE))


def _split_w_ukv(w):
    w4 = w.reshape(KV_LORA, H // 2, 2, HP)
    z = jnp.zeros((KV_LORA, H // 2, NOPE), w.dtype)
    kn, vv = w4[..., :NOPE], w4[..., NOPE:]
    w_k = jnp.stack([jnp.concatenate([kn[:, :, 0], z], -1), jnp.concatenate([kn[:, :, 1], z], -1)], axis=2)
    w_v = jnp.stack([jnp.concatenate([vv[:, :, 0], z], -1), jnp.concatenate([z, vv[:, :, 1]], -1)], axis=2)
    return w_k.reshape(KV_LORA, H * HP), w_v.reshape(KV_LORA, H * HP)


def _join_w_ukv(g_k, g_v):
    gk = g_k.reshape(KV_LORA, H // 2, 2, HP)
    gv = g_v.reshape(KV_LORA, H // 2, 2, HP)
    even = jnp.concatenate([gk[:, :, 0, :NOPE], gv[:, :, 0, :VDIM]], -1)
    odd = jnp.concatenate([gk[:, :, 1, :NOPE], gv[:, :, 1, VDIM:]], -1)
    return jnp.stack([even, odd], axis=2).reshape(KV_LORA, H * HP)


def _rope_tables():
    half = ROPE // 2
    inv = ROPE_THETA ** (-jnp.arange(half, dtype=F32) / half)
    ang = jnp.arange(S, dtype=F32)[:, None] * inv[None, :]
    cos, sin = jnp.cos(ang), jnp.sin(ang)
    ones, zeros = jnp.ones((S, NOPE), F32), jnp.zeros((S, NOPE), F32)
    tail1, tail0 = jnp.ones((S, HP - NOPE - ROPE), F32), jnp.zeros((S, HP - NOPE - ROPE), F32)
    zh = jnp.zeros((S, half), F32)
    c = jnp.concatenate([ones, cos, cos, tail1], axis=1)
    sm = jnp.concatenate([zeros, -sin, zh, tail0], axis=1)
    sp = jnp.concatenate([zeros, zh, sin, tail0], axis=1)
    return c, sm, sp


def _local_step(x, mod, target, g_norm1, w_in_p, g_cq, w_uq_p, g_ckv, w_k, w_v, rel_bias, g_out_a, g_out_b, w_out,
                g_norm2, w_ffn_in, w_ffn_out, g_final, late_weights=None, on_ffn_grads=None, on_last_grads=None):
    nb = x.shape[0] // S
    sh1, sc1, g1, sh2, sc2, g2 = (mod[:, n].reshape(nb, 1, D) for n in range(N_MOD))
    rc, rsm, rsp = _rope_tables()
    biasm = _band_bias(rel_bias)

    h1 = _pre1(x, g_norm1, sc1, sh1)
    proj = _mm_nn(h1, w_in_p, F32, "proj")
    q, k, v, cqn, ckvn = _mla_pre(proj, g_cq, g_ckv, w_uq_p, w_k, w_v, rc, rsm, rsp)
    out_b, lse_b = _mla_fwd(q, k, v)
    out_a, lse_a = _dil_fwd(proj, biasm)
    y = _post_attn(out_a, out_b, g_out_a, g_out_b)
    if late_weights is not None:
        w_out, w_ffn_in, w_ffn_out = late_weights(y)
    mix = _mm_nn(y, w_out, F32, "mix")
    x2, h2 = _resid_norm2(x, mix, g1, g_norm2, sc2, sh2)
    ffn_g, ffn_u, act = _ffn_in(h2, w_ffn_in)
    f = _mm_nn(act, w_ffn_out, F32, "ffn_out")
    dx3, df, loss_cols, dg_final, dg2 = _final(x2, f, g2, g_final, target)

    dg_, du_ = _d_act(df, w_ffn_out, ffn_g, ffn_u)
    gw_ffn_out = _mm_tn(act, [df], "gw_ffn_out")
    dh2 = _d_h2(dg_, du_, w_ffn_in)
    gw_ffn_in = _mm_tn(h2, [dg_, du_], "gw_ffn_in")
    dx2, dsh2, dsc2, dg_norm2, dg1, dmix = _norm_bwd(x2, dh2, dx3, g_norm2, sc2, gate=(mix, g1))
    dy = _mm_nt(dmix, w_out, F32, "d_y")
    gw_out = _mm_tn(y, [dmix], "gw_out")
    if on_ffn_grads is not None:
        g_out_a = g_out_a + on_ffn_grads(gw_ffn_in, gw_ffn_out, gw_out)
    dout_a, dout_b, dg_out_a, dg_out_b = _post_attn_bwd(dy, out_a, out_b, g_out_a, g_out_b)
    dq_b, dk_b, dv_b = _mla_bwd(q, k, v, out_b, dout_b, lse_b)
    dq_a, dk_a, dv_a, dlogits = _dil_bwd(proj, biasm, out_a, dout_a, lse_a)
    g_rel = _rel_bias_grad(dlogits)
    dqr, dkr, dvr, dtail, dg_cq, dg_ckv = _mla_pre_bwd(proj, dq_b, dk_b, dv_b, g_cq, g_ckv, w_uq_p, w_k, w_v, rc, rsm, rsp)
    gw_uq = _mm_tn(cqn, [dqr], "gw_uq")
    gw_k, gw_v = _mm_tn(ckvn, [dkr, dvr], "gw_kv")
    dproj = jnp.concatenate([dq_a.astype(BF16), dk_a.astype(BF16), dv_a.astype(BF16), dtail], axis=1)
    dh1 = _mm_nt(dproj, w_in_p, F32, "d_h1")
    gw_in = _mm_tn(h1, [dproj], "gw_in")
    if on_last_grads is not None:
        g_norm1 = g_norm1 + on_last_grads(dict(w_in=gw_in, w_uq=gw_uq, w_k=gw_k, w_v=gw_v))
    grad_x, dsh1, dsc1, dg_norm1 = _norm_bwd(x, dh1, dx2, g_norm1, sc1)

    dmod = [dsh1, dsc1, dg1, dsh2, dsc2, dg2]
    small = dict(g_norm1=dg_norm1, g_cq=dg_cq, g_ckv=dg_ckv, rel_bias=g_rel, g_out_a=dg_out_a, g_out_b=dg_out_b,
                 g_norm2=dg_norm2, g_final=dg_final)
    big = dict(w_in=gw_in, w_uq=gw_uq, w_k=gw_k, w_v=gw_v, w_out=gw_out, w_ffn_in=gw_ffn_in, w_ffn_out=gw_ffn_out)
    return grad_x, dmod, loss_cols, small, big


def kernel(x, c, w_ada, b_ada, g_norm1, w_in, g_cq, w_uq, g_ckv, w_ukv, rel_bias, g_out_a, g_out_b, w_out, g_norm2, w_ffn_in, w_ffn_out, g_final, loss_target, m_w_ada, m_b_ada, m_g_norm1, m_w_in, m_g_cq, m_w_uq, m_g_ckv, m_w_ukv, m_rel_bias, m_g_out_a, m_g_out_b, m_w_out, m_g_norm2, m_w_ffn_in, m_w_ffn_out, m_g_final, v_w_ada, v_b_ada, v_g_norm1, v_w_in, v_g_cq, v_w_uq, v_g_ckv, v_w_ukv, v_rel_bias, v_g_out_a, v_g_out_b, v_w_out, v_g_norm2, v_w_ffn_in, v_w_ffn_out, v_g_final):
    nb = x.shape[0]
    t = nb * S
    xt, tt = x.reshape(t, D), loss_target.reshape(t, D)
    me = 4 * lax.axis_index("x") + 2 * lax.axis_index("y") + lax.axis_index("c")

    early = [w_in[0], w_uq[0], w_ukv[0]]
    gathered = _exchange([_silu_rows(c)] + [s.astype(BF16) for s in early], [True] * 4, "gather_weights")
    cond_all = gathered[0].reshape(N_DEV * nb, D)
    w_in_f, w_uq_f, w_ukv_f = (_cols_from_blocks(g) for g in gathered[1:4])
    w_k, w_v = _split_w_ukv(w_ukv_f)

    ncol = N_MOD * D // N_DEV
    b_slab = lax.dynamic_slice(b_ada, (0, me * ncol), (1, ncol))
    slab = _mod_slab(cond_all, w_ada[0], b_slab)
    (mod_rows,) = _exchange([slab.reshape(N_DEV, nb, ncol)], [False], "scatter_mod")
    mod = jnp.transpose(mod_rows, (1, 0, 2)).reshape(nb, N_MOD, D)

    late = [s.astype(BF16) for s in (w_out[0], w_ffn_in[0], w_ffn_out[0])]
    late_send, late_recv, late_src, late_land, late_token = _exchange_start(
        late, [_own_block_in_place(s, me) for s in late], [True] * 3, mod_rows, "gather_late_start")
    g_norm1_t = g_norm1 + late_token[:1, :1]

    def late_weights(after):
        w_out_g, w_ffn_in_g, w_ffn_out_g = _exchange_wait(late_send, late_recv, late_src, late_land, [True] * 3, after,
                                                          "gather_late_wait")
        return w_out_g.reshape(D, D), _cols_from_blocks(w_ffn_in_g), w_ffn_out_g.reshape(D_FF, D)

    flight = {}

    def start_grads(key, src, name):
        land = [_own_block_in_place(lax.dynamic_index_in_dim(s, me, 0, keepdims=False), me) for s in src]
        send, recv, src, land, token = _exchange_start(src, land, [False] * len(src), src[0], name)
        flight[key] = (send, recv, src, land)
        return token[:1, :1]

    def half_blocks(g):
        return jnp.transpose(g.reshape(D, N_DEV // 2, 2 * D_FF // N_DEV), (1, 0, 2))

    def on_ffn_grads(gw_ffn_in, gw_ffn_out, gw_out):
        return start_grads("ffn", [jnp.concatenate([half_blocks(g) for g in gw_ffn_in], axis=0),
                                   gw_ffn_out.reshape(N_DEV, D_FF // N_DEV, D), gw_out.reshape(N_DEV, D // N_DEV, D)],
                           "exchange_ffn_start")

    def on_last_grads(gw):
        return start_grads("rest", [_cols_to_blocks(_unpad_w_in(gw["w_in"])), _cols_to_blocks(_unpad_w_uq(gw["w_uq"])),
                                    _cols_to_blocks(_join_w_ukv(gw["w_k"], gw["w_v"]))], "exchange_rest_start")

    grad_x, dmod, loss_cols, small, _ = _local_step(
        xt, mod, tt, g_norm1_t, _pad_w_in(w_in_f), g_cq, _pad_w_uq(w_uq_f), g_ckv, w_k, w_v, rel_bias, g_out_a, g_out_b,
        None, g_norm2, None, None, g_final.reshape(1, D), late_weights=late_weights, on_ffn_grads=on_ffn_grads,
        on_last_grads=on_last_grads)

    upd = {}

    def land_and_update(key, names, after, name):
        got = _exchange_wait(*flight[key], [False] * len(names), after, name)
        for n, p in zip(names, got):
            w, m, v = big[n]
            upd[n] = _adamw(p, w[0], m[0], v[0], "adamw_" + n)

    big = dict(w_in=(w_in, m_w_in, v_w_in), w_uq=(w_uq, m_w_uq, v_w_uq), w_ukv=(w_ukv, m_w_ukv, v_w_ukv),
               w_out=(w_out, m_w_out, v_w_out), w_ffn_in=(w_ffn_in, m_w_ffn_in, v_w_ffn_in),
               w_ffn_out=(w_ffn_out, m_w_ffn_out, v_w_ffn_out))
    land_and_update("ffn", ["w_ffn_in", "w_ffn_out", "w_out"], grad_x, "exchange_ffn_wait")
    land_and_update("rest", ["w_in", "w_uq", "w_ukv"], upd["w_out"][0], "exchange_rest_wait")

    mine, dmod_blocks = _pack_small(dmod, [small[n] for n, _ in ROW_PARAMS], loss_cols)
    dmod_cols, pay, rel = _exchange([dmod_blocks, mine, small["rel_bias"]], [False, True, True], "exchange_small",
                                    after=upd["w_ukv"][0])
    g_ada = _ada_grad(cond_all, dmod_cols.reshape(N_DEV * nb, ncol))
    upd["w_ada"] = _adamw(g_ada[None], w_ada[0], m_w_ada[0], v_w_ada[0], "adamw_w_ada")
    row = lambda a: a.reshape(1, D)
    small_names = ["b_ada"] + [n for n, _ in ROW_PARAMS] + ["rel_bias"]
    small_w = [b_ada, g_norm1, g_cq, g_ckv, g_out_a, g_out_b, g_norm2, row(g_final), rel_bias]
    small_m = [m_b_ada, m_g_norm1, m_g_cq, m_g_ckv, m_g_out_a, m_g_out_b, m_g_norm2, row(m_g_final), m_rel_bias]
    small_v = [v_b_ada, v_g_norm1, v_g_cq, v_g_ckv, v_g_out_a, v_g_out_b, v_g_norm2, row(v_g_final), v_rel_bias]
    small_upd, loss8 = _small_update(pay, rel, small_w, small_m, small_v)
    upd.update(zip(small_names, small_upd))

    order = ["w_ada", "b_ada", "g_norm1", "w_in", "g_cq", "w_uq", "g_ckv", "w_ukv", "rel_bias", "g_out_a", "g_out_b",
             "w_out", "g_norm2", "w_ffn_in", "w_ffn_out", "g_final"]
    like = dict(w_ada=w_ada, w_in=w_in, w_uq=w_uq, w_ukv=w_ukv, w_out=w_out, w_ffn_in=w_ffn_in, w_ffn_out=w_ffn_out,
                g_final=g_final)
    outs = [loss8[0, 0], grad_x.reshape(x.shape)]
    for part in range(4):
        for n in order:
            val = upd[n][part]
            outs.append(val.reshape(like[n].shape) if n in like else val)
    return tuple(outs)
```

```python
import functools

import numpy as np
import jax
import jax.numpy as jnp
from jax import lax
from jax.experimental import pallas as pl
from jax.experimental.pallas import tpu as pltpu

F32, BF16 = jnp.float32, jnp.bfloat16

N_DEV = 8
D = 1024
S = 2048
H = 8
E_A = 64
D_A = H * E_A
Q_LORA, KV_LORA = 384, 256
NOPE, ROPE, VDIM = 64, 32, 64
HP = 128
P_IN = 3 * D_A + Q_LORA + KV_LORA + ROPE
P_PAD = 3 * D_A + Q_LORA + KV_LORA + HP
TAIL0 = 3 * D_A
TAIL = P_PAD - TAIL0
D_FF = 2816
N_MOD = 6
EPS = 1e-6
NEG = -1e30
BLK = 128
DILATIONS = (1, 4, 16)
N_BUCKETS, MAX_DISTANCE = 32, 2048
ROPE_THETA = 10000.0
SCALE_A = E_A ** -0.5
SCALE_B = (NOPE + ROPE) ** -0.5
B1, B2, LR, ADAM_EPS, WD, STEP = 0.9, 0.999, 0.001, 1e-8, 0.01, 10
VMEM_LIMIT = 56 * 1024 * 1024


def _cp(*sem):
    return pltpu.CompilerParams(dimension_semantics=sem, vmem_limit_bytes=VMEM_LIMIT)


def _pick(n, prefs):
    for p in prefs:
        if n % p == 0:
            return p
    raise ValueError(f"no tile of {prefs} divides {n}")


OPERAND_BYTES = 6 * 1024 * 1024


def _pick_rows(m, k):
    return _pick(m, [p for p in (1024, 512, 256, 128, 16) if p * k * 2 <= OPERAND_BYTES])


def _dot(a, b, dims):
    return lax.dot_general(a, b, (dims, ((), ())), preferred_element_type=F32)


def _mm_nn(a, b, out_dtype, name):
    m, k = a.shape
    n = b.shape[1]
    tm, tn = _pick_rows(m, k), _pick(n, (512, 384, 256, 128))

    def body(a_ref, b_ref, o_ref):
        o_ref[...] = _dot(a_ref[...], b_ref[...], ((1,), (0,))).astype(o_ref.dtype)

    return pl.pallas_call(
        body, name=name, grid=(m // tm, n // tn),
        in_specs=[pl.BlockSpec((tm, k), lambda i, j: (i, 0)), pl.BlockSpec((k, tn), lambda i, j: (0, j))],
        out_specs=pl.BlockSpec((tm, tn), lambda i, j: (i, j)),
        out_shape=jax.ShapeDtypeStruct((m, n), out_dtype),
        compiler_params=_cp("parallel", "parallel"),
    )(a, b)


def _mm_nt(a, b, out_dtype, name):
    m, k = a.shape
    n = b.shape[0]
    tm, tn = _pick_rows(m, k), _pick(n, (512, 384, 256, 128))

    def body(a_ref, b_ref, o_ref):
        o_ref[...] = _dot(a_ref[...], b_ref[...], ((1,), (1,))).astype(o_ref.dtype)

    return pl.pallas_call(
        body, name=name, grid=(m // tm, n // tn),
        in_specs=[pl.BlockSpec((tm, k), lambda i, j: (i, 0)), pl.BlockSpec((tn, k), lambda i, j: (j, 0))],
        out_specs=pl.BlockSpec((tm, tn), lambda i, j: (i, j)),
        out_shape=jax.ShapeDtypeStruct((m, n), out_dtype),
        compiler_params=_cp("parallel", "parallel"),
    )(a, b)


def _mm_tn(a, bs, name):
    t, m = a.shape
    n = bs[0].shape[1]
    nb_ = len(bs)
    tm, tc = _pick(m, (512, 384, 256, 128)), _pick(t, (512, 16))
    tn = n if tm <= 256 and nb_ * n * t * 2 <= 2 * OPERAND_BYTES else _pick(n, (512, 384, 256, 128))

    def body(*refs):
        a_ref, b_refs, o_refs, at_ref = refs[0], refs[1:1 + nb_], refs[1 + nb_:1 + 2 * nb_], refs[-1]

        @pl.when(pl.program_id(1) == 0)
        def _():
            def chunk(c, _):
                rows = pl.ds(pl.multiple_of(c * tc, tc), tc)
                at_ref[:, rows] = a_ref[rows, :].T
                return 0

            lax.fori_loop(0, t // tc, chunk, 0)

        for b_ref, o_ref in zip(b_refs, o_refs):
            o_ref[...] = _dot(at_ref[...], b_ref[...], ((1,), (0,))).astype(BF16)

    res = pl.pallas_call(
        body, name=name, grid=(m // tm, n // tn),
        in_specs=[pl.BlockSpec((t, tm), lambda i, j: (0, i))] + [pl.BlockSpec((t, tn), lambda i, j: (0, j))] * nb_,
        out_specs=[pl.BlockSpec((tm, tn), lambda i, j: (i, j))] * nb_,
        out_shape=[jax.ShapeDtypeStruct((m, n), BF16)] * nb_,
        scratch_shapes=[pltpu.VMEM((tm, t), BF16)],
        compiler_params=_cp("parallel", "arbitrary"),
    )(a, *bs)
    return res[0] if nb_ == 1 else res


EPI = 256


def _silu_parts(g):
    sg = 1.0 / (1.0 + jnp.exp(-g))
    return sg, g * sg


def _ffn_in(h2, w):
    t, k = h2.shape
    tm, tn = _pick_rows(t, k), _pick(D_FF, (256, 128))
    nj = D_FF // tn

    def body(h_ref, wg_ref, wu_ref, g_ref, u_ref, a_ref):
        for r in range(tm // EPI):
            rows = slice(r * EPI, (r + 1) * EPI)
            hv = h_ref[rows, :]
            gb = _dot(hv, wg_ref[...], ((1,), (0,))).astype(BF16)
            ub = _dot(hv, wu_ref[...], ((1,), (0,))).astype(BF16)
            g_ref[rows, :] = gb
            u_ref[rows, :] = ub
            a_ref[rows, :] = (_silu_parts(gb.astype(F32))[1] * ub.astype(F32)).astype(BF16)

    blk = pl.BlockSpec((tm, tn), lambda i, j: (i, j))
    return pl.pallas_call(
        body, name="ffn_in", grid=(t // tm, nj),
        in_specs=[pl.BlockSpec((tm, k), lambda i, j: (i, 0)), pl.BlockSpec((k, tn), lambda i, j: (0, j)),
                  pl.BlockSpec((k, tn), lambda i, j: (0, j + nj))],
        out_specs=[blk] * 3, out_shape=[jax.ShapeDtypeStruct((t, D_FF), BF16)] * 3,
        compiler_params=_cp("parallel", "parallel"),
    )(h2, w, w)


def _d_act(df, w, g, u):
    t, k = df.shape
    tm, tn = _pick_rows(t, k), _pick(D_FF, (256, 128))

    def body(df_ref, w_ref, g_ref, u_ref, dg_ref, du_ref):
        for r in range(tm // EPI):
            rows = slice(r * EPI, (r + 1) * EPI)
            da = _dot(df_ref[rows, :], w_ref[...], ((1,), (1,)))
            gv = g_ref[rows, :].astype(F32)
            sg, silu = _silu_parts(gv)
            dg_ref[rows, :] = (da * u_ref[rows, :].astype(F32) * (sg * (1.0 + gv * (1.0 - sg)))).astype(BF16)
            du_ref[rows, :] = (da * silu).astype(BF16)

    blk = pl.BlockSpec((tm, tn), lambda i, j: (i, j))
    return pl.pallas_call(
        body, name="d_act", grid=(t // tm, D_FF // tn),
        in_specs=[pl.BlockSpec((tm, k), lambda i, j: (i, 0)), pl.BlockSpec((tn, k), lambda i, j: (j, 0)), blk, blk],
        out_specs=[blk] * 2, out_shape=[jax.ShapeDtypeStruct((t, D_FF), BF16)] * 2,
        compiler_params=_cp("parallel", "parallel"),
    )(df, w, g, u)


def _d_h2(dg, du, w):
    t = dg.shape[0]
    n = w.shape[0]
    tm, tn = _pick_rows(t, 2 * D_FF), _pick(n, (512, 256, 128))

    def body(dg_ref, du_ref, wg_ref, wu_ref, o_ref):
        o_ref[...] = (_dot(dg_ref[...], wg_ref[...], ((1,), (1,))) + _dot(du_ref[...], wu_ref[...], ((1,), (1,))))

    return pl.pallas_call(
        body, name="d_h2", grid=(t // tm, n // tn),
        in_specs=[pl.BlockSpec((tm, D_FF), lambda i, j: (i, 0)), pl.BlockSpec((tm, D_FF), lambda i, j: (i, 0)),
                  pl.BlockSpec((tn, D_FF), lambda i, j: (j, 0)), pl.BlockSpec((tn, D_FF), lambda i, j: (j, 1))],
        out_specs=pl.BlockSpec((tm, tn), lambda i, j: (i, j)),
        out_shape=jax.ShapeDtypeStruct((t, n), F32),
        compiler_params=_cp("parallel", "parallel"),
    )(dg, du, w, w)


TM = 256


def _row(w):
    return pl.BlockSpec((TM, w), lambda i: (i, 0))


def _row_at(w, col):
    return pl.BlockSpec((TM, w), lambda i: (i, col))


def _vec(w):
    return pl.BlockSpec((1, w), lambda i: (0, 0))


def _per_ex(w):
    return pl.BlockSpec((1, 1, w), lambda i: (i // (S // TM), 0, 0))


def _pos(w):
    return pl.BlockSpec((TM, w), lambda i: (i % (S // TM), 0))


def _full(shape):
    return pl.BlockSpec(shape, lambda i: (0,) * len(shape))


def _rms(x):
    return lax.rsqrt(jnp.mean(x * x, axis=-1, keepdims=True) + EPS)


def _rms_bwd(n, r, dn):
    return r * (dn - n * jnp.mean(dn * n, axis=-1, keepdims=True))


def _colsum(v):
    return jnp.sum(v, axis=0, keepdims=True)


def _acc_first(i, ref, val, every=None):
    first = (i == 0) if every is None else (i % every == 0)

    @pl.when(first)
    def _():
        ref[...] = jnp.zeros_like(ref)

    ref[...] += val.reshape(ref.shape)


def _pre1(x, g, sc, sh):
    t = x.shape[0]

    def body(x_ref, g_ref, sc_ref, sh_ref, h_ref):
        xv = x_ref[...]
        n = xv * _rms(xv)
        h_ref[...] = ((n * g_ref[...]) * (1.0 + sc_ref[0]) + sh_ref[0]).astype(BF16)

    return pl.pallas_call(
        body, name="pre1", grid=(t // TM,),
        in_specs=[_row(D), _vec(D), _per_ex(D), _per_ex(D)],
        out_specs=_row(D), out_shape=jax.ShapeDtypeStruct((t, D), BF16),
        compiler_params=_cp("parallel"),
    )(x, g, sc, sh)


def _rope_fwd(v, c, sm, sp):
    return v * c + pltpu.roll(v, HP - ROPE // 2, 1) * sm + pltpu.roll(v, ROPE // 2, 1) * sp


def _rope_bwd(dv, c, sm, sp):
    return dv * c + pltpu.roll(dv * sm, ROPE // 2, 1) + pltpu.roll(dv * sp, HP - ROPE // 2, 1)


def _mla_pre(proj, g_cq, g_ckv, w_uq, w_k, w_v, rc, rsm, rsp):
    t = proj.shape[0]

    def body(tail_ref, gq_ref, gkv_ref, wuq_ref, wk_ref, wv_ref, c_ref, sm_ref, sp_ref,
             q_ref, k_ref, v_ref, cqn_ref, ckvn_ref):
        tail = tail_ref[...]
        cq, ckv, kr = tail[:, :Q_LORA], tail[:, Q_LORA:Q_LORA + KV_LORA], tail[:, Q_LORA + KV_LORA:]
        cqn = (cq * _rms(cq) * gq_ref[...]).astype(BF16)
        ckvn = (ckv * _rms(ckv) * gkv_ref[...]).astype(BF16)
        cqn_ref[...] = cqn
        ckvn_ref[...] = ckvn
        c, sm, sp = c_ref[...], sm_ref[...], sp_ref[...]
        q = _dot(cqn, wuq_ref[...], ((1,), (0,)))
        kn = _dot(ckvn, wk_ref[...], ((1,), (0,)))
        v_ref[...] = _dot(ckvn, wv_ref[...], ((1,), (0,))).astype(BF16)
        krr = _rope_fwd(kr, c, sm, sp)
        for h in range(H):
            sl = slice(h * HP, (h + 1) * HP)
            q_ref[:, sl] = _rope_fwd(q[:, sl], c, sm, sp).astype(BF16)
            k_ref[:, sl] = (kn[:, sl] + krr).astype(BF16)

    wide = H * HP
    return pl.pallas_call(
        body, name="mla_pre", grid=(t // TM,),
        in_specs=[_row_at(TAIL, TAIL0 // TAIL), _vec(Q_LORA), _vec(KV_LORA), _full((Q_LORA, wide)),
                  _full((KV_LORA, wide)), _full((KV_LORA, wide)), _pos(HP), _pos(HP), _pos(HP)],
        out_specs=[_row(wide), _row(wide), _row(wide), _row(Q_LORA), _row(KV_LORA)],
        out_shape=[jax.ShapeDtypeStruct((t, wide), BF16)] * 3
        + [jax.ShapeDtypeStruct((t, Q_LORA), BF16), jax.ShapeDtypeStruct((t, KV_LORA), BF16)],
        compiler_params=_cp("parallel"),
    )(proj, g_cq, g_ckv, w_uq, w_k, w_v, rc, rsm, rsp)


def _mla_pre_bwd(proj, dq_, dk_, dv_, g_cq, g_ckv, w_uq, w_k, w_v, rc, rsm, rsp):
    t = proj.shape[0]
    wide = H * HP

    def body(tail_ref, dq_ref, dk_ref, dv_ref, gq_ref, gkv_ref, wuq_ref, wk_ref, wv_ref, c_ref, sm_ref, sp_ref,
             dqo_ref, dko_ref, dvo_ref, dtail_ref, dgq_ref, dgkv_ref):
        i = pl.program_id(0)
        tail = tail_ref[...]
        cq, ckv = tail[:, :Q_LORA], tail[:, Q_LORA:Q_LORA + KV_LORA]
        c, sm, sp = c_ref[...], sm_ref[...], sp_ref[...]
        dkr = jnp.zeros((TM, HP), F32)
        for h in range(H):
            sl = slice(h * HP, (h + 1) * HP)
            dqo_ref[:, sl] = _rope_bwd(dq_ref[:, sl], c, sm, sp).astype(BF16)
            dkr = dkr + dk_ref[:, sl]
        lane = lax.broadcasted_iota(jnp.int32, (TM, HP), 1)
        dkr = jnp.where((lane >= NOPE) & (lane < NOPE + ROPE), _rope_bwd(dkr, c, sm, sp), 0.0)
        dkb = dk_ref[...].astype(BF16)
        dvb = dv_ref[...].astype(BF16)
        dko_ref[...] = dkb
        dvo_ref[...] = dvb
        dcqn = _dot(dqo_ref[...], wuq_ref[...], ((1,), (1,)))
        dckvn = _dot(dkb, wk_ref[...], ((1,), (1,))) + _dot(dvb, wv_ref[...], ((1,), (1,)))
        rq, rkv = _rms(cq), _rms(ckv)
        nq, nkv = cq * rq, ckv * rkv
        _acc_first(i, dgq_ref, _colsum(dcqn * nq))
        _acc_first(i, dgkv_ref, _colsum(dckvn * nkv))
        dtail_ref[:, :Q_LORA] = _rms_bwd(nq, rq, dcqn * gq_ref[...]).astype(BF16)
        dtail_ref[:, Q_LORA:Q_LORA + KV_LORA] = _rms_bwd(nkv, rkv, dckvn * gkv_ref[...]).astype(BF16)
        dtail_ref[:, Q_LORA + KV_LORA:] = dkr.astype(BF16)

    return pl.pallas_call(
        body, name="mla_pre_bwd", grid=(t // TM,),
        in_specs=[_row_at(TAIL, TAIL0 // TAIL), _row(wide), _row(wide), _row(wide), _vec(Q_LORA), _vec(KV_LORA),
                  _full((Q_LORA, wide)), _full((KV_LORA, wide)), _full((KV_LORA, wide)), _pos(HP), _pos(HP), _pos(HP)],
        out_specs=[_row(wide), _row(wide), _row(wide), _row(TAIL), _vec(Q_LORA), _vec(KV_LORA)],
        out_shape=[jax.ShapeDtypeStruct((t, wide), BF16)] * 3 + [jax.ShapeDtypeStruct((t, TAIL), BF16),
                   jax.ShapeDtypeStruct((1, Q_LORA), F32), jax.ShapeDtypeStruct((1, KV_LORA), F32)],
        compiler_params=_cp("arbitrary"),
    )(proj, dq_, dk_, dv_, g_cq, g_ckv, w_uq, w_k, w_v, rc, rsm, rsp)


def _post_attn(out_a, out_b, g_a, g_b):
    t = out_a.shape[0]

    def body(a_ref, b_ref, ga_ref, gb_ref, y_ref):
        a, b = a_ref[...], b_ref[...]
        y_ref[:, :D_A] = (a * _rms(a) * ga_ref[...]).astype(BF16)
        y_ref[:, D_A:] = (b * _rms(b) * gb_ref[...]).astype(BF16)

    return pl.pallas_call(
        body, name="post_attn", grid=(t // TM,),
        in_specs=[_row(D_A), _row(D_A), _vec(D_A), _vec(D_A)],
        out_specs=_row(D), out_shape=jax.ShapeDtypeStruct((t, D), BF16),
        compiler_params=_cp("parallel"),
    )(out_a, out_b, g_a, g_b)


def _post_attn_bwd(dy, out_a, out_b, g_a, g_b):
    t = dy.shape[0]

    def body(dy_ref, a_ref, b_ref, ga_ref, gb_ref, da_ref, db_ref, dga_ref, dgb_ref):
        i = pl.program_id(0)
        dy_ = dy_ref[...]
        for src, g_ref, dst, dg_ref, sl in ((a_ref, ga_ref, da_ref, dga_ref, slice(0, D_A)),
                                            (b_ref, gb_ref, db_ref, dgb_ref, slice(D_A, D))):
            v = src[...]
            r = _rms(v)
            n = v * r
            dyv = dy_[:, sl]
            _acc_first(i, dg_ref, _colsum(dyv * n))
            dst[...] = _rms_bwd(n, r, dyv * g_ref[...])

    return pl.pallas_call(
        body, name="post_attn_bwd", grid=(t // TM,),
        in_specs=[_row(D), _row(D_A), _row(D_A), _vec(D_A), _vec(D_A)],
        out_specs=[_row(D_A), _row(D_A), _vec(D_A), _vec(D_A)],
        out_shape=[jax.ShapeDtypeStruct((t, D_A), F32)] * 2 + [jax.ShapeDtypeStruct((1, D_A), F32)] * 2,
        compiler_params=_cp("arbitrary"),
    )(dy, out_a, out_b, g_a, g_b)


def _resid_norm2(x, mix, g1, g, sc, sh):
    t = x.shape[0]

    def body(x_ref, mix_ref, g1_ref, g_ref, sc_ref, sh_ref, x2_ref, h_ref):
        x2 = x_ref[...] + g1_ref[0] * mix_ref[...]
        x2_ref[...] = x2
        n = x2 * _rms(x2)
        h_ref[...] = ((n * g_ref[...]) * (1.0 + sc_ref[0]) + sh_ref[0]).astype(BF16)

    return pl.pallas_call(
        body, name="resid_norm2", grid=(t // TM,),
        in_specs=[_row(D), _row(D), _per_ex(D), _vec(D), _per_ex(D), _per_ex(D)],
        out_specs=[_row(D), _row(D)],
        out_shape=[jax.ShapeDtypeStruct((t, D), F32), jax.ShapeDtypeStruct((t, D), BF16)],
        compiler_params=_cp("parallel"),
    )(x, mix, g1, g, sc, sh)


def _sigmoid(v):
    return 1.0 / (1.0 + jnp.exp(-v))


def _final(x2, f, g2, g_fin, target):
    t = x2.shape[0]
    nb = t // S
    tpb = S // TM

    def body(x2_ref, f_ref, g2_ref, g_ref, t_ref, dx3_ref, df_ref, loss_ref, dgf_ref, dg2_ref):
        i = pl.program_id(0)
        fv = f_ref[...]
        x3 = x2_ref[...] + g2_ref[0] * fv
        r = _rms(x3)
        n = x3 * r
        err = n * g_ref[...] - t_ref[...]
        _acc_first(i, loss_ref, _colsum(err * err))
        dy = err * (1.0 / D)
        _acc_first(i, dgf_ref, _colsum(dy * n))
        dx3 = _rms_bwd(n, r, dy * g_ref[...])
        dx3_ref[...] = dx3
        _acc_first(i, dg2_ref, _colsum(dx3 * fv), every=tpb)
        df_ref[...] = (dx3 * g2_ref[0]).astype(BF16)

    return pl.pallas_call(
        body, name="final", grid=(t // TM,),
        in_specs=[_row(D), _row(D), _per_ex(D), _vec(D), _row(D)],
        out_specs=[_row(D), _row(D), _vec(D), _vec(D), _per_ex(D)],
        out_shape=[jax.ShapeDtypeStruct((t, D), F32), jax.ShapeDtypeStruct((t, D), BF16),
                   jax.ShapeDtypeStruct((1, D), F32), jax.ShapeDtypeStruct((1, D), F32),
                   jax.ShapeDtypeStruct((nb, 1, D), F32)],
        compiler_params=_cp("arbitrary"),
    )(x2, f, g2, g_fin, target)


def _norm_bwd(xin, dh, dres, g, sc, gate=None):
    t = xin.shape[0]
    nb = t // S
    tpb = S // TM
    gated = gate is not None

    def body(*refs):
        if gated:
            x_ref, dh_ref, dres_ref, g_ref, sc_ref, mix_ref, g1_ref, dx_ref, dsh_ref, dsc_ref, dg_ref, dg1_ref, dmix_ref = refs
        else:
            x_ref, dh_ref, dres_ref, g_ref, sc_ref, dx_ref, dsh_ref, dsc_ref, dg_ref = refs
        i = pl.program_id(0)
        xv, dhv = x_ref[...], dh_ref[...]
        r = _rms(xv)
        n = xv * r
        gv = g_ref[...]
        _acc_first(i, dsh_ref, _colsum(dhv), every=tpb)
        _acc_first(i, dsc_ref, _colsum(dhv * (n * gv)), every=tpb)
        dng = dhv * (1.0 + sc_ref[0])
        _acc_first(i, dg_ref, _colsum(dng * n))
        dx = dres_ref[...] + _rms_bwd(n, r, dng * gv)
        dx_ref[...] = dx
        if gated:
            _acc_first(i, dg1_ref, _colsum(dx * mix_ref[...]), every=tpb)
            dmix_ref[...] = (dx * g1_ref[0]).astype(BF16)

    in_specs = [_row(D), _row(D), _row(D), _vec(D), _per_ex(D)]
    out_specs = [_row(D), _per_ex(D), _per_ex(D), _vec(D)]
    out_shape = [jax.ShapeDtypeStruct((t, D), F32), jax.ShapeDtypeStruct((nb, 1, D), F32),
                 jax.ShapeDtypeStruct((nb, 1, D), F32), jax.ShapeDtypeStruct((1, D), F32)]
    args = [xin, dh, dres, g, sc]
    if gated:
        in_specs += [_row(D), _per_ex(D)]
        out_specs += [_per_ex(D), _row(D)]
        out_shape += [jax.ShapeDtypeStruct((nb, 1, D), F32), jax.ShapeDtypeStruct((t, D), BF16)]
        args += list(gate)
    return pl.pallas_call(
        body, name="norm2_bwd" if gated else "norm1_bwd", grid=(t // TM,),
        in_specs=in_specs, out_specs=out_specs, out_shape=out_shape,
        compiler_params=_cp("arbitrary"),
    )(*args)


TQ = 256
TB = 512


def _mla_fwd(q, k, v):
    t = q.shape[0]
    nb = t // S

    def body(q_ref, k_ref, v_ref, o_ref, lse_ref):
        causal = lax.broadcasted_iota(jnp.int32, (TB, TB), 0) >= lax.broadcasted_iota(jnp.int32, (TB, TB), 1)
        heads = [slice(h * HP, (h + 1) * HP) for h in range(2)]
        for i in range(S // TB):
            ri, past = slice(i * TB, (i + 1) * TB), slice(0, i * TB)
            qhs = [q_ref[ri, sl] for sl in heads]
            sd = [jnp.where(causal, _dot(qh, k_ref[ri, sl], ((1,), (1,))) * SCALE_B, NEG) for qh, sl in zip(qhs, heads)]
            ms = [jnp.max(s, axis=-1, keepdims=True) for s in sd]
            if i:
                so = [_dot(qh, k_ref[past, sl], ((1,), (1,))) * SCALE_B for qh, sl in zip(qhs, heads)]
                ms = [jnp.maximum(m, jnp.max(s, axis=-1, keepdims=True)) for m, s in zip(ms, so)]
            pd = [jnp.exp(s - m) for s, m in zip(sd, ms)]
            ls = [jnp.sum(p, axis=-1, keepdims=True) for p in pd]
            acc = [_dot(p.astype(BF16), v_ref[ri, sl], ((1,), (0,))) for p, sl in zip(pd, heads)]
            if i:
                po = [jnp.exp(s - m) for s, m in zip(so, ms)]
                ls = [l + jnp.sum(p, axis=-1, keepdims=True) for l, p in zip(ls, po)]
                acc = [a + _dot(p.astype(BF16), v_ref[past, sl], ((1,), (0,))) for a, p, sl in zip(acc, po, heads)]
            o_ref[ri, :] = acc[0] / ls[0] + acc[1] / ls[1]
            for sl, m, l in zip(heads, ms, ls):
                lse_ref[ri, sl] = jnp.broadcast_to(m + jnp.log(l), (TB, HP))

    wide2 = pl.BlockSpec((S, 2 * HP), lambda b, p: (b, p))
    return pl.pallas_call(
        body, name="mla_fwd", grid=(nb, H // 2),
        in_specs=[wide2, wide2, wide2],
        out_specs=[pl.BlockSpec((S, HP), lambda b, p: (b, p)), wide2],
        out_shape=[jax.ShapeDtypeStruct((t, H * VDIM), F32), jax.ShapeDtypeStruct((t, H * HP), F32)],
        compiler_params=_cp("parallel", "parallel"),
    )(q, k, v)


def _mla_bwd(q, k, v, o, do, lse):
    t = q.shape[0]
    nb = t // S
    nq = S // TQ

    def body(q_ref, k_ref, v_ref, o_ref, do_ref, lse_ref, dq_ref, dk_ref, dv_ref):
        lane = lax.broadcasted_iota(jnp.int32, (TB, HP), 1)
        causal = lax.broadcasted_iota(jnp.int32, (TB, TB), 0) >= lax.broadcasted_iota(jnp.int32, (TB, TB), 1)
        heads = [slice(h * HP, (h + 1) * HP) for h in range(2)]
        nblk = S // TB
        for i in reversed(range(nblk)):
            ri, past = slice(i * TB, (i + 1) * TB), slice(0, i * TB)
            dov = do_ref[ri, :]
            prod = dov * o_ref[ri, :]
            dob = dov.astype(BF16)
            deltas = [jnp.sum(jnp.where((lane < VDIM) if h == 0 else (lane >= VDIM), prod, 0.0), axis=-1, keepdims=True)
                      for h in range(2)]
            qhs = [q_ref[ri, sl] for sl in heads]
            lses = [lse_ref[ri, sl][:, :1] for sl in heads]
            for rows, diagonal in ((ri, True), (past, False)):
                if rows.stop == rows.start:
                    continue
                ps = [jnp.exp(_dot(qh, k_ref[rows, sl], ((1,), (1,))) * SCALE_B - lse) for qh, sl, lse in zip(qhs, heads, lses)]
                if diagonal:
                    ps = [jnp.where(causal, p, 0.0) for p in ps]
                dps = [_dot(dob, v_ref[rows, sl], ((1,), (1,))) for sl in heads]
                dss = [(p * (dp - delta) * SCALE_B).astype(BF16) for p, dp, delta in zip(ps, dps, deltas)]
                for sl, qh, p, ds in zip(heads, qhs, ps, dss):
                    dq = _dot(ds, k_ref[rows, sl], ((1,), (0,)))
                    dk = _dot(ds, qh, ((0,), (0,)))
                    dv = _dot(p.astype(BF16), dob, ((0,), (0,)))
                    if diagonal:
                        dq_ref[ri, sl] = dq
                    else:
                        dq_ref[ri, sl] += dq
                    if i == nblk - 1:
                        dk_ref[rows, sl] = dk
                        dv_ref[rows, sl] = dv
                    else:
                        dk_ref[rows, sl] += dk
                        dv_ref[rows, sl] += dv

    wide2 = pl.BlockSpec((S, 2 * HP), lambda b, p: (b, p))
    pair = pl.BlockSpec((S, HP), lambda b, p: (b, p))
    return pl.pallas_call(
        body, name="mla_bwd", grid=(nb, H // 2),
        in_specs=[wide2, wide2, wide2, pair, pair, wide2],
        out_specs=[wide2, wide2, wide2],
        out_shape=[jax.ShapeDtypeStruct((t, H * HP), F32)] * 3,
        compiler_params=_cp("parallel", "parallel"),
    )(q, k, v, o, do, lse)


def _t5_bucket(dist):
    max_exact = N_BUCKETS // 2
    d = np.maximum(dist, 1).astype(np.float64)
    large = max_exact + (np.log(d / max_exact) / np.log(MAX_DISTANCE / max_exact) * (N_BUCKETS - max_exact)).astype(np.int64)
    large = np.minimum(large, N_BUCKETS - 1)
    return np.where(dist < max_exact, dist, large).astype(np.int32)


def _band_geometry():
    a = np.arange(BLK)[:, None]
    bk = np.arange(2 * BLK)[None, :]
    steps = BLK + a - bk
    valid = (steps >= 0) & (steps <= BLK)
    buckets = np.stack([_t5_bucket(np.clip(steps, 0, BLK) * d) for d in DILATIONS])
    return buckets, valid


def _band_bias(rel_bias):
    buckets, valid = _band_geometry()
    onehot = (jnp.asarray(buckets)[..., None] == jnp.arange(N_BUCKETS)).astype(F32)
    bias = jnp.einsum("rqkn,nh->rhqk", onehot, rel_bias, precision=lax.Precision.HIGHEST)
    bias = jnp.where(jnp.asarray(valid)[None, None], bias, NEG)
    return bias.reshape(3, H // 2, 2 * BLK, 2 * BLK)


def _dil_items():
    items = []
    for r, d in enumerate(DILATIONS):
        for res in range(d):
            for blk in range(S // d // BLK):
                items.append((r, d, blk * BLK * d + res, blk > 0))
    return items


GROUP = 4


def _strided(start, d):
    return pl.ds(start, BLK) if d == 1 else pl.ds(start, BLK, stride=d)


def _stack_heads(tile, own):
    return jnp.where(own, jnp.concatenate([tile, tile], axis=0), 0.0).astype(BF16)


def _own_lanes():
    row = lax.broadcasted_iota(jnp.int32, (2 * BLK, HP), 0)
    lane = lax.broadcasted_iota(jnp.int32, (2 * BLK, HP), 1)
    return (lane < E_A) == (row < BLK)


def _dil_fwd(proj, biasm):
    t = proj.shape[0]
    nb = t // S

    def body(q_ref, k_ref, v_ref, b_ref, o_ref, lse_ref, ob_ref, lb_ref):
        lane = lax.broadcasted_iota(jnp.int32, (BLK, HP), 1)
        own = _own_lanes()
        items = _dil_items()
        for g in range(0, len(items), GROUP):
            grp = items[g:g + GROUP]
            ss, vts = [], []
            for r, d, start, has_prev in grp:
                cur = _strided(start, d)
                rows = [_strided(start - BLK * d, d), cur] if has_prev else [cur]
                q2 = _stack_heads(q_ref[cur, :], own)
                kt = jnp.concatenate([k_ref[x, :] for x in rows], axis=0).astype(BF16)
                vts.append(jnp.concatenate([v_ref[x, :] for x in rows], axis=0).astype(BF16))
                bias = b_ref[r, 0] if has_prev else b_ref[r, 0, :, BLK:]
                ss.append(_dot(q2, kt, ((1,), (1,))) * SCALE_A + bias)
            ms = [jnp.max(s, axis=-1, keepdims=True) for s in ss]
            ps = [jnp.exp(s - m) for s, m in zip(ss, ms)]
            ls = [jnp.sum(p, axis=-1, keepdims=True) for p in ps]
            for (r, d, start, _), p, vt, m, l in zip(grp, ps, vts, ms, ls):
                cur = _strided(start, d)
                o2 = _dot(p.astype(BF16), vt, ((1,), (0,))) / l
                lse2 = m + jnp.log(l)
                ob_ref[r, cur, :] = jnp.where(lane < E_A, o2[:BLK], o2[BLK:])
                lb_ref[r, cur, :] = jnp.where(lane < E_A, lse2[:BLK], lse2[BLK:])

        def merge(c, _):
            rows = pl.ds(pl.multiple_of(c * TQ, TQ), TQ)
            l0, l1, l2 = lb_ref[0, rows, :], lb_ref[1, rows, :], lb_ref[2, rows, :]
            m = jnp.maximum(jnp.maximum(l0, l1), l2)
            e0, e1, e2 = jnp.exp(l0 - m), jnp.exp(l1 - m), jnp.exp(l2 - m)
            tot = e0 + e1 + e2
            o_ref[rows, :] = (e0 * ob_ref[0, rows, :] + e1 * ob_ref[1, rows, :] + e2 * ob_ref[2, rows, :]) / tot
            lse_ref[rows, :] = m + jnp.log(tot)
            return 0

        lax.fori_loop(0, S // TQ, merge, 0)

    npair = H // 2
    return pl.pallas_call(
        body, name="dil_fwd", grid=(nb, npair),
        in_specs=[pl.BlockSpec((S, HP), lambda b, p: (b, p)), pl.BlockSpec((S, HP), lambda b, p: (b, npair + p)),
                  pl.BlockSpec((S, HP), lambda b, p: (b, 2 * npair + p)),
                  pl.BlockSpec((3, 1, 2 * BLK, 2 * BLK), lambda b, p: (0, p, 0, 0))],
        out_specs=[pl.BlockSpec((S, HP), lambda b, p: (b, p))] * 2,
        out_shape=[jax.ShapeDtypeStruct((t, D_A), F32)] * 2,
        scratch_shapes=[pltpu.VMEM((3, S, HP), F32), pltpu.VMEM((3, S, HP), F32)],
        compiler_params=_cp("parallel", "parallel"),
    )(proj, proj, proj, biasm)


def _dil_bwd(proj, biasm, o, do, lse):
    t = proj.shape[0]
    nb = t // S

    def body(q_ref, k_ref, v_ref, b_ref, o_ref, do_ref, lse_ref, dq_ref, dk_ref, dv_ref, ds_ref):
        dq_ref[...] = jnp.zeros_like(dq_ref)
        dk_ref[...] = jnp.zeros_like(dk_ref)
        dv_ref[...] = jnp.zeros_like(dv_ref)
        ds_ref[...] = jnp.zeros_like(ds_ref)
        lane = lax.broadcasted_iota(jnp.int32, (BLK, HP), 1)
        own = _own_lanes()
        items = _dil_items()
        for g in range(0, len(items), GROUP):
            grp = items[g:g + GROUP]
            q2s, kts, do2s, ss, dps, lse2s, delta2s = [], [], [], [], [], [], []
            for r, d, start, has_prev in grp:
                cur = _strided(start, d)
                rows = [_strided(start - BLK * d, d), cur] if has_prev else [cur]
                q2 = _stack_heads(q_ref[cur, :], own)
                kt = jnp.concatenate([k_ref[x, :] for x in rows], axis=0).astype(BF16)
                vt = jnp.concatenate([v_ref[x, :] for x in rows], axis=0).astype(BF16)
                dot_ = do_ref[cur, :]
                prod = dot_ * o_ref[cur, :]
                lset = lse_ref[cur, :]
                do2 = _stack_heads(dot_, own)
                bias = b_ref[r, 0] if has_prev else b_ref[r, 0, :, BLK:]
                ss.append(_dot(q2, kt, ((1,), (1,))) * SCALE_A + bias)
                dps.append(_dot(do2, vt, ((1,), (1,))))
                lse2s.append(jnp.concatenate([lset[:, :1], lset[:, E_A:E_A + 1]], axis=0))
                delta2s.append(jnp.concatenate([jnp.sum(jnp.where(lane < E_A, prod, 0.0), axis=-1, keepdims=True),
                                                jnp.sum(jnp.where(lane >= E_A, prod, 0.0), axis=-1, keepdims=True)], axis=0))
                q2s.append(q2)
                kts.append(kt)
                do2s.append(do2)
            ps = [jnp.exp(s - lse2) for s, lse2 in zip(ss, lse2s)]
            dls = [p * (dp - delta2) for p, dp, delta2 in zip(ps, dps, delta2s)]
            for (r, d, start, has_prev), q2, kt, do2, p, dl in zip(grp, q2s, kts, do2s, ps, dls):
                cur = _strided(start, d)
                dsb = (dl * SCALE_A).astype(BF16)
                dq2 = _dot(dsb, kt, ((1,), (0,)))
                dkt = _dot(dsb, q2, ((0,), (0,)))
                dvt = _dot(p.astype(BF16), do2, ((0,), (0,)))
                dq_ref[cur, :] += jnp.where(lane < E_A, dq2[:BLK], dq2[BLK:])
                if has_prev:
                    prev = _strided(start - BLK * d, d)
                    ds_ref[0, r, 0] += dl
                    dk_ref[prev, :] += dkt[:BLK]
                    dv_ref[prev, :] += dvt[:BLK]
                    dk_ref[cur, :] += dkt[BLK:]
                    dv_ref[cur, :] += dvt[BLK:]
                else:
                    ds_ref[0, r, 0, :, BLK:] += dl
                    dk_ref[cur, :] += dkt
                    dv_ref[cur, :] += dvt

    npair = H // 2
    pair = pl.BlockSpec((S, HP), lambda b, p: (b, p))
    return pl.pallas_call(
        body, name="dil_bwd", grid=(nb, npair),
        in_specs=[pair, pl.BlockSpec((S, HP), lambda b, p: (b, npair + p)),
                  pl.BlockSpec((S, HP), lambda b, p: (b, 2 * npair + p)),
                  pl.BlockSpec((3, 1, 2 * BLK, 2 * BLK), lambda b, p: (0, p, 0, 0)), pair, pair, pair],
        out_specs=[pair, pair, pair, pl.BlockSpec((1, 3, 1, 2 * BLK, 2 * BLK), lambda b, p: (b, 0, p, 0, 0))],
        out_shape=[jax.ShapeDtypeStruct((t, D_A), F32)] * 3 + [jax.ShapeDtypeStruct((nb, 3, npair, 2 * BLK, 2 * BLK), F32)],
        compiler_params=_cp("parallel", "parallel"),
    )(proj, proj, proj, biasm, o, do, lse)


def _rel_bias_grad(dlogits):
    nb = dlogits.shape[0]
    buckets, _ = _band_geometry()
    kk = 3 * BLK * 2 * BLK
    dl = jnp.transpose(dlogits.reshape(nb, 3, H, BLK, 2 * BLK), (0, 2, 1, 3, 4)).reshape(nb, H, kk)
    bk = jnp.asarray(buckets.reshape(1, kk))
    tk = kk // 12

    def body(dl_ref, bk_ref, o_ref):
        j = pl.program_id(0)
        onehot = (bk_ref[...] == lax.broadcasted_iota(jnp.int32, (N_BUCKETS, tk), 0)).astype(F32)
        tot = dl_ref[0]
        for b in range(1, nb):
            tot = tot + dl_ref[b]
        part = lax.dot_general(onehot, tot, ((((1,), (1,))), ((), ())), preferred_element_type=F32,
                               precision=lax.Precision.HIGHEST)
        _acc_first(j, o_ref, part)

    return pl.pallas_call(
        body, name="rel_bias_grad", grid=(kk // tk,),
        in_specs=[pl.BlockSpec((nb, H, tk), lambda j: (0, 0, j)), pl.BlockSpec((1, tk), lambda j: (0, j))],
        out_specs=pl.BlockSpec((N_BUCKETS, H), lambda j: (0, 0)),
        out_shape=jax.ShapeDtypeStruct((N_BUCKETS, H), F32),
        compiler_params=_cp("arbitrary"),
    )(dl, bk)


def _mesh_place():
    x, y, c = lax.axis_index("x"), lax.axis_index("y"), lax.axis_index("c")
    return x, y, c


def _peer(k):
    x, y, c = _mesh_place()
    px = 1 - x if k & 4 else x
    py = 1 - y if k & 2 else y
    pc = 1 - c if k & 1 else c
    return (px, py, pc), 4 * px + 2 * py + pc


ANY = pl.BlockSpec(memory_space=pl.ANY)


def _exchange(arrays, gathers, name, after=None):
    n_arr = len(arrays)

    def body(*refs):
        ins, outs = refs[:n_arr], refs[n_arr + 1:2 * n_arr + 1]
        send, recv, loc = refs[2 * n_arr + 1:]
        x, y, c = _mesh_place()
        me = 4 * x + 2 * y + c
        local = [pltpu.make_async_copy(ins[a] if gathers[a] else ins[a].at[me], outs[a].at[me], loc.at[a])
                 for a in range(n_arr)]
        remote = _peer_copies(ins, outs, send, recv, gathers)
        for cp in local:
            cp.start()
        for put, _ in remote:
            put.start()
        for cp in local:
            cp.wait()
        for put, got in remote:
            put.wait_send()
            got.wait_recv()

    return pl.pallas_call(
        body, name=name,
        in_specs=[ANY] * (n_arr + 1), out_specs=[ANY] * n_arr,
        out_shape=[jax.ShapeDtypeStruct(((N_DEV,) if g else ()) + a.shape, a.dtype) for a, g in zip(arrays, gathers)],
        scratch_shapes=[pltpu.SemaphoreType.DMA((n_arr * (N_DEV - 1),)), pltpu.SemaphoreType.DMA((n_arr * (N_DEV - 1),)),
                        pltpu.SemaphoreType.DMA((n_arr,))],
        compiler_params=pltpu.CompilerParams(has_side_effects=True),
    )(*arrays, arrays[0] if after is None else after)


HBM = pl.BlockSpec(memory_space=pltpu.HBM)
SEM = pl.BlockSpec(memory_space=pltpu.SEMAPHORE)
DATAFLOW = pltpu.SideEffectType.DATAFLOW_SIDE_EFFECTING


def _own_block_in_place(block, me):
    land = lax.empty((N_DEV,) + block.shape, block.dtype)
    return lax.dynamic_update_slice(land, block[None], (me,) + (0,) * block.ndim)


def _peer_copies(srcs, lands, send, recv, gathers):
    x, y, c = _mesh_place()
    me = 4 * x + 2 * y + c
    out = []
    for a, (src, land) in enumerate(zip(srcs, lands)):
        for k in range(1, N_DEV):
            dev, idx = _peer(k)
            sem = a * (N_DEV - 1) + k - 1
            mine = src if gathers[a] else src.at[idx]
            put = pltpu.make_async_remote_copy(mine, land.at[me], send.at[sem], recv.at[sem],
                                               device_id=dev, device_id_type=pl.DeviceIdType.MESH)
            got = pltpu.make_async_remote_copy(mine, land.at[idx], send.at[sem], recv.at[sem],
                                               device_id=dev, device_id_type=pl.DeviceIdType.MESH)
            out.append((put, got))
    return out


def _exchange_start(srcs, lands, gather, after, name):
    n = len(srcs)

    def body(*refs):
        srcs_, lands_, send, recv = refs[:n], refs[n:2 * n], refs[2 * n + 1], refs[2 * n + 2]
        for put, _ in _peer_copies(srcs_, lands_, send, recv, gather):
            put.start()
        refs[-1][...] = jnp.zeros_like(refs[-1])

    nsem = n * (N_DEV - 1)
    thru = [pltpu.HBM(a.shape, a.dtype) for a in list(srcs) + list(lands)]
    res = pl.pallas_call(
        body, name=name,
        out_shape=(pltpu.SemaphoreType.DMA((nsem,)), pltpu.SemaphoreType.DMA((nsem,)), *thru, jax.ShapeDtypeStruct((8, 128), F32)),
        in_specs=[HBM] * (2 * n) + [ANY],
        out_specs=(SEM, SEM, *([HBM] * (2 * n)), pl.BlockSpec(memory_space=pltpu.VMEM)),
        input_output_aliases={i: 2 + i for i in range(2 * n)},
        compiler_params=pltpu.CompilerParams(has_side_effects=DATAFLOW),
    )(*[pltpu.with_memory_space_constraint(a, pltpu.HBM) for a in list(srcs) + list(lands)], after)
    return res[0], res[1], list(res[2:2 + n]), list(res[2 + n:2 + 2 * n]), res[-1]


def _exchange_wait(send, recv, srcs, lands, gather, after, name):
    n = len(srcs)

    def body(*refs):
        srcs_, lands_, send_, recv_ = refs[:n], refs[n:2 * n], refs[2 * n], refs[2 * n + 1]
        for put, got in _peer_copies(srcs_, lands_, send_, recv_, gather):
            put.wait_send()
            got.wait_recv()

    thru = [pltpu.HBM(a.shape, a.dtype) for a in list(srcs) + list(lands)]
    res = pl.pallas_call(
        body, name=name, out_shape=tuple(thru),
        in_specs=[HBM] * (2 * n) + [SEM, SEM, ANY], out_specs=tuple([HBM] * (2 * n)),
        input_output_aliases={i: i for i in range(2 * n)},
        compiler_params=pltpu.CompilerParams(has_side_effects=DATAFLOW),
    )(*srcs, *lands, send, recv, after)
    return list(res[n:])


def _silu_rows(c):
    def body(c_ref, o_ref):
        v = c_ref[...]
        o_ref[...] = v * _sigmoid(v)

    return pl.pallas_call(body, name="cond", out_shape=jax.ShapeDtypeStruct(c.shape, F32))(c)


def _mod_slab(cond_all, w_ada, b_slab):
    def body(c_ref, w_ref, b_ref, o_ref):
        o_ref[...] = _dot(c_ref[...].astype(BF16), w_ref[0].astype(BF16), ((1,), (0,))) + b_ref[...]

    return pl.pallas_call(body, name="mod_slab",
                          out_shape=jax.ShapeDtypeStruct((cond_all.shape[0], w_ada.shape[2]), F32),
                          compiler_params=pltpu.CompilerParams(vmem_limit_bytes=VMEM_LIMIT))(cond_all, w_ada, b_slab)


def _ada_grad(cond_all, dmod_cols):
    def body(c_ref, d_ref, o_ref):
        o_ref[...] = _dot(c_ref[...].astype(BF16), d_ref[...].astype(BF16), ((0,), (0,)))

    return pl.pallas_call(body, name="ada_grad",
                          out_shape=jax.ShapeDtypeStruct((cond_all.shape[1], dmod_cols.shape[1]), F32),
                          compiler_params=pltpu.CompilerParams(vmem_limit_bytes=VMEM_LIMIT))(cond_all, dmod_cols)


def _adam_math(g, w, m, v):
    m2 = B1 * m + (1.0 - B1) * g
    v2 = B2 * v + (1.0 - B2) * (g * g)
    m_hat = m2 / (1.0 - B1 ** STEP)
    v_hat = v2 / (1.0 - B2 ** STEP)
    return -LR * (m_hat / (jnp.sqrt(v_hat) + ADAM_EPS) + WD * w), m2, v2


def _adamw(parts, w, m, v, name):
    n, rows, cols = parts.shape
    tr = _pick(rows, (128, 96, 64, 32, 16, 8))

    def body(p_ref, w_ref, m_ref, v_ref, g_ref, d_ref, m2_ref, v2_ref):
        g = p_ref[0].astype(F32)
        for s in range(1, n):
            g = g + p_ref[s].astype(F32)
        g_ref[0] = g
        d_ref[0], m2_ref[0], v2_ref[0] = _adam_math(g, w_ref[0], m_ref[0], v_ref[0])

    blk = pl.BlockSpec((1, tr, cols), lambda i: (0, i, 0))
    return pl.pallas_call(
        body, name=name, grid=(rows // tr,),
        in_specs=[pl.BlockSpec((n, tr, cols), lambda i: (0, i, 0)), blk, blk, blk],
        out_specs=[blk] * 4, out_shape=[jax.ShapeDtypeStruct((1, rows, cols), F32)] * 4,
        compiler_params=_cp("parallel"),
    )(parts, w, m, v)


ROW_PARAMS = (("g_norm1", D), ("g_cq", Q_LORA), ("g_ckv", KV_LORA), ("g_out_a", D_A), ("g_out_b", D_A), ("g_norm2", D),
              ("g_final", D))
LOSS_ROW = N_MOD + len(ROW_PARAMS)
PAY_ROWS = 16
NCOL = N_MOD * D // N_DEV


def _pack_small(dmods, rows, loss_cols):
    nb = dmods[0].shape[0]
    nrow = len(ROW_PARAMS)

    def body(*refs):
        dm, rw, loss_ref, pay_ref, blk_ref = refs[:N_MOD], refs[N_MOD:N_MOD + nrow], refs[N_MOD + nrow], refs[-2], refs[-1]
        pay_ref[...] = jnp.zeros_like(pay_ref)
        for k in range(N_MOD):
            tot = dm[k][0]
            for b in range(1, nb):
                tot = tot + dm[k][b]
            pay_ref[k:k + 1, :] = tot
        for i, (_, n) in enumerate(ROW_PARAMS):
            pay_ref[N_MOD + i:N_MOD + i + 1, :n] = rw[i][...]
        pay_ref[LOSS_ROW:LOSS_ROW + 1, :] = loss_ref[...]
        for j in range(N_DEV):
            done = 0
            while done < NCOL:
                seg, off = divmod(j * NCOL + done, D)
                ln = min(NCOL - done, D - off)
                for b in range(nb):
                    blk_ref[j, b:b + 1, done:done + ln] = dm[seg][b][:, off:off + ln]
                done += ln

    return pl.pallas_call(
        body, name="pack_small",
        out_shape=[jax.ShapeDtypeStruct((PAY_ROWS, D), F32), jax.ShapeDtypeStruct((N_DEV, nb, NCOL), F32)],
    )(*dmods, *rows, loss_cols)


def _small_update(pay, rel, ws, ms, vs):
    n_par = len(ws)

    def body(*refs):
        pay_ref, rel_ref = refs[:2]
        w_refs, m_refs, v_refs = (refs[2 + s * n_par:2 + (s + 1) * n_par] for s in range(3))
        outs, loss_ref = refs[2 + 3 * n_par:-1], refs[-1]
        tot, rtot = pay_ref[0], rel_ref[0]
        for s in range(1, N_DEV):
            tot, rtot = tot + pay_ref[s], rtot + rel_ref[s]

        def update(p, g, sl):
            outs[4 * p][:, sl] = g
            outs[4 * p + 1][:, sl], outs[4 * p + 2][:, sl], outs[4 * p + 3][:, sl] = _adam_math(
                g, w_refs[p][:, sl], m_refs[p][:, sl], v_refs[p][:, sl])

        for k in range(N_MOD):
            update(0, tot[k:k + 1, :], slice(k * D, (k + 1) * D))
        for i, (_, n) in enumerate(ROW_PARAMS):
            update(1 + i, tot[N_MOD + i:N_MOD + i + 1, :n], slice(0, n))
        update(n_par - 1, rtot, slice(0, H))
        loss_ref[...] = jnp.broadcast_to((0.5 / D) * jnp.sum(tot[LOSS_ROW:LOSS_ROW + 1, :]), loss_ref.shape)

    shapes = [jax.ShapeDtypeStruct(w.shape, F32) for w in ws for _ in range(4)]
    res = pl.pallas_call(
        body, name="small_update", out_shape=shapes + [jax.ShapeDtypeStruct((8, 128), F32)],
    )(pay, rel, *ws, *ms, *vs)
    return [tuple(res[4 * p:4 * p + 4]) for p in range(n_par)], res[-1]


def _cols_from_blocks(g):
    return jnp.transpose(g, (1, 0, 2)).reshape(g.shape[1], N_DEV * g.shape[2])


def _cols_to_blocks(w):
    r, c = w.shape
    return jnp.transpose(w.reshape(r, N_DEV, c // N_DEV), (1, 0, 2))


def _pad_w_in(w):
    z = jnp.zeros((w.shape[0], NOPE), w.dtype)
    return jnp.concatenate([w[:, :P_IN - ROPE], z, w[:, P_IN - ROPE:], z[:, :HP - NOPE - ROPE]], axis=1)


def _unpad_w_in(g):
    k0 = P_IN - ROPE + NOPE
    return jnp.concatenate([g[:, :P_IN - ROPE], g[:, k0:k0 + ROPE]], axis=1)


def _pad_w_uq(w):
    w3 = w.reshape(Q_LORA, H, NOPE + ROPE)
    return jnp.pad(w3, ((0, 0), (0, 0), (0, HP - NOPE - ROPE))).reshape(Q_LORA, H * HP)


def _unpad_w_uq(g):
    return g.reshape(Q_LORA, H, HP)[:, :, :NOPE + ROPE].reshape(Q_LORA, H * (NOPE + ROPE))


def _split_w_ukv(w):
    w4 = w.reshape(KV_LORA, H // 2, 2, HP)
    z = jnp.zeros((KV_LORA, H // 2, NOPE), w.dtype)
    kn, vv = w4[..., :NOPE], w4[..., NOPE:]
    w_k = jnp.stack([jnp.concatenate([kn[:, :, 0], z], -1), jnp.concatenate([kn[:, :, 1], z], -1)], axis=2)
    w_v = jnp.stack([jnp.concatenate([vv[:, :, 0], z], -1), jnp.concatenate([z, vv[:, :, 1]], -1)], axis=2)
    return w_k.reshape(KV_LORA, H * HP), w_v.reshape(KV_LORA, H * HP)


def _join_w_ukv(g_k, g_v):
    gk = g_k.reshape(KV_LORA, H // 2, 2, HP)
    gv = g_v.reshape(KV_LORA, H // 2, 2, HP)
    even = jnp.concatenate([gk[:, :, 0, :NOPE], gv[:, :, 0, :VDIM]], -1)
    odd = jnp.concatenate([gk[:, :, 1, :NOPE], gv[:, :, 1, VDIM:]], -1)
    return jnp.stack([even, odd], axis=2).reshape(KV_LORA, H * HP)


def _rope_tables():
    half = ROPE // 2
    inv = ROPE_THETA ** (-jnp.arange(half, dtype=F32) / half)
    ang = jnp.arange(S, dtype=F32)[:, None] * inv[None, :]
    cos, sin = jnp.cos(ang), jnp.sin(ang)
    ones, zeros = jnp.ones((S, NOPE), F32), jnp.zeros((S, NOPE), F32)
    tail1, tail0 = jnp.ones((S, HP - NOPE - ROPE), F32), jnp.zeros((S, HP - NOPE - ROPE), F32)
    zh = jnp.zeros((S, half), F32)
    c = jnp.concatenate([ones, cos, cos, tail1], axis=1)
    sm = jnp.concatenate([zeros, -sin, zh, tail0], axis=1)
    sp = jnp.concatenate([zeros, zh, sin, tail0], axis=1)
    return c, sm, sp


def _local_step(x, mod, target, g_norm1, w_in_p, g_cq, w_uq_p, g_ckv, w_k, w_v, rel_bias, g_out_a, g_out_b, w_out,
                g_norm2, w_ffn_in, w_ffn_out, g_final, late_weights=None, on_ffn_grads=None, on_last_grads=None):
    nb = x.shape[0] // S
    sh1, sc1, g1, sh2, sc2, g2 = (mod[:, n].reshape(nb, 1, D) for n in range(N_MOD))
    rc, rsm, rsp = _rope_tables()
    biasm = _band_bias(rel_bias)

    h1 = _pre1(x, g_norm1, sc1, sh1)
    proj = _mm_nn(h1, w_in_p, F32, "proj")
    q, k, v, cqn, ckvn = _mla_pre(proj, g_cq, g_ckv, w_uq_p, w_k, w_v, rc, rsm, rsp)
    out_b, lse_b = _mla_fwd(q, k, v)
    out_a, lse_a = _dil_fwd(proj, biasm)
    y = _post_attn(out_a, out_b, g_out_a, g_out_b)
    if late_weights is not None:
        w_out, w_ffn_in, w_ffn_out = late_weights(y)
    mix = _mm_nn(y, w_out, F32, "mix")
    x2, h2 = _resid_norm2(x, mix, g1, g_norm2, sc2, sh2)
    ffn_g, ffn_u, act = _ffn_in(h2, w_ffn_in)
    f = _mm_nn(act, w_ffn_out, F32, "ffn_out")
    dx3, df, loss_cols, dg_final, dg2 = _final(x2, f, g2, g_final, target)

    dg_, du_ = _d_act(df, w_ffn_out, ffn_g, ffn_u)
    gw_ffn_out = _mm_tn(act, [df], "gw_ffn_out")
    dh2 = _d_h2(dg_, du_, w_ffn_in)
    gw_ffn_in = _mm_tn(h2, [dg_, du_], "gw_ffn_in")
    dx2, dsh2, dsc2, dg_norm2, dg1, dmix = _norm_bwd(x2, dh2, dx3, g_norm2, sc2, gate=(mix, g1))
    dy = _mm_nt(dmix, w_out, F32, "d_y")
    gw_out = _mm_tn(y, [dmix], "gw_out")
    if on_ffn_grads is not None:
        g_out_a = g_out_a + on_ffn_grads(gw_ffn_in, gw_ffn_out, gw_out)
    dout_a, dout_b, dg_out_a, dg_out_b = _post_attn_bwd(dy, out_a, out_b, g_out_a, g_out_b)
    dq_b, dk_b, dv_b = _mla_bwd(q, k, v, out_b, dout_b, lse_b)
    dq_a, dk_a, dv_a, dlogits = _dil_bwd(proj, biasm, out_a, dout_a, lse_a)
    g_rel = _rel_bias_grad(dlogits)
    dqr, dkr, dvr, dtail, dg_cq, dg_ckv = _mla_pre_bwd(proj, dq_b, dk_b, dv_b, g_cq, g_ckv, w_uq_p, w_k, w_v, rc, rsm, rsp)
    gw_uq = _mm_tn(cqn, [dqr], "gw_uq")
    gw_k, gw_v = _mm_tn(ckvn, [dkr, dvr], "gw_kv")
    dproj = jnp.concatenate([dq_a.astype(BF16), dk_a.astype(BF16), dv_a.astype(BF16), dtail], axis=1)
    dh1 = _mm_nt(dproj, w_in_p, F32, "d_h1")
    gw_in = _mm_tn(h1, [dproj], "gw_in")
    if on_last_grads is not None:
        g_norm1 = g_norm1 + on_last_grads(dict(w_in=gw_in, w_uq=gw_uq, w_k=gw_k, w_v=gw_v))
    grad_x, dsh1, dsc1, dg_norm1 = _norm_bwd(x, dh1, dx2, g_norm1, sc1)

    dmod = [dsh1, dsc1, dg1, dsh2, dsc2, dg2]
    small = dict(g_norm1=dg_norm1, g_cq=dg_cq, g_ckv=dg_ckv, rel_bias=g_rel, g_out_a=dg_out_a, g_out_b=dg_out_b,
                 g_norm2=dg_norm2, g_final=dg_final)
    big = dict(w_in=gw_in, w_uq=gw_uq, w_k=gw_k, w_v=gw_v, w_out=gw_out, w_ffn_in=gw_ffn_in, w_ffn_out=gw_ffn_out)
    return grad_x, dmod, loss_cols, small, big


def kernel(x, c, w_ada, b_ada, g_norm1, w_in, g_cq, w_uq, g_ckv, w_ukv, rel_bias, g_out_a, g_out_b, w_out, g_norm2, w_ffn_in, w_ffn_out, g_final, loss_target, m_w_ada, m_b_ada, m_g_norm1, m_w_in, m_g_cq, m_w_uq, m_g_ckv, m_w_ukv, m_rel_bias, m_g_out_a, m_g_out_b, m_w_out, m_g_norm2, m_w_ffn_in, m_w_ffn_out, m_g_final, v_w_ada, v_b_ada, v_g_norm1, v_w_in, v_g_cq, v_w_uq, v_g_ckv, v_w_ukv, v_rel_bias, v_g_out_a, v_g_out_b, v_w_out, v_g_norm2, v_w_ffn_in, v_w_ffn_out, v_g_final):
    nb = x.shape[0]
    t = nb * S
    xt, tt = x.reshape(t, D), loss_target.reshape(t, D)
    me = 4 * lax.axis_index("x") + 2 * lax.axis_index("y") + lax.axis_index("c")

    early = [w_in[0], w_uq[0], w_ukv[0]]
    gathered = _exchange([_silu_rows(c)] + [s.astype(BF16) for s in early], [True] * 4, "gather_weights")
    cond_all = gathered[0].reshape(N_DEV * nb, D)
    w_in_f, w_uq_f, w_ukv_f = (_cols_from_blocks(g) for g in gathered[1:4])
    w_k, w_v = _split_w_ukv(w_ukv_f)

    ncol = N_MOD * D // N_DEV
    b_slab = lax.dynamic_slice(b_ada, (0, me * ncol), (1, ncol))
    slab = _mod_slab(cond_all, w_ada, b_slab)
    (mod_rows,) = _exchange([slab.reshape(N_DEV, nb, ncol)], [False], "scatter_mod")
    mod = jnp.transpose(mod_rows, (1, 0, 2)).reshape(nb, N_MOD, D)

    late = [s.astype(BF16) for s in (w_out[0], w_ffn_in[0], w_ffn_out[0])]
    late_send, late_recv, late_src, late_land, late_token = _exchange_start(
        late, [_own_block_in_place(s, me) for s in late], [True] * 3, mod_rows, "gather_late_start")
    g_norm1_t = g_norm1 + late_token[:1, :1]

    def late_weights(after):
        w_out_g, w_ffn_in_g, w_ffn_out_g = _exchange_wait(late_send, late_recv, late_src, late_land, [True] * 3, after,
                                                          "gather_late_wait")
        return w_out_g.reshape(D, D), _cols_from_blocks(w_ffn_in_g), w_ffn_out_g.reshape(D_FF, D)

    flight = {}

    def start_grads(key, src, name):
        land = [_own_block_in_place(lax.dynamic_index_in_dim(s, me, 0, keepdims=False), me) for s in src]
        send, recv, src, land, token = _exchange_start(src, land, [False] * len(src), src[0], name)
        flight[key] = (send, recv, src, land)
        return token[:1, :1]

    def half_blocks(g):
        return jnp.transpose(g.reshape(D, N_DEV // 2, 2 * D_FF // N_DEV), (1, 0, 2))

    def on_ffn_grads(gw_ffn_in, gw_ffn_out, gw_out):
        return start_grads("ffn", [jnp.concatenate([half_blocks(g) for g in gw_ffn_in], axis=0),
                                   gw_ffn_out.reshape(N_DEV, D_FF // N_DEV, D), gw_out.reshape(N_DEV, D // N_DEV, D)],
                           "exchange_ffn_start")

    def on_last_grads(gw):
        return start_grads("rest", [_cols_to_blocks(_unpad_w_in(gw["w_in"])), _cols_to_blocks(_unpad_w_uq(gw["w_uq"])),
                                    _cols_to_blocks(_join_w_ukv(gw["w_k"], gw["w_v"]))], "exchange_rest_start")

    grad_x, dmod, loss_cols, small, _ = _local_step(
        xt, mod, tt, g_norm1_t, _pad_w_in(w_in_f), g_cq, _pad_w_uq(w_uq_f), g_ckv, w_k, w_v, rel_bias, g_out_a, g_out_b,
        None, g_norm2, None, None, g_final.reshape(1, D), late_weights=late_weights, on_ffn_grads=on_ffn_grads,
        on_last_grads=on_last_grads)

    upd = {}

    def land_and_update(key, names, after, name):
        got = _exchange_wait(*flight[key], [False] * len(names), after, name)
        for n, p in zip(names, got):
            w, m, v = big[n]
            upd[n] = _adamw(p, w, m, v, "adamw_" + n)

    big = dict(w_in=(w_in, m_w_in, v_w_in), w_uq=(w_uq, m_w_uq, v_w_uq), w_ukv=(w_ukv, m_w_ukv, v_w_ukv),
               w_out=(w_out, m_w_out, v_w_out), w_ffn_in=(w_ffn_in, m_w_ffn_in, v_w_ffn_in),
               w_ffn_out=(w_ffn_out, m_w_ffn_out, v_w_ffn_out))
    land_and_update("ffn", ["w_ffn_in", "w_ffn_out", "w_out"], grad_x, "exchange_ffn_wait")
    land_and_update("rest", ["w_in", "w_uq", "w_ukv"], upd["w_out"][0], "exchange_rest_wait")

    mine, dmod_blocks = _pack_small(dmod, [small[n] for n, _ in ROW_PARAMS], loss_cols)
    dmod_cols, pay, rel = _exchange([dmod_blocks, mine, small["rel_bias"]], [False, True, True], "exchange_small",
                                    after=upd["w_ukv"][0])
    g_ada = _ada_grad(cond_all, dmod_cols.reshape(N_DEV * nb, ncol))
    upd["w_ada"] = _adamw(g_ada[None], w_ada, m_w_ada, v_w_ada, "adamw_w_ada")
    row = lambda a: a.reshape(1, D)
    small_names = ["b_ada"] + [n for n, _ in ROW_PARAMS] + ["rel_bias"]
    small_w = [b_ada, g_norm1, g_cq, g_ckv, g_out_a, g_out_b, g_norm2, row(g_final), rel_bias]
    small_m = [m_b_ada, m_g_norm1, m_g_cq, m_g_ckv, m_g_out_a, m_g_out_b, m_g_norm2, row(m_g_final), m_rel_bias]
    small_v = [v_b_ada, v_g_norm1, v_g_cq, v_g_ckv, v_g_out_a, v_g_out_b, v_g_norm2, row(v_g_final), v_rel_bias]
    small_upd, loss8 = _small_update(pay, rel, small_w, small_m, small_v)
    upd.update(zip(small_names, small_upd))

    order = ["w_ada", "b_ada", "g_norm1", "w_in", "g_cq", "w_uq", "g_ckv", "w_ukv", "rel_bias", "g_out_a", "g_out_b",
             "w_out", "g_norm2", "w_ffn_in", "w_ffn_out", "g_final"]
    like = dict(g_final=g_final)
    outs = [loss8[0, 0], grad_x.reshape(x.shape)]
    for part in range(4):
        for n in order:
            val = upd[n][part]
            outs.append(val.reshape(like[n].shape) if n in like else val)
    return tuple(outs)
```

```python
import functools

import numpy as np
import jax
import jax.numpy as jnp
from jax import lax
from jax.experimental import pallas as pl
from jax.experimental.pallas import tpu as pltpu

F32, BF16 = jnp.float32, jnp.bfloat16

N_DEV = 8
D = 1024
S = 2048
H = 8
E_A = 64
D_A = H * E_A
Q_LORA, KV_LORA = 384, 256
NOPE, ROPE, VDIM = 64, 32, 64
HP = 128
P_IN = 3 * D_A + Q_LORA + KV_LORA + ROPE
P_PAD = 3 * D_A + Q_LORA + KV_LORA + HP
TAIL0 = 3 * D_A
TAIL = P_PAD - TAIL0
D_FF = 2816
N_MOD = 6
EPS = 1e-6
NEG = -1e30
BLK = 128
DILATIONS = (1, 4, 16)
N_BUCKETS, MAX_DISTANCE = 32, 2048
ROPE_THETA = 10000.0
SCALE_A = E_A ** -0.5
SCALE_B = (NOPE + ROPE) ** -0.5
B1, B2, LR, ADAM_EPS, WD, STEP = 0.9, 0.999, 0.001, 1e-8, 0.01, 10
VMEM_LIMIT = 56 * 1024 * 1024


def _cp(*sem):
    return pltpu.CompilerParams(dimension_semantics=sem, vmem_limit_bytes=VMEM_LIMIT)


def _pick(n, prefs):
    for p in prefs:
        if n % p == 0:
            return p
    raise ValueError(f"no tile of {prefs} divides {n}")


OPERAND_BYTES = 6 * 1024 * 1024


def _pick_rows(m, k):
    return _pick(m, [p for p in (1024, 512, 256, 128, 16) if p * k * 2 <= OPERAND_BYTES])


def _dot(a, b, dims):
    return lax.dot_general(a, b, (dims, ((), ())), preferred_element_type=F32)


def _mm_nn(a, b, out_dtype, name):
    m, k = a.shape
    n = b.shape[1]
    tm, tn = _pick_rows(m, k), _pick(n, (512, 384, 256, 128))

    def body(a_ref, b_ref, o_ref):
        o_ref[...] = _dot(a_ref[...], b_ref[...], ((1,), (0,))).astype(o_ref.dtype)

    return pl.pallas_call(
        body, name=name, grid=(m // tm, n // tn),
        in_specs=[pl.BlockSpec((tm, k), lambda i, j: (i, 0)), pl.BlockSpec((k, tn), lambda i, j: (0, j))],
        out_specs=pl.BlockSpec((tm, tn), lambda i, j: (i, j)),
        out_shape=jax.ShapeDtypeStruct((m, n), out_dtype),
        compiler_params=_cp("parallel", "parallel"),
    )(a, b)


def _mm_nt(a, b, out_dtype, name, after=None):
    m, k = a.shape
    n = b.shape[0]
    tm, tn = _pick_rows(m, k), _pick(n, (512, 384, 256, 128))

    def body(a_ref, b_ref, *rest):
        o_ref = rest[-1]
        o_ref[...] = _dot(a_ref[...], b_ref[...], ((1,), (1,))).astype(o_ref.dtype)

    extra = [] if after is None else [after]
    return pl.pallas_call(
        body, name=name, grid=(m // tm, n // tn),
        in_specs=[pl.BlockSpec((tm, k), lambda i, j: (i, 0)), pl.BlockSpec((tn, k), lambda i, j: (j, 0))] + [ANY] * len(extra),
        out_specs=pl.BlockSpec((tm, tn), lambda i, j: (i, j)),
        out_shape=jax.ShapeDtypeStruct((m, n), out_dtype),
        compiler_params=_cp("parallel", "parallel"),
    )(a, b, *extra)


def _mm_tn(a, bs, name):
    t, m = a.shape
    n = bs[0].shape[1]
    nb_ = len(bs)
    tc = _pick(t, (512, 16))
    tn = _pick(n, (512, 384, 256, 128))
    tm = _pick(m, [p for p in (1024, 512, 384, 256, 128) if (3 * p + 2 * nb_ * tn) * t * 2 <= VMEM_LIMIT - 2 * OPERAND_BYTES])
    if tm <= 256 and nb_ * n * t * 2 <= 2 * OPERAND_BYTES:
        tn = n

    def body(*refs):
        a_ref, b_refs, o_refs, at_ref = refs[0], refs[1:1 + nb_], refs[1 + nb_:1 + 2 * nb_], refs[-1]

        @pl.when(pl.program_id(1) == 0)
        def _():
            def chunk(c, _):
                rows = pl.ds(pl.multiple_of(c * tc, tc), tc)
                at_ref[:, rows] = a_ref[rows, :].T
                return 0

            lax.fori_loop(0, t // tc, chunk, 0)

        for b_ref, o_ref in zip(b_refs, o_refs):
            o_ref[...] = _dot(at_ref[...], b_ref[...], ((1,), (0,))).astype(BF16)

    res = pl.pallas_call(
        body, name=name, grid=(m // tm, n // tn),
        in_specs=[pl.BlockSpec((t, tm), lambda i, j: (0, i))] + [pl.BlockSpec((t, tn), lambda i, j: (0, j))] * nb_,
        out_specs=[pl.BlockSpec((tm, tn), lambda i, j: (i, j))] * nb_,
        out_shape=[jax.ShapeDtypeStruct((m, n), BF16)] * nb_,
        scratch_shapes=[pltpu.VMEM((tm, t), BF16)],
        compiler_params=_cp("parallel", "arbitrary"),
    )(a, *bs)
    return res[0] if nb_ == 1 else res


EPI = 256


def _silu_parts(g):
    sg = 1.0 / (1.0 + jnp.exp(-g))
    return sg, g * sg


def _ffn_in(h2, w):
    t, k = h2.shape
    tm, tn = _pick_rows(t, k), _pick(D_FF, (256, 128))
    nj = D_FF // tn

    def body(h_ref, wg_ref, wu_ref, g_ref, u_ref, a_ref):
        for r in range(tm // EPI):
            rows = slice(r * EPI, (r + 1) * EPI)
            hv = h_ref[rows, :]
            gb = _dot(hv, wg_ref[...], ((1,), (0,))).astype(BF16)
            ub = _dot(hv, wu_ref[...], ((1,), (0,))).astype(BF16)
            g_ref[rows, :] = gb
            u_ref[rows, :] = ub
            a_ref[rows, :] = (_silu_parts(gb.astype(F32))[1] * ub.astype(F32)).astype(BF16)

    blk = pl.BlockSpec((tm, tn), lambda i, j: (i, j))
    return pl.pallas_call(
        body, name="ffn_in", grid=(t // tm, nj),
        in_specs=[pl.BlockSpec((tm, k), lambda i, j: (i, 0)), pl.BlockSpec((k, tn), lambda i, j: (0, j)),
                  pl.BlockSpec((k, tn), lambda i, j: (0, j + nj))],
        out_specs=[blk] * 3, out_shape=[jax.ShapeDtypeStruct((t, D_FF), BF16)] * 3,
        compiler_params=_cp("parallel", "parallel"),
    )(h2, w, w)


def _d_act(df, w, g, u):
    t, k = df.shape
    tm, tn = _pick_rows(t, k), _pick(D_FF, (256, 128))

    def body(df_ref, w_ref, g_ref, u_ref, dg_ref, du_ref):
        for r in range(tm // EPI):
            rows = slice(r * EPI, (r + 1) * EPI)
            da = _dot(df_ref[rows, :], w_ref[...], ((1,), (1,)))
            gv = g_ref[rows, :].astype(F32)
            sg, silu = _silu_parts(gv)
            dg_ref[rows, :] = (da * u_ref[rows, :].astype(F32) * (sg * (1.0 + gv * (1.0 - sg)))).astype(BF16)
            du_ref[rows, :] = (da * silu).astype(BF16)

    blk = pl.BlockSpec((tm, tn), lambda i, j: (i, j))
    return pl.pallas_call(
        body, name="d_act", grid=(t // tm, D_FF // tn),
        in_specs=[pl.BlockSpec((tm, k), lambda i, j: (i, 0)), pl.BlockSpec((tn, k), lambda i, j: (j, 0)), blk, blk],
        out_specs=[blk] * 2, out_shape=[jax.ShapeDtypeStruct((t, D_FF), BF16)] * 2,
        compiler_params=_cp("parallel", "parallel"),
    )(df, w, g, u)


def _d_h2(dg, du, w):
    t = dg.shape[0]
    n = w.shape[0]
    tm, tn = _pick_rows(t, D_FF), _pick(n, (512, 256, 128))

    def body(dg_ref, du_ref, wg_ref, wu_ref, o_ref):
        o_ref[...] = (_dot(dg_ref[...], wg_ref[...], ((1,), (1,))) + _dot(du_ref[...], wu_ref[...], ((1,), (1,))))

    return pl.pallas_call(
        body, name="d_h2", grid=(t // tm, n // tn),
        in_specs=[pl.BlockSpec((tm, D_FF), lambda i, j: (i, 0)), pl.BlockSpec((tm, D_FF), lambda i, j: (i, 0)),
                  pl.BlockSpec((tn, D_FF), lambda i, j: (j, 0)), pl.BlockSpec((tn, D_FF), lambda i, j: (j, 1))],
        out_specs=pl.BlockSpec((tm, tn), lambda i, j: (i, j)),
        out_shape=jax.ShapeDtypeStruct((t, n), F32),
        compiler_params=_cp("parallel", "parallel"),
    )(dg, du, w, w)


TM = 256


def _row(w):
    return pl.BlockSpec((TM, w), lambda i: (i, 0))


def _row_at(w, col):
    return pl.BlockSpec((TM, w), lambda i: (i, col))


def _vec(w):
    return pl.BlockSpec((1, w), lambda i: (0, 0))


def _per_ex(w):
    return pl.BlockSpec((1, 1, w), lambda i: (i // (S // TM), 0, 0))


def _pos(w):
    return pl.BlockSpec((TM, w), lambda i: (i % (S // TM), 0))


def _full(shape):
    return pl.BlockSpec(shape, lambda i: (0,) * len(shape))


def _rms(x):
    return lax.rsqrt(jnp.mean(x * x, axis=-1, keepdims=True) + EPS)


def _rms_bwd(n, r, dn):
    return r * (dn - n * jnp.mean(dn * n, axis=-1, keepdims=True))


def _colsum(v):
    return jnp.sum(v, axis=0, keepdims=True)


def _acc_first(i, ref, val, every=None):
    first = (i == 0) if every is None else (i % every == 0)

    @pl.when(first)
    def _():
        ref[...] = jnp.zeros_like(ref)

    ref[...] += val.reshape(ref.shape)


def _pre1(x, g, sc, sh):
    t = x.shape[0]

    def body(x_ref, g_ref, sc_ref, sh_ref, h_ref):
        xv = x_ref[...]
        n = xv * _rms(xv)
        h_ref[...] = ((n * g_ref[...]) * (1.0 + sc_ref[0]) + sh_ref[0]).astype(BF16)

    return pl.pallas_call(
        body, name="pre1", grid=(t // TM,),
        in_specs=[_row(D), _vec(D), _per_ex(D), _per_ex(D)],
        out_specs=_row(D), out_shape=jax.ShapeDtypeStruct((t, D), BF16),
        compiler_params=_cp("parallel"),
    )(x, g, sc, sh)


def _rope_fwd(v, c, sm, sp):
    return v * c + pltpu.roll(v, HP - ROPE // 2, 1) * sm + pltpu.roll(v, ROPE // 2, 1) * sp


def _rope_bwd(dv, c, sm, sp):
    return dv * c + pltpu.roll(dv * sm, ROPE // 2, 1) + pltpu.roll(dv * sp, HP - ROPE // 2, 1)


def _mla_pre(proj, g_cq, g_ckv, w_uq, w_k, w_v, rc, rsm, rsp):
    t = proj.shape[0]

    def body(tail_ref, gq_ref, gkv_ref, wuq_ref, wk_ref, wv_ref, c_ref, sm_ref, sp_ref,
             q_ref, k_ref, v_ref, cqn_ref, ckvn_ref):
        tail = tail_ref[...]
        cq, ckv, kr = tail[:, :Q_LORA], tail[:, Q_LORA:Q_LORA + KV_LORA], tail[:, Q_LORA + KV_LORA:]
        cqn = (cq * _rms(cq) * gq_ref[...]).astype(BF16)
        ckvn = (ckv * _rms(ckv) * gkv_ref[...]).astype(BF16)
        cqn_ref[...] = cqn
        ckvn_ref[...] = ckvn
        c, sm, sp = c_ref[...], sm_ref[...], sp_ref[...]
        q = _dot(cqn, wuq_ref[...], ((1,), (0,)))
        kn = _dot(ckvn, wk_ref[...], ((1,), (0,)))
        v_ref[...] = _dot(ckvn, wv_ref[...], ((1,), (0,))).astype(BF16)
        krr = _rope_fwd(kr, c, sm, sp)
        for h in range(H):
            sl = slice(h * HP, (h + 1) * HP)
            q_ref[:, sl] = _rope_fwd(q[:, sl], c, sm, sp).astype(BF16)
            k_ref[:, sl] = (kn[:, sl] + krr).astype(BF16)

    wide = H * HP
    return pl.pallas_call(
        body, name="mla_pre", grid=(t // TM,),
        in_specs=[_row_at(TAIL, TAIL0 // TAIL), _vec(Q_LORA), _vec(KV_LORA), _full((Q_LORA, wide)),
                  _full((KV_LORA, wide)), _full((KV_LORA, wide)), _pos(HP), _pos(HP), _pos(HP)],
        out_specs=[_row(wide), _row(wide), _row(wide), _row(Q_LORA), _row(KV_LORA)],
        out_shape=[jax.ShapeDtypeStruct((t, wide), BF16)] * 3
        + [jax.ShapeDtypeStruct((t, Q_LORA), BF16), jax.ShapeDtypeStruct((t, KV_LORA), BF16)],
        compiler_params=_cp("parallel"),
    )(proj, g_cq, g_ckv, w_uq, w_k, w_v, rc, rsm, rsp)


def _mla_pre_bwd(proj, dq_, dk_, dv_, g_cq, g_ckv, w_uq, w_k, w_v, rc, rsm, rsp):
    t = proj.shape[0]
    wide = H * HP

    def body(tail_ref, dq_ref, dk_ref, dv_ref, gq_ref, gkv_ref, wuq_ref, wk_ref, wv_ref, c_ref, sm_ref, sp_ref,
             dqo_ref, dko_ref, dvo_ref, dtail_ref, dgq_ref, dgkv_ref):
        i = pl.program_id(0)
        tail = tail_ref[...]
        cq, ckv = tail[:, :Q_LORA], tail[:, Q_LORA:Q_LORA + KV_LORA]
        c, sm, sp = c_ref[...], sm_ref[...], sp_ref[...]
        dkr = jnp.zeros((TM, HP), F32)
        for h in range(H):
            sl = slice(h * HP, (h + 1) * HP)
            dqo_ref[:, sl] = _rope_bwd(dq_ref[:, sl], c, sm, sp).astype(BF16)
            dkr = dkr + dk_ref[:, sl]
        lane = lax.broadcasted_iota(jnp.int32, (TM, HP), 1)
        dkr = jnp.where((lane >= NOPE) & (lane < NOPE + ROPE), _rope_bwd(dkr, c, sm, sp), 0.0)
        dkb = dk_ref[...].astype(BF16)
        dvb = dv_ref[...].astype(BF16)
        dko_ref[...] = dkb
        dvo_ref[...] = dvb
        dcqn = _dot(dqo_ref[...], wuq_ref[...], ((1,), (1,)))
        dckvn = _dot(dkb, wk_ref[...], ((1,), (1,))) + _dot(dvb, wv_ref[...], ((1,), (1,)))
        rq, rkv = _rms(cq), _rms(ckv)
        nq, nkv = cq * rq, ckv * rkv
        _acc_first(i, dgq_ref, _colsum(dcqn * nq))
        _acc_first(i, dgkv_ref, _colsum(dckvn * nkv))
        dtail_ref[:, :Q_LORA] = _rms_bwd(nq, rq, dcqn * gq_ref[...]).astype(BF16)
        dtail_ref[:, Q_LORA:Q_LORA + KV_LORA] = _rms_bwd(nkv, rkv, dckvn * gkv_ref[...]).astype(BF16)
        dtail_ref[:, Q_LORA + KV_LORA:] = dkr.astype(BF16)

    return pl.pallas_call(
        body, name="mla_pre_bwd", grid=(t // TM,),
        in_specs=[_row_at(TAIL, TAIL0 // TAIL), _row(wide), _row(wide), _row(wide), _vec(Q_LORA), _vec(KV_LORA),
                  _full((Q_LORA, wide)), _full((KV_LORA, wide)), _full((KV_LORA, wide)), _pos(HP), _pos(HP), _pos(HP)],
        out_specs=[_row(wide), _row(wide), _row(wide), _row(TAIL), _vec(Q_LORA), _vec(KV_LORA)],
        out_shape=[jax.ShapeDtypeStruct((t, wide), BF16)] * 3 + [jax.ShapeDtypeStruct((t, TAIL), BF16),
                   jax.ShapeDtypeStruct((1, Q_LORA), F32), jax.ShapeDtypeStruct((1, KV_LORA), F32)],
        compiler_params=_cp("arbitrary"),
    )(proj, dq_, dk_, dv_, g_cq, g_ckv, w_uq, w_k, w_v, rc, rsm, rsp)


def _post_attn(out_a, out_b, g_a, g_b):
    t = out_a.shape[0]

    def body(a_ref, b_ref, ga_ref, gb_ref, y_ref):
        a, b = a_ref[...], b_ref[...]
        y_ref[:, :D_A] = (a * _rms(a) * ga_ref[...]).astype(BF16)
        y_ref[:, D_A:] = (b * _rms(b) * gb_ref[...]).astype(BF16)

    return pl.pallas_call(
        body, name="post_attn", grid=(t // TM,),
        in_specs=[_row(D_A), _row(D_A), _vec(D_A), _vec(D_A)],
        out_specs=_row(D), out_shape=jax.ShapeDtypeStruct((t, D), BF16),
        compiler_params=_cp("parallel"),
    )(out_a, out_b, g_a, g_b)


def _post_attn_bwd(dy, out_a, out_b, g_a, g_b):
    t = dy.shape[0]

    def body(dy_ref, a_ref, b_ref, ga_ref, gb_ref, da_ref, db_ref, dga_ref, dgb_ref):
        i = pl.program_id(0)
        dy_ = dy_ref[...]
        for src, g_ref, dst, dg_ref, sl in ((a_ref, ga_ref, da_ref, dga_ref, slice(0, D_A)),
                                            (b_ref, gb_ref, db_ref, dgb_ref, slice(D_A, D))):
            v = src[...]
            r = _rms(v)
            n = v * r
            dyv = dy_[:, sl]
            _acc_first(i, dg_ref, _colsum(dyv * n))
            dst[...] = _rms_bwd(n, r, dyv * g_ref[...])

    return pl.pallas_call(
        body, name="post_attn_bwd", grid=(t // TM,),
        in_specs=[_row(D), _row(D_A), _row(D_A), _vec(D_A), _vec(D_A)],
        out_specs=[_row(D_A), _row(D_A), _vec(D_A), _vec(D_A)],
        out_shape=[jax.ShapeDtypeStruct((t, D_A), F32)] * 2 + [jax.ShapeDtypeStruct((1, D_A), F32)] * 2,
        compiler_params=_cp("arbitrary"),
    )(dy, out_a, out_b, g_a, g_b)


def _resid_norm2(x, mix, g1, g, sc, sh):
    t = x.shape[0]

    def body(x_ref, mix_ref, g1_ref, g_ref, sc_ref, sh_ref, x2_ref, h_ref):
        x2 = x_ref[...] + g1_ref[0] * mix_ref[...]
        x2_ref[...] = x2
        n = x2 * _rms(x2)
        h_ref[...] = ((n * g_ref[...]) * (1.0 + sc_ref[0]) + sh_ref[0]).astype(BF16)

    return pl.pallas_call(
        body, name="resid_norm2", grid=(t // TM,),
        in_specs=[_row(D), _row(D), _per_ex(D), _vec(D), _per_ex(D), _per_ex(D)],
        out_specs=[_row(D), _row(D)],
        out_shape=[jax.ShapeDtypeStruct((t, D), F32), jax.ShapeDtypeStruct((t, D), BF16)],
        compiler_params=_cp("parallel"),
    )(x, mix, g1, g, sc, sh)


def _sigmoid(v):
    return 1.0 / (1.0 + jnp.exp(-v))


def _final(x2, f, g2, g_fin, target):
    t = x2.shape[0]
    nb = t // S
    tpb = S // TM

    def body(x2_ref, f_ref, g2_ref, g_ref, t_ref, dx3_ref, df_ref, loss_ref, dgf_ref, dg2_ref):
        i = pl.program_id(0)
        fv = f_ref[...]
        x3 = x2_ref[...] + g2_ref[0] * fv
        r = _rms(x3)
        n = x3 * r
        err = n * g_ref[...] - t_ref[...]
        _acc_first(i, loss_ref, _colsum(err * err))
        dy = err * (1.0 / D)
        _acc_first(i, dgf_ref, _colsum(dy * n))
        dx3 = _rms_bwd(n, r, dy * g_ref[...])
        dx3_ref[...] = dx3
        _acc_first(i, dg2_ref, _colsum(dx3 * fv), every=tpb)
        df_ref[...] = (dx3 * g2_ref[0]).astype(BF16)

    return pl.pallas_call(
        body, name="final", grid=(t // TM,),
        in_specs=[_row(D), _row(D), _per_ex(D), _vec(D), _row(D)],
        out_specs=[_row(D), _row(D), _vec(D), _vec(D), _per_ex(D)],
        out_shape=[jax.ShapeDtypeStruct((t, D), F32), jax.ShapeDtypeStruct((t, D), BF16),
                   jax.ShapeDtypeStruct((1, D), F32), jax.ShapeDtypeStruct((1, D), F32),
                   jax.ShapeDtypeStruct((nb, 1, D), F32)],
        compiler_params=_cp("arbitrary"),
    )(x2, f, g2, g_fin, target)


def _norm_bwd(xin, dh, dres, g, sc, gate=None):
    t = xin.shape[0]
    nb = t // S
    tpb = S // TM
    gated = gate is not None

    def body(*refs):
        if gated:
            x_ref, dh_ref, dres_ref, g_ref, sc_ref, mix_ref, g1_ref, dx_ref, dsh_ref, dsc_ref, dg_ref, dg1_ref, dmix_ref = refs
        else:
            x_ref, dh_ref, dres_ref, g_ref, sc_ref, dx_ref, dsh_ref, dsc_ref, dg_ref = refs
        i = pl.program_id(0)
        xv, dhv = x_ref[...], dh_ref[...]
        r = _rms(xv)
        n = xv * r
        gv = g_ref[...]
        _acc_first(i, dsh_ref, _colsum(dhv), every=tpb)
        _acc_first(i, dsc_ref, _colsum(dhv * (n * gv)), every=tpb)
        dng = dhv * (1.0 + sc_ref[0])
        _acc_first(i, dg_ref, _colsum(dng * n))
        dx = dres_ref[...] + _rms_bwd(n, r, dng * gv)
        dx_ref[...] = dx
        if gated:
            _acc_first(i, dg1_ref, _colsum(dx * mix_ref[...]), every=tpb)
            dmix_ref[...] = (dx * g1_ref[0]).astype(BF16)

    in_specs = [_row(D), _row(D), _row(D), _vec(D), _per_ex(D)]
    out_specs = [_row(D), _per_ex(D), _per_ex(D), _vec(D)]
    out_shape = [jax.ShapeDtypeStruct((t, D), F32), jax.ShapeDtypeStruct((nb, 1, D), F32),
                 jax.ShapeDtypeStruct((nb, 1, D), F32), jax.ShapeDtypeStruct((1, D), F32)]
    args = [xin, dh, dres, g, sc]
    if gated:
        in_specs += [_row(D), _per_ex(D)]
        out_specs += [_per_ex(D), _row(D)]
        out_shape += [jax.ShapeDtypeStruct((nb, 1, D), F32), jax.ShapeDtypeStruct((t, D), BF16)]
        args += list(gate)
    return pl.pallas_call(
        body, name="norm2_bwd" if gated else "norm1_bwd", grid=(t // TM,),
        in_specs=in_specs, out_specs=out_specs, out_shape=out_shape,
        compiler_params=_cp("arbitrary"),
    )(*args)


TQ = 256
TB = 512


def _mla_fwd(q, k, v):
    t = q.shape[0]
    nb = t // S

    def body(q_ref, k_ref, v_ref, o_ref, lse_ref):
        causal = lax.broadcasted_iota(jnp.int32, (TB, TB), 0) >= lax.broadcasted_iota(jnp.int32, (TB, TB), 1)
        heads = [slice(h * HP, (h + 1) * HP) for h in range(2)]
        for i in range(S // TB):
            ri, past = slice(i * TB, (i + 1) * TB), slice(0, i * TB)
            qhs = [q_ref[ri, sl] for sl in heads]
            sd = [jnp.where(causal, _dot(qh, k_ref[ri, sl], ((1,), (1,))) * SCALE_B, NEG) for qh, sl in zip(qhs, heads)]
            ms = [jnp.max(s, axis=-1, keepdims=True) for s in sd]
            if i:
                so = [_dot(qh, k_ref[past, sl], ((1,), (1,))) * SCALE_B for qh, sl in zip(qhs, heads)]
                ms = [jnp.maximum(m, jnp.max(s, axis=-1, keepdims=True)) for m, s in zip(ms, so)]
            pd = [jnp.exp(s - m) for s, m in zip(sd, ms)]
            ls = [jnp.sum(p, axis=-1, keepdims=True) for p in pd]
            acc = [_dot(p.astype(BF16), v_ref[ri, sl], ((1,), (0,))) for p, sl in zip(pd, heads)]
            if i:
                po = [jnp.exp(s - m) for s, m in zip(so, ms)]
                ls = [l + jnp.sum(p, axis=-1, keepdims=True) for l, p in zip(ls, po)]
                acc = [a + _dot(p.astype(BF16), v_ref[past, sl], ((1,), (0,))) for a, p, sl in zip(acc, po, heads)]
            o_ref[ri, :] = acc[0] / ls[0] + acc[1] / ls[1]
            for sl, m, l in zip(heads, ms, ls):
                lse_ref[ri, sl] = jnp.broadcast_to(m + jnp.log(l), (TB, HP))

    wide2 = pl.BlockSpec((S, 2 * HP), lambda b, p: (b, p))
    return pl.pallas_call(
        body, name="mla_fwd", grid=(nb, H // 2),
        in_specs=[wide2, wide2, wide2],
        out_specs=[pl.BlockSpec((S, HP), lambda b, p: (b, p)), wide2],
        out_shape=[jax.ShapeDtypeStruct((t, H * VDIM), F32), jax.ShapeDtypeStruct((t, H * HP), F32)],
        compiler_params=_cp("parallel", "parallel"),
    )(q, k, v)


def _mla_bwd(q, k, v, o, do, lse):
    t = q.shape[0]
    nb = t // S
    nq = S // TQ

    def body(q_ref, k_ref, v_ref, o_ref, do_ref, lse_ref, dq_ref, dk_ref, dv_ref):
        lane = lax.broadcasted_iota(jnp.int32, (TB, HP), 1)
        causal = lax.broadcasted_iota(jnp.int32, (TB, TB), 0) >= lax.broadcasted_iota(jnp.int32, (TB, TB), 1)
        heads = [slice(h * HP, (h + 1) * HP) for h in range(2)]
        nblk = S // TB
        for i in reversed(range(nblk)):
            ri, past = slice(i * TB, (i + 1) * TB), slice(0, i * TB)
            dov = do_ref[ri, :]
            prod = dov * o_ref[ri, :]
            dob = dov.astype(BF16)
            deltas = [jnp.sum(jnp.where((lane < VDIM) if h == 0 else (lane >= VDIM), prod, 0.0), axis=-1, keepdims=True)
                      for h in range(2)]
            qhs = [q_ref[ri, sl] for sl in heads]
            lses = [lse_ref[ri, sl][:, :1] for sl in heads]
            for rows, diagonal in ((ri, True), (past, False)):
                if rows.stop == rows.start:
                    continue
                ps = [jnp.exp(_dot(qh, k_ref[rows, sl], ((1,), (1,))) * SCALE_B - lse) for qh, sl, lse in zip(qhs, heads, lses)]
                if diagonal:
                    ps = [jnp.where(causal, p, 0.0) for p in ps]
                dps = [_dot(dob, v_ref[rows, sl], ((1,), (1,))) for sl in heads]
                dss = [(p * (dp - delta) * SCALE_B).astype(BF16) for p, dp, delta in zip(ps, dps, deltas)]
                for sl, qh, p, ds in zip(heads, qhs, ps, dss):
                    dq = _dot(ds, k_ref[rows, sl], ((1,), (0,)))
                    dk = _dot(ds, qh, ((0,), (0,)))
                    dv = _dot(p.astype(BF16), dob, ((0,), (0,)))
                    if diagonal:
                        dq_ref[ri, sl] = dq
                    else:
                        dq_ref[ri, sl] += dq
                    if i == nblk - 1:
                        dk_ref[rows, sl] = dk
                        dv_ref[rows, sl] = dv
                    else:
                        dk_ref[rows, sl] += dk
                        dv_ref[rows, sl] += dv

    wide2 = pl.BlockSpec((S, 2 * HP), lambda b, p: (b, p))
    pair = pl.BlockSpec((S, HP), lambda b, p: (b, p))
    return pl.pallas_call(
        body, name="mla_bwd", grid=(nb, H // 2),
        in_specs=[wide2, wide2, wide2, pair, pair, wide2],
        out_specs=[wide2, wide2, wide2],
        out_shape=[jax.ShapeDtypeStruct((t, H * HP), F32)] * 3,
        compiler_params=_cp("parallel", "parallel"),
    )(q, k, v, o, do, lse)


def _t5_bucket(dist):
    max_exact = N_BUCKETS // 2
    d = np.maximum(dist, 1).astype(np.float64)
    large = max_exact + (np.log(d / max_exact) / np.log(MAX_DISTANCE / max_exact) * (N_BUCKETS - max_exact)).astype(np.int64)
    large = np.minimum(large, N_BUCKETS - 1)
    return np.where(dist < max_exact, dist, large).astype(np.int32)


def _band_geometry():
    a = np.arange(BLK)[:, None]
    bk = np.arange(2 * BLK)[None, :]
    steps = BLK + a - bk
    valid = (steps >= 0) & (steps <= BLK)
    buckets = np.stack([_t5_bucket(np.clip(steps, 0, BLK) * d) for d in DILATIONS])
    return buckets, valid


def _band_bias(rel_bias):
    buckets, valid = _band_geometry()
    onehot = (jnp.asarray(buckets)[..., None] == jnp.arange(N_BUCKETS)).astype(F32)
    bias = jnp.einsum("rqkn,nh->rhqk", onehot, rel_bias, precision=lax.Precision.HIGHEST)
    bias = jnp.where(jnp.asarray(valid)[None, None], bias, NEG)
    return bias.reshape(3, H // 2, 2 * BLK, 2 * BLK)


def _dil_items():
    items = []
    for r, d in enumerate(DILATIONS):
        for res in range(d):
            for blk in range(S // d // BLK):
                items.append((r, d, blk * BLK * d + res, blk > 0))
    return items


GROUP = 4


def _strided(start, d):
    return pl.ds(start, BLK) if d == 1 else pl.ds(start, BLK, stride=d)


def _stack_heads(tile, own):
    return jnp.where(own, jnp.concatenate([tile, tile], axis=0), 0.0).astype(BF16)


def _own_lanes():
    row = lax.broadcasted_iota(jnp.int32, (2 * BLK, HP), 0)
    lane = lax.broadcasted_iota(jnp.int32, (2 * BLK, HP), 1)
    return (lane < E_A) == (row < BLK)


def _dil_fwd(proj, biasm):
    t = proj.shape[0]
    nb = t // S

    def body(q_ref, k_ref, v_ref, b_ref, o_ref, lse_ref, ob_ref, lb_ref):
        lane = lax.broadcasted_iota(jnp.int32, (BLK, HP), 1)
        own = _own_lanes()
        items = _dil_items()
        for g in range(0, len(items), GROUP):
            grp = items[g:g + GROUP]
            ss, vts = [], []
            for r, d, start, has_prev in grp:
                cur = _strided(start, d)
                rows = [_strided(start - BLK * d, d), cur] if has_prev else [cur]
                q2 = _stack_heads(q_ref[cur, :], own)
                kt = jnp.concatenate([k_ref[x, :] for x in rows], axis=0).astype(BF16)
                vts.append(jnp.concatenate([v_ref[x, :] for x in rows], axis=0).astype(BF16))
                bias = b_ref[r, 0] if has_prev else b_ref[r, 0, :, BLK:]
                ss.append(_dot(q2, kt, ((1,), (1,))) * SCALE_A + bias)
            ms = [jnp.max(s, axis=-1, keepdims=True) for s in ss]
            ps = [jnp.exp(s - m) for s, m in zip(ss, ms)]
            ls = [jnp.sum(p, axis=-1, keepdims=True) for p in ps]
            for (r, d, start, _), p, vt, m, l in zip(grp, ps, vts, ms, ls):
                cur = _strided(start, d)
                o2 = _dot(p.astype(BF16), vt, ((1,), (0,))) / l
                lse2 = m + jnp.log(l)
                ob_ref[r, cur, :] = jnp.where(lane < E_A, o2[:BLK], o2[BLK:])
                lb_ref[r, cur, :] = jnp.where(lane < E_A, lse2[:BLK], lse2[BLK:])

        def merge(c, _):
            rows = pl.ds(pl.multiple_of(c * TQ, TQ), TQ)
            l0, l1, l2 = lb_ref[0, rows, :], lb_ref[1, rows, :], lb_ref[2, rows, :]
            m = jnp.maximum(jnp.maximum(l0, l1), l2)
            e0, e1, e2 = jnp.exp(l0 - m), jnp.exp(l1 - m), jnp.exp(l2 - m)
            tot = e0 + e1 + e2
            o_ref[rows, :] = (e0 * ob_ref[0, rows, :] + e1 * ob_ref[1, rows, :] + e2 * ob_ref[2, rows, :]) / tot
            lse_ref[rows, :] = m + jnp.log(tot)
            return 0

        lax.fori_loop(0, S // TQ, merge, 0)

    npair = H // 2
    return pl.pallas_call(
        body, name="dil_fwd", grid=(nb, npair),
        in_specs=[pl.BlockSpec((S, HP), lambda b, p: (b, p)), pl.BlockSpec((S, HP), lambda b, p: (b, npair + p)),
                  pl.BlockSpec((S, HP), lambda b, p: (b, 2 * npair + p)),
                  pl.BlockSpec((3, 1, 2 * BLK, 2 * BLK), lambda b, p: (0, p, 0, 0))],
        out_specs=[pl.BlockSpec((S, HP), lambda b, p: (b, p))] * 2,
        out_shape=[jax.ShapeDtypeStruct((t, D_A), F32)] * 2,
        scratch_shapes=[pltpu.VMEM((3, S, HP), F32), pltpu.VMEM((3, S, HP), F32)],
        compiler_params=_cp("parallel", "parallel"),
    )(proj, proj, proj, biasm)


def _dil_bwd(proj, biasm, o, do, lse):
    t = proj.shape[0]
    nb = t // S

    def body(q_ref, k_ref, v_ref, b_ref, o_ref, do_ref, lse_ref, dq_out, dk_out, dv_out, ds_ref, dq_ref, dk_ref, dv_ref):
        dq_ref[...] = jnp.zeros_like(dq_ref)
        dk_ref[...] = jnp.zeros_like(dk_ref)
        dv_ref[...] = jnp.zeros_like(dv_ref)
        ds_ref[...] = jnp.zeros_like(ds_ref)
        lane = lax.broadcasted_iota(jnp.int32, (BLK, HP), 1)
        own = _own_lanes()
        items = _dil_items()
        for g in range(0, len(items), GROUP):
            grp = items[g:g + GROUP]
            q2s, kts, do2s, ss, dps, lse2s, delta2s = [], [], [], [], [], [], []
            for r, d, start, has_prev in grp:
                cur = _strided(start, d)
                rows = [_strided(start - BLK * d, d), cur] if has_prev else [cur]
                q2 = _stack_heads(q_ref[cur, :], own)
                kt = jnp.concatenate([k_ref[x, :] for x in rows], axis=0).astype(BF16)
                vt = jnp.concatenate([v_ref[x, :] for x in rows], axis=0).astype(BF16)
                dot_ = do_ref[cur, :]
                prod = dot_ * o_ref[cur, :]
                lset = lse_ref[cur, :]
                do2 = _stack_heads(dot_, own)
                bias = b_ref[r, 0] if has_prev else b_ref[r, 0, :, BLK:]
                ss.append(_dot(q2, kt, ((1,), (1,))) * SCALE_A + bias)
                dps.append(_dot(do2, vt, ((1,), (1,))))
                lse2s.append(jnp.concatenate([lset[:, :1], lset[:, E_A:E_A + 1]], axis=0))
                delta2s.append(jnp.concatenate([jnp.sum(jnp.where(lane < E_A, prod, 0.0), axis=-1, keepdims=True),
                                                jnp.sum(jnp.where(lane >= E_A, prod, 0.0), axis=-1, keepdims=True)], axis=0))
                q2s.append(q2)
                kts.append(kt)
                do2s.append(do2)
            ps = [jnp.exp(s - lse2) for s, lse2 in zip(ss, lse2s)]
            dls = [p * (dp - delta2) for p, dp, delta2 in zip(ps, dps, delta2s)]
            for (r, d, start, has_prev), q2, kt, do2, p, dl in zip(grp, q2s, kts, do2s, ps, dls):
                cur = _strided(start, d)
                dsb = (dl * SCALE_A).astype(BF16)
                dq2 = _dot(dsb, kt, ((1,), (0,)))
                dkt = _dot(dsb, q2, ((0,), (0,)))
                dvt = _dot(p.astype(BF16), do2, ((0,), (0,)))
                dq_ref[cur, :] += jnp.where(lane < E_A, dq2[:BLK], dq2[BLK:])
                if has_prev:
                    prev = _strided(start - BLK * d, d)
                    ds_ref[0, r, 0] += dl
                    dk_ref[prev, :] += dkt[:BLK]
                    dv_ref[prev, :] += dvt[:BLK]
                    dk_ref[cur, :] += dkt[BLK:]
                    dv_ref[cur, :] += dvt[BLK:]
                else:
                    ds_ref[0, r, 0, :, BLK:] += dl
                    dk_ref[cur, :] += dkt
                    dv_ref[cur, :] += dvt
        dq_out[...] = dq_ref[...].astype(BF16)
        dk_out[...] = dk_ref[...].astype(BF16)
        dv_out[...] = dv_ref[...].astype(BF16)

    npair = H // 2
    pair = pl.BlockSpec((S, HP), lambda b, p: (b, p))
    return pl.pallas_call(
        body, name="dil_bwd", grid=(nb, npair),
        in_specs=[pair, pl.BlockSpec((S, HP), lambda b, p: (b, npair + p)),
                  pl.BlockSpec((S, HP), lambda b, p: (b, 2 * npair + p)),
                  pl.BlockSpec((3, 1, 2 * BLK, 2 * BLK), lambda b, p: (0, p, 0, 0)), pair, pair, pair],
        out_specs=[pair, pair, pair, pl.BlockSpec((1, 3, 1, 2 * BLK, 2 * BLK), lambda b, p: (b, 0, p, 0, 0))],
        out_shape=[jax.ShapeDtypeStruct((t, D_A), BF16)] * 3 + [jax.ShapeDtypeStruct((nb, 3, npair, 2 * BLK, 2 * BLK), F32)],
        scratch_shapes=[pltpu.VMEM((S, HP), F32)] * 3,
        compiler_params=_cp("parallel", "parallel"),
    )(proj, proj, proj, biasm, o, do, lse)


def _rel_bias_grad(dlogits):
    nb = dlogits.shape[0]
    buckets, _ = _band_geometry()
    kk = 3 * BLK * 2 * BLK
    dl = jnp.transpose(dlogits.reshape(nb, 3, H, BLK, 2 * BLK), (0, 2, 1, 3, 4)).reshape(nb, H, kk)
    bk = jnp.asarray(buckets.reshape(1, kk))
    tk = kk // 12

    def body(dl_ref, bk_ref, o_ref):
        j = pl.program_id(0)
        onehot = (bk_ref[...] == lax.broadcasted_iota(jnp.int32, (N_BUCKETS, tk), 0)).astype(F32)
        tot = dl_ref[0]
        for b in range(1, nb):
            tot = tot + dl_ref[b]
        part = lax.dot_general(onehot, tot, ((((1,), (1,))), ((), ())), preferred_element_type=F32,
                               precision=lax.Precision.HIGHEST)
        _acc_first(j, o_ref, part)

    return pl.pallas_call(
        body, name="rel_bias_grad", grid=(kk // tk,),
        in_specs=[pl.BlockSpec((nb, H, tk), lambda j: (0, 0, j)), pl.BlockSpec((1, tk), lambda j: (0, j))],
        out_specs=pl.BlockSpec((N_BUCKETS, H), lambda j: (0, 0)),
        out_shape=jax.ShapeDtypeStruct((N_BUCKETS, H), F32),
        compiler_params=_cp("arbitrary"),
    )(dl, bk)


def _mesh_place():
    x, y, c = lax.axis_index("x"), lax.axis_index("y"), lax.axis_index("c")
    return x, y, c


def _peer(k):
    x, y, c = _mesh_place()
    px = 1 - x if k & 4 else x
    py = 1 - y if k & 2 else y
    pc = 1 - c if k & 1 else c
    return (px, py, pc), 4 * px + 2 * py + pc


ANY = pl.BlockSpec(memory_space=pl.ANY)


def _exchange(arrays, gathers, name, after=None):
    n_arr = len(arrays)

    def body(*refs):
        ins, outs = refs[:n_arr], refs[n_arr + 1:2 * n_arr + 1]
        send, recv, loc = refs[2 * n_arr + 1:]
        x, y, c = _mesh_place()
        me = 4 * x + 2 * y + c
        local = [pltpu.make_async_copy(ins[a] if gathers[a] else ins[a].at[me], outs[a].at[me], loc.at[a])
                 for a in range(n_arr)]
        remote = _peer_copies(ins, outs, send, recv, gathers)
        for cp in local:
            cp.start()
        for put, _ in remote:
            put.start()
        for cp in local:
            cp.wait()
        for put, got in remote:
            put.wait_send()
            got.wait_recv()

    return pl.pallas_call(
        body, name=name,
        in_specs=[ANY] * (n_arr + 1), out_specs=[ANY] * n_arr,
        out_shape=[jax.ShapeDtypeStruct(((N_DEV,) if g else ()) + a.shape, a.dtype) for a, g in zip(arrays, gathers)],
        scratch_shapes=[pltpu.SemaphoreType.DMA((n_arr * (N_DEV - 1),)), pltpu.SemaphoreType.DMA((n_arr * (N_DEV - 1),)),
                        pltpu.SemaphoreType.DMA((n_arr,))],
        compiler_params=pltpu.CompilerParams(has_side_effects=True),
    )(*arrays, arrays[0] if after is None else after)


def _gather_two_level(arrays, name):
    n_arr = len(arrays)
    per = N_DEV - 1

    def body(*refs):
        ins, outs = refs[:n_arr], refs[n_arr:2 * n_arr]
        send, recv, loc = refs[2 * n_arr:]
        x, y, c = _mesh_place()
        me, sibling = (x, y, c), (x, y, 1 - c)
        chips = [(1 - x, y), (x, 1 - y), (1 - x, 1 - y)]

        def block(a, place):
            px, py, pc = place
            return outs[a].at[4 * px + 2 * py + pc]

        def copy(a, k, place, to, src=None):
            dst = block(a, place)
            return pltpu.make_async_remote_copy(dst if src is None else src, dst, send.at[a * per + k], recv.at[a * per + k],
                                                device_id=to, device_id_type=pl.DeviceIdType.MESH)

        local = [pltpu.make_async_copy(ins[a], block(a, me), loc.at[a]) for a in range(n_arr)]
        for cp in local:
            cp.start()
        first = []
        for a in range(n_arr):
            first.append(copy(a, 0, me, sibling, src=ins[a]))
            first += [copy(a, 1 + j, me, (*chip, c), src=ins[a]) for j, chip in enumerate(chips)]
        for cp in first:
            cp.start()
        passed = []
        for j, chip in enumerate(chips):
            for a in range(n_arr):
                copy(a, 1 + j, (*chip, c), me).wait_recv()
                passed.append(copy(a, 4 + j, (*chip, c), sibling))
                passed[-1].start()
        for a in range(n_arr):
            copy(a, 0, sibling, me).wait_recv()
            for j, chip in enumerate(chips):
                copy(a, 4 + j, (*chip, 1 - c), me).wait_recv()
        for cp in first + passed:
            cp.wait_send()
        for cp in local:
            cp.wait()

    return pl.pallas_call(
        body, name=name,
        in_specs=[ANY] * n_arr, out_specs=[ANY] * n_arr,
        out_shape=[jax.ShapeDtypeStruct((N_DEV,) + a.shape, a.dtype) for a in arrays],
        scratch_shapes=[pltpu.SemaphoreType.DMA((n_arr * per,)), pltpu.SemaphoreType.DMA((n_arr * per,)),
                        pltpu.SemaphoreType.DMA((n_arr,))],
        compiler_params=pltpu.CompilerParams(has_side_effects=True),
    )(*arrays)


HBM = pl.BlockSpec(memory_space=pltpu.HBM)
SEM = pl.BlockSpec(memory_space=pltpu.SEMAPHORE)
DATAFLOW = pltpu.SideEffectType.DATAFLOW_SIDE_EFFECTING


def _own_block_in_place(block, me):
    land = lax.empty((N_DEV,) + block.shape, block.dtype)
    return lax.dynamic_update_slice(land, block[None], (me,) + (0,) * block.ndim)


def _peer_copies(srcs, lands, send, recv, gathers):
    x, y, c = _mesh_place()
    me = 4 * x + 2 * y + c
    out = []
    for a, (src, land) in enumerate(zip(srcs, lands)):
        for k in range(1, N_DEV):
            dev, idx = _peer(k)
            sem = a * (N_DEV - 1) + k - 1
            mine = src if gathers[a] else src.at[idx]
            put = pltpu.make_async_remote_copy(mine, land.at[me], send.at[sem], recv.at[sem],
                                               device_id=dev, device_id_type=pl.DeviceIdType.MESH)
            got = pltpu.make_async_remote_copy(mine, land.at[idx], send.at[sem], recv.at[sem],
                                               device_id=dev, device_id_type=pl.DeviceIdType.MESH)
            out.append((put, got))
    return out


def _exchange_start(srcs, lands, gather, after, name):
    n = len(srcs)

    def body(*refs):
        srcs_, lands_, send, recv = refs[:n], refs[n:2 * n], refs[2 * n + 1], refs[2 * n + 2]
        for put, _ in _peer_copies(srcs_, lands_, send, recv, gather):
            put.start()
        refs[-1][...] = jnp.zeros_like(refs[-1])

    nsem = n * (N_DEV - 1)
    thru = [pltpu.HBM(a.shape, a.dtype) for a in list(srcs) + list(lands)]
    res = pl.pallas_call(
        body, name=name,
        out_shape=(pltpu.SemaphoreType.DMA((nsem,)), pltpu.SemaphoreType.DMA((nsem,)), *thru, jax.ShapeDtypeStruct((8, 128), F32)),
        in_specs=[HBM] * (2 * n) + [ANY],
        out_specs=(SEM, SEM, *([HBM] * (2 * n)), pl.BlockSpec(memory_space=pltpu.VMEM)),
        input_output_aliases={i: 2 + i for i in range(2 * n)},
        compiler_params=pltpu.CompilerParams(has_side_effects=DATAFLOW),
    )(*[pltpu.with_memory_space_constraint(a, pltpu.HBM) for a in list(srcs) + list(lands)], after)
    return res[0], res[1], list(res[2:2 + n]), list(res[2 + n:2 + 2 * n]), res[-1]


def _exchange_wait(send, recv, srcs, lands, gather, after, name):
    n = len(srcs)

    def body(*refs):
        srcs_, lands_, send_, recv_ = refs[:n], refs[n:2 * n], refs[2 * n], refs[2 * n + 1]
        for put, got in _peer_copies(srcs_, lands_, send_, recv_, gather):
            put.wait_send()
            got.wait_recv()

    thru = [pltpu.HBM(a.shape, a.dtype) for a in list(srcs) + list(lands)]
    res = pl.pallas_call(
        body, name=name, out_shape=tuple(thru),
        in_specs=[HBM] * (2 * n) + [SEM, SEM, ANY], out_specs=tuple([HBM] * (2 * n)),
        input_output_aliases={i: i for i in range(2 * n)},
        compiler_params=pltpu.CompilerParams(has_side_effects=DATAFLOW),
    )(*srcs, *lands, send, recv, after)
    return list(res[n:])


def _silu_rows(c):
    def body(c_ref, o_ref):
        v = c_ref[...]
        o_ref[...] = v * _sigmoid(v)

    return pl.pallas_call(body, name="cond", out_shape=jax.ShapeDtypeStruct(c.shape, F32))(c)


def _mod_slab(cond_all, w_ada, b_slab):
    def body(c_ref, w_ref, b_ref, o_ref):
        o_ref[...] = _dot(c_ref[...].astype(BF16), w_ref[0].astype(BF16), ((1,), (0,))) + b_ref[...]

    return pl.pallas_call(body, name="mod_slab",
                          out_shape=jax.ShapeDtypeStruct((cond_all.shape[0], w_ada.shape[2]), F32),
                          compiler_params=pltpu.CompilerParams(vmem_limit_bytes=VMEM_LIMIT))(cond_all, w_ada, b_slab)


def _ada_grad(cond_all, dmod_cols):
    def body(c_ref, d_ref, o_ref):
        o_ref[...] = _dot(c_ref[...].astype(BF16), d_ref[...].astype(BF16), ((0,), (0,)))

    return pl.pallas_call(body, name="ada_grad",
                          out_shape=jax.ShapeDtypeStruct((cond_all.shape[1], dmod_cols.shape[1]), F32),
                          compiler_params=pltpu.CompilerParams(vmem_limit_bytes=VMEM_LIMIT))(cond_all, dmod_cols)


def _adam_math(g, w, m, v):
    m2 = B1 * m + (1.0 - B1) * g
    v2 = B2 * v + (1.0 - B2) * (g * g)
    m_hat = m2 / (1.0 - B1 ** STEP)
    v_hat = v2 / (1.0 - B2 ** STEP)
    return -LR * (m_hat / (jnp.sqrt(v_hat) + ADAM_EPS) + WD * w), m2, v2


def _adamw(parts, w, m, v, name):
    n, rows, cols = parts.shape
    tr = _pick(rows, (128, 96, 64, 32, 16, 8))

    def body(p_ref, w_ref, m_ref, v_ref, g_ref, d_ref, m2_ref, v2_ref):
        g = p_ref[0].astype(F32)
        for s in range(1, n):
            g = g + p_ref[s].astype(F32)
        g_ref[0] = g
        d_ref[0], m2_ref[0], v2_ref[0] = _adam_math(g, w_ref[0], m_ref[0], v_ref[0])

    blk = pl.BlockSpec((1, tr, cols), lambda i: (0, i, 0))
    return pl.pallas_call(
        body, name=name, grid=(rows // tr,),
        in_specs=[pl.BlockSpec((n, tr, cols), lambda i: (0, i, 0)), blk, blk, blk],
        out_specs=[blk] * 4, out_shape=[jax.ShapeDtypeStruct((1, rows, cols), F32)] * 4,
        compiler_params=_cp("parallel"),
    )(parts, w, m, v)


ROW_PARAMS = (("g_norm1", D), ("g_cq", Q_LORA), ("g_ckv", KV_LORA), ("g_out_a", D_A), ("g_out_b", D_A), ("g_norm2", D),
              ("g_final", D))
LOSS_ROW = N_MOD + len(ROW_PARAMS)
PAY_ROWS = 16
NCOL = N_MOD * D // N_DEV


def _pack_small(dmods, rows, loss_cols):
    nb = dmods[0].shape[0]
    nrow = len(ROW_PARAMS)

    def body(*refs):
        dm, rw, loss_ref, pay_ref, blk_ref = refs[:N_MOD], refs[N_MOD:N_MOD + nrow], refs[N_MOD + nrow], refs[-2], refs[-1]
        pay_ref[...] = jnp.zeros_like(pay_ref)
        for k in range(N_MOD):
            tot = dm[k][0]
            for b in range(1, nb):
                tot = tot + dm[k][b]
            pay_ref[k:k + 1, :] = tot
        for i, (_, n) in enumerate(ROW_PARAMS):
            pay_ref[N_MOD + i:N_MOD + i + 1, :n] = rw[i][...]
        pay_ref[LOSS_ROW:LOSS_ROW + 1, :] = loss_ref[...]
        for j in range(N_DEV):
            done = 0
            while done < NCOL:
                seg, off = divmod(j * NCOL + done, D)
                ln = min(NCOL - done, D - off)
                for b in range(nb):
                    blk_ref[j, b:b + 1, done:done + ln] = dm[seg][b][:, off:off + ln]
                done += ln

    return pl.pallas_call(
        body, name="pack_small",
        out_shape=[jax.ShapeDtypeStruct((PAY_ROWS, D), F32), jax.ShapeDtypeStruct((N_DEV, nb, NCOL), F32)],
    )(*dmods, *rows, loss_cols)


def _small_update(pay, rel, ws, ms, vs):
    n_par = len(ws)

    def body(*refs):
        pay_ref, rel_ref = refs[:2]
        w_refs, m_refs, v_refs = (refs[2 + s * n_par:2 + (s + 1) * n_par] for s in range(3))
        outs, loss_ref = refs[2 + 3 * n_par:-1], refs[-1]
        tot, rtot = pay_ref[0], rel_ref[0]
        for s in range(1, N_DEV):
            tot, rtot = tot + pay_ref[s], rtot + rel_ref[s]

        def update(p, g, sl):
            outs[4 * p][:, sl] = g
            outs[4 * p + 1][:, sl], outs[4 * p + 2][:, sl], outs[4 * p + 3][:, sl] = _adam_math(
                g, w_refs[p][:, sl], m_refs[p][:, sl], v_refs[p][:, sl])

        for k in range(N_MOD):
            update(0, tot[k:k + 1, :], slice(k * D, (k + 1) * D))
        for i, (_, n) in enumerate(ROW_PARAMS):
            update(1 + i, tot[N_MOD + i:N_MOD + i + 1, :n], slice(0, n))
        update(n_par - 1, rtot, slice(0, H))
        loss_ref[...] = jnp.broadcast_to((0.5 / D) * jnp.sum(tot[LOSS_ROW:LOSS_ROW + 1, :]), loss_ref.shape)

    shapes = [jax.ShapeDtypeStruct(w.shape, F32) for w in ws for _ in range(4)]
    res = pl.pallas_call(
        body, name="small_update", out_shape=shapes + [jax.ShapeDtypeStruct((8, 128), F32)],
    )(pay, rel, *ws, *ms, *vs)
    return [tuple(res[4 * p:4 * p + 4]) for p in range(n_par)], res[-1]


def _cols_from_blocks(g):
    return jnp.transpose(g, (1, 0, 2)).reshape(g.shape[1], N_DEV * g.shape[2])


def _cols_to_blocks(w):
    r, c = w.shape
    return jnp.transpose(w.reshape(r, N_DEV, c // N_DEV), (1, 0, 2))


def _pad_w_in(w):
    z = jnp.zeros((w.shape[0], NOPE), w.dtype)
    return jnp.concatenate([w[:, :P_IN - ROPE], z, w[:, P_IN - ROPE:], z[:, :HP - NOPE - ROPE]], axis=1)


def _unpad_w_in(g):
    k0 = P_IN - ROPE + NOPE
    return jnp.concatenate([g[:, :P_IN - ROPE], g[:, k0:k0 + ROPE]], axis=1)


def _pad_w_uq(w):
    w3 = w.reshape(Q_LORA, H, NOPE + ROPE)
    return jnp.pad(w3, ((0, 0), (0, 0), (0, HP - NOPE - ROPE))).reshape(Q_LORA, H * HP)


def _unpad_w_uq(g):
    return g.reshape(Q_LORA, H, HP)[:, :, :NOPE + ROPE].reshape(Q_LORA, H * (NOPE + ROPE))


def _split_w_ukv(w):
    w4 = w.reshape(KV_LORA, H // 2, 2, HP)
    z = jnp.zeros((KV_LORA, H // 2, NOPE), w.dtype)
    kn, vv = w4[..., :NOPE], w4[..., NOPE:]
    w_k = jnp.stack([jnp.concatenate([kn[:, :, 0], z], -1), jnp.concatenate([kn[:, :, 1], z], -1)], axis=2)
    w_v = jnp.stack([jnp.concatenate([vv[:, :, 0], z], -1), jnp.concatenate([z, vv[:, :, 1]], -1)], axis=2)
    return w_k.reshape(KV_LORA, H * HP), w_v.reshape(KV_LORA, H * HP)


def _join_w_ukv(g_k, g_v):
    gk = g_k.reshape(KV_LORA, H // 2, 2, HP)
    gv = g_v.reshape(KV_LORA, H // 2, 2, HP)
    even = jnp.concatenate([gk[:, :, 0, :NOPE], gv[:, :, 0, :VDIM]], -1)
    odd = jnp.concatenate([gk[:, :, 1, :NOPE], gv[:, :, 1, VDIM:]], -1)
    return jnp.stack([even, odd], axis=2).reshape(KV_LORA, H * HP)


def _rope_tables():
    half = ROPE // 2
    inv = ROPE_THETA ** (-jnp.arange(half, dtype=F32) / half)
    ang = jnp.arange(S, dtype=F32)[:, None] * inv[None, :]
    cos, sin = jnp.cos(ang), jnp.sin(ang)
    ones, zeros = jnp.ones((S, NOPE), F32), jnp.zeros((S, NOPE), F32)
    tail1, tail0 = jnp.ones((S, HP - NOPE - ROPE), F32), jnp.zeros((S, HP - NOPE - ROPE), F32)
    zh = jnp.zeros((S, half), F32)
    c = jnp.concatenate([ones, cos, cos, tail1], axis=1)
    sm = jnp.concatenate([zeros, -sin, zh, tail0], axis=1)
    sp = jnp.concatenate([zeros, zh, sin, tail0], axis=1)
    return c, sm, sp


def _local_step(x, mod, target, g_norm1, w_in_p, g_cq, w_uq_p, g_ckv, w_k, w_v, rel_bias, g_out_a, g_out_b, w_out,
                g_norm2, w_ffn_in, w_ffn_out, g_final, late_weights=None, on_ffn_grads=None, on_last_grads=None):
    nb = x.shape[0] // S
    sh1, sc1, g1, sh2, sc2, g2 = (mod[:, n].reshape(nb, 1, D) for n in range(N_MOD))
    rc, rsm, rsp = _rope_tables()
    biasm = _band_bias(rel_bias)

    h1 = _pre1(x, g_norm1, sc1, sh1)
    proj = _mm_nn(h1, w_in_p, F32, "proj")
    q, k, v, cqn, ckvn = _mla_pre(proj, g_cq, g_ckv, w_uq_p, w_k, w_v, rc, rsm, rsp)
    out_b, lse_b = _mla_fwd(q, k, v)
    out_a, lse_a = _dil_fwd(proj, biasm)
    y = _post_attn(out_a, out_b, g_out_a, g_out_b)
    if late_weights is not None:
        w_out, w_ffn_in, w_ffn_out = late_weights(y)
    mix = _mm_nn(y, w_out, F32, "mix")
    x2, h2 = _resid_norm2(x, mix, g1, g_norm2, sc2, sh2)
    ffn_g, ffn_u, act = _ffn_in(h2, w_ffn_in)
    f = _mm_nn(act, w_ffn_out, F32, "ffn_out")
    dx3, df, loss_cols, dg_final, dg2 = _final(x2, f, g2, g_final, target)

    dg_, du_ = _d_act(df, w_ffn_out, ffn_g, ffn_u)
    gw_ffn_out = _mm_tn(act, [df], "gw_ffn_out")
    dh2 = _d_h2(dg_, du_, w_ffn_in)
    gw_ffn_in = _mm_tn(h2, [dg_, du_], "gw_ffn_in")
    dx2, dsh2, dsc2, dg_norm2, dg1, dmix = _norm_bwd(x2, dh2, dx3, g_norm2, sc2, gate=(mix, g1))
    dy = _mm_nt(dmix, w_out, F32, "d_y")
    gw_out = _mm_tn(y, [dmix], "gw_out")
    if on_ffn_grads is not None:
        g_out_a = g_out_a + on_ffn_grads(gw_ffn_in, gw_ffn_out, gw_out)
    dout_a, dout_b, dg_out_a, dg_out_b = _post_attn_bwd(dy, out_a, out_b, g_out_a, g_out_b)
    dq_b, dk_b, dv_b = _mla_bwd(q, k, v, out_b, dout_b, lse_b)
    dq_a, dk_a, dv_a, dlogits = _dil_bwd(proj, biasm, out_a, dout_a, lse_a)
    g_rel = _rel_bias_grad(dlogits)
    dqr, dkr, dvr, dtail, dg_cq, dg_ckv = _mla_pre_bwd(proj, dq_b, dk_b, dv_b, g_cq, g_ckv, w_uq_p, w_k, w_v, rc, rsm, rsp)
    gw_uq = _mm_tn(cqn, [dqr], "gw_uq")
    gw_k, gw_v = _mm_tn(ckvn, [dkr, dvr], "gw_kv")
    dproj = jnp.concatenate([dq_a, dk_a, dv_a, dtail], axis=1)
    gw_in = _mm_tn(h1, [dproj], "gw_in")
    if on_last_grads is not None:
        started = on_last_grads(dict(w_in=gw_in, w_uq=gw_uq, w_k=gw_k, w_v=gw_v))
    else:
        started = None
    dh1 = _mm_nt(dproj, w_in_p, F32, "d_h1", after=started)
    grad_x, dsh1, dsc1, dg_norm1 = _norm_bwd(x, dh1, dx2, g_norm1, sc1)

    dmod = [dsh1, dsc1, dg1, dsh2, dsc2, dg2]
    small = dict(g_norm1=dg_norm1, g_cq=dg_cq, g_ckv=dg_ckv, rel_bias=g_rel, g_out_a=dg_out_a, g_out_b=dg_out_b,
                 g_norm2=dg_norm2, g_final=dg_final)
    big = dict(w_in=gw_in, w_uq=gw_uq, w_k=gw_k, w_v=gw_v, w_out=gw_out, w_ffn_in=gw_ffn_in, w_ffn_out=gw_ffn_out)
    return grad_x, dmod, loss_cols, small, big


def kernel(x, c, w_ada, b_ada, g_norm1, w_in, g_cq, w_uq, g_ckv, w_ukv, rel_bias, g_out_a, g_out_b, w_out, g_norm2, w_ffn_in, w_ffn_out, g_final, loss_target, m_w_ada, m_b_ada, m_g_norm1, m_w_in, m_g_cq, m_w_uq, m_g_ckv, m_w_ukv, m_rel_bias, m_g_out_a, m_g_out_b, m_w_out, m_g_norm2, m_w_ffn_in, m_w_ffn_out, m_g_final, v_w_ada, v_b_ada, v_g_norm1, v_w_in, v_g_cq, v_w_uq, v_g_ckv, v_w_ukv, v_rel_bias, v_g_out_a, v_g_out_b, v_w_out, v_g_norm2, v_w_ffn_in, v_w_ffn_out, v_g_final):
    nb = x.shape[0]
    t = nb * S
    xt, tt = x.reshape(t, D), loss_target.reshape(t, D)
    me = 4 * lax.axis_index("x") + 2 * lax.axis_index("y") + lax.axis_index("c")

    early = [w_in[0], w_uq[0], w_ukv[0]]
    gathered = _gather_two_level([_silu_rows(c)] + [s.astype(BF16) for s in early], "gather_weights")
    cond_all = gathered[0].reshape(N_DEV * nb, D)
    w_in_f, w_uq_f, w_ukv_f = (_cols_from_blocks(g) for g in gathered[1:4])
    w_k, w_v = _split_w_ukv(w_ukv_f)

    ncol = N_MOD * D // N_DEV
    b_slab = lax.dynamic_slice(b_ada, (0, me * ncol), (1, ncol))
    slab = _mod_slab(cond_all, w_ada, b_slab)
    (mod_rows,) = _exchange([slab.reshape(N_DEV, nb, ncol)], [False], "scatter_mod")
    mod = jnp.transpose(mod_rows, (1, 0, 2)).reshape(nb, N_MOD, D)

    late = [s.astype(BF16) for s in (w_out[0], w_ffn_in[0], w_ffn_out[0])]
    late_send, late_recv, late_src, late_land, late_token = _exchange_start(
        late, [_own_block_in_place(s, me) for s in late], [True] * 3, mod_rows, "gather_late_start")
    g_norm1_t = g_norm1 + late_token[:1, :1]

    def late_weights(after):
        w_out_g, w_ffn_in_g, w_ffn_out_g = _exchange_wait(late_send, late_recv, late_src, late_land, [True] * 3, after,
                                                          "gather_late_wait")
        return w_out_g.reshape(D, D), _cols_from_blocks(w_ffn_in_g), w_ffn_out_g.reshape(D_FF, D)

    flight = {}

    def start_grads(key, src, name):
        land = [_own_block_in_place(lax.dynamic_index_in_dim(s, me, 0, keepdims=False), me) for s in src]
        send, recv, src, land, token = _exchange_start(src, land, [False] * len(src), src[0], name)
        flight[key] = (send, recv, src, land)
        return token[:1, :1]

    def half_blocks(g):
        return jnp.transpose(g.reshape(D, N_DEV // 2, 2 * D_FF // N_DEV), (1, 0, 2))

    def on_ffn_grads(gw_ffn_in, gw_ffn_out, gw_out):
        return start_grads("ffn", [jnp.concatenate([half_blocks(g) for g in gw_ffn_in], axis=0),
                                   gw_ffn_out.reshape(N_DEV, D_FF // N_DEV, D), gw_out.reshape(N_DEV, D // N_DEV, D)],
                           "exchange_ffn_start")

    def on_last_grads(gw):
        return start_grads("rest", [_cols_to_blocks(_unpad_w_in(gw["w_in"])), _cols_to_blocks(_unpad_w_uq(gw["w_uq"])),
                                    _cols_to_blocks(_join_w_ukv(gw["w_k"], gw["w_v"]))], "exchange_rest_start")

    grad_x, dmod, loss_cols, small, _ = _local_step(
        xt, mod, tt, g_norm1_t, _pad_w_in(w_in_f), g_cq, _pad_w_uq(w_uq_f), g_ckv, w_k, w_v, rel_bias, g_out_a, g_out_b,
        None, g_norm2, None, None, g_final.reshape(1, D), late_weights=late_weights, on_ffn_grads=on_ffn_grads,
        on_last_grads=on_last_grads)

    upd = {}

    def land_and_update(key, names, after, name):
        got = _exchange_wait(*flight[key], [False] * len(names), after, name)
        for n, p in zip(names, got):
            w, m, v = big[n]
            upd[n] = _adamw(p, w, m, v, "adamw_" + n)

    big = dict(w_in=(w_in, m_w_in, v_w_in), w_uq=(w_uq, m_w_uq, v_w_uq), w_ukv=(w_ukv, m_w_ukv, v_w_ukv),
               w_out=(w_out, m_w_out, v_w_out), w_ffn_in=(w_ffn_in, m_w_ffn_in, v_w_ffn_in),
               w_ffn_out=(w_ffn_out, m_w_ffn_out, v_w_ffn_out))
    land_and_update("ffn", ["w_ffn_in", "w_ffn_out", "w_out"], grad_x, "exchange_ffn_wait")
    land_and_update("rest", ["w_in", "w_uq", "w_ukv"], upd["w_out"][0], "exchange_rest_wait")

    mine, dmod_blocks = _pack_small(dmod, [small[n] for n, _ in ROW_PARAMS], loss_cols)
    dmod_cols, pay, rel = _exchange([dmod_blocks, mine, small["rel_bias"]], [False, True, True], "exchange_small",
                                    after=upd["w_ukv"][0])
    g_ada = _ada_grad(cond_all, dmod_cols.reshape(N_DEV * nb, ncol))
    upd["w_ada"] = _adamw(g_ada[None], w_ada, m_w_ada, v_w_ada, "adamw_w_ada")
    row = lambda a: a.reshape(1, D)
    small_names = ["b_ada"] + [n for n, _ in ROW_PARAMS] + ["rel_bias"]
    small_w = [b_ada, g_norm1, g_cq, g_ckv, g_out_a, g_out_b, g_norm2, row(g_final), rel_bias]
    small_m = [m_b_ada, m_g_norm1, m_g_cq, m_g_ckv, m_g_out_a, m_g_out_b, m_g_norm2, row(m_g_final), m_rel_bias]
    small_v = [v_b_ada, v_g_norm1, v_g_cq, v_g_ckv, v_g_out_a, v_g_out_b, v_g_norm2, row(v_g_final), v_rel_bias]
    small_upd, loss8 = _small_update(pay, rel, small_w, small_m, small_v)
    upd.update(zip(small_names, small_upd))

    order = ["w_ada", "b_ada", "g_norm1", "w_in", "g_cq", "w_uq", "g_ckv", "w_ukv", "rel_bias", "g_out_a", "g_out_b",
             "w_out", "g_norm2", "w_ffn_in", "w_ffn_out", "g_final"]
    like = dict(g_final=g_final)
    outs = [loss8[0, 0], grad_x.reshape(x.shape)]
    for part in range(4):
        for n in order:
            val = upd[n][part]
            outs.append(val.reshape(like[n].shape) if n in like else val)
    return tuple(outs)
```

```python
import functools

import numpy as np
import jax
import jax.numpy as jnp
from jax import lax
from jax.experimental import pallas as pl
from jax.experimental.pallas import tpu as pltpu

F32, BF16 = jnp.float32, jnp.bfloat16

N_DEV = 8
D = 1024
S = 2048
H = 8
E_A = 64
D_A = H * E_A
Q_LORA, KV_LORA = 384, 256
NOPE, ROPE, VDIM = 64, 32, 64
HP = 128
P_IN = 3 * D_A + Q_LORA + KV_LORA + ROPE
P_PAD = 3 * D_A + Q_LORA + KV_LORA + HP
TAIL0 = 3 * D_A
TAIL = P_PAD - TAIL0
D_FF = 2816
N_MOD = 6
EPS = 1e-6
NEG = -1e30
BLK = 128
DILATIONS = (1, 4, 16)
N_BUCKETS, MAX_DISTANCE = 32, 2048
ROPE_THETA = 10000.0
SCALE_A = E_A ** -0.5
SCALE_B = (NOPE + ROPE) ** -0.5
B1, B2, LR, ADAM_EPS, WD, STEP = 0.9, 0.999, 0.001, 1e-8, 0.01, 10
VMEM_LIMIT = 56 * 1024 * 1024


def _cp(*sem):
    return pltpu.CompilerParams(dimension_semantics=sem, vmem_limit_bytes=VMEM_LIMIT)


def _pick(n, prefs):
    for p in prefs:
        if n % p == 0:
            return p
    raise ValueError(f"no tile of {prefs} divides {n}")


OPERAND_BYTES = 6 * 1024 * 1024


def _pick_rows(m, k):
    return _pick(m, [p for p in (1024, 512, 256, 128, 16) if p * k * 2 <= OPERAND_BYTES])


def _dot(a, b, dims):
    return lax.dot_general(a, b, (dims, ((), ())), preferred_element_type=F32)


def _mm_nn(a, b, out_dtype, name):
    m, k = a.shape
    n = b.shape[1]
    tm, tn = _pick_rows(m, k), _pick(n, (512, 384, 256, 128))

    def body(a_ref, b_ref, o_ref):
        o_ref[...] = _dot(a_ref[...], b_ref[...], ((1,), (0,))).astype(o_ref.dtype)

    return pl.pallas_call(
        body, name=name, grid=(m // tm, n // tn),
        in_specs=[pl.BlockSpec((tm, k), lambda i, j: (i, 0)), pl.BlockSpec((k, tn), lambda i, j: (0, j))],
        out_specs=pl.BlockSpec((tm, tn), lambda i, j: (i, j)),
        out_shape=jax.ShapeDtypeStruct((m, n), out_dtype),
        compiler_params=_cp("parallel", "parallel"),
    )(a, b)


def _mm_nt(a, b, out_dtype, name, after=None):
    m, k = a.shape
    n = b.shape[0]
    tm, tn = _pick_rows(m, k), _pick(n, (512, 384, 256, 128))

    def body(a_ref, b_ref, *rest):
        o_ref = rest[-1]
        o_ref[...] = _dot(a_ref[...], b_ref[...], ((1,), (1,))).astype(o_ref.dtype)

    extra = [] if after is None else [after]
    return pl.pallas_call(
        body, name=name, grid=(m // tm, n // tn),
        in_specs=[pl.BlockSpec((tm, k), lambda i, j: (i, 0)), pl.BlockSpec((tn, k), lambda i, j: (j, 0))] + [ANY] * len(extra),
        out_specs=pl.BlockSpec((tm, tn), lambda i, j: (i, j)),
        out_shape=jax.ShapeDtypeStruct((m, n), out_dtype),
        compiler_params=_cp("parallel", "parallel"),
    )(a, b, *extra)


def _mm_tn(a, bs, name):
    t, m = a.shape
    n = bs[0].shape[1]
    nb_ = len(bs)
    tc = _pick(t, (512, 16))
    tn = _pick(n, (512, 384, 256, 128))
    tm = _pick(m, [p for p in (1024, 512, 384, 256, 128) if (3 * p + 2 * nb_ * tn) * t * 2 <= VMEM_LIMIT - 2 * OPERAND_BYTES])
    if tm <= 256 and nb_ * n * t * 2 <= 2 * OPERAND_BYTES:
        tn = n

    def body(*refs):
        a_ref, b_refs, o_refs, at_ref = refs[0], refs[1:1 + nb_], refs[1 + nb_:1 + 2 * nb_], refs[-1]

        @pl.when(pl.program_id(1) == 0)
        def _():
            def chunk(c, _):
                rows = pl.ds(pl.multiple_of(c * tc, tc), tc)
                at_ref[:, rows] = a_ref[rows, :].T
                return 0

            lax.fori_loop(0, t // tc, chunk, 0)

        for b_ref, o_ref in zip(b_refs, o_refs):
            o_ref[...] = _dot(at_ref[...], b_ref[...], ((1,), (0,))).astype(BF16)

    res = pl.pallas_call(
        body, name=name, grid=(m // tm, n // tn),
        in_specs=[pl.BlockSpec((t, tm), lambda i, j: (0, i))] + [pl.BlockSpec((t, tn), lambda i, j: (0, j))] * nb_,
        out_specs=[pl.BlockSpec((tm, tn), lambda i, j: (i, j))] * nb_,
        out_shape=[jax.ShapeDtypeStruct((m, n), BF16)] * nb_,
        scratch_shapes=[pltpu.VMEM((tm, t), BF16)],
        compiler_params=_cp("parallel", "arbitrary"),
    )(a, *bs)
    return res[0] if nb_ == 1 else res


EPI = 256


def _silu_parts(g):
    sg = 0.5 * jnp.tanh(0.5 * g) + 0.5
    return sg, g * sg


def _ffn_in(h2, w):
    t, k = h2.shape
    tm, tn = _pick_rows(t, k), _pick(D_FF, (256, 128))
    nj = D_FF // tn

    def body(h_ref, wg_ref, wu_ref, g_ref, u_ref, a_ref):
        for r in range(tm // EPI):
            rows = slice(r * EPI, (r + 1) * EPI)
            hv = h_ref[rows, :]
            g = _dot(hv, wg_ref[...], ((1,), (0,)))
            u = _dot(hv, wu_ref[...], ((1,), (0,)))
            g_ref[rows, :] = g.astype(BF16)
            u_ref[rows, :] = u.astype(BF16)
            a_ref[rows, :] = (_silu_parts(g)[1] * u).astype(BF16)

    blk = pl.BlockSpec((tm, tn), lambda i, j: (i, j))
    return pl.pallas_call(
        body, name="ffn_in", grid=(t // tm, nj),
        in_specs=[pl.BlockSpec((tm, k), lambda i, j: (i, 0)), pl.BlockSpec((k, tn), lambda i, j: (0, j)),
                  pl.BlockSpec((k, tn), lambda i, j: (0, j + nj))],
        out_specs=[blk] * 3, out_shape=[jax.ShapeDtypeStruct((t, D_FF), BF16)] * 3,
        compiler_params=_cp("parallel", "parallel"),
    )(h2, w, w)


def _d_act(df, w, g, u):
    t, k = df.shape
    tm, tn = _pick_rows(t, k), _pick(D_FF, (256, 128))

    def body(df_ref, w_ref, g_ref, u_ref, dg_ref, du_ref):
        for r in range(tm // EPI):
            rows = slice(r * EPI, (r + 1) * EPI)
            da = _dot(df_ref[rows, :], w_ref[...], ((1,), (1,)))
            gv = g_ref[rows, :].astype(F32)
            sg, silu = _silu_parts(gv)
            dg_ref[rows, :] = ((da * u_ref[rows, :].astype(F32)) * (sg + silu * (1.0 - sg))).astype(BF16)
            du_ref[rows, :] = (da * silu).astype(BF16)

    blk = pl.BlockSpec((tm, tn), lambda i, j: (i, j))
    return pl.pallas_call(
        body, name="d_act", grid=(t // tm, D_FF // tn),
        in_specs=[pl.BlockSpec((tm, k), lambda i, j: (i, 0)), pl.BlockSpec((tn, k), lambda i, j: (j, 0)), blk, blk],
        out_specs=[blk] * 2, out_shape=[jax.ShapeDtypeStruct((t, D_FF), BF16)] * 2,
        compiler_params=_cp("parallel", "parallel"),
    )(df, w, g, u)


def _d_h2(dg, du, w):
    t = dg.shape[0]
    n = w.shape[0]
    tm, tn = _pick_rows(t, D_FF), _pick(n, (512, 256, 128))

    def body(dg_ref, du_ref, wg_ref, wu_ref, o_ref):
        o_ref[...] = (_dot(dg_ref[...], wg_ref[...], ((1,), (1,)))
                      + _dot(du_ref[...], wu_ref[...], ((1,), (1,)))).astype(BF16)

    return pl.pallas_call(
        body, name="d_h2", grid=(t // tm, n // tn),
        in_specs=[pl.BlockSpec((tm, D_FF), lambda i, j: (i, 0)), pl.BlockSpec((tm, D_FF), lambda i, j: (i, 0)),
                  pl.BlockSpec((tn, D_FF), lambda i, j: (j, 0)), pl.BlockSpec((tn, D_FF), lambda i, j: (j, 1))],
        out_specs=pl.BlockSpec((tm, tn), lambda i, j: (i, j)),
        out_shape=jax.ShapeDtypeStruct((t, n), BF16),
        compiler_params=_cp("parallel", "parallel"),
    )(dg, du, w, w)


TM = 256


def _row(w):
    return pl.BlockSpec((TM, w), lambda i: (i, 0))


def _row_at(w, col):
    return pl.BlockSpec((TM, w), lambda i: (i, col))


def _vec(w):
    return pl.BlockSpec((1, w), lambda i: (0, 0))


def _per_ex(w):
    return pl.BlockSpec((1, 1, w), lambda i: (i // (S // TM), 0, 0))


def _pos(w):
    return pl.BlockSpec((TM, w), lambda i: (i % (S // TM), 0))


def _full(shape):
    return pl.BlockSpec(shape, lambda i: (0,) * len(shape))


def _rms(x):
    return lax.rsqrt(jnp.mean(x * x, axis=-1, keepdims=True) + EPS)


def _rms_bwd(n, r, dn):
    return r * (dn - n * jnp.mean(dn * n, axis=-1, keepdims=True))


def _colsum(v):
    return jnp.sum(v, axis=0, keepdims=True)


def _acc_first(i, ref, val, every=None):
    first = (i == 0) if every is None else (i % every == 0)

    @pl.when(first)
    def _():
        ref[...] = jnp.zeros_like(ref)

    ref[...] += val.reshape(ref.shape)


def _pre1(x, g, sc, sh):
    t = x.shape[0]

    def body(x_ref, g_ref, sc_ref, sh_ref, h_ref):
        xv = x_ref[...]
        n = xv * _rms(xv)
        h_ref[...] = ((n * g_ref[...]) * (1.0 + sc_ref[0]) + sh_ref[0]).astype(BF16)

    return pl.pallas_call(
        body, name="pre1", grid=(t // TM,),
        in_specs=[_row(D), _vec(D), _per_ex(D), _per_ex(D)],
        out_specs=_row(D), out_shape=jax.ShapeDtypeStruct((t, D), BF16),
        compiler_params=_cp("parallel"),
    )(x, g, sc, sh)


def _rope_fwd(v, c, sm, sp):
    return v * c + pltpu.roll(v, HP - ROPE // 2, 1) * sm + pltpu.roll(v, ROPE // 2, 1) * sp


def _rope_bwd(dv, c, sm, sp):
    return dv * c + pltpu.roll(dv * sm, ROPE // 2, 1) + pltpu.roll(dv * sp, HP - ROPE // 2, 1)


def _mla_pre(proj, g_cq, g_ckv, w_uq, w_k, w_v, rc, rsm, rsp):
    t = proj.shape[0]

    def body(tail_ref, gq_ref, gkv_ref, wuq_ref, wk_ref, wv_ref, c_ref, sm_ref, sp_ref,
             q_ref, k_ref, v_ref, cqn_ref, ckvn_ref):
        tail = tail_ref[...]
        cq, ckv, kr = tail[:, :Q_LORA], tail[:, Q_LORA:Q_LORA + KV_LORA], tail[:, Q_LORA + KV_LORA:]
        cqn = (cq * _rms(cq) * gq_ref[...]).astype(BF16)
        ckvn = (ckv * _rms(ckv) * gkv_ref[...]).astype(BF16)
        cqn_ref[...] = cqn
        ckvn_ref[...] = ckvn
        c, sm, sp = c_ref[...], sm_ref[...], sp_ref[...]
        q = _dot(cqn, wuq_ref[...], ((1,), (0,)))
        kn = _dot(ckvn, wk_ref[...], ((1,), (0,)))
        v_ref[...] = _dot(ckvn, wv_ref[...], ((1,), (0,))).astype(BF16)
        krr = _rope_fwd(kr, c, sm, sp)
        for h in range(H):
            sl = slice(h * HP, (h + 1) * HP)
            q_ref[:, sl] = _rope_fwd(q[:, sl], c, sm, sp).astype(BF16)
            k_ref[:, sl] = (kn[:, sl] + krr).astype(BF16)

    wide = H * HP
    return pl.pallas_call(
        body, name="mla_pre", grid=(t // TM,),
        in_specs=[_row_at(TAIL, TAIL0 // TAIL), _vec(Q_LORA), _vec(KV_LORA), _full((Q_LORA, wide)),
                  _full((KV_LORA, wide)), _full((KV_LORA, wide)), _pos(HP), _pos(HP), _pos(HP)],
        out_specs=[_row(wide), _row(wide), _row(wide), _row(Q_LORA), _row(KV_LORA)],
        out_shape=[jax.ShapeDtypeStruct((t, wide), BF16)] * 3
        + [jax.ShapeDtypeStruct((t, Q_LORA), BF16), jax.ShapeDtypeStruct((t, KV_LORA), BF16)],
        compiler_params=_cp("parallel"),
    )(proj, g_cq, g_ckv, w_uq, w_k, w_v, rc, rsm, rsp)


def _mla_pre_bwd(proj, dq_, dk_, dv_, g_cq, g_ckv, w_uq, w_k, w_v, rc, rsm, rsp):
    t = proj.shape[0]
    wide = H * HP

    def body(tail_ref, dq_ref, dk_ref, dv_ref, gq_ref, gkv_ref, wuq_ref, wk_ref, wv_ref, c_ref, sm_ref, sp_ref,
             dqo_ref, dko_ref, dvo_ref, dtail_ref, dgq_ref, dgkv_ref):
        i = pl.program_id(0)
        tail = tail_ref[...]
        cq, ckv = tail[:, :Q_LORA], tail[:, Q_LORA:Q_LORA + KV_LORA]
        c, sm, sp = c_ref[...], sm_ref[...], sp_ref[...]
        dkr = jnp.zeros((TM, HP), F32)
        for h in range(H):
            sl = slice(h * HP, (h + 1) * HP)
            dqo_ref[:, sl] = _rope_bwd(dq_ref[:, sl], c, sm, sp).astype(BF16)
            dkr = dkr + dk_ref[:, sl]
        lane = lax.broadcasted_iota(jnp.int32, (TM, HP), 1)
        dkr = jnp.where((lane >= NOPE) & (lane < NOPE + ROPE), _rope_bwd(dkr, c, sm, sp), 0.0)
        dkb = dk_ref[...].astype(BF16)
        dvb = dv_ref[...].astype(BF16)
        dko_ref[...] = dkb
        dvo_ref[...] = dvb
        dcqn = _dot(dqo_ref[...], wuq_ref[...], ((1,), (1,)))
        dckvn = _dot(dkb, wk_ref[...], ((1,), (1,))) + _dot(dvb, wv_ref[...], ((1,), (1,)))
        rq, rkv = _rms(cq), _rms(ckv)
        nq, nkv = cq * rq, ckv * rkv
        _acc_first(i, dgq_ref, _colsum(dcqn * nq))
        _acc_first(i, dgkv_ref, _colsum(dckvn * nkv))
        dtail_ref[:, :Q_LORA] = _rms_bwd(nq, rq, dcqn * gq_ref[...]).astype(BF16)
        dtail_ref[:, Q_LORA:Q_LORA + KV_LORA] = _rms_bwd(nkv, rkv, dckvn * gkv_ref[...]).astype(BF16)
        dtail_ref[:, Q_LORA + KV_LORA:] = dkr.astype(BF16)

    return pl.pallas_call(
        body, name="mla_pre_bwd", grid=(t // TM,),
        in_specs=[_row_at(TAIL, TAIL0 // TAIL), _row(wide), _row(wide), _row(wide), _vec(Q_LORA), _vec(KV_LORA),
                  _full((Q_LORA, wide)), _full((KV_LORA, wide)), _full((KV_LORA, wide)), _pos(HP), _pos(HP), _pos(HP)],
        out_specs=[_row(wide), _row(wide), _row(wide), _row(TAIL), _vec(Q_LORA), _vec(KV_LORA)],
        out_shape=[jax.ShapeDtypeStruct((t, wide), BF16)] * 3 + [jax.ShapeDtypeStruct((t, TAIL), BF16),
                   jax.ShapeDtypeStruct((1, Q_LORA), F32), jax.ShapeDtypeStruct((1, KV_LORA), F32)],
        compiler_params=_cp("arbitrary"),
    )(proj, dq_, dk_, dv_, g_cq, g_ckv, w_uq, w_k, w_v, rc, rsm, rsp)


def _post_attn(out_a, out_b, g_a, g_b):
    t = out_a.shape[0]

    def body(a_ref, b_ref, ga_ref, gb_ref, y_ref):
        a, b = a_ref[...], b_ref[...]
        y_ref[:, :D_A] = (a * _rms(a) * ga_ref[...]).astype(BF16)
        y_ref[:, D_A:] = (b * _rms(b) * gb_ref[...]).astype(BF16)

    return pl.pallas_call(
        body, name="post_attn", grid=(t // TM,),
        in_specs=[_row(D_A), _row(D_A), _vec(D_A), _vec(D_A)],
        out_specs=_row(D), out_shape=jax.ShapeDtypeStruct((t, D), BF16),
        compiler_params=_cp("parallel"),
    )(out_a, out_b, g_a, g_b)


def _post_attn_bwd(dy, out_a, out_b, g_a, g_b):
    t = dy.shape[0]

    def body(dy_ref, a_ref, b_ref, ga_ref, gb_ref, da_ref, db_ref, dga_ref, dgb_ref):
        i = pl.program_id(0)
        dy_ = dy_ref[...].astype(F32)
        for src, g_ref, dst, dg_ref, sl in ((a_ref, ga_ref, da_ref, dga_ref, slice(0, D_A)),
                                            (b_ref, gb_ref, db_ref, dgb_ref, slice(D_A, D))):
            v = src[...]
            r = _rms(v)
            n = v * r
            dyv = dy_[:, sl]
            _acc_first(i, dg_ref, _colsum(dyv * n))
            dst[...] = _rms_bwd(n, r, dyv * g_ref[...])

    return pl.pallas_call(
        body, name="post_attn_bwd", grid=(t // TM,),
        in_specs=[_row(D), _row(D_A), _row(D_A), _vec(D_A), _vec(D_A)],
        out_specs=[_row(D_A), _row(D_A), _vec(D_A), _vec(D_A)],
        out_shape=[jax.ShapeDtypeStruct((t, D_A), F32)] * 2 + [jax.ShapeDtypeStruct((1, D_A), F32)] * 2,
        compiler_params=_cp("arbitrary"),
    )(dy, out_a, out_b, g_a, g_b)


def _resid_norm2(x, mix, g1, g, sc, sh):
    t = x.shape[0]

    def body(x_ref, mix_ref, g1_ref, g_ref, sc_ref, sh_ref, x2_ref, h_ref):
        x2 = x_ref[...] + g1_ref[0] * mix_ref[...]
        x2_ref[...] = x2
        n = x2 * _rms(x2)
        h_ref[...] = ((n * g_ref[...]) * (1.0 + sc_ref[0]) + sh_ref[0]).astype(BF16)

    return pl.pallas_call(
        body, name="resid_norm2", grid=(t // TM,),
        in_specs=[_row(D), _row(D), _per_ex(D), _vec(D), _per_ex(D), _per_ex(D)],
        out_specs=[_row(D), _row(D)],
        out_shape=[jax.ShapeDtypeStruct((t, D), F32), jax.ShapeDtypeStruct((t, D), BF16)],
        compiler_params=_cp("parallel"),
    )(x, mix, g1, g, sc, sh)


def _sigmoid(v):
    return 1.0 / (1.0 + jnp.exp(-v))


def _final(x2, f, g2, g_fin, target):
    t = x2.shape[0]
    nb = t // S
    tpb = S // TM

    def body(x2_ref, f_ref, g2_ref, g_ref, t_ref, dx3_ref, df_ref, loss_ref, dgf_ref, dg2_ref):
        i = pl.program_id(0)
        fv = f_ref[...].astype(F32)
        x3 = x2_ref[...] + g2_ref[0] * fv
        r = _rms(x3)
        n = x3 * r
        err = n * g_ref[...] - t_ref[...]
        _acc_first(i, loss_ref, _colsum(err * err))
        dy = err * (1.0 / D)
        _acc_first(i, dgf_ref, _colsum(dy * n))
        dx3 = _rms_bwd(n, r, dy * g_ref[...])
        dx3_ref[...] = dx3
        _acc_first(i, dg2_ref, _colsum(dx3 * fv), every=tpb)
        df_ref[...] = (dx3 * g2_ref[0]).astype(BF16)

    return pl.pallas_call(
        body, name="final", grid=(t // TM,),
        in_specs=[_row(D), _row(D), _per_ex(D), _vec(D), _row(D)],
        out_specs=[_row(D), _row(D), _vec(D), _vec(D), _per_ex(D)],
        out_shape=[jax.ShapeDtypeStruct((t, D), F32), jax.ShapeDtypeStruct((t, D), BF16),
                   jax.ShapeDtypeStruct((1, D), F32), jax.ShapeDtypeStruct((1, D), F32),
                   jax.ShapeDtypeStruct((nb, 1, D), F32)],
        compiler_params=_cp("arbitrary"),
    )(x2, f, g2, g_fin, target)


def _norm_bwd(xin, dh, dres, g, sc, gate=None):
    t = xin.shape[0]
    nb = t // S
    tpb = S // TM
    gated = gate is not None

    def body(*refs):
        if gated:
            x_ref, dh_ref, dres_ref, g_ref, sc_ref, mix_ref, g1_ref, dx_ref, dsh_ref, dsc_ref, dg_ref, dg1_ref, dmix_ref = refs
        else:
            x_ref, dh_ref, dres_ref, g_ref, sc_ref, dx_ref, dsh_ref, dsc_ref, dg_ref = refs
        i = pl.program_id(0)
        xv, dhv = x_ref[...], dh_ref[...].astype(F32)
        r = _rms(xv)
        n = xv * r
        gv = g_ref[...]
        _acc_first(i, dsh_ref, _colsum(dhv), every=tpb)
        _acc_first(i, dsc_ref, _colsum(dhv * (n * gv)), every=tpb)
        dng = dhv * (1.0 + sc_ref[0])
        _acc_first(i, dg_ref, _colsum(dng * n))
        dx = dres_ref[...] + _rms_bwd(n, r, dng * gv)
        dx_ref[...] = dx
        if gated:
            _acc_first(i, dg1_ref, _colsum(dx * mix_ref[...].astype(F32)), every=tpb)
            dmix_ref[...] = (dx * g1_ref[0]).astype(BF16)

    in_specs = [_row(D), _row(D), _row(D), _vec(D), _per_ex(D)]
    out_specs = [_row(D), _per_ex(D), _per_ex(D), _vec(D)]
    out_shape = [jax.ShapeDtypeStruct((t, D), F32), jax.ShapeDtypeStruct((nb, 1, D), F32),
                 jax.ShapeDtypeStruct((nb, 1, D), F32), jax.ShapeDtypeStruct((1, D), F32)]
    args = [xin, dh, dres, g, sc]
    if gated:
        in_specs += [_row(D), _per_ex(D)]
        out_specs += [_per_ex(D), _row(D)]
        out_shape += [jax.ShapeDtypeStruct((nb, 1, D), F32), jax.ShapeDtypeStruct((t, D), BF16)]
        args += list(gate)
    return pl.pallas_call(
        body, name="norm2_bwd" if gated else "norm1_bwd", grid=(t // TM,),
        in_specs=in_specs, out_specs=out_specs, out_shape=out_shape,
        compiler_params=_cp("arbitrary"),
    )(*args)


TQ = 256
TB = 512


def _mla_fwd(q, k, v):
    t = q.shape[0]
    nb = t // S

    def body(q_ref, k_ref, v_ref, o_ref, lse_ref):
        causal = lax.broadcasted_iota(jnp.int32, (TB, TB), 0) >= lax.broadcasted_iota(jnp.int32, (TB, TB), 1)
        heads = [slice(h * HP, (h + 1) * HP) for h in range(2)]
        for i in range(S // TB):
            ri, past = slice(i * TB, (i + 1) * TB), slice(0, i * TB)
            qhs = [q_ref[ri, sl] for sl in heads]
            sd = [jnp.where(causal, _dot(qh, k_ref[ri, sl], ((1,), (1,))) * SCALE_B, NEG) for qh, sl in zip(qhs, heads)]
            ms = [jnp.max(s, axis=-1, keepdims=True) for s in sd]
            if i:
                so = [_dot(qh, k_ref[past, sl], ((1,), (1,))) * SCALE_B for qh, sl in zip(qhs, heads)]
                ms = [jnp.maximum(m, jnp.max(s, axis=-1, keepdims=True)) for m, s in zip(ms, so)]
            pd = [jnp.exp(s - m) for s, m in zip(sd, ms)]
            ls = [jnp.sum(p, axis=-1, keepdims=True) for p in pd]
            acc = [_dot(p.astype(BF16), v_ref[ri, sl], ((1,), (0,))) for p, sl in zip(pd, heads)]
            if i:
                po = [jnp.exp(s - m) for s, m in zip(so, ms)]
                ls = [l + jnp.sum(p, axis=-1, keepdims=True) for l, p in zip(ls, po)]
                acc = [a + _dot(p.astype(BF16), v_ref[past, sl], ((1,), (0,))) for a, p, sl in zip(acc, po, heads)]
            o_ref[ri, :] = acc[0] / ls[0] + acc[1] / ls[1]
            for sl, m, l in zip(heads, ms, ls):
                lse_ref[ri, sl] = jnp.broadcast_to(m + jnp.log(l), (TB, HP))

    wide2 = pl.BlockSpec((S, 2 * HP), lambda b, p: (b, p))
    return pl.pallas_call(
        body, name="mla_fwd", grid=(nb, H // 2),
        in_specs=[wide2, wide2, wide2],
        out_specs=[pl.BlockSpec((S, HP), lambda b, p: (b, p)), wide2],
        out_shape=[jax.ShapeDtypeStruct((t, H * VDIM), F32), jax.ShapeDtypeStruct((t, H * HP), F32)],
        compiler_params=_cp("parallel", "parallel"),
    )(q, k, v)


def _mla_bwd(q, k, v, o, do, lse):
    t = q.shape[0]
    nb = t // S
    nq = S // TQ

    def body(q_ref, k_ref, v_ref, o_ref, do_ref, lse_ref, dq_ref, dk_ref, dv_ref):
        lane = lax.broadcasted_iota(jnp.int32, (TB, HP), 1)
        causal = lax.broadcasted_iota(jnp.int32, (TB, TB), 0) >= lax.broadcasted_iota(jnp.int32, (TB, TB), 1)
        heads = [slice(h * HP, (h + 1) * HP) for h in range(2)]
        nblk = S // TB
        for i in reversed(range(nblk)):
            ri, past = slice(i * TB, (i + 1) * TB), slice(0, i * TB)
            dov = do_ref[ri, :]
            prod = dov * o_ref[ri, :]
            dob = dov.astype(BF16)
            deltas = [jnp.sum(jnp.where((lane < VDIM) if h == 0 else (lane >= VDIM), prod, 0.0), axis=-1, keepdims=True)
                      for h in range(2)]
            qhs = [q_ref[ri, sl] for sl in heads]
            lses = [lse_ref[ri, sl][:, :1] for sl in heads]
            for rows, diagonal in ((ri, True), (past, False)):
                if rows.stop == rows.start:
                    continue
                ps = [jnp.exp(_dot(qh, k_ref[rows, sl], ((1,), (1,))) * SCALE_B - lse) for qh, sl, lse in zip(qhs, heads, lses)]
                if diagonal:
                    ps = [jnp.where(causal, p, 0.0) for p in ps]
                dps = [_dot(dob, v_ref[rows, sl], ((1,), (1,))) for sl in heads]
                dss = [(p * (dp - delta) * SCALE_B).astype(BF16) for p, dp, delta in zip(ps, dps, deltas)]
                for sl, qh, p, ds in zip(heads, qhs, ps, dss):
                    dq = _dot(ds, k_ref[rows, sl], ((1,), (0,)))
                    dk = _dot(ds, qh, ((0,), (0,)))
                    dv = _dot(p.astype(BF16), dob, ((0,), (0,)))
                    if diagonal:
                        dq_ref[ri, sl] = dq
                    else:
                        dq_ref[ri, sl] += dq
                    if i == nblk - 1:
                        dk_ref[rows, sl] = dk
                        dv_ref[rows, sl] = dv
                    else:
                        dk_ref[rows, sl] += dk
                        dv_ref[rows, sl] += dv

    wide2 = pl.BlockSpec((S, 2 * HP), lambda b, p: (b, p))
    pair = pl.BlockSpec((S, HP), lambda b, p: (b, p))
    return pl.pallas_call(
        body, name="mla_bwd", grid=(nb, H // 2),
        in_specs=[wide2, wide2, wide2, pair, pair, wide2],
        out_specs=[wide2, wide2, wide2],
        out_shape=[jax.ShapeDtypeStruct((t, H * HP), F32)] * 3,
        compiler_params=_cp("parallel", "parallel"),
    )(q, k, v, o, do, lse)


def _t5_bucket(dist):
    max_exact = N_BUCKETS // 2
    d = np.maximum(dist, 1).astype(np.float64)
    large = max_exact + (np.log(d / max_exact) / np.log(MAX_DISTANCE / max_exact) * (N_BUCKETS - max_exact)).astype(np.int64)
    large = np.minimum(large, N_BUCKETS - 1)
    return np.where(dist < max_exact, dist, large).astype(np.int32)


def _band_geometry():
    a = np.arange(BLK)[:, None]
    bk = np.arange(2 * BLK)[None, :]
    steps = BLK + a - bk
    valid = (steps >= 0) & (steps <= BLK)
    buckets = np.stack([_t5_bucket(np.clip(steps, 0, BLK) * d) for d in DILATIONS])
    return buckets, valid


def _band_bias(rel_bias):
    buckets, valid = _band_geometry()
    onehot = (jnp.asarray(buckets)[..., None] == jnp.arange(N_BUCKETS)).astype(F32)
    bias = jnp.einsum("rqkn,nh->rhqk", onehot, rel_bias, precision=lax.Precision.HIGHEST)
    bias = jnp.where(jnp.asarray(valid)[None, None], bias, NEG)
    return bias.reshape(3, H // 2, 2 * BLK, 2 * BLK)


def _dil_items():
    items = []
    for r, d in enumerate(DILATIONS):
        for res in range(d):
            for blk in range(S // d // BLK):
                items.append((r, d, blk * BLK * d + res, blk > 0))
    return items


GROUP = 4


def _strided(start, d):
    return pl.ds(start, BLK) if d == 1 else pl.ds(start, BLK, stride=d)


def _stack_heads(tile, own):
    return jnp.where(own, jnp.concatenate([tile, tile], axis=0), 0.0).astype(BF16)


def _own_lanes():
    row = lax.broadcasted_iota(jnp.int32, (2 * BLK, HP), 0)
    lane = lax.broadcasted_iota(jnp.int32, (2 * BLK, HP), 1)
    return (lane < E_A) == (row < BLK)


def _dil_fwd(proj, biasm):
    t = proj.shape[0]
    nb = t // S

    def body(q_ref, k_ref, v_ref, b_ref, o_ref, lse_ref, ob_ref, lb_ref):
        lane = lax.broadcasted_iota(jnp.int32, (BLK, HP), 1)
        own = _own_lanes()
        items = _dil_items()
        for g in range(0, len(items), GROUP):
            grp = items[g:g + GROUP]
            ss, vts = [], []
            for r, d, start, has_prev in grp:
                cur = _strided(start, d)
                rows = [_strided(start - BLK * d, d), cur] if has_prev else [cur]
                q2 = _stack_heads(q_ref[cur, :], own)
                kt = jnp.concatenate([k_ref[x, :] for x in rows], axis=0).astype(BF16)
                vts.append(jnp.concatenate([v_ref[x, :] for x in rows], axis=0).astype(BF16))
                bias = b_ref[r, 0] if has_prev else b_ref[r, 0, :, BLK:]
                ss.append(_dot(q2, kt, ((1,), (1,))) * SCALE_A + bias)
            ms = [jnp.max(s, axis=-1, keepdims=True) for s in ss]
            ps = [jnp.exp(s - m) for s, m in zip(ss, ms)]
            ls = [jnp.sum(p, axis=-1, keepdims=True) for p in ps]
            for (r, d, start, _), p, vt, m, l in zip(grp, ps, vts, ms, ls):
                cur = _strided(start, d)
                o2 = _dot(p.astype(BF16), vt, ((1,), (0,))) / l
                lse2 = m + jnp.log(l)
                ob_ref[r, cur, :] = jnp.where(lane < E_A, o2[:BLK], o2[BLK:])
                lb_ref[r, cur, :] = jnp.where(lane < E_A, lse2[:BLK], lse2[BLK:])

        def merge(c, _):
            rows = pl.ds(pl.multiple_of(c * TQ, TQ), TQ)
            l0, l1, l2 = lb_ref[0, rows, :], lb_ref[1, rows, :], lb_ref[2, rows, :]
            m = jnp.maximum(jnp.maximum(l0, l1), l2)
            e0, e1, e2 = jnp.exp(l0 - m), jnp.exp(l1 - m), jnp.exp(l2 - m)
            tot = e0 + e1 + e2
            o_ref[rows, :] = (e0 * ob_ref[0, rows, :] + e1 * ob_ref[1, rows, :] + e2 * ob_ref[2, rows, :]) / tot
            lse_ref[rows, :] = m + jnp.log(tot)
            return 0

        lax.fori_loop(0, S // TQ, merge, 0)

    npair = H // 2
    return pl.pallas_call(
        body, name="dil_fwd", grid=(nb, npair),
        in_specs=[pl.BlockSpec((S, HP), lambda b, p: (b, p)), pl.BlockSpec((S, HP), lambda b, p: (b, npair + p)),
                  pl.BlockSpec((S, HP), lambda b, p: (b, 2 * npair + p)),
                  pl.BlockSpec((3, 1, 2 * BLK, 2 * BLK), lambda b, p: (0, p, 0, 0))],
        out_specs=[pl.BlockSpec((S, HP), lambda b, p: (b, p))] * 2,
        out_shape=[jax.ShapeDtypeStruct((t, D_A), F32)] * 2,
        scratch_shapes=[pltpu.VMEM((3, S, HP), F32), pltpu.VMEM((3, S, HP), F32)],
        compiler_params=_cp("parallel", "parallel"),
    )(proj, proj, proj, biasm)


def _dil_bwd(proj, biasm, o, do, lse):
    t = proj.shape[0]
    nb = t // S

    def body(q_ref, k_ref, v_ref, b_ref, o_ref, do_ref, lse_ref, dq_out, dk_out, dv_out, ds_ref, dq_ref, dk_ref, dv_ref):
        dq_ref[...] = jnp.zeros_like(dq_ref)
        dk_ref[...] = jnp.zeros_like(dk_ref)
        dv_ref[...] = jnp.zeros_like(dv_ref)
        ds_ref[...] = jnp.zeros_like(ds_ref)
        lane = lax.broadcasted_iota(jnp.int32, (BLK, HP), 1)
        own = _own_lanes()
        items = _dil_items()
        for g in range(0, len(items), GROUP):
            grp = items[g:g + GROUP]
            q2s, kts, do2s, ss, dps, lse2s, delta2s = [], [], [], [], [], [], []
            for r, d, start, has_prev in grp:
                cur = _strided(start, d)
                rows = [_strided(start - BLK * d, d), cur] if has_prev else [cur]
                q2 = _stack_heads(q_ref[cur, :], own)
                kt = jnp.concatenate([k_ref[x, :] for x in rows], axis=0).astype(BF16)
                vt = jnp.concatenate([v_ref[x, :] for x in rows], axis=0).astype(BF16)
                dot_ = do_ref[cur, :]
                prod = dot_ * o_ref[cur, :]
                lset = lse_ref[cur, :]
                do2 = _stack_heads(dot_, own)
                bias = b_ref[r, 0] if has_prev else b_ref[r, 0, :, BLK:]
                ss.append(_dot(q2, kt, ((1,), (1,))) * SCALE_A + bias)
                dps.append(_dot(do2, vt, ((1,), (1,))))
                lse2s.append(jnp.concatenate([lset[:, :1], lset[:, E_A:E_A + 1]], axis=0))
                delta2s.append(jnp.concatenate([jnp.sum(jnp.where(lane < E_A, prod, 0.0), axis=-1, keepdims=True),
                                                jnp.sum(jnp.where(lane >= E_A, prod, 0.0), axis=-1, keepdims=True)], axis=0))
                q2s.append(q2)
                kts.append(kt)
                do2s.append(do2)
            ps = [jnp.exp(s - lse2) for s, lse2 in zip(ss, lse2s)]
            dls = [p * (dp - delta2) for p, dp, delta2 in zip(ps, dps, delta2s)]
            for (r, d, start, has_prev), q2, kt, do2, p, dl in zip(grp, q2s, kts, do2s, ps, dls):
                cur = _strided(start, d)
                dsb = (dl * SCALE_A).astype(BF16)
                dq2 = _dot(dsb, kt, ((1,), (0,)))
                dkt = _dot(dsb, q2, ((0,), (0,)))
                dvt = _dot(p.astype(BF16), do2, ((0,), (0,)))
                dq_ref[cur, :] += jnp.where(lane < E_A, dq2[:BLK], dq2[BLK:])
                if has_prev:
                    prev = _strided(start - BLK * d, d)
                    ds_ref[0, r, 0] += dl
                    dk_ref[prev, :] += dkt[:BLK]
                    dv_ref[prev, :] += dvt[:BLK]
                    dk_ref[cur, :] += dkt[BLK:]
                    dv_ref[cur, :] += dvt[BLK:]
                else:
                    ds_ref[0, r, 0, :, BLK:] += dl
                    dk_ref[cur, :] += dkt
                    dv_ref[cur, :] += dvt
        dq_out[...] = dq_ref[...].astype(BF16)
        dk_out[...] = dk_ref[...].astype(BF16)
        dv_out[...] = dv_ref[...].astype(BF16)

    npair = H // 2
    pair = pl.BlockSpec((S, HP), lambda b, p: (b, p))
    return pl.pallas_call(
        body, name="dil_bwd", grid=(nb, npair),
        in_specs=[pair, pl.BlockSpec((S, HP), lambda b, p: (b, npair + p)),
                  pl.BlockSpec((S, HP), lambda b, p: (b, 2 * npair + p)),
                  pl.BlockSpec((3, 1, 2 * BLK, 2 * BLK), lambda b, p: (0, p, 0, 0)), pair, pair, pair],
        out_specs=[pair, pair, pair, pl.BlockSpec((1, 3, 1, 2 * BLK, 2 * BLK), lambda b, p: (b, 0, p, 0, 0))],
        out_shape=[jax.ShapeDtypeStruct((t, D_A), BF16)] * 3 + [jax.ShapeDtypeStruct((nb, 3, npair, 2 * BLK, 2 * BLK), F32)],
        scratch_shapes=[pltpu.VMEM((S, HP), F32)] * 3,
        compiler_params=_cp("parallel", "parallel"),
    )(proj, proj, proj, biasm, o, do, lse)


def _rel_bias_grad(dlogits):
    nb = dlogits.shape[0]
    buckets, _ = _band_geometry()
    kk = 3 * BLK * 2 * BLK
    dl = jnp.transpose(dlogits.reshape(nb, 3, H, BLK, 2 * BLK), (0, 2, 1, 3, 4)).reshape(nb, H, kk)
    bk = jnp.asarray(buckets.reshape(1, kk))
    tk = kk // 12

    def body(dl_ref, bk_ref, o_ref):
        j = pl.program_id(0)
        onehot = (bk_ref[...] == lax.broadcasted_iota(jnp.int32, (N_BUCKETS, tk), 0)).astype(F32)
        tot = dl_ref[0]
        for b in range(1, nb):
            tot = tot + dl_ref[b]
        part = lax.dot_general(onehot, tot, ((((1,), (1,))), ((), ())), preferred_element_type=F32,
                               precision=lax.Precision.HIGHEST)
        _acc_first(j, o_ref, part)

    return pl.pallas_call(
        body, name="rel_bias_grad", grid=(kk // tk,),
        in_specs=[pl.BlockSpec((nb, H, tk), lambda j: (0, 0, j)), pl.BlockSpec((1, tk), lambda j: (0, j))],
        out_specs=pl.BlockSpec((N_BUCKETS, H), lambda j: (0, 0)),
        out_shape=jax.ShapeDtypeStruct((N_BUCKETS, H), F32),
        compiler_params=_cp("arbitrary"),
    )(dl, bk)


def _mesh_place():
    x, y, c = lax.axis_index("x"), lax.axis_index("y"), lax.axis_index("c")
    return x, y, c


def _peer(k):
    x, y, c = _mesh_place()
    px = 1 - x if k & 4 else x
    py = 1 - y if k & 2 else y
    pc = 1 - c if k & 1 else c
    return (px, py, pc), 4 * px + 2 * py + pc


ANY = pl.BlockSpec(memory_space=pl.ANY)


def _exchange(arrays, gathers, name, after=None):
    n_arr = len(arrays)

    def body(*refs):
        ins, outs = refs[:n_arr], refs[n_arr + 1:2 * n_arr + 1]
        send, recv, loc = refs[2 * n_arr + 1:]
        x, y, c = _mesh_place()
        me = 4 * x + 2 * y + c
        local = [pltpu.make_async_copy(ins[a] if gathers[a] else ins[a].at[me], outs[a].at[me], loc.at[a])
                 for a in range(n_arr)]
        remote = _peer_copies(ins, outs, send, recv, gathers)
        for cp in local:
            cp.start()
        for put, _ in remote:
            put.start()
        for cp in local:
            cp.wait()
        for put, got in remote:
            put.wait_send()
            got.wait_recv()

    return pl.pallas_call(
        body, name=name,
        in_specs=[ANY] * (n_arr + 1), out_specs=[ANY] * n_arr,
        out_shape=[jax.ShapeDtypeStruct(((N_DEV,) if g else ()) + a.shape, a.dtype) for a, g in zip(arrays, gathers)],
        scratch_shapes=[pltpu.SemaphoreType.DMA((n_arr * (N_DEV - 1),)), pltpu.SemaphoreType.DMA((n_arr * (N_DEV - 1),)),
                        pltpu.SemaphoreType.DMA((n_arr,))],
        compiler_params=pltpu.CompilerParams(has_side_effects=True),
    )(*arrays, arrays[0] if after is None else after)


def _gather_two_level(arrays, name):
    n_arr = len(arrays)
    per = N_DEV - 1

    def body(*refs):
        ins, outs = refs[:n_arr], refs[n_arr:2 * n_arr]
        send, recv, loc = refs[2 * n_arr:]
        x, y, c = _mesh_place()
        me, sibling = (x, y, c), (x, y, 1 - c)
        chips = [(1 - x, y), (x, 1 - y), (1 - x, 1 - y)]

        def block(a, place):
            px, py, pc = place
            return outs[a].at[4 * px + 2 * py + pc]

        def copy(a, k, place, to, src=None):
            dst = block(a, place)
            return pltpu.make_async_remote_copy(dst if src is None else src, dst, send.at[a * per + k], recv.at[a * per + k],
                                                device_id=to, device_id_type=pl.DeviceIdType.MESH)

        local = [pltpu.make_async_copy(ins[a], block(a, me), loc.at[a]) for a in range(n_arr)]
        for cp in local:
            cp.start()
        first = []
        for a in range(n_arr):
            first.append(copy(a, 0, me, sibling, src=ins[a]))
            first += [copy(a, 1 + j, me, (*chip, c), src=ins[a]) for j, chip in enumerate(chips)]
        for cp in first:
            cp.start()
        passed = []
        for j, chip in enumerate(chips):
            for a in range(n_arr):
                copy(a, 1 + j, (*chip, c), me).wait_recv()
                passed.append(copy(a, 4 + j, (*chip, c), sibling))
                passed[-1].start()
        for a in range(n_arr):
            copy(a, 0, sibling, me).wait_recv()
            for j, chip in enumerate(chips):
                copy(a, 4 + j, (*chip, 1 - c), me).wait_recv()
        for cp in first + passed:
            cp.wait_send()
        for cp in local:
            cp.wait()

    return pl.pallas_call(
        body, name=name,
        in_specs=[ANY] * n_arr, out_specs=[ANY] * n_arr,
        out_shape=[jax.ShapeDtypeStruct((N_DEV,) + a.shape, a.dtype) for a in arrays],
        scratch_shapes=[pltpu.SemaphoreType.DMA((n_arr * per,)), pltpu.SemaphoreType.DMA((n_arr * per,)),
                        pltpu.SemaphoreType.DMA((n_arr,))],
        compiler_params=pltpu.CompilerParams(has_side_effects=True),
    )(*arrays)


HBM = pl.BlockSpec(memory_space=pltpu.HBM)
SEM = pl.BlockSpec(memory_space=pltpu.SEMAPHORE)
DATAFLOW = pltpu.SideEffectType.DATAFLOW_SIDE_EFFECTING


def _own_block_in_place(block, me):
    land = lax.empty((N_DEV,) + block.shape, block.dtype)
    return lax.dynamic_update_slice(land, block[None], (me,) + (0,) * block.ndim)


def _peer_copies(srcs, lands, send, recv, gathers):
    x, y, c = _mesh_place()
    me = 4 * x + 2 * y + c
    out = []
    for a, (src, land) in enumerate(zip(srcs, lands)):
        for k in range(1, N_DEV):
            dev, idx = _peer(k)
            sem = a * (N_DEV - 1) + k - 1
            mine = src if gathers[a] else src.at[idx]
            put = pltpu.make_async_remote_copy(mine, land.at[me], send.at[sem], recv.at[sem],
                                               device_id=dev, device_id_type=pl.DeviceIdType.MESH)
            got = pltpu.make_async_remote_copy(mine, land.at[idx], send.at[sem], recv.at[sem],
                                               device_id=dev, device_id_type=pl.DeviceIdType.MESH)
            out.append((put, got))
    return out


def _exchange_start(srcs, lands, gather, after, name):
    n = len(srcs)

    def body(*refs):
        srcs_, lands_, send, recv = refs[:n], refs[n:2 * n], refs[2 * n + 1], refs[2 * n + 2]
        for put, _ in _peer_copies(srcs_, lands_, send, recv, gather):
            put.start()
        refs[-1][...] = jnp.zeros_like(refs[-1])

    nsem = n * (N_DEV - 1)
    thru = [pltpu.HBM(a.shape, a.dtype) for a in list(srcs) + list(lands)]
    res = pl.pallas_call(
        body, name=name,
        out_shape=(pltpu.SemaphoreType.DMA((nsem,)), pltpu.SemaphoreType.DMA((nsem,)), *thru, jax.ShapeDtypeStruct((8, 128), F32)),
        in_specs=[HBM] * (2 * n) + [ANY],
        out_specs=(SEM, SEM, *([HBM] * (2 * n)), pl.BlockSpec(memory_space=pltpu.VMEM)),
        input_output_aliases={i: 2 + i for i in range(2 * n)},
        compiler_params=pltpu.CompilerParams(has_side_effects=DATAFLOW),
    )(*[pltpu.with_memory_space_constraint(a, pltpu.HBM) for a in list(srcs) + list(lands)], after)
    return res[0], res[1], list(res[2:2 + n]), list(res[2 + n:2 + 2 * n]), res[-1]


def _exchange_wait(send, recv, srcs, lands, gather, after, name):
    n = len(srcs)

    def body(*refs):
        srcs_, lands_, send_, recv_ = refs[:n], refs[n:2 * n], refs[2 * n], refs[2 * n + 1]
        for put, got in _peer_copies(srcs_, lands_, send_, recv_, gather):
            put.wait_send()
            got.wait_recv()

    thru = [pltpu.HBM(a.shape, a.dtype) for a in list(srcs) + list(lands)]
    res = pl.pallas_call(
        body, name=name, out_shape=tuple(thru),
        in_specs=[HBM] * (2 * n) + [SEM, SEM, ANY], out_specs=tuple([HBM] * (2 * n)),
        input_output_aliases={i: i for i in range(2 * n)},
        compiler_params=pltpu.CompilerParams(has_side_effects=DATAFLOW),
    )(*srcs, *lands, send, recv, after)
    return list(res[n:])


def _silu_rows(c):
    def body(c_ref, o_ref):
        v = c_ref[...]
        o_ref[...] = v * _sigmoid(v)

    return pl.pallas_call(body, name="cond", out_shape=jax.ShapeDtypeStruct(c.shape, F32))(c)


def _mod_slab(cond_all, w_ada, b_slab):
    def body(c_ref, w_ref, b_ref, o_ref):
        o_ref[...] = _dot(c_ref[...].astype(BF16), w_ref[0].astype(BF16), ((1,), (0,))) + b_ref[...]

    return pl.pallas_call(body, name="mod_slab",
                          out_shape=jax.ShapeDtypeStruct((cond_all.shape[0], w_ada.shape[2]), F32),
                          compiler_params=pltpu.CompilerParams(vmem_limit_bytes=VMEM_LIMIT))(cond_all, w_ada, b_slab)


def _ada_grad(cond_all, dmod_cols):
    def body(c_ref, d_ref, o_ref):
        o_ref[...] = _dot(c_ref[...].astype(BF16), d_ref[...].astype(BF16), ((0,), (0,)))

    return pl.pallas_call(body, name="ada_grad",
                          out_shape=jax.ShapeDtypeStruct((cond_all.shape[1], dmod_cols.shape[1]), F32),
                          compiler_params=pltpu.CompilerParams(vmem_limit_bytes=VMEM_LIMIT))(cond_all, dmod_cols)


def _adam_math(g, w, m, v):
    m2 = B1 * m + (1.0 - B1) * g
    v2 = B2 * v + (1.0 - B2) * (g * g)
    m_hat = m2 / (1.0 - B1 ** STEP)
    v_hat = v2 / (1.0 - B2 ** STEP)
    return -LR * (m_hat / (jnp.sqrt(v_hat) + ADAM_EPS) + WD * w), m2, v2


def _adamw(parts, w, m, v, name):
    n, rows, cols = parts.shape
    tr = _pick(rows, (128, 96, 64, 32, 16, 8))

    def body(p_ref, w_ref, m_ref, v_ref, g_ref, d_ref, m2_ref, v2_ref):
        g = p_ref[0].astype(F32)
        for s in range(1, n):
            g = g + p_ref[s].astype(F32)
        g_ref[0] = g
        d_ref[0], m2_ref[0], v2_ref[0] = _adam_math(g, w_ref[0], m_ref[0], v_ref[0])

    blk = pl.BlockSpec((1, tr, cols), lambda i: (0, i, 0))
    return pl.pallas_call(
        body, name=name, grid=(rows // tr,),
        in_specs=[pl.BlockSpec((n, tr, cols), lambda i: (0, i, 0)), blk, blk, blk],
        out_specs=[blk] * 4, out_shape=[jax.ShapeDtypeStruct((1, rows, cols), F32)] * 4,
        compiler_params=_cp("parallel"),
    )(parts, w, m, v)


ROW_PARAMS = (("g_norm1", D), ("g_cq", Q_LORA), ("g_ckv", KV_LORA), ("g_out_a", D_A), ("g_out_b", D_A), ("g_norm2", D),
              ("g_final", D))
LOSS_ROW = N_MOD + len(ROW_PARAMS)
PAY_ROWS = 16
NCOL = N_MOD * D // N_DEV


def _pack_small(dmods, rows, loss_cols):
    nb = dmods[0].shape[0]
    nrow = len(ROW_PARAMS)

    def body(*refs):
        dm, rw, loss_ref, pay_ref, blk_ref = refs[:N_MOD], refs[N_MOD:N_MOD + nrow], refs[N_MOD + nrow], refs[-2], refs[-1]
        pay_ref[...] = jnp.zeros_like(pay_ref)
        for k in range(N_MOD):
            tot = dm[k][0]
            for b in range(1, nb):
                tot = tot + dm[k][b]
            pay_ref[k:k + 1, :] = tot
        for i, (_, n) in enumerate(ROW_PARAMS):
            pay_ref[N_MOD + i:N_MOD + i + 1, :n] = rw[i][...]
        pay_ref[LOSS_ROW:LOSS_ROW + 1, :] = loss_ref[...]
        for j in range(N_DEV):
            done = 0
            while done < NCOL:
                seg, off = divmod(j * NCOL + done, D)
                ln = min(NCOL - done, D - off)
                for b in range(nb):
                    blk_ref[j, b:b + 1, done:done + ln] = dm[seg][b][:, off:off + ln]
                done += ln

    return pl.pallas_call(
        body, name="pack_small",
        out_shape=[jax.ShapeDtypeStruct((PAY_ROWS, D), F32), jax.ShapeDtypeStruct((N_DEV, nb, NCOL), F32)],
    )(*dmods, *rows, loss_cols)


def _small_update(pay, rel, ws, ms, vs):
    n_par = len(ws)

    def body(*refs):
        pay_ref, rel_ref = refs[:2]
        w_refs, m_refs, v_refs = (refs[2 + s * n_par:2 + (s + 1) * n_par] for s in range(3))
        outs, loss_ref = refs[2 + 3 * n_par:-1], refs[-1]
        tot, rtot = pay_ref[0], rel_ref[0]
        for s in range(1, N_DEV):
            tot, rtot = tot + pay_ref[s], rtot + rel_ref[s]

        def update(p, g, sl):
            outs[4 * p][:, sl] = g
            outs[4 * p + 1][:, sl], outs[4 * p + 2][:, sl], outs[4 * p + 3][:, sl] = _adam_math(
                g, w_refs[p][:, sl], m_refs[p][:, sl], v_refs[p][:, sl])

        for k in range(N_MOD):
            update(0, tot[k:k + 1, :], slice(k * D, (k + 1) * D))
        for i, (_, n) in enumerate(ROW_PARAMS):
            update(1 + i, tot[N_MOD + i:N_MOD + i + 1, :n], slice(0, n))
        update(n_par - 1, rtot, slice(0, H))
        loss_ref[...] = jnp.broadcast_to((0.5 / D) * jnp.sum(tot[LOSS_ROW:LOSS_ROW + 1, :]), loss_ref.shape)

    shapes = [jax.ShapeDtypeStruct(w.shape, F32) for w in ws for _ in range(4)]
    res = pl.pallas_call(
        body, name="small_update", out_shape=shapes + [jax.ShapeDtypeStruct((8, 128), F32)],
    )(pay, rel, *ws, *ms, *vs)
    return [tuple(res[4 * p:4 * p + 4]) for p in range(n_par)], res[-1]


def _cols_from_blocks(g):
    return jnp.transpose(g, (1, 0, 2)).reshape(g.shape[1], N_DEV * g.shape[2])


def _cols_to_blocks(w):
    r, c = w.shape
    return jnp.transpose(w.reshape(r, N_DEV, c // N_DEV), (1, 0, 2))


def _pad_w_in(w):
    z = jnp.zeros((w.shape[0], NOPE), w.dtype)
    return jnp.concatenate([w[:, :P_IN - ROPE], z, w[:, P_IN - ROPE:], z[:, :HP - NOPE - ROPE]], axis=1)


def _unpad_w_in(g):
    k0 = P_IN - ROPE + NOPE
    return jnp.concatenate([g[:, :P_IN - ROPE], g[:, k0:k0 + ROPE]], axis=1)


def _pad_w_uq(w):
    w3 = w.reshape(Q_LORA, H, NOPE + ROPE)
    return jnp.pad(w3, ((0, 0), (0, 0), (0, HP - NOPE - ROPE))).reshape(Q_LORA, H * HP)


def _unpad_w_uq(g):
    return g.reshape(Q_LORA, H, HP)[:, :, :NOPE + ROPE].reshape(Q_LORA, H * (NOPE + ROPE))


def _split_w_ukv(w):
    w4 = w.reshape(KV_LORA, H // 2, 2, HP)
    z = jnp.zeros((KV_LORA, H // 2, NOPE), w.dtype)
    kn, vv = w4[..., :NOPE], w4[..., NOPE:]
    w_k = jnp.stack([jnp.concatenate([kn[:, :, 0], z], -1), jnp.concatenate([kn[:, :, 1], z], -1)], axis=2)
    w_v = jnp.stack([jnp.concatenate([vv[:, :, 0], z], -1), jnp.concatenate([z, vv[:, :, 1]], -1)], axis=2)
    return w_k.reshape(KV_LORA, H * HP), w_v.reshape(KV_LORA, H * HP)


def _join_w_ukv(g_k, g_v):
    gk = g_k.reshape(KV_LORA, H // 2, 2, HP)
    gv = g_v.reshape(KV_LORA, H // 2, 2, HP)
    even = jnp.concatenate([gk[:, :, 0, :NOPE], gv[:, :, 0, :VDIM]], -1)
    odd = jnp.concatenate([gk[:, :, 1, :NOPE], gv[:, :, 1, VDIM:]], -1)
    return jnp.stack([even, odd], axis=2).reshape(KV_LORA, H * HP)


def _rope_tables():
    half = ROPE // 2
    inv = ROPE_THETA ** (-jnp.arange(half, dtype=F32) / half)
    ang = jnp.arange(S, dtype=F32)[:, None] * inv[None, :]
    cos, sin = jnp.cos(ang), jnp.sin(ang)
    ones, zeros = jnp.ones((S, NOPE), F32), jnp.zeros((S, NOPE), F32)
    tail1, tail0 = jnp.ones((S, HP - NOPE - ROPE), F32), jnp.zeros((S, HP - NOPE - ROPE), F32)
    zh = jnp.zeros((S, half), F32)
    c = jnp.concatenate([ones, cos, cos, tail1], axis=1)
    sm = jnp.concatenate([zeros, -sin, zh, tail0], axis=1)
    sp = jnp.concatenate([zeros, zh, sin, tail0], axis=1)
    return c, sm, sp


def _local_step(x, mod, target, g_norm1, w_in_p, g_cq, w_uq_p, g_ckv, w_k, w_v, rel_bias, g_out_a, g_out_b, w_out,
                g_norm2, w_ffn_in, w_ffn_out, g_final, late_weights=None, on_ffn_grads=None, on_last_grads=None):
    nb = x.shape[0] // S
    sh1, sc1, g1, sh2, sc2, g2 = (mod[:, n].reshape(nb, 1, D) for n in range(N_MOD))
    rc, rsm, rsp = _rope_tables()
    biasm = _band_bias(rel_bias)

    h1 = _pre1(x, g_norm1, sc1, sh1)
    proj = _mm_nn(h1, w_in_p, F32, "proj")
    q, k, v, cqn, ckvn = _mla_pre(proj, g_cq, g_ckv, w_uq_p, w_k, w_v, rc, rsm, rsp)
    out_b, lse_b = _mla_fwd(q, k, v)
    out_a, lse_a = _dil_fwd(proj, biasm)
    y = _post_attn(out_a, out_b, g_out_a, g_out_b)
    if late_weights is not None:
        w_out, w_ffn_in, w_ffn_out = late_weights(y)
    mix = _mm_nn(y, w_out, BF16, "mix")
    x2, h2 = _resid_norm2(x, mix, g1, g_norm2, sc2, sh2)
    ffn_g, ffn_u, act = _ffn_in(h2, w_ffn_in)
    f = _mm_nn(act, w_ffn_out, BF16, "ffn_out")
    dx3, df, loss_cols, dg_final, dg2 = _final(x2, f, g2, g_final, target)

    dg_, du_ = _d_act(df, w_ffn_out, ffn_g, ffn_u)
    gw_ffn_out = _mm_tn(act, [df], "gw_ffn_out")
    dh2 = _d_h2(dg_, du_, w_ffn_in)
    gw_ffn_in = _mm_tn(h2, [dg_, du_], "gw_ffn_in")
    dx2, dsh2, dsc2, dg_norm2, dg1, dmix = _norm_bwd(x2, dh2, dx3, g_norm2, sc2, gate=(mix, g1))
    dy = _mm_nt(dmix, w_out, BF16, "d_y")
    gw_out = _mm_tn(y, [dmix], "gw_out")
    if on_ffn_grads is not None:
        g_out_a = g_out_a + on_ffn_grads(gw_ffn_in, gw_ffn_out, gw_out)
    dout_a, dout_b, dg_out_a, dg_out_b = _post_attn_bwd(dy, out_a, out_b, g_out_a, g_out_b)
    dq_b, dk_b, dv_b = _mla_bwd(q, k, v, out_b, dout_b, lse_b)
    dq_a, dk_a, dv_a, dlogits = _dil_bwd(proj, biasm, out_a, dout_a, lse_a)
    g_rel = _rel_bias_grad(dlogits)
    dqr, dkr, dvr, dtail, dg_cq, dg_ckv = _mla_pre_bwd(proj, dq_b, dk_b, dv_b, g_cq, g_ckv, w_uq_p, w_k, w_v, rc, rsm, rsp)
    gw_uq = _mm_tn(cqn, [dqr], "gw_uq")
    gw_k, gw_v = _mm_tn(ckvn, [dkr, dvr], "gw_kv")
    dproj = jnp.concatenate([dq_a, dk_a, dv_a, dtail], axis=1)
    gw_in = _mm_tn(h1, [dproj], "gw_in")
    if on_last_grads is not None:
        started = on_last_grads(dict(w_in=gw_in, w_uq=gw_uq, w_k=gw_k, w_v=gw_v))
    else:
        started = None
    dh1 = _mm_nt(dproj, w_in_p, BF16, "d_h1", after=started)
    grad_x, dsh1, dsc1, dg_norm1 = _norm_bwd(x, dh1, dx2, g_norm1, sc1)

    dmod = [dsh1, dsc1, dg1, dsh2, dsc2, dg2]
    small = dict(g_norm1=dg_norm1, g_cq=dg_cq, g_ckv=dg_ckv, rel_bias=g_rel, g_out_a=dg_out_a, g_out_b=dg_out_b,
                 g_norm2=dg_norm2, g_final=dg_final)
    big = dict(w_in=gw_in, w_uq=gw_uq, w_k=gw_k, w_v=gw_v, w_out=gw_out, w_ffn_in=gw_ffn_in, w_ffn_out=gw_ffn_out)
    return grad_x, dmod, loss_cols, small, big


def kernel(x, c, w_ada, b_ada, g_norm1, w_in, g_cq, w_uq, g_ckv, w_ukv, rel_bias, g_out_a, g_out_b, w_out, g_norm2, w_ffn_in, w_ffn_out, g_final, loss_target, m_w_ada, m_b_ada, m_g_norm1, m_w_in, m_g_cq, m_w_uq, m_g_ckv, m_w_ukv, m_rel_bias, m_g_out_a, m_g_out_b, m_w_out, m_g_norm2, m_w_ffn_in, m_w_ffn_out, m_g_final, v_w_ada, v_b_ada, v_g_norm1, v_w_in, v_g_cq, v_w_uq, v_g_ckv, v_w_ukv, v_rel_bias, v_g_out_a, v_g_out_b, v_w_out, v_g_norm2, v_w_ffn_in, v_w_ffn_out, v_g_final):
    nb = x.shape[0]
    t = nb * S
    xt, tt = x.reshape(t, D), loss_target.reshape(t, D)
    me = 4 * lax.axis_index("x") + 2 * lax.axis_index("y") + lax.axis_index("c")

    early = [w_in[0], w_uq[0], w_ukv[0]]
    gathered = _gather_two_level([_silu_rows(c)] + [s.astype(BF16) for s in early], "gather_weights")
    cond_all = gathered[0].reshape(N_DEV * nb, D)
    w_in_f, w_uq_f, w_ukv_f = (_cols_from_blocks(g) for g in gathered[1:4])
    w_k, w_v = _split_w_ukv(w_ukv_f)

    ncol = N_MOD * D // N_DEV
    b_slab = lax.dynamic_slice(b_ada, (0, me * ncol), (1, ncol))
    slab = _mod_slab(cond_all, w_ada, b_slab)
    (mod_rows,) = _exchange([slab.reshape(N_DEV, nb, ncol)], [False], "scatter_mod")
    mod = jnp.transpose(mod_rows, (1, 0, 2)).reshape(nb, N_MOD, D)

    late = [s.astype(BF16) for s in (w_out[0], w_ffn_in[0], w_ffn_out[0])]
    late_send, late_recv, late_src, late_land, late_token = _exchange_start(
        late, [_own_block_in_place(s, me) for s in late], [True] * 3, mod_rows, "gather_late_start")
    g_norm1_t = g_norm1 + late_token[:1, :1]

    def late_weights(after):
        w_out_g, w_ffn_in_g, w_ffn_out_g = _exchange_wait(late_send, late_recv, late_src, late_land, [True] * 3, after,
                                                          "gather_late_wait")
        return w_out_g.reshape(D, D), _cols_from_blocks(w_ffn_in_g), w_ffn_out_g.reshape(D_FF, D)

    flight = {}

    def start_grads(key, src, name):
        land = [_own_block_in_place(lax.dynamic_index_in_dim(s, me, 0, keepdims=False), me) for s in src]
        send, recv, src, land, token = _exchange_start(src, land, [False] * len(src), src[0], name)
        flight[key] = (send, recv, src, land)
        return token[:1, :1]

    def half_blocks(g):
        return jnp.transpose(g.reshape(D, N_DEV // 2, 2 * D_FF // N_DEV), (1, 0, 2))

    def on_ffn_grads(gw_ffn_in, gw_ffn_out, gw_out):
        return start_grads("ffn", [jnp.concatenate([half_blocks(g) for g in gw_ffn_in], axis=0),
                                   gw_ffn_out.reshape(N_DEV, D_FF // N_DEV, D), gw_out.reshape(N_DEV, D // N_DEV, D)],
                           "exchange_ffn_start")

    def on_last_grads(gw):
        return start_grads("rest", [_cols_to_blocks(_unpad_w_in(gw["w_in"])), _cols_to_blocks(_unpad_w_uq(gw["w_uq"])),
                                    _cols_to_blocks(_join_w_ukv(gw["w_k"], gw["w_v"]))], "exchange_rest_start")

    grad_x, dmod, loss_cols, small, _ = _local_step(
        xt, mod, tt, g_norm1_t, _pad_w_in(w_in_f), g_cq, _pad_w_uq(w_uq_f), g_ckv, w_k, w_v, rel_bias, g_out_a, g_out_b,
        None, g_norm2, None, None, g_final.reshape(1, D), late_weights=late_weights, on_ffn_grads=on_ffn_grads,
        on_last_grads=on_last_grads)

    upd = {}

    def land_and_update(key, names, after, name):
        got = _exchange_wait(*flight[key], [False] * len(names), after, name)
        for n, p in zip(names, got):
            w, m, v = big[n]
            upd[n] = _adamw(p, w, m, v, "adamw_" + n)

    big = dict(w_in=(w_in, m_w_in, v_w_in), w_uq=(w_uq, m_w_uq, v_w_uq), w_ukv=(w_ukv, m_w_ukv, v_w_ukv),
               w_out=(w_out, m_w_out, v_w_out), w_ffn_in=(w_ffn_in, m_w_ffn_in, v_w_ffn_in),
               w_ffn_out=(w_ffn_out, m_w_ffn_out, v_w_ffn_out))
    land_and_update("ffn", ["w_ffn_in", "w_ffn_out", "w_out"], grad_x, "exchange_ffn_wait")
    land_and_update("rest", ["w_in", "w_uq", "w_ukv"], upd["w_out"][0], "exchange_rest_wait")

    mine, dmod_blocks = _pack_small(dmod, [small[n] for n, _ in ROW_PARAMS], loss_cols)
    dmod_cols, pay, rel = _exchange([dmod_blocks, mine, small["rel_bias"]], [False, True, True], "exchange_small",
                                    after=upd["w_ukv"][0])
    g_ada = _ada_grad(cond_all, dmod_cols.reshape(N_DEV * nb, ncol))
    upd["w_ada"] = _adamw(g_ada[None], w_ada, m_w_ada, v_w_ada, "adamw_w_ada")
    row = lambda a: a.reshape(1, D)
    small_names = ["b_ada"] + [n for n, _ in ROW_PARAMS] + ["rel_bias"]
    small_w = [b_ada, g_norm1, g_cq, g_ckv, g_out_a, g_out_b, g_norm2, row(g_final), rel_bias]
    small_m = [m_b_ada, m_g_norm1, m_g_cq, m_g_ckv, m_g_out_a, m_g_out_b, m_g_norm2, row(m_g_final), m_rel_bias]
    small_v = [v_b_ada, v_g_norm1, v_g_cq, v_g_ckv, v_g_out_a, v_g_out_b, v_g_norm2, row(v_g_final), v_rel_bias]
    small_upd, loss8 = _small_update(pay, rel, small_w, small_m, small_v)
    upd.update(zip(small_names, small_upd))

    order = ["w_ada", "b_ada", "g_norm1", "w_in", "g_cq", "w_uq", "g_ckv", "w_ukv", "rel_bias", "g_out_a", "g_out_b",
             "w_out", "g_norm2", "w_ffn_in", "w_ffn_out", "g_final"]
    like = dict(g_final=g_final)
    outs = [loss8[0, 0], grad_x.reshape(x.shape)]
    for part in range(4):
        for n in order:
            val = upd[n][part]
            outs.append(val.reshape(like[n].shape) if n in like else val)
    return tuple(outs)
```

```python
import functools

import numpy as np
import jax
import jax.numpy as jnp
from jax import lax
from jax.experimental import pallas as pl
from jax.experimental.pallas import tpu as pltpu

F32, BF16 = jnp.float32, jnp.bfloat16

N_DEV = 8
D = 1024
S = 2048
H = 8
E_A = 64
D_A = H * E_A
Q_LORA, KV_LORA = 384, 256
NOPE, ROPE, VDIM = 64, 32, 64
HP = 128
P_IN = 3 * D_A + Q_LORA + KV_LORA + ROPE
P_PAD = 3 * D_A + Q_LORA + KV_LORA + HP
TAIL0 = 3 * D_A
TAIL = P_PAD - TAIL0
D_FF = 2816
N_MOD = 6
EPS = 1e-6
NEG = -1e30
BLK = 128
DILATIONS = (1, 4, 16)
N_BUCKETS, MAX_DISTANCE = 32, 2048
ROPE_THETA = 10000.0
SCALE_A = E_A ** -0.5
SCALE_B = (NOPE + ROPE) ** -0.5
B1, B2, LR, ADAM_EPS, WD, STEP = 0.9, 0.999, 0.001, 1e-8, 0.01, 10
VMEM_LIMIT = 56 * 1024 * 1024


def _cp(*sem):
    return pltpu.CompilerParams(dimension_semantics=sem, vmem_limit_bytes=VMEM_LIMIT)


def _pick(n, prefs):
    for p in prefs:
        if n % p == 0:
            return p
    raise ValueError(f"no tile of {prefs} divides {n}")


OPERAND_BYTES = 6 * 1024 * 1024


def _pick_rows(m, k):
    return _pick(m, [p for p in (1024, 512, 256, 128, 16) if p * k * 2 <= OPERAND_BYTES])


MATMUL_BYTES = 40 * 1024 * 1024


def _stream_rows(m, fixed, per_row):
    return _pick(m, [p for p in (4096, 2048, 1024, 512, 256, 128, 16) if fixed + p * per_row <= MATMUL_BYTES])


def _dot(a, b, dims):
    return lax.dot_general(a, b, (dims, ((), ())), preferred_element_type=F32)


def _mm_nn(a, b, out_dtype, name):
    m, k = a.shape
    n = b.shape[1]
    tn = _pick(n, (512, 384, 256, 128))
    tm = _stream_rows(m, 4 * k * tn, 4 * k + (2 * jnp.dtype(out_dtype).itemsize + 4) * tn)

    def body(a_ref, b_ref, o_ref):
        o_ref[...] = _dot(a_ref[...], b_ref[...], ((1,), (0,))).astype(o_ref.dtype)

    return pl.pallas_call(
        body, name=name, grid=(m // tm, n // tn),
        in_specs=[pl.BlockSpec((tm, k), lambda i, j: (i, 0)), pl.BlockSpec((k, tn), lambda i, j: (0, j))],
        out_specs=pl.BlockSpec((tm, tn), lambda i, j: (i, j)),
        out_shape=jax.ShapeDtypeStruct((m, n), out_dtype),
        compiler_params=_cp("parallel", "parallel"),
    )(a, b)


def _mm_nt(a, b, out_dtype, name, after=None):
    m, k = a.shape
    n = b.shape[0]
    tn = _pick(n, (512, 384, 256, 128))
    tm = _stream_rows(m, 4 * k * tn, 4 * k + (2 * jnp.dtype(out_dtype).itemsize + 4) * tn)

    def body(a_ref, b_ref, *rest):
        o_ref = rest[-1]
        o_ref[...] = _dot(a_ref[...], b_ref[...], ((1,), (1,))).astype(o_ref.dtype)

    extra = [] if after is None else [after]
    return pl.pallas_call(
        body, name=name, grid=(m // tm, n // tn),
        in_specs=[pl.BlockSpec((tm, k), lambda i, j: (i, 0)), pl.BlockSpec((tn, k), lambda i, j: (j, 0))] + [ANY] * len(extra),
        out_specs=pl.BlockSpec((tm, tn), lambda i, j: (i, j)),
        out_shape=jax.ShapeDtypeStruct((m, n), out_dtype),
        compiler_params=_cp("parallel", "parallel"),
    )(a, b, *extra)


def _mm_tn(a, bs, name):
    t, m = a.shape
    n = bs[0].shape[1]
    nb_ = len(bs)
    tc = _pick(t, (512, 16))
    tn = _pick(n, (512, 384, 256, 128))
    tm = _pick(m, [p for p in (1024, 512, 384, 256, 128) if (3 * p + 2 * nb_ * tn) * t * 2 <= VMEM_LIMIT - 2 * OPERAND_BYTES])
    if tm <= 256 and nb_ * n * t * 2 <= 2 * OPERAND_BYTES:
        tn = n

    def body(*refs):
        a_ref, b_refs, o_refs, at_ref = refs[0], refs[1:1 + nb_], refs[1 + nb_:1 + 2 * nb_], refs[-1]

        @pl.when(pl.program_id(1) == 0)
        def _():
            def chunk(c, _):
                rows = pl.ds(pl.multiple_of(c * tc, tc), tc)
                at_ref[:, rows] = a_ref[rows, :].T
                return 0

            lax.fori_loop(0, t // tc, chunk, 0)

        for b_ref, o_ref in zip(b_refs, o_refs):
            o_ref[...] = _dot(at_ref[...], b_ref[...], ((1,), (0,))).astype(BF16)

    res = pl.pallas_call(
        body, name=name, grid=(m // tm, n // tn),
        in_specs=[pl.BlockSpec((t, tm), lambda i, j: (0, i))] + [pl.BlockSpec((t, tn), lambda i, j: (0, j))] * nb_,
        out_specs=[pl.BlockSpec((tm, tn), lambda i, j: (i, j))] * nb_,
        out_shape=[jax.ShapeDtypeStruct((m, n), BF16)] * nb_,
        scratch_shapes=[pltpu.VMEM((tm, t), BF16)],
        compiler_params=_cp("parallel", "arbitrary"),
    )(a, *bs)
    return res[0] if nb_ == 1 else res


EPI = 256


def _silu_parts(g):
    sg = 0.5 * jnp.tanh(0.5 * g) + 0.5
    return sg, g * sg


def _ffn_in(h2, w):
    t, k = h2.shape
    tn = _pick(D_FF, (256, 128))
    tm = _stream_rows(t, 8 * k * tn, 4 * k + (3 * 2 * 2 + 2 * 4) * tn)
    nj = D_FF // tn

    def body(h_ref, wg_ref, wu_ref, g_ref, u_ref, a_ref):
        hv = h_ref[...]
        g_all = _dot(hv, wg_ref[...], ((1,), (0,)))
        u_all = _dot(hv, wu_ref[...], ((1,), (0,)))
        for r in range(tm // EPI):
            rows = slice(r * EPI, (r + 1) * EPI)
            g, u = g_all[rows], u_all[rows]
            g_ref[rows, :] = g.astype(BF16)
            u_ref[rows, :] = u.astype(BF16)
            a_ref[rows, :] = (_silu_parts(g)[1] * u).astype(BF16)

    blk = pl.BlockSpec((tm, tn), lambda i, j: (i, j))
    return pl.pallas_call(
        body, name="ffn_in", grid=(t // tm, nj),
        in_specs=[pl.BlockSpec((tm, k), lambda i, j: (i, 0)), pl.BlockSpec((k, tn), lambda i, j: (0, j)),
                  pl.BlockSpec((k, tn), lambda i, j: (0, j + nj))],
        out_specs=[blk] * 3, out_shape=[jax.ShapeDtypeStruct((t, D_FF), BF16)] * 3,
        compiler_params=_cp("parallel", "parallel"),
    )(h2, w, w)


def _d_act(df, w, g, u):
    t, k = df.shape
    tn = _pick(D_FF, (256, 128))
    tm = _stream_rows(t, 4 * k * tn, 4 * k + (4 * 2 * 2 + 4) * tn)

    def body(df_ref, w_ref, g_ref, u_ref, dg_ref, du_ref):
        da_all = _dot(df_ref[...], w_ref[...], ((1,), (1,)))
        for r in range(tm // EPI):
            rows = slice(r * EPI, (r + 1) * EPI)
            da = da_all[rows]
            gv = g_ref[rows, :].astype(F32)
            sg, silu = _silu_parts(gv)
            dg_ref[rows, :] = ((da * u_ref[rows, :].astype(F32)) * (sg + silu * (1.0 - sg))).astype(BF16)
            du_ref[rows, :] = (da * silu).astype(BF16)

    blk = pl.BlockSpec((tm, tn), lambda i, j: (i, j))
    return pl.pallas_call(
        body, name="d_act", grid=(t // tm, D_FF // tn),
        in_specs=[pl.BlockSpec((tm, k), lambda i, j: (i, 0)), pl.BlockSpec((tn, k), lambda i, j: (j, 0)), blk, blk],
        out_specs=[blk] * 2, out_shape=[jax.ShapeDtypeStruct((t, D_FF), BF16)] * 2,
        compiler_params=_cp("parallel", "parallel"),
    )(df, w, g, u)


def _d_h2(dg, du, w):
    t = dg.shape[0]
    n = w.shape[0]
    tm, tn = _pick_rows(t, D_FF), _pick(n, (512, 256, 128))

    def body(dg_ref, du_ref, wg_ref, wu_ref, o_ref):
        o_ref[...] = (_dot(dg_ref[...], wg_ref[...], ((1,), (1,)))
                      + _dot(du_ref[...], wu_ref[...], ((1,), (1,)))).astype(BF16)

    return pl.pallas_call(
        body, name="d_h2", grid=(t // tm, n // tn),
        in_specs=[pl.BlockSpec((tm, D_FF), lambda i, j: (i, 0)), pl.BlockSpec((tm, D_FF), lambda i, j: (i, 0)),
                  pl.BlockSpec((tn, D_FF), lambda i, j: (j, 0)), pl.BlockSpec((tn, D_FF), lambda i, j: (j, 1))],
        out_specs=pl.BlockSpec((tm, tn), lambda i, j: (i, j)),
        out_shape=jax.ShapeDtypeStruct((t, n), BF16),
        compiler_params=_cp("parallel", "parallel"),
    )(dg, du, w, w)


TM = 256


def _row(w):
    return pl.BlockSpec((TM, w), lambda i: (i, 0))


def _row_at(w, col):
    return pl.BlockSpec((TM, w), lambda i: (i, col))


def _vec(w):
    return pl.BlockSpec((1, w), lambda i: (0, 0))


def _per_ex(w):
    return pl.BlockSpec((1, 1, w), lambda i: (i // (S // TM), 0, 0))


def _pos(w):
    return pl.BlockSpec((TM, w), lambda i: (i % (S // TM), 0))


def _full(shape):
    return pl.BlockSpec(shape, lambda i: (0,) * len(shape))


def _rms(x):
    return lax.rsqrt(jnp.mean(x * x, axis=-1, keepdims=True) + EPS)


def _rms_bwd(n, r, dn):
    return r * (dn - n * jnp.mean(dn * n, axis=-1, keepdims=True))


def _colsum(v):
    return jnp.sum(v, axis=0, keepdims=True)


def _acc_first(i, ref, val, every=None):
    first = (i == 0) if every is None else (i % every == 0)

    @pl.when(first)
    def _():
        ref[...] = jnp.zeros_like(ref)

    ref[...] += val.reshape(ref.shape)


def _pre1(x, g, sc, sh):
    t = x.shape[0]

    def body(x_ref, g_ref, sc_ref, sh_ref, h_ref):
        xv = x_ref[...]
        n = xv * _rms(xv)
        h_ref[...] = ((n * g_ref[...]) * (1.0 + sc_ref[0]) + sh_ref[0]).astype(BF16)

    return pl.pallas_call(
        body, name="pre1", grid=(t // TM,),
        in_specs=[_row(D), _vec(D), _per_ex(D), _per_ex(D)],
        out_specs=_row(D), out_shape=jax.ShapeDtypeStruct((t, D), BF16),
        compiler_params=_cp("parallel"),
    )(x, g, sc, sh)


def _rope_fwd(v, c, sm, sp):
    return v * c + pltpu.roll(v, HP - ROPE // 2, 1) * sm + pltpu.roll(v, ROPE // 2, 1) * sp


def _rope_bwd(dv, c, sm, sp):
    return dv * c + pltpu.roll(dv * sm, ROPE // 2, 1) + pltpu.roll(dv * sp, HP - ROPE // 2, 1)


def _mla_pre(proj, g_cq, g_ckv, w_uq, w_k, w_v, rc, rsm, rsp):
    t = proj.shape[0]

    def body(tail_ref, gq_ref, gkv_ref, wuq_ref, wk_ref, wv_ref, c_ref, sm_ref, sp_ref,
             q_ref, k_ref, v_ref, cqn_ref, ckvn_ref):
        tail = tail_ref[...]
        cq, ckv, kr = tail[:, :Q_LORA], tail[:, Q_LORA:Q_LORA + KV_LORA], tail[:, Q_LORA + KV_LORA:]
        cqn = (cq * _rms(cq) * gq_ref[...]).astype(BF16)
        ckvn = (ckv * _rms(ckv) * gkv_ref[...]).astype(BF16)
        cqn_ref[...] = cqn
        ckvn_ref[...] = ckvn
        c, sm, sp = c_ref[...], sm_ref[...], sp_ref[...]
        q = _dot(cqn, wuq_ref[...], ((1,), (0,)))
        kn = _dot(ckvn, wk_ref[...], ((1,), (0,)))
        v_ref[...] = _dot(ckvn, wv_ref[...], ((1,), (0,))).astype(BF16)
        krr = _rope_fwd(kr, c, sm, sp)
        for h in range(H):
            sl = slice(h * HP, (h + 1) * HP)
            q_ref[:, sl] = _rope_fwd(q[:, sl], c, sm, sp).astype(BF16)
            k_ref[:, sl] = (kn[:, sl] + krr).astype(BF16)

    wide = H * HP
    return pl.pallas_call(
        body, name="mla_pre", grid=(t // TM,),
        in_specs=[_row_at(TAIL, TAIL0 // TAIL), _vec(Q_LORA), _vec(KV_LORA), _full((Q_LORA, wide)),
                  _full((KV_LORA, wide)), _full((KV_LORA, wide)), _pos(HP), _pos(HP), _pos(HP)],
        out_specs=[_row(wide), _row(wide), _row(wide), _row(Q_LORA), _row(KV_LORA)],
        out_shape=[jax.ShapeDtypeStruct((t, wide), BF16)] * 3
        + [jax.ShapeDtypeStruct((t, Q_LORA), BF16), jax.ShapeDtypeStruct((t, KV_LORA), BF16)],
        compiler_params=_cp("parallel"),
    )(proj, g_cq, g_ckv, w_uq, w_k, w_v, rc, rsm, rsp)


def _mla_pre_bwd(proj, dq_, dk_, dv_, g_cq, g_ckv, w_uq, w_k, w_v, rc, rsm, rsp):
    t = proj.shape[0]
    wide = H * HP

    def body(tail_ref, dq_ref, dk_ref, dv_ref, gq_ref, gkv_ref, wuq_ref, wk_ref, wv_ref, c_ref, sm_ref, sp_ref,
             dqo_ref, dko_ref, dvo_ref, dtail_ref, dgq_ref, dgkv_ref):
        i = pl.program_id(0)
        tail = tail_ref[...]
        cq, ckv = tail[:, :Q_LORA], tail[:, Q_LORA:Q_LORA + KV_LORA]
        c, sm, sp = c_ref[...], sm_ref[...], sp_ref[...]
        dkr = jnp.zeros((TM, HP), F32)
        for h in range(H):
            sl = slice(h * HP, (h + 1) * HP)
            dqo_ref[:, sl] = _rope_bwd(dq_ref[:, sl], c, sm, sp).astype(BF16)
            dkr = dkr + dk_ref[:, sl]
        lane = lax.broadcasted_iota(jnp.int32, (TM, HP), 1)
        dkr = jnp.where((lane >= NOPE) & (lane < NOPE + ROPE), _rope_bwd(dkr, c, sm, sp), 0.0)
        dkb = dk_ref[...].astype(BF16)
        dvb = dv_ref[...].astype(BF16)
        dko_ref[...] = dkb
        dvo_ref[...] = dvb
        dcqn = _dot(dqo_ref[...], wuq_ref[...], ((1,), (1,)))
        dckvn = _dot(dkb, wk_ref[...], ((1,), (1,))) + _dot(dvb, wv_ref[...], ((1,), (1,)))
        rq, rkv = _rms(cq), _rms(ckv)
        nq, nkv = cq * rq, ckv * rkv
        _acc_first(i, dgq_ref, _colsum(dcqn * nq))
        _acc_first(i, dgkv_ref, _colsum(dckvn * nkv))
        dtail_ref[:, :Q_LORA] = _rms_bwd(nq, rq, dcqn * gq_ref[...]).astype(BF16)
        dtail_ref[:, Q_LORA:Q_LORA + KV_LORA] = _rms_bwd(nkv, rkv, dckvn * gkv_ref[...]).astype(BF16)
        dtail_ref[:, Q_LORA + KV_LORA:] = dkr.astype(BF16)

    return pl.pallas_call(
        body, name="mla_pre_bwd", grid=(t // TM,),
        in_specs=[_row_at(TAIL, TAIL0 // TAIL), _row(wide), _row(wide), _row(wide), _vec(Q_LORA), _vec(KV_LORA),
                  _full((Q_LORA, wide)), _full((KV_LORA, wide)), _full((KV_LORA, wide)), _pos(HP), _pos(HP), _pos(HP)],
        out_specs=[_row(wide), _row(wide), _row(wide), _row(TAIL), _vec(Q_LORA), _vec(KV_LORA)],
        out_shape=[jax.ShapeDtypeStruct((t, wide), BF16)] * 3 + [jax.ShapeDtypeStruct((t, TAIL), BF16),
                   jax.ShapeDtypeStruct((1, Q_LORA), F32), jax.ShapeDtypeStruct((1, KV_LORA), F32)],
        compiler_params=_cp("arbitrary"),
    )(proj, dq_, dk_, dv_, g_cq, g_ckv, w_uq, w_k, w_v, rc, rsm, rsp)


def _post_attn(out_a, out_b, g_a, g_b):
    t = out_a.shape[0]

    def body(a_ref, b_ref, ga_ref, gb_ref, y_ref):
        a, b = a_ref[...], b_ref[...]
        y_ref[:, :D_A] = (a * _rms(a) * ga_ref[...]).astype(BF16)
        y_ref[:, D_A:] = (b * _rms(b) * gb_ref[...]).astype(BF16)

    return pl.pallas_call(
        body, name="post_attn", grid=(t // TM,),
        in_specs=[_row(D_A), _row(D_A), _vec(D_A), _vec(D_A)],
        out_specs=_row(D), out_shape=jax.ShapeDtypeStruct((t, D), BF16),
        compiler_params=_cp("parallel"),
    )(out_a, out_b, g_a, g_b)


def _post_attn_bwd(dy, out_a, out_b, g_a, g_b):
    t = dy.shape[0]

    def body(dy_ref, a_ref, b_ref, ga_ref, gb_ref, da_ref, db_ref, dga_ref, dgb_ref):
        i = pl.program_id(0)
        dy_ = dy_ref[...].astype(F32)
        for src, g_ref, dst, dg_ref, sl in ((a_ref, ga_ref, da_ref, dga_ref, slice(0, D_A)),
                                            (b_ref, gb_ref, db_ref, dgb_ref, slice(D_A, D))):
            v = src[...]
            r = _rms(v)
            n = v * r
            dyv = dy_[:, sl]
            _acc_first(i, dg_ref, _colsum(dyv * n))
            dst[...] = _rms_bwd(n, r, dyv * g_ref[...])

    return pl.pallas_call(
        body, name="post_attn_bwd", grid=(t // TM,),
        in_specs=[_row(D), _row(D_A), _row(D_A), _vec(D_A), _vec(D_A)],
        out_specs=[_row(D_A), _row(D_A), _vec(D_A), _vec(D_A)],
        out_shape=[jax.ShapeDtypeStruct((t, D_A), F32)] * 2 + [jax.ShapeDtypeStruct((1, D_A), F32)] * 2,
        compiler_params=_cp("arbitrary"),
    )(dy, out_a, out_b, g_a, g_b)


def _resid_norm2(x, mix, g1, g, sc, sh):
    t = x.shape[0]

    def body(x_ref, mix_ref, g1_ref, g_ref, sc_ref, sh_ref, x2_ref, h_ref):
        x2 = x_ref[...] + g1_ref[0] * mix_ref[...]
        x2_ref[...] = x2
        n = x2 * _rms(x2)
        h_ref[...] = ((n * g_ref[...]) * (1.0 + sc_ref[0]) + sh_ref[0]).astype(BF16)

    return pl.pallas_call(
        body, name="resid_norm2", grid=(t // TM,),
        in_specs=[_row(D), _row(D), _per_ex(D), _vec(D), _per_ex(D), _per_ex(D)],
        out_specs=[_row(D), _row(D)],
        out_shape=[jax.ShapeDtypeStruct((t, D), F32), jax.ShapeDtypeStruct((t, D), BF16)],
        compiler_params=_cp("parallel"),
    )(x, mix, g1, g, sc, sh)


def _sigmoid(v):
    return 1.0 / (1.0 + jnp.exp(-v))


def _final(x2, f, g2, g_fin, target):
    t = x2.shape[0]
    nb = t // S
    tpb = S // TM

    def body(x2_ref, f_ref, g2_ref, g_ref, t_ref, dx3_ref, df_ref, loss_ref, dgf_ref, dg2_ref):
        i = pl.program_id(0)
        fv = f_ref[...].astype(F32)
        x3 = x2_ref[...] + g2_ref[0] * fv
        r = _rms(x3)
        n = x3 * r
        err = n * g_ref[...] - t_ref[...]
        _acc_first(i, loss_ref, _colsum(err * err))
        dy = err * (1.0 / D)
        _acc_first(i, dgf_ref, _colsum(dy * n))
        dx3 = _rms_bwd(n, r, dy * g_ref[...])
        dx3_ref[...] = dx3
        _acc_first(i, dg2_ref, _colsum(dx3 * fv), every=tpb)
        df_ref[...] = (dx3 * g2_ref[0]).astype(BF16)

    return pl.pallas_call(
        body, name="final", grid=(t // TM,),
        in_specs=[_row(D), _row(D), _per_ex(D), _vec(D), _row(D)],
        out_specs=[_row(D), _row(D), _vec(D), _vec(D), _per_ex(D)],
        out_shape=[jax.ShapeDtypeStruct((t, D), F32), jax.ShapeDtypeStruct((t, D), BF16),
                   jax.ShapeDtypeStruct((1, D), F32), jax.ShapeDtypeStruct((1, D), F32),
                   jax.ShapeDtypeStruct((nb, 1, D), F32)],
        compiler_params=_cp("arbitrary"),
    )(x2, f, g2, g_fin, target)


def _norm_bwd(xin, dh, dres, g, sc, gate=None):
    t = xin.shape[0]
    nb = t // S
    tpb = S // TM
    gated = gate is not None

    def body(*refs):
        if gated:
            x_ref, dh_ref, dres_ref, g_ref, sc_ref, mix_ref, g1_ref, dx_ref, dsh_ref, dsc_ref, dg_ref, dg1_ref, dmix_ref = refs
        else:
            x_ref, dh_ref, dres_ref, g_ref, sc_ref, dx_ref, dsh_ref, dsc_ref, dg_ref = refs
        i = pl.program_id(0)
        xv, dhv = x_ref[...], dh_ref[...].astype(F32)
        r = _rms(xv)
        n = xv * r
        gv = g_ref[...]
        _acc_first(i, dsh_ref, _colsum(dhv), every=tpb)
        _acc_first(i, dsc_ref, _colsum(dhv * (n * gv)), every=tpb)
        dng = dhv * (1.0 + sc_ref[0])
        _acc_first(i, dg_ref, _colsum(dng * n))
        dx = dres_ref[...] + _rms_bwd(n, r, dng * gv)
        dx_ref[...] = dx
        if gated:
            _acc_first(i, dg1_ref, _colsum(dx * mix_ref[...].astype(F32)), every=tpb)
            dmix_ref[...] = (dx * g1_ref[0]).astype(BF16)

    in_specs = [_row(D), _row(D), _row(D), _vec(D), _per_ex(D)]
    out_specs = [_row(D), _per_ex(D), _per_ex(D), _vec(D)]
    out_shape = [jax.ShapeDtypeStruct((t, D), F32), jax.ShapeDtypeStruct((nb, 1, D), F32),
                 jax.ShapeDtypeStruct((nb, 1, D), F32), jax.ShapeDtypeStruct((1, D), F32)]
    args = [xin, dh, dres, g, sc]
    if gated:
        in_specs += [_row(D), _per_ex(D)]
        out_specs += [_per_ex(D), _row(D)]
        out_shape += [jax.ShapeDtypeStruct((nb, 1, D), F32), jax.ShapeDtypeStruct((t, D), BF16)]
        args += list(gate)
    return pl.pallas_call(
        body, name="norm2_bwd" if gated else "norm1_bwd", grid=(t // TM,),
        in_specs=in_specs, out_specs=out_specs, out_shape=out_shape,
        compiler_params=_cp("arbitrary"),
    )(*args)


TQ = 256
TB = 512


def _mla_fwd(q, k, v):
    t = q.shape[0]
    nb = t // S

    def body(q_ref, k_ref, v_ref, o_ref, lse_ref):
        causal = lax.broadcasted_iota(jnp.int32, (TB, TB), 0) >= lax.broadcasted_iota(jnp.int32, (TB, TB), 1)
        heads = [slice(h * HP, (h + 1) * HP) for h in range(2)]
        for i in range(S // TB):
            ri, past = slice(i * TB, (i + 1) * TB), slice(0, i * TB)
            qhs = [q_ref[ri, sl] for sl in heads]
            sd = [jnp.where(causal, _dot(qh, k_ref[ri, sl], ((1,), (1,))) * SCALE_B, NEG) for qh, sl in zip(qhs, heads)]
            ms = [jnp.max(s, axis=-1, keepdims=True) for s in sd]
            if i:
                so = [_dot(qh, k_ref[past, sl], ((1,), (1,))) * SCALE_B for qh, sl in zip(qhs, heads)]
                ms = [jnp.maximum(m, jnp.max(s, axis=-1, keepdims=True)) for m, s in zip(ms, so)]
            pd = [jnp.exp(s - m) for s, m in zip(sd, ms)]
            ls = [jnp.sum(p, axis=-1, keepdims=True) for p in pd]
            acc = [_dot(p.astype(BF16), v_ref[ri, sl], ((1,), (0,))) for p, sl in zip(pd, heads)]
            if i:
                po = [jnp.exp(s - m) for s, m in zip(so, ms)]
                ls = [l + jnp.sum(p, axis=-1, keepdims=True) for l, p in zip(ls, po)]
                acc = [a + _dot(p.astype(BF16), v_ref[past, sl], ((1,), (0,))) for a, p, sl in zip(acc, po, heads)]
            o_ref[ri, :] = acc[0] / ls[0] + acc[1] / ls[1]
            for sl, m, l in zip(heads, ms, ls):
                lse_ref[ri, sl] = jnp.broadcast_to(m + jnp.log(l), (TB, HP))

    wide2 = pl.BlockSpec((S, 2 * HP), lambda b, p: (b, p))
    return pl.pallas_call(
        body, name="mla_fwd", grid=(nb, H // 2),
        in_specs=[wide2, wide2, wide2],
        out_specs=[pl.BlockSpec((S, HP), lambda b, p: (b, p)), wide2],
        out_shape=[jax.ShapeDtypeStruct((t, H * VDIM), F32), jax.ShapeDtypeStruct((t, H * HP), F32)],
        compiler_params=_cp("parallel", "parallel"),
    )(q, k, v)


def _mla_bwd(q, k, v, o, do, lse):
    t = q.shape[0]
    nb = t // S
    nq = S // TQ

    def body(q_ref, k_ref, v_ref, o_ref, do_ref, lse_ref, dq_ref, dk_ref, dv_ref):
        lane = lax.broadcasted_iota(jnp.int32, (TB, HP), 1)
        causal = lax.broadcasted_iota(jnp.int32, (TB, TB), 0) >= lax.broadcasted_iota(jnp.int32, (TB, TB), 1)
        heads = [slice(h * HP, (h + 1) * HP) for h in range(2)]
        nblk = S // TB
        for i in reversed(range(nblk)):
            ri, past = slice(i * TB, (i + 1) * TB), slice(0, i * TB)
            dov = do_ref[ri, :]
            prod = dov * o_ref[ri, :]
            dob = dov.astype(BF16)
            deltas = [jnp.sum(jnp.where((lane < VDIM) if h == 0 else (lane >= VDIM), prod, 0.0), axis=-1, keepdims=True)
                      for h in range(2)]
            qhs = [q_ref[ri, sl] for sl in heads]
            lses = [lse_ref[ri, sl][:, :1] for sl in heads]
            for rows, diagonal in ((ri, True), (past, False)):
                if rows.stop == rows.start:
                    continue
                ps = [jnp.exp(_dot(qh, k_ref[rows, sl], ((1,), (1,))) * SCALE_B - lse) for qh, sl, lse in zip(qhs, heads, lses)]
                if diagonal:
                    ps = [jnp.where(causal, p, 0.0) for p in ps]
                dps = [_dot(dob, v_ref[rows, sl], ((1,), (1,))) for sl in heads]
                dss = [(p * (dp - delta) * SCALE_B).astype(BF16) for p, dp, delta in zip(ps, dps, deltas)]
                for sl, qh, p, ds in zip(heads, qhs, ps, dss):
                    dq = _dot(ds, k_ref[rows, sl], ((1,), (0,)))
                    dk = _dot(ds, qh, ((0,), (0,)))
                    dv = _dot(p.astype(BF16), dob, ((0,), (0,)))
                    if diagonal:
                        dq_ref[ri, sl] = dq
                    else:
                        dq_ref[ri, sl] += dq
                    if i == nblk - 1:
                        dk_ref[rows, sl] = dk
                        dv_ref[rows, sl] = dv
                    else:
                        dk_ref[rows, sl] += dk
                        dv_ref[rows, sl] += dv

    wide2 = pl.BlockSpec((S, 2 * HP), lambda b, p: (b, p))
    pair = pl.BlockSpec((S, HP), lambda b, p: (b, p))
    return pl.pallas_call(
        body, name="mla_bwd", grid=(nb, H // 2),
        in_specs=[wide2, wide2, wide2, pair, pair, wide2],
        out_specs=[wide2, wide2, wide2],
        out_shape=[jax.ShapeDtypeStruct((t, H * HP), F32)] * 3,
        compiler_params=_cp("parallel", "parallel"),
    )(q, k, v, o, do, lse)


def _t5_bucket(dist):
    max_exact = N_BUCKETS // 2
    d = np.maximum(dist, 1).astype(np.float64)
    large = max_exact + (np.log(d / max_exact) / np.log(MAX_DISTANCE / max_exact) * (N_BUCKETS - max_exact)).astype(np.int64)
    large = np.minimum(large, N_BUCKETS - 1)
    return np.where(dist < max_exact, dist, large).astype(np.int32)


def _band_geometry():
    a = np.arange(BLK)[:, None]
    bk = np.arange(2 * BLK)[None, :]
    steps = BLK + a - bk
    valid = (steps >= 0) & (steps <= BLK)
    buckets = np.stack([_t5_bucket(np.clip(steps, 0, BLK) * d) for d in DILATIONS])
    return buckets, valid


def _band_bias(rel_bias):
    buckets, valid = _band_geometry()
    onehot = (jnp.asarray(buckets)[..., None] == jnp.arange(N_BUCKETS)).astype(F32)
    bias = jnp.einsum("rqkn,nh->rhqk", onehot, rel_bias, precision=lax.Precision.HIGHEST)
    bias = jnp.where(jnp.asarray(valid)[None, None], bias, NEG)
    return bias.reshape(3, H // 2, 2 * BLK, 2 * BLK)


def _dil_items():
    items = []
    for r, d in enumerate(DILATIONS):
        for res in range(d):
            for blk in range(S // d // BLK):
                items.append((r, d, blk * BLK * d + res, blk > 0))
    return items


GROUP = 4


def _strided(start, d):
    return pl.ds(start, BLK) if d == 1 else pl.ds(start, BLK, stride=d)


def _stack_heads(tile, own):
    return jnp.where(own, jnp.concatenate([tile, tile], axis=0), 0.0).astype(BF16)


def _own_lanes():
    row = lax.broadcasted_iota(jnp.int32, (2 * BLK, HP), 0)
    lane = lax.broadcasted_iota(jnp.int32, (2 * BLK, HP), 1)
    return (lane < E_A) == (row < BLK)


def _dil_fwd(proj, biasm):
    t = proj.shape[0]
    nb = t // S

    def body(q_ref, k_ref, v_ref, b_ref, o_ref, lse_ref, ob_ref, lb_ref):
        lane = lax.broadcasted_iota(jnp.int32, (BLK, HP), 1)
        own = _own_lanes()
        items = _dil_items()
        for g in range(0, len(items), GROUP):
            grp = items[g:g + GROUP]
            ss, vts = [], []
            for r, d, start, has_prev in grp:
                cur = _strided(start, d)
                rows = [_strided(start - BLK * d, d), cur] if has_prev else [cur]
                q2 = _stack_heads(q_ref[cur, :], own)
                kt = jnp.concatenate([k_ref[x, :] for x in rows], axis=0).astype(BF16)
                vts.append(jnp.concatenate([v_ref[x, :] for x in rows], axis=0).astype(BF16))
                bias = b_ref[r, 0] if has_prev else b_ref[r, 0, :, BLK:]
                ss.append(_dot(q2, kt, ((1,), (1,))) * SCALE_A + bias)
            ms = [jnp.max(s, axis=-1, keepdims=True) for s in ss]
            ps = [jnp.exp(s - m) for s, m in zip(ss, ms)]
            ls = [jnp.sum(p, axis=-1, keepdims=True) for p in ps]
            for (r, d, start, _), p, vt, m, l in zip(grp, ps, vts, ms, ls):
                cur = _strided(start, d)
                o2 = _dot(p.astype(BF16), vt, ((1,), (0,))) / l
                lse2 = m + jnp.log(l)
                ob_ref[r, cur, :] = jnp.where(lane < E_A, o2[:BLK], o2[BLK:])
                lb_ref[r, cur, :] = jnp.where(lane < E_A, lse2[:BLK], lse2[BLK:])

        def merge(c, _):
            rows = pl.ds(pl.multiple_of(c * TQ, TQ), TQ)
            l0, l1, l2 = lb_ref[0, rows, :], lb_ref[1, rows, :], lb_ref[2, rows, :]
            m = jnp.maximum(jnp.maximum(l0, l1), l2)
            e0, e1, e2 = jnp.exp(l0 - m), jnp.exp(l1 - m), jnp.exp(l2 - m)
            tot = e0 + e1 + e2
            o_ref[rows, :] = (e0 * ob_ref[0, rows, :] + e1 * ob_ref[1, rows, :] + e2 * ob_ref[2, rows, :]) / tot
            lse_ref[rows, :] = m + jnp.log(tot)
            return 0

        lax.fori_loop(0, S // TQ, merge, 0)

    npair = H // 2
    return pl.pallas_call(
        body, name="dil_fwd", grid=(nb, npair),
        in_specs=[pl.BlockSpec((S, HP), lambda b, p: (b, p)), pl.BlockSpec((S, HP), lambda b, p: (b, npair + p)),
                  pl.BlockSpec((S, HP), lambda b, p: (b, 2 * npair + p)),
                  pl.BlockSpec((3, 1, 2 * BLK, 2 * BLK), lambda b, p: (0, p, 0, 0))],
        out_specs=[pl.BlockSpec((S, HP), lambda b, p: (b, p))] * 2,
        out_shape=[jax.ShapeDtypeStruct((t, D_A), F32)] * 2,
        scratch_shapes=[pltpu.VMEM((3, S, HP), F32), pltpu.VMEM((3, S, HP), F32)],
        compiler_params=_cp("parallel", "parallel"),
    )(proj, proj, proj, biasm)


def _dil_bwd(proj, biasm, o, do, lse):
    t = proj.shape[0]
    nb = t // S

    def body(q_ref, k_ref, v_ref, b_ref, o_ref, do_ref, lse_ref, dq_out, dk_out, dv_out, ds_ref, dq_ref, dk_ref, dv_ref):
        dq_ref[...] = jnp.zeros_like(dq_ref)
        dk_ref[...] = jnp.zeros_like(dk_ref)
        dv_ref[...] = jnp.zeros_like(dv_ref)
        ds_ref[...] = jnp.zeros_like(ds_ref)
        lane = lax.broadcasted_iota(jnp.int32, (BLK, HP), 1)
        own = _own_lanes()
        items = _dil_items()
        for g in range(0, len(items), GROUP):
            grp = items[g:g + GROUP]
            q2s, kts, do2s, ss, dps, lse2s, delta2s = [], [], [], [], [], [], []
            for r, d, start, has_prev in grp:
                cur = _strided(start, d)
                rows = [_strided(start - BLK * d, d), cur] if has_prev else [cur]
                q2 = _stack_heads(q_ref[cur, :], own)
                kt = jnp.concatenate([k_ref[x, :] for x in rows], axis=0).astype(BF16)
                vt = jnp.concatenate([v_ref[x, :] for x in rows], axis=0).astype(BF16)
                dot_ = do_ref[cur, :]
                prod = dot_ * o_ref[cur, :]
                lset = lse_ref[cur, :]
                do2 = _stack_heads(dot_, own)
                bias = b_ref[r, 0] if has_prev else b_ref[r, 0, :, BLK:]
                ss.append(_dot(q2, kt, ((1,), (1,))) * SCALE_A + bias)
                dps.append(_dot(do2, vt, ((1,), (1,))))
                lse2s.append(jnp.concatenate([lset[:, :1], lset[:, E_A:E_A + 1]], axis=0))
                delta2s.append(jnp.concatenate([jnp.sum(jnp.where(lane < E_A, prod, 0.0), axis=-1, keepdims=True),
                                                jnp.sum(jnp.where(lane >= E_A, prod, 0.0), axis=-1, keepdims=True)], axis=0))
                q2s.append(q2)
                kts.append(kt)
                do2s.append(do2)
            ps = [jnp.exp(s - lse2) for s, lse2 in zip(ss, lse2s)]
            dls = [p * (dp - delta2) for p, dp, delta2 in zip(ps, dps, delta2s)]
            for (r, d, start, has_prev), q2, kt, do2, p, dl in zip(grp, q2s, kts, do2s, ps, dls):
                cur = _strided(start, d)
                dsb = (dl * SCALE_A).astype(BF16)
                dq2 = _dot(dsb, kt, ((1,), (0,)))
                dkt = _dot(dsb, q2, ((0,), (0,)))
                dvt = _dot(p.astype(BF16), do2, ((0,), (0,)))
                dq_ref[cur, :] += jnp.where(lane < E_A, dq2[:BLK], dq2[BLK:])
                if has_prev:
                    prev = _strided(start - BLK * d, d)
                    ds_ref[0, r, 0] += dl
                    dk_ref[prev, :] += dkt[:BLK]
                    dv_ref[prev, :] += dvt[:BLK]
                    dk_ref[cur, :] += dkt[BLK:]
                    dv_ref[cur, :] += dvt[BLK:]
                else:
                    ds_ref[0, r, 0, :, BLK:] += dl
                    dk_ref[cur, :] += dkt
                    dv_ref[cur, :] += dvt
        dq_out[...] = dq_ref[...].astype(BF16)
        dk_out[...] = dk_ref[...].astype(BF16)
        dv_out[...] = dv_ref[...].astype(BF16)

    npair = H // 2
    pair = pl.BlockSpec((S, HP), lambda b, p: (b, p))
    return pl.pallas_call(
        body, name="dil_bwd", grid=(nb, npair),
        in_specs=[pair, pl.BlockSpec((S, HP), lambda b, p: (b, npair + p)),
                  pl.BlockSpec((S, HP), lambda b, p: (b, 2 * npair + p)),
                  pl.BlockSpec((3, 1, 2 * BLK, 2 * BLK), lambda b, p: (0, p, 0, 0)), pair, pair, pair],
        out_specs=[pair, pair, pair, pl.BlockSpec((1, 3, 1, 2 * BLK, 2 * BLK), lambda b, p: (b, 0, p, 0, 0))],
        out_shape=[jax.ShapeDtypeStruct((t, D_A), BF16)] * 3 + [jax.ShapeDtypeStruct((nb, 3, npair, 2 * BLK, 2 * BLK), F32)],
        scratch_shapes=[pltpu.VMEM((S, HP), F32)] * 3,
        compiler_params=_cp("parallel", "parallel"),
    )(proj, proj, proj, biasm, o, do, lse)


def _rel_bias_grad(dlogits):
    nb = dlogits.shape[0]
    buckets, _ = _band_geometry()
    kk = 3 * BLK * 2 * BLK
    dl = jnp.transpose(dlogits.reshape(nb, 3, H, BLK, 2 * BLK), (0, 2, 1, 3, 4)).reshape(nb, H, kk)
    bk = jnp.asarray(buckets.reshape(1, kk))
    tk = kk // 12

    def body(dl_ref, bk_ref, o_ref):
        j = pl.program_id(0)
        onehot = (bk_ref[...] == lax.broadcasted_iota(jnp.int32, (N_BUCKETS, tk), 0)).astype(F32)
        tot = dl_ref[0]
        for b in range(1, nb):
            tot = tot + dl_ref[b]
        part = lax.dot_general(onehot, tot, ((((1,), (1,))), ((), ())), preferred_element_type=F32,
                               precision=lax.Precision.HIGHEST)
        _acc_first(j, o_ref, part)

    return pl.pallas_call(
        body, name="rel_bias_grad", grid=(kk // tk,),
        in_specs=[pl.BlockSpec((nb, H, tk), lambda j: (0, 0, j)), pl.BlockSpec((1, tk), lambda j: (0, j))],
        out_specs=pl.BlockSpec((N_BUCKETS, H), lambda j: (0, 0)),
        out_shape=jax.ShapeDtypeStruct((N_BUCKETS, H), F32),
        compiler_params=_cp("arbitrary"),
    )(dl, bk)


def _mesh_place():
    x, y, c = lax.axis_index("x"), lax.axis_index("y"), lax.axis_index("c")
    return x, y, c


def _peer(k):
    x, y, c = _mesh_place()
    px = 1 - x if k & 4 else x
    py = 1 - y if k & 2 else y
    pc = 1 - c if k & 1 else c
    return (px, py, pc), 4 * px + 2 * py + pc


ANY = pl.BlockSpec(memory_space=pl.ANY)


def _exchange(arrays, gathers, name, after=None):
    n_arr = len(arrays)

    def body(*refs):
        ins, outs = refs[:n_arr], refs[n_arr + 1:2 * n_arr + 1]
        send, recv, loc = refs[2 * n_arr + 1:]
        x, y, c = _mesh_place()
        me = 4 * x + 2 * y + c
        local = [pltpu.make_async_copy(ins[a] if gathers[a] else ins[a].at[me], outs[a].at[me], loc.at[a])
                 for a in range(n_arr)]
        remote = _peer_copies(ins, outs, send, recv, gathers)
        for cp in local:
            cp.start()
        for put, _ in remote:
            put.start()
        for cp in local:
            cp.wait()
        for put, got in remote:
            put.wait_send()
            got.wait_recv()

    return pl.pallas_call(
        body, name=name,
        in_specs=[ANY] * (n_arr + 1), out_specs=[ANY] * n_arr,
        out_shape=[jax.ShapeDtypeStruct(((N_DEV,) if g else ()) + a.shape, a.dtype) for a, g in zip(arrays, gathers)],
        scratch_shapes=[pltpu.SemaphoreType.DMA((n_arr * (N_DEV - 1),)), pltpu.SemaphoreType.DMA((n_arr * (N_DEV - 1),)),
                        pltpu.SemaphoreType.DMA((n_arr,))],
        compiler_params=pltpu.CompilerParams(has_side_effects=True),
    )(*arrays, arrays[0] if after is None else after)


def _gather_two_level(arrays, name):
    n_arr = len(arrays)
    per = N_DEV - 1

    def body(*refs):
        ins, outs = refs[:n_arr], refs[n_arr:2 * n_arr]
        send, recv, loc = refs[2 * n_arr:]
        x, y, c = _mesh_place()
        me, sibling = (x, y, c), (x, y, 1 - c)
        chips = [(1 - x, y), (x, 1 - y), (1 - x, 1 - y)]

        def block(a, place):
            px, py, pc = place
            return outs[a].at[4 * px + 2 * py + pc]

        def copy(a, k, place, to, src=None):
            dst = block(a, place)
            return pltpu.make_async_remote_copy(dst if src is None else src, dst, send.at[a * per + k], recv.at[a * per + k],
                                                device_id=to, device_id_type=pl.DeviceIdType.MESH)

        local = [pltpu.make_async_copy(ins[a], block(a, me), loc.at[a]) for a in range(n_arr)]
        for cp in local:
            cp.start()
        first = []
        for a in range(n_arr):
            first.append(copy(a, 0, me, sibling, src=ins[a]))
            first += [copy(a, 1 + j, me, (*chip, c), src=ins[a]) for j, chip in enumerate(chips)]
        for cp in first:
            cp.start()
        passed = []
        for j, chip in enumerate(chips):
            for a in range(n_arr):
                copy(a, 1 + j, (*chip, c), me).wait_recv()
                passed.append(copy(a, 4 + j, (*chip, c), sibling))
                passed[-1].start()
        for a in range(n_arr):
            copy(a, 0, sibling, me).wait_recv()
            for j, chip in enumerate(chips):
                copy(a, 4 + j, (*chip, 1 - c), me).wait_recv()
        for cp in first + passed:
            cp.wait_send()
        for cp in local:
            cp.wait()

    return pl.pallas_call(
        body, name=name,
        in_specs=[ANY] * n_arr, out_specs=[ANY] * n_arr,
        out_shape=[jax.ShapeDtypeStruct((N_DEV,) + a.shape, a.dtype) for a in arrays],
        scratch_shapes=[pltpu.SemaphoreType.DMA((n_arr * per,)), pltpu.SemaphoreType.DMA((n_arr * per,)),
                        pltpu.SemaphoreType.DMA((n_arr,))],
        compiler_params=pltpu.CompilerParams(has_side_effects=True),
    )(*arrays)


HBM = pl.BlockSpec(memory_space=pltpu.HBM)
SEM = pl.BlockSpec(memory_space=pltpu.SEMAPHORE)
DATAFLOW = pltpu.SideEffectType.DATAFLOW_SIDE_EFFECTING


def _own_block_in_place(block, me):
    land = lax.empty((N_DEV,) + block.shape, block.dtype)
    return lax.dynamic_update_slice(land, block[None], (me,) + (0,) * block.ndim)


def _peer_copies(srcs, lands, send, recv, gathers):
    x, y, c = _mesh_place()
    me = 4 * x + 2 * y + c
    out = []
    for a, (src, land) in enumerate(zip(srcs, lands)):
        for k in range(1, N_DEV):
            dev, idx = _peer(k)
            sem = a * (N_DEV - 1) + k - 1
            mine = src if gathers[a] else src.at[idx]
            put = pltpu.make_async_remote_copy(mine, land.at[me], send.at[sem], recv.at[sem],
                                               device_id=dev, device_id_type=pl.DeviceIdType.MESH)
            got = pltpu.make_async_remote_copy(mine, land.at[idx], send.at[sem], recv.at[sem],
                                               device_id=dev, device_id_type=pl.DeviceIdType.MESH)
            out.append((put, got))
    return out


def _exchange_start(srcs, lands, gather, after, name):
    n = len(srcs)

    def body(*refs):
        srcs_, lands_, send, recv = refs[:n], refs[n:2 * n], refs[2 * n + 1], refs[2 * n + 2]
        for put, _ in _peer_copies(srcs_, lands_, send, recv, gather):
            put.start()
        refs[-1][...] = jnp.zeros_like(refs[-1])

    nsem = n * (N_DEV - 1)
    thru = [pltpu.HBM(a.shape, a.dtype) for a in list(srcs) + list(lands)]
    res = pl.pallas_call(
        body, name=name,
        out_shape=(pltpu.SemaphoreType.DMA((nsem,)), pltpu.SemaphoreType.DMA((nsem,)), *thru, jax.ShapeDtypeStruct((8, 128), F32)),
        in_specs=[HBM] * (2 * n) + [ANY],
        out_specs=(SEM, SEM, *([HBM] * (2 * n)), pl.BlockSpec(memory_space=pltpu.VMEM)),
        input_output_aliases={i: 2 + i for i in range(2 * n)},
        compiler_params=pltpu.CompilerParams(has_side_effects=DATAFLOW),
    )(*[pltpu.with_memory_space_constraint(a, pltpu.HBM) for a in list(srcs) + list(lands)], after)
    return res[0], res[1], list(res[2:2 + n]), list(res[2 + n:2 + 2 * n]), res[-1]


def _exchange_wait(send, recv, srcs, lands, gather, after, name):
    n = len(srcs)

    def body(*refs):
        srcs_, lands_, send_, recv_ = refs[:n], refs[n:2 * n], refs[2 * n], refs[2 * n + 1]
        for put, got in _peer_copies(srcs_, lands_, send_, recv_, gather):
            put.wait_send()
            got.wait_recv()

    thru = [pltpu.HBM(a.shape, a.dtype) for a in list(srcs) + list(lands)]
    res = pl.pallas_call(
        body, name=name, out_shape=tuple(thru),
        in_specs=[HBM] * (2 * n) + [SEM, SEM, ANY], out_specs=tuple([HBM] * (2 * n)),
        input_output_aliases={i: i for i in range(2 * n)},
        compiler_params=pltpu.CompilerParams(has_side_effects=DATAFLOW),
    )(*srcs, *lands, send, recv, after)
    return list(res[n:])


def _silu_rows(c):
    def body(c_ref, o_ref):
        v = c_ref[...]
        o_ref[...] = v * _sigmoid(v)

    return pl.pallas_call(body, name="cond", out_shape=jax.ShapeDtypeStruct(c.shape, F32))(c)


def _mod_slab(cond_all, w_ada, b_slab):
    def body(c_ref, w_ref, b_ref, o_ref):
        o_ref[...] = _dot(c_ref[...].astype(BF16), w_ref[0].astype(BF16), ((1,), (0,))) + b_ref[...]

    return pl.pallas_call(body, name="mod_slab",
                          out_shape=jax.ShapeDtypeStruct((cond_all.shape[0], w_ada.shape[2]), F32),
                          compiler_params=pltpu.CompilerParams(vmem_limit_bytes=VMEM_LIMIT))(cond_all, w_ada, b_slab)


def _ada_grad(cond_all, dmod_cols):
    def body(c_ref, d_ref, o_ref):
        o_ref[...] = _dot(c_ref[...].astype(BF16), d_ref[...].astype(BF16), ((0,), (0,)))

    return pl.pallas_call(body, name="ada_grad",
                          out_shape=jax.ShapeDtypeStruct((cond_all.shape[1], dmod_cols.shape[1]), F32),
                          compiler_params=pltpu.CompilerParams(vmem_limit_bytes=VMEM_LIMIT))(cond_all, dmod_cols)


def _adam_math(g, w, m, v):
    m2 = B1 * m + (1.0 - B1) * g
    v2 = B2 * v + (1.0 - B2) * (g * g)
    m_hat = m2 / (1.0 - B1 ** STEP)
    v_hat = v2 / (1.0 - B2 ** STEP)
    return -LR * (m_hat / (jnp.sqrt(v_hat) + ADAM_EPS) + WD * w), m2, v2


def _adamw(parts, w, m, v, name):
    n, rows, cols = parts.shape
    tr = _pick(rows, (128, 96, 64, 32, 16, 8))

    def body(p_ref, w_ref, m_ref, v_ref, g_ref, d_ref, m2_ref, v2_ref):
        g = p_ref[0].astype(F32)
        for s in range(1, n):
            g = g + p_ref[s].astype(F32)
        g_ref[0] = g
        d_ref[0], m2_ref[0], v2_ref[0] = _adam_math(g, w_ref[0], m_ref[0], v_ref[0])

    blk = pl.BlockSpec((1, tr, cols), lambda i: (0, i, 0))
    return pl.pallas_call(
        body, name=name, grid=(rows // tr,),
        in_specs=[pl.BlockSpec((n, tr, cols), lambda i: (0, i, 0)), blk, blk, blk],
        out_specs=[blk] * 4, out_shape=[jax.ShapeDtypeStruct((1, rows, cols), F32)] * 4,
        compiler_params=_cp("parallel"),
    )(parts, w, m, v)


ROW_PARAMS = (("g_norm1", D), ("g_cq", Q_LORA), ("g_ckv", KV_LORA), ("g_out_a", D_A), ("g_out_b", D_A), ("g_norm2", D),
              ("g_final", D))
LOSS_ROW = N_MOD + len(ROW_PARAMS)
PAY_ROWS = 16
NCOL = N_MOD * D // N_DEV


def _pack_small(dmods, rows, loss_cols):
    nb = dmods[0].shape[0]
    nrow = len(ROW_PARAMS)

    def body(*refs):
        dm, rw, loss_ref, pay_ref, blk_ref = refs[:N_MOD], refs[N_MOD:N_MOD + nrow], refs[N_MOD + nrow], refs[-2], refs[-1]
        pay_ref[...] = jnp.zeros_like(pay_ref)
        for k in range(N_MOD):
            tot = dm[k][0]
            for b in range(1, nb):
                tot = tot + dm[k][b]
            pay_ref[k:k + 1, :] = tot
        for i, (_, n) in enumerate(ROW_PARAMS):
            pay_ref[N_MOD + i:N_MOD + i + 1, :n] = rw[i][...]
        pay_ref[LOSS_ROW:LOSS_ROW + 1, :] = loss_ref[...]
        for j in range(N_DEV):
            done = 0
            while done < NCOL:
                seg, off = divmod(j * NCOL + done, D)
                ln = min(NCOL - done, D - off)
                for b in range(nb):
                    blk_ref[j, b:b + 1, done:done + ln] = dm[seg][b][:, off:off + ln]
                done += ln

    return pl.pallas_call(
        body, name="pack_small",
        out_shape=[jax.ShapeDtypeStruct((PAY_ROWS, D), F32), jax.ShapeDtypeStruct((N_DEV, nb, NCOL), F32)],
    )(*dmods, *rows, loss_cols)


def _small_update(pay, rel, ws, ms, vs):
    n_par = len(ws)

    def body(*refs):
        pay_ref, rel_ref = refs[:2]
        w_refs, m_refs, v_refs = (refs[2 + s * n_par:2 + (s + 1) * n_par] for s in range(3))
        outs, loss_ref = refs[2 + 3 * n_par:-1], refs[-1]
        tot, rtot = pay_ref[0], rel_ref[0]
        for s in range(1, N_DEV):
            tot, rtot = tot + pay_ref[s], rtot + rel_ref[s]

        def update(p, g, sl):
            outs[4 * p][:, sl] = g
            outs[4 * p + 1][:, sl], outs[4 * p + 2][:, sl], outs[4 * p + 3][:, sl] = _adam_math(
                g, w_refs[p][:, sl], m_refs[p][:, sl], v_refs[p][:, sl])

        for k in range(N_MOD):
            update(0, tot[k:k + 1, :], slice(k * D, (k + 1) * D))
        for i, (_, n) in enumerate(ROW_PARAMS):
            update(1 + i, tot[N_MOD + i:N_MOD + i + 1, :n], slice(0, n))
        update(n_par - 1, rtot, slice(0, H))
        loss_ref[...] = jnp.broadcast_to((0.5 / D) * jnp.sum(tot[LOSS_ROW:LOSS_ROW + 1, :]), loss_ref.shape)

    shapes = [jax.ShapeDtypeStruct(w.shape, F32) for w in ws for _ in range(4)]
    res = pl.pallas_call(
        body, name="small_update", out_shape=shapes + [jax.ShapeDtypeStruct((8, 128), F32)],
    )(pay, rel, *ws, *ms, *vs)
    return [tuple(res[4 * p:4 * p + 4]) for p in range(n_par)], res[-1]


def _cols_from_blocks(g):
    return jnp.transpose(g, (1, 0, 2)).reshape(g.shape[1], N_DEV * g.shape[2])


def _cols_to_blocks(w):
    r, c = w.shape
    return jnp.transpose(w.reshape(r, N_DEV, c // N_DEV), (1, 0, 2))


def _pad_w_in(w):
    z = jnp.zeros((w.shape[0], NOPE), w.dtype)
    return jnp.concatenate([w[:, :P_IN - ROPE], z, w[:, P_IN - ROPE:], z[:, :HP - NOPE - ROPE]], axis=1)


def _unpad_w_in(g):
    k0 = P_IN - ROPE + NOPE
    return jnp.concatenate([g[:, :P_IN - ROPE], g[:, k0:k0 + ROPE]], axis=1)


def _pad_w_uq(w):
    w3 = w.reshape(Q_LORA, H, NOPE + ROPE)
    return jnp.pad(w3, ((0, 0), (0, 0), (0, HP - NOPE - ROPE))).reshape(Q_LORA, H * HP)


def _unpad_w_uq(g):
    return g.reshape(Q_LORA, H, HP)[:, :, :NOPE + ROPE].reshape(Q_LORA, H * (NOPE + ROPE))


def _split_w_ukv(w):
    w4 = w.reshape(KV_LORA, H // 2, 2, HP)
    z = jnp.zeros((KV_LORA, H // 2, NOPE), w.dtype)
    kn, vv = w4[..., :NOPE], w4[..., NOPE:]
    w_k = jnp.stack([jnp.concatenate([kn[:, :, 0], z], -1), jnp.concatenate([kn[:, :, 1], z], -1)], axis=2)
    w_v = jnp.stack([jnp.concatenate([vv[:, :, 0], z], -1), jnp.concatenate([z, vv[:, :, 1]], -1)], axis=2)
    return w_k.reshape(KV_LORA, H * HP), w_v.reshape(KV_LORA, H * HP)


def _join_w_ukv(g_k, g_v):
    gk = g_k.reshape(KV_LORA, H // 2, 2, HP)
    gv = g_v.reshape(KV_LORA, H // 2, 2, HP)
    even = jnp.concatenate([gk[:, :, 0, :NOPE], gv[:, :, 0, :VDIM]], -1)
    odd = jnp.concatenate([gk[:, :, 1, :NOPE], gv[:, :, 1, VDIM:]], -1)
    return jnp.stack([even, odd], axis=2).reshape(KV_LORA, H * HP)


def _rope_tables():
    half = ROPE // 2
    inv = ROPE_THETA ** (-jnp.arange(half, dtype=F32) / half)
    ang = jnp.arange(S, dtype=F32)[:, None] * inv[None, :]
    cos, sin = jnp.cos(ang), jnp.sin(ang)
    ones, zeros = jnp.ones((S, NOPE), F32), jnp.zeros((S, NOPE), F32)
    tail1, tail0 = jnp.ones((S, HP - NOPE - ROPE), F32), jnp.zeros((S, HP - NOPE - ROPE), F32)
    zh = jnp.zeros((S, half), F32)
    c = jnp.concatenate([ones, cos, cos, tail1], axis=1)
    sm = jnp.concatenate([zeros, -sin, zh, tail0], axis=1)
    sp = jnp.concatenate([zeros, zh, sin, tail0], axis=1)
    return c, sm, sp


def _local_step(x, mod, target, g_norm1, w_in_p, g_cq, w_uq_p, g_ckv, w_k, w_v, rel_bias, g_out_a, g_out_b, w_out,
                g_norm2, w_ffn_in, w_ffn_out, g_final, late_weights=None, on_ffn_grads=None, on_last_grads=None):
    nb = x.shape[0] // S
    sh1, sc1, g1, sh2, sc2, g2 = (mod[:, n].reshape(nb, 1, D) for n in range(N_MOD))
    rc, rsm, rsp = _rope_tables()
    biasm = _band_bias(rel_bias)

    h1 = _pre1(x, g_norm1, sc1, sh1)
    proj = _mm_nn(h1, w_in_p, F32, "proj")
    q, k, v, cqn, ckvn = _mla_pre(proj, g_cq, g_ckv, w_uq_p, w_k, w_v, rc, rsm, rsp)
    out_b, lse_b = _mla_fwd(q, k, v)
    out_a, lse_a = _dil_fwd(proj, biasm)
    y = _post_attn(out_a, out_b, g_out_a, g_out_b)
    if late_weights is not None:
        w_out, w_ffn_in, w_ffn_out = late_weights(y)
    mix = _mm_nn(y, w_out, BF16, "mix")
    x2, h2 = _resid_norm2(x, mix, g1, g_norm2, sc2, sh2)
    ffn_g, ffn_u, act = _ffn_in(h2, w_ffn_in)
    f = _mm_nn(act, w_ffn_out, BF16, "ffn_out")
    dx3, df, loss_cols, dg_final, dg2 = _final(x2, f, g2, g_final, target)

    dg_, du_ = _d_act(df, w_ffn_out, ffn_g, ffn_u)
    gw_ffn_out = _mm_tn(act, [df], "gw_ffn_out")
    dh2 = _d_h2(dg_, du_, w_ffn_in)
    gw_ffn_in = _mm_tn(h2, [dg_, du_], "gw_ffn_in")
    dx2, dsh2, dsc2, dg_norm2, dg1, dmix = _norm_bwd(x2, dh2, dx3, g_norm2, sc2, gate=(mix, g1))
    dy = _mm_nt(dmix, w_out, BF16, "d_y")
    gw_out = _mm_tn(y, [dmix], "gw_out")
    if on_ffn_grads is not None:
        g_out_a = g_out_a + on_ffn_grads(gw_ffn_in, gw_ffn_out, gw_out)
    dout_a, dout_b, dg_out_a, dg_out_b = _post_attn_bwd(dy, out_a, out_b, g_out_a, g_out_b)
    dq_b, dk_b, dv_b = _mla_bwd(q, k, v, out_b, dout_b, lse_b)
    dq_a, dk_a, dv_a, dlogits = _dil_bwd(proj, biasm, out_a, dout_a, lse_a)
    g_rel = _rel_bias_grad(dlogits)
    dqr, dkr, dvr, dtail, dg_cq, dg_ckv = _mla_pre_bwd(proj, dq_b, dk_b, dv_b, g_cq, g_ckv, w_uq_p, w_k, w_v, rc, rsm, rsp)
    gw_uq = _mm_tn(cqn, [dqr], "gw_uq")
    gw_k, gw_v = _mm_tn(ckvn, [dkr, dvr], "gw_kv")
    dproj = jnp.concatenate([dq_a, dk_a, dv_a, dtail], axis=1)
    gw_in = _mm_tn(h1, [dproj], "gw_in")
    if on_last_grads is not None:
        started = on_last_grads(dict(w_in=gw_in, w_uq=gw_uq, w_k=gw_k, w_v=gw_v))
    else:
        started = None
    dh1 = _mm_nt(dproj, w_in_p, BF16, "d_h1", after=started)
    grad_x, dsh1, dsc1, dg_norm1 = _norm_bwd(x, dh1, dx2, g_norm1, sc1)

    dmod = [dsh1, dsc1, dg1, dsh2, dsc2, dg2]
    small = dict(g_norm1=dg_norm1, g_cq=dg_cq, g_ckv=dg_ckv, rel_bias=g_rel, g_out_a=dg_out_a, g_out_b=dg_out_b,
                 g_norm2=dg_norm2, g_final=dg_final)
    big = dict(w_in=gw_in, w_uq=gw_uq, w_k=gw_k, w_v=gw_v, w_out=gw_out, w_ffn_in=gw_ffn_in, w_ffn_out=gw_ffn_out)
    return grad_x, dmod, loss_cols, small, big


def kernel(x, c, w_ada, b_ada, g_norm1, w_in, g_cq, w_uq, g_ckv, w_ukv, rel_bias, g_out_a, g_out_b, w_out, g_norm2, w_ffn_in, w_ffn_out, g_final, loss_target, m_w_ada, m_b_ada, m_g_norm1, m_w_in, m_g_cq, m_w_uq, m_g_ckv, m_w_ukv, m_rel_bias, m_g_out_a, m_g_out_b, m_w_out, m_g_norm2, m_w_ffn_in, m_w_ffn_out, m_g_final, v_w_ada, v_b_ada, v_g_norm1, v_w_in, v_g_cq, v_w_uq, v_g_ckv, v_w_ukv, v_rel_bias, v_g_out_a, v_g_out_b, v_w_out, v_g_norm2, v_w_ffn_in, v_w_ffn_out, v_g_final):
    nb = x.shape[0]
    t = nb * S
    xt, tt = x.reshape(t, D), loss_target.reshape(t, D)
    me = 4 * lax.axis_index("x") + 2 * lax.axis_index("y") + lax.axis_index("c")

    early = [w_in[0], w_uq[0], w_ukv[0]]
    gathered = _gather_two_level([_silu_rows(c)] + [s.astype(BF16) for s in early], "gather_weights")
    cond_all = gathered[0].reshape(N_DEV * nb, D)
    w_in_f, w_uq_f, w_ukv_f = (_cols_from_blocks(g) for g in gathered[1:4])
    w_k, w_v = _split_w_ukv(w_ukv_f)

    ncol = N_MOD * D // N_DEV
    b_slab = lax.dynamic_slice(b_ada, (0, me * ncol), (1, ncol))
    slab = _mod_slab(cond_all, w_ada, b_slab)
    (mod_rows,) = _exchange([slab.reshape(N_DEV, nb, ncol)], [False], "scatter_mod")
    mod = jnp.transpose(mod_rows, (1, 0, 2)).reshape(nb, N_MOD, D)

    late = [s.astype(BF16) for s in (w_out[0], w_ffn_in[0], w_ffn_out[0])]
    late_send, late_recv, late_src, late_land, late_token = _exchange_start(
        late, [_own_block_in_place(s, me) for s in late], [True] * 3, mod_rows, "gather_late_start")
    g_norm1_t = g_norm1 + late_token[:1, :1]

    def late_weights(after):
        w_out_g, w_ffn_in_g, w_ffn_out_g = _exchange_wait(late_send, late_recv, late_src, late_land, [True] * 3, after,
                                                          "gather_late_wait")
        return w_out_g.reshape(D, D), _cols_from_blocks(w_ffn_in_g), w_ffn_out_g.reshape(D_FF, D)

    flight = {}

    def start_grads(key, src, name):
        land = [_own_block_in_place(lax.dynamic_index_in_dim(s, me, 0, keepdims=False), me) for s in src]
        send, recv, src, land, token = _exchange_start(src, land, [False] * len(src), src[0], name)
        flight[key] = (send, recv, src, land)
        return token[:1, :1]

    def half_blocks(g):
        return jnp.transpose(g.reshape(D, N_DEV // 2, 2 * D_FF // N_DEV), (1, 0, 2))

    def on_ffn_grads(gw_ffn_in, gw_ffn_out, gw_out):
        return start_grads("ffn", [jnp.concatenate([half_blocks(g) for g in gw_ffn_in], axis=0),
                                   gw_ffn_out.reshape(N_DEV, D_FF // N_DEV, D), gw_out.reshape(N_DEV, D // N_DEV, D)],
                           "exchange_ffn_start")

    def on_last_grads(gw):
        return start_grads("rest", [_cols_to_blocks(_unpad_w_in(gw["w_in"])), _cols_to_blocks(_unpad_w_uq(gw["w_uq"])),
                                    _cols_to_blocks(_join_w_ukv(gw["w_k"], gw["w_v"]))], "exchange_rest_start")

    grad_x, dmod, loss_cols, small, _ = _local_step(
        xt, mod, tt, g_norm1_t, _pad_w_in(w_in_f), g_cq, _pad_w_uq(w_uq_f), g_ckv, w_k, w_v, rel_bias, g_out_a, g_out_b,
        None, g_norm2, None, None, g_final.reshape(1, D), late_weights=late_weights, on_ffn_grads=on_ffn_grads,
        on_last_grads=on_last_grads)

    upd = {}

    def land_and_update(key, names, after, name):
        got = _exchange_wait(*flight[key], [False] * len(names), after, name)
        for n, p in zip(names, got):
            w, m, v = big[n]
            upd[n] = _adamw(p, w, m, v, "adamw_" + n)

    big = dict(w_in=(w_in, m_w_in, v_w_in), w_uq=(w_uq, m_w_uq, v_w_uq), w_ukv=(w_ukv, m_w_ukv, v_w_ukv),
               w_out=(w_out, m_w_out, v_w_out), w_ffn_in=(w_ffn_in, m_w_ffn_in, v_w_ffn_in),
               w_ffn_out=(w_ffn_out, m_w_ffn_out, v_w_ffn_out))
    land_and_update("ffn", ["w_ffn_in", "w_ffn_out", "w_out"], grad_x, "exchange_ffn_wait")
    land_and_update("rest", ["w_in", "w_uq", "w_ukv"], upd["w_out"][0], "exchange_rest_wait")

    mine, dmod_blocks = _pack_small(dmod, [small[n] for n, _ in ROW_PARAMS], loss_cols)
    dmod_cols, pay, rel = _exchange([dmod_blocks, mine, small["rel_bias"]], [False, True, True], "exchange_small",
                                    after=upd["w_ukv"][0])
    g_ada = _ada_grad(cond_all, dmod_cols.reshape(N_DEV * nb, ncol))
    upd["w_ada"] = _adamw(g_ada[None], w_ada, m_w_ada, v_w_ada, "adamw_w_ada")
    row = lambda a: a.reshape(1, D)
    small_names = ["b_ada"] + [n for n, _ in ROW_PARAMS] + ["rel_bias"]
    small_w = [b_ada, g_norm1, g_cq, g_ckv, g_out_a, g_out_b, g_norm2, row(g_final), rel_bias]
    small_m = [m_b_ada, m_g_norm1, m_g_cq, m_g_ckv, m_g_out_a, m_g_out_b, m_g_norm2, row(m_g_final), m_rel_bias]
    small_v = [v_b_ada, v_g_norm1, v_g_cq, v_g_ckv, v_g_out_a, v_g_out_b, v_g_norm2, row(v_g_final), v_rel_bias]
    small_upd, loss8 = _small_update(pay, rel, small_w, small_m, small_v)
    upd.update(zip(small_names, small_upd))

    order = ["w_ada", "b_ada", "g_norm1", "w_in", "g_cq", "w_uq", "g_ckv", "w_ukv", "rel_bias", "g_out_a", "g_out_b",
             "w_out", "g_norm2", "w_ffn_in", "w_ffn_out", "g_final"]
    like = dict(g_final=g_final)
    outs = [loss8[0, 0], grad_x.reshape(x.shape)]
    for part in range(4):
        for n in order:
            val = upd[n][part]
            outs.append(val.reshape(like[n].shape) if n in like else val)
    return tuple(outs)
```

```python
import functools

import numpy as np
import jax
import jax.numpy as jnp
from jax import lax
from jax.experimental import pallas as pl
from jax.experimental.pallas import tpu as pltpu

F32, BF16 = jnp.float32, jnp.bfloat16

N_DEV = 8
D = 1024
S = 2048
H = 8
E_A = 64
D_A = H * E_A
Q_LORA, KV_LORA = 384, 256
NOPE, ROPE, VDIM = 64, 32, 64
HP = 128
P_IN = 3 * D_A + Q_LORA + KV_LORA + ROPE
P_PAD = 3 * D_A + Q_LORA + KV_LORA + HP
TAIL0 = 3 * D_A
TAIL = P_PAD - TAIL0
D_FF = 2816
N_MOD = 6
EPS = 1e-6
NEG = -1e30
BLK = 128
DILATIONS = (1, 4, 16)
N_BUCKETS, MAX_DISTANCE = 32, 2048
ROPE_THETA = 10000.0
SCALE_A = E_A ** -0.5
SCALE_B = (NOPE + ROPE) ** -0.5
B1, B2, LR, ADAM_EPS, WD, STEP = 0.9, 0.999, 0.001, 1e-8, 0.01, 10
VMEM_LIMIT = 56 * 1024 * 1024


def _cp(*sem):
    return pltpu.CompilerParams(dimension_semantics=sem, vmem_limit_bytes=VMEM_LIMIT)


def _pick(n, prefs):
    for p in prefs:
        if n % p == 0:
            return p
    raise ValueError(f"no tile of {prefs} divides {n}")


OPERAND_BYTES = 6 * 1024 * 1024


def _pick_rows(m, k):
    return _pick(m, [p for p in (1024, 512, 256, 128, 16) if p * k * 2 <= OPERAND_BYTES])


MATMUL_BYTES = 40 * 1024 * 1024


def _stream_rows(m, fixed, per_row):
    return _pick(m, [p for p in (4096, 2048, 1024, 512, 256, 128, 16) if fixed + p * per_row <= MATMUL_BYTES])


def _dot(a, b, dims):
    return lax.dot_general(a, b, (dims, ((), ())), preferred_element_type=F32)


def _mm_nn(a, b, out_dtype, name):
    m, k = a.shape
    n = b.shape[1]
    tn = _pick(n, (512, 384, 256, 128))
    tm = _stream_rows(m, 4 * k * tn, 4 * k + (2 * jnp.dtype(out_dtype).itemsize + 4) * tn)

    def body(a_ref, b_ref, o_ref):
        o_ref[...] = _dot(a_ref[...], b_ref[...], ((1,), (0,))).astype(o_ref.dtype)

    return pl.pallas_call(
        body, name=name, grid=(m // tm, n // tn),
        in_specs=[pl.BlockSpec((tm, k), lambda i, j: (i, 0)), pl.BlockSpec((k, tn), lambda i, j: (0, j))],
        out_specs=pl.BlockSpec((tm, tn), lambda i, j: (i, j)),
        out_shape=jax.ShapeDtypeStruct((m, n), out_dtype),
        compiler_params=_cp("parallel", "parallel"),
    )(a, b)


def _mm_nt(a, b, out_dtype, name, after=None):
    m, k = a.shape
    n = b.shape[0]
    tn = _pick(n, (512, 384, 256, 128))
    tm = _stream_rows(m, 4 * k * tn, 4 * k + (2 * jnp.dtype(out_dtype).itemsize + 4) * tn)

    def body(a_ref, b_ref, *rest):
        o_ref = rest[-1]
        o_ref[...] = _dot(a_ref[...], b_ref[...], ((1,), (1,))).astype(o_ref.dtype)

    extra = [] if after is None else [after]
    return pl.pallas_call(
        body, name=name, grid=(m // tm, n // tn),
        in_specs=[pl.BlockSpec((tm, k), lambda i, j: (i, 0)), pl.BlockSpec((tn, k), lambda i, j: (j, 0))] + [ANY] * len(extra),
        out_specs=pl.BlockSpec((tm, tn), lambda i, j: (i, j)),
        out_shape=jax.ShapeDtypeStruct((m, n), out_dtype),
        compiler_params=_cp("parallel", "parallel"),
    )(a, b, *extra)


def _mm_tn(a, bs, name):
    t, m = a.shape
    n = bs[0].shape[1]
    nb_ = len(bs)
    tc = _pick(t, (512, 16))
    tn = _pick(n, (512, 384, 256, 128))
    tm = _pick(m, [p for p in (1024, 512, 384, 256, 128) if (3 * p + 2 * nb_ * tn) * t * 2 <= VMEM_LIMIT - 2 * OPERAND_BYTES])
    if tm <= 256 and nb_ * n * t * 2 <= 2 * OPERAND_BYTES:
        tn = n

    def body(*refs):
        a_ref, b_refs, o_refs, at_ref = refs[0], refs[1:1 + nb_], refs[1 + nb_:1 + 2 * nb_], refs[-1]

        @pl.when(pl.program_id(1) == 0)
        def _():
            def chunk(c, _):
                rows = pl.ds(pl.multiple_of(c * tc, tc), tc)
                at_ref[:, rows] = a_ref[rows, :].T
                return 0

            lax.fori_loop(0, t // tc, chunk, 0)

        for b_ref, o_ref in zip(b_refs, o_refs):
            o_ref[...] = _dot(at_ref[...], b_ref[...], ((1,), (0,))).astype(BF16)

    res = pl.pallas_call(
        body, name=name, grid=(m // tm, n // tn),
        in_specs=[pl.BlockSpec((t, tm), lambda i, j: (0, i))] + [pl.BlockSpec((t, tn), lambda i, j: (0, j))] * nb_,
        out_specs=[pl.BlockSpec((tm, tn), lambda i, j: (i, j))] * nb_,
        out_shape=[jax.ShapeDtypeStruct((m, n), BF16)] * nb_,
        scratch_shapes=[pltpu.VMEM((tm, t), BF16)],
        compiler_params=_cp("parallel", "arbitrary"),
    )(a, *bs)
    return res[0] if nb_ == 1 else res


EPI = 256


def _silu_parts(g):
    sg = 0.5 * jnp.tanh(0.5 * g) + 0.5
    return sg, g * sg


def _ffn_in(h2, w):
    t, k = h2.shape
    tn = _pick(D_FF, (256, 128))
    tm = _stream_rows(t, 8 * k * tn, 4 * k + (3 * 2 * 2 + 2 * 4) * tn)
    nj = D_FF // tn

    def body(h_ref, wg_ref, wu_ref, g_ref, u_ref, a_ref):
        hv = h_ref[...]
        g_all = _dot(hv, wg_ref[...], ((1,), (0,)))
        u_all = _dot(hv, wu_ref[...], ((1,), (0,)))
        for r in range(tm // EPI):
            rows = slice(r * EPI, (r + 1) * EPI)
            g, u = g_all[rows], u_all[rows]
            g_ref[rows, :] = g.astype(BF16)
            u_ref[rows, :] = u.astype(BF16)
            a_ref[rows, :] = (_silu_parts(g)[1] * u).astype(BF16)

    blk = pl.BlockSpec((tm, tn), lambda i, j: (i, j))
    return pl.pallas_call(
        body, name="ffn_in", grid=(t // tm, nj),
        in_specs=[pl.BlockSpec((tm, k), lambda i, j: (i, 0)), pl.BlockSpec((k, tn), lambda i, j: (0, j)),
                  pl.BlockSpec((k, tn), lambda i, j: (0, j + nj))],
        out_specs=[blk] * 3, out_shape=[jax.ShapeDtypeStruct((t, D_FF), BF16)] * 3,
        compiler_params=_cp("parallel", "parallel"),
    )(h2, w, w)


def _d_act(df, w, g, u):
    t, k = df.shape
    tn = _pick(D_FF, (256, 128))
    tm = _stream_rows(t, 4 * k * tn, 4 * k + (4 * 2 * 2 + 4) * tn)

    def body(df_ref, w_ref, g_ref, u_ref, dg_ref, du_ref):
        da_all = _dot(df_ref[...], w_ref[...], ((1,), (1,)))
        for r in range(tm // EPI):
            rows = slice(r * EPI, (r + 1) * EPI)
            da = da_all[rows]
            gv = g_ref[rows, :].astype(F32)
            sg, silu = _silu_parts(gv)
            dg_ref[rows, :] = ((da * u_ref[rows, :].astype(F32)) * (sg + silu * (1.0 - sg))).astype(BF16)
            du_ref[rows, :] = (da * silu).astype(BF16)

    blk = pl.BlockSpec((tm, tn), lambda i, j: (i, j))
    return pl.pallas_call(
        body, name="d_act", grid=(t // tm, D_FF // tn),
        in_specs=[pl.BlockSpec((tm, k), lambda i, j: (i, 0)), pl.BlockSpec((tn, k), lambda i, j: (j, 0)), blk, blk],
        out_specs=[blk] * 2, out_shape=[jax.ShapeDtypeStruct((t, D_FF), BF16)] * 2,
        compiler_params=_cp("parallel", "parallel"),
    )(df, w, g, u)


def _d_h2(dg, du, w):
    t = dg.shape[0]
    n = w.shape[0]
    tm, tn = _pick_rows(t, D_FF), _pick(n, (512, 256, 128))

    def body(dg_ref, du_ref, wg_ref, wu_ref, o_ref):
        o_ref[...] = (_dot(dg_ref[...], wg_ref[...], ((1,), (1,)))
                      + _dot(du_ref[...], wu_ref[...], ((1,), (1,)))).astype(BF16)

    return pl.pallas_call(
        body, name="d_h2", grid=(t // tm, n // tn),
        in_specs=[pl.BlockSpec((tm, D_FF), lambda i, j: (i, 0)), pl.BlockSpec((tm, D_FF), lambda i, j: (i, 0)),
                  pl.BlockSpec((tn, D_FF), lambda i, j: (j, 0)), pl.BlockSpec((tn, D_FF), lambda i, j: (j, 1))],
        out_specs=pl.BlockSpec((tm, tn), lambda i, j: (i, j)),
        out_shape=jax.ShapeDtypeStruct((t, n), BF16),
        compiler_params=_cp("parallel", "parallel"),
    )(dg, du, w, w)


TM = 256


def _row(w):
    return pl.BlockSpec((TM, w), lambda i: (i, 0))


def _row_at(w, col):
    return pl.BlockSpec((TM, w), lambda i: (i, col))


def _vec(w):
    return pl.BlockSpec((1, w), lambda i: (0, 0))


def _per_ex(w):
    return pl.BlockSpec((1, 1, w), lambda i: (i // (S // TM), 0, 0))


def _pos(w):
    return pl.BlockSpec((TM, w), lambda i: (i % (S // TM), 0))


def _full(shape):
    return pl.BlockSpec(shape, lambda i: (0,) * len(shape))


def _rms(x):
    return lax.rsqrt(jnp.mean(x * x, axis=-1, keepdims=True) + EPS)


def _rms_bwd(n, r, dn):
    return r * (dn - n * jnp.mean(dn * n, axis=-1, keepdims=True))


def _colsum(v):
    return jnp.sum(v, axis=0, keepdims=True)


def _acc_first(i, ref, val, every=None):
    first = (i == 0) if every is None else (i % every == 0)

    @pl.when(first)
    def _():
        ref[...] = jnp.zeros_like(ref)

    ref[...] += val.reshape(ref.shape)


def _pre1(x, g, sc, sh):
    t = x.shape[0]

    def body(x_ref, g_ref, sc_ref, sh_ref, h_ref):
        xv = x_ref[...]
        n = xv * _rms(xv)
        h_ref[...] = ((n * g_ref[...]) * (1.0 + sc_ref[0]) + sh_ref[0]).astype(BF16)

    return pl.pallas_call(
        body, name="pre1", grid=(t // TM,),
        in_specs=[_row(D), _vec(D), _per_ex(D), _per_ex(D)],
        out_specs=_row(D), out_shape=jax.ShapeDtypeStruct((t, D), BF16),
        compiler_params=_cp("parallel"),
    )(x, g, sc, sh)


def _rope_fwd(v, c, sm, sp):
    return v * c + pltpu.roll(v, HP - ROPE // 2, 1) * sm + pltpu.roll(v, ROPE // 2, 1) * sp


def _rope_bwd(dv, c, sm, sp):
    return dv * c + pltpu.roll(dv * sm, ROPE // 2, 1) + pltpu.roll(dv * sp, HP - ROPE // 2, 1)


def _mla_pre(proj, g_cq, g_ckv, w_uq, w_k, w_v, rc, rsm, rsp):
    t = proj.shape[0]

    def body(tail_ref, gq_ref, gkv_ref, wuq_ref, wk_ref, wv_ref, c_ref, sm_ref, sp_ref,
             q_ref, k_ref, v_ref, cqn_ref, ckvn_ref):
        tail = tail_ref[...]
        cq, ckv, kr = tail[:, :Q_LORA], tail[:, Q_LORA:Q_LORA + KV_LORA], tail[:, Q_LORA + KV_LORA:]
        cqn = (cq * _rms(cq) * gq_ref[...]).astype(BF16)
        ckvn = (ckv * _rms(ckv) * gkv_ref[...]).astype(BF16)
        cqn_ref[...] = cqn
        ckvn_ref[...] = ckvn
        c, sm, sp = c_ref[...], sm_ref[...], sp_ref[...]
        q = _dot(cqn, wuq_ref[...], ((1,), (0,)))
        kn = _dot(ckvn, wk_ref[...], ((1,), (0,)))
        v_ref[...] = _dot(ckvn, wv_ref[...], ((1,), (0,))).astype(BF16)
        krr = _rope_fwd(kr, c, sm, sp)
        for h in range(H):
            sl = slice(h * HP, (h + 1) * HP)
            q_ref[:, sl] = _rope_fwd(q[:, sl], c, sm, sp).astype(BF16)
            k_ref[:, sl] = (kn[:, sl] + krr).astype(BF16)

    wide = H * HP
    return pl.pallas_call(
        body, name="mla_pre", grid=(t // TM,),
        in_specs=[_row_at(TAIL, TAIL0 // TAIL), _vec(Q_LORA), _vec(KV_LORA), _full((Q_LORA, wide)),
                  _full((KV_LORA, wide)), _full((KV_LORA, wide)), _pos(HP), _pos(HP), _pos(HP)],
        out_specs=[_row(wide), _row(wide), _row(wide), _row(Q_LORA), _row(KV_LORA)],
        out_shape=[jax.ShapeDtypeStruct((t, wide), BF16)] * 3
        + [jax.ShapeDtypeStruct((t, Q_LORA), BF16), jax.ShapeDtypeStruct((t, KV_LORA), BF16)],
        compiler_params=_cp("parallel"),
    )(proj, g_cq, g_ckv, w_uq, w_k, w_v, rc, rsm, rsp)


def _mla_pre_bwd(proj, dq_, dk_, dv_, dqkv_a, g_cq, g_ckv, w_uq, w_k, w_v, rc, rsm, rsp):
    t = proj.shape[0]
    wide = H * HP

    def body(tail_ref, dq_ref, dk_ref, dv_ref, dqa_ref, dka_ref, dva_ref, gq_ref, gkv_ref, wuq_ref, wk_ref, wv_ref,
             c_ref, sm_ref, sp_ref, dqo_ref, dproj_ref, dgq_ref, dgkv_ref):
        i = pl.program_id(0)
        for n, src in enumerate((dqa_ref, dka_ref, dva_ref)):
            dproj_ref[:, n * D_A:(n + 1) * D_A] = src[...]
        dtail_ref = dproj_ref.at[:, TAIL0:]
        tail = tail_ref[...]
        cq, ckv = tail[:, :Q_LORA], tail[:, Q_LORA:Q_LORA + KV_LORA]
        c, sm, sp = c_ref[...], sm_ref[...], sp_ref[...]
        dkr = jnp.zeros((TM, HP), F32)
        for h in range(H):
            sl = slice(h * HP, (h + 1) * HP)
            dqo_ref[:, sl] = _rope_bwd(dq_ref[:, sl].astype(F32), c, sm, sp).astype(BF16)
            dkr = dkr + dk_ref[:, sl].astype(F32)
        lane = lax.broadcasted_iota(jnp.int32, (TM, HP), 1)
        dkr = jnp.where((lane >= NOPE) & (lane < NOPE + ROPE), _rope_bwd(dkr, c, sm, sp), 0.0)
        dkb = dk_ref[...]
        dvb = dv_ref[...]
        dcqn = _dot(dqo_ref[...], wuq_ref[...], ((1,), (1,)))
        dckvn = _dot(dkb, wk_ref[...], ((1,), (1,))) + _dot(dvb, wv_ref[...], ((1,), (1,)))
        rq, rkv = _rms(cq), _rms(ckv)
        nq, nkv = cq * rq, ckv * rkv
        _acc_first(i, dgq_ref, _colsum(dcqn * nq))
        _acc_first(i, dgkv_ref, _colsum(dckvn * nkv))
        dtail_ref[:, :Q_LORA] = _rms_bwd(nq, rq, dcqn * gq_ref[...]).astype(BF16)
        dtail_ref[:, Q_LORA:Q_LORA + KV_LORA] = _rms_bwd(nkv, rkv, dckvn * gkv_ref[...]).astype(BF16)
        dtail_ref[:, Q_LORA + KV_LORA:] = dkr.astype(BF16)

    return pl.pallas_call(
        body, name="mla_pre_bwd", grid=(t // TM,),
        in_specs=[_row_at(TAIL, TAIL0 // TAIL), _row(wide), _row(wide), _row(wide), _row(D_A), _row(D_A), _row(D_A),
                  _vec(Q_LORA), _vec(KV_LORA), _full((Q_LORA, wide)), _full((KV_LORA, wide)), _full((KV_LORA, wide)),
                  _pos(HP), _pos(HP), _pos(HP)],
        out_specs=[_row(wide), _row(P_PAD), _vec(Q_LORA), _vec(KV_LORA)],
        out_shape=[jax.ShapeDtypeStruct((t, wide), BF16), jax.ShapeDtypeStruct((t, P_PAD), BF16),
                   jax.ShapeDtypeStruct((1, Q_LORA), F32), jax.ShapeDtypeStruct((1, KV_LORA), F32)],
        compiler_params=_cp("arbitrary"),
    )(proj, dq_, dk_, dv_, *dqkv_a, g_cq, g_ckv, w_uq, w_k, w_v, rc, rsm, rsp)


def _post_attn(out_a, out_b, g_a, g_b):
    t = out_a.shape[0]

    def body(a_ref, b_ref, ga_ref, gb_ref, y_ref):
        a, b = a_ref[...], b_ref[...]
        y_ref[:, :D_A] = (a * _rms(a) * ga_ref[...]).astype(BF16)
        y_ref[:, D_A:] = (b * _rms(b) * gb_ref[...]).astype(BF16)

    return pl.pallas_call(
        body, name="post_attn", grid=(t // TM,),
        in_specs=[_row(D_A), _row(D_A), _vec(D_A), _vec(D_A)],
        out_specs=_row(D), out_shape=jax.ShapeDtypeStruct((t, D), BF16),
        compiler_params=_cp("parallel"),
    )(out_a, out_b, g_a, g_b)


def _post_attn_bwd(dy, out_a, out_b, g_a, g_b):
    t = dy.shape[0]

    def body(dy_ref, a_ref, b_ref, ga_ref, gb_ref, da_ref, db_ref, dga_ref, dgb_ref):
        i = pl.program_id(0)
        dy_ = dy_ref[...].astype(F32)
        for src, g_ref, dst, dg_ref, sl in ((a_ref, ga_ref, da_ref, dga_ref, slice(0, D_A)),
                                            (b_ref, gb_ref, db_ref, dgb_ref, slice(D_A, D))):
            v = src[...]
            r = _rms(v)
            n = v * r
            dyv = dy_[:, sl]
            _acc_first(i, dg_ref, _colsum(dyv * n))
            dst[...] = _rms_bwd(n, r, dyv * g_ref[...])

    return pl.pallas_call(
        body, name="post_attn_bwd", grid=(t // TM,),
        in_specs=[_row(D), _row(D_A), _row(D_A), _vec(D_A), _vec(D_A)],
        out_specs=[_row(D_A), _row(D_A), _vec(D_A), _vec(D_A)],
        out_shape=[jax.ShapeDtypeStruct((t, D_A), F32)] * 2 + [jax.ShapeDtypeStruct((1, D_A), F32)] * 2,
        compiler_params=_cp("arbitrary"),
    )(dy, out_a, out_b, g_a, g_b)


def _resid_norm2(x, mix, g1, g, sc, sh):
    t = x.shape[0]

    def body(x_ref, mix_ref, g1_ref, g_ref, sc_ref, sh_ref, x2_ref, h_ref):
        x2 = x_ref[...] + g1_ref[0] * mix_ref[...]
        x2_ref[...] = x2
        n = x2 * _rms(x2)
        h_ref[...] = ((n * g_ref[...]) * (1.0 + sc_ref[0]) + sh_ref[0]).astype(BF16)

    return pl.pallas_call(
        body, name="resid_norm2", grid=(t // TM,),
        in_specs=[_row(D), _row(D), _per_ex(D), _vec(D), _per_ex(D), _per_ex(D)],
        out_specs=[_row(D), _row(D)],
        out_shape=[jax.ShapeDtypeStruct((t, D), F32), jax.ShapeDtypeStruct((t, D), BF16)],
        compiler_params=_cp("parallel"),
    )(x, mix, g1, g, sc, sh)


def _sigmoid(v):
    return 1.0 / (1.0 + jnp.exp(-v))


def _final(x2, f, g2, g_fin, target):
    t = x2.shape[0]
    nb = t // S
    tpb = S // TM

    def body(x2_ref, f_ref, g2_ref, g_ref, t_ref, dx3_ref, df_ref, loss_ref, dgf_ref, dg2_ref):
        i = pl.program_id(0)
        fv = f_ref[...].astype(F32)
        x3 = x2_ref[...] + g2_ref[0] * fv
        r = _rms(x3)
        n = x3 * r
        err = n * g_ref[...] - t_ref[...]
        _acc_first(i, loss_ref, _colsum(err * err))
        dy = err * (1.0 / D)
        _acc_first(i, dgf_ref, _colsum(dy * n))
        dx3 = _rms_bwd(n, r, dy * g_ref[...])
        dx3_ref[...] = dx3
        _acc_first(i, dg2_ref, _colsum(dx3 * fv), every=tpb)
        df_ref[...] = (dx3 * g2_ref[0]).astype(BF16)

    return pl.pallas_call(
        body, name="final", grid=(t // TM,),
        in_specs=[_row(D), _row(D), _per_ex(D), _vec(D), _row(D)],
        out_specs=[_row(D), _row(D), _vec(D), _vec(D), _per_ex(D)],
        out_shape=[jax.ShapeDtypeStruct((t, D), F32), jax.ShapeDtypeStruct((t, D), BF16),
                   jax.ShapeDtypeStruct((1, D), F32), jax.ShapeDtypeStruct((1, D), F32),
                   jax.ShapeDtypeStruct((nb, 1, D), F32)],
        compiler_params=_cp("arbitrary"),
    )(x2, f, g2, g_fin, target)


def _norm_bwd(xin, dh, dres, g, sc, gate=None):
    t = xin.shape[0]
    nb = t // S
    tpb = S // TM
    gated = gate is not None

    def body(*refs):
        if gated:
            x_ref, dh_ref, dres_ref, g_ref, sc_ref, mix_ref, g1_ref, dx_ref, dsh_ref, dsc_ref, dg_ref, dg1_ref, dmix_ref = refs
        else:
            x_ref, dh_ref, dres_ref, g_ref, sc_ref, dx_ref, dsh_ref, dsc_ref, dg_ref = refs
        i = pl.program_id(0)
        xv, dhv = x_ref[...], dh_ref[...].astype(F32)
        r = _rms(xv)
        n = xv * r
        gv = g_ref[...]
        _acc_first(i, dsh_ref, _colsum(dhv), every=tpb)
        _acc_first(i, dsc_ref, _colsum(dhv * (n * gv)), every=tpb)
        dng = dhv * (1.0 + sc_ref[0])
        _acc_first(i, dg_ref, _colsum(dng * n))
        dx = dres_ref[...] + _rms_bwd(n, r, dng * gv)
        dx_ref[...] = dx
        if gated:
            _acc_first(i, dg1_ref, _colsum(dx * mix_ref[...].astype(F32)), every=tpb)
            dmix_ref[...] = (dx * g1_ref[0]).astype(BF16)

    in_specs = [_row(D), _row(D), _row(D), _vec(D), _per_ex(D)]
    out_specs = [_row(D), _per_ex(D), _per_ex(D), _vec(D)]
    out_shape = [jax.ShapeDtypeStruct((t, D), F32), jax.ShapeDtypeStruct((nb, 1, D), F32),
                 jax.ShapeDtypeStruct((nb, 1, D), F32), jax.ShapeDtypeStruct((1, D), F32)]
    args = [xin, dh, dres, g, sc]
    if gated:
        in_specs += [_row(D), _per_ex(D)]
        out_specs += [_per_ex(D), _row(D)]
        out_shape += [jax.ShapeDtypeStruct((nb, 1, D), F32), jax.ShapeDtypeStruct((t, D), BF16)]
        args += list(gate)
    return pl.pallas_call(
        body, name="norm2_bwd" if gated else "norm1_bwd", grid=(t // TM,),
        in_specs=in_specs, out_specs=out_specs, out_shape=out_shape,
        compiler_params=_cp("arbitrary"),
    )(*args)


TQ = 256
TB = 512


def _mla_fwd(q, k, v):
    t = q.shape[0]
    nb = t // S

    def body(q_ref, k_ref, v_ref, o_ref, lse_ref):
        causal = lax.broadcasted_iota(jnp.int32, (TB, TB), 0) >= lax.broadcasted_iota(jnp.int32, (TB, TB), 1)
        heads = [slice(h * HP, (h + 1) * HP) for h in range(2)]
        for i in range(S // TB):
            ri, past = slice(i * TB, (i + 1) * TB), slice(0, i * TB)
            qhs = [q_ref[ri, sl] for sl in heads]
            sd = [jnp.where(causal, _dot(qh, k_ref[ri, sl], ((1,), (1,))) * SCALE_B, NEG) for qh, sl in zip(qhs, heads)]
            ms = [jnp.max(s, axis=-1, keepdims=True) for s in sd]
            if i:
                so = [_dot(qh, k_ref[past, sl], ((1,), (1,))) * SCALE_B for qh, sl in zip(qhs, heads)]
                ms = [jnp.maximum(m, jnp.max(s, axis=-1, keepdims=True)) for m, s in zip(ms, so)]
            pd = [jnp.exp(s - m) for s, m in zip(sd, ms)]
            ls = [jnp.sum(p, axis=-1, keepdims=True) for p in pd]
            acc = [_dot(p.astype(BF16), v_ref[ri, sl], ((1,), (0,))) for p, sl in zip(pd, heads)]
            if i:
                po = [jnp.exp(s - m) for s, m in zip(so, ms)]
                ls = [l + jnp.sum(p, axis=-1, keepdims=True) for l, p in zip(ls, po)]
                acc = [a + _dot(p.astype(BF16), v_ref[past, sl], ((1,), (0,))) for a, p, sl in zip(acc, po, heads)]
            o_ref[ri, :] = acc[0] / ls[0] + acc[1] / ls[1]
            for sl, m, l in zip(heads, ms, ls):
                lse_ref[ri, sl] = jnp.broadcast_to(m + jnp.log(l), (TB, HP))

    wide2 = pl.BlockSpec((S, 2 * HP), lambda b, p: (b, p))
    return pl.pallas_call(
        body, name="mla_fwd", grid=(nb, H // 2),
        in_specs=[wide2, wide2, wide2],
        out_specs=[pl.BlockSpec((S, HP), lambda b, p: (b, p)), wide2],
        out_shape=[jax.ShapeDtypeStruct((t, H * VDIM), F32), jax.ShapeDtypeStruct((t, H * HP), F32)],
        compiler_params=_cp("parallel", "parallel"),
    )(q, k, v)


def _mla_bwd(q, k, v, o, do, lse):
    t = q.shape[0]
    nb = t // S

    def body(q_ref, k_ref, v_ref, o_ref, do_ref, lse_ref, dq_out, dk_out, dv_out, dq_ref, dk_ref, dv_ref):
        lane = lax.broadcasted_iota(jnp.int32, (TB, HP), 1)
        causal = lax.broadcasted_iota(jnp.int32, (TB, TB), 0) >= lax.broadcasted_iota(jnp.int32, (TB, TB), 1)
        heads = [slice(h * HP, (h + 1) * HP) for h in range(2)]
        nblk = S // TB
        for i in reversed(range(nblk)):
            ri, past = slice(i * TB, (i + 1) * TB), slice(0, i * TB)
            dov = do_ref[ri, :]
            prod = dov * o_ref[ri, :]
            dob = dov.astype(BF16)
            deltas = [jnp.sum(jnp.where((lane < VDIM) if h == 0 else (lane >= VDIM), prod, 0.0), axis=-1, keepdims=True)
                      for h in range(2)]
            qhs = [q_ref[ri, sl] for sl in heads]
            lses = [lse_ref[ri, sl][:, :1] for sl in heads]
            for rows, diagonal in ((ri, True), (past, False)):
                if rows.stop == rows.start:
                    continue
                ps = [jnp.exp(_dot(qh, k_ref[rows, sl], ((1,), (1,))) * SCALE_B - lse) for qh, sl, lse in zip(qhs, heads, lses)]
                if diagonal:
                    ps = [jnp.where(causal, p, 0.0) for p in ps]
                dps = [_dot(dob, v_ref[rows, sl], ((1,), (1,))) for sl in heads]
                dss = [(p * (dp - delta) * SCALE_B).astype(BF16) for p, dp, delta in zip(ps, dps, deltas)]
                for sl, qh, p, ds in zip(heads, qhs, ps, dss):
                    dq = _dot(ds, k_ref[rows, sl], ((1,), (0,)))
                    dk = _dot(ds, qh, ((0,), (0,)))
                    dv = _dot(p.astype(BF16), dob, ((0,), (0,)))
                    if diagonal:
                        dq_ref[ri, sl] = dq
                    else:
                        dq_ref[ri, sl] += dq
                    if i == nblk - 1:
                        dk_ref[rows, sl] = dk
                        dv_ref[rows, sl] = dv
                    else:
                        dk_ref[rows, sl] += dk
                        dv_ref[rows, sl] += dv
        dq_out[...] = dq_ref[...].astype(BF16)
        dk_out[...] = dk_ref[...].astype(BF16)
        dv_out[...] = dv_ref[...].astype(BF16)

    wide2 = pl.BlockSpec((S, 2 * HP), lambda b, p: (b, p))
    pair = pl.BlockSpec((S, HP), lambda b, p: (b, p))
    return pl.pallas_call(
        body, name="mla_bwd", grid=(nb, H // 2),
        in_specs=[wide2, wide2, wide2, pair, pair, wide2],
        out_specs=[wide2, wide2, wide2],
        out_shape=[jax.ShapeDtypeStruct((t, H * HP), BF16)] * 3,
        scratch_shapes=[pltpu.VMEM((S, 2 * HP), F32)] * 3,
        compiler_params=_cp("parallel", "parallel"),
    )(q, k, v, o, do, lse)


def _t5_bucket(dist):
    max_exact = N_BUCKETS // 2
    d = np.maximum(dist, 1).astype(np.float64)
    large = max_exact + (np.log(d / max_exact) / np.log(MAX_DISTANCE / max_exact) * (N_BUCKETS - max_exact)).astype(np.int64)
    large = np.minimum(large, N_BUCKETS - 1)
    return np.where(dist < max_exact, dist, large).astype(np.int32)


def _band_geometry():
    a = np.arange(BLK)[:, None]
    bk = np.arange(2 * BLK)[None, :]
    steps = BLK + a - bk
    valid = (steps >= 0) & (steps <= BLK)
    buckets = np.stack([_t5_bucket(np.clip(steps, 0, BLK) * d) for d in DILATIONS])
    return buckets, valid


def _band_bias(rel_bias):
    buckets, valid = _band_geometry()
    onehot = (jnp.asarray(buckets)[..., None] == jnp.arange(N_BUCKETS)).astype(F32)
    bias = jnp.einsum("rqkn,nh->rhqk", onehot, rel_bias, precision=lax.Precision.HIGHEST)
    bias = jnp.where(jnp.asarray(valid)[None, None], bias, NEG)
    return bias.reshape(3, H // 2, 2 * BLK, 2 * BLK)


def _dil_items():
    items = []
    for r, d in enumerate(DILATIONS):
        for res in range(d):
            for blk in range(S // d // BLK):
                items.append((r, d, blk * BLK * d + res, blk > 0))
    return items


GROUP = 4


def _strided(start, d):
    return pl.ds(start, BLK) if d == 1 else pl.ds(start, BLK, stride=d)


def _stack_heads(tile, own):
    return jnp.where(own, jnp.concatenate([tile, tile], axis=0), 0.0).astype(BF16)


def _own_lanes():
    row = lax.broadcasted_iota(jnp.int32, (2 * BLK, HP), 0)
    lane = lax.broadcasted_iota(jnp.int32, (2 * BLK, HP), 1)
    return (lane < E_A) == (row < BLK)


def _dil_fwd(proj, biasm):
    t = proj.shape[0]
    nb = t // S

    def body(q_ref, k_ref, v_ref, b_ref, o_ref, lse_ref, ob_ref, lb_ref):
        lane = lax.broadcasted_iota(jnp.int32, (BLK, HP), 1)
        own = _own_lanes()
        items = _dil_items()
        for g in range(0, len(items), GROUP):
            grp = items[g:g + GROUP]
            ss, vts = [], []
            for r, d, start, has_prev in grp:
                cur = _strided(start, d)
                rows = [_strided(start - BLK * d, d), cur] if has_prev else [cur]
                q2 = _stack_heads(q_ref[cur, :] * SCALE_A, own)
                kt = jnp.concatenate([k_ref[x, :] for x in rows], axis=0).astype(BF16)
                vts.append(jnp.concatenate([v_ref[x, :] for x in rows], axis=0).astype(BF16))
                bias = b_ref[r, 0] if has_prev else b_ref[r, 0, :, BLK:]
                ss.append(_dot(q2, kt, ((1,), (1,))) + bias)
            ms = [jnp.max(s, axis=-1, keepdims=True) for s in ss]
            ps = [jnp.exp(s - m) for s, m in zip(ss, ms)]
            ls = [jnp.sum(p, axis=-1, keepdims=True) for p in ps]
            for (r, d, start, _), p, vt, m, l in zip(grp, ps, vts, ms, ls):
                cur = _strided(start, d)
                o2 = _dot(p.astype(BF16), vt, ((1,), (0,))) / l
                lse2 = m + jnp.log(l)
                ob_ref[r, cur, :] = jnp.where(lane < E_A, o2[:BLK], o2[BLK:])
                lb_ref[r, cur, :] = jnp.where(lane < E_A, lse2[:BLK], lse2[BLK:])

        def merge(c, _):
            rows = pl.ds(pl.multiple_of(c * TQ, TQ), TQ)
            l0, l1, l2 = lb_ref[0, rows, :], lb_ref[1, rows, :], lb_ref[2, rows, :]
            m = jnp.maximum(jnp.maximum(l0, l1), l2)
            e0, e1, e2 = jnp.exp(l0 - m), jnp.exp(l1 - m), jnp.exp(l2 - m)
            tot = e0 + e1 + e2
            o_ref[rows, :] = (e0 * ob_ref[0, rows, :] + e1 * ob_ref[1, rows, :] + e2 * ob_ref[2, rows, :]) / tot
            lse_ref[rows, :] = m + jnp.log(tot)
            return 0

        lax.fori_loop(0, S // TQ, merge, 0)

    npair = H // 2
    return pl.pallas_call(
        body, name="dil_fwd", grid=(nb, npair),
        in_specs=[pl.BlockSpec((S, HP), lambda b, p: (b, p)), pl.BlockSpec((S, HP), lambda b, p: (b, npair + p)),
                  pl.BlockSpec((S, HP), lambda b, p: (b, 2 * npair + p)),
                  pl.BlockSpec((3, 1, 2 * BLK, 2 * BLK), lambda b, p: (0, p, 0, 0))],
        out_specs=[pl.BlockSpec((S, HP), lambda b, p: (b, p))] * 2,
        out_shape=[jax.ShapeDtypeStruct((t, D_A), F32)] * 2,
        scratch_shapes=[pltpu.VMEM((3, S, HP), F32), pltpu.VMEM((3, S, HP), F32)],
        compiler_params=_cp("parallel", "parallel"),
    )(proj, proj, proj, biasm)


def _dil_bwd(proj, biasm, o, do, lse):
    t = proj.shape[0]
    nb = t // S

    def body(q_ref, k_ref, v_ref, b_ref, o_ref, do_ref, lse_ref, dq_out, dk_out, dv_out, ds_ref, dq_ref, dk_ref, dv_ref):
        dq_ref[...] = jnp.zeros_like(dq_ref)
        dk_ref[...] = jnp.zeros_like(dk_ref)
        dv_ref[...] = jnp.zeros_like(dv_ref)
        ds_ref[...] = jnp.zeros_like(ds_ref)
        lane = lax.broadcasted_iota(jnp.int32, (BLK, HP), 1)
        own = _own_lanes()
        items = _dil_items()
        for g in range(0, len(items), GROUP):
            grp = items[g:g + GROUP]
            q2s, kts, do2s, ss, dps, lse2s, delta2s = [], [], [], [], [], [], []
            for r, d, start, has_prev in grp:
                cur = _strided(start, d)
                rows = [_strided(start - BLK * d, d), cur] if has_prev else [cur]
                q2 = _stack_heads(q_ref[cur, :] * SCALE_A, own)
                kt = jnp.concatenate([k_ref[x, :] for x in rows], axis=0).astype(BF16)
                vt = jnp.concatenate([v_ref[x, :] for x in rows], axis=0).astype(BF16)
                dot_ = do_ref[cur, :]
                prod = dot_ * o_ref[cur, :]
                lset = lse_ref[cur, :]
                do2 = _stack_heads(dot_, own)
                bias = b_ref[r, 0] if has_prev else b_ref[r, 0, :, BLK:]
                ss.append(_dot(q2, kt, ((1,), (1,))) + bias)
                dps.append(_dot(do2, vt, ((1,), (1,))))
                lse2s.append(jnp.concatenate([lset[:, :1], lset[:, E_A:E_A + 1]], axis=0))
                delta2s.append(jnp.concatenate([jnp.sum(jnp.where(lane < E_A, prod, 0.0), axis=-1, keepdims=True),
                                                jnp.sum(jnp.where(lane >= E_A, prod, 0.0), axis=-1, keepdims=True)], axis=0))
                q2s.append(q2)
                kts.append(kt)
                do2s.append(do2)
            ps = [jnp.exp(s - lse2) for s, lse2 in zip(ss, lse2s)]
            dls = [p * (dp - delta2) for p, dp, delta2 in zip(ps, dps, delta2s)]
            for (r, d, start, has_prev), q2, kt, do2, p, dl in zip(grp, q2s, kts, do2s, ps, dls):
                cur = _strided(start, d)
                dsb = dl.astype(BF16)
                dq2 = _dot(dsb, kt, ((1,), (0,))) * SCALE_A
                dkt = _dot(dsb, q2, ((0,), (0,)))
                dvt = _dot(p.astype(BF16), do2, ((0,), (0,)))
                dq_ref[cur, :] += jnp.where(lane < E_A, dq2[:BLK], dq2[BLK:])
                if has_prev:
                    prev = _strided(start - BLK * d, d)
                    ds_ref[0, r, 0] += dl
                    dk_ref[prev, :] += dkt[:BLK]
                    dv_ref[prev, :] += dvt[:BLK]
                    dk_ref[cur, :] += dkt[BLK:]
                    dv_ref[cur, :] += dvt[BLK:]
                else:
                    ds_ref[0, r, 0, :, BLK:] += dl
                    dk_ref[cur, :] += dkt
                    dv_ref[cur, :] += dvt
        dq_out[...] = dq_ref[...].astype(BF16)
        dk_out[...] = dk_ref[...].astype(BF16)
        dv_out[...] = dv_ref[...].astype(BF16)

    npair = H // 2
    pair = pl.BlockSpec((S, HP), lambda b, p: (b, p))
    return pl.pallas_call(
        body, name="dil_bwd", grid=(nb, npair),
        in_specs=[pair, pl.BlockSpec((S, HP), lambda b, p: (b, npair + p)),
                  pl.BlockSpec((S, HP), lambda b, p: (b, 2 * npair + p)),
                  pl.BlockSpec((3, 1, 2 * BLK, 2 * BLK), lambda b, p: (0, p, 0, 0)), pair, pair, pair],
        out_specs=[pair, pair, pair, pl.BlockSpec((1, 3, 1, 2 * BLK, 2 * BLK), lambda b, p: (b, 0, p, 0, 0))],
        out_shape=[jax.ShapeDtypeStruct((t, D_A), BF16)] * 3 + [jax.ShapeDtypeStruct((nb, 3, npair, 2 * BLK, 2 * BLK), F32)],
        scratch_shapes=[pltpu.VMEM((S, HP), F32)] * 3,
        compiler_params=_cp("parallel", "parallel"),
    )(proj, proj, proj, biasm, o, do, lse)


def _rel_bias_grad(dlogits):
    nb = dlogits.shape[0]
    buckets, _ = _band_geometry()
    kk = 3 * BLK * 2 * BLK
    dl = jnp.transpose(dlogits.reshape(nb, 3, H, BLK, 2 * BLK), (0, 2, 1, 3, 4)).reshape(nb, H, kk)
    bk = jnp.asarray(buckets.reshape(1, kk))
    tk = kk // 12

    def body(dl_ref, bk_ref, o_ref):
        j = pl.program_id(0)
        onehot = (bk_ref[...] == lax.broadcasted_iota(jnp.int32, (N_BUCKETS, tk), 0)).astype(F32)
        tot = dl_ref[0]
        for b in range(1, nb):
            tot = tot + dl_ref[b]
        part = lax.dot_general(onehot, tot, ((((1,), (1,))), ((), ())), preferred_element_type=F32,
                               precision=lax.Precision.HIGHEST)
        _acc_first(j, o_ref, part)

    return pl.pallas_call(
        body, name="rel_bias_grad", grid=(kk // tk,),
        in_specs=[pl.BlockSpec((nb, H, tk), lambda j: (0, 0, j)), pl.BlockSpec((1, tk), lambda j: (0, j))],
        out_specs=pl.BlockSpec((N_BUCKETS, H), lambda j: (0, 0)),
        out_shape=jax.ShapeDtypeStruct((N_BUCKETS, H), F32),
        compiler_params=_cp("arbitrary"),
    )(dl, bk)


def _mesh_place():
    x, y, c = lax.axis_index("x"), lax.axis_index("y"), lax.axis_index("c")
    return x, y, c


def _peer(k):
    x, y, c = _mesh_place()
    px = 1 - x if k & 4 else x
    py = 1 - y if k & 2 else y
    pc = 1 - c if k & 1 else c
    return (px, py, pc), 4 * px + 2 * py + pc


ANY = pl.BlockSpec(memory_space=pl.ANY)


def _exchange(arrays, gathers, name, after=None):
    n_arr = len(arrays)

    def body(*refs):
        ins, outs = refs[:n_arr], refs[n_arr + 1:2 * n_arr + 1]
        send, recv, loc = refs[2 * n_arr + 1:]
        x, y, c = _mesh_place()
        me = 4 * x + 2 * y + c
        local = [pltpu.make_async_copy(ins[a] if gathers[a] else ins[a].at[me], outs[a].at[me], loc.at[a])
                 for a in range(n_arr)]
        remote = _peer_copies(ins, outs, send, recv, gathers)
        for cp in local:
            cp.start()
        for put, _ in remote:
            put.start()
        for cp in local:
            cp.wait()
        for put, got in remote:
            put.wait_send()
            got.wait_recv()

    return pl.pallas_call(
        body, name=name,
        in_specs=[ANY] * (n_arr + 1), out_specs=[ANY] * n_arr,
        out_shape=[jax.ShapeDtypeStruct(((N_DEV,) if g else ()) + a.shape, a.dtype) for a, g in zip(arrays, gathers)],
        scratch_shapes=[pltpu.SemaphoreType.DMA((n_arr * (N_DEV - 1),)), pltpu.SemaphoreType.DMA((n_arr * (N_DEV - 1),)),
                        pltpu.SemaphoreType.DMA((n_arr,))],
        compiler_params=pltpu.CompilerParams(has_side_effects=True),
    )(*arrays, arrays[0] if after is None else after)


def _gather_two_level(arrays, name):
    n_arr = len(arrays)
    per = N_DEV - 1

    def body(*refs):
        ins, outs = refs[:n_arr], refs[n_arr:2 * n_arr]
        send, recv, loc = refs[2 * n_arr:]
        x, y, c = _mesh_place()
        me, sibling = (x, y, c), (x, y, 1 - c)
        chips = [(1 - x, y), (x, 1 - y), (1 - x, 1 - y)]

        def block(a, place):
            px, py, pc = place
            return outs[a].at[4 * px + 2 * py + pc]

        def copy(a, k, place, to, src=None):
            dst = block(a, place)
            return pltpu.make_async_remote_copy(dst if src is None else src, dst, send.at[a * per + k], recv.at[a * per + k],
                                                device_id=to, device_id_type=pl.DeviceIdType.MESH)

        local = [pltpu.make_async_copy(ins[a], block(a, me), loc.at[a]) for a in range(n_arr)]
        for cp in local:
            cp.start()
        first = []
        for a in range(n_arr):
            first.append(copy(a, 0, me, sibling, src=ins[a]))
            first += [copy(a, 1 + j, me, (*chip, c), src=ins[a]) for j, chip in enumerate(chips)]
        for cp in first:
            cp.start()
        passed = []
        for j, chip in enumerate(chips):
            for a in range(n_arr):
                copy(a, 1 + j, (*chip, c), me).wait_recv()
                passed.append(copy(a, 4 + j, (*chip, c), sibling))
                passed[-1].start()
        for a in range(n_arr):
            copy(a, 0, sibling, me).wait_recv()
            for j, chip in enumerate(chips):
                copy(a, 4 + j, (*chip, 1 - c), me).wait_recv()
        for cp in first + passed:
            cp.wait_send()
        for cp in local:
            cp.wait()

    return pl.pallas_call(
        body, name=name,
        in_specs=[ANY] * n_arr, out_specs=[ANY] * n_arr,
        out_shape=[jax.ShapeDtypeStruct((N_DEV,) + a.shape, a.dtype) for a in arrays],
        scratch_shapes=[pltpu.SemaphoreType.DMA((n_arr * per,)), pltpu.SemaphoreType.DMA((n_arr * per,)),
                        pltpu.SemaphoreType.DMA((n_arr,))],
        compiler_params=pltpu.CompilerParams(has_side_effects=True),
    )(*arrays)


HBM = pl.BlockSpec(memory_space=pltpu.HBM)
SEM = pl.BlockSpec(memory_space=pltpu.SEMAPHORE)
DATAFLOW = pltpu.SideEffectType.DATAFLOW_SIDE_EFFECTING


def _own_block_in_place(block, me):
    land = lax.empty((N_DEV,) + block.shape, block.dtype)
    return lax.dynamic_update_slice(land, block[None], (me,) + (0,) * block.ndim)


def _peer_copies(srcs, lands, send, recv, gathers):
    x, y, c = _mesh_place()
    me = 4 * x + 2 * y + c
    out = []
    for a, (src, land) in enumerate(zip(srcs, lands)):
        for k in range(1, N_DEV):
            dev, idx = _peer(k)
            sem = a * (N_DEV - 1) + k - 1
            mine = src if gathers[a] else src.at[idx]
            put = pltpu.make_async_remote_copy(mine, land.at[me], send.at[sem], recv.at[sem],
                                               device_id=dev, device_id_type=pl.DeviceIdType.MESH)
            got = pltpu.make_async_remote_copy(mine, land.at[idx], send.at[sem], recv.at[sem],
                                               device_id=dev, device_id_type=pl.DeviceIdType.MESH)
            out.append((put, got))
    return out


def _exchange_start(srcs, lands, gather, after, name):
    n = len(srcs)

    def body(*refs):
        srcs_, lands_, send, recv = refs[:n], refs[n:2 * n], refs[2 * n + 1], refs[2 * n + 2]
        for put, _ in _peer_copies(srcs_, lands_, send, recv, gather):
            put.start()
        refs[-1][...] = jnp.zeros_like(refs[-1])

    nsem = n * (N_DEV - 1)
    thru = [pltpu.HBM(a.shape, a.dtype) for a in list(srcs) + list(lands)]
    res = pl.pallas_call(
        body, name=name,
        out_shape=(pltpu.SemaphoreType.DMA((nsem,)), pltpu.SemaphoreType.DMA((nsem,)), *thru, jax.ShapeDtypeStruct((8, 128), F32)),
        in_specs=[HBM] * (2 * n) + [ANY],
        out_specs=(SEM, SEM, *([HBM] * (2 * n)), pl.BlockSpec(memory_space=pltpu.VMEM)),
        input_output_aliases={i: 2 + i for i in range(2 * n)},
        compiler_params=pltpu.CompilerParams(has_side_effects=DATAFLOW),
    )(*[pltpu.with_memory_space_constraint(a, pltpu.HBM) for a in list(srcs) + list(lands)], after)
    return res[0], res[1], list(res[2:2 + n]), list(res[2 + n:2 + 2 * n]), res[-1]


def _exchange_wait(send, recv, srcs, lands, gather, after, name):
    n = len(srcs)

    def body(*refs):
        srcs_, lands_, send_, recv_ = refs[:n], refs[n:2 * n], refs[2 * n], refs[2 * n + 1]
        for put, got in _peer_copies(srcs_, lands_, send_, recv_, gather):
            put.wait_send()
            got.wait_recv()

    thru = [pltpu.HBM(a.shape, a.dtype) for a in list(srcs) + list(lands)]
    res = pl.pallas_call(
        body, name=name, out_shape=tuple(thru),
        in_specs=[HBM] * (2 * n) + [SEM, SEM, ANY], out_specs=tuple([HBM] * (2 * n)),
        input_output_aliases={i: i for i in range(2 * n)},
        compiler_params=pltpu.CompilerParams(has_side_effects=DATAFLOW),
    )(*srcs, *lands, send, recv, after)
    return list(res[n:])


def _silu_rows(c):
    def body(c_ref, o_ref):
        v = c_ref[...]
        o_ref[...] = v * _sigmoid(v)

    return pl.pallas_call(body, name="cond", out_shape=jax.ShapeDtypeStruct(c.shape, F32))(c)


def _mod_slab(cond_all, w_ada, b_slab):
    def body(c_ref, w_ref, b_ref, o_ref):
        o_ref[...] = _dot(c_ref[...].astype(BF16), w_ref[0].astype(BF16), ((1,), (0,))) + b_ref[...]

    return pl.pallas_call(body, name="mod_slab",
                          out_shape=jax.ShapeDtypeStruct((cond_all.shape[0], w_ada.shape[2]), F32),
                          compiler_params=pltpu.CompilerParams(vmem_limit_bytes=VMEM_LIMIT))(cond_all, w_ada, b_slab)


def _ada_grad(cond_all, dmod_cols):
    def body(c_ref, d_ref, o_ref):
        o_ref[...] = _dot(c_ref[...].astype(BF16), d_ref[...].astype(BF16), ((0,), (0,)))

    return pl.pallas_call(body, name="ada_grad",
                          out_shape=jax.ShapeDtypeStruct((cond_all.shape[1], dmod_cols.shape[1]), F32),
                          compiler_params=pltpu.CompilerParams(vmem_limit_bytes=VMEM_LIMIT))(cond_all, dmod_cols)


def _adam_math(g, w, m, v):
    m2 = B1 * m + (1.0 - B1) * g
    v2 = B2 * v + (1.0 - B2) * (g * g)
    m_hat = m2 / (1.0 - B1 ** STEP)
    v_hat = v2 / (1.0 - B2 ** STEP)
    return -LR * (m_hat / (jnp.sqrt(v_hat) + ADAM_EPS) + WD * w), m2, v2


def _adamw(parts, w, m, v, name):
    n, rows, cols = parts.shape
    tr = _pick(rows, (128, 96, 64, 32, 16, 8))

    def body(p_ref, w_ref, m_ref, v_ref, g_ref, d_ref, m2_ref, v2_ref):
        g = p_ref[0].astype(F32)
        for s in range(1, n):
            g = g + p_ref[s].astype(F32)
        g_ref[0] = g
        d_ref[0], m2_ref[0], v2_ref[0] = _adam_math(g, w_ref[0], m_ref[0], v_ref[0])

    blk = pl.BlockSpec((1, tr, cols), lambda i: (0, i, 0))
    return pl.pallas_call(
        body, name=name, grid=(rows // tr,),
        in_specs=[pl.BlockSpec((n, tr, cols), lambda i: (0, i, 0)), blk, blk, blk],
        out_specs=[blk] * 4, out_shape=[jax.ShapeDtypeStruct((1, rows, cols), F32)] * 4,
        compiler_params=_cp("parallel"),
    )(parts, w, m, v)


ROW_PARAMS = (("g_norm1", D), ("g_cq", Q_LORA), ("g_ckv", KV_LORA), ("g_out_a", D_A), ("g_out_b", D_A), ("g_norm2", D),
              ("g_final", D))
LOSS_ROW = N_MOD + len(ROW_PARAMS)
PAY_ROWS = 16
NCOL = N_MOD * D // N_DEV


def _pack_small(dmods, rows, loss_cols):
    nb = dmods[0].shape[0]
    nrow = len(ROW_PARAMS)

    def body(*refs):
        dm, rw, loss_ref, pay_ref, blk_ref = refs[:N_MOD], refs[N_MOD:N_MOD + nrow], refs[N_MOD + nrow], refs[-2], refs[-1]
        pay_ref[...] = jnp.zeros_like(pay_ref)
        for k in range(N_MOD):
            tot = dm[k][0]
            for b in range(1, nb):
                tot = tot + dm[k][b]
            pay_ref[k:k + 1, :] = tot
        for i, (_, n) in enumerate(ROW_PARAMS):
            pay_ref[N_MOD + i:N_MOD + i + 1, :n] = rw[i][...]
        pay_ref[LOSS_ROW:LOSS_ROW + 1, :] = loss_ref[...]
        for j in range(N_DEV):
            done = 0
            while done < NCOL:
                seg, off = divmod(j * NCOL + done, D)
                ln = min(NCOL - done, D - off)
                for b in range(nb):
                    blk_ref[j, b:b + 1, done:done + ln] = dm[seg][b][:, off:off + ln]
                done += ln

    return pl.pallas_call(
        body, name="pack_small",
        out_shape=[jax.ShapeDtypeStruct((PAY_ROWS, D), F32), jax.ShapeDtypeStruct((N_DEV, nb, NCOL), F32)],
    )(*dmods, *rows, loss_cols)


def _small_update(pay, rel, ws, ms, vs):
    n_par = len(ws)

    def body(*refs):
        pay_ref, rel_ref = refs[:2]
        w_refs, m_refs, v_refs = (refs[2 + s * n_par:2 + (s + 1) * n_par] for s in range(3))
        outs, loss_ref = refs[2 + 3 * n_par:-1], refs[-1]
        tot, rtot = pay_ref[0], rel_ref[0]
        for s in range(1, N_DEV):
            tot, rtot = tot + pay_ref[s], rtot + rel_ref[s]

        def update(p, g, sl):
            outs[4 * p][:, sl] = g
            outs[4 * p + 1][:, sl], outs[4 * p + 2][:, sl], outs[4 * p + 3][:, sl] = _adam_math(
                g, w_refs[p][:, sl], m_refs[p][:, sl], v_refs[p][:, sl])

        for k in range(N_MOD):
            update(0, tot[k:k + 1, :], slice(k * D, (k + 1) * D))
        for i, (_, n) in enumerate(ROW_PARAMS):
            update(1 + i, tot[N_MOD + i:N_MOD + i + 1, :n], slice(0, n))
        update(n_par - 1, rtot, slice(0, H))
        loss_ref[...] = jnp.broadcast_to((0.5 / D) * jnp.sum(tot[LOSS_ROW:LOSS_ROW + 1, :]), loss_ref.shape)

    shapes = [jax.ShapeDtypeStruct(w.shape, F32) for w in ws for _ in range(4)]
    res = pl.pallas_call(
        body, name="small_update", out_shape=shapes + [jax.ShapeDtypeStruct((8, 128), F32)],
    )(pay, rel, *ws, *ms, *vs)
    return [tuple(res[4 * p:4 * p + 4]) for p in range(n_par)], res[-1]


def _cols_from_blocks(g):
    return jnp.transpose(g, (1, 0, 2)).reshape(g.shape[1], N_DEV * g.shape[2])


def _cols_to_blocks(w):
    r, c = w.shape
    return jnp.transpose(w.reshape(r, N_DEV, c // N_DEV), (1, 0, 2))


def _pad_w_in(w):
    z = jnp.zeros((w.shape[0], NOPE), w.dtype)
    return jnp.concatenate([w[:, :P_IN - ROPE], z, w[:, P_IN - ROPE:], z[:, :HP - NOPE - ROPE]], axis=1)


def _unpad_w_in(g):
    k0 = P_IN - ROPE + NOPE
    return jnp.concatenate([g[:, :P_IN - ROPE], g[:, k0:k0 + ROPE]], axis=1)


def _pad_w_uq(w):
    w3 = w.reshape(Q_LORA, H, NOPE + ROPE)
    return jnp.pad(w3, ((0, 0), (0, 0), (0, HP - NOPE - ROPE))).reshape(Q_LORA, H * HP)


def _unpad_w_uq(g):
    return g.reshape(Q_LORA, H, HP)[:, :, :NOPE + ROPE].reshape(Q_LORA, H * (NOPE + ROPE))


def _split_w_ukv(w):
    w4 = w.reshape(KV_LORA, H // 2, 2, HP)
    z = jnp.zeros((KV_LORA, H // 2, NOPE), w.dtype)
    kn, vv = w4[..., :NOPE], w4[..., NOPE:]
    w_k = jnp.stack([jnp.concatenate([kn[:, :, 0], z], -1), jnp.concatenate([kn[:, :, 1], z], -1)], axis=2)
    w_v = jnp.stack([jnp.concatenate([vv[:, :, 0], z], -1), jnp.concatenate([z, vv[:, :, 1]], -1)], axis=2)
    return w_k.reshape(KV_LORA, H * HP), w_v.reshape(KV_LORA, H * HP)


def _join_w_ukv(g_k, g_v):
    gk = g_k.reshape(KV_LORA, H // 2, 2, HP)
    gv = g_v.reshape(KV_LORA, H // 2, 2, HP)
    even = jnp.concatenate([gk[:, :, 0, :NOPE], gv[:, :, 0, :VDIM]], -1)
    odd = jnp.concatenate([gk[:, :, 1, :NOPE], gv[:, :, 1, VDIM:]], -1)
    return jnp.stack([even, odd], axis=2).reshape(KV_LORA, H * HP)


def _rope_tables():
    half = ROPE // 2
    inv = ROPE_THETA ** (-jnp.arange(half, dtype=F32) / half)
    ang = jnp.arange(S, dtype=F32)[:, None] * inv[None, :]
    cos, sin = jnp.cos(ang), jnp.sin(ang)
    ones, zeros = jnp.ones((S, NOPE), F32), jnp.zeros((S, NOPE), F32)
    tail1, tail0 = jnp.ones((S, HP - NOPE - ROPE), F32), jnp.zeros((S, HP - NOPE - ROPE), F32)
    zh = jnp.zeros((S, half), F32)
    c = jnp.concatenate([ones, cos, cos, tail1], axis=1)
    sm = jnp.concatenate([zeros, -sin, zh, tail0], axis=1)
    sp = jnp.concatenate([zeros, zh, sin, tail0], axis=1)
    return c, sm, sp


def _local_step(x, mod, target, g_norm1, w_in_p, g_cq, w_uq_p, g_ckv, w_k, w_v, rel_bias, g_out_a, g_out_b, w_out,
                g_norm2, w_ffn_in, w_ffn_out, g_final, late_weights=None, on_ffn_grads=None, on_last_grads=None):
    nb = x.shape[0] // S
    sh1, sc1, g1, sh2, sc2, g2 = (mod[:, n].reshape(nb, 1, D) for n in range(N_MOD))
    rc, rsm, rsp = _rope_tables()
    biasm = _band_bias(rel_bias)

    h1 = _pre1(x, g_norm1, sc1, sh1)
    proj = _mm_nn(h1, w_in_p, F32, "proj")
    q, k, v, cqn, ckvn = _mla_pre(proj, g_cq, g_ckv, w_uq_p, w_k, w_v, rc, rsm, rsp)
    out_b, lse_b = _mla_fwd(q, k, v)
    out_a, lse_a = _dil_fwd(proj, biasm)
    y = _post_attn(out_a, out_b, g_out_a, g_out_b)
    if late_weights is not None:
        w_out, w_ffn_in, w_ffn_out = late_weights(y)
    mix = _mm_nn(y, w_out, BF16, "mix")
    x2, h2 = _resid_norm2(x, mix, g1, g_norm2, sc2, sh2)
    ffn_g, ffn_u, act = _ffn_in(h2, w_ffn_in)
    f = _mm_nn(act, w_ffn_out, BF16, "ffn_out")
    dx3, df, loss_cols, dg_final, dg2 = _final(x2, f, g2, g_final, target)

    dg_, du_ = _d_act(df, w_ffn_out, ffn_g, ffn_u)
    gw_ffn_out = _mm_tn(act, [df], "gw_ffn_out")
    dh2 = _d_h2(dg_, du_, w_ffn_in)
    gw_ffn_in = _mm_tn(h2, [dg_, du_], "gw_ffn_in")
    dx2, dsh2, dsc2, dg_norm2, dg1, dmix = _norm_bwd(x2, dh2, dx3, g_norm2, sc2, gate=(mix, g1))
    dy = _mm_nt(dmix, w_out, BF16, "d_y")
    gw_out = _mm_tn(y, [dmix], "gw_out")
    if on_ffn_grads is not None:
        g_out_a = g_out_a + on_ffn_grads(gw_ffn_in, gw_ffn_out, gw_out)
    dout_a, dout_b, dg_out_a, dg_out_b = _post_attn_bwd(dy, out_a, out_b, g_out_a, g_out_b)
    dq_b, dk_b, dv_b = _mla_bwd(q, k, v, out_b, dout_b, lse_b)
    dq_a, dk_a, dv_a, dlogits = _dil_bwd(proj, biasm, out_a, dout_a, lse_a)
    g_rel = _rel_bias_grad(dlogits)
    dqr, dproj, dg_cq, dg_ckv = _mla_pre_bwd(proj, dq_b, dk_b, dv_b, (dq_a, dk_a, dv_a), g_cq, g_ckv, w_uq_p, w_k, w_v,
                                             rc, rsm, rsp)
    gw_uq = _mm_tn(cqn, [dqr], "gw_uq")
    gw_k, gw_v = _mm_tn(ckvn, [dk_b, dv_b], "gw_kv")
    gw_in = _mm_tn(h1, [dproj], "gw_in")
    if on_last_grads is not None:
        started = on_last_grads(dict(w_in=gw_in, w_uq=gw_uq, w_k=gw_k, w_v=gw_v))
    else:
        started = None
    dh1 = _mm_nt(dproj, w_in_p, BF16, "d_h1", after=started)
    grad_x, dsh1, dsc1, dg_norm1 = _norm_bwd(x, dh1, dx2, g_norm1, sc1)

    dmod = [dsh1, dsc1, dg1, dsh2, dsc2, dg2]
    small = dict(g_norm1=dg_norm1, g_cq=dg_cq, g_ckv=dg_ckv, rel_bias=g_rel, g_out_a=dg_out_a, g_out_b=dg_out_b,
                 g_norm2=dg_norm2, g_final=dg_final)
    big = dict(w_in=gw_in, w_uq=gw_uq, w_k=gw_k, w_v=gw_v, w_out=gw_out, w_ffn_in=gw_ffn_in, w_ffn_out=gw_ffn_out)
    return grad_x, dmod, loss_cols, small, big


def kernel(x, c, w_ada, b_ada, g_norm1, w_in, g_cq, w_uq, g_ckv, w_ukv, rel_bias, g_out_a, g_out_b, w_out, g_norm2, w_ffn_in, w_ffn_out, g_final, loss_target, m_w_ada, m_b_ada, m_g_norm1, m_w_in, m_g_cq, m_w_uq, m_g_ckv, m_w_ukv, m_rel_bias, m_g_out_a, m_g_out_b, m_w_out, m_g_norm2, m_w_ffn_in, m_w_ffn_out, m_g_final, v_w_ada, v_b_ada, v_g_norm1, v_w_in, v_g_cq, v_w_uq, v_g_ckv, v_w_ukv, v_rel_bias, v_g_out_a, v_g_out_b, v_w_out, v_g_norm2, v_w_ffn_in, v_w_ffn_out, v_g_final):
    nb = x.shape[0]
    t = nb * S
    xt, tt = x.reshape(t, D), loss_target.reshape(t, D)
    me = 4 * lax.axis_index("x") + 2 * lax.axis_index("y") + lax.axis_index("c")

    early = [w_in[0], w_uq[0], w_ukv[0]]
    gathered = _gather_two_level([_silu_rows(c)] + [s.astype(BF16) for s in early], "gather_weights")
    cond_all = gathered[0].reshape(N_DEV * nb, D)
    w_in_f, w_uq_f, w_ukv_f = (_cols_from_blocks(g) for g in gathered[1:4])
    w_k, w_v = _split_w_ukv(w_ukv_f)

    ncol = N_MOD * D // N_DEV
    b_slab = lax.dynamic_slice(b_ada, (0, me * ncol), (1, ncol))
    slab = _mod_slab(cond_all, w_ada, b_slab)
    (mod_rows,) = _exchange([slab.reshape(N_DEV, nb, ncol)], [False], "scatter_mod")
    mod = jnp.transpose(mod_rows, (1, 0, 2)).reshape(nb, N_MOD, D)

    late = [s.astype(BF16) for s in (w_out[0], w_ffn_in[0], w_ffn_out[0])]
    late_send, late_recv, late_src, late_land, late_token = _exchange_start(
        late, [_own_block_in_place(s, me) for s in late], [True] * 3, mod_rows, "gather_late_start")
    g_norm1_t = g_norm1 + late_token[:1, :1]

    def late_weights(after):
        w_out_g, w_ffn_in_g, w_ffn_out_g = _exchange_wait(late_send, late_recv, late_src, late_land, [True] * 3, after,
                                                          "gather_late_wait")
        return w_out_g.reshape(D, D), _cols_from_blocks(w_ffn_in_g), w_ffn_out_g.reshape(D_FF, D)

    flight = {}

    def start_grads(key, src, name):
        land = [_own_block_in_place(lax.dynamic_index_in_dim(s, me, 0, keepdims=False), me) for s in src]
        send, recv, src, land, token = _exchange_start(src, land, [False] * len(src), src[0], name)
        flight[key] = (send, recv, src, land)
        return token[:1, :1]

    def half_blocks(g):
        return jnp.transpose(g.reshape(D, N_DEV // 2, 2 * D_FF // N_DEV), (1, 0, 2))

    def on_ffn_grads(gw_ffn_in, gw_ffn_out, gw_out):
        return start_grads("ffn", [jnp.concatenate([half_blocks(g) for g in gw_ffn_in], axis=0),
                                   gw_ffn_out.reshape(N_DEV, D_FF // N_DEV, D), gw_out.reshape(N_DEV, D // N_DEV, D)],
                           "exchange_ffn_start")

    def on_last_grads(gw):
        return start_grads("rest", [_cols_to_blocks(_unpad_w_in(gw["w_in"])), _cols_to_blocks(_unpad_w_uq(gw["w_uq"])),
                                    _cols_to_blocks(_join_w_ukv(gw["w_k"], gw["w_v"]))], "exchange_rest_start")

    grad_x, dmod, loss_cols, small, _ = _local_step(
        xt, mod, tt, g_norm1_t, _pad_w_in(w_in_f), g_cq, _pad_w_uq(w_uq_f), g_ckv, w_k, w_v, rel_bias, g_out_a, g_out_b,
        None, g_norm2, None, None, g_final.reshape(1, D), late_weights=late_weights, on_ffn_grads=on_ffn_grads,
        on_last_grads=on_last_grads)

    upd = {}

    def land_and_update(key, names, after, name):
        got = _exchange_wait(*flight[key], [False] * len(names), after, name)
        for n, p in zip(names, got):
            w, m, v = big[n]
            upd[n] = _adamw(p, w, m, v, "adamw_" + n)

    big = dict(w_in=(w_in, m_w_in, v_w_in), w_uq=(w_uq, m_w_uq, v_w_uq), w_ukv=(w_ukv, m_w_ukv, v_w_ukv),
               w_out=(w_out, m_w_out, v_w_out), w_ffn_in=(w_ffn_in, m_w_ffn_in, v_w_ffn_in),
               w_ffn_out=(w_ffn_out, m_w_ffn_out, v_w_ffn_out))
    land_and_update("ffn", ["w_ffn_in", "w_ffn_out", "w_out"], grad_x, "exchange_ffn_wait")
    land_and_update("rest", ["w_in", "w_uq", "w_ukv"], upd["w_out"][0], "exchange_rest_wait")

    mine, dmod_blocks = _pack_small(dmod, [small[n] for n, _ in ROW_PARAMS], loss_cols)
    dmod_cols, pay, rel = _exchange([dmod_blocks, mine, small["rel_bias"]], [False, True, True], "exchange_small",
                                    after=upd["w_ukv"][0])
    g_ada = _ada_grad(cond_all, dmod_cols.reshape(N_DEV * nb, ncol))
    upd["w_ada"] = _adamw(g_ada[None], w_ada, m_w_ada, v_w_ada, "adamw_w_ada")
    row = lambda a: a.reshape(1, D)
    small_names = ["b_ada"] + [n for n, _ in ROW_PARAMS] + ["rel_bias"]
    small_w = [b_ada, g_norm1, g_cq, g_ckv, g_out_a, g_out_b, g_norm2, row(g_final), rel_bias]
    small_m = [m_b_ada, m_g_norm1, m_g_cq, m_g_ckv, m_g_out_a, m_g_out_b, m_g_norm2, row(m_g_final), m_rel_bias]
    small_v = [v_b_ada, v_g_norm1, v_g_cq, v_g_ckv, v_g_out_a, v_g_out_b, v_g_norm2, row(v_g_final), v_rel_bias]
    small_upd, loss8 = _small_update(pay, rel, small_w, small_m, small_v)
    upd.update(zip(small_names, small_upd))

    order = ["w_ada", "b_ada", "g_norm1", "w_in", "g_cq", "w_uq", "g_ckv", "w_ukv", "rel_bias", "g_out_a", "g_out_b",
             "w_out", "g_norm2", "w_ffn_in", "w_ffn_out", "g_final"]
    like = dict(g_final=g_final)
    outs = [loss8[0, 0], grad_x.reshape(x.shape)]
    for part in range(4):
        for n in order:
            val = upd[n][part]
            outs.append(val.reshape(like[n].shape) if n in like else val)
    return tuple(outs)
```

```python
import functools

import numpy as np
import jax
import jax.numpy as jnp
from jax import lax
from jax.experimental import pallas as pl
from jax.experimental.pallas import tpu as pltpu

F32, BF16 = jnp.float32, jnp.bfloat16

N_DEV = 8
D = 1024
S = 2048
H = 8
E_A = 64
D_A = H * E_A
Q_LORA, KV_LORA = 384, 256
NOPE, ROPE, VDIM = 64, 32, 64
HP = 128
P_IN = 3 * D_A + Q_LORA + KV_LORA + ROPE
P_PAD = 3 * D_A + Q_LORA + KV_LORA + HP
TAIL0 = 3 * D_A
TAIL = P_PAD - TAIL0
D_FF = 2816
N_MOD = 6
EPS = 1e-6
NEG = -1e30
BLK = 128
DILATIONS = (1, 4, 16)
N_BUCKETS, MAX_DISTANCE = 32, 2048
ROPE_THETA = 10000.0
SCALE_A = E_A ** -0.5
SCALE_B = (NOPE + ROPE) ** -0.5
B1, B2, LR, ADAM_EPS, WD, STEP = 0.9, 0.999, 0.001, 1e-8, 0.01, 10
VMEM_LIMIT = 56 * 1024 * 1024


def _cp(*sem):
    return pltpu.CompilerParams(dimension_semantics=sem, vmem_limit_bytes=VMEM_LIMIT)


def _pick(n, prefs):
    for p in prefs:
        if n % p == 0:
            return p
    raise ValueError(f"no tile of {prefs} divides {n}")


OPERAND_BYTES = 6 * 1024 * 1024


def _pick_rows(m, k):
    return _pick(m, [p for p in (1024, 512, 256, 128, 16) if p * k * 2 <= OPERAND_BYTES])


MATMUL_BYTES = 40 * 1024 * 1024


def _stream_rows(m, fixed, per_row):
    return _pick(m, [p for p in (4096, 2048, 1024, 512, 256, 128, 16) if fixed + p * per_row <= MATMUL_BYTES])


def _dot(a, b, dims):
    return lax.dot_general(a, b, (dims, ((), ())), preferred_element_type=F32)


def _mm_nn(a, b, out_dtype, name):
    m, k = a.shape
    n = b.shape[1]
    tn = _pick(n, (512, 384, 256, 128))
    tm = _stream_rows(m, 4 * k * tn, 4 * k + (2 * jnp.dtype(out_dtype).itemsize + 4) * tn)

    def body(a_ref, b_ref, o_ref):
        o_ref[...] = _dot(a_ref[...], b_ref[...], ((1,), (0,))).astype(o_ref.dtype)

    return pl.pallas_call(
        body, name=name, grid=(m // tm, n // tn),
        in_specs=[pl.BlockSpec((tm, k), lambda i, j: (i, 0)), pl.BlockSpec((k, tn), lambda i, j: (0, j))],
        out_specs=pl.BlockSpec((tm, tn), lambda i, j: (i, j)),
        out_shape=jax.ShapeDtypeStruct((m, n), out_dtype),
        compiler_params=_cp("parallel", "parallel"),
    )(a, b)


def _mm_nt(a, b, out_dtype, name, after=None):
    m, k = a.shape
    n = b.shape[0]
    tn = _pick(n, (512, 384, 256, 128))
    tm = _stream_rows(m, 4 * k * tn, 4 * k + (2 * jnp.dtype(out_dtype).itemsize + 4) * tn)

    def body(a_ref, b_ref, *rest):
        o_ref = rest[-1]
        o_ref[...] = _dot(a_ref[...], b_ref[...], ((1,), (1,))).astype(o_ref.dtype)

    extra = [] if after is None else [after]
    return pl.pallas_call(
        body, name=name, grid=(m // tm, n // tn),
        in_specs=[pl.BlockSpec((tm, k), lambda i, j: (i, 0)), pl.BlockSpec((tn, k), lambda i, j: (j, 0))] + [ANY] * len(extra),
        out_specs=pl.BlockSpec((tm, tn), lambda i, j: (i, j)),
        out_shape=jax.ShapeDtypeStruct((m, n), out_dtype),
        compiler_params=_cp("parallel", "parallel"),
    )(a, b, *extra)


def _mm_tn(a, bs, name):
    t, m = a.shape
    n = bs[0].shape[1]
    nb_ = len(bs)
    tc = _pick(t, (512, 16))
    tn = _pick(n, (512, 384, 256, 128))
    tm = _pick(m, [p for p in (1024, 512, 384, 256, 128) if (3 * p + 2 * nb_ * tn) * t * 2 <= VMEM_LIMIT - 2 * OPERAND_BYTES])
    if tm <= 256 and nb_ * n * t * 2 <= 2 * OPERAND_BYTES:
        tn = n

    def body(*refs):
        a_ref, b_refs, o_refs, at_ref = refs[0], refs[1:1 + nb_], refs[1 + nb_:1 + 2 * nb_], refs[-1]

        @pl.when(pl.program_id(1) == 0)
        def _():
            def chunk(c, _):
                rows = pl.ds(pl.multiple_of(c * tc, tc), tc)
                at_ref[:, rows] = a_ref[rows, :].T
                return 0

            lax.fori_loop(0, t // tc, chunk, 0)

        for b_ref, o_ref in zip(b_refs, o_refs):
            o_ref[...] = _dot(at_ref[...], b_ref[...], ((1,), (0,))).astype(BF16)

    res = pl.pallas_call(
        body, name=name, grid=(m // tm, n // tn),
        in_specs=[pl.BlockSpec((t, tm), lambda i, j: (0, i))] + [pl.BlockSpec((t, tn), lambda i, j: (0, j))] * nb_,
        out_specs=[pl.BlockSpec((tm, tn), lambda i, j: (i, j))] * nb_,
        out_shape=[jax.ShapeDtypeStruct((m, n), BF16)] * nb_,
        scratch_shapes=[pltpu.VMEM((tm, t), BF16)],
        compiler_params=_cp("parallel", "arbitrary"),
    )(a, *bs)
    return res[0] if nb_ == 1 else res


EPI = 256


def _silu_parts(g):
    sg = 0.5 * jnp.tanh(0.5 * g) + 0.5
    return sg, g * sg


def _ffn_in(h2, wt):
    t, k = h2.shape
    tn = _pick(D_FF, (256, 128))
    tm = _stream_rows(t, 8 * k * tn, 4 * k + (3 * 2 * 2 + 2 * 4) * tn)
    nj = D_FF // tn

    def body(h_ref, wg_ref, wu_ref, g_ref, u_ref, a_ref):
        hv = h_ref[...]
        g_all = _dot(hv, wg_ref[...], ((1,), (1,)))
        u_all = _dot(hv, wu_ref[...], ((1,), (1,)))
        for r in range(tm // EPI):
            rows = slice(r * EPI, (r + 1) * EPI)
            g, u = g_all[rows], u_all[rows]
            g_ref[rows, :] = g.astype(BF16)
            u_ref[rows, :] = u.astype(BF16)
            a_ref[rows, :] = (_silu_parts(g)[1] * u).astype(BF16)

    blk = pl.BlockSpec((tm, tn), lambda i, j: (i, j))
    return pl.pallas_call(
        body, name="ffn_in", grid=(t // tm, nj),
        in_specs=[pl.BlockSpec((tm, k), lambda i, j: (i, 0)), pl.BlockSpec((tn, k), lambda i, j: (j, 0)),
                  pl.BlockSpec((tn, k), lambda i, j: (j + nj, 0))],
        out_specs=[blk] * 3, out_shape=[jax.ShapeDtypeStruct((t, D_FF), BF16)] * 3,
        compiler_params=_cp("parallel", "parallel"),
    )(h2, wt, wt)


def _d_act(df, w, g, u):
    t, k = df.shape
    tn = _pick(D_FF, (256, 128))
    tm = _stream_rows(t, 4 * k * tn, 4 * k + (4 * 2 * 2 + 4) * tn)

    def body(df_ref, w_ref, g_ref, u_ref, dg_ref, du_ref):
        da_all = _dot(df_ref[...], w_ref[...], ((1,), (1,)))
        for r in range(tm // EPI):
            rows = slice(r * EPI, (r + 1) * EPI)
            da = da_all[rows]
            gv = g_ref[rows, :].astype(F32)
            sg, silu = _silu_parts(gv)
            dg_ref[rows, :] = ((da * u_ref[rows, :].astype(F32)) * (sg + silu * (1.0 - sg))).astype(BF16)
            du_ref[rows, :] = (da * silu).astype(BF16)

    blk = pl.BlockSpec((tm, tn), lambda i, j: (i, j))
    return pl.pallas_call(
        body, name="d_act", grid=(t // tm, D_FF // tn),
        in_specs=[pl.BlockSpec((tm, k), lambda i, j: (i, 0)), pl.BlockSpec((tn, k), lambda i, j: (j, 0)), blk, blk],
        out_specs=[blk] * 2, out_shape=[jax.ShapeDtypeStruct((t, D_FF), BF16)] * 2,
        compiler_params=_cp("parallel", "parallel"),
    )(df, w, g, u)


def _d_h2(dg, du, wt):
    t = dg.shape[0]
    n = wt.shape[1]
    tm, tn = _pick_rows(t, D_FF), _pick(n, (512, 256, 128))

    def body(dg_ref, du_ref, wg_ref, wu_ref, o_ref):
        o_ref[...] = (_dot(dg_ref[...], wg_ref[...], ((1,), (0,)))
                      + _dot(du_ref[...], wu_ref[...], ((1,), (0,)))).astype(BF16)

    return pl.pallas_call(
        body, name="d_h2", grid=(t // tm, n // tn),
        in_specs=[pl.BlockSpec((tm, D_FF), lambda i, j: (i, 0)), pl.BlockSpec((tm, D_FF), lambda i, j: (i, 0)),
                  pl.BlockSpec((D_FF, tn), lambda i, j: (0, j)), pl.BlockSpec((D_FF, tn), lambda i, j: (1, j))],
        out_specs=pl.BlockSpec((tm, tn), lambda i, j: (i, j)),
        out_shape=jax.ShapeDtypeStruct((t, n), BF16),
        compiler_params=_cp("parallel", "parallel"),
    )(dg, du, wt, wt)


TM = 256


def _row(w):
    return pl.BlockSpec((TM, w), lambda i: (i, 0))


def _row_at(w, col):
    return pl.BlockSpec((TM, w), lambda i: (i, col))


def _vec(w):
    return pl.BlockSpec((1, w), lambda i: (0, 0))


def _per_ex(w):
    return pl.BlockSpec((1, 1, w), lambda i: (i // (S // TM), 0, 0))


def _pos(w):
    return pl.BlockSpec((TM, w), lambda i: (i % (S // TM), 0))


def _full(shape):
    return pl.BlockSpec(shape, lambda i: (0,) * len(shape))


def _rms(x):
    return lax.rsqrt(jnp.mean(x * x, axis=-1, keepdims=True) + EPS)


def _rms_bwd(n, r, dn):
    return r * (dn - n * jnp.mean(dn * n, axis=-1, keepdims=True))


def _colsum(v):
    return jnp.sum(v, axis=0, keepdims=True)


def _acc_first(i, ref, val, every=None):
    first = (i == 0) if every is None else (i % every == 0)

    @pl.when(first)
    def _():
        ref[...] = jnp.zeros_like(ref)

    ref[...] += val.reshape(ref.shape)


def _pre1(x, g, sc, sh):
    t = x.shape[0]

    def body(x_ref, g_ref, sc_ref, sh_ref, h_ref):
        xv = x_ref[...]
        n = xv * _rms(xv)
        h_ref[...] = ((n * g_ref[...]) * (1.0 + sc_ref[0]) + sh_ref[0]).astype(BF16)

    return pl.pallas_call(
        body, name="pre1", grid=(t // TM,),
        in_specs=[_row(D), _vec(D), _per_ex(D), _per_ex(D)],
        out_specs=_row(D), out_shape=jax.ShapeDtypeStruct((t, D), BF16),
        compiler_params=_cp("parallel"),
    )(x, g, sc, sh)


def _rope_fwd(v, c, sm, sp):
    return v * c + pltpu.roll(v, HP - ROPE // 2, 1) * sm + pltpu.roll(v, ROPE // 2, 1) * sp


def _rope_bwd(dv, c, sm, sp):
    return dv * c + pltpu.roll(dv * sm, ROPE // 2, 1) + pltpu.roll(dv * sp, HP - ROPE // 2, 1)


def _mla_pre(proj, g_cq, g_ckv, w_uq, w_k, w_v, rc, rsm, rsp):
    t = proj.shape[0]

    def body(tail_ref, gq_ref, gkv_ref, wuq_ref, wk_ref, wv_ref, c_ref, sm_ref, sp_ref,
             q_ref, k_ref, v_ref, cqn_ref, ckvn_ref):
        tail = tail_ref[...]
        cq, ckv, kr = tail[:, :Q_LORA], tail[:, Q_LORA:Q_LORA + KV_LORA], tail[:, Q_LORA + KV_LORA:]
        cqn = (cq * _rms(cq) * gq_ref[...]).astype(BF16)
        ckvn = (ckv * _rms(ckv) * gkv_ref[...]).astype(BF16)
        cqn_ref[...] = cqn
        ckvn_ref[...] = ckvn
        c, sm, sp = c_ref[...], sm_ref[...], sp_ref[...]
        q = _dot(cqn, wuq_ref[...], ((1,), (0,)))
        kn = _dot(ckvn, wk_ref[...], ((1,), (0,)))
        v_ref[...] = _dot(ckvn, wv_ref[...], ((1,), (0,))).astype(BF16)
        krr = _rope_fwd(kr, c, sm, sp)
        for h in range(H):
            sl = slice(h * HP, (h + 1) * HP)
            q_ref[:, sl] = _rope_fwd(q[:, sl], c, sm, sp).astype(BF16)
            k_ref[:, sl] = (kn[:, sl] + krr).astype(BF16)

    wide = H * HP
    return pl.pallas_call(
        body, name="mla_pre", grid=(t // TM,),
        in_specs=[_row_at(TAIL, TAIL0 // TAIL), _vec(Q_LORA), _vec(KV_LORA), _full((Q_LORA, wide)),
                  _full((KV_LORA, wide)), _full((KV_LORA, wide)), _pos(HP), _pos(HP), _pos(HP)],
        out_specs=[_row(wide), _row(wide), _row(wide), _row(Q_LORA), _row(KV_LORA)],
        out_shape=[jax.ShapeDtypeStruct((t, wide), BF16)] * 3
        + [jax.ShapeDtypeStruct((t, Q_LORA), BF16), jax.ShapeDtypeStruct((t, KV_LORA), BF16)],
        compiler_params=_cp("parallel"),
    )(proj, g_cq, g_ckv, w_uq, w_k, w_v, rc, rsm, rsp)


def _mla_pre_bwd(proj, dq_, dk_, dv_, dqkv_a, g_cq, g_ckv, w_uq, w_k, w_v, rc, rsm, rsp):
    t = proj.shape[0]
    wide = H * HP

    def body(tail_ref, dq_ref, dk_ref, dv_ref, dqa_ref, dka_ref, dva_ref, gq_ref, gkv_ref, wuq_ref, wk_ref, wv_ref,
             c_ref, sm_ref, sp_ref, dqo_ref, dproj_ref, dgq_ref, dgkv_ref):
        i = pl.program_id(0)
        for n, src in enumerate((dqa_ref, dka_ref, dva_ref)):
            dproj_ref[:, n * D_A:(n + 1) * D_A] = src[...]
        dtail_ref = dproj_ref.at[:, TAIL0:]
        tail = tail_ref[...]
        cq, ckv = tail[:, :Q_LORA], tail[:, Q_LORA:Q_LORA + KV_LORA]
        c, sm, sp = c_ref[...], sm_ref[...], sp_ref[...]
        dkr = jnp.zeros((TM, HP), F32)
        for h in range(H):
            sl = slice(h * HP, (h + 1) * HP)
            dqo_ref[:, sl] = _rope_bwd(dq_ref[:, sl].astype(F32), c, sm, sp).astype(BF16)
            dkr = dkr + dk_ref[:, sl].astype(F32)
        lane = lax.broadcasted_iota(jnp.int32, (TM, HP), 1)
        dkr = jnp.where((lane >= NOPE) & (lane < NOPE + ROPE), _rope_bwd(dkr, c, sm, sp), 0.0)
        dkb = dk_ref[...]
        dvb = dv_ref[...]
        dcqn = _dot(dqo_ref[...], wuq_ref[...], ((1,), (1,)))
        dckvn = _dot(dkb, wk_ref[...], ((1,), (1,))) + _dot(dvb, wv_ref[...], ((1,), (1,)))
        rq, rkv = _rms(cq), _rms(ckv)
        nq, nkv = cq * rq, ckv * rkv
        _acc_first(i, dgq_ref, _colsum(dcqn * nq))
        _acc_first(i, dgkv_ref, _colsum(dckvn * nkv))
        dtail_ref[:, :Q_LORA] = _rms_bwd(nq, rq, dcqn * gq_ref[...]).astype(BF16)
        dtail_ref[:, Q_LORA:Q_LORA + KV_LORA] = _rms_bwd(nkv, rkv, dckvn * gkv_ref[...]).astype(BF16)
        dtail_ref[:, Q_LORA + KV_LORA:] = dkr.astype(BF16)

    return pl.pallas_call(
        body, name="mla_pre_bwd", grid=(t // TM,),
        in_specs=[_row_at(TAIL, TAIL0 // TAIL), _row(wide), _row(wide), _row(wide), _row(D_A), _row(D_A), _row(D_A),
                  _vec(Q_LORA), _vec(KV_LORA), _full((Q_LORA, wide)), _full((KV_LORA, wide)), _full((KV_LORA, wide)),
                  _pos(HP), _pos(HP), _pos(HP)],
        out_specs=[_row(wide), _row(P_PAD), _vec(Q_LORA), _vec(KV_LORA)],
        out_shape=[jax.ShapeDtypeStruct((t, wide), BF16), jax.ShapeDtypeStruct((t, P_PAD), BF16),
                   jax.ShapeDtypeStruct((1, Q_LORA), F32), jax.ShapeDtypeStruct((1, KV_LORA), F32)],
        compiler_params=_cp("arbitrary"),
    )(proj, dq_, dk_, dv_, *dqkv_a, g_cq, g_ckv, w_uq, w_k, w_v, rc, rsm, rsp)


def _post_attn(out_a, out_b, g_a, g_b):
    t = out_a.shape[0]

    def body(a_ref, b_ref, ga_ref, gb_ref, y_ref):
        a, b = a_ref[...], b_ref[...]
        y_ref[:, :D_A] = (a * _rms(a) * ga_ref[...]).astype(BF16)
        y_ref[:, D_A:] = (b * _rms(b) * gb_ref[...]).astype(BF16)

    return pl.pallas_call(
        body, name="post_attn", grid=(t // TM,),
        in_specs=[_row(D_A), _row(D_A), _vec(D_A), _vec(D_A)],
        out_specs=_row(D), out_shape=jax.ShapeDtypeStruct((t, D), BF16),
        compiler_params=_cp("parallel"),
    )(out_a, out_b, g_a, g_b)


def _post_attn_bwd(dy, out_a, out_b, g_a, g_b):
    t = dy.shape[0]

    def body(dy_ref, a_ref, b_ref, ga_ref, gb_ref, da_ref, db_ref, dga_ref, dgb_ref):
        i = pl.program_id(0)
        dy_ = dy_ref[...].astype(F32)
        for src, g_ref, dst, dg_ref, sl in ((a_ref, ga_ref, da_ref, dga_ref, slice(0, D_A)),
                                            (b_ref, gb_ref, db_ref, dgb_ref, slice(D_A, D))):
            v = src[...]
            r = _rms(v)
            n = v * r
            dyv = dy_[:, sl]
            _acc_first(i, dg_ref, _colsum(dyv * n))
            dst[...] = _rms_bwd(n, r, dyv * g_ref[...])

    return pl.pallas_call(
        body, name="post_attn_bwd", grid=(t // TM,),
        in_specs=[_row(D), _row(D_A), _row(D_A), _vec(D_A), _vec(D_A)],
        out_specs=[_row(D_A), _row(D_A), _vec(D_A), _vec(D_A)],
        out_shape=[jax.ShapeDtypeStruct((t, D_A), F32)] * 2 + [jax.ShapeDtypeStruct((1, D_A), F32)] * 2,
        compiler_params=_cp("arbitrary"),
    )(dy, out_a, out_b, g_a, g_b)


def _resid_norm2(x, mix, g1, g, sc, sh):
    t = x.shape[0]

    def body(x_ref, mix_ref, g1_ref, g_ref, sc_ref, sh_ref, x2_ref, h_ref):
        x2 = x_ref[...] + g1_ref[0] * mix_ref[...]
        x2_ref[...] = x2
        n = x2 * _rms(x2)
        h_ref[...] = ((n * g_ref[...]) * (1.0 + sc_ref[0]) + sh_ref[0]).astype(BF16)

    return pl.pallas_call(
        body, name="resid_norm2", grid=(t // TM,),
        in_specs=[_row(D), _row(D), _per_ex(D), _vec(D), _per_ex(D), _per_ex(D)],
        out_specs=[_row(D), _row(D)],
        out_shape=[jax.ShapeDtypeStruct((t, D), F32), jax.ShapeDtypeStruct((t, D), BF16)],
        compiler_params=_cp("parallel"),
    )(x, mix, g1, g, sc, sh)


def _sigmoid(v):
    return 1.0 / (1.0 + jnp.exp(-v))


def _final(x2, f, g2, g_fin, target):
    t = x2.shape[0]
    nb = t // S
    tpb = S // TM

    def body(x2_ref, f_ref, g2_ref, g_ref, t_ref, dx3_ref, df_ref, loss_ref, dgf_ref, dg2_ref):
        i = pl.program_id(0)
        fv = f_ref[...].astype(F32)
        x3 = x2_ref[...] + g2_ref[0] * fv
        r = _rms(x3)
        n = x3 * r
        err = n * g_ref[...] - t_ref[...]
        _acc_first(i, loss_ref, _colsum(err * err))
        dy = err * (1.0 / D)
        _acc_first(i, dgf_ref, _colsum(dy * n))
        dx3 = _rms_bwd(n, r, dy * g_ref[...])
        dx3_ref[...] = dx3
        _acc_first(i, dg2_ref, _colsum(dx3 * fv), every=tpb)
        df_ref[...] = (dx3 * g2_ref[0]).astype(BF16)

    return pl.pallas_call(
        body, name="final", grid=(t // TM,),
        in_specs=[_row(D), _row(D), _per_ex(D), _vec(D), _row(D)],
        out_specs=[_row(D), _row(D), _vec(D), _vec(D), _per_ex(D)],
        out_shape=[jax.ShapeDtypeStruct((t, D), F32), jax.ShapeDtypeStruct((t, D), BF16),
                   jax.ShapeDtypeStruct((1, D), F32), jax.ShapeDtypeStruct((1, D), F32),
                   jax.ShapeDtypeStruct((nb, 1, D), F32)],
        compiler_params=_cp("arbitrary"),
    )(x2, f, g2, g_fin, target)


def _norm_bwd(xin, dh, dres, g, sc, gate=None):
    t = xin.shape[0]
    nb = t // S
    tpb = S // TM
    gated = gate is not None

    def body(*refs):
        if gated:
            x_ref, dh_ref, dres_ref, g_ref, sc_ref, mix_ref, g1_ref, dx_ref, dsh_ref, dsc_ref, dg_ref, dg1_ref, dmix_ref = refs
        else:
            x_ref, dh_ref, dres_ref, g_ref, sc_ref, dx_ref, dsh_ref, dsc_ref, dg_ref = refs
        i = pl.program_id(0)
        xv, dhv = x_ref[...], dh_ref[...].astype(F32)
        r = _rms(xv)
        n = xv * r
        gv = g_ref[...]
        _acc_first(i, dsh_ref, _colsum(dhv), every=tpb)
        _acc_first(i, dsc_ref, _colsum(dhv * (n * gv)), every=tpb)
        dng = dhv * (1.0 + sc_ref[0])
        _acc_first(i, dg_ref, _colsum(dng * n))
        dx = dres_ref[...] + _rms_bwd(n, r, dng * gv)
        dx_ref[...] = dx
        if gated:
            _acc_first(i, dg1_ref, _colsum(dx * mix_ref[...].astype(F32)), every=tpb)
            dmix_ref[...] = (dx * g1_ref[0]).astype(BF16)

    in_specs = [_row(D), _row(D), _row(D), _vec(D), _per_ex(D)]
    out_specs = [_row(D), _per_ex(D), _per_ex(D), _vec(D)]
    out_shape = [jax.ShapeDtypeStruct((t, D), F32), jax.ShapeDtypeStruct((nb, 1, D), F32),
                 jax.ShapeDtypeStruct((nb, 1, D), F32), jax.ShapeDtypeStruct((1, D), F32)]
    args = [xin, dh, dres, g, sc]
    if gated:
        in_specs += [_row(D), _per_ex(D)]
        out_specs += [_per_ex(D), _row(D)]
        out_shape += [jax.ShapeDtypeStruct((nb, 1, D), F32), jax.ShapeDtypeStruct((t, D), BF16)]
        args += list(gate)
    return pl.pallas_call(
        body, name="norm2_bwd" if gated else "norm1_bwd", grid=(t // TM,),
        in_specs=in_specs, out_specs=out_specs, out_shape=out_shape,
        compiler_params=_cp("arbitrary"),
    )(*args)


TQ = 256
TB = 512


def _mla_fwd(q, k, v):
    t = q.shape[0]
    nb = t // S

    def body(q_ref, k_ref, v_ref, o_ref, lse_ref):
        causal = lax.broadcasted_iota(jnp.int32, (TB, TB), 0) >= lax.broadcasted_iota(jnp.int32, (TB, TB), 1)
        heads = [slice(h * HP, (h + 1) * HP) for h in range(2)]
        for i in range(S // TB):
            ri, past = slice(i * TB, (i + 1) * TB), slice(0, i * TB)
            qhs = [q_ref[ri, sl] for sl in heads]
            sd = [jnp.where(causal, _dot(qh, k_ref[ri, sl], ((1,), (1,))) * SCALE_B, NEG) for qh, sl in zip(qhs, heads)]
            ms = [jnp.max(s, axis=-1, keepdims=True) for s in sd]
            if i:
                so = [_dot(qh, k_ref[past, sl], ((1,), (1,))) * SCALE_B for qh, sl in zip(qhs, heads)]
                ms = [jnp.maximum(m, jnp.max(s, axis=-1, keepdims=True)) for m, s in zip(ms, so)]
            pd = [jnp.exp(s - m) for s, m in zip(sd, ms)]
            ls = [jnp.sum(p, axis=-1, keepdims=True) for p in pd]
            acc = [_dot(p.astype(BF16), v_ref[ri, sl], ((1,), (0,))) for p, sl in zip(pd, heads)]
            if i:
                po = [jnp.exp(s - m) for s, m in zip(so, ms)]
                ls = [l + jnp.sum(p, axis=-1, keepdims=True) for l, p in zip(ls, po)]
                acc = [a + _dot(p.astype(BF16), v_ref[past, sl], ((1,), (0,))) for a, p, sl in zip(acc, po, heads)]
            o_ref[ri, :] = acc[0] / ls[0] + acc[1] / ls[1]
            for sl, m, l in zip(heads, ms, ls):
                lse_ref[ri, sl] = jnp.broadcast_to(m + jnp.log(l), (TB, HP))

    wide2 = pl.BlockSpec((S, 2 * HP), lambda b, p: (b, p))
    return pl.pallas_call(
        body, name="mla_fwd", grid=(nb, H // 2),
        in_specs=[wide2, wide2, wide2],
        out_specs=[pl.BlockSpec((S, HP), lambda b, p: (b, p)), wide2],
        out_shape=[jax.ShapeDtypeStruct((t, H * VDIM), F32), jax.ShapeDtypeStruct((t, H * HP), F32)],
        compiler_params=_cp("parallel", "parallel"),
    )(q, k, v)


def _mla_bwd(q, k, v, o, do, lse):
    t = q.shape[0]
    nb = t // S

    def body(q_ref, k_ref, v_ref, o_ref, do_ref, lse_ref, dq_out, dk_out, dv_out, dq_ref, dk_ref, dv_ref):
        lane = lax.broadcasted_iota(jnp.int32, (TB, HP), 1)
        causal = lax.broadcasted_iota(jnp.int32, (TB, TB), 0) >= lax.broadcasted_iota(jnp.int32, (TB, TB), 1)
        heads = [slice(h * HP, (h + 1) * HP) for h in range(2)]
        nblk = S // TB
        for i in reversed(range(nblk)):
            ri, past = slice(i * TB, (i + 1) * TB), slice(0, i * TB)
            dov = do_ref[ri, :]
            prod = dov * o_ref[ri, :]
            dob = dov.astype(BF16)
            deltas = [jnp.sum(jnp.where((lane < VDIM) if h == 0 else (lane >= VDIM), prod, 0.0), axis=-1, keepdims=True)
                      for h in range(2)]
            qhs = [q_ref[ri, sl] for sl in heads]
            lses = [lse_ref[ri, sl][:, :1] for sl in heads]
            for rows, diagonal in ((ri, True), (past, False)):
                if rows.stop == rows.start:
                    continue
                ps = [jnp.exp(_dot(qh, k_ref[rows, sl], ((1,), (1,))) * SCALE_B - lse) for qh, sl, lse in zip(qhs, heads, lses)]
                if diagonal:
                    ps = [jnp.where(causal, p, 0.0) for p in ps]
                dps = [_dot(dob, v_ref[rows, sl], ((1,), (1,))) for sl in heads]
                dss = [(p * (dp - delta) * SCALE_B).astype(BF16) for p, dp, delta in zip(ps, dps, deltas)]
                for sl, qh, p, ds in zip(heads, qhs, ps, dss):
                    dq = _dot(ds, k_ref[rows, sl], ((1,), (0,)))
                    dk = _dot(ds, qh, ((0,), (0,)))
                    dv = _dot(p.astype(BF16), dob, ((0,), (0,)))
                    if diagonal:
                        dq_ref[ri, sl] = dq
                    else:
                        dq_ref[ri, sl] += dq
                    if i == nblk - 1:
                        dk_ref[rows, sl] = dk
                        dv_ref[rows, sl] = dv
                    else:
                        dk_ref[rows, sl] += dk
                        dv_ref[rows, sl] += dv
        dq_out[...] = dq_ref[...].astype(BF16)
        dk_out[...] = dk_ref[...].astype(BF16)
        dv_out[...] = dv_ref[...].astype(BF16)

    wide2 = pl.BlockSpec((S, 2 * HP), lambda b, p: (b, p))
    pair = pl.BlockSpec((S, HP), lambda b, p: (b, p))
    return pl.pallas_call(
        body, name="mla_bwd", grid=(nb, H // 2),
        in_specs=[wide2, wide2, wide2, pair, pair, wide2],
        out_specs=[wide2, wide2, wide2],
        out_shape=[jax.ShapeDtypeStruct((t, H * HP), BF16)] * 3,
        scratch_shapes=[pltpu.VMEM((S, 2 * HP), F32)] * 3,
        compiler_params=_cp("parallel", "parallel"),
    )(q, k, v, o, do, lse)


def _t5_bucket(dist):
    max_exact = N_BUCKETS // 2
    d = np.maximum(dist, 1).astype(np.float64)
    large = max_exact + (np.log(d / max_exact) / np.log(MAX_DISTANCE / max_exact) * (N_BUCKETS - max_exact)).astype(np.int64)
    large = np.minimum(large, N_BUCKETS - 1)
    return np.where(dist < max_exact, dist, large).astype(np.int32)


def _band_geometry():
    a = np.arange(BLK)[:, None]
    bk = np.arange(2 * BLK)[None, :]
    steps = BLK + a - bk
    valid = (steps >= 0) & (steps <= BLK)
    buckets = np.stack([_t5_bucket(np.clip(steps, 0, BLK) * d) for d in DILATIONS])
    return buckets, valid


def _band_bias(rel_bias):
    buckets, valid = _band_geometry()
    onehot = (jnp.asarray(buckets)[..., None] == jnp.arange(N_BUCKETS)).astype(F32)
    bias = jnp.einsum("rqkn,nh->rhqk", onehot, rel_bias, precision=lax.Precision.HIGHEST)
    bias = jnp.where(jnp.asarray(valid)[None, None], bias, NEG)
    return bias.reshape(3, H // 2, 2 * BLK, 2 * BLK)


def _dil_items():
    items = []
    for r, d in enumerate(DILATIONS):
        for res in range(d):
            for blk in range(S // d // BLK):
                items.append((r, d, blk * BLK * d + res, blk > 0))
    return items


GROUP = 4


def _strided(start, d):
    return pl.ds(start, BLK) if d == 1 else pl.ds(start, BLK, stride=d)


def _stack_heads(tile, own):
    return jnp.where(own, jnp.concatenate([tile, tile], axis=0), 0.0).astype(BF16)


def _own_lanes():
    row = lax.broadcasted_iota(jnp.int32, (2 * BLK, HP), 0)
    lane = lax.broadcasted_iota(jnp.int32, (2 * BLK, HP), 1)
    return (lane < E_A) == (row < BLK)


def _dil_fwd(proj, biasm):
    t = proj.shape[0]
    nb = t // S

    def body(q_ref, k_ref, v_ref, b_ref, o_ref, lse_ref, ob_ref, lb_ref):
        lane = lax.broadcasted_iota(jnp.int32, (BLK, HP), 1)
        own = _own_lanes()
        items = _dil_items()
        for g in range(0, len(items), GROUP):
            grp = items[g:g + GROUP]
            ss, vts = [], []
            for r, d, start, has_prev in grp:
                cur = _strided(start, d)
                rows = [_strided(start - BLK * d, d), cur] if has_prev else [cur]
                q2 = _stack_heads(q_ref[cur, :] * SCALE_A, own)
                kt = jnp.concatenate([k_ref[x, :] for x in rows], axis=0).astype(BF16)
                vts.append(jnp.concatenate([v_ref[x, :] for x in rows], axis=0).astype(BF16))
                bias = b_ref[r, 0] if has_prev else b_ref[r, 0, :, BLK:]
                ss.append(_dot(q2, kt, ((1,), (1,))) + bias)
            ms = [jnp.max(s, axis=-1, keepdims=True) for s in ss]
            ps = [jnp.exp(s - m) for s, m in zip(ss, ms)]
            ls = [jnp.sum(p, axis=-1, keepdims=True) for p in ps]
            for (r, d, start, _), p, vt, m, l in zip(grp, ps, vts, ms, ls):
                cur = _strided(start, d)
                o2 = _dot(p.astype(BF16), vt, ((1,), (0,))) / l
                lse2 = m + jnp.log(l)
                ob_ref[r, cur, :] = jnp.where(lane < E_A, o2[:BLK], o2[BLK:])
                lb_ref[r, cur, :] = jnp.where(lane < E_A, lse2[:BLK], lse2[BLK:])

        def merge(c, _):
            rows = pl.ds(pl.multiple_of(c * TQ, TQ), TQ)
            l0, l1, l2 = lb_ref[0, rows, :], lb_ref[1, rows, :], lb_ref[2, rows, :]
            m = jnp.maximum(jnp.maximum(l0, l1), l2)
            e0, e1, e2 = jnp.exp(l0 - m), jnp.exp(l1 - m), jnp.exp(l2 - m)
            tot = e0 + e1 + e2
            o_ref[rows, :] = (e0 * ob_ref[0, rows, :] + e1 * ob_ref[1, rows, :] + e2 * ob_ref[2, rows, :]) / tot
            lse_ref[rows, :] = m + jnp.log(tot)
            return 0

        lax.fori_loop(0, S // TQ, merge, 0)

    npair = H // 2
    return pl.pallas_call(
        body, name="dil_fwd", grid=(nb, npair),
        in_specs=[pl.BlockSpec((S, HP), lambda b, p: (b, p)), pl.BlockSpec((S, HP), lambda b, p: (b, npair + p)),
                  pl.BlockSpec((S, HP), lambda b, p: (b, 2 * npair + p)),
                  pl.BlockSpec((3, 1, 2 * BLK, 2 * BLK), lambda b, p: (0, p, 0, 0))],
        out_specs=[pl.BlockSpec((S, HP), lambda b, p: (b, p))] * 2,
        out_shape=[jax.ShapeDtypeStruct((t, D_A), F32)] * 2,
        scratch_shapes=[pltpu.VMEM((3, S, HP), F32), pltpu.VMEM((3, S, HP), F32)],
        compiler_params=_cp("parallel", "parallel"),
    )(proj, proj, proj, biasm)


def _dil_bwd(proj, biasm, o, do, lse):
    t = proj.shape[0]
    nb = t // S

    def body(q_ref, k_ref, v_ref, b_ref, o_ref, do_ref, lse_ref, dq_out, dk_out, dv_out, ds_ref, dq_ref, dk_ref, dv_ref):
        dq_ref[...] = jnp.zeros_like(dq_ref)
        dk_ref[...] = jnp.zeros_like(dk_ref)
        dv_ref[...] = jnp.zeros_like(dv_ref)
        ds_ref[...] = jnp.zeros_like(ds_ref)
        lane = lax.broadcasted_iota(jnp.int32, (BLK, HP), 1)
        own = _own_lanes()
        items = _dil_items()
        for g in range(0, len(items), GROUP):
            grp = items[g:g + GROUP]
            q2s, kts, do2s, ss, dps, lse2s, delta2s = [], [], [], [], [], [], []
            for r, d, start, has_prev in grp:
                cur = _strided(start, d)
                rows = [_strided(start - BLK * d, d), cur] if has_prev else [cur]
                q2 = _stack_heads(q_ref[cur, :] * SCALE_A, own)
                kt = jnp.concatenate([k_ref[x, :] for x in rows], axis=0).astype(BF16)
                vt = jnp.concatenate([v_ref[x, :] for x in rows], axis=0).astype(BF16)
                dot_ = do_ref[cur, :]
                prod = dot_ * o_ref[cur, :]
                lset = lse_ref[cur, :]
                do2 = _stack_heads(dot_, own)
                bias = b_ref[r, 0] if has_prev else b_ref[r, 0, :, BLK:]
                ss.append(_dot(q2, kt, ((1,), (1,))) + bias)
                dps.append(_dot(do2, vt, ((1,), (1,))))
                lse2s.append(jnp.concatenate([lset[:, :1], lset[:, E_A:E_A + 1]], axis=0))
                delta2s.append(jnp.concatenate([jnp.sum(jnp.where(lane < E_A, prod, 0.0), axis=-1, keepdims=True),
                                                jnp.sum(jnp.where(lane >= E_A, prod, 0.0), axis=-1, keepdims=True)], axis=0))
                q2s.append(q2)
                kts.append(kt)
                do2s.append(do2)
            ps = [jnp.exp(s - lse2) for s, lse2 in zip(ss, lse2s)]
            dls = [p * (dp - delta2) for p, dp, delta2 in zip(ps, dps, delta2s)]
            for (r, d, start, has_prev), q2, kt, do2, p, dl in zip(grp, q2s, kts, do2s, ps, dls):
                cur = _strided(start, d)
                dsb = dl.astype(BF16)
                dq2 = _dot(dsb, kt, ((1,), (0,))) * SCALE_A
                dkt = _dot(dsb, q2, ((0,), (0,)))
                dvt = _dot(p.astype(BF16), do2, ((0,), (0,)))
                dq_ref[cur, :] += jnp.where(lane < E_A, dq2[:BLK], dq2[BLK:])
                if has_prev:
                    prev = _strided(start - BLK * d, d)
                    ds_ref[0, r, 0] += dl
                    dk_ref[prev, :] += dkt[:BLK]
                    dv_ref[prev, :] += dvt[:BLK]
                    dk_ref[cur, :] += dkt[BLK:]
                    dv_ref[cur, :] += dvt[BLK:]
                else:
                    ds_ref[0, r, 0, :, BLK:] += dl
                    dk_ref[cur, :] += dkt
                    dv_ref[cur, :] += dvt
        dq_out[...] = dq_ref[...].astype(BF16)
        dk_out[...] = dk_ref[...].astype(BF16)
        dv_out[...] = dv_ref[...].astype(BF16)

    npair = H // 2
    pair = pl.BlockSpec((S, HP), lambda b, p: (b, p))
    return pl.pallas_call(
        body, name="dil_bwd", grid=(nb, npair),
        in_specs=[pair, pl.BlockSpec((S, HP), lambda b, p: (b, npair + p)),
                  pl.BlockSpec((S, HP), lambda b, p: (b, 2 * npair + p)),
                  pl.BlockSpec((3, 1, 2 * BLK, 2 * BLK), lambda b, p: (0, p, 0, 0)), pair, pair, pair],
        out_specs=[pair, pair, pair, pl.BlockSpec((1, 3, 1, 2 * BLK, 2 * BLK), lambda b, p: (b, 0, p, 0, 0))],
        out_shape=[jax.ShapeDtypeStruct((t, D_A), BF16)] * 3 + [jax.ShapeDtypeStruct((nb, 3, npair, 2 * BLK, 2 * BLK), F32)],
        scratch_shapes=[pltpu.VMEM((S, HP), F32)] * 3,
        compiler_params=_cp("parallel", "parallel"),
    )(proj, proj, proj, biasm, o, do, lse)


def _rel_bias_grad(dlogits):
    nb = dlogits.shape[0]
    buckets, _ = _band_geometry()
    kk = 3 * BLK * 2 * BLK
    dl = jnp.transpose(dlogits.reshape(nb, 3, H, BLK, 2 * BLK), (0, 2, 1, 3, 4)).reshape(nb, H, kk)
    bk = jnp.asarray(buckets.reshape(1, kk))
    tk = kk // 12

    def body(dl_ref, bk_ref, o_ref):
        j = pl.program_id(0)
        onehot = (bk_ref[...] == lax.broadcasted_iota(jnp.int32, (N_BUCKETS, tk), 0)).astype(F32)
        tot = dl_ref[0]
        for b in range(1, nb):
            tot = tot + dl_ref[b]
        part = lax.dot_general(onehot, tot, ((((1,), (1,))), ((), ())), preferred_element_type=F32,
                               precision=lax.Precision.HIGHEST)
        _acc_first(j, o_ref, part)

    return pl.pallas_call(
        body, name="rel_bias_grad", grid=(kk // tk,),
        in_specs=[pl.BlockSpec((nb, H, tk), lambda j: (0, 0, j)), pl.BlockSpec((1, tk), lambda j: (0, j))],
        out_specs=pl.BlockSpec((N_BUCKETS, H), lambda j: (0, 0)),
        out_shape=jax.ShapeDtypeStruct((N_BUCKETS, H), F32),
        compiler_params=_cp("arbitrary"),
    )(dl, bk)


def _mesh_place():
    x, y, c = lax.axis_index("x"), lax.axis_index("y"), lax.axis_index("c")
    return x, y, c


def _peer(k):
    x, y, c = _mesh_place()
    px = 1 - x if k & 4 else x
    py = 1 - y if k & 2 else y
    pc = 1 - c if k & 1 else c
    return (px, py, pc), 4 * px + 2 * py + pc


ANY = pl.BlockSpec(memory_space=pl.ANY)


def _exchange(arrays, gathers, name, after=None):
    n_arr = len(arrays)

    def body(*refs):
        ins, outs = refs[:n_arr], refs[n_arr + 1:2 * n_arr + 1]
        send, recv, loc = refs[2 * n_arr + 1:]
        x, y, c = _mesh_place()
        me = 4 * x + 2 * y + c
        local = [pltpu.make_async_copy(ins[a] if gathers[a] else ins[a].at[me], outs[a].at[me], loc.at[a])
                 for a in range(n_arr)]
        remote = _peer_copies(ins, outs, send, recv, gathers)
        for cp in local:
            cp.start()
        for put, _ in remote:
            put.start()
        for cp in local:
            cp.wait()
        for put, got in remote:
            put.wait_send()
            got.wait_recv()

    return pl.pallas_call(
        body, name=name,
        in_specs=[ANY] * (n_arr + 1), out_specs=[ANY] * n_arr,
        out_shape=[jax.ShapeDtypeStruct(((N_DEV,) if g else ()) + a.shape, a.dtype) for a, g in zip(arrays, gathers)],
        scratch_shapes=[pltpu.SemaphoreType.DMA((n_arr * (N_DEV - 1),)), pltpu.SemaphoreType.DMA((n_arr * (N_DEV - 1),)),
                        pltpu.SemaphoreType.DMA((n_arr,))],
        compiler_params=pltpu.CompilerParams(has_side_effects=True),
    )(*arrays, arrays[0] if after is None else after)


def _gather_two_level(arrays, name):
    n_arr = len(arrays)
    per = N_DEV - 1

    def body(*refs):
        ins, outs = refs[:n_arr], refs[n_arr:2 * n_arr]
        send, recv, loc = refs[2 * n_arr:]
        x, y, c = _mesh_place()
        me, sibling = (x, y, c), (x, y, 1 - c)
        chips = [(1 - x, y), (x, 1 - y), (1 - x, 1 - y)]

        def block(a, place):
            px, py, pc = place
            return outs[a].at[4 * px + 2 * py + pc]

        def copy(a, k, place, to, src=None):
            dst = block(a, place)
            return pltpu.make_async_remote_copy(dst if src is None else src, dst, send.at[a * per + k], recv.at[a * per + k],
                                                device_id=to, device_id_type=pl.DeviceIdType.MESH)

        local = [pltpu.make_async_copy(ins[a], block(a, me), loc.at[a]) for a in range(n_arr)]
        for cp in local:
            cp.start()
        first = []
        for a in range(n_arr):
            first.append(copy(a, 0, me, sibling, src=ins[a]))
            first += [copy(a, 1 + j, me, (*chip, c), src=ins[a]) for j, chip in enumerate(chips)]
        for cp in first:
            cp.start()
        passed = []
        for j, chip in enumerate(chips):
            for a in range(n_arr):
                copy(a, 1 + j, (*chip, c), me).wait_recv()
                passed.append(copy(a, 4 + j, (*chip, c), sibling))
                passed[-1].start()
        for a in range(n_arr):
            copy(a, 0, sibling, me).wait_recv()
            for j, chip in enumerate(chips):
                copy(a, 4 + j, (*chip, 1 - c), me).wait_recv()
        for cp in first + passed:
            cp.wait_send()
        for cp in local:
            cp.wait()

    return pl.pallas_call(
        body, name=name,
        in_specs=[ANY] * n_arr, out_specs=[ANY] * n_arr,
        out_shape=[jax.ShapeDtypeStruct((N_DEV,) + a.shape, a.dtype) for a in arrays],
        scratch_shapes=[pltpu.SemaphoreType.DMA((n_arr * per,)), pltpu.SemaphoreType.DMA((n_arr * per,)),
                        pltpu.SemaphoreType.DMA((n_arr,))],
        compiler_params=pltpu.CompilerParams(has_side_effects=True),
    )(*arrays)


HBM = pl.BlockSpec(memory_space=pltpu.HBM)
SEM = pl.BlockSpec(memory_space=pltpu.SEMAPHORE)
DATAFLOW = pltpu.SideEffectType.DATAFLOW_SIDE_EFFECTING


def _own_block_in_place(block, me):
    land = lax.empty((N_DEV,) + block.shape, block.dtype)
    return lax.dynamic_update_slice(land, block[None], (me,) + (0,) * block.ndim)


def _peer_copies(srcs, lands, send, recv, gathers):
    x, y, c = _mesh_place()
    me = 4 * x + 2 * y + c
    out = []
    for a, (src, land) in enumerate(zip(srcs, lands)):
        for k in range(1, N_DEV):
            dev, idx = _peer(k)
            sem = a * (N_DEV - 1) + k - 1
            mine = src if gathers[a] else src.at[idx]
            put = pltpu.make_async_remote_copy(mine, land.at[me], send.at[sem], recv.at[sem],
                                               device_id=dev, device_id_type=pl.DeviceIdType.MESH)
            got = pltpu.make_async_remote_copy(mine, land.at[idx], send.at[sem], recv.at[sem],
                                               device_id=dev, device_id_type=pl.DeviceIdType.MESH)
            out.append((put, got))
    return out


def _exchange_start(srcs, lands, gather, after, name):
    n = len(srcs)

    def body(*refs):
        srcs_, lands_, send, recv = refs[:n], refs[n:2 * n], refs[2 * n + 1], refs[2 * n + 2]
        for put, _ in _peer_copies(srcs_, lands_, send, recv, gather):
            put.start()
        refs[-1][...] = jnp.zeros_like(refs[-1])

    nsem = n * (N_DEV - 1)
    thru = [pltpu.HBM(a.shape, a.dtype) for a in list(srcs) + list(lands)]
    res = pl.pallas_call(
        body, name=name,
        out_shape=(pltpu.SemaphoreType.DMA((nsem,)), pltpu.SemaphoreType.DMA((nsem,)), *thru, jax.ShapeDtypeStruct((8, 128), F32)),
        in_specs=[HBM] * (2 * n) + [ANY],
        out_specs=(SEM, SEM, *([HBM] * (2 * n)), pl.BlockSpec(memory_space=pltpu.VMEM)),
        input_output_aliases={i: 2 + i for i in range(2 * n)},
        compiler_params=pltpu.CompilerParams(has_side_effects=DATAFLOW),
    )(*[pltpu.with_memory_space_constraint(a, pltpu.HBM) for a in list(srcs) + list(lands)], after)
    return res[0], res[1], list(res[2:2 + n]), list(res[2 + n:2 + 2 * n]), res[-1]


def _exchange_wait(send, recv, srcs, lands, gather, after, name):
    n = len(srcs)

    def body(*refs):
        srcs_, lands_, send_, recv_ = refs[:n], refs[n:2 * n], refs[2 * n], refs[2 * n + 1]
        for put, got in _peer_copies(srcs_, lands_, send_, recv_, gather):
            put.wait_send()
            got.wait_recv()

    thru = [pltpu.HBM(a.shape, a.dtype) for a in list(srcs) + list(lands)]
    res = pl.pallas_call(
        body, name=name, out_shape=tuple(thru),
        in_specs=[HBM] * (2 * n) + [SEM, SEM, ANY], out_specs=tuple([HBM] * (2 * n)),
        input_output_aliases={i: i for i in range(2 * n)},
        compiler_params=pltpu.CompilerParams(has_side_effects=DATAFLOW),
    )(*srcs, *lands, send, recv, after)
    return list(res[n:])


def _silu_rows(c):
    def body(c_ref, o_ref):
        v = c_ref[...]
        o_ref[...] = v * _sigmoid(v)

    return pl.pallas_call(body, name="cond", out_shape=jax.ShapeDtypeStruct(c.shape, F32))(c)


def _mod_slab(cond_all, w_ada, b_slab):
    def body(c_ref, w_ref, b_ref, o_ref):
        o_ref[...] = _dot(c_ref[...].astype(BF16), w_ref[0].astype(BF16), ((1,), (0,))) + b_ref[...]

    return pl.pallas_call(body, name="mod_slab",
                          out_shape=jax.ShapeDtypeStruct((cond_all.shape[0], w_ada.shape[2]), F32),
                          compiler_params=pltpu.CompilerParams(vmem_limit_bytes=VMEM_LIMIT))(cond_all, w_ada, b_slab)


def _ada_grad(cond_all, dmod_cols):
    def body(c_ref, d_ref, o_ref):
        o_ref[...] = _dot(c_ref[...].astype(BF16), d_ref[...].astype(BF16), ((0,), (0,)))

    return pl.pallas_call(body, name="ada_grad",
                          out_shape=jax.ShapeDtypeStruct((cond_all.shape[1], dmod_cols.shape[1]), F32),
                          compiler_params=pltpu.CompilerParams(vmem_limit_bytes=VMEM_LIMIT))(cond_all, dmod_cols)


def _adam_math(g, w, m, v):
    m2 = B1 * m + (1.0 - B1) * g
    v2 = B2 * v + (1.0 - B2) * (g * g)
    m_hat = m2 / (1.0 - B1 ** STEP)
    v_hat = v2 / (1.0 - B2 ** STEP)
    return -LR * (m_hat / (jnp.sqrt(v_hat) + ADAM_EPS) + WD * w), m2, v2


def _adamw(parts, w, m, v, name):
    n, rows, cols = parts.shape
    tr = _pick(rows, (128, 96, 64, 32, 16, 8))

    def body(p_ref, w_ref, m_ref, v_ref, g_ref, d_ref, m2_ref, v2_ref):
        g = p_ref[0].astype(F32)
        for s in range(1, n):
            g = g + p_ref[s].astype(F32)
        g_ref[0] = g
        d_ref[0], m2_ref[0], v2_ref[0] = _adam_math(g, w_ref[0], m_ref[0], v_ref[0])

    blk = pl.BlockSpec((1, tr, cols), lambda i: (0, i, 0))
    return pl.pallas_call(
        body, name=name, grid=(rows // tr,),
        in_specs=[pl.BlockSpec((n, tr, cols), lambda i: (0, i, 0)), blk, blk, blk],
        out_specs=[blk] * 4, out_shape=[jax.ShapeDtypeStruct((1, rows, cols), F32)] * 4,
        compiler_params=_cp("parallel"),
    )(*[pltpu.with_memory_space_constraint(a, pltpu.HBM) for a in (parts, w, m, v)])


ROW_PARAMS = (("g_norm1", D), ("g_cq", Q_LORA), ("g_ckv", KV_LORA), ("g_out_a", D_A), ("g_out_b", D_A), ("g_norm2", D),
              ("g_final", D))
LOSS_ROW = N_MOD + len(ROW_PARAMS)
PAY_ROWS = 16
NCOL = N_MOD * D // N_DEV


def _pack_small(dmods, rows, loss_cols):
    nb = dmods[0].shape[0]
    nrow = len(ROW_PARAMS)

    def body(*refs):
        dm, rw, loss_ref, pay_ref, blk_ref = refs[:N_MOD], refs[N_MOD:N_MOD + nrow], refs[N_MOD + nrow], refs[-2], refs[-1]
        pay_ref[...] = jnp.zeros_like(pay_ref)
        for k in range(N_MOD):
            tot = dm[k][0]
            for b in range(1, nb):
                tot = tot + dm[k][b]
            pay_ref[k:k + 1, :] = tot
        for i, (_, n) in enumerate(ROW_PARAMS):
            pay_ref[N_MOD + i:N_MOD + i + 1, :n] = rw[i][...]
        pay_ref[LOSS_ROW:LOSS_ROW + 1, :] = loss_ref[...]
        for j in range(N_DEV):
            done = 0
            while done < NCOL:
                seg, off = divmod(j * NCOL + done, D)
                ln = min(NCOL - done, D - off)
                for b in range(nb):
                    blk_ref[j, b:b + 1, done:done + ln] = dm[seg][b][:, off:off + ln]
                done += ln

    return pl.pallas_call(
        body, name="pack_small",
        out_shape=[jax.ShapeDtypeStruct((PAY_ROWS, D), F32), jax.ShapeDtypeStruct((N_DEV, nb, NCOL), F32)],
    )(*dmods, *rows, loss_cols)


def _small_update(pay, rel, ws, ms, vs):
    n_par = len(ws)

    def body(*refs):
        pay_ref, rel_ref = refs[:2]
        w_refs, m_refs, v_refs = (refs[2 + s * n_par:2 + (s + 1) * n_par] for s in range(3))
        outs, loss_ref = refs[2 + 3 * n_par:-1], refs[-1]
        tot, rtot = pay_ref[0], rel_ref[0]
        for s in range(1, N_DEV):
            tot, rtot = tot + pay_ref[s], rtot + rel_ref[s]

        def update(p, g, sl):
            outs[4 * p][:, sl] = g
            outs[4 * p + 1][:, sl], outs[4 * p + 2][:, sl], outs[4 * p + 3][:, sl] = _adam_math(
                g, w_refs[p][:, sl], m_refs[p][:, sl], v_refs[p][:, sl])

        for k in range(N_MOD):
            update(0, tot[k:k + 1, :], slice(k * D, (k + 1) * D))
        for i, (_, n) in enumerate(ROW_PARAMS):
            update(1 + i, tot[N_MOD + i:N_MOD + i + 1, :n], slice(0, n))
        update(n_par - 1, rtot, slice(0, H))
        loss_ref[...] = jnp.broadcast_to((0.5 / D) * jnp.sum(tot[LOSS_ROW:LOSS_ROW + 1, :]), loss_ref.shape)

    shapes = [jax.ShapeDtypeStruct(w.shape, F32) for w in ws for _ in range(4)]
    res = pl.pallas_call(
        body, name="small_update", out_shape=shapes + [jax.ShapeDtypeStruct((8, 128), F32)],
    )(pay, rel, *ws, *ms, *vs)
    return [tuple(res[4 * p:4 * p + 4]) for p in range(n_par)], res[-1]


def _cols_from_blocks(g):
    return jnp.transpose(g, (1, 0, 2)).reshape(g.shape[1], N_DEV * g.shape[2])


def _cols_to_blocks(w):
    r, c = w.shape
    return jnp.transpose(w.reshape(r, N_DEV, c // N_DEV), (1, 0, 2))


def _pad_w_in(w):
    z = jnp.zeros((w.shape[0], NOPE), w.dtype)
    return jnp.concatenate([w[:, :P_IN - ROPE], z, w[:, P_IN - ROPE:], z[:, :HP - NOPE - ROPE]], axis=1)


def _unpad_w_in(g):
    k0 = P_IN - ROPE + NOPE
    return jnp.concatenate([g[:, :P_IN - ROPE], g[:, k0:k0 + ROPE]], axis=1)


def _pad_w_uq(w):
    w3 = w.reshape(Q_LORA, H, NOPE + ROPE)
    return jnp.pad(w3, ((0, 0), (0, 0), (0, HP - NOPE - ROPE))).reshape(Q_LORA, H * HP)


def _unpad_w_uq(g):
    return g.reshape(Q_LORA, H, HP)[:, :, :NOPE + ROPE].reshape(Q_LORA, H * (NOPE + ROPE))


def _split_w_ukv(w):
    w4 = w.reshape(KV_LORA, H // 2, 2, HP)
    z = jnp.zeros((KV_LORA, H // 2, NOPE), w.dtype)
    kn, vv = w4[..., :NOPE], w4[..., NOPE:]
    w_k = jnp.stack([jnp.concatenate([kn[:, :, 0], z], -1), jnp.concatenate([kn[:, :, 1], z], -1)], axis=2)
    w_v = jnp.stack([jnp.concatenate([vv[:, :, 0], z], -1), jnp.concatenate([z, vv[:, :, 1]], -1)], axis=2)
    return w_k.reshape(KV_LORA, H * HP), w_v.reshape(KV_LORA, H * HP)


def _join_w_ukv(g_k, g_v):
    gk = g_k.reshape(KV_LORA, H // 2, 2, HP)
    gv = g_v.reshape(KV_LORA, H // 2, 2, HP)
    even = jnp.concatenate([gk[:, :, 0, :NOPE], gv[:, :, 0, :VDIM]], -1)
    odd = jnp.concatenate([gk[:, :, 1, :NOPE], gv[:, :, 1, VDIM:]], -1)
    return jnp.stack([even, odd], axis=2).reshape(KV_LORA, H * HP)


def _rope_tables():
    half = ROPE // 2
    inv = ROPE_THETA ** (-jnp.arange(half, dtype=F32) / half)
    ang = jnp.arange(S, dtype=F32)[:, None] * inv[None, :]
    cos, sin = jnp.cos(ang), jnp.sin(ang)
    ones, zeros = jnp.ones((S, NOPE), F32), jnp.zeros((S, NOPE), F32)
    tail1, tail0 = jnp.ones((S, HP - NOPE - ROPE), F32), jnp.zeros((S, HP - NOPE - ROPE), F32)
    zh = jnp.zeros((S, half), F32)
    c = jnp.concatenate([ones, cos, cos, tail1], axis=1)
    sm = jnp.concatenate([zeros, -sin, zh, tail0], axis=1)
    sp = jnp.concatenate([zeros, zh, sin, tail0], axis=1)
    return c, sm, sp


def _local_step(x, mod, target, g_norm1, w_in_p, g_cq, w_uq_p, g_ckv, w_k, w_v, rel_bias, g_out_a, g_out_b, w_out,
                g_norm2, w_ffn_in, w_ffn_out, g_final, late_weights=None, on_ffn_grads=None, on_last_grads=None):
    nb = x.shape[0] // S
    sh1, sc1, g1, sh2, sc2, g2 = (mod[:, n].reshape(nb, 1, D) for n in range(N_MOD))
    rc, rsm, rsp = _rope_tables()
    biasm = _band_bias(rel_bias)

    h1 = _pre1(x, g_norm1, sc1, sh1)
    proj = _mm_nn(h1, w_in_p, F32, "proj")
    q, k, v, cqn, ckvn = _mla_pre(proj, g_cq, g_ckv, w_uq_p, w_k, w_v, rc, rsm, rsp)
    out_b, lse_b = _mla_fwd(q, k, v)
    out_a, lse_a = _dil_fwd(proj, biasm)
    y = _post_attn(out_a, out_b, g_out_a, g_out_b)
    if late_weights is not None:
        w_out, w_ffn_in, w_ffn_out = late_weights(y)
    mix = _mm_nn(y, w_out, BF16, "mix")
    x2, h2 = _resid_norm2(x, mix, g1, g_norm2, sc2, sh2)
    ffn_g, ffn_u, act = _ffn_in(h2, w_ffn_in)
    f = _mm_nn(act, w_ffn_out, BF16, "ffn_out")
    dx3, df, loss_cols, dg_final, dg2 = _final(x2, f, g2, g_final, target)

    dg_, du_ = _d_act(df, w_ffn_out, ffn_g, ffn_u)
    gw_ffn_out = _mm_tn(act, [df], "gw_ffn_out")
    dh2 = _d_h2(dg_, du_, w_ffn_in)
    gw_ffn_in = (_mm_tn(dg_, [h2], "gw_ffn_g"), _mm_tn(du_, [h2], "gw_ffn_u"))
    dx2, dsh2, dsc2, dg_norm2, dg1, dmix = _norm_bwd(x2, dh2, dx3, g_norm2, sc2, gate=(mix, g1))
    dy = _mm_nt(dmix, w_out, BF16, "d_y")
    gw_out = _mm_tn(y, [dmix], "gw_out")
    if on_ffn_grads is not None:
        g_out_a = g_out_a + on_ffn_grads(gw_ffn_in, gw_ffn_out, gw_out)
    dout_a, dout_b, dg_out_a, dg_out_b = _post_attn_bwd(dy, out_a, out_b, g_out_a, g_out_b)
    dq_b, dk_b, dv_b = _mla_bwd(q, k, v, out_b, dout_b, lse_b)
    dq_a, dk_a, dv_a, dlogits = _dil_bwd(proj, biasm, out_a, dout_a, lse_a)
    g_rel = _rel_bias_grad(dlogits)
    dqr, dproj, dg_cq, dg_ckv = _mla_pre_bwd(proj, dq_b, dk_b, dv_b, (dq_a, dk_a, dv_a), g_cq, g_ckv, w_uq_p, w_k, w_v,
                                             rc, rsm, rsp)
    gw_uq = _mm_tn(cqn, [dqr], "gw_uq")
    gw_k, gw_v = _mm_tn(ckvn, [dk_b, dv_b], "gw_kv")
    gw_in = _mm_tn(h1, [dproj], "gw_in")
    if on_last_grads is not None:
        started = on_last_grads(dict(w_in=gw_in, w_uq=gw_uq, w_k=gw_k, w_v=gw_v))
    else:
        started = None
    dh1 = _mm_nt(dproj, w_in_p, BF16, "d_h1", after=started)
    grad_x, dsh1, dsc1, dg_norm1 = _norm_bwd(x, dh1, dx2, g_norm1, sc1)

    dmod = [dsh1, dsc1, dg1, dsh2, dsc2, dg2]
    small = dict(g_norm1=dg_norm1, g_cq=dg_cq, g_ckv=dg_ckv, rel_bias=g_rel, g_out_a=dg_out_a, g_out_b=dg_out_b,
                 g_norm2=dg_norm2, g_final=dg_final)
    big = dict(w_in=gw_in, w_uq=gw_uq, w_k=gw_k, w_v=gw_v, w_out=gw_out, w_ffn_in=gw_ffn_in, w_ffn_out=gw_ffn_out)
    return grad_x, dmod, loss_cols, small, big


def kernel(x, c, w_ada, b_ada, g_norm1, w_in, g_cq, w_uq, g_ckv, w_ukv, rel_bias, g_out_a, g_out_b, w_out, g_norm2, w_ffn_in, w_ffn_out, g_final, loss_target, m_w_ada, m_b_ada, m_g_norm1, m_w_in, m_g_cq, m_w_uq, m_g_ckv, m_w_ukv, m_rel_bias, m_g_out_a, m_g_out_b, m_w_out, m_g_norm2, m_w_ffn_in, m_w_ffn_out, m_g_final, v_w_ada, v_b_ada, v_g_norm1, v_w_in, v_g_cq, v_w_uq, v_g_ckv, v_w_ukv, v_rel_bias, v_g_out_a, v_g_out_b, v_w_out, v_g_norm2, v_w_ffn_in, v_w_ffn_out, v_g_final):
    nb = x.shape[0]
    t = nb * S
    xt, tt = x.reshape(t, D), loss_target.reshape(t, D)
    me = 4 * lax.axis_index("x") + 2 * lax.axis_index("y") + lax.axis_index("c")

    early = [w_in[0], w_uq[0], w_ukv[0]]
    gathered = _gather_two_level([_silu_rows(c)] + [s.astype(BF16) for s in early], "gather_weights")
    cond_all = gathered[0].reshape(N_DEV * nb, D)
    w_in_f, w_uq_f, w_ukv_f = (_cols_from_blocks(g) for g in gathered[1:4])
    w_k, w_v = _split_w_ukv(w_ukv_f)

    ncol = N_MOD * D // N_DEV
    b_slab = lax.dynamic_slice(b_ada, (0, me * ncol), (1, ncol))
    slab = _mod_slab(cond_all, w_ada, b_slab)
    (mod_rows,) = _exchange([slab.reshape(N_DEV, nb, ncol)], [False], "scatter_mod")
    mod = jnp.transpose(mod_rows, (1, 0, 2)).reshape(nb, N_MOD, D)

    late = [s.astype(BF16) for s in (w_out[0], jnp.swapaxes(w_ffn_in, 1, 2)[0], w_ffn_out[0])]
    late_send, late_recv, late_src, late_land, late_token = _exchange_start(
        late, [_own_block_in_place(s, me) for s in late], [True] * 3, mod_rows, "gather_late_start")
    g_norm1_t = g_norm1 + late_token[:1, :1]

    def late_weights(after):
        w_out_g, w_ffn_in_g, w_ffn_out_g = _exchange_wait(late_send, late_recv, late_src, late_land, [True] * 3, after,
                                                          "gather_late_wait")
        return w_out_g.reshape(D, D), w_ffn_in_g.reshape(2 * D_FF, D), w_ffn_out_g.reshape(D_FF, D)

    flight = {}

    def start_grads(key, src, name):
        land = [_own_block_in_place(lax.dynamic_index_in_dim(s, me, 0, keepdims=False), me) for s in src]
        send, recv, src, land, token = _exchange_start(src, land, [False] * len(src), src[0], name)
        flight[key] = (send, recv, src, land)
        return token[:1, :1]

    def on_ffn_grads(gw_ffn_in, gw_ffn_out, gw_out):
        gt = jnp.concatenate(gw_ffn_in, axis=0)
        return start_grads("ffn", [gt.reshape(N_DEV, 2 * D_FF // N_DEV, D), gw_ffn_out.reshape(N_DEV, D_FF // N_DEV, D),
                                   gw_out.reshape(N_DEV, D // N_DEV, D)], "exchange_ffn_start")

    def on_last_grads(gw):
        return start_grads("rest", [_cols_to_blocks(_unpad_w_in(gw["w_in"])), _cols_to_blocks(_unpad_w_uq(gw["w_uq"])),
                                    _cols_to_blocks(_join_w_ukv(gw["w_k"], gw["w_v"]))], "exchange_rest_start")

    grad_x, dmod, loss_cols, small, _ = _local_step(
        xt, mod, tt, g_norm1_t, _pad_w_in(w_in_f), g_cq, _pad_w_uq(w_uq_f), g_ckv, w_k, w_v, rel_bias, g_out_a, g_out_b,
        None, g_norm2, None, None, g_final.reshape(1, D), late_weights=late_weights, on_ffn_grads=on_ffn_grads,
        on_last_grads=on_last_grads)

    upd = {}

    def land_and_update(key, names, after, name):
        got = _exchange_wait(*flight[key], [False] * len(names), after, name)
        for n, p in zip(names, got):
            w, m, v = big[n]
            upd[n] = _adamw(p, w, m, v, "adamw_" + n)

    def flip(a):
        return jnp.swapaxes(a, 1, 2)

    big = dict(w_in=(w_in, m_w_in, v_w_in), w_uq=(w_uq, m_w_uq, v_w_uq), w_ukv=(w_ukv, m_w_ukv, v_w_ukv),
               w_out=(w_out, m_w_out, v_w_out), w_ffn_in=(flip(w_ffn_in), flip(m_w_ffn_in), flip(v_w_ffn_in)),
               w_ffn_out=(w_ffn_out, m_w_ffn_out, v_w_ffn_out))
    land_and_update("ffn", ["w_ffn_in", "w_ffn_out", "w_out"], grad_x, "exchange_ffn_wait")
    land_and_update("rest", ["w_in", "w_uq", "w_ukv"], upd["w_out"][0], "exchange_rest_wait")
    upd["w_ffn_in"] = tuple(flip(a) for a in upd["w_ffn_in"])

    mine, dmod_blocks = _pack_small(dmod, [small[n] for n, _ in ROW_PARAMS], loss_cols)
    dmod_cols, pay, rel = _exchange([dmod_blocks, mine, small["rel_bias"]], [False, True, True], "exchange_small",
                                    after=upd["w_ukv"][0])
    g_ada = _ada_grad(cond_all, dmod_cols.reshape(N_DEV * nb, ncol))
    upd["w_ada"] = _adamw(g_ada[None], w_ada, m_w_ada, v_w_ada, "adamw_w_ada")
    row = lambda a: a.reshape(1, D)
    small_names = ["b_ada"] + [n for n, _ in ROW_PARAMS] + ["rel_bias"]
    small_w = [b_ada, g_norm1, g_cq, g_ckv, g_out_a, g_out_b, g_norm2, row(g_final), rel_bias]
    small_m = [m_b_ada, m_g_norm1, m_g_cq, m_g_ckv, m_g_out_a, m_g_out_b, m_g_norm2, row(m_g_final), m_rel_bias]
    small_v = [v_b_ada, v_g_norm1, v_g_cq, v_g_ckv, v_g_out_a, v_g_out_b, v_g_norm2, row(v_g_final), v_rel_bias]
    small_upd, loss8 = _small_update(pay, rel, small_w, small_m, small_v)
    upd.update(zip(small_names, small_upd))

    order = ["w_ada", "b_ada", "g_norm1", "w_in", "g_cq", "w_uq", "g_ckv", "w_ukv", "rel_bias", "g_out_a", "g_out_b",
             "w_out", "g_norm2", "w_ffn_in", "w_ffn_out", "g_final"]
    like = dict(g_final=g_final)
    outs = [loss8[0, 0], grad_x.reshape(x.shape)]
    for part in range(4):
        for n in order:
            val = upd[n][part]
            outs.append(val.reshape(like[n].shape) if n in like else val)
    return tuple(outs)
```

```python
import functools

import numpy as np
import jax
import jax.numpy as jnp
from jax import lax
from jax.experimental import pallas as pl
from jax.experimental.pallas import tpu as pltpu

F32, BF16 = jnp.float32, jnp.bfloat16

N_DEV = 8
D = 1024
S = 2048
H = 8
E_A = 64
D_A = H * E_A
Q_LORA, KV_LORA = 384, 256
NOPE, ROPE, VDIM = 64, 32, 64
HP = 128
P_IN = 3 * D_A + Q_LORA + KV_LORA + ROPE
P_PAD = 3 * D_A + Q_LORA + KV_LORA + HP
TAIL0 = 3 * D_A
TAIL = P_PAD - TAIL0
D_FF = 2816
N_MOD = 6
EPS = 1e-6
NEG = -1e30
BLK = 128
DILATIONS = (1, 4, 16)
N_BUCKETS, MAX_DISTANCE = 32, 2048
ROPE_THETA = 10000.0
SCALE_A = E_A ** -0.5
SCALE_B = (NOPE + ROPE) ** -0.5
B1, B2, LR, ADAM_EPS, WD, STEP = 0.9, 0.999, 0.001, 1e-8, 0.01, 10
VMEM_LIMIT = 56 * 1024 * 1024


def _cp(*sem):
    return pltpu.CompilerParams(dimension_semantics=sem, vmem_limit_bytes=VMEM_LIMIT)


def _pick(n, prefs):
    for p in prefs:
        if n % p == 0:
            return p
    raise ValueError(f"no tile of {prefs} divides {n}")


OPERAND_BYTES = 6 * 1024 * 1024


def _pick_rows(m, k):
    return _pick(m, [p for p in (1024, 512, 256, 128, 16) if p * k * 2 <= OPERAND_BYTES])


MATMUL_BYTES = 40 * 1024 * 1024


def _stream_rows(m, fixed, per_row):
    return _pick(m, [p for p in (4096, 2048, 1024, 512, 256, 128, 16) if fixed + p * per_row <= MATMUL_BYTES])


def _dot(a, b, dims):
    return lax.dot_general(a, b, (dims, ((), ())), preferred_element_type=F32)


def _mm_nn(a, b, out_dtype, name):
    m, k = a.shape
    n = b.shape[1]
    tn = _pick(n, (512, 384, 256, 128))
    tm = _stream_rows(m, 4 * k * tn, 4 * k + (2 * jnp.dtype(out_dtype).itemsize + 4) * tn)

    def body(a_ref, b_ref, o_ref):
        o_ref[...] = _dot(a_ref[...], b_ref[...], ((1,), (0,))).astype(o_ref.dtype)

    return pl.pallas_call(
        body, name=name, grid=(m // tm, n // tn),
        in_specs=[pl.BlockSpec((tm, k), lambda i, j: (i, 0)), pl.BlockSpec((k, tn), lambda i, j: (0, j))],
        out_specs=pl.BlockSpec((tm, tn), lambda i, j: (i, j)),
        out_shape=jax.ShapeDtypeStruct((m, n), out_dtype),
        compiler_params=_cp("parallel", "parallel"),
    )(a, b)


def _mm_nt(a, b, out_dtype, name, after=None):
    m, k = a.shape
    n = b.shape[0]
    tn = _pick(n, (512, 384, 256, 128))
    tm = _stream_rows(m, 4 * k * tn, 4 * k + (2 * jnp.dtype(out_dtype).itemsize + 4) * tn)

    def body(a_ref, b_ref, *rest):
        o_ref = rest[-1]
        o_ref[...] = _dot(a_ref[...], b_ref[...], ((1,), (1,))).astype(o_ref.dtype)

    extra = [] if after is None else [after]
    return pl.pallas_call(
        body, name=name, grid=(m // tm, n // tn),
        in_specs=[pl.BlockSpec((tm, k), lambda i, j: (i, 0)), pl.BlockSpec((tn, k), lambda i, j: (j, 0))] + [ANY] * len(extra),
        out_specs=pl.BlockSpec((tm, tn), lambda i, j: (i, j)),
        out_shape=jax.ShapeDtypeStruct((m, n), out_dtype),
        compiler_params=_cp("parallel", "parallel"),
    )(a, b, *extra)


def _mm_tn(a, bs, name):
    t, m = a.shape
    n = bs[0].shape[1]
    nb_ = len(bs)
    tc = _pick(t, (512, 16))
    tn = _pick(n, (512, 384, 256, 128))
    tm = _pick(m, [p for p in (1024, 512, 384, 256, 128) if (3 * p + 2 * nb_ * tn) * t * 2 <= VMEM_LIMIT - 2 * OPERAND_BYTES])
    if tm <= 256 and nb_ * n * t * 2 <= 2 * OPERAND_BYTES:
        tn = n

    def body(*refs):
        a_ref, b_refs, o_refs, at_ref = refs[0], refs[1:1 + nb_], refs[1 + nb_:1 + 2 * nb_], refs[-1]

        @pl.when(pl.program_id(1) == 0)
        def _():
            def chunk(c, _):
                rows = pl.ds(pl.multiple_of(c * tc, tc), tc)
                at_ref[:, rows] = a_ref[rows, :].T
                return 0

            lax.fori_loop(0, t // tc, chunk, 0)

        for b_ref, o_ref in zip(b_refs, o_refs):
            o_ref[...] = _dot(at_ref[...], b_ref[...], ((1,), (0,))).astype(BF16)

    res = pl.pallas_call(
        body, name=name, grid=(m // tm, n // tn),
        in_specs=[pl.BlockSpec((t, tm), lambda i, j: (0, i))] + [pl.BlockSpec((t, tn), lambda i, j: (0, j))] * nb_,
        out_specs=[pl.BlockSpec((tm, tn), lambda i, j: (i, j))] * nb_,
        out_shape=[jax.ShapeDtypeStruct((m, n), BF16)] * nb_,
        scratch_shapes=[pltpu.VMEM((tm, t), BF16)],
        compiler_params=_cp("parallel", "arbitrary"),
    )(a, *bs)
    return res[0] if nb_ == 1 else res


def _mm_tn_pair(a0, a1, b, name):
    t, m = a0.shape
    n = b.shape[1]
    tc, tm = _pick(t, (512, 16)), _pick(m, (256, 128))
    nblk = m // tm

    def body(a0_ref, a1_ref, b_ref, o_ref, at_ref):
        i = pl.program_id(0)

        def transpose_from(src):
            def chunk(c, _):
                rows = pl.ds(pl.multiple_of(c * tc, tc), tc)
                at_ref[:, rows] = src[rows, :].T
                return 0

            lax.fori_loop(0, t // tc, chunk, 0)

        @pl.when(i < nblk)
        def _():
            transpose_from(a0_ref)

        @pl.when(i >= nblk)
        def _():
            transpose_from(a1_ref)

        o_ref[...] = _dot(at_ref[...], b_ref[...], ((1,), (0,))).astype(BF16)

    return pl.pallas_call(
        body, name=name, grid=(2 * nblk,),
        in_specs=[pl.BlockSpec((t, tm), lambda i: (0, jnp.minimum(i, nblk - 1))),
                  pl.BlockSpec((t, tm), lambda i: (0, jnp.maximum(i - nblk, 0))), pl.BlockSpec((t, n), lambda i: (0, 0))],
        out_specs=pl.BlockSpec((tm, n), lambda i: (i, 0)),
        out_shape=jax.ShapeDtypeStruct((2 * m, n), BF16),
        scratch_shapes=[pltpu.VMEM((tm, t), BF16)],
        compiler_params=_cp("arbitrary"),
    )(a0, a1, b)


EPI = 256


def _silu_parts(g):
    sg = 0.5 * jnp.tanh(0.5 * g) + 0.5
    return sg, g * sg


def _ffn_in(h2, wt):
    t, k = h2.shape
    tn = _pick(D_FF, (256, 128))
    tm = _stream_rows(t, 8 * k * tn, 4 * k + (3 * 2 * 2 + 2 * 4) * tn)
    nj = D_FF // tn

    def body(h_ref, wg_ref, wu_ref, g_ref, u_ref, a_ref):
        hv = h_ref[...]
        g_all = _dot(hv, wg_ref[...], ((1,), (1,)))
        u_all = _dot(hv, wu_ref[...], ((1,), (1,)))
        for r in range(tm // EPI):
            rows = slice(r * EPI, (r + 1) * EPI)
            g, u = g_all[rows], u_all[rows]
            g_ref[rows, :] = g.astype(BF16)
            u_ref[rows, :] = u.astype(BF16)
            a_ref[rows, :] = (_silu_parts(g)[1] * u).astype(BF16)

    blk = pl.BlockSpec((tm, tn), lambda i, j: (i, j))
    return pl.pallas_call(
        body, name="ffn_in", grid=(t // tm, nj),
        in_specs=[pl.BlockSpec((tm, k), lambda i, j: (i, 0)), pl.BlockSpec((tn, k), lambda i, j: (j, 0)),
                  pl.BlockSpec((tn, k), lambda i, j: (j + nj, 0))],
        out_specs=[blk] * 3, out_shape=[jax.ShapeDtypeStruct((t, D_FF), BF16)] * 3,
        compiler_params=_cp("parallel", "parallel"),
    )(h2, wt, wt)


def _d_act(df, w, g, u):
    t, k = df.shape
    tn = _pick(D_FF, (256, 128))
    tm = _stream_rows(t, 4 * k * tn, 4 * k + (4 * 2 * 2 + 4) * tn)

    def body(df_ref, w_ref, g_ref, u_ref, dg_ref, du_ref):
        da_all = _dot(df_ref[...], w_ref[...], ((1,), (1,)))
        for r in range(tm // EPI):
            rows = slice(r * EPI, (r + 1) * EPI)
            da = da_all[rows]
            gv = g_ref[rows, :].astype(F32)
            sg, silu = _silu_parts(gv)
            dg_ref[rows, :] = ((da * u_ref[rows, :].astype(F32)) * (sg + silu * (1.0 - sg))).astype(BF16)
            du_ref[rows, :] = (da * silu).astype(BF16)

    blk = pl.BlockSpec((tm, tn), lambda i, j: (i, j))
    return pl.pallas_call(
        body, name="d_act", grid=(t // tm, D_FF // tn),
        in_specs=[pl.BlockSpec((tm, k), lambda i, j: (i, 0)), pl.BlockSpec((tn, k), lambda i, j: (j, 0)), blk, blk],
        out_specs=[blk] * 2, out_shape=[jax.ShapeDtypeStruct((t, D_FF), BF16)] * 2,
        compiler_params=_cp("parallel", "parallel"),
    )(df, w, g, u)


def _d_h2(dg, du, wt):
    t = dg.shape[0]
    n = wt.shape[1]
    tm, tn = _pick_rows(t, D_FF), _pick(n, (512, 256, 128))

    def body(dg_ref, du_ref, wg_ref, wu_ref, o_ref):
        o_ref[...] = (_dot(dg_ref[...], wg_ref[...], ((1,), (0,)))
                      + _dot(du_ref[...], wu_ref[...], ((1,), (0,)))).astype(BF16)

    return pl.pallas_call(
        body, name="d_h2", grid=(t // tm, n // tn),
        in_specs=[pl.BlockSpec((tm, D_FF), lambda i, j: (i, 0)), pl.BlockSpec((tm, D_FF), lambda i, j: (i, 0)),
                  pl.BlockSpec((D_FF, tn), lambda i, j: (0, j)), pl.BlockSpec((D_FF, tn), lambda i, j: (1, j))],
        out_specs=pl.BlockSpec((tm, tn), lambda i, j: (i, j)),
        out_shape=jax.ShapeDtypeStruct((t, n), BF16),
        compiler_params=_cp("parallel", "parallel"),
    )(dg, du, wt, wt)


TM = 256


def _row(w):
    return pl.BlockSpec((TM, w), lambda i: (i, 0))


def _row_at(w, col):
    return pl.BlockSpec((TM, w), lambda i: (i, col))


def _vec(w):
    return pl.BlockSpec((1, w), lambda i: (0, 0))


def _per_ex(w):
    return pl.BlockSpec((1, 1, w), lambda i: (i // (S // TM), 0, 0))


def _pos(w):
    return pl.BlockSpec((TM, w), lambda i: (i % (S // TM), 0))


def _full(shape):
    return pl.BlockSpec(shape, lambda i: (0,) * len(shape))


def _rms(x):
    return lax.rsqrt(jnp.mean(x * x, axis=-1, keepdims=True) + EPS)


def _rms_bwd(n, r, dn):
    return r * (dn - n * jnp.mean(dn * n, axis=-1, keepdims=True))


def _colsum(v):
    return jnp.sum(v, axis=0, keepdims=True)


def _acc_first(i, ref, val, every=None):
    first = (i == 0) if every is None else (i % every == 0)

    @pl.when(first)
    def _():
        ref[...] = jnp.zeros_like(ref)

    ref[...] += val.reshape(ref.shape)


def _pre1(x, g, sc, sh):
    t = x.shape[0]

    def body(x_ref, g_ref, sc_ref, sh_ref, h_ref):
        xv = x_ref[...]
        n = xv * _rms(xv)
        h_ref[...] = ((n * g_ref[...]) * (1.0 + sc_ref[0]) + sh_ref[0]).astype(BF16)

    return pl.pallas_call(
        body, name="pre1", grid=(t // TM,),
        in_specs=[_row(D), _vec(D), _per_ex(D), _per_ex(D)],
        out_specs=_row(D), out_shape=jax.ShapeDtypeStruct((t, D), BF16),
        compiler_params=_cp("parallel"),
    )(x, g, sc, sh)


def _rope_fwd(v, c, sm, sp):
    return v * c + pltpu.roll(v, HP - ROPE // 2, 1) * sm + pltpu.roll(v, ROPE // 2, 1) * sp


def _rope_bwd(dv, c, sm, sp):
    return dv * c + pltpu.roll(dv * sm, ROPE // 2, 1) + pltpu.roll(dv * sp, HP - ROPE // 2, 1)


def _mla_pre(proj, g_cq, g_ckv, w_uq, w_k, w_v, rc, rsm, rsp):
    t = proj.shape[0]

    def body(tail_ref, gq_ref, gkv_ref, wuq_ref, wk_ref, wv_ref, c_ref, sm_ref, sp_ref,
             q_ref, k_ref, v_ref, cqn_ref, ckvn_ref):
        tail = tail_ref[...]
        cq, ckv, kr = tail[:, :Q_LORA], tail[:, Q_LORA:Q_LORA + KV_LORA], tail[:, Q_LORA + KV_LORA:]
        cqn = (cq * _rms(cq) * gq_ref[...]).astype(BF16)
        ckvn = (ckv * _rms(ckv) * gkv_ref[...]).astype(BF16)
        cqn_ref[...] = cqn
        ckvn_ref[...] = ckvn
        c, sm, sp = c_ref[...], sm_ref[...], sp_ref[...]
        q = _dot(cqn, wuq_ref[...], ((1,), (0,)))
        kn = _dot(ckvn, wk_ref[...], ((1,), (0,)))
        v_ref[...] = _dot(ckvn, wv_ref[...], ((1,), (0,))).astype(BF16)
        krr = _rope_fwd(kr, c, sm, sp)
        for h in range(H):
            sl = slice(h * HP, (h + 1) * HP)
            q_ref[:, sl] = _rope_fwd(q[:, sl], c, sm, sp).astype(BF16)
            k_ref[:, sl] = (kn[:, sl] + krr).astype(BF16)

    wide = H * HP
    return pl.pallas_call(
        body, name="mla_pre", grid=(t // TM,),
        in_specs=[_row_at(TAIL, TAIL0 // TAIL), _vec(Q_LORA), _vec(KV_LORA), _full((Q_LORA, wide)),
                  _full((KV_LORA, wide)), _full((KV_LORA, wide)), _pos(HP), _pos(HP), _pos(HP)],
        out_specs=[_row(wide), _row(wide), _row(wide), _row(Q_LORA), _row(KV_LORA)],
        out_shape=[jax.ShapeDtypeStruct((t, wide), BF16)] * 3
        + [jax.ShapeDtypeStruct((t, Q_LORA), BF16), jax.ShapeDtypeStruct((t, KV_LORA), BF16)],
        compiler_params=_cp("parallel"),
    )(proj, g_cq, g_ckv, w_uq, w_k, w_v, rc, rsm, rsp)


def _mla_pre_bwd(proj, dq_, dk_, dv_, dqkv_a, g_cq, g_ckv, w_uq, w_k, w_v, rc, rsm, rsp):
    t = proj.shape[0]
    wide = H * HP

    def body(tail_ref, dq_ref, dk_ref, dv_ref, dqa_ref, dka_ref, dva_ref, gq_ref, gkv_ref, wuq_ref, wk_ref, wv_ref,
             c_ref, sm_ref, sp_ref, dqo_ref, dproj_ref, dgq_ref, dgkv_ref):
        i = pl.program_id(0)
        for n, src in enumerate((dqa_ref, dka_ref, dva_ref)):
            dproj_ref[:, n * D_A:(n + 1) * D_A] = src[...]
        dtail_ref = dproj_ref.at[:, TAIL0:]
        tail = tail_ref[...]
        cq, ckv = tail[:, :Q_LORA], tail[:, Q_LORA:Q_LORA + KV_LORA]
        c, sm, sp = c_ref[...], sm_ref[...], sp_ref[...]
        dkr = jnp.zeros((TM, HP), F32)
        for h in range(H):
            sl = slice(h * HP, (h + 1) * HP)
            dqo_ref[:, sl] = _rope_bwd(dq_ref[:, sl].astype(F32), c, sm, sp).astype(BF16)
            dkr = dkr + dk_ref[:, sl].astype(F32)
        lane = lax.broadcasted_iota(jnp.int32, (TM, HP), 1)
        dkr = jnp.where((lane >= NOPE) & (lane < NOPE + ROPE), _rope_bwd(dkr, c, sm, sp), 0.0)
        dkb = dk_ref[...]
        dvb = dv_ref[...]
        dcqn = _dot(dqo_ref[...], wuq_ref[...], ((1,), (1,)))
        dckvn = _dot(dkb, wk_ref[...], ((1,), (1,))) + _dot(dvb, wv_ref[...], ((1,), (1,)))
        rq, rkv = _rms(cq), _rms(ckv)
        nq, nkv = cq * rq, ckv * rkv
        _acc_first(i, dgq_ref, _colsum(dcqn * nq))
        _acc_first(i, dgkv_ref, _colsum(dckvn * nkv))
        dtail_ref[:, :Q_LORA] = _rms_bwd(nq, rq, dcqn * gq_ref[...]).astype(BF16)
        dtail_ref[:, Q_LORA:Q_LORA + KV_LORA] = _rms_bwd(nkv, rkv, dckvn * gkv_ref[...]).astype(BF16)
        dtail_ref[:, Q_LORA + KV_LORA:] = dkr.astype(BF16)

    return pl.pallas_call(
        body, name="mla_pre_bwd", grid=(t // TM,),
        in_specs=[_row_at(TAIL, TAIL0 // TAIL), _row(wide), _row(wide), _row(wide), _row(D_A), _row(D_A), _row(D_A),
                  _vec(Q_LORA), _vec(KV_LORA), _full((Q_LORA, wide)), _full((KV_LORA, wide)), _full((KV_LORA, wide)),
                  _pos(HP), _pos(HP), _pos(HP)],
        out_specs=[_row(wide), _row(P_PAD), _vec(Q_LORA), _vec(KV_LORA)],
        out_shape=[jax.ShapeDtypeStruct((t, wide), BF16), jax.ShapeDtypeStruct((t, P_PAD), BF16),
                   jax.ShapeDtypeStruct((1, Q_LORA), F32), jax.ShapeDtypeStruct((1, KV_LORA), F32)],
        compiler_params=_cp("arbitrary"),
    )(proj, dq_, dk_, dv_, *dqkv_a, g_cq, g_ckv, w_uq, w_k, w_v, rc, rsm, rsp)


def _post_attn(out_a, out_b, g_a, g_b):
    t = out_a.shape[0]

    def body(a_ref, b_ref, ga_ref, gb_ref, y_ref):
        a, b = a_ref[...], b_ref[...]
        y_ref[:, :D_A] = (a * _rms(a) * ga_ref[...]).astype(BF16)
        y_ref[:, D_A:] = (b * _rms(b) * gb_ref[...]).astype(BF16)

    return pl.pallas_call(
        body, name="post_attn", grid=(t // TM,),
        in_specs=[_row(D_A), _row(D_A), _vec(D_A), _vec(D_A)],
        out_specs=_row(D), out_shape=jax.ShapeDtypeStruct((t, D), BF16),
        compiler_params=_cp("parallel"),
    )(out_a, out_b, g_a, g_b)


def _post_attn_bwd(dy, out_a, out_b, g_a, g_b):
    t = dy.shape[0]

    def body(dy_ref, a_ref, b_ref, ga_ref, gb_ref, da_ref, db_ref, dga_ref, dgb_ref):
        i = pl.program_id(0)
        dy_ = dy_ref[...].astype(F32)
        for src, g_ref, dst, dg_ref, sl in ((a_ref, ga_ref, da_ref, dga_ref, slice(0, D_A)),
                                            (b_ref, gb_ref, db_ref, dgb_ref, slice(D_A, D))):
            v = src[...]
            r = _rms(v)
            n = v * r
            dyv = dy_[:, sl]
            _acc_first(i, dg_ref, _colsum(dyv * n))
            dst[...] = _rms_bwd(n, r, dyv * g_ref[...])

    return pl.pallas_call(
        body, name="post_attn_bwd", grid=(t // TM,),
        in_specs=[_row(D), _row(D_A), _row(D_A), _vec(D_A), _vec(D_A)],
        out_specs=[_row(D_A), _row(D_A), _vec(D_A), _vec(D_A)],
        out_shape=[jax.ShapeDtypeStruct((t, D_A), F32)] * 2 + [jax.ShapeDtypeStruct((1, D_A), F32)] * 2,
        compiler_params=_cp("arbitrary"),
    )(dy, out_a, out_b, g_a, g_b)


def _resid_norm2(x, mix, g1, g, sc, sh):
    t = x.shape[0]

    def body(x_ref, mix_ref, g1_ref, g_ref, sc_ref, sh_ref, x2_ref, h_ref):
        x2 = x_ref[...] + g1_ref[0] * mix_ref[...]
        x2_ref[...] = x2
        n = x2 * _rms(x2)
        h_ref[...] = ((n * g_ref[...]) * (1.0 + sc_ref[0]) + sh_ref[0]).astype(BF16)

    return pl.pallas_call(
        body, name="resid_norm2", grid=(t // TM,),
        in_specs=[_row(D), _row(D), _per_ex(D), _vec(D), _per_ex(D), _per_ex(D)],
        out_specs=[_row(D), _row(D)],
        out_shape=[jax.ShapeDtypeStruct((t, D), F32), jax.ShapeDtypeStruct((t, D), BF16)],
        compiler_params=_cp("parallel"),
    )(x, mix, g1, g, sc, sh)


def _sigmoid(v):
    return 1.0 / (1.0 + jnp.exp(-v))


def _final(x2, f, g2, g_fin, target):
    t = x2.shape[0]
    nb = t // S
    tpb = S // TM

    def body(x2_ref, f_ref, g2_ref, g_ref, t_ref, dx3_ref, df_ref, loss_ref, dgf_ref, dg2_ref):
        i = pl.program_id(0)
        fv = f_ref[...].astype(F32)
        x3 = x2_ref[...] + g2_ref[0] * fv
        r = _rms(x3)
        n = x3 * r
        err = n * g_ref[...] - t_ref[...]
        _acc_first(i, loss_ref, _colsum(err * err))
        dy = err * (1.0 / D)
        _acc_first(i, dgf_ref, _colsum(dy * n))
        dx3 = _rms_bwd(n, r, dy * g_ref[...])
        dx3_ref[...] = dx3
        _acc_first(i, dg2_ref, _colsum(dx3 * fv), every=tpb)
        df_ref[...] = (dx3 * g2_ref[0]).astype(BF16)

    return pl.pallas_call(
        body, name="final", grid=(t // TM,),
        in_specs=[_row(D), _row(D), _per_ex(D), _vec(D), _row(D)],
        out_specs=[_row(D), _row(D), _vec(D), _vec(D), _per_ex(D)],
        out_shape=[jax.ShapeDtypeStruct((t, D), F32), jax.ShapeDtypeStruct((t, D), BF16),
                   jax.ShapeDtypeStruct((1, D), F32), jax.ShapeDtypeStruct((1, D), F32),
                   jax.ShapeDtypeStruct((nb, 1, D), F32)],
        compiler_params=_cp("arbitrary"),
    )(x2, f, g2, g_fin, target)


def _norm_bwd(xin, dh, dres, g, sc, gate=None):
    t = xin.shape[0]
    nb = t // S
    tpb = S // TM
    gated = gate is not None

    def body(*refs):
        if gated:
            x_ref, dh_ref, dres_ref, g_ref, sc_ref, mix_ref, g1_ref, dx_ref, dsh_ref, dsc_ref, dg_ref, dg1_ref, dmix_ref = refs
        else:
            x_ref, dh_ref, dres_ref, g_ref, sc_ref, dx_ref, dsh_ref, dsc_ref, dg_ref = refs
        i = pl.program_id(0)
        xv, dhv = x_ref[...], dh_ref[...].astype(F32)
        r = _rms(xv)
        n = xv * r
        gv = g_ref[...]
        _acc_first(i, dsh_ref, _colsum(dhv), every=tpb)
        _acc_first(i, dsc_ref, _colsum(dhv * (n * gv)), every=tpb)
        dng = dhv * (1.0 + sc_ref[0])
        _acc_first(i, dg_ref, _colsum(dng * n))
        dx = dres_ref[...] + _rms_bwd(n, r, dng * gv)
        dx_ref[...] = dx
        if gated:
            _acc_first(i, dg1_ref, _colsum(dx * mix_ref[...].astype(F32)), every=tpb)
            dmix_ref[...] = (dx * g1_ref[0]).astype(BF16)

    in_specs = [_row(D), _row(D), _row(D), _vec(D), _per_ex(D)]
    out_specs = [_row(D), _per_ex(D), _per_ex(D), _vec(D)]
    out_shape = [jax.ShapeDtypeStruct((t, D), F32), jax.ShapeDtypeStruct((nb, 1, D), F32),
                 jax.ShapeDtypeStruct((nb, 1, D), F32), jax.ShapeDtypeStruct((1, D), F32)]
    args = [xin, dh, dres, g, sc]
    if gated:
        in_specs += [_row(D), _per_ex(D)]
        out_specs += [_per_ex(D), _row(D)]
        out_shape += [jax.ShapeDtypeStruct((nb, 1, D), F32), jax.ShapeDtypeStruct((t, D), BF16)]
        args += list(gate)
    return pl.pallas_call(
        body, name="norm2_bwd" if gated else "norm1_bwd", grid=(t // TM,),
        in_specs=in_specs, out_specs=out_specs, out_shape=out_shape,
        compiler_params=_cp("arbitrary"),
    )(*args)


TQ = 256
TB = 512


def _mla_fwd(q, k, v):
    t = q.shape[0]
    nb = t // S

    def body(q_ref, k_ref, v_ref, o_ref, lse_ref):
        causal = lax.broadcasted_iota(jnp.int32, (TB, TB), 0) >= lax.broadcasted_iota(jnp.int32, (TB, TB), 1)
        heads = [slice(h * HP, (h + 1) * HP) for h in range(2)]
        for i in range(S // TB):
            ri, past = slice(i * TB, (i + 1) * TB), slice(0, i * TB)
            qhs = [q_ref[ri, sl] for sl in heads]
            sd = [jnp.where(causal, _dot(qh, k_ref[ri, sl], ((1,), (1,))) * SCALE_B, NEG) for qh, sl in zip(qhs, heads)]
            ms = [jnp.max(s, axis=-1, keepdims=True) for s in sd]
            if i:
                so = [_dot(qh, k_ref[past, sl], ((1,), (1,))) * SCALE_B for qh, sl in zip(qhs, heads)]
                ms = [jnp.maximum(m, jnp.max(s, axis=-1, keepdims=True)) for m, s in zip(ms, so)]
            pd = [jnp.exp(s - m) for s, m in zip(sd, ms)]
            ls = [jnp.sum(p, axis=-1, keepdims=True) for p in pd]
            acc = [_dot(p.astype(BF16), v_ref[ri, sl], ((1,), (0,))) for p, sl in zip(pd, heads)]
            if i:
                po = [jnp.exp(s - m) for s, m in zip(so, ms)]
                ls = [l + jnp.sum(p, axis=-1, keepdims=True) for l, p in zip(ls, po)]
                acc = [a + _dot(p.astype(BF16), v_ref[past, sl], ((1,), (0,))) for a, p, sl in zip(acc, po, heads)]
            o_ref[ri, :] = acc[0] / ls[0] + acc[1] / ls[1]
            for sl, m, l in zip(heads, ms, ls):
                lse_ref[ri, sl] = jnp.broadcast_to(m + jnp.log(l), (TB, HP))

    wide2 = pl.BlockSpec((S, 2 * HP), lambda b, p: (b, p))
    return pl.pallas_call(
        body, name="mla_fwd", grid=(nb, H // 2),
        in_specs=[wide2, wide2, wide2],
        out_specs=[pl.BlockSpec((S, HP), lambda b, p: (b, p)), wide2],
        out_shape=[jax.ShapeDtypeStruct((t, H * VDIM), F32), jax.ShapeDtypeStruct((t, H * HP), F32)],
        compiler_params=_cp("parallel", "parallel"),
    )(q, k, v)


def _mla_bwd(q, k, v, o, do, lse):
    t = q.shape[0]
    nb = t // S

    def body(q_ref, k_ref, v_ref, o_ref, do_ref, lse_ref, dq_out, dk_out, dv_out, dq_ref, dk_ref, dv_ref):
        lane = lax.broadcasted_iota(jnp.int32, (TB, HP), 1)
        causal = lax.broadcasted_iota(jnp.int32, (TB, TB), 0) >= lax.broadcasted_iota(jnp.int32, (TB, TB), 1)
        heads = [slice(h * HP, (h + 1) * HP) for h in range(2)]
        nblk = S // TB
        for i in reversed(range(nblk)):
            ri, past = slice(i * TB, (i + 1) * TB), slice(0, i * TB)
            dov = do_ref[ri, :]
            prod = dov * o_ref[ri, :]
            dob = dov.astype(BF16)
            deltas = [jnp.sum(jnp.where((lane < VDIM) if h == 0 else (lane >= VDIM), prod, 0.0), axis=-1, keepdims=True)
                      for h in range(2)]
            qhs = [q_ref[ri, sl] for sl in heads]
            lses = [lse_ref[ri, sl][:, :1] for sl in heads]
            for rows, diagonal in ((ri, True), (past, False)):
                if rows.stop == rows.start:
                    continue
                ps = [jnp.exp(_dot(qh, k_ref[rows, sl], ((1,), (1,))) * SCALE_B - lse) for qh, sl, lse in zip(qhs, heads, lses)]
                if diagonal:
                    ps = [jnp.where(causal, p, 0.0) for p in ps]
                dps = [_dot(dob, v_ref[rows, sl], ((1,), (1,))) for sl in heads]
                dss = [(p * (dp - delta) * SCALE_B).astype(BF16) for p, dp, delta in zip(ps, dps, deltas)]
                for sl, qh, p, ds in zip(heads, qhs, ps, dss):
                    dq = _dot(ds, k_ref[rows, sl], ((1,), (0,)))
                    dk = _dot(ds, qh, ((0,), (0,)))
                    dv = _dot(p.astype(BF16), dob, ((0,), (0,)))
                    if diagonal:
                        dq_ref[ri, sl] = dq
                    else:
                        dq_ref[ri, sl] += dq
                    if i == nblk - 1:
                        dk_ref[rows, sl] = dk
                        dv_ref[rows, sl] = dv
                    else:
                        dk_ref[rows, sl] += dk
                        dv_ref[rows, sl] += dv
        dq_out[...] = dq_ref[...].astype(BF16)
        dk_out[...] = dk_ref[...].astype(BF16)
        dv_out[...] = dv_ref[...].astype(BF16)

    wide2 = pl.BlockSpec((S, 2 * HP), lambda b, p: (b, p))
    pair = pl.BlockSpec((S, HP), lambda b, p: (b, p))
    return pl.pallas_call(
        body, name="mla_bwd", grid=(nb, H // 2),
        in_specs=[wide2, wide2, wide2, pair, pair, wide2],
        out_specs=[wide2, wide2, wide2],
        out_shape=[jax.ShapeDtypeStruct((t, H * HP), BF16)] * 3,
        scratch_shapes=[pltpu.VMEM((S, 2 * HP), F32)] * 3,
        compiler_params=_cp("parallel", "parallel"),
    )(q, k, v, o, do, lse)


def _t5_bucket(dist):
    max_exact = N_BUCKETS // 2
    d = np.maximum(dist, 1).astype(np.float64)
    large = max_exact + (np.log(d / max_exact) / np.log(MAX_DISTANCE / max_exact) * (N_BUCKETS - max_exact)).astype(np.int64)
    large = np.minimum(large, N_BUCKETS - 1)
    return np.where(dist < max_exact, dist, large).astype(np.int32)


def _band_geometry():
    a = np.arange(BLK)[:, None]
    bk = np.arange(2 * BLK)[None, :]
    steps = BLK + a - bk
    valid = (steps >= 0) & (steps <= BLK)
    buckets = np.stack([_t5_bucket(np.clip(steps, 0, BLK) * d) for d in DILATIONS])
    return buckets, valid


def _band_bias(rel_bias):
    buckets, valid = _band_geometry()
    onehot = (jnp.asarray(buckets)[..., None] == jnp.arange(N_BUCKETS)).astype(F32)
    bias = jnp.einsum("rqkn,nh->rhqk", onehot, rel_bias, precision=lax.Precision.HIGHEST)
    bias = jnp.where(jnp.asarray(valid)[None, None], bias, NEG)
    return bias.reshape(3, H // 2, 2 * BLK, 2 * BLK)


def _dil_items():
    items = []
    for r, d in enumerate(DILATIONS):
        for res in range(d):
            for blk in range(S // d // BLK):
                items.append((r, d, blk * BLK * d + res, blk > 0))
    return items


GROUP = 4


def _strided(start, d):
    return pl.ds(start, BLK) if d == 1 else pl.ds(start, BLK, stride=d)


def _stack_heads(tile, own):
    return jnp.where(own, jnp.concatenate([tile, tile], axis=0), 0.0).astype(BF16)


def _own_lanes():
    row = lax.broadcasted_iota(jnp.int32, (2 * BLK, HP), 0)
    lane = lax.broadcasted_iota(jnp.int32, (2 * BLK, HP), 1)
    return (lane < E_A) == (row < BLK)


def _dil_fwd(proj, biasm):
    t = proj.shape[0]
    nb = t // S

    def body(q_ref, k_ref, v_ref, b_ref, o_ref, lse_ref, ob_ref, lb_ref):
        lane = lax.broadcasted_iota(jnp.int32, (BLK, HP), 1)
        own = _own_lanes()
        items = _dil_items()
        for g in range(0, len(items), GROUP):
            grp = items[g:g + GROUP]
            ss, vts = [], []
            for r, d, start, has_prev in grp:
                cur = _strided(start, d)
                rows = [_strided(start - BLK * d, d), cur] if has_prev else [cur]
                q2 = _stack_heads(q_ref[cur, :] * SCALE_A, own)
                kt = jnp.concatenate([k_ref[x, :] for x in rows], axis=0).astype(BF16)
                vts.append(jnp.concatenate([v_ref[x, :] for x in rows], axis=0).astype(BF16))
                bias = b_ref[r, 0] if has_prev else b_ref[r, 0, :, BLK:]
                ss.append(_dot(q2, kt, ((1,), (1,))) + bias)
            ms = [jnp.max(s, axis=-1, keepdims=True) for s in ss]
            ps = [jnp.exp(s - m) for s, m in zip(ss, ms)]
            ls = [jnp.sum(p, axis=-1, keepdims=True) for p in ps]
            for (r, d, start, _), p, vt, m, l in zip(grp, ps, vts, ms, ls):
                cur = _strided(start, d)
                o2 = _dot(p.astype(BF16), vt, ((1,), (0,))) / l
                lse2 = m + jnp.log(l)
                ob_ref[r, cur, :] = jnp.where(lane < E_A, o2[:BLK], o2[BLK:])
                lb_ref[r, cur, :] = jnp.where(lane < E_A, lse2[:BLK], lse2[BLK:])

        def merge(c, _):
            rows = pl.ds(pl.multiple_of(c * TQ, TQ), TQ)
            l0, l1, l2 = lb_ref[0, rows, :], lb_ref[1, rows, :], lb_ref[2, rows, :]
            m = jnp.maximum(jnp.maximum(l0, l1), l2)
            e0, e1, e2 = jnp.exp(l0 - m), jnp.exp(l1 - m), jnp.exp(l2 - m)
            tot = e0 + e1 + e2
            o_ref[rows, :] = (e0 * ob_ref[0, rows, :] + e1 * ob_ref[1, rows, :] + e2 * ob_ref[2, rows, :]) / tot
            lse_ref[rows, :] = m + jnp.log(tot)
            return 0

        lax.fori_loop(0, S // TQ, merge, 0)

    npair = H // 2
    return pl.pallas_call(
        body, name="dil_fwd", grid=(nb, npair),
        in_specs=[pl.BlockSpec((S, HP), lambda b, p: (b, p)), pl.BlockSpec((S, HP), lambda b, p: (b, npair + p)),
                  pl.BlockSpec((S, HP), lambda b, p: (b, 2 * npair + p)),
                  pl.BlockSpec((3, 1, 2 * BLK, 2 * BLK), lambda b, p: (0, p, 0, 0))],
        out_specs=[pl.BlockSpec((S, HP), lambda b, p: (b, p))] * 2,
        out_shape=[jax.ShapeDtypeStruct((t, D_A), F32)] * 2,
        scratch_shapes=[pltpu.VMEM((3, S, HP), F32), pltpu.VMEM((3, S, HP), F32)],
        compiler_params=_cp("parallel", "parallel"),
    )(proj, proj, proj, biasm)


def _dil_bwd(proj, biasm, o, do, lse):
    t = proj.shape[0]
    nb = t // S

    def body(q_ref, k_ref, v_ref, b_ref, o_ref, do_ref, lse_ref, dq_out, dk_out, dv_out, ds_ref, dq_ref, dk_ref, dv_ref):
        dq_ref[...] = jnp.zeros_like(dq_ref)
        dk_ref[...] = jnp.zeros_like(dk_ref)
        dv_ref[...] = jnp.zeros_like(dv_ref)
        ds_ref[...] = jnp.zeros_like(ds_ref)
        lane = lax.broadcasted_iota(jnp.int32, (BLK, HP), 1)
        own = _own_lanes()
        items = _dil_items()
        for g in range(0, len(items), GROUP):
            grp = items[g:g + GROUP]
            q2s, kts, do2s, ss, dps, lse2s, delta2s = [], [], [], [], [], [], []
            for r, d, start, has_prev in grp:
                cur = _strided(start, d)
                rows = [_strided(start - BLK * d, d), cur] if has_prev else [cur]
                q2 = _stack_heads(q_ref[cur, :] * SCALE_A, own)
                kt = jnp.concatenate([k_ref[x, :] for x in rows], axis=0).astype(BF16)
                vt = jnp.concatenate([v_ref[x, :] for x in rows], axis=0).astype(BF16)
                dot_ = do_ref[cur, :]
                prod = dot_ * o_ref[cur, :]
                lset = lse_ref[cur, :]
                do2 = _stack_heads(dot_, own)
                bias = b_ref[r, 0] if has_prev else b_ref[r, 0, :, BLK:]
                ss.append(_dot(q2, kt, ((1,), (1,))) + bias)
                dps.append(_dot(do2, vt, ((1,), (1,))))
                lse2s.append(jnp.concatenate([lset[:, :1], lset[:, E_A:E_A + 1]], axis=0))
                delta2s.append(jnp.concatenate([jnp.sum(jnp.where(lane < E_A, prod, 0.0), axis=-1, keepdims=True),
                                                jnp.sum(jnp.where(lane >= E_A, prod, 0.0), axis=-1, keepdims=True)], axis=0))
                q2s.append(q2)
                kts.append(kt)
                do2s.append(do2)
            ps = [jnp.exp(s - lse2) for s, lse2 in zip(ss, lse2s)]
            dls = [p * (dp - delta2) for p, dp, delta2 in zip(ps, dps, delta2s)]
            for (r, d, start, has_prev), q2, kt, do2, p, dl in zip(grp, q2s, kts, do2s, ps, dls):
                cur = _strided(start, d)
                dsb = dl.astype(BF16)
                dq2 = _dot(dsb, kt, ((1,), (0,))) * SCALE_A
                dkt = _dot(dsb, q2, ((0,), (0,)))
                dvt = _dot(p.astype(BF16), do2, ((0,), (0,)))
                dq_ref[cur, :] += jnp.where(lane < E_A, dq2[:BLK], dq2[BLK:])
                if has_prev:
                    prev = _strided(start - BLK * d, d)
                    ds_ref[0, r, 0] += dl
                    dk_ref[prev, :] += dkt[:BLK]
                    dv_ref[prev, :] += dvt[:BLK]
                    dk_ref[cur, :] += dkt[BLK:]
                    dv_ref[cur, :] += dvt[BLK:]
                else:
                    ds_ref[0, r, 0, :, BLK:] += dl
                    dk_ref[cur, :] += dkt
                    dv_ref[cur, :] += dvt
        dq_out[...] = dq_ref[...].astype(BF16)
        dk_out[...] = dk_ref[...].astype(BF16)
        dv_out[...] = dv_ref[...].astype(BF16)

    npair = H // 2
    pair = pl.BlockSpec((S, HP), lambda b, p: (b, p))
    return pl.pallas_call(
        body, name="dil_bwd", grid=(nb, npair),
        in_specs=[pair, pl.BlockSpec((S, HP), lambda b, p: (b, npair + p)),
                  pl.BlockSpec((S, HP), lambda b, p: (b, 2 * npair + p)),
                  pl.BlockSpec((3, 1, 2 * BLK, 2 * BLK), lambda b, p: (0, p, 0, 0)), pair, pair, pair],
        out_specs=[pair, pair, pair, pl.BlockSpec((1, 3, 1, 2 * BLK, 2 * BLK), lambda b, p: (b, 0, p, 0, 0))],
        out_shape=[jax.ShapeDtypeStruct((t, D_A), BF16)] * 3 + [jax.ShapeDtypeStruct((nb, 3, npair, 2 * BLK, 2 * BLK), F32)],
        scratch_shapes=[pltpu.VMEM((S, HP), F32)] * 3,
        compiler_params=_cp("parallel", "parallel"),
    )(proj, proj, proj, biasm, o, do, lse)


def _rel_bias_grad(dlogits):
    nb = dlogits.shape[0]
    buckets, _ = _band_geometry()
    kk = 3 * BLK * 2 * BLK
    dl = jnp.transpose(dlogits.reshape(nb, 3, H, BLK, 2 * BLK), (0, 2, 1, 3, 4)).reshape(nb, H, kk)
    bk = jnp.asarray(buckets.reshape(1, kk))
    tk = kk // 12

    def body(dl_ref, bk_ref, o_ref):
        j = pl.program_id(0)
        onehot = (bk_ref[...] == lax.broadcasted_iota(jnp.int32, (N_BUCKETS, tk), 0)).astype(F32)
        tot = dl_ref[0]
        for b in range(1, nb):
            tot = tot + dl_ref[b]
        part = lax.dot_general(onehot, tot, ((((1,), (1,))), ((), ())), preferred_element_type=F32,
                               precision=lax.Precision.HIGHEST)
        _acc_first(j, o_ref, part)

    return pl.pallas_call(
        body, name="rel_bias_grad", grid=(kk // tk,),
        in_specs=[pl.BlockSpec((nb, H, tk), lambda j: (0, 0, j)), pl.BlockSpec((1, tk), lambda j: (0, j))],
        out_specs=pl.BlockSpec((N_BUCKETS, H), lambda j: (0, 0)),
        out_shape=jax.ShapeDtypeStruct((N_BUCKETS, H), F32),
        compiler_params=_cp("arbitrary"),
    )(dl, bk)


def _mesh_place():
    x, y, c = lax.axis_index("x"), lax.axis_index("y"), lax.axis_index("c")
    return x, y, c


def _peer(k):
    x, y, c = _mesh_place()
    px = 1 - x if k & 4 else x
    py = 1 - y if k & 2 else y
    pc = 1 - c if k & 1 else c
    return (px, py, pc), 4 * px + 2 * py + pc


ANY = pl.BlockSpec(memory_space=pl.ANY)


def _exchange(arrays, gathers, name, after=None):
    n_arr = len(arrays)

    def body(*refs):
        ins, outs = refs[:n_arr], refs[n_arr + 1:2 * n_arr + 1]
        send, recv, loc = refs[2 * n_arr + 1:]
        x, y, c = _mesh_place()
        me = 4 * x + 2 * y + c
        local = [pltpu.make_async_copy(ins[a] if gathers[a] else ins[a].at[me], outs[a].at[me], loc.at[a])
                 for a in range(n_arr)]
        remote = _peer_copies(ins, outs, send, recv, gathers)
        for cp in local:
            cp.start()
        for put, _ in remote:
            put.start()
        for cp in local:
            cp.wait()
        for put, got in remote:
            put.wait_send()
            got.wait_recv()

    return pl.pallas_call(
        body, name=name,
        in_specs=[ANY] * (n_arr + 1), out_specs=[ANY] * n_arr,
        out_shape=[jax.ShapeDtypeStruct(((N_DEV,) if g else ()) + a.shape, a.dtype) for a, g in zip(arrays, gathers)],
        scratch_shapes=[pltpu.SemaphoreType.DMA((n_arr * (N_DEV - 1),)), pltpu.SemaphoreType.DMA((n_arr * (N_DEV - 1),)),
                        pltpu.SemaphoreType.DMA((n_arr,))],
        compiler_params=pltpu.CompilerParams(has_side_effects=True),
    )(*arrays, arrays[0] if after is None else after)


def _gather_two_level(arrays, name):
    n_arr = len(arrays)
    per = N_DEV - 1

    def body(*refs):
        ins, outs = refs[:n_arr], refs[n_arr:2 * n_arr]
        send, recv, loc = refs[2 * n_arr:]
        x, y, c = _mesh_place()
        me, sibling = (x, y, c), (x, y, 1 - c)
        chips = [(1 - x, y), (x, 1 - y), (1 - x, 1 - y)]

        def block(a, place):
            px, py, pc = place
            return outs[a].at[4 * px + 2 * py + pc]

        def copy(a, k, place, to, src=None):
            dst = block(a, place)
            return pltpu.make_async_remote_copy(dst if src is None else src, dst, send.at[a * per + k], recv.at[a * per + k],
                                                device_id=to, device_id_type=pl.DeviceIdType.MESH)

        local = [pltpu.make_async_copy(ins[a], block(a, me), loc.at[a]) for a in range(n_arr)]
        for cp in local:
            cp.start()
        first = []
        for a in range(n_arr):
            first.append(copy(a, 0, me, sibling, src=ins[a]))
            first += [copy(a, 1 + j, me, (*chip, c), src=ins[a]) for j, chip in enumerate(chips)]
        for cp in first:
            cp.start()
        passed = []
        for j, chip in enumerate(chips):
            for a in range(n_arr):
                copy(a, 1 + j, (*chip, c), me).wait_recv()
                passed.append(copy(a, 4 + j, (*chip, c), sibling))
                passed[-1].start()
        for a in range(n_arr):
            copy(a, 0, sibling, me).wait_recv()
            for j, chip in enumerate(chips):
                copy(a, 4 + j, (*chip, 1 - c), me).wait_recv()
        for cp in first + passed:
            cp.wait_send()
        for cp in local:
            cp.wait()

    return pl.pallas_call(
        body, name=name,
        in_specs=[ANY] * n_arr, out_specs=[ANY] * n_arr,
        out_shape=[jax.ShapeDtypeStruct((N_DEV,) + a.shape, a.dtype) for a in arrays],
        scratch_shapes=[pltpu.SemaphoreType.DMA((n_arr * per,)), pltpu.SemaphoreType.DMA((n_arr * per,)),
                        pltpu.SemaphoreType.DMA((n_arr,))],
        compiler_params=pltpu.CompilerParams(has_side_effects=True),
    )(*arrays)


HBM = pl.BlockSpec(memory_space=pltpu.HBM)
SEM = pl.BlockSpec(memory_space=pltpu.SEMAPHORE)
DATAFLOW = pltpu.SideEffectType.DATAFLOW_SIDE_EFFECTING


def _own_block_in_place(block, me):
    land = lax.empty((N_DEV,) + block.shape, block.dtype)
    return lax.dynamic_update_slice(land, block[None], (me,) + (0,) * block.ndim)


def _peer_copies(srcs, lands, send, recv, gathers):
    x, y, c = _mesh_place()
    me = 4 * x + 2 * y + c
    out = []
    for a, (src, land) in enumerate(zip(srcs, lands)):
        for k in range(1, N_DEV):
            dev, idx = _peer(k)
            sem = a * (N_DEV - 1) + k - 1
            mine = src if gathers[a] else src.at[idx]
            put = pltpu.make_async_remote_copy(mine, land.at[me], send.at[sem], recv.at[sem],
                                               device_id=dev, device_id_type=pl.DeviceIdType.MESH)
            got = pltpu.make_async_remote_copy(mine, land.at[idx], send.at[sem], recv.at[sem],
                                               device_id=dev, device_id_type=pl.DeviceIdType.MESH)
            out.append((put, got))
    return out


def _exchange_start(srcs, lands, gather, after, name):
    n = len(srcs)

    def body(*refs):
        srcs_, lands_, send, recv = refs[:n], refs[n:2 * n], refs[2 * n + 1], refs[2 * n + 2]
        for put, _ in _peer_copies(srcs_, lands_, send, recv, gather):
            put.start()
        refs[-1][...] = jnp.zeros_like(refs[-1])

    nsem = n * (N_DEV - 1)
    thru = [pltpu.HBM(a.shape, a.dtype) for a in list(srcs) + list(lands)]
    res = pl.pallas_call(
        body, name=name,
        out_shape=(pltpu.SemaphoreType.DMA((nsem,)), pltpu.SemaphoreType.DMA((nsem,)), *thru, jax.ShapeDtypeStruct((8, 128), F32)),
        in_specs=[HBM] * (2 * n) + [ANY],
        out_specs=(SEM, SEM, *([HBM] * (2 * n)), pl.BlockSpec(memory_space=pltpu.VMEM)),
        input_output_aliases={i: 2 + i for i in range(2 * n)},
        compiler_params=pltpu.CompilerParams(has_side_effects=DATAFLOW),
    )(*[pltpu.with_memory_space_constraint(a, pltpu.HBM) for a in list(srcs) + list(lands)], after)
    return res[0], res[1], list(res[2:2 + n]), list(res[2 + n:2 + 2 * n]), res[-1]


def _exchange_wait(send, recv, srcs, lands, gather, after, name):
    n = len(srcs)

    def body(*refs):
        srcs_, lands_, send_, recv_ = refs[:n], refs[n:2 * n], refs[2 * n], refs[2 * n + 1]
        for put, got in _peer_copies(srcs_, lands_, send_, recv_, gather):
            put.wait_send()
            got.wait_recv()

    thru = [pltpu.HBM(a.shape, a.dtype) for a in list(srcs) + list(lands)]
    res = pl.pallas_call(
        body, name=name, out_shape=tuple(thru),
        in_specs=[HBM] * (2 * n) + [SEM, SEM, ANY], out_specs=tuple([HBM] * (2 * n)),
        input_output_aliases={i: i for i in range(2 * n)},
        compiler_params=pltpu.CompilerParams(has_side_effects=DATAFLOW),
    )(*srcs, *lands, send, recv, after)
    return list(res[n:])


def _silu_rows(c):
    def body(c_ref, o_ref):
        v = c_ref[...]
        o_ref[...] = v * _sigmoid(v)

    return pl.pallas_call(body, name="cond", out_shape=jax.ShapeDtypeStruct(c.shape, F32))(c)


def _mod_slab(cond_all, w_ada, b_slab):
    def body(c_ref, w_ref, b_ref, o_ref):
        o_ref[...] = _dot(c_ref[...].astype(BF16), w_ref[0].astype(BF16), ((1,), (0,))) + b_ref[...]

    return pl.pallas_call(body, name="mod_slab",
                          out_shape=jax.ShapeDtypeStruct((cond_all.shape[0], w_ada.shape[2]), F32),
                          compiler_params=pltpu.CompilerParams(vmem_limit_bytes=VMEM_LIMIT))(cond_all, w_ada, b_slab)


def _ada_grad(cond_all, dmod_cols):
    def body(c_ref, d_ref, o_ref):
        o_ref[...] = _dot(c_ref[...].astype(BF16), d_ref[...].astype(BF16), ((0,), (0,)))

    return pl.pallas_call(body, name="ada_grad",
                          out_shape=jax.ShapeDtypeStruct((cond_all.shape[1], dmod_cols.shape[1]), F32),
                          compiler_params=pltpu.CompilerParams(vmem_limit_bytes=VMEM_LIMIT))(cond_all, dmod_cols)


def _adam_math(g, w, m, v):
    m2 = B1 * m + (1.0 - B1) * g
    v2 = B2 * v + (1.0 - B2) * (g * g)
    m_hat = m2 / (1.0 - B1 ** STEP)
    v_hat = v2 / (1.0 - B2 ** STEP)
    return -LR * (m_hat / (jnp.sqrt(v_hat) + ADAM_EPS) + WD * w), m2, v2


def _adamw(parts, w, m, v, name):
    n, rows, cols = parts.shape
    tr = _pick(rows, (128, 96, 64, 32, 16, 8))

    def body(p_ref, w_ref, m_ref, v_ref, g_ref, d_ref, m2_ref, v2_ref):
        g = p_ref[0].astype(F32)
        for s in range(1, n):
            g = g + p_ref[s].astype(F32)
        g_ref[0] = g
        d_ref[0], m2_ref[0], v2_ref[0] = _adam_math(g, w_ref[0], m_ref[0], v_ref[0])

    blk = pl.BlockSpec((1, tr, cols), lambda i: (0, i, 0))
    return pl.pallas_call(
        body, name=name, grid=(rows // tr,),
        in_specs=[pl.BlockSpec((n, tr, cols), lambda i: (0, i, 0)), blk, blk, blk],
        out_specs=[blk] * 4, out_shape=[jax.ShapeDtypeStruct((1, rows, cols), F32)] * 4,
        compiler_params=_cp("parallel"),
    )(*[pltpu.with_memory_space_constraint(a, pltpu.HBM) for a in (parts, w, m, v)])


ROW_PARAMS = (("g_norm1", D), ("g_cq", Q_LORA), ("g_ckv", KV_LORA), ("g_out_a", D_A), ("g_out_b", D_A), ("g_norm2", D),
              ("g_final", D))
LOSS_ROW = N_MOD + len(ROW_PARAMS)
PAY_ROWS = 16
NCOL = N_MOD * D // N_DEV


def _pack_small(dmods, rows, loss_cols):
    nb = dmods[0].shape[0]
    nrow = len(ROW_PARAMS)

    def body(*refs):
        dm, rw, loss_ref, pay_ref, blk_ref = refs[:N_MOD], refs[N_MOD:N_MOD + nrow], refs[N_MOD + nrow], refs[-2], refs[-1]
        pay_ref[...] = jnp.zeros_like(pay_ref)
        for k in range(N_MOD):
            tot = dm[k][0]
            for b in range(1, nb):
                tot = tot + dm[k][b]
            pay_ref[k:k + 1, :] = tot
        for i, (_, n) in enumerate(ROW_PARAMS):
            pay_ref[N_MOD + i:N_MOD + i + 1, :n] = rw[i][...]
        pay_ref[LOSS_ROW:LOSS_ROW + 1, :] = loss_ref[...]
        for j in range(N_DEV):
            done = 0
            while done < NCOL:
                seg, off = divmod(j * NCOL + done, D)
                ln = min(NCOL - done, D - off)
                for b in range(nb):
                    blk_ref[j, b:b + 1, done:done + ln] = dm[seg][b][:, off:off + ln]
                done += ln

    return pl.pallas_call(
        body, name="pack_small",
        out_shape=[jax.ShapeDtypeStruct((PAY_ROWS, D), F32), jax.ShapeDtypeStruct((N_DEV, nb, NCOL), F32)],
    )(*dmods, *rows, loss_cols)


def _small_update(pay, rel, ws, ms, vs):
    n_par = len(ws)

    def body(*refs):
        pay_ref, rel_ref = refs[:2]
        w_refs, m_refs, v_refs = (refs[2 + s * n_par:2 + (s + 1) * n_par] for s in range(3))
        outs, loss_ref = refs[2 + 3 * n_par:-1], refs[-1]
        tot, rtot = pay_ref[0], rel_ref[0]
        for s in range(1, N_DEV):
            tot, rtot = tot + pay_ref[s], rtot + rel_ref[s]

        def update(p, g, sl):
            outs[4 * p][:, sl] = g
            outs[4 * p + 1][:, sl], outs[4 * p + 2][:, sl], outs[4 * p + 3][:, sl] = _adam_math(
                g, w_refs[p][:, sl], m_refs[p][:, sl], v_refs[p][:, sl])

        for k in range(N_MOD):
            update(0, tot[k:k + 1, :], slice(k * D, (k + 1) * D))
        for i, (_, n) in enumerate(ROW_PARAMS):
            update(1 + i, tot[N_MOD + i:N_MOD + i + 1, :n], slice(0, n))
        update(n_par - 1, rtot, slice(0, H))
        loss_ref[...] = jnp.broadcast_to((0.5 / D) * jnp.sum(tot[LOSS_ROW:LOSS_ROW + 1, :]), loss_ref.shape)

    shapes = [jax.ShapeDtypeStruct(w.shape, F32) for w in ws for _ in range(4)]
    res = pl.pallas_call(
        body, name="small_update", out_shape=shapes + [jax.ShapeDtypeStruct((8, 128), F32)],
    )(pay, rel, *ws, *ms, *vs)
    return [tuple(res[4 * p:4 * p + 4]) for p in range(n_par)], res[-1]


def _cols_from_blocks(g):
    return jnp.transpose(g, (1, 0, 2)).reshape(g.shape[1], N_DEV * g.shape[2])


def _cols_to_blocks(w):
    r, c = w.shape
    return jnp.transpose(w.reshape(r, N_DEV, c // N_DEV), (1, 0, 2))


def _pad_w_in(w):
    z = jnp.zeros((w.shape[0], NOPE), w.dtype)
    return jnp.concatenate([w[:, :P_IN - ROPE], z, w[:, P_IN - ROPE:], z[:, :HP - NOPE - ROPE]], axis=1)


def _unpad_w_in(g):
    k0 = P_IN - ROPE + NOPE
    return jnp.concatenate([g[:, :P_IN - ROPE], g[:, k0:k0 + ROPE]], axis=1)


def _pad_w_uq(w):
    w3 = w.reshape(Q_LORA, H, NOPE + ROPE)
    return jnp.pad(w3, ((0, 0), (0, 0), (0, HP - NOPE - ROPE))).reshape(Q_LORA, H * HP)


def _unpad_w_uq(g):
    return g.reshape(Q_LORA, H, HP)[:, :, :NOPE + ROPE].reshape(Q_LORA, H * (NOPE + ROPE))


def _split_w_ukv(w):
    w4 = w.reshape(KV_LORA, H // 2, 2, HP)
    z = jnp.zeros((KV_LORA, H // 2, NOPE), w.dtype)
    kn, vv = w4[..., :NOPE], w4[..., NOPE:]
    w_k = jnp.stack([jnp.concatenate([kn[:, :, 0], z], -1), jnp.concatenate([kn[:, :, 1], z], -1)], axis=2)
    w_v = jnp.stack([jnp.concatenate([vv[:, :, 0], z], -1), jnp.concatenate([z, vv[:, :, 1]], -1)], axis=2)
    return w_k.reshape(KV_LORA, H * HP), w_v.reshape(KV_LORA, H * HP)


def _join_w_ukv(g_k, g_v):
    gk = g_k.reshape(KV_LORA, H // 2, 2, HP)
    gv = g_v.reshape(KV_LORA, H // 2, 2, HP)
    even = jnp.concatenate([gk[:, :, 0, :NOPE], gv[:, :, 0, :VDIM]], -1)
    odd = jnp.concatenate([gk[:, :, 1, :NOPE], gv[:, :, 1, VDIM:]], -1)
    return jnp.stack([even, odd], axis=2).reshape(KV_LORA, H * HP)


def _rope_tables():
    half = ROPE // 2
    inv = ROPE_THETA ** (-jnp.arange(half, dtype=F32) / half)
    ang = jnp.arange(S, dtype=F32)[:, None] * inv[None, :]
    cos, sin = jnp.cos(ang), jnp.sin(ang)
    ones, zeros = jnp.ones((S, NOPE), F32), jnp.zeros((S, NOPE), F32)
    tail1, tail0 = jnp.ones((S, HP - NOPE - ROPE), F32), jnp.zeros((S, HP - NOPE - ROPE), F32)
    zh = jnp.zeros((S, half), F32)
    c = jnp.concatenate([ones, cos, cos, tail1], axis=1)
    sm = jnp.concatenate([zeros, -sin, zh, tail0], axis=1)
    sp = jnp.concatenate([zeros, zh, sin, tail0], axis=1)
    return c, sm, sp


def _local_step(x, mod, target, g_norm1, w_in_p, g_cq, w_uq_p, g_ckv, w_k, w_v, rel_bias, g_out_a, g_out_b, w_out,
                g_norm2, w_ffn_in, w_ffn_out, g_final, late_weights=None, on_ffn_grads=None, on_last_grads=None):
    nb = x.shape[0] // S
    sh1, sc1, g1, sh2, sc2, g2 = (mod[:, n].reshape(nb, 1, D) for n in range(N_MOD))
    rc, rsm, rsp = _rope_tables()
    biasm = _band_bias(rel_bias)

    h1 = _pre1(x, g_norm1, sc1, sh1)
    proj = _mm_nn(h1, w_in_p, F32, "proj")
    q, k, v, cqn, ckvn = _mla_pre(proj, g_cq, g_ckv, w_uq_p, w_k, w_v, rc, rsm, rsp)
    out_b, lse_b = _mla_fwd(q, k, v)
    out_a, lse_a = _dil_fwd(proj, biasm)
    y = _post_attn(out_a, out_b, g_out_a, g_out_b)
    if late_weights is not None:
        w_out, w_ffn_in, w_ffn_out = late_weights(y)
    mix = _mm_nn(y, w_out, BF16, "mix")
    x2, h2 = _resid_norm2(x, mix, g1, g_norm2, sc2, sh2)
    ffn_g, ffn_u, act = _ffn_in(h2, w_ffn_in)
    f = _mm_nn(act, w_ffn_out, BF16, "ffn_out")
    dx3, df, loss_cols, dg_final, dg2 = _final(x2, f, g2, g_final, target)

    dg_, du_ = _d_act(df, w_ffn_out, ffn_g, ffn_u)
    gw_ffn_out = _mm_tn(act, [df], "gw_ffn_out")
    dh2 = _d_h2(dg_, du_, w_ffn_in)
    gw_ffn_in = _mm_tn_pair(dg_, du_, h2, "gw_ffn_in")
    dx2, dsh2, dsc2, dg_norm2, dg1, dmix = _norm_bwd(x2, dh2, dx3, g_norm2, sc2, gate=(mix, g1))
    dy = _mm_nt(dmix, w_out, BF16, "d_y")
    gw_out = _mm_tn(y, [dmix], "gw_out")
    if on_ffn_grads is not None:
        g_out_a = g_out_a + on_ffn_grads(gw_ffn_in, gw_ffn_out, gw_out)
    dout_a, dout_b, dg_out_a, dg_out_b = _post_attn_bwd(dy, out_a, out_b, g_out_a, g_out_b)
    dq_b, dk_b, dv_b = _mla_bwd(q, k, v, out_b, dout_b, lse_b)
    dq_a, dk_a, dv_a, dlogits = _dil_bwd(proj, biasm, out_a, dout_a, lse_a)
    g_rel = _rel_bias_grad(dlogits)
    dqr, dproj, dg_cq, dg_ckv = _mla_pre_bwd(proj, dq_b, dk_b, dv_b, (dq_a, dk_a, dv_a), g_cq, g_ckv, w_uq_p, w_k, w_v,
                                             rc, rsm, rsp)
    gw_uq = _mm_tn(cqn, [dqr], "gw_uq")
    gw_k, gw_v = _mm_tn(ckvn, [dk_b, dv_b], "gw_kv")
    gw_in = _mm_tn(h1, [dproj], "gw_in")
    if on_last_grads is not None:
        started = on_last_grads(dict(w_in=gw_in, w_uq=gw_uq, w_k=gw_k, w_v=gw_v))
    else:
        started = None
    dh1 = _mm_nt(dproj, w_in_p, BF16, "d_h1", after=started)
    grad_x, dsh1, dsc1, dg_norm1 = _norm_bwd(x, dh1, dx2, g_norm1, sc1)

    dmod = [dsh1, dsc1, dg1, dsh2, dsc2, dg2]
    small = dict(g_norm1=dg_norm1, g_cq=dg_cq, g_ckv=dg_ckv, rel_bias=g_rel, g_out_a=dg_out_a, g_out_b=dg_out_b,
                 g_norm2=dg_norm2, g_final=dg_final)
    big = dict(w_in=gw_in, w_uq=gw_uq, w_k=gw_k, w_v=gw_v, w_out=gw_out, w_ffn_in=gw_ffn_in, w_ffn_out=gw_ffn_out)
    return grad_x, dmod, loss_cols, small, big


def kernel(x, c, w_ada, b_ada, g_norm1, w_in, g_cq, w_uq, g_ckv, w_ukv, rel_bias, g_out_a, g_out_b, w_out, g_norm2, w_ffn_in, w_ffn_out, g_final, loss_target, m_w_ada, m_b_ada, m_g_norm1, m_w_in, m_g_cq, m_w_uq, m_g_ckv, m_w_ukv, m_rel_bias, m_g_out_a, m_g_out_b, m_w_out, m_g_norm2, m_w_ffn_in, m_w_ffn_out, m_g_final, v_w_ada, v_b_ada, v_g_norm1, v_w_in, v_g_cq, v_w_uq, v_g_ckv, v_w_ukv, v_rel_bias, v_g_out_a, v_g_out_b, v_w_out, v_g_norm2, v_w_ffn_in, v_w_ffn_out, v_g_final):
    nb = x.shape[0]
    t = nb * S
    xt, tt = x.reshape(t, D), loss_target.reshape(t, D)
    me = 4 * lax.axis_index("x") + 2 * lax.axis_index("y") + lax.axis_index("c")

    early = [w_in[0], w_uq[0], w_ukv[0]]
    gathered = _gather_two_level([_silu_rows(c)] + [s.astype(BF16) for s in early], "gather_weights")
    cond_all = gathered[0].reshape(N_DEV * nb, D)
    w_in_f, w_uq_f, w_ukv_f = (_cols_from_blocks(g) for g in gathered[1:4])
    w_k, w_v = _split_w_ukv(w_ukv_f)

    ncol = N_MOD * D // N_DEV
    b_slab = lax.dynamic_slice(b_ada, (0, me * ncol), (1, ncol))
    slab = _mod_slab(cond_all, w_ada, b_slab)
    (mod_rows,) = _exchange([slab.reshape(N_DEV, nb, ncol)], [False], "scatter_mod")
    mod = jnp.transpose(mod_rows, (1, 0, 2)).reshape(nb, N_MOD, D)

    late = [s.astype(BF16) for s in (w_out[0], jnp.swapaxes(w_ffn_in, 1, 2)[0], w_ffn_out[0])]
    late_send, late_recv, late_src, late_land, late_token = _exchange_start(
        late, [_own_block_in_place(s, me) for s in late], [True] * 3, mod_rows, "gather_late_start")
    g_norm1_t = g_norm1 + late_token[:1, :1]

    def late_weights(after):
        w_out_g, w_ffn_in_g, w_ffn_out_g = _exchange_wait(late_send, late_recv, late_src, late_land, [True] * 3, after,
                                                          "gather_late_wait")
        return w_out_g.reshape(D, D), w_ffn_in_g.reshape(2 * D_FF, D), w_ffn_out_g.reshape(D_FF, D)

    flight = {}

    def start_grads(key, src, name):
        land = [_own_block_in_place(lax.dynamic_index_in_dim(s, me, 0, keepdims=False), me) for s in src]
        send, recv, src, land, token = _exchange_start(src, land, [False] * len(src), src[0], name)
        flight[key] = (send, recv, src, land)
        return token[:1, :1]

    def on_ffn_grads(gw_ffn_in, gw_ffn_out, gw_out):
        return start_grads("ffn", [gw_ffn_in.reshape(N_DEV, 2 * D_FF // N_DEV, D), gw_ffn_out.reshape(N_DEV, D_FF // N_DEV, D),
                                   gw_out.reshape(N_DEV, D // N_DEV, D)], "exchange_ffn_start")

    def on_last_grads(gw):
        return start_grads("rest", [_cols_to_blocks(_unpad_w_in(gw["w_in"])), _cols_to_blocks(_unpad_w_uq(gw["w_uq"])),
                                    _cols_to_blocks(_join_w_ukv(gw["w_k"], gw["w_v"]))], "exchange_rest_start")

    grad_x, dmod, loss_cols, small, _ = _local_step(
        xt, mod, tt, g_norm1_t, _pad_w_in(w_in_f), g_cq, _pad_w_uq(w_uq_f), g_ckv, w_k, w_v, rel_bias, g_out_a, g_out_b,
        None, g_norm2, None, None, g_final.reshape(1, D), late_weights=late_weights, on_ffn_grads=on_ffn_grads,
        on_last_grads=on_last_grads)

    upd = {}

    def land_and_update(key, names, after, name):
        got = _exchange_wait(*flight[key], [False] * len(names), after, name)
        for n, p in zip(names, got):
            w, m, v = big[n]
            upd[n] = _adamw(p, w, m, v, "adamw_" + n)

    def flip(a):
        return jnp.swapaxes(a, 1, 2)

    big = dict(w_in=(w_in, m_w_in, v_w_in), w_uq=(w_uq, m_w_uq, v_w_uq), w_ukv=(w_ukv, m_w_ukv, v_w_ukv),
               w_out=(w_out, m_w_out, v_w_out), w_ffn_in=(flip(w_ffn_in), flip(m_w_ffn_in), flip(v_w_ffn_in)),
               w_ffn_out=(w_ffn_out, m_w_ffn_out, v_w_ffn_out))
    land_and_update("ffn", ["w_ffn_in", "w_ffn_out", "w_out"], grad_x, "exchange_ffn_wait")
    land_and_update("rest", ["w_in", "w_uq", "w_ukv"], upd["w_out"][0], "exchange_rest_wait")
    upd["w_ffn_in"] = tuple(flip(a) for a in upd["w_ffn_in"])

    mine, dmod_blocks = _pack_small(dmod, [small[n] for n, _ in ROW_PARAMS], loss_cols)
    dmod_cols, pay, rel = _exchange([dmod_blocks, mine, small["rel_bias"]], [False, True, True], "exchange_small",
                                    after=upd["w_ukv"][0])
    g_ada = _ada_grad(cond_all, dmod_cols.reshape(N_DEV * nb, ncol))
    upd["w_ada"] = _adamw(g_ada[None], w_ada, m_w_ada, v_w_ada, "adamw_w_ada")
    row = lambda a: a.reshape(1, D)
    small_names = ["b_ada"] + [n for n, _ in ROW_PARAMS] + ["rel_bias"]
    small_w = [b_ada, g_norm1, g_cq, g_ckv, g_out_a, g_out_b, g_norm2, row(g_final), rel_bias]
    small_m = [m_b_ada, m_g_norm1, m_g_cq, m_g_ckv, m_g_out_a, m_g_out_b, m_g_norm2, row(m_g_final), m_rel_bias]
    small_v = [v_b_ada, v_g_norm1, v_g_cq, v_g_ckv, v_g_out_a, v_g_out_b, v_g_norm2, row(v_g_final), v_rel_bias]
    small_upd, loss8 = _small_update(pay, rel, small_w, small_m, small_v)
    upd.update(zip(small_names, small_upd))

    order = ["w_ada", "b_ada", "g_norm1", "w_in", "g_cq", "w_uq", "g_ckv", "w_ukv", "rel_bias", "g_out_a", "g_out_b",
             "w_out", "g_norm2", "w_ffn_in", "w_ffn_out", "g_final"]
    like = dict(g_final=g_final)
    outs = [loss8[0, 0], grad_x.reshape(x.shape)]
    for part in range(4):
        for n in order:
            val = upd[n][part]
            outs.append(val.reshape(like[n].shape) if n in like else val)
    return tuple(outs)
```

```python
import functools

import numpy as np
import jax
import jax.numpy as jnp
from jax import lax
from jax.experimental import pallas as pl
from jax.experimental.pallas import tpu as pltpu

F32, BF16 = jnp.float32, jnp.bfloat16

N_DEV = 8
D = 1024
S = 2048
H = 8
E_A = 64
D_A = H * E_A
Q_LORA, KV_LORA = 384, 256
NOPE, ROPE, VDIM = 64, 32, 64
HP = 128
P_IN = 3 * D_A + Q_LORA + KV_LORA + ROPE
P_PAD = 3 * D_A + Q_LORA + KV_LORA + HP
TAIL0 = 3 * D_A
TAIL = P_PAD - TAIL0
D_FF = 2816
N_MOD = 6
EPS = 1e-6
NEG = -1e30
BLK = 128
DILATIONS = (1, 4, 16)
N_BUCKETS, MAX_DISTANCE = 32, 2048
ROPE_THETA = 10000.0
SCALE_A = E_A ** -0.5
SCALE_B = (NOPE + ROPE) ** -0.5
B1, B2, LR, ADAM_EPS, WD, STEP = 0.9, 0.999, 0.001, 1e-8, 0.01, 10
VMEM_LIMIT = 56 * 1024 * 1024


def _cp(*sem):
    return pltpu.CompilerParams(dimension_semantics=sem, vmem_limit_bytes=VMEM_LIMIT)


def _pick(n, prefs):
    for p in prefs:
        if n % p == 0:
            return p
    raise ValueError(f"no tile of {prefs} divides {n}")


OPERAND_BYTES = 6 * 1024 * 1024


def _pick_rows(m, k):
    return _pick(m, [p for p in (1024, 512, 256, 128, 16) if p * k * 2 <= OPERAND_BYTES])


MATMUL_BYTES = 40 * 1024 * 1024


def _stream_rows(m, fixed, per_row):
    return _pick(m, [p for p in (4096, 2048, 1024, 512, 256, 128, 16) if fixed + p * per_row <= MATMUL_BYTES])


def _dot(a, b, dims):
    return lax.dot_general(a, b, (dims, ((), ())), preferred_element_type=F32)


def _mm_nn(a, b, out_dtype, name, after=None):
    m, k = a.shape
    n = b.shape[1]
    tn = _pick(n, (512, 384, 256, 128))
    tm = _stream_rows(m, 4 * k * tn, 4 * k + (2 * jnp.dtype(out_dtype).itemsize + 4) * tn)

    def body(a_ref, b_ref, *rest):
        o_ref = rest[-1]
        o_ref[...] = _dot(a_ref[...], b_ref[...], ((1,), (0,))).astype(o_ref.dtype)

    extra = [] if after is None else [after]
    return pl.pallas_call(
        body, name=name, grid=(m // tm, n // tn),
        in_specs=[pl.BlockSpec((tm, k), lambda i, j: (i, 0)), pl.BlockSpec((k, tn), lambda i, j: (0, j))] + [ANY] * len(extra),
        out_specs=pl.BlockSpec((tm, tn), lambda i, j: (i, j)),
        out_shape=jax.ShapeDtypeStruct((m, n), out_dtype),
        compiler_params=_cp("parallel", "parallel"),
    )(a, b, *extra)


def _mm_nt(a, b, out_dtype, name, after=None):
    m, k = a.shape
    n = b.shape[0]
    tn = _pick(n, (512, 384, 256, 128))
    tm = _stream_rows(m, 4 * k * tn, 4 * k + (2 * jnp.dtype(out_dtype).itemsize + 4) * tn)

    def body(a_ref, b_ref, *rest):
        o_ref = rest[-1]
        o_ref[...] = _dot(a_ref[...], b_ref[...], ((1,), (1,))).astype(o_ref.dtype)

    extra = [] if after is None else [after]
    return pl.pallas_call(
        body, name=name, grid=(m // tm, n // tn),
        in_specs=[pl.BlockSpec((tm, k), lambda i, j: (i, 0)), pl.BlockSpec((tn, k), lambda i, j: (j, 0))] + [ANY] * len(extra),
        out_specs=pl.BlockSpec((tm, tn), lambda i, j: (i, j)),
        out_shape=jax.ShapeDtypeStruct((m, n), out_dtype),
        compiler_params=_cp("parallel", "parallel"),
    )(a, b, *extra)


def _mm_tn(a, bs, name):
    t, m = a.shape
    n = bs[0].shape[1]
    nb_ = len(bs)
    tc = _pick(t, (512, 16))
    tn = _pick(n, (512, 384, 256, 128))
    tm = _pick(m, [p for p in (1024, 512, 384, 256, 128) if (3 * p + 2 * nb_ * tn) * t * 2 <= VMEM_LIMIT - 2 * OPERAND_BYTES])
    if tm <= 256 and nb_ * n * t * 2 <= 2 * OPERAND_BYTES:
        tn = n

    def body(*refs):
        a_ref, b_refs, o_refs, at_ref = refs[0], refs[1:1 + nb_], refs[1 + nb_:1 + 2 * nb_], refs[-1]

        @pl.when(pl.program_id(1) == 0)
        def _():
            def chunk(c, _):
                rows = pl.ds(pl.multiple_of(c * tc, tc), tc)
                at_ref[:, rows] = a_ref[rows, :].T
                return 0

            lax.fori_loop(0, t // tc, chunk, 0)

        for b_ref, o_ref in zip(b_refs, o_refs):
            o_ref[...] = _dot(at_ref[...], b_ref[...], ((1,), (0,))).astype(BF16)

    res = pl.pallas_call(
        body, name=name, grid=(m // tm, n // tn),
        in_specs=[pl.BlockSpec((t, tm), lambda i, j: (0, i))] + [pl.BlockSpec((t, tn), lambda i, j: (0, j))] * nb_,
        out_specs=[pl.BlockSpec((tm, tn), lambda i, j: (i, j))] * nb_,
        out_shape=[jax.ShapeDtypeStruct((m, n), BF16)] * nb_,
        scratch_shapes=[pltpu.VMEM((tm, t), BF16)],
        compiler_params=_cp("parallel", "arbitrary"),
    )(a, *bs)
    return res[0] if nb_ == 1 else res


def _mm_tn_pair(a0, a1, b, name):
    t, m = a0.shape
    n = b.shape[1]
    tc, tm = _pick(t, (512, 16)), _pick(m, (256, 128))
    nblk = m // tm

    def body(a0_ref, a1_ref, b_ref, o_ref, at_ref):
        i = pl.program_id(0)

        def transpose_from(src):
            def chunk(c, _):
                rows = pl.ds(pl.multiple_of(c * tc, tc), tc)
                at_ref[:, rows] = src[rows, :].T
                return 0

            lax.fori_loop(0, t // tc, chunk, 0)

        @pl.when(i < nblk)
        def _():
            transpose_from(a0_ref)

        @pl.when(i >= nblk)
        def _():
            transpose_from(a1_ref)

        o_ref[...] = _dot(at_ref[...], b_ref[...], ((1,), (0,))).astype(BF16)

    return pl.pallas_call(
        body, name=name, grid=(2 * nblk,),
        in_specs=[pl.BlockSpec((t, tm), lambda i: (0, jnp.minimum(i, nblk - 1))),
                  pl.BlockSpec((t, tm), lambda i: (0, jnp.maximum(i - nblk, 0))), pl.BlockSpec((t, n), lambda i: (0, 0))],
        out_specs=pl.BlockSpec((tm, n), lambda i: (i, 0)),
        out_shape=jax.ShapeDtypeStruct((2 * m, n), BF16),
        scratch_shapes=[pltpu.VMEM((tm, t), BF16)],
        compiler_params=_cp("arbitrary"),
    )(a0, a1, b)


EPI = 256


def _silu_parts(g):
    sg = 0.5 * jnp.tanh(0.5 * g) + 0.5
    return sg, g * sg


def _ffn_in(h2, wt):
    t, k = h2.shape
    tn = _pick(D_FF, (256, 128))
    tm = _stream_rows(t, 8 * k * tn, 4 * k + (3 * 2 * 2 + 2 * 4) * tn)
    nj = D_FF // tn

    def body(h_ref, wg_ref, wu_ref, g_ref, u_ref, a_ref):
        hv = h_ref[...]
        g_all = _dot(hv, wg_ref[...], ((1,), (1,)))
        u_all = _dot(hv, wu_ref[...], ((1,), (1,)))
        for r in range(tm // EPI):
            rows = slice(r * EPI, (r + 1) * EPI)
            g, u = g_all[rows], u_all[rows]
            g_ref[rows, :] = g.astype(BF16)
            u_ref[rows, :] = u.astype(BF16)
            a_ref[rows, :] = (_silu_parts(g)[1] * u).astype(BF16)

    blk = pl.BlockSpec((tm, tn), lambda i, j: (i, j))
    return pl.pallas_call(
        body, name="ffn_in", grid=(t // tm, nj),
        in_specs=[pl.BlockSpec((tm, k), lambda i, j: (i, 0)), pl.BlockSpec((tn, k), lambda i, j: (j, 0)),
                  pl.BlockSpec((tn, k), lambda i, j: (j + nj, 0))],
        out_specs=[blk] * 3, out_shape=[jax.ShapeDtypeStruct((t, D_FF), BF16)] * 3,
        compiler_params=_cp("parallel", "parallel"),
    )(h2, wt, wt)


def _d_act(df, w, g, u):
    t, k = df.shape
    tn = _pick(D_FF, (256, 128))
    tm = _stream_rows(t, 4 * k * tn, 4 * k + (4 * 2 * 2 + 4) * tn)

    def body(df_ref, w_ref, g_ref, u_ref, dg_ref, du_ref):
        da_all = _dot(df_ref[...], w_ref[...], ((1,), (1,)))
        for r in range(tm // EPI):
            rows = slice(r * EPI, (r + 1) * EPI)
            da = da_all[rows]
            gv = g_ref[rows, :].astype(F32)
            sg, silu = _silu_parts(gv)
            dg_ref[rows, :] = ((da * u_ref[rows, :].astype(F32)) * (sg + silu * (1.0 - sg))).astype(BF16)
            du_ref[rows, :] = (da * silu).astype(BF16)

    blk = pl.BlockSpec((tm, tn), lambda i, j: (i, j))
    return pl.pallas_call(
        body, name="d_act", grid=(t // tm, D_FF // tn),
        in_specs=[pl.BlockSpec((tm, k), lambda i, j: (i, 0)), pl.BlockSpec((tn, k), lambda i, j: (j, 0)), blk, blk],
        out_specs=[blk] * 2, out_shape=[jax.ShapeDtypeStruct((t, D_FF), BF16)] * 2,
        compiler_params=_cp("parallel", "parallel"),
    )(df, w, g, u)


def _d_h2(dg, du, wt):
    t = dg.shape[0]
    n = wt.shape[1]
    tm, tn = _pick_rows(t, D_FF), _pick(n, (512, 256, 128))

    def body(dg_ref, du_ref, wg_ref, wu_ref, o_ref):
        o_ref[...] = (_dot(dg_ref[...], wg_ref[...], ((1,), (0,)))
                      + _dot(du_ref[...], wu_ref[...], ((1,), (0,)))).astype(BF16)

    return pl.pallas_call(
        body, name="d_h2", grid=(t // tm, n // tn),
        in_specs=[pl.BlockSpec((tm, D_FF), lambda i, j: (i, 0)), pl.BlockSpec((tm, D_FF), lambda i, j: (i, 0)),
                  pl.BlockSpec((D_FF, tn), lambda i, j: (0, j)), pl.BlockSpec((D_FF, tn), lambda i, j: (1, j))],
        out_specs=pl.BlockSpec((tm, tn), lambda i, j: (i, j)),
        out_shape=jax.ShapeDtypeStruct((t, n), BF16),
        compiler_params=_cp("parallel", "parallel"),
    )(dg, du, wt, wt)


TM = 256


def _row(w):
    return pl.BlockSpec((TM, w), lambda i: (i, 0))


def _row_at(w, col):
    return pl.BlockSpec((TM, w), lambda i: (i, col))


def _vec(w):
    return pl.BlockSpec((1, w), lambda i: (0, 0))


def _per_ex(w):
    return pl.BlockSpec((1, 1, w), lambda i: (i // (S // TM), 0, 0))


def _pos(w):
    return pl.BlockSpec((TM, w), lambda i: (i % (S // TM), 0))


def _full(shape):
    return pl.BlockSpec(shape, lambda i: (0,) * len(shape))


def _rms(x):
    return lax.rsqrt(jnp.mean(x * x, axis=-1, keepdims=True) + EPS)


def _rms_bwd(n, r, dn):
    return r * (dn - n * jnp.mean(dn * n, axis=-1, keepdims=True))


def _colsum(v):
    return jnp.sum(v, axis=0, keepdims=True)


def _acc_first(i, ref, val, every=None):
    first = (i == 0) if every is None else (i % every == 0)

    @pl.when(first)
    def _():
        ref[...] = jnp.zeros_like(ref)

    ref[...] += val.reshape(ref.shape)


def _pre1(x, g, sc, sh):
    t = x.shape[0]

    def body(x_ref, g_ref, sc_ref, sh_ref, h_ref):
        xv = x_ref[...]
        n = xv * _rms(xv)
        h_ref[...] = ((n * g_ref[...]) * (1.0 + sc_ref[0]) + sh_ref[0]).astype(BF16)

    return pl.pallas_call(
        body, name="pre1", grid=(t // TM,),
        in_specs=[_row(D), _vec(D), _per_ex(D), _per_ex(D)],
        out_specs=_row(D), out_shape=jax.ShapeDtypeStruct((t, D), BF16),
        compiler_params=_cp("parallel"),
    )(x, g, sc, sh)


def _rope_fwd(v, c, sm, sp):
    return v * c + pltpu.roll(v, HP - ROPE // 2, 1) * sm + pltpu.roll(v, ROPE // 2, 1) * sp


def _rope_bwd(dv, c, sm, sp):
    return dv * c + pltpu.roll(dv * sm, ROPE // 2, 1) + pltpu.roll(dv * sp, HP - ROPE // 2, 1)


def _mla_pre(proj, g_cq, g_ckv, w_uq, w_k, w_v, rc, rsm, rsp):
    t = proj.shape[0]

    def body(tail_ref, gq_ref, gkv_ref, wuq_ref, wk_ref, wv_ref, c_ref, sm_ref, sp_ref,
             q_ref, k_ref, v_ref, cqn_ref, ckvn_ref):
        tail = tail_ref[...]
        cq, ckv, kr = tail[:, :Q_LORA], tail[:, Q_LORA:Q_LORA + KV_LORA], tail[:, Q_LORA + KV_LORA:]
        cqn = (cq * _rms(cq) * gq_ref[...]).astype(BF16)
        ckvn = (ckv * _rms(ckv) * gkv_ref[...]).astype(BF16)
        cqn_ref[...] = cqn
        ckvn_ref[...] = ckvn
        c, sm, sp = c_ref[...], sm_ref[...], sp_ref[...]
        q = _dot(cqn, wuq_ref[...], ((1,), (0,)))
        kn = _dot(ckvn, wk_ref[...], ((1,), (0,)))
        v_ref[...] = _dot(ckvn, wv_ref[...], ((1,), (0,))).astype(BF16)
        krr = _rope_fwd(kr, c, sm, sp)
        for h in range(H):
            sl = slice(h * HP, (h + 1) * HP)
            q_ref[:, sl] = _rope_fwd(q[:, sl], c, sm, sp).astype(BF16)
            k_ref[:, sl] = (kn[:, sl] + krr).astype(BF16)

    wide = H * HP
    return pl.pallas_call(
        body, name="mla_pre", grid=(t // TM,),
        in_specs=[_row_at(TAIL, TAIL0 // TAIL), _vec(Q_LORA), _vec(KV_LORA), _full((Q_LORA, wide)),
                  _full((KV_LORA, wide)), _full((KV_LORA, wide)), _pos(HP), _pos(HP), _pos(HP)],
        out_specs=[_row(wide), _row(wide), _row(wide), _row(Q_LORA), _row(KV_LORA)],
        out_shape=[jax.ShapeDtypeStruct((t, wide), BF16)] * 3
        + [jax.ShapeDtypeStruct((t, Q_LORA), BF16), jax.ShapeDtypeStruct((t, KV_LORA), BF16)],
        compiler_params=_cp("parallel"),
    )(proj, g_cq, g_ckv, w_uq, w_k, w_v, rc, rsm, rsp)


def _mla_pre_bwd(proj, dq_, dk_, dv_, dqkv_a, g_cq, g_ckv, w_uq, w_k, w_v, rc, rsm, rsp):
    t = proj.shape[0]
    wide = H * HP

    def body(tail_ref, dq_ref, dk_ref, dv_ref, dqa_ref, dka_ref, dva_ref, gq_ref, gkv_ref, wuq_ref, wk_ref, wv_ref,
             c_ref, sm_ref, sp_ref, dqo_ref, dproj_ref, dgq_ref, dgkv_ref):
        i = pl.program_id(0)
        for n, src in enumerate((dqa_ref, dka_ref, dva_ref)):
            dproj_ref[:, n * D_A:(n + 1) * D_A] = src[...]
        dtail_ref = dproj_ref.at[:, TAIL0:]
        tail = tail_ref[...]
        cq, ckv = tail[:, :Q_LORA], tail[:, Q_LORA:Q_LORA + KV_LORA]
        c, sm, sp = c_ref[...], sm_ref[...], sp_ref[...]
        dkr = jnp.zeros((TM, HP), F32)
        for h in range(H):
            sl = slice(h * HP, (h + 1) * HP)
            dqo_ref[:, sl] = _rope_bwd(dq_ref[:, sl].astype(F32), c, sm, sp).astype(BF16)
            dkr = dkr + dk_ref[:, sl].astype(F32)
        lane = lax.broadcasted_iota(jnp.int32, (TM, HP), 1)
        dkr = jnp.where((lane >= NOPE) & (lane < NOPE + ROPE), _rope_bwd(dkr, c, sm, sp), 0.0)
        dkb = dk_ref[...]
        dvb = dv_ref[...]
        dcqn = _dot(dqo_ref[...], wuq_ref[...], ((1,), (1,)))
        dckvn = _dot(dkb, wk_ref[...], ((1,), (1,))) + _dot(dvb, wv_ref[...], ((1,), (1,)))
        rq, rkv = _rms(cq), _rms(ckv)
        nq, nkv = cq * rq, ckv * rkv
        _acc_first(i, dgq_ref, _colsum(dcqn * nq))
        _acc_first(i, dgkv_ref, _colsum(dckvn * nkv))
        dtail_ref[:, :Q_LORA] = _rms_bwd(nq, rq, dcqn * gq_ref[...]).astype(BF16)
        dtail_ref[:, Q_LORA:Q_LORA + KV_LORA] = _rms_bwd(nkv, rkv, dckvn * gkv_ref[...]).astype(BF16)
        dtail_ref[:, Q_LORA + KV_LORA:] = dkr.astype(BF16)

    return pl.pallas_call(
        body, name="mla_pre_bwd", grid=(t // TM,),
        in_specs=[_row_at(TAIL, TAIL0 // TAIL), _row(wide), _row(wide), _row(wide), _row(D_A), _row(D_A), _row(D_A),
                  _vec(Q_LORA), _vec(KV_LORA), _full((Q_LORA, wide)), _full((KV_LORA, wide)), _full((KV_LORA, wide)),
                  _pos(HP), _pos(HP), _pos(HP)],
        out_specs=[_row(wide), _row(P_PAD), _vec(Q_LORA), _vec(KV_LORA)],
        out_shape=[jax.ShapeDtypeStruct((t, wide), BF16), jax.ShapeDtypeStruct((t, P_PAD), BF16),
                   jax.ShapeDtypeStruct((1, Q_LORA), F32), jax.ShapeDtypeStruct((1, KV_LORA), F32)],
        compiler_params=_cp("arbitrary"),
    )(proj, dq_, dk_, dv_, *dqkv_a, g_cq, g_ckv, w_uq, w_k, w_v, rc, rsm, rsp)


def _post_attn(out_a, out_b, g_a, g_b):
    t = out_a.shape[0]

    def body(a_ref, b_ref, ga_ref, gb_ref, y_ref):
        a, b = a_ref[...], b_ref[...]
        y_ref[:, :D_A] = (a * _rms(a) * ga_ref[...]).astype(BF16)
        y_ref[:, D_A:] = (b * _rms(b) * gb_ref[...]).astype(BF16)

    return pl.pallas_call(
        body, name="post_attn", grid=(t // TM,),
        in_specs=[_row(D_A), _row(D_A), _vec(D_A), _vec(D_A)],
        out_specs=_row(D), out_shape=jax.ShapeDtypeStruct((t, D), BF16),
        compiler_params=_cp("parallel"),
    )(out_a, out_b, g_a, g_b)


def _post_attn_bwd(dy, out_a, out_b, g_a, g_b):
    t = dy.shape[0]

    def body(dy_ref, a_ref, b_ref, ga_ref, gb_ref, da_ref, db_ref, dga_ref, dgb_ref):
        i = pl.program_id(0)
        dy_ = dy_ref[...].astype(F32)
        for src, g_ref, dst, dg_ref, sl in ((a_ref, ga_ref, da_ref, dga_ref, slice(0, D_A)),
                                            (b_ref, gb_ref, db_ref, dgb_ref, slice(D_A, D))):
            v = src[...]
            r = _rms(v)
            n = v * r
            dyv = dy_[:, sl]
            _acc_first(i, dg_ref, _colsum(dyv * n))
            dst[...] = _rms_bwd(n, r, dyv * g_ref[...])

    return pl.pallas_call(
        body, name="post_attn_bwd", grid=(t // TM,),
        in_specs=[_row(D), _row(D_A), _row(D_A), _vec(D_A), _vec(D_A)],
        out_specs=[_row(D_A), _row(D_A), _vec(D_A), _vec(D_A)],
        out_shape=[jax.ShapeDtypeStruct((t, D_A), F32)] * 2 + [jax.ShapeDtypeStruct((1, D_A), F32)] * 2,
        compiler_params=_cp("arbitrary"),
    )(dy, out_a, out_b, g_a, g_b)


def _resid_norm2(x, mix, g1, g, sc, sh):
    t = x.shape[0]

    def body(x_ref, mix_ref, g1_ref, g_ref, sc_ref, sh_ref, x2_ref, h_ref):
        x2 = x_ref[...] + g1_ref[0] * mix_ref[...]
        x2_ref[...] = x2
        n = x2 * _rms(x2)
        h_ref[...] = ((n * g_ref[...]) * (1.0 + sc_ref[0]) + sh_ref[0]).astype(BF16)

    return pl.pallas_call(
        body, name="resid_norm2", grid=(t // TM,),
        in_specs=[_row(D), _row(D), _per_ex(D), _vec(D), _per_ex(D), _per_ex(D)],
        out_specs=[_row(D), _row(D)],
        out_shape=[jax.ShapeDtypeStruct((t, D), F32), jax.ShapeDtypeStruct((t, D), BF16)],
        compiler_params=_cp("parallel"),
    )(x, mix, g1, g, sc, sh)


def _sigmoid(v):
    return 1.0 / (1.0 + jnp.exp(-v))


def _final(x2, f, g2, g_fin, target):
    t = x2.shape[0]
    nb = t // S
    tpb = S // TM

    def body(x2_ref, f_ref, g2_ref, g_ref, t_ref, dx3_ref, df_ref, loss_ref, dgf_ref, dg2_ref):
        i = pl.program_id(0)
        fv = f_ref[...].astype(F32)
        x3 = x2_ref[...] + g2_ref[0] * fv
        r = _rms(x3)
        n = x3 * r
        err = n * g_ref[...] - t_ref[...]
        _acc_first(i, loss_ref, _colsum(err * err))
        dy = err * (1.0 / D)
        _acc_first(i, dgf_ref, _colsum(dy * n))
        dx3 = _rms_bwd(n, r, dy * g_ref[...])
        dx3_ref[...] = dx3
        _acc_first(i, dg2_ref, _colsum(dx3 * fv), every=tpb)
        df_ref[...] = (dx3 * g2_ref[0]).astype(BF16)

    return pl.pallas_call(
        body, name="final", grid=(t // TM,),
        in_specs=[_row(D), _row(D), _per_ex(D), _vec(D), _row(D)],
        out_specs=[_row(D), _row(D), _vec(D), _vec(D), _per_ex(D)],
        out_shape=[jax.ShapeDtypeStruct((t, D), F32), jax.ShapeDtypeStruct((t, D), BF16),
                   jax.ShapeDtypeStruct((1, D), F32), jax.ShapeDtypeStruct((1, D), F32),
                   jax.ShapeDtypeStruct((nb, 1, D), F32)],
        compiler_params=_cp("arbitrary"),
    )(x2, f, g2, g_fin, target)


def _norm_bwd(xin, dh, dres, g, sc, gate=None):
    t = xin.shape[0]
    nb = t // S
    tpb = S // TM
    gated = gate is not None

    def body(*refs):
        if gated:
            x_ref, dh_ref, dres_ref, g_ref, sc_ref, mix_ref, g1_ref, dx_ref, dsh_ref, dsc_ref, dg_ref, dg1_ref, dmix_ref = refs
        else:
            x_ref, dh_ref, dres_ref, g_ref, sc_ref, dx_ref, dsh_ref, dsc_ref, dg_ref = refs
        i = pl.program_id(0)
        xv, dhv = x_ref[...], dh_ref[...].astype(F32)
        r = _rms(xv)
        n = xv * r
        gv = g_ref[...]
        _acc_first(i, dsh_ref, _colsum(dhv), every=tpb)
        _acc_first(i, dsc_ref, _colsum(dhv * (n * gv)), every=tpb)
        dng = dhv * (1.0 + sc_ref[0])
        _acc_first(i, dg_ref, _colsum(dng * n))
        dx = dres_ref[...] + _rms_bwd(n, r, dng * gv)
        dx_ref[...] = dx
        if gated:
            _acc_first(i, dg1_ref, _colsum(dx * mix_ref[...].astype(F32)), every=tpb)
            dmix_ref[...] = (dx * g1_ref[0]).astype(BF16)

    in_specs = [_row(D), _row(D), _row(D), _vec(D), _per_ex(D)]
    out_specs = [_row(D), _per_ex(D), _per_ex(D), _vec(D)]
    out_shape = [jax.ShapeDtypeStruct((t, D), F32), jax.ShapeDtypeStruct((nb, 1, D), F32),
                 jax.ShapeDtypeStruct((nb, 1, D), F32), jax.ShapeDtypeStruct((1, D), F32)]
    args = [xin, dh, dres, g, sc]
    if gated:
        in_specs += [_row(D), _per_ex(D)]
        out_specs += [_per_ex(D), _row(D)]
        out_shape += [jax.ShapeDtypeStruct((nb, 1, D), F32), jax.ShapeDtypeStruct((t, D), BF16)]
        args += list(gate)
    return pl.pallas_call(
        body, name="norm2_bwd" if gated else "norm1_bwd", grid=(t // TM,),
        in_specs=in_specs, out_specs=out_specs, out_shape=out_shape,
        compiler_params=_cp("arbitrary"),
    )(*args)


TQ = 256
TB = 512


def _mla_fwd(q, k, v):
    t = q.shape[0]
    nb = t // S

    def body(q_ref, k_ref, v_ref, o_ref, lse_ref):
        causal = lax.broadcasted_iota(jnp.int32, (TB, TB), 0) >= lax.broadcasted_iota(jnp.int32, (TB, TB), 1)
        heads = [slice(h * HP, (h + 1) * HP) for h in range(2)]
        for i in range(S // TB):
            ri, past = slice(i * TB, (i + 1) * TB), slice(0, i * TB)
            qhs = [q_ref[ri, sl] for sl in heads]
            sd = [jnp.where(causal, _dot(qh, k_ref[ri, sl], ((1,), (1,))) * SCALE_B, NEG) for qh, sl in zip(qhs, heads)]
            ms = [jnp.max(s, axis=-1, keepdims=True) for s in sd]
            if i:
                so = [_dot(qh, k_ref[past, sl], ((1,), (1,))) * SCALE_B for qh, sl in zip(qhs, heads)]
                ms = [jnp.maximum(m, jnp.max(s, axis=-1, keepdims=True)) for m, s in zip(ms, so)]
            pd = [jnp.exp(s - m) for s, m in zip(sd, ms)]
            ls = [jnp.sum(p, axis=-1, keepdims=True) for p in pd]
            acc = [_dot(p.astype(BF16), v_ref[ri, sl], ((1,), (0,))) for p, sl in zip(pd, heads)]
            if i:
                po = [jnp.exp(s - m) for s, m in zip(so, ms)]
                ls = [l + jnp.sum(p, axis=-1, keepdims=True) for l, p in zip(ls, po)]
                acc = [a + _dot(p.astype(BF16), v_ref[past, sl], ((1,), (0,))) for a, p, sl in zip(acc, po, heads)]
            o_ref[ri, :] = acc[0] / ls[0] + acc[1] / ls[1]
            for sl, m, l in zip(heads, ms, ls):
                lse_ref[ri, sl] = jnp.broadcast_to(m + jnp.log(l), (TB, HP))

    wide2 = pl.BlockSpec((S, 2 * HP), lambda b, p: (b, p))
    return pl.pallas_call(
        body, name="mla_fwd", grid=(nb, H // 2),
        in_specs=[wide2, wide2, wide2],
        out_specs=[pl.BlockSpec((S, HP), lambda b, p: (b, p)), wide2],
        out_shape=[jax.ShapeDtypeStruct((t, H * VDIM), F32), jax.ShapeDtypeStruct((t, H * HP), F32)],
        compiler_params=_cp("parallel", "parallel"),
    )(q, k, v)


def _mla_bwd(q, k, v, o, do, lse):
    t = q.shape[0]
    nb = t // S

    def body(q_ref, k_ref, v_ref, o_ref, do_ref, lse_ref, dq_out, dk_out, dv_out, dq_ref, dk_ref, dv_ref):
        lane = lax.broadcasted_iota(jnp.int32, (TB, HP), 1)
        causal = lax.broadcasted_iota(jnp.int32, (TB, TB), 0) >= lax.broadcasted_iota(jnp.int32, (TB, TB), 1)
        heads = [slice(h * HP, (h + 1) * HP) for h in range(2)]
        nblk = S // TB
        for i in reversed(range(nblk)):
            ri, past = slice(i * TB, (i + 1) * TB), slice(0, i * TB)
            dov = do_ref[ri, :]
            prod = dov * o_ref[ri, :]
            dob = dov.astype(BF16)
            deltas = [jnp.sum(jnp.where((lane < VDIM) if h == 0 else (lane >= VDIM), prod, 0.0), axis=-1, keepdims=True)
                      for h in range(2)]
            qhs = [q_ref[ri, sl] for sl in heads]
            lses = [lse_ref[ri, sl][:, :1] for sl in heads]
            for rows, diagonal in ((ri, True), (past, False)):
                if rows.stop == rows.start:
                    continue
                ps = [jnp.exp(_dot(qh, k_ref[rows, sl], ((1,), (1,))) * SCALE_B - lse) for qh, sl, lse in zip(qhs, heads, lses)]
                if diagonal:
                    ps = [jnp.where(causal, p, 0.0) for p in ps]
                dps = [_dot(dob, v_ref[rows, sl], ((1,), (1,))) for sl in heads]
                dss = [(p * (dp - delta) * SCALE_B).astype(BF16) for p, dp, delta in zip(ps, dps, deltas)]
                for sl, qh, p, ds in zip(heads, qhs, ps, dss):
                    dq = _dot(ds, k_ref[rows, sl], ((1,), (0,)))
                    dk = _dot(ds, qh, ((0,), (0,)))
                    dv = _dot(p.astype(BF16), dob, ((0,), (0,)))
                    if diagonal:
                        dq_ref[ri, sl] = dq
                    else:
                        dq_ref[ri, sl] += dq
                    if i == nblk - 1:
                        dk_ref[rows, sl] = dk
                        dv_ref[rows, sl] = dv
                    else:
                        dk_ref[rows, sl] += dk
                        dv_ref[rows, sl] += dv
        dq_out[...] = dq_ref[...].astype(BF16)
        dk_out[...] = dk_ref[...].astype(BF16)
        dv_out[...] = dv_ref[...].astype(BF16)

    wide2 = pl.BlockSpec((S, 2 * HP), lambda b, p: (b, p))
    pair = pl.BlockSpec((S, HP), lambda b, p: (b, p))
    return pl.pallas_call(
        body, name="mla_bwd", grid=(nb, H // 2),
        in_specs=[wide2, wide2, wide2, pair, pair, wide2],
        out_specs=[wide2, wide2, wide2],
        out_shape=[jax.ShapeDtypeStruct((t, H * HP), BF16)] * 3,
        scratch_shapes=[pltpu.VMEM((S, 2 * HP), F32)] * 3,
        compiler_params=_cp("parallel", "parallel"),
    )(q, k, v, o, do, lse)


def _t5_bucket(dist):
    max_exact = N_BUCKETS // 2
    d = np.maximum(dist, 1).astype(np.float64)
    large = max_exact + (np.log(d / max_exact) / np.log(MAX_DISTANCE / max_exact) * (N_BUCKETS - max_exact)).astype(np.int64)
    large = np.minimum(large, N_BUCKETS - 1)
    return np.where(dist < max_exact, dist, large).astype(np.int32)


def _band_geometry():
    a = np.arange(BLK)[:, None]
    bk = np.arange(2 * BLK)[None, :]
    steps = BLK + a - bk
    valid = (steps >= 0) & (steps <= BLK)
    buckets = np.stack([_t5_bucket(np.clip(steps, 0, BLK) * d) for d in DILATIONS])
    return buckets, valid


def _band_bias(rel_bias):
    buckets, valid = _band_geometry()
    onehot = (jnp.asarray(buckets)[..., None] == jnp.arange(N_BUCKETS)).astype(F32)
    bias = jnp.einsum("rqkn,nh->rhqk", onehot, rel_bias, precision=lax.Precision.HIGHEST)
    bias = jnp.where(jnp.asarray(valid)[None, None], bias, NEG)
    return bias.reshape(3, H // 2, 2 * BLK, 2 * BLK)


def _dil_items():
    items = []
    for r, d in enumerate(DILATIONS):
        for res in range(d):
            for blk in range(S // d // BLK):
                items.append((r, d, blk * BLK * d + res, blk > 0))
    return items


GROUP = 4


def _strided(start, d):
    return pl.ds(start, BLK) if d == 1 else pl.ds(start, BLK, stride=d)


def _stack_heads(tile, own):
    return jnp.where(own, jnp.concatenate([tile, tile], axis=0), 0.0).astype(BF16)


def _own_lanes():
    row = lax.broadcasted_iota(jnp.int32, (2 * BLK, HP), 0)
    lane = lax.broadcasted_iota(jnp.int32, (2 * BLK, HP), 1)
    return (lane < E_A) == (row < BLK)


def _dil_fwd(proj, biasm):
    t = proj.shape[0]
    nb = t // S

    def body(q_ref, k_ref, v_ref, b_ref, o_ref, lse_ref, ob_ref, lb_ref):
        lane = lax.broadcasted_iota(jnp.int32, (BLK, HP), 1)
        own = _own_lanes()
        items = _dil_items()
        for g in range(0, len(items), GROUP):
            grp = items[g:g + GROUP]
            ss, vts = [], []
            for r, d, start, has_prev in grp:
                cur = _strided(start, d)
                rows = [_strided(start - BLK * d, d), cur] if has_prev else [cur]
                q2 = _stack_heads(q_ref[cur, :] * SCALE_A, own)
                kt = jnp.concatenate([k_ref[x, :] for x in rows], axis=0).astype(BF16)
                vts.append(jnp.concatenate([v_ref[x, :] for x in rows], axis=0).astype(BF16))
                bias = b_ref[r, 0] if has_prev else b_ref[r, 0, :, BLK:]
                ss.append(_dot(q2, kt, ((1,), (1,))) + bias)
            ms = [jnp.max(s, axis=-1, keepdims=True) for s in ss]
            ps = [jnp.exp(s - m) for s, m in zip(ss, ms)]
            ls = [jnp.sum(p, axis=-1, keepdims=True) for p in ps]
            for (r, d, start, _), p, vt, m, l in zip(grp, ps, vts, ms, ls):
                cur = _strided(start, d)
                o2 = _dot(p.astype(BF16), vt, ((1,), (0,))) / l
                lse2 = m + jnp.log(l)
                ob_ref[r, cur, :] = jnp.where(lane < E_A, o2[:BLK], o2[BLK:])
                lb_ref[r, cur, :] = jnp.where(lane < E_A, lse2[:BLK], lse2[BLK:])

        def merge(c, _):
            rows = pl.ds(pl.multiple_of(c * TQ, TQ), TQ)
            l0, l1, l2 = lb_ref[0, rows, :], lb_ref[1, rows, :], lb_ref[2, rows, :]
            m = jnp.maximum(jnp.maximum(l0, l1), l2)
            e0, e1, e2 = jnp.exp(l0 - m), jnp.exp(l1 - m), jnp.exp(l2 - m)
            tot = e0 + e1 + e2
            o_ref[rows, :] = (e0 * ob_ref[0, rows, :] + e1 * ob_ref[1, rows, :] + e2 * ob_ref[2, rows, :]) / tot
            lse_ref[rows, :] = m + jnp.log(tot)
            return 0

        lax.fori_loop(0, S // TQ, merge, 0)

    npair = H // 2
    return pl.pallas_call(
        body, name="dil_fwd", grid=(nb, npair),
        in_specs=[pl.BlockSpec((S, HP), lambda b, p: (b, p)), pl.BlockSpec((S, HP), lambda b, p: (b, npair + p)),
                  pl.BlockSpec((S, HP), lambda b, p: (b, 2 * npair + p)),
                  pl.BlockSpec((3, 1, 2 * BLK, 2 * BLK), lambda b, p: (0, p, 0, 0))],
        out_specs=[pl.BlockSpec((S, HP), lambda b, p: (b, p))] * 2,
        out_shape=[jax.ShapeDtypeStruct((t, D_A), F32)] * 2,
        scratch_shapes=[pltpu.VMEM((3, S, HP), F32), pltpu.VMEM((3, S, HP), F32)],
        compiler_params=_cp("parallel", "parallel"),
    )(proj, proj, proj, biasm)


def _dil_bwd(proj, biasm, o, do, lse):
    t = proj.shape[0]
    nb = t // S

    def body(q_ref, k_ref, v_ref, b_ref, o_ref, do_ref, lse_ref, dq_out, dk_out, dv_out, ds_ref, dq_ref, dk_ref, dv_ref):
        dq_ref[...] = jnp.zeros_like(dq_ref)
        dk_ref[...] = jnp.zeros_like(dk_ref)
        dv_ref[...] = jnp.zeros_like(dv_ref)
        ds_ref[...] = jnp.zeros_like(ds_ref)
        lane = lax.broadcasted_iota(jnp.int32, (BLK, HP), 1)
        own = _own_lanes()
        items = _dil_items()
        for g in range(0, len(items), GROUP):
            grp = items[g:g + GROUP]
            q2s, kts, do2s, ss, dps, lse2s, delta2s = [], [], [], [], [], [], []
            for r, d, start, has_prev in grp:
                cur = _strided(start, d)
                rows = [_strided(start - BLK * d, d), cur] if has_prev else [cur]
                q2 = _stack_heads(q_ref[cur, :] * SCALE_A, own)
                kt = jnp.concatenate([k_ref[x, :] for x in rows], axis=0).astype(BF16)
                vt = jnp.concatenate([v_ref[x, :] for x in rows], axis=0).astype(BF16)
                dot_ = do_ref[cur, :]
                prod = dot_ * o_ref[cur, :]
                lset = lse_ref[cur, :]
                do2 = _stack_heads(dot_, own)
                bias = b_ref[r, 0] if has_prev else b_ref[r, 0, :, BLK:]
                ss.append(_dot(q2, kt, ((1,), (1,))) + bias)
                dps.append(_dot(do2, vt, ((1,), (1,))))
                lse2s.append(jnp.concatenate([lset[:, :1], lset[:, E_A:E_A + 1]], axis=0))
                delta2s.append(jnp.concatenate([jnp.sum(jnp.where(lane < E_A, prod, 0.0), axis=-1, keepdims=True),
                                                jnp.sum(jnp.where(lane >= E_A, prod, 0.0), axis=-1, keepdims=True)], axis=0))
                q2s.append(q2)
                kts.append(kt)
                do2s.append(do2)
            ps = [jnp.exp(s - lse2) for s, lse2 in zip(ss, lse2s)]
            dls = [p * (dp - delta2) for p, dp, delta2 in zip(ps, dps, delta2s)]
            for (r, d, start, has_prev), q2, kt, do2, p, dl in zip(grp, q2s, kts, do2s, ps, dls):
                cur = _strided(start, d)
                dsb = dl.astype(BF16)
                dq2 = _dot(dsb, kt, ((1,), (0,))) * SCALE_A
                dkt = _dot(dsb, q2, ((0,), (0,)))
                dvt = _dot(p.astype(BF16), do2, ((0,), (0,)))
                dq_ref[cur, :] += jnp.where(lane < E_A, dq2[:BLK], dq2[BLK:])
                if has_prev:
                    prev = _strided(start - BLK * d, d)
                    ds_ref[0, r, 0] += dl
                    dk_ref[prev, :] += dkt[:BLK]
                    dv_ref[prev, :] += dvt[:BLK]
                    dk_ref[cur, :] += dkt[BLK:]
                    dv_ref[cur, :] += dvt[BLK:]
                else:
                    ds_ref[0, r, 0, :, BLK:] += dl
                    dk_ref[cur, :] += dkt
                    dv_ref[cur, :] += dvt
        dq_out[...] = dq_ref[...].astype(BF16)
        dk_out[...] = dk_ref[...].astype(BF16)
        dv_out[...] = dv_ref[...].astype(BF16)

    npair = H // 2
    pair = pl.BlockSpec((S, HP), lambda b, p: (b, p))
    return pl.pallas_call(
        body, name="dil_bwd", grid=(nb, npair),
        in_specs=[pair, pl.BlockSpec((S, HP), lambda b, p: (b, npair + p)),
                  pl.BlockSpec((S, HP), lambda b, p: (b, 2 * npair + p)),
                  pl.BlockSpec((3, 1, 2 * BLK, 2 * BLK), lambda b, p: (0, p, 0, 0)), pair, pair, pair],
        out_specs=[pair, pair, pair, pl.BlockSpec((1, 3, 1, 2 * BLK, 2 * BLK), lambda b, p: (b, 0, p, 0, 0))],
        out_shape=[jax.ShapeDtypeStruct((t, D_A), BF16)] * 3 + [jax.ShapeDtypeStruct((nb, 3, npair, 2 * BLK, 2 * BLK), F32)],
        scratch_shapes=[pltpu.VMEM((S, HP), F32)] * 3,
        compiler_params=_cp("parallel", "parallel"),
    )(proj, proj, proj, biasm, o, do, lse)


def _rel_bias_grad(dlogits):
    nb = dlogits.shape[0]
    buckets, _ = _band_geometry()
    kk = 3 * BLK * 2 * BLK
    dl = jnp.transpose(dlogits.reshape(nb, 3, H, BLK, 2 * BLK), (0, 2, 1, 3, 4)).reshape(nb, H, kk)
    bk = jnp.asarray(buckets.reshape(1, kk))
    tk = kk // 12

    def body(dl_ref, bk_ref, o_ref):
        j = pl.program_id(0)
        onehot = (bk_ref[...] == lax.broadcasted_iota(jnp.int32, (N_BUCKETS, tk), 0)).astype(F32)
        tot = dl_ref[0]
        for b in range(1, nb):
            tot = tot + dl_ref[b]
        part = lax.dot_general(onehot, tot, ((((1,), (1,))), ((), ())), preferred_element_type=F32,
                               precision=lax.Precision.HIGHEST)
        _acc_first(j, o_ref, part)

    return pl.pallas_call(
        body, name="rel_bias_grad", grid=(kk // tk,),
        in_specs=[pl.BlockSpec((nb, H, tk), lambda j: (0, 0, j)), pl.BlockSpec((1, tk), lambda j: (0, j))],
        out_specs=pl.BlockSpec((N_BUCKETS, H), lambda j: (0, 0)),
        out_shape=jax.ShapeDtypeStruct((N_BUCKETS, H), F32),
        compiler_params=_cp("arbitrary"),
    )(dl, bk)


def _mesh_place():
    x, y, c = lax.axis_index("x"), lax.axis_index("y"), lax.axis_index("c")
    return x, y, c


def _peer(k):
    x, y, c = _mesh_place()
    px = 1 - x if k & 4 else x
    py = 1 - y if k & 2 else y
    pc = 1 - c if k & 1 else c
    return (px, py, pc), 4 * px + 2 * py + pc


ANY = pl.BlockSpec(memory_space=pl.ANY)


def _exchange(arrays, gathers, name, after=None):
    n_arr = len(arrays)

    def body(*refs):
        ins, outs = refs[:n_arr], refs[n_arr + 1:2 * n_arr + 1]
        send, recv, loc = refs[2 * n_arr + 1:]
        x, y, c = _mesh_place()
        me = 4 * x + 2 * y + c
        local = [pltpu.make_async_copy(ins[a] if gathers[a] else ins[a].at[me], outs[a].at[me], loc.at[a])
                 for a in range(n_arr)]
        remote = _peer_copies(ins, outs, send, recv, gathers)
        for cp in local:
            cp.start()
        for put, _ in remote:
            put.start()
        for cp in local:
            cp.wait()
        for put, got in remote:
            put.wait_send()
            got.wait_recv()

    return pl.pallas_call(
        body, name=name,
        in_specs=[ANY] * (n_arr + 1), out_specs=[ANY] * n_arr,
        out_shape=[jax.ShapeDtypeStruct(((N_DEV,) if g else ()) + a.shape, a.dtype) for a, g in zip(arrays, gathers)],
        scratch_shapes=[pltpu.SemaphoreType.DMA((n_arr * (N_DEV - 1),)), pltpu.SemaphoreType.DMA((n_arr * (N_DEV - 1),)),
                        pltpu.SemaphoreType.DMA((n_arr,))],
        compiler_params=pltpu.CompilerParams(has_side_effects=True),
    )(*arrays, arrays[0] if after is None else after)


def _gather_two_level(arrays, name):
    n_arr = len(arrays)
    per = N_DEV - 1

    def body(*refs):
        ins, outs = refs[:n_arr], refs[n_arr:2 * n_arr]
        send, recv, loc = refs[2 * n_arr:]
        x, y, c = _mesh_place()
        me, sibling = (x, y, c), (x, y, 1 - c)
        chips = [(1 - x, y), (x, 1 - y), (1 - x, 1 - y)]

        def block(a, place):
            px, py, pc = place
            return outs[a].at[4 * px + 2 * py + pc]

        def copy(a, k, place, to, src=None):
            dst = block(a, place)
            return pltpu.make_async_remote_copy(dst if src is None else src, dst, send.at[a * per + k], recv.at[a * per + k],
                                                device_id=to, device_id_type=pl.DeviceIdType.MESH)

        local = [pltpu.make_async_copy(ins[a], block(a, me), loc.at[a]) for a in range(n_arr)]
        for cp in local:
            cp.start()
        first = []
        for a in range(n_arr):
            first.append(copy(a, 0, me, sibling, src=ins[a]))
            first += [copy(a, 1 + j, me, (*chip, c), src=ins[a]) for j, chip in enumerate(chips)]
        for cp in first:
            cp.start()
        passed = []
        for j, chip in enumerate(chips):
            for a in range(n_arr):
                copy(a, 1 + j, (*chip, c), me).wait_recv()
                passed.append(copy(a, 4 + j, (*chip, c), sibling))
                passed[-1].start()
        for a in range(n_arr):
            copy(a, 0, sibling, me).wait_recv()
            for j, chip in enumerate(chips):
                copy(a, 4 + j, (*chip, 1 - c), me).wait_recv()
        for cp in first + passed:
            cp.wait_send()
        for cp in local:
            cp.wait()

    return pl.pallas_call(
        body, name=name,
        in_specs=[ANY] * n_arr, out_specs=[ANY] * n_arr,
        out_shape=[jax.ShapeDtypeStruct((N_DEV,) + a.shape, a.dtype) for a in arrays],
        scratch_shapes=[pltpu.SemaphoreType.DMA((n_arr * per,)), pltpu.SemaphoreType.DMA((n_arr * per,)),
                        pltpu.SemaphoreType.DMA((n_arr,))],
        compiler_params=pltpu.CompilerParams(has_side_effects=True),
    )(*arrays)


HBM = pl.BlockSpec(memory_space=pltpu.HBM)
SEM = pl.BlockSpec(memory_space=pltpu.SEMAPHORE)
DATAFLOW = pltpu.SideEffectType.DATAFLOW_SIDE_EFFECTING


def _own_block_in_place(block, me):
    land = lax.empty((N_DEV,) + block.shape, block.dtype)
    return lax.dynamic_update_slice(land, block[None], (me,) + (0,) * block.ndim)


def _peer_copies(srcs, lands, send, recv, gathers):
    x, y, c = _mesh_place()
    me = 4 * x + 2 * y + c
    out = []
    for a, (src, land) in enumerate(zip(srcs, lands)):
        for k in range(1, N_DEV):
            dev, idx = _peer(k)
            sem = a * (N_DEV - 1) + k - 1
            mine = src if gathers[a] else src.at[idx]
            put = pltpu.make_async_remote_copy(mine, land.at[me], send.at[sem], recv.at[sem],
                                               device_id=dev, device_id_type=pl.DeviceIdType.MESH)
            got = pltpu.make_async_remote_copy(mine, land.at[idx], send.at[sem], recv.at[sem],
                                               device_id=dev, device_id_type=pl.DeviceIdType.MESH)
            out.append((put, got))
    return out


def _exchange_start(srcs, lands, gather, after, name):
    n = len(srcs)

    def body(*refs):
        srcs_, lands_, send, recv = refs[:n], refs[n:2 * n], refs[2 * n + 1], refs[2 * n + 2]
        for put, _ in _peer_copies(srcs_, lands_, send, recv, gather):
            put.start()
        refs[-1][...] = jnp.zeros_like(refs[-1])

    nsem = n * (N_DEV - 1)
    thru = [pltpu.HBM(a.shape, a.dtype) for a in list(srcs) + list(lands)]
    res = pl.pallas_call(
        body, name=name,
        out_shape=(pltpu.SemaphoreType.DMA((nsem,)), pltpu.SemaphoreType.DMA((nsem,)), *thru, jax.ShapeDtypeStruct((8, 128), F32)),
        in_specs=[HBM] * (2 * n) + [ANY],
        out_specs=(SEM, SEM, *([HBM] * (2 * n)), pl.BlockSpec(memory_space=pltpu.VMEM)),
        input_output_aliases={i: 2 + i for i in range(2 * n)},
        compiler_params=pltpu.CompilerParams(has_side_effects=DATAFLOW),
    )(*[pltpu.with_memory_space_constraint(a, pltpu.HBM) for a in list(srcs) + list(lands)], after)
    return res[0], res[1], list(res[2:2 + n]), list(res[2 + n:2 + 2 * n]), res[-1]


def _exchange_wait(send, recv, srcs, lands, gather, after, name):
    n = len(srcs)

    def body(*refs):
        srcs_, lands_, send_, recv_ = refs[:n], refs[n:2 * n], refs[2 * n], refs[2 * n + 1]
        for put, got in _peer_copies(srcs_, lands_, send_, recv_, gather):
            put.wait_send()
            got.wait_recv()

    thru = [pltpu.HBM(a.shape, a.dtype) for a in list(srcs) + list(lands)]
    res = pl.pallas_call(
        body, name=name, out_shape=tuple(thru),
        in_specs=[HBM] * (2 * n) + [SEM, SEM, ANY], out_specs=tuple([HBM] * (2 * n)),
        input_output_aliases={i: i for i in range(2 * n)},
        compiler_params=pltpu.CompilerParams(has_side_effects=DATAFLOW),
    )(*srcs, *lands, send, recv, after)
    return list(res[n:])


def _silu_rows(c):
    def body(c_ref, o_ref):
        v = c_ref[...]
        o_ref[...] = v * _sigmoid(v)

    return pl.pallas_call(body, name="cond", out_shape=jax.ShapeDtypeStruct(c.shape, F32))(c)


def _mod_slab(cond_all, w_ada, b_slab):
    def body(c_ref, w_ref, b_ref, o_ref):
        o_ref[...] = _dot(c_ref[...].astype(BF16), w_ref[0].astype(BF16), ((1,), (0,))) + b_ref[...]

    return pl.pallas_call(body, name="mod_slab",
                          out_shape=jax.ShapeDtypeStruct((cond_all.shape[0], w_ada.shape[2]), F32),
                          compiler_params=pltpu.CompilerParams(vmem_limit_bytes=VMEM_LIMIT))(cond_all, w_ada, b_slab)


def _ada_grad(cond_all, dmod_cols):
    def body(c_ref, d_ref, o_ref):
        o_ref[...] = _dot(c_ref[...].astype(BF16), d_ref[...].astype(BF16), ((0,), (0,)))

    return pl.pallas_call(body, name="ada_grad",
                          out_shape=jax.ShapeDtypeStruct((cond_all.shape[1], dmod_cols.shape[1]), F32),
                          compiler_params=pltpu.CompilerParams(vmem_limit_bytes=VMEM_LIMIT))(cond_all, dmod_cols)


def _adam_math(g, w, m, v):
    m2 = B1 * m + (1.0 - B1) * g
    v2 = B2 * v + (1.0 - B2) * (g * g)
    m_hat = m2 / (1.0 - B1 ** STEP)
    v_hat = v2 / (1.0 - B2 ** STEP)
    return -LR * (m_hat / (jnp.sqrt(v_hat) + ADAM_EPS) + WD * w), m2, v2


def _adamw(parts, w, m, v, name):
    n, rows, cols = parts.shape
    tr = _pick(rows, (128, 96, 64, 32, 16, rows))

    def body(p_ref, w_ref, m_ref, v_ref, g_ref, d_ref, m2_ref, v2_ref):
        g = p_ref[0].astype(F32)
        for s in range(1, n):
            g = g + p_ref[s].astype(F32)
        g_ref[0] = g
        d_ref[0], m2_ref[0], v2_ref[0] = _adam_math(g, w_ref[0], m_ref[0], v_ref[0])

    blk = pl.BlockSpec((1, tr, cols), lambda i: (0, i, 0))
    return pl.pallas_call(
        body, name=name, grid=(rows // tr,),
        in_specs=[pl.BlockSpec((n, tr, cols), lambda i: (0, i, 0)), blk, blk, blk],
        out_specs=[blk] * 4, out_shape=[jax.ShapeDtypeStruct((1, rows, cols), F32)] * 4,
        compiler_params=_cp("parallel"),
    )(*[pltpu.with_memory_space_constraint(a, pltpu.HBM) for a in (parts, w, m, v)])


ROW_PARAMS = (("g_norm1", D), ("g_cq", Q_LORA), ("g_ckv", KV_LORA), ("g_out_a", D_A), ("g_out_b", D_A), ("g_norm2", D),
              ("g_final", D))
LOSS_ROW = N_MOD + len(ROW_PARAMS)
PAY_ROWS = 16
NCOL = N_MOD * D // N_DEV


def _pack_small(dmods, rows, loss_cols):
    nb = dmods[0].shape[0]
    nrow = len(ROW_PARAMS)

    def body(*refs):
        dm, rw, loss_ref, pay_ref, blk_ref = refs[:N_MOD], refs[N_MOD:N_MOD + nrow], refs[N_MOD + nrow], refs[-2], refs[-1]
        pay_ref[...] = jnp.zeros_like(pay_ref)
        for k in range(N_MOD):
            tot = dm[k][0]
            for b in range(1, nb):
                tot = tot + dm[k][b]
            pay_ref[k:k + 1, :] = tot
        for i, (_, n) in enumerate(ROW_PARAMS):
            pay_ref[N_MOD + i:N_MOD + i + 1, :n] = rw[i][...]
        pay_ref[LOSS_ROW:LOSS_ROW + 1, :] = loss_ref[...]
        for j in range(N_DEV):
            done = 0
            while done < NCOL:
                seg, off = divmod(j * NCOL + done, D)
                ln = min(NCOL - done, D - off)
                for b in range(nb):
                    blk_ref[j, b:b + 1, done:done + ln] = dm[seg][b][:, off:off + ln]
                done += ln

    return pl.pallas_call(
        body, name="pack_small",
        out_shape=[jax.ShapeDtypeStruct((PAY_ROWS, D), F32), jax.ShapeDtypeStruct((N_DEV, nb, NCOL), F32)],
    )(*dmods, *rows, loss_cols)


def _small_update(pay, rel, ws, ms, vs):
    n_par = len(ws)

    def body(*refs):
        pay_ref, rel_ref = refs[:2]
        w_refs, m_refs, v_refs = (refs[2 + s * n_par:2 + (s + 1) * n_par] for s in range(3))
        outs, loss_ref = refs[2 + 3 * n_par:-1], refs[-1]
        tot, rtot = pay_ref[0], rel_ref[0]
        for s in range(1, N_DEV):
            tot, rtot = tot + pay_ref[s], rtot + rel_ref[s]

        def update(p, g, sl):
            outs[4 * p][:, sl] = g
            outs[4 * p + 1][:, sl], outs[4 * p + 2][:, sl], outs[4 * p + 3][:, sl] = _adam_math(
                g, w_refs[p][:, sl], m_refs[p][:, sl], v_refs[p][:, sl])

        for k in range(N_MOD):
            update(0, tot[k:k + 1, :], slice(k * D, (k + 1) * D))
        for i, (_, n) in enumerate(ROW_PARAMS):
            update(1 + i, tot[N_MOD + i:N_MOD + i + 1, :n], slice(0, n))
        update(n_par - 1, rtot, slice(0, H))
        loss_ref[...] = jnp.broadcast_to((0.5 / D) * jnp.sum(tot[LOSS_ROW:LOSS_ROW + 1, :]), loss_ref.shape)

    shapes = [jax.ShapeDtypeStruct(w.shape, F32) for w in ws for _ in range(4)]
    res = pl.pallas_call(
        body, name="small_update", out_shape=shapes + [jax.ShapeDtypeStruct((8, 128), F32)],
    )(pay, rel, *ws, *ms, *vs)
    return [tuple(res[4 * p:4 * p + 4]) for p in range(n_par)], res[-1]


def _cols_from_blocks(g):
    return jnp.transpose(g, (1, 0, 2)).reshape(g.shape[1], N_DEV * g.shape[2])


def _cols_to_blocks(w):
    r, c = w.shape
    return jnp.transpose(w.reshape(r, N_DEV, c // N_DEV), (1, 0, 2))


def _pad_w_in(wt):
    z = jnp.zeros((NOPE, wt.shape[1]), wt.dtype)
    return jnp.concatenate([wt[:P_IN - ROPE], z, wt[P_IN - ROPE:], z[:HP - NOPE - ROPE]], axis=0)


def _unpad_w_in(gt):
    k0 = P_IN - ROPE + NOPE
    return jnp.concatenate([gt[:P_IN - ROPE], gt[k0:k0 + ROPE]], axis=0)


def _pad_w_uq(w):
    w3 = w.reshape(Q_LORA, H, NOPE + ROPE)
    return jnp.pad(w3, ((0, 0), (0, 0), (0, HP - NOPE - ROPE))).reshape(Q_LORA, H * HP)


def _unpad_w_uq(g):
    return g.reshape(Q_LORA, H, HP)[:, :, :NOPE + ROPE].reshape(Q_LORA, H * (NOPE + ROPE))


def _split_w_ukv(w):
    w4 = w.reshape(KV_LORA, H // 2, 2, HP)
    z = jnp.zeros((KV_LORA, H // 2, NOPE), w.dtype)
    kn, vv = w4[..., :NOPE], w4[..., NOPE:]
    w_k = jnp.stack([jnp.concatenate([kn[:, :, 0], z], -1), jnp.concatenate([kn[:, :, 1], z], -1)], axis=2)
    w_v = jnp.stack([jnp.concatenate([vv[:, :, 0], z], -1), jnp.concatenate([z, vv[:, :, 1]], -1)], axis=2)
    return w_k.reshape(KV_LORA, H * HP), w_v.reshape(KV_LORA, H * HP)


def _join_w_ukv(g_k, g_v):
    gk = g_k.reshape(KV_LORA, H // 2, 2, HP)
    gv = g_v.reshape(KV_LORA, H // 2, 2, HP)
    even = jnp.concatenate([gk[:, :, 0, :NOPE], gv[:, :, 0, :VDIM]], -1)
    odd = jnp.concatenate([gk[:, :, 1, :NOPE], gv[:, :, 1, VDIM:]], -1)
    return jnp.stack([even, odd], axis=2).reshape(KV_LORA, H * HP)


def _rope_tables():
    half = ROPE // 2
    inv = ROPE_THETA ** (-jnp.arange(half, dtype=F32) / half)
    ang = jnp.arange(S, dtype=F32)[:, None] * inv[None, :]
    cos, sin = jnp.cos(ang), jnp.sin(ang)
    ones, zeros = jnp.ones((S, NOPE), F32), jnp.zeros((S, NOPE), F32)
    tail1, tail0 = jnp.ones((S, HP - NOPE - ROPE), F32), jnp.zeros((S, HP - NOPE - ROPE), F32)
    zh = jnp.zeros((S, half), F32)
    c = jnp.concatenate([ones, cos, cos, tail1], axis=1)
    sm = jnp.concatenate([zeros, -sin, zh, tail0], axis=1)
    sp = jnp.concatenate([zeros, zh, sin, tail0], axis=1)
    return c, sm, sp


def _local_step(x, mod, target, g_norm1, w_in_p, g_cq, w_uq_p, g_ckv, w_k, w_v, rel_bias, g_out_a, g_out_b, w_out,
                g_norm2, w_ffn_in, w_ffn_out, g_final, late_weights=None, on_ffn_grads=None, on_last_grads=None):
    nb = x.shape[0] // S
    sh1, sc1, g1, sh2, sc2, g2 = (mod[:, n].reshape(nb, 1, D) for n in range(N_MOD))
    rc, rsm, rsp = _rope_tables()
    biasm = _band_bias(rel_bias)

    h1 = _pre1(x, g_norm1, sc1, sh1)
    proj = _mm_nt(h1, w_in_p, F32, "proj")
    q, k, v, cqn, ckvn = _mla_pre(proj, g_cq, g_ckv, w_uq_p, w_k, w_v, rc, rsm, rsp)
    out_b, lse_b = _mla_fwd(q, k, v)
    out_a, lse_a = _dil_fwd(proj, biasm)
    y = _post_attn(out_a, out_b, g_out_a, g_out_b)
    if late_weights is not None:
        w_out, w_ffn_in, w_ffn_out = late_weights(y)
    mix = _mm_nn(y, w_out, BF16, "mix")
    x2, h2 = _resid_norm2(x, mix, g1, g_norm2, sc2, sh2)
    ffn_g, ffn_u, act = _ffn_in(h2, w_ffn_in)
    f = _mm_nn(act, w_ffn_out, BF16, "ffn_out")
    dx3, df, loss_cols, dg_final, dg2 = _final(x2, f, g2, g_final, target)

    dg_, du_ = _d_act(df, w_ffn_out, ffn_g, ffn_u)
    gw_ffn_out = _mm_tn(act, [df], "gw_ffn_out")
    dh2 = _d_h2(dg_, du_, w_ffn_in)
    gw_ffn_in = _mm_tn_pair(dg_, du_, h2, "gw_ffn_in")
    dx2, dsh2, dsc2, dg_norm2, dg1, dmix = _norm_bwd(x2, dh2, dx3, g_norm2, sc2, gate=(mix, g1))
    dy = _mm_nt(dmix, w_out, BF16, "d_y")
    gw_out = _mm_tn(y, [dmix], "gw_out")
    if on_ffn_grads is not None:
        g_out_a = g_out_a + on_ffn_grads(gw_ffn_in, gw_ffn_out, gw_out)
    dout_a, dout_b, dg_out_a, dg_out_b = _post_attn_bwd(dy, out_a, out_b, g_out_a, g_out_b)
    dq_b, dk_b, dv_b = _mla_bwd(q, k, v, out_b, dout_b, lse_b)
    dq_a, dk_a, dv_a, dlogits = _dil_bwd(proj, biasm, out_a, dout_a, lse_a)
    g_rel = _rel_bias_grad(dlogits)
    dqr, dproj, dg_cq, dg_ckv = _mla_pre_bwd(proj, dq_b, dk_b, dv_b, (dq_a, dk_a, dv_a), g_cq, g_ckv, w_uq_p, w_k, w_v,
                                             rc, rsm, rsp)
    gw_uq = _mm_tn(cqn, [dqr], "gw_uq")
    gw_k, gw_v = _mm_tn(ckvn, [dk_b, dv_b], "gw_kv")
    gw_in = _mm_tn(dproj, [h1], "gw_in")
    if on_last_grads is not None:
        started = on_last_grads(dict(w_in=gw_in, w_uq=gw_uq, w_k=gw_k, w_v=gw_v))
    else:
        started = None
    dh1 = _mm_nn(dproj, w_in_p, BF16, "d_h1", after=started)
    grad_x, dsh1, dsc1, dg_norm1 = _norm_bwd(x, dh1, dx2, g_norm1, sc1)

    dmod = [dsh1, dsc1, dg1, dsh2, dsc2, dg2]
    small = dict(g_norm1=dg_norm1, g_cq=dg_cq, g_ckv=dg_ckv, rel_bias=g_rel, g_out_a=dg_out_a, g_out_b=dg_out_b,
                 g_norm2=dg_norm2, g_final=dg_final)
    big = dict(w_in=gw_in, w_uq=gw_uq, w_k=gw_k, w_v=gw_v, w_out=gw_out, w_ffn_in=gw_ffn_in, w_ffn_out=gw_ffn_out)
    return grad_x, dmod, loss_cols, small, big


def kernel(x, c, w_ada, b_ada, g_norm1, w_in, g_cq, w_uq, g_ckv, w_ukv, rel_bias, g_out_a, g_out_b, w_out, g_norm2, w_ffn_in, w_ffn_out, g_final, loss_target, m_w_ada, m_b_ada, m_g_norm1, m_w_in, m_g_cq, m_w_uq, m_g_ckv, m_w_ukv, m_rel_bias, m_g_out_a, m_g_out_b, m_w_out, m_g_norm2, m_w_ffn_in, m_w_ffn_out, m_g_final, v_w_ada, v_b_ada, v_g_norm1, v_w_in, v_g_cq, v_w_uq, v_g_ckv, v_w_ukv, v_rel_bias, v_g_out_a, v_g_out_b, v_w_out, v_g_norm2, v_w_ffn_in, v_w_ffn_out, v_g_final):
    nb = x.shape[0]
    t = nb * S
    xt, tt = x.reshape(t, D), loss_target.reshape(t, D)
    me = 4 * lax.axis_index("x") + 2 * lax.axis_index("y") + lax.axis_index("c")

    early = [jnp.swapaxes(w_in, 1, 2)[0], w_uq[0], w_ukv[0]]
    gathered = _gather_two_level([_silu_rows(c)] + [s.astype(BF16) for s in early], "gather_weights")
    cond_all = gathered[0].reshape(N_DEV * nb, D)
    w_in_t = gathered[1].reshape(P_IN, D)
    w_uq_f, w_ukv_f = (_cols_from_blocks(g) for g in gathered[2:4])
    w_k, w_v = _split_w_ukv(w_ukv_f)

    ncol = N_MOD * D // N_DEV
    b_slab = lax.dynamic_slice(b_ada, (0, me * ncol), (1, ncol))
    slab = _mod_slab(cond_all, w_ada, b_slab)
    (mod_rows,) = _exchange([slab.reshape(N_DEV, nb, ncol)], [False], "scatter_mod")
    mod = jnp.transpose(mod_rows, (1, 0, 2)).reshape(nb, N_MOD, D)

    late = [s.astype(BF16) for s in (w_out[0], jnp.swapaxes(w_ffn_in, 1, 2)[0], w_ffn_out[0])]
    late_send, late_recv, late_src, late_land, late_token = _exchange_start(
        late, [_own_block_in_place(s, me) for s in late], [True] * 3, mod_rows, "gather_late_start")
    g_norm1_t = g_norm1 + late_token[:1, :1]

    def late_weights(after):
        w_out_g, w_ffn_in_g, w_ffn_out_g = _exchange_wait(late_send, late_recv, late_src, late_land, [True] * 3, after,
                                                          "gather_late_wait")
        return w_out_g.reshape(D, D), w_ffn_in_g.reshape(2 * D_FF, D), w_ffn_out_g.reshape(D_FF, D)

    flight = {}

    def start_grads(key, src, name):
        land = [_own_block_in_place(lax.dynamic_index_in_dim(s, me, 0, keepdims=False), me) for s in src]
        send, recv, src, land, token = _exchange_start(src, land, [False] * len(src), src[0], name)
        flight[key] = (send, recv, src, land)
        return token[:1, :1]

    def on_ffn_grads(gw_ffn_in, gw_ffn_out, gw_out):
        return start_grads("ffn", [gw_ffn_in.reshape(N_DEV, 2 * D_FF // N_DEV, D), gw_ffn_out.reshape(N_DEV, D_FF // N_DEV, D),
                                   gw_out.reshape(N_DEV, D // N_DEV, D)], "exchange_ffn_start")

    def on_last_grads(gw):
        return start_grads("rest", [_unpad_w_in(gw["w_in"]).reshape(N_DEV, P_IN // N_DEV, D),
                                    _cols_to_blocks(_unpad_w_uq(gw["w_uq"])),
                                    _cols_to_blocks(_join_w_ukv(gw["w_k"], gw["w_v"]))], "exchange_rest_start")

    grad_x, dmod, loss_cols, small, _ = _local_step(
        xt, mod, tt, g_norm1_t, _pad_w_in(w_in_t), g_cq, _pad_w_uq(w_uq_f), g_ckv, w_k, w_v, rel_bias, g_out_a, g_out_b,
        None, g_norm2, None, None, g_final.reshape(1, D), late_weights=late_weights, on_ffn_grads=on_ffn_grads,
        on_last_grads=on_last_grads)

    upd = {}

    def land_and_update(key, names, after, name):
        got = _exchange_wait(*flight[key], [False] * len(names), after, name)
        for n, p in zip(names, got):
            w, m, v = big[n]
            upd[n] = _adamw(p, w, m, v, "adamw_" + n)

    def flip(a):
        return jnp.swapaxes(a, 1, 2)

    big = dict(w_in=(flip(w_in), flip(m_w_in), flip(v_w_in)), w_uq=(w_uq, m_w_uq, v_w_uq), w_ukv=(w_ukv, m_w_ukv, v_w_ukv),
               w_out=(w_out, m_w_out, v_w_out), w_ffn_in=(flip(w_ffn_in), flip(m_w_ffn_in), flip(v_w_ffn_in)),
               w_ffn_out=(w_ffn_out, m_w_ffn_out, v_w_ffn_out))
    land_and_update("ffn", ["w_ffn_in", "w_ffn_out", "w_out"], grad_x, "exchange_ffn_wait")
    land_and_update("rest", ["w_in", "w_uq", "w_ukv"], upd["w_out"][0], "exchange_rest_wait")
    for n in ("w_in", "w_ffn_in"):
        upd[n] = tuple(flip(a) for a in upd[n])

    mine, dmod_blocks = _pack_small(dmod, [small[n] for n, _ in ROW_PARAMS], loss_cols)
    dmod_cols, pay, rel = _exchange([dmod_blocks, mine, small["rel_bias"]], [False, True, True], "exchange_small",
                                    after=upd["w_ukv"][0])
    g_ada = _ada_grad(cond_all, dmod_cols.reshape(N_DEV * nb, ncol))
    upd["w_ada"] = _adamw(g_ada[None], w_ada, m_w_ada, v_w_ada, "adamw_w_ada")
    row = lambda a: a.reshape(1, D)
    small_names = ["b_ada"] + [n for n, _ in ROW_PARAMS] + ["rel_bias"]
    small_w = [b_ada, g_norm1, g_cq, g_ckv, g_out_a, g_out_b, g_norm2, row(g_final), rel_bias]
    small_m = [m_b_ada, m_g_norm1, m_g_cq, m_g_ckv, m_g_out_a, m_g_out_b, m_g_norm2, row(m_g_final), m_rel_bias]
    small_v = [v_b_ada, v_g_norm1, v_g_cq, v_g_ckv, v_g_out_a, v_g_out_b, v_g_norm2, row(v_g_final), v_rel_bias]
    small_upd, loss8 = _small_update(pay, rel, small_w, small_m, small_v)
    upd.update(zip(small_names, small_upd))

    order = ["w_ada", "b_ada", "g_norm1", "w_in", "g_cq", "w_uq", "g_ckv", "w_ukv", "rel_bias", "g_out_a", "g_out_b",
             "w_out", "g_norm2", "w_ffn_in", "w_ffn_out", "g_final"]
    like = dict(g_final=g_final)
    outs = [loss8[0, 0], grad_x.reshape(x.shape)]
    for part in range(4):
        for n in order:
            val = upd[n][part]
            outs.append(val.reshape(like[n].shape) if n in like else val)
    return tuple(outs)
```

```python
import functools

import numpy as np
import jax
import jax.numpy as jnp
from jax import lax
from jax.experimental import pallas as pl
from jax.experimental.pallas import tpu as pltpu

F32, BF16 = jnp.float32, jnp.bfloat16

N_DEV = 8
D = 1024
S = 2048
H = 8
E_A = 64
D_A = H * E_A
Q_LORA, KV_LORA = 384, 256
NOPE, ROPE, VDIM = 64, 32, 64
HP = 128
P_IN = 3 * D_A + Q_LORA + KV_LORA + ROPE
P_PAD = 3 * D_A + Q_LORA + KV_LORA + HP
TAIL0 = 3 * D_A
TAIL = P_PAD - TAIL0
D_FF = 2816
N_MOD = 6
EPS = 1e-6
NEG = -1e30
BLK = 128
DILATIONS = (1, 4, 16)
N_BUCKETS, MAX_DISTANCE = 32, 2048
ROPE_THETA = 10000.0
SCALE_A = E_A ** -0.5
SCALE_B = (NOPE + ROPE) ** -0.5
B1, B2, LR, ADAM_EPS, WD, STEP = 0.9, 0.999, 0.001, 1e-8, 0.01, 10
VMEM_LIMIT = 56 * 1024 * 1024


def _cp(*sem):
    return pltpu.CompilerParams(dimension_semantics=sem, vmem_limit_bytes=VMEM_LIMIT)


def _pick(n, prefs):
    for p in prefs:
        if n % p == 0:
            return p
    raise ValueError(f"no tile of {prefs} divides {n}")


OPERAND_BYTES = 6 * 1024 * 1024


def _pick_rows(m, k):
    return _pick(m, [p for p in (1024, 512, 256, 128, 16) if p * k * 2 <= OPERAND_BYTES])


MATMUL_BYTES = 40 * 1024 * 1024


def _stream_rows(m, fixed, per_row):
    return _pick(m, [p for p in (4096, 2048, 1024, 512, 256, 128, 16) if fixed + p * per_row <= MATMUL_BYTES])


def _dot(a, b, dims):
    return lax.dot_general(a, b, (dims, ((), ())), preferred_element_type=F32)


def _mm_nn(a, b, out_dtype, name, after=None):
    m, k = a.shape
    n = b.shape[1]
    tn = _pick(n, (512, 384, 256, 128))
    tm = _stream_rows(m, 4 * k * tn, 4 * k + (2 * jnp.dtype(out_dtype).itemsize + 4) * tn)

    def body(a_ref, b_ref, *rest):
        o_ref = rest[-1]
        o_ref[...] = _dot(a_ref[...], b_ref[...], ((1,), (0,))).astype(o_ref.dtype)

    extra = [] if after is None else [after]
    return pl.pallas_call(
        body, name=name, grid=(m // tm, n // tn),
        in_specs=[pl.BlockSpec((tm, k), lambda i, j: (i, 0)), pl.BlockSpec((k, tn), lambda i, j: (0, j))] + [ANY] * len(extra),
        out_specs=pl.BlockSpec((tm, tn), lambda i, j: (i, j)),
        out_shape=jax.ShapeDtypeStruct((m, n), out_dtype),
        compiler_params=_cp("parallel", "parallel"),
    )(a, b, *extra)


def _mm_nt(a, b, out_dtype, name, after=None):
    m, k = a.shape
    n = b.shape[0]
    tn = _pick(n, (512, 384, 256, 128))
    tm = _stream_rows(m, 4 * k * tn, 4 * k + (2 * jnp.dtype(out_dtype).itemsize + 4) * tn)

    def body(a_ref, b_ref, *rest):
        o_ref = rest[-1]
        o_ref[...] = _dot(a_ref[...], b_ref[...], ((1,), (1,))).astype(o_ref.dtype)

    extra = [] if after is None else [after]
    return pl.pallas_call(
        body, name=name, grid=(m // tm, n // tn),
        in_specs=[pl.BlockSpec((tm, k), lambda i, j: (i, 0)), pl.BlockSpec((tn, k), lambda i, j: (j, 0))] + [ANY] * len(extra),
        out_specs=pl.BlockSpec((tm, tn), lambda i, j: (i, j)),
        out_shape=jax.ShapeDtypeStruct((m, n), out_dtype),
        compiler_params=_cp("parallel", "parallel"),
    )(a, b, *extra)


def _mm_tn(a, bs, name):
    t, m = a.shape
    n = bs[0].shape[1]
    nb_ = len(bs)
    tc = _pick(t, (512, 16))
    tn = _pick(n, (512, 384, 256, 128))
    tm = _pick(m, [p for p in (1024, 512, 384, 256, 128) if (3 * p + 2 * nb_ * tn) * t * 2 <= VMEM_LIMIT - 2 * OPERAND_BYTES])
    if tm <= 256 and nb_ * n * t * 2 <= 2 * OPERAND_BYTES:
        tn = n

    def body(*refs):
        a_ref, b_refs, o_refs, at_ref = refs[0], refs[1:1 + nb_], refs[1 + nb_:1 + 2 * nb_], refs[-1]

        @pl.when(pl.program_id(1) == 0)
        def _():
            def chunk(c, _):
                rows = pl.ds(pl.multiple_of(c * tc, tc), tc)
                at_ref[:, rows] = a_ref[rows, :].T
                return 0

            lax.fori_loop(0, t // tc, chunk, 0)

        for b_ref, o_ref in zip(b_refs, o_refs):
            o_ref[...] = _dot(at_ref[...], b_ref[...], ((1,), (0,))).astype(BF16)

    res = pl.pallas_call(
        body, name=name, grid=(m // tm, n // tn),
        in_specs=[pl.BlockSpec((t, tm), lambda i, j: (0, i))] + [pl.BlockSpec((t, tn), lambda i, j: (0, j))] * nb_,
        out_specs=[pl.BlockSpec((tm, tn), lambda i, j: (i, j))] * nb_,
        out_shape=[jax.ShapeDtypeStruct((m, n), BF16)] * nb_,
        scratch_shapes=[pltpu.VMEM((tm, t), BF16)],
        compiler_params=_cp("parallel", "arbitrary"),
    )(a, *bs)
    return res[0] if nb_ == 1 else res


def _mm_tn_rows(a_list, b, name):
    t, m = a_list[0].shape
    n = b.shape[1]
    na = len(a_list)
    tc, tm = _pick(t, (512, 16)), _pick(m, (256, 128))
    nblk = m // tm

    def body(*refs):
        a_refs, b_ref, o_ref, bt_ref, r_ref = refs[:na], refs[na], refs[na + 1], refs[na + 2], refs[na + 3]
        i = pl.program_id(0)

        @pl.when(i == 0)
        def _():
            def chunk(c, _):
                rows = pl.ds(pl.multiple_of(c * tc, tc), tc)
                bt_ref[:, rows] = b_ref[rows, :].T
                return 0

            lax.fori_loop(0, t // tc, chunk, 0)

        for s, a_ref in enumerate(a_refs):
            @pl.when((i >= s * nblk) & (i < (s + 1) * nblk))
            def _(a_ref=a_ref):
                r_ref[...] = _dot(bt_ref[...], a_ref[...], ((1,), (0,)))
                o_ref[...] = r_ref[...].T.astype(BF16)

    return pl.pallas_call(
        body, name=name, grid=(na * nblk,),
        in_specs=[pl.BlockSpec((t, tm), lambda i, s=s: (0, jnp.clip(i - s * nblk, 0, nblk - 1))) for s in range(na)]
        + [pl.BlockSpec((t, n), lambda i: (0, 0))],
        out_specs=pl.BlockSpec((tm, n), lambda i: (i, 0)),
        out_shape=jax.ShapeDtypeStruct((na * m, n), BF16),
        scratch_shapes=[pltpu.VMEM((n, t), BF16), pltpu.VMEM((n, tm), F32)],
        compiler_params=_cp("arbitrary"),
    )(*a_list, b)


EPI = 256


def _silu_parts(g):
    sg = 0.5 * jnp.tanh(0.5 * g) + 0.5
    return sg, g * sg


def _ffn_in(h2, wt):
    t, k = h2.shape
    tn = _pick(D_FF, (256, 128))
    tm = _stream_rows(t, 8 * k * tn, 4 * k + (3 * 2 * 2 + 2 * 4) * tn)
    nj = D_FF // tn

    def body(h_ref, wg_ref, wu_ref, g_ref, u_ref, a_ref):
        hv = h_ref[...]
        g_all = _dot(hv, wg_ref[...], ((1,), (1,)))
        u_all = _dot(hv, wu_ref[...], ((1,), (1,)))
        for r in range(tm // EPI):
            rows = slice(r * EPI, (r + 1) * EPI)
            g, u = g_all[rows], u_all[rows]
            g_ref[rows, :] = g.astype(BF16)
            u_ref[rows, :] = u.astype(BF16)
            a_ref[rows, :] = (_silu_parts(g)[1] * u).astype(BF16)

    blk = pl.BlockSpec((tm, tn), lambda i, j: (i, j))
    return pl.pallas_call(
        body, name="ffn_in", grid=(t // tm, nj),
        in_specs=[pl.BlockSpec((tm, k), lambda i, j: (i, 0)), pl.BlockSpec((tn, k), lambda i, j: (j, 0)),
                  pl.BlockSpec((tn, k), lambda i, j: (j + nj, 0))],
        out_specs=[blk] * 3, out_shape=[jax.ShapeDtypeStruct((t, D_FF), BF16)] * 3,
        compiler_params=_cp("parallel", "parallel"),
    )(h2, wt, wt)


def _d_act(df, w, g, u):
    t, k = df.shape
    tn = _pick(D_FF, (256, 128))
    tm = _stream_rows(t, 4 * k * tn, 4 * k + (4 * 2 * 2 + 4) * tn)

    def body(df_ref, w_ref, g_ref, u_ref, dg_ref, du_ref):
        da_all = _dot(df_ref[...], w_ref[...], ((1,), (1,)))
        for r in range(tm // EPI):
            rows = slice(r * EPI, (r + 1) * EPI)
            da = da_all[rows]
            gv = g_ref[rows, :].astype(F32)
            sg, silu = _silu_parts(gv)
            dg_ref[rows, :] = ((da * u_ref[rows, :].astype(F32)) * (sg + silu * (1.0 - sg))).astype(BF16)
            du_ref[rows, :] = (da * silu).astype(BF16)

    blk = pl.BlockSpec((tm, tn), lambda i, j: (i, j))
    return pl.pallas_call(
        body, name="d_act", grid=(t // tm, D_FF // tn),
        in_specs=[pl.BlockSpec((tm, k), lambda i, j: (i, 0)), pl.BlockSpec((tn, k), lambda i, j: (j, 0)), blk, blk],
        out_specs=[blk] * 2, out_shape=[jax.ShapeDtypeStruct((t, D_FF), BF16)] * 2,
        compiler_params=_cp("parallel", "parallel"),
    )(df, w, g, u)


def _d_h2(dg, du, wt):
    t = dg.shape[0]
    n = wt.shape[1]
    tm, tn = _pick_rows(t, D_FF), _pick(n, (512, 256, 128))

    def body(dg_ref, du_ref, wg_ref, wu_ref, o_ref):
        o_ref[...] = (_dot(dg_ref[...], wg_ref[...], ((1,), (0,)))
                      + _dot(du_ref[...], wu_ref[...], ((1,), (0,)))).astype(BF16)

    return pl.pallas_call(
        body, name="d_h2", grid=(t // tm, n // tn),
        in_specs=[pl.BlockSpec((tm, D_FF), lambda i, j: (i, 0)), pl.BlockSpec((tm, D_FF), lambda i, j: (i, 0)),
                  pl.BlockSpec((D_FF, tn), lambda i, j: (0, j)), pl.BlockSpec((D_FF, tn), lambda i, j: (1, j))],
        out_specs=pl.BlockSpec((tm, tn), lambda i, j: (i, j)),
        out_shape=jax.ShapeDtypeStruct((t, n), BF16),
        compiler_params=_cp("parallel", "parallel"),
    )(dg, du, wt, wt)


TM = 256


def _row(w):
    return pl.BlockSpec((TM, w), lambda i: (i, 0))


def _row_at(w, col):
    return pl.BlockSpec((TM, w), lambda i: (i, col))


def _vec(w):
    return pl.BlockSpec((1, w), lambda i: (0, 0))


def _per_ex(w):
    return pl.BlockSpec((1, 1, w), lambda i: (i // (S // TM), 0, 0))


def _pos(w):
    return pl.BlockSpec((TM, w), lambda i: (i % (S // TM), 0))


def _full(shape):
    return pl.BlockSpec(shape, lambda i: (0,) * len(shape))


def _rms(x):
    return lax.rsqrt(jnp.mean(x * x, axis=-1, keepdims=True) + EPS)


def _rms_bwd(n, r, dn):
    return r * (dn - n * jnp.mean(dn * n, axis=-1, keepdims=True))


def _colsum(v):
    return jnp.sum(v, axis=0, keepdims=True)


def _acc_first(i, ref, val, every=None):
    first = (i == 0) if every is None else (i % every == 0)

    @pl.when(first)
    def _():
        ref[...] = jnp.zeros_like(ref)

    ref[...] += val.reshape(ref.shape)


def _pre1(x, g, sc, sh):
    t = x.shape[0]

    def body(x_ref, g_ref, sc_ref, sh_ref, h_ref):
        xv = x_ref[...]
        n = xv * _rms(xv)
        h_ref[...] = ((n * g_ref[...]) * (1.0 + sc_ref[0]) + sh_ref[0]).astype(BF16)

    return pl.pallas_call(
        body, name="pre1", grid=(t // TM,),
        in_specs=[_row(D), _vec(D), _per_ex(D), _per_ex(D)],
        out_specs=_row(D), out_shape=jax.ShapeDtypeStruct((t, D), BF16),
        compiler_params=_cp("parallel"),
    )(x, g, sc, sh)


def _rope_fwd(v, c, sm, sp):
    return v * c + pltpu.roll(v, HP - ROPE // 2, 1) * sm + pltpu.roll(v, ROPE // 2, 1) * sp


def _rope_bwd(dv, c, sm, sp):
    return dv * c + pltpu.roll(dv * sm, ROPE // 2, 1) + pltpu.roll(dv * sp, HP - ROPE // 2, 1)


def _mla_pre(proj, g_cq, g_ckv, w_uq, w_k, w_v, rc, rsm, rsp):
    t = proj.shape[0]

    def body(tail_ref, gq_ref, gkv_ref, wuq_ref, wk_ref, wv_ref, c_ref, sm_ref, sp_ref,
             q_ref, k_ref, v_ref, cqn_ref, ckvn_ref):
        tail = tail_ref[...]
        cq, ckv, kr = tail[:, :Q_LORA], tail[:, Q_LORA:Q_LORA + KV_LORA], tail[:, Q_LORA + KV_LORA:]
        cqn = (cq * _rms(cq) * gq_ref[...]).astype(BF16)
        ckvn = (ckv * _rms(ckv) * gkv_ref[...]).astype(BF16)
        cqn_ref[...] = cqn
        ckvn_ref[...] = ckvn
        c, sm, sp = c_ref[...], sm_ref[...], sp_ref[...]
        q = _dot(cqn, wuq_ref[...], ((1,), (0,)))
        kn = _dot(ckvn, wk_ref[...], ((1,), (0,)))
        v_ref[...] = _dot(ckvn, wv_ref[...], ((1,), (0,))).astype(BF16)
        krr = _rope_fwd(kr, c, sm, sp)
        for h in range(H):
            sl = slice(h * HP, (h + 1) * HP)
            q_ref[:, sl] = _rope_fwd(q[:, sl], c, sm, sp).astype(BF16)
            k_ref[:, sl] = (kn[:, sl] + krr).astype(BF16)

    wide = H * HP
    return pl.pallas_call(
        body, name="mla_pre", grid=(t // TM,),
        in_specs=[_row_at(TAIL, TAIL0 // TAIL), _vec(Q_LORA), _vec(KV_LORA), _full((Q_LORA, wide)),
                  _full((KV_LORA, wide)), _full((KV_LORA, wide)), _pos(HP), _pos(HP), _pos(HP)],
        out_specs=[_row(wide), _row(wide), _row(wide), _row(Q_LORA), _row(KV_LORA)],
        out_shape=[jax.ShapeDtypeStruct((t, wide), BF16)] * 3
        + [jax.ShapeDtypeStruct((t, Q_LORA), BF16), jax.ShapeDtypeStruct((t, KV_LORA), BF16)],
        compiler_params=_cp("parallel"),
    )(proj, g_cq, g_ckv, w_uq, w_k, w_v, rc, rsm, rsp)


def _mla_pre_bwd(proj, dq_, dk_, dv_, dqkv_a, g_cq, g_ckv, w_uq, w_k, w_v, rc, rsm, rsp):
    t = proj.shape[0]
    wide = H * HP

    def body(tail_ref, dq_ref, dk_ref, dv_ref, dqa_ref, dka_ref, dva_ref, gq_ref, gkv_ref, wuq_ref, wk_ref, wv_ref,
             c_ref, sm_ref, sp_ref, dqo_ref, dproj_ref, dgq_ref, dgkv_ref):
        i = pl.program_id(0)
        for n, src in enumerate((dqa_ref, dka_ref, dva_ref)):
            dproj_ref[:, n * D_A:(n + 1) * D_A] = src[...]
        dtail_ref = dproj_ref.at[:, TAIL0:]
        tail = tail_ref[...]
        cq, ckv = tail[:, :Q_LORA], tail[:, Q_LORA:Q_LORA + KV_LORA]
        c, sm, sp = c_ref[...], sm_ref[...], sp_ref[...]
        dkr = jnp.zeros((TM, HP), F32)
        for h in range(H):
            sl = slice(h * HP, (h + 1) * HP)
            dqo_ref[:, sl] = _rope_bwd(dq_ref[:, sl].astype(F32), c, sm, sp).astype(BF16)
            dkr = dkr + dk_ref[:, sl].astype(F32)
        lane = lax.broadcasted_iota(jnp.int32, (TM, HP), 1)
        dkr = jnp.where((lane >= NOPE) & (lane < NOPE + ROPE), _rope_bwd(dkr, c, sm, sp), 0.0)
        dkb = dk_ref[...]
        dvb = dv_ref[...]
        dcqn = _dot(dqo_ref[...], wuq_ref[...], ((1,), (1,)))
        dckvn = _dot(dkb, wk_ref[...], ((1,), (1,))) + _dot(dvb, wv_ref[...], ((1,), (1,)))
        rq, rkv = _rms(cq), _rms(ckv)
        nq, nkv = cq * rq, ckv * rkv
        _acc_first(i, dgq_ref, _colsum(dcqn * nq))
        _acc_first(i, dgkv_ref, _colsum(dckvn * nkv))
        dtail_ref[:, :Q_LORA] = _rms_bwd(nq, rq, dcqn * gq_ref[...]).astype(BF16)
        dtail_ref[:, Q_LORA:Q_LORA + KV_LORA] = _rms_bwd(nkv, rkv, dckvn * gkv_ref[...]).astype(BF16)
        dtail_ref[:, Q_LORA + KV_LORA:] = dkr.astype(BF16)

    return pl.pallas_call(
        body, name="mla_pre_bwd", grid=(t // TM,),
        in_specs=[_row_at(TAIL, TAIL0 // TAIL), _row(wide), _row(wide), _row(wide), _row(D_A), _row(D_A), _row(D_A),
                  _vec(Q_LORA), _vec(KV_LORA), _full((Q_LORA, wide)), _full((KV_LORA, wide)), _full((KV_LORA, wide)),
                  _pos(HP), _pos(HP), _pos(HP)],
        out_specs=[_row(wide), _row(P_PAD), _vec(Q_LORA), _vec(KV_LORA)],
        out_shape=[jax.ShapeDtypeStruct((t, wide), BF16), jax.ShapeDtypeStruct((t, P_PAD), BF16),
                   jax.ShapeDtypeStruct((1, Q_LORA), F32), jax.ShapeDtypeStruct((1, KV_LORA), F32)],
        compiler_params=_cp("arbitrary"),
    )(proj, dq_, dk_, dv_, *dqkv_a, g_cq, g_ckv, w_uq, w_k, w_v, rc, rsm, rsp)


def _post_attn(out_a, out_b, g_a, g_b):
    t = out_a.shape[0]

    def body(a_ref, b_ref, ga_ref, gb_ref, y_ref):
        a, b = a_ref[...], b_ref[...]
        y_ref[:, :D_A] = (a * _rms(a) * ga_ref[...]).astype(BF16)
        y_ref[:, D_A:] = (b * _rms(b) * gb_ref[...]).astype(BF16)

    return pl.pallas_call(
        body, name="post_attn", grid=(t // TM,),
        in_specs=[_row(D_A), _row(D_A), _vec(D_A), _vec(D_A)],
        out_specs=_row(D), out_shape=jax.ShapeDtypeStruct((t, D), BF16),
        compiler_params=_cp("parallel"),
    )(out_a, out_b, g_a, g_b)


def _post_attn_bwd(dy, out_a, out_b, g_a, g_b):
    t = dy.shape[0]

    def body(dy_ref, a_ref, b_ref, ga_ref, gb_ref, da_ref, db_ref, dga_ref, dgb_ref):
        i = pl.program_id(0)
        dy_ = dy_ref[...].astype(F32)
        for src, g_ref, dst, dg_ref, sl in ((a_ref, ga_ref, da_ref, dga_ref, slice(0, D_A)),
                                            (b_ref, gb_ref, db_ref, dgb_ref, slice(D_A, D))):
            v = src[...]
            r = _rms(v)
            n = v * r
            dyv = dy_[:, sl]
            _acc_first(i, dg_ref, _colsum(dyv * n))
            dst[...] = _rms_bwd(n, r, dyv * g_ref[...])

    return pl.pallas_call(
        body, name="post_attn_bwd", grid=(t // TM,),
        in_specs=[_row(D), _row(D_A), _row(D_A), _vec(D_A), _vec(D_A)],
        out_specs=[_row(D_A), _row(D_A), _vec(D_A), _vec(D_A)],
        out_shape=[jax.ShapeDtypeStruct((t, D_A), F32)] * 2 + [jax.ShapeDtypeStruct((1, D_A), F32)] * 2,
        compiler_params=_cp("arbitrary"),
    )(dy, out_a, out_b, g_a, g_b)


def _resid_norm2(x, mix, g1, g, sc, sh):
    t = x.shape[0]

    def body(x_ref, mix_ref, g1_ref, g_ref, sc_ref, sh_ref, x2_ref, h_ref):
        x2 = x_ref[...] + g1_ref[0] * mix_ref[...]
        x2_ref[...] = x2
        n = x2 * _rms(x2)
        h_ref[...] = ((n * g_ref[...]) * (1.0 + sc_ref[0]) + sh_ref[0]).astype(BF16)

    return pl.pallas_call(
        body, name="resid_norm2", grid=(t // TM,),
        in_specs=[_row(D), _row(D), _per_ex(D), _vec(D), _per_ex(D), _per_ex(D)],
        out_specs=[_row(D), _row(D)],
        out_shape=[jax.ShapeDtypeStruct((t, D), F32), jax.ShapeDtypeStruct((t, D), BF16)],
        compiler_params=_cp("parallel"),
    )(x, mix, g1, g, sc, sh)


def _sigmoid(v):
    return 1.0 / (1.0 + jnp.exp(-v))


def _final(x2, f, g2, g_fin, target):
    t = x2.shape[0]
    nb = t // S
    tpb = S // TM

    def body(x2_ref, f_ref, g2_ref, g_ref, t_ref, dx3_ref, df_ref, loss_ref, dgf_ref, dg2_ref):
        i = pl.program_id(0)
        fv = f_ref[...].astype(F32)
        x3 = x2_ref[...] + g2_ref[0] * fv
        r = _rms(x3)
        n = x3 * r
        err = n * g_ref[...] - t_ref[...]
        _acc_first(i, loss_ref, _colsum(err * err))
        dy = err * (1.0 / D)
        _acc_first(i, dgf_ref, _colsum(dy * n))
        dx3 = _rms_bwd(n, r, dy * g_ref[...])
        dx3_ref[...] = dx3
        _acc_first(i, dg2_ref, _colsum(dx3 * fv), every=tpb)
        df_ref[...] = (dx3 * g2_ref[0]).astype(BF16)

    return pl.pallas_call(
        body, name="final", grid=(t // TM,),
        in_specs=[_row(D), _row(D), _per_ex(D), _vec(D), _row(D)],
        out_specs=[_row(D), _row(D), _vec(D), _vec(D), _per_ex(D)],
        out_shape=[jax.ShapeDtypeStruct((t, D), F32), jax.ShapeDtypeStruct((t, D), BF16),
                   jax.ShapeDtypeStruct((1, D), F32), jax.ShapeDtypeStruct((1, D), F32),
                   jax.ShapeDtypeStruct((nb, 1, D), F32)],
        compiler_params=_cp("arbitrary"),
    )(x2, f, g2, g_fin, target)


def _norm_bwd(xin, dh, dres, g, sc, gate=None):
    t = xin.shape[0]
    nb = t // S
    tpb = S // TM
    gated = gate is not None

    def body(*refs):
        if gated:
            x_ref, dh_ref, dres_ref, g_ref, sc_ref, mix_ref, g1_ref, dx_ref, dsh_ref, dsc_ref, dg_ref, dg1_ref, dmix_ref = refs
        else:
            x_ref, dh_ref, dres_ref, g_ref, sc_ref, dx_ref, dsh_ref, dsc_ref, dg_ref = refs
        i = pl.program_id(0)
        xv, dhv = x_ref[...], dh_ref[...].astype(F32)
        r = _rms(xv)
        n = xv * r
        gv = g_ref[...]
        _acc_first(i, dsh_ref, _colsum(dhv), every=tpb)
        _acc_first(i, dsc_ref, _colsum(dhv * (n * gv)), every=tpb)
        dng = dhv * (1.0 + sc_ref[0])
        _acc_first(i, dg_ref, _colsum(dng * n))
        dx = dres_ref[...] + _rms_bwd(n, r, dng * gv)
        dx_ref[...] = dx
        if gated:
            _acc_first(i, dg1_ref, _colsum(dx * mix_ref[...].astype(F32)), every=tpb)
            dmix_ref[...] = (dx * g1_ref[0]).astype(BF16)

    in_specs = [_row(D), _row(D), _row(D), _vec(D), _per_ex(D)]
    out_specs = [_row(D), _per_ex(D), _per_ex(D), _vec(D)]
    out_shape = [jax.ShapeDtypeStruct((t, D), F32), jax.ShapeDtypeStruct((nb, 1, D), F32),
                 jax.ShapeDtypeStruct((nb, 1, D), F32), jax.ShapeDtypeStruct((1, D), F32)]
    args = [xin, dh, dres, g, sc]
    if gated:
        in_specs += [_row(D), _per_ex(D)]
        out_specs += [_per_ex(D), _row(D)]
        out_shape += [jax.ShapeDtypeStruct((nb, 1, D), F32), jax.ShapeDtypeStruct((t, D), BF16)]
        args += list(gate)
    return pl.pallas_call(
        body, name="norm2_bwd" if gated else "norm1_bwd", grid=(t // TM,),
        in_specs=in_specs, out_specs=out_specs, out_shape=out_shape,
        compiler_params=_cp("arbitrary"),
    )(*args)


TQ = 256
TB = 512


def _mla_fwd(q, k, v):
    t = q.shape[0]
    nb = t // S

    def body(q_ref, k_ref, v_ref, o_ref, lse_ref):
        causal = lax.broadcasted_iota(jnp.int32, (TB, TB), 0) >= lax.broadcasted_iota(jnp.int32, (TB, TB), 1)
        heads = [slice(h * HP, (h + 1) * HP) for h in range(2)]
        for i in range(S // TB):
            ri, past = slice(i * TB, (i + 1) * TB), slice(0, i * TB)
            qhs = [q_ref[ri, sl] for sl in heads]
            sd = [jnp.where(causal, _dot(qh, k_ref[ri, sl], ((1,), (1,))) * SCALE_B, NEG) for qh, sl in zip(qhs, heads)]
            ms = [jnp.max(s, axis=-1, keepdims=True) for s in sd]
            if i:
                so = [_dot(qh, k_ref[past, sl], ((1,), (1,))) * SCALE_B for qh, sl in zip(qhs, heads)]
                ms = [jnp.maximum(m, jnp.max(s, axis=-1, keepdims=True)) for m, s in zip(ms, so)]
            pd = [jnp.exp(s - m) for s, m in zip(sd, ms)]
            ls = [jnp.sum(p, axis=-1, keepdims=True) for p in pd]
            acc = [_dot(p.astype(BF16), v_ref[ri, sl], ((1,), (0,))) for p, sl in zip(pd, heads)]
            if i:
                po = [jnp.exp(s - m) for s, m in zip(so, ms)]
                ls = [l + jnp.sum(p, axis=-1, keepdims=True) for l, p in zip(ls, po)]
                acc = [a + _dot(p.astype(BF16), v_ref[past, sl], ((1,), (0,))) for a, p, sl in zip(acc, po, heads)]
            o_ref[ri, :] = acc[0] / ls[0] + acc[1] / ls[1]
            for sl, m, l in zip(heads, ms, ls):
                lse_ref[ri, sl] = jnp.broadcast_to(m + jnp.log(l), (TB, HP))

    wide2 = pl.BlockSpec((S, 2 * HP), lambda b, p: (b, p))
    return pl.pallas_call(
        body, name="mla_fwd", grid=(nb, H // 2),
        in_specs=[wide2, wide2, wide2],
        out_specs=[pl.BlockSpec((S, HP), lambda b, p: (b, p)), wide2],
        out_shape=[jax.ShapeDtypeStruct((t, H * VDIM), F32), jax.ShapeDtypeStruct((t, H * HP), F32)],
        compiler_params=_cp("parallel", "parallel"),
    )(q, k, v)


def _mla_bwd(q, k, v, o, do, lse):
    t = q.shape[0]
    nb = t // S

    def body(q_ref, k_ref, v_ref, o_ref, do_ref, lse_ref, dq_out, dk_out, dv_out, dq_ref, dk_ref, dv_ref):
        lane = lax.broadcasted_iota(jnp.int32, (TB, HP), 1)
        causal = lax.broadcasted_iota(jnp.int32, (TB, TB), 0) >= lax.broadcasted_iota(jnp.int32, (TB, TB), 1)
        heads = [slice(h * HP, (h + 1) * HP) for h in range(2)]
        nblk = S // TB
        for i in reversed(range(nblk)):
            ri, past = slice(i * TB, (i + 1) * TB), slice(0, i * TB)
            dov = do_ref[ri, :]
            prod = dov * o_ref[ri, :]
            dob = dov.astype(BF16)
            deltas = [jnp.sum(jnp.where((lane < VDIM) if h == 0 else (lane >= VDIM), prod, 0.0), axis=-1, keepdims=True)
                      for h in range(2)]
            qhs = [q_ref[ri, sl] for sl in heads]
            lses = [lse_ref[ri, sl][:, :1] for sl in heads]
            for rows, diagonal in ((ri, True), (past, False)):
                if rows.stop == rows.start:
                    continue
                ps = [jnp.exp(_dot(qh, k_ref[rows, sl], ((1,), (1,))) * SCALE_B - lse) for qh, sl, lse in zip(qhs, heads, lses)]
                if diagonal:
                    ps = [jnp.where(causal, p, 0.0) for p in ps]
                dps = [_dot(dob, v_ref[rows, sl], ((1,), (1,))) for sl in heads]
                dss = [(p * (dp - delta) * SCALE_B).astype(BF16) for p, dp, delta in zip(ps, dps, deltas)]
                for sl, qh, p, ds in zip(heads, qhs, ps, dss):
                    dq = _dot(ds, k_ref[rows, sl], ((1,), (0,)))
                    dk = _dot(ds, qh, ((0,), (0,)))
                    dv = _dot(p.astype(BF16), dob, ((0,), (0,)))
                    if diagonal:
                        dq_ref[ri, sl] = dq
                    else:
                        dq_ref[ri, sl] += dq
                    if i == nblk - 1:
                        dk_ref[rows, sl] = dk
                        dv_ref[rows, sl] = dv
                    else:
                        dk_ref[rows, sl] += dk
                        dv_ref[rows, sl] += dv
        dq_out[...] = dq_ref[...].astype(BF16)
        dk_out[...] = dk_ref[...].astype(BF16)
        dv_out[...] = dv_ref[...].astype(BF16)

    wide2 = pl.BlockSpec((S, 2 * HP), lambda b, p: (b, p))
    pair = pl.BlockSpec((S, HP), lambda b, p: (b, p))
    return pl.pallas_call(
        body, name="mla_bwd", grid=(nb, H // 2),
        in_specs=[wide2, wide2, wide2, pair, pair, wide2],
        out_specs=[wide2, wide2, wide2],
        out_shape=[jax.ShapeDtypeStruct((t, H * HP), BF16)] * 3,
        scratch_shapes=[pltpu.VMEM((S, 2 * HP), F32)] * 3,
        compiler_params=_cp("parallel", "parallel"),
    )(q, k, v, o, do, lse)


def _t5_bucket(dist):
    max_exact = N_BUCKETS // 2
    d = np.maximum(dist, 1).astype(np.float64)
    large = max_exact + (np.log(d / max_exact) / np.log(MAX_DISTANCE / max_exact) * (N_BUCKETS - max_exact)).astype(np.int64)
    large = np.minimum(large, N_BUCKETS - 1)
    return np.where(dist < max_exact, dist, large).astype(np.int32)


def _band_geometry():
    a = np.arange(BLK)[:, None]
    bk = np.arange(2 * BLK)[None, :]
    steps = BLK + a - bk
    valid = (steps >= 0) & (steps <= BLK)
    buckets = np.stack([_t5_bucket(np.clip(steps, 0, BLK) * d) for d in DILATIONS])
    return buckets, valid


def _band_bias(rel_bias):
    buckets, valid = _band_geometry()
    onehot = (jnp.asarray(buckets)[..., None] == jnp.arange(N_BUCKETS)).astype(F32)
    bias = jnp.einsum("rqkn,nh->rhqk", onehot, rel_bias, precision=lax.Precision.HIGHEST)
    bias = jnp.where(jnp.asarray(valid)[None, None], bias, NEG)
    return bias.reshape(3, H // 2, 2 * BLK, 2 * BLK)


def _dil_items():
    items = []
    for r, d in enumerate(DILATIONS):
        for res in range(d):
            for blk in range(S // d // BLK):
                items.append((r, d, blk * BLK * d + res, blk > 0))
    return items


GROUP = 4


def _strided(start, d):
    return pl.ds(start, BLK) if d == 1 else pl.ds(start, BLK, stride=d)


def _stack_heads(tile, own):
    return jnp.where(own, jnp.concatenate([tile, tile], axis=0), 0.0).astype(BF16)


def _own_lanes():
    row = lax.broadcasted_iota(jnp.int32, (2 * BLK, HP), 0)
    lane = lax.broadcasted_iota(jnp.int32, (2 * BLK, HP), 1)
    return (lane < E_A) == (row < BLK)


def _dil_fwd(proj, biasm):
    t = proj.shape[0]
    nb = t // S

    def body(q_ref, k_ref, v_ref, b_ref, o_ref, lse_ref, ob_ref, lb_ref):
        lane = lax.broadcasted_iota(jnp.int32, (BLK, HP), 1)
        own = _own_lanes()
        items = _dil_items()
        for g in range(0, len(items), GROUP):
            grp = items[g:g + GROUP]
            ss, vts = [], []
            for r, d, start, has_prev in grp:
                cur = _strided(start, d)
                rows = [_strided(start - BLK * d, d), cur] if has_prev else [cur]
                q2 = _stack_heads(q_ref[cur, :] * SCALE_A, own)
                kt = jnp.concatenate([k_ref[x, :] for x in rows], axis=0).astype(BF16)
                vts.append(jnp.concatenate([v_ref[x, :] for x in rows], axis=0).astype(BF16))
                bias = b_ref[r, 0] if has_prev else b_ref[r, 0, :, BLK:]
                ss.append(_dot(q2, kt, ((1,), (1,))) + bias)
            ms = [jnp.max(s, axis=-1, keepdims=True) for s in ss]
            ps = [jnp.exp(s - m) for s, m in zip(ss, ms)]
            ls = [jnp.sum(p, axis=-1, keepdims=True) for p in ps]
            for (r, d, start, _), p, vt, m, l in zip(grp, ps, vts, ms, ls):
                cur = _strided(start, d)
                o2 = _dot(p.astype(BF16), vt, ((1,), (0,))) / l
                lse2 = m + jnp.log(l)
                ob_ref[r, cur, :] = jnp.where(lane < E_A, o2[:BLK], o2[BLK:])
                lb_ref[r, cur, :] = jnp.where(lane < E_A, lse2[:BLK], lse2[BLK:])

        def merge(c, _):
            rows = pl.ds(pl.multiple_of(c * TQ, TQ), TQ)
            l0, l1, l2 = lb_ref[0, rows, :], lb_ref[1, rows, :], lb_ref[2, rows, :]
            m = jnp.maximum(jnp.maximum(l0, l1), l2)
            e0, e1, e2 = jnp.exp(l0 - m), jnp.exp(l1 - m), jnp.exp(l2 - m)
            tot = e0 + e1 + e2
            o_ref[rows, :] = (e0 * ob_ref[0, rows, :] + e1 * ob_ref[1, rows, :] + e2 * ob_ref[2, rows, :]) / tot
            lse_ref[rows, :] = m + jnp.log(tot)
            return 0

        lax.fori_loop(0, S // TQ, merge, 0)

    npair = H // 2
    return pl.pallas_call(
        body, name="dil_fwd", grid=(nb, npair),
        in_specs=[pl.BlockSpec((S, HP), lambda b, p: (b, p)), pl.BlockSpec((S, HP), lambda b, p: (b, npair + p)),
                  pl.BlockSpec((S, HP), lambda b, p: (b, 2 * npair + p)),
                  pl.BlockSpec((3, 1, 2 * BLK, 2 * BLK), lambda b, p: (0, p, 0, 0))],
        out_specs=[pl.BlockSpec((S, HP), lambda b, p: (b, p))] * 2,
        out_shape=[jax.ShapeDtypeStruct((t, D_A), F32)] * 2,
        scratch_shapes=[pltpu.VMEM((3, S, HP), F32), pltpu.VMEM((3, S, HP), F32)],
        compiler_params=_cp("parallel", "parallel"),
    )(proj, proj, proj, biasm)


def _dil_bwd(proj, biasm, o, do, lse):
    t = proj.shape[0]
    nb = t // S

    def body(q_ref, k_ref, v_ref, b_ref, o_ref, do_ref, lse_ref, dq_out, dk_out, dv_out, ds_ref, dq_ref, dk_ref, dv_ref):
        dq_ref[...] = jnp.zeros_like(dq_ref)
        dk_ref[...] = jnp.zeros_like(dk_ref)
        dv_ref[...] = jnp.zeros_like(dv_ref)
        ds_ref[...] = jnp.zeros_like(ds_ref)
        lane = lax.broadcasted_iota(jnp.int32, (BLK, HP), 1)
        own = _own_lanes()
        items = _dil_items()
        for g in range(0, len(items), GROUP):
            grp = items[g:g + GROUP]
            q2s, kts, do2s, ss, dps, lse2s, delta2s = [], [], [], [], [], [], []
            for r, d, start, has_prev in grp:
                cur = _strided(start, d)
                rows = [_strided(start - BLK * d, d), cur] if has_prev else [cur]
                q2 = _stack_heads(q_ref[cur, :] * SCALE_A, own)
                kt = jnp.concatenate([k_ref[x, :] for x in rows], axis=0).astype(BF16)
                vt = jnp.concatenate([v_ref[x, :] for x in rows], axis=0).astype(BF16)
                dot_ = do_ref[cur, :]
                prod = dot_ * o_ref[cur, :]
                lset = lse_ref[cur, :]
                do2 = _stack_heads(dot_, own)
                bias = b_ref[r, 0] if has_prev else b_ref[r, 0, :, BLK:]
                ss.append(_dot(q2, kt, ((1,), (1,))) + bias)
                dps.append(_dot(do2, vt, ((1,), (1,))))
                lse2s.append(jnp.concatenate([lset[:, :1], lset[:, E_A:E_A + 1]], axis=0))
                delta2s.append(jnp.concatenate([jnp.sum(jnp.where(lane < E_A, prod, 0.0), axis=-1, keepdims=True),
                                                jnp.sum(jnp.where(lane >= E_A, prod, 0.0), axis=-1, keepdims=True)], axis=0))
                q2s.append(q2)
                kts.append(kt)
                do2s.append(do2)
            ps = [jnp.exp(s - lse2) for s, lse2 in zip(ss, lse2s)]
            dls = [p * (dp - delta2) for p, dp, delta2 in zip(ps, dps, delta2s)]
            for (r, d, start, has_prev), q2, kt, do2, p, dl in zip(grp, q2s, kts, do2s, ps, dls):
                cur = _strided(start, d)
                dsb = dl.astype(BF16)
                dq2 = _dot(dsb, kt, ((1,), (0,))) * SCALE_A
                dkt = _dot(dsb, q2, ((0,), (0,)))
                dvt = _dot(p.astype(BF16), do2, ((0,), (0,)))
                dq_ref[cur, :] += jnp.where(lane < E_A, dq2[:BLK], dq2[BLK:])
                if has_prev:
                    prev = _strided(start - BLK * d, d)
                    ds_ref[0, r, 0] += dl
                    dk_ref[prev, :] += dkt[:BLK]
                    dv_ref[prev, :] += dvt[:BLK]
                    dk_ref[cur, :] += dkt[BLK:]
                    dv_ref[cur, :] += dvt[BLK:]
                else:
                    ds_ref[0, r, 0, :, BLK:] += dl
                    dk_ref[cur, :] += dkt
                    dv_ref[cur, :] += dvt
        dq_out[...] = dq_ref[...].astype(BF16)
        dk_out[...] = dk_ref[...].astype(BF16)
        dv_out[...] = dv_ref[...].astype(BF16)

    npair = H // 2
    pair = pl.BlockSpec((S, HP), lambda b, p: (b, p))
    return pl.pallas_call(
        body, name="dil_bwd", grid=(nb, npair),
        in_specs=[pair, pl.BlockSpec((S, HP), lambda b, p: (b, npair + p)),
                  pl.BlockSpec((S, HP), lambda b, p: (b, 2 * npair + p)),
                  pl.BlockSpec((3, 1, 2 * BLK, 2 * BLK), lambda b, p: (0, p, 0, 0)), pair, pair, pair],
        out_specs=[pair, pair, pair, pl.BlockSpec((1, 3, 1, 2 * BLK, 2 * BLK), lambda b, p: (b, 0, p, 0, 0))],
        out_shape=[jax.ShapeDtypeStruct((t, D_A), BF16)] * 3 + [jax.ShapeDtypeStruct((nb, 3, npair, 2 * BLK, 2 * BLK), F32)],
        scratch_shapes=[pltpu.VMEM((S, HP), F32)] * 3,
        compiler_params=_cp("parallel", "parallel"),
    )(proj, proj, proj, biasm, o, do, lse)


def _rel_bias_grad(dlogits):
    nb = dlogits.shape[0]
    buckets, _ = _band_geometry()
    kk = 3 * BLK * 2 * BLK
    dl = jnp.transpose(dlogits.reshape(nb, 3, H, BLK, 2 * BLK), (0, 2, 1, 3, 4)).reshape(nb, H, kk)
    bk = jnp.asarray(buckets.reshape(1, kk))
    tk = kk // 12

    def body(dl_ref, bk_ref, o_ref):
        j = pl.program_id(0)
        onehot = (bk_ref[...] == lax.broadcasted_iota(jnp.int32, (N_BUCKETS, tk), 0)).astype(F32)
        tot = dl_ref[0]
        for b in range(1, nb):
            tot = tot + dl_ref[b]
        part = lax.dot_general(onehot, tot, ((((1,), (1,))), ((), ())), preferred_element_type=F32,
                               precision=lax.Precision.HIGHEST)
        _acc_first(j, o_ref, part)

    return pl.pallas_call(
        body, name="rel_bias_grad", grid=(kk // tk,),
        in_specs=[pl.BlockSpec((nb, H, tk), lambda j: (0, 0, j)), pl.BlockSpec((1, tk), lambda j: (0, j))],
        out_specs=pl.BlockSpec((N_BUCKETS, H), lambda j: (0, 0)),
        out_shape=jax.ShapeDtypeStruct((N_BUCKETS, H), F32),
        compiler_params=_cp("arbitrary"),
    )(dl, bk)


def _mesh_place():
    x, y, c = lax.axis_index("x"), lax.axis_index("y"), lax.axis_index("c")
    return x, y, c


def _peer(k):
    x, y, c = _mesh_place()
    px = 1 - x if k & 4 else x
    py = 1 - y if k & 2 else y
    pc = 1 - c if k & 1 else c
    return (px, py, pc), 4 * px + 2 * py + pc


ANY = pl.BlockSpec(memory_space=pl.ANY)


def _exchange(arrays, gathers, name, after=None):
    n_arr = len(arrays)

    def body(*refs):
        ins, outs = refs[:n_arr], refs[n_arr + 1:2 * n_arr + 1]
        send, recv, loc = refs[2 * n_arr + 1:]
        x, y, c = _mesh_place()
        me = 4 * x + 2 * y + c
        local = [pltpu.make_async_copy(ins[a] if gathers[a] else ins[a].at[me], outs[a].at[me], loc.at[a])
                 for a in range(n_arr)]
        remote = _peer_copies(ins, outs, send, recv, gathers)
        for cp in local:
            cp.start()
        for put, _ in remote:
            put.start()
        for cp in local:
            cp.wait()
        for put, got in remote:
            put.wait_send()
            got.wait_recv()

    return pl.pallas_call(
        body, name=name,
        in_specs=[ANY] * (n_arr + 1), out_specs=[ANY] * n_arr,
        out_shape=[jax.ShapeDtypeStruct(((N_DEV,) if g else ()) + a.shape, a.dtype) for a, g in zip(arrays, gathers)],
        scratch_shapes=[pltpu.SemaphoreType.DMA((n_arr * (N_DEV - 1),)), pltpu.SemaphoreType.DMA((n_arr * (N_DEV - 1),)),
                        pltpu.SemaphoreType.DMA((n_arr,))],
        compiler_params=pltpu.CompilerParams(has_side_effects=True),
    )(*arrays, arrays[0] if after is None else after)


def _gather_two_level(arrays, name):
    n_arr = len(arrays)
    per = N_DEV - 1

    def body(*refs):
        ins, outs = refs[:n_arr], refs[n_arr:2 * n_arr]
        send, recv, loc = refs[2 * n_arr:]
        x, y, c = _mesh_place()
        me, sibling = (x, y, c), (x, y, 1 - c)
        chips = [(1 - x, y), (x, 1 - y), (1 - x, 1 - y)]

        def block(a, place):
            px, py, pc = place
            return outs[a].at[4 * px + 2 * py + pc]

        def copy(a, k, place, to, src=None):
            dst = block(a, place)
            return pltpu.make_async_remote_copy(dst if src is None else src, dst, send.at[a * per + k], recv.at[a * per + k],
                                                device_id=to, device_id_type=pl.DeviceIdType.MESH)

        local = [pltpu.make_async_copy(ins[a], block(a, me), loc.at[a]) for a in range(n_arr)]
        for cp in local:
            cp.start()
        first = []
        for a in range(n_arr):
            first.append(copy(a, 0, me, sibling, src=ins[a]))
            first += [copy(a, 1 + j, me, (*chip, c), src=ins[a]) for j, chip in enumerate(chips)]
        for cp in first:
            cp.start()
        passed = []
        for j, chip in enumerate(chips):
            for a in range(n_arr):
                copy(a, 1 + j, (*chip, c), me).wait_recv()
                passed.append(copy(a, 4 + j, (*chip, c), sibling))
                passed[-1].start()
        for a in range(n_arr):
            copy(a, 0, sibling, me).wait_recv()
            for j, chip in enumerate(chips):
                copy(a, 4 + j, (*chip, 1 - c), me).wait_recv()
        for cp in first + passed:
            cp.wait_send()
        for cp in local:
            cp.wait()

    return pl.pallas_call(
        body, name=name,
        in_specs=[ANY] * n_arr, out_specs=[ANY] * n_arr,
        out_shape=[jax.ShapeDtypeStruct((N_DEV,) + a.shape, a.dtype) for a in arrays],
        scratch_shapes=[pltpu.SemaphoreType.DMA((n_arr * per,)), pltpu.SemaphoreType.DMA((n_arr * per,)),
                        pltpu.SemaphoreType.DMA((n_arr,))],
        compiler_params=pltpu.CompilerParams(has_side_effects=True),
    )(*arrays)


HBM = pl.BlockSpec(memory_space=pltpu.HBM)
SEM = pl.BlockSpec(memory_space=pltpu.SEMAPHORE)
DATAFLOW = pltpu.SideEffectType.DATAFLOW_SIDE_EFFECTING


def _own_block_in_place(block, me):
    land = lax.empty((N_DEV,) + block.shape, block.dtype)
    return lax.dynamic_update_slice(land, block[None], (me,) + (0,) * block.ndim)


def _peer_copies(srcs, lands, send, recv, gathers):
    x, y, c = _mesh_place()
    me = 4 * x + 2 * y + c
    out = []
    for a, (src, land) in enumerate(zip(srcs, lands)):
        for k in range(1, N_DEV):
            dev, idx = _peer(k)
            sem = a * (N_DEV - 1) + k - 1
            mine = src if gathers[a] else src.at[idx]
            put = pltpu.make_async_remote_copy(mine, land.at[me], send.at[sem], recv.at[sem],
                                               device_id=dev, device_id_type=pl.DeviceIdType.MESH)
            got = pltpu.make_async_remote_copy(mine, land.at[idx], send.at[sem], recv.at[sem],
                                               device_id=dev, device_id_type=pl.DeviceIdType.MESH)
            out.append((put, got))
    return out


def _exchange_start(srcs, lands, gather, after, name):
    n = len(srcs)

    def body(*refs):
        srcs_, lands_, send, recv = refs[:n], refs[n:2 * n], refs[2 * n + 1], refs[2 * n + 2]
        for put, _ in _peer_copies(srcs_, lands_, send, recv, gather):
            put.start()
        refs[-1][...] = jnp.zeros_like(refs[-1])

    nsem = n * (N_DEV - 1)
    thru = [pltpu.HBM(a.shape, a.dtype) for a in list(srcs) + list(lands)]
    res = pl.pallas_call(
        body, name=name,
        out_shape=(pltpu.SemaphoreType.DMA((nsem,)), pltpu.SemaphoreType.DMA((nsem,)), *thru, jax.ShapeDtypeStruct((8, 128), F32)),
        in_specs=[HBM] * (2 * n) + [ANY],
        out_specs=(SEM, SEM, *([HBM] * (2 * n)), pl.BlockSpec(memory_space=pltpu.VMEM)),
        input_output_aliases={i: 2 + i for i in range(2 * n)},
        compiler_params=pltpu.CompilerParams(has_side_effects=DATAFLOW),
    )(*[pltpu.with_memory_space_constraint(a, pltpu.HBM) for a in list(srcs) + list(lands)], after)
    return res[0], res[1], list(res[2:2 + n]), list(res[2 + n:2 + 2 * n]), res[-1]


def _exchange_wait(send, recv, srcs, lands, gather, after, name):
    n = len(srcs)

    def body(*refs):
        srcs_, lands_, send_, recv_ = refs[:n], refs[n:2 * n], refs[2 * n], refs[2 * n + 1]
        for put, got in _peer_copies(srcs_, lands_, send_, recv_, gather):
            put.wait_send()
            got.wait_recv()

    thru = [pltpu.HBM(a.shape, a.dtype) for a in list(srcs) + list(lands)]
    res = pl.pallas_call(
        body, name=name, out_shape=tuple(thru),
        in_specs=[HBM] * (2 * n) + [SEM, SEM, ANY], out_specs=tuple([HBM] * (2 * n)),
        input_output_aliases={i: i for i in range(2 * n)},
        compiler_params=pltpu.CompilerParams(has_side_effects=DATAFLOW),
    )(*srcs, *lands, send, recv, after)
    return list(res[n:])


def _silu_rows(c):
    def body(c_ref, o_ref):
        v = c_ref[...]
        o_ref[...] = v * _sigmoid(v)

    return pl.pallas_call(body, name="cond", out_shape=jax.ShapeDtypeStruct(c.shape, F32))(c)


def _mod_slab(cond_all, w_ada, b_slab):
    def body(c_ref, w_ref, b_ref, o_ref):
        o_ref[...] = _dot(c_ref[...].astype(BF16), w_ref[0].astype(BF16), ((1,), (0,))) + b_ref[...]

    return pl.pallas_call(body, name="mod_slab",
                          out_shape=jax.ShapeDtypeStruct((cond_all.shape[0], w_ada.shape[2]), F32),
                          compiler_params=pltpu.CompilerParams(vmem_limit_bytes=VMEM_LIMIT))(cond_all, w_ada, b_slab)


def _ada_grad(cond_all, dmod_cols):
    def body(c_ref, d_ref, o_ref):
        o_ref[...] = _dot(c_ref[...].astype(BF16), d_ref[...].astype(BF16), ((0,), (0,)))

    return pl.pallas_call(body, name="ada_grad",
                          out_shape=jax.ShapeDtypeStruct((cond_all.shape[1], dmod_cols.shape[1]), F32),
                          compiler_params=pltpu.CompilerParams(vmem_limit_bytes=VMEM_LIMIT))(cond_all, dmod_cols)


def _adam_math(g, w, m, v):
    m2 = B1 * m + (1.0 - B1) * g
    v2 = B2 * v + (1.0 - B2) * (g * g)
    m_hat = m2 / (1.0 - B1 ** STEP)
    v_hat = v2 / (1.0 - B2 ** STEP)
    return -LR * (m_hat / (jnp.sqrt(v_hat) + ADAM_EPS) + WD * w), m2, v2


def _adamw(parts, w, m, v, name):
    n, rows, cols = parts.shape
    tr = _pick(rows, (128, 96, 64, 32, 16, rows))

    def body(p_ref, w_ref, m_ref, v_ref, g_ref, d_ref, m2_ref, v2_ref):
        g = p_ref[0].astype(F32)
        for s in range(1, n):
            g = g + p_ref[s].astype(F32)
        g_ref[0] = g
        d_ref[0], m2_ref[0], v2_ref[0] = _adam_math(g, w_ref[0], m_ref[0], v_ref[0])

    blk = pl.BlockSpec((1, tr, cols), lambda i: (0, i, 0))
    return pl.pallas_call(
        body, name=name, grid=(rows // tr,),
        in_specs=[pl.BlockSpec((n, tr, cols), lambda i: (0, i, 0)), blk, blk, blk],
        out_specs=[blk] * 4, out_shape=[jax.ShapeDtypeStruct((1, rows, cols), F32)] * 4,
        compiler_params=_cp("parallel"),
    )(*[pltpu.with_memory_space_constraint(a, pltpu.HBM) for a in (parts, w, m, v)])


ROW_PARAMS = (("g_norm1", D), ("g_cq", Q_LORA), ("g_ckv", KV_LORA), ("g_out_a", D_A), ("g_out_b", D_A), ("g_norm2", D),
              ("g_final", D))
LOSS_ROW = N_MOD + len(ROW_PARAMS)
PAY_ROWS = 16
NCOL = N_MOD * D // N_DEV


def _pack_small(dmods, rows, loss_cols):
    nb = dmods[0].shape[0]
    nrow = len(ROW_PARAMS)

    def body(*refs):
        dm, rw, loss_ref, pay_ref, blk_ref = refs[:N_MOD], refs[N_MOD:N_MOD + nrow], refs[N_MOD + nrow], refs[-2], refs[-1]
        pay_ref[...] = jnp.zeros_like(pay_ref)
        for k in range(N_MOD):
            tot = dm[k][0]
            for b in range(1, nb):
                tot = tot + dm[k][b]
            pay_ref[k:k + 1, :] = tot
        for i, (_, n) in enumerate(ROW_PARAMS):
            pay_ref[N_MOD + i:N_MOD + i + 1, :n] = rw[i][...]
        pay_ref[LOSS_ROW:LOSS_ROW + 1, :] = loss_ref[...]
        for j in range(N_DEV):
            done = 0
            while done < NCOL:
                seg, off = divmod(j * NCOL + done, D)
                ln = min(NCOL - done, D - off)
                for b in range(nb):
                    blk_ref[j, b:b + 1, done:done + ln] = dm[seg][b][:, off:off + ln]
                done += ln

    return pl.pallas_call(
        body, name="pack_small",
        out_shape=[jax.ShapeDtypeStruct((PAY_ROWS, D), F32), jax.ShapeDtypeStruct((N_DEV, nb, NCOL), F32)],
    )(*dmods, *rows, loss_cols)


def _small_update(pay, rel, ws, ms, vs):
    n_par = len(ws)

    def body(*refs):
        pay_ref, rel_ref = refs[:2]
        w_refs, m_refs, v_refs = (refs[2 + s * n_par:2 + (s + 1) * n_par] for s in range(3))
        outs, loss_ref = refs[2 + 3 * n_par:-1], refs[-1]
        tot, rtot = pay_ref[0], rel_ref[0]
        for s in range(1, N_DEV):
            tot, rtot = tot + pay_ref[s], rtot + rel_ref[s]

        def update(p, g, sl):
            outs[4 * p][:, sl] = g
            outs[4 * p + 1][:, sl], outs[4 * p + 2][:, sl], outs[4 * p + 3][:, sl] = _adam_math(
                g, w_refs[p][:, sl], m_refs[p][:, sl], v_refs[p][:, sl])

        for k in range(N_MOD):
            update(0, tot[k:k + 1, :], slice(k * D, (k + 1) * D))
        for i, (_, n) in enumerate(ROW_PARAMS):
            update(1 + i, tot[N_MOD + i:N_MOD + i + 1, :n], slice(0, n))
        update(n_par - 1, rtot, slice(0, H))
        loss_ref[...] = jnp.broadcast_to((0.5 / D) * jnp.sum(tot[LOSS_ROW:LOSS_ROW + 1, :]), loss_ref.shape)

    shapes = [jax.ShapeDtypeStruct(w.shape, F32) for w in ws for _ in range(4)]
    res = pl.pallas_call(
        body, name="small_update", out_shape=shapes + [jax.ShapeDtypeStruct((8, 128), F32)],
    )(pay, rel, *ws, *ms, *vs)
    return [tuple(res[4 * p:4 * p + 4]) for p in range(n_par)], res[-1]


def _cols_from_blocks(g):
    return jnp.transpose(g, (1, 0, 2)).reshape(g.shape[1], N_DEV * g.shape[2])


def _cols_to_blocks(w):
    r, c = w.shape
    return jnp.transpose(w.reshape(r, N_DEV, c // N_DEV), (1, 0, 2))


def _pad_w_in(wt):
    z = jnp.zeros((NOPE, wt.shape[1]), wt.dtype)
    return jnp.concatenate([wt[:P_IN - ROPE], z, wt[P_IN - ROPE:], z[:HP - NOPE - ROPE]], axis=0)


def _unpad_w_in(gt):
    k0 = P_IN - ROPE + NOPE
    return jnp.concatenate([gt[:P_IN - ROPE], gt[k0:k0 + ROPE]], axis=0)


def _pad_w_uq(w):
    w3 = w.reshape(Q_LORA, H, NOPE + ROPE)
    return jnp.pad(w3, ((0, 0), (0, 0), (0, HP - NOPE - ROPE))).reshape(Q_LORA, H * HP)


def _unpad_w_uq(g):
    return g.reshape(Q_LORA, H, HP)[:, :, :NOPE + ROPE].reshape(Q_LORA, H * (NOPE + ROPE))


def _split_w_ukv(w):
    w4 = w.reshape(KV_LORA, H // 2, 2, HP)
    z = jnp.zeros((KV_LORA, H // 2, NOPE), w.dtype)
    kn, vv = w4[..., :NOPE], w4[..., NOPE:]
    w_k = jnp.stack([jnp.concatenate([kn[:, :, 0], z], -1), jnp.concatenate([kn[:, :, 1], z], -1)], axis=2)
    w_v = jnp.stack([jnp.concatenate([vv[:, :, 0], z], -1), jnp.concatenate([z, vv[:, :, 1]], -1)], axis=2)
    return w_k.reshape(KV_LORA, H * HP), w_v.reshape(KV_LORA, H * HP)


def _join_w_ukv(g_k, g_v):
    gk = g_k.reshape(KV_LORA, H // 2, 2, HP)
    gv = g_v.reshape(KV_LORA, H // 2, 2, HP)
    even = jnp.concatenate([gk[:, :, 0, :NOPE], gv[:, :, 0, :VDIM]], -1)
    odd = jnp.concatenate([gk[:, :, 1, :NOPE], gv[:, :, 1, VDIM:]], -1)
    return jnp.stack([even, odd], axis=2).reshape(KV_LORA, H * HP)


def _rope_tables():
    half = ROPE // 2
    inv = ROPE_THETA ** (-jnp.arange(half, dtype=F32) / half)
    ang = jnp.arange(S, dtype=F32)[:, None] * inv[None, :]
    cos, sin = jnp.cos(ang), jnp.sin(ang)
    ones, zeros = jnp.ones((S, NOPE), F32), jnp.zeros((S, NOPE), F32)
    tail1, tail0 = jnp.ones((S, HP - NOPE - ROPE), F32), jnp.zeros((S, HP - NOPE - ROPE), F32)
    zh = jnp.zeros((S, half), F32)
    c = jnp.concatenate([ones, cos, cos, tail1], axis=1)
    sm = jnp.concatenate([zeros, -sin, zh, tail0], axis=1)
    sp = jnp.concatenate([zeros, zh, sin, tail0], axis=1)
    return c, sm, sp


def _local_step(x, mod, target, g_norm1, w_in_p, g_cq, w_uq_p, g_ckv, w_k, w_v, rel_bias, g_out_a, g_out_b, w_out,
                g_norm2, w_ffn_in, w_ffn_out, g_final, late_weights=None, on_ffn_grads=None, on_last_grads=None):
    nb = x.shape[0] // S
    sh1, sc1, g1, sh2, sc2, g2 = (mod[:, n].reshape(nb, 1, D) for n in range(N_MOD))
    rc, rsm, rsp = _rope_tables()
    biasm = _band_bias(rel_bias)

    h1 = _pre1(x, g_norm1, sc1, sh1)
    proj = _mm_nt(h1, w_in_p, F32, "proj")
    q, k, v, cqn, ckvn = _mla_pre(proj, g_cq, g_ckv, w_uq_p, w_k, w_v, rc, rsm, rsp)
    out_b, lse_b = _mla_fwd(q, k, v)
    out_a, lse_a = _dil_fwd(proj, biasm)
    y = _post_attn(out_a, out_b, g_out_a, g_out_b)
    if late_weights is not None:
        w_out, w_ffn_in, w_ffn_out = late_weights(y)
    mix = _mm_nn(y, w_out, BF16, "mix")
    x2, h2 = _resid_norm2(x, mix, g1, g_norm2, sc2, sh2)
    ffn_g, ffn_u, act = _ffn_in(h2, w_ffn_in)
    f = _mm_nn(act, w_ffn_out, BF16, "ffn_out")
    dx3, df, loss_cols, dg_final, dg2 = _final(x2, f, g2, g_final, target)

    dg_, du_ = _d_act(df, w_ffn_out, ffn_g, ffn_u)
    gw_ffn_out = _mm_tn_rows([act], df, "gw_ffn_out")
    dh2 = _d_h2(dg_, du_, w_ffn_in)
    gw_ffn_in = _mm_tn_rows([dg_, du_], h2, "gw_ffn_in")
    dx2, dsh2, dsc2, dg_norm2, dg1, dmix = _norm_bwd(x2, dh2, dx3, g_norm2, sc2, gate=(mix, g1))
    dy = _mm_nt(dmix, w_out, BF16, "d_y")
    gw_out = _mm_tn(y, [dmix], "gw_out")
    if on_ffn_grads is not None:
        g_out_a = g_out_a + on_ffn_grads(gw_ffn_in, gw_ffn_out, gw_out)
    dout_a, dout_b, dg_out_a, dg_out_b = _post_attn_bwd(dy, out_a, out_b, g_out_a, g_out_b)
    dq_b, dk_b, dv_b = _mla_bwd(q, k, v, out_b, dout_b, lse_b)
    dq_a, dk_a, dv_a, dlogits = _dil_bwd(proj, biasm, out_a, dout_a, lse_a)
    g_rel = _rel_bias_grad(dlogits)
    dqr, dproj, dg_cq, dg_ckv = _mla_pre_bwd(proj, dq_b, dk_b, dv_b, (dq_a, dk_a, dv_a), g_cq, g_ckv, w_uq_p, w_k, w_v,
                                             rc, rsm, rsp)
    gw_uq = _mm_tn(cqn, [dqr], "gw_uq")
    gw_k, gw_v = _mm_tn(ckvn, [dk_b, dv_b], "gw_kv")
    gw_in = _mm_tn_rows([dproj], h1, "gw_in")
    if on_last_grads is not None:
        started = on_last_grads(dict(w_in=gw_in, w_uq=gw_uq, w_k=gw_k, w_v=gw_v))
    else:
        started = None
    dh1 = _mm_nn(dproj, w_in_p, BF16, "d_h1", after=started)
    grad_x, dsh1, dsc1, dg_norm1 = _norm_bwd(x, dh1, dx2, g_norm1, sc1)

    dmod = [dsh1, dsc1, dg1, dsh2, dsc2, dg2]
    small = dict(g_norm1=dg_norm1, g_cq=dg_cq, g_ckv=dg_ckv, rel_bias=g_rel, g_out_a=dg_out_a, g_out_b=dg_out_b,
                 g_norm2=dg_norm2, g_final=dg_final)
    big = dict(w_in=gw_in, w_uq=gw_uq, w_k=gw_k, w_v=gw_v, w_out=gw_out, w_ffn_in=gw_ffn_in, w_ffn_out=gw_ffn_out)
    return grad_x, dmod, loss_cols, small, big


def kernel(x, c, w_ada, b_ada, g_norm1, w_in, g_cq, w_uq, g_ckv, w_ukv, rel_bias, g_out_a, g_out_b, w_out, g_norm2, w_ffn_in, w_ffn_out, g_final, loss_target, m_w_ada, m_b_ada, m_g_norm1, m_w_in, m_g_cq, m_w_uq, m_g_ckv, m_w_ukv, m_rel_bias, m_g_out_a, m_g_out_b, m_w_out, m_g_norm2, m_w_ffn_in, m_w_ffn_out, m_g_final, v_w_ada, v_b_ada, v_g_norm1, v_w_in, v_g_cq, v_w_uq, v_g_ckv, v_w_ukv, v_rel_bias, v_g_out_a, v_g_out_b, v_w_out, v_g_norm2, v_w_ffn_in, v_w_ffn_out, v_g_final):
    nb = x.shape[0]
    t = nb * S
    xt, tt = x.reshape(t, D), loss_target.reshape(t, D)
    me = 4 * lax.axis_index("x") + 2 * lax.axis_index("y") + lax.axis_index("c")

    early = [jnp.swapaxes(w_in, 1, 2)[0], w_uq[0], w_ukv[0]]
    gathered = _gather_two_level([_silu_rows(c)] + [s.astype(BF16) for s in early], "gather_weights")
    cond_all = gathered[0].reshape(N_DEV * nb, D)
    w_in_t = gathered[1].reshape(P_IN, D)
    w_uq_f, w_ukv_f = (_cols_from_blocks(g) for g in gathered[2:4])
    w_k, w_v = _split_w_ukv(w_ukv_f)

    ncol = N_MOD * D // N_DEV
    b_slab = lax.dynamic_slice(b_ada, (0, me * ncol), (1, ncol))
    slab = _mod_slab(cond_all, w_ada, b_slab)
    (mod_rows,) = _exchange([slab.reshape(N_DEV, nb, ncol)], [False], "scatter_mod")
    mod = jnp.transpose(mod_rows, (1, 0, 2)).reshape(nb, N_MOD, D)

    late = [s.astype(BF16) for s in (w_out[0], jnp.swapaxes(w_ffn_in, 1, 2)[0], w_ffn_out[0])]
    late_send, late_recv, late_src, late_land, late_token = _exchange_start(
        late, [_own_block_in_place(s, me) for s in late], [True] * 3, mod_rows, "gather_late_start")
    g_norm1_t = g_norm1 + late_token[:1, :1]

    def late_weights(after):
        w_out_g, w_ffn_in_g, w_ffn_out_g = _exchange_wait(late_send, late_recv, late_src, late_land, [True] * 3, after,
                                                          "gather_late_wait")
        return w_out_g.reshape(D, D), w_ffn_in_g.reshape(2 * D_FF, D), w_ffn_out_g.reshape(D_FF, D)

    flight = {}

    def start_grads(key, src, name):
        land = [_own_block_in_place(lax.dynamic_index_in_dim(s, me, 0, keepdims=False), me) for s in src]
        send, recv, src, land, token = _exchange_start(src, land, [False] * len(src), src[0], name)
        flight[key] = (send, recv, src, land)
        return token[:1, :1]

    def on_ffn_grads(gw_ffn_in, gw_ffn_out, gw_out):
        return start_grads("ffn", [gw_ffn_in.reshape(N_DEV, 2 * D_FF // N_DEV, D), gw_ffn_out.reshape(N_DEV, D_FF // N_DEV, D),
                                   gw_out.reshape(N_DEV, D // N_DEV, D)], "exchange_ffn_start")

    def on_last_grads(gw):
        return start_grads("rest", [_unpad_w_in(gw["w_in"]).reshape(N_DEV, P_IN // N_DEV, D),
                                    _cols_to_blocks(_unpad_w_uq(gw["w_uq"])),
                                    _cols_to_blocks(_join_w_ukv(gw["w_k"], gw["w_v"]))], "exchange_rest_start")

    grad_x, dmod, loss_cols, small, _ = _local_step(
        xt, mod, tt, g_norm1_t, _pad_w_in(w_in_t), g_cq, _pad_w_uq(w_uq_f), g_ckv, w_k, w_v, rel_bias, g_out_a, g_out_b,
        None, g_norm2, None, None, g_final.reshape(1, D), late_weights=late_weights, on_ffn_grads=on_ffn_grads,
        on_last_grads=on_last_grads)

    upd = {}

    def land_and_update(key, names, after, name):
        got = _exchange_wait(*flight[key], [False] * len(names), after, name)
        for n, p in zip(names, got):
            w, m, v = big[n]
            upd[n] = _adamw(p, w, m, v, "adamw_" + n)

    def flip(a):
        return jnp.swapaxes(a, 1, 2)

    big = dict(w_in=(flip(w_in), flip(m_w_in), flip(v_w_in)), w_uq=(w_uq, m_w_uq, v_w_uq), w_ukv=(w_ukv, m_w_ukv, v_w_ukv),
               w_out=(w_out, m_w_out, v_w_out), w_ffn_in=(flip(w_ffn_in), flip(m_w_ffn_in), flip(v_w_ffn_in)),
               w_ffn_out=(w_ffn_out, m_w_ffn_out, v_w_ffn_out))
    land_and_update("ffn", ["w_ffn_in", "w_ffn_out", "w_out"], grad_x, "exchange_ffn_wait")
    land_and_update("rest", ["w_in", "w_uq", "w_ukv"], upd["w_out"][0], "exchange_rest_wait")
    for n in ("w_in", "w_ffn_in"):
        upd[n] = tuple(flip(a) for a in upd[n])

    mine, dmod_blocks = _pack_small(dmod, [small[n] for n, _ in ROW_PARAMS], loss_cols)
    dmod_cols, pay, rel = _exchange([dmod_blocks, mine, small["rel_bias"]], [False, True, True], "exchange_small",
                                    after=upd["w_ukv"][0])
    g_ada = _ada_grad(cond_all, dmod_cols.reshape(N_DEV * nb, ncol))
    upd["w_ada"] = _adamw(g_ada[None], w_ada, m_w_ada, v_w_ada, "adamw_w_ada")
    row = lambda a: a.reshape(1, D)
    small_names = ["b_ada"] + [n for n, _ in ROW_PARAMS] + ["rel_bias"]
    small_w = [b_ada, g_norm1, g_cq, g_ckv, g_out_a, g_out_b, g_norm2, row(g_final), rel_bias]
    small_m = [m_b_ada, m_g_norm1, m_g_cq, m_g_ckv, m_g_out_a, m_g_out_b, m_g_norm2, row(m_g_final), m_rel_bias]
    small_v = [v_b_ada, v_g_norm1, v_g_cq, v_g_ckv, v_g_out_a, v_g_out_b, v_g_norm2, row(v_g_final), v_rel_bias]
    small_upd, loss8 = _small_update(pay, rel, small_w, small_m, small_v)
    upd.update(zip(small_names, small_upd))

    order = ["w_ada", "b_ada", "g_norm1", "w_in", "g_cq", "w_uq", "g_ckv", "w_ukv", "rel_bias", "g_out_a", "g_out_b",
             "w_out", "g_norm2", "w_ffn_in", "w_ffn_out", "g_final"]
    like = dict(g_final=g_final)
    outs = [loss8[0, 0], grad_x.reshape(x.shape)]
    for part in range(4):
        for n in order:
            val = upd[n][part]
            outs.append(val.reshape(like[n].shape) if n in like else val)
    return tuple(outs)
```

```python
import functools

import numpy as np
import jax
import jax.numpy as jnp
from jax import lax
from jax.experimental import pallas as pl
from jax.experimental.pallas import tpu as pltpu

F32, BF16 = jnp.float32, jnp.bfloat16

N_DEV = 8
D = 1024
S = 2048
H = 8
E_A = 64
D_A = H * E_A
Q_LORA, KV_LORA = 384, 256
NOPE, ROPE, VDIM = 64, 32, 64
HP = 128
P_IN = 3 * D_A + Q_LORA + KV_LORA + ROPE
P_PAD = 3 * D_A + Q_LORA + KV_LORA + HP
TAIL0 = 3 * D_A
TAIL = P_PAD - TAIL0
D_FF = 2816
N_MOD = 6
EPS = 1e-6
NEG = -1e30
BLK = 128
DILATIONS = (1, 4, 16)
N_BUCKETS, MAX_DISTANCE = 32, 2048
ROPE_THETA = 10000.0
SCALE_A = E_A ** -0.5
SCALE_B = (NOPE + ROPE) ** -0.5
B1, B2, LR, ADAM_EPS, WD, STEP = 0.9, 0.999, 0.001, 1e-8, 0.01, 10
VMEM_LIMIT = 56 * 1024 * 1024


def _cp(*sem):
    return pltpu.CompilerParams(dimension_semantics=sem, vmem_limit_bytes=VMEM_LIMIT)


def _pick(n, prefs):
    for p in prefs:
        if n % p == 0:
            return p
    raise ValueError(f"no tile of {prefs} divides {n}")


OPERAND_BYTES = 6 * 1024 * 1024


def _pick_rows(m, k):
    return _pick(m, [p for p in (1024, 512, 256, 128, 16) if p * k * 2 <= OPERAND_BYTES])


MATMUL_BYTES = 40 * 1024 * 1024


def _stream_rows(m, fixed, per_row):
    return _pick(m, [p for p in (4096, 2048, 1024, 512, 256, 128, 16) if fixed + p * per_row <= MATMUL_BYTES])


def _dot(a, b, dims):
    return lax.dot_general(a, b, (dims, ((), ())), preferred_element_type=F32)


def _mm_nn(a, b, out_dtype, name, after=None):
    m, k = a.shape
    n = b.shape[1]
    tn = _pick(n, (512, 384, 256, 128))
    tm = _stream_rows(m, 4 * k * tn, 4 * k + (2 * jnp.dtype(out_dtype).itemsize + 4) * tn)

    def body(a_ref, b_ref, *rest):
        o_ref = rest[-1]
        o_ref[...] = _dot(a_ref[...], b_ref[...], ((1,), (0,))).astype(o_ref.dtype)

    extra = [] if after is None else [after]
    return pl.pallas_call(
        body, name=name, grid=(m // tm, n // tn),
        in_specs=[pl.BlockSpec((tm, k), lambda i, j: (i, 0)), pl.BlockSpec((k, tn), lambda i, j: (0, j))] + [ANY] * len(extra),
        out_specs=pl.BlockSpec((tm, tn), lambda i, j: (i, j)),
        out_shape=jax.ShapeDtypeStruct((m, n), out_dtype),
        compiler_params=_cp("parallel", "parallel"),
    )(a, b, *extra)


def _mm_nt(a, b, out_dtype, name, after=None):
    m, k = a.shape
    n = b.shape[0]
    tn = _pick(n, (512, 384, 256, 128))
    tm = _stream_rows(m, 4 * k * tn, 4 * k + (2 * jnp.dtype(out_dtype).itemsize + 4) * tn)

    def body(a_ref, b_ref, *rest):
        o_ref = rest[-1]
        o_ref[...] = _dot(a_ref[...], b_ref[...], ((1,), (1,))).astype(o_ref.dtype)

    extra = [] if after is None else [after]
    return pl.pallas_call(
        body, name=name, grid=(m // tm, n // tn),
        in_specs=[pl.BlockSpec((tm, k), lambda i, j: (i, 0)), pl.BlockSpec((tn, k), lambda i, j: (j, 0))] + [ANY] * len(extra),
        out_specs=pl.BlockSpec((tm, tn), lambda i, j: (i, j)),
        out_shape=jax.ShapeDtypeStruct((m, n), out_dtype),
        compiler_params=_cp("parallel", "parallel"),
    )(a, b, *extra)


def _mm_tn(a, bs, name):
    t, m = a.shape
    n = bs[0].shape[1]
    nb_ = len(bs)
    tc = _pick(t, (512, 16))
    tn = _pick(n, (512, 384, 256, 128))
    tm = _pick(m, [p for p in (1024, 512, 384, 256, 128) if (3 * p + 2 * nb_ * tn) * t * 2 <= VMEM_LIMIT - 2 * OPERAND_BYTES])
    if tm <= 256 and nb_ * n * t * 2 <= 2 * OPERAND_BYTES:
        tn = n

    def body(*refs):
        a_ref, b_refs, o_refs, at_ref = refs[0], refs[1:1 + nb_], refs[1 + nb_:1 + 2 * nb_], refs[-1]

        @pl.when(pl.program_id(1) == 0)
        def _():
            def chunk(c, _):
                rows = pl.ds(pl.multiple_of(c * tc, tc), tc)
                at_ref[:, rows] = a_ref[rows, :].T
                return 0

            lax.fori_loop(0, t // tc, chunk, 0)

        for b_ref, o_ref in zip(b_refs, o_refs):
            o_ref[...] = _dot(at_ref[...], b_ref[...], ((1,), (0,))).astype(BF16)

    res = pl.pallas_call(
        body, name=name, grid=(m // tm, n // tn),
        in_specs=[pl.BlockSpec((t, tm), lambda i, j: (0, i))] + [pl.BlockSpec((t, tn), lambda i, j: (0, j))] * nb_,
        out_specs=[pl.BlockSpec((tm, tn), lambda i, j: (i, j))] * nb_,
        out_shape=[jax.ShapeDtypeStruct((m, n), BF16)] * nb_,
        scratch_shapes=[pltpu.VMEM((tm, t), BF16)],
        compiler_params=_cp("parallel", "arbitrary"),
    )(a, *bs)
    return res[0] if nb_ == 1 else res


def _mm_tn_rows(a_list, b, name):
    t, m = a_list[0].shape
    n = b.shape[1]
    na = len(a_list)
    tc, tm = _pick(t, (512, 16)), _pick(m, (256, 128))
    nblk = m // tm

    def body(*refs):
        a_refs, b_ref, o_ref, bt_ref, r_ref = refs[:na], refs[na], refs[na + 1], refs[na + 2], refs[na + 3]
        i = pl.program_id(0)

        @pl.when(i == 0)
        def _():
            def chunk(c, _):
                rows = pl.ds(pl.multiple_of(c * tc, tc), tc)
                bt_ref[:, rows] = b_ref[rows, :].T
                return 0

            lax.fori_loop(0, t // tc, chunk, 0)

        for s, a_ref in enumerate(a_refs):
            @pl.when((i >= s * nblk) & (i < (s + 1) * nblk))
            def _(a_ref=a_ref):
                r_ref[...] = _dot(bt_ref[...], a_ref[...], ((1,), (0,)))
                o_ref[...] = r_ref[...].T.astype(BF16)

    return pl.pallas_call(
        body, name=name, grid=(na * nblk,),
        in_specs=[pl.BlockSpec((t, tm), lambda i, s=s: (0, jnp.clip(i - s * nblk, 0, nblk - 1))) for s in range(na)]
        + [pl.BlockSpec((t, n), lambda i: (0, 0))],
        out_specs=pl.BlockSpec((tm, n), lambda i: (i, 0)),
        out_shape=jax.ShapeDtypeStruct((na * m, n), BF16),
        scratch_shapes=[pltpu.VMEM((n, t), BF16), pltpu.VMEM((n, tm), F32)],
        compiler_params=_cp("arbitrary"),
    )(*a_list, b)


EPI = 256


def _silu_parts(g):
    sg = 0.5 * jnp.tanh(0.5 * g) + 0.5
    return sg, g * sg


def _ffn_in(h2, wt):
    t, k = h2.shape
    tn = _pick(D_FF, (256, 128))
    tm = _stream_rows(t, 8 * k * tn, 4 * k + (3 * 2 * 2 + 2 * 4) * tn)
    nj = D_FF // tn

    def body(h_ref, wg_ref, wu_ref, g_ref, u_ref, a_ref):
        hv = h_ref[...]
        g_all = _dot(hv, wg_ref[...], ((1,), (1,)))
        u_all = _dot(hv, wu_ref[...], ((1,), (1,)))
        for r in range(tm // EPI):
            rows = slice(r * EPI, (r + 1) * EPI)
            g, u = g_all[rows], u_all[rows]
            g_ref[rows, :] = g.astype(BF16)
            u_ref[rows, :] = u.astype(BF16)
            a_ref[rows, :] = (_silu_parts(g)[1] * u).astype(BF16)

    blk = pl.BlockSpec((tm, tn), lambda i, j: (i, j))
    return pl.pallas_call(
        body, name="ffn_in", grid=(t // tm, nj),
        in_specs=[pl.BlockSpec((tm, k), lambda i, j: (i, 0)), pl.BlockSpec((tn, k), lambda i, j: (j, 0)),
                  pl.BlockSpec((tn, k), lambda i, j: (j + nj, 0))],
        out_specs=[blk] * 3, out_shape=[jax.ShapeDtypeStruct((t, D_FF), BF16)] * 3,
        compiler_params=_cp("parallel", "parallel"),
    )(h2, wt, wt)


def _d_act(df, w, g, u):
    t, k = df.shape
    tn = _pick(D_FF, (256, 128))
    tm = _stream_rows(t, 4 * k * tn, 4 * k + (4 * 2 * 2 + 4) * tn)

    def body(df_ref, w_ref, g_ref, u_ref, dg_ref, du_ref):
        da_all = _dot(df_ref[...], w_ref[...], ((1,), (1,)))
        for r in range(tm // EPI):
            rows = slice(r * EPI, (r + 1) * EPI)
            da = da_all[rows]
            gv = g_ref[rows, :].astype(F32)
            sg, silu = _silu_parts(gv)
            dg_ref[rows, :] = ((da * u_ref[rows, :].astype(F32)) * (sg + silu * (1.0 - sg))).astype(BF16)
            du_ref[rows, :] = (da * silu).astype(BF16)

    blk = pl.BlockSpec((tm, tn), lambda i, j: (i, j))
    return pl.pallas_call(
        body, name="d_act", grid=(t // tm, D_FF // tn),
        in_specs=[pl.BlockSpec((tm, k), lambda i, j: (i, 0)), pl.BlockSpec((tn, k), lambda i, j: (j, 0)), blk, blk],
        out_specs=[blk] * 2, out_shape=[jax.ShapeDtypeStruct((t, D_FF), BF16)] * 2,
        compiler_params=_cp("parallel", "parallel"),
    )(df, w, g, u)


def _d_h2(dg, du, wt):
    t = dg.shape[0]
    n = wt.shape[1]
    tm, tn = _pick_rows(t, D_FF), _pick(n, (512, 256, 128))

    def body(dg_ref, du_ref, wg_ref, wu_ref, o_ref):
        o_ref[...] = (_dot(dg_ref[...], wg_ref[...], ((1,), (0,)))
                      + _dot(du_ref[...], wu_ref[...], ((1,), (0,)))).astype(BF16)

    return pl.pallas_call(
        body, name="d_h2", grid=(t // tm, n // tn),
        in_specs=[pl.BlockSpec((tm, D_FF), lambda i, j: (i, 0)), pl.BlockSpec((tm, D_FF), lambda i, j: (i, 0)),
                  pl.BlockSpec((D_FF, tn), lambda i, j: (0, j)), pl.BlockSpec((D_FF, tn), lambda i, j: (1, j))],
        out_specs=pl.BlockSpec((tm, tn), lambda i, j: (i, j)),
        out_shape=jax.ShapeDtypeStruct((t, n), BF16),
        compiler_params=_cp("parallel", "parallel"),
    )(dg, du, wt, wt)


TM = 512


def _row(w):
    return pl.BlockSpec((TM, w), lambda i: (i, 0))


def _row_at(w, col):
    return pl.BlockSpec((TM, w), lambda i: (i, col))


def _vec(w):
    return pl.BlockSpec((1, w), lambda i: (0, 0))


def _per_ex(w):
    return pl.BlockSpec((1, 1, w), lambda i: (i // (S // TM), 0, 0))


def _pos(w):
    return pl.BlockSpec((TM, w), lambda i: (i % (S // TM), 0))


def _full(shape):
    return pl.BlockSpec(shape, lambda i: (0,) * len(shape))


def _rms(x):
    return lax.rsqrt(jnp.mean(x * x, axis=-1, keepdims=True) + EPS)


def _rms_bwd(n, r, dn):
    return r * (dn - n * jnp.mean(dn * n, axis=-1, keepdims=True))


def _colsum(v):
    return jnp.sum(v, axis=0, keepdims=True)


def _acc_first(i, ref, val, every=None):
    first = (i == 0) if every is None else (i % every == 0)

    @pl.when(first)
    def _():
        ref[...] = jnp.zeros_like(ref)

    ref[...] += val.reshape(ref.shape)


def _pre1(x, g, sc, sh):
    t = x.shape[0]

    def body(x_ref, g_ref, sc_ref, sh_ref, h_ref):
        xv = x_ref[...]
        n = xv * _rms(xv)
        h_ref[...] = ((n * g_ref[...]) * (1.0 + sc_ref[0]) + sh_ref[0]).astype(BF16)

    return pl.pallas_call(
        body, name="pre1", grid=(t // TM,),
        in_specs=[_row(D), _vec(D), _per_ex(D), _per_ex(D)],
        out_specs=_row(D), out_shape=jax.ShapeDtypeStruct((t, D), BF16),
        compiler_params=_cp("parallel"),
    )(x, g, sc, sh)


def _rope_fwd(v, c, sm, sp):
    return v * c + pltpu.roll(v, HP - ROPE // 2, 1) * sm + pltpu.roll(v, ROPE // 2, 1) * sp


def _rope_bwd(dv, c, sm, sp):
    return dv * c + pltpu.roll(dv * sm, ROPE // 2, 1) + pltpu.roll(dv * sp, HP - ROPE // 2, 1)


def _mla_pre(proj, g_cq, g_ckv, w_uq, w_k, w_v, rc, rsm, rsp):
    t = proj.shape[0]

    def body(tail_ref, gq_ref, gkv_ref, wuq_ref, wk_ref, wv_ref, c_ref, sm_ref, sp_ref,
             q_ref, k_ref, v_ref, cqn_ref, ckvn_ref):
        tail = tail_ref[...]
        cq, ckv, kr = tail[:, :Q_LORA], tail[:, Q_LORA:Q_LORA + KV_LORA], tail[:, Q_LORA + KV_LORA:]
        cqn = (cq * _rms(cq) * gq_ref[...]).astype(BF16)
        ckvn = (ckv * _rms(ckv) * gkv_ref[...]).astype(BF16)
        cqn_ref[...] = cqn
        ckvn_ref[...] = ckvn
        c, sm, sp = c_ref[...], sm_ref[...], sp_ref[...]
        q = _dot(cqn, wuq_ref[...], ((1,), (0,)))
        kn = _dot(ckvn, wk_ref[...], ((1,), (0,)))
        v_ref[...] = _dot(ckvn, wv_ref[...], ((1,), (0,))).astype(BF16)
        krr = _rope_fwd(kr, c, sm, sp)
        for h in range(H):
            sl = slice(h * HP, (h + 1) * HP)
            q_ref[:, sl] = _rope_fwd(q[:, sl], c, sm, sp).astype(BF16)
            k_ref[:, sl] = (kn[:, sl] + krr).astype(BF16)

    wide = H * HP
    return pl.pallas_call(
        body, name="mla_pre", grid=(t // TM,),
        in_specs=[_row_at(TAIL, TAIL0 // TAIL), _vec(Q_LORA), _vec(KV_LORA), _full((Q_LORA, wide)),
                  _full((KV_LORA, wide)), _full((KV_LORA, wide)), _pos(HP), _pos(HP), _pos(HP)],
        out_specs=[_row(wide), _row(wide), _row(wide), _row(Q_LORA), _row(KV_LORA)],
        out_shape=[jax.ShapeDtypeStruct((t, wide), BF16)] * 3
        + [jax.ShapeDtypeStruct((t, Q_LORA), BF16), jax.ShapeDtypeStruct((t, KV_LORA), BF16)],
        compiler_params=_cp("parallel"),
    )(proj, g_cq, g_ckv, w_uq, w_k, w_v, rc, rsm, rsp)


def _mla_pre_bwd(proj, dq_, dk_, dv_, dqkv_a, g_cq, g_ckv, w_uq, w_k, w_v, rc, rsm, rsp):
    t = proj.shape[0]
    wide = H * HP

    def body(tail_ref, dq_ref, dk_ref, dv_ref, dqa_ref, dka_ref, dva_ref, gq_ref, gkv_ref, wuq_ref, wk_ref, wv_ref,
             c_ref, sm_ref, sp_ref, dqo_ref, dproj_ref, dgq_ref, dgkv_ref):
        i = pl.program_id(0)
        for n, src in enumerate((dqa_ref, dka_ref, dva_ref)):
            dproj_ref[:, n * D_A:(n + 1) * D_A] = src[...]
        dtail_ref = dproj_ref.at[:, TAIL0:]
        tail = tail_ref[...]
        cq, ckv = tail[:, :Q_LORA], tail[:, Q_LORA:Q_LORA + KV_LORA]
        c, sm, sp = c_ref[...], sm_ref[...], sp_ref[...]
        dkr = jnp.zeros((TM, HP), F32)
        for h in range(H):
            sl = slice(h * HP, (h + 1) * HP)
            dqo_ref[:, sl] = _rope_bwd(dq_ref[:, sl].astype(F32), c, sm, sp).astype(BF16)
            dkr = dkr + dk_ref[:, sl].astype(F32)
        lane = lax.broadcasted_iota(jnp.int32, (TM, HP), 1)
        dkr = jnp.where((lane >= NOPE) & (lane < NOPE + ROPE), _rope_bwd(dkr, c, sm, sp), 0.0)
        dkb = dk_ref[...]
        dvb = dv_ref[...]
        dcqn = _dot(dqo_ref[...], wuq_ref[...], ((1,), (1,)))
        dckvn = _dot(dkb, wk_ref[...], ((1,), (1,))) + _dot(dvb, wv_ref[...], ((1,), (1,)))
        rq, rkv = _rms(cq), _rms(ckv)
        nq, nkv = cq * rq, ckv * rkv
        _acc_first(i, dgq_ref, _colsum(dcqn * nq))
        _acc_first(i, dgkv_ref, _colsum(dckvn * nkv))
        dtail_ref[:, :Q_LORA] = _rms_bwd(nq, rq, dcqn * gq_ref[...]).astype(BF16)
        dtail_ref[:, Q_LORA:Q_LORA + KV_LORA] = _rms_bwd(nkv, rkv, dckvn * gkv_ref[...]).astype(BF16)
        dtail_ref[:, Q_LORA + KV_LORA:] = dkr.astype(BF16)

    return pl.pallas_call(
        body, name="mla_pre_bwd", grid=(t // TM,),
        in_specs=[_row_at(TAIL, TAIL0 // TAIL), _row(wide), _row(wide), _row(wide), _row(D_A), _row(D_A), _row(D_A),
                  _vec(Q_LORA), _vec(KV_LORA), _full((Q_LORA, wide)), _full((KV_LORA, wide)), _full((KV_LORA, wide)),
                  _pos(HP), _pos(HP), _pos(HP)],
        out_specs=[_row(wide), _row(P_PAD), _vec(Q_LORA), _vec(KV_LORA)],
        out_shape=[jax.ShapeDtypeStruct((t, wide), BF16), jax.ShapeDtypeStruct((t, P_PAD), BF16),
                   jax.ShapeDtypeStruct((1, Q_LORA), F32), jax.ShapeDtypeStruct((1, KV_LORA), F32)],
        compiler_params=_cp("arbitrary"),
    )(proj, dq_, dk_, dv_, *dqkv_a, g_cq, g_ckv, w_uq, w_k, w_v, rc, rsm, rsp)


def _post_attn(out_a, out_b, g_a, g_b):
    t = out_a.shape[0]

    def body(a_ref, b_ref, ga_ref, gb_ref, y_ref):
        a, b = a_ref[...], b_ref[...]
        y_ref[:, :D_A] = (a * _rms(a) * ga_ref[...]).astype(BF16)
        y_ref[:, D_A:] = (b * _rms(b) * gb_ref[...]).astype(BF16)

    return pl.pallas_call(
        body, name="post_attn", grid=(t // TM,),
        in_specs=[_row(D_A), _row(D_A), _vec(D_A), _vec(D_A)],
        out_specs=_row(D), out_shape=jax.ShapeDtypeStruct((t, D), BF16),
        compiler_params=_cp("parallel"),
    )(out_a, out_b, g_a, g_b)


def _post_attn_bwd(dy, out_a, out_b, g_a, g_b):
    t = dy.shape[0]

    def body(dy_ref, a_ref, b_ref, ga_ref, gb_ref, da_ref, db_ref, dga_ref, dgb_ref):
        i = pl.program_id(0)
        dy_ = dy_ref[...].astype(F32)
        for src, g_ref, dst, dg_ref, sl in ((a_ref, ga_ref, da_ref, dga_ref, slice(0, D_A)),
                                            (b_ref, gb_ref, db_ref, dgb_ref, slice(D_A, D))):
            v = src[...]
            r = _rms(v)
            n = v * r
            dyv = dy_[:, sl]
            _acc_first(i, dg_ref, _colsum(dyv * n))
            dst[...] = _rms_bwd(n, r, dyv * g_ref[...])

    return pl.pallas_call(
        body, name="post_attn_bwd", grid=(t // TM,),
        in_specs=[_row(D), _row(D_A), _row(D_A), _vec(D_A), _vec(D_A)],
        out_specs=[_row(D_A), _row(D_A), _vec(D_A), _vec(D_A)],
        out_shape=[jax.ShapeDtypeStruct((t, D_A), F32)] * 2 + [jax.ShapeDtypeStruct((1, D_A), F32)] * 2,
        compiler_params=_cp("arbitrary"),
    )(dy, out_a, out_b, g_a, g_b)


def _resid_norm2(x, mix, g1, g, sc, sh):
    t = x.shape[0]

    def body(x_ref, mix_ref, g1_ref, g_ref, sc_ref, sh_ref, x2_ref, h_ref):
        x2 = x_ref[...] + g1_ref[0] * mix_ref[...]
        x2_ref[...] = x2
        n = x2 * _rms(x2)
        h_ref[...] = ((n * g_ref[...]) * (1.0 + sc_ref[0]) + sh_ref[0]).astype(BF16)

    return pl.pallas_call(
        body, name="resid_norm2", grid=(t // TM,),
        in_specs=[_row(D), _row(D), _per_ex(D), _vec(D), _per_ex(D), _per_ex(D)],
        out_specs=[_row(D), _row(D)],
        out_shape=[jax.ShapeDtypeStruct((t, D), F32), jax.ShapeDtypeStruct((t, D), BF16)],
        compiler_params=_cp("parallel"),
    )(x, mix, g1, g, sc, sh)


def _sigmoid(v):
    return 1.0 / (1.0 + jnp.exp(-v))


def _final(x2, f, g2, g_fin, target):
    t = x2.shape[0]
    nb = t // S
    tpb = S // TM

    def body(x2_ref, f_ref, g2_ref, g_ref, t_ref, dx3_ref, df_ref, loss_ref, dgf_ref, dg2_ref):
        i = pl.program_id(0)
        fv = f_ref[...].astype(F32)
        x3 = x2_ref[...] + g2_ref[0] * fv
        r = _rms(x3)
        n = x3 * r
        err = n * g_ref[...] - t_ref[...]
        _acc_first(i, loss_ref, _colsum(err * err))
        dy = err * (1.0 / D)
        _acc_first(i, dgf_ref, _colsum(dy * n))
        dx3 = _rms_bwd(n, r, dy * g_ref[...])
        dx3_ref[...] = dx3
        _acc_first(i, dg2_ref, _colsum(dx3 * fv), every=tpb)
        df_ref[...] = (dx3 * g2_ref[0]).astype(BF16)

    return pl.pallas_call(
        body, name="final", grid=(t // TM,),
        in_specs=[_row(D), _row(D), _per_ex(D), _vec(D), _row(D)],
        out_specs=[_row(D), _row(D), _vec(D), _vec(D), _per_ex(D)],
        out_shape=[jax.ShapeDtypeStruct((t, D), F32), jax.ShapeDtypeStruct((t, D), BF16),
                   jax.ShapeDtypeStruct((1, D), F32), jax.ShapeDtypeStruct((1, D), F32),
                   jax.ShapeDtypeStruct((nb, 1, D), F32)],
        compiler_params=_cp("arbitrary"),
    )(x2, f, g2, g_fin, target)


def _norm_bwd(xin, dh, dres, g, sc, gate=None):
    t = xin.shape[0]
    nb = t // S
    tpb = S // TM
    gated = gate is not None

    def body(*refs):
        if gated:
            x_ref, dh_ref, dres_ref, g_ref, sc_ref, mix_ref, g1_ref, dx_ref, dsh_ref, dsc_ref, dg_ref, dg1_ref, dmix_ref = refs
        else:
            x_ref, dh_ref, dres_ref, g_ref, sc_ref, dx_ref, dsh_ref, dsc_ref, dg_ref = refs
        i = pl.program_id(0)
        xv, dhv = x_ref[...], dh_ref[...].astype(F32)
        r = _rms(xv)
        n = xv * r
        gv = g_ref[...]
        _acc_first(i, dsh_ref, _colsum(dhv), every=tpb)
        _acc_first(i, dsc_ref, _colsum(dhv * (n * gv)), every=tpb)
        dng = dhv * (1.0 + sc_ref[0])
        _acc_first(i, dg_ref, _colsum(dng * n))
        dx = dres_ref[...] + _rms_bwd(n, r, dng * gv)
        dx_ref[...] = dx
        if gated:
            _acc_first(i, dg1_ref, _colsum(dx * mix_ref[...].astype(F32)), every=tpb)
            dmix_ref[...] = (dx * g1_ref[0]).astype(BF16)

    in_specs = [_row(D), _row(D), _row(D), _vec(D), _per_ex(D)]
    out_specs = [_row(D), _per_ex(D), _per_ex(D), _vec(D)]
    out_shape = [jax.ShapeDtypeStruct((t, D), F32), jax.ShapeDtypeStruct((nb, 1, D), F32),
                 jax.ShapeDtypeStruct((nb, 1, D), F32), jax.ShapeDtypeStruct((1, D), F32)]
    args = [xin, dh, dres, g, sc]
    if gated:
        in_specs += [_row(D), _per_ex(D)]
        out_specs += [_per_ex(D), _row(D)]
        out_shape += [jax.ShapeDtypeStruct((nb, 1, D), F32), jax.ShapeDtypeStruct((t, D), BF16)]
        args += list(gate)
    return pl.pallas_call(
        body, name="norm2_bwd" if gated else "norm1_bwd", grid=(t // TM,),
        in_specs=in_specs, out_specs=out_specs, out_shape=out_shape,
        compiler_params=_cp("arbitrary"),
    )(*args)


TQ = 256
TB = 512
FWD_HEADS = 2


def _mla_fwd(q, k, v):
    t = q.shape[0]
    nb = t // S

    def body(q_ref, k_ref, v_ref, o_ref, lse_ref):
        causal = lax.broadcasted_iota(jnp.int32, (TB, TB), 0) >= lax.broadcasted_iota(jnp.int32, (TB, TB), 1)
        heads = [slice(h * HP, (h + 1) * HP) for h in range(FWD_HEADS)]
        for i in range(S // TB):
            ri, past = slice(i * TB, (i + 1) * TB), slice(0, i * TB)
            qhs = [q_ref[ri, sl] for sl in heads]
            sd = [jnp.where(causal, _dot(qh, k_ref[ri, sl], ((1,), (1,))) * SCALE_B, NEG) for qh, sl in zip(qhs, heads)]
            ms = [jnp.max(s, axis=-1, keepdims=True) for s in sd]
            if i:
                so = [_dot(qh, k_ref[past, sl], ((1,), (1,))) * SCALE_B for qh, sl in zip(qhs, heads)]
                ms = [jnp.maximum(m, jnp.max(s, axis=-1, keepdims=True)) for m, s in zip(ms, so)]
            pd = [jnp.exp(s - m) for s, m in zip(sd, ms)]
            ls = [jnp.sum(p, axis=-1, keepdims=True) for p in pd]
            acc = [_dot(p.astype(BF16), v_ref[ri, sl], ((1,), (0,))) for p, sl in zip(pd, heads)]
            if i:
                po = [jnp.exp(s - m) for s, m in zip(so, ms)]
                ls = [l + jnp.sum(p, axis=-1, keepdims=True) for l, p in zip(ls, po)]
                acc = [a + _dot(p.astype(BF16), v_ref[past, sl], ((1,), (0,))) for a, p, sl in zip(acc, po, heads)]
            for pr in range(FWD_HEADS // 2):
                o_ref[ri, pr * HP:(pr + 1) * HP] = acc[2 * pr] / ls[2 * pr] + acc[2 * pr + 1] / ls[2 * pr + 1]
            for sl, m, l in zip(heads, ms, ls):
                lse_ref[ri, sl] = jnp.broadcast_to(m + jnp.log(l), (TB, HP))

    wide2 = pl.BlockSpec((S, FWD_HEADS * HP), lambda b, p: (b, p))
    return pl.pallas_call(
        body, name="mla_fwd", grid=(nb, H // FWD_HEADS),
        in_specs=[wide2, wide2, wide2],
        out_specs=[pl.BlockSpec((S, FWD_HEADS // 2 * HP), lambda b, p: (b, p)), wide2],
        out_shape=[jax.ShapeDtypeStruct((t, H * VDIM), F32), jax.ShapeDtypeStruct((t, H * HP), F32)],
        compiler_params=_cp("parallel", "parallel"),
    )(q, k, v)


def _mla_bwd(q, k, v, o, do, lse):
    t = q.shape[0]
    nb = t // S

    def body(q_ref, k_ref, v_ref, o_ref, do_ref, lse_ref, dq_out, dk_out, dv_out, dq_ref, dk_ref, dv_ref):
        lane = lax.broadcasted_iota(jnp.int32, (TB, HP), 1)
        causal = lax.broadcasted_iota(jnp.int32, (TB, TB), 0) >= lax.broadcasted_iota(jnp.int32, (TB, TB), 1)
        heads = [slice(h * HP, (h + 1) * HP) for h in range(2)]
        nblk = S // TB
        for i in reversed(range(nblk)):
            ri, past = slice(i * TB, (i + 1) * TB), slice(0, i * TB)
            dov = do_ref[ri, :]
            prod = dov * o_ref[ri, :]
            dob = dov.astype(BF16)
            deltas = [jnp.sum(jnp.where((lane < VDIM) if h == 0 else (lane >= VDIM), prod, 0.0), axis=-1, keepdims=True)
                      for h in range(2)]
            qhs = [q_ref[ri, sl] for sl in heads]
            lses = [lse_ref[ri, sl][:, :1] for sl in heads]
            for rows, diagonal in ((ri, True), (past, False)):
                if rows.stop == rows.start:
                    continue
                ps = [jnp.exp(_dot(qh, k_ref[rows, sl], ((1,), (1,))) * SCALE_B - lse) for qh, sl, lse in zip(qhs, heads, lses)]
                if diagonal:
                    ps = [jnp.where(causal, p, 0.0) for p in ps]
                dps = [_dot(dob, v_ref[rows, sl], ((1,), (1,))) for sl in heads]
                dss = [(p * (dp - delta) * SCALE_B).astype(BF16) for p, dp, delta in zip(ps, dps, deltas)]
                for sl, qh, p, ds in zip(heads, qhs, ps, dss):
                    dq = _dot(ds, k_ref[rows, sl], ((1,), (0,)))
                    dk = _dot(ds, qh, ((0,), (0,)))
                    dv = _dot(p.astype(BF16), dob, ((0,), (0,)))
                    if diagonal:
                        dq_ref[ri, sl] = dq
                    else:
                        dq_ref[ri, sl] += dq
                    if i == nblk - 1:
                        dk_ref[rows, sl] = dk
                        dv_ref[rows, sl] = dv
                    else:
                        dk_ref[rows, sl] += dk
                        dv_ref[rows, sl] += dv
        dq_out[...] = dq_ref[...].astype(BF16)
        dk_out[...] = dk_ref[...].astype(BF16)
        dv_out[...] = dv_ref[...].astype(BF16)

    wide2 = pl.BlockSpec((S, 2 * HP), lambda b, p: (b, p))
    pair = pl.BlockSpec((S, HP), lambda b, p: (b, p))
    return pl.pallas_call(
        body, name="mla_bwd", grid=(nb, H // 2),
        in_specs=[wide2, wide2, wide2, pair, pair, wide2],
        out_specs=[wide2, wide2, wide2],
        out_shape=[jax.ShapeDtypeStruct((t, H * HP), BF16)] * 3,
        scratch_shapes=[pltpu.VMEM((S, 2 * HP), F32)] * 3,
        compiler_params=_cp("parallel", "parallel"),
    )(q, k, v, o, do, lse)


def _t5_bucket(dist):
    max_exact = N_BUCKETS // 2
    d = np.maximum(dist, 1).astype(np.float64)
    large = max_exact + (np.log(d / max_exact) / np.log(MAX_DISTANCE / max_exact) * (N_BUCKETS - max_exact)).astype(np.int64)
    large = np.minimum(large, N_BUCKETS - 1)
    return np.where(dist < max_exact, dist, large).astype(np.int32)


def _band_geometry():
    a = np.arange(BLK)[:, None]
    bk = np.arange(2 * BLK)[None, :]
    steps = BLK + a - bk
    valid = (steps >= 0) & (steps <= BLK)
    buckets = np.stack([_t5_bucket(np.clip(steps, 0, BLK) * d) for d in DILATIONS])
    return buckets, valid


def _band_bias(rel_bias):
    buckets, valid = _band_geometry()
    onehot = (jnp.asarray(buckets)[..., None] == jnp.arange(N_BUCKETS)).astype(F32)
    bias = jnp.einsum("rqkn,nh->rhqk", onehot, rel_bias, precision=lax.Precision.HIGHEST)
    bias = jnp.where(jnp.asarray(valid)[None, None], bias, NEG)
    return bias.reshape(3, H // 2, 2 * BLK, 2 * BLK)


def _dil_items():
    items = []
    for r, d in enumerate(DILATIONS):
        for res in range(d):
            for blk in range(S // d // BLK):
                items.append((r, d, blk * BLK * d + res, blk > 0))
    return items


GROUP = 4


def _strided(start, d):
    return pl.ds(start, BLK) if d == 1 else pl.ds(start, BLK, stride=d)


def _stack_heads(tile, own):
    return jnp.where(own, jnp.concatenate([tile, tile], axis=0), 0.0).astype(BF16)


def _own_lanes():
    row = lax.broadcasted_iota(jnp.int32, (2 * BLK, HP), 0)
    lane = lax.broadcasted_iota(jnp.int32, (2 * BLK, HP), 1)
    return (lane < E_A) == (row < BLK)


def _dil_fwd(proj, biasm):
    t = proj.shape[0]
    nb = t // S

    def body(q_ref, k_ref, v_ref, b_ref, o_ref, lse_ref, ob_ref, lb_ref):
        lane = lax.broadcasted_iota(jnp.int32, (BLK, HP), 1)
        own = _own_lanes()
        items = _dil_items()
        for g in range(0, len(items), GROUP):
            grp = items[g:g + GROUP]
            ss, vts = [], []
            for r, d, start, has_prev in grp:
                cur = _strided(start, d)
                rows = [_strided(start - BLK * d, d), cur] if has_prev else [cur]
                q2 = _stack_heads(q_ref[cur, :] * SCALE_A, own)
                kt = jnp.concatenate([k_ref[x, :] for x in rows], axis=0).astype(BF16)
                vts.append(jnp.concatenate([v_ref[x, :] for x in rows], axis=0).astype(BF16))
                bias = b_ref[r, 0] if has_prev else b_ref[r, 0, :, BLK:]
                ss.append(_dot(q2, kt, ((1,), (1,))) + bias)
            ms = [jnp.max(s, axis=-1, keepdims=True) for s in ss]
            ps = [jnp.exp(s - m) for s, m in zip(ss, ms)]
            ls = [jnp.sum(p, axis=-1, keepdims=True) for p in ps]
            for (r, d, start, _), p, vt, m, l in zip(grp, ps, vts, ms, ls):
                cur = _strided(start, d)
                o2 = _dot(p.astype(BF16), vt, ((1,), (0,))) / l
                lse2 = m + jnp.log(l)
                ob_ref[r, cur, :] = jnp.where(lane < E_A, o2[:BLK], o2[BLK:])
                lb_ref[r, cur, :] = jnp.where(lane < E_A, lse2[:BLK], lse2[BLK:])

        def merge(c, _):
            rows = pl.ds(pl.multiple_of(c * TQ, TQ), TQ)
            l0, l1, l2 = lb_ref[0, rows, :], lb_ref[1, rows, :], lb_ref[2, rows, :]
            m = jnp.maximum(jnp.maximum(l0, l1), l2)
            e0, e1, e2 = jnp.exp(l0 - m), jnp.exp(l1 - m), jnp.exp(l2 - m)
            tot = e0 + e1 + e2
            o_ref[rows, :] = (e0 * ob_ref[0, rows, :] + e1 * ob_ref[1, rows, :] + e2 * ob_ref[2, rows, :]) / tot
            lse_ref[rows, :] = m + jnp.log(tot)
            return 0

        lax.fori_loop(0, S // TQ, merge, 0)

    npair = H // 2
    return pl.pallas_call(
        body, name="dil_fwd", grid=(nb, npair),
        in_specs=[pl.BlockSpec((S, HP), lambda b, p: (b, p)), pl.BlockSpec((S, HP), lambda b, p: (b, npair + p)),
                  pl.BlockSpec((S, HP), lambda b, p: (b, 2 * npair + p)),
                  pl.BlockSpec((3, 1, 2 * BLK, 2 * BLK), lambda b, p: (0, p, 0, 0))],
        out_specs=[pl.BlockSpec((S, HP), lambda b, p: (b, p))] * 2,
        out_shape=[jax.ShapeDtypeStruct((t, D_A), F32)] * 2,
        scratch_shapes=[pltpu.VMEM((3, S, HP), F32), pltpu.VMEM((3, S, HP), F32)],
        compiler_params=_cp("parallel", "parallel"),
    )(proj, proj, proj, biasm)


def _dil_bwd(proj, biasm, o, do, lse):
    t = proj.shape[0]
    nb = t // S

    def body(q_ref, k_ref, v_ref, b_ref, o_ref, do_ref, lse_ref, dq_out, dk_out, dv_out, ds_ref, dq_ref, dk_ref, dv_ref):
        dq_ref[...] = jnp.zeros_like(dq_ref)
        dk_ref[...] = jnp.zeros_like(dk_ref)
        dv_ref[...] = jnp.zeros_like(dv_ref)
        ds_ref[...] = jnp.zeros_like(ds_ref)
        lane = lax.broadcasted_iota(jnp.int32, (BLK, HP), 1)
        own = _own_lanes()
        items = _dil_items()
        for g in range(0, len(items), GROUP):
            grp = items[g:g + GROUP]
            q2s, kts, do2s, ss, dps, lse2s, delta2s = [], [], [], [], [], [], []
            for r, d, start, has_prev in grp:
                cur = _strided(start, d)
                rows = [_strided(start - BLK * d, d), cur] if has_prev else [cur]
                q2 = _stack_heads(q_ref[cur, :] * SCALE_A, own)
                kt = jnp.concatenate([k_ref[x, :] for x in rows], axis=0).astype(BF16)
                vt = jnp.concatenate([v_ref[x, :] for x in rows], axis=0).astype(BF16)
                dot_ = do_ref[cur, :]
                prod = dot_ * o_ref[cur, :]
                lset = lse_ref[cur, :]
                do2 = _stack_heads(dot_, own)
                bias = b_ref[r, 0] if has_prev else b_ref[r, 0, :, BLK:]
                ss.append(_dot(q2, kt, ((1,), (1,))) + bias)
                dps.append(_dot(do2, vt, ((1,), (1,))))
                lse2s.append(jnp.concatenate([lset[:, :1], lset[:, E_A:E_A + 1]], axis=0))
                delta2s.append(jnp.concatenate([jnp.sum(jnp.where(lane < E_A, prod, 0.0), axis=-1, keepdims=True),
                                                jnp.sum(jnp.where(lane >= E_A, prod, 0.0), axis=-1, keepdims=True)], axis=0))
                q2s.append(q2)
                kts.append(kt)
                do2s.append(do2)
            ps = [jnp.exp(s - lse2) for s, lse2 in zip(ss, lse2s)]
            dls = [p * (dp - delta2) for p, dp, delta2 in zip(ps, dps, delta2s)]
            for (r, d, start, has_prev), q2, kt, do2, p, dl in zip(grp, q2s, kts, do2s, ps, dls):
                cur = _strided(start, d)
                dsb = dl.astype(BF16)
                dq2 = _dot(dsb, kt, ((1,), (0,))) * SCALE_A
                dkt = _dot(dsb, q2, ((0,), (0,)))
                dvt = _dot(p.astype(BF16), do2, ((0,), (0,)))
                dq_ref[cur, :] += jnp.where(lane < E_A, dq2[:BLK], dq2[BLK:])
                if has_prev:
                    prev = _strided(start - BLK * d, d)
                    ds_ref[0, r, 0] += dl
                    dk_ref[prev, :] += dkt[:BLK]
                    dv_ref[prev, :] += dvt[:BLK]
                    dk_ref[cur, :] += dkt[BLK:]
                    dv_ref[cur, :] += dvt[BLK:]
                else:
                    ds_ref[0, r, 0, :, BLK:] += dl
                    dk_ref[cur, :] += dkt
                    dv_ref[cur, :] += dvt
        dq_out[...] = dq_ref[...].astype(BF16)
        dk_out[...] = dk_ref[...].astype(BF16)
        dv_out[...] = dv_ref[...].astype(BF16)

    npair = H // 2
    pair = pl.BlockSpec((S, HP), lambda b, p: (b, p))
    return pl.pallas_call(
        body, name="dil_bwd", grid=(nb, npair),
        in_specs=[pair, pl.BlockSpec((S, HP), lambda b, p: (b, npair + p)),
                  pl.BlockSpec((S, HP), lambda b, p: (b, 2 * npair + p)),
                  pl.BlockSpec((3, 1, 2 * BLK, 2 * BLK), lambda b, p: (0, p, 0, 0)), pair, pair, pair],
        out_specs=[pair, pair, pair, pl.BlockSpec((1, 3, 1, 2 * BLK, 2 * BLK), lambda b, p: (b, 0, p, 0, 0))],
        out_shape=[jax.ShapeDtypeStruct((t, D_A), BF16)] * 3 + [jax.ShapeDtypeStruct((nb, 3, npair, 2 * BLK, 2 * BLK), F32)],
        scratch_shapes=[pltpu.VMEM((S, HP), F32)] * 3,
        compiler_params=_cp("parallel", "parallel"),
    )(proj, proj, proj, biasm, o, do, lse)


def _rel_bias_grad(dlogits):
    nb = dlogits.shape[0]
    buckets, _ = _band_geometry()
    kk = 3 * BLK * 2 * BLK
    dl = jnp.transpose(dlogits.reshape(nb, 3, H, BLK, 2 * BLK), (0, 2, 1, 3, 4)).reshape(nb, H, kk)
    bk = jnp.asarray(buckets.reshape(1, kk))
    tk = kk // 12

    def body(dl_ref, bk_ref, o_ref):
        j = pl.program_id(0)
        onehot = (bk_ref[...] == lax.broadcasted_iota(jnp.int32, (N_BUCKETS, tk), 0)).astype(F32)
        tot = dl_ref[0]
        for b in range(1, nb):
            tot = tot + dl_ref[b]
        part = lax.dot_general(onehot, tot, ((((1,), (1,))), ((), ())), preferred_element_type=F32,
                               precision=lax.Precision.HIGHEST)
        _acc_first(j, o_ref, part)

    return pl.pallas_call(
        body, name="rel_bias_grad", grid=(kk // tk,),
        in_specs=[pl.BlockSpec((nb, H, tk), lambda j: (0, 0, j)), pl.BlockSpec((1, tk), lambda j: (0, j))],
        out_specs=pl.BlockSpec((N_BUCKETS, H), lambda j: (0, 0)),
        out_shape=jax.ShapeDtypeStruct((N_BUCKETS, H), F32),
        compiler_params=_cp("arbitrary"),
    )(dl, bk)


def _mesh_place():
    x, y, c = lax.axis_index("x"), lax.axis_index("y"), lax.axis_index("c")
    return x, y, c


def _peer(k):
    x, y, c = _mesh_place()
    px = 1 - x if k & 4 else x
    py = 1 - y if k & 2 else y
    pc = 1 - c if k & 1 else c
    return (px, py, pc), 4 * px + 2 * py + pc


ANY = pl.BlockSpec(memory_space=pl.ANY)


def _exchange(arrays, gathers, name, after=None):
    n_arr = len(arrays)

    def body(*refs):
        ins, outs = refs[:n_arr], refs[n_arr + 1:2 * n_arr + 1]
        send, recv, loc = refs[2 * n_arr + 1:]
        x, y, c = _mesh_place()
        me = 4 * x + 2 * y + c
        local = [pltpu.make_async_copy(ins[a] if gathers[a] else ins[a].at[me], outs[a].at[me], loc.at[a])
                 for a in range(n_arr)]
        remote = _peer_copies(ins, outs, send, recv, gathers)
        for cp in local:
            cp.start()
        for put, _ in remote:
            put.start()
        for cp in local:
            cp.wait()
        for put, got in remote:
            put.wait_send()
            got.wait_recv()

    return pl.pallas_call(
        body, name=name,
        in_specs=[ANY] * (n_arr + 1), out_specs=[ANY] * n_arr,
        out_shape=[jax.ShapeDtypeStruct(((N_DEV,) if g else ()) + a.shape, a.dtype) for a, g in zip(arrays, gathers)],
        scratch_shapes=[pltpu.SemaphoreType.DMA((n_arr * (N_DEV - 1),)), pltpu.SemaphoreType.DMA((n_arr * (N_DEV - 1),)),
                        pltpu.SemaphoreType.DMA((n_arr,))],
        compiler_params=pltpu.CompilerParams(has_side_effects=True),
    )(*arrays, arrays[0] if after is None else after)


def _gather_two_level(arrays, name):
    n_arr = len(arrays)
    per = N_DEV - 1

    def body(*refs):
        ins, outs = refs[:n_arr], refs[n_arr:2 * n_arr]
        send, recv, loc = refs[2 * n_arr:]
        x, y, c = _mesh_place()
        me, sibling = (x, y, c), (x, y, 1 - c)
        chips = [(1 - x, y), (x, 1 - y), (1 - x, 1 - y)]

        def block(a, place):
            px, py, pc = place
            return outs[a].at[4 * px + 2 * py + pc]

        def copy(a, k, place, to, src=None):
            dst = block(a, place)
            return pltpu.make_async_remote_copy(dst if src is None else src, dst, send.at[a * per + k], recv.at[a * per + k],
                                                device_id=to, device_id_type=pl.DeviceIdType.MESH)

        local = [pltpu.make_async_copy(ins[a], block(a, me), loc.at[a]) for a in range(n_arr)]
        for cp in local:
            cp.start()
        first = []
        for a in range(n_arr):
            first.append(copy(a, 0, me, sibling, src=ins[a]))
            first += [copy(a, 1 + j, me, (*chip, c), src=ins[a]) for j, chip in enumerate(chips)]
        for cp in first:
            cp.start()
        passed = []
        for j, chip in enumerate(chips):
            for a in range(n_arr):
                copy(a, 1 + j, (*chip, c), me).wait_recv()
                passed.append(copy(a, 4 + j, (*chip, c), sibling))
                passed[-1].start()
        for a in range(n_arr):
            copy(a, 0, sibling, me).wait_recv()
            for j, chip in enumerate(chips):
                copy(a, 4 + j, (*chip, 1 - c), me).wait_recv()
        for cp in first + passed:
            cp.wait_send()
        for cp in local:
            cp.wait()

    return pl.pallas_call(
        body, name=name,
        in_specs=[ANY] * n_arr, out_specs=[ANY] * n_arr,
        out_shape=[jax.ShapeDtypeStruct((N_DEV,) + a.shape, a.dtype) for a in arrays],
        scratch_shapes=[pltpu.SemaphoreType.DMA((n_arr * per,)), pltpu.SemaphoreType.DMA((n_arr * per,)),
                        pltpu.SemaphoreType.DMA((n_arr,))],
        compiler_params=pltpu.CompilerParams(has_side_effects=True),
    )(*arrays)


HBM = pl.BlockSpec(memory_space=pltpu.HBM)
SEM = pl.BlockSpec(memory_space=pltpu.SEMAPHORE)
DATAFLOW = pltpu.SideEffectType.DATAFLOW_SIDE_EFFECTING


def _own_block_in_place(block, me):
    land = lax.empty((N_DEV,) + block.shape, block.dtype)
    return lax.dynamic_update_slice(land, block[None], (me,) + (0,) * block.ndim)


def _peer_copies(srcs, lands, send, recv, gathers):
    x, y, c = _mesh_place()
    me = 4 * x + 2 * y + c
    out = []
    for a, (src, land) in enumerate(zip(srcs, lands)):
        for k in range(1, N_DEV):
            dev, idx = _peer(k)
            sem = a * (N_DEV - 1) + k - 1
            mine = src if gathers[a] else src.at[idx]
            put = pltpu.make_async_remote_copy(mine, land.at[me], send.at[sem], recv.at[sem],
                                               device_id=dev, device_id_type=pl.DeviceIdType.MESH)
            got = pltpu.make_async_remote_copy(mine, land.at[idx], send.at[sem], recv.at[sem],
                                               device_id=dev, device_id_type=pl.DeviceIdType.MESH)
            out.append((put, got))
    return out


def _exchange_start(srcs, lands, gather, after, name):
    n = len(srcs)

    def body(*refs):
        srcs_, lands_, send, recv = refs[:n], refs[n:2 * n], refs[2 * n + 1], refs[2 * n + 2]
        for put, _ in _peer_copies(srcs_, lands_, send, recv, gather):
            put.start()
        refs[-1][...] = jnp.zeros_like(refs[-1])

    nsem = n * (N_DEV - 1)
    thru = [pltpu.HBM(a.shape, a.dtype) for a in list(srcs) + list(lands)]
    res = pl.pallas_call(
        body, name=name,
        out_shape=(pltpu.SemaphoreType.DMA((nsem,)), pltpu.SemaphoreType.DMA((nsem,)), *thru, jax.ShapeDtypeStruct((8, 128), F32)),
        in_specs=[HBM] * (2 * n) + [ANY],
        out_specs=(SEM, SEM, *([HBM] * (2 * n)), pl.BlockSpec(memory_space=pltpu.VMEM)),
        input_output_aliases={i: 2 + i for i in range(2 * n)},
        compiler_params=pltpu.CompilerParams(has_side_effects=DATAFLOW),
    )(*[pltpu.with_memory_space_constraint(a, pltpu.HBM) for a in list(srcs) + list(lands)], after)
    return res[0], res[1], list(res[2:2 + n]), list(res[2 + n:2 + 2 * n]), res[-1]


def _exchange_wait(send, recv, srcs, lands, gather, after, name):
    n = len(srcs)

    def body(*refs):
        srcs_, lands_, send_, recv_ = refs[:n], refs[n:2 * n], refs[2 * n], refs[2 * n + 1]
        for put, got in _peer_copies(srcs_, lands_, send_, recv_, gather):
            put.wait_send()
            got.wait_recv()

    thru = [pltpu.HBM(a.shape, a.dtype) for a in list(srcs) + list(lands)]
    res = pl.pallas_call(
        body, name=name, out_shape=tuple(thru),
        in_specs=[HBM] * (2 * n) + [SEM, SEM, ANY], out_specs=tuple([HBM] * (2 * n)),
        input_output_aliases={i: i for i in range(2 * n)},
        compiler_params=pltpu.CompilerParams(has_side_effects=DATAFLOW),
    )(*srcs, *lands, send, recv, after)
    return list(res[n:])


def _silu_rows(c):
    def body(c_ref, o_ref):
        v = c_ref[...]
        o_ref[...] = v * _sigmoid(v)

    return pl.pallas_call(body, name="cond", out_shape=jax.ShapeDtypeStruct(c.shape, F32))(c)


def _mod_slab(cond_all, w_ada, b_slab):
    def body(c_ref, w_ref, b_ref, o_ref):
        o_ref[...] = _dot(c_ref[...].astype(BF16), w_ref[0].astype(BF16), ((1,), (0,))) + b_ref[...]

    return pl.pallas_call(body, name="mod_slab",
                          out_shape=jax.ShapeDtypeStruct((cond_all.shape[0], w_ada.shape[2]), F32),
                          compiler_params=pltpu.CompilerParams(vmem_limit_bytes=VMEM_LIMIT))(cond_all, w_ada, b_slab)


def _ada_grad(cond_all, dmod_cols):
    def body(c_ref, d_ref, o_ref):
        o_ref[...] = _dot(c_ref[...].astype(BF16), d_ref[...].astype(BF16), ((0,), (0,)))

    return pl.pallas_call(body, name="ada_grad",
                          out_shape=jax.ShapeDtypeStruct((cond_all.shape[1], dmod_cols.shape[1]), F32),
                          compiler_params=pltpu.CompilerParams(vmem_limit_bytes=VMEM_LIMIT))(cond_all, dmod_cols)


def _adam_math(g, w, m, v):
    m2 = B1 * m + (1.0 - B1) * g
    v2 = B2 * v + (1.0 - B2) * (g * g)
    m_hat = m2 / (1.0 - B1 ** STEP)
    v_hat = v2 / (1.0 - B2 ** STEP)
    return -LR * (m_hat / (jnp.sqrt(v_hat) + ADAM_EPS) + WD * w), m2, v2


def _adamw(parts, w, m, v, name):
    n, rows, cols = parts.shape
    tr = max([p for p in range(16, 513, 16) if rows % p == 0] or [rows])

    def body(p_ref, w_ref, m_ref, v_ref, g_ref, d_ref, m2_ref, v2_ref):
        g = p_ref[0].astype(F32)
        for s in range(1, n):
            g = g + p_ref[s].astype(F32)
        g_ref[0] = g
        d_ref[0], m2_ref[0], v2_ref[0] = _adam_math(g, w_ref[0], m_ref[0], v_ref[0])

    blk = pl.BlockSpec((1, tr, cols), lambda i: (0, i, 0))
    return pl.pallas_call(
        body, name=name, grid=(rows // tr,),
        in_specs=[pl.BlockSpec((n, tr, cols), lambda i: (0, i, 0)), blk, blk, blk],
        out_specs=[blk] * 4, out_shape=[jax.ShapeDtypeStruct((1, rows, cols), F32)] * 4,
        compiler_params=_cp("parallel"),
    )(*[pltpu.with_memory_space_constraint(a, pltpu.HBM) for a in (parts, w, m, v)])


ROW_PARAMS = (("g_norm1", D), ("g_cq", Q_LORA), ("g_ckv", KV_LORA), ("g_out_a", D_A), ("g_out_b", D_A), ("g_norm2", D),
              ("g_final", D))
LOSS_ROW = N_MOD + len(ROW_PARAMS)
PAY_ROWS = 16
NCOL = N_MOD * D // N_DEV


def _pack_small(dmods, rows, loss_cols):
    nb = dmods[0].shape[0]
    nrow = len(ROW_PARAMS)

    def body(*refs):
        dm, rw, loss_ref, pay_ref, blk_ref = refs[:N_MOD], refs[N_MOD:N_MOD + nrow], refs[N_MOD + nrow], refs[-2], refs[-1]
        pay_ref[...] = jnp.zeros_like(pay_ref)
        for k in range(N_MOD):
            tot = dm[k][0]
            for b in range(1, nb):
                tot = tot + dm[k][b]
            pay_ref[k:k + 1, :] = tot
        for i, (_, n) in enumerate(ROW_PARAMS):
            pay_ref[N_MOD + i:N_MOD + i + 1, :n] = rw[i][...]
        pay_ref[LOSS_ROW:LOSS_ROW + 1, :] = loss_ref[...]
        for j in range(N_DEV):
            done = 0
            while done < NCOL:
                seg, off = divmod(j * NCOL + done, D)
                ln = min(NCOL - done, D - off)
                for b in range(nb):
                    blk_ref[j, b:b + 1, done:done + ln] = dm[seg][b][:, off:off + ln]
                done += ln

    return pl.pallas_call(
        body, name="pack_small",
        out_shape=[jax.ShapeDtypeStruct((PAY_ROWS, D), F32), jax.ShapeDtypeStruct((N_DEV, nb, NCOL), F32)],
    )(*dmods, *rows, loss_cols)


def _small_update(pay, rel, ws, ms, vs):
    n_par = len(ws)

    def body(*refs):
        pay_ref, rel_ref = refs[:2]
        w_refs, m_refs, v_refs = (refs[2 + s * n_par:2 + (s + 1) * n_par] for s in range(3))
        outs, loss_ref = refs[2 + 3 * n_par:-1], refs[-1]
        tot, rtot = pay_ref[0], rel_ref[0]
        for s in range(1, N_DEV):
            tot, rtot = tot + pay_ref[s], rtot + rel_ref[s]

        def update(p, g, sl):
            outs[4 * p][:, sl] = g
            outs[4 * p + 1][:, sl], outs[4 * p + 2][:, sl], outs[4 * p + 3][:, sl] = _adam_math(
                g, w_refs[p][:, sl], m_refs[p][:, sl], v_refs[p][:, sl])

        for k in range(N_MOD):
            update(0, tot[k:k + 1, :], slice(k * D, (k + 1) * D))
        for i, (_, n) in enumerate(ROW_PARAMS):
            update(1 + i, tot[N_MOD + i:N_MOD + i + 1, :n], slice(0, n))
        update(n_par - 1, rtot, slice(0, H))
        loss_ref[...] = jnp.broadcast_to((0.5 / D) * jnp.sum(tot[LOSS_ROW:LOSS_ROW + 1, :]), loss_ref.shape)

    shapes = [jax.ShapeDtypeStruct(w.shape, F32) for w in ws for _ in range(4)]
    res = pl.pallas_call(
        body, name="small_update", out_shape=shapes + [jax.ShapeDtypeStruct((8, 128), F32)],
    )(pay, rel, *ws, *ms, *vs)
    return [tuple(res[4 * p:4 * p + 4]) for p in range(n_par)], res[-1]


def _cols_from_blocks(g):
    return jnp.transpose(g, (1, 0, 2)).reshape(g.shape[1], N_DEV * g.shape[2])


def _cols_to_blocks(w):
    r, c = w.shape
    return jnp.transpose(w.reshape(r, N_DEV, c // N_DEV), (1, 0, 2))


def _pad_w_in(wt):
    z = jnp.zeros((NOPE, wt.shape[1]), wt.dtype)
    return jnp.concatenate([wt[:P_IN - ROPE], z, wt[P_IN - ROPE:], z[:HP - NOPE - ROPE]], axis=0)


def _unpad_w_in(gt):
    k0 = P_IN - ROPE + NOPE
    return jnp.concatenate([gt[:P_IN - ROPE], gt[k0:k0 + ROPE]], axis=0)


def _pad_w_uq(w):
    w3 = w.reshape(Q_LORA, H, NOPE + ROPE)
    return jnp.pad(w3, ((0, 0), (0, 0), (0, HP - NOPE - ROPE))).reshape(Q_LORA, H * HP)


def _unpad_w_uq(g):
    return g.reshape(Q_LORA, H, HP)[:, :, :NOPE + ROPE].reshape(Q_LORA, H * (NOPE + ROPE))


def _split_w_ukv(w):
    w4 = w.reshape(KV_LORA, H // 2, 2, HP)
    z = jnp.zeros((KV_LORA, H // 2, NOPE), w.dtype)
    kn, vv = w4[..., :NOPE], w4[..., NOPE:]
    w_k = jnp.stack([jnp.concatenate([kn[:, :, 0], z], -1), jnp.concatenate([kn[:, :, 1], z], -1)], axis=2)
    w_v = jnp.stack([jnp.concatenate([vv[:, :, 0], z], -1), jnp.concatenate([z, vv[:, :, 1]], -1)], axis=2)
    return w_k.reshape(KV_LORA, H * HP), w_v.reshape(KV_LORA, H * HP)


def _join_w_ukv(g_k, g_v):
    gk = g_k.reshape(KV_LORA, H // 2, 2, HP)
    gv = g_v.reshape(KV_LORA, H // 2, 2, HP)
    even = jnp.concatenate([gk[:, :, 0, :NOPE], gv[:, :, 0, :VDIM]], -1)
    odd = jnp.concatenate([gk[:, :, 1, :NOPE], gv[:, :, 1, VDIM:]], -1)
    return jnp.stack([even, odd], axis=2).reshape(KV_LORA, H * HP)


def _rope_tables():
    half = ROPE // 2
    inv = ROPE_THETA ** (-jnp.arange(half, dtype=F32) / half)
    ang = jnp.arange(S, dtype=F32)[:, None] * inv[None, :]
    cos, sin = jnp.cos(ang), jnp.sin(ang)
    ones, zeros = jnp.ones((S, NOPE), F32), jnp.zeros((S, NOPE), F32)
    tail1, tail0 = jnp.ones((S, HP - NOPE - ROPE), F32), jnp.zeros((S, HP - NOPE - ROPE), F32)
    zh = jnp.zeros((S, half), F32)
    c = jnp.concatenate([ones, cos, cos, tail1], axis=1)
    sm = jnp.concatenate([zeros, -sin, zh, tail0], axis=1)
    sp = jnp.concatenate([zeros, zh, sin, tail0], axis=1)
    return c, sm, sp


def _local_step(x, mod, target, g_norm1, w_in_p, g_cq, w_uq_p, g_ckv, w_k, w_v, rel_bias, g_out_a, g_out_b, w_out,
                g_norm2, w_ffn_in, w_ffn_out, g_final, late_weights=None, on_ffn_grads=None, on_last_grads=None):
    nb = x.shape[0] // S
    sh1, sc1, g1, sh2, sc2, g2 = (mod[:, n].reshape(nb, 1, D) for n in range(N_MOD))
    rc, rsm, rsp = _rope_tables()
    biasm = _band_bias(rel_bias)

    h1 = _pre1(x, g_norm1, sc1, sh1)
    proj = _mm_nt(h1, w_in_p, F32, "proj")
    q, k, v, cqn, ckvn = _mla_pre(proj, g_cq, g_ckv, w_uq_p, w_k, w_v, rc, rsm, rsp)
    out_b, lse_b = _mla_fwd(q, k, v)
    out_a, lse_a = _dil_fwd(proj, biasm)
    y = _post_attn(out_a, out_b, g_out_a, g_out_b)
    if late_weights is not None:
        w_out, w_ffn_in, w_ffn_out = late_weights(y)
    mix = _mm_nn(y, w_out, BF16, "mix")
    x2, h2 = _resid_norm2(x, mix, g1, g_norm2, sc2, sh2)
    ffn_g, ffn_u, act = _ffn_in(h2, w_ffn_in)
    f = _mm_nn(act, w_ffn_out, BF16, "ffn_out")
    dx3, df, loss_cols, dg_final, dg2 = _final(x2, f, g2, g_final, target)

    dg_, du_ = _d_act(df, w_ffn_out, ffn_g, ffn_u)
    gw_ffn_out = _mm_tn_rows([act], df, "gw_ffn_out")
    dh2 = _d_h2(dg_, du_, w_ffn_in)
    gw_ffn_in = _mm_tn_rows([dg_, du_], h2, "gw_ffn_in")
    dx2, dsh2, dsc2, dg_norm2, dg1, dmix = _norm_bwd(x2, dh2, dx3, g_norm2, sc2, gate=(mix, g1))
    dy = _mm_nt(dmix, w_out, BF16, "d_y")
    gw_out = _mm_tn(y, [dmix], "gw_out")
    if on_ffn_grads is not None:
        g_out_a = g_out_a + on_ffn_grads(gw_ffn_in, gw_ffn_out, gw_out)
    dout_a, dout_b, dg_out_a, dg_out_b = _post_attn_bwd(dy, out_a, out_b, g_out_a, g_out_b)
    dq_b, dk_b, dv_b = _mla_bwd(q, k, v, out_b, dout_b, lse_b)
    dq_a, dk_a, dv_a, dlogits = _dil_bwd(proj, biasm, out_a, dout_a, lse_a)
    g_rel = _rel_bias_grad(dlogits)
    dqr, dproj, dg_cq, dg_ckv = _mla_pre_bwd(proj, dq_b, dk_b, dv_b, (dq_a, dk_a, dv_a), g_cq, g_ckv, w_uq_p, w_k, w_v,
                                             rc, rsm, rsp)
    gw_uq = _mm_tn(cqn, [dqr], "gw_uq")
    gw_k, gw_v = _mm_tn(ckvn, [dk_b, dv_b], "gw_kv")
    gw_in = _mm_tn_rows([dproj], h1, "gw_in")
    if on_last_grads is not None:
        started = on_last_grads(dict(w_in=gw_in, w_uq=gw_uq, w_k=gw_k, w_v=gw_v))
    else:
        started = None
    dh1 = _mm_nn(dproj, w_in_p, BF16, "d_h1", after=started)
    grad_x, dsh1, dsc1, dg_norm1 = _norm_bwd(x, dh1, dx2, g_norm1, sc1)

    dmod = [dsh1, dsc1, dg1, dsh2, dsc2, dg2]
    small = dict(g_norm1=dg_norm1, g_cq=dg_cq, g_ckv=dg_ckv, rel_bias=g_rel, g_out_a=dg_out_a, g_out_b=dg_out_b,
                 g_norm2=dg_norm2, g_final=dg_final)
    big = dict(w_in=gw_in, w_uq=gw_uq, w_k=gw_k, w_v=gw_v, w_out=gw_out, w_ffn_in=gw_ffn_in, w_ffn_out=gw_ffn_out)
    return grad_x, dmod, loss_cols, small, big


def kernel(x, c, w_ada, b_ada, g_norm1, w_in, g_cq, w_uq, g_ckv, w_ukv, rel_bias, g_out_a, g_out_b, w_out, g_norm2, w_ffn_in, w_ffn_out, g_final, loss_target, m_w_ada, m_b_ada, m_g_norm1, m_w_in, m_g_cq, m_w_uq, m_g_ckv, m_w_ukv, m_rel_bias, m_g_out_a, m_g_out_b, m_w_out, m_g_norm2, m_w_ffn_in, m_w_ffn_out, m_g_final, v_w_ada, v_b_ada, v_g_norm1, v_w_in, v_g_cq, v_w_uq, v_g_ckv, v_w_ukv, v_rel_bias, v_g_out_a, v_g_out_b, v_w_out, v_g_norm2, v_w_ffn_in, v_w_ffn_out, v_g_final):
    nb = x.shape[0]
    t = nb * S
    xt, tt = x.reshape(t, D), loss_target.reshape(t, D)
    me = 4 * lax.axis_index("x") + 2 * lax.axis_index("y") + lax.axis_index("c")

    early = [jnp.swapaxes(w_in, 1, 2)[0], w_uq[0], w_ukv[0]]
    gathered = _gather_two_level([_silu_rows(c)] + [s.astype(BF16) for s in early], "gather_weights")
    cond_all = gathered[0].reshape(N_DEV * nb, D)
    w_in_t = gathered[1].reshape(P_IN, D)
    w_uq_f, w_ukv_f = (_cols_from_blocks(g) for g in gathered[2:4])
    w_k, w_v = _split_w_ukv(w_ukv_f)

    ncol = N_MOD * D // N_DEV
    b_slab = lax.dynamic_slice(b_ada, (0, me * ncol), (1, ncol))
    slab = _mod_slab(cond_all, w_ada, b_slab)
    (mod_rows,) = _exchange([slab.reshape(N_DEV, nb, ncol)], [False], "scatter_mod")
    mod = jnp.transpose(mod_rows, (1, 0, 2)).reshape(nb, N_MOD, D)

    late = [s.astype(BF16) for s in (w_out[0], jnp.swapaxes(w_ffn_in, 1, 2)[0], w_ffn_out[0])]
    late_send, late_recv, late_src, late_land, late_token = _exchange_start(
        late, [_own_block_in_place(s, me) for s in late], [True] * 3, mod_rows, "gather_late_start")
    g_norm1_t = g_norm1 + late_token[:1, :1]

    def late_weights(after):
        w_out_g, w_ffn_in_g, w_ffn_out_g = _exchange_wait(late_send, late_recv, late_src, late_land, [True] * 3, after,
                                                          "gather_late_wait")
        return w_out_g.reshape(D, D), w_ffn_in_g.reshape(2 * D_FF, D), w_ffn_out_g.reshape(D_FF, D)

    flight = {}

    def start_grads(key, src, name):
        land = [_own_block_in_place(lax.dynamic_index_in_dim(s, me, 0, keepdims=False), me) for s in src]
        send, recv, src, land, token = _exchange_start(src, land, [False] * len(src), src[0], name)
        flight[key] = (send, recv, src, land)
        return token[:1, :1]

    def on_ffn_grads(gw_ffn_in, gw_ffn_out, gw_out):
        return start_grads("ffn", [gw_ffn_in.reshape(N_DEV, 2 * D_FF // N_DEV, D), gw_ffn_out.reshape(N_DEV, D_FF // N_DEV, D),
                                   gw_out.reshape(N_DEV, D // N_DEV, D)], "exchange_ffn_start")

    def on_last_grads(gw):
        return start_grads("rest", [_unpad_w_in(gw["w_in"]).reshape(N_DEV, P_IN // N_DEV, D),
                                    _cols_to_blocks(_unpad_w_uq(gw["w_uq"])),
                                    _cols_to_blocks(_join_w_ukv(gw["w_k"], gw["w_v"]))], "exchange_rest_start")

    grad_x, dmod, loss_cols, small, _ = _local_step(
        xt, mod, tt, g_norm1_t, _pad_w_in(w_in_t), g_cq, _pad_w_uq(w_uq_f), g_ckv, w_k, w_v, rel_bias, g_out_a, g_out_b,
        None, g_norm2, None, None, g_final.reshape(1, D), late_weights=late_weights, on_ffn_grads=on_ffn_grads,
        on_last_grads=on_last_grads)

    upd = {}

    def land_and_update(key, names, after, name):
        got = _exchange_wait(*flight[key], [False] * len(names), after, name)
        for n, p in zip(names, got):
            w, m, v = big[n]
            upd[n] = _adamw(p, w, m, v, "adamw_" + n)

    def flip(a):
        return jnp.swapaxes(a, 1, 2)

    big = dict(w_in=(flip(w_in), flip(m_w_in), flip(v_w_in)), w_uq=(w_uq, m_w_uq, v_w_uq), w_ukv=(w_ukv, m_w_ukv, v_w_ukv),
               w_out=(w_out, m_w_out, v_w_out), w_ffn_in=(flip(w_ffn_in), flip(m_w_ffn_in), flip(v_w_ffn_in)),
               w_ffn_out=(w_ffn_out, m_w_ffn_out, v_w_ffn_out))
    land_and_update("ffn", ["w_ffn_in", "w_ffn_out", "w_out"], grad_x, "exchange_ffn_wait")
    land_and_update("rest", ["w_in", "w_uq", "w_ukv"], upd["w_out"][0], "exchange_rest_wait")
    for n in ("w_in", "w_ffn_in"):
        upd[n] = tuple(flip(a) for a in upd[n])

    mine, dmod_blocks = _pack_small(dmod, [small[n] for n, _ in ROW_PARAMS], loss_cols)
    dmod_cols, pay, rel = _exchange([dmod_blocks, mine, small["rel_bias"]], [False, True, True], "exchange_small",
                                    after=upd["w_ukv"][0])
    g_ada = _ada_grad(cond_all, dmod_cols.reshape(N_DEV * nb, ncol))
    upd["w_ada"] = _adamw(g_ada[None], w_ada, m_w_ada, v_w_ada, "adamw_w_ada")
    row = lambda a: a.reshape(1, D)
    small_names = ["b_ada"] + [n for n, _ in ROW_PARAMS] + ["rel_bias"]
    small_w = [b_ada, g_norm1, g_cq, g_ckv, g_out_a, g_out_b, g_norm2, row(g_final), rel_bias]
    small_m = [m_b_ada, m_g_norm1, m_g_cq, m_g_ckv, m_g_out_a, m_g_out_b, m_g_norm2, row(m_g_final), m_rel_bias]
    small_v = [v_b_ada, v_g_norm1, v_g_cq, v_g_ckv, v_g_out_a, v_g_out_b, v_g_norm2, row(v_g_final), v_rel_bias]
    small_upd, loss8 = _small_update(pay, rel, small_w, small_m, small_v)
    upd.update(zip(small_names, small_upd))

    order = ["w_ada", "b_ada", "g_norm1", "w_in", "g_cq", "w_uq", "g_ckv", "w_ukv", "rel_bias", "g_out_a", "g_out_b",
             "w_out", "g_norm2", "w_ffn_in", "w_ffn_out", "g_final"]
    like = dict(g_final=g_final)
    outs = [loss8[0, 0], grad_x.reshape(x.shape)]
    for part in range(4):
        for n in order:
            val = upd[n][part]
            outs.append(val.reshape(like[n].shape) if n in like else val)
    return tuple(outs)
```

```python
import functools

import numpy as np
import jax
import jax.numpy as jnp
from jax import lax
from jax.experimental import pallas as pl
from jax.experimental.pallas import tpu as pltpu

F32, BF16 = jnp.float32, jnp.bfloat16

N_DEV = 8
D = 1024
S = 2048
H = 8
E_A = 64
D_A = H * E_A
Q_LORA, KV_LORA = 384, 256
NOPE, ROPE, VDIM = 64, 32, 64
HP = 128
P_IN = 3 * D_A + Q_LORA + KV_LORA + ROPE
P_PAD = 3 * D_A + Q_LORA + KV_LORA + HP
TAIL0 = 3 * D_A
TAIL = P_PAD - TAIL0
D_FF = 2816
N_MOD = 6
EPS = 1e-6
NEG = -1e30
BLK = 128
DILATIONS = (1, 4, 16)
N_BUCKETS, MAX_DISTANCE = 32, 2048
ROPE_THETA = 10000.0
SCALE_A = E_A ** -0.5
SCALE_B = (NOPE + ROPE) ** -0.5
B1, B2, LR, ADAM_EPS, WD, STEP = 0.9, 0.999, 0.001, 1e-8, 0.01, 10
VMEM_LIMIT = 56 * 1024 * 1024


def _cp(*sem):
    return pltpu.CompilerParams(dimension_semantics=sem, vmem_limit_bytes=VMEM_LIMIT)


def _pick(n, prefs):
    for p in prefs:
        if n % p == 0:
            return p
    raise ValueError(f"no tile of {prefs} divides {n}")


OPERAND_BYTES = 6 * 1024 * 1024


def _pick_rows(m, k):
    return _pick(m, [p for p in (1024, 512, 256, 128, 16) if p * k * 2 <= OPERAND_BYTES])


MATMUL_BYTES = 40 * 1024 * 1024


def _stream_rows(m, fixed, per_row):
    return _pick(m, [p for p in (4096, 2048, 1024, 512, 256, 128, 16) if fixed + p * per_row <= MATMUL_BYTES])


def _dot(a, b, dims):
    return lax.dot_general(a, b, (dims, ((), ())), preferred_element_type=F32)


def _mm_nn(a, b, out_dtype, name, after=None):
    m, k = a.shape
    n = b.shape[1]
    tn = _pick(n, (512, 384, 256, 128))
    tm = _stream_rows(m, 4 * k * tn, 4 * k + (2 * jnp.dtype(out_dtype).itemsize + 4) * tn)

    def body(a_ref, b_ref, *rest):
        o_ref = rest[-1]
        o_ref[...] = _dot(a_ref[...], b_ref[...], ((1,), (0,))).astype(o_ref.dtype)

    extra = [] if after is None else [after]
    return pl.pallas_call(
        body, name=name, grid=(m // tm, n // tn),
        in_specs=[pl.BlockSpec((tm, k), lambda i, j: (i, 0)), pl.BlockSpec((k, tn), lambda i, j: (0, j))] + [ANY] * len(extra),
        out_specs=pl.BlockSpec((tm, tn), lambda i, j: (i, j)),
        out_shape=jax.ShapeDtypeStruct((m, n), out_dtype),
        compiler_params=_cp("parallel", "parallel"),
    )(a, b, *extra)


def _mm_nt(a, b, out_dtype, name, after=None):
    m, k = a.shape
    n = b.shape[0]
    tn = _pick(n, (512, 384, 256, 128))
    tm = _stream_rows(m, 4 * k * tn, 4 * k + (2 * jnp.dtype(out_dtype).itemsize + 4) * tn)

    def body(a_ref, b_ref, *rest):
        o_ref = rest[-1]
        o_ref[...] = _dot(a_ref[...], b_ref[...], ((1,), (1,))).astype(o_ref.dtype)

    extra = [] if after is None else [after]
    return pl.pallas_call(
        body, name=name, grid=(m // tm, n // tn),
        in_specs=[pl.BlockSpec((tm, k), lambda i, j: (i, 0)), pl.BlockSpec((tn, k), lambda i, j: (j, 0))] + [ANY] * len(extra),
        out_specs=pl.BlockSpec((tm, tn), lambda i, j: (i, j)),
        out_shape=jax.ShapeDtypeStruct((m, n), out_dtype),
        compiler_params=_cp("parallel", "parallel"),
    )(a, b, *extra)


def _mm_tn(a, bs, name):
    t, m = a.shape
    n = bs[0].shape[1]
    nb_ = len(bs)
    tc = _pick(t, (512, 16))
    tn = _pick(n, (512, 384, 256, 128))
    tm = _pick(m, [p for p in (1024, 512, 384, 256, 128) if (3 * p + 2 * nb_ * tn) * t * 2 <= VMEM_LIMIT - 2 * OPERAND_BYTES])
    if tm <= 256 and nb_ * n * t * 2 <= 2 * OPERAND_BYTES:
        tn = n

    def body(*refs):
        a_ref, b_refs, o_refs, at_ref = refs[0], refs[1:1 + nb_], refs[1 + nb_:1 + 2 * nb_], refs[-1]

        @pl.when(pl.program_id(1) == 0)
        def _():
            def chunk(c, _):
                rows = pl.ds(pl.multiple_of(c * tc, tc), tc)
                at_ref[:, rows] = a_ref[rows, :].T
                return 0

            lax.fori_loop(0, t // tc, chunk, 0)

        for b_ref, o_ref in zip(b_refs, o_refs):
            o_ref[...] = _dot(at_ref[...], b_ref[...], ((1,), (0,))).astype(BF16)

    res = pl.pallas_call(
        body, name=name, grid=(m // tm, n // tn),
        in_specs=[pl.BlockSpec((t, tm), lambda i, j: (0, i))] + [pl.BlockSpec((t, tn), lambda i, j: (0, j))] * nb_,
        out_specs=[pl.BlockSpec((tm, tn), lambda i, j: (i, j))] * nb_,
        out_shape=[jax.ShapeDtypeStruct((m, n), BF16)] * nb_,
        scratch_shapes=[pltpu.VMEM((tm, t), BF16)],
        compiler_params=_cp("parallel", "arbitrary"),
    )(a, *bs)
    return res[0] if nb_ == 1 else res


def _mm_tn_rows(a_list, b, name):
    t, m = a_list[0].shape
    n = b.shape[1]
    na = len(a_list)
    tc, tm = _pick(t, (512, 16)), _pick(m, (256, 128))
    nblk = m // tm

    def body(*refs):
        a_refs, b_ref, o_ref, bt_ref, r_ref = refs[:na], refs[na], refs[na + 1], refs[na + 2], refs[na + 3]
        i = pl.program_id(0)

        @pl.when(i == 0)
        def _():
            def chunk(c, _):
                rows = pl.ds(pl.multiple_of(c * tc, tc), tc)
                bt_ref[:, rows] = b_ref[rows, :].T
                return 0

            lax.fori_loop(0, t // tc, chunk, 0)

        for s, a_ref in enumerate(a_refs):
            @pl.when((i >= s * nblk) & (i < (s + 1) * nblk))
            def _(a_ref=a_ref):
                r_ref[...] = _dot(bt_ref[...], a_ref[...], ((1,), (0,)))
                o_ref[...] = r_ref[...].T.astype(BF16)

    return pl.pallas_call(
        body, name=name, grid=(na * nblk,),
        in_specs=[pl.BlockSpec((t, tm), lambda i, s=s: (0, jnp.clip(i - s * nblk, 0, nblk - 1))) for s in range(na)]
        + [pl.BlockSpec((t, n), lambda i: (0, 0))],
        out_specs=pl.BlockSpec((tm, n), lambda i: (i, 0)),
        out_shape=jax.ShapeDtypeStruct((na * m, n), BF16),
        scratch_shapes=[pltpu.VMEM((n, t), BF16), pltpu.VMEM((n, tm), F32)],
        compiler_params=_cp("arbitrary"),
    )(*a_list, b)


EPI = 256


def _silu_parts(g):
    sg = 0.5 * jnp.tanh(0.5 * g) + 0.5
    return sg, g * sg


def _ffn_in(h2, wt):
    t, k = h2.shape
    tn = _pick(D_FF, (256, 128))
    tm = _stream_rows(t, 8 * k * tn, 4 * k + (3 * 2 * 2 + 2 * 4) * tn)
    nj = D_FF // tn

    def body(h_ref, wg_ref, wu_ref, g_ref, u_ref, a_ref):
        hv = h_ref[...]
        g_all = _dot(hv, wg_ref[...], ((1,), (1,)))
        u_all = _dot(hv, wu_ref[...], ((1,), (1,)))
        for r in range(tm // EPI):
            rows = slice(r * EPI, (r + 1) * EPI)
            g, u = g_all[rows], u_all[rows]
            g_ref[rows, :] = g.astype(BF16)
            u_ref[rows, :] = u.astype(BF16)
            a_ref[rows, :] = (_silu_parts(g)[1] * u).astype(BF16)

    blk = pl.BlockSpec((tm, tn), lambda i, j: (i, j))
    return pl.pallas_call(
        body, name="ffn_in", grid=(t // tm, nj),
        in_specs=[pl.BlockSpec((tm, k), lambda i, j: (i, 0)), pl.BlockSpec((tn, k), lambda i, j: (j, 0)),
                  pl.BlockSpec((tn, k), lambda i, j: (j + nj, 0))],
        out_specs=[blk] * 3, out_shape=[jax.ShapeDtypeStruct((t, D_FF), BF16)] * 3,
        compiler_params=_cp("parallel", "parallel"),
    )(h2, wt, wt)


def _d_act(df, w, g, u):
    t, k = df.shape
    tn = _pick(D_FF, (256, 128))
    tm = _stream_rows(t, 4 * k * tn, 4 * k + (4 * 2 * 2 + 4) * tn)

    def body(df_ref, w_ref, g_ref, u_ref, dg_ref, du_ref):
        da_all = _dot(df_ref[...], w_ref[...], ((1,), (1,)))
        for r in range(tm // EPI):
            rows = slice(r * EPI, (r + 1) * EPI)
            da = da_all[rows]
            gv = g_ref[rows, :].astype(F32)
            sg, silu = _silu_parts(gv)
            dg_ref[rows, :] = ((da * u_ref[rows, :].astype(F32)) * (sg + silu * (1.0 - sg))).astype(BF16)
            du_ref[rows, :] = (da * silu).astype(BF16)

    blk = pl.BlockSpec((tm, tn), lambda i, j: (i, j))
    return pl.pallas_call(
        body, name="d_act", grid=(t // tm, D_FF // tn),
        in_specs=[pl.BlockSpec((tm, k), lambda i, j: (i, 0)), pl.BlockSpec((tn, k), lambda i, j: (j, 0)), blk, blk],
        out_specs=[blk] * 2, out_shape=[jax.ShapeDtypeStruct((t, D_FF), BF16)] * 2,
        compiler_params=_cp("parallel", "parallel"),
    )(df, w, g, u)


def _d_h2(dg, du, wt):
    t = dg.shape[0]
    n = wt.shape[1]
    tm, tn = _pick_rows(t, D_FF), _pick(n, (512, 256, 128))

    def body(dg_ref, du_ref, wg_ref, wu_ref, o_ref):
        o_ref[...] = (_dot(dg_ref[...], wg_ref[...], ((1,), (0,)))
                      + _dot(du_ref[...], wu_ref[...], ((1,), (0,)))).astype(BF16)

    return pl.pallas_call(
        body, name="d_h2", grid=(t // tm, n // tn),
        in_specs=[pl.BlockSpec((tm, D_FF), lambda i, j: (i, 0)), pl.BlockSpec((tm, D_FF), lambda i, j: (i, 0)),
                  pl.BlockSpec((D_FF, tn), lambda i, j: (0, j)), pl.BlockSpec((D_FF, tn), lambda i, j: (1, j))],
        out_specs=pl.BlockSpec((tm, tn), lambda i, j: (i, j)),
        out_shape=jax.ShapeDtypeStruct((t, n), BF16),
        compiler_params=_cp("parallel", "parallel"),
    )(dg, du, wt, wt)


TM = 1024


def _row(w):
    return pl.BlockSpec((TM, w), lambda i: (i, 0))


def _row_at(w, col):
    return pl.BlockSpec((TM, w), lambda i: (i, col))


def _vec(w):
    return pl.BlockSpec((1, w), lambda i: (0, 0))


def _per_ex(w):
    return pl.BlockSpec((1, 1, w), lambda i: (i // (S // TM), 0, 0))


def _pos(w):
    return pl.BlockSpec((TM, w), lambda i: (i % (S // TM), 0))


def _full(shape):
    return pl.BlockSpec(shape, lambda i: (0,) * len(shape))


def _rms(x):
    return lax.rsqrt(jnp.mean(x * x, axis=-1, keepdims=True) + EPS)


def _rms_bwd(n, r, dn):
    return r * (dn - n * jnp.mean(dn * n, axis=-1, keepdims=True))


def _colsum(v):
    return jnp.sum(v, axis=0, keepdims=True)


def _acc_first(i, ref, val, every=None):
    first = (i == 0) if every is None else (i % every == 0)

    @pl.when(first)
    def _():
        ref[...] = jnp.zeros_like(ref)

    ref[...] += val.reshape(ref.shape)


def _pre1(x, g, sc, sh):
    t = x.shape[0]

    def body(x_ref, g_ref, sc_ref, sh_ref, h_ref):
        xv = x_ref[...]
        n = xv * _rms(xv)
        h_ref[...] = ((n * g_ref[...]) * (1.0 + sc_ref[0]) + sh_ref[0]).astype(BF16)

    return pl.pallas_call(
        body, name="pre1", grid=(t // TM,),
        in_specs=[_row(D), _vec(D), _per_ex(D), _per_ex(D)],
        out_specs=_row(D), out_shape=jax.ShapeDtypeStruct((t, D), BF16),
        compiler_params=_cp("parallel"),
    )(x, g, sc, sh)


def _rope_fwd(v, c, sm, sp):
    return v * c + pltpu.roll(v, HP - ROPE // 2, 1) * sm + pltpu.roll(v, ROPE // 2, 1) * sp


def _rope_bwd(dv, c, sm, sp):
    return dv * c + pltpu.roll(dv * sm, ROPE // 2, 1) + pltpu.roll(dv * sp, HP - ROPE // 2, 1)


def _mla_pre(proj, g_cq, g_ckv, w_uq, w_k, w_v, rc, rsm, rsp):
    t = proj.shape[0]

    def body(tail_ref, gq_ref, gkv_ref, wuq_ref, wk_ref, wv_ref, c_ref, sm_ref, sp_ref,
             q_ref, k_ref, v_ref, cqn_ref, ckvn_ref):
        tail = tail_ref[...]
        cq, ckv, kr = tail[:, :Q_LORA], tail[:, Q_LORA:Q_LORA + KV_LORA], tail[:, Q_LORA + KV_LORA:]
        cqn = (cq * _rms(cq) * gq_ref[...]).astype(BF16)
        ckvn = (ckv * _rms(ckv) * gkv_ref[...]).astype(BF16)
        cqn_ref[...] = cqn
        ckvn_ref[...] = ckvn
        c, sm, sp = c_ref[...], sm_ref[...], sp_ref[...]
        q = _dot(cqn, wuq_ref[...], ((1,), (0,)))
        kn = _dot(ckvn, wk_ref[...], ((1,), (0,)))
        v_ref[...] = _dot(ckvn, wv_ref[...], ((1,), (0,))).astype(BF16)
        krr = _rope_fwd(kr, c, sm, sp)
        for h in range(H):
            sl = slice(h * HP, (h + 1) * HP)
            q_ref[:, sl] = _rope_fwd(q[:, sl], c, sm, sp).astype(BF16)
            k_ref[:, sl] = (kn[:, sl] + krr).astype(BF16)

    wide = H * HP
    return pl.pallas_call(
        body, name="mla_pre", grid=(t // TM,),
        in_specs=[_row_at(TAIL, TAIL0 // TAIL), _vec(Q_LORA), _vec(KV_LORA), _full((Q_LORA, wide)),
                  _full((KV_LORA, wide)), _full((KV_LORA, wide)), _pos(HP), _pos(HP), _pos(HP)],
        out_specs=[_row(wide), _row(wide), _row(wide), _row(Q_LORA), _row(KV_LORA)],
        out_shape=[jax.ShapeDtypeStruct((t, wide), BF16)] * 3
        + [jax.ShapeDtypeStruct((t, Q_LORA), BF16), jax.ShapeDtypeStruct((t, KV_LORA), BF16)],
        compiler_params=_cp("parallel"),
    )(proj, g_cq, g_ckv, w_uq, w_k, w_v, rc, rsm, rsp)


def _mla_pre_bwd(proj, dq_, dk_, dv_, dqkv_a, g_cq, g_ckv, w_uq, w_k, w_v, rc, rsm, rsp):
    t = proj.shape[0]
    wide = H * HP

    def body(tail_ref, dq_ref, dk_ref, dv_ref, dqa_ref, dka_ref, dva_ref, gq_ref, gkv_ref, wuq_ref, wk_ref, wv_ref,
             c_ref, sm_ref, sp_ref, dqo_ref, dproj_ref, dgq_ref, dgkv_ref):
        i = pl.program_id(0)
        for n, src in enumerate((dqa_ref, dka_ref, dva_ref)):
            dproj_ref[:, n * D_A:(n + 1) * D_A] = src[...]
        dtail_ref = dproj_ref.at[:, TAIL0:]
        tail = tail_ref[...]
        cq, ckv = tail[:, :Q_LORA], tail[:, Q_LORA:Q_LORA + KV_LORA]
        c, sm, sp = c_ref[...], sm_ref[...], sp_ref[...]
        dkr = jnp.zeros((TM, HP), F32)
        for h in range(H):
            sl = slice(h * HP, (h + 1) * HP)
            dqo_ref[:, sl] = _rope_bwd(dq_ref[:, sl].astype(F32), c, sm, sp).astype(BF16)
            dkr = dkr + dk_ref[:, sl].astype(F32)
        lane = lax.broadcasted_iota(jnp.int32, (TM, HP), 1)
        dkr = jnp.where((lane >= NOPE) & (lane < NOPE + ROPE), _rope_bwd(dkr, c, sm, sp), 0.0)
        dkb = dk_ref[...]
        dvb = dv_ref[...]
        dcqn = _dot(dqo_ref[...], wuq_ref[...], ((1,), (1,)))
        dckvn = _dot(dkb, wk_ref[...], ((1,), (1,))) + _dot(dvb, wv_ref[...], ((1,), (1,)))
        rq, rkv = _rms(cq), _rms(ckv)
        nq, nkv = cq * rq, ckv * rkv
        _acc_first(i, dgq_ref, _colsum(dcqn * nq))
        _acc_first(i, dgkv_ref, _colsum(dckvn * nkv))
        dtail_ref[:, :Q_LORA] = _rms_bwd(nq, rq, dcqn * gq_ref[...]).astype(BF16)
        dtail_ref[:, Q_LORA:Q_LORA + KV_LORA] = _rms_bwd(nkv, rkv, dckvn * gkv_ref[...]).astype(BF16)
        dtail_ref[:, Q_LORA + KV_LORA:] = dkr.astype(BF16)

    return pl.pallas_call(
        body, name="mla_pre_bwd", grid=(t // TM,),
        in_specs=[_row_at(TAIL, TAIL0 // TAIL), _row(wide), _row(wide), _row(wide), _row(D_A), _row(D_A), _row(D_A),
                  _vec(Q_LORA), _vec(KV_LORA), _full((Q_LORA, wide)), _full((KV_LORA, wide)), _full((KV_LORA, wide)),
                  _pos(HP), _pos(HP), _pos(HP)],
        out_specs=[_row(wide), _row(P_PAD), _vec(Q_LORA), _vec(KV_LORA)],
        out_shape=[jax.ShapeDtypeStruct((t, wide), BF16), jax.ShapeDtypeStruct((t, P_PAD), BF16),
                   jax.ShapeDtypeStruct((1, Q_LORA), F32), jax.ShapeDtypeStruct((1, KV_LORA), F32)],
        compiler_params=_cp("arbitrary"),
    )(proj, dq_, dk_, dv_, *dqkv_a, g_cq, g_ckv, w_uq, w_k, w_v, rc, rsm, rsp)


def _post_attn(out_a, out_b, g_a, g_b):
    t = out_a.shape[0]

    def body(a_ref, b_ref, ga_ref, gb_ref, y_ref):
        a, b = a_ref[...], b_ref[...]
        y_ref[:, :D_A] = (a * _rms(a) * ga_ref[...]).astype(BF16)
        y_ref[:, D_A:] = (b * _rms(b) * gb_ref[...]).astype(BF16)

    return pl.pallas_call(
        body, name="post_attn", grid=(t // TM,),
        in_specs=[_row(D_A), _row(D_A), _vec(D_A), _vec(D_A)],
        out_specs=_row(D), out_shape=jax.ShapeDtypeStruct((t, D), BF16),
        compiler_params=_cp("parallel"),
    )(out_a, out_b, g_a, g_b)


def _post_attn_bwd(dy, out_a, out_b, g_a, g_b):
    t = dy.shape[0]

    def body(dy_ref, a_ref, b_ref, ga_ref, gb_ref, da_ref, db_ref, dga_ref, dgb_ref):
        i = pl.program_id(0)
        dy_ = dy_ref[...].astype(F32)
        for src, g_ref, dst, dg_ref, sl in ((a_ref, ga_ref, da_ref, dga_ref, slice(0, D_A)),
                                            (b_ref, gb_ref, db_ref, dgb_ref, slice(D_A, D))):
            v = src[...]
            r = _rms(v)
            n = v * r
            dyv = dy_[:, sl]
            _acc_first(i, dg_ref, _colsum(dyv * n))
            dst[...] = _rms_bwd(n, r, dyv * g_ref[...])

    return pl.pallas_call(
        body, name="post_attn_bwd", grid=(t // TM,),
        in_specs=[_row(D), _row(D_A), _row(D_A), _vec(D_A), _vec(D_A)],
        out_specs=[_row(D_A), _row(D_A), _vec(D_A), _vec(D_A)],
        out_shape=[jax.ShapeDtypeStruct((t, D_A), F32)] * 2 + [jax.ShapeDtypeStruct((1, D_A), F32)] * 2,
        compiler_params=_cp("arbitrary"),
    )(dy, out_a, out_b, g_a, g_b)


def _resid_norm2(x, mix, g1, g, sc, sh):
    t = x.shape[0]

    def body(x_ref, mix_ref, g1_ref, g_ref, sc_ref, sh_ref, x2_ref, h_ref):
        x2 = x_ref[...] + g1_ref[0] * mix_ref[...]
        x2_ref[...] = x2
        n = x2 * _rms(x2)
        h_ref[...] = ((n * g_ref[...]) * (1.0 + sc_ref[0]) + sh_ref[0]).astype(BF16)

    return pl.pallas_call(
        body, name="resid_norm2", grid=(t // TM,),
        in_specs=[_row(D), _row(D), _per_ex(D), _vec(D), _per_ex(D), _per_ex(D)],
        out_specs=[_row(D), _row(D)],
        out_shape=[jax.ShapeDtypeStruct((t, D), F32), jax.ShapeDtypeStruct((t, D), BF16)],
        compiler_params=_cp("parallel"),
    )(x, mix, g1, g, sc, sh)


def _sigmoid(v):
    return 1.0 / (1.0 + jnp.exp(-v))


def _final(x2, f, g2, g_fin, target):
    t = x2.shape[0]
    nb = t // S
    tpb = S // TM

    def body(x2_ref, f_ref, g2_ref, g_ref, t_ref, dx3_ref, df_ref, loss_ref, dgf_ref, dg2_ref):
        i = pl.program_id(0)
        fv = f_ref[...].astype(F32)
        x3 = x2_ref[...] + g2_ref[0] * fv
        r = _rms(x3)
        n = x3 * r
        err = n * g_ref[...] - t_ref[...]
        _acc_first(i, loss_ref, _colsum(err * err))
        dy = err * (1.0 / D)
        _acc_first(i, dgf_ref, _colsum(dy * n))
        dx3 = _rms_bwd(n, r, dy * g_ref[...])
        dx3_ref[...] = dx3
        _acc_first(i, dg2_ref, _colsum(dx3 * fv), every=tpb)
        df_ref[...] = (dx3 * g2_ref[0]).astype(BF16)

    return pl.pallas_call(
        body, name="final", grid=(t // TM,),
        in_specs=[_row(D), _row(D), _per_ex(D), _vec(D), _row(D)],
        out_specs=[_row(D), _row(D), _vec(D), _vec(D), _per_ex(D)],
        out_shape=[jax.ShapeDtypeStruct((t, D), F32), jax.ShapeDtypeStruct((t, D), BF16),
                   jax.ShapeDtypeStruct((1, D), F32), jax.ShapeDtypeStruct((1, D), F32),
                   jax.ShapeDtypeStruct((nb, 1, D), F32)],
        compiler_params=_cp("arbitrary"),
    )(x2, f, g2, g_fin, target)


def _norm_bwd(xin, dh, dres, g, sc, gate=None):
    t = xin.shape[0]
    nb = t // S
    tpb = S // TM
    gated = gate is not None

    def body(*refs):
        if gated:
            x_ref, dh_ref, dres_ref, g_ref, sc_ref, mix_ref, g1_ref, dx_ref, dsh_ref, dsc_ref, dg_ref, dg1_ref, dmix_ref = refs
        else:
            x_ref, dh_ref, dres_ref, g_ref, sc_ref, dx_ref, dsh_ref, dsc_ref, dg_ref = refs
        i = pl.program_id(0)
        xv, dhv = x_ref[...], dh_ref[...].astype(F32)
        r = _rms(xv)
        n = xv * r
        gv = g_ref[...]
        _acc_first(i, dsh_ref, _colsum(dhv), every=tpb)
        _acc_first(i, dsc_ref, _colsum(dhv * (n * gv)), every=tpb)
        dng = dhv * (1.0 + sc_ref[0])
        _acc_first(i, dg_ref, _colsum(dng * n))
        dx = dres_ref[...] + _rms_bwd(n, r, dng * gv)
        dx_ref[...] = dx
        if gated:
            _acc_first(i, dg1_ref, _colsum(dx * mix_ref[...].astype(F32)), every=tpb)
            dmix_ref[...] = (dx * g1_ref[0]).astype(BF16)

    in_specs = [_row(D), _row(D), _row(D), _vec(D), _per_ex(D)]
    out_specs = [_row(D), _per_ex(D), _per_ex(D), _vec(D)]
    out_shape = [jax.ShapeDtypeStruct((t, D), F32), jax.ShapeDtypeStruct((nb, 1, D), F32),
                 jax.ShapeDtypeStruct((nb, 1, D), F32), jax.ShapeDtypeStruct((1, D), F32)]
    args = [xin, dh, dres, g, sc]
    if gated:
        in_specs += [_row(D), _per_ex(D)]
        out_specs += [_per_ex(D), _row(D)]
        out_shape += [jax.ShapeDtypeStruct((nb, 1, D), F32), jax.ShapeDtypeStruct((t, D), BF16)]
        args += list(gate)
    return pl.pallas_call(
        body, name="norm2_bwd" if gated else "norm1_bwd", grid=(t // TM,),
        in_specs=in_specs, out_specs=out_specs, out_shape=out_shape,
        compiler_params=_cp("arbitrary"),
    )(*args)


TQ = 256
TB = 512
FWD_HEADS = 2


def _mla_fwd(q, k, v):
    t = q.shape[0]
    nb = t // S

    def body(q_ref, k_ref, v_ref, o_ref, lse_ref):
        causal = lax.broadcasted_iota(jnp.int32, (TB, TB), 0) >= lax.broadcasted_iota(jnp.int32, (TB, TB), 1)
        heads = [slice(h * HP, (h + 1) * HP) for h in range(FWD_HEADS)]
        for i in range(S // TB):
            ri, past = slice(i * TB, (i + 1) * TB), slice(0, i * TB)
            qhs = [q_ref[ri, sl] for sl in heads]
            sd = [jnp.where(causal, _dot(qh, k_ref[ri, sl], ((1,), (1,))) * SCALE_B, NEG) for qh, sl in zip(qhs, heads)]
            ms = [jnp.max(s, axis=-1, keepdims=True) for s in sd]
            if i:
                so = [_dot(qh, k_ref[past, sl], ((1,), (1,))) * SCALE_B for qh, sl in zip(qhs, heads)]
                ms = [jnp.maximum(m, jnp.max(s, axis=-1, keepdims=True)) for m, s in zip(ms, so)]
            pd = [jnp.exp(s - m) for s, m in zip(sd, ms)]
            ls = [jnp.sum(p, axis=-1, keepdims=True) for p in pd]
            acc = [_dot(p.astype(BF16), v_ref[ri, sl], ((1,), (0,))) for p, sl in zip(pd, heads)]
            if i:
                po = [jnp.exp(s - m) for s, m in zip(so, ms)]
                ls = [l + jnp.sum(p, axis=-1, keepdims=True) for l, p in zip(ls, po)]
                acc = [a + _dot(p.astype(BF16), v_ref[past, sl], ((1,), (0,))) for a, p, sl in zip(acc, po, heads)]
            for pr in range(FWD_HEADS // 2):
                o_ref[ri, pr * HP:(pr + 1) * HP] = acc[2 * pr] / ls[2 * pr] + acc[2 * pr + 1] / ls[2 * pr + 1]
            for sl, m, l in zip(heads, ms, ls):
                lse_ref[ri, sl] = jnp.broadcast_to(m + jnp.log(l), (TB, HP))

    wide2 = pl.BlockSpec((S, FWD_HEADS * HP), lambda b, p: (b, p))
    return pl.pallas_call(
        body, name="mla_fwd", grid=(nb, H // FWD_HEADS),
        in_specs=[wide2, wide2, wide2],
        out_specs=[pl.BlockSpec((S, FWD_HEADS // 2 * HP), lambda b, p: (b, p)), wide2],
        out_shape=[jax.ShapeDtypeStruct((t, H * VDIM), F32), jax.ShapeDtypeStruct((t, H * HP), F32)],
        compiler_params=_cp("parallel", "parallel"),
    )(q, k, v)


def _mla_bwd(q, k, v, o, do, lse):
    t = q.shape[0]
    nb = t // S

    def body(q_ref, k_ref, v_ref, o_ref, do_ref, lse_ref, dq_out, dk_out, dv_out, dq_ref, dk_ref, dv_ref):
        lane = lax.broadcasted_iota(jnp.int32, (TB, HP), 1)
        causal = lax.broadcasted_iota(jnp.int32, (TB, TB), 0) >= lax.broadcasted_iota(jnp.int32, (TB, TB), 1)
        heads = [slice(h * HP, (h + 1) * HP) for h in range(2)]
        nblk = S // TB
        for i in reversed(range(nblk)):
            ri, past = slice(i * TB, (i + 1) * TB), slice(0, i * TB)
            dov = do_ref[ri, :]
            prod = dov * o_ref[ri, :]
            dob = dov.astype(BF16)
            deltas = [jnp.sum(jnp.where((lane < VDIM) if h == 0 else (lane >= VDIM), prod, 0.0), axis=-1, keepdims=True)
                      for h in range(2)]
            qhs = [q_ref[ri, sl] for sl in heads]
            lses = [lse_ref[ri, sl][:, :1] for sl in heads]
            for rows, diagonal in ((ri, True), (past, False)):
                if rows.stop == rows.start:
                    continue
                ps = [jnp.exp(_dot(qh, k_ref[rows, sl], ((1,), (1,))) * SCALE_B - lse) for qh, sl, lse in zip(qhs, heads, lses)]
                if diagonal:
                    ps = [jnp.where(causal, p, 0.0) for p in ps]
                dps = [_dot(dob, v_ref[rows, sl], ((1,), (1,))) for sl in heads]
                dss = [(p * (dp - delta) * SCALE_B).astype(BF16) for p, dp, delta in zip(ps, dps, deltas)]
                for sl, qh, p, ds in zip(heads, qhs, ps, dss):
                    dq = _dot(ds, k_ref[rows, sl], ((1,), (0,)))
                    dk = _dot(ds, qh, ((0,), (0,)))
                    dv = _dot(p.astype(BF16), dob, ((0,), (0,)))
                    if diagonal:
                        dq_ref[ri, sl] = dq
                    else:
                        dq_ref[ri, sl] += dq
                    if i == nblk - 1:
                        dk_ref[rows, sl] = dk
                        dv_ref[rows, sl] = dv
                    else:
                        dk_ref[rows, sl] += dk
                        dv_ref[rows, sl] += dv
        dq_out[...] = dq_ref[...].astype(BF16)
        dk_out[...] = dk_ref[...].astype(BF16)
        dv_out[...] = dv_ref[...].astype(BF16)

    wide2 = pl.BlockSpec((S, 2 * HP), lambda b, p: (b, p))
    pair = pl.BlockSpec((S, HP), lambda b, p: (b, p))
    return pl.pallas_call(
        body, name="mla_bwd", grid=(nb, H // 2),
        in_specs=[wide2, wide2, wide2, pair, pair, wide2],
        out_specs=[wide2, wide2, wide2],
        out_shape=[jax.ShapeDtypeStruct((t, H * HP), BF16)] * 3,
        scratch_shapes=[pltpu.VMEM((S, 2 * HP), F32)] * 3,
        compiler_params=_cp("parallel", "parallel"),
    )(q, k, v, o, do, lse)


def _t5_bucket(dist):
    max_exact = N_BUCKETS // 2
    d = np.maximum(dist, 1).astype(np.float64)
    large = max_exact + (np.log(d / max_exact) / np.log(MAX_DISTANCE / max_exact) * (N_BUCKETS - max_exact)).astype(np.int64)
    large = np.minimum(large, N_BUCKETS - 1)
    return np.where(dist < max_exact, dist, large).astype(np.int32)


def _band_geometry():
    a = np.arange(BLK)[:, None]
    bk = np.arange(2 * BLK)[None, :]
    steps = BLK + a - bk
    valid = (steps >= 0) & (steps <= BLK)
    buckets = np.stack([_t5_bucket(np.clip(steps, 0, BLK) * d) for d in DILATIONS])
    return buckets, valid


def _band_bias(rel_bias):
    buckets, valid = _band_geometry()
    onehot = (jnp.asarray(buckets)[..., None] == jnp.arange(N_BUCKETS)).astype(F32)
    bias = jnp.einsum("rqkn,nh->rhqk", onehot, rel_bias, precision=lax.Precision.HIGHEST)
    bias = jnp.where(jnp.asarray(valid)[None, None], bias, NEG)
    return bias.reshape(3, H // 2, 2 * BLK, 2 * BLK)


def _dil_items():
    items = []
    for r, d in enumerate(DILATIONS):
        for res in range(d):
            for blk in range(S // d // BLK):
                items.append((r, d, blk * BLK * d + res, blk > 0))
    return items


GROUP = 4


def _strided(start, d):
    return pl.ds(start, BLK) if d == 1 else pl.ds(start, BLK, stride=d)


def _stack_heads(tile, own):
    return jnp.where(own, jnp.concatenate([tile, tile], axis=0), 0.0).astype(BF16)


def _own_lanes():
    row = lax.broadcasted_iota(jnp.int32, (2 * BLK, HP), 0)
    lane = lax.broadcasted_iota(jnp.int32, (2 * BLK, HP), 1)
    return (lane < E_A) == (row < BLK)


def _dil_fwd(proj, biasm):
    t = proj.shape[0]
    nb = t // S

    def body(q_ref, k_ref, v_ref, b_ref, o_ref, lse_ref, ob_ref, lb_ref):
        lane = lax.broadcasted_iota(jnp.int32, (BLK, HP), 1)
        own = _own_lanes()
        items = _dil_items()
        for g in range(0, len(items), GROUP):
            grp = items[g:g + GROUP]
            ss, vts = [], []
            for r, d, start, has_prev in grp:
                cur = _strided(start, d)
                rows = [_strided(start - BLK * d, d), cur] if has_prev else [cur]
                q2 = _stack_heads(q_ref[cur, :] * SCALE_A, own)
                kt = jnp.concatenate([k_ref[x, :] for x in rows], axis=0).astype(BF16)
                vts.append(jnp.concatenate([v_ref[x, :] for x in rows], axis=0).astype(BF16))
                bias = b_ref[r, 0] if has_prev else b_ref[r, 0, :, BLK:]
                ss.append(_dot(q2, kt, ((1,), (1,))) + bias)
            ms = [jnp.max(s, axis=-1, keepdims=True) for s in ss]
            ps = [jnp.exp(s - m) for s, m in zip(ss, ms)]
            ls = [jnp.sum(p, axis=-1, keepdims=True) for p in ps]
            for (r, d, start, _), p, vt, m, l in zip(grp, ps, vts, ms, ls):
                cur = _strided(start, d)
                o2 = _dot(p.astype(BF16), vt, ((1,), (0,))) / l
                lse2 = m + jnp.log(l)
                ob_ref[r, cur, :] = jnp.where(lane < E_A, o2[:BLK], o2[BLK:])
                lb_ref[r, cur, :] = jnp.where(lane < E_A, lse2[:BLK], lse2[BLK:])

        def merge(c, _):
            rows = pl.ds(pl.multiple_of(c * TQ, TQ), TQ)
            l0, l1, l2 = lb_ref[0, rows, :], lb_ref[1, rows, :], lb_ref[2, rows, :]
            m = jnp.maximum(jnp.maximum(l0, l1), l2)
            e0, e1, e2 = jnp.exp(l0 - m), jnp.exp(l1 - m), jnp.exp(l2 - m)
            tot = e0 + e1 + e2
            o_ref[rows, :] = (e0 * ob_ref[0, rows, :] + e1 * ob_ref[1, rows, :] + e2 * ob_ref[2, rows, :]) / tot
            lse_ref[rows, :] = m + jnp.log(tot)
            return 0

        lax.fori_loop(0, S // TQ, merge, 0)

    npair = H // 2
    return pl.pallas_call(
        body, name="dil_fwd", grid=(nb, npair),
        in_specs=[pl.BlockSpec((S, HP), lambda b, p: (b, p)), pl.BlockSpec((S, HP), lambda b, p: (b, npair + p)),
                  pl.BlockSpec((S, HP), lambda b, p: (b, 2 * npair + p)),
                  pl.BlockSpec((3, 1, 2 * BLK, 2 * BLK), lambda b, p: (0, p, 0, 0))],
        out_specs=[pl.BlockSpec((S, HP), lambda b, p: (b, p))] * 2,
        out_shape=[jax.ShapeDtypeStruct((t, D_A), F32)] * 2,
        scratch_shapes=[pltpu.VMEM((3, S, HP), F32), pltpu.VMEM((3, S, HP), F32)],
        compiler_params=_cp("parallel", "parallel"),
    )(proj, proj, proj, biasm)


def _dil_bwd(proj, biasm, o, do, lse):
    t = proj.shape[0]
    nb = t // S

    def body(q_ref, k_ref, v_ref, b_ref, o_ref, do_ref, lse_ref, dq_out, dk_out, dv_out, ds_ref, dq_ref, dk_ref, dv_ref):
        dq_ref[...] = jnp.zeros_like(dq_ref)
        dk_ref[...] = jnp.zeros_like(dk_ref)
        dv_ref[...] = jnp.zeros_like(dv_ref)
        ds_ref[...] = jnp.zeros_like(ds_ref)
        lane = lax.broadcasted_iota(jnp.int32, (BLK, HP), 1)
        own = _own_lanes()
        items = _dil_items()
        for g in range(0, len(items), GROUP):
            grp = items[g:g + GROUP]
            q2s, kts, do2s, ss, dps, lse2s, delta2s = [], [], [], [], [], [], []
            for r, d, start, has_prev in grp:
                cur = _strided(start, d)
                rows = [_strided(start - BLK * d, d), cur] if has_prev else [cur]
                q2 = _stack_heads(q_ref[cur, :] * SCALE_A, own)
                kt = jnp.concatenate([k_ref[x, :] for x in rows], axis=0).astype(BF16)
                vt = jnp.concatenate([v_ref[x, :] for x in rows], axis=0).astype(BF16)
                dot_ = do_ref[cur, :]
                prod = dot_ * o_ref[cur, :]
                lset = lse_ref[cur, :]
                do2 = _stack_heads(dot_, own)
                bias = b_ref[r, 0] if has_prev else b_ref[r, 0, :, BLK:]
                ss.append(_dot(q2, kt, ((1,), (1,))) + bias)
                dps.append(_dot(do2, vt, ((1,), (1,))))
                lse2s.append(jnp.concatenate([lset[:, :1], lset[:, E_A:E_A + 1]], axis=0))
                delta2s.append(jnp.concatenate([jnp.sum(jnp.where(lane < E_A, prod, 0.0), axis=-1, keepdims=True),
                                                jnp.sum(jnp.where(lane >= E_A, prod, 0.0), axis=-1, keepdims=True)], axis=0))
                q2s.append(q2)
                kts.append(kt)
                do2s.append(do2)
            ps = [jnp.exp(s - lse2) for s, lse2 in zip(ss, lse2s)]
            dls = [p * (dp - delta2) for p, dp, delta2 in zip(ps, dps, delta2s)]
            for (r, d, start, has_prev), q2, kt, do2, p, dl in zip(grp, q2s, kts, do2s, ps, dls):
                cur = _strided(start, d)
                dsb = dl.astype(BF16)
                dq2 = _dot(dsb, kt, ((1,), (0,))) * SCALE_A
                dkt = _dot(dsb, q2, ((0,), (0,)))
                dvt = _dot(p.astype(BF16), do2, ((0,), (0,)))
                dq_ref[cur, :] += jnp.where(lane < E_A, dq2[:BLK], dq2[BLK:])
                if has_prev:
                    prev = _strided(start - BLK * d, d)
                    ds_ref[0, r, 0] += dl
                    dk_ref[prev, :] += dkt[:BLK]
                    dv_ref[prev, :] += dvt[:BLK]
                    dk_ref[cur, :] += dkt[BLK:]
                    dv_ref[cur, :] += dvt[BLK:]
                else:
                    ds_ref[0, r, 0, :, BLK:] += dl
                    dk_ref[cur, :] += dkt
                    dv_ref[cur, :] += dvt
        dq_out[...] = dq_ref[...].astype(BF16)
        dk_out[...] = dk_ref[...].astype(BF16)
        dv_out[...] = dv_ref[...].astype(BF16)

    npair = H // 2
    pair = pl.BlockSpec((S, HP), lambda b, p: (b, p))
    return pl.pallas_call(
        body, name="dil_bwd", grid=(nb, npair),
        in_specs=[pair, pl.BlockSpec((S, HP), lambda b, p: (b, npair + p)),
                  pl.BlockSpec((S, HP), lambda b, p: (b, 2 * npair + p)),
                  pl.BlockSpec((3, 1, 2 * BLK, 2 * BLK), lambda b, p: (0, p, 0, 0)), pair, pair, pair],
        out_specs=[pair, pair, pair, pl.BlockSpec((1, 3, 1, 2 * BLK, 2 * BLK), lambda b, p: (b, 0, p, 0, 0))],
        out_shape=[jax.ShapeDtypeStruct((t, D_A), BF16)] * 3 + [jax.ShapeDtypeStruct((nb, 3, npair, 2 * BLK, 2 * BLK), F32)],
        scratch_shapes=[pltpu.VMEM((S, HP), F32)] * 3,
        compiler_params=_cp("parallel", "parallel"),
    )(proj, proj, proj, biasm, o, do, lse)


def _rel_bias_grad(dlogits):
    nb = dlogits.shape[0]
    buckets, _ = _band_geometry()
    kk = 3 * BLK * 2 * BLK
    dl = jnp.transpose(dlogits.reshape(nb, 3, H, BLK, 2 * BLK), (0, 2, 1, 3, 4)).reshape(nb, H, kk)
    bk = jnp.asarray(buckets.reshape(1, kk))
    tk = kk // 12

    def body(dl_ref, bk_ref, o_ref):
        j = pl.program_id(0)
        onehot = (bk_ref[...] == lax.broadcasted_iota(jnp.int32, (N_BUCKETS, tk), 0)).astype(F32)
        tot = dl_ref[0]
        for b in range(1, nb):
            tot = tot + dl_ref[b]
        part = lax.dot_general(onehot, tot, ((((1,), (1,))), ((), ())), preferred_element_type=F32,
                               precision=lax.Precision.HIGHEST)
        _acc_first(j, o_ref, part)

    return pl.pallas_call(
        body, name="rel_bias_grad", grid=(kk // tk,),
        in_specs=[pl.BlockSpec((nb, H, tk), lambda j: (0, 0, j)), pl.BlockSpec((1, tk), lambda j: (0, j))],
        out_specs=pl.BlockSpec((N_BUCKETS, H), lambda j: (0, 0)),
        out_shape=jax.ShapeDtypeStruct((N_BUCKETS, H), F32),
        compiler_params=_cp("arbitrary"),
    )(dl, bk)


def _mesh_place():
    x, y, c = lax.axis_index("x"), lax.axis_index("y"), lax.axis_index("c")
    return x, y, c


def _peer(k):
    x, y, c = _mesh_place()
    px = 1 - x if k & 4 else x
    py = 1 - y if k & 2 else y
    pc = 1 - c if k & 1 else c
    return (px, py, pc), 4 * px + 2 * py + pc


ANY = pl.BlockSpec(memory_space=pl.ANY)


def _exchange(arrays, gathers, name, after=None):
    n_arr = len(arrays)

    def body(*refs):
        ins, outs = refs[:n_arr], refs[n_arr + 1:2 * n_arr + 1]
        send, recv, loc = refs[2 * n_arr + 1:]
        x, y, c = _mesh_place()
        me = 4 * x + 2 * y + c
        local = [pltpu.make_async_copy(ins[a] if gathers[a] else ins[a].at[me], outs[a].at[me], loc.at[a])
                 for a in range(n_arr)]
        remote = _peer_copies(ins, outs, send, recv, gathers)
        for cp in local:
            cp.start()
        for put, _ in remote:
            put.start()
        for cp in local:
            cp.wait()
        for put, got in remote:
            put.wait_send()
            got.wait_recv()

    return pl.pallas_call(
        body, name=name,
        in_specs=[ANY] * (n_arr + 1), out_specs=[ANY] * n_arr,
        out_shape=[jax.ShapeDtypeStruct(((N_DEV,) if g else ()) + a.shape, a.dtype) for a, g in zip(arrays, gathers)],
        scratch_shapes=[pltpu.SemaphoreType.DMA((n_arr * (N_DEV - 1),)), pltpu.SemaphoreType.DMA((n_arr * (N_DEV - 1),)),
                        pltpu.SemaphoreType.DMA((n_arr,))],
        compiler_params=pltpu.CompilerParams(has_side_effects=True),
    )(*arrays, arrays[0] if after is None else after)


def _gather_two_level(arrays, name):
    n_arr = len(arrays)
    per = N_DEV - 1

    def body(*refs):
        ins, outs = refs[:n_arr], refs[n_arr:2 * n_arr]
        send, recv, loc = refs[2 * n_arr:]
        x, y, c = _mesh_place()
        me, sibling = (x, y, c), (x, y, 1 - c)
        chips = [(1 - x, y), (x, 1 - y), (1 - x, 1 - y)]

        def block(a, place):
            px, py, pc = place
            return outs[a].at[4 * px + 2 * py + pc]

        def copy(a, k, place, to, src=None):
            dst = block(a, place)
            return pltpu.make_async_remote_copy(dst if src is None else src, dst, send.at[a * per + k], recv.at[a * per + k],
                                                device_id=to, device_id_type=pl.DeviceIdType.MESH)

        local = [pltpu.make_async_copy(ins[a], block(a, me), loc.at[a]) for a in range(n_arr)]
        for cp in local:
            cp.start()
        first = []
        for a in range(n_arr):
            first.append(copy(a, 0, me, sibling, src=ins[a]))
            first += [copy(a, 1 + j, me, (*chip, c), src=ins[a]) for j, chip in enumerate(chips)]
        for cp in first:
            cp.start()
        passed = []
        for j, chip in enumerate(chips):
            for a in range(n_arr):
                copy(a, 1 + j, (*chip, c), me).wait_recv()
                passed.append(copy(a, 4 + j, (*chip, c), sibling))
                passed[-1].start()
        for a in range(n_arr):
            copy(a, 0, sibling, me).wait_recv()
            for j, chip in enumerate(chips):
                copy(a, 4 + j, (*chip, 1 - c), me).wait_recv()
        for cp in first + passed:
            cp.wait_send()
        for cp in local:
            cp.wait()

    return pl.pallas_call(
        body, name=name,
        in_specs=[ANY] * n_arr, out_specs=[ANY] * n_arr,
        out_shape=[jax.ShapeDtypeStruct((N_DEV,) + a.shape, a.dtype) for a in arrays],
        scratch_shapes=[pltpu.SemaphoreType.DMA((n_arr * per,)), pltpu.SemaphoreType.DMA((n_arr * per,)),
                        pltpu.SemaphoreType.DMA((n_arr,))],
        compiler_params=pltpu.CompilerParams(has_side_effects=True),
    )(*arrays)


HBM = pl.BlockSpec(memory_space=pltpu.HBM)
SEM = pl.BlockSpec(memory_space=pltpu.SEMAPHORE)
DATAFLOW = pltpu.SideEffectType.DATAFLOW_SIDE_EFFECTING


def _own_block_in_place(block, me):
    land = lax.empty((N_DEV,) + block.shape, block.dtype)
    return lax.dynamic_update_slice(land, block[None], (me,) + (0,) * block.ndim)


def _peer_copies(srcs, lands, send, recv, gathers):
    x, y, c = _mesh_place()
    me = 4 * x + 2 * y + c
    out = []
    for a, (src, land) in enumerate(zip(srcs, lands)):
        for k in range(1, N_DEV):
            dev, idx = _peer(k)
            sem = a * (N_DEV - 1) + k - 1
            mine = src if gathers[a] else src.at[idx]
            put = pltpu.make_async_remote_copy(mine, land.at[me], send.at[sem], recv.at[sem],
                                               device_id=dev, device_id_type=pl.DeviceIdType.MESH)
            got = pltpu.make_async_remote_copy(mine, land.at[idx], send.at[sem], recv.at[sem],
                                               device_id=dev, device_id_type=pl.DeviceIdType.MESH)
            out.append((put, got))
    return out


def _exchange_start(srcs, lands, gather, after, name):
    n = len(srcs)

    def body(*refs):
        srcs_, lands_, send, recv = refs[:n], refs[n:2 * n], refs[2 * n + 1], refs[2 * n + 2]
        for put, _ in _peer_copies(srcs_, lands_, send, recv, gather):
            put.start()
        refs[-1][...] = jnp.zeros_like(refs[-1])

    nsem = n * (N_DEV - 1)
    thru = [pltpu.HBM(a.shape, a.dtype) for a in list(srcs) + list(lands)]
    res = pl.pallas_call(
        body, name=name,
        out_shape=(pltpu.SemaphoreType.DMA((nsem,)), pltpu.SemaphoreType.DMA((nsem,)), *thru, jax.ShapeDtypeStruct((8, 128), F32)),
        in_specs=[HBM] * (2 * n) + [ANY],
        out_specs=(SEM, SEM, *([HBM] * (2 * n)), pl.BlockSpec(memory_space=pltpu.VMEM)),
        input_output_aliases={i: 2 + i for i in range(2 * n)},
        compiler_params=pltpu.CompilerParams(has_side_effects=DATAFLOW),
    )(*[pltpu.with_memory_space_constraint(a, pltpu.HBM) for a in list(srcs) + list(lands)], after)
    return res[0], res[1], list(res[2:2 + n]), list(res[2 + n:2 + 2 * n]), res[-1]


def _exchange_wait(send, recv, srcs, lands, gather, after, name):
    n = len(srcs)

    def body(*refs):
        srcs_, lands_, send_, recv_ = refs[:n], refs[n:2 * n], refs[2 * n], refs[2 * n + 1]
        for put, got in _peer_copies(srcs_, lands_, send_, recv_, gather):
            put.wait_send()
            got.wait_recv()

    thru = [pltpu.HBM(a.shape, a.dtype) for a in list(srcs) + list(lands)]
    res = pl.pallas_call(
        body, name=name, out_shape=tuple(thru),
        in_specs=[HBM] * (2 * n) + [SEM, SEM, ANY], out_specs=tuple([HBM] * (2 * n)),
        input_output_aliases={i: i for i in range(2 * n)},
        compiler_params=pltpu.CompilerParams(has_side_effects=DATAFLOW),
    )(*srcs, *lands, send, recv, after)
    return list(res[n:])


def _silu_rows(c):
    def body(c_ref, o_ref):
        v = c_ref[...]
        o_ref[...] = v * _sigmoid(v)

    return pl.pallas_call(body, name="cond", out_shape=jax.ShapeDtypeStruct(c.shape, F32))(c)


def _mod_slab(cond_all, w_ada, b_slab):
    def body(c_ref, w_ref, b_ref, o_ref):
        o_ref[...] = _dot(c_ref[...].astype(BF16), w_ref[0].astype(BF16), ((1,), (0,))) + b_ref[...]

    return pl.pallas_call(body, name="mod_slab",
                          out_shape=jax.ShapeDtypeStruct((cond_all.shape[0], w_ada.shape[2]), F32),
                          compiler_params=pltpu.CompilerParams(vmem_limit_bytes=VMEM_LIMIT))(cond_all, w_ada, b_slab)


def _ada_grad(cond_all, dmod_cols):
    def body(c_ref, d_ref, o_ref):
        o_ref[...] = _dot(c_ref[...].astype(BF16), d_ref[...].astype(BF16), ((0,), (0,)))

    return pl.pallas_call(body, name="ada_grad",
                          out_shape=jax.ShapeDtypeStruct((cond_all.shape[1], dmod_cols.shape[1]), F32),
                          compiler_params=pltpu.CompilerParams(vmem_limit_bytes=VMEM_LIMIT))(cond_all, dmod_cols)


def _adam_math(g, w, m, v):
    m2 = B1 * m + (1.0 - B1) * g
    v2 = B2 * v + (1.0 - B2) * (g * g)
    m_hat = m2 / (1.0 - B1 ** STEP)
    v_hat = v2 / (1.0 - B2 ** STEP)
    return -LR * (m_hat / (jnp.sqrt(v_hat) + ADAM_EPS) + WD * w), m2, v2


def _adamw(parts, w, m, v, name):
    n, rows, cols = parts.shape
    tr = max([p for p in range(16, 513, 16) if rows % p == 0] or [rows])

    def body(p_ref, w_ref, m_ref, v_ref, g_ref, d_ref, m2_ref, v2_ref):
        g = p_ref[0].astype(F32)
        for s in range(1, n):
            g = g + p_ref[s].astype(F32)
        g_ref[0] = g
        d_ref[0], m2_ref[0], v2_ref[0] = _adam_math(g, w_ref[0], m_ref[0], v_ref[0])

    blk = pl.BlockSpec((1, tr, cols), lambda i: (0, i, 0))
    return pl.pallas_call(
        body, name=name, grid=(rows // tr,),
        in_specs=[pl.BlockSpec((n, tr, cols), lambda i: (0, i, 0)), blk, blk, blk],
        out_specs=[blk] * 4, out_shape=[jax.ShapeDtypeStruct((1, rows, cols), F32)] * 4,
        compiler_params=_cp("parallel"),
    )(*[pltpu.with_memory_space_constraint(a, pltpu.HBM) for a in (parts, w, m, v)])


ROW_PARAMS = (("g_norm1", D), ("g_cq", Q_LORA), ("g_ckv", KV_LORA), ("g_out_a", D_A), ("g_out_b", D_A), ("g_norm2", D),
              ("g_final", D))
LOSS_ROW = N_MOD + len(ROW_PARAMS)
PAY_ROWS = 16
NCOL = N_MOD * D // N_DEV


def _pack_small(dmods, rows, loss_cols):
    nb = dmods[0].shape[0]
    nrow = len(ROW_PARAMS)

    def body(*refs):
        dm, rw, loss_ref, pay_ref, blk_ref = refs[:N_MOD], refs[N_MOD:N_MOD + nrow], refs[N_MOD + nrow], refs[-2], refs[-1]
        pay_ref[...] = jnp.zeros_like(pay_ref)
        for k in range(N_MOD):
            tot = dm[k][0]
            for b in range(1, nb):
                tot = tot + dm[k][b]
            pay_ref[k:k + 1, :] = tot
        for i, (_, n) in enumerate(ROW_PARAMS):
            pay_ref[N_MOD + i:N_MOD + i + 1, :n] = rw[i][...]
        pay_ref[LOSS_ROW:LOSS_ROW + 1, :] = loss_ref[...]
        for j in range(N_DEV):
            done = 0
            while done < NCOL:
                seg, off = divmod(j * NCOL + done, D)
                ln = min(NCOL - done, D - off)
                for b in range(nb):
                    blk_ref[j, b:b + 1, done:done + ln] = dm[seg][b][:, off:off + ln]
                done += ln

    return pl.pallas_call(
        body, name="pack_small",
        out_shape=[jax.ShapeDtypeStruct((PAY_ROWS, D), F32), jax.ShapeDtypeStruct((N_DEV, nb, NCOL), F32)],
    )(*dmods, *rows, loss_cols)


def _small_update(pay, rel, ws, ms, vs):
    n_par = len(ws)

    def body(*refs):
        pay_ref, rel_ref = refs[:2]
        w_refs, m_refs, v_refs = (refs[2 + s * n_par:2 + (s + 1) * n_par] for s in range(3))
        outs, loss_ref = refs[2 + 3 * n_par:-1], refs[-1]
        tot, rtot = pay_ref[0], rel_ref[0]
        for s in range(1, N_DEV):
            tot, rtot = tot + pay_ref[s], rtot + rel_ref[s]

        def update(p, g, sl):
            outs[4 * p][:, sl] = g
            outs[4 * p + 1][:, sl], outs[4 * p + 2][:, sl], outs[4 * p + 3][:, sl] = _adam_math(
                g, w_refs[p][:, sl], m_refs[p][:, sl], v_refs[p][:, sl])

        for k in range(N_MOD):
            update(0, tot[k:k + 1, :], slice(k * D, (k + 1) * D))
        for i, (_, n) in enumerate(ROW_PARAMS):
            update(1 + i, tot[N_MOD + i:N_MOD + i + 1, :n], slice(0, n))
        update(n_par - 1, rtot, slice(0, H))
        loss_ref[...] = jnp.broadcast_to((0.5 / D) * jnp.sum(tot[LOSS_ROW:LOSS_ROW + 1, :]), loss_ref.shape)

    shapes = [jax.ShapeDtypeStruct(w.shape, F32) for w in ws for _ in range(4)]
    res = pl.pallas_call(
        body, name="small_update", out_shape=shapes + [jax.ShapeDtypeStruct((8, 128), F32)],
    )(pay, rel, *ws, *ms, *vs)
    return [tuple(res[4 * p:4 * p + 4]) for p in range(n_par)], res[-1]


def _cols_from_blocks(g):
    return jnp.transpose(g, (1, 0, 2)).reshape(g.shape[1], N_DEV * g.shape[2])


def _cols_to_blocks(w):
    r, c = w.shape
    return jnp.transpose(w.reshape(r, N_DEV, c // N_DEV), (1, 0, 2))


def _pad_w_in(wt):
    z = jnp.zeros((NOPE, wt.shape[1]), wt.dtype)
    return jnp.concatenate([wt[:P_IN - ROPE], z, wt[P_IN - ROPE:], z[:HP - NOPE - ROPE]], axis=0)


def _unpad_w_in(gt):
    k0 = P_IN - ROPE + NOPE
    return jnp.concatenate([gt[:P_IN - ROPE], gt[k0:k0 + ROPE]], axis=0)


def _pad_w_uq(w):
    w3 = w.reshape(Q_LORA, H, NOPE + ROPE)
    return jnp.pad(w3, ((0, 0), (0, 0), (0, HP - NOPE - ROPE))).reshape(Q_LORA, H * HP)


def _unpad_w_uq(g):
    return g.reshape(Q_LORA, H, HP)[:, :, :NOPE + ROPE].reshape(Q_LORA, H * (NOPE + ROPE))


def _split_w_ukv(w):
    w4 = w.reshape(KV_LORA, H // 2, 2, HP)
    z = jnp.zeros((KV_LORA, H // 2, NOPE), w.dtype)
    kn, vv = w4[..., :NOPE], w4[..., NOPE:]
    w_k = jnp.stack([jnp.concatenate([kn[:, :, 0], z], -1), jnp.concatenate([kn[:, :, 1], z], -1)], axis=2)
    w_v = jnp.stack([jnp.concatenate([vv[:, :, 0], z], -1), jnp.concatenate([z, vv[:, :, 1]], -1)], axis=2)
    return w_k.reshape(KV_LORA, H * HP), w_v.reshape(KV_LORA, H * HP)


def _join_w_ukv(g_k, g_v):
    gk = g_k.reshape(KV_LORA, H // 2, 2, HP)
    gv = g_v.reshape(KV_LORA, H // 2, 2, HP)
    even = jnp.concatenate([gk[:, :, 0, :NOPE], gv[:, :, 0, :VDIM]], -1)
    odd = jnp.concatenate([gk[:, :, 1, :NOPE], gv[:, :, 1, VDIM:]], -1)
    return jnp.stack([even, odd], axis=2).reshape(KV_LORA, H * HP)


def _rope_tables():
    half = ROPE // 2
    inv = ROPE_THETA ** (-jnp.arange(half, dtype=F32) / half)
    ang = jnp.arange(S, dtype=F32)[:, None] * inv[None, :]
    cos, sin = jnp.cos(ang), jnp.sin(ang)
    ones, zeros = jnp.ones((S, NOPE), F32), jnp.zeros((S, NOPE), F32)
    tail1, tail0 = jnp.ones((S, HP - NOPE - ROPE), F32), jnp.zeros((S, HP - NOPE - ROPE), F32)
    zh = jnp.zeros((S, half), F32)
    c = jnp.concatenate([ones, cos, cos, tail1], axis=1)
    sm = jnp.concatenate([zeros, -sin, zh, tail0], axis=1)
    sp = jnp.concatenate([zeros, zh, sin, tail0], axis=1)
    return c, sm, sp


def _local_step(x, mod, target, g_norm1, w_in_p, g_cq, w_uq_p, g_ckv, w_k, w_v, rel_bias, g_out_a, g_out_b, w_out,
                g_norm2, w_ffn_in, w_ffn_out, g_final, late_weights=None, on_ffn_grads=None, on_last_grads=None):
    nb = x.shape[0] // S
    sh1, sc1, g1, sh2, sc2, g2 = (mod[:, n].reshape(nb, 1, D) for n in range(N_MOD))
    rc, rsm, rsp = _rope_tables()
    biasm = _band_bias(rel_bias)

    h1 = _pre1(x, g_norm1, sc1, sh1)
    proj = _mm_nt(h1, w_in_p, F32, "proj")
    q, k, v, cqn, ckvn = _mla_pre(proj, g_cq, g_ckv, w_uq_p, w_k, w_v, rc, rsm, rsp)
    out_b, lse_b = _mla_fwd(q, k, v)
    out_a, lse_a = _dil_fwd(proj, biasm)
    y = _post_attn(out_a, out_b, g_out_a, g_out_b)
    if late_weights is not None:
        w_out, w_ffn_in, w_ffn_out = late_weights(y)
    mix = _mm_nn(y, w_out, BF16, "mix")
    x2, h2 = _resid_norm2(x, mix, g1, g_norm2, sc2, sh2)
    ffn_g, ffn_u, act = _ffn_in(h2, w_ffn_in)
    f = _mm_nn(act, w_ffn_out, BF16, "ffn_out")
    dx3, df, loss_cols, dg_final, dg2 = _final(x2, f, g2, g_final, target)

    dg_, du_ = _d_act(df, w_ffn_out, ffn_g, ffn_u)
    gw_ffn_out = _mm_tn_rows([act], df, "gw_ffn_out")
    dh2 = _d_h2(dg_, du_, w_ffn_in)
    gw_ffn_in = _mm_tn_rows([dg_, du_], h2, "gw_ffn_in")
    dx2, dsh2, dsc2, dg_norm2, dg1, dmix = _norm_bwd(x2, dh2, dx3, g_norm2, sc2, gate=(mix, g1))
    dy = _mm_nt(dmix, w_out, BF16, "d_y")
    gw_out = _mm_tn(y, [dmix], "gw_out")
    if on_ffn_grads is not None:
        g_out_a = g_out_a + on_ffn_grads(gw_ffn_in, gw_ffn_out, gw_out)
    dout_a, dout_b, dg_out_a, dg_out_b = _post_attn_bwd(dy, out_a, out_b, g_out_a, g_out_b)
    dq_b, dk_b, dv_b = _mla_bwd(q, k, v, out_b, dout_b, lse_b)
    dq_a, dk_a, dv_a, dlogits = _dil_bwd(proj, biasm, out_a, dout_a, lse_a)
    g_rel = _rel_bias_grad(dlogits)
    dqr, dproj, dg_cq, dg_ckv = _mla_pre_bwd(proj, dq_b, dk_b, dv_b, (dq_a, dk_a, dv_a), g_cq, g_ckv, w_uq_p, w_k, w_v,
                                             rc, rsm, rsp)
    gw_uq = _mm_tn(cqn, [dqr], "gw_uq")
    gw_k, gw_v = _mm_tn(ckvn, [dk_b, dv_b], "gw_kv")
    gw_in = _mm_tn_rows([dproj], h1, "gw_in")
    if on_last_grads is not None:
        started = on_last_grads(dict(w_in=gw_in, w_uq=gw_uq, w_k=gw_k, w_v=gw_v))
    else:
        started = None
    dh1 = _mm_nn(dproj, w_in_p, BF16, "d_h1", after=started)
    grad_x, dsh1, dsc1, dg_norm1 = _norm_bwd(x, dh1, dx2, g_norm1, sc1)

    dmod = [dsh1, dsc1, dg1, dsh2, dsc2, dg2]
    small = dict(g_norm1=dg_norm1, g_cq=dg_cq, g_ckv=dg_ckv, rel_bias=g_rel, g_out_a=dg_out_a, g_out_b=dg_out_b,
                 g_norm2=dg_norm2, g_final=dg_final)
    big = dict(w_in=gw_in, w_uq=gw_uq, w_k=gw_k, w_v=gw_v, w_out=gw_out, w_ffn_in=gw_ffn_in, w_ffn_out=gw_ffn_out)
    return grad_x, dmod, loss_cols, small, big


def kernel(x, c, w_ada, b_ada, g_norm1, w_in, g_cq, w_uq, g_ckv, w_ukv, rel_bias, g_out_a, g_out_b, w_out, g_norm2, w_ffn_in, w_ffn_out, g_final, loss_target, m_w_ada, m_b_ada, m_g_norm1, m_w_in, m_g_cq, m_w_uq, m_g_ckv, m_w_ukv, m_rel_bias, m_g_out_a, m_g_out_b, m_w_out, m_g_norm2, m_w_ffn_in, m_w_ffn_out, m_g_final, v_w_ada, v_b_ada, v_g_norm1, v_w_in, v_g_cq, v_w_uq, v_g_ckv, v_w_ukv, v_rel_bias, v_g_out_a, v_g_out_b, v_w_out, v_g_norm2, v_w_ffn_in, v_w_ffn_out, v_g_final):
    nb = x.shape[0]
    t = nb * S
    xt, tt = x.reshape(t, D), loss_target.reshape(t, D)
    me = 4 * lax.axis_index("x") + 2 * lax.axis_index("y") + lax.axis_index("c")

    early = [jnp.swapaxes(w_in, 1, 2)[0], w_uq[0], w_ukv[0]]
    gathered = _gather_two_level([_silu_rows(c)] + [s.astype(BF16) for s in early], "gather_weights")
    cond_all = gathered[0].reshape(N_DEV * nb, D)
    w_in_t = gathered[1].reshape(P_IN, D)
    w_uq_f, w_ukv_f = (_cols_from_blocks(g) for g in gathered[2:4])
    w_k, w_v = _split_w_ukv(w_ukv_f)

    ncol = N_MOD * D // N_DEV
    b_slab = lax.dynamic_slice(b_ada, (0, me * ncol), (1, ncol))
    slab = _mod_slab(cond_all, w_ada, b_slab)
    (mod_rows,) = _exchange([slab.reshape(N_DEV, nb, ncol)], [False], "scatter_mod")
    mod = jnp.transpose(mod_rows, (1, 0, 2)).reshape(nb, N_MOD, D)

    late = [s.astype(BF16) for s in (w_out[0], jnp.swapaxes(w_ffn_in, 1, 2)[0], w_ffn_out[0])]
    late_send, late_recv, late_src, late_land, late_token = _exchange_start(
        late, [_own_block_in_place(s, me) for s in late], [True] * 3, mod_rows, "gather_late_start")
    g_norm1_t = g_norm1 + late_token[:1, :1]

    def late_weights(after):
        w_out_g, w_ffn_in_g, w_ffn_out_g = _exchange_wait(late_send, late_recv, late_src, late_land, [True] * 3, after,
                                                          "gather_late_wait")
        return w_out_g.reshape(D, D), w_ffn_in_g.reshape(2 * D_FF, D), w_ffn_out_g.reshape(D_FF, D)

    flight = {}

    def start_grads(key, src, name):
        land = [_own_block_in_place(lax.dynamic_index_in_dim(s, me, 0, keepdims=False), me) for s in src]
        send, recv, src, land, token = _exchange_start(src, land, [False] * len(src), src[0], name)
        flight[key] = (send, recv, src, land)
        return token[:1, :1]

    def on_ffn_grads(gw_ffn_in, gw_ffn_out, gw_out):
        return start_grads("ffn", [gw_ffn_in.reshape(N_DEV, 2 * D_FF // N_DEV, D), gw_ffn_out.reshape(N_DEV, D_FF // N_DEV, D),
                                   gw_out.reshape(N_DEV, D // N_DEV, D)], "exchange_ffn_start")

    def on_last_grads(gw):
        return start_grads("rest", [_unpad_w_in(gw["w_in"]).reshape(N_DEV, P_IN // N_DEV, D),
                                    _cols_to_blocks(_unpad_w_uq(gw["w_uq"])),
                                    _cols_to_blocks(_join_w_ukv(gw["w_k"], gw["w_v"]))], "exchange_rest_start")

    grad_x, dmod, loss_cols, small, _ = _local_step(
        xt, mod, tt, g_norm1_t, _pad_w_in(w_in_t), g_cq, _pad_w_uq(w_uq_f), g_ckv, w_k, w_v, rel_bias, g_out_a, g_out_b,
        None, g_norm2, None, None, g_final.reshape(1, D), late_weights=late_weights, on_ffn_grads=on_ffn_grads,
        on_last_grads=on_last_grads)

    upd = {}

    def land_and_update(key, names, after, name):
        got = _exchange_wait(*flight[key], [False] * len(names), after, name)
        for n, p in zip(names, got):
            w, m, v = big[n]
            upd[n] = _adamw(p, w, m, v, "adamw_" + n)

    def flip(a):
        return jnp.swapaxes(a, 1, 2)

    big = dict(w_in=(flip(w_in), flip(m_w_in), flip(v_w_in)), w_uq=(w_uq, m_w_uq, v_w_uq), w_ukv=(w_ukv, m_w_ukv, v_w_ukv),
               w_out=(w_out, m_w_out, v_w_out), w_ffn_in=(flip(w_ffn_in), flip(m_w_ffn_in), flip(v_w_ffn_in)),
               w_ffn_out=(w_ffn_out, m_w_ffn_out, v_w_ffn_out))
    land_and_update("ffn", ["w_ffn_in", "w_ffn_out", "w_out"], grad_x, "exchange_ffn_wait")
    land_and_update("rest", ["w_in", "w_uq", "w_ukv"], upd["w_out"][0], "exchange_rest_wait")
    for n in ("w_in", "w_ffn_in"):
        upd[n] = tuple(flip(a) for a in upd[n])

    mine, dmod_blocks = _pack_small(dmod, [small[n] for n, _ in ROW_PARAMS], loss_cols)
    dmod_cols, pay, rel = _exchange([dmod_blocks, mine, small["rel_bias"]], [False, True, True], "exchange_small",
                                    after=upd["w_ukv"][0])
    g_ada = _ada_grad(cond_all, dmod_cols.reshape(N_DEV * nb, ncol))
    upd["w_ada"] = _adamw(g_ada[None], w_ada, m_w_ada, v_w_ada, "adamw_w_ada")
    row = lambda a: a.reshape(1, D)
    small_names = ["b_ada"] + [n for n, _ in ROW_PARAMS] + ["rel_bias"]
    small_w = [b_ada, g_norm1, g_cq, g_ckv, g_out_a, g_out_b, g_norm2, row(g_final), rel_bias]
    small_m = [m_b_ada, m_g_norm1, m_g_cq, m_g_ckv, m_g_out_a, m_g_out_b, m_g_norm2, row(m_g_final), m_rel_bias]
    small_v = [v_b_ada, v_g_norm1, v_g_cq, v_g_ckv, v_g_out_a, v_g_out_b, v_g_norm2, row(v_g_final), v_rel_bias]
    small_upd, loss8 = _small_update(pay, rel, small_w, small_m, small_v)
    upd.update(zip(small_names, small_upd))

    order = ["w_ada", "b_ada", "g_norm1", "w_in", "g_cq", "w_uq", "g_ckv", "w_ukv", "rel_bias", "g_out_a", "g_out_b",
             "w_out", "g_norm2", "w_ffn_in", "w_ffn_out", "g_final"]
    like = dict(g_final=g_final)
    outs = [loss8[0, 0], grad_x.reshape(x.shape)]
    for part in range(4):
        for n in order:
            val = upd[n][part]
            outs.append(val.reshape(like[n].shape) if n in like else val)
    return tuple(outs)
```

```python
import functools

import numpy as np
import jax
import jax.numpy as jnp
from jax import lax
from jax.experimental import pallas as pl
from jax.experimental.pallas import tpu as pltpu

F32, BF16 = jnp.float32, jnp.bfloat16

N_DEV = 8
D = 1024
S = 2048
H = 8
E_A = 64
D_A = H * E_A
Q_LORA, KV_LORA = 384, 256
NOPE, ROPE, VDIM = 64, 32, 64
HP = 128
P_IN = 3 * D_A + Q_LORA + KV_LORA + ROPE
P_PAD = 3 * D_A + Q_LORA + KV_LORA + HP
TAIL0 = 3 * D_A
TAIL = P_PAD - TAIL0
D_FF = 2816
N_MOD = 6
EPS = 1e-6
NEG = -1e30
BLK = 128
DILATIONS = (1, 4, 16)
N_BUCKETS, MAX_DISTANCE = 32, 2048
ROPE_THETA = 10000.0
SCALE_A = E_A ** -0.5
SCALE_B = (NOPE + ROPE) ** -0.5
B1, B2, LR, ADAM_EPS, WD, STEP = 0.9, 0.999, 0.001, 1e-8, 0.01, 10
VMEM_LIMIT = 56 * 1024 * 1024


def _cp(*sem):
    return pltpu.CompilerParams(dimension_semantics=sem, vmem_limit_bytes=VMEM_LIMIT)


def _pick(n, prefs):
    for p in prefs:
        if n % p == 0:
            return p
    raise ValueError(f"no tile of {prefs} divides {n}")


OPERAND_BYTES = 6 * 1024 * 1024


def _pick_rows(m, k):
    return _pick(m, [p for p in (1024, 512, 256, 128, 16) if p * k * 2 <= OPERAND_BYTES])


MATMUL_BYTES = 40 * 1024 * 1024


def _stream_rows(m, fixed, per_row):
    return _pick(m, [p for p in (4096, 2048, 1024, 512, 256, 128, 16) if fixed + p * per_row <= MATMUL_BYTES])


def _dot(a, b, dims):
    return lax.dot_general(a, b, (dims, ((), ())), preferred_element_type=F32)


def _mm_nn(a, b, out_dtype, name, after=None):
    m, k = a.shape
    n = b.shape[1]
    tn = _pick(n, (512, 384, 256, 128))
    tm = _stream_rows(m, 4 * k * tn, 4 * k + (2 * jnp.dtype(out_dtype).itemsize + 4) * tn)

    def body(a_ref, b_ref, *rest):
        o_ref = rest[-1]
        o_ref[...] = _dot(a_ref[...], b_ref[...], ((1,), (0,))).astype(o_ref.dtype)

    extra = [] if after is None else [after]
    return pl.pallas_call(
        body, name=name, grid=(m // tm, n // tn),
        in_specs=[pl.BlockSpec((tm, k), lambda i, j: (i, 0)), pl.BlockSpec((k, tn), lambda i, j: (0, j))] + [ANY] * len(extra),
        out_specs=pl.BlockSpec((tm, tn), lambda i, j: (i, j)),
        out_shape=jax.ShapeDtypeStruct((m, n), out_dtype),
        compiler_params=_cp("parallel", "parallel"),
    )(a, b, *extra)


def _mm_nt(a, b, out_dtype, name, after=None):
    m, k = a.shape
    n = b.shape[0]
    tn = _pick(n, (512, 384, 256, 128))
    tm = _stream_rows(m, 4 * k * tn, 4 * k + (2 * jnp.dtype(out_dtype).itemsize + 4) * tn)

    def body(a_ref, b_ref, *rest):
        o_ref = rest[-1]
        o_ref[...] = _dot(a_ref[...], b_ref[...], ((1,), (1,))).astype(o_ref.dtype)

    extra = [] if after is None else [after]
    return pl.pallas_call(
        body, name=name, grid=(m // tm, n // tn),
        in_specs=[pl.BlockSpec((tm, k), lambda i, j: (i, 0)), pl.BlockSpec((tn, k), lambda i, j: (j, 0))] + [ANY] * len(extra),
        out_specs=pl.BlockSpec((tm, tn), lambda i, j: (i, j)),
        out_shape=jax.ShapeDtypeStruct((m, n), out_dtype),
        compiler_params=_cp("parallel", "parallel"),
    )(a, b, *extra)


def _mm_tn(a, bs, name):
    t, m = a.shape
    n = bs[0].shape[1]
    nb_ = len(bs)
    tc = _pick(t, (512, 16))
    tn = _pick(n, (512, 384, 256, 128))
    tm = _pick(m, [p for p in (1024, 512, 384, 256, 128) if (3 * p + 2 * nb_ * tn) * t * 2 <= VMEM_LIMIT - 2 * OPERAND_BYTES])
    if tm <= 256 and nb_ * n * t * 2 <= 2 * OPERAND_BYTES:
        tn = n

    def body(*refs):
        a_ref, b_refs, o_refs, at_ref = refs[0], refs[1:1 + nb_], refs[1 + nb_:1 + 2 * nb_], refs[-1]

        @pl.when(pl.program_id(1) == 0)
        def _():
            def chunk(c, _):
                rows = pl.ds(pl.multiple_of(c * tc, tc), tc)
                at_ref[:, rows] = a_ref[rows, :].T
                return 0

            lax.fori_loop(0, t // tc, chunk, 0)

        for b_ref, o_ref in zip(b_refs, o_refs):
            o_ref[...] = _dot(at_ref[...], b_ref[...], ((1,), (0,))).astype(BF16)

    res = pl.pallas_call(
        body, name=name, grid=(m // tm, n // tn),
        in_specs=[pl.BlockSpec((t, tm), lambda i, j: (0, i))] + [pl.BlockSpec((t, tn), lambda i, j: (0, j))] * nb_,
        out_specs=[pl.BlockSpec((tm, tn), lambda i, j: (i, j))] * nb_,
        out_shape=[jax.ShapeDtypeStruct((m, n), BF16)] * nb_,
        scratch_shapes=[pltpu.VMEM((tm, t), BF16)],
        compiler_params=_cp("parallel", "arbitrary"),
    )(a, *bs)
    return res[0] if nb_ == 1 else res


def _mm_tn_rows(a_list, b, name):
    t, m = a_list[0].shape
    n = b.shape[1]
    na = len(a_list)
    tc, tm = _pick(t, (512, 16)), _pick(m, (256, 128))
    nblk = m // tm

    def body(*refs):
        a_refs, b_ref, o_ref, bt_ref, r_ref = refs[:na], refs[na], refs[na + 1], refs[na + 2], refs[na + 3]
        i = pl.program_id(0)

        @pl.when(i == 0)
        def _():
            def chunk(c, _):
                rows = pl.ds(pl.multiple_of(c * tc, tc), tc)
                bt_ref[:, rows] = b_ref[rows, :].T
                return 0

            lax.fori_loop(0, t // tc, chunk, 0)

        for s, a_ref in enumerate(a_refs):
            @pl.when((i >= s * nblk) & (i < (s + 1) * nblk))
            def _(a_ref=a_ref):
                r_ref[...] = _dot(bt_ref[...], a_ref[...], ((1,), (0,)))
                o_ref[...] = r_ref[...].T.astype(BF16)

    return pl.pallas_call(
        body, name=name, grid=(na * nblk,),
        in_specs=[pl.BlockSpec((t, tm), lambda i, s=s: (0, jnp.clip(i - s * nblk, 0, nblk - 1))) for s in range(na)]
        + [pl.BlockSpec((t, n), lambda i: (0, 0))],
        out_specs=pl.BlockSpec((tm, n), lambda i: (i, 0)),
        out_shape=jax.ShapeDtypeStruct((na * m, n), BF16),
        scratch_shapes=[pltpu.VMEM((n, t), BF16), pltpu.VMEM((n, tm), F32)],
        compiler_params=_cp("arbitrary"),
    )(*a_list, b)


EPI = 256


def _silu_parts(g):
    sg = 0.5 * jnp.tanh(0.5 * g) + 0.5
    return sg, g * sg


def _ffn_in(h2, wt):
    t, k = h2.shape
    tn = _pick(D_FF, (256, 128))
    tm = _stream_rows(t, 8 * k * tn, 4 * k + (3 * 2 * 2 + 2 * 4) * tn)
    nj = D_FF // tn

    def body(h_ref, wg_ref, wu_ref, g_ref, u_ref, a_ref):
        hv = h_ref[...]
        g_all = _dot(hv, wg_ref[...], ((1,), (1,)))
        u_all = _dot(hv, wu_ref[...], ((1,), (1,)))
        for r in range(tm // EPI):
            rows = slice(r * EPI, (r + 1) * EPI)
            g, u = g_all[rows], u_all[rows]
            g_ref[rows, :] = g.astype(BF16)
            u_ref[rows, :] = u.astype(BF16)
            a_ref[rows, :] = (_silu_parts(g)[1] * u).astype(BF16)

    blk = pl.BlockSpec((tm, tn), lambda i, j: (i, j))
    return pl.pallas_call(
        body, name="ffn_in", grid=(t // tm, nj),
        in_specs=[pl.BlockSpec((tm, k), lambda i, j: (i, 0)), pl.BlockSpec((tn, k), lambda i, j: (j, 0)),
                  pl.BlockSpec((tn, k), lambda i, j: (j + nj, 0))],
        out_specs=[blk] * 3, out_shape=[jax.ShapeDtypeStruct((t, D_FF), BF16)] * 3,
        compiler_params=_cp("parallel", "parallel"),
    )(h2, wt, wt)


def _d_act(df, w, g, u):
    t, k = df.shape
    tn = _pick(D_FF, (256, 128))
    tm = _stream_rows(t, 4 * k * tn, 4 * k + (4 * 2 * 2 + 4) * tn)

    def body(df_ref, w_ref, g_ref, u_ref, dg_ref, du_ref):
        da_all = _dot(df_ref[...], w_ref[...], ((1,), (1,)))
        for r in range(tm // EPI):
            rows = slice(r * EPI, (r + 1) * EPI)
            da = da_all[rows]
            gv = g_ref[rows, :].astype(F32)
            sg, silu = _silu_parts(gv)
            dg_ref[rows, :] = ((da * u_ref[rows, :].astype(F32)) * (sg + silu * (1.0 - sg))).astype(BF16)
            du_ref[rows, :] = (da * silu).astype(BF16)

    blk = pl.BlockSpec((tm, tn), lambda i, j: (i, j))
    return pl.pallas_call(
        body, name="d_act", grid=(t // tm, D_FF // tn),
        in_specs=[pl.BlockSpec((tm, k), lambda i, j: (i, 0)), pl.BlockSpec((tn, k), lambda i, j: (j, 0)), blk, blk],
        out_specs=[blk] * 2, out_shape=[jax.ShapeDtypeStruct((t, D_FF), BF16)] * 2,
        compiler_params=_cp("parallel", "parallel"),
    )(df, w, g, u)


def _d_h2(dg, du, wt):
    t = dg.shape[0]
    n = wt.shape[1]
    tm, tn = _pick_rows(t, D_FF), _pick(n, (512, 256, 128))

    def body(dg_ref, du_ref, wg_ref, wu_ref, o_ref):
        o_ref[...] = (_dot(dg_ref[...], wg_ref[...], ((1,), (0,)))
                      + _dot(du_ref[...], wu_ref[...], ((1,), (0,)))).astype(BF16)

    return pl.pallas_call(
        body, name="d_h2", grid=(t // tm, n // tn),
        in_specs=[pl.BlockSpec((tm, D_FF), lambda i, j: (i, 0)), pl.BlockSpec((tm, D_FF), lambda i, j: (i, 0)),
                  pl.BlockSpec((D_FF, tn), lambda i, j: (0, j)), pl.BlockSpec((D_FF, tn), lambda i, j: (1, j))],
        out_specs=pl.BlockSpec((tm, tn), lambda i, j: (i, j)),
        out_shape=jax.ShapeDtypeStruct((t, n), BF16),
        compiler_params=_cp("parallel", "parallel"),
    )(dg, du, wt, wt)


TM = 1024


def _row(w):
    return pl.BlockSpec((TM, w), lambda i: (i, 0))


def _row_at(w, col):
    return pl.BlockSpec((TM, w), lambda i: (i, col))


def _vec(w):
    return pl.BlockSpec((1, w), lambda i: (0, 0))


def _per_ex(w):
    return pl.BlockSpec((1, 1, w), lambda i: (i // (S // TM), 0, 0))


def _pos(w):
    return pl.BlockSpec((TM, w), lambda i: (i % (S // TM), 0))


def _full(shape):
    return pl.BlockSpec(shape, lambda i: (0,) * len(shape))


def _rms(x):
    return lax.rsqrt(jnp.mean(x * x, axis=-1, keepdims=True) + EPS)


def _rms_bwd(n, r, dn):
    return r * (dn - n * jnp.mean(dn * n, axis=-1, keepdims=True))


def _colsum(v):
    return jnp.sum(v, axis=0, keepdims=True)


def _acc_first(i, ref, val, every=None):
    first = (i == 0) if every is None else (i % every == 0)

    @pl.when(first)
    def _():
        ref[...] = jnp.zeros_like(ref)

    ref[...] += val.reshape(ref.shape)


def _pre1(x, g, sc, sh):
    t = x.shape[0]

    def body(x_ref, g_ref, sc_ref, sh_ref, h_ref):
        xv = x_ref[...]
        n = xv * _rms(xv)
        h_ref[...] = ((n * g_ref[...]) * (1.0 + sc_ref[0]) + sh_ref[0]).astype(BF16)

    return pl.pallas_call(
        body, name="pre1", grid=(t // TM,),
        in_specs=[_row(D), _vec(D), _per_ex(D), _per_ex(D)],
        out_specs=_row(D), out_shape=jax.ShapeDtypeStruct((t, D), BF16),
        compiler_params=_cp("parallel"),
    )(x, g, sc, sh)


def _rope_fwd(v, c, sm, sp):
    return v * c + pltpu.roll(v, HP - ROPE // 2, 1) * sm + pltpu.roll(v, ROPE // 2, 1) * sp


def _rope_bwd(dv, c, sm, sp):
    return dv * c + pltpu.roll(dv * sm, ROPE // 2, 1) + pltpu.roll(dv * sp, HP - ROPE // 2, 1)


def _mla_pre(proj, g_cq, g_ckv, w_uq, w_k, w_v, rc, rsm, rsp):
    t = proj.shape[0]

    def body(tail_ref, gq_ref, gkv_ref, wuq_ref, wk_ref, wv_ref, c_ref, sm_ref, sp_ref,
             q_ref, k_ref, v_ref, cqn_ref, ckvn_ref):
        tail = tail_ref[...]
        cq, ckv, kr = tail[:, :Q_LORA], tail[:, Q_LORA:Q_LORA + KV_LORA], tail[:, Q_LORA + KV_LORA:]
        cqn = (cq * _rms(cq) * gq_ref[...]).astype(BF16)
        ckvn = (ckv * _rms(ckv) * gkv_ref[...]).astype(BF16)
        cqn_ref[...] = cqn
        ckvn_ref[...] = ckvn
        c, sm, sp = c_ref[...], sm_ref[...], sp_ref[...]
        q = _dot(cqn, wuq_ref[...], ((1,), (1,)))
        kn = _dot(ckvn, wk_ref[...], ((1,), (0,)))
        v_ref[...] = _dot(ckvn, wv_ref[...], ((1,), (0,))).astype(BF16)
        krr = _rope_fwd(kr, c, sm, sp)
        for h in range(H):
            sl = slice(h * HP, (h + 1) * HP)
            q_ref[:, sl] = _rope_fwd(q[:, sl], c, sm, sp).astype(BF16)
            k_ref[:, sl] = (kn[:, sl] + krr).astype(BF16)

    wide = H * HP
    return pl.pallas_call(
        body, name="mla_pre", grid=(t // TM,),
        in_specs=[_row_at(TAIL, TAIL0 // TAIL), _vec(Q_LORA), _vec(KV_LORA), _full((wide, Q_LORA)),
                  _full((KV_LORA, wide)), _full((KV_LORA, wide)), _pos(HP), _pos(HP), _pos(HP)],
        out_specs=[_row(wide), _row(wide), _row(wide), _row(Q_LORA), _row(KV_LORA)],
        out_shape=[jax.ShapeDtypeStruct((t, wide), BF16)] * 3
        + [jax.ShapeDtypeStruct((t, Q_LORA), BF16), jax.ShapeDtypeStruct((t, KV_LORA), BF16)],
        compiler_params=_cp("parallel"),
    )(proj, g_cq, g_ckv, w_uq, w_k, w_v, rc, rsm, rsp)


def _mla_pre_bwd(proj, dq_, dk_, dv_, dqkv_a, g_cq, g_ckv, w_uq, w_k, w_v, rc, rsm, rsp):
    t = proj.shape[0]
    wide = H * HP

    def body(tail_ref, dq_ref, dk_ref, dv_ref, dqa_ref, dka_ref, dva_ref, gq_ref, gkv_ref, wuq_ref, wk_ref, wv_ref,
             c_ref, sm_ref, sp_ref, dqo_ref, dproj_ref, dgq_ref, dgkv_ref):
        i = pl.program_id(0)
        for n, src in enumerate((dqa_ref, dka_ref, dva_ref)):
            dproj_ref[:, n * D_A:(n + 1) * D_A] = src[...]
        dtail_ref = dproj_ref.at[:, TAIL0:]
        tail = tail_ref[...]
        cq, ckv = tail[:, :Q_LORA], tail[:, Q_LORA:Q_LORA + KV_LORA]
        c, sm, sp = c_ref[...], sm_ref[...], sp_ref[...]
        dkr = jnp.zeros((TM, HP), F32)
        for h in range(H):
            sl = slice(h * HP, (h + 1) * HP)
            dqo_ref[:, sl] = _rope_bwd(dq_ref[:, sl].astype(F32), c, sm, sp).astype(BF16)
            dkr = dkr + dk_ref[:, sl].astype(F32)
        lane = lax.broadcasted_iota(jnp.int32, (TM, HP), 1)
        dkr = jnp.where((lane >= NOPE) & (lane < NOPE + ROPE), _rope_bwd(dkr, c, sm, sp), 0.0)
        dkb = dk_ref[...]
        dvb = dv_ref[...]
        dcqn = _dot(dqo_ref[...], wuq_ref[...], ((1,), (0,)))
        dckvn = _dot(dkb, wk_ref[...], ((1,), (1,))) + _dot(dvb, wv_ref[...], ((1,), (1,)))
        rq, rkv = _rms(cq), _rms(ckv)
        nq, nkv = cq * rq, ckv * rkv
        _acc_first(i, dgq_ref, _colsum(dcqn * nq))
        _acc_first(i, dgkv_ref, _colsum(dckvn * nkv))
        dtail_ref[:, :Q_LORA] = _rms_bwd(nq, rq, dcqn * gq_ref[...]).astype(BF16)
        dtail_ref[:, Q_LORA:Q_LORA + KV_LORA] = _rms_bwd(nkv, rkv, dckvn * gkv_ref[...]).astype(BF16)
        dtail_ref[:, Q_LORA + KV_LORA:] = dkr.astype(BF16)

    return pl.pallas_call(
        body, name="mla_pre_bwd", grid=(t // TM,),
        in_specs=[_row_at(TAIL, TAIL0 // TAIL), _row(wide), _row(wide), _row(wide), _row(D_A), _row(D_A), _row(D_A),
                  _vec(Q_LORA), _vec(KV_LORA), _full((wide, Q_LORA)), _full((KV_LORA, wide)), _full((KV_LORA, wide)),
                  _pos(HP), _pos(HP), _pos(HP)],
        out_specs=[_row(wide), _row(P_PAD), _vec(Q_LORA), _vec(KV_LORA)],
        out_shape=[jax.ShapeDtypeStruct((t, wide), BF16), jax.ShapeDtypeStruct((t, P_PAD), BF16),
                   jax.ShapeDtypeStruct((1, Q_LORA), F32), jax.ShapeDtypeStruct((1, KV_LORA), F32)],
        compiler_params=_cp("arbitrary"),
    )(proj, dq_, dk_, dv_, *dqkv_a, g_cq, g_ckv, w_uq, w_k, w_v, rc, rsm, rsp)


def _post_attn(out_a, out_b, g_a, g_b):
    t = out_a.shape[0]

    def body(a_ref, b_ref, ga_ref, gb_ref, y_ref):
        a, b = a_ref[...], b_ref[...]
        y_ref[:, :D_A] = (a * _rms(a) * ga_ref[...]).astype(BF16)
        y_ref[:, D_A:] = (b * _rms(b) * gb_ref[...]).astype(BF16)

    return pl.pallas_call(
        body, name="post_attn", grid=(t // TM,),
        in_specs=[_row(D_A), _row(D_A), _vec(D_A), _vec(D_A)],
        out_specs=_row(D), out_shape=jax.ShapeDtypeStruct((t, D), BF16),
        compiler_params=_cp("parallel"),
    )(out_a, out_b, g_a, g_b)


def _post_attn_bwd(dy, out_a, out_b, g_a, g_b):
    t = dy.shape[0]

    def body(dy_ref, a_ref, b_ref, ga_ref, gb_ref, da_ref, db_ref, dga_ref, dgb_ref):
        i = pl.program_id(0)
        dy_ = dy_ref[...].astype(F32)
        for src, g_ref, dst, dg_ref, sl in ((a_ref, ga_ref, da_ref, dga_ref, slice(0, D_A)),
                                            (b_ref, gb_ref, db_ref, dgb_ref, slice(D_A, D))):
            v = src[...]
            r = _rms(v)
            n = v * r
            dyv = dy_[:, sl]
            _acc_first(i, dg_ref, _colsum(dyv * n))
            dst[...] = _rms_bwd(n, r, dyv * g_ref[...])

    return pl.pallas_call(
        body, name="post_attn_bwd", grid=(t // TM,),
        in_specs=[_row(D), _row(D_A), _row(D_A), _vec(D_A), _vec(D_A)],
        out_specs=[_row(D_A), _row(D_A), _vec(D_A), _vec(D_A)],
        out_shape=[jax.ShapeDtypeStruct((t, D_A), F32)] * 2 + [jax.ShapeDtypeStruct((1, D_A), F32)] * 2,
        compiler_params=_cp("arbitrary"),
    )(dy, out_a, out_b, g_a, g_b)


def _resid_norm2(x, mix, g1, g, sc, sh):
    t = x.shape[0]

    def body(x_ref, mix_ref, g1_ref, g_ref, sc_ref, sh_ref, x2_ref, h_ref):
        x2 = x_ref[...] + g1_ref[0] * mix_ref[...]
        x2_ref[...] = x2
        n = x2 * _rms(x2)
        h_ref[...] = ((n * g_ref[...]) * (1.0 + sc_ref[0]) + sh_ref[0]).astype(BF16)

    return pl.pallas_call(
        body, name="resid_norm2", grid=(t // TM,),
        in_specs=[_row(D), _row(D), _per_ex(D), _vec(D), _per_ex(D), _per_ex(D)],
        out_specs=[_row(D), _row(D)],
        out_shape=[jax.ShapeDtypeStruct((t, D), F32), jax.ShapeDtypeStruct((t, D), BF16)],
        compiler_params=_cp("parallel"),
    )(x, mix, g1, g, sc, sh)


def _sigmoid(v):
    return 1.0 / (1.0 + jnp.exp(-v))


def _final(x2, f, g2, g_fin, target):
    t = x2.shape[0]
    nb = t // S
    tpb = S // TM

    def body(x2_ref, f_ref, g2_ref, g_ref, t_ref, dx3_ref, df_ref, loss_ref, dgf_ref, dg2_ref):
        i = pl.program_id(0)
        fv = f_ref[...].astype(F32)
        x3 = x2_ref[...] + g2_ref[0] * fv
        r = _rms(x3)
        n = x3 * r
        err = n * g_ref[...] - t_ref[...]
        _acc_first(i, loss_ref, _colsum(err * err))
        dy = err * (1.0 / D)
        _acc_first(i, dgf_ref, _colsum(dy * n))
        dx3 = _rms_bwd(n, r, dy * g_ref[...])
        dx3_ref[...] = dx3
        _acc_first(i, dg2_ref, _colsum(dx3 * fv), every=tpb)
        df_ref[...] = (dx3 * g2_ref[0]).astype(BF16)

    return pl.pallas_call(
        body, name="final", grid=(t // TM,),
        in_specs=[_row(D), _row(D), _per_ex(D), _vec(D), _row(D)],
        out_specs=[_row(D), _row(D), _vec(D), _vec(D), _per_ex(D)],
        out_shape=[jax.ShapeDtypeStruct((t, D), F32), jax.ShapeDtypeStruct((t, D), BF16),
                   jax.ShapeDtypeStruct((1, D), F32), jax.ShapeDtypeStruct((1, D), F32),
                   jax.ShapeDtypeStruct((nb, 1, D), F32)],
        compiler_params=_cp("arbitrary"),
    )(x2, f, g2, g_fin, target)


def _norm_bwd(xin, dh, dres, g, sc, gate=None):
    t = xin.shape[0]
    nb = t // S
    tpb = S // TM
    gated = gate is not None

    def body(*refs):
        if gated:
            x_ref, dh_ref, dres_ref, g_ref, sc_ref, mix_ref, g1_ref, dx_ref, dsh_ref, dsc_ref, dg_ref, dg1_ref, dmix_ref = refs
        else:
            x_ref, dh_ref, dres_ref, g_ref, sc_ref, dx_ref, dsh_ref, dsc_ref, dg_ref = refs
        i = pl.program_id(0)
        xv, dhv = x_ref[...], dh_ref[...].astype(F32)
        r = _rms(xv)
        n = xv * r
        gv = g_ref[...]
        _acc_first(i, dsh_ref, _colsum(dhv), every=tpb)
        _acc_first(i, dsc_ref, _colsum(dhv * (n * gv)), every=tpb)
        dng = dhv * (1.0 + sc_ref[0])
        _acc_first(i, dg_ref, _colsum(dng * n))
        dx = dres_ref[...] + _rms_bwd(n, r, dng * gv)
        dx_ref[...] = dx
        if gated:
            _acc_first(i, dg1_ref, _colsum(dx * mix_ref[...].astype(F32)), every=tpb)
            dmix_ref[...] = (dx * g1_ref[0]).astype(BF16)

    in_specs = [_row(D), _row(D), _row(D), _vec(D), _per_ex(D)]
    out_specs = [_row(D), _per_ex(D), _per_ex(D), _vec(D)]
    out_shape = [jax.ShapeDtypeStruct((t, D), F32), jax.ShapeDtypeStruct((nb, 1, D), F32),
                 jax.ShapeDtypeStruct((nb, 1, D), F32), jax.ShapeDtypeStruct((1, D), F32)]
    args = [xin, dh, dres, g, sc]
    if gated:
        in_specs += [_row(D), _per_ex(D)]
        out_specs += [_per_ex(D), _row(D)]
        out_shape += [jax.ShapeDtypeStruct((nb, 1, D), F32), jax.ShapeDtypeStruct((t, D), BF16)]
        args += list(gate)
    return pl.pallas_call(
        body, name="norm2_bwd" if gated else "norm1_bwd", grid=(t // TM,),
        in_specs=in_specs, out_specs=out_specs, out_shape=out_shape,
        compiler_params=_cp("arbitrary"),
    )(*args)


TQ = 256
TB = 512
FWD_HEADS = 2


def _mla_fwd(q, k, v):
    t = q.shape[0]
    nb = t // S

    def body(q_ref, k_ref, v_ref, o_ref, lse_ref):
        causal = lax.broadcasted_iota(jnp.int32, (TB, TB), 0) >= lax.broadcasted_iota(jnp.int32, (TB, TB), 1)
        heads = [slice(h * HP, (h + 1) * HP) for h in range(FWD_HEADS)]
        for i in range(S // TB):
            ri, past = slice(i * TB, (i + 1) * TB), slice(0, i * TB)
            qhs = [q_ref[ri, sl] for sl in heads]
            sd = [jnp.where(causal, _dot(qh, k_ref[ri, sl], ((1,), (1,))) * SCALE_B, NEG) for qh, sl in zip(qhs, heads)]
            ms = [jnp.max(s, axis=-1, keepdims=True) for s in sd]
            if i:
                so = [_dot(qh, k_ref[past, sl], ((1,), (1,))) * SCALE_B for qh, sl in zip(qhs, heads)]
                ms = [jnp.maximum(m, jnp.max(s, axis=-1, keepdims=True)) for m, s in zip(ms, so)]
            pd = [jnp.exp(s - m) for s, m in zip(sd, ms)]
            ls = [jnp.sum(p, axis=-1, keepdims=True) for p in pd]
            acc = [_dot(p.astype(BF16), v_ref[ri, sl], ((1,), (0,))) for p, sl in zip(pd, heads)]
            if i:
                po = [jnp.exp(s - m) for s, m in zip(so, ms)]
                ls = [l + jnp.sum(p, axis=-1, keepdims=True) for l, p in zip(ls, po)]
                acc = [a + _dot(p.astype(BF16), v_ref[past, sl], ((1,), (0,))) for a, p, sl in zip(acc, po, heads)]
            for pr in range(FWD_HEADS // 2):
                o_ref[ri, pr * HP:(pr + 1) * HP] = acc[2 * pr] / ls[2 * pr] + acc[2 * pr + 1] / ls[2 * pr + 1]
            for sl, m, l in zip(heads, ms, ls):
                lse_ref[ri, sl] = jnp.broadcast_to(m + jnp.log(l), (TB, HP))

    wide2 = pl.BlockSpec((S, FWD_HEADS * HP), lambda b, p: (b, p))
    return pl.pallas_call(
        body, name="mla_fwd", grid=(nb, H // FWD_HEADS),
        in_specs=[wide2, wide2, wide2],
        out_specs=[pl.BlockSpec((S, FWD_HEADS // 2 * HP), lambda b, p: (b, p)), wide2],
        out_shape=[jax.ShapeDtypeStruct((t, H * VDIM), F32), jax.ShapeDtypeStruct((t, H * HP), F32)],
        compiler_params=_cp("parallel", "parallel"),
    )(q, k, v)


def _mla_bwd(q, k, v, o, do, lse):
    t = q.shape[0]
    nb = t // S

    def body(q_ref, k_ref, v_ref, o_ref, do_ref, lse_ref, dq_out, dk_out, dv_out, dq_ref, dk_ref, dv_ref):
        lane = lax.broadcasted_iota(jnp.int32, (TB, HP), 1)
        causal = lax.broadcasted_iota(jnp.int32, (TB, TB), 0) >= lax.broadcasted_iota(jnp.int32, (TB, TB), 1)
        heads = [slice(h * HP, (h + 1) * HP) for h in range(2)]
        nblk = S // TB
        for i in reversed(range(nblk)):
            ri, past = slice(i * TB, (i + 1) * TB), slice(0, i * TB)
            dov = do_ref[ri, :]
            prod = dov * o_ref[ri, :]
            dob = dov.astype(BF16)
            deltas = [jnp.sum(jnp.where((lane < VDIM) if h == 0 else (lane >= VDIM), prod, 0.0), axis=-1, keepdims=True)
                      for h in range(2)]
            qhs = [q_ref[ri, sl] for sl in heads]
            lses = [lse_ref[ri, sl][:, :1] for sl in heads]
            for rows, diagonal in ((ri, True), (past, False)):
                if rows.stop == rows.start:
                    continue
                ps = [jnp.exp(_dot(qh, k_ref[rows, sl], ((1,), (1,))) * SCALE_B - lse) for qh, sl, lse in zip(qhs, heads, lses)]
                if diagonal:
                    ps = [jnp.where(causal, p, 0.0) for p in ps]
                dps = [_dot(dob, v_ref[rows, sl], ((1,), (1,))) for sl in heads]
                dss = [(p * (dp - delta) * SCALE_B).astype(BF16) for p, dp, delta in zip(ps, dps, deltas)]
                for sl, qh, p, ds in zip(heads, qhs, ps, dss):
                    dq = _dot(ds, k_ref[rows, sl], ((1,), (0,)))
                    dk = _dot(ds, qh, ((0,), (0,)))
                    dv = _dot(p.astype(BF16), dob, ((0,), (0,)))
                    if diagonal:
                        dq_ref[ri, sl] = dq
                    else:
                        dq_ref[ri, sl] += dq
                    if i == nblk - 1:
                        dk_ref[rows, sl] = dk
                        dv_ref[rows, sl] = dv
                    else:
                        dk_ref[rows, sl] += dk
                        dv_ref[rows, sl] += dv
        dq_out[...] = dq_ref[...].astype(BF16)
        dk_out[...] = dk_ref[...].astype(BF16)
        dv_out[...] = dv_ref[...].astype(BF16)

    wide2 = pl.BlockSpec((S, 2 * HP), lambda b, p: (b, p))
    pair = pl.BlockSpec((S, HP), lambda b, p: (b, p))
    return pl.pallas_call(
        body, name="mla_bwd", grid=(nb, H // 2),
        in_specs=[wide2, wide2, wide2, pair, pair, wide2],
        out_specs=[wide2, wide2, wide2],
        out_shape=[jax.ShapeDtypeStruct((t, H * HP), BF16)] * 3,
        scratch_shapes=[pltpu.VMEM((S, 2 * HP), F32)] * 3,
        compiler_params=_cp("parallel", "parallel"),
    )(q, k, v, o, do, lse)


def _t5_bucket(dist):
    max_exact = N_BUCKETS // 2
    d = np.maximum(dist, 1).astype(np.float64)
    large = max_exact + (np.log(d / max_exact) / np.log(MAX_DISTANCE / max_exact) * (N_BUCKETS - max_exact)).astype(np.int64)
    large = np.minimum(large, N_BUCKETS - 1)
    return np.where(dist < max_exact, dist, large).astype(np.int32)


def _band_geometry():
    a = np.arange(BLK)[:, None]
    bk = np.arange(2 * BLK)[None, :]
    steps = BLK + a - bk
    valid = (steps >= 0) & (steps <= BLK)
    buckets = np.stack([_t5_bucket(np.clip(steps, 0, BLK) * d) for d in DILATIONS])
    return buckets, valid


def _band_bias(rel_bias):
    buckets, valid = _band_geometry()
    onehot = (jnp.asarray(buckets)[..., None] == jnp.arange(N_BUCKETS)).astype(F32)
    bias = jnp.einsum("rqkn,nh->rhqk", onehot, rel_bias, precision=lax.Precision.HIGHEST)
    bias = jnp.where(jnp.asarray(valid)[None, None], bias, NEG)
    return bias.reshape(3, H // 2, 2 * BLK, 2 * BLK)


def _dil_items():
    items = []
    for r, d in enumerate(DILATIONS):
        for res in range(d):
            for blk in range(S // d // BLK):
                items.append((r, d, blk * BLK * d + res, blk > 0))
    return items


GROUP = 4


def _strided(start, d):
    return pl.ds(start, BLK) if d == 1 else pl.ds(start, BLK, stride=d)


def _stack_heads(tile, own):
    return jnp.where(own, jnp.concatenate([tile, tile], axis=0), 0.0).astype(BF16)


def _own_lanes():
    row = lax.broadcasted_iota(jnp.int32, (2 * BLK, HP), 0)
    lane = lax.broadcasted_iota(jnp.int32, (2 * BLK, HP), 1)
    return (lane < E_A) == (row < BLK)


def _dil_fwd(proj, biasm):
    t = proj.shape[0]
    nb = t // S

    def body(q_ref, k_ref, v_ref, b_ref, o_ref, lse_ref, ob_ref, lb_ref):
        lane = lax.broadcasted_iota(jnp.int32, (BLK, HP), 1)
        own = _own_lanes()
        items = _dil_items()
        for g in range(0, len(items), GROUP):
            grp = items[g:g + GROUP]
            ss, vts = [], []
            for r, d, start, has_prev in grp:
                cur = _strided(start, d)
                rows = [_strided(start - BLK * d, d), cur] if has_prev else [cur]
                q2 = _stack_heads(q_ref[cur, :] * SCALE_A, own)
                kt = jnp.concatenate([k_ref[x, :] for x in rows], axis=0).astype(BF16)
                vts.append(jnp.concatenate([v_ref[x, :] for x in rows], axis=0).astype(BF16))
                bias = b_ref[r, 0] if has_prev else b_ref[r, 0, :, BLK:]
                ss.append(_dot(q2, kt, ((1,), (1,))) + bias)
            ms = [jnp.max(s, axis=-1, keepdims=True) for s in ss]
            ps = [jnp.exp(s - m) for s, m in zip(ss, ms)]
            ls = [jnp.sum(p, axis=-1, keepdims=True) for p in ps]
            for (r, d, start, _), p, vt, m, l in zip(grp, ps, vts, ms, ls):
                cur = _strided(start, d)
                o2 = _dot(p.astype(BF16), vt, ((1,), (0,))) / l
                lse2 = m + jnp.log(l)
                ob_ref[r, cur, :] = jnp.where(lane < E_A, o2[:BLK], o2[BLK:])
                lb_ref[r, cur, :] = jnp.where(lane < E_A, lse2[:BLK], lse2[BLK:])

        def merge(c, _):
            rows = pl.ds(pl.multiple_of(c * TQ, TQ), TQ)
            l0, l1, l2 = lb_ref[0, rows, :], lb_ref[1, rows, :], lb_ref[2, rows, :]
            m = jnp.maximum(jnp.maximum(l0, l1), l2)
            e0, e1, e2 = jnp.exp(l0 - m), jnp.exp(l1 - m), jnp.exp(l2 - m)
            tot = e0 + e1 + e2
            o_ref[rows, :] = (e0 * ob_ref[0, rows, :] + e1 * ob_ref[1, rows, :] + e2 * ob_ref[2, rows, :]) / tot
            lse_ref[rows, :] = m + jnp.log(tot)
            return 0

        lax.fori_loop(0, S // TQ, merge, 0)

    npair = H // 2
    return pl.pallas_call(
        body, name="dil_fwd", grid=(nb, npair),
        in_specs=[pl.BlockSpec((S, HP), lambda b, p: (b, p)), pl.BlockSpec((S, HP), lambda b, p: (b, npair + p)),
                  pl.BlockSpec((S, HP), lambda b, p: (b, 2 * npair + p)),
                  pl.BlockSpec((3, 1, 2 * BLK, 2 * BLK), lambda b, p: (0, p, 0, 0))],
        out_specs=[pl.BlockSpec((S, HP), lambda b, p: (b, p))] * 2,
        out_shape=[jax.ShapeDtypeStruct((t, D_A), F32)] * 2,
        scratch_shapes=[pltpu.VMEM((3, S, HP), F32), pltpu.VMEM((3, S, HP), F32)],
        compiler_params=_cp("parallel", "parallel"),
    )(proj, proj, proj, biasm)


def _dil_bwd(proj, biasm, o, do, lse):
    t = proj.shape[0]
    nb = t // S

    def body(q_ref, k_ref, v_ref, b_ref, o_ref, do_ref, lse_ref, dq_out, dk_out, dv_out, ds_ref, dq_ref, dk_ref, dv_ref):
        dq_ref[...] = jnp.zeros_like(dq_ref)
        dk_ref[...] = jnp.zeros_like(dk_ref)
        dv_ref[...] = jnp.zeros_like(dv_ref)
        ds_ref[...] = jnp.zeros_like(ds_ref)
        lane = lax.broadcasted_iota(jnp.int32, (BLK, HP), 1)
        own = _own_lanes()
        items = _dil_items()
        for g in range(0, len(items), GROUP):
            grp = items[g:g + GROUP]
            q2s, kts, do2s, ss, dps, lse2s, delta2s = [], [], [], [], [], [], []
            for r, d, start, has_prev in grp:
                cur = _strided(start, d)
                rows = [_strided(start - BLK * d, d), cur] if has_prev else [cur]
                q2 = _stack_heads(q_ref[cur, :] * SCALE_A, own)
                kt = jnp.concatenate([k_ref[x, :] for x in rows], axis=0).astype(BF16)
                vt = jnp.concatenate([v_ref[x, :] for x in rows], axis=0).astype(BF16)
                dot_ = do_ref[cur, :]
                prod = dot_ * o_ref[cur, :]
                lset = lse_ref[cur, :]
                do2 = _stack_heads(dot_, own)
                bias = b_ref[r, 0] if has_prev else b_ref[r, 0, :, BLK:]
                ss.append(_dot(q2, kt, ((1,), (1,))) + bias)
                dps.append(_dot(do2, vt, ((1,), (1,))))
                lse2s.append(jnp.concatenate([lset[:, :1], lset[:, E_A:E_A + 1]], axis=0))
                delta2s.append(jnp.concatenate([jnp.sum(jnp.where(lane < E_A, prod, 0.0), axis=-1, keepdims=True),
                                                jnp.sum(jnp.where(lane >= E_A, prod, 0.0), axis=-1, keepdims=True)], axis=0))
                q2s.append(q2)
                kts.append(kt)
                do2s.append(do2)
            ps = [jnp.exp(s - lse2) for s, lse2 in zip(ss, lse2s)]
            dls = [p * (dp - delta2) for p, dp, delta2 in zip(ps, dps, delta2s)]
            for (r, d, start, has_prev), q2, kt, do2, p, dl in zip(grp, q2s, kts, do2s, ps, dls):
                cur = _strided(start, d)
                dsb = dl.astype(BF16)
                dq2 = _dot(dsb, kt, ((1,), (0,))) * SCALE_A
                dkt = _dot(dsb, q2, ((0,), (0,)))
                dvt = _dot(p.astype(BF16), do2, ((0,), (0,)))
                dq_ref[cur, :] += jnp.where(lane < E_A, dq2[:BLK], dq2[BLK:])
                if has_prev:
                    prev = _strided(start - BLK * d, d)
                    ds_ref[0, r, 0] += dl
                    dk_ref[prev, :] += dkt[:BLK]
                    dv_ref[prev, :] += dvt[:BLK]
                    dk_ref[cur, :] += dkt[BLK:]
                    dv_ref[cur, :] += dvt[BLK:]
                else:
                    ds_ref[0, r, 0, :, BLK:] += dl
                    dk_ref[cur, :] += dkt
                    dv_ref[cur, :] += dvt
        dq_out[...] = dq_ref[...].astype(BF16)
        dk_out[...] = dk_ref[...].astype(BF16)
        dv_out[...] = dv_ref[...].astype(BF16)

    npair = H // 2
    pair = pl.BlockSpec((S, HP), lambda b, p: (b, p))
    return pl.pallas_call(
        body, name="dil_bwd", grid=(nb, npair),
        in_specs=[pair, pl.BlockSpec((S, HP), lambda b, p: (b, npair + p)),
                  pl.BlockSpec((S, HP), lambda b, p: (b, 2 * npair + p)),
                  pl.BlockSpec((3, 1, 2 * BLK, 2 * BLK), lambda b, p: (0, p, 0, 0)), pair, pair, pair],
        out_specs=[pair, pair, pair, pl.BlockSpec((1, 3, 1, 2 * BLK, 2 * BLK), lambda b, p: (b, 0, p, 0, 0))],
        out_shape=[jax.ShapeDtypeStruct((t, D_A), BF16)] * 3 + [jax.ShapeDtypeStruct((nb, 3, npair, 2 * BLK, 2 * BLK), F32)],
        scratch_shapes=[pltpu.VMEM((S, HP), F32)] * 3,
        compiler_params=_cp("parallel", "parallel"),
    )(proj, proj, proj, biasm, o, do, lse)


def _rel_bias_grad(dlogits):
    nb = dlogits.shape[0]
    buckets, _ = _band_geometry()
    kk = 3 * BLK * 2 * BLK
    dl = jnp.transpose(dlogits.reshape(nb, 3, H, BLK, 2 * BLK), (0, 2, 1, 3, 4)).reshape(nb, H, kk)
    bk = jnp.asarray(buckets.reshape(1, kk))
    tk = kk // 12

    def body(dl_ref, bk_ref, o_ref):
        j = pl.program_id(0)
        onehot = (bk_ref[...] == lax.broadcasted_iota(jnp.int32, (N_BUCKETS, tk), 0)).astype(F32)
        tot = dl_ref[0]
        for b in range(1, nb):
            tot = tot + dl_ref[b]
        part = lax.dot_general(onehot, tot, ((((1,), (1,))), ((), ())), preferred_element_type=F32,
                               precision=lax.Precision.HIGHEST)
        _acc_first(j, o_ref, part)

    return pl.pallas_call(
        body, name="rel_bias_grad", grid=(kk // tk,),
        in_specs=[pl.BlockSpec((nb, H, tk), lambda j: (0, 0, j)), pl.BlockSpec((1, tk), lambda j: (0, j))],
        out_specs=pl.BlockSpec((N_BUCKETS, H), lambda j: (0, 0)),
        out_shape=jax.ShapeDtypeStruct((N_BUCKETS, H), F32),
        compiler_params=_cp("arbitrary"),
    )(dl, bk)


def _mesh_place():
    x, y, c = lax.axis_index("x"), lax.axis_index("y"), lax.axis_index("c")
    return x, y, c


def _peer(k):
    x, y, c = _mesh_place()
    px = 1 - x if k & 4 else x
    py = 1 - y if k & 2 else y
    pc = 1 - c if k & 1 else c
    return (px, py, pc), 4 * px + 2 * py + pc


ANY = pl.BlockSpec(memory_space=pl.ANY)


def _exchange(arrays, gathers, name, after=None):
    n_arr = len(arrays)

    def body(*refs):
        ins, outs = refs[:n_arr], refs[n_arr + 1:2 * n_arr + 1]
        send, recv, loc = refs[2 * n_arr + 1:]
        x, y, c = _mesh_place()
        me = 4 * x + 2 * y + c
        local = [pltpu.make_async_copy(ins[a] if gathers[a] else ins[a].at[me], outs[a].at[me], loc.at[a])
                 for a in range(n_arr)]
        remote = _peer_copies(ins, outs, send, recv, gathers)
        for cp in local:
            cp.start()
        for put, _ in remote:
            put.start()
        for cp in local:
            cp.wait()
        for put, got in remote:
            put.wait_send()
            got.wait_recv()

    return pl.pallas_call(
        body, name=name,
        in_specs=[ANY] * (n_arr + 1), out_specs=[ANY] * n_arr,
        out_shape=[jax.ShapeDtypeStruct(((N_DEV,) if g else ()) + a.shape, a.dtype) for a, g in zip(arrays, gathers)],
        scratch_shapes=[pltpu.SemaphoreType.DMA((n_arr * (N_DEV - 1),)), pltpu.SemaphoreType.DMA((n_arr * (N_DEV - 1),)),
                        pltpu.SemaphoreType.DMA((n_arr,))],
        compiler_params=pltpu.CompilerParams(has_side_effects=True),
    )(*arrays, arrays[0] if after is None else after)


def _gather_two_level(arrays, name):
    n_arr = len(arrays)
    per = N_DEV - 1

    def body(*refs):
        ins, outs = refs[:n_arr], refs[n_arr:2 * n_arr]
        send, recv, loc = refs[2 * n_arr:]
        x, y, c = _mesh_place()
        me, sibling = (x, y, c), (x, y, 1 - c)
        chips = [(1 - x, y), (x, 1 - y), (1 - x, 1 - y)]

        def block(a, place):
            px, py, pc = place
            return outs[a].at[4 * px + 2 * py + pc]

        def copy(a, k, place, to, src=None):
            dst = block(a, place)
            return pltpu.make_async_remote_copy(dst if src is None else src, dst, send.at[a * per + k], recv.at[a * per + k],
                                                device_id=to, device_id_type=pl.DeviceIdType.MESH)

        local = [pltpu.make_async_copy(ins[a], block(a, me), loc.at[a]) for a in range(n_arr)]
        for cp in local:
            cp.start()
        first = []
        for a in range(n_arr):
            first.append(copy(a, 0, me, sibling, src=ins[a]))
            first += [copy(a, 1 + j, me, (*chip, c), src=ins[a]) for j, chip in enumerate(chips)]
        for cp in first:
            cp.start()
        passed = []
        for j, chip in enumerate(chips):
            for a in range(n_arr):
                copy(a, 1 + j, (*chip, c), me).wait_recv()
                passed.append(copy(a, 4 + j, (*chip, c), sibling))
                passed[-1].start()
        for a in range(n_arr):
            copy(a, 0, sibling, me).wait_recv()
            for j, chip in enumerate(chips):
                copy(a, 4 + j, (*chip, 1 - c), me).wait_recv()
        for cp in first + passed:
            cp.wait_send()
        for cp in local:
            cp.wait()

    return pl.pallas_call(
        body, name=name,
        in_specs=[ANY] * n_arr, out_specs=[ANY] * n_arr,
        out_shape=[jax.ShapeDtypeStruct((N_DEV,) + a.shape, a.dtype) for a in arrays],
        scratch_shapes=[pltpu.SemaphoreType.DMA((n_arr * per,)), pltpu.SemaphoreType.DMA((n_arr * per,)),
                        pltpu.SemaphoreType.DMA((n_arr,))],
        compiler_params=pltpu.CompilerParams(has_side_effects=True),
    )(*arrays)


HBM = pl.BlockSpec(memory_space=pltpu.HBM)
SEM = pl.BlockSpec(memory_space=pltpu.SEMAPHORE)
DATAFLOW = pltpu.SideEffectType.DATAFLOW_SIDE_EFFECTING


def _own_block_in_place(block, me):
    land = lax.empty((N_DEV,) + block.shape, block.dtype)
    return lax.dynamic_update_slice(land, block[None], (me,) + (0,) * block.ndim)


def _peer_copies(srcs, lands, send, recv, gathers):
    x, y, c = _mesh_place()
    me = 4 * x + 2 * y + c
    out = []
    for a, (src, land) in enumerate(zip(srcs, lands)):
        for k in range(1, N_DEV):
            dev, idx = _peer(k)
            sem = a * (N_DEV - 1) + k - 1
            mine = src if gathers[a] else src.at[idx]
            put = pltpu.make_async_remote_copy(mine, land.at[me], send.at[sem], recv.at[sem],
                                               device_id=dev, device_id_type=pl.DeviceIdType.MESH)
            got = pltpu.make_async_remote_copy(mine, land.at[idx], send.at[sem], recv.at[sem],
                                               device_id=dev, device_id_type=pl.DeviceIdType.MESH)
            out.append((put, got))
    return out


def _exchange_start(srcs, lands, gather, after, name):
    n = len(srcs)
    extra = [] if after is None else [after]

    def body(*refs):
        srcs_, lands_ = refs[:n], refs[n:2 * n]
        send, recv = refs[2 * n + len(extra)], refs[2 * n + len(extra) + 1]
        for put, _ in _peer_copies(srcs_, lands_, send, recv, gather):
            put.start()
        refs[-1][...] = jnp.zeros_like(refs[-1])

    nsem = n * (N_DEV - 1)
    thru = [pltpu.HBM(a.shape, a.dtype) for a in list(srcs) + list(lands)]
    res = pl.pallas_call(
        body, name=name,
        out_shape=(pltpu.SemaphoreType.DMA((nsem,)), pltpu.SemaphoreType.DMA((nsem,)), *thru, jax.ShapeDtypeStruct((8, 128), F32)),
        in_specs=[HBM] * (2 * n) + [ANY] * len(extra),
        out_specs=(SEM, SEM, *([HBM] * (2 * n)), pl.BlockSpec(memory_space=pltpu.VMEM)),
        input_output_aliases={i: 2 + i for i in range(2 * n)},
        compiler_params=pltpu.CompilerParams(has_side_effects=DATAFLOW),
    )(*[pltpu.with_memory_space_constraint(a, pltpu.HBM) for a in list(srcs) + list(lands)], *extra)
    return res[0], res[1], list(res[2:2 + n]), list(res[2 + n:2 + 2 * n]), res[-1]


def _exchange_wait(send, recv, srcs, lands, gather, after, name):
    n = len(srcs)

    def body(*refs):
        srcs_, lands_, send_, recv_ = refs[:n], refs[n:2 * n], refs[2 * n], refs[2 * n + 1]
        for put, got in _peer_copies(srcs_, lands_, send_, recv_, gather):
            put.wait_send()
            got.wait_recv()

    thru = [pltpu.HBM(a.shape, a.dtype) for a in list(srcs) + list(lands)]
    res = pl.pallas_call(
        body, name=name, out_shape=tuple(thru),
        in_specs=[HBM] * (2 * n) + [SEM, SEM, ANY], out_specs=tuple([HBM] * (2 * n)),
        input_output_aliases={i: i for i in range(2 * n)},
        compiler_params=pltpu.CompilerParams(has_side_effects=DATAFLOW),
    )(*srcs, *lands, send, recv, after)
    return list(res[n:])


def _silu_rows(c):
    def body(c_ref, o_ref):
        v = c_ref[...]
        o_ref[...] = v * _sigmoid(v)

    return pl.pallas_call(body, name="cond", out_shape=jax.ShapeDtypeStruct(c.shape, F32))(c)


def _mod_slab(cond_all, w_ada, b_slab):
    def body(c_ref, w_ref, b_ref, o_ref):
        o_ref[...] = _dot(c_ref[...].astype(BF16), w_ref[0].astype(BF16), ((1,), (0,))) + b_ref[...]

    return pl.pallas_call(body, name="mod_slab",
                          out_shape=jax.ShapeDtypeStruct((cond_all.shape[0], w_ada.shape[2]), F32),
                          compiler_params=pltpu.CompilerParams(vmem_limit_bytes=VMEM_LIMIT))(cond_all, w_ada, b_slab)


def _ada_grad(cond_all, dmod_cols):
    def body(c_ref, d_ref, o_ref):
        o_ref[...] = _dot(c_ref[...].astype(BF16), d_ref[...].astype(BF16), ((0,), (0,)))

    return pl.pallas_call(body, name="ada_grad",
                          out_shape=jax.ShapeDtypeStruct((cond_all.shape[1], dmod_cols.shape[1]), F32),
                          compiler_params=pltpu.CompilerParams(vmem_limit_bytes=VMEM_LIMIT))(cond_all, dmod_cols)


def _adam_math(g, w, m, v):
    m2 = B1 * m + (1.0 - B1) * g
    v2 = B2 * v + (1.0 - B2) * (g * g)
    m_hat = m2 / (1.0 - B1 ** STEP)
    v_hat = v2 / (1.0 - B2 ** STEP)
    return -LR * (m_hat / (jnp.sqrt(v_hat) + ADAM_EPS) + WD * w), m2, v2


def _adamw(parts, w, m, v, name):
    n, rows, cols = parts.shape
    tr = max([p for p in range(16, 513, 16) if rows % p == 0] or [rows])

    def body(p_ref, w_ref, m_ref, v_ref, g_ref, d_ref, m2_ref, v2_ref):
        g = p_ref[0].astype(F32)
        for s in range(1, n):
            g = g + p_ref[s].astype(F32)
        g_ref[0] = g
        d_ref[0], m2_ref[0], v2_ref[0] = _adam_math(g, w_ref[0], m_ref[0], v_ref[0])

    blk = pl.BlockSpec((1, tr, cols), lambda i: (0, i, 0))
    return pl.pallas_call(
        body, name=name, grid=(rows // tr,),
        in_specs=[pl.BlockSpec((n, tr, cols), lambda i: (0, i, 0)), blk, blk, blk],
        out_specs=[blk] * 4, out_shape=[jax.ShapeDtypeStruct((1, rows, cols), F32)] * 4,
        compiler_params=_cp("parallel"),
    )(*[pltpu.with_memory_space_constraint(a, pltpu.HBM) for a in (parts, w, m, v)])


ROW_PARAMS = (("g_norm1", D), ("g_cq", Q_LORA), ("g_ckv", KV_LORA), ("g_out_a", D_A), ("g_out_b", D_A), ("g_norm2", D),
              ("g_final", D))
LOSS_ROW = N_MOD + len(ROW_PARAMS)
PAY_ROWS = 16
NCOL = N_MOD * D // N_DEV


def _pack_small(dmods, rows, loss_cols):
    nb = dmods[0].shape[0]
    nrow = len(ROW_PARAMS)

    def body(*refs):
        dm, rw, loss_ref, pay_ref, blk_ref = refs[:N_MOD], refs[N_MOD:N_MOD + nrow], refs[N_MOD + nrow], refs[-2], refs[-1]
        pay_ref[...] = jnp.zeros_like(pay_ref)
        for k in range(N_MOD):
            tot = dm[k][0]
            for b in range(1, nb):
                tot = tot + dm[k][b]
            pay_ref[k:k + 1, :] = tot
        for i, (_, n) in enumerate(ROW_PARAMS):
            pay_ref[N_MOD + i:N_MOD + i + 1, :n] = rw[i][...]
        pay_ref[LOSS_ROW:LOSS_ROW + 1, :] = loss_ref[...]
        for j in range(N_DEV):
            done = 0
            while done < NCOL:
                seg, off = divmod(j * NCOL + done, D)
                ln = min(NCOL - done, D - off)
                for b in range(nb):
                    blk_ref[j, b:b + 1, done:done + ln] = dm[seg][b][:, off:off + ln]
                done += ln

    return pl.pallas_call(
        body, name="pack_small",
        out_shape=[jax.ShapeDtypeStruct((PAY_ROWS, D), F32), jax.ShapeDtypeStruct((N_DEV, nb, NCOL), F32)],
    )(*dmods, *rows, loss_cols)


def _small_update(pay, rel, ws, ms, vs):
    n_par = len(ws)

    def body(*refs):
        pay_ref, rel_ref = refs[:2]
        w_refs, m_refs, v_refs = (refs[2 + s * n_par:2 + (s + 1) * n_par] for s in range(3))
        outs, loss_ref = refs[2 + 3 * n_par:-1], refs[-1]
        tot, rtot = pay_ref[0], rel_ref[0]
        for s in range(1, N_DEV):
            tot, rtot = tot + pay_ref[s], rtot + rel_ref[s]

        def update(p, g, sl):
            outs[4 * p][:, sl] = g
            outs[4 * p + 1][:, sl], outs[4 * p + 2][:, sl], outs[4 * p + 3][:, sl] = _adam_math(
                g, w_refs[p][:, sl], m_refs[p][:, sl], v_refs[p][:, sl])

        for k in range(N_MOD):
            update(0, tot[k:k + 1, :], slice(k * D, (k + 1) * D))
        for i, (_, n) in enumerate(ROW_PARAMS):
            update(1 + i, tot[N_MOD + i:N_MOD + i + 1, :n], slice(0, n))
        update(n_par - 1, rtot, slice(0, H))
        loss_ref[...] = jnp.broadcast_to((0.5 / D) * jnp.sum(tot[LOSS_ROW:LOSS_ROW + 1, :]), loss_ref.shape)

    shapes = [jax.ShapeDtypeStruct(w.shape, F32) for w in ws for _ in range(4)]
    res = pl.pallas_call(
        body, name="small_update", out_shape=shapes + [jax.ShapeDtypeStruct((8, 128), F32)],
    )(pay, rel, *ws, *ms, *vs)
    return [tuple(res[4 * p:4 * p + 4]) for p in range(n_par)], res[-1]


def _cols_from_blocks(g):
    return jnp.transpose(g, (1, 0, 2)).reshape(g.shape[1], N_DEV * g.shape[2])


def _cols_to_blocks(w):
    r, c = w.shape
    return jnp.transpose(w.reshape(r, N_DEV, c // N_DEV), (1, 0, 2))


def _pad_w_in(wt):
    z = jnp.zeros((NOPE, wt.shape[1]), wt.dtype)
    return jnp.concatenate([wt[:P_IN - ROPE], z, wt[P_IN - ROPE:], z[:HP - NOPE - ROPE]], axis=0)


def _unpad_w_in(gt):
    k0 = P_IN - ROPE + NOPE
    return jnp.concatenate([gt[:P_IN - ROPE], gt[k0:k0 + ROPE]], axis=0)


def _pad_w_uq(wt):
    return jnp.pad(wt, ((0, 0), (0, HP - NOPE - ROPE), (0, 0))).reshape(H * HP, Q_LORA)


def _unpad_w_uq(gt):
    return gt.reshape(H, HP, Q_LORA)[:, :NOPE + ROPE]


def _split_w_ukv(w):
    w4 = w.reshape(KV_LORA, H // 2, 2, HP)
    z = jnp.zeros((KV_LORA, H // 2, NOPE), w.dtype)
    kn, vv = w4[..., :NOPE], w4[..., NOPE:]
    w_k = jnp.stack([jnp.concatenate([kn[:, :, 0], z], -1), jnp.concatenate([kn[:, :, 1], z], -1)], axis=2)
    w_v = jnp.stack([jnp.concatenate([vv[:, :, 0], z], -1), jnp.concatenate([z, vv[:, :, 1]], -1)], axis=2)
    return w_k.reshape(KV_LORA, H * HP), w_v.reshape(KV_LORA, H * HP)


def _join_w_ukv(g_k, g_v):
    gk = g_k.reshape(KV_LORA, H // 2, 2, HP)
    gv = g_v.reshape(KV_LORA, H // 2, 2, HP)
    even = jnp.concatenate([gk[:, :, 0, :NOPE], gv[:, :, 0, :VDIM]], -1)
    odd = jnp.concatenate([gk[:, :, 1, :NOPE], gv[:, :, 1, VDIM:]], -1)
    return jnp.stack([even, odd], axis=2).reshape(KV_LORA, H * HP)


def _rope_tables():
    half = ROPE // 2
    inv = ROPE_THETA ** (-jnp.arange(half, dtype=F32) / half)
    ang = jnp.arange(S, dtype=F32)[:, None] * inv[None, :]
    cos, sin = jnp.cos(ang), jnp.sin(ang)
    ones, zeros = jnp.ones((S, NOPE), F32), jnp.zeros((S, NOPE), F32)
    tail1, tail0 = jnp.ones((S, HP - NOPE - ROPE), F32), jnp.zeros((S, HP - NOPE - ROPE), F32)
    zh = jnp.zeros((S, half), F32)
    c = jnp.concatenate([ones, cos, cos, tail1], axis=1)
    sm = jnp.concatenate([zeros, -sin, zh, tail0], axis=1)
    sp = jnp.concatenate([zeros, zh, sin, tail0], axis=1)
    return c, sm, sp


def _local_step(x, mod, target, g_norm1, w_in_p, g_cq, w_uq_p, g_ckv, w_k, w_v, rel_bias, g_out_a, g_out_b, w_out,
                g_norm2, w_ffn_in, w_ffn_out, g_final, late_weights=None, on_ffn_grads=None, on_last_grads=None):
    nb = x.shape[0] // S
    sh1, sc1, g1, sh2, sc2, g2 = (mod[:, n].reshape(nb, 1, D) for n in range(N_MOD))
    rc, rsm, rsp = _rope_tables()
    biasm = _band_bias(rel_bias)

    h1 = _pre1(x, g_norm1, sc1, sh1)
    proj = _mm_nt(h1, w_in_p, F32, "proj")
    q, k, v, cqn, ckvn = _mla_pre(proj, g_cq, g_ckv, w_uq_p, w_k, w_v, rc, rsm, rsp)
    out_b, lse_b = _mla_fwd(q, k, v)
    out_a, lse_a = _dil_fwd(proj, biasm)
    y = _post_attn(out_a, out_b, g_out_a, g_out_b)
    if late_weights is not None:
        w_out, w_ffn_in, w_ffn_out = late_weights(y)
    mix = _mm_nn(y, w_out, BF16, "mix")
    x2, h2 = _resid_norm2(x, mix, g1, g_norm2, sc2, sh2)
    ffn_g, ffn_u, act = _ffn_in(h2, w_ffn_in)
    f = _mm_nn(act, w_ffn_out, BF16, "ffn_out")
    dx3, df, loss_cols, dg_final, dg2 = _final(x2, f, g2, g_final, target)

    dg_, du_ = _d_act(df, w_ffn_out, ffn_g, ffn_u)
    gw_ffn_out = _mm_tn_rows([act], df, "gw_ffn_out")
    dh2 = _d_h2(dg_, du_, w_ffn_in)
    gw_ffn_in = _mm_tn_rows([dg_, du_], h2, "gw_ffn_in")
    dx2, dsh2, dsc2, dg_norm2, dg1, dmix = _norm_bwd(x2, dh2, dx3, g_norm2, sc2, gate=(mix, g1))
    dy = _mm_nt(dmix, w_out, BF16, "d_y")
    gw_out = _mm_tn(y, [dmix], "gw_out")
    if on_ffn_grads is not None:
        g_out_a = g_out_a + on_ffn_grads(gw_ffn_in, gw_ffn_out, gw_out)
    dout_a, dout_b, dg_out_a, dg_out_b = _post_attn_bwd(dy, out_a, out_b, g_out_a, g_out_b)
    dq_b, dk_b, dv_b = _mla_bwd(q, k, v, out_b, dout_b, lse_b)
    dq_a, dk_a, dv_a, dlogits = _dil_bwd(proj, biasm, out_a, dout_a, lse_a)
    g_rel = _rel_bias_grad(dlogits)
    dqr, dproj, dg_cq, dg_ckv = _mla_pre_bwd(proj, dq_b, dk_b, dv_b, (dq_a, dk_a, dv_a), g_cq, g_ckv, w_uq_p, w_k, w_v,
                                             rc, rsm, rsp)
    gw_uq = _mm_tn(dqr, [cqn], "gw_uq")
    gw_k, gw_v = _mm_tn(ckvn, [dk_b, dv_b], "gw_kv")
    gw_in = _mm_tn_rows([dproj], h1, "gw_in")
    if on_last_grads is not None:
        started = on_last_grads(dict(w_in=gw_in, w_uq=gw_uq, w_k=gw_k, w_v=gw_v))
    else:
        started = None
    dh1 = _mm_nn(dproj, w_in_p, BF16, "d_h1", after=started)
    grad_x, dsh1, dsc1, dg_norm1 = _norm_bwd(x, dh1, dx2, g_norm1, sc1)

    dmod = [dsh1, dsc1, dg1, dsh2, dsc2, dg2]
    small = dict(g_norm1=dg_norm1, g_cq=dg_cq, g_ckv=dg_ckv, rel_bias=g_rel, g_out_a=dg_out_a, g_out_b=dg_out_b,
                 g_norm2=dg_norm2, g_final=dg_final)
    big = dict(w_in=gw_in, w_uq=gw_uq, w_k=gw_k, w_v=gw_v, w_out=gw_out, w_ffn_in=gw_ffn_in, w_ffn_out=gw_ffn_out)
    return grad_x, dmod, loss_cols, small, big


def kernel(x, c, w_ada, b_ada, g_norm1, w_in, g_cq, w_uq, g_ckv, w_ukv, rel_bias, g_out_a, g_out_b, w_out, g_norm2, w_ffn_in, w_ffn_out, g_final, loss_target, m_w_ada, m_b_ada, m_g_norm1, m_w_in, m_g_cq, m_w_uq, m_g_ckv, m_w_ukv, m_rel_bias, m_g_out_a, m_g_out_b, m_w_out, m_g_norm2, m_w_ffn_in, m_w_ffn_out, m_g_final, v_w_ada, v_b_ada, v_g_norm1, v_w_in, v_g_cq, v_w_uq, v_g_ckv, v_w_ukv, v_rel_bias, v_g_out_a, v_g_out_b, v_w_out, v_g_norm2, v_w_ffn_in, v_w_ffn_out, v_g_final):
    nb = x.shape[0]
    t = nb * S
    xt, tt = x.reshape(t, D), loss_target.reshape(t, D)
    me = 4 * lax.axis_index("x") + 2 * lax.axis_index("y") + lax.axis_index("c")

    early = [jnp.swapaxes(w_in, 1, 2)[0], jnp.swapaxes(w_uq, 1, 2)[0], w_ukv[0]]
    gathered = _gather_two_level([_silu_rows(c)] + [s.astype(BF16) for s in early], "gather_weights")
    cond_all = gathered[0].reshape(N_DEV * nb, D)
    w_in_t = gathered[1].reshape(P_IN, D)
    w_ukv_f = _cols_from_blocks(gathered[3])
    w_k, w_v = _split_w_ukv(w_ukv_f)

    ncol = N_MOD * D // N_DEV
    b_slab = lax.dynamic_slice(b_ada, (0, me * ncol), (1, ncol))
    slab = _mod_slab(cond_all, w_ada, b_slab)
    (mod_rows,) = _exchange([slab.reshape(N_DEV, nb, ncol)], [False], "scatter_mod")
    mod = jnp.transpose(mod_rows, (1, 0, 2)).reshape(nb, N_MOD, D)

    late = [s.astype(BF16) for s in (w_out[0], jnp.swapaxes(w_ffn_in, 1, 2)[0], w_ffn_out[0])]
    late_send, late_recv, late_src, late_land, late_token = _exchange_start(
        late, [_own_block_in_place(s, me) for s in late], [True] * 3, mod_rows, "gather_late_start")
    g_norm1_t = g_norm1 + late_token[:1, :1]

    def late_weights(after):
        w_out_g, w_ffn_in_g, w_ffn_out_g = _exchange_wait(late_send, late_recv, late_src, late_land, [True] * 3, after,
                                                          "gather_late_wait")
        return w_out_g.reshape(D, D), w_ffn_in_g.reshape(2 * D_FF, D), w_ffn_out_g.reshape(D_FF, D)

    flight = {}

    def start_grads(key, src, name):
        land = [_own_block_in_place(lax.dynamic_index_in_dim(s, me, 0, keepdims=False), me) for s in src]
        send, recv, src, land, token = _exchange_start(src, land, [False] * len(src), None, name)
        flight[key] = (send, recv, src, land)
        return token[:1, :1]

    def on_ffn_grads(gw_ffn_in, gw_ffn_out, gw_out):
        return start_grads("ffn", [gw_ffn_in.reshape(N_DEV, 2 * D_FF // N_DEV, D), gw_ffn_out.reshape(N_DEV, D_FF // N_DEV, D),
                                   gw_out.reshape(N_DEV, D // N_DEV, D)], "exchange_ffn_start")

    def on_last_grads(gw):
        return start_grads("rest", [_unpad_w_in(gw["w_in"]).reshape(N_DEV, P_IN // N_DEV, D),
                                    _unpad_w_uq(gw["w_uq"]),
                                    _cols_to_blocks(_join_w_ukv(gw["w_k"], gw["w_v"]))], "exchange_rest_start")

    grad_x, dmod, loss_cols, small, _ = _local_step(
        xt, mod, tt, g_norm1_t, _pad_w_in(w_in_t), g_cq, _pad_w_uq(gathered[2]), g_ckv, w_k, w_v, rel_bias, g_out_a, g_out_b,
        None, g_norm2, None, None, g_final.reshape(1, D), late_weights=late_weights, on_ffn_grads=on_ffn_grads,
        on_last_grads=on_last_grads)

    upd = {}

    def land_and_update(key, names, after, name):
        got = _exchange_wait(*flight[key], [False] * len(names), after, name)
        for n, p in zip(names, got):
            w, m, v = big[n]
            upd[n] = _adamw(p, w, m, v, "adamw_" + n)

    def flip(a):
        return jnp.swapaxes(a, 1, 2)

    big = dict(w_in=(flip(w_in), flip(m_w_in), flip(v_w_in)), w_uq=(flip(w_uq), flip(m_w_uq), flip(v_w_uq)),
               w_ukv=(w_ukv, m_w_ukv, v_w_ukv),
               w_out=(w_out, m_w_out, v_w_out), w_ffn_in=(flip(w_ffn_in), flip(m_w_ffn_in), flip(v_w_ffn_in)),
               w_ffn_out=(w_ffn_out, m_w_ffn_out, v_w_ffn_out))
    land_and_update("ffn", ["w_ffn_in", "w_ffn_out", "w_out"], grad_x, "exchange_ffn_wait")
    land_and_update("rest", ["w_in", "w_uq", "w_ukv"], upd["w_out"][0], "exchange_rest_wait")
    for n in ("w_in", "w_uq", "w_ffn_in"):
        upd[n] = tuple(flip(a) for a in upd[n])

    mine, dmod_blocks = _pack_small(dmod, [small[n] for n, _ in ROW_PARAMS], loss_cols)
    dmod_cols, pay, rel = _exchange([dmod_blocks, mine, small["rel_bias"]], [False, True, True], "exchange_small",
                                    after=upd["w_ukv"][0])
    g_ada = _ada_grad(cond_all, dmod_cols.reshape(N_DEV * nb, ncol))
    upd["w_ada"] = _adamw(g_ada[None], w_ada, m_w_ada, v_w_ada, "adamw_w_ada")
    row = lambda a: a.reshape(1, D)
    small_names = ["b_ada"] + [n for n, _ in ROW_PARAMS] + ["rel_bias"]
    small_w = [b_ada, g_norm1, g_cq, g_ckv, g_out_a, g_out_b, g_norm2, row(g_final), rel_bias]
    small_m = [m_b_ada, m_g_norm1, m_g_cq, m_g_ckv, m_g_out_a, m_g_out_b, m_g_norm2, row(m_g_final), m_rel_bias]
    small_v = [v_b_ada, v_g_norm1, v_g_cq, v_g_ckv, v_g_out_a, v_g_out_b, v_g_norm2, row(v_g_final), v_rel_bias]
    small_upd, loss8 = _small_update(pay, rel, small_w, small_m, small_v)
    upd.update(zip(small_names, small_upd))

    order = ["w_ada", "b_ada", "g_norm1", "w_in", "g_cq", "w_uq", "g_ckv", "w_ukv", "rel_bias", "g_out_a", "g_out_b",
             "w_out", "g_norm2", "w_ffn_in", "w_ffn_out", "g_final"]
    like = dict(g_final=g_final)
    outs = [loss8[0, 0], grad_x.reshape(x.shape)]
    for part in range(4):
        for n in order:
            val = upd[n][part]
            outs.append(val.reshape(like[n].shape) if n in like else val)
    return tuple(outs)
```

```python
import functools

import numpy as np
import jax
import jax.numpy as jnp
from jax import lax
from jax.experimental import pallas as pl
from jax.experimental.pallas import tpu as pltpu

F32, BF16 = jnp.float32, jnp.bfloat16

N_DEV = 8
D = 1024
S = 2048
H = 8
E_A = 64
D_A = H * E_A
Q_LORA, KV_LORA = 384, 256
NOPE, ROPE, VDIM = 64, 32, 64
HP = 128
P_IN = 3 * D_A + Q_LORA + KV_LORA + ROPE
P_PAD = 3 * D_A + Q_LORA + KV_LORA + HP
TAIL0 = 3 * D_A
TAIL = P_PAD - TAIL0
D_FF = 2816
N_MOD = 6
EPS = 1e-6
NEG = -1e30
BLK = 128
DILATIONS = (1, 4, 16)
N_BUCKETS, MAX_DISTANCE = 32, 2048
ROPE_THETA = 10000.0
SCALE_A = E_A ** -0.5
SCALE_B = (NOPE + ROPE) ** -0.5
B1, B2, LR, ADAM_EPS, WD, STEP = 0.9, 0.999, 0.001, 1e-8, 0.01, 10
VMEM_LIMIT = 56 * 1024 * 1024


def _cp(*sem):
    return pltpu.CompilerParams(dimension_semantics=sem, vmem_limit_bytes=VMEM_LIMIT)


def _pick(n, prefs):
    for p in prefs:
        if n % p == 0:
            return p
    raise ValueError(f"no tile of {prefs} divides {n}")


OPERAND_BYTES = 6 * 1024 * 1024


def _pick_rows(m, k):
    return _pick(m, [p for p in (1024, 512, 256, 128, 16) if p * k * 2 <= OPERAND_BYTES])


MATMUL_BYTES = 40 * 1024 * 1024


def _stream_rows(m, fixed, per_row):
    return _pick(m, [p for p in (4096, 2048, 1024, 512, 256, 128, 16) if fixed + p * per_row <= MATMUL_BYTES])


def _dot(a, b, dims):
    return lax.dot_general(a, b, (dims, ((), ())), preferred_element_type=F32)


def _mm_nn(a, b, out_dtype, name, after=None):
    m, k = a.shape
    n = b.shape[1]
    tn = _pick(n, (512, 384, 256, 128))
    tm = _stream_rows(m, 4 * k * tn, 4 * k + (2 * jnp.dtype(out_dtype).itemsize + 4) * tn)

    def body(a_ref, b_ref, *rest):
        o_ref = rest[-1]
        o_ref[...] = _dot(a_ref[...], b_ref[...], ((1,), (0,))).astype(o_ref.dtype)

    extra = [] if after is None else [after]
    return pl.pallas_call(
        body, name=name, grid=(m // tm, n // tn),
        in_specs=[pl.BlockSpec((tm, k), lambda i, j: (i, 0)), pl.BlockSpec((k, tn), lambda i, j: (0, j))] + [ANY] * len(extra),
        out_specs=pl.BlockSpec((tm, tn), lambda i, j: (i, j)),
        out_shape=jax.ShapeDtypeStruct((m, n), out_dtype),
        compiler_params=_cp("parallel", "parallel"),
    )(a, b, *extra)


def _mm_nt(a, b, out_dtype, name, after=None):
    m, k = a.shape
    n = b.shape[0]
    tn = _pick(n, (512, 384, 256, 128))
    tm = _stream_rows(m, 4 * k * tn, 4 * k + (2 * jnp.dtype(out_dtype).itemsize + 4) * tn)

    def body(a_ref, b_ref, *rest):
        o_ref = rest[-1]
        o_ref[...] = _dot(a_ref[...], b_ref[...], ((1,), (1,))).astype(o_ref.dtype)

    extra = [] if after is None else [after]
    return pl.pallas_call(
        body, name=name, grid=(m // tm, n // tn),
        in_specs=[pl.BlockSpec((tm, k), lambda i, j: (i, 0)), pl.BlockSpec((tn, k), lambda i, j: (j, 0))] + [ANY] * len(extra),
        out_specs=pl.BlockSpec((tm, tn), lambda i, j: (i, j)),
        out_shape=jax.ShapeDtypeStruct((m, n), out_dtype),
        compiler_params=_cp("parallel", "parallel"),
    )(a, b, *extra)


def _mm_tn(a, bs, name):
    t, m = a.shape
    n = bs[0].shape[1]
    nb_ = len(bs)
    tc = _pick(t, (512, 16))
    tn = _pick(n, (512, 384, 256, 128))
    tm = _pick(m, [p for p in (1024, 512, 384, 256, 128) if (3 * p + 2 * nb_ * tn) * t * 2 <= VMEM_LIMIT - 2 * OPERAND_BYTES])
    if tm <= 256 and nb_ * n * t * 2 <= 2 * OPERAND_BYTES:
        tn = n

    def body(*refs):
        a_ref, b_refs, o_refs, at_ref = refs[0], refs[1:1 + nb_], refs[1 + nb_:1 + 2 * nb_], refs[-1]

        @pl.when(pl.program_id(1) == 0)
        def _():
            def chunk(c, _):
                rows = pl.ds(pl.multiple_of(c * tc, tc), tc)
                at_ref[:, rows] = a_ref[rows, :].T
                return 0

            lax.fori_loop(0, t // tc, chunk, 0)

        for b_ref, o_ref in zip(b_refs, o_refs):
            o_ref[...] = _dot(at_ref[...], b_ref[...], ((1,), (0,))).astype(BF16)

    res = pl.pallas_call(
        body, name=name, grid=(m // tm, n // tn),
        in_specs=[pl.BlockSpec((t, tm), lambda i, j: (0, i))] + [pl.BlockSpec((t, tn), lambda i, j: (0, j))] * nb_,
        out_specs=[pl.BlockSpec((tm, tn), lambda i, j: (i, j))] * nb_,
        out_shape=[jax.ShapeDtypeStruct((m, n), BF16)] * nb_,
        scratch_shapes=[pltpu.VMEM((tm, t), BF16)],
        compiler_params=_cp("parallel", "arbitrary"),
    )(a, *bs)
    return res[0] if nb_ == 1 else res


def _mm_tn_rows(a_list, b, name):
    t, m = a_list[0].shape
    n = b.shape[1]
    na = len(a_list)
    tc, tm = _pick(t, (512, 16)), _pick(m, (256, 128))
    nblk = m // tm

    def body(*refs):
        a_refs, b_ref, o_ref, bt_ref, r_ref = refs[:na], refs[na], refs[na + 1], refs[na + 2], refs[na + 3]
        i = pl.program_id(0)

        @pl.when(i == 0)
        def _():
            def chunk(c, _):
                rows = pl.ds(pl.multiple_of(c * tc, tc), tc)
                bt_ref[:, rows] = b_ref[rows, :].T
                return 0

            lax.fori_loop(0, t // tc, chunk, 0)

        for s, a_ref in enumerate(a_refs):
            @pl.when((i >= s * nblk) & (i < (s + 1) * nblk))
            def _(a_ref=a_ref):
                r_ref[...] = _dot(bt_ref[...], a_ref[...], ((1,), (0,)))
                o_ref[...] = r_ref[...].T.astype(BF16)

    return pl.pallas_call(
        body, name=name, grid=(na * nblk,),
        in_specs=[pl.BlockSpec((t, tm), lambda i, s=s: (0, jnp.clip(i - s * nblk, 0, nblk - 1))) for s in range(na)]
        + [pl.BlockSpec((t, n), lambda i: (0, 0))],
        out_specs=pl.BlockSpec((tm, n), lambda i: (i, 0)),
        out_shape=jax.ShapeDtypeStruct((na * m, n), BF16),
        scratch_shapes=[pltpu.VMEM((n, t), BF16), pltpu.VMEM((n, tm), F32)],
        compiler_params=_cp("arbitrary"),
    )(*a_list, b)


EPI = 256


def _silu_parts(g):
    sg = 0.5 * jnp.tanh(0.5 * g) + 0.5
    return sg, g * sg


def _ffn_in(h2, wt):
    t, k = h2.shape
    tn = _pick(D_FF, (256, 128))
    tm = _stream_rows(t, 8 * k * tn, 4 * k + (3 * 2 * 2 + 2 * 4) * tn)
    nj = D_FF // tn

    def body(h_ref, wg_ref, wu_ref, g_ref, u_ref, a_ref):
        hv = h_ref[...]
        g_all = _dot(hv, wg_ref[...], ((1,), (1,)))
        u_all = _dot(hv, wu_ref[...], ((1,), (1,)))
        for r in range(tm // EPI):
            rows = slice(r * EPI, (r + 1) * EPI)
            g, u = g_all[rows], u_all[rows]
            g_ref[rows, :] = g.astype(BF16)
            u_ref[rows, :] = u.astype(BF16)
            a_ref[rows, :] = (_silu_parts(g)[1] * u).astype(BF16)

    blk = pl.BlockSpec((tm, tn), lambda i, j: (i, j))
    return pl.pallas_call(
        body, name="ffn_in", grid=(t // tm, nj),
        in_specs=[pl.BlockSpec((tm, k), lambda i, j: (i, 0)), pl.BlockSpec((tn, k), lambda i, j: (j, 0)),
                  pl.BlockSpec((tn, k), lambda i, j: (j + nj, 0))],
        out_specs=[blk] * 3, out_shape=[jax.ShapeDtypeStruct((t, D_FF), BF16)] * 3,
        compiler_params=_cp("parallel", "parallel"),
    )(h2, wt, wt)


def _d_act(df, w, g, u):
    t, k = df.shape
    tn = _pick(D_FF, (256, 128))
    tm = _stream_rows(t, 4 * k * tn, 4 * k + (4 * 2 * 2 + 4) * tn)

    def body(df_ref, w_ref, g_ref, u_ref, dg_ref, du_ref):
        da_all = _dot(df_ref[...], w_ref[...], ((1,), (1,)))
        for r in range(tm // EPI):
            rows = slice(r * EPI, (r + 1) * EPI)
            da = da_all[rows]
            gv = g_ref[rows, :].astype(F32)
            sg, silu = _silu_parts(gv)
            dg_ref[rows, :] = ((da * u_ref[rows, :].astype(F32)) * (sg + silu * (1.0 - sg))).astype(BF16)
            du_ref[rows, :] = (da * silu).astype(BF16)

    blk = pl.BlockSpec((tm, tn), lambda i, j: (i, j))
    return pl.pallas_call(
        body, name="d_act", grid=(t // tm, D_FF // tn),
        in_specs=[pl.BlockSpec((tm, k), lambda i, j: (i, 0)), pl.BlockSpec((tn, k), lambda i, j: (j, 0)), blk, blk],
        out_specs=[blk] * 2, out_shape=[jax.ShapeDtypeStruct((t, D_FF), BF16)] * 2,
        compiler_params=_cp("parallel", "parallel"),
    )(df, w, g, u)


def _d_h2(dg, du, wt):
    t = dg.shape[0]
    n = wt.shape[1]
    tm, tn = _pick_rows(t, D_FF), _pick(n, (512, 256, 128))

    def body(dg_ref, du_ref, wg_ref, wu_ref, o_ref):
        o_ref[...] = (_dot(dg_ref[...], wg_ref[...], ((1,), (0,)))
                      + _dot(du_ref[...], wu_ref[...], ((1,), (0,)))).astype(BF16)

    return pl.pallas_call(
        body, name="d_h2", grid=(t // tm, n // tn),
        in_specs=[pl.BlockSpec((tm, D_FF), lambda i, j: (i, 0)), pl.BlockSpec((tm, D_FF), lambda i, j: (i, 0)),
                  pl.BlockSpec((D_FF, tn), lambda i, j: (0, j)), pl.BlockSpec((D_FF, tn), lambda i, j: (1, j))],
        out_specs=pl.BlockSpec((tm, tn), lambda i, j: (i, j)),
        out_shape=jax.ShapeDtypeStruct((t, n), BF16),
        compiler_params=_cp("parallel", "parallel"),
    )(dg, du, wt, wt)


TM = 1024


def _row(w):
    return pl.BlockSpec((TM, w), lambda i: (i, 0))


def _row_at(w, col):
    return pl.BlockSpec((TM, w), lambda i: (i, col))


def _vec(w):
    return pl.BlockSpec((1, w), lambda i: (0, 0))


def _per_ex(w):
    return pl.BlockSpec((1, 1, w), lambda i: (i // (S // TM), 0, 0))


def _pos(w):
    return pl.BlockSpec((TM, w), lambda i: (i % (S // TM), 0))


def _full(shape):
    return pl.BlockSpec(shape, lambda i: (0,) * len(shape))


def _rms(x):
    return lax.rsqrt(jnp.mean(x * x, axis=-1, keepdims=True) + EPS)


def _rms_bwd(n, r, dn):
    return r * (dn - n * jnp.mean(dn * n, axis=-1, keepdims=True))


def _colsum(v):
    return jnp.sum(v, axis=0, keepdims=True)


def _acc_first(i, ref, val, every=None):
    first = (i == 0) if every is None else (i % every == 0)

    @pl.when(first)
    def _():
        ref[...] = jnp.zeros_like(ref)

    ref[...] += val.reshape(ref.shape)


def _pre1(x, g, sc, sh):
    t = x.shape[0]

    def body(x_ref, g_ref, sc_ref, sh_ref, h_ref):
        xv = x_ref[...]
        n = xv * _rms(xv)
        h_ref[...] = ((n * g_ref[...]) * (1.0 + sc_ref[0]) + sh_ref[0]).astype(BF16)

    return pl.pallas_call(
        body, name="pre1", grid=(t // TM,),
        in_specs=[_row(D), _vec(D), _per_ex(D), _per_ex(D)],
        out_specs=_row(D), out_shape=jax.ShapeDtypeStruct((t, D), BF16),
        compiler_params=_cp("parallel"),
    )(x, g, sc, sh)


def _rope_fwd(v, c, sm, sp):
    return v * c + pltpu.roll(v, HP - ROPE // 2, 1) * sm + pltpu.roll(v, ROPE // 2, 1) * sp


def _rope_bwd(dv, c, sm, sp):
    return dv * c + pltpu.roll(dv * sm, ROPE // 2, 1) + pltpu.roll(dv * sp, HP - ROPE // 2, 1)


def _mla_pre(proj, g_cq, g_ckv, w_uq, w_k, w_v, rc, rsm, rsp):
    t = proj.shape[0]

    def body(tail_ref, gq_ref, gkv_ref, wuq_ref, wk_ref, wv_ref, c_ref, sm_ref, sp_ref,
             q_ref, k_ref, v_ref, cqn_ref, ckvn_ref):
        tail = tail_ref[...]
        cq, ckv, kr = tail[:, :Q_LORA], tail[:, Q_LORA:Q_LORA + KV_LORA], tail[:, Q_LORA + KV_LORA:]
        cqn = (cq * _rms(cq) * gq_ref[...]).astype(BF16)
        ckvn = (ckv * _rms(ckv) * gkv_ref[...]).astype(BF16)
        cqn_ref[...] = cqn
        ckvn_ref[...] = ckvn
        c, sm, sp = c_ref[...], sm_ref[...], sp_ref[...]
        q = _dot(cqn, wuq_ref[...], ((1,), (1,)))
        kn = _dot(ckvn, wk_ref[...], ((1,), (0,)))
        v_ref[...] = _dot(ckvn, wv_ref[...], ((1,), (0,))).astype(BF16)
        krr = _rope_fwd(kr, c, sm, sp)
        for h in range(H):
            sl = slice(h * HP, (h + 1) * HP)
            q_ref[:, sl] = _rope_fwd(q[:, sl], c, sm, sp).astype(BF16)
            k_ref[:, sl] = (kn[:, sl] + krr).astype(BF16)

    wide = H * HP
    return pl.pallas_call(
        body, name="mla_pre", grid=(t // TM,),
        in_specs=[_row_at(TAIL, TAIL0 // TAIL), _vec(Q_LORA), _vec(KV_LORA), _full((wide, Q_LORA)),
                  _full((KV_LORA, wide)), _full((KV_LORA, wide)), _pos(HP), _pos(HP), _pos(HP)],
        out_specs=[_row(wide), _row(wide), _row(wide), _row(Q_LORA), _row(KV_LORA)],
        out_shape=[jax.ShapeDtypeStruct((t, wide), BF16)] * 3
        + [jax.ShapeDtypeStruct((t, Q_LORA), BF16), jax.ShapeDtypeStruct((t, KV_LORA), BF16)],
        compiler_params=_cp("parallel"),
    )(proj, g_cq, g_ckv, w_uq, w_k, w_v, rc, rsm, rsp)


def _mla_pre_bwd(proj, dq_, dk_, dv_, dqkv_a, g_cq, g_ckv, w_uq, w_k, w_v, rc, rsm, rsp):
    t = proj.shape[0]
    wide = H * HP

    def body(tail_ref, dq_ref, dk_ref, dv_ref, dqa_ref, dka_ref, dva_ref, gq_ref, gkv_ref, wuq_ref, wk_ref, wv_ref,
             c_ref, sm_ref, sp_ref, dqo_ref, dproj_ref, dgq_ref, dgkv_ref):
        i = pl.program_id(0)
        for n, src in enumerate((dqa_ref, dka_ref, dva_ref)):
            dproj_ref[:, n * D_A:(n + 1) * D_A] = src[...]
        dtail_ref = dproj_ref.at[:, TAIL0:]
        tail = tail_ref[...]
        cq, ckv = tail[:, :Q_LORA], tail[:, Q_LORA:Q_LORA + KV_LORA]
        c, sm, sp = c_ref[...], sm_ref[...], sp_ref[...]
        dkr = jnp.zeros((TM, HP), F32)
        for h in range(H):
            sl = slice(h * HP, (h + 1) * HP)
            dqo_ref[:, sl] = _rope_bwd(dq_ref[:, sl].astype(F32), c, sm, sp).astype(BF16)
            dkr = dkr + dk_ref[:, sl].astype(F32)
        lane = lax.broadcasted_iota(jnp.int32, (TM, HP), 1)
        dkr = jnp.where((lane >= NOPE) & (lane < NOPE + ROPE), _rope_bwd(dkr, c, sm, sp), 0.0)
        dkb = dk_ref[...]
        dvb = dv_ref[...]
        dcqn = _dot(dqo_ref[...], wuq_ref[...], ((1,), (0,)))
        dckvn = _dot(dkb, wk_ref[...], ((1,), (1,))) + _dot(dvb, wv_ref[...], ((1,), (1,)))
        rq, rkv = _rms(cq), _rms(ckv)
        nq, nkv = cq * rq, ckv * rkv
        _acc_first(i, dgq_ref, _colsum(dcqn * nq))
        _acc_first(i, dgkv_ref, _colsum(dckvn * nkv))
        dtail_ref[:, :Q_LORA] = _rms_bwd(nq, rq, dcqn * gq_ref[...]).astype(BF16)
        dtail_ref[:, Q_LORA:Q_LORA + KV_LORA] = _rms_bwd(nkv, rkv, dckvn * gkv_ref[...]).astype(BF16)
        dtail_ref[:, Q_LORA + KV_LORA:] = dkr.astype(BF16)

    return pl.pallas_call(
        body, name="mla_pre_bwd", grid=(t // TM,),
        in_specs=[_row_at(TAIL, TAIL0 // TAIL), _row(wide), _row(wide), _row(wide), _row(D_A), _row(D_A), _row(D_A),
                  _vec(Q_LORA), _vec(KV_LORA), _full((wide, Q_LORA)), _full((KV_LORA, wide)), _full((KV_LORA, wide)),
                  _pos(HP), _pos(HP), _pos(HP)],
        out_specs=[_row(wide), _row(P_PAD), _vec(Q_LORA), _vec(KV_LORA)],
        out_shape=[jax.ShapeDtypeStruct((t, wide), BF16), jax.ShapeDtypeStruct((t, P_PAD), BF16),
                   jax.ShapeDtypeStruct((1, Q_LORA), F32), jax.ShapeDtypeStruct((1, KV_LORA), F32)],
        compiler_params=_cp("arbitrary"),
    )(proj, dq_, dk_, dv_, *dqkv_a, g_cq, g_ckv, w_uq, w_k, w_v, rc, rsm, rsp)


def _post_attn(out_a, out_b, g_a, g_b):
    t = out_a.shape[0]

    def body(a_ref, b_ref, ga_ref, gb_ref, y_ref):
        a, b = a_ref[...], b_ref[...]
        y_ref[:, :D_A] = (a * _rms(a) * ga_ref[...]).astype(BF16)
        y_ref[:, D_A:] = (b * _rms(b) * gb_ref[...]).astype(BF16)

    return pl.pallas_call(
        body, name="post_attn", grid=(t // TM,),
        in_specs=[_row(D_A), _row(D_A), _vec(D_A), _vec(D_A)],
        out_specs=_row(D), out_shape=jax.ShapeDtypeStruct((t, D), BF16),
        compiler_params=_cp("parallel"),
    )(out_a, out_b, g_a, g_b)


def _post_attn_bwd(dy, out_a, out_b, g_a, g_b):
    t = dy.shape[0]

    def body(dy_ref, a_ref, b_ref, ga_ref, gb_ref, da_ref, db_ref, dga_ref, dgb_ref):
        i = pl.program_id(0)
        dy_ = dy_ref[...].astype(F32)
        for src, g_ref, dst, dg_ref, sl in ((a_ref, ga_ref, da_ref, dga_ref, slice(0, D_A)),
                                            (b_ref, gb_ref, db_ref, dgb_ref, slice(D_A, D))):
            v = src[...]
            r = _rms(v)
            n = v * r
            dyv = dy_[:, sl]
            _acc_first(i, dg_ref, _colsum(dyv * n))
            dst[...] = _rms_bwd(n, r, dyv * g_ref[...])

    return pl.pallas_call(
        body, name="post_attn_bwd", grid=(t // TM,),
        in_specs=[_row(D), _row(D_A), _row(D_A), _vec(D_A), _vec(D_A)],
        out_specs=[_row(D_A), _row(D_A), _vec(D_A), _vec(D_A)],
        out_shape=[jax.ShapeDtypeStruct((t, D_A), F32)] * 2 + [jax.ShapeDtypeStruct((1, D_A), F32)] * 2,
        compiler_params=_cp("arbitrary"),
    )(dy, out_a, out_b, g_a, g_b)


def _resid_norm2(x, mix, g1, g, sc, sh):
    t = x.shape[0]

    def body(x_ref, mix_ref, g1_ref, g_ref, sc_ref, sh_ref, x2_ref, h_ref):
        x2 = x_ref[...] + g1_ref[0] * mix_ref[...]
        x2_ref[...] = x2
        n = x2 * _rms(x2)
        h_ref[...] = ((n * g_ref[...]) * (1.0 + sc_ref[0]) + sh_ref[0]).astype(BF16)

    return pl.pallas_call(
        body, name="resid_norm2", grid=(t // TM,),
        in_specs=[_row(D), _row(D), _per_ex(D), _vec(D), _per_ex(D), _per_ex(D)],
        out_specs=[_row(D), _row(D)],
        out_shape=[jax.ShapeDtypeStruct((t, D), F32), jax.ShapeDtypeStruct((t, D), BF16)],
        compiler_params=_cp("parallel"),
    )(x, mix, g1, g, sc, sh)


def _sigmoid(v):
    return 1.0 / (1.0 + jnp.exp(-v))


def _final(x2, f, g2, g_fin, target):
    t = x2.shape[0]
    nb = t // S
    tpb = S // TM

    def body(x2_ref, f_ref, g2_ref, g_ref, t_ref, dx3_ref, df_ref, loss_ref, dgf_ref, dg2_ref):
        i = pl.program_id(0)
        fv = f_ref[...].astype(F32)
        x3 = x2_ref[...] + g2_ref[0] * fv
        r = _rms(x3)
        n = x3 * r
        err = n * g_ref[...] - t_ref[...]
        _acc_first(i, loss_ref, _colsum(err * err))
        dy = err * (1.0 / D)
        _acc_first(i, dgf_ref, _colsum(dy * n))
        dx3 = _rms_bwd(n, r, dy * g_ref[...])
        dx3_ref[...] = dx3
        _acc_first(i, dg2_ref, _colsum(dx3 * fv), every=tpb)
        df_ref[...] = (dx3 * g2_ref[0]).astype(BF16)

    return pl.pallas_call(
        body, name="final", grid=(t // TM,),
        in_specs=[_row(D), _row(D), _per_ex(D), _vec(D), _row(D)],
        out_specs=[_row(D), _row(D), _vec(D), _vec(D), _per_ex(D)],
        out_shape=[jax.ShapeDtypeStruct((t, D), F32), jax.ShapeDtypeStruct((t, D), BF16),
                   jax.ShapeDtypeStruct((1, D), F32), jax.ShapeDtypeStruct((1, D), F32),
                   jax.ShapeDtypeStruct((nb, 1, D), F32)],
        compiler_params=_cp("arbitrary"),
    )(x2, f, g2, g_fin, target)


def _norm_bwd(xin, dh, dres, g, sc, gate=None):
    t = xin.shape[0]
    nb = t // S
    tpb = S // TM
    gated = gate is not None

    def body(*refs):
        if gated:
            x_ref, dh_ref, dres_ref, g_ref, sc_ref, mix_ref, g1_ref, dx_ref, dsh_ref, dsc_ref, dg_ref, dg1_ref, dmix_ref = refs
        else:
            x_ref, dh_ref, dres_ref, g_ref, sc_ref, dx_ref, dsh_ref, dsc_ref, dg_ref = refs
        i = pl.program_id(0)
        xv, dhv = x_ref[...], dh_ref[...].astype(F32)
        r = _rms(xv)
        n = xv * r
        gv = g_ref[...]
        _acc_first(i, dsh_ref, _colsum(dhv), every=tpb)
        _acc_first(i, dsc_ref, _colsum(dhv * (n * gv)), every=tpb)
        dng = dhv * (1.0 + sc_ref[0])
        _acc_first(i, dg_ref, _colsum(dng * n))
        dx = dres_ref[...] + _rms_bwd(n, r, dng * gv)
        dx_ref[...] = dx
        if gated:
            _acc_first(i, dg1_ref, _colsum(dx * mix_ref[...].astype(F32)), every=tpb)
            dmix_ref[...] = (dx * g1_ref[0]).astype(BF16)

    in_specs = [_row(D), _row(D), _row(D), _vec(D), _per_ex(D)]
    out_specs = [_row(D), _per_ex(D), _per_ex(D), _vec(D)]
    out_shape = [jax.ShapeDtypeStruct((t, D), F32), jax.ShapeDtypeStruct((nb, 1, D), F32),
                 jax.ShapeDtypeStruct((nb, 1, D), F32), jax.ShapeDtypeStruct((1, D), F32)]
    args = [xin, dh, dres, g, sc]
    if gated:
        in_specs += [_row(D), _per_ex(D)]
        out_specs += [_per_ex(D), _row(D)]
        out_shape += [jax.ShapeDtypeStruct((nb, 1, D), F32), jax.ShapeDtypeStruct((t, D), BF16)]
        args += list(gate)
    return pl.pallas_call(
        body, name="norm2_bwd" if gated else "norm1_bwd", grid=(t // TM,),
        in_specs=in_specs, out_specs=out_specs, out_shape=out_shape,
        compiler_params=_cp("arbitrary"),
    )(*args)


TQ = 256
TB = 512
FWD_HEADS = 2


def _mla_fwd(q, k, v):
    t = q.shape[0]
    nb = t // S

    def body(q_ref, k_ref, v_ref, o_ref, lse_ref):
        causal = lax.broadcasted_iota(jnp.int32, (TB, TB), 0) >= lax.broadcasted_iota(jnp.int32, (TB, TB), 1)
        heads = [slice(h * HP, (h + 1) * HP) for h in range(FWD_HEADS)]
        for i in range(S // TB):
            ri, past = slice(i * TB, (i + 1) * TB), slice(0, i * TB)
            qhs = [q_ref[ri, sl] for sl in heads]
            sd = [jnp.where(causal, _dot(qh, k_ref[ri, sl], ((1,), (1,))) * SCALE_B, NEG) for qh, sl in zip(qhs, heads)]
            ms = [jnp.max(s, axis=-1, keepdims=True) for s in sd]
            if i:
                so = [_dot(qh, k_ref[past, sl], ((1,), (1,))) * SCALE_B for qh, sl in zip(qhs, heads)]
                ms = [jnp.maximum(m, jnp.max(s, axis=-1, keepdims=True)) for m, s in zip(ms, so)]
            pd = [jnp.exp(s - m) for s, m in zip(sd, ms)]
            ls = [jnp.sum(p, axis=-1, keepdims=True) for p in pd]
            acc = [_dot(p.astype(BF16), v_ref[ri, sl], ((1,), (0,))) for p, sl in zip(pd, heads)]
            if i:
                po = [jnp.exp(s - m) for s, m in zip(so, ms)]
                ls = [l + jnp.sum(p, axis=-1, keepdims=True) for l, p in zip(ls, po)]
                acc = [a + _dot(p.astype(BF16), v_ref[past, sl], ((1,), (0,))) for a, p, sl in zip(acc, po, heads)]
            for pr in range(FWD_HEADS // 2):
                o_ref[ri, pr * HP:(pr + 1) * HP] = acc[2 * pr] / ls[2 * pr] + acc[2 * pr + 1] / ls[2 * pr + 1]
            for sl, m, l in zip(heads, ms, ls):
                lse_ref[ri, sl] = jnp.broadcast_to(m + jnp.log(l), (TB, HP))

    wide2 = pl.BlockSpec((S, FWD_HEADS * HP), lambda b, p: (b, p))
    return pl.pallas_call(
        body, name="mla_fwd", grid=(nb, H // FWD_HEADS),
        in_specs=[wide2, wide2, wide2],
        out_specs=[pl.BlockSpec((S, FWD_HEADS // 2 * HP), lambda b, p: (b, p)), wide2],
        out_shape=[jax.ShapeDtypeStruct((t, H * VDIM), F32), jax.ShapeDtypeStruct((t, H * HP), F32)],
        compiler_params=_cp("parallel", "parallel"),
    )(q, k, v)


def _mla_bwd(q, k, v, o, do, lse):
    t = q.shape[0]
    nb = t // S

    def body(q_ref, k_ref, v_ref, o_ref, do_ref, lse_ref, dq_out, dk_out, dv_out, dq_ref, dk_ref, dv_ref):
        lane = lax.broadcasted_iota(jnp.int32, (TB, HP), 1)
        causal = lax.broadcasted_iota(jnp.int32, (TB, TB), 0) >= lax.broadcasted_iota(jnp.int32, (TB, TB), 1)
        heads = [slice(h * HP, (h + 1) * HP) for h in range(2)]
        nblk = S // TB
        for i in reversed(range(nblk)):
            ri, past = slice(i * TB, (i + 1) * TB), slice(0, i * TB)
            dov = do_ref[ri, :]
            prod = dov * o_ref[ri, :]
            dob = dov.astype(BF16)
            deltas = [jnp.sum(jnp.where((lane < VDIM) if h == 0 else (lane >= VDIM), prod, 0.0), axis=-1, keepdims=True)
                      for h in range(2)]
            qhs = [q_ref[ri, sl] for sl in heads]
            lses = [lse_ref[ri, sl][:, :1] for sl in heads]
            for rows, diagonal in ((ri, True), (past, False)):
                if rows.stop == rows.start:
                    continue
                ps = [jnp.exp(_dot(qh, k_ref[rows, sl], ((1,), (1,))) * SCALE_B - lse) for qh, sl, lse in zip(qhs, heads, lses)]
                if diagonal:
                    ps = [jnp.where(causal, p, 0.0) for p in ps]
                dps = [_dot(dob, v_ref[rows, sl], ((1,), (1,))) for sl in heads]
                dss = [(p * (dp - delta) * SCALE_B).astype(BF16) for p, dp, delta in zip(ps, dps, deltas)]
                for sl, qh, p, ds in zip(heads, qhs, ps, dss):
                    dq = _dot(ds, k_ref[rows, sl], ((1,), (0,)))
                    dk = _dot(ds, qh, ((0,), (0,)))
                    dv = _dot(p.astype(BF16), dob, ((0,), (0,)))
                    if diagonal:
                        dq_ref[ri, sl] = dq
                    else:
                        dq_ref[ri, sl] += dq
                    if i == nblk - 1:
                        dk_ref[rows, sl] = dk
                        dv_ref[rows, sl] = dv
                    else:
                        dk_ref[rows, sl] += dk
                        dv_ref[rows, sl] += dv
        dq_out[...] = dq_ref[...].astype(BF16)
        dk_out[...] = dk_ref[...].astype(BF16)
        dv_out[...] = dv_ref[...].astype(BF16)

    wide2 = pl.BlockSpec((S, 2 * HP), lambda b, p: (b, p))
    pair = pl.BlockSpec((S, HP), lambda b, p: (b, p))
    return pl.pallas_call(
        body, name="mla_bwd", grid=(nb, H // 2),
        in_specs=[wide2, wide2, wide2, pair, pair, wide2],
        out_specs=[wide2, wide2, wide2],
        out_shape=[jax.ShapeDtypeStruct((t, H * HP), BF16)] * 3,
        scratch_shapes=[pltpu.VMEM((S, 2 * HP), F32)] * 3,
        compiler_params=_cp("parallel", "parallel"),
    )(q, k, v, o, do, lse)


def _t5_bucket(dist):
    max_exact = N_BUCKETS // 2
    d = np.maximum(dist, 1).astype(np.float64)
    large = max_exact + (np.log(d / max_exact) / np.log(MAX_DISTANCE / max_exact) * (N_BUCKETS - max_exact)).astype(np.int64)
    large = np.minimum(large, N_BUCKETS - 1)
    return np.where(dist < max_exact, dist, large).astype(np.int32)


def _band_geometry():
    a = np.arange(BLK)[:, None]
    bk = np.arange(2 * BLK)[None, :]
    steps = BLK + a - bk
    valid = (steps >= 0) & (steps <= BLK)
    buckets = np.stack([_t5_bucket(np.clip(steps, 0, BLK) * d) for d in DILATIONS])
    return buckets, valid


def _band_bias(rel_bias):
    buckets, valid = _band_geometry()
    onehot = (jnp.asarray(buckets)[..., None] == jnp.arange(N_BUCKETS)).astype(F32)
    bias = jnp.einsum("rqkn,nh->rhqk", onehot, rel_bias, precision=lax.Precision.HIGHEST)
    bias = jnp.where(jnp.asarray(valid)[None, None], bias, NEG)
    return bias.reshape(3, H // 2, 2 * BLK, 2 * BLK)


def _dil_items():
    items = []
    for r, d in enumerate(DILATIONS):
        for res in range(d):
            for blk in range(S // d // BLK):
                items.append((r, d, blk * BLK * d + res, blk > 0))
    return items


GROUP = 4


def _strided(start, d):
    return pl.ds(start, BLK) if d == 1 else pl.ds(start, BLK, stride=d)


def _stack_heads(tile, own):
    return jnp.where(own, jnp.concatenate([tile, tile], axis=0), 0.0).astype(BF16)


def _own_lanes():
    row = lax.broadcasted_iota(jnp.int32, (2 * BLK, HP), 0)
    lane = lax.broadcasted_iota(jnp.int32, (2 * BLK, HP), 1)
    return (lane < E_A) == (row < BLK)


def _dil_fwd(proj, biasm):
    t = proj.shape[0]
    nb = t // S

    def body(q_ref, k_ref, v_ref, b_ref, o_ref, lse_ref, ob_ref, lb_ref):
        lane = lax.broadcasted_iota(jnp.int32, (BLK, HP), 1)
        own = _own_lanes()
        items = _dil_items()
        for g in range(0, len(items), GROUP):
            grp = items[g:g + GROUP]
            ss, vts = [], []
            for r, d, start, has_prev in grp:
                cur = _strided(start, d)
                rows = [_strided(start - BLK * d, d), cur] if has_prev else [cur]
                q2 = _stack_heads(q_ref[cur, :] * SCALE_A, own)
                kt = jnp.concatenate([k_ref[x, :] for x in rows], axis=0).astype(BF16)
                vts.append(jnp.concatenate([v_ref[x, :] for x in rows], axis=0).astype(BF16))
                bias = b_ref[r, 0] if has_prev else b_ref[r, 0, :, BLK:]
                ss.append(_dot(q2, kt, ((1,), (1,))) + bias)
            ms = [jnp.max(s, axis=-1, keepdims=True) for s in ss]
            ps = [jnp.exp(s - m) for s, m in zip(ss, ms)]
            ls = [jnp.sum(p, axis=-1, keepdims=True) for p in ps]
            for (r, d, start, _), p, vt, m, l in zip(grp, ps, vts, ms, ls):
                cur = _strided(start, d)
                o2 = _dot(p.astype(BF16), vt, ((1,), (0,))) / l
                lse2 = m + jnp.log(l)
                ob_ref[r, cur, :] = jnp.where(lane < E_A, o2[:BLK], o2[BLK:])
                lb_ref[r, cur, :] = jnp.where(lane < E_A, lse2[:BLK], lse2[BLK:])

        def merge(c, _):
            rows = pl.ds(pl.multiple_of(c * TQ, TQ), TQ)
            l0, l1, l2 = lb_ref[0, rows, :], lb_ref[1, rows, :], lb_ref[2, rows, :]
            m = jnp.maximum(jnp.maximum(l0, l1), l2)
            e0, e1, e2 = jnp.exp(l0 - m), jnp.exp(l1 - m), jnp.exp(l2 - m)
            tot = e0 + e1 + e2
            o_ref[rows, :] = (e0 * ob_ref[0, rows, :] + e1 * ob_ref[1, rows, :] + e2 * ob_ref[2, rows, :]) / tot
            lse_ref[rows, :] = m + jnp.log(tot)
            return 0

        lax.fori_loop(0, S // TQ, merge, 0)

    npair = H // 2
    return pl.pallas_call(
        body, name="dil_fwd", grid=(nb, npair),
        in_specs=[pl.BlockSpec((S, HP), lambda b, p: (b, p)), pl.BlockSpec((S, HP), lambda b, p: (b, npair + p)),
                  pl.BlockSpec((S, HP), lambda b, p: (b, 2 * npair + p)),
                  pl.BlockSpec((3, 1, 2 * BLK, 2 * BLK), lambda b, p: (0, p, 0, 0))],
        out_specs=[pl.BlockSpec((S, HP), lambda b, p: (b, p))] * 2,
        out_shape=[jax.ShapeDtypeStruct((t, D_A), F32)] * 2,
        scratch_shapes=[pltpu.VMEM((3, S, HP), F32), pltpu.VMEM((3, S, HP), F32)],
        compiler_params=_cp("parallel", "parallel"),
    )(proj, proj, proj, biasm)


def _dil_bwd(proj, biasm, o, do, lse):
    t = proj.shape[0]
    nb = t // S

    def body(q_ref, k_ref, v_ref, b_ref, o_ref, do_ref, lse_ref, dq_out, dk_out, dv_out, ds_ref, dq_ref, dk_ref, dv_ref):
        dq_ref[...] = jnp.zeros_like(dq_ref)
        dk_ref[...] = jnp.zeros_like(dk_ref)
        dv_ref[...] = jnp.zeros_like(dv_ref)
        ds_ref[...] = jnp.zeros_like(ds_ref)
        lane = lax.broadcasted_iota(jnp.int32, (BLK, HP), 1)
        own = _own_lanes()
        items = _dil_items()
        for g in range(0, len(items), GROUP):
            grp = items[g:g + GROUP]
            q2s, kts, do2s, ss, dps, lse2s, delta2s = [], [], [], [], [], [], []
            for r, d, start, has_prev in grp:
                cur = _strided(start, d)
                rows = [_strided(start - BLK * d, d), cur] if has_prev else [cur]
                q2 = _stack_heads(q_ref[cur, :] * SCALE_A, own)
                kt = jnp.concatenate([k_ref[x, :] for x in rows], axis=0).astype(BF16)
                vt = jnp.concatenate([v_ref[x, :] for x in rows], axis=0).astype(BF16)
                dot_ = do_ref[cur, :]
                prod = dot_ * o_ref[cur, :]
                lset = lse_ref[cur, :]
                do2 = _stack_heads(dot_, own)
                bias = b_ref[r, 0] if has_prev else b_ref[r, 0, :, BLK:]
                ss.append(_dot(q2, kt, ((1,), (1,))) + bias)
                dps.append(_dot(do2, vt, ((1,), (1,))))
                lse2s.append(jnp.concatenate([lset[:, :1], lset[:, E_A:E_A + 1]], axis=0))
                delta2s.append(jnp.concatenate([jnp.sum(jnp.where(lane < E_A, prod, 0.0), axis=-1, keepdims=True),
                                                jnp.sum(jnp.where(lane >= E_A, prod, 0.0), axis=-1, keepdims=True)], axis=0))
                q2s.append(q2)
                kts.append(kt)
                do2s.append(do2)
            ps = [jnp.exp(s - lse2) for s, lse2 in zip(ss, lse2s)]
            dls = [p * (dp - delta2) for p, dp, delta2 in zip(ps, dps, delta2s)]
            for (r, d, start, has_prev), q2, kt, do2, p, dl in zip(grp, q2s, kts, do2s, ps, dls):
                cur = _strided(start, d)
                dsb = dl.astype(BF16)
                dq2 = _dot(dsb, kt, ((1,), (0,))) * SCALE_A
                dkt = _dot(dsb, q2, ((0,), (0,)))
                dvt = _dot(p.astype(BF16), do2, ((0,), (0,)))
                dq_ref[cur, :] += jnp.where(lane < E_A, dq2[:BLK], dq2[BLK:])
                if has_prev:
                    prev = _strided(start - BLK * d, d)
                    ds_ref[0, r, 0] += dl
                    dk_ref[prev, :] += dkt[:BLK]
                    dv_ref[prev, :] += dvt[:BLK]
                    dk_ref[cur, :] += dkt[BLK:]
                    dv_ref[cur, :] += dvt[BLK:]
                else:
                    ds_ref[0, r, 0, :, BLK:] += dl
                    dk_ref[cur, :] += dkt
                    dv_ref[cur, :] += dvt
        dq_out[...] = dq_ref[...].astype(BF16)
        dk_out[...] = dk_ref[...].astype(BF16)
        dv_out[...] = dv_ref[...].astype(BF16)

    npair = H // 2
    pair = pl.BlockSpec((S, HP), lambda b, p: (b, p))
    return pl.pallas_call(
        body, name="dil_bwd", grid=(nb, npair),
        in_specs=[pair, pl.BlockSpec((S, HP), lambda b, p: (b, npair + p)),
                  pl.BlockSpec((S, HP), lambda b, p: (b, 2 * npair + p)),
                  pl.BlockSpec((3, 1, 2 * BLK, 2 * BLK), lambda b, p: (0, p, 0, 0)), pair, pair, pair],
        out_specs=[pair, pair, pair, pl.BlockSpec((1, 3, 1, 2 * BLK, 2 * BLK), lambda b, p: (b, 0, p, 0, 0))],
        out_shape=[jax.ShapeDtypeStruct((t, D_A), BF16)] * 3 + [jax.ShapeDtypeStruct((nb, 3, npair, 2 * BLK, 2 * BLK), F32)],
        scratch_shapes=[pltpu.VMEM((S, HP), F32)] * 3,
        compiler_params=_cp("parallel", "parallel"),
    )(proj, proj, proj, biasm, o, do, lse)


def _rel_bias_grad(dlogits):
    nb = dlogits.shape[0]
    buckets, _ = _band_geometry()
    kk = 3 * BLK * 2 * BLK
    dl = jnp.transpose(dlogits.reshape(nb, 3, H, BLK, 2 * BLK), (0, 2, 1, 3, 4)).reshape(nb, H, kk)
    bk = jnp.asarray(buckets.reshape(1, kk))
    tk = kk // 12

    def body(dl_ref, bk_ref, o_ref):
        j = pl.program_id(0)
        onehot = (bk_ref[...] == lax.broadcasted_iota(jnp.int32, (N_BUCKETS, tk), 0)).astype(F32)
        tot = dl_ref[0]
        for b in range(1, nb):
            tot = tot + dl_ref[b]
        part = lax.dot_general(onehot, tot, ((((1,), (1,))), ((), ())), preferred_element_type=F32,
                               precision=lax.Precision.HIGHEST)
        _acc_first(j, o_ref, part)

    return pl.pallas_call(
        body, name="rel_bias_grad", grid=(kk // tk,),
        in_specs=[pl.BlockSpec((nb, H, tk), lambda j: (0, 0, j)), pl.BlockSpec((1, tk), lambda j: (0, j))],
        out_specs=pl.BlockSpec((N_BUCKETS, H), lambda j: (0, 0)),
        out_shape=jax.ShapeDtypeStruct((N_BUCKETS, H), F32),
        compiler_params=_cp("arbitrary"),
    )(dl, bk)


def _mesh_place():
    x, y, c = lax.axis_index("x"), lax.axis_index("y"), lax.axis_index("c")
    return x, y, c


def _peer(k):
    x, y, c = _mesh_place()
    px = 1 - x if k & 4 else x
    py = 1 - y if k & 2 else y
    pc = 1 - c if k & 1 else c
    return (px, py, pc), 4 * px + 2 * py + pc


ANY = pl.BlockSpec(memory_space=pl.ANY)


def _exchange(arrays, gathers, name, after=None):
    n_arr = len(arrays)

    def body(*refs):
        ins, outs = refs[:n_arr], refs[n_arr + 1:2 * n_arr + 1]
        send, recv, loc = refs[2 * n_arr + 1:]
        x, y, c = _mesh_place()
        me = 4 * x + 2 * y + c
        local = [pltpu.make_async_copy(ins[a] if gathers[a] else ins[a].at[me], outs[a].at[me], loc.at[a])
                 for a in range(n_arr)]
        remote = _peer_copies(ins, outs, send, recv, gathers)
        for cp in local:
            cp.start()
        for put, _ in remote:
            put.start()
        for cp in local:
            cp.wait()
        for put, got in remote:
            put.wait_send()
            got.wait_recv()

    return pl.pallas_call(
        body, name=name,
        in_specs=[ANY] * (n_arr + 1), out_specs=[ANY] * n_arr,
        out_shape=[jax.ShapeDtypeStruct(((N_DEV,) if g else ()) + a.shape, a.dtype) for a, g in zip(arrays, gathers)],
        scratch_shapes=[pltpu.SemaphoreType.DMA((n_arr * (N_DEV - 1),)), pltpu.SemaphoreType.DMA((n_arr * (N_DEV - 1),)),
                        pltpu.SemaphoreType.DMA((n_arr,))],
        compiler_params=pltpu.CompilerParams(has_side_effects=True),
    )(*arrays, arrays[0] if after is None else after)


def _gather_two_level(arrays, name):
    n_arr = len(arrays)
    per = N_DEV - 1

    def body(*refs):
        ins, outs = refs[:n_arr], refs[n_arr:2 * n_arr]
        send, recv, loc = refs[2 * n_arr:]
        x, y, c = _mesh_place()
        me, sibling = (x, y, c), (x, y, 1 - c)
        chips = [(1 - x, y), (x, 1 - y), (1 - x, 1 - y)]

        def block(a, place):
            px, py, pc = place
            return outs[a].at[4 * px + 2 * py + pc]

        def copy(a, k, place, to, src=None):
            dst = block(a, place)
            return pltpu.make_async_remote_copy(dst if src is None else src, dst, send.at[a * per + k], recv.at[a * per + k],
                                                device_id=to, device_id_type=pl.DeviceIdType.MESH)

        local = [pltpu.make_async_copy(ins[a], block(a, me), loc.at[a]) for a in range(n_arr)]
        for cp in local:
            cp.start()
        first = []
        for a in range(n_arr):
            first.append(copy(a, 0, me, sibling, src=ins[a]))
            first += [copy(a, 1 + j, me, (*chip, c), src=ins[a]) for j, chip in enumerate(chips)]
        for cp in first:
            cp.start()
        passed = []
        for j, chip in enumerate(chips):
            for a in range(n_arr):
                copy(a, 1 + j, (*chip, c), me).wait_recv()
                passed.append(copy(a, 4 + j, (*chip, c), sibling))
                passed[-1].start()
        for a in range(n_arr):
            copy(a, 0, sibling, me).wait_recv()
            for j, chip in enumerate(chips):
                copy(a, 4 + j, (*chip, 1 - c), me).wait_recv()
        for cp in first + passed:
            cp.wait_send()
        for cp in local:
            cp.wait()

    return pl.pallas_call(
        body, name=name,
        in_specs=[ANY] * n_arr, out_specs=[ANY] * n_arr,
        out_shape=[jax.ShapeDtypeStruct((N_DEV,) + a.shape, a.dtype) for a in arrays],
        scratch_shapes=[pltpu.SemaphoreType.DMA((n_arr * per,)), pltpu.SemaphoreType.DMA((n_arr * per,)),
                        pltpu.SemaphoreType.DMA((n_arr,))],
        compiler_params=pltpu.CompilerParams(has_side_effects=True),
    )(*arrays)


HBM = pl.BlockSpec(memory_space=pltpu.HBM)
SEM = pl.BlockSpec(memory_space=pltpu.SEMAPHORE)
DATAFLOW = pltpu.SideEffectType.DATAFLOW_SIDE_EFFECTING


def _own_block_in_place(block, me):
    land = lax.empty((N_DEV,) + block.shape, block.dtype)
    return lax.dynamic_update_slice(land, block[None], (me,) + (0,) * block.ndim)


def _peer_copies(srcs, lands, send, recv, gathers):
    x, y, c = _mesh_place()
    me = 4 * x + 2 * y + c
    out = []
    for a, (src, land) in enumerate(zip(srcs, lands)):
        for k in range(1, N_DEV):
            dev, idx = _peer(k)
            sem = a * (N_DEV - 1) + k - 1
            mine = src if gathers[a] else src.at[idx]
            put = pltpu.make_async_remote_copy(mine, land.at[me], send.at[sem], recv.at[sem],
                                               device_id=dev, device_id_type=pl.DeviceIdType.MESH)
            got = pltpu.make_async_remote_copy(mine, land.at[idx], send.at[sem], recv.at[sem],
                                               device_id=dev, device_id_type=pl.DeviceIdType.MESH)
            out.append((put, got))
    return out


def _exchange_start(srcs, lands, gather, after, name):
    n = len(srcs)
    extra = [] if after is None else [after]

    def body(*refs):
        srcs_, lands_ = refs[:n], refs[n:2 * n]
        send, recv = refs[2 * n + len(extra)], refs[2 * n + len(extra) + 1]
        for put, _ in _peer_copies(srcs_, lands_, send, recv, gather):
            put.start()
        refs[-1][...] = jnp.zeros_like(refs[-1])

    nsem = n * (N_DEV - 1)
    thru = [pltpu.HBM(a.shape, a.dtype) for a in list(srcs) + list(lands)]
    res = pl.pallas_call(
        body, name=name,
        out_shape=(pltpu.SemaphoreType.DMA((nsem,)), pltpu.SemaphoreType.DMA((nsem,)), *thru, jax.ShapeDtypeStruct((8, 128), F32)),
        in_specs=[HBM] * (2 * n) + [ANY] * len(extra),
        out_specs=(SEM, SEM, *([HBM] * (2 * n)), pl.BlockSpec(memory_space=pltpu.VMEM)),
        input_output_aliases={i: 2 + i for i in range(2 * n)},
        compiler_params=pltpu.CompilerParams(has_side_effects=DATAFLOW),
    )(*[pltpu.with_memory_space_constraint(a, pltpu.HBM) for a in list(srcs) + list(lands)], *extra)
    return res[0], res[1], list(res[2:2 + n]), list(res[2 + n:2 + 2 * n]), res[-1]


def _exchange_wait(send, recv, srcs, lands, gather, after, name):
    n = len(srcs)

    def body(*refs):
        srcs_, lands_, send_, recv_ = refs[:n], refs[n:2 * n], refs[2 * n], refs[2 * n + 1]
        for put, got in _peer_copies(srcs_, lands_, send_, recv_, gather):
            put.wait_send()
            got.wait_recv()

    thru = [pltpu.HBM(a.shape, a.dtype) for a in list(srcs) + list(lands)]
    res = pl.pallas_call(
        body, name=name, out_shape=tuple(thru),
        in_specs=[HBM] * (2 * n) + [SEM, SEM, ANY], out_specs=tuple([HBM] * (2 * n)),
        input_output_aliases={i: i for i in range(2 * n)},
        compiler_params=pltpu.CompilerParams(has_side_effects=DATAFLOW),
    )(*srcs, *lands, send, recv, after)
    return list(res[n:])


def _silu_rows(c):
    def body(c_ref, o_ref):
        v = c_ref[...]
        o_ref[...] = v * _sigmoid(v)

    return pl.pallas_call(body, name="cond", out_shape=jax.ShapeDtypeStruct(c.shape, F32))(c)


def _mod_slab(cond_all, w_ada, b_slab):
    def body(c_ref, w_ref, b_ref, o_ref):
        o_ref[...] = _dot(c_ref[...].astype(BF16), w_ref[0].astype(BF16), ((1,), (0,))) + b_ref[...]

    return pl.pallas_call(body, name="mod_slab",
                          out_shape=jax.ShapeDtypeStruct((cond_all.shape[0], w_ada.shape[2]), F32),
                          compiler_params=pltpu.CompilerParams(vmem_limit_bytes=VMEM_LIMIT))(cond_all, w_ada, b_slab)


def _ada_grad(cond_all, dmod_cols):
    def body(c_ref, d_ref, o_ref):
        o_ref[...] = _dot(c_ref[...].astype(BF16), d_ref[...].astype(BF16), ((0,), (0,)))

    return pl.pallas_call(body, name="ada_grad",
                          out_shape=jax.ShapeDtypeStruct((cond_all.shape[1], dmod_cols.shape[1]), F32),
                          compiler_params=pltpu.CompilerParams(vmem_limit_bytes=VMEM_LIMIT))(cond_all, dmod_cols)


def _adam_math(g, w, m, v):
    m2 = B1 * m + (1.0 - B1) * g
    v2 = B2 * v + (1.0 - B2) * (g * g)
    m_hat = m2 / (1.0 - B1 ** STEP)
    v_hat = v2 / (1.0 - B2 ** STEP)
    return -LR * (m_hat / (jnp.sqrt(v_hat) + ADAM_EPS) + WD * w), m2, v2


def _adamw(parts, w, m, v, name):
    n, rows, cols = parts.shape
    tr = max([p for p in range(16, 513, 16) if rows % p == 0] or [rows])

    def body(p_ref, w_ref, m_ref, v_ref, g_ref, d_ref, m2_ref, v2_ref):
        g = p_ref[0].astype(F32)
        for s in range(1, n):
            g = g + p_ref[s].astype(F32)
        g_ref[0] = g
        d_ref[0], m2_ref[0], v2_ref[0] = _adam_math(g, w_ref[0], m_ref[0], v_ref[0])

    blk = pl.BlockSpec((1, tr, cols), lambda i: (0, i, 0))
    return pl.pallas_call(
        body, name=name, grid=(rows // tr,),
        in_specs=[pl.BlockSpec((n, tr, cols), lambda i: (0, i, 0)), blk, blk, blk],
        out_specs=[blk] * 4, out_shape=[jax.ShapeDtypeStruct((1, rows, cols), F32)] * 4,
        compiler_params=_cp("parallel"),
    )(parts, w, m, v)


ROW_PARAMS = (("g_norm1", D), ("g_cq", Q_LORA), ("g_ckv", KV_LORA), ("g_out_a", D_A), ("g_out_b", D_A), ("g_norm2", D),
              ("g_final", D))
LOSS_ROW = N_MOD + len(ROW_PARAMS)
PAY_ROWS = 16
NCOL = N_MOD * D // N_DEV


def _pack_small(dmods, rows, loss_cols):
    nb = dmods[0].shape[0]
    nrow = len(ROW_PARAMS)

    def body(*refs):
        dm, rw, loss_ref, pay_ref, blk_ref = refs[:N_MOD], refs[N_MOD:N_MOD + nrow], refs[N_MOD + nrow], refs[-2], refs[-1]
        pay_ref[...] = jnp.zeros_like(pay_ref)
        for k in range(N_MOD):
            tot = dm[k][0]
            for b in range(1, nb):
                tot = tot + dm[k][b]
            pay_ref[k:k + 1, :] = tot
        for i, (_, n) in enumerate(ROW_PARAMS):
            pay_ref[N_MOD + i:N_MOD + i + 1, :n] = rw[i][...]
        pay_ref[LOSS_ROW:LOSS_ROW + 1, :] = loss_ref[...]
        for j in range(N_DEV):
            done = 0
            while done < NCOL:
                seg, off = divmod(j * NCOL + done, D)
                ln = min(NCOL - done, D - off)
                for b in range(nb):
                    blk_ref[j, b:b + 1, done:done + ln] = dm[seg][b][:, off:off + ln]
                done += ln

    return pl.pallas_call(
        body, name="pack_small",
        out_shape=[jax.ShapeDtypeStruct((PAY_ROWS, D), F32), jax.ShapeDtypeStruct((N_DEV, nb, NCOL), F32)],
    )(*dmods, *rows, loss_cols)


def _small_update(pay, rel, ws, ms, vs):
    n_par = len(ws)

    def body(*refs):
        pay_ref, rel_ref = refs[:2]
        w_refs, m_refs, v_refs = (refs[2 + s * n_par:2 + (s + 1) * n_par] for s in range(3))
        outs, loss_ref = refs[2 + 3 * n_par:-1], refs[-1]
        tot, rtot = pay_ref[0], rel_ref[0]
        for s in range(1, N_DEV):
            tot, rtot = tot + pay_ref[s], rtot + rel_ref[s]

        def update(p, g, sl):
            outs[4 * p][:, sl] = g
            outs[4 * p + 1][:, sl], outs[4 * p + 2][:, sl], outs[4 * p + 3][:, sl] = _adam_math(
                g, w_refs[p][:, sl], m_refs[p][:, sl], v_refs[p][:, sl])

        for k in range(N_MOD):
            update(0, tot[k:k + 1, :], slice(k * D, (k + 1) * D))
        for i, (_, n) in enumerate(ROW_PARAMS):
            update(1 + i, tot[N_MOD + i:N_MOD + i + 1, :n], slice(0, n))
        update(n_par - 1, rtot, slice(0, H))
        loss_ref[...] = jnp.broadcast_to((0.5 / D) * jnp.sum(tot[LOSS_ROW:LOSS_ROW + 1, :]), loss_ref.shape)

    shapes = [jax.ShapeDtypeStruct(w.shape, F32) for w in ws for _ in range(4)]
    res = pl.pallas_call(
        body, name="small_update", out_shape=shapes + [jax.ShapeDtypeStruct((8, 128), F32)],
    )(pay, rel, *ws, *ms, *vs)
    return [tuple(res[4 * p:4 * p + 4]) for p in range(n_par)], res[-1]


def _cols_from_blocks(g):
    return jnp.transpose(g, (1, 0, 2)).reshape(g.shape[1], N_DEV * g.shape[2])


def _cols_to_blocks(w):
    r, c = w.shape
    return jnp.transpose(w.reshape(r, N_DEV, c // N_DEV), (1, 0, 2))


def _pad_w_in(wt):
    z = jnp.zeros((NOPE, wt.shape[1]), wt.dtype)
    return jnp.concatenate([wt[:P_IN - ROPE], z, wt[P_IN - ROPE:], z[:HP - NOPE - ROPE]], axis=0)


def _unpad_w_in(gt):
    k0 = P_IN - ROPE + NOPE
    return jnp.concatenate([gt[:P_IN - ROPE], gt[k0:k0 + ROPE]], axis=0)


def _pad_w_uq(wt):
    return jnp.pad(wt, ((0, 0), (0, HP - NOPE - ROPE), (0, 0))).reshape(H * HP, Q_LORA)


def _unpad_w_uq(gt):
    return gt.reshape(H, HP, Q_LORA)[:, :NOPE + ROPE]


def _split_w_ukv(w):
    w4 = w.reshape(KV_LORA, H // 2, 2, HP)
    z = jnp.zeros((KV_LORA, H // 2, NOPE), w.dtype)
    kn, vv = w4[..., :NOPE], w4[..., NOPE:]
    w_k = jnp.stack([jnp.concatenate([kn[:, :, 0], z], -1), jnp.concatenate([kn[:, :, 1], z], -1)], axis=2)
    w_v = jnp.stack([jnp.concatenate([vv[:, :, 0], z], -1), jnp.concatenate([z, vv[:, :, 1]], -1)], axis=2)
    return w_k.reshape(KV_LORA, H * HP), w_v.reshape(KV_LORA, H * HP)


def _join_w_ukv(g_k, g_v):
    gk = g_k.reshape(KV_LORA, H // 2, 2, HP)
    gv = g_v.reshape(KV_LORA, H // 2, 2, HP)
    even = jnp.concatenate([gk[:, :, 0, :NOPE], gv[:, :, 0, :VDIM]], -1)
    odd = jnp.concatenate([gk[:, :, 1, :NOPE], gv[:, :, 1, VDIM:]], -1)
    return jnp.stack([even, odd], axis=2).reshape(KV_LORA, H * HP)


def _rope_tables():
    half = ROPE // 2
    inv = np.float32(ROPE_THETA) ** (-np.arange(half, dtype=np.float32) / np.float32(half))
    ang = np.arange(S, dtype=np.float32)[:, None] * inv[None, :].astype(np.float32)
    cos, sin = np.cos(ang).astype(np.float32), np.sin(ang).astype(np.float32)
    ones, zeros = np.ones((S, NOPE), np.float32), np.zeros((S, NOPE), np.float32)
    tail1, tail0 = np.ones((S, HP - NOPE - ROPE), np.float32), np.zeros((S, HP - NOPE - ROPE), np.float32)
    zh = np.zeros((S, half), np.float32)
    c = np.concatenate([ones, cos, cos, tail1], axis=1)
    sm = np.concatenate([zeros, -sin, zh, tail0], axis=1)
    sp = np.concatenate([zeros, zh, sin, tail0], axis=1)
    return jnp.asarray(c), jnp.asarray(sm), jnp.asarray(sp)


def _local_step(x, mod, target, g_norm1, w_in_p, g_cq, w_uq_p, g_ckv, w_k, w_v, rel_bias, g_out_a, g_out_b, w_out,
                g_norm2, w_ffn_in, w_ffn_out, g_final, late_weights=None, on_ffn_grads=None, on_last_grads=None):
    nb = x.shape[0] // S
    sh1, sc1, g1, sh2, sc2, g2 = (mod[:, n].reshape(nb, 1, D) for n in range(N_MOD))
    rc, rsm, rsp = _rope_tables()
    biasm = _band_bias(rel_bias)

    h1 = _pre1(x, g_norm1, sc1, sh1)
    proj = _mm_nt(h1, w_in_p, F32, "proj")
    q, k, v, cqn, ckvn = _mla_pre(proj, g_cq, g_ckv, w_uq_p, w_k, w_v, rc, rsm, rsp)
    out_b, lse_b = _mla_fwd(q, k, v)
    out_a, lse_a = _dil_fwd(proj, biasm)
    y = _post_attn(out_a, out_b, g_out_a, g_out_b)
    if late_weights is not None:
        w_out, w_ffn_in, w_ffn_out = late_weights(y)
    mix = _mm_nn(y, w_out, BF16, "mix")
    x2, h2 = _resid_norm2(x, mix, g1, g_norm2, sc2, sh2)
    ffn_g, ffn_u, act = _ffn_in(h2, w_ffn_in)
    f = _mm_nn(act, w_ffn_out, BF16, "ffn_out")
    dx3, df, loss_cols, dg_final, dg2 = _final(x2, f, g2, g_final, target)

    dg_, du_ = _d_act(df, w_ffn_out, ffn_g, ffn_u)
    gw_ffn_out = _mm_tn_rows([act], df, "gw_ffn_out")
    dh2 = _d_h2(dg_, du_, w_ffn_in)
    gw_ffn_in = _mm_tn_rows([dg_, du_], h2, "gw_ffn_in")
    dx2, dsh2, dsc2, dg_norm2, dg1, dmix = _norm_bwd(x2, dh2, dx3, g_norm2, sc2, gate=(mix, g1))
    dy = _mm_nt(dmix, w_out, BF16, "d_y")
    gw_out = _mm_tn(y, [dmix], "gw_out")
    if on_ffn_grads is not None:
        g_out_a = g_out_a + on_ffn_grads(gw_ffn_in, gw_ffn_out, gw_out)
    dout_a, dout_b, dg_out_a, dg_out_b = _post_attn_bwd(dy, out_a, out_b, g_out_a, g_out_b)
    dq_b, dk_b, dv_b = _mla_bwd(q, k, v, out_b, dout_b, lse_b)
    dq_a, dk_a, dv_a, dlogits = _dil_bwd(proj, biasm, out_a, dout_a, lse_a)
    g_rel = _rel_bias_grad(dlogits)
    dqr, dproj, dg_cq, dg_ckv = _mla_pre_bwd(proj, dq_b, dk_b, dv_b, (dq_a, dk_a, dv_a), g_cq, g_ckv, w_uq_p, w_k, w_v,
                                             rc, rsm, rsp)
    gw_uq = _mm_tn(dqr, [cqn], "gw_uq")
    gw_k, gw_v = _mm_tn(ckvn, [dk_b, dv_b], "gw_kv")
    gw_in = _mm_tn_rows([dproj], h1, "gw_in")
    if on_last_grads is not None:
        started = on_last_grads(dict(w_in=gw_in, w_uq=gw_uq, w_k=gw_k, w_v=gw_v))
    else:
        started = None
    dh1 = _mm_nn(dproj, w_in_p, BF16, "d_h1", after=started)
    grad_x, dsh1, dsc1, dg_norm1 = _norm_bwd(x, dh1, dx2, g_norm1, sc1)

    dmod = [dsh1, dsc1, dg1, dsh2, dsc2, dg2]
    small = dict(g_norm1=dg_norm1, g_cq=dg_cq, g_ckv=dg_ckv, rel_bias=g_rel, g_out_a=dg_out_a, g_out_b=dg_out_b,
                 g_norm2=dg_norm2, g_final=dg_final)
    big = dict(w_in=gw_in, w_uq=gw_uq, w_k=gw_k, w_v=gw_v, w_out=gw_out, w_ffn_in=gw_ffn_in, w_ffn_out=gw_ffn_out)
    return grad_x, dmod, loss_cols, small, big


def kernel(x, c, w_ada, b_ada, g_norm1, w_in, g_cq, w_uq, g_ckv, w_ukv, rel_bias, g_out_a, g_out_b, w_out, g_norm2, w_ffn_in, w_ffn_out, g_final, loss_target, m_w_ada, m_b_ada, m_g_norm1, m_w_in, m_g_cq, m_w_uq, m_g_ckv, m_w_ukv, m_rel_bias, m_g_out_a, m_g_out_b, m_w_out, m_g_norm2, m_w_ffn_in, m_w_ffn_out, m_g_final, v_w_ada, v_b_ada, v_g_norm1, v_w_in, v_g_cq, v_w_uq, v_g_ckv, v_w_ukv, v_rel_bias, v_g_out_a, v_g_out_b, v_w_out, v_g_norm2, v_w_ffn_in, v_w_ffn_out, v_g_final):
    nb = x.shape[0]
    t = nb * S
    xt, tt = x.reshape(t, D), loss_target.reshape(t, D)
    me = 4 * lax.axis_index("x") + 2 * lax.axis_index("y") + lax.axis_index("c")

    early = [jnp.swapaxes(w_in, 1, 2)[0], jnp.swapaxes(w_uq, 1, 2)[0], w_ukv[0]]
    gathered = _gather_two_level([_silu_rows(c)] + [s.astype(BF16) for s in early], "gather_weights")
    cond_all = gathered[0].reshape(N_DEV * nb, D)
    w_in_t = gathered[1].reshape(P_IN, D)
    w_ukv_f = _cols_from_blocks(gathered[3])
    w_k, w_v = _split_w_ukv(w_ukv_f)

    ncol = N_MOD * D // N_DEV
    b_slab = lax.dynamic_slice(b_ada, (0, me * ncol), (1, ncol))
    slab = _mod_slab(cond_all, w_ada, b_slab)
    (mod_rows,) = _exchange([slab.reshape(N_DEV, nb, ncol)], [False], "scatter_mod")
    mod = jnp.transpose(mod_rows, (1, 0, 2)).reshape(nb, N_MOD, D)

    late = [s.astype(BF16) for s in (w_out[0], jnp.swapaxes(w_ffn_in, 1, 2)[0], w_ffn_out[0])]
    late_send, late_recv, late_src, late_land, late_token = _exchange_start(
        late, [_own_block_in_place(s, me) for s in late], [True] * 3, mod_rows, "gather_late_start")
    g_norm1_t = g_norm1 + late_token[:1, :1]

    def late_weights(after):
        w_out_g, w_ffn_in_g, w_ffn_out_g = _exchange_wait(late_send, late_recv, late_src, late_land, [True] * 3, after,
                                                          "gather_late_wait")
        return w_out_g.reshape(D, D), w_ffn_in_g.reshape(2 * D_FF, D), w_ffn_out_g.reshape(D_FF, D)

    flight = {}

    def start_grads(key, src, name):
        land = [_own_block_in_place(lax.dynamic_index_in_dim(s, me, 0, keepdims=False), me) for s in src]
        send, recv, src, land, token = _exchange_start(src, land, [False] * len(src), None, name)
        flight[key] = (send, recv, src, land)
        return token[:1, :1]

    def on_ffn_grads(gw_ffn_in, gw_ffn_out, gw_out):
        return start_grads("ffn", [gw_ffn_in.reshape(N_DEV, 2 * D_FF // N_DEV, D), gw_ffn_out.reshape(N_DEV, D_FF // N_DEV, D),
                                   gw_out.reshape(N_DEV, D // N_DEV, D)], "exchange_ffn_start")

    def on_last_grads(gw):
        return start_grads("rest", [_unpad_w_in(gw["w_in"]).reshape(N_DEV, P_IN // N_DEV, D),
                                    _unpad_w_uq(gw["w_uq"]),
                                    _cols_to_blocks(_join_w_ukv(gw["w_k"], gw["w_v"]))], "exchange_rest_start")

    grad_x, dmod, loss_cols, small, _ = _local_step(
        xt, mod, tt, g_norm1_t, _pad_w_in(w_in_t), g_cq, _pad_w_uq(gathered[2]), g_ckv, w_k, w_v, rel_bias, g_out_a, g_out_b,
        None, g_norm2, None, None, g_final.reshape(1, D), late_weights=late_weights, on_ffn_grads=on_ffn_grads,
        on_last_grads=on_last_grads)

    upd = {}

    def land_and_update(key, names, after, name):
        got = _exchange_wait(*flight[key], [False] * len(names), after, name)
        for n, p in zip(names, got):
            w, m, v = big[n]
            upd[n] = _adamw(p, w, m, v, "adamw_" + n)

    def flip(a):
        return jnp.swapaxes(a, 1, 2)

    big = dict(w_in=(flip(w_in), flip(m_w_in), flip(v_w_in)), w_uq=(flip(w_uq), flip(m_w_uq), flip(v_w_uq)),
               w_ukv=(w_ukv, m_w_ukv, v_w_ukv),
               w_out=(w_out, m_w_out, v_w_out), w_ffn_in=(flip(w_ffn_in), flip(m_w_ffn_in), flip(v_w_ffn_in)),
               w_ffn_out=(w_ffn_out, m_w_ffn_out, v_w_ffn_out))
    land_and_update("ffn", ["w_ffn_in", "w_ffn_out", "w_out"], grad_x, "exchange_ffn_wait")
    land_and_update("rest", ["w_in", "w_uq", "w_ukv"], upd["w_out"][0], "exchange_rest_wait")
    for n in ("w_in", "w_uq", "w_ffn_in"):
        upd[n] = tuple(flip(a) for a in upd[n])

    mine, dmod_blocks = _pack_small(dmod, [small[n] for n, _ in ROW_PARAMS], loss_cols)
    dmod_cols, pay, rel = _exchange([dmod_blocks, mine, small["rel_bias"]], [False, True, True], "exchange_small",
                                    after=upd["w_ukv"][0])
    g_ada = _ada_grad(cond_all, dmod_cols.reshape(N_DEV * nb, ncol))
    upd["w_ada"] = _adamw(g_ada[None], w_ada, m_w_ada, v_w_ada, "adamw_w_ada")
    row = lambda a: a.reshape(1, D)
    small_names = ["b_ada"] + [n for n, _ in ROW_PARAMS] + ["rel_bias"]
    small_w = [b_ada, g_norm1, g_cq, g_ckv, g_out_a, g_out_b, g_norm2, row(g_final), rel_bias]
    small_m = [m_b_ada, m_g_norm1, m_g_cq, m_g_ckv, m_g_out_a, m_g_out_b, m_g_norm2, row(m_g_final), m_rel_bias]
    small_v = [v_b_ada, v_g_norm1, v_g_cq, v_g_ckv, v_g_out_a, v_g_out_b, v_g_norm2, row(v_g_final), v_rel_bias]
    small_upd, loss8 = _small_update(pay, rel, small_w, small_m, small_v)
    upd.update(zip(small_names, small_upd))

    order = ["w_ada", "b_ada", "g_norm1", "w_in", "g_cq", "w_uq", "g_ckv", "w_ukv", "rel_bias", "g_out_a", "g_out_b",
             "w_out", "g_norm2", "w_ffn_in", "w_ffn_out", "g_final"]
    like = dict(g_final=g_final)
    outs = [loss8[0, 0], grad_x.reshape(x.shape)]
    for part in range(4):
        for n in order:
            val = upd[n][part]
            outs.append(val.reshape(like[n].shape) if n in like else val)
    return tuple(outs)
```

```python
import functools

import numpy as np
import jax
import jax.numpy as jnp
from jax import lax
from jax.experimental import pallas as pl
from jax.experimental.pallas import tpu as pltpu

F32, BF16 = jnp.float32, jnp.bfloat16

N_DEV = 8
D = 1024
S = 2048
H = 8
E_A = 64
D_A = H * E_A
Q_LORA, KV_LORA = 384, 256
NOPE, ROPE, VDIM = 64, 32, 64
HP = 128
P_IN = 3 * D_A + Q_LORA + KV_LORA + ROPE
P_PAD = 3 * D_A + Q_LORA + KV_LORA + HP
TAIL0 = 3 * D_A
TAIL = P_PAD - TAIL0
D_FF = 2816
N_MOD = 6
EPS = 1e-6
NEG = -1e30
BLK = 128
DILATIONS = (1, 4, 16)
N_BUCKETS, MAX_DISTANCE = 32, 2048
ROPE_THETA = 10000.0
SCALE_A = E_A ** -0.5
SCALE_B = (NOPE + ROPE) ** -0.5
B1, B2, LR, ADAM_EPS, WD, STEP = 0.9, 0.999, 0.001, 1e-8, 0.01, 10
VMEM_LIMIT = 56 * 1024 * 1024


def _cp(*sem):
    return pltpu.CompilerParams(dimension_semantics=sem, vmem_limit_bytes=VMEM_LIMIT)


def _pick(n, prefs):
    for p in prefs:
        if n % p == 0:
            return p
    raise ValueError(f"no tile of {prefs} divides {n}")


OPERAND_BYTES = 6 * 1024 * 1024


def _pick_rows(m, k):
    return _pick(m, [p for p in (1024, 512, 256, 128, 16) if p * k * 2 <= OPERAND_BYTES])


MATMUL_BYTES = 44 * 1024 * 1024


def _stream_rows(m, fixed, per_row):
    return _pick(m, [p for p in (4096, 2048, 1024, 512, 256, 128, 16) if fixed + p * per_row <= MATMUL_BYTES])


def _dot(a, b, dims):
    return lax.dot_general(a, b, (dims, ((), ())), preferred_element_type=F32)


def _mm_nn(a, b, out_dtype, name, after=None):
    m, k = a.shape
    n = b.shape[1]
    tn = _pick(n, (512, 384, 256, 128))
    tm = _stream_rows(m, 4 * k * tn, 4 * k + (2 * jnp.dtype(out_dtype).itemsize + 4) * tn)

    def body(a_ref, b_ref, *rest):
        o_ref = rest[-1]
        o_ref[...] = _dot(a_ref[...], b_ref[...], ((1,), (0,))).astype(o_ref.dtype)

    extra = [] if after is None else [after]
    return pl.pallas_call(
        body, name=name, grid=(m // tm, n // tn),
        in_specs=[pl.BlockSpec((tm, k), lambda i, j: (i, 0)), pl.BlockSpec((k, tn), lambda i, j: (0, j))] + [ANY] * len(extra),
        out_specs=pl.BlockSpec((tm, tn), lambda i, j: (i, j)),
        out_shape=jax.ShapeDtypeStruct((m, n), out_dtype),
        compiler_params=_cp("parallel", "parallel"),
    )(a, b, *extra)


def _mm_nt(a, b, out_dtype, name, after=None):
    m, k = a.shape
    n = b.shape[0]
    tn = _pick(n, (512, 384, 256, 128))
    tm = _stream_rows(m, 4 * k * tn, 4 * k + (2 * jnp.dtype(out_dtype).itemsize + 4) * tn)

    def body(a_ref, b_ref, *rest):
        o_ref = rest[-1]
        o_ref[...] = _dot(a_ref[...], b_ref[...], ((1,), (1,))).astype(o_ref.dtype)

    extra = [] if after is None else [after]
    return pl.pallas_call(
        body, name=name, grid=(m // tm, n // tn),
        in_specs=[pl.BlockSpec((tm, k), lambda i, j: (i, 0)), pl.BlockSpec((tn, k), lambda i, j: (j, 0))] + [ANY] * len(extra),
        out_specs=pl.BlockSpec((tm, tn), lambda i, j: (i, j)),
        out_shape=jax.ShapeDtypeStruct((m, n), out_dtype),
        compiler_params=_cp("parallel", "parallel"),
    )(a, b, *extra)


def _mm_tn(a, bs, name):
    t, m = a.shape
    n = bs[0].shape[1]
    nb_ = len(bs)
    tc = _pick(t, (512, 16))
    tn = _pick(n, (512, 384, 256, 128))
    tm = _pick(m, [p for p in (1024, 512, 384, 256, 128) if (3 * p + 2 * nb_ * tn) * t * 2 <= VMEM_LIMIT - 2 * OPERAND_BYTES])
    if tm <= 256 and nb_ * n * t * 2 <= 2 * OPERAND_BYTES:
        tn = n

    def body(*refs):
        a_ref, b_refs, o_refs, at_ref = refs[0], refs[1:1 + nb_], refs[1 + nb_:1 + 2 * nb_], refs[-1]

        @pl.when(pl.program_id(1) == 0)
        def _():
            def chunk(c, _):
                rows = pl.ds(pl.multiple_of(c * tc, tc), tc)
                at_ref[:, rows] = a_ref[rows, :].T
                return 0

            lax.fori_loop(0, t // tc, chunk, 0)

        for b_ref, o_ref in zip(b_refs, o_refs):
            o_ref[...] = _dot(at_ref[...], b_ref[...], ((1,), (0,))).astype(BF16)

    res = pl.pallas_call(
        body, name=name, grid=(m // tm, n // tn),
        in_specs=[pl.BlockSpec((t, tm), lambda i, j: (0, i))] + [pl.BlockSpec((t, tn), lambda i, j: (0, j))] * nb_,
        out_specs=[pl.BlockSpec((tm, tn), lambda i, j: (i, j))] * nb_,
        out_shape=[jax.ShapeDtypeStruct((m, n), BF16)] * nb_,
        scratch_shapes=[pltpu.VMEM((tm, t), BF16)],
        compiler_params=_cp("parallel", "arbitrary"),
    )(a, *bs)
    return res[0] if nb_ == 1 else res


def _mm_tn_rows(a_list, b, name):
    t, m = a_list[0].shape
    n = b.shape[1]
    na = len(a_list)
    tc, tm = _pick(t, (512, 16)), _pick(m, (256, 128))
    nblk = m // tm

    def body(*refs):
        a_refs, b_ref, o_ref, bt_ref, r_ref = refs[:na], refs[na], refs[na + 1], refs[na + 2], refs[na + 3]
        i = pl.program_id(0)

        @pl.when(i == 0)
        def _():
            def chunk(c, _):
                rows = pl.ds(pl.multiple_of(c * tc, tc), tc)
                bt_ref[:, rows] = b_ref[rows, :].T
                return 0

            lax.fori_loop(0, t // tc, chunk, 0)

        for s, a_ref in enumerate(a_refs):
            @pl.when((i >= s * nblk) & (i < (s + 1) * nblk))
            def _(a_ref=a_ref):
                r_ref[...] = _dot(bt_ref[...], a_ref[...], ((1,), (0,)))
                o_ref[...] = r_ref[...].T.astype(BF16)

    return pl.pallas_call(
        body, name=name, grid=(na * nblk,),
        in_specs=[pl.BlockSpec((t, tm), lambda i, s=s: (0, jnp.clip(i - s * nblk, 0, nblk - 1))) for s in range(na)]
        + [pl.BlockSpec((t, n), lambda i: (0, 0))],
        out_specs=pl.BlockSpec((tm, n), lambda i: (i, 0)),
        out_shape=jax.ShapeDtypeStruct((na * m, n), BF16),
        scratch_shapes=[pltpu.VMEM((n, t), BF16), pltpu.VMEM((n, tm), F32)],
        compiler_params=_cp("arbitrary"),
    )(*a_list, b)


EPI = 256


def _silu_parts(g):
    sg = 0.5 * jnp.tanh(0.5 * g) + 0.5
    return sg, g * sg


def _ffn_in(h2, wt):
    t, k = h2.shape
    tn = _pick(D_FF, (256, 128))
    tm = _stream_rows(t, 8 * k * tn, 4 * k + (4 * 2 * 2 + 2 * 4) * tn)
    nj = D_FF // tn

    def body(h_ref, wg_ref, wu_ref, u_ref, s_ref, ds_ref, a_ref):
        hv = h_ref[...]
        g_all = _dot(hv, wg_ref[...], ((1,), (1,)))
        u_all = _dot(hv, wu_ref[...], ((1,), (1,)))
        for r in range(tm // EPI):
            rows = slice(r * EPI, (r + 1) * EPI)
            g, u = g_all[rows], u_all[rows]
            sg, silu = _silu_parts(g)
            u_ref[rows, :] = u.astype(BF16)
            s_ref[rows, :] = silu.astype(BF16)
            ds_ref[rows, :] = (sg + silu * (1.0 - sg)).astype(BF16)
            a_ref[rows, :] = (silu * u).astype(BF16)

    blk = pl.BlockSpec((tm, tn), lambda i, j: (i, j))
    return pl.pallas_call(
        body, name="ffn_in", grid=(t // tm, nj),
        in_specs=[pl.BlockSpec((tm, k), lambda i, j: (i, 0)), pl.BlockSpec((tn, k), lambda i, j: (j, 0)),
                  pl.BlockSpec((tn, k), lambda i, j: (j + nj, 0))],
        out_specs=[blk] * 4, out_shape=[jax.ShapeDtypeStruct((t, D_FF), BF16)] * 4,
        compiler_params=_cp("parallel", "parallel"),
    )(h2, wt, wt)


def _d_act(df, w, u, silu, dsilu):
    t, k = df.shape
    tn = _pick(D_FF, (256, 128))
    tm = _stream_rows(t, 4 * k * tn, 4 * k + (5 * 2 * 2 + 4) * tn)

    def body(df_ref, w_ref, u_ref, s_ref, ds_ref, dg_ref, du_ref):
        da_all = _dot(df_ref[...], w_ref[...], ((1,), (1,)))
        for r in range(tm // EPI):
            rows = slice(r * EPI, (r + 1) * EPI)
            da = da_all[rows]
            dg_ref[rows, :] = ((da * u_ref[rows, :].astype(F32)) * ds_ref[rows, :].astype(F32)).astype(BF16)
            du_ref[rows, :] = (da * s_ref[rows, :].astype(F32)).astype(BF16)

    blk = pl.BlockSpec((tm, tn), lambda i, j: (i, j))
    return pl.pallas_call(
        body, name="d_act", grid=(t // tm, D_FF // tn),
        in_specs=[pl.BlockSpec((tm, k), lambda i, j: (i, 0)), pl.BlockSpec((tn, k), lambda i, j: (j, 0)), blk, blk, blk],
        out_specs=[blk] * 2, out_shape=[jax.ShapeDtypeStruct((t, D_FF), BF16)] * 2,
        compiler_params=_cp("parallel", "parallel"),
    )(df, w, u, silu, dsilu)


def _d_h2(dg, du, wt):
    t = dg.shape[0]
    n = wt.shape[1]
    tm, tn = _pick_rows(t, D_FF), _pick(n, (512, 256, 128))

    def body(dg_ref, du_ref, wg_ref, wu_ref, o_ref):
        o_ref[...] = (_dot(dg_ref[...], wg_ref[...], ((1,), (0,)))
                      + _dot(du_ref[...], wu_ref[...], ((1,), (0,)))).astype(BF16)

    return pl.pallas_call(
        body, name="d_h2", grid=(t // tm, n // tn),
        in_specs=[pl.BlockSpec((tm, D_FF), lambda i, j: (i, 0)), pl.BlockSpec((tm, D_FF), lambda i, j: (i, 0)),
                  pl.BlockSpec((D_FF, tn), lambda i, j: (0, j)), pl.BlockSpec((D_FF, tn), lambda i, j: (1, j))],
        out_specs=pl.BlockSpec((tm, tn), lambda i, j: (i, j)),
        out_shape=jax.ShapeDtypeStruct((t, n), BF16),
        compiler_params=_cp("parallel", "parallel"),
    )(dg, du, wt, wt)


TM = 1024


def _row(w):
    return pl.BlockSpec((TM, w), lambda i: (i, 0))


def _row_at(w, col):
    return pl.BlockSpec((TM, w), lambda i: (i, col))


def _vec(w):
    return pl.BlockSpec((1, w), lambda i: (0, 0))


def _per_ex(w):
    return pl.BlockSpec((1, 1, w), lambda i: (i // (S // TM), 0, 0))


def _pos(w):
    return pl.BlockSpec((TM, w), lambda i: (i % (S // TM), 0))


def _full(shape):
    return pl.BlockSpec(shape, lambda i: (0,) * len(shape))


def _rms(x):
    return lax.rsqrt(jnp.mean(x * x, axis=-1, keepdims=True) + EPS)


def _rms_bwd(n, r, dn):
    return r * (dn - n * jnp.mean(dn * n, axis=-1, keepdims=True))


def _colsum(v):
    return jnp.sum(v, axis=0, keepdims=True)


def _acc_first(i, ref, val, every=None):
    first = (i == 0) if every is None else (i % every == 0)

    @pl.when(first)
    def _():
        ref[...] = jnp.zeros_like(ref)

    ref[...] += val.reshape(ref.shape)


def _pre1(x, g, sc, sh):
    t = x.shape[0]

    def body(x_ref, g_ref, sc_ref, sh_ref, h_ref):
        xv = x_ref[...]
        n = xv * _rms(xv)
        h_ref[...] = ((n * g_ref[...]) * (1.0 + sc_ref[0]) + sh_ref[0]).astype(BF16)

    return pl.pallas_call(
        body, name="pre1", grid=(t // TM,),
        in_specs=[_row(D), _vec(D), _per_ex(D), _per_ex(D)],
        out_specs=_row(D), out_shape=jax.ShapeDtypeStruct((t, D), BF16),
        compiler_params=_cp("parallel"),
    )(x, g, sc, sh)


def _rope_fwd(v, c, sm, sp):
    return v * c + pltpu.roll(v, HP - ROPE // 2, 1) * sm + pltpu.roll(v, ROPE // 2, 1) * sp


def _rope_bwd(dv, c, sm, sp):
    return dv * c + pltpu.roll(dv * sm, ROPE // 2, 1) + pltpu.roll(dv * sp, HP - ROPE // 2, 1)


def _mla_pre(proj, g_cq, g_ckv, w_uq, w_k, w_v, rc, rsm, rsp):
    t = proj.shape[0]

    def body(tail_ref, gq_ref, gkv_ref, wuq_ref, wk_ref, wv_ref, c_ref, sm_ref, sp_ref,
             q_ref, k_ref, v_ref, cqn_ref, ckvn_ref):
        tail = tail_ref[...]
        cq, ckv, kr = tail[:, :Q_LORA], tail[:, Q_LORA:Q_LORA + KV_LORA], tail[:, Q_LORA + KV_LORA:]
        cqn = (cq * _rms(cq) * gq_ref[...]).astype(BF16)
        ckvn = (ckv * _rms(ckv) * gkv_ref[...]).astype(BF16)
        cqn_ref[...] = cqn
        ckvn_ref[...] = ckvn
        c, sm, sp = c_ref[...], sm_ref[...], sp_ref[...]
        q = _dot(cqn, wuq_ref[...], ((1,), (1,)))
        kn = _dot(ckvn, wk_ref[...], ((1,), (0,)))
        v_ref[...] = _dot(ckvn, wv_ref[...], ((1,), (0,))).astype(BF16)
        krr = _rope_fwd(kr, c, sm, sp)
        for h in range(H):
            sl = slice(h * HP, (h + 1) * HP)
            q_ref[:, sl] = _rope_fwd(q[:, sl], c, sm, sp).astype(BF16)
            k_ref[:, sl] = (kn[:, sl] + krr).astype(BF16)

    wide = H * HP
    return pl.pallas_call(
        body, name="mla_pre", grid=(t // TM,),
        in_specs=[_row_at(TAIL, TAIL0 // TAIL), _vec(Q_LORA), _vec(KV_LORA), _full((wide, Q_LORA)),
                  _full((KV_LORA, wide)), _full((KV_LORA, wide)), _pos(HP), _pos(HP), _pos(HP)],
        out_specs=[_row(wide), _row(wide), _row(wide), _row(Q_LORA), _row(KV_LORA)],
        out_shape=[jax.ShapeDtypeStruct((t, wide), BF16)] * 3
        + [jax.ShapeDtypeStruct((t, Q_LORA), BF16), jax.ShapeDtypeStruct((t, KV_LORA), BF16)],
        compiler_params=_cp("parallel"),
    )(proj, g_cq, g_ckv, w_uq, w_k, w_v, rc, rsm, rsp)


def _mla_pre_bwd(proj, dq_, dk_, dv_, dqkv_a, g_cq, g_ckv, w_uq, w_k, w_v, rc, rsm, rsp):
    t = proj.shape[0]
    wide = H * HP

    def body(tail_ref, dq_ref, dk_ref, dv_ref, dqa_ref, dka_ref, dva_ref, gq_ref, gkv_ref, wuq_ref, wk_ref, wv_ref,
             c_ref, sm_ref, sp_ref, dqo_ref, dproj_ref, dgq_ref, dgkv_ref):
        i = pl.program_id(0)
        for n, src in enumerate((dqa_ref, dka_ref, dva_ref)):
            dproj_ref[:, n * D_A:(n + 1) * D_A] = src[...]
        dtail_ref = dproj_ref.at[:, TAIL0:]
        tail = tail_ref[...]
        cq, ckv = tail[:, :Q_LORA], tail[:, Q_LORA:Q_LORA + KV_LORA]
        c, sm, sp = c_ref[...], sm_ref[...], sp_ref[...]
        dkr = jnp.zeros((TM, HP), F32)
        for h in range(H):
            sl = slice(h * HP, (h + 1) * HP)
            dqo_ref[:, sl] = _rope_bwd(dq_ref[:, sl].astype(F32), c, sm, sp).astype(BF16)
            dkr = dkr + dk_ref[:, sl].astype(F32)
        lane = lax.broadcasted_iota(jnp.int32, (TM, HP), 1)
        dkr = jnp.where((lane >= NOPE) & (lane < NOPE + ROPE), _rope_bwd(dkr, c, sm, sp), 0.0)
        dkb = dk_ref[...]
        dvb = dv_ref[...]
        dcqn = _dot(dqo_ref[...], wuq_ref[...], ((1,), (0,)))
        dckvn = _dot(dkb, wk_ref[...], ((1,), (1,))) + _dot(dvb, wv_ref[...], ((1,), (1,)))
        rq, rkv = _rms(cq), _rms(ckv)
        nq, nkv = cq * rq, ckv * rkv
        _acc_first(i, dgq_ref, _colsum(dcqn * nq))
        _acc_first(i, dgkv_ref, _colsum(dckvn * nkv))
        dtail_ref[:, :Q_LORA] = _rms_bwd(nq, rq, dcqn * gq_ref[...]).astype(BF16)
        dtail_ref[:, Q_LORA:Q_LORA + KV_LORA] = _rms_bwd(nkv, rkv, dckvn * gkv_ref[...]).astype(BF16)
        dtail_ref[:, Q_LORA + KV_LORA:] = dkr.astype(BF16)

    return pl.pallas_call(
        body, name="mla_pre_bwd", grid=(t // TM,),
        in_specs=[_row_at(TAIL, TAIL0 // TAIL), _row(wide), _row(wide), _row(wide), _row(D_A), _row(D_A), _row(D_A),
                  _vec(Q_LORA), _vec(KV_LORA), _full((wide, Q_LORA)), _full((KV_LORA, wide)), _full((KV_LORA, wide)),
                  _pos(HP), _pos(HP), _pos(HP)],
        out_specs=[_row(wide), _row(P_PAD), _vec(Q_LORA), _vec(KV_LORA)],
        out_shape=[jax.ShapeDtypeStruct((t, wide), BF16), jax.ShapeDtypeStruct((t, P_PAD), BF16),
                   jax.ShapeDtypeStruct((1, Q_LORA), F32), jax.ShapeDtypeStruct((1, KV_LORA), F32)],
        compiler_params=_cp("arbitrary"),
    )(proj, dq_, dk_, dv_, *dqkv_a, g_cq, g_ckv, w_uq, w_k, w_v, rc, rsm, rsp)


def _post_attn(out_a, out_b, g_a, g_b):
    t = out_a.shape[0]

    def body(a_ref, b_ref, ga_ref, gb_ref, y_ref):
        a, b = a_ref[...], b_ref[...]
        y_ref[:, :D_A] = (a * _rms(a) * ga_ref[...]).astype(BF16)
        y_ref[:, D_A:] = (b * _rms(b) * gb_ref[...]).astype(BF16)

    return pl.pallas_call(
        body, name="post_attn", grid=(t // TM,),
        in_specs=[_row(D_A), _row(D_A), _vec(D_A), _vec(D_A)],
        out_specs=_row(D), out_shape=jax.ShapeDtypeStruct((t, D), BF16),
        compiler_params=_cp("parallel"),
    )(out_a, out_b, g_a, g_b)


def _post_attn_bwd(dy, out_a, out_b, g_a, g_b):
    t = dy.shape[0]

    def body(dy_ref, a_ref, b_ref, ga_ref, gb_ref, da_ref, db_ref, dga_ref, dgb_ref):
        i = pl.program_id(0)
        dy_ = dy_ref[...].astype(F32)
        for src, g_ref, dst, dg_ref, sl in ((a_ref, ga_ref, da_ref, dga_ref, slice(0, D_A)),
                                            (b_ref, gb_ref, db_ref, dgb_ref, slice(D_A, D))):
            v = src[...]
            r = _rms(v)
            n = v * r
            dyv = dy_[:, sl]
            _acc_first(i, dg_ref, _colsum(dyv * n))
            dst[...] = _rms_bwd(n, r, dyv * g_ref[...])

    return pl.pallas_call(
        body, name="post_attn_bwd", grid=(t // TM,),
        in_specs=[_row(D), _row(D_A), _row(D_A), _vec(D_A), _vec(D_A)],
        out_specs=[_row(D_A), _row(D_A), _vec(D_A), _vec(D_A)],
        out_shape=[jax.ShapeDtypeStruct((t, D_A), F32)] * 2 + [jax.ShapeDtypeStruct((1, D_A), F32)] * 2,
        compiler_params=_cp("arbitrary"),
    )(dy, out_a, out_b, g_a, g_b)


def _resid_norm2(x, mix, g1, g, sc, sh):
    t = x.shape[0]

    def body(x_ref, mix_ref, g1_ref, g_ref, sc_ref, sh_ref, x2_ref, h_ref):
        x2 = x_ref[...] + g1_ref[0] * mix_ref[...]
        x2_ref[...] = x2
        n = x2 * _rms(x2)
        h_ref[...] = ((n * g_ref[...]) * (1.0 + sc_ref[0]) + sh_ref[0]).astype(BF16)

    return pl.pallas_call(
        body, name="resid_norm2", grid=(t // TM,),
        in_specs=[_row(D), _row(D), _per_ex(D), _vec(D), _per_ex(D), _per_ex(D)],
        out_specs=[_row(D), _row(D)],
        out_shape=[jax.ShapeDtypeStruct((t, D), F32), jax.ShapeDtypeStruct((t, D), BF16)],
        compiler_params=_cp("parallel"),
    )(x, mix, g1, g, sc, sh)


def _sigmoid(v):
    return 1.0 / (1.0 + jnp.exp(-v))


def _final(x2, f, g2, g_fin, target):
    t = x2.shape[0]
    nb = t // S
    tpb = S // TM

    def body(x2_ref, f_ref, g2_ref, g_ref, t_ref, dx3_ref, df_ref, loss_ref, dgf_ref, dg2_ref):
        i = pl.program_id(0)
        fv = f_ref[...].astype(F32)
        x3 = x2_ref[...] + g2_ref[0] * fv
        r = _rms(x3)
        n = x3 * r
        err = n * g_ref[...] - t_ref[...]
        _acc_first(i, loss_ref, _colsum(err * err))
        dy = err * (1.0 / D)
        _acc_first(i, dgf_ref, _colsum(dy * n))
        dx3 = _rms_bwd(n, r, dy * g_ref[...])
        dx3_ref[...] = dx3
        _acc_first(i, dg2_ref, _colsum(dx3 * fv), every=tpb)
        df_ref[...] = (dx3 * g2_ref[0]).astype(BF16)

    return pl.pallas_call(
        body, name="final", grid=(t // TM,),
        in_specs=[_row(D), _row(D), _per_ex(D), _vec(D), _row(D)],
        out_specs=[_row(D), _row(D), _vec(D), _vec(D), _per_ex(D)],
        out_shape=[jax.ShapeDtypeStruct((t, D), F32), jax.ShapeDtypeStruct((t, D), BF16),
                   jax.ShapeDtypeStruct((1, D), F32), jax.ShapeDtypeStruct((1, D), F32),
                   jax.ShapeDtypeStruct((nb, 1, D), F32)],
        compiler_params=_cp("arbitrary"),
    )(x2, f, g2, g_fin, target)


def _norm_bwd(xin, dh, dres, g, sc, gate=None):
    t = xin.shape[0]
    nb = t // S
    tpb = S // TM
    gated = gate is not None

    def body(*refs):
        if gated:
            x_ref, dh_ref, dres_ref, g_ref, sc_ref, mix_ref, g1_ref, dx_ref, dsh_ref, dsc_ref, dg_ref, dg1_ref, dmix_ref = refs
        else:
            x_ref, dh_ref, dres_ref, g_ref, sc_ref, dx_ref, dsh_ref, dsc_ref, dg_ref = refs
        i = pl.program_id(0)
        xv, dhv = x_ref[...], dh_ref[...].astype(F32)
        r = _rms(xv)
        n = xv * r
        gv = g_ref[...]
        _acc_first(i, dsh_ref, _colsum(dhv), every=tpb)
        _acc_first(i, dsc_ref, _colsum(dhv * (n * gv)), every=tpb)
        dng = dhv * (1.0 + sc_ref[0])
        _acc_first(i, dg_ref, _colsum(dng * n))
        dx = dres_ref[...] + _rms_bwd(n, r, dng * gv)
        dx_ref[...] = dx
        if gated:
            _acc_first(i, dg1_ref, _colsum(dx * mix_ref[...].astype(F32)), every=tpb)
            dmix_ref[...] = (dx * g1_ref[0]).astype(BF16)

    in_specs = [_row(D), _row(D), _row(D), _vec(D), _per_ex(D)]
    out_specs = [_row(D), _per_ex(D), _per_ex(D), _vec(D)]
    out_shape = [jax.ShapeDtypeStruct((t, D), F32), jax.ShapeDtypeStruct((nb, 1, D), F32),
                 jax.ShapeDtypeStruct((nb, 1, D), F32), jax.ShapeDtypeStruct((1, D), F32)]
    args = [xin, dh, dres, g, sc]
    if gated:
        in_specs += [_row(D), _per_ex(D)]
        out_specs += [_per_ex(D), _row(D)]
        out_shape += [jax.ShapeDtypeStruct((nb, 1, D), F32), jax.ShapeDtypeStruct((t, D), BF16)]
        args += list(gate)
    return pl.pallas_call(
        body, name="norm2_bwd" if gated else "norm1_bwd", grid=(t // TM,),
        in_specs=in_specs, out_specs=out_specs, out_shape=out_shape,
        compiler_params=_cp("arbitrary"),
    )(*args)


TQ = 256
TB = 512
FWD_HEADS = 2


def _mla_fwd(q, k, v):
    t = q.shape[0]
    nb = t // S

    def body(q_ref, k_ref, v_ref, o_ref, lse_ref):
        causal = lax.broadcasted_iota(jnp.int32, (TB, TB), 0) >= lax.broadcasted_iota(jnp.int32, (TB, TB), 1)
        heads = [slice(h * HP, (h + 1) * HP) for h in range(FWD_HEADS)]
        for i in range(S // TB):
            ri, past = slice(i * TB, (i + 1) * TB), slice(0, i * TB)
            qhs = [q_ref[ri, sl] for sl in heads]
            sd = [jnp.where(causal, _dot(qh, k_ref[ri, sl], ((1,), (1,))) * SCALE_B, NEG) for qh, sl in zip(qhs, heads)]
            ms = [jnp.max(s, axis=-1, keepdims=True) for s in sd]
            if i:
                so = [_dot(qh, k_ref[past, sl], ((1,), (1,))) * SCALE_B for qh, sl in zip(qhs, heads)]
                ms = [jnp.maximum(m, jnp.max(s, axis=-1, keepdims=True)) for m, s in zip(ms, so)]
            pd = [jnp.exp(s - m) for s, m in zip(sd, ms)]
            ls = [jnp.sum(p, axis=-1, keepdims=True) for p in pd]
            acc = [_dot(p.astype(BF16), v_ref[ri, sl], ((1,), (0,))) for p, sl in zip(pd, heads)]
            if i:
                po = [jnp.exp(s - m) for s, m in zip(so, ms)]
                ls = [l + jnp.sum(p, axis=-1, keepdims=True) for l, p in zip(ls, po)]
                acc = [a + _dot(p.astype(BF16), v_ref[past, sl], ((1,), (0,))) for a, p, sl in zip(acc, po, heads)]
            for pr in range(FWD_HEADS // 2):
                o_ref[ri, pr * HP:(pr + 1) * HP] = acc[2 * pr] / ls[2 * pr] + acc[2 * pr + 1] / ls[2 * pr + 1]
            for sl, m, l in zip(heads, ms, ls):
                lse_ref[ri, sl] = jnp.broadcast_to(m + jnp.log(l), (TB, HP))

    wide2 = pl.BlockSpec((S, FWD_HEADS * HP), lambda b, p: (b, p))
    return pl.pallas_call(
        body, name="mla_fwd", grid=(nb, H // FWD_HEADS),
        in_specs=[wide2, wide2, wide2],
        out_specs=[pl.BlockSpec((S, FWD_HEADS // 2 * HP), lambda b, p: (b, p)), wide2],
        out_shape=[jax.ShapeDtypeStruct((t, H * VDIM), F32), jax.ShapeDtypeStruct((t, H * HP), F32)],
        compiler_params=_cp("parallel", "parallel"),
    )(q, k, v)


def _mla_bwd(q, k, v, o, do, lse):
    t = q.shape[0]
    nb = t // S

    def body(q_ref, k_ref, v_ref, o_ref, do_ref, lse_ref, dq_out, dk_out, dv_out, dq_ref, dk_ref, dv_ref):
        lane = lax.broadcasted_iota(jnp.int32, (TB, HP), 1)
        causal = lax.broadcasted_iota(jnp.int32, (TB, TB), 0) >= lax.broadcasted_iota(jnp.int32, (TB, TB), 1)
        heads = [slice(h * HP, (h + 1) * HP) for h in range(2)]
        nblk = S // TB
        for i in reversed(range(nblk)):
            ri, past = slice(i * TB, (i + 1) * TB), slice(0, i * TB)
            dov = do_ref[ri, :]
            prod = dov * o_ref[ri, :]
            dob = dov.astype(BF16)
            deltas = [jnp.sum(jnp.where((lane < VDIM) if h == 0 else (lane >= VDIM), prod, 0.0), axis=-1, keepdims=True)
                      for h in range(2)]
            qhs = [q_ref[ri, sl] for sl in heads]
            lses = [lse_ref[ri, sl][:, :1] for sl in heads]
            for rows, diagonal in ((ri, True), (past, False)):
                if rows.stop == rows.start:
                    continue
                ps = [jnp.exp(_dot(qh, k_ref[rows, sl], ((1,), (1,))) * SCALE_B - lse) for qh, sl, lse in zip(qhs, heads, lses)]
                if diagonal:
                    ps = [jnp.where(causal, p, 0.0) for p in ps]
                dps = [_dot(dob, v_ref[rows, sl], ((1,), (1,))) for sl in heads]
                dss = [(p * (dp - delta) * SCALE_B).astype(BF16) for p, dp, delta in zip(ps, dps, deltas)]
                for sl, qh, p, ds in zip(heads, qhs, ps, dss):
                    dq = _dot(ds, k_ref[rows, sl], ((1,), (0,)))
                    dk = _dot(ds, qh, ((0,), (0,)))
                    dv = _dot(p.astype(BF16), dob, ((0,), (0,)))
                    if diagonal:
                        dq_ref[ri, sl] = dq
                    else:
                        dq_ref[ri, sl] += dq
                    if i == nblk - 1:
                        dk_ref[rows, sl] = dk
                        dv_ref[rows, sl] = dv
                    else:
                        dk_ref[rows, sl] += dk
                        dv_ref[rows, sl] += dv
        dq_out[...] = dq_ref[...].astype(BF16)
        dk_out[...] = dk_ref[...].astype(BF16)
        dv_out[...] = dv_ref[...].astype(BF16)

    wide2 = pl.BlockSpec((S, 2 * HP), lambda b, p: (b, p))
    pair = pl.BlockSpec((S, HP), lambda b, p: (b, p))
    return pl.pallas_call(
        body, name="mla_bwd", grid=(nb, H // 2),
        in_specs=[wide2, wide2, wide2, pair, pair, wide2],
        out_specs=[wide2, wide2, wide2],
        out_shape=[jax.ShapeDtypeStruct((t, H * HP), BF16)] * 3,
        scratch_shapes=[pltpu.VMEM((S, 2 * HP), F32)] * 3,
        compiler_params=_cp("parallel", "parallel"),
    )(q, k, v, o, do, lse)


def _t5_bucket(dist):
    max_exact = N_BUCKETS // 2
    d = np.maximum(dist, 1).astype(np.float64)
    large = max_exact + (np.log(d / max_exact) / np.log(MAX_DISTANCE / max_exact) * (N_BUCKETS - max_exact)).astype(np.int64)
    large = np.minimum(large, N_BUCKETS - 1)
    return np.where(dist < max_exact, dist, large).astype(np.int32)


def _band_geometry():
    a = np.arange(BLK)[:, None]
    bk = np.arange(2 * BLK)[None, :]
    steps = BLK + a - bk
    valid = (steps >= 0) & (steps <= BLK)
    buckets = np.stack([_t5_bucket(np.clip(steps, 0, BLK) * d) for d in DILATIONS])
    return buckets, valid


def _band_bias(rel_bias):
    buckets, valid = _band_geometry()
    onehot = (jnp.asarray(buckets)[..., None] == jnp.arange(N_BUCKETS)).astype(F32)
    bias = jnp.einsum("rqkn,nh->rhqk", onehot, rel_bias, precision=lax.Precision.HIGHEST)
    bias = jnp.where(jnp.asarray(valid)[None, None], bias, NEG)
    return bias.reshape(3, H // 2, 2 * BLK, 2 * BLK)


def _dil_items():
    items = []
    for r, d in enumerate(DILATIONS):
        for res in range(d):
            for blk in range(S // d // BLK):
                items.append((r, d, blk * BLK * d + res, blk > 0))
    return items


GROUP = 4


def _strided(start, d):
    return pl.ds(start, BLK) if d == 1 else pl.ds(start, BLK, stride=d)


def _stack_heads(tile, own):
    return jnp.where(own, jnp.concatenate([tile, tile], axis=0), 0.0).astype(BF16)


def _own_lanes():
    row = lax.broadcasted_iota(jnp.int32, (2 * BLK, HP), 0)
    lane = lax.broadcasted_iota(jnp.int32, (2 * BLK, HP), 1)
    return (lane < E_A) == (row < BLK)


def _dil_fwd(proj, biasm):
    t = proj.shape[0]
    nb = t // S

    def body(q_ref, k_ref, v_ref, b_ref, o_ref, lse_ref, ob_ref, lb_ref):
        lane = lax.broadcasted_iota(jnp.int32, (BLK, HP), 1)
        own = _own_lanes()
        items = _dil_items()
        for g in range(0, len(items), GROUP):
            grp = items[g:g + GROUP]
            ss, vts = [], []
            for r, d, start, has_prev in grp:
                cur = _strided(start, d)
                rows = [_strided(start - BLK * d, d), cur] if has_prev else [cur]
                q2 = _stack_heads(q_ref[cur, :] * SCALE_A, own)
                kt = jnp.concatenate([k_ref[x, :] for x in rows], axis=0).astype(BF16)
                vts.append(jnp.concatenate([v_ref[x, :] for x in rows], axis=0).astype(BF16))
                bias = b_ref[r, 0] if has_prev else b_ref[r, 0, :, BLK:]
                ss.append(_dot(q2, kt, ((1,), (1,))) + bias)
            ms = [jnp.max(s, axis=-1, keepdims=True) for s in ss]
            ps = [jnp.exp(s - m) for s, m in zip(ss, ms)]
            ls = [jnp.sum(p, axis=-1, keepdims=True) for p in ps]
            for (r, d, start, _), p, vt, m, l in zip(grp, ps, vts, ms, ls):
                cur = _strided(start, d)
                o2 = _dot(p.astype(BF16), vt, ((1,), (0,))) / l
                lse2 = m + jnp.log(l)
                ob_ref[r, cur, :] = jnp.where(lane < E_A, o2[:BLK], o2[BLK:])
                lb_ref[r, cur, :] = jnp.where(lane < E_A, lse2[:BLK], lse2[BLK:])

        def merge(c, _):
            rows = pl.ds(pl.multiple_of(c * TQ, TQ), TQ)
            l0, l1, l2 = lb_ref[0, rows, :], lb_ref[1, rows, :], lb_ref[2, rows, :]
            m = jnp.maximum(jnp.maximum(l0, l1), l2)
            e0, e1, e2 = jnp.exp(l0 - m), jnp.exp(l1 - m), jnp.exp(l2 - m)
            tot = e0 + e1 + e2
            o_ref[rows, :] = (e0 * ob_ref[0, rows, :] + e1 * ob_ref[1, rows, :] + e2 * ob_ref[2, rows, :]) / tot
            lse_ref[rows, :] = m + jnp.log(tot)
            return 0

        lax.fori_loop(0, S // TQ, merge, 0)

    npair = H // 2
    return pl.pallas_call(
        body, name="dil_fwd", grid=(nb, npair),
        in_specs=[pl.BlockSpec((S, HP), lambda b, p: (b, p)), pl.BlockSpec((S, HP), lambda b, p: (b, npair + p)),
                  pl.BlockSpec((S, HP), lambda b, p: (b, 2 * npair + p)),
                  pl.BlockSpec((3, 1, 2 * BLK, 2 * BLK), lambda b, p: (0, p, 0, 0))],
        out_specs=[pl.BlockSpec((S, HP), lambda b, p: (b, p))] * 2,
        out_shape=[jax.ShapeDtypeStruct((t, D_A), F32)] * 2,
        scratch_shapes=[pltpu.VMEM((3, S, HP), F32), pltpu.VMEM((3, S, HP), F32)],
        compiler_params=_cp("parallel", "parallel"),
    )(proj, proj, proj, biasm)


def _dil_bwd(proj, biasm, o, do, lse):
    t = proj.shape[0]
    nb = t // S

    def body(q_ref, k_ref, v_ref, b_ref, o_ref, do_ref, lse_ref, dq_out, dk_out, dv_out, ds_ref, dq_ref, dk_ref, dv_ref):
        dq_ref[...] = jnp.zeros_like(dq_ref)
        dk_ref[...] = jnp.zeros_like(dk_ref)
        dv_ref[...] = jnp.zeros_like(dv_ref)
        ds_ref[...] = jnp.zeros_like(ds_ref)
        lane = lax.broadcasted_iota(jnp.int32, (BLK, HP), 1)
        own = _own_lanes()
        items = _dil_items()
        for g in range(0, len(items), GROUP):
            grp = items[g:g + GROUP]
            q2s, kts, do2s, ss, dps, lse2s, delta2s = [], [], [], [], [], [], []
            for r, d, start, has_prev in grp:
                cur = _strided(start, d)
                rows = [_strided(start - BLK * d, d), cur] if has_prev else [cur]
                q2 = _stack_heads(q_ref[cur, :] * SCALE_A, own)
                kt = jnp.concatenate([k_ref[x, :] for x in rows], axis=0).astype(BF16)
                vt = jnp.concatenate([v_ref[x, :] for x in rows], axis=0).astype(BF16)
                dot_ = do_ref[cur, :]
                prod = dot_ * o_ref[cur, :]
                lset = lse_ref[cur, :]
                do2 = _stack_heads(dot_, own)
                bias = b_ref[r, 0] if has_prev else b_ref[r, 0, :, BLK:]
                ss.append(_dot(q2, kt, ((1,), (1,))) + bias)
                dps.append(_dot(do2, vt, ((1,), (1,))))
                lse2s.append(jnp.concatenate([lset[:, :1], lset[:, E_A:E_A + 1]], axis=0))
                delta2s.append(jnp.concatenate([jnp.sum(jnp.where(lane < E_A, prod, 0.0), axis=-1, keepdims=True),
                                                jnp.sum(jnp.where(lane >= E_A, prod, 0.0), axis=-1, keepdims=True)], axis=0))
                q2s.append(q2)
                kts.append(kt)
                do2s.append(do2)
            ps = [jnp.exp(s - lse2) for s, lse2 in zip(ss, lse2s)]
            dls = [p * (dp - delta2) for p, dp, delta2 in zip(ps, dps, delta2s)]
            for (r, d, start, has_prev), q2, kt, do2, p, dl in zip(grp, q2s, kts, do2s, ps, dls):
                cur = _strided(start, d)
                dsb = dl.astype(BF16)
                dq2 = _dot(dsb, kt, ((1,), (0,))) * SCALE_A
                dkt = _dot(dsb, q2, ((0,), (0,)))
                dvt = _dot(p.astype(BF16), do2, ((0,), (0,)))
                dq_ref[cur, :] += jnp.where(lane < E_A, dq2[:BLK], dq2[BLK:])
                if has_prev:
                    prev = _strided(start - BLK * d, d)
                    ds_ref[0, r, 0] += dl
                    dk_ref[prev, :] += dkt[:BLK]
                    dv_ref[prev, :] += dvt[:BLK]
                    dk_ref[cur, :] += dkt[BLK:]
                    dv_ref[cur, :] += dvt[BLK:]
                else:
                    ds_ref[0, r, 0, :, BLK:] += dl
                    dk_ref[cur, :] += dkt
                    dv_ref[cur, :] += dvt
        dq_out[...] = dq_ref[...].astype(BF16)
        dk_out[...] = dk_ref[...].astype(BF16)
        dv_out[...] = dv_ref[...].astype(BF16)

    npair = H // 2
    pair = pl.BlockSpec((S, HP), lambda b, p: (b, p))
    return pl.pallas_call(
        body, name="dil_bwd", grid=(nb, npair),
        in_specs=[pair, pl.BlockSpec((S, HP), lambda b, p: (b, npair + p)),
                  pl.BlockSpec((S, HP), lambda b, p: (b, 2 * npair + p)),
                  pl.BlockSpec((3, 1, 2 * BLK, 2 * BLK), lambda b, p: (0, p, 0, 0)), pair, pair, pair],
        out_specs=[pair, pair, pair, pl.BlockSpec((1, 3, 1, 2 * BLK, 2 * BLK), lambda b, p: (b, 0, p, 0, 0))],
        out_shape=[jax.ShapeDtypeStruct((t, D_A), BF16)] * 3 + [jax.ShapeDtypeStruct((nb, 3, npair, 2 * BLK, 2 * BLK), F32)],
        scratch_shapes=[pltpu.VMEM((S, HP), F32)] * 3,
        compiler_params=_cp("parallel", "parallel"),
    )(proj, proj, proj, biasm, o, do, lse)


def _rel_bias_grad(dlogits):
    nb = dlogits.shape[0]
    buckets, _ = _band_geometry()
    kk = 3 * BLK * 2 * BLK
    dl = jnp.transpose(dlogits.reshape(nb, 3, H, BLK, 2 * BLK), (0, 2, 1, 3, 4)).reshape(nb, H, kk)
    bk = jnp.asarray(buckets.reshape(1, kk))
    tk = kk // 12

    def body(dl_ref, bk_ref, o_ref):
        j = pl.program_id(0)
        onehot = (bk_ref[...] == lax.broadcasted_iota(jnp.int32, (N_BUCKETS, tk), 0)).astype(F32)
        tot = dl_ref[0]
        for b in range(1, nb):
            tot = tot + dl_ref[b]
        part = lax.dot_general(onehot, tot, ((((1,), (1,))), ((), ())), preferred_element_type=F32,
                               precision=lax.Precision.HIGHEST)
        _acc_first(j, o_ref, part)

    return pl.pallas_call(
        body, name="rel_bias_grad", grid=(kk // tk,),
        in_specs=[pl.BlockSpec((nb, H, tk), lambda j: (0, 0, j)), pl.BlockSpec((1, tk), lambda j: (0, j))],
        out_specs=pl.BlockSpec((N_BUCKETS, H), lambda j: (0, 0)),
        out_shape=jax.ShapeDtypeStruct((N_BUCKETS, H), F32),
        compiler_params=_cp("arbitrary"),
    )(dl, bk)


def _mesh_place():
    x, y, c = lax.axis_index("x"), lax.axis_index("y"), lax.axis_index("c")
    return x, y, c


def _peer(k):
    x, y, c = _mesh_place()
    px = 1 - x if k & 4 else x
    py = 1 - y if k & 2 else y
    pc = 1 - c if k & 1 else c
    return (px, py, pc), 4 * px + 2 * py + pc


ANY = pl.BlockSpec(memory_space=pl.ANY)


def _exchange(arrays, gathers, name, after=None):
    n_arr = len(arrays)

    def body(*refs):
        ins, outs = refs[:n_arr], refs[n_arr + 1:2 * n_arr + 1]
        send, recv, loc = refs[2 * n_arr + 1:]
        x, y, c = _mesh_place()
        me = 4 * x + 2 * y + c
        local = [pltpu.make_async_copy(ins[a] if gathers[a] else ins[a].at[me], outs[a].at[me], loc.at[a])
                 for a in range(n_arr)]
        remote = _peer_copies(ins, outs, send, recv, gathers)
        for cp in local:
            cp.start()
        for put, _ in remote:
            put.start()
        for cp in local:
            cp.wait()
        for put, got in remote:
            put.wait_send()
            got.wait_recv()

    return pl.pallas_call(
        body, name=name,
        in_specs=[ANY] * (n_arr + 1), out_specs=[ANY] * n_arr,
        out_shape=[jax.ShapeDtypeStruct(((N_DEV,) if g else ()) + a.shape, a.dtype) for a, g in zip(arrays, gathers)],
        scratch_shapes=[pltpu.SemaphoreType.DMA((n_arr * (N_DEV - 1),)), pltpu.SemaphoreType.DMA((n_arr * (N_DEV - 1),)),
                        pltpu.SemaphoreType.DMA((n_arr,))],
        compiler_params=pltpu.CompilerParams(has_side_effects=True),
    )(*arrays, arrays[0] if after is None else after)


def _gather_two_level(arrays, name):
    n_arr = len(arrays)
    per = N_DEV - 1

    def body(*refs):
        ins, outs = refs[:n_arr], refs[n_arr:2 * n_arr]
        send, recv, loc = refs[2 * n_arr:]
        x, y, c = _mesh_place()
        me, sibling = (x, y, c), (x, y, 1 - c)
        chips = [(1 - x, y), (x, 1 - y), (1 - x, 1 - y)]

        def block(a, place):
            px, py, pc = place
            return outs[a].at[4 * px + 2 * py + pc]

        def copy(a, k, place, to, src=None):
            dst = block(a, place)
            return pltpu.make_async_remote_copy(dst if src is None else src, dst, send.at[a * per + k], recv.at[a * per + k],
                                                device_id=to, device_id_type=pl.DeviceIdType.MESH)

        local = [pltpu.make_async_copy(ins[a], block(a, me), loc.at[a]) for a in range(n_arr)]
        for cp in local:
            cp.start()
        first = []
        for a in range(n_arr):
            first.append(copy(a, 0, me, sibling, src=ins[a]))
            first += [copy(a, 1 + j, me, (*chip, c), src=ins[a]) for j, chip in enumerate(chips)]
        for cp in first:
            cp.start()
        passed = []
        for j, chip in enumerate(chips):
            for a in range(n_arr):
                copy(a, 1 + j, (*chip, c), me).wait_recv()
                passed.append(copy(a, 4 + j, (*chip, c), sibling))
                passed[-1].start()
        for a in range(n_arr):
            copy(a, 0, sibling, me).wait_recv()
            for j, chip in enumerate(chips):
                copy(a, 4 + j, (*chip, 1 - c), me).wait_recv()
        for cp in first + passed:
            cp.wait_send()
        for cp in local:
            cp.wait()

    return pl.pallas_call(
        body, name=name,
        in_specs=[ANY] * n_arr, out_specs=[ANY] * n_arr,
        out_shape=[jax.ShapeDtypeStruct((N_DEV,) + a.shape, a.dtype) for a in arrays],
        scratch_shapes=[pltpu.SemaphoreType.DMA((n_arr * per,)), pltpu.SemaphoreType.DMA((n_arr * per,)),
                        pltpu.SemaphoreType.DMA((n_arr,))],
        compiler_params=pltpu.CompilerParams(has_side_effects=True),
    )(*arrays)


HBM = pl.BlockSpec(memory_space=pltpu.HBM)
SEM = pl.BlockSpec(memory_space=pltpu.SEMAPHORE)
DATAFLOW = pltpu.SideEffectType.DATAFLOW_SIDE_EFFECTING


def _own_block_in_place(block, me):
    land = lax.empty((N_DEV,) + block.shape, block.dtype)
    return lax.dynamic_update_slice(land, block[None], (me,) + (0,) * block.ndim)


def _peer_copies(srcs, lands, send, recv, gathers):
    x, y, c = _mesh_place()
    me = 4 * x + 2 * y + c
    out = []
    for a, (src, land) in enumerate(zip(srcs, lands)):
        for k in range(1, N_DEV):
            dev, idx = _peer(k)
            sem = a * (N_DEV - 1) + k - 1
            mine = src if gathers[a] else src.at[idx]
            put = pltpu.make_async_remote_copy(mine, land.at[me], send.at[sem], recv.at[sem],
                                               device_id=dev, device_id_type=pl.DeviceIdType.MESH)
            got = pltpu.make_async_remote_copy(mine, land.at[idx], send.at[sem], recv.at[sem],
                                               device_id=dev, device_id_type=pl.DeviceIdType.MESH)
            out.append((put, got))
    return out


def _exchange_start(srcs, lands, gather, after, name):
    n = len(srcs)
    extra = [] if after is None else [after]

    def body(*refs):
        srcs_, lands_ = refs[:n], refs[n:2 * n]
        send, recv = refs[2 * n + len(extra)], refs[2 * n + len(extra) + 1]
        for put, _ in _peer_copies(srcs_, lands_, send, recv, gather):
            put.start()
        refs[-1][...] = jnp.zeros_like(refs[-1])

    nsem = n * (N_DEV - 1)
    thru = [pltpu.HBM(a.shape, a.dtype) for a in list(srcs) + list(lands)]
    res = pl.pallas_call(
        body, name=name,
        out_shape=(pltpu.SemaphoreType.DMA((nsem,)), pltpu.SemaphoreType.DMA((nsem,)), *thru, jax.ShapeDtypeStruct((8, 128), F32)),
        in_specs=[HBM] * (2 * n) + [ANY] * len(extra),
        out_specs=(SEM, SEM, *([HBM] * (2 * n)), pl.BlockSpec(memory_space=pltpu.VMEM)),
        input_output_aliases={i: 2 + i for i in range(2 * n)},
        compiler_params=pltpu.CompilerParams(has_side_effects=DATAFLOW),
    )(*[pltpu.with_memory_space_constraint(a, pltpu.HBM) for a in list(srcs) + list(lands)], *extra)
    return res[0], res[1], list(res[2:2 + n]), list(res[2 + n:2 + 2 * n]), res[-1]


def _exchange_wait(send, recv, srcs, lands, gather, after, name):
    n = len(srcs)

    def body(*refs):
        srcs_, lands_, send_, recv_ = refs[:n], refs[n:2 * n], refs[2 * n], refs[2 * n + 1]
        for put, got in _peer_copies(srcs_, lands_, send_, recv_, gather):
            put.wait_send()
            got.wait_recv()

    thru = [pltpu.HBM(a.shape, a.dtype) for a in list(srcs) + list(lands)]
    res = pl.pallas_call(
        body, name=name, out_shape=tuple(thru),
        in_specs=[HBM] * (2 * n) + [SEM, SEM, ANY], out_specs=tuple([HBM] * (2 * n)),
        input_output_aliases={i: i for i in range(2 * n)},
        compiler_params=pltpu.CompilerParams(has_side_effects=DATAFLOW),
    )(*srcs, *lands, send, recv, after)
    return list(res[n:])


def _silu_rows(c):
    def body(c_ref, o_ref):
        v = c_ref[...]
        o_ref[...] = v * _sigmoid(v)

    return pl.pallas_call(body, name="cond", out_shape=jax.ShapeDtypeStruct(c.shape, F32))(c)


def _mod_slab(cond_all, w_ada, b_slab):
    def body(c_ref, w_ref, b_ref, o_ref):
        o_ref[...] = _dot(c_ref[...].astype(BF16), w_ref[0].astype(BF16), ((1,), (0,))) + b_ref[...]

    return pl.pallas_call(body, name="mod_slab",
                          out_shape=jax.ShapeDtypeStruct((cond_all.shape[0], w_ada.shape[2]), F32),
                          compiler_params=pltpu.CompilerParams(vmem_limit_bytes=VMEM_LIMIT))(cond_all, w_ada, b_slab)


def _ada_grad(cond_all, dmod_cols):
    def body(c_ref, d_ref, o_ref):
        o_ref[...] = _dot(c_ref[...].astype(BF16), d_ref[...].astype(BF16), ((0,), (0,)))

    return pl.pallas_call(body, name="ada_grad",
                          out_shape=jax.ShapeDtypeStruct((cond_all.shape[1], dmod_cols.shape[1]), F32),
                          compiler_params=pltpu.CompilerParams(vmem_limit_bytes=VMEM_LIMIT))(cond_all, dmod_cols)


def _adam_math(g, w, m, v):
    m2 = B1 * m + (1.0 - B1) * g
    v2 = B2 * v + (1.0 - B2) * (g * g)
    m_hat = m2 / (1.0 - B1 ** STEP)
    v_hat = v2 / (1.0 - B2 ** STEP)
    return -LR * (m_hat / (jnp.sqrt(v_hat) + ADAM_EPS) + WD * w), m2, v2


def _adamw(parts, w, m, v, name):
    n, rows, cols = parts.shape
    tr = max([p for p in range(16, 513, 16) if rows % p == 0] or [rows])

    def body(p_ref, w_ref, m_ref, v_ref, g_ref, d_ref, m2_ref, v2_ref):
        g = p_ref[0].astype(F32)
        for s in range(1, n):
            g = g + p_ref[s].astype(F32)
        g_ref[0] = g
        d_ref[0], m2_ref[0], v2_ref[0] = _adam_math(g, w_ref[0], m_ref[0], v_ref[0])

    blk = pl.BlockSpec((1, tr, cols), lambda i: (0, i, 0))
    return pl.pallas_call(
        body, name=name, grid=(rows // tr,),
        in_specs=[pl.BlockSpec((n, tr, cols), lambda i: (0, i, 0)), blk, blk, blk],
        out_specs=[blk] * 4, out_shape=[jax.ShapeDtypeStruct((1, rows, cols), F32)] * 4,
        compiler_params=_cp("parallel"),
    )(parts, w, m, v)


ROW_PARAMS = (("g_norm1", D), ("g_cq", Q_LORA), ("g_ckv", KV_LORA), ("g_out_a", D_A), ("g_out_b", D_A), ("g_norm2", D),
              ("g_final", D))
LOSS_ROW = N_MOD + len(ROW_PARAMS)
PAY_ROWS = 16
NCOL = N_MOD * D // N_DEV


def _pack_small(dmods, rows, loss_cols):
    nb = dmods[0].shape[0]
    nrow = len(ROW_PARAMS)

    def body(*refs):
        dm, rw, loss_ref, pay_ref, blk_ref = refs[:N_MOD], refs[N_MOD:N_MOD + nrow], refs[N_MOD + nrow], refs[-2], refs[-1]
        pay_ref[...] = jnp.zeros_like(pay_ref)
        for k in range(N_MOD):
            tot = dm[k][0]
            for b in range(1, nb):
                tot = tot + dm[k][b]
            pay_ref[k:k + 1, :] = tot
        for i, (_, n) in enumerate(ROW_PARAMS):
            pay_ref[N_MOD + i:N_MOD + i + 1, :n] = rw[i][...]
        pay_ref[LOSS_ROW:LOSS_ROW + 1, :] = loss_ref[...]
        for j in range(N_DEV):
            done = 0
            while done < NCOL:
                seg, off = divmod(j * NCOL + done, D)
                ln = min(NCOL - done, D - off)
                for b in range(nb):
                    blk_ref[j, b:b + 1, done:done + ln] = dm[seg][b][:, off:off + ln]
                done += ln

    return pl.pallas_call(
        body, name="pack_small",
        out_shape=[jax.ShapeDtypeStruct((PAY_ROWS, D), F32), jax.ShapeDtypeStruct((N_DEV, nb, NCOL), F32)],
    )(*dmods, *rows, loss_cols)


def _small_update(pay, rel, ws, ms, vs):
    n_par = len(ws)

    def body(*refs):
        pay_ref, rel_ref = refs[:2]
        w_refs, m_refs, v_refs = (refs[2 + s * n_par:2 + (s + 1) * n_par] for s in range(3))
        outs, loss_ref = refs[2 + 3 * n_par:-1], refs[-1]
        tot, rtot = pay_ref[0], rel_ref[0]
        for s in range(1, N_DEV):
            tot, rtot = tot + pay_ref[s], rtot + rel_ref[s]

        def update(p, g, sl):
            outs[4 * p][:, sl] = g
            outs[4 * p + 1][:, sl], outs[4 * p + 2][:, sl], outs[4 * p + 3][:, sl] = _adam_math(
                g, w_refs[p][:, sl], m_refs[p][:, sl], v_refs[p][:, sl])

        for k in range(N_MOD):
            update(0, tot[k:k + 1, :], slice(k * D, (k + 1) * D))
        for i, (_, n) in enumerate(ROW_PARAMS):
            update(1 + i, tot[N_MOD + i:N_MOD + i + 1, :n], slice(0, n))
        update(n_par - 1, rtot, slice(0, H))
        loss_ref[...] = jnp.broadcast_to((0.5 / D) * jnp.sum(tot[LOSS_ROW:LOSS_ROW + 1, :]), loss_ref.shape)

    shapes = [jax.ShapeDtypeStruct(w.shape, F32) for w in ws for _ in range(4)]
    res = pl.pallas_call(
        body, name="small_update", out_shape=shapes + [jax.ShapeDtypeStruct((8, 128), F32)],
    )(pay, rel, *ws, *ms, *vs)
    return [tuple(res[4 * p:4 * p + 4]) for p in range(n_par)], res[-1]


def _cols_from_blocks(g):
    return jnp.transpose(g, (1, 0, 2)).reshape(g.shape[1], N_DEV * g.shape[2])


def _cols_to_blocks(w):
    r, c = w.shape
    return jnp.transpose(w.reshape(r, N_DEV, c // N_DEV), (1, 0, 2))


def _pad_w_in(wt):
    z = jnp.zeros((NOPE, wt.shape[1]), wt.dtype)
    return jnp.concatenate([wt[:P_IN - ROPE], z, wt[P_IN - ROPE:], z[:HP - NOPE - ROPE]], axis=0)


def _unpad_w_in(gt):
    k0 = P_IN - ROPE + NOPE
    return jnp.concatenate([gt[:P_IN - ROPE], gt[k0:k0 + ROPE]], axis=0)


def _pad_w_uq(wt):
    return jnp.pad(wt, ((0, 0), (0, HP - NOPE - ROPE), (0, 0))).reshape(H * HP, Q_LORA)


def _unpad_w_uq(gt):
    return gt.reshape(H, HP, Q_LORA)[:, :NOPE + ROPE]


def _split_w_ukv(w):
    w4 = w.reshape(KV_LORA, H // 2, 2, HP)
    z = jnp.zeros((KV_LORA, H // 2, NOPE), w.dtype)
    kn, vv = w4[..., :NOPE], w4[..., NOPE:]
    w_k = jnp.stack([jnp.concatenate([kn[:, :, 0], z], -1), jnp.concatenate([kn[:, :, 1], z], -1)], axis=2)
    w_v = jnp.stack([jnp.concatenate([vv[:, :, 0], z], -1), jnp.concatenate([z, vv[:, :, 1]], -1)], axis=2)
    return w_k.reshape(KV_LORA, H * HP), w_v.reshape(KV_LORA, H * HP)


def _join_w_ukv(g_k, g_v):
    gk = g_k.reshape(KV_LORA, H // 2, 2, HP)
    gv = g_v.reshape(KV_LORA, H // 2, 2, HP)
    even = jnp.concatenate([gk[:, :, 0, :NOPE], gv[:, :, 0, :VDIM]], -1)
    odd = jnp.concatenate([gk[:, :, 1, :NOPE], gv[:, :, 1, VDIM:]], -1)
    return jnp.stack([even, odd], axis=2).reshape(KV_LORA, H * HP)


def _rope_tables():
    half = ROPE // 2
    inv = np.float32(ROPE_THETA) ** (-np.arange(half, dtype=np.float32) / np.float32(half))
    ang = np.arange(S, dtype=np.float32)[:, None] * inv[None, :].astype(np.float32)
    cos, sin = np.cos(ang).astype(np.float32), np.sin(ang).astype(np.float32)
    ones, zeros = np.ones((S, NOPE), np.float32), np.zeros((S, NOPE), np.float32)
    tail1, tail0 = np.ones((S, HP - NOPE - ROPE), np.float32), np.zeros((S, HP - NOPE - ROPE), np.float32)
    zh = np.zeros((S, half), np.float32)
    c = np.concatenate([ones, cos, cos, tail1], axis=1)
    sm = np.concatenate([zeros, -sin, zh, tail0], axis=1)
    sp = np.concatenate([zeros, zh, sin, tail0], axis=1)
    return jnp.asarray(c), jnp.asarray(sm), jnp.asarray(sp)


def _local_step(x, mod, target, g_norm1, w_in_p, g_cq, w_uq_p, g_ckv, w_k, w_v, rel_bias, g_out_a, g_out_b, w_out,
                g_norm2, w_ffn_in, w_ffn_out, g_final, late_weights=None, on_ffn_grads=None, on_last_grads=None):
    nb = x.shape[0] // S
    sh1, sc1, g1, sh2, sc2, g2 = (mod[:, n].reshape(nb, 1, D) for n in range(N_MOD))
    rc, rsm, rsp = _rope_tables()
    biasm = _band_bias(rel_bias)

    h1 = _pre1(x, g_norm1, sc1, sh1)
    proj = _mm_nt(h1, w_in_p, F32, "proj")
    q, k, v, cqn, ckvn = _mla_pre(proj, g_cq, g_ckv, w_uq_p, w_k, w_v, rc, rsm, rsp)
    out_b, lse_b = _mla_fwd(q, k, v)
    out_a, lse_a = _dil_fwd(proj, biasm)
    y = _post_attn(out_a, out_b, g_out_a, g_out_b)
    if late_weights is not None:
        w_out, w_ffn_in, w_ffn_out = late_weights(y)
    mix = _mm_nn(y, w_out, BF16, "mix")
    x2, h2 = _resid_norm2(x, mix, g1, g_norm2, sc2, sh2)
    ffn_u, ffn_silu, ffn_dsilu, act = _ffn_in(h2, w_ffn_in)
    f = _mm_nn(act, w_ffn_out, BF16, "ffn_out")
    dx3, df, loss_cols, dg_final, dg2 = _final(x2, f, g2, g_final, target)

    dg_, du_ = _d_act(df, w_ffn_out, ffn_u, ffn_silu, ffn_dsilu)
    gw_ffn_out = _mm_tn_rows([act], df, "gw_ffn_out")
    dh2 = _d_h2(dg_, du_, w_ffn_in)
    gw_ffn_in = _mm_tn_rows([dg_, du_], h2, "gw_ffn_in")
    dx2, dsh2, dsc2, dg_norm2, dg1, dmix = _norm_bwd(x2, dh2, dx3, g_norm2, sc2, gate=(mix, g1))
    dy = _mm_nt(dmix, w_out, BF16, "d_y")
    gw_out = _mm_tn(y, [dmix], "gw_out")
    if on_ffn_grads is not None:
        g_out_a = g_out_a + on_ffn_grads(gw_ffn_in, gw_ffn_out, gw_out)
    dout_a, dout_b, dg_out_a, dg_out_b = _post_attn_bwd(dy, out_a, out_b, g_out_a, g_out_b)
    dq_b, dk_b, dv_b = _mla_bwd(q, k, v, out_b, dout_b, lse_b)
    dq_a, dk_a, dv_a, dlogits = _dil_bwd(proj, biasm, out_a, dout_a, lse_a)
    g_rel = _rel_bias_grad(dlogits)
    dqr, dproj, dg_cq, dg_ckv = _mla_pre_bwd(proj, dq_b, dk_b, dv_b, (dq_a, dk_a, dv_a), g_cq, g_ckv, w_uq_p, w_k, w_v,
                                             rc, rsm, rsp)
    gw_uq = _mm_tn(dqr, [cqn], "gw_uq")
    gw_k, gw_v = _mm_tn(ckvn, [dk_b, dv_b], "gw_kv")
    gw_in = _mm_tn_rows([dproj], h1, "gw_in")
    if on_last_grads is not None:
        started = on_last_grads(dict(w_in=gw_in, w_uq=gw_uq, w_k=gw_k, w_v=gw_v))
    else:
        started = None
    dh1 = _mm_nn(dproj, w_in_p, BF16, "d_h1", after=started)
    grad_x, dsh1, dsc1, dg_norm1 = _norm_bwd(x, dh1, dx2, g_norm1, sc1)

    dmod = [dsh1, dsc1, dg1, dsh2, dsc2, dg2]
    small = dict(g_norm1=dg_norm1, g_cq=dg_cq, g_ckv=dg_ckv, rel_bias=g_rel, g_out_a=dg_out_a, g_out_b=dg_out_b,
                 g_norm2=dg_norm2, g_final=dg_final)
    big = dict(w_in=gw_in, w_uq=gw_uq, w_k=gw_k, w_v=gw_v, w_out=gw_out, w_ffn_in=gw_ffn_in, w_ffn_out=gw_ffn_out)
    return grad_x, dmod, loss_cols, small, big


def kernel(x, c, w_ada, b_ada, g_norm1, w_in, g_cq, w_uq, g_ckv, w_ukv, rel_bias, g_out_a, g_out_b, w_out, g_norm2, w_ffn_in, w_ffn_out, g_final, loss_target, m_w_ada, m_b_ada, m_g_norm1, m_w_in, m_g_cq, m_w_uq, m_g_ckv, m_w_ukv, m_rel_bias, m_g_out_a, m_g_out_b, m_w_out, m_g_norm2, m_w_ffn_in, m_w_ffn_out, m_g_final, v_w_ada, v_b_ada, v_g_norm1, v_w_in, v_g_cq, v_w_uq, v_g_ckv, v_w_ukv, v_rel_bias, v_g_out_a, v_g_out_b, v_w_out, v_g_norm2, v_w_ffn_in, v_w_ffn_out, v_g_final):
    nb = x.shape[0]
    t = nb * S
    xt, tt = x.reshape(t, D), loss_target.reshape(t, D)
    me = 4 * lax.axis_index("x") + 2 * lax.axis_index("y") + lax.axis_index("c")

    early = [jnp.swapaxes(w_in, 1, 2)[0], jnp.swapaxes(w_uq, 1, 2)[0], w_ukv[0]]
    gathered = _gather_two_level([_silu_rows(c)] + [s.astype(BF16) for s in early], "gather_weights")
    cond_all = gathered[0].reshape(N_DEV * nb, D)
    w_in_t = gathered[1].reshape(P_IN, D)
    w_ukv_f = _cols_from_blocks(gathered[3])
    w_k, w_v = _split_w_ukv(w_ukv_f)

    ncol = N_MOD * D // N_DEV
    b_slab = lax.dynamic_slice(b_ada, (0, me * ncol), (1, ncol))
    slab = _mod_slab(cond_all, w_ada, b_slab)
    (mod_rows,) = _exchange([slab.reshape(N_DEV, nb, ncol)], [False], "scatter_mod")
    mod = jnp.transpose(mod_rows, (1, 0, 2)).reshape(nb, N_MOD, D)

    late = [s.astype(BF16) for s in (w_out[0], jnp.swapaxes(w_ffn_in, 1, 2)[0], w_ffn_out[0])]
    late_send, late_recv, late_src, late_land, late_token = _exchange_start(
        late, [_own_block_in_place(s, me) for s in late], [True] * 3, mod_rows, "gather_late_start")
    g_norm1_t = g_norm1 + late_token[:1, :1]

    def late_weights(after):
        w_out_g, w_ffn_in_g, w_ffn_out_g = _exchange_wait(late_send, late_recv, late_src, late_land, [True] * 3, after,
                                                          "gather_late_wait")
        return w_out_g.reshape(D, D), w_ffn_in_g.reshape(2 * D_FF, D), w_ffn_out_g.reshape(D_FF, D)

    flight = {}

    def start_grads(key, src, name):
        land = [_own_block_in_place(lax.dynamic_index_in_dim(s, me, 0, keepdims=False), me) for s in src]
        send, recv, src, land, token = _exchange_start(src, land, [False] * len(src), None, name)
        flight[key] = (send, recv, src, land)
        return token[:1, :1]

    def on_ffn_grads(gw_ffn_in, gw_ffn_out, gw_out):
        return start_grads("ffn", [gw_ffn_in.reshape(N_DEV, 2 * D_FF // N_DEV, D), gw_ffn_out.reshape(N_DEV, D_FF // N_DEV, D),
                                   gw_out.reshape(N_DEV, D // N_DEV, D)], "exchange_ffn_start")

    def on_last_grads(gw):
        return start_grads("rest", [_unpad_w_in(gw["w_in"]).reshape(N_DEV, P_IN // N_DEV, D),
                                    _unpad_w_uq(gw["w_uq"]),
                                    _cols_to_blocks(_join_w_ukv(gw["w_k"], gw["w_v"]))], "exchange_rest_start")

    grad_x, dmod, loss_cols, small, _ = _local_step(
        xt, mod, tt, g_norm1_t, _pad_w_in(w_in_t), g_cq, _pad_w_uq(gathered[2]), g_ckv, w_k, w_v, rel_bias, g_out_a, g_out_b,
        None, g_norm2, None, None, g_final.reshape(1, D), late_weights=late_weights, on_ffn_grads=on_ffn_grads,
        on_last_grads=on_last_grads)

    upd = {}

    def land_and_update(key, names, after, name):
        got = _exchange_wait(*flight[key], [False] * len(names), after, name)
        for n, p in zip(names, got):
            w, m, v = big[n]
            upd[n] = _adamw(p, w, m, v, "adamw_" + n)

    def flip(a):
        return jnp.swapaxes(a, 1, 2)

    big = dict(w_in=(flip(w_in), flip(m_w_in), flip(v_w_in)), w_uq=(flip(w_uq), flip(m_w_uq), flip(v_w_uq)),
               w_ukv=(w_ukv, m_w_ukv, v_w_ukv),
               w_out=(w_out, m_w_out, v_w_out), w_ffn_in=(flip(w_ffn_in), flip(m_w_ffn_in), flip(v_w_ffn_in)),
               w_ffn_out=(w_ffn_out, m_w_ffn_out, v_w_ffn_out))
    land_and_update("ffn", ["w_ffn_in", "w_ffn_out", "w_out"], grad_x, "exchange_ffn_wait")
    land_and_update("rest", ["w_in", "w_uq", "w_ukv"], upd["w_out"][0], "exchange_rest_wait")
    for n in ("w_in", "w_uq", "w_ffn_in"):
        upd[n] = tuple(flip(a) for a in upd[n])

    mine, dmod_blocks = _pack_small(dmod, [small[n] for n, _ in ROW_PARAMS], loss_cols)
    dmod_cols, pay, rel = _exchange([dmod_blocks, mine, small["rel_bias"]], [False, True, True], "exchange_small",
                                    after=upd["w_ukv"][0])
    g_ada = _ada_grad(cond_all, dmod_cols.reshape(N_DEV * nb, ncol))
    upd["w_ada"] = _adamw(g_ada[None], w_ada, m_w_ada, v_w_ada, "adamw_w_ada")
    row = lambda a: a.reshape(1, D)
    small_names = ["b_ada"] + [n for n, _ in ROW_PARAMS] + ["rel_bias"]
    small_w = [b_ada, g_norm1, g_cq, g_ckv, g_out_a, g_out_b, g_norm2, row(g_final), rel_bias]
    small_m = [m_b_ada, m_g_norm1, m_g_cq, m_g_ckv, m_g_out_a, m_g_out_b, m_g_norm2, row(m_g_final), m_rel_bias]
    small_v = [v_b_ada, v_g_norm1, v_g_cq, v_g_ckv, v_g_out_a, v_g_out_b, v_g_norm2, row(v_g_final), v_rel_bias]
    small_upd, loss8 = _small_update(pay, rel, small_w, small_m, small_v)
    upd.update(zip(small_names, small_upd))

    order = ["w_ada", "b_ada", "g_norm1", "w_in", "g_cq", "w_uq", "g_ckv", "w_ukv", "rel_bias", "g_out_a", "g_out_b",
             "w_out", "g_norm2", "w_ffn_in", "w_ffn_out", "g_final"]
    like = dict(g_final=g_final)
    outs = [loss8[0, 0], grad_x.reshape(x.shape)]
    for part in range(4):
        for n in order:
            val = upd[n][part]
            outs.append(val.reshape(like[n].shape) if n in like else val)
    return tuple(outs)
```

```python
import numpy as np
import jax
import jax.numpy as jnp
from jax import lax
from jax.experimental import pallas as pl
from jax.experimental.pallas import tpu as pltpu

F32, BF16 = jnp.float32, jnp.bfloat16

N_DEV = 8
D = 1024
S = 2048
H = 8
E_A = 64
D_A = H * E_A
Q_LORA, KV_LORA = 384, 256
NOPE, ROPE, VDIM = 64, 32, 64
HP = 128
P_IN = 3 * D_A + Q_LORA + KV_LORA + ROPE
P_PAD = 3 * D_A + Q_LORA + KV_LORA + HP
TAIL0 = 3 * D_A
TAIL = P_PAD - TAIL0
D_FF = 2816
N_MOD = 6
EPS = 1e-6
NEG = -1e30
BLK = 128
DILATIONS = (1, 4, 16)
N_BUCKETS, MAX_DISTANCE = 32, 2048
ROPE_THETA = 10000.0
SCALE_A = E_A ** -0.5
SCALE_B = (NOPE + ROPE) ** -0.5
B1, B2, LR, ADAM_EPS, WD, STEP = 0.9, 0.999, 0.001, 1e-8, 0.01, 10
VMEM_LIMIT = 56 * 1024 * 1024


def _cp(*sem):
    return pltpu.CompilerParams(dimension_semantics=sem, vmem_limit_bytes=VMEM_LIMIT)


def _pick(n, prefs):
    for p in prefs:
        if n % p == 0:
            return p
    raise ValueError(f"no tile of {prefs} divides {n}")


OPERAND_BYTES = 6 * 1024 * 1024


def _pick_rows(m, k):
    return _pick(m, [p for p in (1024, 512, 256, 128, 16) if p * k * 2 <= OPERAND_BYTES])


MATMUL_BYTES = 40 * 1024 * 1024


def _stream_rows(m, fixed, per_row):
    return _pick(m, [p for p in (4096, 2048, 1024, 512, 256, 128, 16) if fixed + p * per_row <= MATMUL_BYTES])


def _dot(a, b, dims):
    return lax.dot_general(a, b, (dims, ((), ())), preferred_element_type=F32)


def _mm_nn(a, b, out_dtype, name, after=None):
    m, k = a.shape
    n = b.shape[1]
    tn = _pick(n, (512, 256, 384, 128))
    tm = _stream_rows(m, 4 * k * tn, 4 * k + (2 * jnp.dtype(out_dtype).itemsize + 4) * tn)

    def body(a_ref, b_ref, *rest):
        o_ref = rest[-1]
        o_ref[...] = _dot(a_ref[...], b_ref[...], ((1,), (0,))).astype(o_ref.dtype)

    extra = [] if after is None else [after]
    return pl.pallas_call(
        body, name=name, grid=(m // tm, n // tn),
        in_specs=[pl.BlockSpec((tm, k), lambda i, j: (i, 0)), pl.BlockSpec((k, tn), lambda i, j: (0, j))] + [ANY] * len(extra),
        out_specs=pl.BlockSpec((tm, tn), lambda i, j: (i, j)),
        out_shape=jax.ShapeDtypeStruct((m, n), out_dtype),
        compiler_params=_cp("parallel", "parallel"),
    )(a, b, *extra)


def _mm_nt(a, b, out_dtype, name, after=None):
    m, k = a.shape
    n = b.shape[0]
    tn = _pick(n, (512, 256, 384, 128))
    tm = _stream_rows(m, 4 * k * tn, 4 * k + (2 * jnp.dtype(out_dtype).itemsize + 4) * tn)

    def body(a_ref, b_ref, *rest):
        o_ref = rest[-1]
        o_ref[...] = _dot(a_ref[...], b_ref[...], ((1,), (1,))).astype(o_ref.dtype)

    extra = [] if after is None else [after]
    return pl.pallas_call(
        body, name=name, grid=(m // tm, n // tn),
        in_specs=[pl.BlockSpec((tm, k), lambda i, j: (i, 0)), pl.BlockSpec((tn, k), lambda i, j: (j, 0))] + [ANY] * len(extra),
        out_specs=pl.BlockSpec((tm, tn), lambda i, j: (i, j)),
        out_shape=jax.ShapeDtypeStruct((m, n), out_dtype),
        compiler_params=_cp("parallel", "parallel"),
    )(a, b, *extra)


def _mm_tn(a, bs, name):
    t, m = a.shape
    n = bs[0].shape[1]
    nb_ = len(bs)
    tc = _pick(t, (512, 16))
    tn = _pick(n, (512, 384, 256, 128))
    tm = _pick(m, [p for p in (1024, 512, 384, 256, 128) if (3 * p + 2 * nb_ * tn) * t * 2 <= VMEM_LIMIT - 2 * OPERAND_BYTES])
    if tm <= 256 and nb_ * n * t * 2 <= 2 * OPERAND_BYTES:
        tn = n

    def body(*refs):
        a_ref, b_refs, o_refs, at_ref = refs[0], refs[1:1 + nb_], refs[1 + nb_:1 + 2 * nb_], refs[-1]

        @pl.when(pl.program_id(1) == 0)
        def _():
            def chunk(c, _):
                rows = pl.ds(pl.multiple_of(c * tc, tc), tc)
                at_ref[:, rows] = a_ref[rows, :].T
                return 0

            lax.fori_loop(0, t // tc, chunk, 0)

        for b_ref, o_ref in zip(b_refs, o_refs):
            o_ref[...] = _dot(at_ref[...], b_ref[...], ((1,), (0,))).astype(BF16)

    res = pl.pallas_call(
        body, name=name, grid=(m // tm, n // tn),
        in_specs=[pl.BlockSpec((t, tm), lambda i, j: (0, i))] + [pl.BlockSpec((t, tn), lambda i, j: (0, j))] * nb_,
        out_specs=[pl.BlockSpec((tm, tn), lambda i, j: (i, j))] * nb_,
        out_shape=[jax.ShapeDtypeStruct((m, n), BF16)] * nb_,
        scratch_shapes=[pltpu.VMEM((tm, t), BF16)],
        compiler_params=_cp("parallel", "arbitrary"),
    )(a, *bs)
    return res[0] if nb_ == 1 else res


def _mm_tn_rows(a_list, b, name):
    t, m = a_list[0].shape
    n = b.shape[1]
    na = len(a_list)
    tc, tm = _pick(t, (512, 16)), _pick(m, (256, 128))
    nblk = m // tm

    def body(*refs):
        a_refs, b_ref, o_ref, bt_ref, r_ref = refs[:na], refs[na], refs[na + 1], refs[na + 2], refs[na + 3]
        i = pl.program_id(0)

        @pl.when(i == 0)
        def _():
            def chunk(c, _):
                rows = pl.ds(pl.multiple_of(c * tc, tc), tc)
                bt_ref[:, rows] = b_ref[rows, :].T
                return 0

            lax.fori_loop(0, t // tc, chunk, 0)

        for s, a_ref in enumerate(a_refs):
            @pl.when((i >= s * nblk) & (i < (s + 1) * nblk))
            def _(a_ref=a_ref):
                r_ref[...] = _dot(bt_ref[...], a_ref[...], ((1,), (0,)))
                o_ref[...] = r_ref[...].T.astype(BF16)

    return pl.pallas_call(
        body, name=name, grid=(na * nblk,),
        in_specs=[pl.BlockSpec((t, tm), lambda i, s=s: (0, jnp.clip(i - s * nblk, 0, nblk - 1))) for s in range(na)]
        + [pl.BlockSpec((t, n), lambda i: (0, 0))],
        out_specs=pl.BlockSpec((tm, n), lambda i: (i, 0)),
        out_shape=jax.ShapeDtypeStruct((na * m, n), BF16),
        scratch_shapes=[pltpu.VMEM((n, t), BF16), pltpu.VMEM((n, tm), F32)],
        compiler_params=_cp("arbitrary"),
    )(*a_list, b)


EPI = 256


def _silu_parts(g):
    sg = 0.5 * jnp.tanh(0.5 * g) + 0.5
    return sg, g * sg


def _ffn_in(h2, wt):
    t, k = h2.shape
    tn = _pick(D_FF, (256, 128))
    tm = _stream_rows(t, 8 * k * tn, 4 * k + (3 * 2 * 2 + 2 * 4) * tn)
    nj = D_FF // tn

    def body(h_ref, wg_ref, wu_ref, g_ref, u_ref, a_ref):
        hv = h_ref[...]
        g_all = _dot(hv, wg_ref[...], ((1,), (1,)))
        u_all = _dot(hv, wu_ref[...], ((1,), (1,)))
        for r in range(tm // EPI):
            rows = slice(r * EPI, (r + 1) * EPI)
            g, u = g_all[rows], u_all[rows]
            g_ref[rows, :] = g.astype(BF16)
            u_ref[rows, :] = u.astype(BF16)
            a_ref[rows, :] = (_silu_parts(g)[1] * u).astype(BF16)

    blk = pl.BlockSpec((tm, tn), lambda i, j: (i, j))
    return pl.pallas_call(
        body, name="ffn_in", grid=(t // tm, nj),
        in_specs=[pl.BlockSpec((tm, k), lambda i, j: (i, 0)), pl.BlockSpec((tn, k), lambda i, j: (j, 0)),
                  pl.BlockSpec((tn, k), lambda i, j: (j + nj, 0))],
        out_specs=[blk] * 3, out_shape=[jax.ShapeDtypeStruct((t, D_FF), BF16)] * 3,
        compiler_params=_cp("parallel", "parallel"),
    )(h2, wt, wt)


def _d_act(df, w, g, u):
    t, k = df.shape
    tn = _pick(D_FF, (256, 128))
    tm = _stream_rows(t, 4 * k * tn, 4 * k + (4 * 2 * 2 + 4) * tn)

    def body(df_ref, w_ref, g_ref, u_ref, dg_ref, du_ref):
        da_all = _dot(df_ref[...], w_ref[...], ((1,), (1,)))
        for r in range(tm // EPI):
            rows = slice(r * EPI, (r + 1) * EPI)
            da = da_all[rows]
            gv = g_ref[rows, :].astype(F32)
            sg, silu = _silu_parts(gv)
            dg_ref[rows, :] = ((da * u_ref[rows, :].astype(F32)) * (sg + silu * (1.0 - sg))).astype(BF16)
            du_ref[rows, :] = (da * silu).astype(BF16)

    blk = pl.BlockSpec((tm, tn), lambda i, j: (i, j))
    return pl.pallas_call(
        body, name="d_act", grid=(t // tm, D_FF // tn),
        in_specs=[pl.BlockSpec((tm, k), lambda i, j: (i, 0)), pl.BlockSpec((tn, k), lambda i, j: (j, 0)), blk, blk],
        out_specs=[blk] * 2, out_shape=[jax.ShapeDtypeStruct((t, D_FF), BF16)] * 2,
        compiler_params=_cp("parallel", "parallel"),
    )(df, w, g, u)


def _d_h2(dg, du, wt):
    t = dg.shape[0]
    n = wt.shape[1]
    tm, tn = _pick_rows(t, D_FF), _pick(n, (512, 256, 128))

    def body(dg_ref, du_ref, wg_ref, wu_ref, o_ref):
        o_ref[...] = (_dot(dg_ref[...], wg_ref[...], ((1,), (0,)))
                      + _dot(du_ref[...], wu_ref[...], ((1,), (0,)))).astype(BF16)

    return pl.pallas_call(
        body, name="d_h2", grid=(t // tm, n // tn),
        in_specs=[pl.BlockSpec((tm, D_FF), lambda i, j: (i, 0)), pl.BlockSpec((tm, D_FF), lambda i, j: (i, 0)),
                  pl.BlockSpec((D_FF, tn), lambda i, j: (0, j)), pl.BlockSpec((D_FF, tn), lambda i, j: (1, j))],
        out_specs=pl.BlockSpec((tm, tn), lambda i, j: (i, j)),
        out_shape=jax.ShapeDtypeStruct((t, n), BF16),
        compiler_params=_cp("parallel", "parallel"),
    )(dg, du, wt, wt)


TM = 1024


def _row(w):
    return pl.BlockSpec((TM, w), lambda i: (i, 0))


def _row_at(w, col):
    return pl.BlockSpec((TM, w), lambda i: (i, col))


def _vec(w):
    return pl.BlockSpec((1, w), lambda i: (0, 0))


def _per_ex(w):
    return pl.BlockSpec((1, 1, w), lambda i: (i // (S // TM), 0, 0))


def _pos(w):
    return pl.BlockSpec((TM, w), lambda i: (i % (S // TM), 0))


def _full(shape):
    return pl.BlockSpec(shape, lambda i: (0,) * len(shape))


def _rms(x):
    return lax.rsqrt(jnp.mean(x * x, axis=-1, keepdims=True) + EPS)


def _rms_bwd(n, r, dn):
    return r * (dn - n * jnp.mean(dn * n, axis=-1, keepdims=True))


def _colsum(v):
    return jnp.sum(v, axis=0, keepdims=True)


def _acc_first(i, ref, val, every=None):
    first = (i == 0) if every is None else (i % every == 0)

    @pl.when(first)
    def _():
        ref[...] = jnp.zeros_like(ref)

    ref[...] += val.reshape(ref.shape)


def _pre1(x, g, sc, sh):
    t = x.shape[0]

    def body(x_ref, g_ref, sc_ref, sh_ref, h_ref):
        xv = x_ref[...]
        n = xv * _rms(xv)
        h_ref[...] = ((n * g_ref[...]) * (1.0 + sc_ref[0]) + sh_ref[0]).astype(BF16)

    return pl.pallas_call(
        body, name="pre1", grid=(t // TM,),
        in_specs=[_row(D), _vec(D), _per_ex(D), _per_ex(D)],
        out_specs=_row(D), out_shape=jax.ShapeDtypeStruct((t, D), BF16),
        compiler_params=_cp("parallel"),
    )(x, g, sc, sh)


def _rope_fwd(v, c, sm, sp):
    return v * c + pltpu.roll(v, HP - ROPE // 2, 1) * sm + pltpu.roll(v, ROPE // 2, 1) * sp


def _rope_bwd(dv, c, sm, sp):
    return dv * c + pltpu.roll(dv * sm, ROPE // 2, 1) + pltpu.roll(dv * sp, HP - ROPE // 2, 1)


def _mla_pre(proj, g_cq, g_ckv, w_uq, w_k, w_v, rc, rsm, rsp):
    t = proj.shape[0]

    def body(tail_ref, gq_ref, gkv_ref, wuq_ref, wk_ref, wv_ref, c_ref, sm_ref, sp_ref,
             q_ref, k_ref, v_ref, cqn_ref, ckvn_ref):
        tail = tail_ref[...]
        cq, ckv, kr = tail[:, :Q_LORA], tail[:, Q_LORA:Q_LORA + KV_LORA], tail[:, Q_LORA + KV_LORA:]
        cqn = (cq * _rms(cq) * gq_ref[...]).astype(BF16)
        ckvn = (ckv * _rms(ckv) * gkv_ref[...]).astype(BF16)
        cqn_ref[...] = cqn
        ckvn_ref[...] = ckvn
        c, sm, sp = c_ref[...], sm_ref[...], sp_ref[...]
        q = _dot(cqn, wuq_ref[...], ((1,), (1,)))
        kn = _dot(ckvn, wk_ref[...], ((1,), (0,)))
        v_ref[...] = _dot(ckvn, wv_ref[...], ((1,), (0,))).astype(BF16)
        krr = _rope_fwd(kr, c, sm, sp)
        for h in range(H):
            sl = slice(h * HP, (h + 1) * HP)
            q_ref[:, sl] = _rope_fwd(q[:, sl], c, sm, sp).astype(BF16)
            k_ref[:, sl] = (kn[:, sl] + krr).astype(BF16)

    wide = H * HP
    return pl.pallas_call(
        body, name="mla_pre", grid=(t // TM,),
        in_specs=[_row_at(TAIL, TAIL0 // TAIL), _vec(Q_LORA), _vec(KV_LORA), _full((wide, Q_LORA)),
                  _full((KV_LORA, wide)), _full((KV_LORA, wide)), _pos(HP), _pos(HP), _pos(HP)],
        out_specs=[_row(wide), _row(wide), _row(wide), _row(Q_LORA), _row(KV_LORA)],
        out_shape=[jax.ShapeDtypeStruct((t, wide), BF16)] * 3
        + [jax.ShapeDtypeStruct((t, Q_LORA), BF16), jax.ShapeDtypeStruct((t, KV_LORA), BF16)],
        compiler_params=_cp("parallel"),
    )(proj, g_cq, g_ckv, w_uq, w_k, w_v, rc, rsm, rsp)


def _mla_pre_bwd(proj, dq_, dk_, dv_, dqkv_a, g_cq, g_ckv, w_uq, w_k, w_v, rc, rsm, rsp):
    t = proj.shape[0]
    wide = H * HP

    def body(tail_ref, dq_ref, dk_ref, dv_ref, dqa_ref, dka_ref, dva_ref, gq_ref, gkv_ref, wuq_ref, wk_ref, wv_ref,
             c_ref, sm_ref, sp_ref, dqo_ref, dproj_ref, dgq_ref, dgkv_ref):
        i = pl.program_id(0)
        for n, src in enumerate((dqa_ref, dka_ref, dva_ref)):
            dproj_ref[:, n * D_A:(n + 1) * D_A] = src[...]
        dtail_ref = dproj_ref.at[:, TAIL0:]
        tail = tail_ref[...]
        cq, ckv = tail[:, :Q_LORA], tail[:, Q_LORA:Q_LORA + KV_LORA]
        c, sm, sp = c_ref[...], sm_ref[...], sp_ref[...]
        dkr = jnp.zeros((TM, HP), F32)
        for h in range(H):
            sl = slice(h * HP, (h + 1) * HP)
            dqo_ref[:, sl] = _rope_bwd(dq_ref[:, sl].astype(F32), c, sm, sp).astype(BF16)
            dkr = dkr + dk_ref[:, sl].astype(F32)
        lane = lax.broadcasted_iota(jnp.int32, (TM, HP), 1)
        dkr = jnp.where((lane >= NOPE) & (lane < NOPE + ROPE), _rope_bwd(dkr, c, sm, sp), 0.0)
        dkb = dk_ref[...]
        dvb = dv_ref[...]
        dcqn = _dot(dqo_ref[...], wuq_ref[...], ((1,), (0,)))
        dckvn = _dot(dkb, wk_ref[...], ((1,), (1,))) + _dot(dvb, wv_ref[...], ((1,), (1,)))
        rq, rkv = _rms(cq), _rms(ckv)
        nq, nkv = cq * rq, ckv * rkv
        _acc_first(i, dgq_ref, _colsum(dcqn * nq))
        _acc_first(i, dgkv_ref, _colsum(dckvn * nkv))
        dtail_ref[:, :Q_LORA] = _rms_bwd(nq, rq, dcqn * gq_ref[...]).astype(BF16)
        dtail_ref[:, Q_LORA:Q_LORA + KV_LORA] = _rms_bwd(nkv, rkv, dckvn * gkv_ref[...]).astype(BF16)
        dtail_ref[:, Q_LORA + KV_LORA:] = dkr.astype(BF16)

    return pl.pallas_call(
        body, name="mla_pre_bwd", grid=(t // TM,),
        in_specs=[_row_at(TAIL, TAIL0 // TAIL), _row(wide), _row(wide), _row(wide), _row(D_A), _row(D_A), _row(D_A),
                  _vec(Q_LORA), _vec(KV_LORA), _full((wide, Q_LORA)), _full((KV_LORA, wide)), _full((KV_LORA, wide)),
                  _pos(HP), _pos(HP), _pos(HP)],
        out_specs=[_row(wide), _row(P_PAD), _vec(Q_LORA), _vec(KV_LORA)],
        out_shape=[jax.ShapeDtypeStruct((t, wide), BF16), jax.ShapeDtypeStruct((t, P_PAD), BF16),
                   jax.ShapeDtypeStruct((1, Q_LORA), F32), jax.ShapeDtypeStruct((1, KV_LORA), F32)],
        compiler_params=_cp("arbitrary"),
    )(proj, dq_, dk_, dv_, *dqkv_a, g_cq, g_ckv, w_uq, w_k, w_v, rc, rsm, rsp)


def _post_attn(out_a, out_b, g_a, g_b):
    t = out_a.shape[0]

    def body(a_ref, b_ref, ga_ref, gb_ref, y_ref):
        a, b = a_ref[...], b_ref[...]
        y_ref[:, :D_A] = (a * _rms(a) * ga_ref[...]).astype(BF16)
        y_ref[:, D_A:] = (b * _rms(b) * gb_ref[...]).astype(BF16)

    return pl.pallas_call(
        body, name="post_attn", grid=(t // TM,),
        in_specs=[_row(D_A), _row(D_A), _vec(D_A), _vec(D_A)],
        out_specs=_row(D), out_shape=jax.ShapeDtypeStruct((t, D), BF16),
        compiler_params=_cp("parallel"),
    )(out_a, out_b, g_a, g_b)


def _post_attn_bwd(dy, out_a, out_b, g_a, g_b):
    t = dy.shape[0]

    def body(dy_ref, a_ref, b_ref, ga_ref, gb_ref, da_ref, db_ref, dga_ref, dgb_ref):
        i = pl.program_id(0)
        dy_ = dy_ref[...].astype(F32)
        for src, g_ref, dst, dg_ref, sl in ((a_ref, ga_ref, da_ref, dga_ref, slice(0, D_A)),
                                            (b_ref, gb_ref, db_ref, dgb_ref, slice(D_A, D))):
            v = src[...]
            r = _rms(v)
            n = v * r
            dyv = dy_[:, sl]
            _acc_first(i, dg_ref, _colsum(dyv * n))
            dst[...] = _rms_bwd(n, r, dyv * g_ref[...])

    return pl.pallas_call(
        body, name="post_attn_bwd", grid=(t // TM,),
        in_specs=[_row(D), _row(D_A), _row(D_A), _vec(D_A), _vec(D_A)],
        out_specs=[_row(D_A), _row(D_A), _vec(D_A), _vec(D_A)],
        out_shape=[jax.ShapeDtypeStruct((t, D_A), F32)] * 2 + [jax.ShapeDtypeStruct((1, D_A), F32)] * 2,
        compiler_params=_cp("arbitrary"),
    )(dy, out_a, out_b, g_a, g_b)


def _resid_norm2(x, mix, g1, g, sc, sh):
    t = x.shape[0]

    def body(x_ref, mix_ref, g1_ref, g_ref, sc_ref, sh_ref, x2_ref, h_ref):
        x2 = x_ref[...] + g1_ref[0] * mix_ref[...]
        x2_ref[...] = x2
        n = x2 * _rms(x2)
        h_ref[...] = ((n * g_ref[...]) * (1.0 + sc_ref[0]) + sh_ref[0]).astype(BF16)

    return pl.pallas_call(
        body, name="resid_norm2", grid=(t // TM,),
        in_specs=[_row(D), _row(D), _per_ex(D), _vec(D), _per_ex(D), _per_ex(D)],
        out_specs=[_row(D), _row(D)],
        out_shape=[jax.ShapeDtypeStruct((t, D), F32), jax.ShapeDtypeStruct((t, D), BF16)],
        compiler_params=_cp("parallel"),
    )(x, mix, g1, g, sc, sh)


def _sigmoid(v):
    return 1.0 / (1.0 + jnp.exp(-v))


def _final(x2, f, g2, g_fin, target):
    t = x2.shape[0]
    nb = t // S
    tpb = S // TM

    def body(x2_ref, f_ref, g2_ref, g_ref, t_ref, dx3_ref, df_ref, loss_ref, dgf_ref, dg2_ref):
        i = pl.program_id(0)
        fv = f_ref[...].astype(F32)
        x3 = x2_ref[...] + g2_ref[0] * fv
        r = _rms(x3)
        n = x3 * r
        err = n * g_ref[...] - t_ref[...]
        _acc_first(i, loss_ref, _colsum(err * err))
        dy = err * (1.0 / D)
        _acc_first(i, dgf_ref, _colsum(dy * n))
        dx3 = _rms_bwd(n, r, dy * g_ref[...])
        dx3_ref[...] = dx3
        _acc_first(i, dg2_ref, _colsum(dx3 * fv), every=tpb)
        df_ref[...] = (dx3 * g2_ref[0]).astype(BF16)

    return pl.pallas_call(
        body, name="final", grid=(t // TM,),
        in_specs=[_row(D), _row(D), _per_ex(D), _vec(D), _row(D)],
        out_specs=[_row(D), _row(D), _vec(D), _vec(D), _per_ex(D)],
        out_shape=[jax.ShapeDtypeStruct((t, D), F32), jax.ShapeDtypeStruct((t, D), BF16),
                   jax.ShapeDtypeStruct((1, D), F32), jax.ShapeDtypeStruct((1, D), F32),
                   jax.ShapeDtypeStruct((nb, 1, D), F32)],
        compiler_params=_cp("arbitrary"),
    )(x2, f, g2, g_fin, target)


def _norm_bwd(xin, dh, dres, g, sc, gate=None):
    t = xin.shape[0]
    nb = t // S
    tpb = S // TM
    gated = gate is not None

    def body(*refs):
        if gated:
            x_ref, dh_ref, dres_ref, g_ref, sc_ref, mix_ref, g1_ref, dx_ref, dsh_ref, dsc_ref, dg_ref, dg1_ref, dmix_ref = refs
        else:
            x_ref, dh_ref, dres_ref, g_ref, sc_ref, dx_ref, dsh_ref, dsc_ref, dg_ref = refs
        i = pl.program_id(0)
        xv, dhv = x_ref[...], dh_ref[...].astype(F32)
        r = _rms(xv)
        n = xv * r
        gv = g_ref[...]
        _acc_first(i, dsh_ref, _colsum(dhv), every=tpb)
        _acc_first(i, dsc_ref, _colsum(dhv * (n * gv)), every=tpb)
        dng = dhv * (1.0 + sc_ref[0])
        _acc_first(i, dg_ref, _colsum(dng * n))
        dx = dres_ref[...] + _rms_bwd(n, r, dng * gv)
        dx_ref[...] = dx
        if gated:
            _acc_first(i, dg1_ref, _colsum(dx * mix_ref[...].astype(F32)), every=tpb)
            dmix_ref[...] = (dx * g1_ref[0]).astype(BF16)

    in_specs = [_row(D), _row(D), _row(D), _vec(D), _per_ex(D)]
    out_specs = [_row(D), _per_ex(D), _per_ex(D), _vec(D)]
    out_shape = [jax.ShapeDtypeStruct((t, D), F32), jax.ShapeDtypeStruct((nb, 1, D), F32),
                 jax.ShapeDtypeStruct((nb, 1, D), F32), jax.ShapeDtypeStruct((1, D), F32)]
    args = [xin, dh, dres, g, sc]
    if gated:
        in_specs += [_row(D), _per_ex(D)]
        out_specs += [_per_ex(D), _row(D)]
        out_shape += [jax.ShapeDtypeStruct((nb, 1, D), F32), jax.ShapeDtypeStruct((t, D), BF16)]
        args += list(gate)
    return pl.pallas_call(
        body, name="norm2_bwd" if gated else "norm1_bwd", grid=(t // TM,),
        in_specs=in_specs, out_specs=out_specs, out_shape=out_shape,
        compiler_params=_cp("arbitrary"),
    )(*args)


TQ = 256
TB = 512
FWD_HEADS = 2


def _mla_fwd(q, k, v):
    t = q.shape[0]
    nb = t // S

    def body(q_ref, k_ref, v_ref, o_ref, lse_ref):
        causal = lax.broadcasted_iota(jnp.int32, (TB, TB), 0) >= lax.broadcasted_iota(jnp.int32, (TB, TB), 1)
        heads = [slice(h * HP, (h + 1) * HP) for h in range(FWD_HEADS)]
        for i in range(S // TB):
            ri, past = slice(i * TB, (i + 1) * TB), slice(0, i * TB)
            qhs = [q_ref[ri, sl] for sl in heads]
            sd = [jnp.where(causal, _dot(qh, k_ref[ri, sl], ((1,), (1,))) * SCALE_B, NEG) for qh, sl in zip(qhs, heads)]
            ms = [jnp.max(s, axis=-1, keepdims=True) for s in sd]
            if i:
                so = [_dot(qh, k_ref[past, sl], ((1,), (1,))) * SCALE_B for qh, sl in zip(qhs, heads)]
                ms = [jnp.maximum(m, jnp.max(s, axis=-1, keepdims=True)) for m, s in zip(ms, so)]
            pd = [jnp.exp(s - m) for s, m in zip(sd, ms)]
            ls = [jnp.sum(p, axis=-1, keepdims=True) for p in pd]
            acc = [_dot(p.astype(BF16), v_ref[ri, sl], ((1,), (0,))) for p, sl in zip(pd, heads)]
            if i:
                po = [jnp.exp(s - m) for s, m in zip(so, ms)]
                ls = [l + jnp.sum(p, axis=-1, keepdims=True) for l, p in zip(ls, po)]
                acc = [a + _dot(p.astype(BF16), v_ref[past, sl], ((1,), (0,))) for a, p, sl in zip(acc, po, heads)]
            for pr in range(FWD_HEADS // 2):
                o_ref[ri, pr * HP:(pr + 1) * HP] = acc[2 * pr] / ls[2 * pr] + acc[2 * pr + 1] / ls[2 * pr + 1]
            for sl, m, l in zip(heads, ms, ls):
                lse_ref[ri, sl] = jnp.broadcast_to(m + jnp.log(l), (TB, HP))

    wide2 = pl.BlockSpec((S, FWD_HEADS * HP), lambda b, p: (b, p))
    return pl.pallas_call(
        body, name="mla_fwd", grid=(nb, H // FWD_HEADS),
        in_specs=[wide2, wide2, wide2],
        out_specs=[pl.BlockSpec((S, FWD_HEADS // 2 * HP), lambda b, p: (b, p)), wide2],
        out_shape=[jax.ShapeDtypeStruct((t, H * VDIM), F32), jax.ShapeDtypeStruct((t, H * HP), F32)],
        compiler_params=_cp("parallel", "parallel"),
    )(q, k, v)


def _mla_bwd(q, k, v, o, do, lse):
    t = q.shape[0]
    nb = t // S

    def body(q_ref, k_ref, v_ref, o_ref, do_ref, lse_ref, dq_out, dk_out, dv_out, dq_ref, dk_ref, dv_ref):
        lane = lax.broadcasted_iota(jnp.int32, (TB, HP), 1)
        causal = lax.broadcasted_iota(jnp.int32, (TB, TB), 0) >= lax.broadcasted_iota(jnp.int32, (TB, TB), 1)
        heads = [slice(h * HP, (h + 1) * HP) for h in range(2)]
        nblk = S // TB
        for i in reversed(range(nblk)):
            ri, past = slice(i * TB, (i + 1) * TB), slice(0, i * TB)
            dov = do_ref[ri, :]
            prod = dov * o_ref[ri, :]
            dob = dov.astype(BF16)
            deltas = [jnp.sum(jnp.where((lane < VDIM) if h == 0 else (lane >= VDIM), prod, 0.0), axis=-1, keepdims=True)
                      for h in range(2)]
            qhs = [q_ref[ri, sl] for sl in heads]
            lses = [lse_ref[ri, sl][:, :1] for sl in heads]
            for rows, diagonal in ((ri, True), (past, False)):
                if rows.stop == rows.start:
                    continue
                ps = [jnp.exp(_dot(qh, k_ref[rows, sl], ((1,), (1,))) * SCALE_B - lse) for qh, sl, lse in zip(qhs, heads, lses)]
                if diagonal:
                    ps = [jnp.where(causal, p, 0.0) for p in ps]
                dps = [_dot(dob, v_ref[rows, sl], ((1,), (1,))) for sl in heads]
                dss = [(p * (dp - delta) * SCALE_B).astype(BF16) for p, dp, delta in zip(ps, dps, deltas)]
                for sl, qh, p, ds in zip(heads, qhs, ps, dss):
                    dq = _dot(ds, k_ref[rows, sl], ((1,), (0,)))
                    dk = _dot(ds, qh, ((0,), (0,)))
                    dv = _dot(p.astype(BF16), dob, ((0,), (0,)))
                    if diagonal:
                        dq_ref[ri, sl] = dq
                    else:
                        dq_ref[ri, sl] += dq
                    if i == nblk - 1:
                        dk_ref[rows, sl] = dk
                        dv_ref[rows, sl] = dv
                    else:
                        dk_ref[rows, sl] += dk
                        dv_ref[rows, sl] += dv
        dq_out[...] = dq_ref[...].astype(BF16)
        dk_out[...] = dk_ref[...].astype(BF16)
        dv_out[...] = dv_ref[...].astype(BF16)

    wide2 = pl.BlockSpec((S, 2 * HP), lambda b, p: (b, p))
    pair = pl.BlockSpec((S, HP), lambda b, p: (b, p))
    return pl.pallas_call(
        body, name="mla_bwd", grid=(nb, H // 2),
        in_specs=[wide2, wide2, wide2, pair, pair, wide2],
        out_specs=[wide2, wide2, wide2],
        out_shape=[jax.ShapeDtypeStruct((t, H * HP), BF16)] * 3,
        scratch_shapes=[pltpu.VMEM((S, 2 * HP), F32)] * 3,
        compiler_params=_cp("parallel", "parallel"),
    )(q, k, v, o, do, lse)


def _t5_bucket(dist):
    max_exact = N_BUCKETS // 2
    d = np.maximum(dist, 1).astype(np.float64)
    large = max_exact + (np.log(d / max_exact) / np.log(MAX_DISTANCE / max_exact) * (N_BUCKETS - max_exact)).astype(np.int64)
    large = np.minimum(large, N_BUCKETS - 1)
    return np.where(dist < max_exact, dist, large).astype(np.int32)


def _band_geometry():
    a = np.arange(BLK)[:, None]
    bk = np.arange(2 * BLK)[None, :]
    steps = BLK + a - bk
    valid = (steps >= 0) & (steps <= BLK)
    buckets = np.stack([_t5_bucket(np.clip(steps, 0, BLK) * d) for d in DILATIONS])
    return buckets, valid


def _band_bias(rel_bias):
    buckets, valid = _band_geometry()
    onehot = (jnp.asarray(buckets)[..., None] == jnp.arange(N_BUCKETS)).astype(F32)
    bias = jnp.einsum("rqkn,nh->rhqk", onehot, rel_bias, precision=lax.Precision.HIGHEST)
    bias = jnp.where(jnp.asarray(valid)[None, None], bias, NEG)
    return bias.reshape(3, H // 2, 2 * BLK, 2 * BLK)


def _dil_items():
    items = []
    for r, d in enumerate(DILATIONS):
        for res in range(d):
            for blk in range(S // d // BLK):
                items.append((r, d, blk * BLK * d + res, blk > 0))
    return items


GROUP = 4


def _strided(start, d):
    return pl.ds(start, BLK) if d == 1 else pl.ds(start, BLK, stride=d)


def _stack_heads(tile, own):
    return jnp.where(own, jnp.concatenate([tile, tile], axis=0), 0.0).astype(BF16)


def _own_lanes():
    row = lax.broadcasted_iota(jnp.int32, (2 * BLK, HP), 0)
    lane = lax.broadcasted_iota(jnp.int32, (2 * BLK, HP), 1)
    return (lane < E_A) == (row < BLK)


def _dil_fwd(proj, biasm):
    t = proj.shape[0]
    nb = t // S

    def body(q_ref, k_ref, v_ref, b_ref, o_ref, lse_ref, ob_ref, lb_ref):
        lane = lax.broadcasted_iota(jnp.int32, (BLK, HP), 1)
        own = _own_lanes()
        items = _dil_items()
        for g in range(0, len(items), GROUP):
            grp = items[g:g + GROUP]
            ss, vts = [], []
            for r, d, start, has_prev in grp:
                cur = _strided(start, d)
                rows = [_strided(start - BLK * d, d), cur] if has_prev else [cur]
                q2 = _stack_heads(q_ref[cur, :] * SCALE_A, own)
                kt = jnp.concatenate([k_ref[x, :] for x in rows], axis=0).astype(BF16)
                vts.append(jnp.concatenate([v_ref[x, :] for x in rows], axis=0).astype(BF16))
                bias = b_ref[r, 0] if has_prev else b_ref[r, 0, :, BLK:]
                ss.append(_dot(q2, kt, ((1,), (1,))) + bias)
            ms = [jnp.max(s, axis=-1, keepdims=True) for s in ss]
            ps = [jnp.exp(s - m) for s, m in zip(ss, ms)]
            ls = [jnp.sum(p, axis=-1, keepdims=True) for p in ps]
            for (r, d, start, _), p, vt, m, l in zip(grp, ps, vts, ms, ls):
                cur = _strided(start, d)
                o2 = _dot(p.astype(BF16), vt, ((1,), (0,))) / l
                lse2 = m + jnp.log(l)
                ob_ref[r, cur, :] = jnp.where(lane < E_A, o2[:BLK], o2[BLK:])
                lb_ref[r, cur, :] = jnp.where(lane < E_A, lse2[:BLK], lse2[BLK:])

        def merge(c, _):
            rows = pl.ds(pl.multiple_of(c * TQ, TQ), TQ)
            l0, l1, l2 = lb_ref[0, rows, :], lb_ref[1, rows, :], lb_ref[2, rows, :]
            m = jnp.maximum(jnp.maximum(l0, l1), l2)
            e0, e1, e2 = jnp.exp(l0 - m), jnp.exp(l1 - m), jnp.exp(l2 - m)
            tot = e0 + e1 + e2
            o_ref[rows, :] = (e0 * ob_ref[0, rows, :] + e1 * ob_ref[1, rows, :] + e2 * ob_ref[2, rows, :]) / tot
            lse_ref[rows, :] = m + jnp.log(tot)
            return 0

        lax.fori_loop(0, S // TQ, merge, 0)

    npair = H // 2
    return pl.pallas_call(
        body, name="dil_fwd", grid=(nb, npair),
        in_specs=[pl.BlockSpec((S, HP), lambda b, p: (b, p)), pl.BlockSpec((S, HP), lambda b, p: (b, npair + p)),
                  pl.BlockSpec((S, HP), lambda b, p: (b, 2 * npair + p)),
                  pl.BlockSpec((3, 1, 2 * BLK, 2 * BLK), lambda b, p: (0, p, 0, 0))],
        out_specs=[pl.BlockSpec((S, HP), lambda b, p: (b, p))] * 2,
        out_shape=[jax.ShapeDtypeStruct((t, D_A), F32)] * 2,
        scratch_shapes=[pltpu.VMEM((3, S, HP), F32), pltpu.VMEM((3, S, HP), F32)],
        compiler_params=_cp("parallel", "parallel"),
    )(proj, proj, proj, biasm)


def _dil_bwd(proj, biasm, o, do, lse):
    t = proj.shape[0]
    nb = t // S

    def body(q_ref, k_ref, v_ref, b_ref, o_ref, do_ref, lse_ref, dq_out, dk_out, dv_out, ds_ref, dq_ref, dk_ref, dv_ref):
        dq_ref[...] = jnp.zeros_like(dq_ref)
        dk_ref[...] = jnp.zeros_like(dk_ref)
        dv_ref[...] = jnp.zeros_like(dv_ref)
        ds_ref[...] = jnp.zeros_like(ds_ref)
        lane = lax.broadcasted_iota(jnp.int32, (BLK, HP), 1)
        own = _own_lanes()
        items = _dil_items()
        for g in range(0, len(items), GROUP):
            grp = items[g:g + GROUP]
            q2s, kts, do2s, ss, dps, lse2s, delta2s = [], [], [], [], [], [], []
            for r, d, start, has_prev in grp:
                cur = _strided(start, d)
                rows = [_strided(start - BLK * d, d), cur] if has_prev else [cur]
                q2 = _stack_heads(q_ref[cur, :] * SCALE_A, own)
                kt = jnp.concatenate([k_ref[x, :] for x in rows], axis=0).astype(BF16)
                vt = jnp.concatenate([v_ref[x, :] for x in rows], axis=0).astype(BF16)
                dot_ = do_ref[cur, :]
                prod = dot_ * o_ref[cur, :]
                lset = lse_ref[cur, :]
                do2 = _stack_heads(dot_, own)
                bias = b_ref[r, 0] if has_prev else b_ref[r, 0, :, BLK:]
                ss.append(_dot(q2, kt, ((1,), (1,))) + bias)
                dps.append(_dot(do2, vt, ((1,), (1,))))
                lse2s.append(jnp.concatenate([lset[:, :1], lset[:, E_A:E_A + 1]], axis=0))
                delta2s.append(jnp.concatenate([jnp.sum(jnp.where(lane < E_A, prod, 0.0), axis=-1, keepdims=True),
                                                jnp.sum(jnp.where(lane >= E_A, prod, 0.0), axis=-1, keepdims=True)], axis=0))
                q2s.append(q2)
                kts.append(kt)
                do2s.append(do2)
            ps = [jnp.exp(s - lse2) for s, lse2 in zip(ss, lse2s)]
            dls = [p * (dp - delta2) for p, dp, delta2 in zip(ps, dps, delta2s)]
            for (r, d, start, has_prev), q2, kt, do2, p, dl in zip(grp, q2s, kts, do2s, ps, dls):
                cur = _strided(start, d)
                dsb = dl.astype(BF16)
                dq2 = _dot(dsb, kt, ((1,), (0,))) * SCALE_A
                dkt = _dot(dsb, q2, ((0,), (0,)))
                dvt = _dot(p.astype(BF16), do2, ((0,), (0,)))
                dq_ref[cur, :] += jnp.where(lane < E_A, dq2[:BLK], dq2[BLK:])
                if has_prev:
                    prev = _strided(start - BLK * d, d)
                    ds_ref[0, r, 0] += dl
                    dk_ref[prev, :] += dkt[:BLK]
                    dv_ref[prev, :] += dvt[:BLK]
                    dk_ref[cur, :] += dkt[BLK:]
                    dv_ref[cur, :] += dvt[BLK:]
                else:
                    ds_ref[0, r, 0, :, BLK:] += dl
                    dk_ref[cur, :] += dkt
                    dv_ref[cur, :] += dvt
        dq_out[...] = dq_ref[...].astype(BF16)
        dk_out[...] = dk_ref[...].astype(BF16)
        dv_out[...] = dv_ref[...].astype(BF16)

    npair = H // 2
    pair = pl.BlockSpec((S, HP), lambda b, p: (b, p))
    return pl.pallas_call(
        body, name="dil_bwd", grid=(nb, npair),
        in_specs=[pair, pl.BlockSpec((S, HP), lambda b, p: (b, npair + p)),
                  pl.BlockSpec((S, HP), lambda b, p: (b, 2 * npair + p)),
                  pl.BlockSpec((3, 1, 2 * BLK, 2 * BLK), lambda b, p: (0, p, 0, 0)), pair, pair, pair],
        out_specs=[pair, pair, pair, pl.BlockSpec((1, 3, 1, 2 * BLK, 2 * BLK), lambda b, p: (b, 0, p, 0, 0))],
        out_shape=[jax.ShapeDtypeStruct((t, D_A), BF16)] * 3 + [jax.ShapeDtypeStruct((nb, 3, npair, 2 * BLK, 2 * BLK), F32)],
        scratch_shapes=[pltpu.VMEM((S, HP), F32)] * 3,
        compiler_params=_cp("parallel", "parallel"),
    )(proj, proj, proj, biasm, o, do, lse)


def _rel_bias_grad(dlogits):
    nb = dlogits.shape[0]
    buckets, _ = _band_geometry()
    kk = 3 * BLK * 2 * BLK
    dl = jnp.transpose(dlogits.reshape(nb, 3, H, BLK, 2 * BLK), (0, 2, 1, 3, 4)).reshape(nb, H, kk)
    bk = jnp.asarray(buckets.reshape(1, kk))
    tk = kk // 12

    def body(dl_ref, bk_ref, o_ref):
        j = pl.program_id(0)
        onehot = (bk_ref[...] == lax.broadcasted_iota(jnp.int32, (N_BUCKETS, tk), 0)).astype(F32)
        tot = dl_ref[0]
        for b in range(1, nb):
            tot = tot + dl_ref[b]
        part = lax.dot_general(onehot, tot, ((((1,), (1,))), ((), ())), preferred_element_type=F32,
                               precision=lax.Precision.HIGHEST)
        _acc_first(j, o_ref, part)

    return pl.pallas_call(
        body, name="rel_bias_grad", grid=(kk // tk,),
        in_specs=[pl.BlockSpec((nb, H, tk), lambda j: (0, 0, j)), pl.BlockSpec((1, tk), lambda j: (0, j))],
        out_specs=pl.BlockSpec((N_BUCKETS, H), lambda j: (0, 0)),
        out_shape=jax.ShapeDtypeStruct((N_BUCKETS, H), F32),
        compiler_params=_cp("arbitrary"),
    )(dl, bk)


def _mesh_place():
    x, y, c = lax.axis_index("x"), lax.axis_index("y"), lax.axis_index("c")
    return x, y, c


def _peer(k):
    x, y, c = _mesh_place()
    px = 1 - x if k & 4 else x
    py = 1 - y if k & 2 else y
    pc = 1 - c if k & 1 else c
    return (px, py, pc), 4 * px + 2 * py + pc


ANY = pl.BlockSpec(memory_space=pl.ANY)


def _exchange(arrays, gathers, name, after=None):
    n_arr = len(arrays)

    def body(*refs):
        ins, outs = refs[:n_arr], refs[n_arr + 1:2 * n_arr + 1]
        send, recv, loc = refs[2 * n_arr + 1:]
        x, y, c = _mesh_place()
        me = 4 * x + 2 * y + c
        local = [pltpu.make_async_copy(ins[a] if gathers[a] else ins[a].at[me], outs[a].at[me], loc.at[a])
                 for a in range(n_arr)]
        remote = _peer_copies(ins, outs, send, recv, gathers)
        for cp in local:
            cp.start()
        for put, _ in remote:
            put.start()
        for cp in local:
            cp.wait()
        for put, got in remote:
            put.wait_send()
            got.wait_recv()

    return pl.pallas_call(
        body, name=name,
        in_specs=[ANY] * (n_arr + 1), out_specs=[ANY] * n_arr,
        out_shape=[jax.ShapeDtypeStruct(((N_DEV,) if g else ()) + a.shape, a.dtype) for a, g in zip(arrays, gathers)],
        scratch_shapes=[pltpu.SemaphoreType.DMA((n_arr * (N_DEV - 1),)), pltpu.SemaphoreType.DMA((n_arr * (N_DEV - 1),)),
                        pltpu.SemaphoreType.DMA((n_arr,))],
        compiler_params=pltpu.CompilerParams(has_side_effects=True),
    )(*arrays, arrays[0] if after is None else after)


def _gather_two_level(arrays, name):
    n_arr = len(arrays)
    per = N_DEV - 1

    def body(*refs):
        ins, outs = refs[:n_arr], refs[n_arr:2 * n_arr]
        send, recv, loc = refs[2 * n_arr:]
        x, y, c = _mesh_place()
        me, sibling = (x, y, c), (x, y, 1 - c)
        chips = [(1 - x, y), (x, 1 - y), (1 - x, 1 - y)]

        def block(a, place):
            px, py, pc = place
            return outs[a].at[4 * px + 2 * py + pc]

        def copy(a, k, place, to, src=None):
            dst = block(a, place)
            return pltpu.make_async_remote_copy(dst if src is None else src, dst, send.at[a * per + k], recv.at[a * per + k],
                                                device_id=to, device_id_type=pl.DeviceIdType.MESH)

        local = [pltpu.make_async_copy(ins[a], block(a, me), loc.at[a]) for a in range(n_arr)]
        for cp in local:
            cp.start()
        first = []
        for a in range(n_arr):
            first.append(copy(a, 0, me, sibling, src=ins[a]))
            first += [copy(a, 1 + j, me, (*chip, c), src=ins[a]) for j, chip in enumerate(chips)]
        for cp in first:
            cp.start()
        passed = []
        for j, chip in enumerate(chips):
            for a in range(n_arr):
                copy(a, 1 + j, (*chip, c), me).wait_recv()
                passed.append(copy(a, 4 + j, (*chip, c), sibling))
                passed[-1].start()
        for a in range(n_arr):
            copy(a, 0, sibling, me).wait_recv()
            for j, chip in enumerate(chips):
                copy(a, 4 + j, (*chip, 1 - c), me).wait_recv()
        for cp in first + passed:
            cp.wait_send()
        for cp in local:
            cp.wait()

    return pl.pallas_call(
        body, name=name,
        in_specs=[ANY] * n_arr, out_specs=[ANY] * n_arr,
        out_shape=[jax.ShapeDtypeStruct((N_DEV,) + a.shape, a.dtype) for a in arrays],
        scratch_shapes=[pltpu.SemaphoreType.DMA((n_arr * per,)), pltpu.SemaphoreType.DMA((n_arr * per,)),
                        pltpu.SemaphoreType.DMA((n_arr,))],
        compiler_params=pltpu.CompilerParams(has_side_effects=True),
    )(*arrays)


HBM = pl.BlockSpec(memory_space=pltpu.HBM)
SEM = pl.BlockSpec(memory_space=pltpu.SEMAPHORE)
DATAFLOW = pltpu.SideEffectType.DATAFLOW_SIDE_EFFECTING


def _own_block_in_place(block, me):
    land = lax.empty((N_DEV,) + block.shape, block.dtype)
    return lax.dynamic_update_slice(land, block[None], (me,) + (0,) * block.ndim)


def _peer_copies(srcs, lands, send, recv, gathers):
    x, y, c = _mesh_place()
    me = 4 * x + 2 * y + c
    out = []
    for a, (src, land) in enumerate(zip(srcs, lands)):
        for k in range(1, N_DEV):
            dev, idx = _peer(k)
            sem = a * (N_DEV - 1) + k - 1
            mine = src if gathers[a] else src.at[idx]
            put = pltpu.make_async_remote_copy(mine, land.at[me], send.at[sem], recv.at[sem],
                                               device_id=dev, device_id_type=pl.DeviceIdType.MESH)
            got = pltpu.make_async_remote_copy(mine, land.at[idx], send.at[sem], recv.at[sem],
                                               device_id=dev, device_id_type=pl.DeviceIdType.MESH)
            out.append((put, got))
    return out


def _exchange_start(srcs, lands, gather, after, name):
    n = len(srcs)
    extra = [] if after is None else [after]

    def body(*refs):
        srcs_, lands_ = refs[:n], refs[n:2 * n]
        send, recv = refs[2 * n + len(extra)], refs[2 * n + len(extra) + 1]
        for put, _ in _peer_copies(srcs_, lands_, send, recv, gather):
            put.start()
        refs[-1][...] = jnp.zeros_like(refs[-1])

    nsem = n * (N_DEV - 1)
    thru = [pltpu.HBM(a.shape, a.dtype) for a in list(srcs) + list(lands)]
    res = pl.pallas_call(
        body, name=name,
        out_shape=(pltpu.SemaphoreType.DMA((nsem,)), pltpu.SemaphoreType.DMA((nsem,)), *thru, jax.ShapeDtypeStruct((8, 128), F32)),
        in_specs=[HBM] * (2 * n) + [ANY] * len(extra),
        out_specs=(SEM, SEM, *([HBM] * (2 * n)), pl.BlockSpec(memory_space=pltpu.VMEM)),
        input_output_aliases={i: 2 + i for i in range(2 * n)},
        compiler_params=pltpu.CompilerParams(has_side_effects=DATAFLOW),
    )(*[pltpu.with_memory_space_constraint(a, pltpu.HBM) for a in list(srcs) + list(lands)], *extra)
    return res[0], res[1], list(res[2:2 + n]), list(res[2 + n:2 + 2 * n]), res[-1]


def _exchange_wait(send, recv, srcs, lands, gather, after, name):
    n = len(srcs)

    def body(*refs):
        srcs_, lands_, send_, recv_ = refs[:n], refs[n:2 * n], refs[2 * n], refs[2 * n + 1]
        for put, got in _peer_copies(srcs_, lands_, send_, recv_, gather):
            put.wait_send()
            got.wait_recv()

    thru = [pltpu.HBM(a.shape, a.dtype) for a in list(srcs) + list(lands)]
    res = pl.pallas_call(
        body, name=name, out_shape=tuple(thru),
        in_specs=[HBM] * (2 * n) + [SEM, SEM, ANY], out_specs=tuple([HBM] * (2 * n)),
        input_output_aliases={i: i for i in range(2 * n)},
        compiler_params=pltpu.CompilerParams(has_side_effects=DATAFLOW),
    )(*srcs, *lands, send, recv, after)
    return list(res[n:])


def _silu_rows(c):
    def body(c_ref, o_ref):
        v = c_ref[...]
        o_ref[...] = v * _sigmoid(v)

    return pl.pallas_call(body, name="cond", out_shape=jax.ShapeDtypeStruct(c.shape, F32))(c)


def _mod_slab(cond_all, w_ada, b_slab):
    def body(c_ref, w_ref, b_ref, o_ref):
        o_ref[...] = _dot(c_ref[...].astype(BF16), w_ref[0].astype(BF16), ((1,), (0,))) + b_ref[...]

    return pl.pallas_call(body, name="mod_slab",
                          out_shape=jax.ShapeDtypeStruct((cond_all.shape[0], w_ada.shape[2]), F32),
                          compiler_params=pltpu.CompilerParams(vmem_limit_bytes=VMEM_LIMIT))(cond_all, w_ada, b_slab)


def _ada_grad(cond_all, dmod_cols):
    def body(c_ref, d_ref, o_ref):
        o_ref[...] = _dot(c_ref[...].astype(BF16), d_ref[...].astype(BF16), ((0,), (0,)))

    return pl.pallas_call(body, name="ada_grad",
                          out_shape=jax.ShapeDtypeStruct((cond_all.shape[1], dmod_cols.shape[1]), F32),
                          compiler_params=pltpu.CompilerParams(vmem_limit_bytes=VMEM_LIMIT))(cond_all, dmod_cols)


def _adam_math(g, w, m, v):
    m2 = B1 * m + (1.0 - B1) * g
    v2 = B2 * v + (1.0 - B2) * (g * g)
    m_hat = m2 / (1.0 - B1 ** STEP)
    v_hat = v2 / (1.0 - B2 ** STEP)
    return -LR * (m_hat / (jnp.sqrt(v_hat) + ADAM_EPS) + WD * w), m2, v2


def _adamw(parts, w, m, v, name):
    n, rows, cols = parts.shape
    tr = max([p for p in range(16, 513, 16) if rows % p == 0] or [rows])

    def body(p_ref, w_ref, m_ref, v_ref, g_ref, d_ref, m2_ref, v2_ref):
        g = p_ref[0].astype(F32)
        for s in range(1, n):
            g = g + p_ref[s].astype(F32)
        g_ref[0] = g
        d_ref[0], m2_ref[0], v2_ref[0] = _adam_math(g, w_ref[0], m_ref[0], v_ref[0])

    blk = pl.BlockSpec((1, tr, cols), lambda i: (0, i, 0))
    return pl.pallas_call(
        body, name=name, grid=(rows // tr,),
        in_specs=[pl.BlockSpec((n, tr, cols), lambda i: (0, i, 0)), blk, blk, blk],
        out_specs=[blk] * 4, out_shape=[jax.ShapeDtypeStruct((1, rows, cols), F32)] * 4,
        compiler_params=_cp("parallel"),
    )(parts, w, m, v)


ROW_PARAMS = (("g_norm1", D), ("g_cq", Q_LORA), ("g_ckv", KV_LORA), ("g_out_a", D_A), ("g_out_b", D_A), ("g_norm2", D),
              ("g_final", D))
LOSS_ROW = N_MOD + len(ROW_PARAMS)
PAY_ROWS = 16
NCOL = N_MOD * D // N_DEV


def _pack_small(dmods, rows, loss_cols):
    nb = dmods[0].shape[0]
    nrow = len(ROW_PARAMS)

    def body(*refs):
        dm, rw, loss_ref, pay_ref, blk_ref = refs[:N_MOD], refs[N_MOD:N_MOD + nrow], refs[N_MOD + nrow], refs[-2], refs[-1]
        pay_ref[...] = jnp.zeros_like(pay_ref)
        for k in range(N_MOD):
            tot = dm[k][0]
            for b in range(1, nb):
                tot = tot + dm[k][b]
            pay_ref[k:k + 1, :] = tot
        for i, (_, n) in enumerate(ROW_PARAMS):
            pay_ref[N_MOD + i:N_MOD + i + 1, :n] = rw[i][...]
        pay_ref[LOSS_ROW:LOSS_ROW + 1, :] = loss_ref[...]
        for j in range(N_DEV):
            done = 0
            while done < NCOL:
                seg, off = divmod(j * NCOL + done, D)
                ln = min(NCOL - done, D - off)
                for b in range(nb):
                    blk_ref[j, b:b + 1, done:done + ln] = dm[seg][b][:, off:off + ln]
                done += ln

    return pl.pallas_call(
        body, name="pack_small",
        out_shape=[jax.ShapeDtypeStruct((PAY_ROWS, D), F32), jax.ShapeDtypeStruct((N_DEV, nb, NCOL), F32)],
    )(*dmods, *rows, loss_cols)


def _small_update(pay, rel, ws, ms, vs):
    n_par = len(ws)

    def body(*refs):
        pay_ref, rel_ref = refs[:2]
        w_refs, m_refs, v_refs = (refs[2 + s * n_par:2 + (s + 1) * n_par] for s in range(3))
        outs, loss_ref = refs[2 + 3 * n_par:-1], refs[-1]
        tot, rtot = pay_ref[0], rel_ref[0]
        for s in range(1, N_DEV):
            tot, rtot = tot + pay_ref[s], rtot + rel_ref[s]

        def update(p, g, sl):
            outs[4 * p][:, sl] = g
            outs[4 * p + 1][:, sl], outs[4 * p + 2][:, sl], outs[4 * p + 3][:, sl] = _adam_math(
                g, w_refs[p][:, sl], m_refs[p][:, sl], v_refs[p][:, sl])

        for k in range(N_MOD):
            update(0, tot[k:k + 1, :], slice(k * D, (k + 1) * D))
        for i, (_, n) in enumerate(ROW_PARAMS):
            update(1 + i, tot[N_MOD + i:N_MOD + i + 1, :n], slice(0, n))
        update(n_par - 1, rtot, slice(0, H))
        loss_ref[...] = jnp.broadcast_to((0.5 / D) * jnp.sum(tot[LOSS_ROW:LOSS_ROW + 1, :]), loss_ref.shape)

    shapes = [jax.ShapeDtypeStruct(w.shape, F32) for w in ws for _ in range(4)]
    res = pl.pallas_call(
        body, name="small_update", out_shape=shapes + [jax.ShapeDtypeStruct((8, 128), F32)],
    )(pay, rel, *ws, *ms, *vs)
    return [tuple(res[4 * p:4 * p + 4]) for p in range(n_par)], res[-1]


def _cols_from_blocks(g):
    return jnp.transpose(g, (1, 0, 2)).reshape(g.shape[1], N_DEV * g.shape[2])


def _cols_to_blocks(w):
    r, c = w.shape
    return jnp.transpose(w.reshape(r, N_DEV, c // N_DEV), (1, 0, 2))


def _pad_w_in(wt):
    z = jnp.zeros((NOPE, wt.shape[1]), wt.dtype)
    return jnp.concatenate([wt[:P_IN - ROPE], z, wt[P_IN - ROPE:], z[:HP - NOPE - ROPE]], axis=0)


def _unpad_w_in(gt):
    k0 = P_IN - ROPE + NOPE
    return jnp.concatenate([gt[:P_IN - ROPE], gt[k0:k0 + ROPE]], axis=0)


def _pad_w_uq(wt):
    return jnp.pad(wt, ((0, 0), (0, HP - NOPE - ROPE), (0, 0))).reshape(H * HP, Q_LORA)


def _unpad_w_uq(gt):
    return gt.reshape(H, HP, Q_LORA)[:, :NOPE + ROPE]


def _split_w_ukv(w):
    w4 = w.reshape(KV_LORA, H // 2, 2, HP)
    z = jnp.zeros((KV_LORA, H // 2, NOPE), w.dtype)
    kn, vv = w4[..., :NOPE], w4[..., NOPE:]
    w_k = jnp.stack([jnp.concatenate([kn[:, :, 0], z], -1), jnp.concatenate([kn[:, :, 1], z], -1)], axis=2)
    w_v = jnp.stack([jnp.concatenate([vv[:, :, 0], z], -1), jnp.concatenate([z, vv[:, :, 1]], -1)], axis=2)
    return w_k.reshape(KV_LORA, H * HP), w_v.reshape(KV_LORA, H * HP)


def _join_w_ukv(g_k, g_v):
    gk = g_k.reshape(KV_LORA, H // 2, 2, HP)
    gv = g_v.reshape(KV_LORA, H // 2, 2, HP)
    even = jnp.concatenate([gk[:, :, 0, :NOPE], gv[:, :, 0, :VDIM]], -1)
    odd = jnp.concatenate([gk[:, :, 1, :NOPE], gv[:, :, 1, VDIM:]], -1)
    return jnp.stack([even, odd], axis=2).reshape(KV_LORA, H * HP)


def _rope_tables():
    half = ROPE // 2
    inv = np.float32(ROPE_THETA) ** (-np.arange(half, dtype=np.float32) / np.float32(half))
    ang = np.arange(S, dtype=np.float32)[:, None] * inv[None, :].astype(np.float32)
    cos, sin = np.cos(ang).astype(np.float32), np.sin(ang).astype(np.float32)
    ones, zeros = np.ones((S, NOPE), np.float32), np.zeros((S, NOPE), np.float32)
    tail1, tail0 = np.ones((S, HP - NOPE - ROPE), np.float32), np.zeros((S, HP - NOPE - ROPE), np.float32)
    zh = np.zeros((S, half), np.float32)
    c = np.concatenate([ones, cos, cos, tail1], axis=1)
    sm = np.concatenate([zeros, -sin, zh, tail0], axis=1)
    sp = np.concatenate([zeros, zh, sin, tail0], axis=1)
    return jnp.asarray(c), jnp.asarray(sm), jnp.asarray(sp)


def _local_step(x, mod, target, g_norm1, w_in_p, g_cq, w_uq_p, g_ckv, w_k, w_v, rel_bias, g_out_a, g_out_b, w_out,
                g_norm2, w_ffn_in, w_ffn_out, g_final, late_weights=None, on_ffn_grads=None, on_last_grads=None):
    nb = x.shape[0] // S
    sh1, sc1, g1, sh2, sc2, g2 = (mod[:, n].reshape(nb, 1, D) for n in range(N_MOD))
    rc, rsm, rsp = _rope_tables()
    biasm = _band_bias(rel_bias)

    h1 = _pre1(x, g_norm1, sc1, sh1)
    proj = _mm_nt(h1, w_in_p, F32, "proj")
    q, k, v, cqn, ckvn = _mla_pre(proj, g_cq, g_ckv, w_uq_p, w_k, w_v, rc, rsm, rsp)
    out_b, lse_b = _mla_fwd(q, k, v)
    out_a, lse_a = _dil_fwd(proj, biasm)
    y = _post_attn(out_a, out_b, g_out_a, g_out_b)
    if late_weights is not None:
        w_out, w_ffn_in, w_ffn_out = late_weights(y)
    mix = _mm_nn(y, w_out, BF16, "mix")
    x2, h2 = _resid_norm2(x, mix, g1, g_norm2, sc2, sh2)
    ffn_g, ffn_u, act = _ffn_in(h2, w_ffn_in)
    f = _mm_nn(act, w_ffn_out, BF16, "ffn_out")
    dx3, df, loss_cols, dg_final, dg2 = _final(x2, f, g2, g_final, target)

    dg_, du_ = _d_act(df, w_ffn_out, ffn_g, ffn_u)
    gw_ffn_out = _mm_tn_rows([act], df, "gw_ffn_out")
    dh2 = _d_h2(dg_, du_, w_ffn_in)
    gw_ffn_in = _mm_tn_rows([dg_, du_], h2, "gw_ffn_in")
    dx2, dsh2, dsc2, dg_norm2, dg1, dmix = _norm_bwd(x2, dh2, dx3, g_norm2, sc2, gate=(mix, g1))
    dy = _mm_nt(dmix, w_out, BF16, "d_y")
    gw_out = _mm_tn(y, [dmix], "gw_out")
    if on_ffn_grads is not None:
        g_out_a = g_out_a + on_ffn_grads(gw_ffn_in, gw_ffn_out, gw_out)
    dout_a, dout_b, dg_out_a, dg_out_b = _post_attn_bwd(dy, out_a, out_b, g_out_a, g_out_b)
    dq_b, dk_b, dv_b = _mla_bwd(q, k, v, out_b, dout_b, lse_b)
    dq_a, dk_a, dv_a, dlogits = _dil_bwd(proj, biasm, out_a, dout_a, lse_a)
    g_rel = _rel_bias_grad(dlogits)
    dqr, dproj, dg_cq, dg_ckv = _mla_pre_bwd(proj, dq_b, dk_b, dv_b, (dq_a, dk_a, dv_a), g_cq, g_ckv, w_uq_p, w_k, w_v,
                                             rc, rsm, rsp)
    gw_uq = _mm_tn(dqr, [cqn], "gw_uq")
    gw_k, gw_v = _mm_tn(ckvn, [dk_b, dv_b], "gw_kv")
    gw_in = _mm_tn_rows([dproj], h1, "gw_in")
    if on_last_grads is not None:
        started = on_last_grads(dict(w_in=gw_in, w_uq=gw_uq, w_k=gw_k, w_v=gw_v))
    else:
        started = None
    dh1 = _mm_nn(dproj, w_in_p, BF16, "d_h1", after=started)
    grad_x, dsh1, dsc1, dg_norm1 = _norm_bwd(x, dh1, dx2, g_norm1, sc1)

    dmod = [dsh1, dsc1, dg1, dsh2, dsc2, dg2]
    small = dict(g_norm1=dg_norm1, g_cq=dg_cq, g_ckv=dg_ckv, rel_bias=g_rel, g_out_a=dg_out_a, g_out_b=dg_out_b,
                 g_norm2=dg_norm2, g_final=dg_final)
    big = dict(w_in=gw_in, w_uq=gw_uq, w_k=gw_k, w_v=gw_v, w_out=gw_out, w_ffn_in=gw_ffn_in, w_ffn_out=gw_ffn_out)
    return grad_x, dmod, loss_cols, small, big


def kernel(x, c, w_ada, b_ada, g_norm1, w_in, g_cq, w_uq, g_ckv, w_ukv, rel_bias, g_out_a, g_out_b, w_out, g_norm2, w_ffn_in, w_ffn_out, g_final, loss_target, m_w_ada, m_b_ada, m_g_norm1, m_w_in, m_g_cq, m_w_uq, m_g_ckv, m_w_ukv, m_rel_bias, m_g_out_a, m_g_out_b, m_w_out, m_g_norm2, m_w_ffn_in, m_w_ffn_out, m_g_final, v_w_ada, v_b_ada, v_g_norm1, v_w_in, v_g_cq, v_w_uq, v_g_ckv, v_w_ukv, v_rel_bias, v_g_out_a, v_g_out_b, v_w_out, v_g_norm2, v_w_ffn_in, v_w_ffn_out, v_g_final):
    nb = x.shape[0]
    t = nb * S
    xt, tt = x.reshape(t, D), loss_target.reshape(t, D)
    me = 4 * lax.axis_index("x") + 2 * lax.axis_index("y") + lax.axis_index("c")

    early = [jnp.swapaxes(w_in, 1, 2)[0], jnp.swapaxes(w_uq, 1, 2)[0], w_ukv[0]]
    gathered = _gather_two_level([_silu_rows(c)] + [s.astype(BF16) for s in early], "gather_weights")
    cond_all = gathered[0].reshape(N_DEV * nb, D)
    w_in_t = gathered[1].reshape(P_IN, D)
    w_ukv_f = _cols_from_blocks(gathered[3])
    w_k, w_v = _split_w_ukv(w_ukv_f)

    ncol = N_MOD * D // N_DEV
    b_slab = lax.dynamic_slice(b_ada, (0, me * ncol), (1, ncol))
    slab = _mod_slab(cond_all, w_ada, b_slab)
    (mod_rows,) = _exchange([slab.reshape(N_DEV, nb, ncol)], [False], "scatter_mod")
    mod = jnp.transpose(mod_rows, (1, 0, 2)).reshape(nb, N_MOD, D)

    late = [s.astype(BF16) for s in (w_out[0], jnp.swapaxes(w_ffn_in, 1, 2)[0], w_ffn_out[0])]
    late_send, late_recv, late_src, late_land, late_token = _exchange_start(
        late, [_own_block_in_place(s, me) for s in late], [True] * 3, mod_rows, "gather_late_start")
    g_norm1_t = g_norm1 + late_token[:1, :1]

    def late_weights(after):
        w_out_g, w_ffn_in_g, w_ffn_out_g = _exchange_wait(late_send, late_recv, late_src, late_land, [True] * 3, after,
                                                          "gather_late_wait")
        return w_out_g.reshape(D, D), w_ffn_in_g.reshape(2 * D_FF, D), w_ffn_out_g.reshape(D_FF, D)

    flight = {}

    def start_grads(key, src, name):
        land = [_own_block_in_place(lax.dynamic_index_in_dim(s, me, 0, keepdims=False), me) for s in src]
        send, recv, src, land, token = _exchange_start(src, land, [False] * len(src), None, name)
        flight[key] = (send, recv, src, land)
        return token[:1, :1]

    def on_ffn_grads(gw_ffn_in, gw_ffn_out, gw_out):
        return start_grads("ffn", [gw_ffn_in.reshape(N_DEV, 2 * D_FF // N_DEV, D), gw_ffn_out.reshape(N_DEV, D_FF // N_DEV, D),
                                   gw_out.reshape(N_DEV, D // N_DEV, D)], "exchange_ffn_start")

    def on_last_grads(gw):
        return start_grads("rest", [_unpad_w_in(gw["w_in"]).reshape(N_DEV, P_IN // N_DEV, D),
                                    _unpad_w_uq(gw["w_uq"]),
                                    _cols_to_blocks(_join_w_ukv(gw["w_k"], gw["w_v"]))], "exchange_rest_start")

    grad_x, dmod, loss_cols, small, _ = _local_step(
        xt, mod, tt, g_norm1_t, _pad_w_in(w_in_t), g_cq, _pad_w_uq(gathered[2]), g_ckv, w_k, w_v, rel_bias, g_out_a, g_out_b,
        None, g_norm2, None, None, g_final.reshape(1, D), late_weights=late_weights, on_ffn_grads=on_ffn_grads,
        on_last_grads=on_last_grads)

    upd = {}

    def land_and_update(key, names, after, name):
        got = _exchange_wait(*flight[key], [False] * len(names), after, name)
        for n, p in zip(names, got):
            w, m, v = big[n]
            upd[n] = _adamw(p, w, m, v, "adamw_" + n)

    def flip(a):
        return jnp.swapaxes(a, 1, 2)

    big = dict(w_in=(flip(w_in), flip(m_w_in), flip(v_w_in)), w_uq=(flip(w_uq), flip(m_w_uq), flip(v_w_uq)),
               w_ukv=(w_ukv, m_w_ukv, v_w_ukv),
               w_out=(w_out, m_w_out, v_w_out), w_ffn_in=(flip(w_ffn_in), flip(m_w_ffn_in), flip(v_w_ffn_in)),
               w_ffn_out=(w_ffn_out, m_w_ffn_out, v_w_ffn_out))
    land_and_update("ffn", ["w_ffn_in", "w_ffn_out", "w_out"], grad_x, "exchange_ffn_wait")
    land_and_update("rest", ["w_in", "w_uq", "w_ukv"], upd["w_out"][0], "exchange_rest_wait")
    for n in ("w_in", "w_uq", "w_ffn_in"):
        upd[n] = tuple(flip(a) for a in upd[n])

    mine, dmod_blocks = _pack_small(dmod, [small[n] for n, _ in ROW_PARAMS], loss_cols)
    dmod_cols, pay, rel = _exchange([dmod_blocks, mine, small["rel_bias"]], [False, True, True], "exchange_small",
                                    after=upd["w_ukv"][0])
    g_ada = _ada_grad(cond_all, dmod_cols.reshape(N_DEV * nb, ncol))
    upd["w_ada"] = _adamw(g_ada[None], w_ada, m_w_ada, v_w_ada, "adamw_w_ada")
    row = lambda a: a.reshape(1, D)
    small_names = ["b_ada"] + [n for n, _ in ROW_PARAMS] + ["rel_bias"]
    small_w = [b_ada, g_norm1, g_cq, g_ckv, g_out_a, g_out_b, g_norm2, row(g_final), rel_bias]
    small_m = [m_b_ada, m_g_norm1, m_g_cq, m_g_ckv, m_g_out_a, m_g_out_b, m_g_norm2, row(m_g_final), m_rel_bias]
    small_v = [v_b_ada, v_g_norm1, v_g_cq, v_g_ckv, v_g_out_a, v_g_out_b, v_g_norm2, row(v_g_final), v_rel_bias]
    small_upd, loss8 = _small_update(pay, rel, small_w, small_m, small_v)
    upd.update(zip(small_names, small_upd))

    order = ["w_ada", "b_ada", "g_norm1", "w_in", "g_cq", "w_uq", "g_ckv", "w_ukv", "rel_bias", "g_out_a", "g_out_b",
             "w_out", "g_norm2", "w_ffn_in", "w_ffn_out", "g_final"]
    like = dict(g_final=g_final)
    outs = [loss8[0, 0], grad_x.reshape(x.shape)]
    for part in range(4):
        for n in order:
            val = upd[n][part]
            outs.append(val.reshape(like[n].shape) if n in like else val)
    return tuple(outs)
```

```python
import numpy as np
import jax
import jax.numpy as jnp
from jax import lax
from jax.experimental import pallas as pl
from jax.experimental.pallas import tpu as pltpu

F32, BF16 = jnp.float32, jnp.bfloat16

N_DEV = 8
D = 1024
S = 2048
H = 8
E_A = 64
D_A = H * E_A
Q_LORA, KV_LORA = 384, 256
NOPE, ROPE, VDIM = 64, 32, 64
HP = 128
P_IN = 3 * D_A + Q_LORA + KV_LORA + ROPE
P_PAD = 3 * D_A + Q_LORA + KV_LORA + HP
TAIL0 = 3 * D_A
TAIL = P_PAD - TAIL0
D_FF = 2816
N_MOD = 6
EPS = 1e-6
NEG = -1e30
BLK = 128
DILATIONS = (1, 4, 16)
N_BUCKETS, MAX_DISTANCE = 32, 2048
ROPE_THETA = 10000.0
SCALE_A = E_A ** -0.5
SCALE_B = (NOPE + ROPE) ** -0.5
B1, B2, LR, ADAM_EPS, WD, STEP = 0.9, 0.999, 0.001, 1e-8, 0.01, 10
VMEM_LIMIT = 56 * 1024 * 1024


def _cp(*sem):
    return pltpu.CompilerParams(dimension_semantics=sem, vmem_limit_bytes=VMEM_LIMIT)


def _pick(n, prefs):
    for p in prefs:
        if n % p == 0:
            return p
    raise ValueError(f"no tile of {prefs} divides {n}")


OPERAND_BYTES = 6 * 1024 * 1024


def _pick_rows(m, k):
    return _pick(m, [p for p in (1024, 512, 256, 128, 16) if p * k * 2 <= OPERAND_BYTES])


MATMUL_BYTES = 40 * 1024 * 1024


def _stream_rows(m, fixed, per_row):
    return _pick(m, [p for p in (4096, 2048, 1024, 512, 256, 128, 16) if fixed + p * per_row <= MATMUL_BYTES])


def _dot(a, b, dims):
    return lax.dot_general(a, b, (dims, ((), ())), preferred_element_type=F32)


def _mm_nn(a, b, out_dtype, name, after=None):
    m, k = a.shape
    n = b.shape[1]
    tn = _pick(n, (512, 256, 384, 128))
    tm = _stream_rows(m, 4 * k * tn, 4 * k + (2 * jnp.dtype(out_dtype).itemsize + 4) * tn)

    def body(a_ref, b_ref, *rest):
        o_ref = rest[-1]
        o_ref[...] = _dot(a_ref[...], b_ref[...], ((1,), (0,))).astype(o_ref.dtype)

    extra = [] if after is None else [after]
    return pl.pallas_call(
        body, name=name, grid=(m // tm, n // tn),
        in_specs=[pl.BlockSpec((tm, k), lambda i, j: (i, 0)), pl.BlockSpec((k, tn), lambda i, j: (0, j))] + [ANY] * len(extra),
        out_specs=pl.BlockSpec((tm, tn), lambda i, j: (i, j)),
        out_shape=jax.ShapeDtypeStruct((m, n), out_dtype),
        compiler_params=_cp("parallel", "parallel"),
    )(a, b, *extra)


def _mm_nt(a, b, out_dtype, name, after=None):
    m, k = a.shape
    n = b.shape[0]
    tn = _pick(n, (512, 256, 384, 128))
    tm = _stream_rows(m, 4 * k * tn, 4 * k + (2 * jnp.dtype(out_dtype).itemsize + 4) * tn)

    def body(a_ref, b_ref, *rest):
        o_ref = rest[-1]
        o_ref[...] = _dot(a_ref[...], b_ref[...], ((1,), (1,))).astype(o_ref.dtype)

    extra = [] if after is None else [after]
    return pl.pallas_call(
        body, name=name, grid=(m // tm, n // tn),
        in_specs=[pl.BlockSpec((tm, k), lambda i, j: (i, 0)), pl.BlockSpec((tn, k), lambda i, j: (j, 0))] + [ANY] * len(extra),
        out_specs=pl.BlockSpec((tm, tn), lambda i, j: (i, j)),
        out_shape=jax.ShapeDtypeStruct((m, n), out_dtype),
        compiler_params=_cp("parallel", "parallel"),
    )(a, b, *extra)


def _mm_tn(a, bs, name):
    t, m = a.shape
    n = bs[0].shape[1]
    nb_ = len(bs)
    tc = _pick(t, (512, 16))
    tn = _pick(n, (512, 384, 256, 128))
    tm = _pick(m, [p for p in (1024, 512, 384, 256, 128) if (3 * p + 2 * nb_ * tn) * t * 2 <= VMEM_LIMIT - 2 * OPERAND_BYTES])
    if tm <= 256 and nb_ * n * t * 2 <= 2 * OPERAND_BYTES:
        tn = n

    def body(*refs):
        a_ref, b_refs, o_refs, at_ref = refs[0], refs[1:1 + nb_], refs[1 + nb_:1 + 2 * nb_], refs[-1]

        @pl.when(pl.program_id(1) == 0)
        def _():
            def chunk(c, _):
                rows = pl.ds(pl.multiple_of(c * tc, tc), tc)
                at_ref[:, rows] = a_ref[rows, :].T
                return 0

            lax.fori_loop(0, t // tc, chunk, 0)

        for b_ref, o_ref in zip(b_refs, o_refs):
            o_ref[...] = _dot(at_ref[...], b_ref[...], ((1,), (0,))).astype(BF16)

    res = pl.pallas_call(
        body, name=name, grid=(m // tm, n // tn),
        in_specs=[pl.BlockSpec((t, tm), lambda i, j: (0, i))] + [pl.BlockSpec((t, tn), lambda i, j: (0, j))] * nb_,
        out_specs=[pl.BlockSpec((tm, tn), lambda i, j: (i, j))] * nb_,
        out_shape=[jax.ShapeDtypeStruct((m, n), BF16)] * nb_,
        scratch_shapes=[pltpu.VMEM((tm, t), BF16)],
        compiler_params=_cp("parallel", "arbitrary"),
    )(a, *bs)
    return res[0] if nb_ == 1 else res


def _mm_tn_rows(a_list, b, name):
    t, m = a_list[0].shape
    n = b.shape[1]
    na = len(a_list)
    tc, tm = _pick(t, (512, 16)), _pick(m, (256, 128))
    nblk = m // tm

    def body(*refs):
        a_refs, b_ref, o_ref, bt_ref, r_ref = refs[:na], refs[na], refs[na + 1], refs[na + 2], refs[na + 3]
        i = pl.program_id(0)

        @pl.when(i == 0)
        def _():
            def chunk(c, _):
                rows = pl.ds(pl.multiple_of(c * tc, tc), tc)
                bt_ref[:, rows] = b_ref[rows, :].T
                return 0

            lax.fori_loop(0, t // tc, chunk, 0)

        for s, a_ref in enumerate(a_refs):
            @pl.when((i >= s * nblk) & (i < (s + 1) * nblk))
            def _(a_ref=a_ref):
                r_ref[...] = _dot(bt_ref[...], a_ref[...], ((1,), (0,)))
                o_ref[...] = r_ref[...].T.astype(BF16)

    return pl.pallas_call(
        body, name=name, grid=(na * nblk,),
        in_specs=[pl.BlockSpec((t, tm), lambda i, s=s: (0, jnp.clip(i - s * nblk, 0, nblk - 1))) for s in range(na)]
        + [pl.BlockSpec((t, n), lambda i: (0, 0))],
        out_specs=pl.BlockSpec((tm, n), lambda i: (i, 0)),
        out_shape=jax.ShapeDtypeStruct((na * m, n), BF16),
        scratch_shapes=[pltpu.VMEM((n, t), BF16), pltpu.VMEM((n, tm), F32)],
        compiler_params=_cp("arbitrary"),
    )(*a_list, b)


EPI = 256


def _silu_parts(g):
    sg = 0.5 * jnp.tanh(0.5 * g) + 0.5
    return sg, g * sg


def _ffn_in(h2, wt):
    t, k = h2.shape
    tn = _pick(D_FF, (256, 128))
    tm = _stream_rows(t, 8 * k * tn, 4 * k + (3 * 2 * 2 + 2 * 4) * tn)
    nj = D_FF // tn

    def body(h_ref, wg_ref, wu_ref, g_ref, u_ref, a_ref):
        hv = h_ref[...]
        g_all = _dot(hv, wg_ref[...], ((1,), (1,)))
        u_all = _dot(hv, wu_ref[...], ((1,), (1,)))
        for r in range(tm // EPI):
            rows = slice(r * EPI, (r + 1) * EPI)
            g, u = g_all[rows], u_all[rows]
            g_ref[rows, :] = g.astype(BF16)
            u_ref[rows, :] = u.astype(BF16)
            a_ref[rows, :] = (_silu_parts(g)[1] * u).astype(BF16)

    blk = pl.BlockSpec((tm, tn), lambda i, j: (i, j))
    return pl.pallas_call(
        body, name="ffn_in", grid=(t // tm, nj),
        in_specs=[pl.BlockSpec((tm, k), lambda i, j: (i, 0)), pl.BlockSpec((tn, k), lambda i, j: (j, 0)),
                  pl.BlockSpec((tn, k), lambda i, j: (j + nj, 0))],
        out_specs=[blk] * 3, out_shape=[jax.ShapeDtypeStruct((t, D_FF), BF16)] * 3,
        compiler_params=_cp("parallel", "parallel"),
    )(h2, wt, wt)


def _d_act(df, w, g, u):
    t, k = df.shape
    tn = _pick(D_FF, (256, 128))
    tm = min(2048, _stream_rows(t, 4 * k * tn, 4 * k + (4 * 2 * 2 + 4) * tn))

    def body(df_ref, w_ref, g_ref, u_ref, dg_ref, du_ref):
        da_all = _dot(df_ref[...], w_ref[...], ((1,), (1,)))
        for r in range(tm // EPI):
            rows = slice(r * EPI, (r + 1) * EPI)
            da = da_all[rows]
            gv = g_ref[rows, :].astype(F32)
            sg, silu = _silu_parts(gv)
            dg_ref[rows, :] = ((da * u_ref[rows, :].astype(F32)) * (sg + silu * (1.0 - sg))).astype(BF16)
            du_ref[rows, :] = (da * silu).astype(BF16)

    blk = pl.BlockSpec((tm, tn), lambda i, j: (i, j))
    return pl.pallas_call(
        body, name="d_act", grid=(t // tm, D_FF // tn),
        in_specs=[pl.BlockSpec((tm, k), lambda i, j: (i, 0)), pl.BlockSpec((tn, k), lambda i, j: (j, 0)), blk, blk],
        out_specs=[blk] * 2, out_shape=[jax.ShapeDtypeStruct((t, D_FF), BF16)] * 2,
        compiler_params=_cp("parallel", "parallel"),
    )(df, w, g, u)


def _d_h2(dg, du, wt):
    t = dg.shape[0]
    n = wt.shape[1]
    tm, tn = _pick_rows(t, D_FF), _pick(n, (512, 256, 128))

    def body(dg_ref, du_ref, wg_ref, wu_ref, o_ref):
        o_ref[...] = (_dot(dg_ref[...], wg_ref[...], ((1,), (0,)))
                      + _dot(du_ref[...], wu_ref[...], ((1,), (0,)))).astype(BF16)

    return pl.pallas_call(
        body, name="d_h2", grid=(t // tm, n // tn),
        in_specs=[pl.BlockSpec((tm, D_FF), lambda i, j: (i, 0)), pl.BlockSpec((tm, D_FF), lambda i, j: (i, 0)),
                  pl.BlockSpec((D_FF, tn), lambda i, j: (0, j)), pl.BlockSpec((D_FF, tn), lambda i, j: (1, j))],
        out_specs=pl.BlockSpec((tm, tn), lambda i, j: (i, j)),
        out_shape=jax.ShapeDtypeStruct((t, n), BF16),
        compiler_params=_cp("parallel", "parallel"),
    )(dg, du, wt, wt)


TM = 1024


def _row(w):
    return pl.BlockSpec((TM, w), lambda i: (i, 0))


def _row_at(w, col):
    return pl.BlockSpec((TM, w), lambda i: (i, col))


def _vec(w):
    return pl.BlockSpec((1, w), lambda i: (0, 0))


def _per_ex(w):
    return pl.BlockSpec((1, 1, w), lambda i: (i // (S // TM), 0, 0))


def _pos(w):
    return pl.BlockSpec((TM, w), lambda i: (i % (S // TM), 0))


def _full(shape):
    return pl.BlockSpec(shape, lambda i: (0,) * len(shape))


def _rms(x):
    return lax.rsqrt(jnp.mean(x * x, axis=-1, keepdims=True) + EPS)


def _rms_bwd(n, r, dn):
    return r * (dn - n * jnp.mean(dn * n, axis=-1, keepdims=True))


def _colsum(v):
    return jnp.sum(v, axis=0, keepdims=True)


def _acc_first(i, ref, val, every=None):
    first = (i == 0) if every is None else (i % every == 0)

    @pl.when(first)
    def _():
        ref[...] = jnp.zeros_like(ref)

    ref[...] += val.reshape(ref.shape)


def _pre1(x, g, sc, sh):
    t = x.shape[0]

    def body(x_ref, g_ref, sc_ref, sh_ref, h_ref):
        xv = x_ref[...]
        n = xv * _rms(xv)
        h_ref[...] = ((n * g_ref[...]) * (1.0 + sc_ref[0]) + sh_ref[0]).astype(BF16)

    return pl.pallas_call(
        body, name="pre1", grid=(t // TM,),
        in_specs=[_row(D), _vec(D), _per_ex(D), _per_ex(D)],
        out_specs=_row(D), out_shape=jax.ShapeDtypeStruct((t, D), BF16),
        compiler_params=_cp("parallel"),
    )(x, g, sc, sh)


def _rope_fwd(v, c, sm, sp):
    return v * c + pltpu.roll(v, HP - ROPE // 2, 1) * sm + pltpu.roll(v, ROPE // 2, 1) * sp


def _rope_bwd(dv, c, sm, sp):
    return dv * c + pltpu.roll(dv * sm, ROPE // 2, 1) + pltpu.roll(dv * sp, HP - ROPE // 2, 1)


def _mla_pre(proj, g_cq, g_ckv, w_uq, w_k, w_v, rc, rsm, rsp):
    t = proj.shape[0]

    def body(tail_ref, gq_ref, gkv_ref, wuq_ref, wk_ref, wv_ref, c_ref, sm_ref, sp_ref,
             q_ref, k_ref, v_ref, cqn_ref, ckvn_ref):
        tail = tail_ref[...]
        cq, ckv, kr = tail[:, :Q_LORA], tail[:, Q_LORA:Q_LORA + KV_LORA], tail[:, Q_LORA + KV_LORA:]
        cqn = (cq * _rms(cq) * gq_ref[...]).astype(BF16)
        ckvn = (ckv * _rms(ckv) * gkv_ref[...]).astype(BF16)
        cqn_ref[...] = cqn
        ckvn_ref[...] = ckvn
        c, sm, sp = c_ref[...], sm_ref[...], sp_ref[...]
        q = _dot(cqn, wuq_ref[...], ((1,), (1,)))
        kn = _dot(ckvn, wk_ref[...], ((1,), (0,)))
        v_ref[...] = _dot(ckvn, wv_ref[...], ((1,), (0,))).astype(BF16)
        krr = _rope_fwd(kr, c, sm, sp)
        for h in range(H):
            sl = slice(h * HP, (h + 1) * HP)
            q_ref[:, sl] = _rope_fwd(q[:, sl], c, sm, sp).astype(BF16)
            k_ref[:, sl] = (kn[:, sl] + krr).astype(BF16)

    wide = H * HP
    return pl.pallas_call(
        body, name="mla_pre", grid=(t // TM,),
        in_specs=[_row_at(TAIL, TAIL0 // TAIL), _vec(Q_LORA), _vec(KV_LORA), _full((wide, Q_LORA)),
                  _full((KV_LORA, wide)), _full((KV_LORA, wide)), _pos(HP), _pos(HP), _pos(HP)],
        out_specs=[_row(wide), _row(wide), _row(wide), _row(Q_LORA), _row(KV_LORA)],
        out_shape=[jax.ShapeDtypeStruct((t, wide), BF16)] * 3
        + [jax.ShapeDtypeStruct((t, Q_LORA), BF16), jax.ShapeDtypeStruct((t, KV_LORA), BF16)],
        compiler_params=_cp("parallel"),
    )(proj, g_cq, g_ckv, w_uq, w_k, w_v, rc, rsm, rsp)


def _mla_pre_bwd(proj, dq_, dk_, dv_, dqkv_a, g_cq, g_ckv, w_uq, w_k, w_v, rc, rsm, rsp):
    t = proj.shape[0]
    wide = H * HP

    def body(tail_ref, dq_ref, dk_ref, dv_ref, dqa_ref, dka_ref, dva_ref, gq_ref, gkv_ref, wuq_ref, wk_ref, wv_ref,
             c_ref, sm_ref, sp_ref, dqo_ref, dproj_ref, dgq_ref, dgkv_ref):
        i = pl.program_id(0)
        for n, src in enumerate((dqa_ref, dka_ref, dva_ref)):
            dproj_ref[:, n * D_A:(n + 1) * D_A] = src[...]
        dtail_ref = dproj_ref.at[:, TAIL0:]
        tail = tail_ref[...]
        cq, ckv = tail[:, :Q_LORA], tail[:, Q_LORA:Q_LORA + KV_LORA]
        c, sm, sp = c_ref[...], sm_ref[...], sp_ref[...]
        dkr = jnp.zeros((TM, HP), F32)
        for h in range(H):
            sl = slice(h * HP, (h + 1) * HP)
            dqo_ref[:, sl] = _rope_bwd(dq_ref[:, sl].astype(F32), c, sm, sp).astype(BF16)
            dkr = dkr + dk_ref[:, sl].astype(F32)
        lane = lax.broadcasted_iota(jnp.int32, (TM, HP), 1)
        dkr = jnp.where((lane >= NOPE) & (lane < NOPE + ROPE), _rope_bwd(dkr, c, sm, sp), 0.0)
        dkb = dk_ref[...]
        dvb = dv_ref[...]
        dcqn = _dot(dqo_ref[...], wuq_ref[...], ((1,), (0,)))
        dckvn = _dot(dkb, wk_ref[...], ((1,), (1,))) + _dot(dvb, wv_ref[...], ((1,), (1,)))
        rq, rkv = _rms(cq), _rms(ckv)
        nq, nkv = cq * rq, ckv * rkv
        _acc_first(i, dgq_ref, _colsum(dcqn * nq))
        _acc_first(i, dgkv_ref, _colsum(dckvn * nkv))
        dtail_ref[:, :Q_LORA] = _rms_bwd(nq, rq, dcqn * gq_ref[...]).astype(BF16)
        dtail_ref[:, Q_LORA:Q_LORA + KV_LORA] = _rms_bwd(nkv, rkv, dckvn * gkv_ref[...]).astype(BF16)
        dtail_ref[:, Q_LORA + KV_LORA:] = dkr.astype(BF16)

    return pl.pallas_call(
        body, name="mla_pre_bwd", grid=(t // TM,),
        in_specs=[_row_at(TAIL, TAIL0 // TAIL), _row(wide), _row(wide), _row(wide), _row(D_A), _row(D_A), _row(D_A),
                  _vec(Q_LORA), _vec(KV_LORA), _full((wide, Q_LORA)), _full((KV_LORA, wide)), _full((KV_LORA, wide)),
                  _pos(HP), _pos(HP), _pos(HP)],
        out_specs=[_row(wide), _row(P_PAD), _vec(Q_LORA), _vec(KV_LORA)],
        out_shape=[jax.ShapeDtypeStruct((t, wide), BF16), jax.ShapeDtypeStruct((t, P_PAD), BF16),
                   jax.ShapeDtypeStruct((1, Q_LORA), F32), jax.ShapeDtypeStruct((1, KV_LORA), F32)],
        compiler_params=_cp("arbitrary"),
    )(proj, dq_, dk_, dv_, *dqkv_a, g_cq, g_ckv, w_uq, w_k, w_v, rc, rsm, rsp)


def _post_attn(out_a, out_b, g_a, g_b):
    t = out_a.shape[0]

    def body(a_ref, b_ref, ga_ref, gb_ref, y_ref):
        a, b = a_ref[...], b_ref[...]
        y_ref[:, :D_A] = (a * _rms(a) * ga_ref[...]).astype(BF16)
        y_ref[:, D_A:] = (b * _rms(b) * gb_ref[...]).astype(BF16)

    return pl.pallas_call(
        body, name="post_attn", grid=(t // TM,),
        in_specs=[_row(D_A), _row(D_A), _vec(D_A), _vec(D_A)],
        out_specs=_row(D), out_shape=jax.ShapeDtypeStruct((t, D), BF16),
        compiler_params=_cp("parallel"),
    )(out_a, out_b, g_a, g_b)


def _post_attn_bwd(dy, out_a, out_b, g_a, g_b):
    t = dy.shape[0]

    def body(dy_ref, a_ref, b_ref, ga_ref, gb_ref, da_ref, db_ref, dga_ref, dgb_ref):
        i = pl.program_id(0)
        dy_ = dy_ref[...].astype(F32)
        for src, g_ref, dst, dg_ref, sl in ((a_ref, ga_ref, da_ref, dga_ref, slice(0, D_A)),
                                            (b_ref, gb_ref, db_ref, dgb_ref, slice(D_A, D))):
            v = src[...]
            r = _rms(v)
            n = v * r
            dyv = dy_[:, sl]
            _acc_first(i, dg_ref, _colsum(dyv * n))
            dst[...] = _rms_bwd(n, r, dyv * g_ref[...])

    return pl.pallas_call(
        body, name="post_attn_bwd", grid=(t // TM,),
        in_specs=[_row(D), _row(D_A), _row(D_A), _vec(D_A), _vec(D_A)],
        out_specs=[_row(D_A), _row(D_A), _vec(D_A), _vec(D_A)],
        out_shape=[jax.ShapeDtypeStruct((t, D_A), F32)] * 2 + [jax.ShapeDtypeStruct((1, D_A), F32)] * 2,
        compiler_params=_cp("arbitrary"),
    )(dy, out_a, out_b, g_a, g_b)


def _resid_norm2(x, mix, g1, g, sc, sh):
    t = x.shape[0]

    def body(x_ref, mix_ref, g1_ref, g_ref, sc_ref, sh_ref, x2_ref, h_ref):
        x2 = x_ref[...] + g1_ref[0] * mix_ref[...]
        x2_ref[...] = x2
        n = x2 * _rms(x2)
        h_ref[...] = ((n * g_ref[...]) * (1.0 + sc_ref[0]) + sh_ref[0]).astype(BF16)

    return pl.pallas_call(
        body, name="resid_norm2", grid=(t // TM,),
        in_specs=[_row(D), _row(D), _per_ex(D), _vec(D), _per_ex(D), _per_ex(D)],
        out_specs=[_row(D), _row(D)],
        out_shape=[jax.ShapeDtypeStruct((t, D), F32), jax.ShapeDtypeStruct((t, D), BF16)],
        compiler_params=_cp("parallel"),
    )(x, mix, g1, g, sc, sh)


def _sigmoid(v):
    return 1.0 / (1.0 + jnp.exp(-v))


def _final(x2, f, g2, g_fin, target):
    t = x2.shape[0]
    nb = t // S
    tpb = S // TM

    def body(x2_ref, f_ref, g2_ref, g_ref, t_ref, dx3_ref, df_ref, loss_ref, dgf_ref, dg2_ref):
        i = pl.program_id(0)
        fv = f_ref[...].astype(F32)
        x3 = x2_ref[...] + g2_ref[0] * fv
        r = _rms(x3)
        n = x3 * r
        err = n * g_ref[...] - t_ref[...]
        _acc_first(i, loss_ref, _colsum(err * err))
        dy = err * (1.0 / D)
        _acc_first(i, dgf_ref, _colsum(dy * n))
        dx3 = _rms_bwd(n, r, dy * g_ref[...])
        dx3_ref[...] = dx3
        _acc_first(i, dg2_ref, _colsum(dx3 * fv), every=tpb)
        df_ref[...] = (dx3 * g2_ref[0]).astype(BF16)

    return pl.pallas_call(
        body, name="final", grid=(t // TM,),
        in_specs=[_row(D), _row(D), _per_ex(D), _vec(D), _row(D)],
        out_specs=[_row(D), _row(D), _vec(D), _vec(D), _per_ex(D)],
        out_shape=[jax.ShapeDtypeStruct((t, D), F32), jax.ShapeDtypeStruct((t, D), BF16),
                   jax.ShapeDtypeStruct((1, D), F32), jax.ShapeDtypeStruct((1, D), F32),
                   jax.ShapeDtypeStruct((nb, 1, D), F32)],
        compiler_params=_cp("arbitrary"),
    )(x2, f, g2, g_fin, target)


def _norm_bwd(xin, dh, dres, g, sc, gate=None):
    t = xin.shape[0]
    nb = t // S
    tpb = S // TM
    gated = gate is not None

    def body(*refs):
        if gated:
            x_ref, dh_ref, dres_ref, g_ref, sc_ref, mix_ref, g1_ref, dx_ref, dsh_ref, dsc_ref, dg_ref, dg1_ref, dmix_ref = refs
        else:
            x_ref, dh_ref, dres_ref, g_ref, sc_ref, dx_ref, dsh_ref, dsc_ref, dg_ref = refs
        i = pl.program_id(0)
        xv, dhv = x_ref[...], dh_ref[...].astype(F32)
        r = _rms(xv)
        n = xv * r
        gv = g_ref[...]
        _acc_first(i, dsh_ref, _colsum(dhv), every=tpb)
        _acc_first(i, dsc_ref, _colsum(dhv * (n * gv)), every=tpb)
        dng = dhv * (1.0 + sc_ref[0])
        _acc_first(i, dg_ref, _colsum(dng * n))
        dx = dres_ref[...] + _rms_bwd(n, r, dng * gv)
        dx_ref[...] = dx
        if gated:
            _acc_first(i, dg1_ref, _colsum(dx * mix_ref[...].astype(F32)), every=tpb)
            dmix_ref[...] = (dx * g1_ref[0]).astype(BF16)

    in_specs = [_row(D), _row(D), _row(D), _vec(D), _per_ex(D)]
    out_specs = [_row(D), _per_ex(D), _per_ex(D), _vec(D)]
    out_shape = [jax.ShapeDtypeStruct((t, D), F32), jax.ShapeDtypeStruct((nb, 1, D), F32),
                 jax.ShapeDtypeStruct((nb, 1, D), F32), jax.ShapeDtypeStruct((1, D), F32)]
    args = [xin, dh, dres, g, sc]
    if gated:
        in_specs += [_row(D), _per_ex(D)]
        out_specs += [_per_ex(D), _row(D)]
        out_shape += [jax.ShapeDtypeStruct((nb, 1, D), F32), jax.ShapeDtypeStruct((t, D), BF16)]
        args += list(gate)
    return pl.pallas_call(
        body, name="norm2_bwd" if gated else "norm1_bwd", grid=(t // TM,),
        in_specs=in_specs, out_specs=out_specs, out_shape=out_shape,
        compiler_params=_cp("arbitrary"),
    )(*args)


TQ = 256
TB = 512
FWD_HEADS = 2


def _mla_fwd(q, k, v):
    t = q.shape[0]
    nb = t // S

    def body(q_ref, k_ref, v_ref, o_ref, lse_ref):
        causal = lax.broadcasted_iota(jnp.int32, (TB, TB), 0) >= lax.broadcasted_iota(jnp.int32, (TB, TB), 1)
        heads = [slice(h * HP, (h + 1) * HP) for h in range(FWD_HEADS)]
        for i in range(S // TB):
            ri, past = slice(i * TB, (i + 1) * TB), slice(0, i * TB)
            qhs = [q_ref[ri, sl] for sl in heads]
            sd = [jnp.where(causal, _dot(qh, k_ref[ri, sl], ((1,), (1,))) * SCALE_B, NEG) for qh, sl in zip(qhs, heads)]
            ms = [jnp.max(s, axis=-1, keepdims=True) for s in sd]
            if i:
                so = [_dot(qh, k_ref[past, sl], ((1,), (1,))) * SCALE_B for qh, sl in zip(qhs, heads)]
                ms = [jnp.maximum(m, jnp.max(s, axis=-1, keepdims=True)) for m, s in zip(ms, so)]
            pd = [jnp.exp(s - m) for s, m in zip(sd, ms)]
            ls = [jnp.sum(p, axis=-1, keepdims=True) for p in pd]
            acc = [_dot(p.astype(BF16), v_ref[ri, sl], ((1,), (0,))) for p, sl in zip(pd, heads)]
            if i:
                po = [jnp.exp(s - m) for s, m in zip(so, ms)]
                ls = [l + jnp.sum(p, axis=-1, keepdims=True) for l, p in zip(ls, po)]
                acc = [a + _dot(p.astype(BF16), v_ref[past, sl], ((1,), (0,))) for a, p, sl in zip(acc, po, heads)]
            for pr in range(FWD_HEADS // 2):
                o_ref[ri, pr * HP:(pr + 1) * HP] = acc[2 * pr] / ls[2 * pr] + acc[2 * pr + 1] / ls[2 * pr + 1]
            for sl, m, l in zip(heads, ms, ls):
                lse_ref[ri, sl] = jnp.broadcast_to(m + jnp.log(l), (TB, HP))

    wide2 = pl.BlockSpec((S, FWD_HEADS * HP), lambda b, p: (b, p))
    return pl.pallas_call(
        body, name="mla_fwd", grid=(nb, H // FWD_HEADS),
        in_specs=[wide2, wide2, wide2],
        out_specs=[pl.BlockSpec((S, FWD_HEADS // 2 * HP), lambda b, p: (b, p)), wide2],
        out_shape=[jax.ShapeDtypeStruct((t, H * VDIM), F32), jax.ShapeDtypeStruct((t, H * HP), F32)],
        compiler_params=_cp("parallel", "parallel"),
    )(q, k, v)


def _mla_bwd(q, k, v, o, do, lse):
    t = q.shape[0]
    nb = t // S

    def body(q_ref, k_ref, v_ref, o_ref, do_ref, lse_ref, dq_out, dk_out, dv_out, dq_ref, dk_ref, dv_ref):
        lane = lax.broadcasted_iota(jnp.int32, (TB, HP), 1)
        causal = lax.broadcasted_iota(jnp.int32, (TB, TB), 0) >= lax.broadcasted_iota(jnp.int32, (TB, TB), 1)
        heads = [slice(h * HP, (h + 1) * HP) for h in range(2)]
        nblk = S // TB
        for i in reversed(range(nblk)):
            ri, past = slice(i * TB, (i + 1) * TB), slice(0, i * TB)
            dov = do_ref[ri, :]
            prod = dov * o_ref[ri, :]
            dob = dov.astype(BF16)
            deltas = [jnp.sum(jnp.where((lane < VDIM) if h == 0 else (lane >= VDIM), prod, 0.0), axis=-1, keepdims=True)
                      for h in range(2)]
            qhs = [q_ref[ri, sl] for sl in heads]
            lses = [lse_ref[ri, sl][:, :1] for sl in heads]
            for rows, diagonal in ((ri, True), (past, False)):
                if rows.stop == rows.start:
                    continue
                ps = [jnp.exp(_dot(qh, k_ref[rows, sl], ((1,), (1,))) * SCALE_B - lse) for qh, sl, lse in zip(qhs, heads, lses)]
                if diagonal:
                    ps = [jnp.where(causal, p, 0.0) for p in ps]
                dps = [_dot(dob, v_ref[rows, sl], ((1,), (1,))) for sl in heads]
                dss = [(p * (dp - delta) * SCALE_B).astype(BF16) for p, dp, delta in zip(ps, dps, deltas)]
                for sl, qh, p, ds in zip(heads, qhs, ps, dss):
                    dq = _dot(ds, k_ref[rows, sl], ((1,), (0,)))
                    dk = _dot(ds, qh, ((0,), (0,)))
                    dv = _dot(p.astype(BF16), dob, ((0,), (0,)))
                    if diagonal:
                        dq_ref[ri, sl] = dq
                    else:
                        dq_ref[ri, sl] += dq
                    if i == nblk - 1:
                        dk_ref[rows, sl] = dk
                        dv_ref[rows, sl] = dv
                    else:
                        dk_ref[rows, sl] += dk
                        dv_ref[rows, sl] += dv
        dq_out[...] = dq_ref[...].astype(BF16)
        dk_out[...] = dk_ref[...].astype(BF16)
        dv_out[...] = dv_ref[...].astype(BF16)

    wide2 = pl.BlockSpec((S, 2 * HP), lambda b, p: (b, p))
    pair = pl.BlockSpec((S, HP), lambda b, p: (b, p))
    return pl.pallas_call(
        body, name="mla_bwd", grid=(nb, H // 2),
        in_specs=[wide2, wide2, wide2, pair, pair, wide2],
        out_specs=[wide2, wide2, wide2],
        out_shape=[jax.ShapeDtypeStruct((t, H * HP), BF16)] * 3,
        scratch_shapes=[pltpu.VMEM((S, 2 * HP), F32)] * 3,
        compiler_params=_cp("parallel", "parallel"),
    )(q, k, v, o, do, lse)


def _t5_bucket(dist):
    max_exact = N_BUCKETS // 2
    d = np.maximum(dist, 1).astype(np.float64)
    large = max_exact + (np.log(d / max_exact) / np.log(MAX_DISTANCE / max_exact) * (N_BUCKETS - max_exact)).astype(np.int64)
    large = np.minimum(large, N_BUCKETS - 1)
    return np.where(dist < max_exact, dist, large).astype(np.int32)


def _band_geometry():
    a = np.arange(BLK)[:, None]
    bk = np.arange(2 * BLK)[None, :]
    steps = BLK + a - bk
    valid = (steps >= 0) & (steps <= BLK)
    buckets = np.stack([_t5_bucket(np.clip(steps, 0, BLK) * d) for d in DILATIONS])
    return buckets, valid


def _band_bias(rel_bias):
    buckets, valid = _band_geometry()
    onehot = (jnp.asarray(buckets)[..., None] == jnp.arange(N_BUCKETS)).astype(F32)
    bias = jnp.einsum("rqkn,nh->rhqk", onehot, rel_bias, precision=lax.Precision.HIGHEST)
    bias = jnp.where(jnp.asarray(valid)[None, None], bias, NEG)
    return bias.reshape(3, H // 2, 2 * BLK, 2 * BLK)


def _dil_items():
    items = []
    for r, d in enumerate(DILATIONS):
        for res in range(d):
            for blk in range(S // d // BLK):
                items.append((r, d, blk * BLK * d + res, blk > 0))
    return items


GROUP = 4


def _strided(start, d):
    return pl.ds(start, BLK) if d == 1 else pl.ds(start, BLK, stride=d)


def _stack_heads(tile, own):
    return jnp.where(own, jnp.concatenate([tile, tile], axis=0), 0.0).astype(BF16)


def _own_lanes():
    row = lax.broadcasted_iota(jnp.int32, (2 * BLK, HP), 0)
    lane = lax.broadcasted_iota(jnp.int32, (2 * BLK, HP), 1)
    return (lane < E_A) == (row < BLK)


def _dil_fwd(proj, biasm):
    t = proj.shape[0]
    nb = t // S

    def body(q_ref, k_ref, v_ref, b_ref, o_ref, lse_ref, ob_ref, lb_ref):
        lane = lax.broadcasted_iota(jnp.int32, (BLK, HP), 1)
        own = _own_lanes()
        items = _dil_items()
        for g in range(0, len(items), GROUP):
            grp = items[g:g + GROUP]
            ss, vts = [], []
            for r, d, start, has_prev in grp:
                cur = _strided(start, d)
                rows = [_strided(start - BLK * d, d), cur] if has_prev else [cur]
                q2 = _stack_heads(q_ref[cur, :] * SCALE_A, own)
                kt = jnp.concatenate([k_ref[x, :] for x in rows], axis=0).astype(BF16)
                vts.append(jnp.concatenate([v_ref[x, :] for x in rows], axis=0).astype(BF16))
                bias = b_ref[r, 0] if has_prev else b_ref[r, 0, :, BLK:]
                ss.append(_dot(q2, kt, ((1,), (1,))) + bias)
            ms = [jnp.max(s, axis=-1, keepdims=True) for s in ss]
            ps = [jnp.exp(s - m) for s, m in zip(ss, ms)]
            ls = [jnp.sum(p, axis=-1, keepdims=True) for p in ps]
            for (r, d, start, _), p, vt, m, l in zip(grp, ps, vts, ms, ls):
                cur = _strided(start, d)
                o2 = _dot(p.astype(BF16), vt, ((1,), (0,))) / l
                lse2 = m + jnp.log(l)
                ob_ref[r, cur, :] = jnp.where(lane < E_A, o2[:BLK], o2[BLK:])
                lb_ref[r, cur, :] = jnp.where(lane < E_A, lse2[:BLK], lse2[BLK:])

        def merge(c, _):
            rows = pl.ds(pl.multiple_of(c * TQ, TQ), TQ)
            l0, l1, l2 = lb_ref[0, rows, :], lb_ref[1, rows, :], lb_ref[2, rows, :]
            m = jnp.maximum(jnp.maximum(l0, l1), l2)
            e0, e1, e2 = jnp.exp(l0 - m), jnp.exp(l1 - m), jnp.exp(l2 - m)
            tot = e0 + e1 + e2
            o_ref[rows, :] = (e0 * ob_ref[0, rows, :] + e1 * ob_ref[1, rows, :] + e2 * ob_ref[2, rows, :]) / tot
            lse_ref[rows, :] = m + jnp.log(tot)
            return 0

        lax.fori_loop(0, S // TQ, merge, 0)

    npair = H // 2
    return pl.pallas_call(
        body, name="dil_fwd", grid=(nb, npair),
        in_specs=[pl.BlockSpec((S, HP), lambda b, p: (b, p)), pl.BlockSpec((S, HP), lambda b, p: (b, npair + p)),
                  pl.BlockSpec((S, HP), lambda b, p: (b, 2 * npair + p)),
                  pl.BlockSpec((3, 1, 2 * BLK, 2 * BLK), lambda b, p: (0, p, 0, 0))],
        out_specs=[pl.BlockSpec((S, HP), lambda b, p: (b, p))] * 2,
        out_shape=[jax.ShapeDtypeStruct((t, D_A), F32)] * 2,
        scratch_shapes=[pltpu.VMEM((3, S, HP), F32), pltpu.VMEM((3, S, HP), F32)],
        compiler_params=_cp("parallel", "parallel"),
    )(proj, proj, proj, biasm)


def _dil_bwd(proj, biasm, o, do, lse):
    t = proj.shape[0]
    nb = t // S

    def body(q_ref, k_ref, v_ref, b_ref, o_ref, do_ref, lse_ref, dq_out, dk_out, dv_out, ds_ref, dq_ref, dk_ref, dv_ref):
        dq_ref[...] = jnp.zeros_like(dq_ref)
        dk_ref[...] = jnp.zeros_like(dk_ref)
        dv_ref[...] = jnp.zeros_like(dv_ref)
        ds_ref[...] = jnp.zeros_like(ds_ref)
        lane = lax.broadcasted_iota(jnp.int32, (BLK, HP), 1)
        own = _own_lanes()
        items = _dil_items()
        for g in range(0, len(items), GROUP):
            grp = items[g:g + GROUP]
            q2s, kts, do2s, ss, dps, lse2s, delta2s = [], [], [], [], [], [], []
            for r, d, start, has_prev in grp:
                cur = _strided(start, d)
                rows = [_strided(start - BLK * d, d), cur] if has_prev else [cur]
                q2 = _stack_heads(q_ref[cur, :] * SCALE_A, own)
                kt = jnp.concatenate([k_ref[x, :] for x in rows], axis=0).astype(BF16)
                vt = jnp.concatenate([v_ref[x, :] for x in rows], axis=0).astype(BF16)
                dot_ = do_ref[cur, :]
                prod = dot_ * o_ref[cur, :]
                lset = lse_ref[cur, :]
                do2 = _stack_heads(dot_, own)
                bias = b_ref[r, 0] if has_prev else b_ref[r, 0, :, BLK:]
                ss.append(_dot(q2, kt, ((1,), (1,))) + bias)
                dps.append(_dot(do2, vt, ((1,), (1,))))
                lse2s.append(jnp.concatenate([lset[:, :1], lset[:, E_A:E_A + 1]], axis=0))
                delta2s.append(jnp.concatenate([jnp.sum(jnp.where(lane < E_A, prod, 0.0), axis=-1, keepdims=True),
                                                jnp.sum(jnp.where(lane >= E_A, prod, 0.0), axis=-1, keepdims=True)], axis=0))
                q2s.append(q2)
                kts.append(kt)
                do2s.append(do2)
            ps = [jnp.exp(s - lse2) for s, lse2 in zip(ss, lse2s)]
            dls = [p * (dp - delta2) for p, dp, delta2 in zip(ps, dps, delta2s)]
            for (r, d, start, has_prev), q2, kt, do2, p, dl in zip(grp, q2s, kts, do2s, ps, dls):
                cur = _strided(start, d)
                dsb = dl.astype(BF16)
                dq2 = _dot(dsb, kt, ((1,), (0,))) * SCALE_A
                dkt = _dot(dsb, q2, ((0,), (0,)))
                dvt = _dot(p.astype(BF16), do2, ((0,), (0,)))
                dq_ref[cur, :] += jnp.where(lane < E_A, dq2[:BLK], dq2[BLK:])
                if has_prev:
                    prev = _strided(start - BLK * d, d)
                    ds_ref[0, r, 0] += dl
                    dk_ref[prev, :] += dkt[:BLK]
                    dv_ref[prev, :] += dvt[:BLK]
                    dk_ref[cur, :] += dkt[BLK:]
                    dv_ref[cur, :] += dvt[BLK:]
                else:
                    ds_ref[0, r, 0, :, BLK:] += dl
                    dk_ref[cur, :] += dkt
                    dv_ref[cur, :] += dvt
        dq_out[...] = dq_ref[...].astype(BF16)
        dk_out[...] = dk_ref[...].astype(BF16)
        dv_out[...] = dv_ref[...].astype(BF16)

    npair = H // 2
    pair = pl.BlockSpec((S, HP), lambda b, p: (b, p))
    return pl.pallas_call(
        body, name="dil_bwd", grid=(nb, npair),
        in_specs=[pair, pl.BlockSpec((S, HP), lambda b, p: (b, npair + p)),
                  pl.BlockSpec((S, HP), lambda b, p: (b, 2 * npair + p)),
                  pl.BlockSpec((3, 1, 2 * BLK, 2 * BLK), lambda b, p: (0, p, 0, 0)), pair, pair, pair],
        out_specs=[pair, pair, pair, pl.BlockSpec((1, 3, 1, 2 * BLK, 2 * BLK), lambda b, p: (b, 0, p, 0, 0))],
        out_shape=[jax.ShapeDtypeStruct((t, D_A), BF16)] * 3 + [jax.ShapeDtypeStruct((nb, 3, npair, 2 * BLK, 2 * BLK), F32)],
        scratch_shapes=[pltpu.VMEM((S, HP), F32)] * 3,
        compiler_params=_cp("parallel", "parallel"),
    )(proj, proj, proj, biasm, o, do, lse)


def _rel_bias_grad(dlogits):
    nb = dlogits.shape[0]
    buckets, _ = _band_geometry()
    kk = 3 * BLK * 2 * BLK
    dl = jnp.transpose(dlogits.reshape(nb, 3, H, BLK, 2 * BLK), (0, 2, 1, 3, 4)).reshape(nb, H, kk)
    bk = jnp.asarray(buckets.reshape(1, kk))
    tk = kk // 4

    def body(dl_ref, bk_ref, o_ref):
        j = pl.program_id(0)
        onehot = (bk_ref[...] == lax.broadcasted_iota(jnp.int32, (N_BUCKETS, tk), 0)).astype(F32)
        tot = dl_ref[0]
        for b in range(1, nb):
            tot = tot + dl_ref[b]
        part = lax.dot_general(onehot, tot, ((((1,), (1,))), ((), ())), preferred_element_type=F32,
                               precision=lax.Precision.HIGHEST)
        _acc_first(j, o_ref, part)

    return pl.pallas_call(
        body, name="rel_bias_grad", grid=(kk // tk,),
        in_specs=[pl.BlockSpec((nb, H, tk), lambda j: (0, 0, j)), pl.BlockSpec((1, tk), lambda j: (0, j))],
        out_specs=pl.BlockSpec((N_BUCKETS, H), lambda j: (0, 0)),
        out_shape=jax.ShapeDtypeStruct((N_BUCKETS, H), F32),
        compiler_params=_cp("arbitrary"),
    )(dl, bk)


def _mesh_place():
    x, y, c = lax.axis_index("x"), lax.axis_index("y"), lax.axis_index("c")
    return x, y, c


def _peer(k):
    x, y, c = _mesh_place()
    px = 1 - x if k & 4 else x
    py = 1 - y if k & 2 else y
    pc = 1 - c if k & 1 else c
    return (px, py, pc), 4 * px + 2 * py + pc


ANY = pl.BlockSpec(memory_space=pl.ANY)


def _exchange(arrays, gathers, name, after=None):
    n_arr = len(arrays)

    def body(*refs):
        ins, outs = refs[:n_arr], refs[n_arr + 1:2 * n_arr + 1]
        send, recv, loc = refs[2 * n_arr + 1:]
        x, y, c = _mesh_place()
        me = 4 * x + 2 * y + c
        local = [pltpu.make_async_copy(ins[a] if gathers[a] else ins[a].at[me], outs[a].at[me], loc.at[a])
                 for a in range(n_arr)]
        remote = _peer_copies(ins, outs, send, recv, gathers)
        for cp in local:
            cp.start()
        for put, _ in remote:
            put.start()
        for cp in local:
            cp.wait()
        for put, got in remote:
            put.wait_send()
            got.wait_recv()

    return pl.pallas_call(
        body, name=name,
        in_specs=[ANY] * (n_arr + 1), out_specs=[ANY] * n_arr,
        out_shape=[jax.ShapeDtypeStruct(((N_DEV,) if g else ()) + a.shape, a.dtype) for a, g in zip(arrays, gathers)],
        scratch_shapes=[pltpu.SemaphoreType.DMA((n_arr * (N_DEV - 1),)), pltpu.SemaphoreType.DMA((n_arr * (N_DEV - 1),)),
                        pltpu.SemaphoreType.DMA((n_arr,))],
        compiler_params=pltpu.CompilerParams(has_side_effects=True),
    )(*arrays, arrays[0] if after is None else after)


def _gather_two_level(arrays, name):
    n_arr = len(arrays)
    per = N_DEV - 1

    def body(*refs):
        ins, outs = refs[:n_arr], refs[n_arr:2 * n_arr]
        send, recv, loc = refs[2 * n_arr:]
        x, y, c = _mesh_place()
        me, sibling = (x, y, c), (x, y, 1 - c)
        chips = [(1 - x, y), (x, 1 - y), (1 - x, 1 - y)]

        def block(a, place):
            px, py, pc = place
            return outs[a].at[4 * px + 2 * py + pc]

        def copy(a, k, place, to, src=None):
            dst = block(a, place)
            return pltpu.make_async_remote_copy(dst if src is None else src, dst, send.at[a * per + k], recv.at[a * per + k],
                                                device_id=to, device_id_type=pl.DeviceIdType.MESH)

        local = [pltpu.make_async_copy(ins[a], block(a, me), loc.at[a]) for a in range(n_arr)]
        for cp in local:
            cp.start()
        first = []
        for a in range(n_arr):
            first.append(copy(a, 0, me, sibling, src=ins[a]))
            first += [copy(a, 1 + j, me, (*chip, c), src=ins[a]) for j, chip in enumerate(chips)]
        for cp in first:
            cp.start()
        passed = []
        for j, chip in enumerate(chips):
            for a in range(n_arr):
                copy(a, 1 + j, (*chip, c), me).wait_recv()
                passed.append(copy(a, 4 + j, (*chip, c), sibling))
                passed[-1].start()
        for a in range(n_arr):
            copy(a, 0, sibling, me).wait_recv()
            for j, chip in enumerate(chips):
                copy(a, 4 + j, (*chip, 1 - c), me).wait_recv()
        for cp in first + passed:
            cp.wait_send()
        for cp in local:
            cp.wait()

    return pl.pallas_call(
        body, name=name,
        in_specs=[ANY] * n_arr, out_specs=[ANY] * n_arr,
        out_shape=[jax.ShapeDtypeStruct((N_DEV,) + a.shape, a.dtype) for a in arrays],
        scratch_shapes=[pltpu.SemaphoreType.DMA((n_arr * per,)), pltpu.SemaphoreType.DMA((n_arr * per,)),
                        pltpu.SemaphoreType.DMA((n_arr,))],
        compiler_params=pltpu.CompilerParams(has_side_effects=True),
    )(*arrays)


HBM = pl.BlockSpec(memory_space=pltpu.HBM)
SEM = pl.BlockSpec(memory_space=pltpu.SEMAPHORE)
DATAFLOW = pltpu.SideEffectType.DATAFLOW_SIDE_EFFECTING


def _own_block_in_place(block, me):
    land = lax.empty((N_DEV,) + block.shape, block.dtype)
    return lax.dynamic_update_slice(land, block[None], (me,) + (0,) * block.ndim)


def _peer_copies(srcs, lands, send, recv, gathers):
    x, y, c = _mesh_place()
    me = 4 * x + 2 * y + c
    out = []
    for a, (src, land) in enumerate(zip(srcs, lands)):
        for k in range(1, N_DEV):
            dev, idx = _peer(k)
            sem = a * (N_DEV - 1) + k - 1
            mine = src if gathers[a] else src.at[idx]
            put = pltpu.make_async_remote_copy(mine, land.at[me], send.at[sem], recv.at[sem],
                                               device_id=dev, device_id_type=pl.DeviceIdType.MESH)
            got = pltpu.make_async_remote_copy(mine, land.at[idx], send.at[sem], recv.at[sem],
                                               device_id=dev, device_id_type=pl.DeviceIdType.MESH)
            out.append((put, got))
    return out


def _exchange_start(srcs, lands, gather, after, name):
    n = len(srcs)
    extra = [] if after is None else [after]

    def body(*refs):
        srcs_, lands_ = refs[:n], refs[n:2 * n]
        send, recv = refs[2 * n + len(extra)], refs[2 * n + len(extra) + 1]
        for put, _ in _peer_copies(srcs_, lands_, send, recv, gather):
            put.start()
        refs[-1][...] = jnp.zeros_like(refs[-1])

    nsem = n * (N_DEV - 1)
    thru = [pltpu.HBM(a.shape, a.dtype) for a in list(srcs) + list(lands)]
    res = pl.pallas_call(
        body, name=name,
        out_shape=(pltpu.SemaphoreType.DMA((nsem,)), pltpu.SemaphoreType.DMA((nsem,)), *thru, jax.ShapeDtypeStruct((8, 128), F32)),
        in_specs=[HBM] * (2 * n) + [ANY] * len(extra),
        out_specs=(SEM, SEM, *([HBM] * (2 * n)), pl.BlockSpec(memory_space=pltpu.VMEM)),
        input_output_aliases={i: 2 + i for i in range(2 * n)},
        compiler_params=pltpu.CompilerParams(has_side_effects=DATAFLOW),
    )(*[pltpu.with_memory_space_constraint(a, pltpu.HBM) for a in list(srcs) + list(lands)], *extra)
    return res[0], res[1], list(res[2:2 + n]), list(res[2 + n:2 + 2 * n]), res[-1]


def _exchange_wait(send, recv, srcs, lands, gather, after, name):
    n = len(srcs)

    def body(*refs):
        srcs_, lands_, send_, recv_ = refs[:n], refs[n:2 * n], refs[2 * n], refs[2 * n + 1]
        for put, got in _peer_copies(srcs_, lands_, send_, recv_, gather):
            put.wait_send()
            got.wait_recv()

    thru = [pltpu.HBM(a.shape, a.dtype) for a in list(srcs) + list(lands)]
    res = pl.pallas_call(
        body, name=name, out_shape=tuple(thru),
        in_specs=[HBM] * (2 * n) + [SEM, SEM, ANY], out_specs=tuple([HBM] * (2 * n)),
        input_output_aliases={i: i for i in range(2 * n)},
        compiler_params=pltpu.CompilerParams(has_side_effects=DATAFLOW),
    )(*srcs, *lands, send, recv, after)
    return list(res[n:])


def _silu_rows(c):
    def body(c_ref, o_ref):
        v = c_ref[...]
        o_ref[...] = v * _sigmoid(v)

    return pl.pallas_call(body, name="cond", out_shape=jax.ShapeDtypeStruct(c.shape, F32))(c)


def _mod_slab(cond_all, w_ada, b_slab):
    def body(c_ref, w_ref, b_ref, o_ref):
        o_ref[...] = _dot(c_ref[...].astype(BF16), w_ref[0].astype(BF16), ((1,), (0,))) + b_ref[...]

    return pl.pallas_call(body, name="mod_slab",
                          out_shape=jax.ShapeDtypeStruct((cond_all.shape[0], w_ada.shape[2]), F32),
                          compiler_params=pltpu.CompilerParams(vmem_limit_bytes=VMEM_LIMIT))(cond_all, w_ada, b_slab)


def _ada_grad(cond_all, dmod_cols):
    def body(c_ref, d_ref, o_ref):
        o_ref[...] = _dot(c_ref[...].astype(BF16), d_ref[...].astype(BF16), ((0,), (0,)))

    return pl.pallas_call(body, name="ada_grad",
                          out_shape=jax.ShapeDtypeStruct((cond_all.shape[1], dmod_cols.shape[1]), F32),
                          compiler_params=pltpu.CompilerParams(vmem_limit_bytes=VMEM_LIMIT))(cond_all, dmod_cols)


def _adam_math(g, w, m, v):
    m2 = B1 * m + (1.0 - B1) * g
    v2 = B2 * v + (1.0 - B2) * (g * g)
    m_hat = m2 / (1.0 - B1 ** STEP)
    v_hat = v2 / (1.0 - B2 ** STEP)
    return -LR * (m_hat / (jnp.sqrt(v_hat) + ADAM_EPS) + WD * w), m2, v2


def _adamw(parts, w, m, v, name):
    n, rows, cols = parts.shape
    tr = max([p for p in range(16, 513, 16) if rows % p == 0] or [rows])

    def body(p_ref, w_ref, m_ref, v_ref, g_ref, d_ref, m2_ref, v2_ref):
        g = p_ref[0].astype(F32)
        for s in range(1, n):
            g = g + p_ref[s].astype(F32)
        g_ref[0] = g
        d_ref[0], m2_ref[0], v2_ref[0] = _adam_math(g, w_ref[0], m_ref[0], v_ref[0])

    blk = pl.BlockSpec((1, tr, cols), lambda i: (0, i, 0))
    return pl.pallas_call(
        body, name=name, grid=(rows // tr,),
        in_specs=[pl.BlockSpec((n, tr, cols), lambda i: (0, i, 0)), blk, blk, blk],
        out_specs=[blk] * 4, out_shape=[jax.ShapeDtypeStruct((1, rows, cols), F32)] * 4,
        compiler_params=_cp("parallel"),
    )(parts, w, m, v)


ROW_PARAMS = (("g_norm1", D), ("g_cq", Q_LORA), ("g_ckv", KV_LORA), ("g_out_a", D_A), ("g_out_b", D_A), ("g_norm2", D),
              ("g_final", D))
LOSS_ROW = N_MOD + len(ROW_PARAMS)
PAY_ROWS = 16
NCOL = N_MOD * D // N_DEV


def _pack_small(dmods, rows, loss_cols):
    nb = dmods[0].shape[0]
    nrow = len(ROW_PARAMS)

    def body(*refs):
        dm, rw, loss_ref, pay_ref, blk_ref = refs[:N_MOD], refs[N_MOD:N_MOD + nrow], refs[N_MOD + nrow], refs[-2], refs[-1]
        pay_ref[...] = jnp.zeros_like(pay_ref)
        for k in range(N_MOD):
            tot = dm[k][0]
            for b in range(1, nb):
                tot = tot + dm[k][b]
            pay_ref[k:k + 1, :] = tot
        for i, (_, n) in enumerate(ROW_PARAMS):
            pay_ref[N_MOD + i:N_MOD + i + 1, :n] = rw[i][...]
        pay_ref[LOSS_ROW:LOSS_ROW + 1, :] = loss_ref[...]
        for j in range(N_DEV):
            done = 0
            while done < NCOL:
                seg, off = divmod(j * NCOL + done, D)
                ln = min(NCOL - done, D - off)
                for b in range(nb):
                    blk_ref[j, b:b + 1, done:done + ln] = dm[seg][b][:, off:off + ln]
                done += ln

    return pl.pallas_call(
        body, name="pack_small",
        out_shape=[jax.ShapeDtypeStruct((PAY_ROWS, D), F32), jax.ShapeDtypeStruct((N_DEV, nb, NCOL), F32)],
    )(*dmods, *rows, loss_cols)


def _small_update(pay, rel, ws, ms, vs):
    n_par = len(ws)

    def body(*refs):
        pay_ref, rel_ref = refs[:2]
        w_refs, m_refs, v_refs = (refs[2 + s * n_par:2 + (s + 1) * n_par] for s in range(3))
        outs, loss_ref = refs[2 + 3 * n_par:-1], refs[-1]
        tot, rtot = pay_ref[0], rel_ref[0]
        for s in range(1, N_DEV):
            tot, rtot = tot + pay_ref[s], rtot + rel_ref[s]

        def update(p, g, sl):
            outs[4 * p][:, sl] = g
            outs[4 * p + 1][:, sl], outs[4 * p + 2][:, sl], outs[4 * p + 3][:, sl] = _adam_math(
                g, w_refs[p][:, sl], m_refs[p][:, sl], v_refs[p][:, sl])

        for k in range(N_MOD):
            update(0, tot[k:k + 1, :], slice(k * D, (k + 1) * D))
        for i, (_, n) in enumerate(ROW_PARAMS):
            update(1 + i, tot[N_MOD + i:N_MOD + i + 1, :n], slice(0, n))
        update(n_par - 1, rtot, slice(0, H))
        loss_ref[...] = jnp.broadcast_to((0.5 / D) * jnp.sum(tot[LOSS_ROW:LOSS_ROW + 1, :]), loss_ref.shape)

    shapes = [jax.ShapeDtypeStruct(w.shape, F32) for w in ws for _ in range(4)]
    res = pl.pallas_call(
        body, name="small_update", out_shape=shapes + [jax.ShapeDtypeStruct((8, 128), F32)],
    )(pay, rel, *ws, *ms, *vs)
    return [tuple(res[4 * p:4 * p + 4]) for p in range(n_par)], res[-1]


def _cols_from_blocks(g):
    return jnp.transpose(g, (1, 0, 2)).reshape(g.shape[1], N_DEV * g.shape[2])


def _cols_to_blocks(w):
    r, c = w.shape
    return jnp.transpose(w.reshape(r, N_DEV, c // N_DEV), (1, 0, 2))


def _pad_w_in(wt):
    z = jnp.zeros((NOPE, wt.shape[1]), wt.dtype)
    return jnp.concatenate([wt[:P_IN - ROPE], z, wt[P_IN - ROPE:], z[:HP - NOPE - ROPE]], axis=0)


def _unpad_w_in(gt):
    k0 = P_IN - ROPE + NOPE
    return jnp.concatenate([gt[:P_IN - ROPE], gt[k0:k0 + ROPE]], axis=0)


def _pad_w_uq(wt):
    return jnp.pad(wt, ((0, 0), (0, HP - NOPE - ROPE), (0, 0))).reshape(H * HP, Q_LORA)


def _unpad_w_uq(gt):
    return gt.reshape(H, HP, Q_LORA)[:, :NOPE + ROPE]


def _split_w_ukv(w):
    w4 = w.reshape(KV_LORA, H // 2, 2, HP)
    z = jnp.zeros((KV_LORA, H // 2, NOPE), w.dtype)
    kn, vv = w4[..., :NOPE], w4[..., NOPE:]
    w_k = jnp.stack([jnp.concatenate([kn[:, :, 0], z], -1), jnp.concatenate([kn[:, :, 1], z], -1)], axis=2)
    w_v = jnp.stack([jnp.concatenate([vv[:, :, 0], z], -1), jnp.concatenate([z, vv[:, :, 1]], -1)], axis=2)
    return w_k.reshape(KV_LORA, H * HP), w_v.reshape(KV_LORA, H * HP)


def _join_w_ukv(g_k, g_v):
    gk = g_k.reshape(KV_LORA, H // 2, 2, HP)
    gv = g_v.reshape(KV_LORA, H // 2, 2, HP)
    even = jnp.concatenate([gk[:, :, 0, :NOPE], gv[:, :, 0, :VDIM]], -1)
    odd = jnp.concatenate([gk[:, :, 1, :NOPE], gv[:, :, 1, VDIM:]], -1)
    return jnp.stack([even, odd], axis=2).reshape(KV_LORA, H * HP)


def _rope_tables():
    half = ROPE // 2
    inv = np.float32(ROPE_THETA) ** (-np.arange(half, dtype=np.float32) / np.float32(half))
    ang = np.arange(S, dtype=np.float32)[:, None] * inv[None, :].astype(np.float32)
    cos, sin = np.cos(ang).astype(np.float32), np.sin(ang).astype(np.float32)
    ones, zeros = np.ones((S, NOPE), np.float32), np.zeros((S, NOPE), np.float32)
    tail1, tail0 = np.ones((S, HP - NOPE - ROPE), np.float32), np.zeros((S, HP - NOPE - ROPE), np.float32)
    zh = np.zeros((S, half), np.float32)
    c = np.concatenate([ones, cos, cos, tail1], axis=1)
    sm = np.concatenate([zeros, -sin, zh, tail0], axis=1)
    sp = np.concatenate([zeros, zh, sin, tail0], axis=1)
    return jnp.asarray(c), jnp.asarray(sm), jnp.asarray(sp)


def _local_step(x, mod, target, g_norm1, w_in_p, g_cq, w_uq_p, g_ckv, w_k, w_v, rel_bias, g_out_a, g_out_b, w_out,
                g_norm2, w_ffn_in, w_ffn_out, g_final, late_weights=None, on_ffn_grads=None, on_last_grads=None):
    nb = x.shape[0] // S
    sh1, sc1, g1, sh2, sc2, g2 = (mod[:, n].reshape(nb, 1, D) for n in range(N_MOD))
    rc, rsm, rsp = _rope_tables()
    biasm = _band_bias(rel_bias)

    h1 = _pre1(x, g_norm1, sc1, sh1)
    proj = _mm_nt(h1, w_in_p, F32, "proj")
    q, k, v, cqn, ckvn = _mla_pre(proj, g_cq, g_ckv, w_uq_p, w_k, w_v, rc, rsm, rsp)
    out_b, lse_b = _mla_fwd(q, k, v)
    out_a, lse_a = _dil_fwd(proj, biasm)
    y = _post_attn(out_a, out_b, g_out_a, g_out_b)
    if late_weights is not None:
        w_out, w_ffn_in, w_ffn_out = late_weights(y)
    mix = _mm_nn(y, w_out, BF16, "mix")
    x2, h2 = _resid_norm2(x, mix, g1, g_norm2, sc2, sh2)
    ffn_g, ffn_u, act = _ffn_in(h2, w_ffn_in)
    f = _mm_nn(act, w_ffn_out, BF16, "ffn_out")
    dx3, df, loss_cols, dg_final, dg2 = _final(x2, f, g2, g_final, target)

    dg_, du_ = _d_act(df, w_ffn_out, ffn_g, ffn_u)
    gw_ffn_out = _mm_tn_rows([act], df, "gw_ffn_out")
    dh2 = _d_h2(dg_, du_, w_ffn_in)
    gw_ffn_in = _mm_tn_rows([dg_, du_], h2, "gw_ffn_in")
    dx2, dsh2, dsc2, dg_norm2, dg1, dmix = _norm_bwd(x2, dh2, dx3, g_norm2, sc2, gate=(mix, g1))
    dy = _mm_nt(dmix, w_out, BF16, "d_y")
    gw_out = _mm_tn(y, [dmix], "gw_out")
    if on_ffn_grads is not None:
        g_out_a = g_out_a + on_ffn_grads(gw_ffn_in, gw_ffn_out, gw_out)
    dout_a, dout_b, dg_out_a, dg_out_b = _post_attn_bwd(dy, out_a, out_b, g_out_a, g_out_b)
    dq_b, dk_b, dv_b = _mla_bwd(q, k, v, out_b, dout_b, lse_b)
    dq_a, dk_a, dv_a, dlogits = _dil_bwd(proj, biasm, out_a, dout_a, lse_a)
    g_rel = _rel_bias_grad(dlogits)
    dqr, dproj, dg_cq, dg_ckv = _mla_pre_bwd(proj, dq_b, dk_b, dv_b, (dq_a, dk_a, dv_a), g_cq, g_ckv, w_uq_p, w_k, w_v,
                                             rc, rsm, rsp)
    gw_uq = _mm_tn(dqr, [cqn], "gw_uq")
    gw_k, gw_v = _mm_tn(ckvn, [dk_b, dv_b], "gw_kv")
    gw_in = _mm_tn_rows([dproj], h1, "gw_in")
    if on_last_grads is not None:
        started = on_last_grads(dict(w_in=gw_in, w_uq=gw_uq, w_k=gw_k, w_v=gw_v))
    else:
        started = None
    dh1 = _mm_nn(dproj, w_in_p, BF16, "d_h1", after=started)
    grad_x, dsh1, dsc1, dg_norm1 = _norm_bwd(x, dh1, dx2, g_norm1, sc1)

    dmod = [dsh1, dsc1, dg1, dsh2, dsc2, dg2]
    small = dict(g_norm1=dg_norm1, g_cq=dg_cq, g_ckv=dg_ckv, rel_bias=g_rel, g_out_a=dg_out_a, g_out_b=dg_out_b,
                 g_norm2=dg_norm2, g_final=dg_final)
    big = dict(w_in=gw_in, w_uq=gw_uq, w_k=gw_k, w_v=gw_v, w_out=gw_out, w_ffn_in=gw_ffn_in, w_ffn_out=gw_ffn_out)
    return grad_x, dmod, loss_cols, small, big


def kernel(x, c, w_ada, b_ada, g_norm1, w_in, g_cq, w_uq, g_ckv, w_ukv, rel_bias, g_out_a, g_out_b, w_out, g_norm2, w_ffn_in, w_ffn_out, g_final, loss_target, m_w_ada, m_b_ada, m_g_norm1, m_w_in, m_g_cq, m_w_uq, m_g_ckv, m_w_ukv, m_rel_bias, m_g_out_a, m_g_out_b, m_w_out, m_g_norm2, m_w_ffn_in, m_w_ffn_out, m_g_final, v_w_ada, v_b_ada, v_g_norm1, v_w_in, v_g_cq, v_w_uq, v_g_ckv, v_w_ukv, v_rel_bias, v_g_out_a, v_g_out_b, v_w_out, v_g_norm2, v_w_ffn_in, v_w_ffn_out, v_g_final):
    nb = x.shape[0]
    t = nb * S
    xt, tt = x.reshape(t, D), loss_target.reshape(t, D)
    me = 4 * lax.axis_index("x") + 2 * lax.axis_index("y") + lax.axis_index("c")

    early = [jnp.swapaxes(w_in, 1, 2)[0], jnp.swapaxes(w_uq, 1, 2)[0], w_ukv[0]]
    gathered = _gather_two_level([_silu_rows(c)] + [s.astype(BF16) for s in early], "gather_weights")
    cond_all = gathered[0].reshape(N_DEV * nb, D)
    w_in_t = gathered[1].reshape(P_IN, D)
    w_ukv_f = _cols_from_blocks(gathered[3])
    w_k, w_v = _split_w_ukv(w_ukv_f)

    ncol = N_MOD * D // N_DEV
    b_slab = lax.dynamic_slice(b_ada, (0, me * ncol), (1, ncol))
    slab = _mod_slab(cond_all, w_ada, b_slab)
    (mod_rows,) = _exchange([slab.reshape(N_DEV, nb, ncol)], [False], "scatter_mod")
    mod = jnp.transpose(mod_rows, (1, 0, 2)).reshape(nb, N_MOD, D)

    late = [s.astype(BF16) for s in (w_out[0], jnp.swapaxes(w_ffn_in, 1, 2)[0], w_ffn_out[0])]
    late_send, late_recv, late_src, late_land, late_token = _exchange_start(
        late, [_own_block_in_place(s, me) for s in late], [True] * 3, mod_rows, "gather_late_start")
    g_norm1_t = g_norm1 + late_token[:1, :1]

    def late_weights(after):
        w_out_g, w_ffn_in_g, w_ffn_out_g = _exchange_wait(late_send, late_recv, late_src, late_land, [True] * 3, after,
                                                          "gather_late_wait")
        return w_out_g.reshape(D, D), w_ffn_in_g.reshape(2 * D_FF, D), w_ffn_out_g.reshape(D_FF, D)

    flight = {}

    def start_grads(key, src, name):
        land = [_own_block_in_place(lax.dynamic_index_in_dim(s, me, 0, keepdims=False), me) for s in src]
        send, recv, src, land, token = _exchange_start(src, land, [False] * len(src), None, name)
        flight[key] = (send, recv, src, land)
        return token[:1, :1]

    def on_ffn_grads(gw_ffn_in, gw_ffn_out, gw_out):
        return start_grads("ffn", [gw_ffn_in.reshape(N_DEV, 2 * D_FF // N_DEV, D), gw_ffn_out.reshape(N_DEV, D_FF // N_DEV, D),
                                   gw_out.reshape(N_DEV, D // N_DEV, D)], "exchange_ffn_start")

    def on_last_grads(gw):
        return start_grads("rest", [_unpad_w_in(gw["w_in"]).reshape(N_DEV, P_IN // N_DEV, D),
                                    _unpad_w_uq(gw["w_uq"]),
                                    _cols_to_blocks(_join_w_ukv(gw["w_k"], gw["w_v"]))], "exchange_rest_start")

    grad_x, dmod, loss_cols, small, _ = _local_step(
        xt, mod, tt, g_norm1_t, _pad_w_in(w_in_t), g_cq, _pad_w_uq(gathered[2]), g_ckv, w_k, w_v, rel_bias, g_out_a, g_out_b,
        None, g_norm2, None, None, g_final.reshape(1, D), late_weights=late_weights, on_ffn_grads=on_ffn_grads,
        on_last_grads=on_last_grads)

    upd = {}

    def land_and_update(key, names, after, name):
        got = _exchange_wait(*flight[key], [False] * len(names), after, name)
        for n, p in zip(names, got):
            w, m, v = big[n]
            upd[n] = _adamw(p, w, m, v, "adamw_" + n)

    def flip(a):
        return jnp.swapaxes(a, 1, 2)

    big = dict(w_in=(flip(w_in), flip(m_w_in), flip(v_w_in)), w_uq=(flip(w_uq), flip(m_w_uq), flip(v_w_uq)),
               w_ukv=(w_ukv, m_w_ukv, v_w_ukv),
               w_out=(w_out, m_w_out, v_w_out), w_ffn_in=(flip(w_ffn_in), flip(m_w_ffn_in), flip(v_w_ffn_in)),
               w_ffn_out=(w_ffn_out, m_w_ffn_out, v_w_ffn_out))
    land_and_update("ffn", ["w_ffn_in", "w_ffn_out", "w_out"], grad_x, "exchange_ffn_wait")
    land_and_update("rest", ["w_in", "w_uq", "w_ukv"], upd["w_out"][0], "exchange_rest_wait")
    for n in ("w_in", "w_uq", "w_ffn_in"):
        upd[n] = tuple(flip(a) for a in upd[n])

    mine, dmod_blocks = _pack_small(dmod, [small[n] for n, _ in ROW_PARAMS], loss_cols)
    dmod_cols, pay, rel = _exchange([dmod_blocks, mine, small["rel_bias"]], [False, True, True], "exchange_small",
                                    after=upd["w_ukv"][0])
    g_ada = _ada_grad(cond_all, dmod_cols.reshape(N_DEV * nb, ncol))
    upd["w_ada"] = _adamw(g_ada[None], w_ada, m_w_ada, v_w_ada, "adamw_w_ada")
    row = lambda a: a.reshape(1, D)
    small_names = ["b_ada"] + [n for n, _ in ROW_PARAMS] + ["rel_bias"]
    small_w = [b_ada, g_norm1, g_cq, g_ckv, g_out_a, g_out_b, g_norm2, row(g_final), rel_bias]
    small_m = [m_b_ada, m_g_norm1, m_g_cq, m_g_ckv, m_g_out_a, m_g_out_b, m_g_norm2, row(m_g_final), m_rel_bias]
    small_v = [v_b_ada, v_g_norm1, v_g_cq, v_g_ckv, v_g_out_a, v_g_out_b, v_g_norm2, row(v_g_final), v_rel_bias]
    small_upd, loss8 = _small_update(pay, rel, small_w, small_m, small_v)
    upd.update(zip(small_names, small_upd))

    order = ["w_ada", "b_ada", "g_norm1", "w_in", "g_cq", "w_uq", "g_ckv", "w_ukv", "rel_bias", "g_out_a", "g_out_b",
             "w_out", "g_norm2", "w_ffn_in", "w_ffn_out", "g_final"]
    like = dict(g_final=g_final)
    outs = [loss8[0, 0], grad_x.reshape(x.shape)]
    for part in range(4):
        for n in order:
            val = upd[n][part]
            outs.append(val.reshape(like[n].shape) if n in like else val)
    return tuple(outs)
```

```python
import numpy as np
import jax
import jax.numpy as jnp
from jax import lax
from jax.experimental import pallas as pl
from jax.experimental.pallas import tpu as pltpu

F32, BF16 = jnp.float32, jnp.bfloat16

N_DEV = 8
D = 1024
S = 2048
H = 8
E_A = 64
D_A = H * E_A
Q_LORA, KV_LORA = 384, 256
NOPE, ROPE, VDIM = 64, 32, 64
HP = 128
P_IN = 3 * D_A + Q_LORA + KV_LORA + ROPE
P_PAD = 3 * D_A + Q_LORA + KV_LORA + HP
TAIL0 = 3 * D_A
TAIL = P_PAD - TAIL0
D_FF = 2816
N_MOD = 6
EPS = 1e-6
NEG = -1e30
BLK = 128
DILATIONS = (1, 4, 16)
N_BUCKETS, MAX_DISTANCE = 32, 2048
ROPE_THETA = 10000.0
SCALE_A = E_A ** -0.5
SCALE_B = (NOPE + ROPE) ** -0.5
B1, B2, LR, ADAM_EPS, WD, STEP = 0.9, 0.999, 0.001, 1e-8, 0.01, 10
VMEM_LIMIT = 56 * 1024 * 1024


def _cp(*sem):
    return pltpu.CompilerParams(dimension_semantics=sem, vmem_limit_bytes=VMEM_LIMIT)


def _pick(n, prefs):
    for p in prefs:
        if n % p == 0:
            return p
    raise ValueError(f"no tile of {prefs} divides {n}")


OPERAND_BYTES = 6 * 1024 * 1024


def _pick_rows(m, k):
    return _pick(m, [p for p in (1024, 512, 256, 128, 16) if p * k * 2 <= OPERAND_BYTES])


MATMUL_BYTES = 40 * 1024 * 1024


def _stream_rows(m, fixed, per_row):
    return _pick(m, [p for p in (4096, 2048, 1024, 512, 256, 128, 16) if fixed + p * per_row <= MATMUL_BYTES])


def _dot(a, b, dims):
    return lax.dot_general(a, b, (dims, ((), ())), preferred_element_type=F32)


def _mm_nn(a, b, out_dtype, name, after=None):
    m, k = a.shape
    n = b.shape[1]
    tn = _pick(n, (512, 256, 384, 128))
    tm = _stream_rows(m, 4 * k * tn, 4 * k + (2 * jnp.dtype(out_dtype).itemsize + 4) * tn)

    def body(a_ref, b_ref, *rest):
        o_ref = rest[-1]
        o_ref[...] = _dot(a_ref[...], b_ref[...], ((1,), (0,))).astype(o_ref.dtype)

    extra = [] if after is None else [after]
    return pl.pallas_call(
        body, name=name, grid=(m // tm, n // tn),
        in_specs=[pl.BlockSpec((tm, k), lambda i, j: (i, 0)), pl.BlockSpec((k, tn), lambda i, j: (0, j))] + [ANY] * len(extra),
        out_specs=pl.BlockSpec((tm, tn), lambda i, j: (i, j)),
        out_shape=jax.ShapeDtypeStruct((m, n), out_dtype),
        compiler_params=_cp("parallel", "parallel"),
    )(a, b, *extra)


def _mm_nt(a, b, out_dtype, name, after=None):
    m, k = a.shape
    n = b.shape[0]
    tn = _pick(n, (512, 256, 384, 128))
    tm = _stream_rows(m, 4 * k * tn, 4 * k + (2 * jnp.dtype(out_dtype).itemsize + 4) * tn)

    def body(a_ref, b_ref, *rest):
        o_ref = rest[-1]
        o_ref[...] = _dot(a_ref[...], b_ref[...], ((1,), (1,))).astype(o_ref.dtype)

    extra = [] if after is None else [after]
    return pl.pallas_call(
        body, name=name, grid=(m // tm, n // tn),
        in_specs=[pl.BlockSpec((tm, k), lambda i, j: (i, 0)), pl.BlockSpec((tn, k), lambda i, j: (j, 0))] + [ANY] * len(extra),
        out_specs=pl.BlockSpec((tm, tn), lambda i, j: (i, j)),
        out_shape=jax.ShapeDtypeStruct((m, n), out_dtype),
        compiler_params=_cp("parallel", "parallel"),
    )(a, b, *extra)


def _mm_tn(a, bs, name):
    t, m = a.shape
    n = bs[0].shape[1]
    nb_ = len(bs)
    tc = _pick(t, (512, 16))
    tn = _pick(n, (512, 384, 256, 128))
    tm = _pick(m, [p for p in (1024, 512, 384, 256, 128) if (3 * p + 2 * nb_ * tn) * t * 2 <= VMEM_LIMIT - 2 * OPERAND_BYTES])
    if tm <= 256 and nb_ * n * t * 2 <= 2 * OPERAND_BYTES:
        tn = n

    def body(*refs):
        a_ref, b_refs, o_refs, at_ref = refs[0], refs[1:1 + nb_], refs[1 + nb_:1 + 2 * nb_], refs[-1]

        @pl.when(pl.program_id(1) == 0)
        def _():
            def chunk(c, _):
                rows = pl.ds(pl.multiple_of(c * tc, tc), tc)
                at_ref[:, rows] = a_ref[rows, :].T
                return 0

            lax.fori_loop(0, t // tc, chunk, 0)

        for b_ref, o_ref in zip(b_refs, o_refs):
            o_ref[...] = _dot(at_ref[...], b_ref[...], ((1,), (0,))).astype(BF16)

    res = pl.pallas_call(
        body, name=name, grid=(m // tm, n // tn),
        in_specs=[pl.BlockSpec((t, tm), lambda i, j: (0, i))] + [pl.BlockSpec((t, tn), lambda i, j: (0, j))] * nb_,
        out_specs=[pl.BlockSpec((tm, tn), lambda i, j: (i, j))] * nb_,
        out_shape=[jax.ShapeDtypeStruct((m, n), BF16)] * nb_,
        scratch_shapes=[pltpu.VMEM((tm, t), BF16)],
        compiler_params=_cp("parallel", "arbitrary"),
    )(a, *bs)
    return res[0] if nb_ == 1 else res


def _mm_tn_rows(a_list, b, name):
    t, m = a_list[0].shape
    n = b.shape[1]
    na = len(a_list)
    tc, tm = _pick(t, (512, 16)), _pick(m, (256, 128))
    nblk = m // tm

    def body(*refs):
        a_refs, b_ref, o_ref, bt_ref, r_ref = refs[:na], refs[na], refs[na + 1], refs[na + 2], refs[na + 3]
        i = pl.program_id(0)

        @pl.when(i == 0)
        def _():
            def chunk(c, _):
                rows = pl.ds(pl.multiple_of(c * tc, tc), tc)
                bt_ref[:, rows] = b_ref[rows, :].T
                return 0

            lax.fori_loop(0, t // tc, chunk, 0)

        for s, a_ref in enumerate(a_refs):
            @pl.when((i >= s * nblk) & (i < (s + 1) * nblk))
            def _(a_ref=a_ref):
                r_ref[...] = _dot(bt_ref[...], a_ref[...], ((1,), (0,)))
                o_ref[...] = r_ref[...].T.astype(BF16)

    return pl.pallas_call(
        body, name=name, grid=(na * nblk,),
        in_specs=[pl.BlockSpec((t, tm), lambda i, s=s: (0, jnp.clip(i - s * nblk, 0, nblk - 1))) for s in range(na)]
        + [pl.BlockSpec((t, n), lambda i: (0, 0))],
        out_specs=pl.BlockSpec((tm, n), lambda i: (i, 0)),
        out_shape=jax.ShapeDtypeStruct((na * m, n), BF16),
        scratch_shapes=[pltpu.VMEM((n, t), BF16), pltpu.VMEM((n, tm), F32)],
        compiler_params=_cp("arbitrary"),
    )(*a_list, b)


EPI = 256


def _silu_parts(g):
    sg = 0.5 * jnp.tanh(0.5 * g) + 0.5
    return sg, g * sg


def _ffn_in(h2, wt):
    t, k = h2.shape
    tn = _pick(D_FF, (256, 128))
    tm = _stream_rows(t, 8 * k * tn, 4 * k + (3 * 2 * 2 + 2 * 4) * tn)
    nj = D_FF // tn

    def body(h_ref, wg_ref, wu_ref, g_ref, u_ref, a_ref):
        hv = h_ref[...]
        g_all = _dot(hv, wg_ref[...], ((1,), (1,)))
        u_all = _dot(hv, wu_ref[...], ((1,), (1,)))
        for r in range(tm // EPI):
            rows = slice(r * EPI, (r + 1) * EPI)
            g, u = g_all[rows], u_all[rows]
            g_ref[rows, :] = g.astype(BF16)
            u_ref[rows, :] = u.astype(BF16)
            a_ref[rows, :] = (_silu_parts(g)[1] * u).astype(BF16)

    blk = pl.BlockSpec((tm, tn), lambda i, j: (i, j))
    return pl.pallas_call(
        body, name="ffn_in", grid=(t // tm, nj),
        in_specs=[pl.BlockSpec((tm, k), lambda i, j: (i, 0)), pl.BlockSpec((tn, k), lambda i, j: (j, 0)),
                  pl.BlockSpec((tn, k), lambda i, j: (j + nj, 0))],
        out_specs=[blk] * 3, out_shape=[jax.ShapeDtypeStruct((t, D_FF), BF16)] * 3,
        compiler_params=_cp("parallel", "parallel"),
    )(h2, wt, wt)


def _d_act(df, w, g, u):
    t, k = df.shape
    tn = _pick(D_FF, (256, 128))
    tm = _stream_rows(t, 4 * k * tn, 4 * k + (4 * 2 * 2 + 4) * tn)

    def body(df_ref, w_ref, g_ref, u_ref, dg_ref, du_ref):
        da_all = _dot(df_ref[...], w_ref[...], ((1,), (1,)))
        for r in range(tm // EPI):
            rows = slice(r * EPI, (r + 1) * EPI)
            da = da_all[rows]
            gv = g_ref[rows, :].astype(F32)
            sg, silu = _silu_parts(gv)
            dg_ref[rows, :] = ((da * u_ref[rows, :].astype(F32)) * (sg + silu * (1.0 - sg))).astype(BF16)
            du_ref[rows, :] = (da * silu).astype(BF16)

    blk = pl.BlockSpec((tm, tn), lambda i, j: (i, j))
    return pl.pallas_call(
        body, name="d_act", grid=(t // tm, D_FF // tn),
        in_specs=[pl.BlockSpec((tm, k), lambda i, j: (i, 0)), pl.BlockSpec((tn, k), lambda i, j: (j, 0)), blk, blk],
        out_specs=[blk] * 2, out_shape=[jax.ShapeDtypeStruct((t, D_FF), BF16)] * 2,
        compiler_params=_cp("parallel", "parallel"),
    )(df, w, g, u)


def _d_h2(dg, du, wt):
    t = dg.shape[0]
    n = wt.shape[1]
    tm, tn = _pick_rows(t, D_FF), _pick(n, (512, 256, 128))

    def body(dg_ref, du_ref, wg_ref, wu_ref, o_ref):
        o_ref[...] = (_dot(dg_ref[...], wg_ref[...], ((1,), (0,)))
                      + _dot(du_ref[...], wu_ref[...], ((1,), (0,)))).astype(BF16)

    return pl.pallas_call(
        body, name="d_h2", grid=(t // tm, n // tn),
        in_specs=[pl.BlockSpec((tm, D_FF), lambda i, j: (i, 0)), pl.BlockSpec((tm, D_FF), lambda i, j: (i, 0)),
                  pl.BlockSpec((D_FF, tn), lambda i, j: (0, j)), pl.BlockSpec((D_FF, tn), lambda i, j: (1, j))],
        out_specs=pl.BlockSpec((tm, tn), lambda i, j: (i, j)),
        out_shape=jax.ShapeDtypeStruct((t, n), BF16),
        compiler_params=_cp("parallel", "parallel"),
    )(dg, du, wt, wt)


TM = 1024


def _row(w):
    return pl.BlockSpec((TM, w), lambda i: (i, 0))


def _row_at(w, col):
    return pl.BlockSpec((TM, w), lambda i: (i, col))


def _vec(w):
    return pl.BlockSpec((1, w), lambda i: (0, 0))


def _per_ex(w):
    return pl.BlockSpec((1, 1, w), lambda i: (i // (S // TM), 0, 0))


def _pos(w):
    return pl.BlockSpec((TM, w), lambda i: (i % (S // TM), 0))


def _full(shape):
    return pl.BlockSpec(shape, lambda i: (0,) * len(shape))


def _rms(x):
    return lax.rsqrt(jnp.mean(x * x, axis=-1, keepdims=True) + EPS)


def _rms_bwd(n, r, dn):
    return r * (dn - n * jnp.mean(dn * n, axis=-1, keepdims=True))


def _colsum(v):
    return jnp.sum(v, axis=0, keepdims=True)


def _acc_first(i, ref, val, every=None):
    first = (i == 0) if every is None else (i % every == 0)

    @pl.when(first)
    def _():
        ref[...] = jnp.zeros_like(ref)

    ref[...] += val.reshape(ref.shape)


def _pre1(x, g, sc, sh):
    t = x.shape[0]

    def body(x_ref, g_ref, sc_ref, sh_ref, h_ref):
        xv = x_ref[...]
        n = xv * _rms(xv)
        h_ref[...] = ((n * g_ref[...]) * (1.0 + sc_ref[0]) + sh_ref[0]).astype(BF16)

    return pl.pallas_call(
        body, name="pre1", grid=(t // TM,),
        in_specs=[_row(D), _vec(D), _per_ex(D), _per_ex(D)],
        out_specs=_row(D), out_shape=jax.ShapeDtypeStruct((t, D), BF16),
        compiler_params=_cp("parallel"),
    )(x, g, sc, sh)


def _rope_fwd(v, c, sm, sp):
    return v * c + pltpu.roll(v, HP - ROPE // 2, 1) * sm + pltpu.roll(v, ROPE // 2, 1) * sp


def _rope_bwd(dv, c, sm, sp):
    return dv * c + pltpu.roll(dv * sm, ROPE // 2, 1) + pltpu.roll(dv * sp, HP - ROPE // 2, 1)


def _mla_pre(proj, g_cq, g_ckv, w_uq, w_k, w_v, rc, rsm, rsp):
    t = proj.shape[0]

    def body(tail_ref, gq_ref, gkv_ref, wuq_ref, wk_ref, wv_ref, c_ref, sm_ref, sp_ref,
             q_ref, k_ref, v_ref, cqn_ref, ckvn_ref):
        tail = tail_ref[...]
        cq, ckv, kr = tail[:, :Q_LORA], tail[:, Q_LORA:Q_LORA + KV_LORA], tail[:, Q_LORA + KV_LORA:]
        cqn = (cq * _rms(cq) * gq_ref[...]).astype(BF16)
        ckvn = (ckv * _rms(ckv) * gkv_ref[...]).astype(BF16)
        cqn_ref[...] = cqn
        ckvn_ref[...] = ckvn
        c, sm, sp = c_ref[...], sm_ref[...], sp_ref[...]
        q = _dot(cqn, wuq_ref[...], ((1,), (1,)))
        kn = _dot(ckvn, wk_ref[...], ((1,), (0,)))
        v_ref[...] = _dot(ckvn, wv_ref[...], ((1,), (0,))).astype(BF16)
        krr = _rope_fwd(kr, c, sm, sp)
        for h in range(H):
            sl = slice(h * HP, (h + 1) * HP)
            q_ref[:, sl] = _rope_fwd(q[:, sl], c, sm, sp).astype(BF16)
            k_ref[:, sl] = (kn[:, sl] + krr).astype(BF16)

    wide = H * HP
    return pl.pallas_call(
        body, name="mla_pre", grid=(t // TM,),
        in_specs=[_row_at(TAIL, TAIL0 // TAIL), _vec(Q_LORA), _vec(KV_LORA), _full((wide, Q_LORA)),
                  _full((KV_LORA, wide)), _full((KV_LORA, wide)), _pos(HP), _pos(HP), _pos(HP)],
        out_specs=[_row(wide), _row(wide), _row(wide), _row(Q_LORA), _row(KV_LORA)],
        out_shape=[jax.ShapeDtypeStruct((t, wide), BF16)] * 3
        + [jax.ShapeDtypeStruct((t, Q_LORA), BF16), jax.ShapeDtypeStruct((t, KV_LORA), BF16)],
        compiler_params=_cp("parallel"),
    )(proj, g_cq, g_ckv, w_uq, w_k, w_v, rc, rsm, rsp)


def _mla_pre_bwd(proj, dq_, dk_, dv_, dqkv_a, g_cq, g_ckv, w_uq, w_k, w_v, rc, rsm, rsp):
    t = proj.shape[0]
    wide = H * HP

    def body(tail_ref, dq_ref, dk_ref, dv_ref, dqa_ref, dka_ref, dva_ref, gq_ref, gkv_ref, wuq_ref, wk_ref, wv_ref,
             c_ref, sm_ref, sp_ref, dqo_ref, dproj_ref, dgq_ref, dgkv_ref):
        i = pl.program_id(0)
        for n, src in enumerate((dqa_ref, dka_ref, dva_ref)):
            dproj_ref[:, n * D_A:(n + 1) * D_A] = src[...]
        dtail_ref = dproj_ref.at[:, TAIL0:]
        tail = tail_ref[...]
        cq, ckv = tail[:, :Q_LORA], tail[:, Q_LORA:Q_LORA + KV_LORA]
        c, sm, sp = c_ref[...], sm_ref[...], sp_ref[...]
        dkr = jnp.zeros((TM, HP), F32)
        for h in range(H):
            sl = slice(h * HP, (h + 1) * HP)
            dqo_ref[:, sl] = _rope_bwd(dq_ref[:, sl].astype(F32), c, sm, sp).astype(BF16)
            dkr = dkr + dk_ref[:, sl].astype(F32)
        lane = lax.broadcasted_iota(jnp.int32, (TM, HP), 1)
        dkr = jnp.where((lane >= NOPE) & (lane < NOPE + ROPE), _rope_bwd(dkr, c, sm, sp), 0.0)
        dkb = dk_ref[...]
        dvb = dv_ref[...]
        dcqn = _dot(dqo_ref[...], wuq_ref[...], ((1,), (0,)))
        dckvn = _dot(dkb, wk_ref[...], ((1,), (1,))) + _dot(dvb, wv_ref[...], ((1,), (1,)))
        rq, rkv = _rms(cq), _rms(ckv)
        nq, nkv = cq * rq, ckv * rkv
        _acc_first(i, dgq_ref, _colsum(dcqn * nq))
        _acc_first(i, dgkv_ref, _colsum(dckvn * nkv))
        dtail_ref[:, :Q_LORA] = _rms_bwd(nq, rq, dcqn * gq_ref[...]).astype(BF16)
        dtail_ref[:, Q_LORA:Q_LORA + KV_LORA] = _rms_bwd(nkv, rkv, dckvn * gkv_ref[...]).astype(BF16)
        dtail_ref[:, Q_LORA + KV_LORA:] = dkr.astype(BF16)

    return pl.pallas_call(
        body, name="mla_pre_bwd", grid=(t // TM,),
        in_specs=[_row_at(TAIL, TAIL0 // TAIL), _row(wide), _row(wide), _row(wide), _row(D_A), _row(D_A), _row(D_A),
                  _vec(Q_LORA), _vec(KV_LORA), _full((wide, Q_LORA)), _full((KV_LORA, wide)), _full((KV_LORA, wide)),
                  _pos(HP), _pos(HP), _pos(HP)],
        out_specs=[_row(wide), _row(P_PAD), _vec(Q_LORA), _vec(KV_LORA)],
        out_shape=[jax.ShapeDtypeStruct((t, wide), BF16), jax.ShapeDtypeStruct((t, P_PAD), BF16),
                   jax.ShapeDtypeStruct((1, Q_LORA), F32), jax.ShapeDtypeStruct((1, KV_LORA), F32)],
        compiler_params=_cp("arbitrary"),
    )(proj, dq_, dk_, dv_, *dqkv_a, g_cq, g_ckv, w_uq, w_k, w_v, rc, rsm, rsp)


def _post_attn(out_a, out_b, g_a, g_b):
    t = out_a.shape[0]

    def body(a_ref, b_ref, ga_ref, gb_ref, y_ref):
        a, b = a_ref[...], b_ref[...]
        y_ref[:, :D_A] = (a * _rms(a) * ga_ref[...]).astype(BF16)
        y_ref[:, D_A:] = (b * _rms(b) * gb_ref[...]).astype(BF16)

    return pl.pallas_call(
        body, name="post_attn", grid=(t // TM,),
        in_specs=[_row(D_A), _row(D_A), _vec(D_A), _vec(D_A)],
        out_specs=_row(D), out_shape=jax.ShapeDtypeStruct((t, D), BF16),
        compiler_params=_cp("parallel"),
    )(out_a, out_b, g_a, g_b)


def _post_attn_bwd(dy, out_a, out_b, g_a, g_b):
    t = dy.shape[0]

    def body(dy_ref, a_ref, b_ref, ga_ref, gb_ref, da_ref, db_ref, dga_ref, dgb_ref):
        i = pl.program_id(0)
        dy_ = dy_ref[...].astype(F32)
        for src, g_ref, dst, dg_ref, sl in ((a_ref, ga_ref, da_ref, dga_ref, slice(0, D_A)),
                                            (b_ref, gb_ref, db_ref, dgb_ref, slice(D_A, D))):
            v = src[...]
            r = _rms(v)
            n = v * r
            dyv = dy_[:, sl]
            _acc_first(i, dg_ref, _colsum(dyv * n))
            dst[...] = _rms_bwd(n, r, dyv * g_ref[...])

    return pl.pallas_call(
        body, name="post_attn_bwd", grid=(t // TM,),
        in_specs=[_row(D), _row(D_A), _row(D_A), _vec(D_A), _vec(D_A)],
        out_specs=[_row(D_A), _row(D_A), _vec(D_A), _vec(D_A)],
        out_shape=[jax.ShapeDtypeStruct((t, D_A), F32)] * 2 + [jax.ShapeDtypeStruct((1, D_A), F32)] * 2,
        compiler_params=_cp("arbitrary"),
    )(dy, out_a, out_b, g_a, g_b)


def _resid_norm2(x, mix, g1, g, sc, sh):
    t = x.shape[0]

    def body(x_ref, mix_ref, g1_ref, g_ref, sc_ref, sh_ref, x2_ref, h_ref):
        x2 = x_ref[...] + g1_ref[0] * mix_ref[...]
        x2_ref[...] = x2
        n = x2 * _rms(x2)
        h_ref[...] = ((n * g_ref[...]) * (1.0 + sc_ref[0]) + sh_ref[0]).astype(BF16)

    return pl.pallas_call(
        body, name="resid_norm2", grid=(t // TM,),
        in_specs=[_row(D), _row(D), _per_ex(D), _vec(D), _per_ex(D), _per_ex(D)],
        out_specs=[_row(D), _row(D)],
        out_shape=[jax.ShapeDtypeStruct((t, D), F32), jax.ShapeDtypeStruct((t, D), BF16)],
        compiler_params=_cp("parallel"),
    )(x, mix, g1, g, sc, sh)


def _sigmoid(v):
    return 1.0 / (1.0 + jnp.exp(-v))


def _final(x2, f, g2, g_fin, target):
    t = x2.shape[0]
    nb = t // S
    tpb = S // TM

    def body(x2_ref, f_ref, g2_ref, g_ref, t_ref, dx3_ref, df_ref, loss_ref, dgf_ref, dg2_ref):
        i = pl.program_id(0)
        fv = f_ref[...].astype(F32)
        x3 = x2_ref[...] + g2_ref[0] * fv
        r = _rms(x3)
        n = x3 * r
        err = n * g_ref[...] - t_ref[...]
        _acc_first(i, loss_ref, _colsum(err * err))
        dy = err * (1.0 / D)
        _acc_first(i, dgf_ref, _colsum(dy * n))
        dx3 = _rms_bwd(n, r, dy * g_ref[...])
        dx3_ref[...] = dx3
        _acc_first(i, dg2_ref, _colsum(dx3 * fv), every=tpb)
        df_ref[...] = (dx3 * g2_ref[0]).astype(BF16)

    return pl.pallas_call(
        body, name="final", grid=(t // TM,),
        in_specs=[_row(D), _row(D), _per_ex(D), _vec(D), _row(D)],
        out_specs=[_row(D), _row(D), _vec(D), _vec(D), _per_ex(D)],
        out_shape=[jax.ShapeDtypeStruct((t, D), F32), jax.ShapeDtypeStruct((t, D), BF16),
                   jax.ShapeDtypeStruct((1, D), F32), jax.ShapeDtypeStruct((1, D), F32),
                   jax.ShapeDtypeStruct((nb, 1, D), F32)],
        compiler_params=_cp("arbitrary"),
    )(x2, f, g2, g_fin, target)


def _norm_bwd(xin, dh, dres, g, sc, gate=None):
    t = xin.shape[0]
    nb = t // S
    tpb = S // TM
    gated = gate is not None

    def body(*refs):
        if gated:
            x_ref, dh_ref, dres_ref, g_ref, sc_ref, mix_ref, g1_ref, dx_ref, dsh_ref, dsc_ref, dg_ref, dg1_ref, dmix_ref = refs
        else:
            x_ref, dh_ref, dres_ref, g_ref, sc_ref, dx_ref, dsh_ref, dsc_ref, dg_ref = refs
        i = pl.program_id(0)
        xv, dhv = x_ref[...], dh_ref[...].astype(F32)
        r = _rms(xv)
        n = xv * r
        gv = g_ref[...]
        _acc_first(i, dsh_ref, _colsum(dhv), every=tpb)
        _acc_first(i, dsc_ref, _colsum(dhv * (n * gv)), every=tpb)
        dng = dhv * (1.0 + sc_ref[0])
        _acc_first(i, dg_ref, _colsum(dng * n))
        dx = dres_ref[...] + _rms_bwd(n, r, dng * gv)
        dx_ref[...] = dx
        if gated:
            _acc_first(i, dg1_ref, _colsum(dx * mix_ref[...].astype(F32)), every=tpb)
            dmix_ref[...] = (dx * g1_ref[0]).astype(BF16)

    in_specs = [_row(D), _row(D), _row(D), _vec(D), _per_ex(D)]
    out_specs = [_row(D), _per_ex(D), _per_ex(D), _vec(D)]
    out_shape = [jax.ShapeDtypeStruct((t, D), F32), jax.ShapeDtypeStruct((nb, 1, D), F32),
                 jax.ShapeDtypeStruct((nb, 1, D), F32), jax.ShapeDtypeStruct((1, D), F32)]
    args = [xin, dh, dres, g, sc]
    if gated:
        in_specs += [_row(D), _per_ex(D)]
        out_specs += [_per_ex(D), _row(D)]
        out_shape += [jax.ShapeDtypeStruct((nb, 1, D), F32), jax.ShapeDtypeStruct((t, D), BF16)]
        args += list(gate)
    return pl.pallas_call(
        body, name="norm2_bwd" if gated else "norm1_bwd", grid=(t // TM,),
        in_specs=in_specs, out_specs=out_specs, out_shape=out_shape,
        compiler_params=_cp("arbitrary"),
    )(*args)


TQ = 256
TB = 512
FWD_HEADS = 2


def _mla_fwd(q, k, v):
    t = q.shape[0]
    nb = t // S

    def body(q_ref, k_ref, v_ref, o_ref, lse_ref):
        causal = lax.broadcasted_iota(jnp.int32, (TB, TB), 0) >= lax.broadcasted_iota(jnp.int32, (TB, TB), 1)
        heads = [slice(h * HP, (h + 1) * HP) for h in range(FWD_HEADS)]
        for i in range(S // TB):
            ri, past = slice(i * TB, (i + 1) * TB), slice(0, i * TB)
            qhs = [q_ref[ri, sl] for sl in heads]
            sd = [jnp.where(causal, _dot(qh, k_ref[ri, sl], ((1,), (1,))) * SCALE_B, NEG) for qh, sl in zip(qhs, heads)]
            ms = [jnp.max(s, axis=-1, keepdims=True) for s in sd]
            if i:
                so = [_dot(qh, k_ref[past, sl], ((1,), (1,))) * SCALE_B for qh, sl in zip(qhs, heads)]
                ms = [jnp.maximum(m, jnp.max(s, axis=-1, keepdims=True)) for m, s in zip(ms, so)]
            pd = [jnp.exp(s - m) for s, m in zip(sd, ms)]
            ls = [jnp.sum(p, axis=-1, keepdims=True) for p in pd]
            acc = [_dot(p.astype(BF16), v_ref[ri, sl], ((1,), (0,))) for p, sl in zip(pd, heads)]
            if i:
                po = [jnp.exp(s - m) for s, m in zip(so, ms)]
                ls = [l + jnp.sum(p, axis=-1, keepdims=True) for l, p in zip(ls, po)]
                acc = [a + _dot(p.astype(BF16), v_ref[past, sl], ((1,), (0,))) for a, p, sl in zip(acc, po, heads)]
            for pr in range(FWD_HEADS // 2):
                o_ref[ri, pr * HP:(pr + 1) * HP] = acc[2 * pr] / ls[2 * pr] + acc[2 * pr + 1] / ls[2 * pr + 1]
            for sl, m, l in zip(heads, ms, ls):
                lse_ref[ri, sl] = jnp.broadcast_to(m + jnp.log(l), (TB, HP))

    wide2 = pl.BlockSpec((S, FWD_HEADS * HP), lambda b, p: (b, p))
    return pl.pallas_call(
        body, name="mla_fwd", grid=(nb, H // FWD_HEADS),
        in_specs=[wide2, wide2, wide2],
        out_specs=[pl.BlockSpec((S, FWD_HEADS // 2 * HP), lambda b, p: (b, p)), wide2],
        out_shape=[jax.ShapeDtypeStruct((t, H * VDIM), F32), jax.ShapeDtypeStruct((t, H * HP), F32)],
        compiler_params=_cp("parallel", "parallel"),
    )(q, k, v)


def _mla_bwd(q, k, v, o, do, lse):
    t = q.shape[0]
    nb = t // S

    def body(q_ref, k_ref, v_ref, o_ref, do_ref, lse_ref, dq_out, dk_out, dv_out, dq_ref, dk_ref, dv_ref):
        lane = lax.broadcasted_iota(jnp.int32, (TB, HP), 1)
        causal = lax.broadcasted_iota(jnp.int32, (TB, TB), 0) >= lax.broadcasted_iota(jnp.int32, (TB, TB), 1)
        heads = [slice(h * HP, (h + 1) * HP) for h in range(2)]
        nblk = S // TB
        for i in reversed(range(nblk)):
            ri, past = slice(i * TB, (i + 1) * TB), slice(0, i * TB)
            dov = do_ref[ri, :]
            prod = dov * o_ref[ri, :]
            dob = dov.astype(BF16)
            deltas = [jnp.sum(jnp.where((lane < VDIM) if h == 0 else (lane >= VDIM), prod, 0.0), axis=-1, keepdims=True)
                      for h in range(2)]
            qhs = [q_ref[ri, sl] for sl in heads]
            lses = [lse_ref[ri, sl][:, :1] for sl in heads]
            for rows, diagonal in ((ri, True), (past, False)):
                if rows.stop == rows.start:
                    continue
                ps = [jnp.exp(_dot(qh, k_ref[rows, sl], ((1,), (1,))) * SCALE_B - lse) for qh, sl, lse in zip(qhs, heads, lses)]
                if diagonal:
                    ps = [jnp.where(causal, p, 0.0) for p in ps]
                dps = [_dot(dob, v_ref[rows, sl], ((1,), (1,))) for sl in heads]
                dss = [(p * (dp - delta) * SCALE_B).astype(BF16) for p, dp, delta in zip(ps, dps, deltas)]
                for sl, qh, p, ds in zip(heads, qhs, ps, dss):
                    dq = _dot(ds, k_ref[rows, sl], ((1,), (0,)))
                    dk = _dot(ds, qh, ((0,), (0,)))
                    dv = _dot(p.astype(BF16), dob, ((0,), (0,)))
                    if diagonal:
                        dq_ref[ri, sl] = dq
                    else:
                        dq_ref[ri, sl] += dq
                    if i == nblk - 1:
                        dk_ref[rows, sl] = dk
                        dv_ref[rows, sl] = dv
                    else:
                        dk_ref[rows, sl] += dk
                        dv_ref[rows, sl] += dv
        dq_out[...] = dq_ref[...].astype(BF16)
        dk_out[...] = dk_ref[...].astype(BF16)
        dv_out[...] = dv_ref[...].astype(BF16)

    wide2 = pl.BlockSpec((S, 2 * HP), lambda b, p: (b, p))
    pair = pl.BlockSpec((S, HP), lambda b, p: (b, p))
    return pl.pallas_call(
        body, name="mla_bwd", grid=(nb, H // 2),
        in_specs=[wide2, wide2, wide2, pair, pair, wide2],
        out_specs=[wide2, wide2, wide2],
        out_shape=[jax.ShapeDtypeStruct((t, H * HP), BF16)] * 3,
        scratch_shapes=[pltpu.VMEM((S, 2 * HP), F32)] * 3,
        compiler_params=_cp("parallel", "parallel"),
    )(q, k, v, o, do, lse)


def _t5_bucket(dist):
    max_exact = N_BUCKETS // 2
    d = np.maximum(dist, 1).astype(np.float64)
    large = max_exact + (np.log(d / max_exact) / np.log(MAX_DISTANCE / max_exact) * (N_BUCKETS - max_exact)).astype(np.int64)
    large = np.minimum(large, N_BUCKETS - 1)
    return np.where(dist < max_exact, dist, large).astype(np.int32)


def _band_geometry():
    a = np.arange(BLK)[:, None]
    bk = np.arange(2 * BLK)[None, :]
    steps = BLK + a - bk
    valid = (steps >= 0) & (steps <= BLK)
    buckets = np.stack([_t5_bucket(np.clip(steps, 0, BLK) * d) for d in DILATIONS])
    return buckets, valid


def _band_bias(rel_bias):
    buckets, valid = _band_geometry()
    onehot = (jnp.asarray(buckets)[..., None] == jnp.arange(N_BUCKETS)).astype(F32)
    bias = jnp.einsum("rqkn,nh->rhqk", onehot, rel_bias, precision=lax.Precision.HIGHEST)
    bias = jnp.where(jnp.asarray(valid)[None, None], bias, NEG)
    return bias.reshape(3, H // 2, 2 * BLK, 2 * BLK)


def _dil_items():
    items = []
    for r, d in enumerate(DILATIONS):
        for res in range(d):
            for blk in range(S // d // BLK):
                items.append((r, d, blk * BLK * d + res, blk > 0))
    return items


GROUP = 4


def _strided(start, d):
    return pl.ds(start, BLK) if d == 1 else pl.ds(start, BLK, stride=d)


def _stack_heads(tile, own):
    return jnp.where(own, jnp.concatenate([tile, tile], axis=0), 0.0).astype(BF16)


def _own_lanes():
    row = lax.broadcasted_iota(jnp.int32, (2 * BLK, HP), 0)
    lane = lax.broadcasted_iota(jnp.int32, (2 * BLK, HP), 1)
    return (lane < E_A) == (row < BLK)


def _dil_fwd(proj, biasm):
    t = proj.shape[0]
    nb = t // S

    def body(q_ref, k_ref, v_ref, b_ref, o_ref, lse_ref, ob_ref, lb_ref):
        lane = lax.broadcasted_iota(jnp.int32, (BLK, HP), 1)
        own = _own_lanes()
        items = _dil_items()
        for g in range(0, len(items), GROUP):
            grp = items[g:g + GROUP]
            ss, vts = [], []
            for r, d, start, has_prev in grp:
                cur = _strided(start, d)
                rows = [_strided(start - BLK * d, d), cur] if has_prev else [cur]
                q2 = _stack_heads(q_ref[cur, :] * SCALE_A, own)
                kt = jnp.concatenate([k_ref[x, :] for x in rows], axis=0).astype(BF16)
                vts.append(jnp.concatenate([v_ref[x, :] for x in rows], axis=0).astype(BF16))
                bias = b_ref[r, 0] if has_prev else b_ref[r, 0, :, BLK:]
                ss.append(_dot(q2, kt, ((1,), (1,))) + bias)
            ms = [jnp.max(s, axis=-1, keepdims=True) for s in ss]
            ps = [jnp.exp(s - m) for s, m in zip(ss, ms)]
            ls = [jnp.sum(p, axis=-1, keepdims=True) for p in ps]
            for (r, d, start, _), p, vt, m, l in zip(grp, ps, vts, ms, ls):
                cur = _strided(start, d)
                o2 = _dot(p.astype(BF16), vt, ((1,), (0,))) / l
                lse2 = m + jnp.log(l)
                ob_ref[r, cur, :] = jnp.where(lane < E_A, o2[:BLK], o2[BLK:])
                lb_ref[r, cur, :] = jnp.where(lane < E_A, lse2[:BLK], lse2[BLK:])

        def merge(c, _):
            rows = pl.ds(pl.multiple_of(c * TQ, TQ), TQ)
            l0, l1, l2 = lb_ref[0, rows, :], lb_ref[1, rows, :], lb_ref[2, rows, :]
            m = jnp.maximum(jnp.maximum(l0, l1), l2)
            e0, e1, e2 = jnp.exp(l0 - m), jnp.exp(l1 - m), jnp.exp(l2 - m)
            tot = e0 + e1 + e2
            o_ref[rows, :] = (e0 * ob_ref[0, rows, :] + e1 * ob_ref[1, rows, :] + e2 * ob_ref[2, rows, :]) / tot
            lse_ref[rows, :] = m + jnp.log(tot)
            return 0

        lax.fori_loop(0, S // TQ, merge, 0)

    npair = H // 2
    return pl.pallas_call(
        body, name="dil_fwd", grid=(nb, npair),
        in_specs=[pl.BlockSpec((S, HP), lambda b, p: (b, p)), pl.BlockSpec((S, HP), lambda b, p: (b, npair + p)),
                  pl.BlockSpec((S, HP), lambda b, p: (b, 2 * npair + p)),
                  pl.BlockSpec((3, 1, 2 * BLK, 2 * BLK), lambda b, p: (0, p, 0, 0))],
        out_specs=[pl.BlockSpec((S, HP), lambda b, p: (b, p))] * 2,
        out_shape=[jax.ShapeDtypeStruct((t, D_A), F32)] * 2,
        scratch_shapes=[pltpu.VMEM((3, S, HP), F32), pltpu.VMEM((3, S, HP), F32)],
        compiler_params=_cp("parallel", "parallel"),
    )(proj, proj, proj, biasm)


def _dil_bwd(proj, biasm, o, do, lse):
    t = proj.shape[0]
    nb = t // S

    def body(q_ref, k_ref, v_ref, b_ref, o_ref, do_ref, lse_ref, dq_out, dk_out, dv_out, ds_ref, dq_ref, dk_ref, dv_ref):
        dq_ref[...] = jnp.zeros_like(dq_ref)
        dk_ref[...] = jnp.zeros_like(dk_ref)
        dv_ref[...] = jnp.zeros_like(dv_ref)
        ds_ref[...] = jnp.zeros_like(ds_ref)
        lane = lax.broadcasted_iota(jnp.int32, (BLK, HP), 1)
        own = _own_lanes()
        items = _dil_items()
        for g in range(0, len(items), GROUP):
            grp = items[g:g + GROUP]
            q2s, kts, do2s, ss, dps, lse2s, delta2s = [], [], [], [], [], [], []
            for r, d, start, has_prev in grp:
                cur = _strided(start, d)
                rows = [_strided(start - BLK * d, d), cur] if has_prev else [cur]
                q2 = _stack_heads(q_ref[cur, :] * SCALE_A, own)
                kt = jnp.concatenate([k_ref[x, :] for x in rows], axis=0).astype(BF16)
                vt = jnp.concatenate([v_ref[x, :] for x in rows], axis=0).astype(BF16)
                dot_ = do_ref[cur, :]
                prod = dot_ * o_ref[cur, :]
                lset = lse_ref[cur, :]
                do2 = _stack_heads(dot_, own)
                bias = b_ref[r, 0] if has_prev else b_ref[r, 0, :, BLK:]
                ss.append(_dot(q2, kt, ((1,), (1,))) + bias)
                dps.append(_dot(do2, vt, ((1,), (1,))))
                lse2s.append(jnp.concatenate([lset[:, :1], lset[:, E_A:E_A + 1]], axis=0))
                delta2s.append(jnp.concatenate([jnp.sum(jnp.where(lane < E_A, prod, 0.0), axis=-1, keepdims=True),
                                                jnp.sum(jnp.where(lane >= E_A, prod, 0.0), axis=-1, keepdims=True)], axis=0))
                q2s.append(q2)
                kts.append(kt)
                do2s.append(do2)
            ps = [jnp.exp(s - lse2) for s, lse2 in zip(ss, lse2s)]
            dls = [p * (dp - delta2) for p, dp, delta2 in zip(ps, dps, delta2s)]
            for (r, d, start, has_prev), q2, kt, do2, p, dl in zip(grp, q2s, kts, do2s, ps, dls):
                cur = _strided(start, d)
                dsb = dl.astype(BF16)
                dq2 = _dot(dsb, kt, ((1,), (0,))) * SCALE_A
                dkt = _dot(dsb, q2, ((0,), (0,)))
                dvt = _dot(p.astype(BF16), do2, ((0,), (0,)))
                dq_ref[cur, :] += jnp.where(lane < E_A, dq2[:BLK], dq2[BLK:])
                if has_prev:
                    prev = _strided(start - BLK * d, d)
                    ds_ref[0, r, 0] += dl
                    dk_ref[prev, :] += dkt[:BLK]
                    dv_ref[prev, :] += dvt[:BLK]
                    dk_ref[cur, :] += dkt[BLK:]
                    dv_ref[cur, :] += dvt[BLK:]
                else:
                    ds_ref[0, r, 0, :, BLK:] += dl
                    dk_ref[cur, :] += dkt
                    dv_ref[cur, :] += dvt
        dq_out[...] = dq_ref[...].astype(BF16)
        dk_out[...] = dk_ref[...].astype(BF16)
        dv_out[...] = dv_ref[...].astype(BF16)

    npair = H // 2
    pair = pl.BlockSpec((S, HP), lambda b, p: (b, p))
    return pl.pallas_call(
        body, name="dil_bwd", grid=(nb, npair),
        in_specs=[pair, pl.BlockSpec((S, HP), lambda b, p: (b, npair + p)),
                  pl.BlockSpec((S, HP), lambda b, p: (b, 2 * npair + p)),
                  pl.BlockSpec((3, 1, 2 * BLK, 2 * BLK), lambda b, p: (0, p, 0, 0)), pair, pair, pair],
        out_specs=[pair, pair, pair, pl.BlockSpec((1, 3, 1, 2 * BLK, 2 * BLK), lambda b, p: (b, 0, p, 0, 0))],
        out_shape=[jax.ShapeDtypeStruct((t, D_A), BF16)] * 3 + [jax.ShapeDtypeStruct((nb, 3, npair, 2 * BLK, 2 * BLK), F32)],
        scratch_shapes=[pltpu.VMEM((S, HP), F32)] * 3,
        compiler_params=_cp("parallel", "parallel"),
    )(proj, proj, proj, biasm, o, do, lse)


def _rel_bias_grad(dlogits):
    nb = dlogits.shape[0]
    buckets, _ = _band_geometry()
    kk = 3 * BLK * 2 * BLK
    dl = jnp.transpose(dlogits.reshape(nb, 3, H, BLK, 2 * BLK), (0, 2, 1, 3, 4)).reshape(nb, H, kk)
    bk = jnp.asarray(buckets.reshape(1, kk))
    tk = kk // 4

    def body(dl_ref, bk_ref, o_ref):
        j = pl.program_id(0)
        onehot = (bk_ref[...] == lax.broadcasted_iota(jnp.int32, (N_BUCKETS, tk), 0)).astype(F32)
        tot = dl_ref[0]
        for b in range(1, nb):
            tot = tot + dl_ref[b]
        part = lax.dot_general(onehot, tot, ((((1,), (1,))), ((), ())), preferred_element_type=F32,
                               precision=lax.Precision.HIGHEST)
        _acc_first(j, o_ref, part)

    return pl.pallas_call(
        body, name="rel_bias_grad", grid=(kk // tk,),
        in_specs=[pl.BlockSpec((nb, H, tk), lambda j: (0, 0, j)), pl.BlockSpec((1, tk), lambda j: (0, j))],
        out_specs=pl.BlockSpec((N_BUCKETS, H), lambda j: (0, 0)),
        out_shape=jax.ShapeDtypeStruct((N_BUCKETS, H), F32),
        compiler_params=_cp("arbitrary"),
    )(dl, bk)


def _mesh_place():
    x, y, c = lax.axis_index("x"), lax.axis_index("y"), lax.axis_index("c")
    return x, y, c


def _peer(k):
    x, y, c = _mesh_place()
    px = 1 - x if k & 4 else x
    py = 1 - y if k & 2 else y
    pc = 1 - c if k & 1 else c
    return (px, py, pc), 4 * px + 2 * py + pc


ANY = pl.BlockSpec(memory_space=pl.ANY)


def _exchange(arrays, gathers, name, after=None):
    n_arr = len(arrays)

    def body(*refs):
        ins, outs = refs[:n_arr], refs[n_arr + 1:2 * n_arr + 1]
        send, recv, loc = refs[2 * n_arr + 1:]
        x, y, c = _mesh_place()
        me = 4 * x + 2 * y + c
        local = [pltpu.make_async_copy(ins[a] if gathers[a] else ins[a].at[me], outs[a].at[me], loc.at[a])
                 for a in range(n_arr)]
        remote = _peer_copies(ins, outs, send, recv, gathers)
        for cp in local:
            cp.start()
        for put, _ in remote:
            put.start()
        for cp in local:
            cp.wait()
        for put, got in remote:
            put.wait_send()
            got.wait_recv()

    return pl.pallas_call(
        body, name=name,
        in_specs=[ANY] * (n_arr + 1), out_specs=[ANY] * n_arr,
        out_shape=[jax.ShapeDtypeStruct(((N_DEV,) if g else ()) + a.shape, a.dtype) for a, g in zip(arrays, gathers)],
        scratch_shapes=[pltpu.SemaphoreType.DMA((n_arr * (N_DEV - 1),)), pltpu.SemaphoreType.DMA((n_arr * (N_DEV - 1),)),
                        pltpu.SemaphoreType.DMA((n_arr,))],
        compiler_params=pltpu.CompilerParams(has_side_effects=True),
    )(*arrays, arrays[0] if after is None else after)


def _gather_two_level(arrays, name):
    n_arr = len(arrays)
    per = N_DEV - 1

    def body(*refs):
        ins, outs = refs[:n_arr], refs[n_arr:2 * n_arr]
        send, recv, loc = refs[2 * n_arr:]
        x, y, c = _mesh_place()
        me, sibling = (x, y, c), (x, y, 1 - c)
        chips = [(1 - x, y), (x, 1 - y), (1 - x, 1 - y)]

        def block(a, place):
            px, py, pc = place
            return outs[a].at[4 * px + 2 * py + pc]

        def copy(a, k, place, to, src=None):
            dst = block(a, place)
            return pltpu.make_async_remote_copy(dst if src is None else src, dst, send.at[a * per + k], recv.at[a * per + k],
                                                device_id=to, device_id_type=pl.DeviceIdType.MESH)

        local = [pltpu.make_async_copy(ins[a], block(a, me), loc.at[a]) for a in range(n_arr)]
        for cp in local:
            cp.start()
        first = []
        for a in range(n_arr):
            first.append(copy(a, 0, me, sibling, src=ins[a]))
            first += [copy(a, 1 + j, me, (*chip, c), src=ins[a]) for j, chip in enumerate(chips)]
        for cp in first:
            cp.start()
        passed = []
        for j, chip in enumerate(chips):
            for a in range(n_arr):
                copy(a, 1 + j, (*chip, c), me).wait_recv()
                passed.append(copy(a, 4 + j, (*chip, c), sibling))
                passed[-1].start()
        for a in range(n_arr):
            copy(a, 0, sibling, me).wait_recv()
            for j, chip in enumerate(chips):
                copy(a, 4 + j, (*chip, 1 - c), me).wait_recv()
        for cp in first + passed:
            cp.wait_send()
        for cp in local:
            cp.wait()

    return pl.pallas_call(
        body, name=name,
        in_specs=[ANY] * n_arr, out_specs=[ANY] * n_arr,
        out_shape=[jax.ShapeDtypeStruct((N_DEV,) + a.shape, a.dtype) for a in arrays],
        scratch_shapes=[pltpu.SemaphoreType.DMA((n_arr * per,)), pltpu.SemaphoreType.DMA((n_arr * per,)),
                        pltpu.SemaphoreType.DMA((n_arr,))],
        compiler_params=pltpu.CompilerParams(has_side_effects=True),
    )(*arrays)


HBM = pl.BlockSpec(memory_space=pltpu.HBM)
SEM = pl.BlockSpec(memory_space=pltpu.SEMAPHORE)
DATAFLOW = pltpu.SideEffectType.DATAFLOW_SIDE_EFFECTING


def _own_block_in_place(block, me):
    land = lax.empty((N_DEV,) + block.shape, block.dtype)
    return lax.dynamic_update_slice(land, block[None], (me,) + (0,) * block.ndim)


def _peer_copies(srcs, lands, send, recv, gathers):
    x, y, c = _mesh_place()
    me = 4 * x + 2 * y + c
    out = []
    for a, (src, land) in enumerate(zip(srcs, lands)):
        for k in range(1, N_DEV):
            dev, idx = _peer(k)
            sem = a * (N_DEV - 1) + k - 1
            mine = src if gathers[a] else src.at[idx]
            put = pltpu.make_async_remote_copy(mine, land.at[me], send.at[sem], recv.at[sem],
                                               device_id=dev, device_id_type=pl.DeviceIdType.MESH)
            got = pltpu.make_async_remote_copy(mine, land.at[idx], send.at[sem], recv.at[sem],
                                               device_id=dev, device_id_type=pl.DeviceIdType.MESH)
            out.append((put, got))
    return out


def _exchange_start(srcs, lands, gather, after, name):
    n = len(srcs)
    extra = [] if after is None else [after]

    def body(*refs):
        srcs_, lands_ = refs[:n], refs[n:2 * n]
        send, recv = refs[2 * n + len(extra)], refs[2 * n + len(extra) + 1]
        for put, _ in _peer_copies(srcs_, lands_, send, recv, gather):
            put.start()
        refs[-1][...] = jnp.zeros_like(refs[-1])

    nsem = n * (N_DEV - 1)
    thru = [pltpu.HBM(a.shape, a.dtype) for a in list(srcs) + list(lands)]
    res = pl.pallas_call(
        body, name=name,
        out_shape=(pltpu.SemaphoreType.DMA((nsem,)), pltpu.SemaphoreType.DMA((nsem,)), *thru, jax.ShapeDtypeStruct((8, 128), F32)),
        in_specs=[HBM] * (2 * n) + [ANY] * len(extra),
        out_specs=(SEM, SEM, *([HBM] * (2 * n)), pl.BlockSpec(memory_space=pltpu.VMEM)),
        input_output_aliases={i: 2 + i for i in range(2 * n)},
        compiler_params=pltpu.CompilerParams(has_side_effects=DATAFLOW),
    )(*[pltpu.with_memory_space_constraint(a, pltpu.HBM) for a in list(srcs) + list(lands)], *extra)
    return res[0], res[1], list(res[2:2 + n]), list(res[2 + n:2 + 2 * n]), res[-1]


def _exchange_wait(send, recv, srcs, lands, gather, after, name):
    n = len(srcs)

    def body(*refs):
        srcs_, lands_, send_, recv_ = refs[:n], refs[n:2 * n], refs[2 * n], refs[2 * n + 1]
        for put, got in _peer_copies(srcs_, lands_, send_, recv_, gather):
            put.wait_send()
            got.wait_recv()

    thru = [pltpu.HBM(a.shape, a.dtype) for a in list(srcs) + list(lands)]
    res = pl.pallas_call(
        body, name=name, out_shape=tuple(thru),
        in_specs=[HBM] * (2 * n) + [SEM, SEM, ANY], out_specs=tuple([HBM] * (2 * n)),
        input_output_aliases={i: i for i in range(2 * n)},
        compiler_params=pltpu.CompilerParams(has_side_effects=DATAFLOW),
    )(*srcs, *lands, send, recv, after)
    return list(res[n:])


def _silu_rows(c):
    def body(c_ref, o_ref):
        v = c_ref[...]
        o_ref[...] = v * _sigmoid(v)

    return pl.pallas_call(body, name="cond", out_shape=jax.ShapeDtypeStruct(c.shape, F32))(c)


def _mod_slab(cond_all, w_ada, b_slab):
    def body(c_ref, w_ref, b_ref, o_ref):
        o_ref[...] = _dot(c_ref[...].astype(BF16), w_ref[0].astype(BF16), ((1,), (0,))) + b_ref[...]

    return pl.pallas_call(body, name="mod_slab",
                          out_shape=jax.ShapeDtypeStruct((cond_all.shape[0], w_ada.shape[2]), F32),
                          compiler_params=pltpu.CompilerParams(vmem_limit_bytes=VMEM_LIMIT))(cond_all, w_ada, b_slab)


def _ada_grad(cond_all, dmod_cols):
    def body(c_ref, d_ref, o_ref):
        o_ref[...] = _dot(c_ref[...].astype(BF16), d_ref[...].astype(BF16), ((0,), (0,)))

    return pl.pallas_call(body, name="ada_grad",
                          out_shape=jax.ShapeDtypeStruct((cond_all.shape[1], dmod_cols.shape[1]), F32),
                          compiler_params=pltpu.CompilerParams(vmem_limit_bytes=VMEM_LIMIT))(cond_all, dmod_cols)


def _adam_math(g, w, m, v):
    m2 = B1 * m + (1.0 - B1) * g
    v2 = B2 * v + (1.0 - B2) * (g * g)
    m_hat = m2 / (1.0 - B1 ** STEP)
    v_hat = v2 / (1.0 - B2 ** STEP)
    return -LR * (m_hat / (jnp.sqrt(v_hat) + ADAM_EPS) + WD * w), m2, v2


def _adamw(parts, w, m, v, name):
    n, rows, cols = parts.shape
    tr = max([p for p in range(16, 513, 16) if rows % p == 0] or [rows])

    def body(p_ref, w_ref, m_ref, v_ref, g_ref, d_ref, m2_ref, v2_ref):
        g = p_ref[0].astype(F32)
        for s in range(1, n):
            g = g + p_ref[s].astype(F32)
        g_ref[0] = g
        d_ref[0], m2_ref[0], v2_ref[0] = _adam_math(g, w_ref[0], m_ref[0], v_ref[0])

    blk = pl.BlockSpec((1, tr, cols), lambda i: (0, i, 0))
    return pl.pallas_call(
        body, name=name, grid=(rows // tr,),
        in_specs=[pl.BlockSpec((n, tr, cols), lambda i: (0, i, 0)), blk, blk, blk],
        out_specs=[blk] * 4, out_shape=[jax.ShapeDtypeStruct((1, rows, cols), F32)] * 4,
        compiler_params=_cp("parallel"),
    )(parts, w, m, v)


ROW_PARAMS = (("g_norm1", D), ("g_cq", Q_LORA), ("g_ckv", KV_LORA), ("g_out_a", D_A), ("g_out_b", D_A), ("g_norm2", D),
              ("g_final", D))
LOSS_ROW = N_MOD + len(ROW_PARAMS)
PAY_ROWS = 16
NCOL = N_MOD * D // N_DEV


def _pack_small(dmods, rows, loss_cols):
    nb = dmods[0].shape[0]
    nrow = len(ROW_PARAMS)

    def body(*refs):
        dm, rw, loss_ref, pay_ref, blk_ref = refs[:N_MOD], refs[N_MOD:N_MOD + nrow], refs[N_MOD + nrow], refs[-2], refs[-1]
        pay_ref[...] = jnp.zeros_like(pay_ref)
        for k in range(N_MOD):
            tot = dm[k][0]
            for b in range(1, nb):
                tot = tot + dm[k][b]
            pay_ref[k:k + 1, :] = tot
        for i, (_, n) in enumerate(ROW_PARAMS):
            pay_ref[N_MOD + i:N_MOD + i + 1, :n] = rw[i][...]
        pay_ref[LOSS_ROW:LOSS_ROW + 1, :] = loss_ref[...]
        for j in range(N_DEV):
            done = 0
            while done < NCOL:
                seg, off = divmod(j * NCOL + done, D)
                ln = min(NCOL - done, D - off)
                for b in range(nb):
                    blk_ref[j, b:b + 1, done:done + ln] = dm[seg][b][:, off:off + ln]
                done += ln

    return pl.pallas_call(
        body, name="pack_small",
        out_shape=[jax.ShapeDtypeStruct((PAY_ROWS, D), F32), jax.ShapeDtypeStruct((N_DEV, nb, NCOL), F32)],
    )(*dmods, *rows, loss_cols)


def _small_update(pay, rel, ws, ms, vs):
    n_par = len(ws)

    def body(*refs):
        pay_ref, rel_ref = refs[:2]
        w_refs, m_refs, v_refs = (refs[2 + s * n_par:2 + (s + 1) * n_par] for s in range(3))
        outs, loss_ref = refs[2 + 3 * n_par:-1], refs[-1]
        tot, rtot = pay_ref[0], rel_ref[0]
        for s in range(1, N_DEV):
            tot, rtot = tot + pay_ref[s], rtot + rel_ref[s]

        def update(p, g, sl):
            outs[4 * p][:, sl] = g
            outs[4 * p + 1][:, sl], outs[4 * p + 2][:, sl], outs[4 * p + 3][:, sl] = _adam_math(
                g, w_refs[p][:, sl], m_refs[p][:, sl], v_refs[p][:, sl])

        for k in range(N_MOD):
            update(0, tot[k:k + 1, :], slice(k * D, (k + 1) * D))
        for i, (_, n) in enumerate(ROW_PARAMS):
            update(1 + i, tot[N_MOD + i:N_MOD + i + 1, :n], slice(0, n))
        update(n_par - 1, rtot, slice(0, H))
        loss_ref[...] = jnp.broadcast_to((0.5 / D) * jnp.sum(tot[LOSS_ROW:LOSS_ROW + 1, :]), loss_ref.shape)

    shapes = [jax.ShapeDtypeStruct(w.shape, F32) for w in ws for _ in range(4)]
    res = pl.pallas_call(
        body, name="small_update", out_shape=shapes + [jax.ShapeDtypeStruct((8, 128), F32)],
    )(pay, rel, *ws, *ms, *vs)
    return [tuple(res[4 * p:4 * p + 4]) for p in range(n_par)], res[-1]


def _cols_from_blocks(g):
    return jnp.transpose(g, (1, 0, 2)).reshape(g.shape[1], N_DEV * g.shape[2])


def _cols_to_blocks(w):
    r, c = w.shape
    return jnp.transpose(w.reshape(r, N_DEV, c // N_DEV), (1, 0, 2))


def _pad_w_in(wt):
    z = jnp.zeros((NOPE, wt.shape[1]), wt.dtype)
    return jnp.concatenate([wt[:P_IN - ROPE], z, wt[P_IN - ROPE:], z[:HP - NOPE - ROPE]], axis=0)


def _unpad_w_in(gt):
    k0 = P_IN - ROPE + NOPE
    return jnp.concatenate([gt[:P_IN - ROPE], gt[k0:k0 + ROPE]], axis=0)


def _pad_w_uq(wt):
    return jnp.pad(wt, ((0, 0), (0, HP - NOPE - ROPE), (0, 0))).reshape(H * HP, Q_LORA)


def _unpad_w_uq(gt):
    return gt.reshape(H, HP, Q_LORA)[:, :NOPE + ROPE]


def _split_w_ukv(w):
    w4 = w.reshape(KV_LORA, H // 2, 2, HP)
    z = jnp.zeros((KV_LORA, H // 2, NOPE), w.dtype)
    kn, vv = w4[..., :NOPE], w4[..., NOPE:]
    w_k = jnp.stack([jnp.concatenate([kn[:, :, 0], z], -1), jnp.concatenate([kn[:, :, 1], z], -1)], axis=2)
    w_v = jnp.stack([jnp.concatenate([vv[:, :, 0], z], -1), jnp.concatenate([z, vv[:, :, 1]], -1)], axis=2)
    return w_k.reshape(KV_LORA, H * HP), w_v.reshape(KV_LORA, H * HP)


def _join_w_ukv(g_k, g_v):
    gk = g_k.reshape(KV_LORA, H // 2, 2, HP)
    gv = g_v.reshape(KV_LORA, H // 2, 2, HP)
    even = jnp.concatenate([gk[:, :, 0, :NOPE], gv[:, :, 0, :VDIM]], -1)
    odd = jnp.concatenate([gk[:, :, 1, :NOPE], gv[:, :, 1, VDIM:]], -1)
    return jnp.stack([even, odd], axis=2).reshape(KV_LORA, H * HP)


def _rope_tables():
    half = ROPE // 2
    inv = np.float32(ROPE_THETA) ** (-np.arange(half, dtype=np.float32) / np.float32(half))
    ang = np.arange(S, dtype=np.float32)[:, None] * inv[None, :].astype(np.float32)
    cos, sin = np.cos(ang).astype(np.float32), np.sin(ang).astype(np.float32)
    ones, zeros = np.ones((S, NOPE), np.float32), np.zeros((S, NOPE), np.float32)
    tail1, tail0 = np.ones((S, HP - NOPE - ROPE), np.float32), np.zeros((S, HP - NOPE - ROPE), np.float32)
    zh = np.zeros((S, half), np.float32)
    c = np.concatenate([ones, cos, cos, tail1], axis=1)
    sm = np.concatenate([zeros, -sin, zh, tail0], axis=1)
    sp = np.concatenate([zeros, zh, sin, tail0], axis=1)
    return jnp.asarray(c), jnp.asarray(sm), jnp.asarray(sp)


def _local_step(x, mod, target, g_norm1, w_in_p, g_cq, w_uq_p, g_ckv, w_k, w_v, rel_bias, g_out_a, g_out_b, w_out,
                g_norm2, w_ffn_in, w_ffn_out, g_final, late_weights=None, on_ffn_grads=None, on_last_grads=None):
    nb = x.shape[0] // S
    sh1, sc1, g1, sh2, sc2, g2 = (mod[:, n].reshape(nb, 1, D) for n in range(N_MOD))
    rc, rsm, rsp = _rope_tables()
    biasm = _band_bias(rel_bias)

    h1 = _pre1(x, g_norm1, sc1, sh1)
    proj = _mm_nt(h1, w_in_p, F32, "proj")
    q, k, v, cqn, ckvn = _mla_pre(proj, g_cq, g_ckv, w_uq_p, w_k, w_v, rc, rsm, rsp)
    out_b, lse_b = _mla_fwd(q, k, v)
    out_a, lse_a = _dil_fwd(proj, biasm)
    y = _post_attn(out_a, out_b, g_out_a, g_out_b)
    if late_weights is not None:
        w_out, w_ffn_in, w_ffn_out = late_weights(y)
    mix = _mm_nn(y, w_out, BF16, "mix")
    x2, h2 = _resid_norm2(x, mix, g1, g_norm2, sc2, sh2)
    ffn_g, ffn_u, act = _ffn_in(h2, w_ffn_in)
    f = _mm_nn(act, w_ffn_out, BF16, "ffn_out")
    dx3, df, loss_cols, dg_final, dg2 = _final(x2, f, g2, g_final, target)

    dg_, du_ = _d_act(df, w_ffn_out, ffn_g, ffn_u)
    gw_ffn_out = _mm_tn_rows([act], df, "gw_ffn_out")
    dh2 = _d_h2(dg_, du_, w_ffn_in)
    gw_ffn_in = _mm_tn_rows([dg_, du_], h2, "gw_ffn_in")
    dx2, dsh2, dsc2, dg_norm2, dg1, dmix = _norm_bwd(x2, dh2, dx3, g_norm2, sc2, gate=(mix, g1))
    dy = _mm_nt(dmix, w_out, BF16, "d_y")
    gw_out = _mm_tn(y, [dmix], "gw_out")
    if on_ffn_grads is not None:
        g_out_a = g_out_a + on_ffn_grads(gw_ffn_in, gw_ffn_out, gw_out)
    dout_a, dout_b, dg_out_a, dg_out_b = _post_attn_bwd(dy, out_a, out_b, g_out_a, g_out_b)
    dq_b, dk_b, dv_b = _mla_bwd(q, k, v, out_b, dout_b, lse_b)
    dq_a, dk_a, dv_a, dlogits = _dil_bwd(proj, biasm, out_a, dout_a, lse_a)
    g_rel = _rel_bias_grad(dlogits)
    dqr, dproj, dg_cq, dg_ckv = _mla_pre_bwd(proj, dq_b, dk_b, dv_b, (dq_a, dk_a, dv_a), g_cq, g_ckv, w_uq_p, w_k, w_v,
                                             rc, rsm, rsp)
    gw_uq = _mm_tn(dqr, [cqn], "gw_uq")
    gw_k, gw_v = _mm_tn(ckvn, [dk_b, dv_b], "gw_kv")
    gw_in = _mm_tn_rows([dproj], h1, "gw_in")
    if on_last_grads is not None:
        started = on_last_grads(dict(w_in=gw_in, w_uq=gw_uq, w_k=gw_k, w_v=gw_v))
    else:
        started = None
    dh1 = _mm_nn(dproj, w_in_p, BF16, "d_h1", after=started)
    grad_x, dsh1, dsc1, dg_norm1 = _norm_bwd(x, dh1, dx2, g_norm1, sc1)

    dmod = [dsh1, dsc1, dg1, dsh2, dsc2, dg2]
    small = dict(g_norm1=dg_norm1, g_cq=dg_cq, g_ckv=dg_ckv, rel_bias=g_rel, g_out_a=dg_out_a, g_out_b=dg_out_b,
                 g_norm2=dg_norm2, g_final=dg_final)
    big = dict(w_in=gw_in, w_uq=gw_uq, w_k=gw_k, w_v=gw_v, w_out=gw_out, w_ffn_in=gw_ffn_in, w_ffn_out=gw_ffn_out)
    return grad_x, dmod, loss_cols, small, big


def kernel(x, c, w_ada, b_ada, g_norm1, w_in, g_cq, w_uq, g_ckv, w_ukv, rel_bias, g_out_a, g_out_b, w_out, g_norm2, w_ffn_in, w_ffn_out, g_final, loss_target, m_w_ada, m_b_ada, m_g_norm1, m_w_in, m_g_cq, m_w_uq, m_g_ckv, m_w_ukv, m_rel_bias, m_g_out_a, m_g_out_b, m_w_out, m_g_norm2, m_w_ffn_in, m_w_ffn_out, m_g_final, v_w_ada, v_b_ada, v_g_norm1, v_w_in, v_g_cq, v_w_uq, v_g_ckv, v_w_ukv, v_rel_bias, v_g_out_a, v_g_out_b, v_w_out, v_g_norm2, v_w_ffn_in, v_w_ffn_out, v_g_final):
    nb = x.shape[0]
    t = nb * S
    xt, tt = x.reshape(t, D), loss_target.reshape(t, D)
    me = 4 * lax.axis_index("x") + 2 * lax.axis_index("y") + lax.axis_index("c")

    early = [jnp.swapaxes(w_in, 1, 2)[0], jnp.swapaxes(w_uq, 1, 2)[0], w_ukv[0]]
    gathered = _gather_two_level([_silu_rows(c)] + [s.astype(BF16) for s in early], "gather_weights")
    cond_all = gathered[0].reshape(N_DEV * nb, D)
    w_in_t = gathered[1].reshape(P_IN, D)
    w_ukv_f = _cols_from_blocks(gathered[3])
    w_k, w_v = _split_w_ukv(w_ukv_f)

    ncol = N_MOD * D // N_DEV
    b_slab = lax.dynamic_slice(b_ada, (0, me * ncol), (1, ncol))
    slab = _mod_slab(cond_all, w_ada, b_slab)
    (mod_rows,) = _exchange([slab.reshape(N_DEV, nb, ncol)], [False], "scatter_mod")
    mod = jnp.transpose(mod_rows, (1, 0, 2)).reshape(nb, N_MOD, D)

    late = [s.astype(BF16) for s in (w_out[0], jnp.swapaxes(w_ffn_in, 1, 2)[0], w_ffn_out[0])]
    late_send, late_recv, late_src, late_land, late_token = _exchange_start(
        late, [_own_block_in_place(s, me) for s in late], [True] * 3, mod_rows, "gather_late_start")
    g_norm1_t = g_norm1 + late_token[:1, :1]

    def late_weights(after):
        w_out_g, w_ffn_in_g, w_ffn_out_g = _exchange_wait(late_send, late_recv, late_src, late_land, [True] * 3, after,
                                                          "gather_late_wait")
        return w_out_g.reshape(D, D), w_ffn_in_g.reshape(2 * D_FF, D), w_ffn_out_g.reshape(D_FF, D)

    flight = {}

    def start_grads(key, src, name):
        land = [_own_block_in_place(lax.dynamic_index_in_dim(s, me, 0, keepdims=False), me) for s in src]
        send, recv, src, land, token = _exchange_start(src, land, [False] * len(src), None, name)
        flight[key] = (send, recv, src, land)
        return token[:1, :1]

    def on_ffn_grads(gw_ffn_in, gw_ffn_out, gw_out):
        return start_grads("ffn", [gw_ffn_in.reshape(N_DEV, 2 * D_FF // N_DEV, D), gw_ffn_out.reshape(N_DEV, D_FF // N_DEV, D),
                                   gw_out.reshape(N_DEV, D // N_DEV, D)], "exchange_ffn_start")

    def on_last_grads(gw):
        return start_grads("rest", [_unpad_w_in(gw["w_in"]).reshape(N_DEV, P_IN // N_DEV, D),
                                    _unpad_w_uq(gw["w_uq"]),
                                    _cols_to_blocks(_join_w_ukv(gw["w_k"], gw["w_v"]))], "exchange_rest_start")

    grad_x, dmod, loss_cols, small, _ = _local_step(
        xt, mod, tt, g_norm1_t, _pad_w_in(w_in_t), g_cq, _pad_w_uq(gathered[2]), g_ckv, w_k, w_v, rel_bias, g_out_a, g_out_b,
        None, g_norm2, None, None, g_final.reshape(1, D), late_weights=late_weights, on_ffn_grads=on_ffn_grads,
        on_last_grads=on_last_grads)

    mine, dmod_blocks = _pack_small(dmod, [small[n] for n, _ in ROW_PARAMS], loss_cols)
    small_src = [dmod_blocks, mine, small["rel_bias"]]
    small_gather = [False, True, True]
    small_land = [_own_block_in_place(lax.dynamic_index_in_dim(dmod_blocks, me, 0, keepdims=False), me),
                  _own_block_in_place(mine, me), _own_block_in_place(small["rel_bias"], me)]
    small_send, small_recv, small_src, small_land, _ = _exchange_start(small_src, small_land, small_gather, None,
                                                                       "exchange_small_start")

    upd = {}

    def land_and_update(key, names, after, name):
        got = _exchange_wait(*flight[key], [False] * len(names), after, name)
        for n, p in zip(names, got):
            w, m, v = big[n]
            upd[n] = _adamw(p, w, m, v, "adamw_" + n)

    def flip(a):
        return jnp.swapaxes(a, 1, 2)

    big = dict(w_in=(flip(w_in), flip(m_w_in), flip(v_w_in)), w_uq=(flip(w_uq), flip(m_w_uq), flip(v_w_uq)),
               w_ukv=(w_ukv, m_w_ukv, v_w_ukv),
               w_out=(w_out, m_w_out, v_w_out), w_ffn_in=(flip(w_ffn_in), flip(m_w_ffn_in), flip(v_w_ffn_in)),
               w_ffn_out=(w_ffn_out, m_w_ffn_out, v_w_ffn_out))
    land_and_update("ffn", ["w_ffn_in", "w_ffn_out", "w_out"], grad_x, "exchange_ffn_wait")
    land_and_update("rest", ["w_in", "w_uq", "w_ukv"], upd["w_out"][0], "exchange_rest_wait")
    for n in ("w_in", "w_uq", "w_ffn_in"):
        upd[n] = tuple(flip(a) for a in upd[n])

    dmod_cols, pay, rel = _exchange_wait(small_send, small_recv, small_src, small_land, small_gather, upd["w_ukv"][0],
                                         "exchange_small_wait")
    g_ada = _ada_grad(cond_all, dmod_cols.reshape(N_DEV * nb, ncol))
    upd["w_ada"] = _adamw(g_ada[None], w_ada, m_w_ada, v_w_ada, "adamw_w_ada")
    row = lambda a: a.reshape(1, D)
    small_names = ["b_ada"] + [n for n, _ in ROW_PARAMS] + ["rel_bias"]
    small_w = [b_ada, g_norm1, g_cq, g_ckv, g_out_a, g_out_b, g_norm2, row(g_final), rel_bias]
    small_m = [m_b_ada, m_g_norm1, m_g_cq, m_g_ckv, m_g_out_a, m_g_out_b, m_g_norm2, row(m_g_final), m_rel_bias]
    small_v = [v_b_ada, v_g_norm1, v_g_cq, v_g_ckv, v_g_out_a, v_g_out_b, v_g_norm2, row(v_g_final), v_rel_bias]
    small_upd, loss8 = _small_update(pay, rel, small_w, small_m, small_v)
    upd.update(zip(small_names, small_upd))

    order = ["w_ada", "b_ada", "g_norm1", "w_in", "g_cq", "w_uq", "g_ckv", "w_ukv", "rel_bias", "g_out_a", "g_out_b",
             "w_out", "g_norm2", "w_ffn_in", "w_ffn_out", "g_final"]
    like = dict(g_final=g_final)
    outs = [loss8[0, 0], grad_x.reshape(x.shape)]
    for part in range(4):
        for n in order:
            val = upd[n][part]
            outs.append(val.reshape(like[n].shape) if n in like else val)
    return tuple(outs)
```

```python
import numpy as np
import jax
import jax.numpy as jnp
from jax import lax
from jax.experimental import pallas as pl
from jax.experimental.pallas import tpu as pltpu

F32, BF16 = jnp.float32, jnp.bfloat16

N_DEV = 8
D = 1024
S = 2048
H = 8
E_A = 64
D_A = H * E_A
Q_LORA, KV_LORA = 384, 256
NOPE, ROPE, VDIM = 64, 32, 64
HP = 128
P_IN = 3 * D_A + Q_LORA + KV_LORA + ROPE
P_PAD = 3 * D_A + Q_LORA + KV_LORA + HP
TAIL0 = 3 * D_A
TAIL = P_PAD - TAIL0
D_FF = 2816
N_MOD = 6
EPS = 1e-6
NEG = -1e30
BLK = 128
DILATIONS = (1, 4, 16)
N_BUCKETS, MAX_DISTANCE = 32, 2048
ROPE_THETA = 10000.0
SCALE_A = E_A ** -0.5
SCALE_B = (NOPE + ROPE) ** -0.5
B1, B2, LR, ADAM_EPS, WD, STEP = 0.9, 0.999, 0.001, 1e-8, 0.01, 10
VMEM_LIMIT = 56 * 1024 * 1024


def _cp(*sem):
    return pltpu.CompilerParams(dimension_semantics=sem, vmem_limit_bytes=VMEM_LIMIT)


def _pick(n, prefs):
    for p in prefs:
        if n % p == 0:
            return p
    raise ValueError(f"no tile of {prefs} divides {n}")


OPERAND_BYTES = 6 * 1024 * 1024


def _pick_rows(m, k):
    return _pick(m, [p for p in (1024, 512, 256, 128, 16) if p * k * 2 <= OPERAND_BYTES])


MATMUL_BYTES = 40 * 1024 * 1024


def _stream_rows(m, fixed, per_row):
    return _pick(m, [p for p in (4096, 2048, 1024, 512, 256, 128, 16) if fixed + p * per_row <= MATMUL_BYTES])


def _dot(a, b, dims):
    return lax.dot_general(a, b, (dims, ((), ())), preferred_element_type=F32)


def _mm_nn(a, b, out_dtype, name, after=None):
    m, k = a.shape
    n = b.shape[1]
    tn = _pick(n, (512, 256, 384, 128))
    tm = _stream_rows(m, 4 * k * tn, 4 * k + (2 * jnp.dtype(out_dtype).itemsize + 4) * tn)

    def body(a_ref, b_ref, *rest):
        o_ref = rest[-1]
        o_ref[...] = _dot(a_ref[...], b_ref[...], ((1,), (0,))).astype(o_ref.dtype)

    extra = [] if after is None else [after]
    return pl.pallas_call(
        body, name=name, grid=(m // tm, n // tn),
        in_specs=[pl.BlockSpec((tm, k), lambda i, j: (i, 0)), pl.BlockSpec((k, tn), lambda i, j: (0, j))] + [ANY] * len(extra),
        out_specs=pl.BlockSpec((tm, tn), lambda i, j: (i, j)),
        out_shape=jax.ShapeDtypeStruct((m, n), out_dtype),
        compiler_params=_cp("parallel", "parallel"),
    )(a, b, *extra)


def _mm_nt(a, b, out_dtype, name, after=None):
    m, k = a.shape
    n = b.shape[0]
    tn = _pick(n, (512, 256, 384, 128))
    tm = _stream_rows(m, 4 * k * tn, 4 * k + (2 * jnp.dtype(out_dtype).itemsize + 4) * tn)

    def body(a_ref, b_ref, *rest):
        o_ref = rest[-1]
        o_ref[...] = _dot(a_ref[...], b_ref[...], ((1,), (1,))).astype(o_ref.dtype)

    extra = [] if after is None else [after]
    return pl.pallas_call(
        body, name=name, grid=(m // tm, n // tn),
        in_specs=[pl.BlockSpec((tm, k), lambda i, j: (i, 0)), pl.BlockSpec((tn, k), lambda i, j: (j, 0))] + [ANY] * len(extra),
        out_specs=pl.BlockSpec((tm, tn), lambda i, j: (i, j)),
        out_shape=jax.ShapeDtypeStruct((m, n), out_dtype),
        compiler_params=_cp("parallel", "parallel"),
    )(a, b, *extra)


def _mm_tn(a, bs, name):
    t, m = a.shape
    n = bs[0].shape[1]
    nb_ = len(bs)
    tc = _pick(t, (512, 16))
    tn = _pick(n, (512, 384, 256, 128))
    tm = _pick(m, [p for p in (1024, 512, 384, 256, 128) if (3 * p + 2 * nb_ * tn) * t * 2 <= VMEM_LIMIT - 2 * OPERAND_BYTES])
    if tm <= 256 and nb_ * n * t * 2 <= 2 * OPERAND_BYTES:
        tn = n

    def body(*refs):
        a_ref, b_refs, o_refs, at_ref = refs[0], refs[1:1 + nb_], refs[1 + nb_:1 + 2 * nb_], refs[-1]

        @pl.when(pl.program_id(1) == 0)
        def _():
            def chunk(c, _):
                rows = pl.ds(pl.multiple_of(c * tc, tc), tc)
                at_ref[:, rows] = a_ref[rows, :].T
                return 0

            lax.fori_loop(0, t // tc, chunk, 0)

        for b_ref, o_ref in zip(b_refs, o_refs):
            o_ref[...] = _dot(at_ref[...], b_ref[...], ((1,), (0,))).astype(BF16)

    res = pl.pallas_call(
        body, name=name, grid=(m // tm, n // tn),
        in_specs=[pl.BlockSpec((t, tm), lambda i, j: (0, i))] + [pl.BlockSpec((t, tn), lambda i, j: (0, j))] * nb_,
        out_specs=[pl.BlockSpec((tm, tn), lambda i, j: (i, j))] * nb_,
        out_shape=[jax.ShapeDtypeStruct((m, n), BF16)] * nb_,
        scratch_shapes=[pltpu.VMEM((tm, t), BF16)],
        compiler_params=_cp("parallel", "arbitrary"),
    )(a, *bs)
    return res[0] if nb_ == 1 else res


def _mm_tn_rows(a_list, b, name):
    t, m = a_list[0].shape
    n = b.shape[1]
    na = len(a_list)
    tc, tm = _pick(t, (512, 16)), _pick(m, (256, 128))
    nblk = m // tm

    def body(*refs):
        a_refs, b_ref, o_ref, bt_ref, r_ref = refs[:na], refs[na], refs[na + 1], refs[na + 2], refs[na + 3]
        i = pl.program_id(0)

        @pl.when(i == 0)
        def _():
            def chunk(c, _):
                rows = pl.ds(pl.multiple_of(c * tc, tc), tc)
                bt_ref[:, rows] = b_ref[rows, :].T
                return 0

            lax.fori_loop(0, t // tc, chunk, 0)

        for s, a_ref in enumerate(a_refs):
            @pl.when((i >= s * nblk) & (i < (s + 1) * nblk))
            def _(a_ref=a_ref):
                r_ref[...] = _dot(bt_ref[...], a_ref[...], ((1,), (0,)))
                o_ref[...] = r_ref[...].T.astype(BF16)

    return pl.pallas_call(
        body, name=name, grid=(na * nblk,),
        in_specs=[pl.BlockSpec((t, tm), lambda i, s=s: (0, jnp.clip(i - s * nblk, 0, nblk - 1))) for s in range(na)]
        + [pl.BlockSpec((t, n), lambda i: (0, 0))],
        out_specs=pl.BlockSpec((tm, n), lambda i: (i, 0)),
        out_shape=jax.ShapeDtypeStruct((na * m, n), BF16),
        scratch_shapes=[pltpu.VMEM((n, t), BF16), pltpu.VMEM((n, tm), F32)],
        compiler_params=_cp("arbitrary"),
    )(*a_list, b)


EPI = 256


def _silu_parts(g):
    sg = 0.5 * jnp.tanh(0.5 * g) + 0.5
    return sg, g * sg


def _ffn_in(h2, wt):
    t, k = h2.shape
    tn = _pick(D_FF, (256, 128))
    tm = _stream_rows(t, 8 * k * tn, 4 * k + (3 * 2 * 2 + 2 * 4) * tn)
    nj = D_FF // tn

    def body(h_ref, wg_ref, wu_ref, g_ref, u_ref, a_ref):
        hv = h_ref[...]
        g_all = _dot(hv, wg_ref[...], ((1,), (1,)))
        u_all = _dot(hv, wu_ref[...], ((1,), (1,)))
        for r in range(tm // EPI):
            rows = slice(r * EPI, (r + 1) * EPI)
            g, u = g_all[rows], u_all[rows]
            g_ref[rows, :] = g.astype(BF16)
            u_ref[rows, :] = u.astype(BF16)
            a_ref[rows, :] = (_silu_parts(g)[1] * u).astype(BF16)

    blk = pl.BlockSpec((tm, tn), lambda i, j: (i, j))
    return pl.pallas_call(
        body, name="ffn_in", grid=(t // tm, nj),
        in_specs=[pl.BlockSpec((tm, k), lambda i, j: (i, 0)), pl.BlockSpec((tn, k), lambda i, j: (j, 0)),
                  pl.BlockSpec((tn, k), lambda i, j: (j + nj, 0))],
        out_specs=[blk] * 3, out_shape=[jax.ShapeDtypeStruct((t, D_FF), BF16)] * 3,
        compiler_params=_cp("parallel", "parallel"),
    )(h2, wt, wt)


def _d_act(df, w, g, u):
    t, k = df.shape
    tn = _pick(D_FF, (256, 128))
    tm = _stream_rows(t, 4 * k * tn, 4 * k + (4 * 2 * 2 + 4) * tn)

    def body(df_ref, w_ref, g_ref, u_ref, dg_ref, du_ref):
        da_all = _dot(df_ref[...], w_ref[...], ((1,), (1,)))
        for r in range(tm // EPI):
            rows = slice(r * EPI, (r + 1) * EPI)
            da = da_all[rows]
            gv = g_ref[rows, :].astype(F32)
            sg, silu = _silu_parts(gv)
            dg_ref[rows, :] = ((da * u_ref[rows, :].astype(F32)) * (sg + silu * (1.0 - sg))).astype(BF16)
            du_ref[rows, :] = (da * silu).astype(BF16)

    blk = pl.BlockSpec((tm, tn), lambda i, j: (i, j))
    return pl.pallas_call(
        body, name="d_act", grid=(t // tm, D_FF // tn),
        in_specs=[pl.BlockSpec((tm, k), lambda i, j: (i, 0)), pl.BlockSpec((tn, k), lambda i, j: (j, 0)), blk, blk],
        out_specs=[blk] * 2, out_shape=[jax.ShapeDtypeStruct((t, D_FF), BF16)] * 2,
        compiler_params=_cp("parallel", "parallel"),
    )(df, w, g, u)


def _d_h2(dg, du, wt):
    t = dg.shape[0]
    n = wt.shape[1]
    tm, tn = _pick_rows(t, D_FF), _pick(n, (512, 256, 128))

    def body(dg_ref, du_ref, wg_ref, wu_ref, o_ref):
        o_ref[...] = (_dot(dg_ref[...], wg_ref[...], ((1,), (0,)))
                      + _dot(du_ref[...], wu_ref[...], ((1,), (0,)))).astype(BF16)

    return pl.pallas_call(
        body, name="d_h2", grid=(t // tm, n // tn),
        in_specs=[pl.BlockSpec((tm, D_FF), lambda i, j: (i, 0)), pl.BlockSpec((tm, D_FF), lambda i, j: (i, 0)),
                  pl.BlockSpec((D_FF, tn), lambda i, j: (0, j)), pl.BlockSpec((D_FF, tn), lambda i, j: (1, j))],
        out_specs=pl.BlockSpec((tm, tn), lambda i, j: (i, j)),
        out_shape=jax.ShapeDtypeStruct((t, n), BF16),
        compiler_params=_cp("parallel", "parallel"),
    )(dg, du, wt, wt)


TM = 1024


def _row(w):
    return pl.BlockSpec((TM, w), lambda i: (i, 0))


def _row_at(w, col):
    return pl.BlockSpec((TM, w), lambda i: (i, col))


def _vec(w):
    return pl.BlockSpec((1, w), lambda i: (0, 0))


def _per_ex(w):
    return pl.BlockSpec((1, 1, w), lambda i: (i // (S // TM), 0, 0))


def _pos(w):
    return pl.BlockSpec((TM, w), lambda i: (i % (S // TM), 0))


def _full(shape):
    return pl.BlockSpec(shape, lambda i: (0,) * len(shape))


def _rms(x):
    return lax.rsqrt(jnp.mean(x * x, axis=-1, keepdims=True) + EPS)


def _rms_bwd(n, r, dn):
    return r * (dn - n * jnp.mean(dn * n, axis=-1, keepdims=True))


def _colsum(v):
    return jnp.sum(v, axis=0, keepdims=True)


def _acc_first(i, ref, val, every=None):
    first = (i == 0) if every is None else (i % every == 0)

    @pl.when(first)
    def _():
        ref[...] = jnp.zeros_like(ref)

    ref[...] += val.reshape(ref.shape)


def _pre1(x, g, sc, sh):
    t = x.shape[0]

    def body(x_ref, g_ref, sc_ref, sh_ref, h_ref):
        xv = x_ref[...]
        n = xv * _rms(xv)
        h_ref[...] = ((n * g_ref[...]) * (1.0 + sc_ref[0]) + sh_ref[0]).astype(BF16)

    return pl.pallas_call(
        body, name="pre1", grid=(t // TM,),
        in_specs=[_row(D), _vec(D), _per_ex(D), _per_ex(D)],
        out_specs=_row(D), out_shape=jax.ShapeDtypeStruct((t, D), BF16),
        compiler_params=_cp("parallel"),
    )(x, g, sc, sh)


def _rope_fwd(v, c, sm, sp):
    return v * c + pltpu.roll(v, HP - ROPE // 2, 1) * sm + pltpu.roll(v, ROPE // 2, 1) * sp


def _rope_bwd(dv, c, sm, sp):
    return dv * c + pltpu.roll(dv * sm, ROPE // 2, 1) + pltpu.roll(dv * sp, HP - ROPE // 2, 1)


def _mla_pre(proj, g_cq, g_ckv, w_uq, w_k, w_v, rc, rsm, rsp):
    t = proj.shape[0]

    def body(tail_ref, gq_ref, gkv_ref, wuq_ref, wk_ref, wv_ref, c_ref, sm_ref, sp_ref,
             q_ref, k_ref, v_ref, cqn_ref, ckvn_ref):
        tail = tail_ref[...]
        cq, ckv, kr = tail[:, :Q_LORA], tail[:, Q_LORA:Q_LORA + KV_LORA], tail[:, Q_LORA + KV_LORA:]
        cqn = (cq * _rms(cq) * gq_ref[...]).astype(BF16)
        ckvn = (ckv * _rms(ckv) * gkv_ref[...]).astype(BF16)
        cqn_ref[...] = cqn
        ckvn_ref[...] = ckvn
        c, sm, sp = c_ref[...], sm_ref[...], sp_ref[...]
        q = _dot(cqn, wuq_ref[...], ((1,), (1,)))
        kn = _dot(ckvn, wk_ref[...], ((1,), (0,)))
        v_ref[...] = _dot(ckvn, wv_ref[...], ((1,), (0,))).astype(BF16)
        krr = _rope_fwd(kr, c, sm, sp)
        for h in range(H):
            sl = slice(h * HP, (h + 1) * HP)
            q_ref[:, sl] = _rope_fwd(q[:, sl], c, sm, sp).astype(BF16)
            k_ref[:, sl] = (kn[:, sl] + krr).astype(BF16)

    wide = H * HP
    return pl.pallas_call(
        body, name="mla_pre", grid=(t // TM,),
        in_specs=[_row_at(TAIL, TAIL0 // TAIL), _vec(Q_LORA), _vec(KV_LORA), _full((wide, Q_LORA)),
                  _full((KV_LORA, wide)), _full((KV_LORA, wide)), _pos(HP), _pos(HP), _pos(HP)],
        out_specs=[_row(wide), _row(wide), _row(wide), _row(Q_LORA), _row(KV_LORA)],
        out_shape=[jax.ShapeDtypeStruct((t, wide), BF16)] * 3
        + [jax.ShapeDtypeStruct((t, Q_LORA), BF16), jax.ShapeDtypeStruct((t, KV_LORA), BF16)],
        compiler_params=_cp("parallel"),
    )(proj, g_cq, g_ckv, w_uq, w_k, w_v, rc, rsm, rsp)


def _mla_pre_bwd(proj, dq_, dk_, dv_, dqkv_a, g_cq, g_ckv, w_uq, w_k, w_v, rc, rsm, rsp):
    t = proj.shape[0]
    wide = H * HP

    def body(tail_ref, dq_ref, dk_ref, dv_ref, dqa_ref, dka_ref, dva_ref, gq_ref, gkv_ref, wuq_ref, wk_ref, wv_ref,
             c_ref, sm_ref, sp_ref, dqo_ref, dproj_ref, dgq_ref, dgkv_ref):
        i = pl.program_id(0)
        for n, src in enumerate((dqa_ref, dka_ref, dva_ref)):
            dproj_ref[:, n * D_A:(n + 1) * D_A] = src[...]
        dtail_ref = dproj_ref.at[:, TAIL0:]
        tail = tail_ref[...]
        cq, ckv = tail[:, :Q_LORA], tail[:, Q_LORA:Q_LORA + KV_LORA]
        c, sm, sp = c_ref[...], sm_ref[...], sp_ref[...]
        dkr = jnp.zeros((TM, HP), F32)
        for h in range(H):
            sl = slice(h * HP, (h + 1) * HP)
            dqo_ref[:, sl] = _rope_bwd(dq_ref[:, sl].astype(F32), c, sm, sp).astype(BF16)
            dkr = dkr + dk_ref[:, sl].astype(F32)
        lane = lax.broadcasted_iota(jnp.int32, (TM, HP), 1)
        dkr = jnp.where((lane >= NOPE) & (lane < NOPE + ROPE), _rope_bwd(dkr, c, sm, sp), 0.0)
        dkb = dk_ref[...]
        dvb = dv_ref[...]
        dcqn = _dot(dqo_ref[...], wuq_ref[...], ((1,), (0,)))
        dckvn = _dot(dkb, wk_ref[...], ((1,), (1,))) + _dot(dvb, wv_ref[...], ((1,), (1,)))
        rq, rkv = _rms(cq), _rms(ckv)
        nq, nkv = cq * rq, ckv * rkv
        _acc_first(i, dgq_ref, _colsum(dcqn * nq))
        _acc_first(i, dgkv_ref, _colsum(dckvn * nkv))
        dtail_ref[:, :Q_LORA] = _rms_bwd(nq, rq, dcqn * gq_ref[...]).astype(BF16)
        dtail_ref[:, Q_LORA:Q_LORA + KV_LORA] = _rms_bwd(nkv, rkv, dckvn * gkv_ref[...]).astype(BF16)
        dtail_ref[:, Q_LORA + KV_LORA:] = dkr.astype(BF16)

    return pl.pallas_call(
        body, name="mla_pre_bwd", grid=(t // TM,),
        in_specs=[_row_at(TAIL, TAIL0 // TAIL), _row(wide), _row(wide), _row(wide), _row(D_A), _row(D_A), _row(D_A),
                  _vec(Q_LORA), _vec(KV_LORA), _full((wide, Q_LORA)), _full((KV_LORA, wide)), _full((KV_LORA, wide)),
                  _pos(HP), _pos(HP), _pos(HP)],
        out_specs=[_row(wide), _row(P_PAD), _vec(Q_LORA), _vec(KV_LORA)],
        out_shape=[jax.ShapeDtypeStruct((t, wide), BF16), jax.ShapeDtypeStruct((t, P_PAD), BF16),
                   jax.ShapeDtypeStruct((1, Q_LORA), F32), jax.ShapeDtypeStruct((1, KV_LORA), F32)],
        compiler_params=_cp("arbitrary"),
    )(proj, dq_, dk_, dv_, *dqkv_a, g_cq, g_ckv, w_uq, w_k, w_v, rc, rsm, rsp)


def _post_attn(out_a, out_b, g_a, g_b):
    t = out_a.shape[0]

    def body(a_ref, b_ref, ga_ref, gb_ref, y_ref):
        a, b = a_ref[...], b_ref[...]
        y_ref[:, :D_A] = (a * _rms(a) * ga_ref[...]).astype(BF16)
        y_ref[:, D_A:] = (b * _rms(b) * gb_ref[...]).astype(BF16)

    return pl.pallas_call(
        body, name="post_attn", grid=(t // TM,),
        in_specs=[_row(D_A), _row(D_A), _vec(D_A), _vec(D_A)],
        out_specs=_row(D), out_shape=jax.ShapeDtypeStruct((t, D), BF16),
        compiler_params=_cp("parallel"),
    )(out_a, out_b, g_a, g_b)


def _post_attn_bwd(dy, out_a, out_b, g_a, g_b):
    t = dy.shape[0]

    def body(dy_ref, a_ref, b_ref, ga_ref, gb_ref, da_ref, db_ref, dga_ref, dgb_ref):
        i = pl.program_id(0)
        dy_ = dy_ref[...].astype(F32)
        for src, g_ref, dst, dg_ref, sl in ((a_ref, ga_ref, da_ref, dga_ref, slice(0, D_A)),
                                            (b_ref, gb_ref, db_ref, dgb_ref, slice(D_A, D))):
            v = src[...]
            r = _rms(v)
            n = v * r
            dyv = dy_[:, sl]
            _acc_first(i, dg_ref, _colsum(dyv * n))
            dst[...] = _rms_bwd(n, r, dyv * g_ref[...])

    return pl.pallas_call(
        body, name="post_attn_bwd", grid=(t // TM,),
        in_specs=[_row(D), _row(D_A), _row(D_A), _vec(D_A), _vec(D_A)],
        out_specs=[_row(D_A), _row(D_A), _vec(D_A), _vec(D_A)],
        out_shape=[jax.ShapeDtypeStruct((t, D_A), F32)] * 2 + [jax.ShapeDtypeStruct((1, D_A), F32)] * 2,
        compiler_params=_cp("arbitrary"),
    )(dy, out_a, out_b, g_a, g_b)


def _resid_norm2(x, mix, g1, g, sc, sh):
    t = x.shape[0]

    def body(x_ref, mix_ref, g1_ref, g_ref, sc_ref, sh_ref, x2_ref, h_ref):
        x2 = x_ref[...] + g1_ref[0] * mix_ref[...]
        x2_ref[...] = x2
        n = x2 * _rms(x2)
        h_ref[...] = ((n * g_ref[...]) * (1.0 + sc_ref[0]) + sh_ref[0]).astype(BF16)

    return pl.pallas_call(
        body, name="resid_norm2", grid=(t // TM,),
        in_specs=[_row(D), _row(D), _per_ex(D), _vec(D), _per_ex(D), _per_ex(D)],
        out_specs=[_row(D), _row(D)],
        out_shape=[jax.ShapeDtypeStruct((t, D), F32), jax.ShapeDtypeStruct((t, D), BF16)],
        compiler_params=_cp("parallel"),
    )(x, mix, g1, g, sc, sh)


def _sigmoid(v):
    return 1.0 / (1.0 + jnp.exp(-v))


def _final(x2, f, g2, g_fin, target):
    t = x2.shape[0]
    nb = t // S
    tpb = S // TM

    def body(x2_ref, f_ref, g2_ref, g_ref, t_ref, dx3_ref, df_ref, loss_ref, dgf_ref, dg2_ref):
        i = pl.program_id(0)
        fv = f_ref[...].astype(F32)
        x3 = x2_ref[...] + g2_ref[0] * fv
        r = _rms(x3)
        n = x3 * r
        err = n * g_ref[...] - t_ref[...]
        _acc_first(i, loss_ref, _colsum(err * err))
        dy = err * (1.0 / D)
        _acc_first(i, dgf_ref, _colsum(dy * n))
        dx3 = _rms_bwd(n, r, dy * g_ref[...])
        dx3_ref[...] = dx3.astype(BF16)
        _acc_first(i, dg2_ref, _colsum(dx3 * fv), every=tpb)
        df_ref[...] = (dx3 * g2_ref[0]).astype(BF16)

    return pl.pallas_call(
        body, name="final", grid=(t // TM,),
        in_specs=[_row(D), _row(D), _per_ex(D), _vec(D), _row(D)],
        out_specs=[_row(D), _row(D), _vec(D), _vec(D), _per_ex(D)],
        out_shape=[jax.ShapeDtypeStruct((t, D), BF16), jax.ShapeDtypeStruct((t, D), BF16),
                   jax.ShapeDtypeStruct((1, D), F32), jax.ShapeDtypeStruct((1, D), F32),
                   jax.ShapeDtypeStruct((nb, 1, D), F32)],
        compiler_params=_cp("arbitrary"),
    )(x2, f, g2, g_fin, target)


def _norm_bwd(xin, dh, dres, g, sc, gate=None):
    t = xin.shape[0]
    nb = t // S
    tpb = S // TM
    gated = gate is not None

    def body(*refs):
        if gated:
            x_ref, dh_ref, dres_ref, g_ref, sc_ref, mix_ref, g1_ref, dx_ref, dsh_ref, dsc_ref, dg_ref, dg1_ref, dmix_ref = refs
        else:
            x_ref, dh_ref, dres_ref, g_ref, sc_ref, dx_ref, dsh_ref, dsc_ref, dg_ref = refs
        i = pl.program_id(0)
        xv, dhv = x_ref[...], dh_ref[...].astype(F32)
        r = _rms(xv)
        n = xv * r
        gv = g_ref[...]
        _acc_first(i, dsh_ref, _colsum(dhv), every=tpb)
        _acc_first(i, dsc_ref, _colsum(dhv * (n * gv)), every=tpb)
        dng = dhv * (1.0 + sc_ref[0])
        _acc_first(i, dg_ref, _colsum(dng * n))
        dx = dres_ref[...].astype(F32) + _rms_bwd(n, r, dng * gv)
        dx_ref[...] = dx.astype(dx_ref.dtype)
        if gated:
            _acc_first(i, dg1_ref, _colsum(dx * mix_ref[...].astype(F32)), every=tpb)
            dmix_ref[...] = (dx * g1_ref[0]).astype(BF16)

    in_specs = [_row(D), _row(D), _row(D), _vec(D), _per_ex(D)]
    out_specs = [_row(D), _per_ex(D), _per_ex(D), _vec(D)]
    out_shape = [jax.ShapeDtypeStruct((t, D), BF16 if gated else F32), jax.ShapeDtypeStruct((nb, 1, D), F32),
                 jax.ShapeDtypeStruct((nb, 1, D), F32), jax.ShapeDtypeStruct((1, D), F32)]
    args = [xin, dh, dres, g, sc]
    if gated:
        in_specs += [_row(D), _per_ex(D)]
        out_specs += [_per_ex(D), _row(D)]
        out_shape += [jax.ShapeDtypeStruct((nb, 1, D), F32), jax.ShapeDtypeStruct((t, D), BF16)]
        args += list(gate)
    return pl.pallas_call(
        body, name="norm2_bwd" if gated else "norm1_bwd", grid=(t // TM,),
        in_specs=in_specs, out_specs=out_specs, out_shape=out_shape,
        compiler_params=_cp("arbitrary"),
    )(*args)


TQ = 256
TB = 512
FWD_HEADS = 2


def _mla_fwd(q, k, v):
    t = q.shape[0]
    nb = t // S

    def body(q_ref, k_ref, v_ref, o_ref, lse_ref):
        causal = lax.broadcasted_iota(jnp.int32, (TB, TB), 0) >= lax.broadcasted_iota(jnp.int32, (TB, TB), 1)
        heads = [slice(h * HP, (h + 1) * HP) for h in range(FWD_HEADS)]
        for i in range(S // TB):
            ri, past = slice(i * TB, (i + 1) * TB), slice(0, i * TB)
            qhs = [q_ref[ri, sl] for sl in heads]
            sd = [jnp.where(causal, _dot(qh, k_ref[ri, sl], ((1,), (1,))) * SCALE_B, NEG) for qh, sl in zip(qhs, heads)]
            ms = [jnp.max(s, axis=-1, keepdims=True) for s in sd]
            if i:
                so = [_dot(qh, k_ref[past, sl], ((1,), (1,))) * SCALE_B for qh, sl in zip(qhs, heads)]
                ms = [jnp.maximum(m, jnp.max(s, axis=-1, keepdims=True)) for m, s in zip(ms, so)]
            pd = [jnp.exp(s - m) for s, m in zip(sd, ms)]
            ls = [jnp.sum(p, axis=-1, keepdims=True) for p in pd]
            acc = [_dot(p.astype(BF16), v_ref[ri, sl], ((1,), (0,))) for p, sl in zip(pd, heads)]
            if i:
                po = [jnp.exp(s - m) for s, m in zip(so, ms)]
                ls = [l + jnp.sum(p, axis=-1, keepdims=True) for l, p in zip(ls, po)]
                acc = [a + _dot(p.astype(BF16), v_ref[past, sl], ((1,), (0,))) for a, p, sl in zip(acc, po, heads)]
            for pr in range(FWD_HEADS // 2):
                o_ref[ri, pr * HP:(pr + 1) * HP] = acc[2 * pr] / ls[2 * pr] + acc[2 * pr + 1] / ls[2 * pr + 1]
            for sl, m, l in zip(heads, ms, ls):
                lse_ref[ri, sl] = jnp.broadcast_to(m + jnp.log(l), (TB, HP))

    wide2 = pl.BlockSpec((S, FWD_HEADS * HP), lambda b, p: (b, p))
    return pl.pallas_call(
        body, name="mla_fwd", grid=(nb, H // FWD_HEADS),
        in_specs=[wide2, wide2, wide2],
        out_specs=[pl.BlockSpec((S, FWD_HEADS // 2 * HP), lambda b, p: (b, p)), wide2],
        out_shape=[jax.ShapeDtypeStruct((t, H * VDIM), F32), jax.ShapeDtypeStruct((t, H * HP), F32)],
        compiler_params=_cp("parallel", "parallel"),
    )(q, k, v)


def _mla_bwd(q, k, v, o, do, lse):
    t = q.shape[0]
    nb = t // S

    def body(q_ref, k_ref, v_ref, o_ref, do_ref, lse_ref, dq_out, dk_out, dv_out, dq_ref, dk_ref, dv_ref):
        lane = lax.broadcasted_iota(jnp.int32, (TB, HP), 1)
        causal = lax.broadcasted_iota(jnp.int32, (TB, TB), 0) >= lax.broadcasted_iota(jnp.int32, (TB, TB), 1)
        heads = [slice(h * HP, (h + 1) * HP) for h in range(2)]
        nblk = S // TB
        for i in reversed(range(nblk)):
            ri, past = slice(i * TB, (i + 1) * TB), slice(0, i * TB)
            dov = do_ref[ri, :]
            prod = dov * o_ref[ri, :]
            dob = dov.astype(BF16)
            deltas = [jnp.sum(jnp.where((lane < VDIM) if h == 0 else (lane >= VDIM), prod, 0.0), axis=-1, keepdims=True)
                      for h in range(2)]
            qhs = [q_ref[ri, sl] for sl in heads]
            lses = [lse_ref[ri, sl][:, :1] for sl in heads]
            for rows, diagonal in ((ri, True), (past, False)):
                if rows.stop == rows.start:
                    continue
                ps = [jnp.exp(_dot(qh, k_ref[rows, sl], ((1,), (1,))) * SCALE_B - lse) for qh, sl, lse in zip(qhs, heads, lses)]
                if diagonal:
                    ps = [jnp.where(causal, p, 0.0) for p in ps]
                dps = [_dot(dob, v_ref[rows, sl], ((1,), (1,))) for sl in heads]
                dss = [(p * (dp - delta) * SCALE_B).astype(BF16) for p, dp, delta in zip(ps, dps, deltas)]
                for sl, qh, p, ds in zip(heads, qhs, ps, dss):
                    dq = _dot(ds, k_ref[rows, sl], ((1,), (0,)))
                    dk = _dot(ds, qh, ((0,), (0,)))
                    dv = _dot(p.astype(BF16), dob, ((0,), (0,)))
                    if diagonal:
                        dq_ref[ri, sl] = dq
                    else:
                        dq_ref[ri, sl] += dq
                    if i == nblk - 1:
                        dk_ref[rows, sl] = dk
                        dv_ref[rows, sl] = dv
                    else:
                        dk_ref[rows, sl] += dk
                        dv_ref[rows, sl] += dv
        dq_out[...] = dq_ref[...].astype(BF16)
        dk_out[...] = dk_ref[...].astype(BF16)
        dv_out[...] = dv_ref[...].astype(BF16)

    wide2 = pl.BlockSpec((S, 2 * HP), lambda b, p: (b, p))
    pair = pl.BlockSpec((S, HP), lambda b, p: (b, p))
    return pl.pallas_call(
        body, name="mla_bwd", grid=(nb, H // 2),
        in_specs=[wide2, wide2, wide2, pair, pair, wide2],
        out_specs=[wide2, wide2, wide2],
        out_shape=[jax.ShapeDtypeStruct((t, H * HP), BF16)] * 3,
        scratch_shapes=[pltpu.VMEM((S, 2 * HP), F32)] * 3,
        compiler_params=_cp("parallel", "parallel"),
    )(q, k, v, o, do, lse)


def _t5_bucket(dist):
    max_exact = N_BUCKETS // 2
    d = np.maximum(dist, 1).astype(np.float64)
    large = max_exact + (np.log(d / max_exact) / np.log(MAX_DISTANCE / max_exact) * (N_BUCKETS - max_exact)).astype(np.int64)
    large = np.minimum(large, N_BUCKETS - 1)
    return np.where(dist < max_exact, dist, large).astype(np.int32)


def _band_geometry():
    a = np.arange(BLK)[:, None]
    bk = np.arange(2 * BLK)[None, :]
    steps = BLK + a - bk
    valid = (steps >= 0) & (steps <= BLK)
    buckets = np.stack([_t5_bucket(np.clip(steps, 0, BLK) * d) for d in DILATIONS])
    return buckets, valid


def _band_bias(rel_bias):
    buckets, valid = _band_geometry()
    onehot = (jnp.asarray(buckets)[..., None] == jnp.arange(N_BUCKETS)).astype(F32)
    bias = jnp.einsum("rqkn,nh->rhqk", onehot, rel_bias, precision=lax.Precision.HIGHEST)
    bias = jnp.where(jnp.asarray(valid)[None, None], bias, NEG)
    return bias.reshape(3, H // 2, 2 * BLK, 2 * BLK)


def _dil_items():
    items = []
    for r, d in enumerate(DILATIONS):
        for res in range(d):
            for blk in range(S // d // BLK):
                items.append((r, d, blk * BLK * d + res, blk > 0))
    return items


GROUP = 4


def _strided(start, d):
    return pl.ds(start, BLK) if d == 1 else pl.ds(start, BLK, stride=d)


def _stack_heads(tile, own):
    return jnp.where(own, jnp.concatenate([tile, tile], axis=0), 0.0).astype(BF16)


def _own_lanes():
    row = lax.broadcasted_iota(jnp.int32, (2 * BLK, HP), 0)
    lane = lax.broadcasted_iota(jnp.int32, (2 * BLK, HP), 1)
    return (lane < E_A) == (row < BLK)


def _dil_fwd(proj, biasm):
    t = proj.shape[0]
    nb = t // S

    def body(q_ref, k_ref, v_ref, b_ref, o_ref, lse_ref, ob_ref, lb_ref):
        lane = lax.broadcasted_iota(jnp.int32, (BLK, HP), 1)
        own = _own_lanes()
        items = _dil_items()
        for g in range(0, len(items), GROUP):
            grp = items[g:g + GROUP]
            ss, vts = [], []
            for r, d, start, has_prev in grp:
                cur = _strided(start, d)
                rows = [_strided(start - BLK * d, d), cur] if has_prev else [cur]
                q2 = _stack_heads(q_ref[cur, :] * SCALE_A, own)
                kt = jnp.concatenate([k_ref[x, :] for x in rows], axis=0).astype(BF16)
                vts.append(jnp.concatenate([v_ref[x, :] for x in rows], axis=0).astype(BF16))
                bias = b_ref[r, 0] if has_prev else b_ref[r, 0, :, BLK:]
                ss.append(_dot(q2, kt, ((1,), (1,))) + bias)
            ms = [jnp.max(s, axis=-1, keepdims=True) for s in ss]
            ps = [jnp.exp(s - m) for s, m in zip(ss, ms)]
            ls = [jnp.sum(p, axis=-1, keepdims=True) for p in ps]
            for (r, d, start, _), p, vt, m, l in zip(grp, ps, vts, ms, ls):
                cur = _strided(start, d)
                o2 = _dot(p.astype(BF16), vt, ((1,), (0,))) / l
                lse2 = m + jnp.log(l)
                ob_ref[r, cur, :] = jnp.where(lane < E_A, o2[:BLK], o2[BLK:])
                lb_ref[r, cur, :] = jnp.where(lane < E_A, lse2[:BLK], lse2[BLK:])

        def merge(c, _):
            rows = pl.ds(pl.multiple_of(c * TQ, TQ), TQ)
            l0, l1, l2 = lb_ref[0, rows, :], lb_ref[1, rows, :], lb_ref[2, rows, :]
            m = jnp.maximum(jnp.maximum(l0, l1), l2)
            e0, e1, e2 = jnp.exp(l0 - m), jnp.exp(l1 - m), jnp.exp(l2 - m)
            tot = e0 + e1 + e2
            o_ref[rows, :] = (e0 * ob_ref[0, rows, :] + e1 * ob_ref[1, rows, :] + e2 * ob_ref[2, rows, :]) / tot
            lse_ref[rows, :] = m + jnp.log(tot)
            return 0

        lax.fori_loop(0, S // TQ, merge, 0)

    npair = H // 2
    return pl.pallas_call(
        body, name="dil_fwd", grid=(nb, npair),
        in_specs=[pl.BlockSpec((S, HP), lambda b, p: (b, p)), pl.BlockSpec((S, HP), lambda b, p: (b, npair + p)),
                  pl.BlockSpec((S, HP), lambda b, p: (b, 2 * npair + p)),
                  pl.BlockSpec((3, 1, 2 * BLK, 2 * BLK), lambda b, p: (0, p, 0, 0))],
        out_specs=[pl.BlockSpec((S, HP), lambda b, p: (b, p))] * 2,
        out_shape=[jax.ShapeDtypeStruct((t, D_A), F32)] * 2,
        scratch_shapes=[pltpu.VMEM((3, S, HP), F32), pltpu.VMEM((3, S, HP), F32)],
        compiler_params=_cp("parallel", "parallel"),
    )(proj, proj, proj, biasm)


def _dil_bwd(proj, biasm, o, do, lse):
    t = proj.shape[0]
    nb = t // S

    def body(q_ref, k_ref, v_ref, b_ref, o_ref, do_ref, lse_ref, dq_out, dk_out, dv_out, ds_ref, dq_ref, dk_ref, dv_ref):
        dq_ref[...] = jnp.zeros_like(dq_ref)
        dk_ref[...] = jnp.zeros_like(dk_ref)
        dv_ref[...] = jnp.zeros_like(dv_ref)
        ds_ref[...] = jnp.zeros_like(ds_ref)
        lane = lax.broadcasted_iota(jnp.int32, (BLK, HP), 1)
        own = _own_lanes()
        items = _dil_items()
        for g in range(0, len(items), GROUP):
            grp = items[g:g + GROUP]
            q2s, kts, do2s, ss, dps, lse2s, delta2s = [], [], [], [], [], [], []
            for r, d, start, has_prev in grp:
                cur = _strided(start, d)
                rows = [_strided(start - BLK * d, d), cur] if has_prev else [cur]
                q2 = _stack_heads(q_ref[cur, :] * SCALE_A, own)
                kt = jnp.concatenate([k_ref[x, :] for x in rows], axis=0).astype(BF16)
                vt = jnp.concatenate([v_ref[x, :] for x in rows], axis=0).astype(BF16)
                dot_ = do_ref[cur, :]
                prod = dot_ * o_ref[cur, :]
                lset = lse_ref[cur, :]
                do2 = _stack_heads(dot_, own)
                bias = b_ref[r, 0] if has_prev else b_ref[r, 0, :, BLK:]
                ss.append(_dot(q2, kt, ((1,), (1,))) + bias)
                dps.append(_dot(do2, vt, ((1,), (1,))))
                lse2s.append(jnp.concatenate([lset[:, :1], lset[:, E_A:E_A + 1]], axis=0))
                delta2s.append(jnp.concatenate([jnp.sum(jnp.where(lane < E_A, prod, 0.0), axis=-1, keepdims=True),
                                                jnp.sum(jnp.where(lane >= E_A, prod, 0.0), axis=-1, keepdims=True)], axis=0))
                q2s.append(q2)
                kts.append(kt)
                do2s.append(do2)
            ps = [jnp.exp(s - lse2) for s, lse2 in zip(ss, lse2s)]
            dls = [p * (dp - delta2) for p, dp, delta2 in zip(ps, dps, delta2s)]
            for (r, d, start, has_prev), q2, kt, do2, p, dl in zip(grp, q2s, kts, do2s, ps, dls):
                cur = _strided(start, d)
                dsb = dl.astype(BF16)
                dq2 = _dot(dsb, kt, ((1,), (0,))) * SCALE_A
                dkt = _dot(dsb, q2, ((0,), (0,)))
                dvt = _dot(p.astype(BF16), do2, ((0,), (0,)))
                dq_ref[cur, :] += jnp.where(lane < E_A, dq2[:BLK], dq2[BLK:])
                if has_prev:
                    prev = _strided(start - BLK * d, d)
                    ds_ref[0, r, 0] += dl
                    dk_ref[prev, :] += dkt[:BLK]
                    dv_ref[prev, :] += dvt[:BLK]
                    dk_ref[cur, :] += dkt[BLK:]
                    dv_ref[cur, :] += dvt[BLK:]
                else:
                    ds_ref[0, r, 0, :, BLK:] += dl
                    dk_ref[cur, :] += dkt
                    dv_ref[cur, :] += dvt
        dq_out[...] = dq_ref[...].astype(BF16)
        dk_out[...] = dk_ref[...].astype(BF16)
        dv_out[...] = dv_ref[...].astype(BF16)

    npair = H // 2
    pair = pl.BlockSpec((S, HP), lambda b, p: (b, p))
    return pl.pallas_call(
        body, name="dil_bwd", grid=(nb, npair),
        in_specs=[pair, pl.BlockSpec((S, HP), lambda b, p: (b, npair + p)),
                  pl.BlockSpec((S, HP), lambda b, p: (b, 2 * npair + p)),
                  pl.BlockSpec((3, 1, 2 * BLK, 2 * BLK), lambda b, p: (0, p, 0, 0)), pair, pair, pair],
        out_specs=[pair, pair, pair, pl.BlockSpec((1, 3, 1, 2 * BLK, 2 * BLK), lambda b, p: (b, 0, p, 0, 0))],
        out_shape=[jax.ShapeDtypeStruct((t, D_A), BF16)] * 3 + [jax.ShapeDtypeStruct((nb, 3, npair, 2 * BLK, 2 * BLK), F32)],
        scratch_shapes=[pltpu.VMEM((S, HP), F32)] * 3,
        compiler_params=_cp("parallel", "parallel"),
    )(proj, proj, proj, biasm, o, do, lse)


def _rel_bias_grad(dlogits):
    nb = dlogits.shape[0]
    buckets, _ = _band_geometry()
    kk = 3 * BLK * 2 * BLK
    dl = jnp.transpose(dlogits.reshape(nb, 3, H, BLK, 2 * BLK), (0, 2, 1, 3, 4)).reshape(nb, H, kk)
    bk = jnp.asarray(buckets.reshape(1, kk))
    tk = kk // 4

    def body(dl_ref, bk_ref, o_ref):
        j = pl.program_id(0)
        onehot = (bk_ref[...] == lax.broadcasted_iota(jnp.int32, (N_BUCKETS, tk), 0)).astype(F32)
        tot = dl_ref[0]
        for b in range(1, nb):
            tot = tot + dl_ref[b]
        part = lax.dot_general(onehot, tot, ((((1,), (1,))), ((), ())), preferred_element_type=F32,
                               precision=lax.Precision.HIGHEST)
        _acc_first(j, o_ref, part)

    return pl.pallas_call(
        body, name="rel_bias_grad", grid=(kk // tk,),
        in_specs=[pl.BlockSpec((nb, H, tk), lambda j: (0, 0, j)), pl.BlockSpec((1, tk), lambda j: (0, j))],
        out_specs=pl.BlockSpec((N_BUCKETS, H), lambda j: (0, 0)),
        out_shape=jax.ShapeDtypeStruct((N_BUCKETS, H), F32),
        compiler_params=_cp("arbitrary"),
    )(dl, bk)


def _mesh_place():
    x, y, c = lax.axis_index("x"), lax.axis_index("y"), lax.axis_index("c")
    return x, y, c


def _peer(k):
    x, y, c = _mesh_place()
    px = 1 - x if k & 4 else x
    py = 1 - y if k & 2 else y
    pc = 1 - c if k & 1 else c
    return (px, py, pc), 4 * px + 2 * py + pc


ANY = pl.BlockSpec(memory_space=pl.ANY)


def _exchange(arrays, gathers, name, after=None):
    n_arr = len(arrays)

    def body(*refs):
        ins, outs = refs[:n_arr], refs[n_arr + 1:2 * n_arr + 1]
        send, recv, loc = refs[2 * n_arr + 1:]
        x, y, c = _mesh_place()
        me = 4 * x + 2 * y + c
        local = [pltpu.make_async_copy(ins[a] if gathers[a] else ins[a].at[me], outs[a].at[me], loc.at[a])
                 for a in range(n_arr)]
        remote = _peer_copies(ins, outs, send, recv, gathers)
        for cp in local:
            cp.start()
        for put, _ in remote:
            put.start()
        for cp in local:
            cp.wait()
        for put, got in remote:
            put.wait_send()
            got.wait_recv()

    return pl.pallas_call(
        body, name=name,
        in_specs=[ANY] * (n_arr + 1), out_specs=[ANY] * n_arr,
        out_shape=[jax.ShapeDtypeStruct(((N_DEV,) if g else ()) + a.shape, a.dtype) for a, g in zip(arrays, gathers)],
        scratch_shapes=[pltpu.SemaphoreType.DMA((n_arr * (N_DEV - 1),)), pltpu.SemaphoreType.DMA((n_arr * (N_DEV - 1),)),
                        pltpu.SemaphoreType.DMA((n_arr,))],
        compiler_params=pltpu.CompilerParams(has_side_effects=True),
    )(*arrays, arrays[0] if after is None else after)


def _gather_two_level(arrays, name):
    n_arr = len(arrays)
    per = N_DEV - 1

    def body(*refs):
        ins, outs = refs[:n_arr], refs[n_arr:2 * n_arr]
        send, recv, loc = refs[2 * n_arr:]
        x, y, c = _mesh_place()
        me, sibling = (x, y, c), (x, y, 1 - c)
        chips = [(1 - x, y), (x, 1 - y), (1 - x, 1 - y)]

        def block(a, place):
            px, py, pc = place
            return outs[a].at[4 * px + 2 * py + pc]

        def copy(a, k, place, to, src=None):
            dst = block(a, place)
            return pltpu.make_async_remote_copy(dst if src is None else src, dst, send.at[a * per + k], recv.at[a * per + k],
                                                device_id=to, device_id_type=pl.DeviceIdType.MESH)

        local = [pltpu.make_async_copy(ins[a], block(a, me), loc.at[a]) for a in range(n_arr)]
        for cp in local:
            cp.start()
        first = []
        for a in range(n_arr):
            first.append(copy(a, 0, me, sibling, src=ins[a]))
            first += [copy(a, 1 + j, me, (*chip, c), src=ins[a]) for j, chip in enumerate(chips)]
        for cp in first:
            cp.start()
        passed = []
        for j, chip in enumerate(chips):
            for a in range(n_arr):
                copy(a, 1 + j, (*chip, c), me).wait_recv()
                passed.append(copy(a, 4 + j, (*chip, c), sibling))
                passed[-1].start()
        for a in range(n_arr):
            copy(a, 0, sibling, me).wait_recv()
            for j, chip in enumerate(chips):
                copy(a, 4 + j, (*chip, 1 - c), me).wait_recv()
        for cp in first + passed:
            cp.wait_send()
        for cp in local:
            cp.wait()

    return pl.pallas_call(
        body, name=name,
        in_specs=[ANY] * n_arr, out_specs=[ANY] * n_arr,
        out_shape=[jax.ShapeDtypeStruct((N_DEV,) + a.shape, a.dtype) for a in arrays],
        scratch_shapes=[pltpu.SemaphoreType.DMA((n_arr * per,)), pltpu.SemaphoreType.DMA((n_arr * per,)),
                        pltpu.SemaphoreType.DMA((n_arr,))],
        compiler_params=pltpu.CompilerParams(has_side_effects=True),
    )(*arrays)


HBM = pl.BlockSpec(memory_space=pltpu.HBM)
SEM = pl.BlockSpec(memory_space=pltpu.SEMAPHORE)
DATAFLOW = pltpu.SideEffectType.DATAFLOW_SIDE_EFFECTING


def _own_block_in_place(block, me):
    land = lax.empty((N_DEV,) + block.shape, block.dtype)
    return lax.dynamic_update_slice(land, block[None], (me,) + (0,) * block.ndim)


def _peer_copies(srcs, lands, send, recv, gathers):
    x, y, c = _mesh_place()
    me = 4 * x + 2 * y + c
    out = []
    for a, (src, land) in enumerate(zip(srcs, lands)):
        for k in range(1, N_DEV):
            dev, idx = _peer(k)
            sem = a * (N_DEV - 1) + k - 1
            mine = src if gathers[a] else src.at[idx]
            put = pltpu.make_async_remote_copy(mine, land.at[me], send.at[sem], recv.at[sem],
                                               device_id=dev, device_id_type=pl.DeviceIdType.MESH)
            got = pltpu.make_async_remote_copy(mine, land.at[idx], send.at[sem], recv.at[sem],
                                               device_id=dev, device_id_type=pl.DeviceIdType.MESH)
            out.append((put, got))
    return out


def _exchange_start(srcs, lands, gather, after, name):
    n = len(srcs)
    extra = [] if after is None else [after]

    def body(*refs):
        srcs_, lands_ = refs[:n], refs[n:2 * n]
        send, recv = refs[2 * n + len(extra)], refs[2 * n + len(extra) + 1]
        for put, _ in _peer_copies(srcs_, lands_, send, recv, gather):
            put.start()
        refs[-1][...] = jnp.zeros_like(refs[-1])

    nsem = n * (N_DEV - 1)
    thru = [pltpu.HBM(a.shape, a.dtype) for a in list(srcs) + list(lands)]
    res = pl.pallas_call(
        body, name=name,
        out_shape=(pltpu.SemaphoreType.DMA((nsem,)), pltpu.SemaphoreType.DMA((nsem,)), *thru, jax.ShapeDtypeStruct((8, 128), F32)),
        in_specs=[HBM] * (2 * n) + [ANY] * len(extra),
        out_specs=(SEM, SEM, *([HBM] * (2 * n)), pl.BlockSpec(memory_space=pltpu.VMEM)),
        input_output_aliases={i: 2 + i for i in range(2 * n)},
        compiler_params=pltpu.CompilerParams(has_side_effects=DATAFLOW),
    )(*[pltpu.with_memory_space_constraint(a, pltpu.HBM) for a in list(srcs) + list(lands)], *extra)
    return res[0], res[1], list(res[2:2 + n]), list(res[2 + n:2 + 2 * n]), res[-1]


def _exchange_wait(send, recv, srcs, lands, gather, after, name):
    n = len(srcs)

    def body(*refs):
        srcs_, lands_, send_, recv_ = refs[:n], refs[n:2 * n], refs[2 * n], refs[2 * n + 1]
        for put, got in _peer_copies(srcs_, lands_, send_, recv_, gather):
            put.wait_send()
            got.wait_recv()

    thru = [pltpu.HBM(a.shape, a.dtype) for a in list(srcs) + list(lands)]
    res = pl.pallas_call(
        body, name=name, out_shape=tuple(thru),
        in_specs=[HBM] * (2 * n) + [SEM, SEM, ANY], out_specs=tuple([HBM] * (2 * n)),
        input_output_aliases={i: i for i in range(2 * n)},
        compiler_params=pltpu.CompilerParams(has_side_effects=DATAFLOW),
    )(*srcs, *lands, send, recv, after)
    return list(res[n:])


def _silu_rows(c):
    def body(c_ref, o_ref):
        v = c_ref[...]
        o_ref[...] = v * _sigmoid(v)

    return pl.pallas_call(body, name="cond", out_shape=jax.ShapeDtypeStruct(c.shape, F32))(c)


def _mod_slab(cond_all, w_ada, b_slab):
    def body(c_ref, w_ref, b_ref, o_ref):
        o_ref[...] = _dot(c_ref[...].astype(BF16), w_ref[0].astype(BF16), ((1,), (0,))) + b_ref[...]

    return pl.pallas_call(body, name="mod_slab",
                          out_shape=jax.ShapeDtypeStruct((cond_all.shape[0], w_ada.shape[2]), F32),
                          compiler_params=pltpu.CompilerParams(vmem_limit_bytes=VMEM_LIMIT))(cond_all, w_ada, b_slab)


def _ada_grad(cond_all, dmod_cols):
    def body(c_ref, d_ref, o_ref):
        o_ref[...] = _dot(c_ref[...].astype(BF16), d_ref[...].astype(BF16), ((0,), (0,)))

    return pl.pallas_call(body, name="ada_grad",
                          out_shape=jax.ShapeDtypeStruct((cond_all.shape[1], dmod_cols.shape[1]), F32),
                          compiler_params=pltpu.CompilerParams(vmem_limit_bytes=VMEM_LIMIT))(cond_all, dmod_cols)


def _adam_math(g, w, m, v):
    m2 = B1 * m + (1.0 - B1) * g
    v2 = B2 * v + (1.0 - B2) * (g * g)
    m_hat = m2 / (1.0 - B1 ** STEP)
    v_hat = v2 / (1.0 - B2 ** STEP)
    return -LR * (m_hat / (jnp.sqrt(v_hat) + ADAM_EPS) + WD * w), m2, v2


def _adamw(parts, w, m, v, name):
    n, rows, cols = parts.shape
    tr = max([p for p in range(16, 513, 16) if rows % p == 0] or [rows])

    def body(p_ref, w_ref, m_ref, v_ref, g_ref, d_ref, m2_ref, v2_ref):
        g = p_ref[0].astype(F32)
        for s in range(1, n):
            g = g + p_ref[s].astype(F32)
        g_ref[0] = g
        d_ref[0], m2_ref[0], v2_ref[0] = _adam_math(g, w_ref[0], m_ref[0], v_ref[0])

    blk = pl.BlockSpec((1, tr, cols), lambda i: (0, i, 0))
    return pl.pallas_call(
        body, name=name, grid=(rows // tr,),
        in_specs=[pl.BlockSpec((n, tr, cols), lambda i: (0, i, 0)), blk, blk, blk],
        out_specs=[blk] * 4, out_shape=[jax.ShapeDtypeStruct((1, rows, cols), F32)] * 4,
        compiler_params=_cp("parallel"),
    )(parts, w, m, v)


ROW_PARAMS = (("g_norm1", D), ("g_cq", Q_LORA), ("g_ckv", KV_LORA), ("g_out_a", D_A), ("g_out_b", D_A), ("g_norm2", D),
              ("g_final", D))
LOSS_ROW = N_MOD + len(ROW_PARAMS)
PAY_ROWS = 16
NCOL = N_MOD * D // N_DEV


def _pack_small(dmods, rows, loss_cols):
    nb = dmods[0].shape[0]
    nrow = len(ROW_PARAMS)

    def body(*refs):
        dm, rw, loss_ref, pay_ref, blk_ref = refs[:N_MOD], refs[N_MOD:N_MOD + nrow], refs[N_MOD + nrow], refs[-2], refs[-1]
        pay_ref[...] = jnp.zeros_like(pay_ref)
        for k in range(N_MOD):
            tot = dm[k][0]
            for b in range(1, nb):
                tot = tot + dm[k][b]
            pay_ref[k:k + 1, :] = tot
        for i, (_, n) in enumerate(ROW_PARAMS):
            pay_ref[N_MOD + i:N_MOD + i + 1, :n] = rw[i][...]
        pay_ref[LOSS_ROW:LOSS_ROW + 1, :] = loss_ref[...]
        for j in range(N_DEV):
            done = 0
            while done < NCOL:
                seg, off = divmod(j * NCOL + done, D)
                ln = min(NCOL - done, D - off)
                for b in range(nb):
                    blk_ref[j, b:b + 1, done:done + ln] = dm[seg][b][:, off:off + ln]
                done += ln

    return pl.pallas_call(
        body, name="pack_small",
        out_shape=[jax.ShapeDtypeStruct((PAY_ROWS, D), F32), jax.ShapeDtypeStruct((N_DEV, nb, NCOL), F32)],
    )(*dmods, *rows, loss_cols)


def _small_update(pay, rel, ws, ms, vs):
    n_par = len(ws)

    def body(*refs):
        pay_ref, rel_ref = refs[:2]
        w_refs, m_refs, v_refs = (refs[2 + s * n_par:2 + (s + 1) * n_par] for s in range(3))
        outs, loss_ref = refs[2 + 3 * n_par:-1], refs[-1]
        tot, rtot = pay_ref[0], rel_ref[0]
        for s in range(1, N_DEV):
            tot, rtot = tot + pay_ref[s], rtot + rel_ref[s]

        def update(p, g, sl):
            outs[4 * p][:, sl] = g
            outs[4 * p + 1][:, sl], outs[4 * p + 2][:, sl], outs[4 * p + 3][:, sl] = _adam_math(
                g, w_refs[p][:, sl], m_refs[p][:, sl], v_refs[p][:, sl])

        for k in range(N_MOD):
            update(0, tot[k:k + 1, :], slice(k * D, (k + 1) * D))
        for i, (_, n) in enumerate(ROW_PARAMS):
            update(1 + i, tot[N_MOD + i:N_MOD + i + 1, :n], slice(0, n))
        update(n_par - 1, rtot, slice(0, H))
        loss_ref[...] = jnp.broadcast_to((0.5 / D) * jnp.sum(tot[LOSS_ROW:LOSS_ROW + 1, :]), loss_ref.shape)

    shapes = [jax.ShapeDtypeStruct(w.shape, F32) for w in ws for _ in range(4)]
    res = pl.pallas_call(
        body, name="small_update", out_shape=shapes + [jax.ShapeDtypeStruct((8, 128), F32)],
    )(pay, rel, *ws, *ms, *vs)
    return [tuple(res[4 * p:4 * p + 4]) for p in range(n_par)], res[-1]


def _cols_from_blocks(g):
    return jnp.transpose(g, (1, 0, 2)).reshape(g.shape[1], N_DEV * g.shape[2])


def _cols_to_blocks(w):
    r, c = w.shape
    return jnp.transpose(w.reshape(r, N_DEV, c // N_DEV), (1, 0, 2))


def _pad_w_in(wt):
    z = jnp.zeros((NOPE, wt.shape[1]), wt.dtype)
    return jnp.concatenate([wt[:P_IN - ROPE], z, wt[P_IN - ROPE:], z[:HP - NOPE - ROPE]], axis=0)


def _unpad_w_in(gt):
    k0 = P_IN - ROPE + NOPE
    return jnp.concatenate([gt[:P_IN - ROPE], gt[k0:k0 + ROPE]], axis=0)


def _pad_w_uq(wt):
    return jnp.pad(wt, ((0, 0), (0, HP - NOPE - ROPE), (0, 0))).reshape(H * HP, Q_LORA)


def _unpad_w_uq(gt):
    return gt.reshape(H, HP, Q_LORA)[:, :NOPE + ROPE]


def _split_w_ukv(w):
    w4 = w.reshape(KV_LORA, H // 2, 2, HP)
    z = jnp.zeros((KV_LORA, H // 2, NOPE), w.dtype)
    kn, vv = w4[..., :NOPE], w4[..., NOPE:]
    w_k = jnp.stack([jnp.concatenate([kn[:, :, 0], z], -1), jnp.concatenate([kn[:, :, 1], z], -1)], axis=2)
    w_v = jnp.stack([jnp.concatenate([vv[:, :, 0], z], -1), jnp.concatenate([z, vv[:, :, 1]], -1)], axis=2)
    return w_k.reshape(KV_LORA, H * HP), w_v.reshape(KV_LORA, H * HP)


def _join_w_ukv(g_k, g_v):
    gk = g_k.reshape(KV_LORA, H // 2, 2, HP)
    gv = g_v.reshape(KV_LORA, H // 2, 2, HP)
    even = jnp.concatenate([gk[:, :, 0, :NOPE], gv[:, :, 0, :VDIM]], -1)
    odd = jnp.concatenate([gk[:, :, 1, :NOPE], gv[:, :, 1, VDIM:]], -1)
    return jnp.stack([even, odd], axis=2).reshape(KV_LORA, H * HP)


def _rope_tables():
    half = ROPE // 2
    inv = np.float32(ROPE_THETA) ** (-np.arange(half, dtype=np.float32) / np.float32(half))
    ang = np.arange(S, dtype=np.float32)[:, None] * inv[None, :].astype(np.float32)
    cos, sin = np.cos(ang).astype(np.float32), np.sin(ang).astype(np.float32)
    ones, zeros = np.ones((S, NOPE), np.float32), np.zeros((S, NOPE), np.float32)
    tail1, tail0 = np.ones((S, HP - NOPE - ROPE), np.float32), np.zeros((S, HP - NOPE - ROPE), np.float32)
    zh = np.zeros((S, half), np.float32)
    c = np.concatenate([ones, cos, cos, tail1], axis=1)
    sm = np.concatenate([zeros, -sin, zh, tail0], axis=1)
    sp = np.concatenate([zeros, zh, sin, tail0], axis=1)
    return jnp.asarray(c), jnp.asarray(sm), jnp.asarray(sp)


def _local_step(x, mod, target, g_norm1, w_in_p, g_cq, w_uq_p, g_ckv, w_k, w_v, rel_bias, g_out_a, g_out_b, w_out,
                g_norm2, w_ffn_in, w_ffn_out, g_final, late_weights=None, on_ffn_grads=None, on_last_grads=None):
    nb = x.shape[0] // S
    sh1, sc1, g1, sh2, sc2, g2 = (mod[:, n].reshape(nb, 1, D) for n in range(N_MOD))
    rc, rsm, rsp = _rope_tables()
    biasm = _band_bias(rel_bias)

    h1 = _pre1(x, g_norm1, sc1, sh1)
    proj = _mm_nt(h1, w_in_p, F32, "proj")
    q, k, v, cqn, ckvn = _mla_pre(proj, g_cq, g_ckv, w_uq_p, w_k, w_v, rc, rsm, rsp)
    out_b, lse_b = _mla_fwd(q, k, v)
    out_a, lse_a = _dil_fwd(proj, biasm)
    y = _post_attn(out_a, out_b, g_out_a, g_out_b)
    if late_weights is not None:
        w_out, w_ffn_in, w_ffn_out = late_weights(y)
    mix = _mm_nn(y, w_out, BF16, "mix")
    x2, h2 = _resid_norm2(x, mix, g1, g_norm2, sc2, sh2)
    ffn_g, ffn_u, act = _ffn_in(h2, w_ffn_in)
    f = _mm_nn(act, w_ffn_out, BF16, "ffn_out")
    dx3, df, loss_cols, dg_final, dg2 = _final(x2, f, g2, g_final, target)

    dg_, du_ = _d_act(df, w_ffn_out, ffn_g, ffn_u)
    gw_ffn_out = _mm_tn_rows([act], df, "gw_ffn_out")
    dh2 = _d_h2(dg_, du_, w_ffn_in)
    gw_ffn_in = _mm_tn_rows([dg_, du_], h2, "gw_ffn_in")
    dx2, dsh2, dsc2, dg_norm2, dg1, dmix = _norm_bwd(x2, dh2, dx3, g_norm2, sc2, gate=(mix, g1))
    dy = _mm_nt(dmix, w_out, BF16, "d_y")
    gw_out = _mm_tn(y, [dmix], "gw_out")
    if on_ffn_grads is not None:
        g_out_a = g_out_a + on_ffn_grads(gw_ffn_in, gw_ffn_out, gw_out)
    dout_a, dout_b, dg_out_a, dg_out_b = _post_attn_bwd(dy, out_a, out_b, g_out_a, g_out_b)
    dq_b, dk_b, dv_b = _mla_bwd(q, k, v, out_b, dout_b, lse_b)
    dq_a, dk_a, dv_a, dlogits = _dil_bwd(proj, biasm, out_a, dout_a, lse_a)
    g_rel = _rel_bias_grad(dlogits)
    dqr, dproj, dg_cq, dg_ckv = _mla_pre_bwd(proj, dq_b, dk_b, dv_b, (dq_a, dk_a, dv_a), g_cq, g_ckv, w_uq_p, w_k, w_v,
                                             rc, rsm, rsp)
    gw_uq = _mm_tn(dqr, [cqn], "gw_uq")
    gw_k, gw_v = _mm_tn(ckvn, [dk_b, dv_b], "gw_kv")
    gw_in = _mm_tn_rows([dproj], h1, "gw_in")
    if on_last_grads is not None:
        started = on_last_grads(dict(w_in=gw_in, w_uq=gw_uq, w_k=gw_k, w_v=gw_v))
    else:
        started = None
    dh1 = _mm_nn(dproj, w_in_p, BF16, "d_h1", after=started)
    grad_x, dsh1, dsc1, dg_norm1 = _norm_bwd(x, dh1, dx2, g_norm1, sc1)

    dmod = [dsh1, dsc1, dg1, dsh2, dsc2, dg2]
    small = dict(g_norm1=dg_norm1, g_cq=dg_cq, g_ckv=dg_ckv, rel_bias=g_rel, g_out_a=dg_out_a, g_out_b=dg_out_b,
                 g_norm2=dg_norm2, g_final=dg_final)
    big = dict(w_in=gw_in, w_uq=gw_uq, w_k=gw_k, w_v=gw_v, w_out=gw_out, w_ffn_in=gw_ffn_in, w_ffn_out=gw_ffn_out)
    return grad_x, dmod, loss_cols, small, big


def kernel(x, c, w_ada, b_ada, g_norm1, w_in, g_cq, w_uq, g_ckv, w_ukv, rel_bias, g_out_a, g_out_b, w_out, g_norm2, w_ffn_in, w_ffn_out, g_final, loss_target, m_w_ada, m_b_ada, m_g_norm1, m_w_in, m_g_cq, m_w_uq, m_g_ckv, m_w_ukv, m_rel_bias, m_g_out_a, m_g_out_b, m_w_out, m_g_norm2, m_w_ffn_in, m_w_ffn_out, m_g_final, v_w_ada, v_b_ada, v_g_norm1, v_w_in, v_g_cq, v_w_uq, v_g_ckv, v_w_ukv, v_rel_bias, v_g_out_a, v_g_out_b, v_w_out, v_g_norm2, v_w_ffn_in, v_w_ffn_out, v_g_final):
    nb = x.shape[0]
    t = nb * S
    xt, tt = x.reshape(t, D), loss_target.reshape(t, D)
    me = 4 * lax.axis_index("x") + 2 * lax.axis_index("y") + lax.axis_index("c")

    early = [jnp.swapaxes(w_in, 1, 2)[0], jnp.swapaxes(w_uq, 1, 2)[0], w_ukv[0]]
    gathered = _gather_two_level([_silu_rows(c)] + [s.astype(BF16) for s in early], "gather_weights")
    cond_all = gathered[0].reshape(N_DEV * nb, D)
    w_in_t = gathered[1].reshape(P_IN, D)
    w_ukv_f = _cols_from_blocks(gathered[3])
    w_k, w_v = _split_w_ukv(w_ukv_f)

    ncol = N_MOD * D // N_DEV
    b_slab = lax.dynamic_slice(b_ada, (0, me * ncol), (1, ncol))
    slab = _mod_slab(cond_all, w_ada, b_slab)
    (mod_rows,) = _exchange([slab.reshape(N_DEV, nb, ncol)], [False], "scatter_mod")
    mod = jnp.transpose(mod_rows, (1, 0, 2)).reshape(nb, N_MOD, D)

    late = [s.astype(BF16) for s in (w_out[0], jnp.swapaxes(w_ffn_in, 1, 2)[0], w_ffn_out[0])]
    late_send, late_recv, late_src, late_land, late_token = _exchange_start(
        late, [_own_block_in_place(s, me) for s in late], [True] * 3, mod_rows, "gather_late_start")
    g_norm1_t = g_norm1 + late_token[:1, :1]

    def late_weights(after):
        w_out_g, w_ffn_in_g, w_ffn_out_g = _exchange_wait(late_send, late_recv, late_src, late_land, [True] * 3, after,
                                                          "gather_late_wait")
        return w_out_g.reshape(D, D), w_ffn_in_g.reshape(2 * D_FF, D), w_ffn_out_g.reshape(D_FF, D)

    flight = {}

    def start_grads(key, src, name):
        land = [_own_block_in_place(lax.dynamic_index_in_dim(s, me, 0, keepdims=False), me) for s in src]
        send, recv, src, land, token = _exchange_start(src, land, [False] * len(src), None, name)
        flight[key] = (send, recv, src, land)
        return token[:1, :1]

    def on_ffn_grads(gw_ffn_in, gw_ffn_out, gw_out):
        return start_grads("ffn", [gw_ffn_in.reshape(N_DEV, 2 * D_FF // N_DEV, D), gw_ffn_out.reshape(N_DEV, D_FF // N_DEV, D),
                                   gw_out.reshape(N_DEV, D // N_DEV, D)], "exchange_ffn_start")

    def on_last_grads(gw):
        return start_grads("rest", [_unpad_w_in(gw["w_in"]).reshape(N_DEV, P_IN // N_DEV, D),
                                    _unpad_w_uq(gw["w_uq"]),
                                    _cols_to_blocks(_join_w_ukv(gw["w_k"], gw["w_v"]))], "exchange_rest_start")

    grad_x, dmod, loss_cols, small, _ = _local_step(
        xt, mod, tt, g_norm1_t, _pad_w_in(w_in_t), g_cq, _pad_w_uq(gathered[2]), g_ckv, w_k, w_v, rel_bias, g_out_a, g_out_b,
        None, g_norm2, None, None, g_final.reshape(1, D), late_weights=late_weights, on_ffn_grads=on_ffn_grads,
        on_last_grads=on_last_grads)

    mine, dmod_blocks = _pack_small(dmod, [small[n] for n, _ in ROW_PARAMS], loss_cols)
    small_src = [dmod_blocks, mine, small["rel_bias"]]
    small_gather = [False, True, True]
    small_land = [_own_block_in_place(lax.dynamic_index_in_dim(dmod_blocks, me, 0, keepdims=False), me),
                  _own_block_in_place(mine, me), _own_block_in_place(small["rel_bias"], me)]
    small_send, small_recv, small_src, small_land, _ = _exchange_start(small_src, small_land, small_gather, None,
                                                                       "exchange_small_start")

    upd = {}

    def land_and_update(key, names, after, name):
        got = _exchange_wait(*flight[key], [False] * len(names), after, name)
        for n, p in zip(names, got):
            w, m, v = big[n]
            upd[n] = _adamw(p, w, m, v, "adamw_" + n)

    def flip(a):
        return jnp.swapaxes(a, 1, 2)

    big = dict(w_in=(flip(w_in), flip(m_w_in), flip(v_w_in)), w_uq=(flip(w_uq), flip(m_w_uq), flip(v_w_uq)),
               w_ukv=(w_ukv, m_w_ukv, v_w_ukv),
               w_out=(w_out, m_w_out, v_w_out), w_ffn_in=(flip(w_ffn_in), flip(m_w_ffn_in), flip(v_w_ffn_in)),
               w_ffn_out=(w_ffn_out, m_w_ffn_out, v_w_ffn_out))
    land_and_update("ffn", ["w_ffn_in", "w_ffn_out", "w_out"], grad_x, "exchange_ffn_wait")
    land_and_update("rest", ["w_in", "w_uq", "w_ukv"], upd["w_out"][0], "exchange_rest_wait")
    for n in ("w_in", "w_uq", "w_ffn_in"):
        upd[n] = tuple(flip(a) for a in upd[n])

    dmod_cols, pay, rel = _exchange_wait(small_send, small_recv, small_src, small_land, small_gather, upd["w_ukv"][0],
                                         "exchange_small_wait")
    g_ada = _ada_grad(cond_all, dmod_cols.reshape(N_DEV * nb, ncol))
    upd["w_ada"] = _adamw(g_ada[None], w_ada, m_w_ada, v_w_ada, "adamw_w_ada")
    row = lambda a: a.reshape(1, D)
    small_names = ["b_ada"] + [n for n, _ in ROW_PARAMS] + ["rel_bias"]
    small_w = [b_ada, g_norm1, g_cq, g_ckv, g_out_a, g_out_b, g_norm2, row(g_final), rel_bias]
    small_m = [m_b_ada, m_g_norm1, m_g_cq, m_g_ckv, m_g_out_a, m_g_out_b, m_g_norm2, row(m_g_final), m_rel_bias]
    small_v = [v_b_ada, v_g_norm1, v_g_cq, v_g_ckv, v_g_out_a, v_g_out_b, v_g_norm2, row(v_g_final), v_rel_bias]
    small_upd, loss8 = _small_update(pay, rel, small_w, small_m, small_v)
    upd.update(zip(small_names, small_upd))

    order = ["w_ada", "b_ada", "g_norm1", "w_in", "g_cq", "w_uq", "g_ckv", "w_ukv", "rel_bias", "g_out_a", "g_out_b",
             "w_out", "g_norm2", "w_ffn_in", "w_ffn_out", "g_final"]
    like = dict(g_final=g_final)
    outs = [loss8[0, 0], grad_x.reshape(x.shape)]
    for part in range(4):
        for n in order:
            val = upd[n][part]
            outs.append(val.reshape(like[n].shape) if n in like else val)
    return tuple(outs)
```

```python
import numpy as np
import jax
import jax.numpy as jnp
from jax import lax
from jax.experimental import pallas as pl
from jax.experimental.pallas import tpu as pltpu

F32, BF16 = jnp.float32, jnp.bfloat16

N_DEV = 8
D = 1024
S = 2048
H = 8
E_A = 64
D_A = H * E_A
Q_LORA, KV_LORA = 384, 256
NOPE, ROPE, VDIM = 64, 32, 64
HP = 128
P_IN = 3 * D_A + Q_LORA + KV_LORA + ROPE
P_PAD = 3 * D_A + Q_LORA + KV_LORA + HP
TAIL0 = 3 * D_A
TAIL = P_PAD - TAIL0
D_FF = 2816
N_MOD = 6
EPS = 1e-6
NEG = -1e30
BLK = 128
DILATIONS = (1, 4, 16)
N_BUCKETS, MAX_DISTANCE = 32, 2048
ROPE_THETA = 10000.0
SCALE_A = E_A ** -0.5
SCALE_B = (NOPE + ROPE) ** -0.5
B1, B2, LR, ADAM_EPS, WD, STEP = 0.9, 0.999, 0.001, 1e-8, 0.01, 10
VMEM_LIMIT = 56 * 1024 * 1024


def _cp(*sem):
    return pltpu.CompilerParams(dimension_semantics=sem, vmem_limit_bytes=VMEM_LIMIT)


def _pick(n, prefs):
    for p in prefs:
        if n % p == 0:
            return p
    raise ValueError(f"no tile of {prefs} divides {n}")


OPERAND_BYTES = 6 * 1024 * 1024


def _pick_rows(m, k):
    return _pick(m, [p for p in (1024, 512, 256, 128, 16) if p * k * 2 <= OPERAND_BYTES])


MATMUL_BYTES = 40 * 1024 * 1024


def _stream_rows(m, fixed, per_row):
    return _pick(m, [p for p in (4096, 2048, 1024, 512, 256, 128, 16) if fixed + p * per_row <= MATMUL_BYTES])


def _dot(a, b, dims):
    return lax.dot_general(a, b, (dims, ((), ())), preferred_element_type=F32)


def _mm_nn(a, b, out_dtype, name, after=None):
    m, k = a.shape
    n = b.shape[1]
    tn = _pick(n, (512, 256, 384, 128))
    tm = _stream_rows(m, 4 * k * tn, 4 * k + (2 * jnp.dtype(out_dtype).itemsize + 4) * tn)

    def body(a_ref, b_ref, *rest):
        o_ref = rest[-1]
        o_ref[...] = _dot(a_ref[...], b_ref[...], ((1,), (0,))).astype(o_ref.dtype)

    extra = [] if after is None else [after]
    return pl.pallas_call(
        body, name=name, grid=(m // tm, n // tn),
        in_specs=[pl.BlockSpec((tm, k), lambda i, j: (i, 0)), pl.BlockSpec((k, tn), lambda i, j: (0, j))] + [ANY] * len(extra),
        out_specs=pl.BlockSpec((tm, tn), lambda i, j: (i, j)),
        out_shape=jax.ShapeDtypeStruct((m, n), out_dtype),
        compiler_params=_cp("parallel", "parallel"),
    )(a, b, *extra)


def _mm_nt(a, b, out_dtype, name, after=None):
    m, k = a.shape
    n = b.shape[0]
    tn = _pick(n, (512, 256, 384, 128))
    tm = _stream_rows(m, 4 * k * tn, 4 * k + (2 * jnp.dtype(out_dtype).itemsize + 4) * tn)

    def body(a_ref, b_ref, *rest):
        o_ref = rest[-1]
        o_ref[...] = _dot(a_ref[...], b_ref[...], ((1,), (1,))).astype(o_ref.dtype)

    extra = [] if after is None else [after]
    return pl.pallas_call(
        body, name=name, grid=(m // tm, n // tn),
        in_specs=[pl.BlockSpec((tm, k), lambda i, j: (i, 0)), pl.BlockSpec((tn, k), lambda i, j: (j, 0))] + [ANY] * len(extra),
        out_specs=pl.BlockSpec((tm, tn), lambda i, j: (i, j)),
        out_shape=jax.ShapeDtypeStruct((m, n), out_dtype),
        compiler_params=_cp("parallel", "parallel"),
    )(a, b, *extra)


def _mm_tn(a, bs, name):
    t, m = a.shape
    n = bs[0].shape[1]
    nb_ = len(bs)
    tc = _pick(t, (512, 16))
    tn = _pick(n, (512, 384, 256, 128))
    tm = _pick(m, [p for p in (1024, 512, 384, 256, 128) if (3 * p + 2 * nb_ * tn) * t * 2 <= VMEM_LIMIT - 2 * OPERAND_BYTES])
    if tm <= 256 and nb_ * n * t * 2 <= 2 * OPERAND_BYTES:
        tn = n

    def body(*refs):
        a_ref, b_refs, o_refs, at_ref = refs[0], refs[1:1 + nb_], refs[1 + nb_:1 + 2 * nb_], refs[-1]

        @pl.when(pl.program_id(1) == 0)
        def _():
            def chunk(c, _):
                rows = pl.ds(pl.multiple_of(c * tc, tc), tc)
                at_ref[:, rows] = a_ref[rows, :].T
                return 0

            lax.fori_loop(0, t // tc, chunk, 0)

        for b_ref, o_ref in zip(b_refs, o_refs):
            o_ref[...] = _dot(at_ref[...], b_ref[...], ((1,), (0,))).astype(BF16)

    res = pl.pallas_call(
        body, name=name, grid=(m // tm, n // tn),
        in_specs=[pl.BlockSpec((t, tm), lambda i, j: (0, i))] + [pl.BlockSpec((t, tn), lambda i, j: (0, j))] * nb_,
        out_specs=[pl.BlockSpec((tm, tn), lambda i, j: (i, j))] * nb_,
        out_shape=[jax.ShapeDtypeStruct((m, n), BF16)] * nb_,
        scratch_shapes=[pltpu.VMEM((tm, t), BF16)],
        compiler_params=_cp("parallel", "arbitrary"),
    )(a, *bs)
    return res[0] if nb_ == 1 else res


def _mm_tn_rows(a_list, b, name):
    t, m = a_list[0].shape
    n = b.shape[1]
    na = len(a_list)
    tc, tm = _pick(t, (512, 16)), _pick(m, (256, 128))
    nblk = m // tm

    def body(*refs):
        a_refs, b_ref, o_ref, bt_ref, r_ref = refs[:na], refs[na], refs[na + 1], refs[na + 2], refs[na + 3]
        i = pl.program_id(0)

        @pl.when(i == 0)
        def _():
            def chunk(c, _):
                rows = pl.ds(pl.multiple_of(c * tc, tc), tc)
                bt_ref[:, rows] = b_ref[rows, :].T
                return 0

            lax.fori_loop(0, t // tc, chunk, 0)

        for s, a_ref in enumerate(a_refs):
            @pl.when((i >= s * nblk) & (i < (s + 1) * nblk))
            def _(a_ref=a_ref):
                r_ref[...] = _dot(bt_ref[...], a_ref[...], ((1,), (0,)))
                o_ref[...] = r_ref[...].T.astype(BF16)

    return pl.pallas_call(
        body, name=name, grid=(na * nblk,),
        in_specs=[pl.BlockSpec((t, tm), lambda i, s=s: (0, jnp.clip(i - s * nblk, 0, nblk - 1))) for s in range(na)]
        + [pl.BlockSpec((t, n), lambda i: (0, 0))],
        out_specs=pl.BlockSpec((tm, n), lambda i: (i, 0)),
        out_shape=jax.ShapeDtypeStruct((na * m, n), BF16),
        scratch_shapes=[pltpu.VMEM((n, t), BF16), pltpu.VMEM((n, tm), F32)],
        compiler_params=_cp("arbitrary"),
    )(*a_list, b)


EPI = 256


def _silu_parts(g):
    sg = 0.5 * jnp.tanh(0.5 * g) + 0.5
    return sg, g * sg


def _ffn_in(h2, wt):
    t, k = h2.shape
    tn = _pick(D_FF, (256, 128))
    tm = _stream_rows(t, 8 * k * tn, 4 * k + (3 * 2 * 2 + 2 * 4) * tn)
    nj = D_FF // tn

    def body(h_ref, wg_ref, wu_ref, g_ref, u_ref, a_ref):
        hv = h_ref[...]
        g_all = _dot(hv, wg_ref[...], ((1,), (1,)))
        u_all = _dot(hv, wu_ref[...], ((1,), (1,)))
        for r in range(tm // EPI):
            rows = slice(r * EPI, (r + 1) * EPI)
            g, u = g_all[rows], u_all[rows]
            g_ref[rows, :] = g.astype(BF16)
            u_ref[rows, :] = u.astype(BF16)
            a_ref[rows, :] = (_silu_parts(g)[1] * u).astype(BF16)

    blk = pl.BlockSpec((tm, tn), lambda i, j: (i, j))
    return pl.pallas_call(
        body, name="ffn_in", grid=(t // tm, nj),
        in_specs=[pl.BlockSpec((tm, k), lambda i, j: (i, 0)), pl.BlockSpec((tn, k), lambda i, j: (j, 0)),
                  pl.BlockSpec((tn, k), lambda i, j: (j + nj, 0))],
        out_specs=[blk] * 3, out_shape=[jax.ShapeDtypeStruct((t, D_FF), BF16)] * 3,
        compiler_params=_cp("parallel", "parallel"),
    )(h2, wt, wt)


def _d_act(df, w, g, u):
    t, k = df.shape
    tn = _pick(D_FF, (256, 128))
    tm = _stream_rows(t, 4 * k * tn, 4 * k + (4 * 2 * 2 + 4) * tn)

    def body(df_ref, w_ref, g_ref, u_ref, dg_ref, du_ref):
        da_all = _dot(df_ref[...], w_ref[...], ((1,), (1,)))
        for r in range(tm // EPI):
            rows = slice(r * EPI, (r + 1) * EPI)
            da = da_all[rows]
            gv = g_ref[rows, :].astype(F32)
            sg, silu = _silu_parts(gv)
            dg_ref[rows, :] = ((da * u_ref[rows, :].astype(F32)) * (sg + silu * (1.0 - sg))).astype(BF16)
            du_ref[rows, :] = (da * silu).astype(BF16)

    blk = pl.BlockSpec((tm, tn), lambda i, j: (i, j))
    return pl.pallas_call(
        body, name="d_act", grid=(t // tm, D_FF // tn),
        in_specs=[pl.BlockSpec((tm, k), lambda i, j: (i, 0)), pl.BlockSpec((tn, k), lambda i, j: (j, 0)), blk, blk],
        out_specs=[blk] * 2, out_shape=[jax.ShapeDtypeStruct((t, D_FF), BF16)] * 2,
        compiler_params=_cp("parallel", "parallel"),
    )(df, w, g, u)


def _d_h2(dg, du, wt):
    t = dg.shape[0]
    n = wt.shape[1]
    tm, tn = _pick_rows(t, D_FF), _pick(n, (512, 256, 128))

    def body(dg_ref, du_ref, wg_ref, wu_ref, o_ref):
        o_ref[...] = (_dot(dg_ref[...], wg_ref[...], ((1,), (0,)))
                      + _dot(du_ref[...], wu_ref[...], ((1,), (0,)))).astype(BF16)

    return pl.pallas_call(
        body, name="d_h2", grid=(t // tm, n // tn),
        in_specs=[pl.BlockSpec((tm, D_FF), lambda i, j: (i, 0)), pl.BlockSpec((tm, D_FF), lambda i, j: (i, 0)),
                  pl.BlockSpec((D_FF, tn), lambda i, j: (0, j)), pl.BlockSpec((D_FF, tn), lambda i, j: (1, j))],
        out_specs=pl.BlockSpec((tm, tn), lambda i, j: (i, j)),
        out_shape=jax.ShapeDtypeStruct((t, n), BF16),
        compiler_params=_cp("parallel", "parallel"),
    )(dg, du, wt, wt)


TM = 1024


def _row(w):
    return pl.BlockSpec((TM, w), lambda i: (i, 0))


def _row_at(w, col):
    return pl.BlockSpec((TM, w), lambda i: (i, col))


def _vec(w):
    return pl.BlockSpec((1, w), lambda i: (0, 0))


def _per_ex(w):
    return pl.BlockSpec((1, 1, w), lambda i: (i // (S // TM), 0, 0))


def _pos(w):
    return pl.BlockSpec((TM, w), lambda i: (i % (S // TM), 0))


def _full(shape):
    return pl.BlockSpec(shape, lambda i: (0,) * len(shape))


def _rms(x):
    return lax.rsqrt(jnp.mean(x * x, axis=-1, keepdims=True) + EPS)


def _rms_bwd(n, r, dn):
    return r * (dn - n * jnp.mean(dn * n, axis=-1, keepdims=True))


def _colsum(v):
    return jnp.sum(v, axis=0, keepdims=True)


def _acc_first(i, ref, val, every=None):
    first = (i == 0) if every is None else (i % every == 0)

    @pl.when(first)
    def _():
        ref[...] = jnp.zeros_like(ref)

    ref[...] += val.reshape(ref.shape)


def _pre1(x, g, sc, sh):
    t = x.shape[0]

    def body(x_ref, g_ref, sc_ref, sh_ref, h_ref):
        xv = x_ref[...]
        n = xv * _rms(xv)
        h_ref[...] = ((n * g_ref[...]) * (1.0 + sc_ref[0]) + sh_ref[0]).astype(BF16)

    return pl.pallas_call(
        body, name="pre1", grid=(t // TM,),
        in_specs=[_row(D), _vec(D), _per_ex(D), _per_ex(D)],
        out_specs=_row(D), out_shape=jax.ShapeDtypeStruct((t, D), BF16),
        compiler_params=_cp("parallel"),
    )(x, g, sc, sh)


def _rope_fwd(v, c, sm, sp):
    return v * c + pltpu.roll(v, HP - ROPE // 2, 1) * sm + pltpu.roll(v, ROPE // 2, 1) * sp


def _rope_bwd(dv, c, sm, sp):
    return dv * c + pltpu.roll(dv * sm, ROPE // 2, 1) + pltpu.roll(dv * sp, HP - ROPE // 2, 1)


def _mla_pre(proj, g_cq, g_ckv, w_uq, w_k, w_v, rc, rsm, rsp):
    t = proj.shape[0]

    def body(tail_ref, gq_ref, gkv_ref, wuq_ref, wk_ref, wv_ref, c_ref, sm_ref, sp_ref,
             q_ref, k_ref, v_ref, cqn_ref, ckvn_ref):
        tail = tail_ref[...]
        cq, ckv, kr = tail[:, :Q_LORA], tail[:, Q_LORA:Q_LORA + KV_LORA], tail[:, Q_LORA + KV_LORA:]
        cqn = (cq * _rms(cq) * gq_ref[...]).astype(BF16)
        ckvn = (ckv * _rms(ckv) * gkv_ref[...]).astype(BF16)
        cqn_ref[...] = cqn
        ckvn_ref[...] = ckvn
        c, sm, sp = c_ref[...], sm_ref[...], sp_ref[...]
        q = _dot(cqn, wuq_ref[...], ((1,), (1,)))
        kn = _dot(ckvn, wk_ref[...], ((1,), (0,)))
        v_ref[...] = _dot(ckvn, wv_ref[...], ((1,), (0,))).astype(BF16)
        krr = _rope_fwd(kr, c, sm, sp)
        for h in range(H):
            sl = slice(h * HP, (h + 1) * HP)
            q_ref[:, sl] = _rope_fwd(q[:, sl], c, sm, sp).astype(BF16)
            k_ref[:, sl] = (kn[:, sl] + krr).astype(BF16)

    wide = H * HP
    return pl.pallas_call(
        body, name="mla_pre", grid=(t // TM,),
        in_specs=[_row_at(TAIL, TAIL0 // TAIL), _vec(Q_LORA), _vec(KV_LORA), _full((wide, Q_LORA)),
                  _full((KV_LORA, wide)), _full((KV_LORA, wide)), _pos(HP), _pos(HP), _pos(HP)],
        out_specs=[_row(wide), _row(wide), _row(wide), _row(Q_LORA), _row(KV_LORA)],
        out_shape=[jax.ShapeDtypeStruct((t, wide), BF16)] * 3
        + [jax.ShapeDtypeStruct((t, Q_LORA), BF16), jax.ShapeDtypeStruct((t, KV_LORA), BF16)],
        compiler_params=_cp("parallel"),
    )(proj, g_cq, g_ckv, w_uq, w_k, w_v, rc, rsm, rsp)


def _mla_pre_bwd(proj, dq_, dk_, dv_, dqkv_a, g_cq, g_ckv, w_uq, w_k, w_v, rc, rsm, rsp):
    t = proj.shape[0]
    wide = H * HP

    def body(tail_ref, dq_ref, dk_ref, dv_ref, dqa_ref, dka_ref, dva_ref, gq_ref, gkv_ref, wuq_ref, wk_ref, wv_ref,
             c_ref, sm_ref, sp_ref, dqo_ref, dproj_ref, dgq_ref, dgkv_ref):
        i = pl.program_id(0)
        for n, src in enumerate((dqa_ref, dka_ref, dva_ref)):
            dproj_ref[:, n * D_A:(n + 1) * D_A] = src[...]
        dtail_ref = dproj_ref.at[:, TAIL0:]
        tail = tail_ref[...]
        cq, ckv = tail[:, :Q_LORA], tail[:, Q_LORA:Q_LORA + KV_LORA]
        c, sm, sp = c_ref[...], sm_ref[...], sp_ref[...]
        dkr = jnp.zeros((TM, HP), F32)
        for h in range(H):
            sl = slice(h * HP, (h + 1) * HP)
            dqo_ref[:, sl] = _rope_bwd(dq_ref[:, sl].astype(F32), c, sm, sp).astype(BF16)
            dkr = dkr + dk_ref[:, sl].astype(F32)
        lane = lax.broadcasted_iota(jnp.int32, (TM, HP), 1)
        dkr = jnp.where((lane >= NOPE) & (lane < NOPE + ROPE), _rope_bwd(dkr, c, sm, sp), 0.0)
        dkb = dk_ref[...]
        dvb = dv_ref[...]
        dcqn = _dot(dqo_ref[...], wuq_ref[...], ((1,), (0,)))
        dckvn = _dot(dkb, wk_ref[...], ((1,), (1,))) + _dot(dvb, wv_ref[...], ((1,), (1,)))
        rq, rkv = _rms(cq), _rms(ckv)
        nq, nkv = cq * rq, ckv * rkv
        _acc_first(i, dgq_ref, _colsum(dcqn * nq))
        _acc_first(i, dgkv_ref, _colsum(dckvn * nkv))
        dtail_ref[:, :Q_LORA] = _rms_bwd(nq, rq, dcqn * gq_ref[...]).astype(BF16)
        dtail_ref[:, Q_LORA:Q_LORA + KV_LORA] = _rms_bwd(nkv, rkv, dckvn * gkv_ref[...]).astype(BF16)
        dtail_ref[:, Q_LORA + KV_LORA:] = dkr.astype(BF16)

    return pl.pallas_call(
        body, name="mla_pre_bwd", grid=(t // TM,),
        in_specs=[_row_at(TAIL, TAIL0 // TAIL), _row(wide), _row(wide), _row(wide), _row(D_A), _row(D_A), _row(D_A),
                  _vec(Q_LORA), _vec(KV_LORA), _full((wide, Q_LORA)), _full((KV_LORA, wide)), _full((KV_LORA, wide)),
                  _pos(HP), _pos(HP), _pos(HP)],
        out_specs=[_row(wide), _row(P_PAD), _vec(Q_LORA), _vec(KV_LORA)],
        out_shape=[jax.ShapeDtypeStruct((t, wide), BF16), jax.ShapeDtypeStruct((t, P_PAD), BF16),
                   jax.ShapeDtypeStruct((1, Q_LORA), F32), jax.ShapeDtypeStruct((1, KV_LORA), F32)],
        compiler_params=_cp("arbitrary"),
    )(proj, dq_, dk_, dv_, *dqkv_a, g_cq, g_ckv, w_uq, w_k, w_v, rc, rsm, rsp)


def _post_attn(out_a, out_b, g_a, g_b):
    t = out_a.shape[0]

    def body(a_ref, b_ref, ga_ref, gb_ref, y_ref):
        a, b = a_ref[...], b_ref[...].astype(F32)
        y_ref[:, :D_A] = (a * _rms(a) * ga_ref[...]).astype(BF16)
        y_ref[:, D_A:] = (b * _rms(b) * gb_ref[...]).astype(BF16)

    return pl.pallas_call(
        body, name="post_attn", grid=(t // TM,),
        in_specs=[_row(D_A), _row(D_A), _vec(D_A), _vec(D_A)],
        out_specs=_row(D), out_shape=jax.ShapeDtypeStruct((t, D), BF16),
        compiler_params=_cp("parallel"),
    )(out_a, out_b, g_a, g_b)


def _post_attn_bwd(dy, out_a, out_b, g_a, g_b):
    t = dy.shape[0]

    def body(dy_ref, a_ref, b_ref, ga_ref, gb_ref, da_ref, db_ref, dga_ref, dgb_ref):
        i = pl.program_id(0)
        dy_ = dy_ref[...].astype(F32)
        for src, g_ref, dst, dg_ref, sl in ((a_ref, ga_ref, da_ref, dga_ref, slice(0, D_A)),
                                            (b_ref, gb_ref, db_ref, dgb_ref, slice(D_A, D))):
            v = src[...].astype(F32)
            r = _rms(v)
            n = v * r
            dyv = dy_[:, sl]
            _acc_first(i, dg_ref, _colsum(dyv * n))
            dst[...] = _rms_bwd(n, r, dyv * g_ref[...]).astype(dst.dtype)

    return pl.pallas_call(
        body, name="post_attn_bwd", grid=(t // TM,),
        in_specs=[_row(D), _row(D_A), _row(D_A), _vec(D_A), _vec(D_A)],
        out_specs=[_row(D_A), _row(D_A), _vec(D_A), _vec(D_A)],
        out_shape=[jax.ShapeDtypeStruct((t, D_A), F32), jax.ShapeDtypeStruct((t, D_A), BF16)]
        + [jax.ShapeDtypeStruct((1, D_A), F32)] * 2,
        compiler_params=_cp("arbitrary"),
    )(dy, out_a, out_b, g_a, g_b)


def _resid_norm2(x, mix, g1, g, sc, sh):
    t = x.shape[0]

    def body(x_ref, mix_ref, g1_ref, g_ref, sc_ref, sh_ref, x2_ref, h_ref):
        x2 = x_ref[...] + g1_ref[0] * mix_ref[...]
        x2_ref[...] = x2
        n = x2 * _rms(x2)
        h_ref[...] = ((n * g_ref[...]) * (1.0 + sc_ref[0]) + sh_ref[0]).astype(BF16)

    return pl.pallas_call(
        body, name="resid_norm2", grid=(t // TM,),
        in_specs=[_row(D), _row(D), _per_ex(D), _vec(D), _per_ex(D), _per_ex(D)],
        out_specs=[_row(D), _row(D)],
        out_shape=[jax.ShapeDtypeStruct((t, D), F32), jax.ShapeDtypeStruct((t, D), BF16)],
        compiler_params=_cp("parallel"),
    )(x, mix, g1, g, sc, sh)


def _sigmoid(v):
    return 1.0 / (1.0 + jnp.exp(-v))


def _final(x2, f, g2, g_fin, target):
    t = x2.shape[0]
    nb = t // S
    tpb = S // TM

    def body(x2_ref, f_ref, g2_ref, g_ref, t_ref, dx3_ref, df_ref, loss_ref, dgf_ref, dg2_ref):
        i = pl.program_id(0)
        fv = f_ref[...].astype(F32)
        x3 = x2_ref[...] + g2_ref[0] * fv
        r = _rms(x3)
        n = x3 * r
        err = n * g_ref[...] - t_ref[...]
        _acc_first(i, loss_ref, _colsum(err * err))
        dy = err * (1.0 / D)
        _acc_first(i, dgf_ref, _colsum(dy * n))
        dx3 = _rms_bwd(n, r, dy * g_ref[...])
        dx3_ref[...] = dx3.astype(BF16)
        _acc_first(i, dg2_ref, _colsum(dx3 * fv), every=tpb)
        df_ref[...] = (dx3 * g2_ref[0]).astype(BF16)

    return pl.pallas_call(
        body, name="final", grid=(t // TM,),
        in_specs=[_row(D), _row(D), _per_ex(D), _vec(D), _row(D)],
        out_specs=[_row(D), _row(D), _vec(D), _vec(D), _per_ex(D)],
        out_shape=[jax.ShapeDtypeStruct((t, D), BF16), jax.ShapeDtypeStruct((t, D), BF16),
                   jax.ShapeDtypeStruct((1, D), F32), jax.ShapeDtypeStruct((1, D), F32),
                   jax.ShapeDtypeStruct((nb, 1, D), F32)],
        compiler_params=_cp("arbitrary"),
    )(x2, f, g2, g_fin, target)


def _norm_bwd(xin, dh, dres, g, sc, gate=None):
    t = xin.shape[0]
    nb = t // S
    tpb = S // TM
    gated = gate is not None

    def body(*refs):
        if gated:
            x_ref, dh_ref, dres_ref, g_ref, sc_ref, mix_ref, g1_ref, dx_ref, dsh_ref, dsc_ref, dg_ref, dg1_ref, dmix_ref = refs
        else:
            x_ref, dh_ref, dres_ref, g_ref, sc_ref, dx_ref, dsh_ref, dsc_ref, dg_ref = refs
        i = pl.program_id(0)
        xv, dhv = x_ref[...], dh_ref[...].astype(F32)
        r = _rms(xv)
        n = xv * r
        gv = g_ref[...]
        _acc_first(i, dsh_ref, _colsum(dhv), every=tpb)
        _acc_first(i, dsc_ref, _colsum(dhv * (n * gv)), every=tpb)
        dng = dhv * (1.0 + sc_ref[0])
        _acc_first(i, dg_ref, _colsum(dng * n))
        dx = dres_ref[...].astype(F32) + _rms_bwd(n, r, dng * gv)
        dx_ref[...] = dx.astype(dx_ref.dtype)
        if gated:
            _acc_first(i, dg1_ref, _colsum(dx * mix_ref[...].astype(F32)), every=tpb)
            dmix_ref[...] = (dx * g1_ref[0]).astype(BF16)

    in_specs = [_row(D), _row(D), _row(D), _vec(D), _per_ex(D)]
    out_specs = [_row(D), _per_ex(D), _per_ex(D), _vec(D)]
    out_shape = [jax.ShapeDtypeStruct((t, D), BF16 if gated else F32), jax.ShapeDtypeStruct((nb, 1, D), F32),
                 jax.ShapeDtypeStruct((nb, 1, D), F32), jax.ShapeDtypeStruct((1, D), F32)]
    args = [xin, dh, dres, g, sc]
    if gated:
        in_specs += [_row(D), _per_ex(D)]
        out_specs += [_per_ex(D), _row(D)]
        out_shape += [jax.ShapeDtypeStruct((nb, 1, D), F32), jax.ShapeDtypeStruct((t, D), BF16)]
        args += list(gate)
    return pl.pallas_call(
        body, name="norm2_bwd" if gated else "norm1_bwd", grid=(t // TM,),
        in_specs=in_specs, out_specs=out_specs, out_shape=out_shape,
        compiler_params=_cp("arbitrary"),
    )(*args)


TQ = 256
TB = 512
FWD_HEADS = 2


def _mla_fwd(q, k, v):
    t = q.shape[0]
    nb = t // S

    def body(q_ref, k_ref, v_ref, o_ref, lse_ref):
        causal = lax.broadcasted_iota(jnp.int32, (TB, TB), 0) >= lax.broadcasted_iota(jnp.int32, (TB, TB), 1)
        heads = [slice(h * HP, (h + 1) * HP) for h in range(FWD_HEADS)]
        for i in range(S // TB):
            ri, past = slice(i * TB, (i + 1) * TB), slice(0, i * TB)
            qhs = [q_ref[ri, sl] for sl in heads]
            sd = [jnp.where(causal, _dot(qh, k_ref[ri, sl], ((1,), (1,))) * SCALE_B, NEG) for qh, sl in zip(qhs, heads)]
            ms = [jnp.max(s, axis=-1, keepdims=True) for s in sd]
            if i:
                so = [_dot(qh, k_ref[past, sl], ((1,), (1,))) * SCALE_B for qh, sl in zip(qhs, heads)]
                ms = [jnp.maximum(m, jnp.max(s, axis=-1, keepdims=True)) for m, s in zip(ms, so)]
            pd = [jnp.exp(s - m) for s, m in zip(sd, ms)]
            ls = [jnp.sum(p, axis=-1, keepdims=True) for p in pd]
            acc = [_dot(p.astype(BF16), v_ref[ri, sl], ((1,), (0,))) for p, sl in zip(pd, heads)]
            if i:
                po = [jnp.exp(s - m) for s, m in zip(so, ms)]
                ls = [l + jnp.sum(p, axis=-1, keepdims=True) for l, p in zip(ls, po)]
                acc = [a + _dot(p.astype(BF16), v_ref[past, sl], ((1,), (0,))) for a, p, sl in zip(acc, po, heads)]
            for pr in range(FWD_HEADS // 2):
                o_ref[ri, pr * HP:(pr + 1) * HP] = (acc[2 * pr] / ls[2 * pr]
                                                    + acc[2 * pr + 1] / ls[2 * pr + 1]).astype(o_ref.dtype)
            for sl, m, l in zip(heads, ms, ls):
                lse_ref[ri, sl] = jnp.broadcast_to(m + jnp.log(l), (TB, HP))

    wide2 = pl.BlockSpec((S, FWD_HEADS * HP), lambda b, p: (b, p))
    return pl.pallas_call(
        body, name="mla_fwd", grid=(nb, H // FWD_HEADS),
        in_specs=[wide2, wide2, wide2],
        out_specs=[pl.BlockSpec((S, FWD_HEADS // 2 * HP), lambda b, p: (b, p)), wide2],
        out_shape=[jax.ShapeDtypeStruct((t, H * VDIM), BF16), jax.ShapeDtypeStruct((t, H * HP), F32)],
        compiler_params=_cp("parallel", "parallel"),
    )(q, k, v)


def _mla_bwd(q, k, v, o, do, lse):
    t = q.shape[0]
    nb = t // S

    def body(q_ref, k_ref, v_ref, o_ref, do_ref, lse_ref, dq_out, dk_out, dv_out, dq_ref, dk_ref, dv_ref):
        lane = lax.broadcasted_iota(jnp.int32, (TB, HP), 1)
        causal = lax.broadcasted_iota(jnp.int32, (TB, TB), 0) >= lax.broadcasted_iota(jnp.int32, (TB, TB), 1)
        heads = [slice(h * HP, (h + 1) * HP) for h in range(2)]
        nblk = S // TB
        for i in reversed(range(nblk)):
            ri, past = slice(i * TB, (i + 1) * TB), slice(0, i * TB)
            dov = do_ref[ri, :].astype(F32)
            prod = dov * o_ref[ri, :].astype(F32)
            dob = dov.astype(BF16)
            deltas = [jnp.sum(jnp.where((lane < VDIM) if h == 0 else (lane >= VDIM), prod, 0.0), axis=-1, keepdims=True)
                      for h in range(2)]
            qhs = [q_ref[ri, sl] for sl in heads]
            lses = [lse_ref[ri, sl][:, :1] for sl in heads]
            for rows, diagonal in ((ri, True), (past, False)):
                if rows.stop == rows.start:
                    continue
                ps = [jnp.exp(_dot(qh, k_ref[rows, sl], ((1,), (1,))) * SCALE_B - lse) for qh, sl, lse in zip(qhs, heads, lses)]
                if diagonal:
                    ps = [jnp.where(causal, p, 0.0) for p in ps]
                dps = [_dot(dob, v_ref[rows, sl], ((1,), (1,))) for sl in heads]
                dss = [(p * (dp - delta) * SCALE_B).astype(BF16) for p, dp, delta in zip(ps, dps, deltas)]
                for sl, qh, p, ds in zip(heads, qhs, ps, dss):
                    dq = _dot(ds, k_ref[rows, sl], ((1,), (0,)))
                    dk = _dot(ds, qh, ((0,), (0,)))
                    dv = _dot(p.astype(BF16), dob, ((0,), (0,)))
                    if diagonal:
                        dq_ref[ri, sl] = dq
                    else:
                        dq_ref[ri, sl] += dq
                    if i == nblk - 1:
                        dk_ref[rows, sl] = dk
                        dv_ref[rows, sl] = dv
                    else:
                        dk_ref[rows, sl] += dk
                        dv_ref[rows, sl] += dv
        dq_out[...] = dq_ref[...].astype(BF16)
        dk_out[...] = dk_ref[...].astype(BF16)
        dv_out[...] = dv_ref[...].astype(BF16)

    wide2 = pl.BlockSpec((S, 2 * HP), lambda b, p: (b, p))
    pair = pl.BlockSpec((S, HP), lambda b, p: (b, p))
    return pl.pallas_call(
        body, name="mla_bwd", grid=(nb, H // 2),
        in_specs=[wide2, wide2, wide2, pair, pair, wide2],
        out_specs=[wide2, wide2, wide2],
        out_shape=[jax.ShapeDtypeStruct((t, H * HP), BF16)] * 3,
        scratch_shapes=[pltpu.VMEM((S, 2 * HP), F32)] * 3,
        compiler_params=_cp("parallel", "parallel"),
    )(q, k, v, o, do, lse)


def _t5_bucket(dist):
    max_exact = N_BUCKETS // 2
    d = np.maximum(dist, 1).astype(np.float64)
    large = max_exact + (np.log(d / max_exact) / np.log(MAX_DISTANCE / max_exact) * (N_BUCKETS - max_exact)).astype(np.int64)
    large = np.minimum(large, N_BUCKETS - 1)
    return np.where(dist < max_exact, dist, large).astype(np.int32)


def _band_geometry():
    a = np.arange(BLK)[:, None]
    bk = np.arange(2 * BLK)[None, :]
    steps = BLK + a - bk
    valid = (steps >= 0) & (steps <= BLK)
    buckets = np.stack([_t5_bucket(np.clip(steps, 0, BLK) * d) for d in DILATIONS])
    return buckets, valid


def _band_bias(rel_bias):
    buckets, valid = _band_geometry()
    onehot = (jnp.asarray(buckets)[..., None] == jnp.arange(N_BUCKETS)).astype(F32)
    bias = jnp.einsum("rqkn,nh->rhqk", onehot, rel_bias, precision=lax.Precision.HIGHEST)
    bias = jnp.where(jnp.asarray(valid)[None, None], bias, NEG)
    return bias.reshape(3, H // 2, 2 * BLK, 2 * BLK)


def _dil_items():
    items = []
    for r, d in enumerate(DILATIONS):
        for res in range(d):
            for blk in range(S // d // BLK):
                items.append((r, d, blk * BLK * d + res, blk > 0))
    return items


GROUP = 4


def _strided(start, d):
    return pl.ds(start, BLK) if d == 1 else pl.ds(start, BLK, stride=d)


def _stack_heads(tile, own):
    return jnp.where(own, jnp.concatenate([tile, tile], axis=0), 0.0).astype(BF16)


def _own_lanes():
    row = lax.broadcasted_iota(jnp.int32, (2 * BLK, HP), 0)
    lane = lax.broadcasted_iota(jnp.int32, (2 * BLK, HP), 1)
    return (lane < E_A) == (row < BLK)


def _dil_fwd(proj, biasm):
    t = proj.shape[0]
    nb = t // S

    def body(q_ref, k_ref, v_ref, b_ref, o_ref, lse_ref, ob_ref, lb_ref):
        lane = lax.broadcasted_iota(jnp.int32, (BLK, HP), 1)
        own = _own_lanes()
        items = _dil_items()
        for g in range(0, len(items), GROUP):
            grp = items[g:g + GROUP]
            ss, vts = [], []
            for r, d, start, has_prev in grp:
                cur = _strided(start, d)
                rows = [_strided(start - BLK * d, d), cur] if has_prev else [cur]
                q2 = _stack_heads(q_ref[cur, :] * SCALE_A, own)
                kt = jnp.concatenate([k_ref[x, :] for x in rows], axis=0).astype(BF16)
                vts.append(jnp.concatenate([v_ref[x, :] for x in rows], axis=0).astype(BF16))
                bias = b_ref[r, 0] if has_prev else b_ref[r, 0, :, BLK:]
                ss.append(_dot(q2, kt, ((1,), (1,))) + bias)
            ms = [jnp.max(s, axis=-1, keepdims=True) for s in ss]
            ps = [jnp.exp(s - m) for s, m in zip(ss, ms)]
            ls = [jnp.sum(p, axis=-1, keepdims=True) for p in ps]
            for (r, d, start, _), p, vt, m, l in zip(grp, ps, vts, ms, ls):
                cur = _strided(start, d)
                o2 = _dot(p.astype(BF16), vt, ((1,), (0,))) / l
                lse2 = m + jnp.log(l)
                ob_ref[r, cur, :] = jnp.where(lane < E_A, o2[:BLK], o2[BLK:])
                lb_ref[r, cur, :] = jnp.where(lane < E_A, lse2[:BLK], lse2[BLK:])

        def merge(c, _):
            rows = pl.ds(pl.multiple_of(c * TQ, TQ), TQ)
            l0, l1, l2 = lb_ref[0, rows, :], lb_ref[1, rows, :], lb_ref[2, rows, :]
            m = jnp.maximum(jnp.maximum(l0, l1), l2)
            e0, e1, e2 = jnp.exp(l0 - m), jnp.exp(l1 - m), jnp.exp(l2 - m)
            tot = e0 + e1 + e2
            o_ref[rows, :] = (e0 * ob_ref[0, rows, :] + e1 * ob_ref[1, rows, :] + e2 * ob_ref[2, rows, :]) / tot
            lse_ref[rows, :] = m + jnp.log(tot)
            return 0

        lax.fori_loop(0, S // TQ, merge, 0)

    npair = H // 2
    return pl.pallas_call(
        body, name="dil_fwd", grid=(nb, npair),
        in_specs=[pl.BlockSpec((S, HP), lambda b, p: (b, p)), pl.BlockSpec((S, HP), lambda b, p: (b, npair + p)),
                  pl.BlockSpec((S, HP), lambda b, p: (b, 2 * npair + p)),
                  pl.BlockSpec((3, 1, 2 * BLK, 2 * BLK), lambda b, p: (0, p, 0, 0))],
        out_specs=[pl.BlockSpec((S, HP), lambda b, p: (b, p))] * 2,
        out_shape=[jax.ShapeDtypeStruct((t, D_A), F32)] * 2,
        scratch_shapes=[pltpu.VMEM((3, S, HP), F32), pltpu.VMEM((3, S, HP), F32)],
        compiler_params=_cp("parallel", "parallel"),
    )(proj, proj, proj, biasm)


def _dil_bwd(proj, biasm, o, do, lse):
    t = proj.shape[0]
    nb = t // S

    def body(q_ref, k_ref, v_ref, b_ref, o_ref, do_ref, lse_ref, dq_out, dk_out, dv_out, ds_ref, dq_ref, dk_ref, dv_ref):
        dq_ref[...] = jnp.zeros_like(dq_ref)
        dk_ref[...] = jnp.zeros_like(dk_ref)
        dv_ref[...] = jnp.zeros_like(dv_ref)
        ds_ref[...] = jnp.zeros_like(ds_ref)
        lane = lax.broadcasted_iota(jnp.int32, (BLK, HP), 1)
        own = _own_lanes()
        items = _dil_items()
        for g in range(0, len(items), GROUP):
            grp = items[g:g + GROUP]
            q2s, kts, do2s, ss, dps, lse2s, delta2s = [], [], [], [], [], [], []
            for r, d, start, has_prev in grp:
                cur = _strided(start, d)
                rows = [_strided(start - BLK * d, d), cur] if has_prev else [cur]
                q2 = _stack_heads(q_ref[cur, :] * SCALE_A, own)
                kt = jnp.concatenate([k_ref[x, :] for x in rows], axis=0).astype(BF16)
                vt = jnp.concatenate([v_ref[x, :] for x in rows], axis=0).astype(BF16)
                dot_ = do_ref[cur, :]
                prod = dot_ * o_ref[cur, :]
                lset = lse_ref[cur, :]
                do2 = _stack_heads(dot_, own)
                bias = b_ref[r, 0] if has_prev else b_ref[r, 0, :, BLK:]
                ss.append(_dot(q2, kt, ((1,), (1,))) + bias)
                dps.append(_dot(do2, vt, ((1,), (1,))))
                lse2s.append(jnp.concatenate([lset[:, :1], lset[:, E_A:E_A + 1]], axis=0))
                delta2s.append(jnp.concatenate([jnp.sum(jnp.where(lane < E_A, prod, 0.0), axis=-1, keepdims=True),
                                                jnp.sum(jnp.where(lane >= E_A, prod, 0.0), axis=-1, keepdims=True)], axis=0))
                q2s.append(q2)
                kts.append(kt)
                do2s.append(do2)
            ps = [jnp.exp(s - lse2) for s, lse2 in zip(ss, lse2s)]
            dls = [p * (dp - delta2) for p, dp, delta2 in zip(ps, dps, delta2s)]
            for (r, d, start, has_prev), q2, kt, do2, p, dl in zip(grp, q2s, kts, do2s, ps, dls):
                cur = _strided(start, d)
                dsb = dl.astype(BF16)
                dq2 = _dot(dsb, kt, ((1,), (0,))) * SCALE_A
                dkt = _dot(dsb, q2, ((0,), (0,)))
                dvt = _dot(p.astype(BF16), do2, ((0,), (0,)))
                dq_ref[cur, :] += jnp.where(lane < E_A, dq2[:BLK], dq2[BLK:])
                if has_prev:
                    prev = _strided(start - BLK * d, d)
                    ds_ref[0, r, 0] += dl
                    dk_ref[prev, :] += dkt[:BLK]
                    dv_ref[prev, :] += dvt[:BLK]
                    dk_ref[cur, :] += dkt[BLK:]
                    dv_ref[cur, :] += dvt[BLK:]
                else:
                    ds_ref[0, r, 0, :, BLK:] += dl
                    dk_ref[cur, :] += dkt
                    dv_ref[cur, :] += dvt
        dq_out[...] = dq_ref[...].astype(BF16)
        dk_out[...] = dk_ref[...].astype(BF16)
        dv_out[...] = dv_ref[...].astype(BF16)

    npair = H // 2
    pair = pl.BlockSpec((S, HP), lambda b, p: (b, p))
    return pl.pallas_call(
        body, name="dil_bwd", grid=(nb, npair),
        in_specs=[pair, pl.BlockSpec((S, HP), lambda b, p: (b, npair + p)),
                  pl.BlockSpec((S, HP), lambda b, p: (b, 2 * npair + p)),
                  pl.BlockSpec((3, 1, 2 * BLK, 2 * BLK), lambda b, p: (0, p, 0, 0)), pair, pair, pair],
        out_specs=[pair, pair, pair, pl.BlockSpec((1, 3, 1, 2 * BLK, 2 * BLK), lambda b, p: (b, 0, p, 0, 0))],
        out_shape=[jax.ShapeDtypeStruct((t, D_A), BF16)] * 3 + [jax.ShapeDtypeStruct((nb, 3, npair, 2 * BLK, 2 * BLK), F32)],
        scratch_shapes=[pltpu.VMEM((S, HP), F32)] * 3,
        compiler_params=_cp("parallel", "parallel"),
    )(proj, proj, proj, biasm, o, do, lse)


def _rel_bias_grad(dlogits):
    nb = dlogits.shape[0]
    buckets, _ = _band_geometry()
    kk = 3 * BLK * 2 * BLK
    dl = jnp.transpose(dlogits.reshape(nb, 3, H, BLK, 2 * BLK), (0, 2, 1, 3, 4)).reshape(nb, H, kk)
    bk = jnp.asarray(buckets.reshape(1, kk))
    tk = kk // 4

    def body(dl_ref, bk_ref, o_ref):
        j = pl.program_id(0)
        onehot = (bk_ref[...] == lax.broadcasted_iota(jnp.int32, (N_BUCKETS, tk), 0)).astype(F32)
        tot = dl_ref[0]
        for b in range(1, nb):
            tot = tot + dl_ref[b]
        part = lax.dot_general(onehot, tot, ((((1,), (1,))), ((), ())), preferred_element_type=F32,
                               precision=lax.Precision.HIGHEST)
        _acc_first(j, o_ref, part)

    return pl.pallas_call(
        body, name="rel_bias_grad", grid=(kk // tk,),
        in_specs=[pl.BlockSpec((nb, H, tk), lambda j: (0, 0, j)), pl.BlockSpec((1, tk), lambda j: (0, j))],
        out_specs=pl.BlockSpec((N_BUCKETS, H), lambda j: (0, 0)),
        out_shape=jax.ShapeDtypeStruct((N_BUCKETS, H), F32),
        compiler_params=_cp("arbitrary"),
    )(dl, bk)


def _mesh_place():
    x, y, c = lax.axis_index("x"), lax.axis_index("y"), lax.axis_index("c")
    return x, y, c


def _peer(k):
    x, y, c = _mesh_place()
    px = 1 - x if k & 4 else x
    py = 1 - y if k & 2 else y
    pc = 1 - c if k & 1 else c
    return (px, py, pc), 4 * px + 2 * py + pc


ANY = pl.BlockSpec(memory_space=pl.ANY)


def _exchange(arrays, gathers, name, after=None):
    n_arr = len(arrays)

    def body(*refs):
        ins, outs = refs[:n_arr], refs[n_arr + 1:2 * n_arr + 1]
        send, recv, loc = refs[2 * n_arr + 1:]
        x, y, c = _mesh_place()
        me = 4 * x + 2 * y + c
        local = [pltpu.make_async_copy(ins[a] if gathers[a] else ins[a].at[me], outs[a].at[me], loc.at[a])
                 for a in range(n_arr)]
        remote = _peer_copies(ins, outs, send, recv, gathers)
        for cp in local:
            cp.start()
        for put, _ in remote:
            put.start()
        for cp in local:
            cp.wait()
        for put, got in remote:
            put.wait_send()
            got.wait_recv()

    return pl.pallas_call(
        body, name=name,
        in_specs=[ANY] * (n_arr + 1), out_specs=[ANY] * n_arr,
        out_shape=[jax.ShapeDtypeStruct(((N_DEV,) if g else ()) + a.shape, a.dtype) for a, g in zip(arrays, gathers)],
        scratch_shapes=[pltpu.SemaphoreType.DMA((n_arr * (N_DEV - 1),)), pltpu.SemaphoreType.DMA((n_arr * (N_DEV - 1),)),
                        pltpu.SemaphoreType.DMA((n_arr,))],
        compiler_params=pltpu.CompilerParams(has_side_effects=True),
    )(*arrays, arrays[0] if after is None else after)


def _gather_two_level(arrays, name):
    n_arr = len(arrays)
    per = N_DEV - 1

    def body(*refs):
        ins, outs = refs[:n_arr], refs[n_arr:2 * n_arr]
        send, recv, loc = refs[2 * n_arr:]
        x, y, c = _mesh_place()
        me, sibling = (x, y, c), (x, y, 1 - c)
        chips = [(1 - x, y), (x, 1 - y), (1 - x, 1 - y)]

        def block(a, place):
            px, py, pc = place
            return outs[a].at[4 * px + 2 * py + pc]

        def copy(a, k, place, to, src=None):
            dst = block(a, place)
            return pltpu.make_async_remote_copy(dst if src is None else src, dst, send.at[a * per + k], recv.at[a * per + k],
                                                device_id=to, device_id_type=pl.DeviceIdType.MESH)

        local = [pltpu.make_async_copy(ins[a], block(a, me), loc.at[a]) for a in range(n_arr)]
        for cp in local:
            cp.start()
        first = []
        for a in range(n_arr):
            first.append(copy(a, 0, me, sibling, src=ins[a]))
            first += [copy(a, 1 + j, me, (*chip, c), src=ins[a]) for j, chip in enumerate(chips)]
        for cp in first:
            cp.start()
        passed = []
        for j, chip in enumerate(chips):
            for a in range(n_arr):
                copy(a, 1 + j, (*chip, c), me).wait_recv()
                passed.append(copy(a, 4 + j, (*chip, c), sibling))
                passed[-1].start()
        for a in range(n_arr):
            copy(a, 0, sibling, me).wait_recv()
            for j, chip in enumerate(chips):
                copy(a, 4 + j, (*chip, 1 - c), me).wait_recv()
        for cp in first + passed:
            cp.wait_send()
        for cp in local:
            cp.wait()

    return pl.pallas_call(
        body, name=name,
        in_specs=[ANY] * n_arr, out_specs=[ANY] * n_arr,
        out_shape=[jax.ShapeDtypeStruct((N_DEV,) + a.shape, a.dtype) for a in arrays],
        scratch_shapes=[pltpu.SemaphoreType.DMA((n_arr * per,)), pltpu.SemaphoreType.DMA((n_arr * per,)),
                        pltpu.SemaphoreType.DMA((n_arr,))],
        compiler_params=pltpu.CompilerParams(has_side_effects=True),
    )(*arrays)


HBM = pl.BlockSpec(memory_space=pltpu.HBM)
SEM = pl.BlockSpec(memory_space=pltpu.SEMAPHORE)
DATAFLOW = pltpu.SideEffectType.DATAFLOW_SIDE_EFFECTING


def _own_block_in_place(block, me):
    land = lax.empty((N_DEV,) + block.shape, block.dtype)
    return lax.dynamic_update_slice(land, block[None], (me,) + (0,) * block.ndim)


def _peer_copies(srcs, lands, send, recv, gathers):
    x, y, c = _mesh_place()
    me = 4 * x + 2 * y + c
    out = []
    for a, (src, land) in enumerate(zip(srcs, lands)):
        for k in range(1, N_DEV):
            dev, idx = _peer(k)
            sem = a * (N_DEV - 1) + k - 1
            mine = src if gathers[a] else src.at[idx]
            put = pltpu.make_async_remote_copy(mine, land.at[me], send.at[sem], recv.at[sem],
                                               device_id=dev, device_id_type=pl.DeviceIdType.MESH)
            got = pltpu.make_async_remote_copy(mine, land.at[idx], send.at[sem], recv.at[sem],
                                               device_id=dev, device_id_type=pl.DeviceIdType.MESH)
            out.append((put, got))
    return out


def _exchange_start(srcs, lands, gather, after, name):
    n = len(srcs)
    extra = [] if after is None else [after]

    def body(*refs):
        srcs_, lands_ = refs[:n], refs[n:2 * n]
        send, recv = refs[2 * n + len(extra)], refs[2 * n + len(extra) + 1]
        for put, _ in _peer_copies(srcs_, lands_, send, recv, gather):
            put.start()
        refs[-1][...] = jnp.zeros_like(refs[-1])

    nsem = n * (N_DEV - 1)
    thru = [pltpu.HBM(a.shape, a.dtype) for a in list(srcs) + list(lands)]
    res = pl.pallas_call(
        body, name=name,
        out_shape=(pltpu.SemaphoreType.DMA((nsem,)), pltpu.SemaphoreType.DMA((nsem,)), *thru, jax.ShapeDtypeStruct((8, 128), F32)),
        in_specs=[HBM] * (2 * n) + [ANY] * len(extra),
        out_specs=(SEM, SEM, *([HBM] * (2 * n)), pl.BlockSpec(memory_space=pltpu.VMEM)),
        input_output_aliases={i: 2 + i for i in range(2 * n)},
        compiler_params=pltpu.CompilerParams(has_side_effects=DATAFLOW),
    )(*[pltpu.with_memory_space_constraint(a, pltpu.HBM) for a in list(srcs) + list(lands)], *extra)
    return res[0], res[1], list(res[2:2 + n]), list(res[2 + n:2 + 2 * n]), res[-1]


def _exchange_wait(send, recv, srcs, lands, gather, after, name):
    n = len(srcs)

    def body(*refs):
        srcs_, lands_, send_, recv_ = refs[:n], refs[n:2 * n], refs[2 * n], refs[2 * n + 1]
        for put, got in _peer_copies(srcs_, lands_, send_, recv_, gather):
            put.wait_send()
            got.wait_recv()

    thru = [pltpu.HBM(a.shape, a.dtype) for a in list(srcs) + list(lands)]
    res = pl.pallas_call(
        body, name=name, out_shape=tuple(thru),
        in_specs=[HBM] * (2 * n) + [SEM, SEM, ANY], out_specs=tuple([HBM] * (2 * n)),
        input_output_aliases={i: i for i in range(2 * n)},
        compiler_params=pltpu.CompilerParams(has_side_effects=DATAFLOW),
    )(*srcs, *lands, send, recv, after)
    return list(res[n:])


def _silu_rows(c):
    def body(c_ref, o_ref):
        v = c_ref[...]
        o_ref[...] = v * _sigmoid(v)

    return pl.pallas_call(body, name="cond", out_shape=jax.ShapeDtypeStruct(c.shape, F32))(c)


def _mod_slab(cond_all, w_ada, b_slab):
    def body(c_ref, w_ref, b_ref, o_ref):
        o_ref[...] = _dot(c_ref[...].astype(BF16), w_ref[0].astype(BF16), ((1,), (0,))) + b_ref[...]

    return pl.pallas_call(body, name="mod_slab",
                          out_shape=jax.ShapeDtypeStruct((cond_all.shape[0], w_ada.shape[2]), F32),
                          compiler_params=pltpu.CompilerParams(vmem_limit_bytes=VMEM_LIMIT))(cond_all, w_ada, b_slab)


def _ada_grad(cond_all, dmod_cols):
    def body(c_ref, d_ref, o_ref):
        o_ref[...] = _dot(c_ref[...].astype(BF16), d_ref[...].astype(BF16), ((0,), (0,)))

    return pl.pallas_call(body, name="ada_grad",
                          out_shape=jax.ShapeDtypeStruct((cond_all.shape[1], dmod_cols.shape[1]), F32),
                          compiler_params=pltpu.CompilerParams(vmem_limit_bytes=VMEM_LIMIT))(cond_all, dmod_cols)


def _adam_math(g, w, m, v):
    m2 = B1 * m + (1.0 - B1) * g
    v2 = B2 * v + (1.0 - B2) * (g * g)
    m_hat = m2 / (1.0 - B1 ** STEP)
    v_hat = v2 / (1.0 - B2 ** STEP)
    return -LR * (m_hat / (jnp.sqrt(v_hat) + ADAM_EPS) + WD * w), m2, v2


def _adamw(parts, w, m, v, name):
    n, rows, cols = parts.shape
    tr = max([p for p in range(16, 513, 16) if rows % p == 0] or [rows])

    def body(p_ref, w_ref, m_ref, v_ref, g_ref, d_ref, m2_ref, v2_ref):
        g = p_ref[0].astype(F32)
        for s in range(1, n):
            g = g + p_ref[s].astype(F32)
        g_ref[0] = g
        d_ref[0], m2_ref[0], v2_ref[0] = _adam_math(g, w_ref[0], m_ref[0], v_ref[0])

    blk = pl.BlockSpec((1, tr, cols), lambda i: (0, i, 0))
    return pl.pallas_call(
        body, name=name, grid=(rows // tr,),
        in_specs=[pl.BlockSpec((n, tr, cols), lambda i: (0, i, 0)), blk, blk, blk],
        out_specs=[blk] * 4, out_shape=[jax.ShapeDtypeStruct((1, rows, cols), F32)] * 4,
        compiler_params=_cp("parallel"),
    )(parts, w, m, v)


ROW_PARAMS = (("g_norm1", D), ("g_cq", Q_LORA), ("g_ckv", KV_LORA), ("g_out_a", D_A), ("g_out_b", D_A), ("g_norm2", D),
              ("g_final", D))
LOSS_ROW = N_MOD + len(ROW_PARAMS)
PAY_ROWS = 16
NCOL = N_MOD * D // N_DEV


def _pack_small(dmods, rows, loss_cols):
    nb = dmods[0].shape[0]
    nrow = len(ROW_PARAMS)

    def body(*refs):
        dm, rw, loss_ref, pay_ref, blk_ref = refs[:N_MOD], refs[N_MOD:N_MOD + nrow], refs[N_MOD + nrow], refs[-2], refs[-1]
        pay_ref[...] = jnp.zeros_like(pay_ref)
        for k in range(N_MOD):
            tot = dm[k][0]
            for b in range(1, nb):
                tot = tot + dm[k][b]
            pay_ref[k:k + 1, :] = tot
        for i, (_, n) in enumerate(ROW_PARAMS):
            pay_ref[N_MOD + i:N_MOD + i + 1, :n] = rw[i][...]
        pay_ref[LOSS_ROW:LOSS_ROW + 1, :] = loss_ref[...]
        for j in range(N_DEV):
            done = 0
            while done < NCOL:
                seg, off = divmod(j * NCOL + done, D)
                ln = min(NCOL - done, D - off)
                for b in range(nb):
                    blk_ref[j, b:b + 1, done:done + ln] = dm[seg][b][:, off:off + ln]
                done += ln

    return pl.pallas_call(
        body, name="pack_small",
        out_shape=[jax.ShapeDtypeStruct((PAY_ROWS, D), F32), jax.ShapeDtypeStruct((N_DEV, nb, NCOL), F32)],
    )(*dmods, *rows, loss_cols)


def _small_update(pay, rel, ws, ms, vs):
    n_par = len(ws)

    def body(*refs):
        pay_ref, rel_ref = refs[:2]
        w_refs, m_refs, v_refs = (refs[2 + s * n_par:2 + (s + 1) * n_par] for s in range(3))
        outs, loss_ref = refs[2 + 3 * n_par:-1], refs[-1]
        tot, rtot = pay_ref[0], rel_ref[0]
        for s in range(1, N_DEV):
            tot, rtot = tot + pay_ref[s], rtot + rel_ref[s]

        def update(p, g, sl):
            outs[4 * p][:, sl] = g
            outs[4 * p + 1][:, sl], outs[4 * p + 2][:, sl], outs[4 * p + 3][:, sl] = _adam_math(
                g, w_refs[p][:, sl], m_refs[p][:, sl], v_refs[p][:, sl])

        for k in range(N_MOD):
            update(0, tot[k:k + 1, :], slice(k * D, (k + 1) * D))
        for i, (_, n) in enumerate(ROW_PARAMS):
            update(1 + i, tot[N_MOD + i:N_MOD + i + 1, :n], slice(0, n))
        update(n_par - 1, rtot, slice(0, H))
        loss_ref[...] = jnp.broadcast_to((0.5 / D) * jnp.sum(tot[LOSS_ROW:LOSS_ROW + 1, :]), loss_ref.shape)

    shapes = [jax.ShapeDtypeStruct(w.shape, F32) for w in ws for _ in range(4)]
    res = pl.pallas_call(
        body, name="small_update", out_shape=shapes + [jax.ShapeDtypeStruct((8, 128), F32)],
    )(pay, rel, *ws, *ms, *vs)
    return [tuple(res[4 * p:4 * p + 4]) for p in range(n_par)], res[-1]


def _cols_from_blocks(g):
    return jnp.transpose(g, (1, 0, 2)).reshape(g.shape[1], N_DEV * g.shape[2])


def _cols_to_blocks(w):
    r, c = w.shape
    return jnp.transpose(w.reshape(r, N_DEV, c // N_DEV), (1, 0, 2))


def _pad_w_in(wt):
    z = jnp.zeros((NOPE, wt.shape[1]), wt.dtype)
    return jnp.concatenate([wt[:P_IN - ROPE], z, wt[P_IN - ROPE:], z[:HP - NOPE - ROPE]], axis=0)


def _unpad_w_in(gt):
    k0 = P_IN - ROPE + NOPE
    return jnp.concatenate([gt[:P_IN - ROPE], gt[k0:k0 + ROPE]], axis=0)


def _pad_w_uq(wt):
    return jnp.pad(wt, ((0, 0), (0, HP - NOPE - ROPE), (0, 0))).reshape(H * HP, Q_LORA)


def _unpad_w_uq(gt):
    return gt.reshape(H, HP, Q_LORA)[:, :NOPE + ROPE]


def _split_w_ukv(w):
    w4 = w.reshape(KV_LORA, H // 2, 2, HP)
    z = jnp.zeros((KV_LORA, H // 2, NOPE), w.dtype)
    kn, vv = w4[..., :NOPE], w4[..., NOPE:]
    w_k = jnp.stack([jnp.concatenate([kn[:, :, 0], z], -1), jnp.concatenate([kn[:, :, 1], z], -1)], axis=2)
    w_v = jnp.stack([jnp.concatenate([vv[:, :, 0], z], -1), jnp.concatenate([z, vv[:, :, 1]], -1)], axis=2)
    return w_k.reshape(KV_LORA, H * HP), w_v.reshape(KV_LORA, H * HP)


def _join_w_ukv(g_k, g_v):
    gk = g_k.reshape(KV_LORA, H // 2, 2, HP)
    gv = g_v.reshape(KV_LORA, H // 2, 2, HP)
    even = jnp.concatenate([gk[:, :, 0, :NOPE], gv[:, :, 0, :VDIM]], -1)
    odd = jnp.concatenate([gk[:, :, 1, :NOPE], gv[:, :, 1, VDIM:]], -1)
    return jnp.stack([even, odd], axis=2).reshape(KV_LORA, H * HP)


def _rope_tables():
    half = ROPE // 2
    inv = np.float32(ROPE_THETA) ** (-np.arange(half, dtype=np.float32) / np.float32(half))
    ang = np.arange(S, dtype=np.float32)[:, None] * inv[None, :].astype(np.float32)
    cos, sin = np.cos(ang).astype(np.float32), np.sin(ang).astype(np.float32)
    ones, zeros = np.ones((S, NOPE), np.float32), np.zeros((S, NOPE), np.float32)
    tail1, tail0 = np.ones((S, HP - NOPE - ROPE), np.float32), np.zeros((S, HP - NOPE - ROPE), np.float32)
    zh = np.zeros((S, half), np.float32)
    c = np.concatenate([ones, cos, cos, tail1], axis=1)
    sm = np.concatenate([zeros, -sin, zh, tail0], axis=1)
    sp = np.concatenate([zeros, zh, sin, tail0], axis=1)
    return jnp.asarray(c), jnp.asarray(sm), jnp.asarray(sp)


def _local_step(x, mod, target, g_norm1, w_in_p, g_cq, w_uq_p, g_ckv, w_k, w_v, rel_bias, g_out_a, g_out_b, w_out,
                g_norm2, w_ffn_in, w_ffn_out, g_final, late_weights=None, on_ffn_grads=None, on_last_grads=None):
    nb = x.shape[0] // S
    sh1, sc1, g1, sh2, sc2, g2 = (mod[:, n].reshape(nb, 1, D) for n in range(N_MOD))
    rc, rsm, rsp = _rope_tables()
    biasm = _band_bias(rel_bias)

    h1 = _pre1(x, g_norm1, sc1, sh1)
    proj = _mm_nt(h1, w_in_p, F32, "proj")
    q, k, v, cqn, ckvn = _mla_pre(proj, g_cq, g_ckv, w_uq_p, w_k, w_v, rc, rsm, rsp)
    out_b, lse_b = _mla_fwd(q, k, v)
    out_a, lse_a = _dil_fwd(proj, biasm)
    y = _post_attn(out_a, out_b, g_out_a, g_out_b)
    if late_weights is not None:
        w_out, w_ffn_in, w_ffn_out = late_weights(y)
    mix = _mm_nn(y, w_out, BF16, "mix")
    x2, h2 = _resid_norm2(x, mix, g1, g_norm2, sc2, sh2)
    ffn_g, ffn_u, act = _ffn_in(h2, w_ffn_in)
    f = _mm_nn(act, w_ffn_out, BF16, "ffn_out")
    dx3, df, loss_cols, dg_final, dg2 = _final(x2, f, g2, g_final, target)

    dg_, du_ = _d_act(df, w_ffn_out, ffn_g, ffn_u)
    gw_ffn_out = _mm_tn_rows([act], df, "gw_ffn_out")
    dh2 = _d_h2(dg_, du_, w_ffn_in)
    gw_ffn_in = _mm_tn_rows([dg_, du_], h2, "gw_ffn_in")
    dx2, dsh2, dsc2, dg_norm2, dg1, dmix = _norm_bwd(x2, dh2, dx3, g_norm2, sc2, gate=(mix, g1))
    dy = _mm_nt(dmix, w_out, BF16, "d_y")
    gw_out = _mm_tn(y, [dmix], "gw_out")
    if on_ffn_grads is not None:
        g_out_a = g_out_a + on_ffn_grads(gw_ffn_in, gw_ffn_out, gw_out)
    dout_a, dout_b, dg_out_a, dg_out_b = _post_attn_bwd(dy, out_a, out_b, g_out_a, g_out_b)
    dq_b, dk_b, dv_b = _mla_bwd(q, k, v, out_b, dout_b, lse_b)
    dq_a, dk_a, dv_a, dlogits = _dil_bwd(proj, biasm, out_a, dout_a, lse_a)
    g_rel = _rel_bias_grad(dlogits)
    dqr, dproj, dg_cq, dg_ckv = _mla_pre_bwd(proj, dq_b, dk_b, dv_b, (dq_a, dk_a, dv_a), g_cq, g_ckv, w_uq_p, w_k, w_v,
                                             rc, rsm, rsp)
    gw_uq = _mm_tn(dqr, [cqn], "gw_uq")
    gw_k, gw_v = _mm_tn(ckvn, [dk_b, dv_b], "gw_kv")
    gw_in = _mm_tn_rows([dproj], h1, "gw_in")
    if on_last_grads is not None:
        started = on_last_grads(dict(w_in=gw_in, w_uq=gw_uq, w_k=gw_k, w_v=gw_v))
    else:
        started = None
    dh1 = _mm_nn(dproj, w_in_p, BF16, "d_h1", after=started)
    grad_x, dsh1, dsc1, dg_norm1 = _norm_bwd(x, dh1, dx2, g_norm1, sc1)

    dmod = [dsh1, dsc1, dg1, dsh2, dsc2, dg2]
    small = dict(g_norm1=dg_norm1, g_cq=dg_cq, g_ckv=dg_ckv, rel_bias=g_rel, g_out_a=dg_out_a, g_out_b=dg_out_b,
                 g_norm2=dg_norm2, g_final=dg_final)
    big = dict(w_in=gw_in, w_uq=gw_uq, w_k=gw_k, w_v=gw_v, w_out=gw_out, w_ffn_in=gw_ffn_in, w_ffn_out=gw_ffn_out)
    return grad_x, dmod, loss_cols, small, big


def kernel(x, c, w_ada, b_ada, g_norm1, w_in, g_cq, w_uq, g_ckv, w_ukv, rel_bias, g_out_a, g_out_b, w_out, g_norm2, w_ffn_in, w_ffn_out, g_final, loss_target, m_w_ada, m_b_ada, m_g_norm1, m_w_in, m_g_cq, m_w_uq, m_g_ckv, m_w_ukv, m_rel_bias, m_g_out_a, m_g_out_b, m_w_out, m_g_norm2, m_w_ffn_in, m_w_ffn_out, m_g_final, v_w_ada, v_b_ada, v_g_norm1, v_w_in, v_g_cq, v_w_uq, v_g_ckv, v_w_ukv, v_rel_bias, v_g_out_a, v_g_out_b, v_w_out, v_g_norm2, v_w_ffn_in, v_w_ffn_out, v_g_final):
    nb = x.shape[0]
    t = nb * S
    xt, tt = x.reshape(t, D), loss_target.reshape(t, D)
    me = 4 * lax.axis_index("x") + 2 * lax.axis_index("y") + lax.axis_index("c")

    early = [jnp.swapaxes(w_in, 1, 2)[0], jnp.swapaxes(w_uq, 1, 2)[0], w_ukv[0]]
    gathered = _gather_two_level([_silu_rows(c)] + [s.astype(BF16) for s in early], "gather_weights")
    cond_all = gathered[0].reshape(N_DEV * nb, D)
    w_in_t = gathered[1].reshape(P_IN, D)
    w_ukv_f = _cols_from_blocks(gathered[3])
    w_k, w_v = _split_w_ukv(w_ukv_f)

    ncol = N_MOD * D // N_DEV
    b_slab = lax.dynamic_slice(b_ada, (0, me * ncol), (1, ncol))
    slab = _mod_slab(cond_all, w_ada, b_slab)
    (mod_rows,) = _exchange([slab.reshape(N_DEV, nb, ncol)], [False], "scatter_mod")
    mod = jnp.transpose(mod_rows, (1, 0, 2)).reshape(nb, N_MOD, D)

    late = [s.astype(BF16) for s in (w_out[0], jnp.swapaxes(w_ffn_in, 1, 2)[0], w_ffn_out[0])]
    late_send, late_recv, late_src, late_land, late_token = _exchange_start(
        late, [_own_block_in_place(s, me) for s in late], [True] * 3, mod_rows, "gather_late_start")
    g_norm1_t = g_norm1 + late_token[:1, :1]

    def late_weights(after):
        w_out_g, w_ffn_in_g, w_ffn_out_g = _exchange_wait(late_send, late_recv, late_src, late_land, [True] * 3, after,
                                                          "gather_late_wait")
        return w_out_g.reshape(D, D), w_ffn_in_g.reshape(2 * D_FF, D), w_ffn_out_g.reshape(D_FF, D)

    flight = {}

    def start_grads(key, src, name):
        land = [_own_block_in_place(lax.dynamic_index_in_dim(s, me, 0, keepdims=False), me) for s in src]
        send, recv, src, land, token = _exchange_start(src, land, [False] * len(src), None, name)
        flight[key] = (send, recv, src, land)
        return token[:1, :1]

    def on_ffn_grads(gw_ffn_in, gw_ffn_out, gw_out):
        return start_grads("ffn", [gw_ffn_in.reshape(N_DEV, 2 * D_FF // N_DEV, D), gw_ffn_out.reshape(N_DEV, D_FF // N_DEV, D),
                                   gw_out.reshape(N_DEV, D // N_DEV, D)], "exchange_ffn_start")

    def on_last_grads(gw):
        return start_grads("rest", [_unpad_w_in(gw["w_in"]).reshape(N_DEV, P_IN // N_DEV, D),
                                    _unpad_w_uq(gw["w_uq"]),
                                    _cols_to_blocks(_join_w_ukv(gw["w_k"], gw["w_v"]))], "exchange_rest_start")

    grad_x, dmod, loss_cols, small, _ = _local_step(
        xt, mod, tt, g_norm1_t, _pad_w_in(w_in_t), g_cq, _pad_w_uq(gathered[2]), g_ckv, w_k, w_v, rel_bias, g_out_a, g_out_b,
        None, g_norm2, None, None, g_final.reshape(1, D), late_weights=late_weights, on_ffn_grads=on_ffn_grads,
        on_last_grads=on_last_grads)

    mine, dmod_blocks = _pack_small(dmod, [small[n] for n, _ in ROW_PARAMS], loss_cols)
    small_src = [dmod_blocks, mine, small["rel_bias"]]
    small_gather = [False, True, True]
    small_land = [_own_block_in_place(lax.dynamic_index_in_dim(dmod_blocks, me, 0, keepdims=False), me),
                  _own_block_in_place(mine, me), _own_block_in_place(small["rel_bias"], me)]
    small_send, small_recv, small_src, small_land, _ = _exchange_start(small_src, small_land, small_gather, None,
                                                                       "exchange_small_start")

    upd = {}

    def land_and_update(key, names, after, name):
        got = _exchange_wait(*flight[key], [False] * len(names), after, name)
        for n, p in zip(names, got):
            w, m, v = big[n]
            upd[n] = _adamw(p, w, m, v, "adamw_" + n)

    def flip(a):
        return jnp.swapaxes(a, 1, 2)

    big = dict(w_in=(flip(w_in), flip(m_w_in), flip(v_w_in)), w_uq=(flip(w_uq), flip(m_w_uq), flip(v_w_uq)),
               w_ukv=(w_ukv, m_w_ukv, v_w_ukv),
               w_out=(w_out, m_w_out, v_w_out), w_ffn_in=(flip(w_ffn_in), flip(m_w_ffn_in), flip(v_w_ffn_in)),
               w_ffn_out=(w_ffn_out, m_w_ffn_out, v_w_ffn_out))
    land_and_update("ffn", ["w_ffn_in", "w_ffn_out", "w_out"], grad_x, "exchange_ffn_wait")
    land_and_update("rest", ["w_in", "w_uq", "w_ukv"], upd["w_out"][0], "exchange_rest_wait")
    for n in ("w_in", "w_uq", "w_ffn_in"):
        upd[n] = tuple(flip(a) for a in upd[n])

    dmod_cols, pay, rel = _exchange_wait(small_send, small_recv, small_src, small_land, small_gather, upd["w_ukv"][0],
                                         "exchange_small_wait")
    g_ada = _ada_grad(cond_all, dmod_cols.reshape(N_DEV * nb, ncol))
    upd["w_ada"] = _adamw(g_ada[None], w_ada, m_w_ada, v_w_ada, "adamw_w_ada")
    row = lambda a: a.reshape(1, D)
    small_names = ["b_ada"] + [n for n, _ in ROW_PARAMS] + ["rel_bias"]
    small_w = [b_ada, g_norm1, g_cq, g_ckv, g_out_a, g_out_b, g_norm2, row(g_final), rel_bias]
    small_m = [m_b_ada, m_g_norm1, m_g_cq, m_g_ckv, m_g_out_a, m_g_out_b, m_g_norm2, row(m_g_final), m_rel_bias]
    small_v = [v_b_ada, v_g_norm1, v_g_cq, v_g_ckv, v_g_out_a, v_g_out_b, v_g_norm2, row(v_g_final), v_rel_bias]
    small_upd, loss8 = _small_update(pay, rel, small_w, small_m, small_v)
    upd.update(zip(small_names, small_upd))

    order = ["w_ada", "b_ada", "g_norm1", "w_in", "g_cq", "w_uq", "g_ckv", "w_ukv", "rel_bias", "g_out_a", "g_out_b",
             "w_out", "g_norm2", "w_ffn_in", "w_ffn_out", "g_final"]
    like = dict(g_final=g_final)
    outs = [loss8[0, 0], grad_x.reshape(x.shape)]
    for part in range(4):
        for n in order:
            val = upd[n][part]
            outs.append(val.reshape(like[n].shape) if n in like else val)
    return tuple(outs)
```

```python
import numpy as np
import jax
import jax.numpy as jnp
from jax import lax
from jax.experimental import pallas as pl
from jax.experimental.pallas import tpu as pltpu

F32, BF16 = jnp.float32, jnp.bfloat16

N_DEV = 8
D = 1024
S = 2048
H = 8
E_A = 64
D_A = H * E_A
Q_LORA, KV_LORA = 384, 256
NOPE, ROPE, VDIM = 64, 32, 64
HP = 128
P_IN = 3 * D_A + Q_LORA + KV_LORA + ROPE
P_PAD = 3 * D_A + Q_LORA + KV_LORA + HP
TAIL0 = 3 * D_A
TAIL = P_PAD - TAIL0
D_FF = 2816
N_MOD = 6
EPS = 1e-6
NEG = -1e30
BLK = 128
DILATIONS = (1, 4, 16)
N_BUCKETS, MAX_DISTANCE = 32, 2048
ROPE_THETA = 10000.0
SCALE_A = E_A ** -0.5
SCALE_B = (NOPE + ROPE) ** -0.5
B1, B2, LR, ADAM_EPS, WD, STEP = 0.9, 0.999, 0.001, 1e-8, 0.01, 10
VMEM_LIMIT = 56 * 1024 * 1024


def _cp(*sem):
    return pltpu.CompilerParams(dimension_semantics=sem, vmem_limit_bytes=VMEM_LIMIT)


def _pick(n, prefs):
    for p in prefs:
        if n % p == 0:
            return p
    raise ValueError(f"no tile of {prefs} divides {n}")


OPERAND_BYTES = 6 * 1024 * 1024


def _pick_rows(m, k):
    return _pick(m, [p for p in (1024, 512, 256, 128, 16) if p * k * 2 <= OPERAND_BYTES])


MATMUL_BYTES = 40 * 1024 * 1024


def _stream_rows(m, fixed, per_row):
    return _pick(m, [p for p in (4096, 2048, 1024, 512, 256, 128, 16) if fixed + p * per_row <= MATMUL_BYTES])


def _dot(a, b, dims):
    return lax.dot_general(a, b, (dims, ((), ())), preferred_element_type=F32)


def _mm_nn(a, b, out_dtype, name, after=None):
    m, k = a.shape
    n = b.shape[1]
    tn = _pick(n, (512, 256, 384, 128))
    tm = _stream_rows(m, 4 * k * tn, 4 * k + (2 * jnp.dtype(out_dtype).itemsize + 4) * tn)

    def body(a_ref, b_ref, *rest):
        o_ref = rest[-1]
        o_ref[...] = _dot(a_ref[...], b_ref[...], ((1,), (0,))).astype(o_ref.dtype)

    extra = [] if after is None else [after]
    return pl.pallas_call(
        body, name=name, grid=(m // tm, n // tn),
        in_specs=[pl.BlockSpec((tm, k), lambda i, j: (i, 0)), pl.BlockSpec((k, tn), lambda i, j: (0, j))] + [ANY] * len(extra),
        out_specs=pl.BlockSpec((tm, tn), lambda i, j: (i, j)),
        out_shape=jax.ShapeDtypeStruct((m, n), out_dtype),
        compiler_params=_cp("parallel", "parallel"),
    )(a, b, *extra)


def _mm_nt(a, b, out_dtype, name, after=None):
    m, k = a.shape
    n = b.shape[0]
    tn = _pick(n, (512, 256, 384, 128))
    tm = _stream_rows(m, 4 * k * tn, 4 * k + (2 * jnp.dtype(out_dtype).itemsize + 4) * tn)

    def body(a_ref, b_ref, *rest):
        o_ref = rest[-1]
        o_ref[...] = _dot(a_ref[...], b_ref[...], ((1,), (1,))).astype(o_ref.dtype)

    extra = [] if after is None else [after]
    return pl.pallas_call(
        body, name=name, grid=(m // tm, n // tn),
        in_specs=[pl.BlockSpec((tm, k), lambda i, j: (i, 0)), pl.BlockSpec((tn, k), lambda i, j: (j, 0))] + [ANY] * len(extra),
        out_specs=pl.BlockSpec((tm, tn), lambda i, j: (i, j)),
        out_shape=jax.ShapeDtypeStruct((m, n), out_dtype),
        compiler_params=_cp("parallel", "parallel"),
    )(a, b, *extra)


def _mm_tn(a, bs, name):
    t, m = a.shape
    n = bs[0].shape[1]
    nb_ = len(bs)
    tc = _pick(t, (512, 16))
    tn = _pick(n, (512, 384, 256, 128))
    tm = _pick(m, [p for p in (1024, 512, 384, 256, 128) if (3 * p + 2 * nb_ * tn) * t * 2 <= VMEM_LIMIT - 2 * OPERAND_BYTES])
    if tm <= 256 and nb_ * n * t * 2 <= 2 * OPERAND_BYTES:
        tn = n

    def body(*refs):
        a_ref, b_refs, o_refs, at_ref = refs[0], refs[1:1 + nb_], refs[1 + nb_:1 + 2 * nb_], refs[-1]

        @pl.when(pl.program_id(1) == 0)
        def _():
            def chunk(c, _):
                rows = pl.ds(pl.multiple_of(c * tc, tc), tc)
                at_ref[:, rows] = a_ref[rows, :].T
                return 0

            lax.fori_loop(0, t // tc, chunk, 0)

        for b_ref, o_ref in zip(b_refs, o_refs):
            o_ref[...] = _dot(at_ref[...], b_ref[...], ((1,), (0,))).astype(BF16)

    res = pl.pallas_call(
        body, name=name, grid=(m // tm, n // tn),
        in_specs=[pl.BlockSpec((t, tm), lambda i, j: (0, i))] + [pl.BlockSpec((t, tn), lambda i, j: (0, j))] * nb_,
        out_specs=[pl.BlockSpec((tm, tn), lambda i, j: (i, j))] * nb_,
        out_shape=[jax.ShapeDtypeStruct((m, n), BF16)] * nb_,
        scratch_shapes=[pltpu.VMEM((tm, t), BF16)],
        compiler_params=_cp("parallel", "arbitrary"),
    )(a, *bs)
    return res[0] if nb_ == 1 else res


def _mm_tn_rows(a_list, b, name):
    t, m = a_list[0].shape
    n = b.shape[1]
    na = len(a_list)
    tc, tm = _pick(t, (512, 16)), _pick(m, (256, 128))
    nblk = m // tm

    def body(*refs):
        a_refs, b_ref, o_ref, bt_ref, r_ref = refs[:na], refs[na], refs[na + 1], refs[na + 2], refs[na + 3]
        i = pl.program_id(0)

        @pl.when(i == 0)
        def _():
            def chunk(c, _):
                rows = pl.ds(pl.multiple_of(c * tc, tc), tc)
                bt_ref[:, rows] = b_ref[rows, :].T
                return 0

            lax.fori_loop(0, t // tc, chunk, 0)

        for s, a_ref in enumerate(a_refs):
            @pl.when((i >= s * nblk) & (i < (s + 1) * nblk))
            def _(a_ref=a_ref):
                r_ref[...] = _dot(bt_ref[...], a_ref[...], ((1,), (0,)))
                o_ref[...] = r_ref[...].T.astype(BF16)

    return pl.pallas_call(
        body, name=name, grid=(na * nblk,),
        in_specs=[pl.BlockSpec((t, tm), lambda i, s=s: (0, jnp.clip(i - s * nblk, 0, nblk - 1))) for s in range(na)]
        + [pl.BlockSpec((t, n), lambda i: (0, 0))],
        out_specs=pl.BlockSpec((tm, n), lambda i: (i, 0)),
        out_shape=jax.ShapeDtypeStruct((na * m, n), BF16),
        scratch_shapes=[pltpu.VMEM((n, t), BF16), pltpu.VMEM((n, tm), F32)],
        compiler_params=_cp("arbitrary"),
    )(*a_list, b)


EPI = 256


def _silu_parts(g):
    sg = 0.5 * jnp.tanh(0.5 * g) + 0.5
    return sg, g * sg


def _ffn_in(h2, wt):
    t, k = h2.shape
    tn = _pick(D_FF, (256, 128))
    tm = _stream_rows(t, 8 * k * tn, 4 * k + (3 * 2 * 2 + 2 * 4) * tn)
    nj = D_FF // tn

    def body(h_ref, wg_ref, wu_ref, g_ref, u_ref, a_ref):
        hv = h_ref[...]
        g_all = _dot(hv, wg_ref[...], ((1,), (1,)))
        u_all = _dot(hv, wu_ref[...], ((1,), (1,)))
        for r in range(tm // EPI):
            rows = slice(r * EPI, (r + 1) * EPI)
            g, u = g_all[rows], u_all[rows]
            g_ref[rows, :] = g.astype(BF16)
            u_ref[rows, :] = u.astype(BF16)
            a_ref[rows, :] = (_silu_parts(g)[1] * u).astype(BF16)

    blk = pl.BlockSpec((tm, tn), lambda i, j: (i, j))
    return pl.pallas_call(
        body, name="ffn_in", grid=(t // tm, nj),
        in_specs=[pl.BlockSpec((tm, k), lambda i, j: (i, 0)), pl.BlockSpec((tn, k), lambda i, j: (j, 0)),
                  pl.BlockSpec((tn, k), lambda i, j: (j + nj, 0))],
        out_specs=[blk] * 3, out_shape=[jax.ShapeDtypeStruct((t, D_FF), BF16)] * 3,
        compiler_params=_cp("parallel", "parallel"),
    )(h2, wt, wt)


def _d_act(df, w, g, u):
    t, k = df.shape
    tn = _pick(D_FF, (256, 128))
    tm = _stream_rows(t, 4 * k * tn, 4 * k + (4 * 2 * 2 + 4) * tn)

    def body(df_ref, w_ref, g_ref, u_ref, dg_ref, du_ref):
        da_all = _dot(df_ref[...], w_ref[...], ((1,), (1,)))
        for r in range(tm // EPI):
            rows = slice(r * EPI, (r + 1) * EPI)
            da = da_all[rows]
            gv = g_ref[rows, :].astype(F32)
            sg, silu = _silu_parts(gv)
            dg_ref[rows, :] = ((da * u_ref[rows, :].astype(F32)) * (sg + silu * (1.0 - sg))).astype(BF16)
            du_ref[rows, :] = (da * silu).astype(BF16)

    blk = pl.BlockSpec((tm, tn), lambda i, j: (i, j))
    return pl.pallas_call(
        body, name="d_act", grid=(t // tm, D_FF // tn),
        in_specs=[pl.BlockSpec((tm, k), lambda i, j: (i, 0)), pl.BlockSpec((tn, k), lambda i, j: (j, 0)), blk, blk],
        out_specs=[blk] * 2, out_shape=[jax.ShapeDtypeStruct((t, D_FF), BF16)] * 2,
        compiler_params=_cp("parallel", "parallel"),
    )(df, w, g, u)


def _d_h2(dg, du, wt):
    t = dg.shape[0]
    n = wt.shape[1]
    tm, tn = _pick_rows(t, D_FF), _pick(n, (512, 256, 128))

    def body(dg_ref, du_ref, wg_ref, wu_ref, o_ref):
        o_ref[...] = (_dot(dg_ref[...], wg_ref[...], ((1,), (0,)))
                      + _dot(du_ref[...], wu_ref[...], ((1,), (0,)))).astype(BF16)

    return pl.pallas_call(
        body, name="d_h2", grid=(t // tm, n // tn),
        in_specs=[pl.BlockSpec((tm, D_FF), lambda i, j: (i, 0)), pl.BlockSpec((tm, D_FF), lambda i, j: (i, 0)),
                  pl.BlockSpec((D_FF, tn), lambda i, j: (0, j)), pl.BlockSpec((D_FF, tn), lambda i, j: (1, j))],
        out_specs=pl.BlockSpec((tm, tn), lambda i, j: (i, j)),
        out_shape=jax.ShapeDtypeStruct((t, n), BF16),
        compiler_params=_cp("parallel", "parallel"),
    )(dg, du, wt, wt)


TM = 1024


def _row(w):
    return pl.BlockSpec((TM, w), lambda i: (i, 0))


def _row_at(w, col):
    return pl.BlockSpec((TM, w), lambda i: (i, col))


def _vec(w):
    return pl.BlockSpec((1, w), lambda i: (0, 0))


def _per_ex(w):
    return pl.BlockSpec((1, 1, w), lambda i: (i // (S // TM), 0, 0))


def _pos(w):
    return pl.BlockSpec((TM, w), lambda i: (i % (S // TM), 0))


def _full(shape):
    return pl.BlockSpec(shape, lambda i: (0,) * len(shape))


def _rms(x):
    return lax.rsqrt(jnp.mean(x * x, axis=-1, keepdims=True) + EPS)


def _rms_bwd(n, r, dn):
    return r * (dn - n * jnp.mean(dn * n, axis=-1, keepdims=True))


def _colsum(v):
    return jnp.sum(v, axis=0, keepdims=True)


def _acc_first(i, ref, val, every=None):
    first = (i == 0) if every is None else (i % every == 0)

    @pl.when(first)
    def _():
        ref[...] = jnp.zeros_like(ref)

    ref[...] += val.reshape(ref.shape)


def _pre1(x, g, sc, sh):
    t = x.shape[0]

    def body(x_ref, g_ref, sc_ref, sh_ref, h_ref):
        xv = x_ref[...]
        n = xv * _rms(xv)
        h_ref[...] = ((n * g_ref[...]) * (1.0 + sc_ref[0]) + sh_ref[0]).astype(BF16)

    return pl.pallas_call(
        body, name="pre1", grid=(t // TM,),
        in_specs=[_row(D), _vec(D), _per_ex(D), _per_ex(D)],
        out_specs=_row(D), out_shape=jax.ShapeDtypeStruct((t, D), BF16),
        compiler_params=_cp("parallel"),
    )(x, g, sc, sh)


def _rope_fwd(v, c, sm, sp):
    return v * c + pltpu.roll(v, HP - ROPE // 2, 1) * sm + pltpu.roll(v, ROPE // 2, 1) * sp


def _rope_bwd(dv, c, sm, sp):
    return dv * c + pltpu.roll(dv * sm, ROPE // 2, 1) + pltpu.roll(dv * sp, HP - ROPE // 2, 1)


def _mla_pre(proj, g_cq, g_ckv, w_uq, w_k, w_v, rc, rsm, rsp):
    t = proj.shape[0]

    def body(tail_ref, gq_ref, gkv_ref, wuq_ref, wk_ref, wv_ref, c_ref, sm_ref, sp_ref,
             q_ref, k_ref, v_ref, cqn_ref, ckvn_ref):
        tail = tail_ref[...]
        cq, ckv, kr = tail[:, :Q_LORA], tail[:, Q_LORA:Q_LORA + KV_LORA], tail[:, Q_LORA + KV_LORA:]
        cqn = (cq * _rms(cq) * gq_ref[...]).astype(BF16)
        ckvn = (ckv * _rms(ckv) * gkv_ref[...]).astype(BF16)
        cqn_ref[...] = cqn
        ckvn_ref[...] = ckvn
        c, sm, sp = c_ref[...], sm_ref[...], sp_ref[...]
        q = _dot(cqn, wuq_ref[...], ((1,), (1,)))
        kn = _dot(ckvn, wk_ref[...], ((1,), (0,)))
        v_ref[...] = _dot(ckvn, wv_ref[...], ((1,), (0,))).astype(BF16)
        krr = _rope_fwd(kr, c, sm, sp)
        for h in range(H):
            sl = slice(h * HP, (h + 1) * HP)
            q_ref[:, sl] = _rope_fwd(q[:, sl], c, sm, sp).astype(BF16)
            k_ref[:, sl] = (kn[:, sl] + krr).astype(BF16)

    wide = H * HP
    return pl.pallas_call(
        body, name="mla_pre", grid=(t // TM,),
        in_specs=[_row_at(TAIL, TAIL0 // TAIL), _vec(Q_LORA), _vec(KV_LORA), _full((wide, Q_LORA)),
                  _full((KV_LORA, wide)), _full((KV_LORA, wide)), _pos(HP), _pos(HP), _pos(HP)],
        out_specs=[_row(wide), _row(wide), _row(wide), _row(Q_LORA), _row(KV_LORA)],
        out_shape=[jax.ShapeDtypeStruct((t, wide), BF16)] * 3
        + [jax.ShapeDtypeStruct((t, Q_LORA), BF16), jax.ShapeDtypeStruct((t, KV_LORA), BF16)],
        compiler_params=_cp("parallel"),
    )(proj, g_cq, g_ckv, w_uq, w_k, w_v, rc, rsm, rsp)


def _mla_pre_bwd(proj, dq_, dk_, dv_, dqkv_a, g_cq, g_ckv, w_uq, w_k, w_v, rc, rsm, rsp):
    t = proj.shape[0]
    wide = H * HP

    def body(tail_ref, dq_ref, dk_ref, dv_ref, dqa_ref, dka_ref, dva_ref, gq_ref, gkv_ref, wuq_ref, wk_ref, wv_ref,
             c_ref, sm_ref, sp_ref, dqo_ref, dproj_ref, dgq_ref, dgkv_ref):
        i = pl.program_id(0)
        for n, src in enumerate((dqa_ref, dka_ref, dva_ref)):
            dproj_ref[:, n * D_A:(n + 1) * D_A] = src[...]
        dtail_ref = dproj_ref.at[:, TAIL0:]
        tail = tail_ref[...]
        cq, ckv = tail[:, :Q_LORA], tail[:, Q_LORA:Q_LORA + KV_LORA]
        c, sm, sp = c_ref[...], sm_ref[...], sp_ref[...]
        dkr = jnp.zeros((TM, HP), F32)
        for h in range(H):
            sl = slice(h * HP, (h + 1) * HP)
            dqo_ref[:, sl] = _rope_bwd(dq_ref[:, sl].astype(F32), c, sm, sp).astype(BF16)
            dkr = dkr + dk_ref[:, sl].astype(F32)
        lane = lax.broadcasted_iota(jnp.int32, (TM, HP), 1)
        dkr = jnp.where((lane >= NOPE) & (lane < NOPE + ROPE), _rope_bwd(dkr, c, sm, sp), 0.0)
        dkb = dk_ref[...]
        dvb = dv_ref[...]
        dcqn = _dot(dqo_ref[...], wuq_ref[...], ((1,), (0,)))
        dckvn = _dot(dkb, wk_ref[...], ((1,), (1,))) + _dot(dvb, wv_ref[...], ((1,), (1,)))
        rq, rkv = _rms(cq), _rms(ckv)
        nq, nkv = cq * rq, ckv * rkv
        _acc_first(i, dgq_ref, _colsum(dcqn * nq))
        _acc_first(i, dgkv_ref, _colsum(dckvn * nkv))
        dtail_ref[:, :Q_LORA] = _rms_bwd(nq, rq, dcqn * gq_ref[...]).astype(BF16)
        dtail_ref[:, Q_LORA:Q_LORA + KV_LORA] = _rms_bwd(nkv, rkv, dckvn * gkv_ref[...]).astype(BF16)
        dtail_ref[:, Q_LORA + KV_LORA:] = dkr.astype(BF16)

    return pl.pallas_call(
        body, name="mla_pre_bwd", grid=(t // TM,),
        in_specs=[_row_at(TAIL, TAIL0 // TAIL), _row(wide), _row(wide), _row(wide), _row(D_A), _row(D_A), _row(D_A),
                  _vec(Q_LORA), _vec(KV_LORA), _full((wide, Q_LORA)), _full((KV_LORA, wide)), _full((KV_LORA, wide)),
                  _pos(HP), _pos(HP), _pos(HP)],
        out_specs=[_row(wide), _row(P_PAD), _vec(Q_LORA), _vec(KV_LORA)],
        out_shape=[jax.ShapeDtypeStruct((t, wide), BF16), jax.ShapeDtypeStruct((t, P_PAD), BF16),
                   jax.ShapeDtypeStruct((1, Q_LORA), F32), jax.ShapeDtypeStruct((1, KV_LORA), F32)],
        compiler_params=_cp("arbitrary"),
    )(proj, dq_, dk_, dv_, *dqkv_a, g_cq, g_ckv, w_uq, w_k, w_v, rc, rsm, rsp)


def _post_attn(out_a, out_b, g_a, g_b):
    t = out_a.shape[0]

    def body(a_ref, b_ref, ga_ref, gb_ref, y_ref):
        a, b = a_ref[...], b_ref[...]
        y_ref[:, :D_A] = (a * _rms(a) * ga_ref[...]).astype(BF16)
        y_ref[:, D_A:] = (b * _rms(b) * gb_ref[...]).astype(BF16)

    return pl.pallas_call(
        body, name="post_attn", grid=(t // TM,),
        in_specs=[_row(D_A), _row(D_A), _vec(D_A), _vec(D_A)],
        out_specs=_row(D), out_shape=jax.ShapeDtypeStruct((t, D), BF16),
        compiler_params=_cp("parallel"),
    )(out_a, out_b, g_a, g_b)


def _post_attn_bwd(dy, out_a, out_b, g_a, g_b):
    t = dy.shape[0]

    def body(dy_ref, a_ref, b_ref, ga_ref, gb_ref, da_ref, db_ref, dga_ref, dgb_ref):
        i = pl.program_id(0)
        dy_ = dy_ref[...].astype(F32)
        for src, g_ref, dst, dg_ref, sl in ((a_ref, ga_ref, da_ref, dga_ref, slice(0, D_A)),
                                            (b_ref, gb_ref, db_ref, dgb_ref, slice(D_A, D))):
            v = src[...]
            r = _rms(v)
            n = v * r
            dyv = dy_[:, sl]
            _acc_first(i, dg_ref, _colsum(dyv * n))
            dst[...] = _rms_bwd(n, r, dyv * g_ref[...])

    return pl.pallas_call(
        body, name="post_attn_bwd", grid=(t // TM,),
        in_specs=[_row(D), _row(D_A), _row(D_A), _vec(D_A), _vec(D_A)],
        out_specs=[_row(D_A), _row(D_A), _vec(D_A), _vec(D_A)],
        out_shape=[jax.ShapeDtypeStruct((t, D_A), F32)] * 2 + [jax.ShapeDtypeStruct((1, D_A), F32)] * 2,
        compiler_params=_cp("arbitrary"),
    )(dy, out_a, out_b, g_a, g_b)


def _resid_norm2(x, mix, g1, g, sc, sh):
    t = x.shape[0]

    def body(x_ref, mix_ref, g1_ref, g_ref, sc_ref, sh_ref, x2_ref, h_ref):
        x2 = x_ref[...] + g1_ref[0] * mix_ref[...]
        x2_ref[...] = x2
        n = x2 * _rms(x2)
        h_ref[...] = ((n * g_ref[...]) * (1.0 + sc_ref[0]) + sh_ref[0]).astype(BF16)

    return pl.pallas_call(
        body, name="resid_norm2", grid=(t // TM,),
        in_specs=[_row(D), _row(D), _per_ex(D), _vec(D), _per_ex(D), _per_ex(D)],
        out_specs=[_row(D), _row(D)],
        out_shape=[jax.ShapeDtypeStruct((t, D), F32), jax.ShapeDtypeStruct((t, D), BF16)],
        compiler_params=_cp("parallel"),
    )(x, mix, g1, g, sc, sh)


def _sigmoid(v):
    return 1.0 / (1.0 + jnp.exp(-v))


def _final(x2, f, g2, g_fin, target):
    t = x2.shape[0]
    nb = t // S
    tpb = S // TM

    def body(x2_ref, f_ref, g2_ref, g_ref, t_ref, dx3_ref, df_ref, loss_ref, dgf_ref, dg2_ref):
        i = pl.program_id(0)
        fv = f_ref[...].astype(F32)
        x3 = x2_ref[...] + g2_ref[0] * fv
        r = _rms(x3)
        n = x3 * r
        err = n * g_ref[...] - t_ref[...]
        _acc_first(i, loss_ref, _colsum(err * err))
        dy = err * (1.0 / D)
        _acc_first(i, dgf_ref, _colsum(dy * n))
        dx3 = _rms_bwd(n, r, dy * g_ref[...])
        dx3_ref[...] = dx3.astype(BF16)
        _acc_first(i, dg2_ref, _colsum(dx3 * fv), every=tpb)
        df_ref[...] = (dx3 * g2_ref[0]).astype(BF16)

    return pl.pallas_call(
        body, name="final", grid=(t // TM,),
        in_specs=[_row(D), _row(D), _per_ex(D), _vec(D), _row(D)],
        out_specs=[_row(D), _row(D), _vec(D), _vec(D), _per_ex(D)],
        out_shape=[jax.ShapeDtypeStruct((t, D), BF16), jax.ShapeDtypeStruct((t, D), BF16),
                   jax.ShapeDtypeStruct((1, D), F32), jax.ShapeDtypeStruct((1, D), F32),
                   jax.ShapeDtypeStruct((nb, 1, D), F32)],
        compiler_params=_cp("arbitrary"),
    )(x2, f, g2, g_fin, target)


def _norm_bwd(xin, dh, dres, g, sc, gate=None):
    t = xin.shape[0]
    nb = t // S
    tpb = S // TM
    gated = gate is not None

    def body(*refs):
        if gated:
            x_ref, dh_ref, dres_ref, g_ref, sc_ref, mix_ref, g1_ref, dx_ref, dsh_ref, dsc_ref, dg_ref, dg1_ref, dmix_ref = refs
        else:
            x_ref, dh_ref, dres_ref, g_ref, sc_ref, dx_ref, dsh_ref, dsc_ref, dg_ref = refs
        i = pl.program_id(0)
        xv, dhv = x_ref[...], dh_ref[...].astype(F32)
        r = _rms(xv)
        n = xv * r
        gv = g_ref[...]
        _acc_first(i, dsh_ref, _colsum(dhv), every=tpb)
        _acc_first(i, dsc_ref, _colsum(dhv * (n * gv)), every=tpb)
        dng = dhv * (1.0 + sc_ref[0])
        _acc_first(i, dg_ref, _colsum(dng * n))
        dx = dres_ref[...].astype(F32) + _rms_bwd(n, r, dng * gv)
        dx_ref[...] = dx.astype(dx_ref.dtype)
        if gated:
            _acc_first(i, dg1_ref, _colsum(dx * mix_ref[...].astype(F32)), every=tpb)
            dmix_ref[...] = (dx * g1_ref[0]).astype(BF16)

    in_specs = [_row(D), _row(D), _row(D), _vec(D), _per_ex(D)]
    out_specs = [_row(D), _per_ex(D), _per_ex(D), _vec(D)]
    out_shape = [jax.ShapeDtypeStruct((t, D), BF16 if gated else F32), jax.ShapeDtypeStruct((nb, 1, D), F32),
                 jax.ShapeDtypeStruct((nb, 1, D), F32), jax.ShapeDtypeStruct((1, D), F32)]
    args = [xin, dh, dres, g, sc]
    if gated:
        in_specs += [_row(D), _per_ex(D)]
        out_specs += [_per_ex(D), _row(D)]
        out_shape += [jax.ShapeDtypeStruct((nb, 1, D), F32), jax.ShapeDtypeStruct((t, D), BF16)]
        args += list(gate)
    return pl.pallas_call(
        body, name="norm2_bwd" if gated else "norm1_bwd", grid=(t // TM,),
        in_specs=in_specs, out_specs=out_specs, out_shape=out_shape,
        compiler_params=_cp("arbitrary"),
    )(*args)


TQ = 256
TB = 512
FWD_HEADS = 2


def _mla_fwd(q, k, v):
    t = q.shape[0]
    nb = t // S

    def body(q_ref, k_ref, v_ref, o_ref, lse_ref):
        causal = lax.broadcasted_iota(jnp.int32, (TB, TB), 0) >= lax.broadcasted_iota(jnp.int32, (TB, TB), 1)
        heads = [slice(h * HP, (h + 1) * HP) for h in range(FWD_HEADS)]
        for i in range(S // TB):
            ri, past = slice(i * TB, (i + 1) * TB), slice(0, i * TB)
            qhs = [q_ref[ri, sl] for sl in heads]
            sd = [jnp.where(causal, _dot(qh, k_ref[ri, sl], ((1,), (1,))) * SCALE_B, NEG) for qh, sl in zip(qhs, heads)]
            ms = [jnp.max(s, axis=-1, keepdims=True) for s in sd]
            if i:
                so = [_dot(qh, k_ref[past, sl], ((1,), (1,))) * SCALE_B for qh, sl in zip(qhs, heads)]
                ms = [jnp.maximum(m, jnp.max(s, axis=-1, keepdims=True)) for m, s in zip(ms, so)]
            pd = [jnp.exp(s - m) for s, m in zip(sd, ms)]
            ls = [jnp.sum(p, axis=-1, keepdims=True) for p in pd]
            acc = [_dot(p.astype(BF16), v_ref[ri, sl], ((1,), (0,))) for p, sl in zip(pd, heads)]
            if i:
                po = [jnp.exp(s - m) for s, m in zip(so, ms)]
                ls = [l + jnp.sum(p, axis=-1, keepdims=True) for l, p in zip(ls, po)]
                acc = [a + _dot(p.astype(BF16), v_ref[past, sl], ((1,), (0,))) for a, p, sl in zip(acc, po, heads)]
            for pr in range(FWD_HEADS // 2):
                o_ref[ri, pr * HP:(pr + 1) * HP] = acc[2 * pr] / ls[2 * pr] + acc[2 * pr + 1] / ls[2 * pr + 1]
            for sl, m, l in zip(heads, ms, ls):
                lse_ref[ri, sl] = jnp.broadcast_to(m + jnp.log(l), (TB, HP))

    wide2 = pl.BlockSpec((S, FWD_HEADS * HP), lambda b, p: (b, p))
    return pl.pallas_call(
        body, name="mla_fwd", grid=(nb, H // FWD_HEADS),
        in_specs=[wide2, wide2, wide2],
        out_specs=[pl.BlockSpec((S, FWD_HEADS // 2 * HP), lambda b, p: (b, p)), wide2],
        out_shape=[jax.ShapeDtypeStruct((t, H * VDIM), F32), jax.ShapeDtypeStruct((t, H * HP), F32)],
        compiler_params=_cp("parallel", "parallel"),
    )(q, k, v)


def _mla_bwd(q, k, v, o, do, lse):
    t = q.shape[0]
    nb = t // S

    def body(q_ref, k_ref, v_ref, o_ref, do_ref, lse_ref, dq_out, dk_out, dv_out, dq_ref, dk_ref, dv_ref):
        lane = lax.broadcasted_iota(jnp.int32, (TB, HP), 1)
        causal = lax.broadcasted_iota(jnp.int32, (TB, TB), 0) >= lax.broadcasted_iota(jnp.int32, (TB, TB), 1)
        heads = [slice(h * HP, (h + 1) * HP) for h in range(2)]
        nblk = S // TB
        for i in reversed(range(nblk)):
            ri, past = slice(i * TB, (i + 1) * TB), slice(0, i * TB)
            dov = do_ref[ri, :]
            prod = dov * o_ref[ri, :]
            dob = dov.astype(BF16)
            deltas = [jnp.sum(jnp.where((lane < VDIM) if h == 0 else (lane >= VDIM), prod, 0.0), axis=-1, keepdims=True)
                      for h in range(2)]
            qhs = [q_ref[ri, sl] for sl in heads]
            lses = [lse_ref[ri, sl][:, :1] for sl in heads]
            for rows, diagonal in ((ri, True), (past, False)):
                if rows.stop == rows.start:
                    continue
                ps = [jnp.exp(_dot(qh, k_ref[rows, sl], ((1,), (1,))) * SCALE_B - lse) for qh, sl, lse in zip(qhs, heads, lses)]
                if diagonal:
                    ps = [jnp.where(causal, p, 0.0) for p in ps]
                dps = [_dot(dob, v_ref[rows, sl], ((1,), (1,))) for sl in heads]
                dss = [(p * (dp - delta) * SCALE_B).astype(BF16) for p, dp, delta in zip(ps, dps, deltas)]
                for sl, qh, p, ds in zip(heads, qhs, ps, dss):
                    dq = _dot(ds, k_ref[rows, sl], ((1,), (0,)))
                    dk = _dot(ds, qh, ((0,), (0,)))
                    dv = _dot(p.astype(BF16), dob, ((0,), (0,)))
                    if diagonal:
                        dq_ref[ri, sl] = dq
                    else:
                        dq_ref[ri, sl] += dq
                    if i == nblk - 1:
                        dk_ref[rows, sl] = dk
                        dv_ref[rows, sl] = dv
                    else:
                        dk_ref[rows, sl] += dk
                        dv_ref[rows, sl] += dv
        dq_out[...] = dq_ref[...].astype(BF16)
        dk_out[...] = dk_ref[...].astype(BF16)
        dv_out[...] = dv_ref[...].astype(BF16)

    wide2 = pl.BlockSpec((S, 2 * HP), lambda b, p: (b, p))
    pair = pl.BlockSpec((S, HP), lambda b, p: (b, p))
    return pl.pallas_call(
        body, name="mla_bwd", grid=(nb, H // 2),
        in_specs=[wide2, wide2, wide2, pair, pair, wide2],
        out_specs=[wide2, wide2, wide2],
        out_shape=[jax.ShapeDtypeStruct((t, H * HP), BF16)] * 3,
        scratch_shapes=[pltpu.VMEM((S, 2 * HP), F32)] * 3,
        compiler_params=_cp("parallel", "parallel"),
    )(q, k, v, o, do, lse)


def _t5_bucket(dist):
    max_exact = N_BUCKETS // 2
    d = np.maximum(dist, 1).astype(np.float64)
    large = max_exact + (np.log(d / max_exact) / np.log(MAX_DISTANCE / max_exact) * (N_BUCKETS - max_exact)).astype(np.int64)
    large = np.minimum(large, N_BUCKETS - 1)
    return np.where(dist < max_exact, dist, large).astype(np.int32)


def _band_geometry():
    a = np.arange(BLK)[:, None]
    bk = np.arange(2 * BLK)[None, :]
    steps = BLK + a - bk
    valid = (steps >= 0) & (steps <= BLK)
    buckets = np.stack([_t5_bucket(np.clip(steps, 0, BLK) * d) for d in DILATIONS])
    return buckets, valid


def _band_bias(rel_bias):
    buckets, valid = _band_geometry()
    onehot = (jnp.asarray(buckets)[..., None] == jnp.arange(N_BUCKETS)).astype(F32)
    bias = jnp.einsum("rqkn,nh->rhqk", onehot, rel_bias, precision=lax.Precision.HIGHEST)
    bias = jnp.where(jnp.asarray(valid)[None, None], bias, NEG)
    return bias.reshape(3, H // 2, 2 * BLK, 2 * BLK)


def _dil_items():
    items = []
    for r, d in enumerate(DILATIONS):
        for res in range(d):
            for blk in range(S // d // BLK):
                items.append((r, d, blk * BLK * d + res, blk > 0))
    return items


GROUP = 4


def _strided(start, d):
    return pl.ds(start, BLK) if d == 1 else pl.ds(start, BLK, stride=d)


def _stack_heads(tile, own):
    return jnp.where(own, jnp.concatenate([tile, tile], axis=0), 0.0).astype(BF16)


def _own_lanes():
    row = lax.broadcasted_iota(jnp.int32, (2 * BLK, HP), 0)
    lane = lax.broadcasted_iota(jnp.int32, (2 * BLK, HP), 1)
    return (lane < E_A) == (row < BLK)


def _dil_fwd(proj, biasm):
    t = proj.shape[0]
    nb = t // S

    def body(q_ref, k_ref, v_ref, b_ref, o_ref, lse_ref, ob_ref, lb_ref):
        lane = lax.broadcasted_iota(jnp.int32, (BLK, HP), 1)
        own = _own_lanes()
        items = _dil_items()
        for g in range(0, len(items), GROUP):
            grp = items[g:g + GROUP]
            ss, vts = [], []
            for r, d, start, has_prev in grp:
                cur = _strided(start, d)
                rows = [_strided(start - BLK * d, d), cur] if has_prev else [cur]
                q2 = _stack_heads(q_ref[cur, :] * SCALE_A, own)
                kt = jnp.concatenate([k_ref[x, :] for x in rows], axis=0).astype(BF16)
                vts.append(jnp.concatenate([v_ref[x, :] for x in rows], axis=0).astype(BF16))
                bias = b_ref[r, 0] if has_prev else b_ref[r, 0, :, BLK:]
                ss.append(_dot(q2, kt, ((1,), (1,))) + bias)
            ms = [jnp.max(s, axis=-1, keepdims=True) for s in ss]
            ps = [jnp.exp(s - m) for s, m in zip(ss, ms)]
            ls = [jnp.sum(p, axis=-1, keepdims=True) for p in ps]
            for (r, d, start, _), p, vt, m, l in zip(grp, ps, vts, ms, ls):
                cur = _strided(start, d)
                o2 = _dot(p.astype(BF16), vt, ((1,), (0,))) / l
                lse2 = m + jnp.log(l)
                ob_ref[r, cur, :] = jnp.where(lane < E_A, o2[:BLK], o2[BLK:])
                lb_ref[r, cur, :] = jnp.where(lane < E_A, lse2[:BLK], lse2[BLK:])

        def merge(c, _):
            rows = pl.ds(pl.multiple_of(c * TQ, TQ), TQ)
            l0, l1, l2 = lb_ref[0, rows, :], lb_ref[1, rows, :], lb_ref[2, rows, :]
            m = jnp.maximum(jnp.maximum(l0, l1), l2)
            e0, e1, e2 = jnp.exp(l0 - m), jnp.exp(l1 - m), jnp.exp(l2 - m)
            tot = e0 + e1 + e2
            o_ref[rows, :] = (e0 * ob_ref[0, rows, :] + e1 * ob_ref[1, rows, :] + e2 * ob_ref[2, rows, :]) / tot
            lse_ref[rows, :] = m + jnp.log(tot)
            return 0

        lax.fori_loop(0, S // TQ, merge, 0)

    npair = H // 2
    return pl.pallas_call(
        body, name="dil_fwd", grid=(nb, npair),
        in_specs=[pl.BlockSpec((S, HP), lambda b, p: (b, p)), pl.BlockSpec((S, HP), lambda b, p: (b, npair + p)),
                  pl.BlockSpec((S, HP), lambda b, p: (b, 2 * npair + p)),
                  pl.BlockSpec((3, 1, 2 * BLK, 2 * BLK), lambda b, p: (0, p, 0, 0))],
        out_specs=[pl.BlockSpec((S, HP), lambda b, p: (b, p))] * 2,
        out_shape=[jax.ShapeDtypeStruct((t, D_A), F32)] * 2,
        scratch_shapes=[pltpu.VMEM((3, S, HP), F32), pltpu.VMEM((3, S, HP), F32)],
        compiler_params=_cp("parallel", "parallel"),
    )(proj, proj, proj, biasm)


def _dil_bwd(proj, biasm, o, do, lse):
    t = proj.shape[0]
    nb = t // S

    def body(q_ref, k_ref, v_ref, b_ref, o_ref, do_ref, lse_ref, dq_out, dk_out, dv_out, ds_ref, dq_ref, dk_ref, dv_ref):
        dq_ref[...] = jnp.zeros_like(dq_ref)
        dk_ref[...] = jnp.zeros_like(dk_ref)
        dv_ref[...] = jnp.zeros_like(dv_ref)
        ds_ref[...] = jnp.zeros_like(ds_ref)
        lane = lax.broadcasted_iota(jnp.int32, (BLK, HP), 1)
        own = _own_lanes()
        items = _dil_items()
        for g in range(0, len(items), GROUP):
            grp = items[g:g + GROUP]
            q2s, kts, do2s, ss, dps, lse2s, delta2s = [], [], [], [], [], [], []
            for r, d, start, has_prev in grp:
                cur = _strided(start, d)
                rows = [_strided(start - BLK * d, d), cur] if has_prev else [cur]
                q2 = _stack_heads(q_ref[cur, :] * SCALE_A, own)
                kt = jnp.concatenate([k_ref[x, :] for x in rows], axis=0).astype(BF16)
                vt = jnp.concatenate([v_ref[x, :] for x in rows], axis=0).astype(BF16)
                dot_ = do_ref[cur, :]
                prod = dot_ * o_ref[cur, :]
                lset = lse_ref[cur, :]
                do2 = _stack_heads(dot_, own)
                bias = b_ref[r, 0] if has_prev else b_ref[r, 0, :, BLK:]
                ss.append(_dot(q2, kt, ((1,), (1,))) + bias)
                dps.append(_dot(do2, vt, ((1,), (1,))))
                lse2s.append(jnp.concatenate([lset[:, :1], lset[:, E_A:E_A + 1]], axis=0))
                delta2s.append(jnp.concatenate([jnp.sum(jnp.where(lane < E_A, prod, 0.0), axis=-1, keepdims=True),
                                                jnp.sum(jnp.where(lane >= E_A, prod, 0.0), axis=-1, keepdims=True)], axis=0))
                q2s.append(q2)
                kts.append(kt)
                do2s.append(do2)
            ps = [jnp.exp(s - lse2) for s, lse2 in zip(ss, lse2s)]
            dls = [p * (dp - delta2) for p, dp, delta2 in zip(ps, dps, delta2s)]
            for (r, d, start, has_prev), q2, kt, do2, p, dl in zip(grp, q2s, kts, do2s, ps, dls):
                cur = _strided(start, d)
                dsb = dl.astype(BF16)
                dq2 = _dot(dsb, kt, ((1,), (0,))) * SCALE_A
                dkt = _dot(dsb, q2, ((0,), (0,)))
                dvt = _dot(p.astype(BF16), do2, ((0,), (0,)))
                dq_ref[cur, :] += jnp.where(lane < E_A, dq2[:BLK], dq2[BLK:])
                if has_prev:
                    prev = _strided(start - BLK * d, d)
                    ds_ref[0, r, 0] += dl
                    dk_ref[prev, :] += dkt[:BLK]
                    dv_ref[prev, :] += dvt[:BLK]
                    dk_ref[cur, :] += dkt[BLK:]
                    dv_ref[cur, :] += dvt[BLK:]
                else:
                    ds_ref[0, r, 0, :, BLK:] += dl
                    dk_ref[cur, :] += dkt
                    dv_ref[cur, :] += dvt
        dq_out[...] = dq_ref[...].astype(BF16)
        dk_out[...] = dk_ref[...].astype(BF16)
        dv_out[...] = dv_ref[...].astype(BF16)

    npair = H // 2
    pair = pl.BlockSpec((S, HP), lambda b, p: (b, p))
    return pl.pallas_call(
        body, name="dil_bwd", grid=(nb, npair),
        in_specs=[pair, pl.BlockSpec((S, HP), lambda b, p: (b, npair + p)),
                  pl.BlockSpec((S, HP), lambda b, p: (b, 2 * npair + p)),
                  pl.BlockSpec((3, 1, 2 * BLK, 2 * BLK), lambda b, p: (0, p, 0, 0)), pair, pair, pair],
        out_specs=[pair, pair, pair, pl.BlockSpec((1, 3, 1, 2 * BLK, 2 * BLK), lambda b, p: (b, 0, p, 0, 0))],
        out_shape=[jax.ShapeDtypeStruct((t, D_A), BF16)] * 3 + [jax.ShapeDtypeStruct((nb, 3, npair, 2 * BLK, 2 * BLK), F32)],
        scratch_shapes=[pltpu.VMEM((S, HP), F32)] * 3,
        compiler_params=_cp("parallel", "parallel"),
    )(proj, proj, proj, biasm, o, do, lse)


def _rel_bias_grad(dlogits, after=None):
    nb = dlogits.shape[0]
    buckets, _ = _band_geometry()
    kk = 3 * BLK * 2 * BLK
    dl = jnp.transpose(dlogits.reshape(nb, 3, H, BLK, 2 * BLK), (0, 2, 1, 3, 4)).reshape(nb, H, kk)
    bk = jnp.asarray(buckets.reshape(1, kk))
    tk = kk // 4
    extra = [] if after is None else [after]

    def body(dl_ref, bk_ref, *rest):
        o_ref = rest[-1]
        j = pl.program_id(0)
        onehot = (bk_ref[...] == lax.broadcasted_iota(jnp.int32, (N_BUCKETS, tk), 0)).astype(F32)
        tot = dl_ref[0]
        for b in range(1, nb):
            tot = tot + dl_ref[b]
        part = lax.dot_general(onehot, tot, ((((1,), (1,))), ((), ())), preferred_element_type=F32,
                               precision=lax.Precision.HIGHEST)
        _acc_first(j, o_ref, part)

    return pl.pallas_call(
        body, name="rel_bias_grad", grid=(kk // tk,),
        in_specs=[pl.BlockSpec((nb, H, tk), lambda j: (0, 0, j)), pl.BlockSpec((1, tk), lambda j: (0, j))] + [ANY] * len(extra),
        out_specs=pl.BlockSpec((N_BUCKETS, H), lambda j: (0, 0)),
        out_shape=jax.ShapeDtypeStruct((N_BUCKETS, H), F32),
        compiler_params=_cp("arbitrary"),
    )(dl, bk, *extra)


def _mesh_place():
    x, y, c = lax.axis_index("x"), lax.axis_index("y"), lax.axis_index("c")
    return x, y, c


def _peer(k):
    x, y, c = _mesh_place()
    px = 1 - x if k & 4 else x
    py = 1 - y if k & 2 else y
    pc = 1 - c if k & 1 else c
    return (px, py, pc), 4 * px + 2 * py + pc


ANY = pl.BlockSpec(memory_space=pl.ANY)


def _exchange(arrays, gathers, name, after=None):
    n_arr = len(arrays)

    def body(*refs):
        ins, outs = refs[:n_arr], refs[n_arr + 1:2 * n_arr + 1]
        send, recv, loc = refs[2 * n_arr + 1:]
        x, y, c = _mesh_place()
        me = 4 * x + 2 * y + c
        local = [pltpu.make_async_copy(ins[a] if gathers[a] else ins[a].at[me], outs[a].at[me], loc.at[a])
                 for a in range(n_arr)]
        remote = _peer_copies(ins, outs, send, recv, gathers)
        for cp in local:
            cp.start()
        for put, _ in remote:
            put.start()
        for cp in local:
            cp.wait()
        for put, got in remote:
            put.wait_send()
            got.wait_recv()

    return pl.pallas_call(
        body, name=name,
        in_specs=[ANY] * (n_arr + 1), out_specs=[ANY] * n_arr,
        out_shape=[jax.ShapeDtypeStruct(((N_DEV,) if g else ()) + a.shape, a.dtype) for a, g in zip(arrays, gathers)],
        scratch_shapes=[pltpu.SemaphoreType.DMA((n_arr * (N_DEV - 1),)), pltpu.SemaphoreType.DMA((n_arr * (N_DEV - 1),)),
                        pltpu.SemaphoreType.DMA((n_arr,))],
        compiler_params=pltpu.CompilerParams(has_side_effects=True),
    )(*arrays, arrays[0] if after is None else after)


def _gather_two_level(arrays, name):
    n_arr = len(arrays)
    per = N_DEV - 1

    def body(*refs):
        ins, outs = refs[:n_arr], refs[n_arr:2 * n_arr]
        send, recv, loc = refs[2 * n_arr:]
        x, y, c = _mesh_place()
        me, sibling = (x, y, c), (x, y, 1 - c)
        chips = [(1 - x, y), (x, 1 - y), (1 - x, 1 - y)]

        def block(a, place):
            px, py, pc = place
            return outs[a].at[4 * px + 2 * py + pc]

        def copy(a, k, place, to, src=None):
            dst = block(a, place)
            return pltpu.make_async_remote_copy(dst if src is None else src, dst, send.at[a * per + k], recv.at[a * per + k],
                                                device_id=to, device_id_type=pl.DeviceIdType.MESH)

        local = [pltpu.make_async_copy(ins[a], block(a, me), loc.at[a]) for a in range(n_arr)]
        for cp in local:
            cp.start()
        first = []
        for a in range(n_arr):
            first.append(copy(a, 0, me, sibling, src=ins[a]))
            first += [copy(a, 1 + j, me, (*chip, c), src=ins[a]) for j, chip in enumerate(chips)]
        for cp in first:
            cp.start()
        passed = []
        for j, chip in enumerate(chips):
            for a in range(n_arr):
                copy(a, 1 + j, (*chip, c), me).wait_recv()
                passed.append(copy(a, 4 + j, (*chip, c), sibling))
                passed[-1].start()
        for a in range(n_arr):
            copy(a, 0, sibling, me).wait_recv()
            for j, chip in enumerate(chips):
                copy(a, 4 + j, (*chip, 1 - c), me).wait_recv()
        for cp in first + passed:
            cp.wait_send()
        for cp in local:
            cp.wait()

    return pl.pallas_call(
        body, name=name,
        in_specs=[ANY] * n_arr, out_specs=[ANY] * n_arr,
        out_shape=[jax.ShapeDtypeStruct((N_DEV,) + a.shape, a.dtype) for a in arrays],
        scratch_shapes=[pltpu.SemaphoreType.DMA((n_arr * per,)), pltpu.SemaphoreType.DMA((n_arr * per,)),
                        pltpu.SemaphoreType.DMA((n_arr,))],
        compiler_params=pltpu.CompilerParams(has_side_effects=True),
    )(*arrays)


HBM = pl.BlockSpec(memory_space=pltpu.HBM)
SEM = pl.BlockSpec(memory_space=pltpu.SEMAPHORE)
DATAFLOW = pltpu.SideEffectType.DATAFLOW_SIDE_EFFECTING


def _own_block_in_place(block, me):
    land = lax.empty((N_DEV,) + block.shape, block.dtype)
    return lax.dynamic_update_slice(land, block[None], (me,) + (0,) * block.ndim)


def _peer_copies(srcs, lands, send, recv, gathers):
    x, y, c = _mesh_place()
    me = 4 * x + 2 * y + c
    out = []
    for a, (src, land) in enumerate(zip(srcs, lands)):
        for k in range(1, N_DEV):
            dev, idx = _peer(k)
            sem = a * (N_DEV - 1) + k - 1
            mine = src if gathers[a] else src.at[idx]
            put = pltpu.make_async_remote_copy(mine, land.at[me], send.at[sem], recv.at[sem],
                                               device_id=dev, device_id_type=pl.DeviceIdType.MESH)
            got = pltpu.make_async_remote_copy(mine, land.at[idx], send.at[sem], recv.at[sem],
                                               device_id=dev, device_id_type=pl.DeviceIdType.MESH)
            out.append((put, got))
    return out


def _exchange_start(srcs, lands, gather, after, name):
    n = len(srcs)
    extra = [] if after is None else [after]

    def body(*refs):
        srcs_, lands_ = refs[:n], refs[n:2 * n]
        send, recv = refs[2 * n + len(extra)], refs[2 * n + len(extra) + 1]
        for put, _ in _peer_copies(srcs_, lands_, send, recv, gather):
            put.start()
        refs[-1][...] = jnp.zeros_like(refs[-1])

    nsem = n * (N_DEV - 1)
    thru = [pltpu.HBM(a.shape, a.dtype) for a in list(srcs) + list(lands)]
    res = pl.pallas_call(
        body, name=name,
        out_shape=(pltpu.SemaphoreType.DMA((nsem,)), pltpu.SemaphoreType.DMA((nsem,)), *thru, jax.ShapeDtypeStruct((8, 128), F32)),
        in_specs=[HBM] * (2 * n) + [ANY] * len(extra),
        out_specs=(SEM, SEM, *([HBM] * (2 * n)), pl.BlockSpec(memory_space=pltpu.VMEM)),
        input_output_aliases={i: 2 + i for i in range(2 * n)},
        compiler_params=pltpu.CompilerParams(has_side_effects=DATAFLOW),
    )(*[pltpu.with_memory_space_constraint(a, pltpu.HBM) for a in list(srcs) + list(lands)], *extra)
    return res[0], res[1], list(res[2:2 + n]), list(res[2 + n:2 + 2 * n]), res[-1]


def _exchange_wait(send, recv, srcs, lands, gather, after, name):
    n = len(srcs)

    def body(*refs):
        srcs_, lands_, send_, recv_ = refs[:n], refs[n:2 * n], refs[2 * n], refs[2 * n + 1]
        for put, got in _peer_copies(srcs_, lands_, send_, recv_, gather):
            put.wait_send()
            got.wait_recv()

    thru = [pltpu.HBM(a.shape, a.dtype) for a in list(srcs) + list(lands)]
    res = pl.pallas_call(
        body, name=name, out_shape=tuple(thru),
        in_specs=[HBM] * (2 * n) + [SEM, SEM, ANY], out_specs=tuple([HBM] * (2 * n)),
        input_output_aliases={i: i for i in range(2 * n)},
        compiler_params=pltpu.CompilerParams(has_side_effects=DATAFLOW),
    )(*srcs, *lands, send, recv, after)
    return list(res[n:])


def _silu_rows(c):
    def body(c_ref, o_ref):
        v = c_ref[...]
        o_ref[...] = v * _sigmoid(v)

    return pl.pallas_call(body, name="cond", out_shape=jax.ShapeDtypeStruct(c.shape, F32))(c)


def _mod_slab(cond_all, w_ada, b_slab):
    def body(c_ref, w_ref, b_ref, o_ref):
        o_ref[...] = _dot(c_ref[...].astype(BF16), w_ref[0].astype(BF16), ((1,), (0,))) + b_ref[...]

    return pl.pallas_call(body, name="mod_slab",
                          out_shape=jax.ShapeDtypeStruct((cond_all.shape[0], w_ada.shape[2]), F32),
                          compiler_params=pltpu.CompilerParams(vmem_limit_bytes=VMEM_LIMIT))(cond_all, w_ada, b_slab)


def _ada_grad(cond_all, dmod_cols):
    def body(c_ref, d_ref, o_ref):
        o_ref[...] = _dot(c_ref[...].astype(BF16), d_ref[...].astype(BF16), ((0,), (0,)))

    return pl.pallas_call(body, name="ada_grad",
                          out_shape=jax.ShapeDtypeStruct((cond_all.shape[1], dmod_cols.shape[1]), F32),
                          compiler_params=pltpu.CompilerParams(vmem_limit_bytes=VMEM_LIMIT))(cond_all, dmod_cols)


def _adam_math(g, w, m, v):
    m2 = B1 * m + (1.0 - B1) * g
    v2 = B2 * v + (1.0 - B2) * (g * g)
    m_hat = m2 / (1.0 - B1 ** STEP)
    v_hat = v2 / (1.0 - B2 ** STEP)
    return -LR * (m_hat / (jnp.sqrt(v_hat) + ADAM_EPS) + WD * w), m2, v2


def _adamw(parts, w, m, v, name):
    n, rows, cols = parts.shape
    tr = max([p for p in range(16, 513, 16) if rows % p == 0] or [rows])

    def body(p_ref, w_ref, m_ref, v_ref, g_ref, d_ref, m2_ref, v2_ref):
        g = p_ref[0].astype(F32)
        for s in range(1, n):
            g = g + p_ref[s].astype(F32)
        g_ref[0] = g
        d_ref[0], m2_ref[0], v2_ref[0] = _adam_math(g, w_ref[0], m_ref[0], v_ref[0])

    blk = pl.BlockSpec((1, tr, cols), lambda i: (0, i, 0))
    return pl.pallas_call(
        body, name=name, grid=(rows // tr,),
        in_specs=[pl.BlockSpec((n, tr, cols), lambda i: (0, i, 0)), blk, blk, blk],
        out_specs=[blk] * 4, out_shape=[jax.ShapeDtypeStruct((1, rows, cols), F32)] * 4,
        compiler_params=_cp("parallel"),
    )(parts, w, m, v)


ROW_PARAMS = (("g_norm1", D), ("g_cq", Q_LORA), ("g_ckv", KV_LORA), ("g_out_a", D_A), ("g_out_b", D_A), ("g_norm2", D),
              ("g_final", D))
LOSS_ROW = N_MOD + len(ROW_PARAMS)
PAY_ROWS = 16
NCOL = N_MOD * D // N_DEV


def _pack_small(dmods, rows, loss_cols):
    nb = dmods[0].shape[0]
    nrow = len(ROW_PARAMS)

    def body(*refs):
        dm, rw, loss_ref, pay_ref, blk_ref = refs[:N_MOD], refs[N_MOD:N_MOD + nrow], refs[N_MOD + nrow], refs[-2], refs[-1]
        pay_ref[...] = jnp.zeros_like(pay_ref)
        for k in range(N_MOD):
            tot = dm[k][0]
            for b in range(1, nb):
                tot = tot + dm[k][b]
            pay_ref[k:k + 1, :] = tot
        for i, (_, n) in enumerate(ROW_PARAMS):
            pay_ref[N_MOD + i:N_MOD + i + 1, :n] = rw[i][...]
        pay_ref[LOSS_ROW:LOSS_ROW + 1, :] = loss_ref[...]
        for j in range(N_DEV):
            done = 0
            while done < NCOL:
                seg, off = divmod(j * NCOL + done, D)
                ln = min(NCOL - done, D - off)
                for b in range(nb):
                    blk_ref[j, b:b + 1, done:done + ln] = dm[seg][b][:, off:off + ln]
                done += ln

    return pl.pallas_call(
        body, name="pack_small",
        out_shape=[jax.ShapeDtypeStruct((PAY_ROWS, D), F32), jax.ShapeDtypeStruct((N_DEV, nb, NCOL), F32)],
    )(*dmods, *rows, loss_cols)


def _small_update(pay, rel, ws, ms, vs):
    n_par = len(ws)

    def body(*refs):
        pay_ref, rel_ref = refs[:2]
        w_refs, m_refs, v_refs = (refs[2 + s * n_par:2 + (s + 1) * n_par] for s in range(3))
        outs, loss_ref = refs[2 + 3 * n_par:-1], refs[-1]
        tot, rtot = pay_ref[0], rel_ref[0]
        for s in range(1, N_DEV):
            tot, rtot = tot + pay_ref[s], rtot + rel_ref[s]

        def update(p, g, sl):
            outs[4 * p][:, sl] = g
            outs[4 * p + 1][:, sl], outs[4 * p + 2][:, sl], outs[4 * p + 3][:, sl] = _adam_math(
                g, w_refs[p][:, sl], m_refs[p][:, sl], v_refs[p][:, sl])

        for k in range(N_MOD):
            update(0, tot[k:k + 1, :], slice(k * D, (k + 1) * D))
        for i, (_, n) in enumerate(ROW_PARAMS):
            update(1 + i, tot[N_MOD + i:N_MOD + i + 1, :n], slice(0, n))
        update(n_par - 1, rtot, slice(0, H))
        loss_ref[...] = jnp.broadcast_to((0.5 / D) * jnp.sum(tot[LOSS_ROW:LOSS_ROW + 1, :]), loss_ref.shape)

    shapes = [jax.ShapeDtypeStruct(w.shape, F32) for w in ws for _ in range(4)]
    res = pl.pallas_call(
        body, name="small_update", out_shape=shapes + [jax.ShapeDtypeStruct((8, 128), F32)],
    )(pay, rel, *ws, *ms, *vs)
    return [tuple(res[4 * p:4 * p + 4]) for p in range(n_par)], res[-1]


def _cols_from_blocks(g):
    return jnp.transpose(g, (1, 0, 2)).reshape(g.shape[1], N_DEV * g.shape[2])


def _cols_to_blocks(w):
    r, c = w.shape
    return jnp.transpose(w.reshape(r, N_DEV, c // N_DEV), (1, 0, 2))


def _pad_w_in(wt):
    z = jnp.zeros((NOPE, wt.shape[1]), wt.dtype)
    return jnp.concatenate([wt[:P_IN - ROPE], z, wt[P_IN - ROPE:], z[:HP - NOPE - ROPE]], axis=0)


def _unpad_w_in(gt):
    k0 = P_IN - ROPE + NOPE
    return jnp.concatenate([gt[:P_IN - ROPE], gt[k0:k0 + ROPE]], axis=0)


def _pad_w_uq(wt):
    return jnp.pad(wt, ((0, 0), (0, HP - NOPE - ROPE), (0, 0))).reshape(H * HP, Q_LORA)


def _unpad_w_uq(gt):
    return gt.reshape(H, HP, Q_LORA)[:, :NOPE + ROPE]


def _split_w_ukv(w):
    w4 = w.reshape(KV_LORA, H // 2, 2, HP)
    z = jnp.zeros((KV_LORA, H // 2, NOPE), w.dtype)
    kn, vv = w4[..., :NOPE], w4[..., NOPE:]
    w_k = jnp.stack([jnp.concatenate([kn[:, :, 0], z], -1), jnp.concatenate([kn[:, :, 1], z], -1)], axis=2)
    w_v = jnp.stack([jnp.concatenate([vv[:, :, 0], z], -1), jnp.concatenate([z, vv[:, :, 1]], -1)], axis=2)
    return w_k.reshape(KV_LORA, H * HP), w_v.reshape(KV_LORA, H * HP)


def _join_w_ukv(g_k, g_v):
    gk = g_k.reshape(KV_LORA, H // 2, 2, HP)
    gv = g_v.reshape(KV_LORA, H // 2, 2, HP)
    even = jnp.concatenate([gk[:, :, 0, :NOPE], gv[:, :, 0, :VDIM]], -1)
    odd = jnp.concatenate([gk[:, :, 1, :NOPE], gv[:, :, 1, VDIM:]], -1)
    return jnp.stack([even, odd], axis=2).reshape(KV_LORA, H * HP)


def _rope_tables():
    half = ROPE // 2
    inv = np.float32(ROPE_THETA) ** (-np.arange(half, dtype=np.float32) / np.float32(half))
    ang = np.arange(S, dtype=np.float32)[:, None] * inv[None, :].astype(np.float32)
    cos, sin = np.cos(ang).astype(np.float32), np.sin(ang).astype(np.float32)
    ones, zeros = np.ones((S, NOPE), np.float32), np.zeros((S, NOPE), np.float32)
    tail1, tail0 = np.ones((S, HP - NOPE - ROPE), np.float32), np.zeros((S, HP - NOPE - ROPE), np.float32)
    zh = np.zeros((S, half), np.float32)
    c = np.concatenate([ones, cos, cos, tail1], axis=1)
    sm = np.concatenate([zeros, -sin, zh, tail0], axis=1)
    sp = np.concatenate([zeros, zh, sin, tail0], axis=1)
    return jnp.asarray(c), jnp.asarray(sm), jnp.asarray(sp)


def _local_step(x, mod, target, g_norm1, w_in_p, g_cq, w_uq_p, g_ckv, w_k, w_v, rel_bias, g_out_a, g_out_b, w_out,
                g_norm2, w_ffn_in, w_ffn_out, g_final, late_weights=None, on_ffn_grads=None, on_last_grads=None):
    nb = x.shape[0] // S
    sh1, sc1, g1, sh2, sc2, g2 = (mod[:, n].reshape(nb, 1, D) for n in range(N_MOD))
    rc, rsm, rsp = _rope_tables()
    biasm = _band_bias(rel_bias)

    h1 = _pre1(x, g_norm1, sc1, sh1)
    proj = _mm_nt(h1, w_in_p, F32, "proj")
    q, k, v, cqn, ckvn = _mla_pre(proj, g_cq, g_ckv, w_uq_p, w_k, w_v, rc, rsm, rsp)
    out_b, lse_b = _mla_fwd(q, k, v)
    out_a, lse_a = _dil_fwd(proj, biasm)
    y = _post_attn(out_a, out_b, g_out_a, g_out_b)
    if late_weights is not None:
        w_out, w_ffn_in, w_ffn_out = late_weights(y)
    mix = _mm_nn(y, w_out, BF16, "mix")
    x2, h2 = _resid_norm2(x, mix, g1, g_norm2, sc2, sh2)
    ffn_g, ffn_u, act = _ffn_in(h2, w_ffn_in)
    f = _mm_nn(act, w_ffn_out, BF16, "ffn_out")
    dx3, df, loss_cols, dg_final, dg2 = _final(x2, f, g2, g_final, target)

    dg_, du_ = _d_act(df, w_ffn_out, ffn_g, ffn_u)
    gw_ffn_out = _mm_tn_rows([act], df, "gw_ffn_out")
    dh2 = _d_h2(dg_, du_, w_ffn_in)
    gw_ffn_in = _mm_tn_rows([dg_, du_], h2, "gw_ffn_in")
    dx2, dsh2, dsc2, dg_norm2, dg1, dmix = _norm_bwd(x2, dh2, dx3, g_norm2, sc2, gate=(mix, g1))
    dy = _mm_nt(dmix, w_out, BF16, "d_y")
    gw_out = _mm_tn(y, [dmix], "gw_out")
    if on_ffn_grads is not None:
        g_out_a = g_out_a + on_ffn_grads(gw_ffn_in, gw_ffn_out, gw_out)
    dout_a, dout_b, dg_out_a, dg_out_b = _post_attn_bwd(dy, out_a, out_b, g_out_a, g_out_b)
    dq_b, dk_b, dv_b = _mla_bwd(q, k, v, out_b, dout_b, lse_b)
    dq_a, dk_a, dv_a, dlogits = _dil_bwd(proj, biasm, out_a, dout_a, lse_a)
    dqr, dproj, dg_cq, dg_ckv = _mla_pre_bwd(proj, dq_b, dk_b, dv_b, (dq_a, dk_a, dv_a), g_cq, g_ckv, w_uq_p, w_k, w_v,
                                             rc, rsm, rsp)
    gw_uq = _mm_tn(dqr, [cqn], "gw_uq")
    gw_k, gw_v = _mm_tn(ckvn, [dk_b, dv_b], "gw_kv")
    gw_in = _mm_tn_rows([dproj], h1, "gw_in")
    if on_last_grads is not None:
        started = on_last_grads(dict(w_in=gw_in, w_uq=gw_uq, w_k=gw_k, w_v=gw_v))
    else:
        started = None
    g_rel = _rel_bias_grad(dlogits, after=started)
    dh1 = _mm_nn(dproj, w_in_p, BF16, "d_h1", after=started)
    grad_x, dsh1, dsc1, dg_norm1 = _norm_bwd(x, dh1, dx2, g_norm1, sc1)

    dmod = [dsh1, dsc1, dg1, dsh2, dsc2, dg2]
    small = dict(g_norm1=dg_norm1, g_cq=dg_cq, g_ckv=dg_ckv, rel_bias=g_rel, g_out_a=dg_out_a, g_out_b=dg_out_b,
                 g_norm2=dg_norm2, g_final=dg_final)
    big = dict(w_in=gw_in, w_uq=gw_uq, w_k=gw_k, w_v=gw_v, w_out=gw_out, w_ffn_in=gw_ffn_in, w_ffn_out=gw_ffn_out)
    return grad_x, dmod, loss_cols, small, big


def kernel(x, c, w_ada, b_ada, g_norm1, w_in, g_cq, w_uq, g_ckv, w_ukv, rel_bias, g_out_a, g_out_b, w_out, g_norm2, w_ffn_in, w_ffn_out, g_final, loss_target, m_w_ada, m_b_ada, m_g_norm1, m_w_in, m_g_cq, m_w_uq, m_g_ckv, m_w_ukv, m_rel_bias, m_g_out_a, m_g_out_b, m_w_out, m_g_norm2, m_w_ffn_in, m_w_ffn_out, m_g_final, v_w_ada, v_b_ada, v_g_norm1, v_w_in, v_g_cq, v_w_uq, v_g_ckv, v_w_ukv, v_rel_bias, v_g_out_a, v_g_out_b, v_w_out, v_g_norm2, v_w_ffn_in, v_w_ffn_out, v_g_final):
    nb = x.shape[0]
    t = nb * S
    xt, tt = x.reshape(t, D), loss_target.reshape(t, D)
    me = 4 * lax.axis_index("x") + 2 * lax.axis_index("y") + lax.axis_index("c")

    early = [jnp.swapaxes(w_in, 1, 2)[0], jnp.swapaxes(w_uq, 1, 2)[0], w_ukv[0]]
    gathered = _gather_two_level([_silu_rows(c)] + [s.astype(BF16) for s in early], "gather_weights")
    cond_all = gathered[0].reshape(N_DEV * nb, D)
    w_in_t = gathered[1].reshape(P_IN, D)
    w_ukv_f = _cols_from_blocks(gathered[3])
    w_k, w_v = _split_w_ukv(w_ukv_f)

    ncol = N_MOD * D // N_DEV
    b_slab = lax.dynamic_slice(b_ada, (0, me * ncol), (1, ncol))
    slab = _mod_slab(cond_all, w_ada, b_slab)
    (mod_rows,) = _exchange([slab.reshape(N_DEV, nb, ncol)], [False], "scatter_mod")
    mod = jnp.transpose(mod_rows, (1, 0, 2)).reshape(nb, N_MOD, D)

    late = [s.astype(BF16) for s in (w_out[0], jnp.swapaxes(w_ffn_in, 1, 2)[0], w_ffn_out[0])]
    late_send, late_recv, late_src, late_land, late_token = _exchange_start(
        late, [_own_block_in_place(s, me) for s in late], [True] * 3, mod_rows, "gather_late_start")
    g_norm1_t = g_norm1 + late_token[:1, :1]

    def late_weights(after):
        w_out_g, w_ffn_in_g, w_ffn_out_g = _exchange_wait(late_send, late_recv, late_src, late_land, [True] * 3, after,
                                                          "gather_late_wait")
        return w_out_g.reshape(D, D), w_ffn_in_g.reshape(2 * D_FF, D), w_ffn_out_g.reshape(D_FF, D)

    flight = {}

    def start_grads(key, src, name):
        land = [_own_block_in_place(lax.dynamic_index_in_dim(s, me, 0, keepdims=False), me) for s in src]
        send, recv, src, land, token = _exchange_start(src, land, [False] * len(src), None, name)
        flight[key] = (send, recv, src, land)
        return token[:1, :1]

    def on_ffn_grads(gw_ffn_in, gw_ffn_out, gw_out):
        return start_grads("ffn", [gw_ffn_in.reshape(N_DEV, 2 * D_FF // N_DEV, D), gw_ffn_out.reshape(N_DEV, D_FF // N_DEV, D),
                                   gw_out.reshape(N_DEV, D // N_DEV, D)], "exchange_ffn_start")

    def on_last_grads(gw):
        return start_grads("rest", [_unpad_w_in(gw["w_in"]).reshape(N_DEV, P_IN // N_DEV, D),
                                    _unpad_w_uq(gw["w_uq"]),
                                    _cols_to_blocks(_join_w_ukv(gw["w_k"], gw["w_v"]))], "exchange_rest_start")

    grad_x, dmod, loss_cols, small, _ = _local_step(
        xt, mod, tt, g_norm1_t, _pad_w_in(w_in_t), g_cq, _pad_w_uq(gathered[2]), g_ckv, w_k, w_v, rel_bias, g_out_a, g_out_b,
        None, g_norm2, None, None, g_final.reshape(1, D), late_weights=late_weights, on_ffn_grads=on_ffn_grads,
        on_last_grads=on_last_grads)

    mine, dmod_blocks = _pack_small(dmod, [small[n] for n, _ in ROW_PARAMS], loss_cols)
    small_src = [dmod_blocks, mine, small["rel_bias"]]
    small_gather = [False, True, True]
    small_land = [_own_block_in_place(lax.dynamic_index_in_dim(dmod_blocks, me, 0, keepdims=False), me),
                  _own_block_in_place(mine, me), _own_block_in_place(small["rel_bias"], me)]
    small_send, small_recv, small_src, small_land, _ = _exchange_start(small_src, small_land, small_gather, None,
                                                                       "exchange_small_start")

    upd = {}

    def land_and_update(key, names, after, name):
        got = _exchange_wait(*flight[key], [False] * len(names), after, name)
        for n, p in zip(names, got):
            w, m, v = big[n]
            upd[n] = _adamw(p, w, m, v, "adamw_" + n)

    def flip(a):
        return jnp.swapaxes(a, 1, 2)

    big = dict(w_in=(flip(w_in), flip(m_w_in), flip(v_w_in)), w_uq=(flip(w_uq), flip(m_w_uq), flip(v_w_uq)),
               w_ukv=(w_ukv, m_w_ukv, v_w_ukv),
               w_out=(w_out, m_w_out, v_w_out), w_ffn_in=(flip(w_ffn_in), flip(m_w_ffn_in), flip(v_w_ffn_in)),
               w_ffn_out=(w_ffn_out, m_w_ffn_out, v_w_ffn_out))
    land_and_update("ffn", ["w_ffn_in", "w_ffn_out", "w_out"], grad_x, "exchange_ffn_wait")
    land_and_update("rest", ["w_in", "w_uq", "w_ukv"], upd["w_out"][0], "exchange_rest_wait")
    for n in ("w_in", "w_uq", "w_ffn_in"):
        upd[n] = tuple(flip(a) for a in upd[n])

    dmod_cols, pay, rel = _exchange_wait(small_send, small_recv, small_src, small_land, small_gather, upd["w_ukv"][0],
                                         "exchange_small_wait")
    g_ada = _ada_grad(cond_all, dmod_cols.reshape(N_DEV * nb, ncol))
    upd["w_ada"] = _adamw(g_ada[None], w_ada, m_w_ada, v_w_ada, "adamw_w_ada")
    row = lambda a: a.reshape(1, D)
    small_names = ["b_ada"] + [n for n, _ in ROW_PARAMS] + ["rel_bias"]
    small_w = [b_ada, g_norm1, g_cq, g_ckv, g_out_a, g_out_b, g_norm2, row(g_final), rel_bias]
    small_m = [m_b_ada, m_g_norm1, m_g_cq, m_g_ckv, m_g_out_a, m_g_out_b, m_g_norm2, row(m_g_final), m_rel_bias]
    small_v = [v_b_ada, v_g_norm1, v_g_cq, v_g_ckv, v_g_out_a, v_g_out_b, v_g_norm2, row(v_g_final), v_rel_bias]
    small_upd, loss8 = _small_update(pay, rel, small_w, small_m, small_v)
    upd.update(zip(small_names, small_upd))

    order = ["w_ada", "b_ada", "g_norm1", "w_in", "g_cq", "w_uq", "g_ckv", "w_ukv", "rel_bias", "g_out_a", "g_out_b",
             "w_out", "g_norm2", "w_ffn_in", "w_ffn_out", "g_final"]
    like = dict(g_final=g_final)
    outs = [loss8[0, 0], grad_x.reshape(x.shape)]
    for part in range(4):
        for n in order:
            val = upd[n][part]
            outs.append(val.reshape(like[n].shape) if n in like else val)
    return tuple(outs)
```

```python
import numpy as np
import jax
import jax.numpy as jnp
from jax import lax
from jax.experimental import pallas as pl
from jax.experimental.pallas import tpu as pltpu

F32, BF16 = jnp.float32, jnp.bfloat16

N_DEV = 8
D = 1024
S = 2048
H = 8
E_A = 64
D_A = H * E_A
Q_LORA, KV_LORA = 384, 256
NOPE, ROPE, VDIM = 64, 32, 64
HP = 128
P_IN = 3 * D_A + Q_LORA + KV_LORA + ROPE
P_PAD = 3 * D_A + Q_LORA + KV_LORA + HP
TAIL0 = 3 * D_A
TAIL = P_PAD - TAIL0
D_FF = 2816
N_MOD = 6
EPS = 1e-6
NEG = -1e30
BLK = 128
DILATIONS = (1, 4, 16)
N_BUCKETS, MAX_DISTANCE = 32, 2048
ROPE_THETA = 10000.0
SCALE_A = E_A ** -0.5
SCALE_B = (NOPE + ROPE) ** -0.5
B1, B2, LR, ADAM_EPS, WD, STEP = 0.9, 0.999, 0.001, 1e-8, 0.01, 10
VMEM_LIMIT = 56 * 1024 * 1024


def _cp(*sem):
    return pltpu.CompilerParams(dimension_semantics=sem, vmem_limit_bytes=VMEM_LIMIT)


def _pick(n, prefs):
    for p in prefs:
        if n % p == 0:
            return p
    raise ValueError(f"no tile of {prefs} divides {n}")


OPERAND_BYTES = 6 * 1024 * 1024


def _pick_rows(m, k):
    return _pick(m, [p for p in (1024, 512, 256, 128, 16) if p * k * 2 <= OPERAND_BYTES])


MATMUL_BYTES = 40 * 1024 * 1024


def _stream_rows(m, fixed, per_row):
    return _pick(m, [p for p in (4096, 2048, 1024, 512, 256, 128, 16) if fixed + p * per_row <= MATMUL_BYTES])


def _dot(a, b, dims):
    return lax.dot_general(a, b, (dims, ((), ())), preferred_element_type=F32)


def _mm_nn(a, b, out_dtype, name, after=None):
    m, k = a.shape
    n = b.shape[1]
    tn = _pick(n, (512, 256, 384, 128))
    tm = _stream_rows(m, 4 * k * tn, 4 * k + (2 * jnp.dtype(out_dtype).itemsize + 4) * tn)

    def body(a_ref, b_ref, *rest):
        o_ref = rest[-1]
        o_ref[...] = _dot(a_ref[...], b_ref[...], ((1,), (0,))).astype(o_ref.dtype)

    extra = [] if after is None else [after]
    return pl.pallas_call(
        body, name=name, grid=(m // tm, n // tn),
        in_specs=[pl.BlockSpec((tm, k), lambda i, j: (i, 0)), pl.BlockSpec((k, tn), lambda i, j: (0, j))] + [ANY] * len(extra),
        out_specs=pl.BlockSpec((tm, tn), lambda i, j: (i, j)),
        out_shape=jax.ShapeDtypeStruct((m, n), out_dtype),
        compiler_params=_cp("parallel", "parallel"),
    )(a, b, *extra)


def _mm_nt(a, b, out_dtype, name, after=None):
    m, k = a.shape
    n = b.shape[0]
    tn = _pick(n, (512, 256, 384, 128))
    tm = _stream_rows(m, 4 * k * tn, 4 * k + (2 * jnp.dtype(out_dtype).itemsize + 4) * tn)

    def body(a_ref, b_ref, *rest):
        o_ref = rest[-1]
        o_ref[...] = _dot(a_ref[...], b_ref[...], ((1,), (1,))).astype(o_ref.dtype)

    extra = [] if after is None else [after]
    return pl.pallas_call(
        body, name=name, grid=(m // tm, n // tn),
        in_specs=[pl.BlockSpec((tm, k), lambda i, j: (i, 0)), pl.BlockSpec((tn, k), lambda i, j: (j, 0))] + [ANY] * len(extra),
        out_specs=pl.BlockSpec((tm, tn), lambda i, j: (i, j)),
        out_shape=jax.ShapeDtypeStruct((m, n), out_dtype),
        compiler_params=_cp("parallel", "parallel"),
    )(a, b, *extra)


def _mm_tn(a, bs, name):
    t, m = a.shape
    n = bs[0].shape[1]
    nb_ = len(bs)
    tc = _pick(t, (512, 16))
    tn = _pick(n, (512, 384, 256, 128))
    tm = _pick(m, [p for p in (1024, 512, 384, 256, 128) if (3 * p + 2 * nb_ * tn) * t * 2 <= VMEM_LIMIT - 2 * OPERAND_BYTES])
    if tm <= 256 and nb_ * n * t * 2 <= 2 * OPERAND_BYTES:
        tn = n

    def body(*refs):
        a_ref, b_refs, o_refs, at_ref = refs[0], refs[1:1 + nb_], refs[1 + nb_:1 + 2 * nb_], refs[-1]

        @pl.when(pl.program_id(1) == 0)
        def _():
            def chunk(c, _):
                rows = pl.ds(pl.multiple_of(c * tc, tc), tc)
                at_ref[:, rows] = a_ref[rows, :].T
                return 0

            lax.fori_loop(0, t // tc, chunk, 0)

        for b_ref, o_ref in zip(b_refs, o_refs):
            o_ref[...] = _dot(at_ref[...], b_ref[...], ((1,), (0,))).astype(BF16)

    res = pl.pallas_call(
        body, name=name, grid=(m // tm, n // tn),
        in_specs=[pl.BlockSpec((t, tm), lambda i, j: (0, i))] + [pl.BlockSpec((t, tn), lambda i, j: (0, j))] * nb_,
        out_specs=[pl.BlockSpec((tm, tn), lambda i, j: (i, j))] * nb_,
        out_shape=[jax.ShapeDtypeStruct((m, n), BF16)] * nb_,
        scratch_shapes=[pltpu.VMEM((tm, t), BF16)],
        compiler_params=_cp("parallel", "arbitrary"),
    )(a, *bs)
    return res[0] if nb_ == 1 else res


def _mm_tn_rows(a_list, b, name):
    t, m = a_list[0].shape
    n = b.shape[1]
    na = len(a_list)
    tc, tm = _pick(t, (512, 16)), _pick(m, (256, 128))
    nblk = m // tm

    def body(*refs):
        a_refs, b_ref, o_ref, bt_ref, r_ref = refs[:na], refs[na], refs[na + 1], refs[na + 2], refs[na + 3]
        i = pl.program_id(0)

        @pl.when(i == 0)
        def _():
            def chunk(c, _):
                rows = pl.ds(pl.multiple_of(c * tc, tc), tc)
                bt_ref[:, rows] = b_ref[rows, :].T
                return 0

            lax.fori_loop(0, t // tc, chunk, 0)

        for s, a_ref in enumerate(a_refs):
            @pl.when((i >= s * nblk) & (i < (s + 1) * nblk))
            def _(a_ref=a_ref):
                r_ref[...] = _dot(bt_ref[...], a_ref[...], ((1,), (0,)))
                o_ref[...] = r_ref[...].T.astype(BF16)

    return pl.pallas_call(
        body, name=name, grid=(na * nblk,),
        in_specs=[pl.BlockSpec((t, tm), lambda i, s=s: (0, jnp.clip(i - s * nblk, 0, nblk - 1))) for s in range(na)]
        + [pl.BlockSpec((t, n), lambda i: (0, 0))],
        out_specs=pl.BlockSpec((tm, n), lambda i: (i, 0)),
        out_shape=jax.ShapeDtypeStruct((na * m, n), BF16),
        scratch_shapes=[pltpu.VMEM((n, t), BF16), pltpu.VMEM((n, tm), F32)],
        compiler_params=_cp("arbitrary"),
    )(*a_list, b)


EPI = 256


def _silu_parts(g):
    sg = 0.5 * jnp.tanh(0.5 * g) + 0.5
    return sg, g * sg


def _ffn_in(h2, wt):
    t, k = h2.shape
    tn = _pick(D_FF, (256, 128))
    tm = _stream_rows(t, 8 * k * tn, 4 * k + (3 * 2 * 2 + 2 * 4) * tn)
    nj = D_FF // tn

    def body(h_ref, wg_ref, wu_ref, g_ref, u_ref, a_ref):
        hv = h_ref[...]
        g_all = _dot(hv, wg_ref[...], ((1,), (1,)))
        u_all = _dot(hv, wu_ref[...], ((1,), (1,)))
        for r in range(tm // EPI):
            rows = slice(r * EPI, (r + 1) * EPI)
            g, u = g_all[rows], u_all[rows]
            g_ref[rows, :] = g.astype(BF16)
            u_ref[rows, :] = u.astype(BF16)
            a_ref[rows, :] = (_silu_parts(g)[1] * u).astype(BF16)

    blk = pl.BlockSpec((tm, tn), lambda i, j: (i, j))
    return pl.pallas_call(
        body, name="ffn_in", grid=(t // tm, nj),
        in_specs=[pl.BlockSpec((tm, k), lambda i, j: (i, 0)), pl.BlockSpec((tn, k), lambda i, j: (j, 0)),
                  pl.BlockSpec((tn, k), lambda i, j: (j + nj, 0))],
        out_specs=[blk] * 3, out_shape=[jax.ShapeDtypeStruct((t, D_FF), BF16)] * 3,
        compiler_params=_cp("parallel", "parallel"),
    )(h2, wt, wt)


def _d_act(df, w, g, u):
    t, k = df.shape
    tn = _pick(D_FF, (256, 128))
    nj = D_FF // tn
    nin, nout = 3, 2

    def body(df_ref, w_hbm, g_hbm, u_hbm, dg_hbm, du_hbm, wbuf, gbuf, ubuf, ogbuf, oubuf, isem, osem):
        def cols(j):
            return pl.ds(j * tn if isinstance(j, int) else pl.multiple_of(j * tn, tn), tn)

        def fetch(j, slot):
            return (pltpu.make_async_copy(w_hbm.at[cols(j), :], wbuf.at[slot], isem.at[0, slot]),
                    pltpu.make_async_copy(g_hbm.at[:, cols(j)], gbuf.at[slot], isem.at[1, slot]),
                    pltpu.make_async_copy(u_hbm.at[:, cols(j)], ubuf.at[slot], isem.at[2, slot]))

        def put(j, slot):
            return (pltpu.make_async_copy(ogbuf.at[slot], dg_hbm.at[:, cols(j)], osem.at[0, slot]),
                    pltpu.make_async_copy(oubuf.at[slot], du_hbm.at[:, cols(j)], osem.at[1, slot]))

        for s in range(nin - 1):
            for cp in fetch(s, s):
                cp.start()

        def step(j, _):
            slot, oslot = j % nin, j % nout

            @pl.when(j + nin - 1 < nj)
            def _():
                for cp in fetch(j + nin - 1, (j + nin - 1) % nin):
                    cp.start()

            for cp in fetch(j, slot):
                cp.wait()

            @pl.when(j >= nout)
            def _():
                for cp in put(j - nout, oslot):
                    cp.wait()

            da_all = _dot(df_ref[...], wbuf[slot], ((1,), (1,)))
            for r in range(t // EPI):
                rows = slice(r * EPI, (r + 1) * EPI)
                da = da_all[rows]
                gv = gbuf[slot, rows, :].astype(F32)
                sg, silu = _silu_parts(gv)
                ogbuf[oslot, rows, :] = ((da * ubuf[slot, rows, :].astype(F32)) * (sg + silu * (1.0 - sg))).astype(BF16)
                oubuf[oslot, rows, :] = (da * silu).astype(BF16)
            for cp in put(j, oslot):
                cp.start()
            return 0

        lax.fori_loop(0, nj, step, 0)
        for j in range(nj - nout, nj):
            for cp in put(j, j % nout):
                cp.wait()

    return pl.pallas_call(
        body, name="d_act",
        in_specs=[pl.BlockSpec(memory_space=pltpu.VMEM), ANY, ANY, ANY], out_specs=[ANY, ANY],
        out_shape=[jax.ShapeDtypeStruct((t, D_FF), BF16)] * 2,
        scratch_shapes=[pltpu.VMEM((nin, tn, k), BF16), pltpu.VMEM((nin, t, tn), BF16), pltpu.VMEM((nin, t, tn), BF16),
                        pltpu.VMEM((nout, t, tn), BF16), pltpu.VMEM((nout, t, tn), BF16),
                        pltpu.SemaphoreType.DMA((3, nin)), pltpu.SemaphoreType.DMA((2, nout))],
        compiler_params=pltpu.CompilerParams(vmem_limit_bytes=VMEM_LIMIT),
    )(df, w, g, u)


def _d_h2(dg, du, wt):
    t = dg.shape[0]
    n = wt.shape[1]
    tm, tn = _pick_rows(t, D_FF), _pick(n, (512, 256, 128))

    def body(dg_ref, du_ref, wg_ref, wu_ref, o_ref):
        o_ref[...] = (_dot(dg_ref[...], wg_ref[...], ((1,), (0,)))
                      + _dot(du_ref[...], wu_ref[...], ((1,), (0,)))).astype(BF16)

    return pl.pallas_call(
        body, name="d_h2", grid=(t // tm, n // tn),
        in_specs=[pl.BlockSpec((tm, D_FF), lambda i, j: (i, 0)), pl.BlockSpec((tm, D_FF), lambda i, j: (i, 0)),
                  pl.BlockSpec((D_FF, tn), lambda i, j: (0, j)), pl.BlockSpec((D_FF, tn), lambda i, j: (1, j))],
        out_specs=pl.BlockSpec((tm, tn), lambda i, j: (i, j)),
        out_shape=jax.ShapeDtypeStruct((t, n), BF16),
        compiler_params=_cp("parallel", "parallel"),
    )(dg, du, wt, wt)


TM = 1024


def _row(w):
    return pl.BlockSpec((TM, w), lambda i: (i, 0))


def _row_at(w, col):
    return pl.BlockSpec((TM, w), lambda i: (i, col))


def _vec(w):
    return pl.BlockSpec((1, w), lambda i: (0, 0))


def _per_ex(w):
    return pl.BlockSpec((1, 1, w), lambda i: (i // (S // TM), 0, 0))


def _pos(w):
    return pl.BlockSpec((TM, w), lambda i: (i % (S // TM), 0))


def _full(shape):
    return pl.BlockSpec(shape, lambda i: (0,) * len(shape))


def _rms(x):
    return lax.rsqrt(jnp.mean(x * x, axis=-1, keepdims=True) + EPS)


def _rms_bwd(n, r, dn):
    return r * (dn - n * jnp.mean(dn * n, axis=-1, keepdims=True))


def _colsum(v):
    return jnp.sum(v, axis=0, keepdims=True)


def _acc_first(i, ref, val, every=None):
    first = (i == 0) if every is None else (i % every == 0)

    @pl.when(first)
    def _():
        ref[...] = jnp.zeros_like(ref)

    ref[...] += val.reshape(ref.shape)


def _pre1(x, g, sc, sh):
    t = x.shape[0]

    def body(x_ref, g_ref, sc_ref, sh_ref, h_ref):
        xv = x_ref[...]
        n = xv * _rms(xv)
        h_ref[...] = ((n * g_ref[...]) * (1.0 + sc_ref[0]) + sh_ref[0]).astype(BF16)

    return pl.pallas_call(
        body, name="pre1", grid=(t // TM,),
        in_specs=[_row(D), _vec(D), _per_ex(D), _per_ex(D)],
        out_specs=_row(D), out_shape=jax.ShapeDtypeStruct((t, D), BF16),
        compiler_params=_cp("parallel"),
    )(x, g, sc, sh)


def _rope_fwd(v, c, sm, sp):
    return v * c + pltpu.roll(v, HP - ROPE // 2, 1) * sm + pltpu.roll(v, ROPE // 2, 1) * sp


def _rope_bwd(dv, c, sm, sp):
    return dv * c + pltpu.roll(dv * sm, ROPE // 2, 1) + pltpu.roll(dv * sp, HP - ROPE // 2, 1)


def _mla_pre(proj, g_cq, g_ckv, w_uq, w_k, w_v, rc, rsm, rsp):
    t = proj.shape[0]

    def body(tail_ref, gq_ref, gkv_ref, wuq_ref, wk_ref, wv_ref, c_ref, sm_ref, sp_ref,
             q_ref, k_ref, v_ref, cqn_ref, ckvn_ref):
        tail = tail_ref[...]
        cq, ckv, kr = tail[:, :Q_LORA], tail[:, Q_LORA:Q_LORA + KV_LORA], tail[:, Q_LORA + KV_LORA:]
        cqn = (cq * _rms(cq) * gq_ref[...]).astype(BF16)
        ckvn = (ckv * _rms(ckv) * gkv_ref[...]).astype(BF16)
        cqn_ref[...] = cqn
        ckvn_ref[...] = ckvn
        c, sm, sp = c_ref[...], sm_ref[...], sp_ref[...]
        q = _dot(cqn, wuq_ref[...], ((1,), (1,)))
        kn = _dot(ckvn, wk_ref[...], ((1,), (0,)))
        v_ref[...] = _dot(ckvn, wv_ref[...], ((1,), (0,))).astype(BF16)
        krr = _rope_fwd(kr, c, sm, sp)
        for h in range(H):
            sl = slice(h * HP, (h + 1) * HP)
            q_ref[:, sl] = _rope_fwd(q[:, sl], c, sm, sp).astype(BF16)
            k_ref[:, sl] = (kn[:, sl] + krr).astype(BF16)

    wide = H * HP
    return pl.pallas_call(
        body, name="mla_pre", grid=(t // TM,),
        in_specs=[_row_at(TAIL, TAIL0 // TAIL), _vec(Q_LORA), _vec(KV_LORA), _full((wide, Q_LORA)),
                  _full((KV_LORA, wide)), _full((KV_LORA, wide)), _pos(HP), _pos(HP), _pos(HP)],
        out_specs=[_row(wide), _row(wide), _row(wide), _row(Q_LORA), _row(KV_LORA)],
        out_shape=[jax.ShapeDtypeStruct((t, wide), BF16)] * 3
        + [jax.ShapeDtypeStruct((t, Q_LORA), BF16), jax.ShapeDtypeStruct((t, KV_LORA), BF16)],
        compiler_params=_cp("parallel"),
    )(proj, g_cq, g_ckv, w_uq, w_k, w_v, rc, rsm, rsp)


def _mla_pre_bwd(proj, dq_, dk_, dv_, dqkv_a, g_cq, g_ckv, w_uq, w_k, w_v, rc, rsm, rsp):
    t = proj.shape[0]
    wide = H * HP

    def body(tail_ref, dq_ref, dk_ref, dv_ref, dqa_ref, dka_ref, dva_ref, gq_ref, gkv_ref, wuq_ref, wk_ref, wv_ref,
             c_ref, sm_ref, sp_ref, dqo_ref, dproj_ref, dgq_ref, dgkv_ref):
        i = pl.program_id(0)
        for n, src in enumerate((dqa_ref, dka_ref, dva_ref)):
            dproj_ref[:, n * D_A:(n + 1) * D_A] = src[...]
        dtail_ref = dproj_ref.at[:, TAIL0:]
        tail = tail_ref[...]
        cq, ckv = tail[:, :Q_LORA], tail[:, Q_LORA:Q_LORA + KV_LORA]
        c, sm, sp = c_ref[...], sm_ref[...], sp_ref[...]
        dkr = jnp.zeros((TM, HP), F32)
        for h in range(H):
            sl = slice(h * HP, (h + 1) * HP)
            dqo_ref[:, sl] = _rope_bwd(dq_ref[:, sl].astype(F32), c, sm, sp).astype(BF16)
            dkr = dkr + dk_ref[:, sl].astype(F32)
        lane = lax.broadcasted_iota(jnp.int32, (TM, HP), 1)
        dkr = jnp.where((lane >= NOPE) & (lane < NOPE + ROPE), _rope_bwd(dkr, c, sm, sp), 0.0)
        dkb = dk_ref[...]
        dvb = dv_ref[...]
        dcqn = _dot(dqo_ref[...], wuq_ref[...], ((1,), (0,)))
        dckvn = _dot(dkb, wk_ref[...], ((1,), (1,))) + _dot(dvb, wv_ref[...], ((1,), (1,)))
        rq, rkv = _rms(cq), _rms(ckv)
        nq, nkv = cq * rq, ckv * rkv
        _acc_first(i, dgq_ref, _colsum(dcqn * nq))
        _acc_first(i, dgkv_ref, _colsum(dckvn * nkv))
        dtail_ref[:, :Q_LORA] = _rms_bwd(nq, rq, dcqn * gq_ref[...]).astype(BF16)
        dtail_ref[:, Q_LORA:Q_LORA + KV_LORA] = _rms_bwd(nkv, rkv, dckvn * gkv_ref[...]).astype(BF16)
        dtail_ref[:, Q_LORA + KV_LORA:] = dkr.astype(BF16)

    return pl.pallas_call(
        body, name="mla_pre_bwd", grid=(t // TM,),
        in_specs=[_row_at(TAIL, TAIL0 // TAIL), _row(wide), _row(wide), _row(wide), _row(D_A), _row(D_A), _row(D_A),
                  _vec(Q_LORA), _vec(KV_LORA), _full((wide, Q_LORA)), _full((KV_LORA, wide)), _full((KV_LORA, wide)),
                  _pos(HP), _pos(HP), _pos(HP)],
        out_specs=[_row(wide), _row(P_PAD), _vec(Q_LORA), _vec(KV_LORA)],
        out_shape=[jax.ShapeDtypeStruct((t, wide), BF16), jax.ShapeDtypeStruct((t, P_PAD), BF16),
                   jax.ShapeDtypeStruct((1, Q_LORA), F32), jax.ShapeDtypeStruct((1, KV_LORA), F32)],
        compiler_params=_cp("arbitrary"),
    )(proj, dq_, dk_, dv_, *dqkv_a, g_cq, g_ckv, w_uq, w_k, w_v, rc, rsm, rsp)


def _post_attn(out_a, out_b, g_a, g_b):
    t = out_a.shape[0]

    def body(a_ref, b_ref, ga_ref, gb_ref, y_ref):
        a, b = a_ref[...], b_ref[...]
        y_ref[:, :D_A] = (a * _rms(a) * ga_ref[...]).astype(BF16)
        y_ref[:, D_A:] = (b * _rms(b) * gb_ref[...]).astype(BF16)

    return pl.pallas_call(
        body, name="post_attn", grid=(t // TM,),
        in_specs=[_row(D_A), _row(D_A), _vec(D_A), _vec(D_A)],
        out_specs=_row(D), out_shape=jax.ShapeDtypeStruct((t, D), BF16),
        compiler_params=_cp("parallel"),
    )(out_a, out_b, g_a, g_b)


def _post_attn_bwd(dy, out_a, out_b, g_a, g_b):
    t = dy.shape[0]

    def body(dy_ref, a_ref, b_ref, ga_ref, gb_ref, da_ref, db_ref, dga_ref, dgb_ref):
        i = pl.program_id(0)
        dy_ = dy_ref[...].astype(F32)
        for src, g_ref, dst, dg_ref, sl in ((a_ref, ga_ref, da_ref, dga_ref, slice(0, D_A)),
                                            (b_ref, gb_ref, db_ref, dgb_ref, slice(D_A, D))):
            v = src[...]
            r = _rms(v)
            n = v * r
            dyv = dy_[:, sl]
            _acc_first(i, dg_ref, _colsum(dyv * n))
            dst[...] = _rms_bwd(n, r, dyv * g_ref[...])

    return pl.pallas_call(
        body, name="post_attn_bwd", grid=(t // TM,),
        in_specs=[_row(D), _row(D_A), _row(D_A), _vec(D_A), _vec(D_A)],
        out_specs=[_row(D_A), _row(D_A), _vec(D_A), _vec(D_A)],
        out_shape=[jax.ShapeDtypeStruct((t, D_A), F32)] * 2 + [jax.ShapeDtypeStruct((1, D_A), F32)] * 2,
        compiler_params=_cp("arbitrary"),
    )(dy, out_a, out_b, g_a, g_b)


def _resid_norm2(x, mix, g1, g, sc, sh):
    t = x.shape[0]

    def body(x_ref, mix_ref, g1_ref, g_ref, sc_ref, sh_ref, x2_ref, h_ref):
        x2 = x_ref[...] + g1_ref[0] * mix_ref[...]
        x2_ref[...] = x2
        n = x2 * _rms(x2)
        h_ref[...] = ((n * g_ref[...]) * (1.0 + sc_ref[0]) + sh_ref[0]).astype(BF16)

    return pl.pallas_call(
        body, name="resid_norm2", grid=(t // TM,),
        in_specs=[_row(D), _row(D), _per_ex(D), _vec(D), _per_ex(D), _per_ex(D)],
        out_specs=[_row(D), _row(D)],
        out_shape=[jax.ShapeDtypeStruct((t, D), F32), jax.ShapeDtypeStruct((t, D), BF16)],
        compiler_params=_cp("parallel"),
    )(x, mix, g1, g, sc, sh)


def _sigmoid(v):
    return 1.0 / (1.0 + jnp.exp(-v))


def _final(x2, f, g2, g_fin, target):
    t = x2.shape[0]
    nb = t // S
    tpb = S // TM

    def body(x2_ref, f_ref, g2_ref, g_ref, t_ref, dx3_ref, df_ref, loss_ref, dgf_ref, dg2_ref):
        i = pl.program_id(0)
        fv = f_ref[...].astype(F32)
        x3 = x2_ref[...] + g2_ref[0] * fv
        r = _rms(x3)
        n = x3 * r
        err = n * g_ref[...] - t_ref[...]
        _acc_first(i, loss_ref, _colsum(err * err))
        dy = err * (1.0 / D)
        _acc_first(i, dgf_ref, _colsum(dy * n))
        dx3 = _rms_bwd(n, r, dy * g_ref[...])
        dx3_ref[...] = dx3.astype(BF16)
        _acc_first(i, dg2_ref, _colsum(dx3 * fv), every=tpb)
        df_ref[...] = (dx3 * g2_ref[0]).astype(BF16)

    return pl.pallas_call(
        body, name="final", grid=(t // TM,),
        in_specs=[_row(D), _row(D), _per_ex(D), _vec(D), _row(D)],
        out_specs=[_row(D), _row(D), _vec(D), _vec(D), _per_ex(D)],
        out_shape=[jax.ShapeDtypeStruct((t, D), BF16), jax.ShapeDtypeStruct((t, D), BF16),
                   jax.ShapeDtypeStruct((1, D), F32), jax.ShapeDtypeStruct((1, D), F32),
                   jax.ShapeDtypeStruct((nb, 1, D), F32)],
        compiler_params=_cp("arbitrary"),
    )(x2, f, g2, g_fin, target)


def _norm_bwd(xin, dh, dres, g, sc, gate=None):
    t = xin.shape[0]
    nb = t // S
    tpb = S // TM
    gated = gate is not None

    def body(*refs):
        if gated:
            x_ref, dh_ref, dres_ref, g_ref, sc_ref, mix_ref, g1_ref, dx_ref, dsh_ref, dsc_ref, dg_ref, dg1_ref, dmix_ref = refs
        else:
            x_ref, dh_ref, dres_ref, g_ref, sc_ref, dx_ref, dsh_ref, dsc_ref, dg_ref = refs
        i = pl.program_id(0)
        xv, dhv = x_ref[...], dh_ref[...].astype(F32)
        r = _rms(xv)
        n = xv * r
        gv = g_ref[...]
        _acc_first(i, dsh_ref, _colsum(dhv), every=tpb)
        _acc_first(i, dsc_ref, _colsum(dhv * (n * gv)), every=tpb)
        dng = dhv * (1.0 + sc_ref[0])
        _acc_first(i, dg_ref, _colsum(dng * n))
        dx = dres_ref[...].astype(F32) + _rms_bwd(n, r, dng * gv)
        dx_ref[...] = dx.astype(dx_ref.dtype)
        if gated:
            _acc_first(i, dg1_ref, _colsum(dx * mix_ref[...].astype(F32)), every=tpb)
            dmix_ref[...] = (dx * g1_ref[0]).astype(BF16)

    in_specs = [_row(D), _row(D), _row(D), _vec(D), _per_ex(D)]
    out_specs = [_row(D), _per_ex(D), _per_ex(D), _vec(D)]
    out_shape = [jax.ShapeDtypeStruct((t, D), BF16 if gated else F32), jax.ShapeDtypeStruct((nb, 1, D), F32),
                 jax.ShapeDtypeStruct((nb, 1, D), F32), jax.ShapeDtypeStruct((1, D), F32)]
    args = [xin, dh, dres, g, sc]
    if gated:
        in_specs += [_row(D), _per_ex(D)]
        out_specs += [_per_ex(D), _row(D)]
        out_shape += [jax.ShapeDtypeStruct((nb, 1, D), F32), jax.ShapeDtypeStruct((t, D), BF16)]
        args += list(gate)
    return pl.pallas_call(
        body, name="norm2_bwd" if gated else "norm1_bwd", grid=(t // TM,),
        in_specs=in_specs, out_specs=out_specs, out_shape=out_shape,
        compiler_params=_cp("arbitrary"),
    )(*args)


TQ = 256
TB = 512
FWD_HEADS = 2


def _mla_fwd(q, k, v):
    t = q.shape[0]
    nb = t // S

    def body(q_ref, k_ref, v_ref, o_ref, lse_ref):
        causal = lax.broadcasted_iota(jnp.int32, (TB, TB), 0) >= lax.broadcasted_iota(jnp.int32, (TB, TB), 1)
        heads = [slice(h * HP, (h + 1) * HP) for h in range(FWD_HEADS)]
        for i in range(S // TB):
            ri, past = slice(i * TB, (i + 1) * TB), slice(0, i * TB)
            qhs = [q_ref[ri, sl] for sl in heads]
            sd = [jnp.where(causal, _dot(qh, k_ref[ri, sl], ((1,), (1,))) * SCALE_B, NEG) for qh, sl in zip(qhs, heads)]
            ms = [jnp.max(s, axis=-1, keepdims=True) for s in sd]
            if i:
                so = [_dot(qh, k_ref[past, sl], ((1,), (1,))) * SCALE_B for qh, sl in zip(qhs, heads)]
                ms = [jnp.maximum(m, jnp.max(s, axis=-1, keepdims=True)) for m, s in zip(ms, so)]
            pd = [jnp.exp(s - m) for s, m in zip(sd, ms)]
            ls = [jnp.sum(p, axis=-1, keepdims=True) for p in pd]
            acc = [_dot(p.astype(BF16), v_ref[ri, sl], ((1,), (0,))) for p, sl in zip(pd, heads)]
            if i:
                po = [jnp.exp(s - m) for s, m in zip(so, ms)]
                ls = [l + jnp.sum(p, axis=-1, keepdims=True) for l, p in zip(ls, po)]
                acc = [a + _dot(p.astype(BF16), v_ref[past, sl], ((1,), (0,))) for a, p, sl in zip(acc, po, heads)]
            for pr in range(FWD_HEADS // 2):
                o_ref[ri, pr * HP:(pr + 1) * HP] = acc[2 * pr] / ls[2 * pr] + acc[2 * pr + 1] / ls[2 * pr + 1]
            for sl, m, l in zip(heads, ms, ls):
                lse_ref[ri, sl] = jnp.broadcast_to(m + jnp.log(l), (TB, HP))

    wide2 = pl.BlockSpec((S, FWD_HEADS * HP), lambda b, p: (b, p))
    return pl.pallas_call(
        body, name="mla_fwd", grid=(nb, H // FWD_HEADS),
        in_specs=[wide2, wide2, wide2],
        out_specs=[pl.BlockSpec((S, FWD_HEADS // 2 * HP), lambda b, p: (b, p)), wide2],
        out_shape=[jax.ShapeDtypeStruct((t, H * VDIM), F32), jax.ShapeDtypeStruct((t, H * HP), F32)],
        compiler_params=_cp("parallel", "parallel"),
    )(q, k, v)


def _mla_bwd(q, k, v, o, do, lse):
    t = q.shape[0]
    nb = t // S

    def body(q_ref, k_ref, v_ref, o_ref, do_ref, lse_ref, dq_out, dk_out, dv_out, dq_ref, dk_ref, dv_ref):
        lane = lax.broadcasted_iota(jnp.int32, (TB, HP), 1)
        causal = lax.broadcasted_iota(jnp.int32, (TB, TB), 0) >= lax.broadcasted_iota(jnp.int32, (TB, TB), 1)
        heads = [slice(h * HP, (h + 1) * HP) for h in range(2)]
        nblk = S // TB
        for i in reversed(range(nblk)):
            ri, past = slice(i * TB, (i + 1) * TB), slice(0, i * TB)
            dov = do_ref[ri, :]
            prod = dov * o_ref[ri, :]
            dob = dov.astype(BF16)
            deltas = [jnp.sum(jnp.where((lane < VDIM) if h == 0 else (lane >= VDIM), prod, 0.0), axis=-1, keepdims=True)
                      for h in range(2)]
            qhs = [q_ref[ri, sl] for sl in heads]
            lses = [lse_ref[ri, sl][:, :1] for sl in heads]
            for rows, diagonal in ((ri, True), (past, False)):
                if rows.stop == rows.start:
                    continue
                ps = [jnp.exp(_dot(qh, k_ref[rows, sl], ((1,), (1,))) * SCALE_B - lse) for qh, sl, lse in zip(qhs, heads, lses)]
                if diagonal:
                    ps = [jnp.where(causal, p, 0.0) for p in ps]
                dps = [_dot(dob, v_ref[rows, sl], ((1,), (1,))) for sl in heads]
                dss = [(p * (dp - delta) * SCALE_B).astype(BF16) for p, dp, delta in zip(ps, dps, deltas)]
                for sl, qh, p, ds in zip(heads, qhs, ps, dss):
                    dq = _dot(ds, k_ref[rows, sl], ((1,), (0,)))
                    dk = _dot(ds, qh, ((0,), (0,)))
                    dv = _dot(p.astype(BF16), dob, ((0,), (0,)))
                    if diagonal:
                        dq_ref[ri, sl] = dq
                    else:
                        dq_ref[ri, sl] += dq
                    if i == nblk - 1:
                        dk_ref[rows, sl] = dk
                        dv_ref[rows, sl] = dv
                    else:
                        dk_ref[rows, sl] += dk
                        dv_ref[rows, sl] += dv
        dq_out[...] = dq_ref[...].astype(BF16)
        dk_out[...] = dk_ref[...].astype(BF16)
        dv_out[...] = dv_ref[...].astype(BF16)

    wide2 = pl.BlockSpec((S, 2 * HP), lambda b, p: (b, p))
    pair = pl.BlockSpec((S, HP), lambda b, p: (b, p))
    return pl.pallas_call(
        body, name="mla_bwd", grid=(nb, H // 2),
        in_specs=[wide2, wide2, wide2, pair, pair, wide2],
        out_specs=[wide2, wide2, wide2],
        out_shape=[jax.ShapeDtypeStruct((t, H * HP), BF16)] * 3,
        scratch_shapes=[pltpu.VMEM((S, 2 * HP), F32)] * 3,
        compiler_params=_cp("parallel", "parallel"),
    )(q, k, v, o, do, lse)


def _t5_bucket(dist):
    max_exact = N_BUCKETS // 2
    d = np.maximum(dist, 1).astype(np.float64)
    large = max_exact + (np.log(d / max_exact) / np.log(MAX_DISTANCE / max_exact) * (N_BUCKETS - max_exact)).astype(np.int64)
    large = np.minimum(large, N_BUCKETS - 1)
    return np.where(dist < max_exact, dist, large).astype(np.int32)


def _band_geometry():
    a = np.arange(BLK)[:, None]
    bk = np.arange(2 * BLK)[None, :]
    steps = BLK + a - bk
    valid = (steps >= 0) & (steps <= BLK)
    buckets = np.stack([_t5_bucket(np.clip(steps, 0, BLK) * d) for d in DILATIONS])
    return buckets, valid


def _band_bias(rel_bias):
    buckets, valid = _band_geometry()
    onehot = (jnp.asarray(buckets)[..., None] == jnp.arange(N_BUCKETS)).astype(F32)
    bias = jnp.einsum("rqkn,nh->rhqk", onehot, rel_bias, precision=lax.Precision.HIGHEST)
    bias = jnp.where(jnp.asarray(valid)[None, None], bias, NEG)
    return bias.reshape(3, H // 2, 2 * BLK, 2 * BLK)


def _dil_items():
    items = []
    for r, d in enumerate(DILATIONS):
        for res in range(d):
            for blk in range(S // d // BLK):
                items.append((r, d, blk * BLK * d + res, blk > 0))
    return items


GROUP = 4


def _strided(start, d):
    return pl.ds(start, BLK) if d == 1 else pl.ds(start, BLK, stride=d)


def _stack_heads(tile, own):
    return jnp.where(own, jnp.concatenate([tile, tile], axis=0), 0.0).astype(BF16)


def _own_lanes():
    row = lax.broadcasted_iota(jnp.int32, (2 * BLK, HP), 0)
    lane = lax.broadcasted_iota(jnp.int32, (2 * BLK, HP), 1)
    return (lane < E_A) == (row < BLK)


def _dil_fwd(proj, biasm):
    t = proj.shape[0]
    nb = t // S

    def body(q_ref, k_ref, v_ref, b_ref, o_ref, lse_ref, ob_ref, lb_ref):
        lane = lax.broadcasted_iota(jnp.int32, (BLK, HP), 1)
        own = _own_lanes()
        items = _dil_items()
        for g in range(0, len(items), GROUP):
            grp = items[g:g + GROUP]
            ss, vts = [], []
            for r, d, start, has_prev in grp:
                cur = _strided(start, d)
                rows = [_strided(start - BLK * d, d), cur] if has_prev else [cur]
                q2 = _stack_heads(q_ref[cur, :] * SCALE_A, own)
                kt = jnp.concatenate([k_ref[x, :] for x in rows], axis=0).astype(BF16)
                vts.append(jnp.concatenate([v_ref[x, :] for x in rows], axis=0).astype(BF16))
                bias = b_ref[r, 0] if has_prev else b_ref[r, 0, :, BLK:]
                ss.append(_dot(q2, kt, ((1,), (1,))) + bias)
            ms = [jnp.max(s, axis=-1, keepdims=True) for s in ss]
            ps = [jnp.exp(s - m) for s, m in zip(ss, ms)]
            ls = [jnp.sum(p, axis=-1, keepdims=True) for p in ps]
            for (r, d, start, _), p, vt, m, l in zip(grp, ps, vts, ms, ls):
                cur = _strided(start, d)
                o2 = _dot(p.astype(BF16), vt, ((1,), (0,))) / l
                lse2 = m + jnp.log(l)
                ob_ref[r, cur, :] = jnp.where(lane < E_A, o2[:BLK], o2[BLK:])
                lb_ref[r, cur, :] = jnp.where(lane < E_A, lse2[:BLK], lse2[BLK:])

        def merge(c, _):
            rows = pl.ds(pl.multiple_of(c * TQ, TQ), TQ)
            l0, l1, l2 = lb_ref[0, rows, :], lb_ref[1, rows, :], lb_ref[2, rows, :]
            m = jnp.maximum(jnp.maximum(l0, l1), l2)
            e0, e1, e2 = jnp.exp(l0 - m), jnp.exp(l1 - m), jnp.exp(l2 - m)
            tot = e0 + e1 + e2
            o_ref[rows, :] = (e0 * ob_ref[0, rows, :] + e1 * ob_ref[1, rows, :] + e2 * ob_ref[2, rows, :]) / tot
            lse_ref[rows, :] = m + jnp.log(tot)
            return 0

        lax.fori_loop(0, S // TQ, merge, 0)

    npair = H // 2
    return pl.pallas_call(
        body, name="dil_fwd", grid=(nb, npair),
        in_specs=[pl.BlockSpec((S, HP), lambda b, p: (b, p)), pl.BlockSpec((S, HP), lambda b, p: (b, npair + p)),
                  pl.BlockSpec((S, HP), lambda b, p: (b, 2 * npair + p)),
                  pl.BlockSpec((3, 1, 2 * BLK, 2 * BLK), lambda b, p: (0, p, 0, 0))],
        out_specs=[pl.BlockSpec((S, HP), lambda b, p: (b, p))] * 2,
        out_shape=[jax.ShapeDtypeStruct((t, D_A), F32)] * 2,
        scratch_shapes=[pltpu.VMEM((3, S, HP), F32), pltpu.VMEM((3, S, HP), F32)],
        compiler_params=_cp("parallel", "parallel"),
    )(proj, proj, proj, biasm)


def _dil_bwd(proj, biasm, o, do, lse):
    t = proj.shape[0]
    nb = t // S

    def body(q_ref, k_ref, v_ref, b_ref, o_ref, do_ref, lse_ref, dq_out, dk_out, dv_out, ds_ref, dq_ref, dk_ref, dv_ref):
        dq_ref[...] = jnp.zeros_like(dq_ref)
        dk_ref[...] = jnp.zeros_like(dk_ref)
        dv_ref[...] = jnp.zeros_like(dv_ref)
        ds_ref[...] = jnp.zeros_like(ds_ref)
        lane = lax.broadcasted_iota(jnp.int32, (BLK, HP), 1)
        own = _own_lanes()
        items = _dil_items()
        for g in range(0, len(items), GROUP):
            grp = items[g:g + GROUP]
            q2s, kts, do2s, ss, dps, lse2s, delta2s = [], [], [], [], [], [], []
            for r, d, start, has_prev in grp:
                cur = _strided(start, d)
                rows = [_strided(start - BLK * d, d), cur] if has_prev else [cur]
                q2 = _stack_heads(q_ref[cur, :] * SCALE_A, own)
                kt = jnp.concatenate([k_ref[x, :] for x in rows], axis=0).astype(BF16)
                vt = jnp.concatenate([v_ref[x, :] for x in rows], axis=0).astype(BF16)
                dot_ = do_ref[cur, :]
                prod = dot_ * o_ref[cur, :]
                lset = lse_ref[cur, :]
                do2 = _stack_heads(dot_, own)
                bias = b_ref[r, 0] if has_prev else b_ref[r, 0, :, BLK:]
                ss.append(_dot(q2, kt, ((1,), (1,))) + bias)
                dps.append(_dot(do2, vt, ((1,), (1,))))
                lse2s.append(jnp.concatenate([lset[:, :1], lset[:, E_A:E_A + 1]], axis=0))
                delta2s.append(jnp.concatenate([jnp.sum(jnp.where(lane < E_A, prod, 0.0), axis=-1, keepdims=True),
                                                jnp.sum(jnp.where(lane >= E_A, prod, 0.0), axis=-1, keepdims=True)], axis=0))
                q2s.append(q2)
                kts.append(kt)
                do2s.append(do2)
            ps = [jnp.exp(s - lse2) for s, lse2 in zip(ss, lse2s)]
            dls = [p * (dp - delta2) for p, dp, delta2 in zip(ps, dps, delta2s)]
            for (r, d, start, has_prev), q2, kt, do2, p, dl in zip(grp, q2s, kts, do2s, ps, dls):
                cur = _strided(start, d)
                dsb = dl.astype(BF16)
                dq2 = _dot(dsb, kt, ((1,), (0,))) * SCALE_A
                dkt = _dot(dsb, q2, ((0,), (0,)))
                dvt = _dot(p.astype(BF16), do2, ((0,), (0,)))
                dq_ref[cur, :] += jnp.where(lane < E_A, dq2[:BLK], dq2[BLK:])
                if has_prev:
                    prev = _strided(start - BLK * d, d)
                    ds_ref[0, r, 0] += dl
                    dk_ref[prev, :] += dkt[:BLK]
                    dv_ref[prev, :] += dvt[:BLK]
                    dk_ref[cur, :] += dkt[BLK:]
                    dv_ref[cur, :] += dvt[BLK:]
                else:
                    ds_ref[0, r, 0, :, BLK:] += dl
                    dk_ref[cur, :] += dkt
                    dv_ref[cur, :] += dvt
        dq_out[...] = dq_ref[...].astype(BF16)
        dk_out[...] = dk_ref[...].astype(BF16)
        dv_out[...] = dv_ref[...].astype(BF16)

    npair = H // 2
    pair = pl.BlockSpec((S, HP), lambda b, p: (b, p))
    return pl.pallas_call(
        body, name="dil_bwd", grid=(nb, npair),
        in_specs=[pair, pl.BlockSpec((S, HP), lambda b, p: (b, npair + p)),
                  pl.BlockSpec((S, HP), lambda b, p: (b, 2 * npair + p)),
                  pl.BlockSpec((3, 1, 2 * BLK, 2 * BLK), lambda b, p: (0, p, 0, 0)), pair, pair, pair],
        out_specs=[pair, pair, pair, pl.BlockSpec((1, 3, 1, 2 * BLK, 2 * BLK), lambda b, p: (b, 0, p, 0, 0))],
        out_shape=[jax.ShapeDtypeStruct((t, D_A), BF16)] * 3 + [jax.ShapeDtypeStruct((nb, 3, npair, 2 * BLK, 2 * BLK), F32)],
        scratch_shapes=[pltpu.VMEM((S, HP), F32)] * 3,
        compiler_params=_cp("parallel", "parallel"),
    )(proj, proj, proj, biasm, o, do, lse)


def _rel_bias_grad(dlogits):
    nb = dlogits.shape[0]
    buckets, _ = _band_geometry()
    kk = 3 * BLK * 2 * BLK
    dl = jnp.transpose(dlogits.reshape(nb, 3, H, BLK, 2 * BLK), (0, 2, 1, 3, 4)).reshape(nb, H, kk)
    bk = jnp.asarray(buckets.reshape(1, kk))
    tk = kk // 4

    def body(dl_ref, bk_ref, o_ref):
        j = pl.program_id(0)
        onehot = (bk_ref[...] == lax.broadcasted_iota(jnp.int32, (N_BUCKETS, tk), 0)).astype(F32)
        tot = dl_ref[0]
        for b in range(1, nb):
            tot = tot + dl_ref[b]
        part = lax.dot_general(onehot, tot, ((((1,), (1,))), ((), ())), preferred_element_type=F32,
                               precision=lax.Precision.HIGHEST)
        _acc_first(j, o_ref, part)

    return pl.pallas_call(
        body, name="rel_bias_grad", grid=(kk // tk,),
        in_specs=[pl.BlockSpec((nb, H, tk), lambda j: (0, 0, j)), pl.BlockSpec((1, tk), lambda j: (0, j))],
        out_specs=pl.BlockSpec((N_BUCKETS, H), lambda j: (0, 0)),
        out_shape=jax.ShapeDtypeStruct((N_BUCKETS, H), F32),
        compiler_params=_cp("arbitrary"),
    )(dl, bk)


def _mesh_place():
    x, y, c = lax.axis_index("x"), lax.axis_index("y"), lax.axis_index("c")
    return x, y, c


def _peer(k):
    x, y, c = _mesh_place()
    px = 1 - x if k & 4 else x
    py = 1 - y if k & 2 else y
    pc = 1 - c if k & 1 else c
    return (px, py, pc), 4 * px + 2 * py + pc


ANY = pl.BlockSpec(memory_space=pl.ANY)


def _exchange(arrays, gathers, name, after=None):
    n_arr = len(arrays)

    def body(*refs):
        ins, outs = refs[:n_arr], refs[n_arr + 1:2 * n_arr + 1]
        send, recv, loc = refs[2 * n_arr + 1:]
        x, y, c = _mesh_place()
        me = 4 * x + 2 * y + c
        local = [pltpu.make_async_copy(ins[a] if gathers[a] else ins[a].at[me], outs[a].at[me], loc.at[a])
                 for a in range(n_arr)]
        remote = _peer_copies(ins, outs, send, recv, gathers)
        for cp in local:
            cp.start()
        for put, _ in remote:
            put.start()
        for cp in local:
            cp.wait()
        for put, got in remote:
            put.wait_send()
            got.wait_recv()

    return pl.pallas_call(
        body, name=name,
        in_specs=[ANY] * (n_arr + 1), out_specs=[ANY] * n_arr,
        out_shape=[jax.ShapeDtypeStruct(((N_DEV,) if g else ()) + a.shape, a.dtype) for a, g in zip(arrays, gathers)],
        scratch_shapes=[pltpu.SemaphoreType.DMA((n_arr * (N_DEV - 1),)), pltpu.SemaphoreType.DMA((n_arr * (N_DEV - 1),)),
                        pltpu.SemaphoreType.DMA((n_arr,))],
        compiler_params=pltpu.CompilerParams(has_side_effects=True),
    )(*arrays, arrays[0] if after is None else after)


def _gather_two_level(arrays, name):
    n_arr = len(arrays)
    per = N_DEV - 1

    def body(*refs):
        ins, outs = refs[:n_arr], refs[n_arr:2 * n_arr]
        send, recv, loc = refs[2 * n_arr:]
        x, y, c = _mesh_place()
        me, sibling = (x, y, c), (x, y, 1 - c)
        chips = [(1 - x, y), (x, 1 - y), (1 - x, 1 - y)]

        def block(a, place):
            px, py, pc = place
            return outs[a].at[4 * px + 2 * py + pc]

        def copy(a, k, place, to, src=None):
            dst = block(a, place)
            return pltpu.make_async_remote_copy(dst if src is None else src, dst, send.at[a * per + k], recv.at[a * per + k],
                                                device_id=to, device_id_type=pl.DeviceIdType.MESH)

        local = [pltpu.make_async_copy(ins[a], block(a, me), loc.at[a]) for a in range(n_arr)]
        for cp in local:
            cp.start()
        first = []
        for a in range(n_arr):
            first.append(copy(a, 0, me, sibling, src=ins[a]))
            first += [copy(a, 1 + j, me, (*chip, c), src=ins[a]) for j, chip in enumerate(chips)]
        for cp in first:
            cp.start()
        passed = []
        for j, chip in enumerate(chips):
            for a in range(n_arr):
                copy(a, 1 + j, (*chip, c), me).wait_recv()
                passed.append(copy(a, 4 + j, (*chip, c), sibling))
                passed[-1].start()
        for a in range(n_arr):
            copy(a, 0, sibling, me).wait_recv()
            for j, chip in enumerate(chips):
                copy(a, 4 + j, (*chip, 1 - c), me).wait_recv()
        for cp in first + passed:
            cp.wait_send()
        for cp in local:
            cp.wait()

    return pl.pallas_call(
        body, name=name,
        in_specs=[ANY] * n_arr, out_specs=[ANY] * n_arr,
        out_shape=[jax.ShapeDtypeStruct((N_DEV,) + a.shape, a.dtype) for a in arrays],
        scratch_shapes=[pltpu.SemaphoreType.DMA((n_arr * per,)), pltpu.SemaphoreType.DMA((n_arr * per,)),
                        pltpu.SemaphoreType.DMA((n_arr,))],
        compiler_params=pltpu.CompilerParams(has_side_effects=True),
    )(*arrays)


HBM = pl.BlockSpec(memory_space=pltpu.HBM)
SEM = pl.BlockSpec(memory_space=pltpu.SEMAPHORE)
DATAFLOW = pltpu.SideEffectType.DATAFLOW_SIDE_EFFECTING


def _own_block_in_place(block, me):
    land = lax.empty((N_DEV,) + block.shape, block.dtype)
    return lax.dynamic_update_slice(land, block[None], (me,) + (0,) * block.ndim)


def _peer_copies(srcs, lands, send, recv, gathers):
    x, y, c = _mesh_place()
    me = 4 * x + 2 * y + c
    out = []
    for a, (src, land) in enumerate(zip(srcs, lands)):
        for k in range(1, N_DEV):
            dev, idx = _peer(k)
            sem = a * (N_DEV - 1) + k - 1
            mine = src if gathers[a] else src.at[idx]
            put = pltpu.make_async_remote_copy(mine, land.at[me], send.at[sem], recv.at[sem],
                                               device_id=dev, device_id_type=pl.DeviceIdType.MESH)
            got = pltpu.make_async_remote_copy(mine, land.at[idx], send.at[sem], recv.at[sem],
                                               device_id=dev, device_id_type=pl.DeviceIdType.MESH)
            out.append((put, got))
    return out


def _exchange_start(srcs, lands, gather, after, name):
    n = len(srcs)
    extra = [] if after is None else [after]

    def body(*refs):
        srcs_, lands_ = refs[:n], refs[n:2 * n]
        send, recv = refs[2 * n + len(extra)], refs[2 * n + len(extra) + 1]
        for put, _ in _peer_copies(srcs_, lands_, send, recv, gather):
            put.start()
        refs[-1][...] = jnp.zeros_like(refs[-1])

    nsem = n * (N_DEV - 1)
    thru = [pltpu.HBM(a.shape, a.dtype) for a in list(srcs) + list(lands)]
    res = pl.pallas_call(
        body, name=name,
        out_shape=(pltpu.SemaphoreType.DMA((nsem,)), pltpu.SemaphoreType.DMA((nsem,)), *thru, jax.ShapeDtypeStruct((8, 128), F32)),
        in_specs=[HBM] * (2 * n) + [ANY] * len(extra),
        out_specs=(SEM, SEM, *([HBM] * (2 * n)), pl.BlockSpec(memory_space=pltpu.VMEM)),
        input_output_aliases={i: 2 + i for i in range(2 * n)},
        compiler_params=pltpu.CompilerParams(has_side_effects=DATAFLOW),
    )(*[pltpu.with_memory_space_constraint(a, pltpu.HBM) for a in list(srcs) + list(lands)], *extra)
    return res[0], res[1], list(res[2:2 + n]), list(res[2 + n:2 + 2 * n]), res[-1]


def _exchange_wait(send, recv, srcs, lands, gather, after, name):
    n = len(srcs)

    def body(*refs):
        srcs_, lands_, send_, recv_ = refs[:n], refs[n:2 * n], refs[2 * n], refs[2 * n + 1]
        for put, got in _peer_copies(srcs_, lands_, send_, recv_, gather):
            put.wait_send()
            got.wait_recv()

    thru = [pltpu.HBM(a.shape, a.dtype) for a in list(srcs) + list(lands)]
    res = pl.pallas_call(
        body, name=name, out_shape=tuple(thru),
        in_specs=[HBM] * (2 * n) + [SEM, SEM, ANY], out_specs=tuple([HBM] * (2 * n)),
        input_output_aliases={i: i for i in range(2 * n)},
        compiler_params=pltpu.CompilerParams(has_side_effects=DATAFLOW),
    )(*srcs, *lands, send, recv, after)
    return list(res[n:])


def _silu_rows(c):
    def body(c_ref, o_ref):
        v = c_ref[...]
        o_ref[...] = v * _sigmoid(v)

    return pl.pallas_call(body, name="cond", out_shape=jax.ShapeDtypeStruct(c.shape, F32))(c)


def _mod_slab(cond_all, w_ada, b_slab):
    def body(c_ref, w_ref, b_ref, o_ref):
        o_ref[...] = _dot(c_ref[...].astype(BF16), w_ref[0].astype(BF16), ((1,), (0,))) + b_ref[...]

    return pl.pallas_call(body, name="mod_slab",
                          out_shape=jax.ShapeDtypeStruct((cond_all.shape[0], w_ada.shape[2]), F32),
                          compiler_params=pltpu.CompilerParams(vmem_limit_bytes=VMEM_LIMIT))(cond_all, w_ada, b_slab)


def _ada_grad(cond_all, dmod_cols):
    def body(c_ref, d_ref, o_ref):
        o_ref[...] = _dot(c_ref[...].astype(BF16), d_ref[...].astype(BF16), ((0,), (0,)))

    return pl.pallas_call(body, name="ada_grad",
                          out_shape=jax.ShapeDtypeStruct((cond_all.shape[1], dmod_cols.shape[1]), F32),
                          compiler_params=pltpu.CompilerParams(vmem_limit_bytes=VMEM_LIMIT))(cond_all, dmod_cols)


def _adam_math(g, w, m, v):
    m2 = B1 * m + (1.0 - B1) * g
    v2 = B2 * v + (1.0 - B2) * (g * g)
    m_hat = m2 / (1.0 - B1 ** STEP)
    v_hat = v2 / (1.0 - B2 ** STEP)
    return -LR * (m_hat / (jnp.sqrt(v_hat) + ADAM_EPS) + WD * w), m2, v2


def _adamw(parts, w, m, v, name):
    n, rows, cols = parts.shape
    tr = max([p for p in range(16, 513, 16) if rows % p == 0] or [rows])

    def body(p_ref, w_ref, m_ref, v_ref, g_ref, d_ref, m2_ref, v2_ref):
        g = p_ref[0].astype(F32)
        for s in range(1, n):
            g = g + p_ref[s].astype(F32)
        g_ref[0] = g
        d_ref[0], m2_ref[0], v2_ref[0] = _adam_math(g, w_ref[0], m_ref[0], v_ref[0])

    blk = pl.BlockSpec((1, tr, cols), lambda i: (0, i, 0))
    return pl.pallas_call(
        body, name=name, grid=(rows // tr,),
        in_specs=[pl.BlockSpec((n, tr, cols), lambda i: (0, i, 0)), blk, blk, blk],
        out_specs=[blk] * 4, out_shape=[jax.ShapeDtypeStruct((1, rows, cols), F32)] * 4,
        compiler_params=_cp("parallel"),
    )(parts, w, m, v)


ROW_PARAMS = (("g_norm1", D), ("g_cq", Q_LORA), ("g_ckv", KV_LORA), ("g_out_a", D_A), ("g_out_b", D_A), ("g_norm2", D),
              ("g_final", D))
LOSS_ROW = N_MOD + len(ROW_PARAMS)
PAY_ROWS = 16
NCOL = N_MOD * D // N_DEV


def _pack_small(dmods, rows, loss_cols):
    nb = dmods[0].shape[0]
    nrow = len(ROW_PARAMS)

    def body(*refs):
        dm, rw, loss_ref, pay_ref, blk_ref = refs[:N_MOD], refs[N_MOD:N_MOD + nrow], refs[N_MOD + nrow], refs[-2], refs[-1]
        pay_ref[...] = jnp.zeros_like(pay_ref)
        for k in range(N_MOD):
            tot = dm[k][0]
            for b in range(1, nb):
                tot = tot + dm[k][b]
            pay_ref[k:k + 1, :] = tot
        for i, (_, n) in enumerate(ROW_PARAMS):
            pay_ref[N_MOD + i:N_MOD + i + 1, :n] = rw[i][...]
        pay_ref[LOSS_ROW:LOSS_ROW + 1, :] = loss_ref[...]
        for j in range(N_DEV):
            done = 0
            while done < NCOL:
                seg, off = divmod(j * NCOL + done, D)
                ln = min(NCOL - done, D - off)
                for b in range(nb):
                    blk_ref[j, b:b + 1, done:done + ln] = dm[seg][b][:, off:off + ln]
                done += ln

    return pl.pallas_call(
        body, name="pack_small",
        out_shape=[jax.ShapeDtypeStruct((PAY_ROWS, D), F32), jax.ShapeDtypeStruct((N_DEV, nb, NCOL), F32)],
    )(*dmods, *rows, loss_cols)


def _small_update(pay, rel, ws, ms, vs):
    n_par = len(ws)

    def body(*refs):
        pay_ref, rel_ref = refs[:2]
        w_refs, m_refs, v_refs = (refs[2 + s * n_par:2 + (s + 1) * n_par] for s in range(3))
        outs, loss_ref = refs[2 + 3 * n_par:-1], refs[-1]
        tot, rtot = pay_ref[0], rel_ref[0]
        for s in range(1, N_DEV):
            tot, rtot = tot + pay_ref[s], rtot + rel_ref[s]

        def update(p, g, sl):
            outs[4 * p][:, sl] = g
            outs[4 * p + 1][:, sl], outs[4 * p + 2][:, sl], outs[4 * p + 3][:, sl] = _adam_math(
                g, w_refs[p][:, sl], m_refs[p][:, sl], v_refs[p][:, sl])

        for k in range(N_MOD):
            update(0, tot[k:k + 1, :], slice(k * D, (k + 1) * D))
        for i, (_, n) in enumerate(ROW_PARAMS):
            update(1 + i, tot[N_MOD + i:N_MOD + i + 1, :n], slice(0, n))
        update(n_par - 1, rtot, slice(0, H))
        loss_ref[...] = jnp.broadcast_to((0.5 / D) * jnp.sum(tot[LOSS_ROW:LOSS_ROW + 1, :]), loss_ref.shape)

    shapes = [jax.ShapeDtypeStruct(w.shape, F32) for w in ws for _ in range(4)]
    res = pl.pallas_call(
        body, name="small_update", out_shape=shapes + [jax.ShapeDtypeStruct((8, 128), F32)],
    )(pay, rel, *ws, *ms, *vs)
    return [tuple(res[4 * p:4 * p + 4]) for p in range(n_par)], res[-1]


def _cols_from_blocks(g):
    return jnp.transpose(g, (1, 0, 2)).reshape(g.shape[1], N_DEV * g.shape[2])


def _cols_to_blocks(w):
    r, c = w.shape
    return jnp.transpose(w.reshape(r, N_DEV, c // N_DEV), (1, 0, 2))


def _pad_w_in(wt):
    z = jnp.zeros((NOPE, wt.shape[1]), wt.dtype)
    return jnp.concatenate([wt[:P_IN - ROPE], z, wt[P_IN - ROPE:], z[:HP - NOPE - ROPE]], axis=0)


def _unpad_w_in(gt):
    k0 = P_IN - ROPE + NOPE
    return jnp.concatenate([gt[:P_IN - ROPE], gt[k0:k0 + ROPE]], axis=0)


def _pad_w_uq(wt):
    return jnp.pad(wt, ((0, 0), (0, HP - NOPE - ROPE), (0, 0))).reshape(H * HP, Q_LORA)


def _unpad_w_uq(gt):
    return gt.reshape(H, HP, Q_LORA)[:, :NOPE + ROPE]


def _split_w_ukv(w):
    w4 = w.reshape(KV_LORA, H // 2, 2, HP)
    z = jnp.zeros((KV_LORA, H // 2, NOPE), w.dtype)
    kn, vv = w4[..., :NOPE], w4[..., NOPE:]
    w_k = jnp.stack([jnp.concatenate([kn[:, :, 0], z], -1), jnp.concatenate([kn[:, :, 1], z], -1)], axis=2)
    w_v = jnp.stack([jnp.concatenate([vv[:, :, 0], z], -1), jnp.concatenate([z, vv[:, :, 1]], -1)], axis=2)
    return w_k.reshape(KV_LORA, H * HP), w_v.reshape(KV_LORA, H * HP)


def _join_w_ukv(g_k, g_v):
    gk = g_k.reshape(KV_LORA, H // 2, 2, HP)
    gv = g_v.reshape(KV_LORA, H // 2, 2, HP)
    even = jnp.concatenate([gk[:, :, 0, :NOPE], gv[:, :, 0, :VDIM]], -1)
    odd = jnp.concatenate([gk[:, :, 1, :NOPE], gv[:, :, 1, VDIM:]], -1)
    return jnp.stack([even, odd], axis=2).reshape(KV_LORA, H * HP)


def _rope_tables():
    half = ROPE // 2
    inv = np.float32(ROPE_THETA) ** (-np.arange(half, dtype=np.float32) / np.float32(half))
    ang = np.arange(S, dtype=np.float32)[:, None] * inv[None, :].astype(np.float32)
    cos, sin = np.cos(ang).astype(np.float32), np.sin(ang).astype(np.float32)
    ones, zeros = np.ones((S, NOPE), np.float32), np.zeros((S, NOPE), np.float32)
    tail1, tail0 = np.ones((S, HP - NOPE - ROPE), np.float32), np.zeros((S, HP - NOPE - ROPE), np.float32)
    zh = np.zeros((S, half), np.float32)
    c = np.concatenate([ones, cos, cos, tail1], axis=1)
    sm = np.concatenate([zeros, -sin, zh, tail0], axis=1)
    sp = np.concatenate([zeros, zh, sin, tail0], axis=1)
    return jnp.asarray(c), jnp.asarray(sm), jnp.asarray(sp)


def _local_step(x, mod, target, g_norm1, w_in_p, g_cq, w_uq_p, g_ckv, w_k, w_v, rel_bias, g_out_a, g_out_b, w_out,
                g_norm2, w_ffn_in, w_ffn_out, g_final, late_weights=None, on_ffn_grads=None, on_last_grads=None):
    nb = x.shape[0] // S
    sh1, sc1, g1, sh2, sc2, g2 = (mod[:, n].reshape(nb, 1, D) for n in range(N_MOD))
    rc, rsm, rsp = _rope_tables()
    biasm = _band_bias(rel_bias)

    h1 = _pre1(x, g_norm1, sc1, sh1)
    proj = _mm_nt(h1, w_in_p, F32, "proj")
    q, k, v, cqn, ckvn = _mla_pre(proj, g_cq, g_ckv, w_uq_p, w_k, w_v, rc, rsm, rsp)
    out_b, lse_b = _mla_fwd(q, k, v)
    out_a, lse_a = _dil_fwd(proj, biasm)
    y = _post_attn(out_a, out_b, g_out_a, g_out_b)
    if late_weights is not None:
        w_out, w_ffn_in, w_ffn_out = late_weights(y)
    mix = _mm_nn(y, w_out, BF16, "mix")
    x2, h2 = _resid_norm2(x, mix, g1, g_norm2, sc2, sh2)
    ffn_g, ffn_u, act = _ffn_in(h2, w_ffn_in)
    f = _mm_nn(act, w_ffn_out, BF16, "ffn_out")
    dx3, df, loss_cols, dg_final, dg2 = _final(x2, f, g2, g_final, target)

    dg_, du_ = _d_act(df, w_ffn_out, ffn_g, ffn_u)
    gw_ffn_out = _mm_tn_rows([act], df, "gw_ffn_out")
    dh2 = _d_h2(dg_, du_, w_ffn_in)
    gw_ffn_in = _mm_tn_rows([dg_, du_], h2, "gw_ffn_in")
    dx2, dsh2, dsc2, dg_norm2, dg1, dmix = _norm_bwd(x2, dh2, dx3, g_norm2, sc2, gate=(mix, g1))
    dy = _mm_nt(dmix, w_out, BF16, "d_y")
    gw_out = _mm_tn(y, [dmix], "gw_out")
    if on_ffn_grads is not None:
        g_out_a = g_out_a + on_ffn_grads(gw_ffn_in, gw_ffn_out, gw_out)
    dout_a, dout_b, dg_out_a, dg_out_b = _post_attn_bwd(dy, out_a, out_b, g_out_a, g_out_b)
    dq_b, dk_b, dv_b = _mla_bwd(q, k, v, out_b, dout_b, lse_b)
    dq_a, dk_a, dv_a, dlogits = _dil_bwd(proj, biasm, out_a, dout_a, lse_a)
    g_rel = _rel_bias_grad(dlogits)
    dqr, dproj, dg_cq, dg_ckv = _mla_pre_bwd(proj, dq_b, dk_b, dv_b, (dq_a, dk_a, dv_a), g_cq, g_ckv, w_uq_p, w_k, w_v,
                                             rc, rsm, rsp)
    gw_uq = _mm_tn(dqr, [cqn], "gw_uq")
    gw_k, gw_v = _mm_tn(ckvn, [dk_b, dv_b], "gw_kv")
    gw_in = _mm_tn_rows([dproj], h1, "gw_in")
    if on_last_grads is not None:
        started = on_last_grads(dict(w_in=gw_in, w_uq=gw_uq, w_k=gw_k, w_v=gw_v))
    else:
        started = None
    dh1 = _mm_nn(dproj, w_in_p, BF16, "d_h1", after=started)
    grad_x, dsh1, dsc1, dg_norm1 = _norm_bwd(x, dh1, dx2, g_norm1, sc1)

    dmod = [dsh1, dsc1, dg1, dsh2, dsc2, dg2]
    small = dict(g_norm1=dg_norm1, g_cq=dg_cq, g_ckv=dg_ckv, rel_bias=g_rel, g_out_a=dg_out_a, g_out_b=dg_out_b,
                 g_norm2=dg_norm2, g_final=dg_final)
    big = dict(w_in=gw_in, w_uq=gw_uq, w_k=gw_k, w_v=gw_v, w_out=gw_out, w_ffn_in=gw_ffn_in, w_ffn_out=gw_ffn_out)
    return grad_x, dmod, loss_cols, small, big


def kernel(x, c, w_ada, b_ada, g_norm1, w_in, g_cq, w_uq, g_ckv, w_ukv, rel_bias, g_out_a, g_out_b, w_out, g_norm2, w_ffn_in, w_ffn_out, g_final, loss_target, m_w_ada, m_b_ada, m_g_norm1, m_w_in, m_g_cq, m_w_uq, m_g_ckv, m_w_ukv, m_rel_bias, m_g_out_a, m_g_out_b, m_w_out, m_g_norm2, m_w_ffn_in, m_w_ffn_out, m_g_final, v_w_ada, v_b_ada, v_g_norm1, v_w_in, v_g_cq, v_w_uq, v_g_ckv, v_w_ukv, v_rel_bias, v_g_out_a, v_g_out_b, v_w_out, v_g_norm2, v_w_ffn_in, v_w_ffn_out, v_g_final):
    nb = x.shape[0]
    t = nb * S
    xt, tt = x.reshape(t, D), loss_target.reshape(t, D)
    me = 4 * lax.axis_index("x") + 2 * lax.axis_index("y") + lax.axis_index("c")

    early = [jnp.swapaxes(w_in, 1, 2)[0], jnp.swapaxes(w_uq, 1, 2)[0], w_ukv[0]]
    gathered = _gather_two_level([_silu_rows(c)] + [s.astype(BF16) for s in early], "gather_weights")
    cond_all = gathered[0].reshape(N_DEV * nb, D)
    w_in_t = gathered[1].reshape(P_IN, D)
    w_ukv_f = _cols_from_blocks(gathered[3])
    w_k, w_v = _split_w_ukv(w_ukv_f)

    ncol = N_MOD * D // N_DEV
    b_slab = lax.dynamic_slice(b_ada, (0, me * ncol), (1, ncol))
    slab = _mod_slab(cond_all, w_ada, b_slab)
    (mod_rows,) = _exchange([slab.reshape(N_DEV, nb, ncol)], [False], "scatter_mod")
    mod = jnp.transpose(mod_rows, (1, 0, 2)).reshape(nb, N_MOD, D)

    late = [s.astype(BF16) for s in (w_out[0], jnp.swapaxes(w_ffn_in, 1, 2)[0], w_ffn_out[0])]
    late_send, late_recv, late_src, late_land, late_token = _exchange_start(
        late, [_own_block_in_place(s, me) for s in late], [True] * 3, mod_rows, "gather_late_start")
    g_norm1_t = g_norm1 + late_token[:1, :1]

    def late_weights(after):
        w_out_g, w_ffn_in_g, w_ffn_out_g = _exchange_wait(late_send, late_recv, late_src, late_land, [True] * 3, after,
                                                          "gather_late_wait")
        return w_out_g.reshape(D, D), w_ffn_in_g.reshape(2 * D_FF, D), w_ffn_out_g.reshape(D_FF, D)

    flight = {}

    def start_grads(key, src, name):
        land = [_own_block_in_place(lax.dynamic_index_in_dim(s, me, 0, keepdims=False), me) for s in src]
        send, recv, src, land, token = _exchange_start(src, land, [False] * len(src), None, name)
        flight[key] = (send, recv, src, land)
        return token[:1, :1]

    def on_ffn_grads(gw_ffn_in, gw_ffn_out, gw_out):
        return start_grads("ffn", [gw_ffn_in.reshape(N_DEV, 2 * D_FF // N_DEV, D), gw_ffn_out.reshape(N_DEV, D_FF // N_DEV, D),
                                   gw_out.reshape(N_DEV, D // N_DEV, D)], "exchange_ffn_start")

    def on_last_grads(gw):
        return start_grads("rest", [_unpad_w_in(gw["w_in"]).reshape(N_DEV, P_IN // N_DEV, D),
                                    _unpad_w_uq(gw["w_uq"]),
                                    _cols_to_blocks(_join_w_ukv(gw["w_k"], gw["w_v"]))], "exchange_rest_start")

    grad_x, dmod, loss_cols, small, _ = _local_step(
        xt, mod, tt, g_norm1_t, _pad_w_in(w_in_t), g_cq, _pad_w_uq(gathered[2]), g_ckv, w_k, w_v, rel_bias, g_out_a, g_out_b,
        None, g_norm2, None, None, g_final.reshape(1, D), late_weights=late_weights, on_ffn_grads=on_ffn_grads,
        on_last_grads=on_last_grads)

    mine, dmod_blocks = _pack_small(dmod, [small[n] for n, _ in ROW_PARAMS], loss_cols)
    small_src = [dmod_blocks, mine, small["rel_bias"]]
    small_gather = [False, True, True]
    small_land = [_own_block_in_place(lax.dynamic_index_in_dim(dmod_blocks, me, 0, keepdims=False), me),
                  _own_block_in_place(mine, me), _own_block_in_place(small["rel_bias"], me)]
    small_send, small_recv, small_src, small_land, _ = _exchange_start(small_src, small_land, small_gather, None,
                                                                       "exchange_small_start")

    upd = {}

    def land_and_update(key, names, after, name):
        got = _exchange_wait(*flight[key], [False] * len(names), after, name)
        for n, p in zip(names, got):
            w, m, v = big[n]
            upd[n] = _adamw(p, w, m, v, "adamw_" + n)

    def flip(a):
        return jnp.swapaxes(a, 1, 2)

    big = dict(w_in=(flip(w_in), flip(m_w_in), flip(v_w_in)), w_uq=(flip(w_uq), flip(m_w_uq), flip(v_w_uq)),
               w_ukv=(w_ukv, m_w_ukv, v_w_ukv),
               w_out=(w_out, m_w_out, v_w_out), w_ffn_in=(flip(w_ffn_in), flip(m_w_ffn_in), flip(v_w_ffn_in)),
               w_ffn_out=(w_ffn_out, m_w_ffn_out, v_w_ffn_out))
    land_and_update("ffn", ["w_ffn_in", "w_ffn_out", "w_out"], grad_x, "exchange_ffn_wait")
    land_and_update("rest", ["w_in", "w_uq", "w_ukv"], upd["w_out"][0], "exchange_rest_wait")
    for n in ("w_in", "w_uq", "w_ffn_in"):
        upd[n] = tuple(flip(a) for a in upd[n])

    dmod_cols, pay, rel = _exchange_wait(small_send, small_recv, small_src, small_land, small_gather, upd["w_ukv"][0],
                                         "exchange_small_wait")
    g_ada = _ada_grad(cond_all, dmod_cols.reshape(N_DEV * nb, ncol))
    upd["w_ada"] = _adamw(g_ada[None], w_ada, m_w_ada, v_w_ada, "adamw_w_ada")
    row = lambda a: a.reshape(1, D)
    small_names = ["b_ada"] + [n for n, _ in ROW_PARAMS] + ["rel_bias"]
    small_w = [b_ada, g_norm1, g_cq, g_ckv, g_out_a, g_out_b, g_norm2, row(g_final), rel_bias]
    small_m = [m_b_ada, m_g_norm1, m_g_cq, m_g_ckv, m_g_out_a, m_g_out_b, m_g_norm2, row(m_g_final), m_rel_bias]
    small_v = [v_b_ada, v_g_norm1, v_g_cq, v_g_ckv, v_g_out_a, v_g_out_b, v_g_norm2, row(v_g_final), v_rel_bias]
    small_upd, loss8 = _small_update(pay, rel, small_w, small_m, small_v)
    upd.update(zip(small_names, small_upd))

    order = ["w_ada", "b_ada", "g_norm1", "w_in", "g_cq", "w_uq", "g_ckv", "w_ukv", "rel_bias", "g_out_a", "g_out_b",
             "w_out", "g_norm2", "w_ffn_in", "w_ffn_out", "g_final"]
    like = dict(g_final=g_final)
    outs = [loss8[0, 0], grad_x.reshape(x.shape)]
    for part in range(4):
        for n in order:
            val = upd[n][part]
            outs.append(val.reshape(like[n].shape) if n in like else val)
    return tuple(outs)
```

```python
import numpy as np
import jax
import jax.numpy as jnp
from jax import lax
from jax.experimental import pallas as pl
from jax.experimental.pallas import tpu as pltpu

F32, BF16 = jnp.float32, jnp.bfloat16

N_DEV = 8
D = 1024
S = 2048
H = 8
E_A = 64
D_A = H * E_A
Q_LORA, KV_LORA = 384, 256
NOPE, ROPE, VDIM = 64, 32, 64
HP = 128
P_IN = 3 * D_A + Q_LORA + KV_LORA + ROPE
P_PAD = 3 * D_A + Q_LORA + KV_LORA + HP
TAIL0 = 3 * D_A
TAIL = P_PAD - TAIL0
D_FF = 2816
N_MOD = 6
EPS = 1e-6
NEG = -1e30
BLK = 128
DILATIONS = (1, 4, 16)
N_BUCKETS, MAX_DISTANCE = 32, 2048
ROPE_THETA = 10000.0
SCALE_A = E_A ** -0.5
SCALE_B = (NOPE + ROPE) ** -0.5
B1, B2, LR, ADAM_EPS, WD, STEP = 0.9, 0.999, 0.001, 1e-8, 0.01, 10
VMEM_LIMIT = 56 * 1024 * 1024


def _cp(*sem):
    return pltpu.CompilerParams(dimension_semantics=sem, vmem_limit_bytes=VMEM_LIMIT)


def _pick(n, prefs):
    for p in prefs:
        if n % p == 0:
            return p
    raise ValueError(f"no tile of {prefs} divides {n}")


OPERAND_BYTES = 6 * 1024 * 1024


def _pick_rows(m, k):
    return _pick(m, [p for p in (1024, 512, 256, 128, 16) if p * k * 2 <= OPERAND_BYTES])


MATMUL_BYTES = 40 * 1024 * 1024


def _stream_rows(m, fixed, per_row):
    return _pick(m, [p for p in (4096, 2048, 1024, 512, 256, 128, 16) if fixed + p * per_row <= MATMUL_BYTES])


def _dot(a, b, dims):
    return lax.dot_general(a, b, (dims, ((), ())), preferred_element_type=F32)


def _mm_nn(a, b, out_dtype, name, after=None):
    m, k = a.shape
    n = b.shape[1]
    tn = _pick(n, (512, 256, 384, 128))
    tm = _stream_rows(m, 4 * k * tn, 4 * k + (2 * jnp.dtype(out_dtype).itemsize + 4) * tn)

    def body(a_ref, b_ref, *rest):
        o_ref = rest[-1]
        o_ref[...] = _dot(a_ref[...], b_ref[...], ((1,), (0,))).astype(o_ref.dtype)

    extra = [] if after is None else [after]
    return pl.pallas_call(
        body, name=name, grid=(m // tm, n // tn),
        in_specs=[pl.BlockSpec((tm, k), lambda i, j: (i, 0)), pl.BlockSpec((k, tn), lambda i, j: (0, j))] + [ANY] * len(extra),
        out_specs=pl.BlockSpec((tm, tn), lambda i, j: (i, j)),
        out_shape=jax.ShapeDtypeStruct((m, n), out_dtype),
        compiler_params=_cp("parallel", "parallel"),
    )(a, b, *extra)


def _mm_nt(a, b, out_dtype, name, after=None):
    m, k = a.shape
    n = b.shape[0]
    tn = _pick(n, (512, 256, 384, 128))
    tm = _stream_rows(m, 4 * k * tn, 4 * k + (2 * jnp.dtype(out_dtype).itemsize + 4) * tn)

    def body(a_ref, b_ref, *rest):
        o_ref = rest[-1]
        o_ref[...] = _dot(a_ref[...], b_ref[...], ((1,), (1,))).astype(o_ref.dtype)

    extra = [] if after is None else [after]
    return pl.pallas_call(
        body, name=name, grid=(m // tm, n // tn),
        in_specs=[pl.BlockSpec((tm, k), lambda i, j: (i, 0)), pl.BlockSpec((tn, k), lambda i, j: (j, 0))] + [ANY] * len(extra),
        out_specs=pl.BlockSpec((tm, tn), lambda i, j: (i, j)),
        out_shape=jax.ShapeDtypeStruct((m, n), out_dtype),
        compiler_params=_cp("parallel", "parallel"),
    )(a, b, *extra)


def _mm_tn(a, bs, name):
    t, m = a.shape
    n = bs[0].shape[1]
    nb_ = len(bs)
    tc = _pick(t, (512, 16))
    tn = _pick(n, (512, 384, 256, 128))
    tm = _pick(m, [p for p in (1024, 512, 384, 256, 128) if (3 * p + 2 * nb_ * tn) * t * 2 <= VMEM_LIMIT - 2 * OPERAND_BYTES])
    if tm <= 256 and nb_ * n * t * 2 <= 2 * OPERAND_BYTES:
        tn = n

    def body(*refs):
        a_ref, b_refs, o_refs, at_ref = refs[0], refs[1:1 + nb_], refs[1 + nb_:1 + 2 * nb_], refs[-1]

        @pl.when(pl.program_id(1) == 0)
        def _():
            def chunk(c, _):
                rows = pl.ds(pl.multiple_of(c * tc, tc), tc)
                at_ref[:, rows] = a_ref[rows, :].T
                return 0

            lax.fori_loop(0, t // tc, chunk, 0)

        for b_ref, o_ref in zip(b_refs, o_refs):
            o_ref[...] = _dot(at_ref[...], b_ref[...], ((1,), (0,))).astype(BF16)

    res = pl.pallas_call(
        body, name=name, grid=(m // tm, n // tn),
        in_specs=[pl.BlockSpec((t, tm), lambda i, j: (0, i))] + [pl.BlockSpec((t, tn), lambda i, j: (0, j))] * nb_,
        out_specs=[pl.BlockSpec((tm, tn), lambda i, j: (i, j))] * nb_,
        out_shape=[jax.ShapeDtypeStruct((m, n), BF16)] * nb_,
        scratch_shapes=[pltpu.VMEM((tm, t), BF16)],
        compiler_params=_cp("parallel", "arbitrary"),
    )(a, *bs)
    return res[0] if nb_ == 1 else res


def _mm_tn_rows(a_list, b, name):
    t, m = a_list[0].shape
    n = b.shape[1]
    na = len(a_list)
    tc, tm = _pick(t, (512, 16)), _pick(m, (256, 128))
    nblk = m // tm

    def body(*refs):
        a_refs, b_ref, o_ref, bt_ref, r_ref = refs[:na], refs[na], refs[na + 1], refs[na + 2], refs[na + 3]
        i = pl.program_id(0)

        @pl.when(i == 0)
        def _():
            def chunk(c, _):
                rows = pl.ds(pl.multiple_of(c * tc, tc), tc)
                bt_ref[:, rows] = b_ref[rows, :].T
                return 0

            lax.fori_loop(0, t // tc, chunk, 0)

        for s, a_ref in enumerate(a_refs):
            @pl.when((i >= s * nblk) & (i < (s + 1) * nblk))
            def _(a_ref=a_ref):
                r_ref[...] = _dot(bt_ref[...], a_ref[...], ((1,), (0,)))
                o_ref[...] = r_ref[...].T.astype(BF16)

    return pl.pallas_call(
        body, name=name, grid=(na * nblk,),
        in_specs=[pl.BlockSpec((t, tm), lambda i, s=s: (0, jnp.clip(i - s * nblk, 0, nblk - 1))) for s in range(na)]
        + [pl.BlockSpec((t, n), lambda i: (0, 0))],
        out_specs=pl.BlockSpec((tm, n), lambda i: (i, 0)),
        out_shape=jax.ShapeDtypeStruct((na * m, n), BF16),
        scratch_shapes=[pltpu.VMEM((n, t), BF16), pltpu.VMEM((n, tm), F32)],
        compiler_params=_cp("arbitrary"),
    )(*a_list, b)


EPI = 256


def _silu_parts(g):
    sg = 0.5 * jnp.tanh(0.5 * g) + 0.5
    return sg, g * sg


def _ffn_in(h2, wt):
    t, k = h2.shape
    tn = _pick(D_FF, (256, 128))
    tm = _stream_rows(t, 8 * k * tn, 4 * k + (3 * 2 * 2 + 2 * 4) * tn)
    nj = D_FF // tn

    def body(h_ref, wg_ref, wu_ref, g_ref, u_ref, a_ref):
        hv = h_ref[...]
        g_all = _dot(hv, wg_ref[...], ((1,), (1,)))
        u_all = _dot(hv, wu_ref[...], ((1,), (1,)))
        for r in range(tm // EPI):
            rows = slice(r * EPI, (r + 1) * EPI)
            g, u = g_all[rows], u_all[rows]
            g_ref[rows, :] = g.astype(BF16)
            u_ref[rows, :] = u.astype(BF16)
            a_ref[rows, :] = (_silu_parts(g)[1] * u).astype(BF16)

    blk = pl.BlockSpec((tm, tn), lambda i, j: (i, j))
    return pl.pallas_call(
        body, name="ffn_in", grid=(t // tm, nj),
        in_specs=[pl.BlockSpec((tm, k), lambda i, j: (i, 0)), pl.BlockSpec((tn, k), lambda i, j: (j, 0)),
                  pl.BlockSpec((tn, k), lambda i, j: (j + nj, 0))],
        out_specs=[blk] * 3, out_shape=[jax.ShapeDtypeStruct((t, D_FF), BF16)] * 3,
        compiler_params=_cp("parallel", "parallel"),
    )(h2, wt, wt)


def _d_act(df, w, g, u):
    t, k = df.shape
    tn = _pick(D_FF, (256, 128))
    nj = D_FF // tn
    nin, nout = 3, 2

    def body(df_hbm, w_hbm, g_hbm, u_hbm, dg_hbm, du_hbm, df_ref, wbuf, gbuf, ubuf, ogbuf, oubuf, isem, osem, dsem):
        def cols(j):
            return pl.ds(j * tn if isinstance(j, int) else pl.multiple_of(j * tn, tn), tn)

        def fetch(j, slot):
            return (pltpu.make_async_copy(w_hbm.at[cols(j), :], wbuf.at[slot], isem.at[0, slot]),
                    pltpu.make_async_copy(g_hbm.at[:, cols(j)], gbuf.at[slot], isem.at[1, slot]),
                    pltpu.make_async_copy(u_hbm.at[:, cols(j)], ubuf.at[slot], isem.at[2, slot]))

        def put(j, slot):
            return (pltpu.make_async_copy(ogbuf.at[slot], dg_hbm.at[:, cols(j)], osem.at[0, slot]),
                    pltpu.make_async_copy(oubuf.at[slot], du_hbm.at[:, cols(j)], osem.at[1, slot]))

        whole = pltpu.make_async_copy(df_hbm, df_ref, dsem)
        whole.start()
        for s in range(nin - 1):
            for cp in fetch(s, s):
                cp.start()
        whole.wait()

        def step(j, _):
            slot, oslot = j % nin, j % nout

            @pl.when(j + nin - 1 < nj)
            def _():
                for cp in fetch(j + nin - 1, (j + nin - 1) % nin):
                    cp.start()

            for cp in fetch(j, slot):
                cp.wait()

            @pl.when(j >= nout)
            def _():
                for cp in put(j - nout, oslot):
                    cp.wait()

            da_all = _dot(df_ref[...], wbuf[slot], ((1,), (1,)))
            for r in range(t // EPI):
                rows = slice(r * EPI, (r + 1) * EPI)
                da = da_all[rows]
                gv = gbuf[slot, rows, :].astype(F32)
                sg, silu = _silu_parts(gv)
                ogbuf[oslot, rows, :] = ((da * ubuf[slot, rows, :].astype(F32)) * (sg + silu * (1.0 - sg))).astype(BF16)
                oubuf[oslot, rows, :] = (da * silu).astype(BF16)
            for cp in put(j, oslot):
                cp.start()
            return 0

        lax.fori_loop(0, nj, step, 0)
        for j in range(nj - nout, nj):
            for cp in put(j, j % nout):
                cp.wait()

    return pl.pallas_call(
        body, name="d_act",
        in_specs=[ANY, ANY, ANY, ANY], out_specs=[ANY, ANY],
        out_shape=[jax.ShapeDtypeStruct((t, D_FF), BF16)] * 2,
        scratch_shapes=[pltpu.VMEM((t, k), BF16), pltpu.VMEM((nin, tn, k), BF16), pltpu.VMEM((nin, t, tn), BF16),
                        pltpu.VMEM((nin, t, tn), BF16), pltpu.VMEM((nout, t, tn), BF16), pltpu.VMEM((nout, t, tn), BF16),
                        pltpu.SemaphoreType.DMA((3, nin)), pltpu.SemaphoreType.DMA((2, nout)), pltpu.SemaphoreType.DMA],
        compiler_params=pltpu.CompilerParams(vmem_limit_bytes=VMEM_LIMIT),
    )(df, w, g, u)


def _d_h2(dg, du, wt):
    t = dg.shape[0]
    n = wt.shape[1]
    tm, tn = _pick_rows(t, D_FF), _pick(n, (512, 256, 128))

    def body(dg_ref, du_ref, wg_ref, wu_ref, o_ref):
        o_ref[...] = (_dot(dg_ref[...], wg_ref[...], ((1,), (0,)))
                      + _dot(du_ref[...], wu_ref[...], ((1,), (0,)))).astype(BF16)

    return pl.pallas_call(
        body, name="d_h2", grid=(t // tm, n // tn),
        in_specs=[pl.BlockSpec((tm, D_FF), lambda i, j: (i, 0)), pl.BlockSpec((tm, D_FF), lambda i, j: (i, 0)),
                  pl.BlockSpec((D_FF, tn), lambda i, j: (0, j)), pl.BlockSpec((D_FF, tn), lambda i, j: (1, j))],
        out_specs=pl.BlockSpec((tm, tn), lambda i, j: (i, j)),
        out_shape=jax.ShapeDtypeStruct((t, n), BF16),
        compiler_params=_cp("parallel", "parallel"),
    )(dg, du, wt, wt)


TM = 1024


def _row(w):
    return pl.BlockSpec((TM, w), lambda i: (i, 0))


def _row_at(w, col):
    return pl.BlockSpec((TM, w), lambda i: (i, col))


def _vec(w):
    return pl.BlockSpec((1, w), lambda i: (0, 0))


def _per_ex(w):
    return pl.BlockSpec((1, 1, w), lambda i: (i // (S // TM), 0, 0))


def _pos(w):
    return pl.BlockSpec((TM, w), lambda i: (i % (S // TM), 0))


def _full(shape):
    return pl.BlockSpec(shape, lambda i: (0,) * len(shape))


def _rms(x):
    return lax.rsqrt(jnp.mean(x * x, axis=-1, keepdims=True) + EPS)


def _rms_bwd(n, r, dn):
    return r * (dn - n * jnp.mean(dn * n, axis=-1, keepdims=True))


def _colsum(v):
    return jnp.sum(v, axis=0, keepdims=True)


def _acc_first(i, ref, val, every=None):
    first = (i == 0) if every is None else (i % every == 0)

    @pl.when(first)
    def _():
        ref[...] = jnp.zeros_like(ref)

    ref[...] += val.reshape(ref.shape)


def _pre1(x, g, sc, sh):
    t = x.shape[0]

    def body(x_ref, g_ref, sc_ref, sh_ref, h_ref):
        xv = x_ref[...]
        n = xv * _rms(xv)
        h_ref[...] = ((n * g_ref[...]) * (1.0 + sc_ref[0]) + sh_ref[0]).astype(BF16)

    return pl.pallas_call(
        body, name="pre1", grid=(t // TM,),
        in_specs=[_row(D), _vec(D), _per_ex(D), _per_ex(D)],
        out_specs=_row(D), out_shape=jax.ShapeDtypeStruct((t, D), BF16),
        compiler_params=_cp("parallel"),
    )(x, g, sc, sh)


def _rope_fwd(v, c, sm, sp):
    return v * c + pltpu.roll(v, HP - ROPE // 2, 1) * sm + pltpu.roll(v, ROPE // 2, 1) * sp


def _rope_bwd(dv, c, sm, sp):
    return dv * c + pltpu.roll(dv * sm, ROPE // 2, 1) + pltpu.roll(dv * sp, HP - ROPE // 2, 1)


def _mla_pre(proj, g_cq, g_ckv, w_uq, w_k, w_v, rc, rsm, rsp):
    t = proj.shape[0]

    def body(tail_ref, gq_ref, gkv_ref, wuq_ref, wk_ref, wv_ref, c_ref, sm_ref, sp_ref,
             q_ref, k_ref, v_ref, cqn_ref, ckvn_ref):
        tail = tail_ref[...]
        cq, ckv, kr = tail[:, :Q_LORA], tail[:, Q_LORA:Q_LORA + KV_LORA], tail[:, Q_LORA + KV_LORA:]
        cqn = (cq * _rms(cq) * gq_ref[...]).astype(BF16)
        ckvn = (ckv * _rms(ckv) * gkv_ref[...]).astype(BF16)
        cqn_ref[...] = cqn
        ckvn_ref[...] = ckvn
        c, sm, sp = c_ref[...], sm_ref[...], sp_ref[...]
        q = _dot(cqn, wuq_ref[...], ((1,), (1,)))
        kn = _dot(ckvn, wk_ref[...], ((1,), (0,)))
        v_ref[...] = _dot(ckvn, wv_ref[...], ((1,), (0,))).astype(BF16)
        krr = _rope_fwd(kr, c, sm, sp)
        for h in range(H):
            sl = slice(h * HP, (h + 1) * HP)
            q_ref[:, sl] = _rope_fwd(q[:, sl], c, sm, sp).astype(BF16)
            k_ref[:, sl] = (kn[:, sl] + krr).astype(BF16)

    wide = H * HP
    return pl.pallas_call(
        body, name="mla_pre", grid=(t // TM,),
        in_specs=[_row_at(TAIL, TAIL0 // TAIL), _vec(Q_LORA), _vec(KV_LORA), _full((wide, Q_LORA)),
                  _full((KV_LORA, wide)), _full((KV_LORA, wide)), _pos(HP), _pos(HP), _pos(HP)],
        out_specs=[_row(wide), _row(wide), _row(wide), _row(Q_LORA), _row(KV_LORA)],
        out_shape=[jax.ShapeDtypeStruct((t, wide), BF16)] * 3
        + [jax.ShapeDtypeStruct((t, Q_LORA), BF16), jax.ShapeDtypeStruct((t, KV_LORA), BF16)],
        compiler_params=_cp("parallel"),
    )(proj, g_cq, g_ckv, w_uq, w_k, w_v, rc, rsm, rsp)


def _mla_pre_bwd(proj, dq_, dk_, dv_, dqkv_a, g_cq, g_ckv, w_uq, w_k, w_v, rc, rsm, rsp):
    t = proj.shape[0]
    wide = H * HP

    def body(tail_ref, dq_ref, dk_ref, dv_ref, dqa_ref, dka_ref, dva_ref, gq_ref, gkv_ref, wuq_ref, wk_ref, wv_ref,
             c_ref, sm_ref, sp_ref, dqo_ref, dproj_ref, dgq_ref, dgkv_ref):
        i = pl.program_id(0)
        for n, src in enumerate((dqa_ref, dka_ref, dva_ref)):
            dproj_ref[:, n * D_A:(n + 1) * D_A] = src[...]
        dtail_ref = dproj_ref.at[:, TAIL0:]
        tail = tail_ref[...]
        cq, ckv = tail[:, :Q_LORA], tail[:, Q_LORA:Q_LORA + KV_LORA]
        c, sm, sp = c_ref[...], sm_ref[...], sp_ref[...]
        dkr = jnp.zeros((TM, HP), F32)
        for h in range(H):
            sl = slice(h * HP, (h + 1) * HP)
            dqo_ref[:, sl] = _rope_bwd(dq_ref[:, sl].astype(F32), c, sm, sp).astype(BF16)
            dkr = dkr + dk_ref[:, sl].astype(F32)
        lane = lax.broadcasted_iota(jnp.int32, (TM, HP), 1)
        dkr = jnp.where((lane >= NOPE) & (lane < NOPE + ROPE), _rope_bwd(dkr, c, sm, sp), 0.0)
        dkb = dk_ref[...]
        dvb = dv_ref[...]
        dcqn = _dot(dqo_ref[...], wuq_ref[...], ((1,), (0,)))
        dckvn = _dot(dkb, wk_ref[...], ((1,), (1,))) + _dot(dvb, wv_ref[...], ((1,), (1,)))
        rq, rkv = _rms(cq), _rms(ckv)
        nq, nkv = cq * rq, ckv * rkv
        _acc_first(i, dgq_ref, _colsum(dcqn * nq))
        _acc_first(i, dgkv_ref, _colsum(dckvn * nkv))
        dtail_ref[:, :Q_LORA] = _rms_bwd(nq, rq, dcqn * gq_ref[...]).astype(BF16)
        dtail_ref[:, Q_LORA:Q_LORA + KV_LORA] = _rms_bwd(nkv, rkv, dckvn * gkv_ref[...]).astype(BF16)
        dtail_ref[:, Q_LORA + KV_LORA:] = dkr.astype(BF16)

    return pl.pallas_call(
        body, name="mla_pre_bwd", grid=(t // TM,),
        in_specs=[_row_at(TAIL, TAIL0 // TAIL), _row(wide), _row(wide), _row(wide), _row(D_A), _row(D_A), _row(D_A),
                  _vec(Q_LORA), _vec(KV_LORA), _full((wide, Q_LORA)), _full((KV_LORA, wide)), _full((KV_LORA, wide)),
                  _pos(HP), _pos(HP), _pos(HP)],
        out_specs=[_row(wide), _row(P_PAD), _vec(Q_LORA), _vec(KV_LORA)],
        out_shape=[jax.ShapeDtypeStruct((t, wide), BF16), jax.ShapeDtypeStruct((t, P_PAD), BF16),
                   jax.ShapeDtypeStruct((1, Q_LORA), F32), jax.ShapeDtypeStruct((1, KV_LORA), F32)],
        compiler_params=_cp("arbitrary"),
    )(proj, dq_, dk_, dv_, *dqkv_a, g_cq, g_ckv, w_uq, w_k, w_v, rc, rsm, rsp)


def _post_attn(out_a, out_b, g_a, g_b):
    t = out_a.shape[0]

    def body(a_ref, b_ref, ga_ref, gb_ref, y_ref):
        a, b = a_ref[...], b_ref[...]
        y_ref[:, :D_A] = (a * _rms(a) * ga_ref[...]).astype(BF16)
        y_ref[:, D_A:] = (b * _rms(b) * gb_ref[...]).astype(BF16)

    return pl.pallas_call(
        body, name="post_attn", grid=(t // TM,),
        in_specs=[_row(D_A), _row(D_A), _vec(D_A), _vec(D_A)],
        out_specs=_row(D), out_shape=jax.ShapeDtypeStruct((t, D), BF16),
        compiler_params=_cp("parallel"),
    )(out_a, out_b, g_a, g_b)


def _post_attn_bwd(dy, out_a, out_b, g_a, g_b):
    t = dy.shape[0]

    def body(dy_ref, a_ref, b_ref, ga_ref, gb_ref, da_ref, db_ref, dga_ref, dgb_ref):
        i = pl.program_id(0)
        dy_ = dy_ref[...].astype(F32)
        for src, g_ref, dst, dg_ref, sl in ((a_ref, ga_ref, da_ref, dga_ref, slice(0, D_A)),
                                            (b_ref, gb_ref, db_ref, dgb_ref, slice(D_A, D))):
            v = src[...]
            r = _rms(v)
            n = v * r
            dyv = dy_[:, sl]
            _acc_first(i, dg_ref, _colsum(dyv * n))
            dst[...] = _rms_bwd(n, r, dyv * g_ref[...])

    return pl.pallas_call(
        body, name="post_attn_bwd", grid=(t // TM,),
        in_specs=[_row(D), _row(D_A), _row(D_A), _vec(D_A), _vec(D_A)],
        out_specs=[_row(D_A), _row(D_A), _vec(D_A), _vec(D_A)],
        out_shape=[jax.ShapeDtypeStruct((t, D_A), F32)] * 2 + [jax.ShapeDtypeStruct((1, D_A), F32)] * 2,
        compiler_params=_cp("arbitrary"),
    )(dy, out_a, out_b, g_a, g_b)


def _resid_norm2(x, mix, g1, g, sc, sh):
    t = x.shape[0]

    def body(x_ref, mix_ref, g1_ref, g_ref, sc_ref, sh_ref, x2_ref, h_ref):
        x2 = x_ref[...] + g1_ref[0] * mix_ref[...]
        x2_ref[...] = x2
        n = x2 * _rms(x2)
        h_ref[...] = ((n * g_ref[...]) * (1.0 + sc_ref[0]) + sh_ref[0]).astype(BF16)

    return pl.pallas_call(
        body, name="resid_norm2", grid=(t // TM,),
        in_specs=[_row(D), _row(D), _per_ex(D), _vec(D), _per_ex(D), _per_ex(D)],
        out_specs=[_row(D), _row(D)],
        out_shape=[jax.ShapeDtypeStruct((t, D), F32), jax.ShapeDtypeStruct((t, D), BF16)],
        compiler_params=_cp("parallel"),
    )(x, mix, g1, g, sc, sh)


def _sigmoid(v):
    return 1.0 / (1.0 + jnp.exp(-v))


def _final(x2, f, g2, g_fin, target):
    t = x2.shape[0]
    nb = t // S
    tpb = S // TM

    def body(x2_ref, f_ref, g2_ref, g_ref, t_ref, dx3_ref, df_ref, loss_ref, dgf_ref, dg2_ref):
        i = pl.program_id(0)
        fv = f_ref[...].astype(F32)
        x3 = x2_ref[...] + g2_ref[0] * fv
        r = _rms(x3)
        n = x3 * r
        err = n * g_ref[...] - t_ref[...]
        _acc_first(i, loss_ref, _colsum(err * err))
        dy = err * (1.0 / D)
        _acc_first(i, dgf_ref, _colsum(dy * n))
        dx3 = _rms_bwd(n, r, dy * g_ref[...])
        dx3_ref[...] = dx3.astype(BF16)
        _acc_first(i, dg2_ref, _colsum(dx3 * fv), every=tpb)
        df_ref[...] = (dx3 * g2_ref[0]).astype(BF16)

    return pl.pallas_call(
        body, name="final", grid=(t // TM,),
        in_specs=[_row(D), _row(D), _per_ex(D), _vec(D), _row(D)],
        out_specs=[_row(D), _row(D), _vec(D), _vec(D), _per_ex(D)],
        out_shape=[jax.ShapeDtypeStruct((t, D), BF16), jax.ShapeDtypeStruct((t, D), BF16),
                   jax.ShapeDtypeStruct((1, D), F32), jax.ShapeDtypeStruct((1, D), F32),
                   jax.ShapeDtypeStruct((nb, 1, D), F32)],
        compiler_params=_cp("arbitrary"),
    )(x2, f, g2, g_fin, target)


def _norm_bwd(xin, dh, dres, g, sc, gate=None):
    t = xin.shape[0]
    nb = t // S
    tpb = S // TM
    gated = gate is not None

    def body(*refs):
        if gated:
            x_ref, dh_ref, dres_ref, g_ref, sc_ref, mix_ref, g1_ref, dx_ref, dsh_ref, dsc_ref, dg_ref, dg1_ref, dmix_ref = refs
        else:
            x_ref, dh_ref, dres_ref, g_ref, sc_ref, dx_ref, dsh_ref, dsc_ref, dg_ref = refs
        i = pl.program_id(0)
        xv, dhv = x_ref[...], dh_ref[...].astype(F32)
        r = _rms(xv)
        n = xv * r
        gv = g_ref[...]
        _acc_first(i, dsh_ref, _colsum(dhv), every=tpb)
        _acc_first(i, dsc_ref, _colsum(dhv * (n * gv)), every=tpb)
        dng = dhv * (1.0 + sc_ref[0])
        _acc_first(i, dg_ref, _colsum(dng * n))
        dx = dres_ref[...].astype(F32) + _rms_bwd(n, r, dng * gv)
        dx_ref[...] = dx.astype(dx_ref.dtype)
        if gated:
            _acc_first(i, dg1_ref, _colsum(dx * mix_ref[...].astype(F32)), every=tpb)
            dmix_ref[...] = (dx * g1_ref[0]).astype(BF16)

    in_specs = [_row(D), _row(D), _row(D), _vec(D), _per_ex(D)]
    out_specs = [_row(D), _per_ex(D), _per_ex(D), _vec(D)]
    out_shape = [jax.ShapeDtypeStruct((t, D), BF16 if gated else F32), jax.ShapeDtypeStruct((nb, 1, D), F32),
                 jax.ShapeDtypeStruct((nb, 1, D), F32), jax.ShapeDtypeStruct((1, D), F32)]
    args = [xin, dh, dres, g, sc]
    if gated:
        in_specs += [_row(D), _per_ex(D)]
        out_specs += [_per_ex(D), _row(D)]
        out_shape += [jax.ShapeDtypeStruct((nb, 1, D), F32), jax.ShapeDtypeStruct((t, D), BF16)]
        args += list(gate)
    return pl.pallas_call(
        body, name="norm2_bwd" if gated else "norm1_bwd", grid=(t // TM,),
        in_specs=in_specs, out_specs=out_specs, out_shape=out_shape,
        compiler_params=_cp("arbitrary"),
    )(*args)


TQ = 256
TB = 512
FWD_HEADS = 2


def _mla_fwd(q, k, v):
    t = q.shape[0]
    nb = t // S

    def body(q_ref, k_ref, v_ref, o_ref, lse_ref):
        causal = lax.broadcasted_iota(jnp.int32, (TB, TB), 0) >= lax.broadcasted_iota(jnp.int32, (TB, TB), 1)
        heads = [slice(h * HP, (h + 1) * HP) for h in range(FWD_HEADS)]
        for i in range(S // TB):
            ri, past = slice(i * TB, (i + 1) * TB), slice(0, i * TB)
            qhs = [q_ref[ri, sl] for sl in heads]
            sd = [jnp.where(causal, _dot(qh, k_ref[ri, sl], ((1,), (1,))) * SCALE_B, NEG) for qh, sl in zip(qhs, heads)]
            ms = [jnp.max(s, axis=-1, keepdims=True) for s in sd]
            if i:
                so = [_dot(qh, k_ref[past, sl], ((1,), (1,))) * SCALE_B for qh, sl in zip(qhs, heads)]
                ms = [jnp.maximum(m, jnp.max(s, axis=-1, keepdims=True)) for m, s in zip(ms, so)]
            pd = [jnp.exp(s - m) for s, m in zip(sd, ms)]
            ls = [jnp.sum(p, axis=-1, keepdims=True) for p in pd]
            acc = [_dot(p.astype(BF16), v_ref[ri, sl], ((1,), (0,))) for p, sl in zip(pd, heads)]
            if i:
                po = [jnp.exp(s - m) for s, m in zip(so, ms)]
                ls = [l + jnp.sum(p, axis=-1, keepdims=True) for l, p in zip(ls, po)]
                acc = [a + _dot(p.astype(BF16), v_ref[past, sl], ((1,), (0,))) for a, p, sl in zip(acc, po, heads)]
            for pr in range(FWD_HEADS // 2):
                o_ref[ri, pr * HP:(pr + 1) * HP] = acc[2 * pr] / ls[2 * pr] + acc[2 * pr + 1] / ls[2 * pr + 1]
            for sl, m, l in zip(heads, ms, ls):
                lse_ref[ri, sl] = jnp.broadcast_to(m + jnp.log(l), (TB, HP))

    wide2 = pl.BlockSpec((S, FWD_HEADS * HP), lambda b, p: (b, p))
    return pl.pallas_call(
        body, name="mla_fwd", grid=(nb, H // FWD_HEADS),
        in_specs=[wide2, wide2, wide2],
        out_specs=[pl.BlockSpec((S, FWD_HEADS // 2 * HP), lambda b, p: (b, p)), wide2],
        out_shape=[jax.ShapeDtypeStruct((t, H * VDIM), F32), jax.ShapeDtypeStruct((t, H * HP), F32)],
        compiler_params=_cp("parallel", "parallel"),
    )(q, k, v)


def _mla_bwd(q, k, v, o, do, lse):
    t = q.shape[0]
    nb = t // S

    def body(q_ref, k_ref, v_ref, o_ref, do_ref, lse_ref, dq_out, dk_out, dv_out, dq_ref, dk_ref, dv_ref):
        lane = lax.broadcasted_iota(jnp.int32, (TB, HP), 1)
        causal = lax.broadcasted_iota(jnp.int32, (TB, TB), 0) >= lax.broadcasted_iota(jnp.int32, (TB, TB), 1)
        heads = [slice(h * HP, (h + 1) * HP) for h in range(2)]
        nblk = S // TB
        for i in reversed(range(nblk)):
            ri, past = slice(i * TB, (i + 1) * TB), slice(0, i * TB)
            dov = do_ref[ri, :]
            prod = dov * o_ref[ri, :]
            dob = dov.astype(BF16)
            deltas = [jnp.sum(jnp.where((lane < VDIM) if h == 0 else (lane >= VDIM), prod, 0.0), axis=-1, keepdims=True)
                      for h in range(2)]
            qhs = [q_ref[ri, sl] for sl in heads]
            lses = [lse_ref[ri, sl][:, :1] for sl in heads]
            for rows, diagonal in ((ri, True), (past, False)):
                if rows.stop == rows.start:
                    continue
                ps = [jnp.exp(_dot(qh, k_ref[rows, sl], ((1,), (1,))) * SCALE_B - lse) for qh, sl, lse in zip(qhs, heads, lses)]
                if diagonal:
                    ps = [jnp.where(causal, p, 0.0) for p in ps]
                dps = [_dot(dob, v_ref[rows, sl], ((1,), (1,))) for sl in heads]
                dss = [(p * (dp - delta) * SCALE_B).astype(BF16) for p, dp, delta in zip(ps, dps, deltas)]
                for sl, qh, p, ds in zip(heads, qhs, ps, dss):
                    dq = _dot(ds, k_ref[rows, sl], ((1,), (0,)))
                    dk = _dot(ds, qh, ((0,), (0,)))
                    dv = _dot(p.astype(BF16), dob, ((0,), (0,)))
                    if diagonal:
                        dq_ref[ri, sl] = dq
                    else:
                        dq_ref[ri, sl] += dq
                    if i == nblk - 1:
                        dk_ref[rows, sl] = dk
                        dv_ref[rows, sl] = dv
                    else:
                        dk_ref[rows, sl] += dk
                        dv_ref[rows, sl] += dv
        dq_out[...] = dq_ref[...].astype(BF16)
        dk_out[...] = dk_ref[...].astype(BF16)
        dv_out[...] = dv_ref[...].astype(BF16)

    wide2 = pl.BlockSpec((S, 2 * HP), lambda b, p: (b, p))
    pair = pl.BlockSpec((S, HP), lambda b, p: (b, p))
    return pl.pallas_call(
        body, name="mla_bwd", grid=(nb, H // 2),
        in_specs=[wide2, wide2, wide2, pair, pair, wide2],
        out_specs=[wide2, wide2, wide2],
        out_shape=[jax.ShapeDtypeStruct((t, H * HP), BF16)] * 3,
        scratch_shapes=[pltpu.VMEM((S, 2 * HP), F32)] * 3,
        compiler_params=_cp("parallel", "parallel"),
    )(q, k, v, o, do, lse)


def _t5_bucket(dist):
    max_exact = N_BUCKETS // 2
    d = np.maximum(dist, 1).astype(np.float64)
    large = max_exact + (np.log(d / max_exact) / np.log(MAX_DISTANCE / max_exact) * (N_BUCKETS - max_exact)).astype(np.int64)
    large = np.minimum(large, N_BUCKETS - 1)
    return np.where(dist < max_exact, dist, large).astype(np.int32)


def _band_geometry():
    a = np.arange(BLK)[:, None]
    bk = np.arange(2 * BLK)[None, :]
    steps = BLK + a - bk
    valid = (steps >= 0) & (steps <= BLK)
    buckets = np.stack([_t5_bucket(np.clip(steps, 0, BLK) * d) for d in DILATIONS])
    return buckets, valid


def _band_bias(rel_bias):
    buckets, valid = _band_geometry()
    onehot = (jnp.asarray(buckets)[..., None] == jnp.arange(N_BUCKETS)).astype(F32)
    bias = jnp.einsum("rqkn,nh->rhqk", onehot, rel_bias, precision=lax.Precision.HIGHEST)
    bias = jnp.where(jnp.asarray(valid)[None, None], bias, NEG)
    return bias.reshape(3, H // 2, 2 * BLK, 2 * BLK)


def _dil_items():
    items = []
    for r, d in enumerate(DILATIONS):
        for res in range(d):
            for blk in range(S // d // BLK):
                items.append((r, d, blk * BLK * d + res, blk > 0))
    return items


GROUP = 4


def _strided(start, d):
    return pl.ds(start, BLK) if d == 1 else pl.ds(start, BLK, stride=d)


def _stack_heads(tile, own):
    return jnp.where(own, jnp.concatenate([tile, tile], axis=0), 0.0).astype(BF16)


def _own_lanes():
    row = lax.broadcasted_iota(jnp.int32, (2 * BLK, HP), 0)
    lane = lax.broadcasted_iota(jnp.int32, (2 * BLK, HP), 1)
    return (lane < E_A) == (row < BLK)


def _dil_fwd(proj, biasm):
    t = proj.shape[0]
    nb = t // S

    def body(q_ref, k_ref, v_ref, b_ref, o_ref, lse_ref, ob_ref, lb_ref):
        lane = lax.broadcasted_iota(jnp.int32, (BLK, HP), 1)
        own = _own_lanes()
        items = _dil_items()
        for g in range(0, len(items), GROUP):
            grp = items[g:g + GROUP]
            ss, vts = [], []
            for r, d, start, has_prev in grp:
                cur = _strided(start, d)
                rows = [_strided(start - BLK * d, d), cur] if has_prev else [cur]
                q2 = _stack_heads(q_ref[cur, :] * SCALE_A, own)
                kt = jnp.concatenate([k_ref[x, :] for x in rows], axis=0).astype(BF16)
                vts.append(jnp.concatenate([v_ref[x, :] for x in rows], axis=0).astype(BF16))
                bias = b_ref[r, 0] if has_prev else b_ref[r, 0, :, BLK:]
                ss.append(_dot(q2, kt, ((1,), (1,))) + bias)
            ms = [jnp.max(s, axis=-1, keepdims=True) for s in ss]
            ps = [jnp.exp(s - m) for s, m in zip(ss, ms)]
            ls = [jnp.sum(p, axis=-1, keepdims=True) for p in ps]
            for (r, d, start, _), p, vt, m, l in zip(grp, ps, vts, ms, ls):
                cur = _strided(start, d)
                o2 = _dot(p.astype(BF16), vt, ((1,), (0,))) / l
                lse2 = m + jnp.log(l)
                ob_ref[r, cur, :] = jnp.where(lane < E_A, o2[:BLK], o2[BLK:])
                lb_ref[r, cur, :] = jnp.where(lane < E_A, lse2[:BLK], lse2[BLK:])

        def merge(c, _):
            rows = pl.ds(pl.multiple_of(c * TQ, TQ), TQ)
            l0, l1, l2 = lb_ref[0, rows, :], lb_ref[1, rows, :], lb_ref[2, rows, :]
            m = jnp.maximum(jnp.maximum(l0, l1), l2)
            e0, e1, e2 = jnp.exp(l0 - m), jnp.exp(l1 - m), jnp.exp(l2 - m)
            tot = e0 + e1 + e2
            o_ref[rows, :] = (e0 * ob_ref[0, rows, :] + e1 * ob_ref[1, rows, :] + e2 * ob_ref[2, rows, :]) / tot
            lse_ref[rows, :] = m + jnp.log(tot)
            return 0

        lax.fori_loop(0, S // TQ, merge, 0)

    npair = H // 2
    return pl.pallas_call(
        body, name="dil_fwd", grid=(nb, npair),
        in_specs=[pl.BlockSpec((S, HP), lambda b, p: (b, p)), pl.BlockSpec((S, HP), lambda b, p: (b, npair + p)),
                  pl.BlockSpec((S, HP), lambda b, p: (b, 2 * npair + p)),
                  pl.BlockSpec((3, 1, 2 * BLK, 2 * BLK), lambda b, p: (0, p, 0, 0))],
        out_specs=[pl.BlockSpec((S, HP), lambda b, p: (b, p))] * 2,
        out_shape=[jax.ShapeDtypeStruct((t, D_A), F32)] * 2,
        scratch_shapes=[pltpu.VMEM((3, S, HP), F32), pltpu.VMEM((3, S, HP), F32)],
        compiler_params=_cp("parallel", "parallel"),
    )(proj, proj, proj, biasm)


def _dil_bwd(proj, biasm, o, do, lse):
    t = proj.shape[0]
    nb = t // S

    def body(q_ref, k_ref, v_ref, b_ref, o_ref, do_ref, lse_ref, dq_out, dk_out, dv_out, ds_ref, dq_ref, dk_ref, dv_ref):
        dq_ref[...] = jnp.zeros_like(dq_ref)
        dk_ref[...] = jnp.zeros_like(dk_ref)
        dv_ref[...] = jnp.zeros_like(dv_ref)
        ds_ref[...] = jnp.zeros_like(ds_ref)
        lane = lax.broadcasted_iota(jnp.int32, (BLK, HP), 1)
        own = _own_lanes()
        items = _dil_items()
        for g in range(0, len(items), GROUP):
            grp = items[g:g + GROUP]
            q2s, kts, do2s, ss, dps, lse2s, delta2s = [], [], [], [], [], [], []
            for r, d, start, has_prev in grp:
                cur = _strided(start, d)
                rows = [_strided(start - BLK * d, d), cur] if has_prev else [cur]
                q2 = _stack_heads(q_ref[cur, :] * SCALE_A, own)
                kt = jnp.concatenate([k_ref[x, :] for x in rows], axis=0).astype(BF16)
                vt = jnp.concatenate([v_ref[x, :] for x in rows], axis=0).astype(BF16)
                dot_ = do_ref[cur, :]
                prod = dot_ * o_ref[cur, :]
                lset = lse_ref[cur, :]
                do2 = _stack_heads(dot_, own)
                bias = b_ref[r, 0] if has_prev else b_ref[r, 0, :, BLK:]
                ss.append(_dot(q2, kt, ((1,), (1,))) + bias)
                dps.append(_dot(do2, vt, ((1,), (1,))))
                lse2s.append(jnp.concatenate([lset[:, :1], lset[:, E_A:E_A + 1]], axis=0))
                delta2s.append(jnp.concatenate([jnp.sum(jnp.where(lane < E_A, prod, 0.0), axis=-1, keepdims=True),
                                                jnp.sum(jnp.where(lane >= E_A, prod, 0.0), axis=-1, keepdims=True)], axis=0))
                q2s.append(q2)
                kts.append(kt)
                do2s.append(do2)
            ps = [jnp.exp(s - lse2) for s, lse2 in zip(ss, lse2s)]
            dls = [p * (dp - delta2) for p, dp, delta2 in zip(ps, dps, delta2s)]
            for (r, d, start, has_prev), q2, kt, do2, p, dl in zip(grp, q2s, kts, do2s, ps, dls):
                cur = _strided(start, d)
                dsb = dl.astype(BF16)
                dq2 = _dot(dsb, kt, ((1,), (0,))) * SCALE_A
                dkt = _dot(dsb, q2, ((0,), (0,)))
                dvt = _dot(p.astype(BF16), do2, ((0,), (0,)))
                dq_ref[cur, :] += jnp.where(lane < E_A, dq2[:BLK], dq2[BLK:])
                if has_prev:
                    prev = _strided(start - BLK * d, d)
                    ds_ref[0, r, 0] += dl
                    dk_ref[prev, :] += dkt[:BLK]
                    dv_ref[prev, :] += dvt[:BLK]
                    dk_ref[cur, :] += dkt[BLK:]
                    dv_ref[cur, :] += dvt[BLK:]
                else:
                    ds_ref[0, r, 0, :, BLK:] += dl
                    dk_ref[cur, :] += dkt
                    dv_ref[cur, :] += dvt
        dq_out[...] = dq_ref[...].astype(BF16)
        dk_out[...] = dk_ref[...].astype(BF16)
        dv_out[...] = dv_ref[...].astype(BF16)

    npair = H // 2
    pair = pl.BlockSpec((S, HP), lambda b, p: (b, p))
    return pl.pallas_call(
        body, name="dil_bwd", grid=(nb, npair),
        in_specs=[pair, pl.BlockSpec((S, HP), lambda b, p: (b, npair + p)),
                  pl.BlockSpec((S, HP), lambda b, p: (b, 2 * npair + p)),
                  pl.BlockSpec((3, 1, 2 * BLK, 2 * BLK), lambda b, p: (0, p, 0, 0)), pair, pair, pair],
        out_specs=[pair, pair, pair, pl.BlockSpec((1, 3, 1, 2 * BLK, 2 * BLK), lambda b, p: (b, 0, p, 0, 0))],
        out_shape=[jax.ShapeDtypeStruct((t, D_A), BF16)] * 3 + [jax.ShapeDtypeStruct((nb, 3, npair, 2 * BLK, 2 * BLK), F32)],
        scratch_shapes=[pltpu.VMEM((S, HP), F32)] * 3,
        compiler_params=_cp("parallel", "parallel"),
    )(proj, proj, proj, biasm, o, do, lse)


def _rel_bias_grad(dlogits):
    nb = dlogits.shape[0]
    buckets, _ = _band_geometry()
    kk = 3 * BLK * 2 * BLK
    dl = jnp.transpose(dlogits.reshape(nb, 3, H, BLK, 2 * BLK), (0, 2, 1, 3, 4)).reshape(nb, H, kk)
    bk = jnp.asarray(buckets.reshape(1, kk))
    tk = kk // 4

    def body(dl_ref, bk_ref, o_ref):
        j = pl.program_id(0)
        onehot = (bk_ref[...] == lax.broadcasted_iota(jnp.int32, (N_BUCKETS, tk), 0)).astype(F32)
        tot = dl_ref[0]
        for b in range(1, nb):
            tot = tot + dl_ref[b]
        part = lax.dot_general(onehot, tot, ((((1,), (1,))), ((), ())), preferred_element_type=F32,
                               precision=lax.Precision.HIGHEST)
        _acc_first(j, o_ref, part)

    return pl.pallas_call(
        body, name="rel_bias_grad", grid=(kk // tk,),
        in_specs=[pl.BlockSpec((nb, H, tk), lambda j: (0, 0, j)), pl.BlockSpec((1, tk), lambda j: (0, j))],
        out_specs=pl.BlockSpec((N_BUCKETS, H), lambda j: (0, 0)),
        out_shape=jax.ShapeDtypeStruct((N_BUCKETS, H), F32),
        compiler_params=_cp("arbitrary"),
    )(dl, bk)


def _mesh_place():
    x, y, c = lax.axis_index("x"), lax.axis_index("y"), lax.axis_index("c")
    return x, y, c


def _peer(k):
    x, y, c = _mesh_place()
    px = 1 - x if k & 4 else x
    py = 1 - y if k & 2 else y
    pc = 1 - c if k & 1 else c
    return (px, py, pc), 4 * px + 2 * py + pc


ANY = pl.BlockSpec(memory_space=pl.ANY)


def _exchange(arrays, gathers, name, after=None):
    n_arr = len(arrays)

    def body(*refs):
        ins, outs = refs[:n_arr], refs[n_arr + 1:2 * n_arr + 1]
        send, recv, loc = refs[2 * n_arr + 1:]
        x, y, c = _mesh_place()
        me = 4 * x + 2 * y + c
        local = [pltpu.make_async_copy(ins[a] if gathers[a] else ins[a].at[me], outs[a].at[me], loc.at[a])
                 for a in range(n_arr)]
        remote = _peer_copies(ins, outs, send, recv, gathers)
        for cp in local:
            cp.start()
        for put, _ in remote:
            put.start()
        for cp in local:
            cp.wait()
        for put, got in remote:
            put.wait_send()
            got.wait_recv()

    return pl.pallas_call(
        body, name=name,
        in_specs=[ANY] * (n_arr + 1), out_specs=[ANY] * n_arr,
        out_shape=[jax.ShapeDtypeStruct(((N_DEV,) if g else ()) + a.shape, a.dtype) for a, g in zip(arrays, gathers)],
        scratch_shapes=[pltpu.SemaphoreType.DMA((n_arr * (N_DEV - 1),)), pltpu.SemaphoreType.DMA((n_arr * (N_DEV - 1),)),
                        pltpu.SemaphoreType.DMA((n_arr,))],
        compiler_params=pltpu.CompilerParams(has_side_effects=True),
    )(*arrays, arrays[0] if after is None else after)


def _gather_two_level(arrays, name):
    n_arr = len(arrays)
    per = N_DEV - 1

    def body(*refs):
        ins, outs = refs[:n_arr], refs[n_arr:2 * n_arr]
        send, recv, loc = refs[2 * n_arr:]
        x, y, c = _mesh_place()
        me, sibling = (x, y, c), (x, y, 1 - c)
        chips = [(1 - x, y), (x, 1 - y), (1 - x, 1 - y)]

        def block(a, place):
            px, py, pc = place
            return outs[a].at[4 * px + 2 * py + pc]

        def copy(a, k, place, to, src=None):
            dst = block(a, place)
            return pltpu.make_async_remote_copy(dst if src is None else src, dst, send.at[a * per + k], recv.at[a * per + k],
                                                device_id=to, device_id_type=pl.DeviceIdType.MESH)

        local = [pltpu.make_async_copy(ins[a], block(a, me), loc.at[a]) for a in range(n_arr)]
        for cp in local:
            cp.start()
        first = []
        for a in range(n_arr):
            first.append(copy(a, 0, me, sibling, src=ins[a]))
            first += [copy(a, 1 + j, me, (*chip, c), src=ins[a]) for j, chip in enumerate(chips)]
        for cp in first:
            cp.start()
        passed = []
        for j, chip in enumerate(chips):
            for a in range(n_arr):
                copy(a, 1 + j, (*chip, c), me).wait_recv()
                passed.append(copy(a, 4 + j, (*chip, c), sibling))
                passed[-1].start()
        for a in range(n_arr):
            copy(a, 0, sibling, me).wait_recv()
            for j, chip in enumerate(chips):
                copy(a, 4 + j, (*chip, 1 - c), me).wait_recv()
        for cp in first + passed:
            cp.wait_send()
        for cp in local:
            cp.wait()

    return pl.pallas_call(
        body, name=name,
        in_specs=[ANY] * n_arr, out_specs=[ANY] * n_arr,
        out_shape=[jax.ShapeDtypeStruct((N_DEV,) + a.shape, a.dtype) for a in arrays],
        scratch_shapes=[pltpu.SemaphoreType.DMA((n_arr * per,)), pltpu.SemaphoreType.DMA((n_arr * per,)),
                        pltpu.SemaphoreType.DMA((n_arr,))],
        compiler_params=pltpu.CompilerParams(has_side_effects=True),
    )(*arrays)


HBM = pl.BlockSpec(memory_space=pltpu.HBM)
SEM = pl.BlockSpec(memory_space=pltpu.SEMAPHORE)
DATAFLOW = pltpu.SideEffectType.DATAFLOW_SIDE_EFFECTING


def _own_block_in_place(block, me):
    land = lax.empty((N_DEV,) + block.shape, block.dtype)
    return lax.dynamic_update_slice(land, block[None], (me,) + (0,) * block.ndim)


def _peer_copies(srcs, lands, send, recv, gathers):
    x, y, c = _mesh_place()
    me = 4 * x + 2 * y + c
    out = []
    for a, (src, land) in enumerate(zip(srcs, lands)):
        for k in range(1, N_DEV):
            dev, idx = _peer(k)
            sem = a * (N_DEV - 1) + k - 1
            mine = src if gathers[a] else src.at[idx]
            put = pltpu.make_async_remote_copy(mine, land.at[me], send.at[sem], recv.at[sem],
                                               device_id=dev, device_id_type=pl.DeviceIdType.MESH)
            got = pltpu.make_async_remote_copy(mine, land.at[idx], send.at[sem], recv.at[sem],
                                               device_id=dev, device_id_type=pl.DeviceIdType.MESH)
            out.append((put, got))
    return out


def _exchange_start(srcs, lands, gather, after, name):
    n = len(srcs)
    extra = [] if after is None else [after]

    def body(*refs):
        srcs_, lands_ = refs[:n], refs[n:2 * n]
        send, recv = refs[2 * n + len(extra)], refs[2 * n + len(extra) + 1]
        for put, _ in _peer_copies(srcs_, lands_, send, recv, gather):
            put.start()
        refs[-1][...] = jnp.zeros_like(refs[-1])

    nsem = n * (N_DEV - 1)
    thru = [pltpu.HBM(a.shape, a.dtype) for a in list(srcs) + list(lands)]
    res = pl.pallas_call(
        body, name=name,
        out_shape=(pltpu.SemaphoreType.DMA((nsem,)), pltpu.SemaphoreType.DMA((nsem,)), *thru, jax.ShapeDtypeStruct((8, 128), F32)),
        in_specs=[HBM] * (2 * n) + [ANY] * len(extra),
        out_specs=(SEM, SEM, *([HBM] * (2 * n)), pl.BlockSpec(memory_space=pltpu.VMEM)),
        input_output_aliases={i: 2 + i for i in range(2 * n)},
        compiler_params=pltpu.CompilerParams(has_side_effects=DATAFLOW),
    )(*[pltpu.with_memory_space_constraint(a, pltpu.HBM) for a in list(srcs) + list(lands)], *extra)
    return res[0], res[1], list(res[2:2 + n]), list(res[2 + n:2 + 2 * n]), res[-1]


def _exchange_wait(send, recv, srcs, lands, gather, after, name):
    n = len(srcs)

    def body(*refs):
        srcs_, lands_, send_, recv_ = refs[:n], refs[n:2 * n], refs[2 * n], refs[2 * n + 1]
        for put, got in _peer_copies(srcs_, lands_, send_, recv_, gather):
            put.wait_send()
            got.wait_recv()

    thru = [pltpu.HBM(a.shape, a.dtype) for a in list(srcs) + list(lands)]
    res = pl.pallas_call(
        body, name=name, out_shape=tuple(thru),
        in_specs=[HBM] * (2 * n) + [SEM, SEM, ANY], out_specs=tuple([HBM] * (2 * n)),
        input_output_aliases={i: i for i in range(2 * n)},
        compiler_params=pltpu.CompilerParams(has_side_effects=DATAFLOW),
    )(*srcs, *lands, send, recv, after)
    return list(res[n:])


def _silu_rows(c):
    def body(c_ref, o_ref):
        v = c_ref[...]
        o_ref[...] = v * _sigmoid(v)

    return pl.pallas_call(body, name="cond", out_shape=jax.ShapeDtypeStruct(c.shape, F32))(c)


def _mod_slab(cond_all, w_ada, b_slab):
    def body(c_ref, w_ref, b_ref, o_ref):
        o_ref[...] = _dot(c_ref[...].astype(BF16), w_ref[0].astype(BF16), ((1,), (0,))) + b_ref[...]

    return pl.pallas_call(body, name="mod_slab",
                          out_shape=jax.ShapeDtypeStruct((cond_all.shape[0], w_ada.shape[2]), F32),
                          compiler_params=pltpu.CompilerParams(vmem_limit_bytes=VMEM_LIMIT))(cond_all, w_ada, b_slab)


def _ada_grad(cond_all, dmod_cols):
    def body(c_ref, d_ref, o_ref):
        o_ref[...] = _dot(c_ref[...].astype(BF16), d_ref[...].astype(BF16), ((0,), (0,)))

    return pl.pallas_call(body, name="ada_grad",
                          out_shape=jax.ShapeDtypeStruct((cond_all.shape[1], dmod_cols.shape[1]), F32),
                          compiler_params=pltpu.CompilerParams(vmem_limit_bytes=VMEM_LIMIT))(cond_all, dmod_cols)


def _adam_math(g, w, m, v):
    m2 = B1 * m + (1.0 - B1) * g
    v2 = B2 * v + (1.0 - B2) * (g * g)
    m_hat = m2 / (1.0 - B1 ** STEP)
    v_hat = v2 / (1.0 - B2 ** STEP)
    return -LR * (m_hat / (jnp.sqrt(v_hat) + ADAM_EPS) + WD * w), m2, v2


def _adamw(parts, w, m, v, name):
    n, rows, cols = parts.shape
    tr = max([p for p in range(16, 513, 16) if rows % p == 0] or [rows])

    def body(p_ref, w_ref, m_ref, v_ref, g_ref, d_ref, m2_ref, v2_ref):
        g = p_ref[0].astype(F32)
        for s in range(1, n):
            g = g + p_ref[s].astype(F32)
        g_ref[0] = g
        d_ref[0], m2_ref[0], v2_ref[0] = _adam_math(g, w_ref[0], m_ref[0], v_ref[0])

    blk = pl.BlockSpec((1, tr, cols), lambda i: (0, i, 0))
    return pl.pallas_call(
        body, name=name, grid=(rows // tr,),
        in_specs=[pl.BlockSpec((n, tr, cols), lambda i: (0, i, 0)), blk, blk, blk],
        out_specs=[blk] * 4, out_shape=[jax.ShapeDtypeStruct((1, rows, cols), F32)] * 4,
        compiler_params=_cp("parallel"),
    )(parts, w, m, v)


ROW_PARAMS = (("g_norm1", D), ("g_cq", Q_LORA), ("g_ckv", KV_LORA), ("g_out_a", D_A), ("g_out_b", D_A), ("g_norm2", D),
              ("g_final", D))
LOSS_ROW = N_MOD + len(ROW_PARAMS)
PAY_ROWS = 16
NCOL = N_MOD * D // N_DEV


def _pack_small(dmods, rows, loss_cols):
    nb = dmods[0].shape[0]
    nrow = len(ROW_PARAMS)

    def body(*refs):
        dm, rw, loss_ref, pay_ref, blk_ref = refs[:N_MOD], refs[N_MOD:N_MOD + nrow], refs[N_MOD + nrow], refs[-2], refs[-1]
        pay_ref[...] = jnp.zeros_like(pay_ref)
        for k in range(N_MOD):
            tot = dm[k][0]
            for b in range(1, nb):
                tot = tot + dm[k][b]
            pay_ref[k:k + 1, :] = tot
        for i, (_, n) in enumerate(ROW_PARAMS):
            pay_ref[N_MOD + i:N_MOD + i + 1, :n] = rw[i][...]
        pay_ref[LOSS_ROW:LOSS_ROW + 1, :] = loss_ref[...]
        for j in range(N_DEV):
            done = 0
            while done < NCOL:
                seg, off = divmod(j * NCOL + done, D)
                ln = min(NCOL - done, D - off)
                for b in range(nb):
                    blk_ref[j, b:b + 1, done:done + ln] = dm[seg][b][:, off:off + ln]
                done += ln

    return pl.pallas_call(
        body, name="pack_small",
        out_shape=[jax.ShapeDtypeStruct((PAY_ROWS, D), F32), jax.ShapeDtypeStruct((N_DEV, nb, NCOL), F32)],
    )(*dmods, *rows, loss_cols)


def _small_update(pay, rel, ws, ms, vs):
    n_par = len(ws)

    def body(*refs):
        pay_ref, rel_ref = refs[:2]
        w_refs, m_refs, v_refs = (refs[2 + s * n_par:2 + (s + 1) * n_par] for s in range(3))
        outs, loss_ref = refs[2 + 3 * n_par:-1], refs[-1]
        tot, rtot = pay_ref[0], rel_ref[0]
        for s in range(1, N_DEV):
            tot, rtot = tot + pay_ref[s], rtot + rel_ref[s]

        def update(p, g, sl):
            outs[4 * p][:, sl] = g
            outs[4 * p + 1][:, sl], outs[4 * p + 2][:, sl], outs[4 * p + 3][:, sl] = _adam_math(
                g, w_refs[p][:, sl], m_refs[p][:, sl], v_refs[p][:, sl])

        for k in range(N_MOD):
            update(0, tot[k:k + 1, :], slice(k * D, (k + 1) * D))
        for i, (_, n) in enumerate(ROW_PARAMS):
            update(1 + i, tot[N_MOD + i:N_MOD + i + 1, :n], slice(0, n))
        update(n_par - 1, rtot, slice(0, H))
        loss_ref[...] = jnp.broadcast_to((0.5 / D) * jnp.sum(tot[LOSS_ROW:LOSS_ROW + 1, :]), loss_ref.shape)

    shapes = [jax.ShapeDtypeStruct(w.shape, F32) for w in ws for _ in range(4)]
    res = pl.pallas_call(
        body, name="small_update", out_shape=shapes + [jax.ShapeDtypeStruct((8, 128), F32)],
    )(pay, rel, *ws, *ms, *vs)
    return [tuple(res[4 * p:4 * p + 4]) for p in range(n_par)], res[-1]


def _cols_from_blocks(g):
    return jnp.transpose(g, (1, 0, 2)).reshape(g.shape[1], N_DEV * g.shape[2])


def _cols_to_blocks(w):
    r, c = w.shape
    return jnp.transpose(w.reshape(r, N_DEV, c // N_DEV), (1, 0, 2))


def _pad_w_in(wt):
    z = jnp.zeros((NOPE, wt.shape[1]), wt.dtype)
    return jnp.concatenate([wt[:P_IN - ROPE], z, wt[P_IN - ROPE:], z[:HP - NOPE - ROPE]], axis=0)


def _unpad_w_in(gt):
    k0 = P_IN - ROPE + NOPE
    return jnp.concatenate([gt[:P_IN - ROPE], gt[k0:k0 + ROPE]], axis=0)


def _pad_w_uq(wt):
    return jnp.pad(wt, ((0, 0), (0, HP - NOPE - ROPE), (0, 0))).reshape(H * HP, Q_LORA)


def _unpad_w_uq(gt):
    return gt.reshape(H, HP, Q_LORA)[:, :NOPE + ROPE]


def _split_w_ukv(w):
    w4 = w.reshape(KV_LORA, H // 2, 2, HP)
    z = jnp.zeros((KV_LORA, H // 2, NOPE), w.dtype)
    kn, vv = w4[..., :NOPE], w4[..., NOPE:]
    w_k = jnp.stack([jnp.concatenate([kn[:, :, 0], z], -1), jnp.concatenate([kn[:, :, 1], z], -1)], axis=2)
    w_v = jnp.stack([jnp.concatenate([vv[:, :, 0], z], -1), jnp.concatenate([z, vv[:, :, 1]], -1)], axis=2)
    return w_k.reshape(KV_LORA, H * HP), w_v.reshape(KV_LORA, H * HP)


def _join_w_ukv(g_k, g_v):
    gk = g_k.reshape(KV_LORA, H // 2, 2, HP)
    gv = g_v.reshape(KV_LORA, H // 2, 2, HP)
    even = jnp.concatenate([gk[:, :, 0, :NOPE], gv[:, :, 0, :VDIM]], -1)
    odd = jnp.concatenate([gk[:, :, 1, :NOPE], gv[:, :, 1, VDIM:]], -1)
    return jnp.stack([even, odd], axis=2).reshape(KV_LORA, H * HP)


def _rope_tables():
    half = ROPE // 2
    inv = np.float32(ROPE_THETA) ** (-np.arange(half, dtype=np.float32) / np.float32(half))
    ang = np.arange(S, dtype=np.float32)[:, None] * inv[None, :].astype(np.float32)
    cos, sin = np.cos(ang).astype(np.float32), np.sin(ang).astype(np.float32)
    ones, zeros = np.ones((S, NOPE), np.float32), np.zeros((S, NOPE), np.float32)
    tail1, tail0 = np.ones((S, HP - NOPE - ROPE), np.float32), np.zeros((S, HP - NOPE - ROPE), np.float32)
    zh = np.zeros((S, half), np.float32)
    c = np.concatenate([ones, cos, cos, tail1], axis=1)
    sm = np.concatenate([zeros, -sin, zh, tail0], axis=1)
    sp = np.concatenate([zeros, zh, sin, tail0], axis=1)
    return jnp.asarray(c), jnp.asarray(sm), jnp.asarray(sp)


def _local_step(x, mod, target, g_norm1, w_in_p, g_cq, w_uq_p, g_ckv, w_k, w_v, rel_bias, g_out_a, g_out_b, w_out,
                g_norm2, w_ffn_in, w_ffn_out, g_final, late_weights=None, on_ffn_grads=None, on_last_grads=None):
    nb = x.shape[0] // S
    sh1, sc1, g1, sh2, sc2, g2 = (mod[:, n].reshape(nb, 1, D) for n in range(N_MOD))
    rc, rsm, rsp = _rope_tables()
    biasm = _band_bias(rel_bias)

    h1 = _pre1(x, g_norm1, sc1, sh1)
    proj = _mm_nt(h1, w_in_p, F32, "proj")
    q, k, v, cqn, ckvn = _mla_pre(proj, g_cq, g_ckv, w_uq_p, w_k, w_v, rc, rsm, rsp)
    out_b, lse_b = _mla_fwd(q, k, v)
    out_a, lse_a = _dil_fwd(proj, biasm)
    y = _post_attn(out_a, out_b, g_out_a, g_out_b)
    if late_weights is not None:
        w_out, w_ffn_in, w_ffn_out = late_weights(y)
    mix = _mm_nn(y, w_out, BF16, "mix")
    x2, h2 = _resid_norm2(x, mix, g1, g_norm2, sc2, sh2)
    ffn_g, ffn_u, act = _ffn_in(h2, w_ffn_in)
    f = _mm_nn(act, w_ffn_out, BF16, "ffn_out")
    dx3, df, loss_cols, dg_final, dg2 = _final(x2, f, g2, g_final, target)

    dg_, du_ = _d_act(df, w_ffn_out, ffn_g, ffn_u)
    gw_ffn_out = _mm_tn_rows([act], df, "gw_ffn_out")
    dh2 = _d_h2(dg_, du_, w_ffn_in)
    gw_ffn_in = _mm_tn_rows([dg_, du_], h2, "gw_ffn_in")
    dx2, dsh2, dsc2, dg_norm2, dg1, dmix = _norm_bwd(x2, dh2, dx3, g_norm2, sc2, gate=(mix, g1))
    dy = _mm_nt(dmix, w_out, BF16, "d_y")
    gw_out = _mm_tn(y, [dmix], "gw_out")
    if on_ffn_grads is not None:
        g_out_a = g_out_a + on_ffn_grads(gw_ffn_in, gw_ffn_out, gw_out)
    dout_a, dout_b, dg_out_a, dg_out_b = _post_attn_bwd(dy, out_a, out_b, g_out_a, g_out_b)
    dq_b, dk_b, dv_b = _mla_bwd(q, k, v, out_b, dout_b, lse_b)
    dq_a, dk_a, dv_a, dlogits = _dil_bwd(proj, biasm, out_a, dout_a, lse_a)
    g_rel = _rel_bias_grad(dlogits)
    dqr, dproj, dg_cq, dg_ckv = _mla_pre_bwd(proj, dq_b, dk_b, dv_b, (dq_a, dk_a, dv_a), g_cq, g_ckv, w_uq_p, w_k, w_v,
                                             rc, rsm, rsp)
    gw_uq = _mm_tn(dqr, [cqn], "gw_uq")
    gw_k, gw_v = _mm_tn(ckvn, [dk_b, dv_b], "gw_kv")
    gw_in = _mm_tn_rows([dproj], h1, "gw_in")
    if on_last_grads is not None:
        started = on_last_grads(dict(w_in=gw_in, w_uq=gw_uq, w_k=gw_k, w_v=gw_v))
    else:
        started = None
    dh1 = _mm_nn(dproj, w_in_p, BF16, "d_h1", after=started)
    grad_x, dsh1, dsc1, dg_norm1 = _norm_bwd(x, dh1, dx2, g_norm1, sc1)

    dmod = [dsh1, dsc1, dg1, dsh2, dsc2, dg2]
    small = dict(g_norm1=dg_norm1, g_cq=dg_cq, g_ckv=dg_ckv, rel_bias=g_rel, g_out_a=dg_out_a, g_out_b=dg_out_b,
                 g_norm2=dg_norm2, g_final=dg_final)
    big = dict(w_in=gw_in, w_uq=gw_uq, w_k=gw_k, w_v=gw_v, w_out=gw_out, w_ffn_in=gw_ffn_in, w_ffn_out=gw_ffn_out)
    return grad_x, dmod, loss_cols, small, big


def kernel(x, c, w_ada, b_ada, g_norm1, w_in, g_cq, w_uq, g_ckv, w_ukv, rel_bias, g_out_a, g_out_b, w_out, g_norm2, w_ffn_in, w_ffn_out, g_final, loss_target, m_w_ada, m_b_ada, m_g_norm1, m_w_in, m_g_cq, m_w_uq, m_g_ckv, m_w_ukv, m_rel_bias, m_g_out_a, m_g_out_b, m_w_out, m_g_norm2, m_w_ffn_in, m_w_ffn_out, m_g_final, v_w_ada, v_b_ada, v_g_norm1, v_w_in, v_g_cq, v_w_uq, v_g_ckv, v_w_ukv, v_rel_bias, v_g_out_a, v_g_out_b, v_w_out, v_g_norm2, v_w_ffn_in, v_w_ffn_out, v_g_final):
    nb = x.shape[0]
    t = nb * S
    xt, tt = x.reshape(t, D), loss_target.reshape(t, D)
    me = 4 * lax.axis_index("x") + 2 * lax.axis_index("y") + lax.axis_index("c")

    early = [jnp.swapaxes(w_in, 1, 2)[0], jnp.swapaxes(w_uq, 1, 2)[0], w_ukv[0]]
    gathered = _gather_two_level([_silu_rows(c)] + [s.astype(BF16) for s in early], "gather_weights")
    cond_all = gathered[0].reshape(N_DEV * nb, D)
    w_in_t = gathered[1].reshape(P_IN, D)
    w_ukv_f = _cols_from_blocks(gathered[3])
    w_k, w_v = _split_w_ukv(w_ukv_f)

    ncol = N_MOD * D // N_DEV
    b_slab = lax.dynamic_slice(b_ada, (0, me * ncol), (1, ncol))
    slab = _mod_slab(cond_all, w_ada, b_slab)
    (mod_rows,) = _exchange([slab.reshape(N_DEV, nb, ncol)], [False], "scatter_mod")
    mod = jnp.transpose(mod_rows, (1, 0, 2)).reshape(nb, N_MOD, D)

    late = [s.astype(BF16) for s in (w_out[0], jnp.swapaxes(w_ffn_in, 1, 2)[0], w_ffn_out[0])]
    late_send, late_recv, late_src, late_land, late_token = _exchange_start(
        late, [_own_block_in_place(s, me) for s in late], [True] * 3, mod_rows, "gather_late_start")
    g_norm1_t = g_norm1 + late_token[:1, :1]

    def late_weights(after):
        w_out_g, w_ffn_in_g, w_ffn_out_g = _exchange_wait(late_send, late_recv, late_src, late_land, [True] * 3, after,
                                                          "gather_late_wait")
        return w_out_g.reshape(D, D), w_ffn_in_g.reshape(2 * D_FF, D), w_ffn_out_g.reshape(D_FF, D)

    flight = {}

    def start_grads(key, src, name):
        land = [_own_block_in_place(lax.dynamic_index_in_dim(s, me, 0, keepdims=False), me) for s in src]
        send, recv, src, land, token = _exchange_start(src, land, [False] * len(src), None, name)
        flight[key] = (send, recv, src, land)
        return token[:1, :1]

    def on_ffn_grads(gw_ffn_in, gw_ffn_out, gw_out):
        return start_grads("ffn", [gw_ffn_in.reshape(N_DEV, 2 * D_FF // N_DEV, D), gw_ffn_out.reshape(N_DEV, D_FF // N_DEV, D),
                                   gw_out.reshape(N_DEV, D // N_DEV, D)], "exchange_ffn_start")

    def on_last_grads(gw):
        return start_grads("rest", [_unpad_w_in(gw["w_in"]).reshape(N_DEV, P_IN // N_DEV, D),
                                    _unpad_w_uq(gw["w_uq"]),
                                    _cols_to_blocks(_join_w_ukv(gw["w_k"], gw["w_v"]))], "exchange_rest_start")

    grad_x, dmod, loss_cols, small, _ = _local_step(
        xt, mod, tt, g_norm1_t, _pad_w_in(w_in_t), g_cq, _pad_w_uq(gathered[2]), g_ckv, w_k, w_v, rel_bias, g_out_a, g_out_b,
        None, g_norm2, None, None, g_final.reshape(1, D), late_weights=late_weights, on_ffn_grads=on_ffn_grads,
        on_last_grads=on_last_grads)

    mine, dmod_blocks = _pack_small(dmod, [small[n] for n, _ in ROW_PARAMS], loss_cols)
    small_src = [dmod_blocks, mine, small["rel_bias"]]
    small_gather = [False, True, True]
    small_land = [_own_block_in_place(lax.dynamic_index_in_dim(dmod_blocks, me, 0, keepdims=False), me),
                  _own_block_in_place(mine, me), _own_block_in_place(small["rel_bias"], me)]
    small_send, small_recv, small_src, small_land, _ = _exchange_start(small_src, small_land, small_gather, None,
                                                                       "exchange_small_start")

    upd = {}

    def land_and_update(key, names, after, name):
        got = _exchange_wait(*flight[key], [False] * len(names), after, name)
        for n, p in zip(names, got):
            w, m, v = big[n]
            upd[n] = _adamw(p, w, m, v, "adamw_" + n)

    def flip(a):
        return jnp.swapaxes(a, 1, 2)

    big = dict(w_in=(flip(w_in), flip(m_w_in), flip(v_w_in)), w_uq=(flip(w_uq), flip(m_w_uq), flip(v_w_uq)),
               w_ukv=(w_ukv, m_w_ukv, v_w_ukv),
               w_out=(w_out, m_w_out, v_w_out), w_ffn_in=(flip(w_ffn_in), flip(m_w_ffn_in), flip(v_w_ffn_in)),
               w_ffn_out=(w_ffn_out, m_w_ffn_out, v_w_ffn_out))
    land_and_update("ffn", ["w_ffn_in", "w_ffn_out", "w_out"], grad_x, "exchange_ffn_wait")
    land_and_update("rest", ["w_in", "w_uq", "w_ukv"], upd["w_out"][0], "exchange_rest_wait")
    for n in ("w_in", "w_uq", "w_ffn_in"):
        upd[n] = tuple(flip(a) for a in upd[n])

    dmod_cols, pay, rel = _exchange_wait(small_send, small_recv, small_src, small_land, small_gather, upd["w_ukv"][0],
                                         "exchange_small_wait")
    g_ada = _ada_grad(cond_all, dmod_cols.reshape(N_DEV * nb, ncol))
    upd["w_ada"] = _adamw(g_ada[None], w_ada, m_w_ada, v_w_ada, "adamw_w_ada")
    row = lambda a: a.reshape(1, D)
    small_names = ["b_ada"] + [n for n, _ in ROW_PARAMS] + ["rel_bias"]
    small_w = [b_ada, g_norm1, g_cq, g_ckv, g_out_a, g_out_b, g_norm2, row(g_final), rel_bias]
    small_m = [m_b_ada, m_g_norm1, m_g_cq, m_g_ckv, m_g_out_a, m_g_out_b, m_g_norm2, row(m_g_final), m_rel_bias]
    small_v = [v_b_ada, v_g_norm1, v_g_cq, v_g_ckv, v_g_out_a, v_g_out_b, v_g_norm2, row(v_g_final), v_rel_bias]
    small_upd, loss8 = _small_update(pay, rel, small_w, small_m, small_v)
    upd.update(zip(small_names, small_upd))

    order = ["w_ada", "b_ada", "g_norm1", "w_in", "g_cq", "w_uq", "g_ckv", "w_ukv", "rel_bias", "g_out_a", "g_out_b",
             "w_out", "g_norm2", "w_ffn_in", "w_ffn_out", "g_final"]
    like = dict(g_final=g_final)
    outs = [loss8[0, 0], grad_x.reshape(x.shape)]
    for part in range(4):
        for n in order:
            val = upd[n][part]
            outs.append(val.reshape(like[n].shape) if n in like else val)
    return tuple(outs)
```
